```python
import jax, jax.numpy as jnp
from jax import lax
import numpy as np

D_MODEL = 1024
BATCH = 8
SEQ = 4096
DEPTH = 1

D_MIX = D_MODEL
C_CONV = D_MIX // 2
C_POOL = D_MIX - C_CONV
CONV_HEADS = 8
CONV_WIDTH = 31
POOL_WINDOWS = (2, 4, 8, 16)
N_POOL_GROUPS = len(POOL_WINDOWS)
POOL_GROUP = C_POOL // N_POOL_GROUPS
D_IN = 2 * C_CONV + C_POOL
D_FF = ((8 * D_MODEL // 3 + 255) // 256) * 256
RMS_EPS = 1e-6
LN_EPS = 1e-5

kernel_name = "hybrid_conformer_conv_multiscale_pool_block"


def rmsnorm(x, g):
    xf = x.astype(jnp.float32)
    y = xf * lax.rsqrt(jnp.mean(xf * xf, axis=-1, keepdims=True) + RMS_EPS)
    return (y * g.astype(jnp.float32)).astype(x.dtype)


def layernorm(x, g, b):
    xf = x.astype(jnp.float32)
    mu = jnp.mean(xf, axis=-1, keepdims=True)
    var = jnp.mean(jnp.square(xf - mu), axis=-1, keepdims=True)
    y = (xf - mu) * lax.rsqrt(var + LN_EPS)
    return (y * g.astype(jnp.float32) + b.astype(jnp.float32)).astype(x.dtype)


def conformer_conv_group(a, gate, w_dw, b_dw, ln_g, ln_b):
    u = a * jax.nn.sigmoid(gate)
    k = w_dw.astype(u.dtype)[:, None, :]
    v = lax.conv_general_dilated(
        u, k, window_strides=(1,), padding=[(CONV_WIDTH - 1, 0)],
        dimension_numbers=("NWC", "WIO", "NWC"),
        feature_group_count=C_CONV) + b_dw.astype(u.dtype)
    v = layernorm(v, ln_g, ln_b)
    return jax.nn.silu(v)


def multiscale_pool_group(p, w_pool, s_pool):
    seq = p.shape[1]
    pos = jnp.arange(seq)
    outs = []
    for i, w in enumerate(POOL_WINDOWS):
        seg = p[..., i * POOL_GROUP:(i + 1) * POOL_GROUP].astype(jnp.float32)
        cs = jnp.cumsum(seg, axis=1)
        lag = jnp.pad(cs, ((0, 0), (w, 0), (0, 0)))[:, :seq]
        cnt = jnp.minimum(pos + 1, w).astype(jnp.float32)[None, :, None]
        y = ((cs - lag) / cnt - seg).astype(p.dtype)
        outs.append(jnp.einsum("bsg,gh->bsh", y, w_pool[i]))
    return jnp.concatenate(outs, axis=-1) * s_pool


def swiglu(x, w_gate, w_up, w_down):
    return (jax.nn.silu(x @ w_gate) * (x @ w_up)) @ w_down


def _fwd_setup_inputs(seed: int = 0) -> dict:
    key = jax.random.key(seed)
    ks = jax.random.split(key, 20)
    f = jnp.float32
    n = lambda k, s, sc: jax.random.normal(k, s, f) * sc
    return {
        "x": jax.random.normal(ks[0], (BATCH, SEQ, D_MODEL), f),
        "g_mix": 1.0 + n(ks[1], (DEPTH, D_MODEL), 0.05),
        "w_in": n(ks[2], (DEPTH, D_MODEL, D_IN), D_MODEL ** -0.5),
        "b_in": n(ks[3], (DEPTH, D_IN), 0.02),
        "w_dw": n(ks[4], (DEPTH, CONV_WIDTH, C_CONV), CONV_WIDTH ** -0.5),
        "b_dw": n(ks[5], (DEPTH, C_CONV), 0.02),
        "ln_g": 1.0 + n(ks[6], (DEPTH, C_CONV), 0.05),
        "ln_b": n(ks[7], (DEPTH, C_CONV), 0.02),
        "w_pool": n(ks[8], (DEPTH, N_POOL_GROUPS, POOL_GROUP, POOL_GROUP), POOL_GROUP ** -0.5),
        "s_pool": 1.0 + n(ks[9], (DEPTH, C_POOL), 0.1),
        "w_out": n(ks[10], (DEPTH, D_MIX, D_MODEL), D_MIX ** -0.5),
        "g_ffn": 1.0 + n(ks[11], (DEPTH, D_MODEL), 0.05),
        "w_gate": n(ks[12], (DEPTH, D_MODEL, D_FF), D_MODEL ** -0.5),
        "w_up": n(ks[13], (DEPTH, D_MODEL, D_FF), D_MODEL ** -0.5),
        "w_down": n(ks[14], (DEPTH, D_FF, D_MODEL), D_FF ** -0.5),
        "g_final": 1.0 + n(ks[15], (D_MODEL,), 0.05),
    }


def _fwd_reference(x, g_mix, w_in, b_in, w_dw, b_dw, ln_g, ln_b, w_pool, s_pool,
              w_out, g_ffn, w_gate, w_up, w_down, g_final):
    h = x
    for l in range(DEPTH):
        xn = rmsnorm(h, g_mix[l])
        z = xn @ w_in[l] + b_in[l]
        a = z[..., :C_CONV]
        gate = z[..., C_CONV:2 * C_CONV]
        p = z[..., 2 * C_CONV:]
        y_conv = conformer_conv_group(a, gate, w_dw[l], b_dw[l], ln_g[l], ln_b[l])
        y_pool = multiscale_pool_group(p, w_pool[l], s_pool[l])
        y = jnp.concatenate([y_conv, y_pool], axis=-1)
        h = h + y @ w_out[l]
        h = h + swiglu(rmsnorm(h, g_ffn[l]), w_gate[l], w_up[l], w_down[l])
    return rmsnorm(h, g_final)


import jax as _jax
import jax.numpy as _jnp

TWIN_FORMAT = 'train_step'
FWD_PARAMS = ['x', 'g_mix', 'w_in', 'b_in', 'w_dw', 'b_dw', 'ln_g', 'ln_b', 'w_pool', 's_pool', 'w_out', 'g_ffn', 'w_gate', 'w_up', 'w_down', 'g_final']
TWIN_WEIGHTS = ['g_mix', 'w_in', 'b_in', 'w_dw', 'b_dw', 'ln_g', 'ln_b', 'w_pool', 's_pool', 'w_out', 'g_ffn', 'w_gate', 'w_up', 'w_down', 'g_final']
TWIN_DIFF_INPUT = 'x'
TWIN_INPUTS = ['x', 'g_mix', 'w_in', 'b_in', 'w_dw', 'b_dw', 'ln_g', 'ln_b', 'w_pool', 's_pool', 'w_out', 'g_ffn', 'w_gate', 'w_up', 'w_down', 'g_final', 'loss_target', 'm_g_mix', 'm_w_in', 'm_b_in', 'm_w_dw', 'm_b_dw', 'm_ln_g', 'm_ln_b', 'm_w_pool', 'm_s_pool', 'm_w_out', 'm_g_ffn', 'm_w_gate', 'm_w_up', 'm_w_down', 'm_g_final', 'v_g_mix', 'v_w_in', 'v_b_in', 'v_w_dw', 'v_b_dw', 'v_ln_g', 'v_ln_b', 'v_w_pool', 'v_s_pool', 'v_w_out', 'v_g_ffn', 'v_w_gate', 'v_w_up', 'v_w_down', 'v_g_final']
TWIN_OUTPUTS = ['loss', 'grad_x', 'grad_g_mix', 'grad_w_in', 'grad_b_in', 'grad_w_dw', 'grad_b_dw', 'grad_ln_g', 'grad_ln_b', 'grad_w_pool', 'grad_s_pool', 'grad_w_out', 'grad_g_ffn', 'grad_w_gate', 'grad_w_up', 'grad_w_down', 'grad_g_final', 'delta_g_mix', 'delta_w_in', 'delta_b_in', 'delta_w_dw', 'delta_b_dw', 'delta_ln_g', 'delta_ln_b', 'delta_w_pool', 'delta_s_pool', 'delta_w_out', 'delta_g_ffn', 'delta_w_gate', 'delta_w_up', 'delta_w_down', 'delta_g_final', 'new_m_g_mix', 'new_m_w_in', 'new_m_b_in', 'new_m_w_dw', 'new_m_b_dw', 'new_m_ln_g', 'new_m_ln_b', 'new_m_w_pool', 'new_m_s_pool', 'new_m_w_out', 'new_m_g_ffn', 'new_m_w_gate', 'new_m_w_up', 'new_m_w_down', 'new_m_g_final', 'new_v_g_mix', 'new_v_w_in', 'new_v_b_in', 'new_v_w_dw', 'new_v_b_dw', 'new_v_ln_g', 'new_v_ln_b', 'new_v_w_pool', 'new_v_s_pool', 'new_v_w_out', 'new_v_g_ffn', 'new_v_w_gate', 'new_v_w_up', 'new_v_w_down', 'new_v_g_final']
TWIN_LEAF_KINDS = {'loss': 'loss', 'grad_x': 'grad_x', 'grad_g_mix': 'grad_w', 'grad_w_in': 'grad_w', 'grad_b_in': 'grad_w', 'grad_w_dw': 'grad_w', 'grad_b_dw': 'grad_w', 'grad_ln_g': 'grad_w', 'grad_ln_b': 'grad_w', 'grad_w_pool': 'grad_w', 'grad_s_pool': 'grad_w', 'grad_w_out': 'grad_w', 'grad_g_ffn': 'grad_w', 'grad_w_gate': 'grad_w', 'grad_w_up': 'grad_w', 'grad_w_down': 'grad_w', 'grad_g_final': 'grad_w', 'delta_g_mix': 'delta_w', 'delta_w_in': 'delta_w', 'delta_b_in': 'delta_w', 'delta_w_dw': 'delta_w', 'delta_b_dw': 'delta_w', 'delta_ln_g': 'delta_w', 'delta_ln_b': 'delta_w', 'delta_w_pool': 'delta_w', 'delta_s_pool': 'delta_w', 'delta_w_out': 'delta_w', 'delta_g_ffn': 'delta_w', 'delta_w_gate': 'delta_w', 'delta_w_up': 'delta_w', 'delta_w_down': 'delta_w', 'delta_g_final': 'delta_w', 'new_m_g_mix': 'new_m', 'new_m_w_in': 'new_m', 'new_m_b_in': 'new_m', 'new_m_w_dw': 'new_m', 'new_m_b_dw': 'new_m', 'new_m_ln_g': 'new_m', 'new_m_ln_b': 'new_m', 'new_m_w_pool': 'new_m', 'new_m_s_pool': 'new_m', 'new_m_w_out': 'new_m', 'new_m_g_ffn': 'new_m', 'new_m_w_gate': 'new_m', 'new_m_w_up': 'new_m', 'new_m_w_down': 'new_m', 'new_m_g_final': 'new_m', 'new_v_g_mix': 'new_v', 'new_v_w_in': 'new_v', 'new_v_b_in': 'new_v', 'new_v_w_dw': 'new_v', 'new_v_b_dw': 'new_v', 'new_v_ln_g': 'new_v', 'new_v_ln_b': 'new_v', 'new_v_w_pool': 'new_v', 'new_v_s_pool': 'new_v', 'new_v_w_out': 'new_v', 'new_v_g_ffn': 'new_v', 'new_v_w_gate': 'new_v', 'new_v_w_up': 'new_v', 'new_v_w_down': 'new_v', 'new_v_g_final': 'new_v'}


def _forward(args):
    return _fwd_reference(*[args[k] for k in FWD_PARAMS])


def _output_shape():
    out = _jax.eval_shape(lambda: _forward(_fwd_setup_inputs(0)))
    return out.shape, out.dtype

N_MICROBATCH = 1
ADAM_LR = 0.001
ADAM_B1 = 0.9
ADAM_B2 = 0.999
ADAM_EPS = 1e-08
ADAM_WD = 0.01
ADAM_STEP = 10
PER_EXAMPLE_BATCH_AXIS = {'x': 0, 'loss_target': 0}
SHARED_INPUTS = []
_WEIGHT_DTYPES = {'g_mix': _jnp.float32, 'w_in': _jnp.float32, 'b_in': _jnp.float32, 'w_dw': _jnp.float32, 'b_dw': _jnp.float32, 'ln_g': _jnp.float32, 'ln_b': _jnp.float32, 'w_pool': _jnp.float32, 's_pool': _jnp.float32, 'w_out': _jnp.float32, 'g_ffn': _jnp.float32, 'w_gate': _jnp.float32, 'w_up': _jnp.float32, 'w_down': _jnp.float32, 'g_final': _jnp.float32}
MOMENT_SCALE = {'g_mix': 1.200548e-01, 'w_in': 1.006874e-01, 'b_in': 8.450470e-02, 'w_dw': 9.714853e-02, 'b_dw': 2.235065e-01, 'ln_g': 1.548402e-01, 'ln_b': 1.402631e-01, 'w_pool': 1.431665e-01, 's_pool': 1.504997e-01, 'w_out': 1.225912e-01, 'g_ffn': 1.175449e-01, 'w_gate': 4.793142e-02, 'w_up': 4.668340e-02, 'w_down': 7.739620e-02, 'g_final': 3.206596e+01}


def _to_microbatches(a, axis):
    t = _jnp.moveaxis(a, axis, 0)
    t = t.reshape((N_MICROBATCH, t.shape[0] // N_MICROBATCH) + t.shape[1:])
    return _jnp.moveaxis(t, 1, axis + 1)


def setup_inputs(seed: int = 0) -> dict:
    inp = _fwd_setup_inputs(seed)
    key = _jax.random.fold_in(_jax.random.key(seed), 7919)
    shape, _ = _output_shape()
    out = dict(inp)
    out["loss_target"] = _jax.random.normal(_jax.random.fold_in(key, 0), shape, _jnp.float32)
    for i, name in enumerate(TWIN_WEIGHTS):
        w = inp[name].astype(_jnp.float32)
        if MOMENT_SCALE is None:
            s = _jnp.sqrt(_jnp.mean(_jnp.square(w)) + 1e-30)
        else:
            s = MOMENT_SCALE[name]
        km, kv = _jax.random.split(_jax.random.fold_in(key, i + 1))
        out[name] = w
        out["m_" + name] = s * _jax.random.normal(km, w.shape, _jnp.float32)
        out["v_" + name] = (s * s) * _jax.random.uniform(kv, w.shape, _jnp.float32, 0.5, 1.5)
    if N_MICROBATCH > 1:
        for name, axis in PER_EXAMPLE_BATCH_AXIS.items():
            out[name] = _to_microbatches(out[name], axis)
    return {'x': out['x'], 'g_mix': out['g_mix'], 'w_in': out['w_in'], 'b_in': out['b_in'], 'w_dw': out['w_dw'], 'b_dw': out['b_dw'], 'ln_g': out['ln_g'], 'ln_b': out['ln_b'], 'w_pool': out['w_pool'], 's_pool': out['s_pool'], 'w_out': out['w_out'], 'g_ffn': out['g_ffn'], 'w_gate': out['w_gate'], 'w_up': out['w_up'], 'w_down': out['w_down'], 'g_final': out['g_final'], 'loss_target': out['loss_target'], 'm_g_mix': out['m_g_mix'], 'm_w_in': out['m_w_in'], 'm_b_in': out['m_b_in'], 'm_w_dw': out['m_w_dw'], 'm_b_dw': out['m_b_dw'], 'm_ln_g': out['m_ln_g'], 'm_ln_b': out['m_ln_b'], 'm_w_pool': out['m_w_pool'], 'm_s_pool': out['m_s_pool'], 'm_w_out': out['m_w_out'], 'm_g_ffn': out['m_g_ffn'], 'm_w_gate': out['m_w_gate'], 'm_w_up': out['m_w_up'], 'm_w_down': out['m_w_down'], 'm_g_final': out['m_g_final'], 'v_g_mix': out['v_g_mix'], 'v_w_in': out['v_w_in'], 'v_b_in': out['v_b_in'], 'v_w_dw': out['v_w_dw'], 'v_b_dw': out['v_b_dw'], 'v_ln_g': out['v_ln_g'], 'v_ln_b': out['v_ln_b'], 'v_w_pool': out['v_w_pool'], 'v_s_pool': out['v_s_pool'], 'v_w_out': out['v_w_out'], 'v_g_ffn': out['v_g_ffn'], 'v_w_gate': out['v_w_gate'], 'v_w_up': out['v_w_up'], 'v_w_down': out['v_w_down'], 'v_g_final': out['v_g_final']}


def _loss(weights, diff, rest, loss_target):
    with _jax.named_scope("forward"):
        args = {**rest, TWIN_DIFF_INPUT: diff, **{k: w.astype(_WEIGHT_DTYPES[k]) for k, w in weights.items()}}
        y = _forward(args)
    with _jax.named_scope("loss_head"):
        err = _jnp.square(y.astype(_jnp.float32) - loss_target)
        return 0.5 * _jnp.sum(_jnp.mean(err, axis=-1)) if err.ndim else 0.5 * err


def _adamw(w, g, m, v):
    m = ADAM_B1 * m + (1.0 - ADAM_B1) * g
    v = ADAM_B2 * v + (1.0 - ADAM_B2) * _jnp.square(g)
    m_hat = m / (1.0 - ADAM_B1 ** ADAM_STEP)
    v_hat = v / (1.0 - ADAM_B2 ** ADAM_STEP)
    delta = -ADAM_LR * (m_hat / (_jnp.sqrt(v_hat) + ADAM_EPS) + ADAM_WD * w)
    return delta, m, v


def reference(x, g_mix, w_in, b_in, w_dw, b_dw, ln_g, ln_b, w_pool, s_pool, w_out, g_ffn, w_gate, w_up, w_down, g_final, loss_target, m_g_mix, m_w_in, m_b_in, m_w_dw, m_b_dw, m_ln_g, m_ln_b, m_w_pool, m_s_pool, m_w_out, m_g_ffn, m_w_gate, m_w_up, m_w_down, m_g_final, v_g_mix, v_w_in, v_b_in, v_w_dw, v_b_dw, v_ln_g, v_ln_b, v_w_pool, v_s_pool, v_w_out, v_g_ffn, v_w_gate, v_w_up, v_w_down, v_g_final):
    given = dict(x=x, g_mix=g_mix, w_in=w_in, b_in=b_in, w_dw=w_dw, b_dw=b_dw, ln_g=ln_g, ln_b=ln_b, w_pool=w_pool, s_pool=s_pool, w_out=w_out, g_ffn=g_ffn, w_gate=w_gate, w_up=w_up, w_down=w_down, g_final=g_final, loss_target=loss_target, m_g_mix=m_g_mix, m_w_in=m_w_in, m_b_in=m_b_in, m_w_dw=m_w_dw, m_b_dw=m_b_dw, m_ln_g=m_ln_g, m_ln_b=m_ln_b, m_w_pool=m_w_pool, m_s_pool=m_s_pool, m_w_out=m_w_out, m_g_ffn=m_g_ffn, m_w_gate=m_w_gate, m_w_up=m_w_up, m_w_down=m_w_down, m_g_final=m_g_final, v_g_mix=v_g_mix, v_w_in=v_w_in, v_b_in=v_b_in, v_w_dw=v_w_dw, v_b_dw=v_b_dw, v_ln_g=v_ln_g, v_ln_b=v_ln_b, v_w_pool=v_w_pool, v_s_pool=v_s_pool, v_w_out=v_w_out, v_g_ffn=v_g_ffn, v_w_gate=v_w_gate, v_w_up=v_w_up, v_w_down=v_w_down, v_g_final=v_g_final)
    weights = {n: given[n] for n in TWIN_WEIGHTS}
    shared = {n: given[n] for n in SHARED_INPUTS}
    per_example = {n: given[n] for n in ['x']}
    grad_fn = _jax.value_and_grad(_loss, argnums=(0, 1))

    def one_microbatch(ex, loss_target):
        ex = dict(ex)
        diff = ex.pop(TWIN_DIFF_INPUT)
        return grad_fn(weights, diff, {**shared, **ex}, loss_target)

    if N_MICROBATCH == 1:
        loss, (grad_w, grad_x) = one_microbatch(per_example, given["loss_target"])
    else:
        def body(carry, xs):
            loss_sum, grad_sum = carry
            l_k, (gw_k, gx_k) = one_microbatch(xs[0], xs[1])
            with _jax.named_scope("update"):
                return (loss_sum + l_k, _jax.tree.map(_jnp.add, grad_sum, gw_k)), gx_k

        init = (_jnp.zeros((), _jnp.float32), _jax.tree.map(_jnp.zeros_like, weights))
        (loss, grad_w), grad_x = _jax.lax.scan(body, init, (per_example, given["loss_target"]))
    with _jax.named_scope("update"):
        delta_w, new_m, new_v = {}, {}, {}
        for n in TWIN_WEIGHTS:
            delta_w[n], new_m[n], new_v[n] = _adamw(weights[n], grad_w[n], given["m_" + n], given["v_" + n])
    return (loss, grad_x, *[grad_w[n] for n in TWIN_WEIGHTS], *[delta_w[n] for n in TWIN_WEIGHTS],
            *[new_m[n] for n in TWIN_WEIGHTS], *[new_v[n] for n in TWIN_WEIGHTS])
```

```python
import functools

import jax
import jax.numpy as jnp
from jax import lax
from jax.experimental import pallas as pl
from jax.experimental.pallas import tpu as pltpu

F32 = jnp.float32
BF16 = jnp.bfloat16
MESH = pl.DeviceIdType.MESH
ANY = pl.BlockSpec(memory_space=pl.ANY)

RMS_EPS = 1e-6
LN_EPS = 1e-5
POOL_WINDOWS = (2, 4, 8, 16)
ADAM_LR = 0.001
ADAM_B1 = 0.9
ADAM_B2 = 0.999
ADAM_EPS = 1e-08
ADAM_WD = 0.01
ADAM_STEP = 10

LANES = 128
HALO = 32
CONV_ROWS = 64
VMEM_LIMIT = 56 * 1024 * 1024
PACK_W = 512
N_CHIPS = 4


def _params(n_grid, vmem=VMEM_LIMIT):
    return pltpu.CompilerParams(dimension_semantics=("arbitrary",) * n_grid, vmem_limit_bytes=vmem)


def _tile(n, want, mult=8):
    t = min(n, want)
    while n % t or t % mult:
        t -= 1
    return t


def _sigmoid(x):
    return 1.0 / (1.0 + jnp.exp(-x))


def _dot(a, b, dims):
    return lax.dot_general(a, b, (dims, ((), ())), preferred_element_type=F32)


NN = ((1,), (0,))
NT = ((1,), (1,))
TN = ((0,), (0,))


def _rms_bwd(x, g, dy):
    r = lax.rsqrt(jnp.mean(x * x, axis=-1, keepdims=True) + RMS_EPS)
    xh = x * r
    gy = dy * g
    dx = r * (gy - xh * jnp.mean(gy * xh, axis=-1, keepdims=True))
    return dx, dy * xh


def _accumulate(ref, first, val):
    @pl.when(first)
    def _():
        ref[...] = val

    @pl.when(jnp.logical_not(first))
    def _():
        ref[...] += val


def _in_proj(x, g_mix, w_in_b, b_in):
    T, D = x.shape
    CI = w_in_b.shape[1]
    tm = _tile(T, 512)

    def body(x_ref, g_ref, w_ref, b_ref, z_ref, xn_ref):
        xv = x_ref[...]
        r = lax.rsqrt(jnp.mean(xv * xv, axis=-1, keepdims=True) + RMS_EPS)
        xn = (xv * r * g_ref[...]).astype(BF16)
        xn_ref[...] = xn
        z_ref[...] = _dot(xn, w_ref[...], NN) + b_ref[...]

    return pl.pallas_call(
        body,
        name="in_proj",
        grid=(T // tm,),
        in_specs=[
            pl.BlockSpec((tm, D), lambda i: (i, 0)),
            pl.BlockSpec((1, D), lambda i: (0, 0)),
            pl.BlockSpec((D, CI), lambda i: (0, 0)),
            pl.BlockSpec((1, CI), lambda i: (0, 0)),
        ],
        out_specs=[pl.BlockSpec((tm, CI), lambda i: (i, 0)), pl.BlockSpec((tm, D), lambda i: (i, 0))],
        out_shape=[jax.ShapeDtypeStruct((T, CI), F32), jax.ShapeDtypeStruct((T, D), BF16)],
        compiler_params=_params(1),
    )(x, g_mix, w_in_b, b_in)


def _pool_mean_minus_token(p_scr, cs, w, cnt, tt):
    tok = p_scr[HALO : HALO + tt, cs]
    s = tok
    for d in range(1, w):
        s = s + p_scr[HALO - d : HALO - d + tt, cs]
    return s / cnt - tok


def _seq_fwd(z, w_dw4, b_dw, ln_g, ln_b, w_pool_b, s_pool):
    T, CI = z.shape
    CC = ln_g.shape[1]
    n_grp, G = w_pool_b.shape[0], w_pool_b.shape[-1]
    KW = w_dw4.shape[1]
    D = CC + n_grp * G
    tt = _tile(T, 256, HALO)
    per = tt // HALO

    def body(zc_ref, zp_ref, wdw_ref, bdw_ref, lng_ref, lnb_ref, wp_ref, sp_ref, y_ref, v_ref, u_scr, p_scr):
        i = pl.program_id(0)
        first = i == 0
        u_prev = zp_ref[:, 0:CC] * _sigmoid(zp_ref[:, CC : 2 * CC])
        u_scr[0:HALO, :] = jnp.where(first, 0.0, u_prev)
        p_scr[0:HALO, :] = jnp.where(first, 0.0, zp_ref[:, 2 * CC :])
        u_scr[HALO:, :] = zc_ref[:, 0:CC] * _sigmoid(zc_ref[:, CC : 2 * CC])
        p_scr[HALO:, :] = zc_ref[:, 2 * CC :]

        for j in range(CC // LANES):
            cs = slice(LANES * j, LANES * (j + 1))
            for rb in range(tt // CONV_ROWS):
                acc = jnp.zeros((CONV_ROWS, LANES), F32)
                for k in range(KW):
                    off = HALO - (KW - 1) + k + rb * CONV_ROWS
                    acc = acc + u_scr[off : off + CONV_ROWS, cs] * wdw_ref[j, k : k + 1, :]
                v_ref[rb * CONV_ROWS : (rb + 1) * CONV_ROWS, cs] = acc + bdw_ref[:, cs]

        v = v_ref[...]
        mu = jnp.mean(v, axis=-1, keepdims=True)
        d = v - mu
        var = jnp.mean(d * d, axis=-1, keepdims=True)
        ln = d * lax.rsqrt(var + LN_EPS) * lng_ref[...] + lnb_ref[...]
        y_ref[:, 0:CC] = (ln * _sigmoid(ln)).astype(BF16)

        tpos = i * tt + lax.broadcasted_iota(jnp.int32, (tt, 1), 0)
        for gi, w in enumerate(POOL_WINDOWS):
            cs = slice(G * gi, G * (gi + 1))
            cnt = jnp.minimum(tpos + 1, w).astype(F32)
            yi = _pool_mean_minus_token(p_scr, cs, w, cnt, tt)
            q = _dot(yi.astype(BF16), wp_ref[gi], NN)
            y_ref[:, CC + G * gi : CC + G * (gi + 1)] = (q * sp_ref[:, cs]).astype(BF16)

    const2 = lambda i: (0, 0)
    return pl.pallas_call(
        body,
        name="seq_fwd",
        grid=(T // tt,),
        in_specs=[
            pl.BlockSpec((tt, CI), lambda i: (i, 0)),
            pl.BlockSpec((HALO, CI), lambda i: (jnp.maximum(i * per - 1, 0), 0)),
            pl.BlockSpec(w_dw4.shape, lambda i: (0, 0, 0)),
            pl.BlockSpec((1, CC), const2),
            pl.BlockSpec((1, CC), const2),
            pl.BlockSpec((1, CC), const2),
            pl.BlockSpec(w_pool_b.shape, lambda i: (0, 0, 0)),
            pl.BlockSpec((1, n_grp * G), const2),
        ],
        out_specs=[pl.BlockSpec((tt, D), lambda i: (i, 0)), pl.BlockSpec((tt, CC), lambda i: (i, 0))],
        out_shape=[jax.ShapeDtypeStruct((T, D), BF16), jax.ShapeDtypeStruct((T, CC), F32)],
        scratch_shapes=[pltpu.VMEM((HALO + tt, CC), F32), pltpu.VMEM((HALO + tt, n_grp * G), F32)],
        compiler_params=_params(1),
    )(z, z, w_dw4, b_dw, ln_g, ln_b, w_pool_b, s_pool)


def _out_proj(y_b, x, w_out_b, g_ffn):
    T, D = x.shape
    tm = _tile(T, 512)

    def body(y_ref, x_ref, w_ref, g_ref, h1_ref, hn_ref):
        h1 = x_ref[...] + _dot(y_ref[...], w_ref[...], NN)
        h1_ref[...] = h1
        r = lax.rsqrt(jnp.mean(h1 * h1, axis=-1, keepdims=True) + RMS_EPS)
        hn_ref[...] = (h1 * r * g_ref[...]).astype(BF16)

    row = lambda i: (i, 0)
    return pl.pallas_call(
        body,
        name="out_proj",
        grid=(T // tm,),
        in_specs=[
            pl.BlockSpec((tm, y_b.shape[1]), row),
            pl.BlockSpec((tm, D), row),
            pl.BlockSpec(w_out_b.shape, lambda i: (0, 0)),
            pl.BlockSpec((1, D), lambda i: (0, 0)),
        ],
        out_specs=[pl.BlockSpec((tm, D), row), pl.BlockSpec((tm, D), row)],
        out_shape=[jax.ShapeDtypeStruct((T, D), F32), jax.ShapeDtypeStruct((T, D), BF16)],
        compiler_params=_params(1),
    )(y_b, x, w_out_b, g_ffn)


def _hidden_tile(F):
    return _tile(F, 1408, LANES)


def _gate_up(hn_b, wgT_b, wuT_b):
    T, D = hn_b.shape
    F = wgT_b.shape[0]
    tm, tf = _tile(T, 512), _hidden_tile(F)

    def body(hn_ref, wg_ref, wu_ref, g_ref, u_ref, a_ref):
        hn = hn_ref[...]
        gv = _dot(hn, wg_ref[...], NT)
        uv = _dot(hn, wu_ref[...], NT)
        g_ref[...] = gv.astype(BF16)
        u_ref[...] = uv.astype(BF16)
        a_ref[...] = (gv * _sigmoid(gv) * uv).astype(BF16)

    wspec = pl.BlockSpec((tf, D), lambda j, i: (j, 0))
    ospec = pl.BlockSpec((tm, tf), lambda j, i: (i, j))
    return pl.pallas_call(
        body,
        name="gate_up",
        grid=(F // tf, T // tm),
        in_specs=[pl.BlockSpec((tm, D), lambda j, i: (i, 0)), wspec, wspec],
        out_specs=[ospec, ospec, ospec],
        out_shape=[jax.ShapeDtypeStruct((T, F), BF16)] * 3,
        compiler_params=_params(2),
    )(hn_b, wgT_b, wuT_b)


def _down_loss(a_b, wd_b, h1, target, g_final):
    T, D = h1.shape
    F = a_b.shape[1]
    tm = _tile(T, 256)
    nt = T // tm

    def body(a_ref, w_ref, h1_ref, t_ref, g_ref, dh2_ref, dh2b_ref, loss_ref, dg_ref):
        i = pl.program_id(0)
        h2 = h1_ref[...] + _dot(a_ref[...], w_ref[...], NN)
        r = lax.rsqrt(jnp.mean(h2 * h2, axis=-1, keepdims=True) + RMS_EPS)
        g = g_ref[...]
        diff = h2 * r * g - t_ref[...]
        loss_ref[...] = jnp.full(loss_ref.shape, jnp.sum(diff * diff), F32)
        dh2, dg_rows = _rms_bwd(h2, g, diff * (1.0 / D))
        dh2_ref[...] = dh2
        dh2b_ref[...] = dh2.astype(BF16)
        _accumulate(dg_ref, i == 0, jnp.sum(dg_rows, axis=0, keepdims=True))

    row = lambda i: (i, 0)
    return pl.pallas_call(
        body,
        name="down_loss",
        grid=(nt,),
        in_specs=[
            pl.BlockSpec((tm, F), row),
            pl.BlockSpec((F, D), lambda i: (0, 0)),
            pl.BlockSpec((tm, D), row),
            pl.BlockSpec((tm, D), row),
            pl.BlockSpec((1, D), lambda i: (0, 0)),
        ],
        out_specs=[
            pl.BlockSpec((tm, D), row),
            pl.BlockSpec((tm, D), row),
            pl.BlockSpec((1, 1, LANES), lambda i: (i, 0, 0)),
            pl.BlockSpec((1, D), lambda i: (0, 0)),
        ],
        out_shape=[
            jax.ShapeDtypeStruct((T, D), F32),
            jax.ShapeDtypeStruct((T, D), BF16),
            jax.ShapeDtypeStruct((nt, 1, LANES), F32),
            jax.ShapeDtypeStruct((1, D), F32),
        ],
        compiler_params=_params(1),
    )(a_b, wd_b, h1, target, g_final)


def _ffn_bwd_act(dh2_b, wd_b, g_b, u_b):
    T, D = dh2_b.shape
    F = wd_b.shape[0]
    tm, tf = _tile(T, 512), _hidden_tile(F)

    def body(d_ref, w_ref, g_ref, u_ref, dg_ref, du_ref):
        da = _dot(d_ref[...], w_ref[...], NT)
        gv = g_ref[...].astype(F32)
        uv = u_ref[...].astype(F32)
        sg = _sigmoid(gv)
        silu = gv * sg
        dg_ref[...] = (da * uv * (sg * (1.0 + gv * (1.0 - sg)))).astype(BF16)
        du_ref[...] = (da * silu).astype(BF16)

    aspec = pl.BlockSpec((tm, tf), lambda j, i: (i, j))
    return pl.pallas_call(
        body,
        name="ffn_bwd_act",
        grid=(F // tf, T // tm),
        in_specs=[pl.BlockSpec((tm, D), lambda j, i: (i, 0)), pl.BlockSpec((tf, D), lambda j, i: (j, 0)), aspec, aspec],
        out_specs=[aspec, aspec],
        out_shape=[jax.ShapeDtypeStruct((T, F), BF16)] * 2,
        compiler_params=_params(2),
    )(dh2_b, wd_b, g_b, u_b)


def _ffn_bwd_in(dg_b, du_b, wgT_b, wuT_b, h1, dh2, g_ffn, w_out_b):
    T, D = h1.shape
    F = wgT_b.shape[0]
    DM = w_out_b.shape[0]
    tm = _tile(T, 256)

    def body(dg_ref, du_ref, wg_ref, wu_ref, h1_ref, dh2_ref, g_ref, wo_ref, dh1_ref, dh1b_ref, dy_ref, dgf_ref):
        i = pl.program_id(0)
        dhn = _dot(dg_ref[...], wg_ref[...], NN) + _dot(du_ref[...], wu_ref[...], NN)
        dx, dg_rows = _rms_bwd(h1_ref[...], g_ref[...], dhn)
        dh1 = dh2_ref[...] + dx
        dh1b = dh1.astype(BF16)
        dh1_ref[...] = dh1
        dh1b_ref[...] = dh1b
        dy_ref[...] = _dot(dh1b, wo_ref[...], NT)
        _accumulate(dgf_ref, i == 0, jnp.sum(dg_rows, axis=0, keepdims=True))

    row = lambda i: (i, 0)
    const = lambda i: (0, 0)
    return pl.pallas_call(
        body,
        name="ffn_bwd_in",
        grid=(T // tm,),
        in_specs=[
            pl.BlockSpec((tm, F), row),
            pl.BlockSpec((tm, F), row),
            pl.BlockSpec((F, D), const),
            pl.BlockSpec((F, D), const),
            pl.BlockSpec((tm, D), row),
            pl.BlockSpec((tm, D), row),
            pl.BlockSpec((1, D), const),
            pl.BlockSpec((DM, D), const),
        ],
        out_specs=[pl.BlockSpec((tm, D), row), pl.BlockSpec((tm, D), row), pl.BlockSpec((tm, DM), row), pl.BlockSpec((1, D), const)],
        out_shape=[
            jax.ShapeDtypeStruct((T, D), F32),
            jax.ShapeDtypeStruct((T, D), BF16),
            jax.ShapeDtypeStruct((T, DM), F32),
            jax.ShapeDtypeStruct((1, D), F32),
        ],
        compiler_params=_params(1),
    )(dg_b, du_b, wgT_b, wuT_b, h1, dh2, g_ffn, w_out_b)


def _seq_bwd(z, dy, v, w_dw4, ln_g, ln_b, w_pool_b, s_pool):
    T, CI = z.shape
    CC = ln_g.shape[1]
    n_grp, G = w_pool_b.shape[0], w_pool_b.shape[-1]
    CP = n_grp * G
    KW = w_dw4.shape[1]
    n_cc = CC // LANES
    D = CC + CP
    tt = _tile(T, 256, HALO)
    per = tt // HALO
    n_tiles = T // tt
    last_halo = T // HALO - 1

    def body(zc_ref, zp_ref, dyc_ref, dyn_ref, vc_ref, vn_ref, wdw_ref, lng_ref, lnb_ref, wp_ref, sp_ref,
             dz_ref, dwdw_ref, dbdw_ref, dlng_ref, dlnb_ref, dwp_ref, dsp_ref, dbin_ref,
             dv_scr, u_scr, p_scr, g_scr, dw_scr):
        i = pl.program_id(0)
        first = i == 0
        last = i == n_tiles - 1
        lng, lnb = lng_ref[...], lnb_ref[...]

        def conv_pre(vv, dyc):
            mu = jnp.mean(vv, axis=-1, keepdims=True)
            d = vv - mu
            rs = lax.rsqrt(jnp.mean(d * d, axis=-1, keepdims=True) + LN_EPS)
            xh = d * rs
            ln = xh * lng + lnb
            sg = _sigmoid(ln)
            dln = dyc * (sg * (1.0 + ln * (1.0 - sg)))
            dxh = dln * lng
            dv = rs * (dxh - jnp.mean(dxh, axis=-1, keepdims=True) - xh * jnp.mean(dxh * xh, axis=-1, keepdims=True))
            return dv, dln, xh

        dv_c, dln_c, xh_c = conv_pre(vc_ref[...], dyc_ref[:, 0:CC])
        dv_scr[0:tt, :] = dv_c
        dv_n, _, _ = conv_pre(vn_ref[...], dyn_ref[:, 0:CC])
        dv_scr[tt:, :] = jnp.where(last, 0.0, dv_n)
        _accumulate(dlng_ref, first, jnp.sum(dln_c * xh_c, axis=0, keepdims=True))
        _accumulate(dlnb_ref, first, jnp.sum(dln_c, axis=0, keepdims=True))
        _accumulate(dbdw_ref, first, jnp.sum(dv_c, axis=0, keepdims=True))

        u_scr[...] = zc_ref[:, 0:CC] * _sigmoid(zc_ref[:, CC : 2 * CC])

        @pl.when(first)
        def _():
            dw_scr[...] = jnp.zeros_like(dw_scr)

        for j in range(n_cc):
            cs = slice(LANES * j, LANES * (j + 1))
            gs = slice(CC + LANES * j, CC + LANES * (j + 1))
            dbin_a = jnp.zeros((1, LANES), F32)
            dbin_g = jnp.zeros((1, LANES), F32)
            for rb in range(tt // CONV_ROWS):
                rows = slice(rb * CONV_ROWS, (rb + 1) * CONV_ROWS)
                u_blk = u_scr[rows, cs]
                du = jnp.zeros((CONV_ROWS, LANES), F32)
                for k in range(KW):
                    off = rb * CONV_ROWS + (KW - 1) - k
                    d = dv_scr[off : off + CONV_ROWS, cs]
                    du = du + d * wdw_ref[j, k : k + 1, :]
                    dw_scr[j * HALO + k] += jnp.sum((u_blk * d).reshape(CONV_ROWS // 8, 8, LANES), axis=0)
                a = zc_ref[rows, cs]
                sg = _sigmoid(zc_ref[rows, gs])
                da = du * sg
                dgate = du * a * sg * (1.0 - sg)
                dz_ref[rows, cs] = da.astype(BF16)
                dz_ref[rows, gs] = dgate.astype(BF16)
                dbin_a = dbin_a + jnp.sum(da, axis=0, keepdims=True)
                dbin_g = dbin_g + jnp.sum(dgate, axis=0, keepdims=True)
            _accumulate(dbin_ref.at[:, cs], first, dbin_a)
            _accumulate(dbin_ref.at[:, gs], first, dbin_g)

        @pl.when(last)
        def _():
            dwdw_ref[...] = jnp.sum(dw_scr[...], axis=1).reshape(dwdw_ref.shape)

        p_scr[0:HALO, :] = jnp.where(first, 0.0, zp_ref[:, 2 * CC :])
        p_scr[HALO:, :] = zc_ref[:, 2 * CC :]
        tpos = i * tt + lax.broadcasted_iota(jnp.int32, (tt, 1), 0)
        for gi, w in enumerate(POOL_WINDOWS):
            cs = slice(G * gi, G * (gi + 1))
            ys = slice(CC + G * gi, CC + G * (gi + 1))
            ps = slice(2 * CC + G * gi, 2 * CC + G * (gi + 1))
            cnt = jnp.minimum(tpos + 1, w).astype(F32)
            yib = _pool_mean_minus_token(p_scr, cs, w, cnt, tt).astype(BF16)
            wp = wp_ref[gi]
            sp = sp_ref[:, cs]
            dyp = dyc_ref[:, ys]
            q = _dot(yib, wp, NN)
            _accumulate(dsp_ref.at[:, cs], first, jnp.sum(dyp * q, axis=0, keepdims=True))
            dq_c = (dyp * sp).astype(BF16)
            dq_n = (jnp.where(last, 0.0, dyn_ref[:, ys]) * sp).astype(BF16)
            _accumulate(dwp_ref.at[gi], first, _dot(yib, dq_c, TN))
            dyi_c = _dot(dq_c, wp, NT)
            g_scr[0:tt, cs] = dyi_c / cnt
            g_scr[tt:, cs] = _dot(dq_n, wp, NT) * (1.0 / w)
            dp = -dyi_c
            for d in range(w):
                dp = dp + g_scr[d : d + tt, cs]
            dz_ref[:, ps] = dp.astype(BF16)
            _accumulate(dbin_ref.at[:, ps], first, jnp.sum(dp, axis=0, keepdims=True))

    cur = lambda i: (i, 0)
    prev = lambda i: (jnp.maximum(i * per - 1, 0), 0)
    nxt = lambda i: (jnp.minimum((i + 1) * per, last_halo), 0)
    c2 = lambda i: (0, 0)
    c3 = lambda i: (0, 0, 0)
    return pl.pallas_call(
        body,
        name="seq_bwd",
        grid=(n_tiles,),
        in_specs=[
            pl.BlockSpec((tt, CI), cur),
            pl.BlockSpec((HALO, CI), prev),
            pl.BlockSpec((tt, D), cur),
            pl.BlockSpec((HALO, D), nxt),
            pl.BlockSpec((tt, CC), cur),
            pl.BlockSpec((HALO, CC), nxt),
            pl.BlockSpec(w_dw4.shape, c3),
            pl.BlockSpec((1, CC), c2),
            pl.BlockSpec((1, CC), c2),
            pl.BlockSpec(w_pool_b.shape, c3),
            pl.BlockSpec((1, CP), c2),
        ],
        out_specs=[
            pl.BlockSpec((tt, CI), cur),
            pl.BlockSpec((n_cc, HALO, LANES), c3),
            pl.BlockSpec((1, CC), c2),
            pl.BlockSpec((1, CC), c2),
            pl.BlockSpec((1, CC), c2),
            pl.BlockSpec((n_grp, G, G), c3),
            pl.BlockSpec((1, CP), c2),
            pl.BlockSpec((1, CI), c2),
        ],
        out_shape=[
            jax.ShapeDtypeStruct((T, CI), BF16),
            jax.ShapeDtypeStruct((n_cc, HALO, LANES), F32),
            jax.ShapeDtypeStruct((1, CC), F32),
            jax.ShapeDtypeStruct((1, CC), F32),
            jax.ShapeDtypeStruct((1, CC), F32),
            jax.ShapeDtypeStruct((n_grp, G, G), F32),
            jax.ShapeDtypeStruct((1, CP), F32),
            jax.ShapeDtypeStruct((1, CI), F32),
        ],
        scratch_shapes=[
            pltpu.VMEM((tt + HALO, CC), F32),
            pltpu.VMEM((tt, CC), F32),
            pltpu.VMEM((HALO + tt, CP), F32),
            pltpu.VMEM((tt + HALO, CP), F32),
            pltpu.VMEM((n_cc * HALO, 8, LANES), F32),
        ],
        compiler_params=_params(1),
    )(z, z, dy, dy, v, v, w_dw4, ln_g, ln_b, w_pool_b, s_pool)


def _in_proj_bwd(dz_b, w_in_b, x, dh1, g_mix):
    T, D = x.shape
    CI = w_in_b.shape[1]
    tm = _tile(T, 512)

    def body(dz_ref, w_ref, x_ref, dh1_ref, g_ref, dx_ref, dg_ref):
        i = pl.program_id(0)
        dxn = _dot(dz_ref[...], w_ref[...], NT)
        dx, dg_rows = _rms_bwd(x_ref[...], g_ref[...], dxn)
        dx_ref[...] = dh1_ref[...] + dx
        _accumulate(dg_ref, i == 0, jnp.sum(dg_rows, axis=0, keepdims=True))

    row = lambda i: (i, 0)
    const = lambda i: (0, 0)
    return pl.pallas_call(
        body,
        name="in_proj_bwd",
        grid=(T // tm,),
        in_specs=[
            pl.BlockSpec((tm, CI), row),
            pl.BlockSpec((D, CI), const),
            pl.BlockSpec((tm, D), row),
            pl.BlockSpec((tm, D), row),
            pl.BlockSpec((1, D), const),
        ],
        out_specs=[pl.BlockSpec((tm, D), row), pl.BlockSpec((1, D), const)],
        out_shape=[jax.ShapeDtypeStruct((T, D), F32), jax.ShapeDtypeStruct((1, D), F32)],
        compiler_params=_params(1),
    )(dz_b, w_in_b, x, dh1, g_mix)


def _weight_grad(name, a_b, b_b):
    T, N1 = a_b.shape
    N2 = b_b.shape[1]
    t1 = _tile(N1, 1408, LANES)
    tk = _tile(T, 1024)
    nk = T // tk

    def body(a_ref, b_ref, o_ref, acc):
        k = pl.program_id(1)
        _accumulate(acc, k == 0, _dot(a_ref[...], b_ref[...], TN))

        @pl.when(k == nk - 1)
        def _():
            o_ref[...] = acc[...].astype(BF16)

    return pl.pallas_call(
        body,
        name=name,
        grid=(N1 // t1, nk),
        in_specs=[pl.BlockSpec((tk, t1), lambda n, k: (k, n)), pl.BlockSpec((tk, N2), lambda n, k: (k, 0))],
        out_specs=pl.BlockSpec((t1, N2), lambda n, k: (n, 0)),
        out_shape=jax.ShapeDtypeStruct((N1, N2), BF16),
        scratch_shapes=[pltpu.VMEM((t1, N2), F32)],
        compiler_params=_params(2),
    )(a_b, b_b)


def _chip_window(ref, how, k):
    if how == "all":
        return ref
    if how == "lead":
        return ref.at[k]
    n = ref.shape[0 if how == "rows" else 1] // N_CHIPS
    start = pl.multiple_of(k * n, LANES if how == "cols" else 16)
    if how == "rows":
        return ref.at[pl.ds(start, n), :]
    return ref.at[:, pl.ds(start, n)]


def _other_chips(x, y):
    return [(1 - x, y), (x, 1 - y), (1 - x, 1 - y)]


def _gather_weights(shards, hows, full_shapes):
    n = len(shards)

    def body(*refs):
        ins, outs = refs[:n], refs[n : 2 * n]
        send_sems, recv_sems, local_sems = refs[2 * n :]
        x, y, c = lax.axis_index("x"), lax.axis_index("y"), lax.axis_index("c")
        me = 2 * x + y
        chips = _other_chips(x, y)
        locals_, sends, recvs = [], [], []
        for a in range(n):
            loc = pltpu.make_async_copy(ins[a], _chip_window(outs[a], hows[a], me), local_sems.at[a])
            loc.start()
            locals_.append(loc)
            for j, (px, py) in enumerate(chips):
                s = 3 * a + j
                cp = pltpu.make_async_remote_copy(
                    src_ref=ins[a], dst_ref=_chip_window(outs[a], hows[a], me), send_sem=send_sems.at[s],
                    recv_sem=recv_sems.at[s], device_id=(px, py, c), device_id_type=MESH)
                cp.start()
                sends.append(cp)
                recvs.append(pltpu.make_async_remote_copy(
                    src_ref=ins[a], dst_ref=_chip_window(outs[a], hows[a], 2 * px + py), send_sem=send_sems.at[s],
                    recv_sem=recv_sems.at[s], device_id=(px, py, c), device_id_type=MESH))
        for cp in recvs:
            cp.wait_recv()
        for cp in sends:
            cp.wait_send()
        for loc in locals_:
            loc.wait()

    return pl.pallas_call(
        body,
        name="gather_weights",
        in_specs=[ANY] * n,
        out_specs=[ANY] * n,
        out_shape=[jax.ShapeDtypeStruct(s, a.dtype) for s, a in zip(full_shapes, shards)],
        scratch_shapes=[pltpu.SemaphoreType.DMA((3 * n,)), pltpu.SemaphoreType.DMA((3 * n,)), pltpu.SemaphoreType.DMA((n,))],
        compiler_params=pltpu.CompilerParams(has_side_effects=True),
    )(*shards)


def _scatter_grads(fulls, hows):
    n = len(fulls)

    def part_shape(a, how):
        if how == "all":
            return a.shape
        if how == "rows":
            return (a.shape[0] // N_CHIPS, a.shape[1])
        return (a.shape[0], a.shape[1] // N_CHIPS)

    def body(*refs):
        ins, outs = refs[:n], refs[n : 2 * n]
        send_sems, recv_sems, local_sems = refs[2 * n :]
        x, y, c = lax.axis_index("x"), lax.axis_index("y"), lax.axis_index("c")
        me = 2 * x + y
        copies = []
        for a in range(n):
            loc = pltpu.make_async_copy(_chip_window(ins[a], hows[a], me), outs[a].at[3], local_sems.at[a])
            loc.start()
            copies.append(loc)
            for j, (px, py) in enumerate(_other_chips(x, y)):
                s = 3 * a + j
                cp = pltpu.make_async_remote_copy(
                    src_ref=_chip_window(ins[a], hows[a], 2 * px + py), dst_ref=outs[a].at[j], send_sem=send_sems.at[s],
                    recv_sem=recv_sems.at[s], device_id=(px, py, c), device_id_type=MESH)
                cp.start()
                copies.append(cp)
        for cp in copies:
            cp.wait()

    return pl.pallas_call(
        body,
        name="scatter_grads",
        in_specs=[ANY] * n,
        out_specs=[ANY] * n,
        out_shape=[jax.ShapeDtypeStruct((N_CHIPS,) + part_shape(a, h), a.dtype) for a, h in zip(fulls, hows)],
        scratch_shapes=[pltpu.SemaphoreType.DMA((3 * n,)), pltpu.SemaphoreType.DMA((3 * n,)), pltpu.SemaphoreType.DMA((n,))],
        compiler_params=pltpu.CompilerParams(has_side_effects=True),
    )(*fulls)


def _sum_parts(name, parts):
    _, R, C = parts.shape
    tr = _tile(R, 512)

    def body(p_ref, o_ref):
        f = lambda q: p_ref[q].astype(F32)
        o_ref[...] = (f(3) + f(0)) + (f(1) + f(2))

    return pl.pallas_call(
        body,
        name=name,
        grid=(R // tr,),
        in_specs=[pl.BlockSpec((N_CHIPS, tr, C), lambda i: (0, i, 0))],
        out_specs=pl.BlockSpec((tr, C), lambda i: (i, 0)),
        out_shape=jax.ShapeDtypeStruct((R, C), F32),
        compiler_params=_params(1),
    )(parts)


def _swap_with_sibling(arrays):
    n = len(arrays)

    def body(*refs):
        ins, outs = refs[:n], refs[n : 2 * n]
        send_sems, recv_sems = refs[2 * n :]
        sibling = (lax.axis_index("x"), lax.axis_index("y"), 1 - lax.axis_index("c"))
        copies = []
        for a in range(n):
            cp = pltpu.make_async_remote_copy(
                src_ref=ins[a], dst_ref=outs[a], send_sem=send_sems.at[a], recv_sem=recv_sems.at[a],
                device_id=sibling, device_id_type=MESH)
            cp.start()
            copies.append(cp)
        for cp in copies:
            cp.wait()

    return pl.pallas_call(
        body,
        name="swap_with_sibling",
        in_specs=[ANY] * n,
        out_specs=[ANY] * n,
        out_shape=[jax.ShapeDtypeStruct(a.shape, a.dtype) for a in arrays],
        scratch_shapes=[pltpu.SemaphoreType.DMA((n,)), pltpu.SemaphoreType.DMA((n,))],
        compiler_params=pltpu.CompilerParams(has_side_effects=True),
    )(*arrays)


_M_CORR = 1.0 - ADAM_B1**ADAM_STEP
_V_CORR = 1.0 - ADAM_B2**ADAM_STEP


def _adamw_math(w, g, m, v):
    m = ADAM_B1 * m + (1.0 - ADAM_B1) * g
    v = ADAM_B2 * v + (1.0 - ADAM_B2) * (g * g)
    delta = -ADAM_LR * ((m / _M_CORR) / (jnp.sqrt(v / _V_CORR) + ADAM_EPS) + ADAM_WD * w)
    return delta, m, v


def _adamw(name, w, m, v, g_here, g_there):
    R, C = w.shape
    tr = _tile(R, 256)

    def body(w_ref, m_ref, v_ref, ga_ref, gb_ref, g_ref, d_ref, nm_ref, nv_ref):
        g = ga_ref[...] + gb_ref[...]
        g_ref[...] = g
        d_ref[...], nm_ref[...], nv_ref[...] = _adamw_math(w_ref[...], g, m_ref[...], v_ref[...])

    spec = pl.BlockSpec((tr, C), lambda i: (i, 0))
    return pl.pallas_call(
        body,
        name=name,
        grid=(R // tr,),
        in_specs=[spec] * 5,
        out_specs=[spec] * 4,
        out_shape=[jax.ShapeDtypeStruct((R, C), F32)] * 4,
        compiler_params=_params(1),
    )(w, m, v, g_here, g_there)


class _PackLayout:
    def __init__(self, n_cc, n_grp, G, widths):
        self.dw_rows = (0, HALO)
        self.wp_rows = (HALO, HALO + G)
        self.n_cc, self.n_grp, self.G = n_cc, n_grp, G
        self.vec = {}
        r = HALO + G
        for name, width in widths:
            self.vec[name] = (r, width)
            r += width // PACK_W
        self.rows = -(-r // 8) * 8


def _pack_small(layout, dwdw, dwp, vecs):
    names = list(vecs)

    def body(*refs):
        dw_ref, wp_ref = refs[0], refs[1]
        vec_refs = refs[2 : 2 + len(names)]
        o_ref = refs[-1]
        o_ref[...] = jnp.zeros_like(o_ref)
        for j in range(layout.n_cc):
            o_ref[layout.dw_rows[0] : layout.dw_rows[1], j * LANES : (j + 1) * LANES] = dw_ref[j]
        for i in range(layout.n_grp):
            o_ref[layout.wp_rows[0] : layout.wp_rows[1], i * layout.G : (i + 1) * layout.G] = wp_ref[i]
        for name, ref in zip(names, vec_refs):
            r, width = layout.vec[name]
            for h in range(width // PACK_W):
                o_ref[r + h : r + h + 1, :] = ref[:, h * PACK_W : (h + 1) * PACK_W]

    return pl.pallas_call(
        body,
        name="pack_small",
        out_shape=jax.ShapeDtypeStruct((layout.rows, PACK_W), F32),
        compiler_params=_params(0),
    )(dwdw, dwp, *[vecs[k] for k in names])


def _adamw_small(layout, g_here, g_there, w_dw, m_dw, v_dw, w_pool, m_pool, v_pool, vec_w, vec_m, vec_v):
    names = list(vec_w)
    nv = len(names)

    def body(*refs):
        ga_ref, gb_ref = refs[0], refs[1]
        wdw, mdw, vdw, wp, mp, vp = refs[2:8]
        vw, vm, vv = refs[8 : 8 + nv], refs[8 + nv : 8 + 2 * nv], refs[8 + 2 * nv : 8 + 3 * nv]
        outs = refs[8 + 3 * nv :]
        acc = outs[-1]
        acc[...] = ga_ref[...] + gb_ref[...]

        def emit(o, g, w, m, v, idx=()):
            res = (g,) + _adamw_math(w, g, m, v)
            for ref, val in zip(o, res):
                ref[idx] = val

        me = 2 * lax.axis_index("x") + lax.axis_index("y")
        for j in range(layout.n_cc):

            @pl.when(me == j)
            def _(j=j):
                g = acc[layout.dw_rows[0] : layout.dw_rows[1], j * LANES : (j + 1) * LANES]
                emit(outs[0:4], g, wdw[...], mdw[...], vdw[...], idx=...)

        for i in range(layout.n_grp):
            g = acc[layout.wp_rows[0] : layout.wp_rows[1], i * layout.G : (i + 1) * layout.G]
            emit(outs[4:8], g, wp[i], mp[i], vp[i], idx=i)
        for q, name in enumerate(names):
            r, width = layout.vec[name]
            for h in range(width // PACK_W):
                ls = slice(h * PACK_W, (h + 1) * PACK_W)
                g = acc[r + h : r + h + 1, :]
                emit(outs[8 + 4 * q : 12 + 4 * q], g, vw[q][:, ls], vm[q][:, ls], vv[q][:, ls], idx=(slice(None), ls))

    shapes = [w_dw.shape] * 4 + [w_pool.shape] * 4
    for name in names:
        shapes += [vec_w[name].shape] * 4
    return pl.pallas_call(
        body,
        name="adamw_small",
        out_shape=[jax.ShapeDtypeStruct(s, F32) for s in shapes],
        scratch_shapes=[pltpu.VMEM(g_here.shape, F32)],
        compiler_params=_params(0),
    )(g_here, g_there, w_dw, m_dw, v_dw, w_pool, m_pool, v_pool,
      *[vec_w[k] for k in names], *[vec_m[k] for k in names], *[vec_v[k] for k in names])


def kernel(x, g_mix, w_in, b_in, w_dw, b_dw, ln_g, ln_b, w_pool, s_pool, w_out, g_ffn, w_gate, w_up, w_down, g_final, loss_target, m_g_mix, m_w_in, m_b_in, m_w_dw, m_b_dw, m_ln_g, m_ln_b, m_w_pool, m_s_pool, m_w_out, m_g_ffn, m_w_gate, m_w_up, m_w_down, m_g_final, v_g_mix, v_w_in, v_b_in, v_w_dw, v_b_dw, v_ln_g, v_ln_b, v_w_pool, v_s_pool, v_w_out, v_g_ffn, v_w_gate, v_w_up, v_w_down, v_g_final):
    x2 = x[0]
    target = loss_target[0]
    T, D = x2.shape
    w_in2, w_out2, w_gate2, w_up2, w_down2, w_dw2 = w_in[0], w_out[0], w_gate[0], w_up[0], w_down[0], w_dw[0]
    CI = w_in2.shape[1] * N_CHIPS
    F = w_down2.shape[0] * N_CHIPS
    KW, dw_cols = w_dw2.shape
    n_grp, G = w_pool.shape[1], w_pool.shape[-1]
    w_pool3 = w_pool[0]
    g_final2 = g_final.reshape(1, D)

    shards = [w_in2.astype(BF16), w_out2.astype(BF16), w_gate2.T.astype(BF16), w_up2.T.astype(BF16), w_down2.astype(BF16), w_dw2]
    hows = ["cols", "rows", "rows", "rows", "rows", "lead"]
    full_shapes = [(D, CI), (w_out2.shape[0] * N_CHIPS, D), (F, D), (F, D), (F, D), (N_CHIPS, KW, dw_cols)]
    w_in_b, w_out_b, wgT_b, wuT_b, wd_b, w_dw4 = _gather_weights(shards, hows, full_shapes)
    w_pool_b = w_pool3.astype(BF16)

    z, xn_b = _in_proj(x2, g_mix, w_in_b, b_in)
    y_b, v = _seq_fwd(z, w_dw4, b_dw, ln_g, ln_b, w_pool_b, s_pool)
    h1, hn_b = _out_proj(y_b, x2, w_out_b, g_ffn)
    g_b, u_b, a_b = _gate_up(hn_b, wgT_b, wuT_b)
    dh2, dh2_b, loss_parts, d_g_final = _down_loss(a_b, wd_b, h1, target, g_final2)
    loss = lax.psum(jnp.sum(loss_parts[:, 0, 0]) * (0.5 / D), ("x", "y", "c"))

    dg_b, du_b = _ffn_bwd_act(dh2_b, wd_b, g_b, u_b)
    dh1, dh1_b, dy, d_g_ffn = _ffn_bwd_in(dg_b, du_b, wgT_b, wuT_b, h1, dh2, g_ffn, w_out_b)
    dz_b, d_wdw, d_bdw, d_lng, d_lnb, d_wp, d_sp, d_bin = _seq_bwd(z, dy, v, w_dw4, ln_g, ln_b, w_pool_b, s_pool)
    grad_x, d_g_mix = _in_proj_bwd(dz_b, w_in_b, x2, dh1, g_mix)
    gw_down = _weight_grad("grad_w_down", a_b, dh2_b)
    gw_gateT = _weight_grad("grad_w_gate", dg_b, hn_b)
    gw_upT = _weight_grad("grad_w_up", du_b, hn_b)
    gw_out = _weight_grad("grad_w_out", y_b, dh1_b)
    gw_in = _weight_grad("grad_w_in", xn_b, dz_b)

    vec_grads = {"b_dw": d_bdw, "ln_g": d_lng, "ln_b": d_lnb, "s_pool": d_sp, "g_mix": d_g_mix, "g_ffn": d_g_ffn,
                 "g_final": d_g_final, "b_in": d_bin}
    layout = _PackLayout(dw_cols * N_CHIPS // LANES, n_grp, G, [(k, a.shape[1]) for k, a in vec_grads.items()])
    pack = _pack_small(layout, d_wdw, d_wp, vec_grads)
    parts = _scatter_grads([gw_in, gw_out, gw_gateT, gw_upT, gw_down, pack], ["cols", "rows", "rows", "rows", "rows", "all"])
    names = ["w_in", "w_out", "w_gate", "w_up", "w_down", "small"]
    here = [_sum_parts("sum_" + k, p) for k, p in zip(names, parts)]
    there = _swap_with_sibling(here)
    s_here = dict(zip(names, here))
    s_there = dict(zip(names, there))

    res = {}
    res["w_in"] = _adamw("adamw_w_in", w_in2, m_w_in[0], v_w_in[0], s_here["w_in"], s_there["w_in"])
    res["w_out"] = _adamw("adamw_w_out", w_out2, m_w_out[0], v_w_out[0], s_here["w_out"], s_there["w_out"])
    res["w_gate"] = _adamw("adamw_w_gate", w_gate2, m_w_gate[0], v_w_gate[0], s_here["w_gate"].T, s_there["w_gate"].T)
    res["w_up"] = _adamw("adamw_w_up", w_up2, m_w_up[0], v_w_up[0], s_here["w_up"].T, s_there["w_up"].T)
    res["w_down"] = _adamw("adamw_w_down", w_down2, m_w_down[0], v_w_down[0], s_here["w_down"], s_there["w_down"])

    pad_dw = lambda a: jnp.pad(a[0], ((0, HALO - KW), (0, 0)))
    vec_w = {"b_dw": b_dw, "ln_g": ln_g, "ln_b": ln_b, "s_pool": s_pool, "g_mix": g_mix, "g_ffn": g_ffn, "g_final": g_final2, "b_in": b_in}
    vec_m = {"b_dw": m_b_dw, "ln_g": m_ln_g, "ln_b": m_ln_b, "s_pool": m_s_pool, "g_mix": m_g_mix, "g_ffn": m_g_ffn,
             "g_final": m_g_final.reshape(1, D), "b_in": m_b_in}
    vec_v = {"b_dw": v_b_dw, "ln_g": v_ln_g, "ln_b": v_ln_b, "s_pool": v_s_pool, "g_mix": v_g_mix, "g_ffn": v_g_ffn,
             "g_final": v_g_final.reshape(1, D), "b_in": v_b_in}
    small = _adamw_small(layout, s_here["small"], s_there["small"], pad_dw(w_dw), pad_dw(m_w_dw), pad_dw(v_w_dw),
                         w_pool3, m_w_pool[0], v_w_pool[0], vec_w, vec_m, vec_v)
    res["w_dw"] = [a[:KW][None] for a in small[0:4]]
    res["w_pool"] = [a[None] for a in small[4:8]]
    for q, k in enumerate(vec_w):
        res[k] = list(small[8 + 4 * q : 12 + 4 * q])
    res["g_final"] = [a.reshape(D) for a in res["g_final"]]
    for k in ("w_in", "w_out", "w_gate", "w_up", "w_down"):
        res[k] = [a[None] for a in res[k]]

    order = ["g_mix", "w_in", "b_in", "w_dw", "b_dw", "ln_g", "ln_b", "w_pool", "s_pool", "w_out", "g_ffn", "w_gate", "w_up", "w_down", "g_final"]
    outs = [loss, grad_x[None]]
    for q in range(4):
        outs += [res[k][q] for k in order]
    return tuple(outs)
```

```python
import jax
import jax.numpy as jnp
from jax import lax
from jax.experimental import pallas as pl
from jax.experimental.pallas import tpu as pltpu

F32 = jnp.float32
BF16 = jnp.bfloat16
MESH = pl.DeviceIdType.MESH
ANY = pl.BlockSpec(memory_space=pl.ANY)

RMS_EPS = 1e-6
LN_EPS = 1e-5
POOL_WINDOWS = (2, 4, 8, 16)
ADAM_LR = 0.001
ADAM_B1 = 0.9
ADAM_B2 = 0.999
ADAM_EPS = 1e-08
ADAM_WD = 0.01
ADAM_STEP = 10

LANES = 128
HALO = 32
CONV_ROWS = 64
VMEM_LIMIT = 56 * 1024 * 1024
PACK_W = 512
N_CHIPS = 4
N_DEV = 8


def _tile(n, want, mult=8):
    t = min(n, want)
    while n % t or t % mult:
        t -= 1
    return t


def _sigmoid(x):
    return 1.0 / (1.0 + jnp.exp(-x))


def _dot(a, b, dims):
    return lax.dot_general(a, b, (dims, ((), ())), preferred_element_type=F32)


NN = ((1,), (0,))
NT = ((1,), (1,))
TN = ((0,), (0,))


def _rms_bwd(x, g, dy):
    r = lax.rsqrt(jnp.mean(x * x, axis=-1, keepdims=True) + RMS_EPS)
    xh = x * r
    gy = dy * g
    dx = r * (gy - xh * jnp.mean(gy * xh, axis=-1, keepdims=True))
    return dx, dy * xh


def _accumulate(ref, first, val):
    @pl.when(first)
    def _():
        ref[...] = val

    @pl.when(jnp.logical_not(first))
    def _():
        ref[...] += val


def _place():
    return lax.axis_index("x"), lax.axis_index("y"), lax.axis_index("c")


def _other_chips(x, y):
    return [(1 - x, y), (x, 1 - y), (1 - x, 1 - y)]


def _rows(ref, start, n):
    return ref.at[pl.ds(pl.multiple_of(start, 16), n)]


def _window(ref, how, k, c=None):
    if how == "all":
        return ref
    if how == "lead":
        return ref.at[k]
    if how == "rows":
        n = ref.shape[0] // N_CHIPS
        if c is None:
            return _rows(ref, k * n, n)
        return _rows(ref, k * n + c * (n // 2), n // 2)
    n = ref.shape[1] // N_CHIPS
    cols = pl.ds(pl.multiple_of(k * n, LANES), n)
    if c is None:
        return ref.at[:, cols]
    h = ref.shape[0] // 2
    return ref.at[pl.ds(pl.multiple_of(c * h, 16), h), cols]


def _remote(src, dst, sems, s, device):
    return pltpu.make_async_remote_copy(
        src_ref=src, dst_ref=dst, send_sem=sems.at[s], recv_sem=sems.at[s + 1], device_id=device, device_id_type=MESH)


class _GatherIci:
    def __init__(self, shards, splits):
        self.shards, self.splits = list(shards), list(splits)

    def inputs(self):
        return self.shards

    def out_shapes(self):
        return [jax.ShapeDtypeStruct((3, a.shape[0] // 2 if sp else a.shape[0]) + a.shape[1:], a.dtype)
                for a, sp in zip(self.shards, self.splits)]

    def n_sems(self):
        return 6 * len(self.shards)

    def build(self, ins, outs, sems, base):
        x, y, c = _place()
        starts, waits = [], []
        for a, (src, sp) in enumerate(zip(ins, self.splits)):
            if sp:
                src = _rows(src, c * (src.shape[0] // 2), src.shape[0] // 2)
            for j, (px, py) in enumerate(_other_chips(x, y)):
                cp = _remote(src, outs[a].at[j], sems, base + 6 * a + 2 * j, (px, py, c))
                starts.append(cp.start)
                waits += [cp.wait_recv, cp.wait_send]
        return starts, waits


class _GatherD2d:
    def __init__(self, stages, shards, hows, splits, full_shapes):
        self.stages, self.shards = list(stages), list(shards)
        self.hows, self.splits, self.full_shapes = list(hows), list(splits), list(full_shapes)

    def inputs(self):
        return self.stages + self.shards

    def out_shapes(self):
        return [jax.ShapeDtypeStruct(s, a.dtype) for s, a in zip(self.full_shapes, self.shards)]

    def n_sems(self):
        return 10 * len(self.shards)

    def build(self, ins, outs, sems, base):
        n = len(self.shards)
        stages, shards = ins[:n], ins[n:]
        x, y, c = _place()
        me = 2 * x + y
        starts, waits = [], []
        for a in range(n):
            how, sp, full = self.hows[a], self.splits[a], outs[a]
            s0 = base + 10 * a
            own = pltpu.make_async_copy(shards[a], _window(full, how, me), sems.at[s0])
            starts.append(own.start)
            waits.append(own.wait)
            for j, (px, py) in enumerate(_other_chips(x, y)):
                k = 2 * px + py
                mine = _window(full, how, k, c if sp else None)
                loc = pltpu.make_async_copy(stages[a].at[j], mine, sems.at[s0 + 1 + j])
                starts.append(loc.start)
                waits.append(loc.wait)
                if sp:
                    s = s0 + 4 + 2 * j
                    sib = (x, y, 1 - c)
                    cp = _remote(stages[a].at[j], mine, sems, s, sib)
                    landing = _remote(stages[a].at[j], _window(full, how, k, 1 - c), sems, s, sib)
                    starts.append(cp.start)
                    waits += [landing.wait_recv, cp.wait_send]
        return starts, waits


class _Scatter:
    def __init__(self, fulls, hows):
        self.fulls, self.hows = list(fulls), list(hows)

    def inputs(self):
        return self.fulls

    def _part(self, a, how):
        if how == "all":
            return a.shape
        if how == "rows":
            return (a.shape[0] // N_CHIPS, a.shape[1])
        return (a.shape[0], a.shape[1] // N_CHIPS)

    def out_shapes(self):
        return [jax.ShapeDtypeStruct((N_CHIPS,) + self._part(a, h), a.dtype) for a, h in zip(self.fulls, self.hows)]

    def n_sems(self):
        return 7 * len(self.fulls)

    def build(self, ins, outs, sems, base):
        x, y, c = _place()
        me = 2 * x + y
        starts, waits = [], []
        for a, how in enumerate(self.hows):
            s0 = base + 7 * a
            own = pltpu.make_async_copy(_window(ins[a], how, me), outs[a].at[3], sems.at[s0])
            starts.append(own.start)
            waits.append(own.wait)
            for j, (px, py) in enumerate(_other_chips(x, y)):
                cp = _remote(_window(ins[a], how, 2 * px + py), outs[a].at[j], sems, s0 + 1 + 2 * j, (px, py, c))
                starts.append(cp.start)
                waits += [cp.wait_recv, cp.wait_send]
        return starts, waits


class _Swap:
    def __init__(self, arrays):
        self.arrays = list(arrays)

    def inputs(self):
        return self.arrays

    def out_shapes(self):
        return [jax.ShapeDtypeStruct(a.shape, a.dtype) for a in self.arrays]

    def n_sems(self):
        return 2 * len(self.arrays)

    def build(self, ins, outs, sems, base):
        x, y, c = _place()
        starts, waits = [], []
        for a in range(len(ins)):
            cp = _remote(ins[a], outs[a], sems, base + 2 * a, (x, y, 1 - c))
            starts.append(cp.start)
            waits += [cp.wait_recv, cp.wait_send]
        return starts, waits


def _call(name, body, grid, in_specs, out_specs, out_shape, args, scratch=(), comm=()):
    comm = list(comm)
    n_in, n_out, n_scr = len(args), len(out_shape), len(scratch)
    c_in = [a for op in comm for a in op.inputs()]
    c_out = [s for op in comm for s in op.out_shapes()]
    n_sems = sum(op.n_sems() for op in comm)

    def wrapped(*refs):
        ins = refs[:n_in]
        cin = refs[n_in : n_in + len(c_in)]
        o0 = n_in + len(c_in)
        outs = refs[o0 : o0 + n_out]
        cout = refs[o0 + n_out : o0 + n_out + len(c_out)]
        s0 = o0 + n_out + len(c_out)
        scr = refs[s0 : s0 + n_scr]

        def copies():
            sems = refs[s0 + n_scr]
            starts, waits = [], []
            i_in = i_out = base = 0
            for op in comm:
                ni, no = len(op.inputs()), len(op.out_shapes())
                s, w = op.build(cin[i_in : i_in + ni], cout[i_out : i_out + no], sems, base)
                starts += s
                waits += w
                i_in, i_out, base = i_in + ni, i_out + no, base + op.n_sems()
            return starts, waits

        def run_starts():
            for start in copies()[0]:
                start()

        def run_waits():
            for wait in copies()[1]:
                wait()

        if comm and grid:
            first = last = True
            for d, n in enumerate(grid):
                first = jnp.logical_and(first, pl.program_id(d) == 0)
                last = jnp.logical_and(last, pl.program_id(d) == n - 1)
            pl.when(first)(run_starts)
        elif comm:
            run_starts()
        if body is not None:
            body(*ins, *outs, *scr)
        if comm and grid:
            pl.when(last)(run_waits)
        elif comm:
            run_waits()

    res = pl.pallas_call(
        wrapped,
        name=name,
        grid=grid,
        in_specs=list(in_specs) + [ANY] * len(c_in),
        out_specs=list(out_specs) + [ANY] * len(c_out),
        out_shape=list(out_shape) + c_out,
        scratch_shapes=list(scratch) + ([pltpu.SemaphoreType.DMA((n_sems,))] if comm else []),
        compiler_params=pltpu.CompilerParams(dimension_semantics=("arbitrary",) * len(grid), vmem_limit_bytes=VMEM_LIMIT),
    )(*args, *c_in)
    return tuple(res[:n_out]), tuple(res[n_out:])


def _gather_now(shards, hows, splits, full_shapes):
    ici = _GatherIci(shards, splits)
    n = len(shards)

    def body(*refs):
        ins, stages, fulls, sems = refs[:n], refs[n : 2 * n], refs[2 * n : 3 * n], refs[3 * n]
        d2d = _GatherD2d(stages, shards, hows, splits, full_shapes)
        for op, op_in, op_out, base in ((ici, ins, stages, 0), (d2d, stages + ins, fulls, ici.n_sems())):
            starts, waits = op.build(op_in, op_out, sems, base)
            for start in starts:
                start()
            for wait in waits:
                wait()

    d2d_shapes = _GatherD2d(shards, shards, hows, splits, full_shapes)
    res = pl.pallas_call(
        body,
        name="gather_first",
        in_specs=[ANY] * n,
        out_specs=[ANY] * (2 * n),
        out_shape=ici.out_shapes() + d2d_shapes.out_shapes(),
        scratch_shapes=[pltpu.SemaphoreType.DMA((ici.n_sems() + d2d_shapes.n_sems(),))],
    )(*shards)
    return res[n:]


def _in_proj(x, g_mix, w_in_b, b_in, comm=()):
    T, D = x.shape
    CI = w_in_b.shape[1]
    tm = _tile(T, 512)

    def body(x_ref, g_ref, w_ref, b_ref, z_ref, xn_ref):
        xv = x_ref[...]
        r = lax.rsqrt(jnp.mean(xv * xv, axis=-1, keepdims=True) + RMS_EPS)
        xn = (xv * r * g_ref[...]).astype(BF16)
        xn_ref[...] = xn
        z_ref[...] = _dot(xn, w_ref[...], NN) + b_ref[...]

    return _call(
        "in_proj",
        body,
        (T // tm,),
        [
            pl.BlockSpec((tm, D), lambda i: (i, 0)),
            pl.BlockSpec((1, D), lambda i: (0, 0)),
            pl.BlockSpec((D, CI), lambda i: (0, 0)),
            pl.BlockSpec((1, CI), lambda i: (0, 0)),
        ],
        [pl.BlockSpec((tm, CI), lambda i: (i, 0)), pl.BlockSpec((tm, D), lambda i: (i, 0))],
        [jax.ShapeDtypeStruct((T, CI), F32), jax.ShapeDtypeStruct((T, D), BF16)],
        (x, g_mix, w_in_b, b_in),
        comm=comm,
    )


def _pool_mean_minus_token(p_scr, cs, w, cnt, tt):
    tok = p_scr[HALO : HALO + tt, cs]
    s = tok
    for d in range(1, w):
        s = s + p_scr[HALO - d : HALO - d + tt, cs]
    return s / cnt - tok


def _seq_fwd(z, w_dw4, b_dw, ln_g, ln_b, w_pool_b, s_pool, comm=()):
    T, CI = z.shape
    CC = ln_g.shape[1]
    n_grp, G = w_pool_b.shape[0], w_pool_b.shape[-1]
    KW = w_dw4.shape[1]
    D = CC + n_grp * G
    tt = _tile(T, 256, HALO)
    per = tt // HALO

    def body(zc_ref, zp_ref, wdw_ref, bdw_ref, lng_ref, lnb_ref, wp_ref, sp_ref, y_ref, v_ref, u_scr, p_scr):
        i = pl.program_id(0)
        first = i == 0
        u_prev = zp_ref[:, 0:CC] * _sigmoid(zp_ref[:, CC : 2 * CC])
        u_scr[0:HALO, :] = jnp.where(first, 0.0, u_prev)
        p_scr[0:HALO, :] = jnp.where(first, 0.0, zp_ref[:, 2 * CC :])
        u_scr[HALO:, :] = zc_ref[:, 0:CC] * _sigmoid(zc_ref[:, CC : 2 * CC])
        p_scr[HALO:, :] = zc_ref[:, 2 * CC :]

        for j in range(CC // LANES):
            cs = slice(LANES * j, LANES * (j + 1))
            for rb in range(tt // CONV_ROWS):
                acc = jnp.zeros((CONV_ROWS, LANES), F32)
                for k in range(KW):
                    off = HALO - (KW - 1) + k + rb * CONV_ROWS
                    acc = acc + u_scr[off : off + CONV_ROWS, cs] * wdw_ref[j, k : k + 1, :]
                v_ref[rb * CONV_ROWS : (rb + 1) * CONV_ROWS, cs] = acc + bdw_ref[:, cs]

        v = v_ref[...]
        mu = jnp.mean(v, axis=-1, keepdims=True)
        d = v - mu
        var = jnp.mean(d * d, axis=-1, keepdims=True)
        ln = d * lax.rsqrt(var + LN_EPS) * lng_ref[...] + lnb_ref[...]
        y_ref[:, 0:CC] = (ln * _sigmoid(ln)).astype(BF16)

        tpos = i * tt + lax.broadcasted_iota(jnp.int32, (tt, 1), 0)
        for gi, w in enumerate(POOL_WINDOWS):
            cs = slice(G * gi, G * (gi + 1))
            cnt = jnp.minimum(tpos + 1, w).astype(F32)
            yi = _pool_mean_minus_token(p_scr, cs, w, cnt, tt)
            q = _dot(yi.astype(BF16), wp_ref[gi], NN)
            y_ref[:, CC + G * gi : CC + G * (gi + 1)] = (q * sp_ref[:, cs]).astype(BF16)

    const2 = lambda i: (0, 0)
    return _call(
        "seq_fwd",
        body,
        (T // tt,),
        [
            pl.BlockSpec((tt, CI), lambda i: (i, 0)),
            pl.BlockSpec((HALO, CI), lambda i: (jnp.maximum(i * per - 1, 0), 0)),
            pl.BlockSpec(w_dw4.shape, lambda i: (0, 0, 0)),
            pl.BlockSpec((1, CC), const2),
            pl.BlockSpec((1, CC), const2),
            pl.BlockSpec((1, CC), const2),
            pl.BlockSpec(w_pool_b.shape, lambda i: (0, 0, 0)),
            pl.BlockSpec((1, n_grp * G), const2),
        ],
        [pl.BlockSpec((tt, D), lambda i: (i, 0)), pl.BlockSpec((tt, CC), lambda i: (i, 0))],
        [jax.ShapeDtypeStruct((T, D), BF16), jax.ShapeDtypeStruct((T, CC), F32)],
        (z, z, w_dw4, b_dw, ln_g, ln_b, w_pool_b, s_pool),
        scratch=[pltpu.VMEM((HALO + tt, CC), F32), pltpu.VMEM((HALO + tt, n_grp * G), F32)],
        comm=comm,
    )


def _out_proj(y_b, x, w_out_b, g_ffn, comm=()):
    T, D = x.shape
    tm = _tile(T, 512)

    def body(y_ref, x_ref, w_ref, g_ref, h1_ref, hn_ref):
        h1 = x_ref[...] + _dot(y_ref[...], w_ref[...], NN)
        h1_ref[...] = h1
        r = lax.rsqrt(jnp.mean(h1 * h1, axis=-1, keepdims=True) + RMS_EPS)
        hn_ref[...] = (h1 * r * g_ref[...]).astype(BF16)

    row = lambda i: (i, 0)
    return _call(
        "out_proj",
        body,
        (T // tm,),
        [
            pl.BlockSpec((tm, y_b.shape[1]), row),
            pl.BlockSpec((tm, D), row),
            pl.BlockSpec(w_out_b.shape, lambda i: (0, 0)),
            pl.BlockSpec((1, D), lambda i: (0, 0)),
        ],
        [pl.BlockSpec((tm, D), row), pl.BlockSpec((tm, D), row)],
        [jax.ShapeDtypeStruct((T, D), F32), jax.ShapeDtypeStruct((T, D), BF16)],
        (y_b, x, w_out_b, g_ffn),
        comm=comm,
    )


def _hidden_tile(F):
    return _tile(F, 1408, LANES)


def _gate_up(hn_b, wgT_b, wuT_b, comm=()):
    T, D = hn_b.shape
    F = wgT_b.shape[0]
    tm, tf = _tile(T, 512), _hidden_tile(F)

    def body(hn_ref, wg_ref, wu_ref, g_ref, u_ref, a_ref):
        hn = hn_ref[...]
        gv = _dot(hn, wg_ref[...], NT)
        uv = _dot(hn, wu_ref[...], NT)
        g_ref[...] = gv.astype(BF16)
        u_ref[...] = uv.astype(BF16)
        a_ref[...] = (gv * _sigmoid(gv) * uv).astype(BF16)

    wspec = pl.BlockSpec((tf, D), lambda j, i: (j, 0))
    ospec = pl.BlockSpec((tm, tf), lambda j, i: (i, j))
    return _call(
        "gate_up",
        body,
        (F // tf, T // tm),
        [pl.BlockSpec((tm, D), lambda j, i: (i, 0)), wspec, wspec],
        [ospec, ospec, ospec],
        [jax.ShapeDtypeStruct((T, F), BF16)] * 3,
        (hn_b, wgT_b, wuT_b),
        comm=comm,
    )


def _down_loss(a_b, wd_b, h1, target, g_final, comm=()):
    T, D = h1.shape
    F = a_b.shape[1]
    tm = _tile(T, 256)
    nt = T // tm

    def body(a_ref, w_ref, h1_ref, t_ref, g_ref, dh2_ref, dh2b_ref, loss_ref, dg_ref):
        i = pl.program_id(0)
        h2 = h1_ref[...] + _dot(a_ref[...], w_ref[...], NN)
        r = lax.rsqrt(jnp.mean(h2 * h2, axis=-1, keepdims=True) + RMS_EPS)
        g = g_ref[...]
        diff = h2 * r * g - t_ref[...]
        loss_ref[...] = jnp.full(loss_ref.shape, jnp.sum(diff * diff), F32)
        dh2, dg_rows = _rms_bwd(h2, g, diff * (1.0 / D))
        dh2_ref[...] = dh2
        dh2b_ref[...] = dh2.astype(BF16)
        _accumulate(dg_ref, i == 0, jnp.sum(dg_rows, axis=0, keepdims=True))

    row = lambda i: (i, 0)
    return _call(
        "down_loss",
        body,
        (nt,),
        [
            pl.BlockSpec((tm, F), row),
            pl.BlockSpec((F, D), lambda i: (0, 0)),
            pl.BlockSpec((tm, D), row),
            pl.BlockSpec((tm, D), row),
            pl.BlockSpec((1, D), lambda i: (0, 0)),
        ],
        [
            pl.BlockSpec((tm, D), row),
            pl.BlockSpec((tm, D), row),
            pl.BlockSpec((1, 1, LANES), lambda i: (i, 0, 0)),
            pl.BlockSpec((1, D), lambda i: (0, 0)),
        ],
        [
            jax.ShapeDtypeStruct((T, D), F32),
            jax.ShapeDtypeStruct((T, D), BF16),
            jax.ShapeDtypeStruct((nt, 1, LANES), F32),
            jax.ShapeDtypeStruct((1, D), F32),
        ],
        (a_b, wd_b, h1, target, g_final),
        comm=comm,
    )


def _ffn_bwd_act(dh2_b, wd_b, g_b, u_b, comm=()):
    T, D = dh2_b.shape
    F = wd_b.shape[0]
    tm, tf = _tile(T, 512), _hidden_tile(F)

    def body(d_ref, w_ref, g_ref, u_ref, dg_ref, du_ref):
        da = _dot(d_ref[...], w_ref[...], NT)
        gv = g_ref[...].astype(F32)
        uv = u_ref[...].astype(F32)
        sg = _sigmoid(gv)
        silu = gv * sg
        dg_ref[...] = (da * uv * (sg * (1.0 + gv * (1.0 - sg)))).astype(BF16)
        du_ref[...] = (da * silu).astype(BF16)

    aspec = pl.BlockSpec((tm, tf), lambda j, i: (i, j))
    return _call(
        "ffn_bwd_act",
        body,
        (F // tf, T // tm),
        [pl.BlockSpec((tm, D), lambda j, i: (i, 0)), pl.BlockSpec((tf, D), lambda j, i: (j, 0)), aspec, aspec],
        [aspec, aspec],
        [jax.ShapeDtypeStruct((T, F), BF16)] * 2,
        (dh2_b, wd_b, g_b, u_b),
        comm=comm,
    )


def _ffn_bwd_in(dg_b, du_b, wgT_b, wuT_b, h1, dh2, g_ffn, w_out_b, comm=()):
    T, D = h1.shape
    F = wgT_b.shape[0]
    DM = w_out_b.shape[0]
    tm = _tile(T, 256)

    def body(dg_ref, du_ref, wg_ref, wu_ref, h1_ref, dh2_ref, g_ref, wo_ref, dh1_ref, dh1b_ref, dy_ref, dgf_ref):
        i = pl.program_id(0)
        dhn = _dot(dg_ref[...], wg_ref[...], NN) + _dot(du_ref[...], wu_ref[...], NN)
        dx, dg_rows = _rms_bwd(h1_ref[...], g_ref[...], dhn)
        dh1 = dh2_ref[...] + dx
        dh1b = dh1.astype(BF16)
        dh1_ref[...] = dh1
        dh1b_ref[...] = dh1b
        dy_ref[...] = _dot(dh1b, wo_ref[...], NT)
        _accumulate(dgf_ref, i == 0, jnp.sum(dg_rows, axis=0, keepdims=True))

    row = lambda i: (i, 0)
    const = lambda i: (0, 0)
    return _call(
        "ffn_bwd_in",
        body,
        (T // tm,),
        [
            pl.BlockSpec((tm, F), row),
            pl.BlockSpec((tm, F), row),
            pl.BlockSpec((F, D), const),
            pl.BlockSpec((F, D), const),
            pl.BlockSpec((tm, D), row),
            pl.BlockSpec((tm, D), row),
            pl.BlockSpec((1, D), const),
            pl.BlockSpec((DM, D), const),
        ],
        [pl.BlockSpec((tm, D), row), pl.BlockSpec((tm, D), row), pl.BlockSpec((tm, DM), row), pl.BlockSpec((1, D), const)],
        [
            jax.ShapeDtypeStruct((T, D), F32),
            jax.ShapeDtypeStruct((T, D), BF16),
            jax.ShapeDtypeStruct((T, DM), F32),
            jax.ShapeDtypeStruct((1, D), F32),
        ],
        (dg_b, du_b, wgT_b, wuT_b, h1, dh2, g_ffn, w_out_b),
        comm=comm,
    )


def _seq_bwd(z, dy, v, w_dw4, ln_g, ln_b, w_pool_b, s_pool, comm=()):
    T, CI = z.shape
    CC = ln_g.shape[1]
    n_grp, G = w_pool_b.shape[0], w_pool_b.shape[-1]
    CP = n_grp * G
    KW = w_dw4.shape[1]
    n_cc = CC // LANES
    D = CC + CP
    tt = _tile(T, 256, HALO)
    per = tt // HALO
    n_tiles = T // tt
    last_halo = T // HALO - 1

    def body(zc_ref, zp_ref, dyc_ref, dyn_ref, vc_ref, vn_ref, wdw_ref, lng_ref, lnb_ref, wp_ref, sp_ref,
             dz_ref, dwdw_ref, dbdw_ref, dlng_ref, dlnb_ref, dwp_ref, dsp_ref, dbin_ref,
             dv_scr, u_scr, p_scr, g_scr, dw_scr):
        i = pl.program_id(0)
        first = i == 0
        last = i == n_tiles - 1
        lng, lnb = lng_ref[...], lnb_ref[...]

        def conv_pre(vv, dyc):
            mu = jnp.mean(vv, axis=-1, keepdims=True)
            d = vv - mu
            rs = lax.rsqrt(jnp.mean(d * d, axis=-1, keepdims=True) + LN_EPS)
            xh = d * rs
            ln = xh * lng + lnb
            sg = _sigmoid(ln)
            dln = dyc * (sg * (1.0 + ln * (1.0 - sg)))
            dxh = dln * lng
            dv = rs * (dxh - jnp.mean(dxh, axis=-1, keepdims=True) - xh * jnp.mean(dxh * xh, axis=-1, keepdims=True))
            return dv, dln, xh

        dv_c, dln_c, xh_c = conv_pre(vc_ref[...], dyc_ref[:, 0:CC])
        dv_scr[0:tt, :] = dv_c
        dv_n, _, _ = conv_pre(vn_ref[...], dyn_ref[:, 0:CC])
        dv_scr[tt:, :] = jnp.where(last, 0.0, dv_n)
        _accumulate(dlng_ref, first, jnp.sum(dln_c * xh_c, axis=0, keepdims=True))
        _accumulate(dlnb_ref, first, jnp.sum(dln_c, axis=0, keepdims=True))
        _accumulate(dbdw_ref, first, jnp.sum(dv_c, axis=0, keepdims=True))

        u_scr[...] = zc_ref[:, 0:CC] * _sigmoid(zc_ref[:, CC : 2 * CC])

        @pl.when(first)
        def _():
            dw_scr[...] = jnp.zeros_like(dw_scr)

        for j in range(n_cc):
            cs = slice(LANES * j, LANES * (j + 1))
            gs = slice(CC + LANES * j, CC + LANES * (j + 1))
            dbin_a = jnp.zeros((1, LANES), F32)
            dbin_g = jnp.zeros((1, LANES), F32)
            for rb in range(tt // CONV_ROWS):
                rows = slice(rb * CONV_ROWS, (rb + 1) * CONV_ROWS)
                u_blk = u_scr[rows, cs]
                du = jnp.zeros((CONV_ROWS, LANES), F32)
                for k in range(KW):
                    off = rb * CONV_ROWS + (KW - 1) - k
                    d = dv_scr[off : off + CONV_ROWS, cs]
                    du = du + d * wdw_ref[j, k : k + 1, :]
                    dw_scr[j * HALO + k] += jnp.sum((u_blk * d).reshape(CONV_ROWS // 8, 8, LANES), axis=0)
                a = zc_ref[rows, cs]
                sg = _sigmoid(zc_ref[rows, gs])
                da = du * sg
                dgate = du * a * sg * (1.0 - sg)
                dz_ref[rows, cs] = da.astype(BF16)
                dz_ref[rows, gs] = dgate.astype(BF16)
                dbin_a = dbin_a + jnp.sum(da, axis=0, keepdims=True)
                dbin_g = dbin_g + jnp.sum(dgate, axis=0, keepdims=True)
            _accumulate(dbin_ref.at[:, cs], first, dbin_a)
            _accumulate(dbin_ref.at[:, gs], first, dbin_g)

        @pl.when(last)
        def _():
            dwdw_ref[...] = jnp.sum(dw_scr[...], axis=1).reshape(dwdw_ref.shape)

        p_scr[0:HALO, :] = jnp.where(first, 0.0, zp_ref[:, 2 * CC :])
        p_scr[HALO:, :] = zc_ref[:, 2 * CC :]
        tpos = i * tt + lax.broadcasted_iota(jnp.int32, (tt, 1), 0)
        for gi, w in enumerate(POOL_WINDOWS):
            cs = slice(G * gi, G * (gi + 1))
            ys = slice(CC + G * gi, CC + G * (gi + 1))
            ps = slice(2 * CC + G * gi, 2 * CC + G * (gi + 1))
            cnt = jnp.minimum(tpos + 1, w).astype(F32)
            yib = _pool_mean_minus_token(p_scr, cs, w, cnt, tt).astype(BF16)
            wp = wp_ref[gi]
            sp = sp_ref[:, cs]
            dyp = dyc_ref[:, ys]
            q = _dot(yib, wp, NN)
            _accumulate(dsp_ref.at[:, cs], first, jnp.sum(dyp * q, axis=0, keepdims=True))
            dq_c = (dyp * sp).astype(BF16)
            dq_n = (jnp.where(last, 0.0, dyn_ref[:, ys]) * sp).astype(BF16)
            _accumulate(dwp_ref.at[gi], first, _dot(yib, dq_c, TN))
            dyi_c = _dot(dq_c, wp, NT)
            g_scr[0:tt, cs] = dyi_c / cnt
            g_scr[tt:, cs] = _dot(dq_n, wp, NT) * (1.0 / w)
            dp = -dyi_c
            for d in range(w):
                dp = dp + g_scr[d : d + tt, cs]
            dz_ref[:, ps] = dp.astype(BF16)
            _accumulate(dbin_ref.at[:, ps], first, jnp.sum(dp, axis=0, keepdims=True))

    cur = lambda i: (i, 0)
    prev = lambda i: (jnp.maximum(i * per - 1, 0), 0)
    nxt = lambda i: (jnp.minimum((i + 1) * per, last_halo), 0)
    c2 = lambda i: (0, 0)
    c3 = lambda i: (0, 0, 0)
    return _call(
        "seq_bwd",
        body,
        (n_tiles,),
        [
            pl.BlockSpec((tt, CI), cur),
            pl.BlockSpec((HALO, CI), prev),
            pl.BlockSpec((tt, D), cur),
            pl.BlockSpec((HALO, D), nxt),
            pl.BlockSpec((tt, CC), cur),
            pl.BlockSpec((HALO, CC), nxt),
            pl.BlockSpec(w_dw4.shape, c3),
            pl.BlockSpec((1, CC), c2),
            pl.BlockSpec((1, CC), c2),
            pl.BlockSpec(w_pool_b.shape, c3),
            pl.BlockSpec((1, CP), c2),
        ],
        [
            pl.BlockSpec((tt, CI), cur),
            pl.BlockSpec((n_cc, HALO, LANES), c3),
            pl.BlockSpec((1, CC), c2),
            pl.BlockSpec((1, CC), c2),
            pl.BlockSpec((1, CC), c2),
            pl.BlockSpec((n_grp, G, G), c3),
            pl.BlockSpec((1, CP), c2),
            pl.BlockSpec((1, CI), c2),
        ],
        [
            jax.ShapeDtypeStruct((T, CI), BF16),
            jax.ShapeDtypeStruct((n_cc, HALO, LANES), F32),
            jax.ShapeDtypeStruct((1, CC), F32),
            jax.ShapeDtypeStruct((1, CC), F32),
            jax.ShapeDtypeStruct((1, CC), F32),
            jax.ShapeDtypeStruct((n_grp, G, G), F32),
            jax.ShapeDtypeStruct((1, CP), F32),
            jax.ShapeDtypeStruct((1, CI), F32),
        ],
        (z, z, dy, dy, v, v, w_dw4, ln_g, ln_b, w_pool_b, s_pool),
        scratch=[
            pltpu.VMEM((tt + HALO, CC), F32),
            pltpu.VMEM((tt, CC), F32),
            pltpu.VMEM((HALO + tt, CP), F32),
            pltpu.VMEM((tt + HALO, CP), F32),
            pltpu.VMEM((n_cc * HALO, 8, LANES), F32),
        ],
        comm=comm,
    )


def _in_proj_bwd(dz_b, w_in_b, x, dh1, g_mix, comm=()):
    T, D = x.shape
    CI = w_in_b.shape[1]
    tm = _tile(T, 512)

    def body(dz_ref, w_ref, x_ref, dh1_ref, g_ref, dx_ref, dg_ref):
        i = pl.program_id(0)
        dxn = _dot(dz_ref[...], w_ref[...], NT)
        dx, dg_rows = _rms_bwd(x_ref[...], g_ref[...], dxn)
        dx_ref[...] = dh1_ref[...] + dx
        _accumulate(dg_ref, i == 0, jnp.sum(dg_rows, axis=0, keepdims=True))

    row = lambda i: (i, 0)
    const = lambda i: (0, 0)
    return _call(
        "in_proj_bwd",
        body,
        (T // tm,),
        [
            pl.BlockSpec((tm, CI), row),
            pl.BlockSpec((D, CI), const),
            pl.BlockSpec((tm, D), row),
            pl.BlockSpec((tm, D), row),
            pl.BlockSpec((1, D), const),
        ],
        [pl.BlockSpec((tm, D), row), pl.BlockSpec((1, D), const)],
        [jax.ShapeDtypeStruct((T, D), F32), jax.ShapeDtypeStruct((1, D), F32)],
        (dz_b, w_in_b, x, dh1, g_mix),
        comm=comm,
    )


def _weight_grad(name, a_b, b_b, comm=()):
    T, N1 = a_b.shape
    N2 = b_b.shape[1]
    t1 = _tile(N1, 1408, LANES)
    tk = _tile(T, 1024)
    nk = T // tk

    def body(a_ref, b_ref, o_ref, acc):
        k = pl.program_id(1)
        _accumulate(acc, k == 0, _dot(a_ref[...], b_ref[...], TN))

        @pl.when(k == nk - 1)
        def _():
            o_ref[...] = acc[...].astype(BF16)

    (out,), rest = _call(
        name,
        body,
        (N1 // t1, nk),
        [pl.BlockSpec((tk, t1), lambda n, k: (k, n)), pl.BlockSpec((tk, N2), lambda n, k: (k, 0))],
        [pl.BlockSpec((t1, N2), lambda n, k: (n, 0))],
        [jax.ShapeDtypeStruct((N1, N2), BF16)],
        (a_b, b_b),
        scratch=[pltpu.VMEM((t1, N2), F32)],
        comm=comm,
    )
    return out, rest


def _sum_parts(name, parts, comm=()):
    _, R, C = parts.shape
    tr = _tile(R, 512)

    def body(p_ref, o_ref):
        f = lambda q: p_ref[q].astype(F32)
        o_ref[...] = (f(3) + f(0)) + (f(1) + f(2))

    (out,), rest = _call(
        name,
        body,
        (R // tr,),
        [pl.BlockSpec((N_CHIPS, tr, C), lambda i: (0, i, 0))],
        [pl.BlockSpec((tr, C), lambda i: (i, 0))],
        [jax.ShapeDtypeStruct((R, C), F32)],
        (parts,),
        comm=comm,
    )
    return out, rest


_M_CORR = 1.0 - ADAM_B1**ADAM_STEP
_V_CORR = 1.0 - ADAM_B2**ADAM_STEP


def _adamw_math(w, g, m, v):
    m = ADAM_B1 * m + (1.0 - ADAM_B1) * g
    v = ADAM_B2 * v + (1.0 - ADAM_B2) * (g * g)
    delta = -ADAM_LR * ((m / _M_CORR) / (jnp.sqrt(v / _V_CORR) + ADAM_EPS) + ADAM_WD * w)
    return delta, m, v


def _adamw(name, w, m, v, g_here, g_there, comm=()):
    R, C = w.shape
    tr = _tile(R, 256)

    def body(w_ref, m_ref, v_ref, ga_ref, gb_ref, g_ref, d_ref, nm_ref, nv_ref):
        g = ga_ref[...] + gb_ref[...]
        g_ref[...] = g
        d_ref[...], nm_ref[...], nv_ref[...] = _adamw_math(w_ref[...], g, m_ref[...], v_ref[...])

    spec = pl.BlockSpec((tr, C), lambda i: (i, 0))
    return _call(name, body, (R // tr,), [spec] * 5, [spec] * 4, [jax.ShapeDtypeStruct((R, C), F32)] * 4,
                 (w, m, v, g_here, g_there), comm=comm)


class _PackLayout:
    def __init__(self, n_cc, n_grp, G, widths):
        self.dw_rows = (0, HALO)
        self.wp_rows = (HALO, HALO + G)
        self.n_cc, self.n_grp, self.G = n_cc, n_grp, G
        self.vec = {}
        r = HALO + G
        for name, width in widths:
            self.vec[name] = (r, width)
            r += width // PACK_W
        self.rows = -(-r // 8) * 8


def _pack_small(layout, dwdw, dwp, vecs):
    names = list(vecs)

    def body(*refs):
        dw_ref, wp_ref = refs[0], refs[1]
        vec_refs = refs[2 : 2 + len(names)]
        o_ref = refs[-1]
        o_ref[...] = jnp.zeros_like(o_ref)
        for j in range(layout.n_cc):
            o_ref[layout.dw_rows[0] : layout.dw_rows[1], j * LANES : (j + 1) * LANES] = dw_ref[j]
        for i in range(layout.n_grp):
            o_ref[layout.wp_rows[0] : layout.wp_rows[1], i * layout.G : (i + 1) * layout.G] = wp_ref[i]
        for name, ref in zip(names, vec_refs):
            r, width = layout.vec[name]
            for h in range(width // PACK_W):
                o_ref[r + h : r + h + 1, :] = ref[:, h * PACK_W : (h + 1) * PACK_W]

    return pl.pallas_call(
        body,
        name="pack_small",
        out_shape=jax.ShapeDtypeStruct((layout.rows, PACK_W), F32),
    )(dwdw, dwp, *[vecs[k] for k in names])


def _adamw_small(layout, g_here, g_there, w_dw, m_dw, v_dw, w_pool, m_pool, v_pool, vec_w, vec_m, vec_v):
    names = list(vec_w)
    nv = len(names)

    def body(*refs):
        ga_ref, gb_ref = refs[0], refs[1]
        wdw, mdw, vdw, wp, mp, vp = refs[2:8]
        vw, vm, vv = refs[8 : 8 + nv], refs[8 + nv : 8 + 2 * nv], refs[8 + 2 * nv : 8 + 3 * nv]
        outs = refs[8 + 3 * nv :]
        acc = outs[-1]
        acc[...] = ga_ref[...] + gb_ref[...]

        def emit(o, g, w, m, v, idx=()):
            res = (g,) + _adamw_math(w, g, m, v)
            for ref, val in zip(o, res):
                ref[idx] = val

        me = 2 * lax.axis_index("x") + lax.axis_index("y")
        for j in range(layout.n_cc):

            @pl.when(me == j)
            def _(j=j):
                g = acc[layout.dw_rows[0] : layout.dw_rows[1], j * LANES : (j + 1) * LANES]
                emit(outs[0:4], g, wdw[...], mdw[...], vdw[...], idx=...)

        for i in range(layout.n_grp):
            g = acc[layout.wp_rows[0] : layout.wp_rows[1], i * layout.G : (i + 1) * layout.G]
            emit(outs[4:8], g, wp[i], mp[i], vp[i], idx=i)
        for q, name in enumerate(names):
            r, width = layout.vec[name]
            for h in range(width // PACK_W):
                ls = slice(h * PACK_W, (h + 1) * PACK_W)
                g = acc[r + h : r + h + 1, :]
                emit(outs[8 + 4 * q : 12 + 4 * q], g, vw[q][:, ls], vm[q][:, ls], vv[q][:, ls], idx=(slice(None), ls))

    shapes = [w_dw.shape] * 4 + [w_pool.shape] * 4
    for name in names:
        shapes += [vec_w[name].shape] * 4
    return pl.pallas_call(
        body,
        name="adamw_small",
        out_shape=[jax.ShapeDtypeStruct(s, F32) for s in shapes],
        scratch_shapes=[pltpu.VMEM(g_here.shape, F32)],
    )(g_here, g_there, w_dw, m_dw, v_dw, w_pool, m_pool, v_pool,
      *[vec_w[k] for k in names], *[vec_m[k] for k in names], *[vec_v[k] for k in names])


def _allreduce_adamw_row(g_part, w, m, v, comm=()):
    D = w.shape[1]

    def body(g_ref, w_ref, m_ref, v_ref, go_ref, d_ref, nm_ref, nv_ref, land, sems):
        x, y, c = _place()
        copies = []
        for r in range(1, N_DEV):
            fx, fy, fc = (r >> 2) & 1, (r >> 1) & 1, r & 1
            peer = (1 - x if fx else x, 1 - y if fy else y, 1 - c if fc else c)
            cp = _remote(g_ref, land.at[r], sems, 2 * (r - 1), peer)
            cp.start()
            copies.append(cp)
        for cp in copies:
            cp.wait()
        row = lambda r: g_ref[...] if r == 0 else land[r]
        g = ((row(0) + row(4)) + (row(2) + row(6))) + ((row(1) + row(5)) + (row(3) + row(7)))
        go_ref[...] = g
        d_ref[...], nm_ref[...], nv_ref[...] = _adamw_math(w_ref[...], g, m_ref[...], v_ref[...])

    vm = pl.BlockSpec(memory_space=pltpu.VMEM)
    return _call(
        "allreduce_adamw_g_mix",
        body,
        (),
        [vm] * 4,
        [vm] * 4,
        [jax.ShapeDtypeStruct((1, D), F32)] * 4,
        (g_part, w, m, v),
        scratch=[pltpu.VMEM((N_DEV, 1, D), F32), pltpu.SemaphoreType.DMA((2 * (N_DEV - 1),))],
        comm=comm,
    )


def kernel(x, g_mix, w_in, b_in, w_dw, b_dw, ln_g, ln_b, w_pool, s_pool, w_out, g_ffn, w_gate, w_up, w_down, g_final, loss_target, m_g_mix, m_w_in, m_b_in, m_w_dw, m_b_dw, m_ln_g, m_ln_b, m_w_pool, m_s_pool, m_w_out, m_g_ffn, m_w_gate, m_w_up, m_w_down, m_g_final, v_g_mix, v_w_in, v_b_in, v_w_dw, v_b_dw, v_ln_g, v_ln_b, v_w_pool, v_s_pool, v_w_out, v_g_ffn, v_w_gate, v_w_up, v_w_down, v_g_final):
    x2 = x[0]
    target = loss_target[0]
    T, D = x2.shape
    w_in2, w_out2, w_down2, w_dw2 = w_in[0], w_out[0], w_down[0], w_dw[0]
    w_gateT, w_upT = w_gate[0].T, w_up[0].T
    CI = w_in2.shape[1] * N_CHIPS
    DM = w_out2.shape[0] * N_CHIPS
    F = w_down2.shape[0] * N_CHIPS
    KW, dw_cols = w_dw2.shape
    assert dw_cols == LANES
    n_grp, G = w_pool.shape[1], w_pool.shape[-1]
    w_pool3 = w_pool[0]
    g_final2 = g_final.reshape(1, D)

    s_in, s_out, s_gate, s_up, s_down = (a.astype(BF16) for a in (w_in2, w_out2, w_gateT, w_upT, w_down2))
    w_in_b, w_dw4 = _gather_now([s_in, w_dw2], ["cols", "lead"], [True, False], [(D, CI), (N_CHIPS, KW, dw_cols)])
    w_pool_b = w_pool3.astype(BF16)
    (z, xn_b), (st_out, st_gate) = _in_proj(x2, g_mix, w_in_b, b_in, comm=[_GatherIci([s_out, s_gate], [True, True])])
    (y_b, v), (w_out_b, st_up, st_down) = _seq_fwd(
        z, w_dw4, b_dw, ln_g, ln_b, w_pool_b, s_pool,
        comm=[_GatherD2d([st_out], [s_out], ["rows"], [True], [(DM, D)]), _GatherIci([s_up, s_down], [True, True])])
    (h1, hn_b), (wgT_b, wuT_b, wd_b) = _out_proj(
        y_b, x2, w_out_b, g_ffn,
        comm=[_GatherD2d([st_gate, st_up, st_down], [s_gate, s_up, s_down], ["rows"] * 3, [True] * 3, [(F, D)] * 3)])
    (g_b, u_b, a_b), _ = _gate_up(hn_b, wgT_b, wuT_b)
    (dh2, dh2_b, loss_parts, d_g_final), _ = _down_loss(a_b, wd_b, h1, target, g_final2)
    loss = lax.psum(jnp.sum(loss_parts[:, 0, 0]) * (0.5 / D), ("x", "y", "c"))

    gw_down, _ = _weight_grad("grad_w_down", a_b, dh2_b)
    (dg_b, du_b), (p_down,) = _ffn_bwd_act(dh2_b, wd_b, g_b, u_b, comm=[_Scatter([gw_down], ["rows"])])
    gw_gateT, _ = _weight_grad("grad_w_gate", dg_b, hn_b)
    gw_upT, _ = _weight_grad("grad_w_up", du_b, hn_b)
    sum_down, _ = _sum_parts("sum_w_down", p_down)
    (dh1, dh1_b, dy, d_g_ffn), (p_gate, oth_down) = _ffn_bwd_in(
        dg_b, du_b, wgT_b, wuT_b, h1, dh2, g_ffn, w_out_b, comm=[_Scatter([gw_gateT], ["rows"]), _Swap([sum_down])])
    gw_out, _ = _weight_grad("grad_w_out", y_b, dh1_b)
    sum_gate, _ = _sum_parts("sum_w_gate", p_gate)
    res = {}
    res["w_down"], _ = _adamw("adamw_w_down", w_down2, m_w_down[0], v_w_down[0], sum_down, oth_down)
    (dz_b, d_wdw, d_bdw, d_lng, d_lnb, d_wp, d_sp, d_bin), (p_up, p_out, oth_gate) = _seq_bwd(
        z, dy, v, w_dw4, ln_g, ln_b, w_pool_b, s_pool,
        comm=[_Scatter([gw_upT, gw_out], ["rows", "rows"]), _Swap([sum_gate])])
    gw_in, _ = _weight_grad("grad_w_in", xn_b, dz_b)
    vec_grads = {"b_dw": d_bdw, "ln_g": d_lng, "ln_b": d_lnb, "s_pool": d_sp, "g_ffn": d_g_ffn, "g_final": d_g_final, "b_in": d_bin}
    layout = _PackLayout(dw_cols * N_CHIPS // LANES, n_grp, G, [(k, a.shape[1]) for k, a in vec_grads.items()])
    pack = _pack_small(layout, d_wdw, d_wp, vec_grads)
    sum_up, _ = _sum_parts("sum_w_up", p_up)
    sum_out, _ = _sum_parts("sum_w_out", p_out)
    res["w_gate"], _ = _adamw("adamw_w_gate", w_gateT, m_w_gate[0].T, v_w_gate[0].T, sum_gate, oth_gate)
    (grad_x, d_g_mix), (p_in, p_small, oth_up, oth_out) = _in_proj_bwd(
        dz_b, w_in_b, x2, dh1, g_mix, comm=[_Scatter([gw_in, pack], ["cols", "all"]), _Swap([sum_up, sum_out])])
    sum_in, _ = _sum_parts("sum_w_in", p_in)
    sum_small, _ = _sum_parts("sum_small", p_small)
    res["g_mix"], (oth_in, oth_small) = _allreduce_adamw_row(d_g_mix, g_mix, m_g_mix, v_g_mix, comm=[_Swap([sum_in, sum_small])])
    res["w_up"], _ = _adamw("adamw_w_up", w_upT, m_w_up[0].T, v_w_up[0].T, sum_up, oth_up)
    res["w_out"], _ = _adamw("adamw_w_out", w_out2, m_w_out[0], v_w_out[0], sum_out, oth_out)
    res["w_in"], _ = _adamw("adamw_w_in", w_in2, m_w_in[0], v_w_in[0], sum_in, oth_in)

    pad_dw = lambda a: jnp.pad(a[0], ((0, HALO - KW), (0, 0)))
    vec_w = {"b_dw": b_dw, "ln_g": ln_g, "ln_b": ln_b, "s_pool": s_pool, "g_ffn": g_ffn, "g_final": g_final2, "b_in": b_in}
    vec_m = {"b_dw": m_b_dw, "ln_g": m_ln_g, "ln_b": m_ln_b, "s_pool": m_s_pool, "g_ffn": m_g_ffn,
             "g_final": m_g_final.reshape(1, D), "b_in": m_b_in}
    vec_v = {"b_dw": v_b_dw, "ln_g": v_ln_g, "ln_b": v_ln_b, "s_pool": v_s_pool, "g_ffn": v_g_ffn,
             "g_final": v_g_final.reshape(1, D), "b_in": v_b_in}
    small = _adamw_small(layout, sum_small, oth_small, pad_dw(w_dw), pad_dw(m_w_dw), pad_dw(v_w_dw),
                         w_pool3, m_w_pool[0], v_w_pool[0], vec_w, vec_m, vec_v)
    res["w_dw"] = [a[:KW][None] for a in small[0:4]]
    res["w_pool"] = [a[None] for a in small[4:8]]
    for q, k in enumerate(vec_w):
        res[k] = list(small[8 + 4 * q : 12 + 4 * q])
    res["g_final"] = [a.reshape(D) for a in res["g_final"]]
    for k in ("w_in", "w_out", "w_down"):
        res[k] = [a[None] for a in res[k]]
    for k in ("w_gate", "w_up"):
        res[k] = [a.T[None] for a in res[k]]

    order = ["g_mix", "w_in", "b_in", "w_dw", "b_dw", "ln_g", "ln_b", "w_pool", "s_pool", "w_out", "g_ffn", "w_gate", "w_up", "w_down", "g_final"]
    outs = [loss, grad_x[None]]
    for q in range(4):
        outs += [res[k][q] for k in order]
    return tuple(outs)
```

```python
import jax
import jax.numpy as jnp
from jax import lax
from jax.experimental import pallas as pl
from jax.experimental.pallas import tpu as pltpu

F32 = jnp.float32
BF16 = jnp.bfloat16
MESH = pl.DeviceIdType.MESH
ANY = pl.BlockSpec(memory_space=pl.ANY)

RMS_EPS = 1e-6
LN_EPS = 1e-5
POOL_WINDOWS = (2, 4, 8, 16)
ADAM_LR = 0.001
ADAM_B1 = 0.9
ADAM_B2 = 0.999
ADAM_EPS = 1e-08
ADAM_WD = 0.01
ADAM_STEP = 10

LANES = 128
HALO = 32
CONV_ROWS = 64
VMEM_LIMIT = 56 * 1024 * 1024
PACK_W = 512
N_CHIPS = 4
N_DEV = 8


def _tile(n, want, mult=8):
    t = min(n, want)
    while n % t or t % mult:
        t -= 1
    return t


def _sigmoid(x):
    return 1.0 / (1.0 + jnp.exp(-x))


def _dot(a, b, dims):
    return lax.dot_general(a, b, (dims, ((), ())), preferred_element_type=F32)


NN = ((1,), (0,))
NT = ((1,), (1,))
TN = ((0,), (0,))


def _rms_bwd(x, g, dy):
    r = lax.rsqrt(jnp.mean(x * x, axis=-1, keepdims=True) + RMS_EPS)
    xh = x * r
    gy = dy * g
    dx = r * (gy - xh * jnp.mean(gy * xh, axis=-1, keepdims=True))
    return dx, dy * xh


def _accumulate(ref, first, val):
    @pl.when(first)
    def _():
        ref[...] = val

    @pl.when(jnp.logical_not(first))
    def _():
        ref[...] += val


def _place():
    return lax.axis_index("x"), lax.axis_index("y"), lax.axis_index("c")


def _other_chips(x, y):
    return [(1 - x, y), (x, 1 - y), (1 - x, 1 - y)]


def _rows(ref, start, n):
    return ref.at[pl.ds(pl.multiple_of(start, 16), n)]


def _window(ref, how, k, c=None):
    if how == "all":
        return ref
    if how == "lead":
        return ref.at[k]
    if how == "rows":
        n = ref.shape[0] // N_CHIPS
        if c is None:
            return _rows(ref, k * n, n)
        return _rows(ref, k * n + c * (n // 2), n // 2)
    n = ref.shape[1] // N_CHIPS
    cols = pl.ds(pl.multiple_of(k * n, LANES), n)
    if c is None:
        return ref.at[:, cols]
    h = ref.shape[0] // 2
    return ref.at[pl.ds(pl.multiple_of(c * h, 16), h), cols]


def _remote(src, dst, sems, s, device):
    return pltpu.make_async_remote_copy(
        src_ref=src, dst_ref=dst, send_sem=sems.at[s], recv_sem=sems.at[s + 1], device_id=device, device_id_type=MESH)


class _GatherIci:
    aliased = True

    def __init__(self, fulls, hows, splits):
        self.fulls, self.hows, self.splits = list(fulls), list(hows), list(splits)

    def inputs(self):
        return self.fulls

    def out_shapes(self):
        return [jax.ShapeDtypeStruct(a.shape, a.dtype) for a in self.fulls]

    def n_sems(self):
        return 6 * len(self.fulls)

    def build(self, ins, outs, sems, base):
        x, y, c = _place()
        me = 2 * x + y
        starts, waits = [], []
        for a, (how, sp) in enumerate(zip(self.hows, self.splits)):
            half = c if sp else None
            mine = _window(outs[a], how, me, half)
            for j, (px, py) in enumerate(_other_chips(x, y)):
                s = base + 6 * a + 2 * j
                cp = _remote(mine, mine, sems, s, (px, py, c))
                landing = _remote(mine, _window(outs[a], how, 2 * px + py, half), sems, s, (px, py, c))
                starts.append(cp.start)
                waits += [landing.wait_recv, cp.wait_send]
        return starts, waits


class _GatherD2d:
    aliased = True

    def __init__(self, fulls, hows):
        self.fulls, self.hows = list(fulls), list(hows)

    def inputs(self):
        return self.fulls

    def out_shapes(self):
        return [jax.ShapeDtypeStruct(a.shape, a.dtype) for a in self.fulls]

    def n_sems(self):
        return 6 * len(self.fulls)

    def build(self, ins, outs, sems, base):
        x, y, c = _place()
        starts, waits = [], []
        for a, how in enumerate(self.hows):
            for j, (px, py) in enumerate(_other_chips(x, y)):
                s = base + 6 * a + 2 * j
                got = _window(outs[a], how, 2 * px + py, c)
                cp = _remote(got, got, sems, s, (x, y, 1 - c))
                landing = _remote(got, _window(outs[a], how, 2 * px + py, 1 - c), sems, s, (x, y, 1 - c))
                starts.append(cp.start)
                waits += [landing.wait_recv, cp.wait_send]
        return starts, waits


def _part_shape(a, how):
    if how == "all":
        return a.shape
    if how == "rows":
        return (a.shape[0] // N_CHIPS, a.shape[1])
    return (a.shape[0], a.shape[1] // N_CHIPS)


class _Scatter:
    aliased = False

    def __init__(self, fulls, hows):
        self.fulls, self.hows = list(fulls), list(hows)

    def inputs(self):
        return self.fulls

    def out_shapes(self):
        return [jax.ShapeDtypeStruct((3,) + _part_shape(a, h), a.dtype) for a, h in zip(self.fulls, self.hows)]

    def n_sems(self):
        return 6 * len(self.fulls)

    def build(self, ins, outs, sems, base):
        x, y, c = _place()
        starts, waits = [], []
        for a, how in enumerate(self.hows):
            for j, (px, py) in enumerate(_other_chips(x, y)):
                cp = _remote(_window(ins[a], how, 2 * px + py), outs[a].at[j], sems, base + 6 * a + 2 * j, (px, py, c))
                starts.append(cp.start)
                waits += [cp.wait_recv, cp.wait_send]
        return starts, waits


class _Swap:
    aliased = False

    def __init__(self, arrays):
        self.arrays = list(arrays)

    def inputs(self):
        return self.arrays

    def out_shapes(self):
        return [jax.ShapeDtypeStruct(a.shape, a.dtype) for a in self.arrays]

    def n_sems(self):
        return 2 * len(self.arrays)

    def build(self, ins, outs, sems, base):
        x, y, c = _place()
        starts, waits = [], []
        for a in range(len(ins)):
            cp = _remote(ins[a], outs[a], sems, base + 2 * a, (x, y, 1 - c))
            starts.append(cp.start)
            waits += [cp.wait_recv, cp.wait_send]
        return starts, waits


def _call(name, body, grid, in_specs, out_specs, out_shape, args, scratch=(), comm=()):
    comm = list(comm)
    n_in, n_out, n_scr = len(args), len(out_shape), len(scratch)
    c_in = [a for op in comm for a in op.inputs()]
    c_out = [s for op in comm for s in op.out_shapes()]
    n_sems = sum(op.n_sems() for op in comm)
    aliases, i_in, i_out = {}, 0, 0
    for op in comm:
        if op.aliased:
            for q in range(len(op.inputs())):
                aliases[n_in + i_in + q] = n_out + i_out + q
        i_in, i_out = i_in + len(op.inputs()), i_out + len(op.out_shapes())

    def wrapped(*refs):
        ins = refs[:n_in]
        cin = refs[n_in : n_in + len(c_in)]
        o0 = n_in + len(c_in)
        outs = refs[o0 : o0 + n_out]
        cout = refs[o0 + n_out : o0 + n_out + len(c_out)]
        s0 = o0 + n_out + len(c_out)
        scr = refs[s0 : s0 + n_scr]

        def copies():
            sems = refs[s0 + n_scr]
            starts, waits = [], []
            i_in = i_out = base = 0
            for op in comm:
                ni, no = len(op.inputs()), len(op.out_shapes())
                s, w = op.build(cin[i_in : i_in + ni], cout[i_out : i_out + no], sems, base)
                starts += s
                waits += w
                i_in, i_out, base = i_in + ni, i_out + no, base + op.n_sems()
            return starts, waits

        def run_starts():
            for start in copies()[0]:
                start()

        def run_waits():
            for wait in copies()[1]:
                wait()

        if comm and grid:
            first = last = True
            for d, n in enumerate(grid):
                first = jnp.logical_and(first, pl.program_id(d) == 0)
                last = jnp.logical_and(last, pl.program_id(d) == n - 1)
            pl.when(first)(run_starts)
        elif comm:
            run_starts()
        if body is not None:
            body(*ins, *outs, *scr)
        if comm and grid:
            pl.when(last)(run_waits)
        elif comm:
            run_waits()

    res = pl.pallas_call(
        wrapped,
        name=name,
        grid=grid,
        in_specs=list(in_specs) + [ANY] * len(c_in),
        out_specs=list(out_specs) + [ANY] * len(c_out),
        out_shape=list(out_shape) + c_out,
        scratch_shapes=list(scratch) + ([pltpu.SemaphoreType.DMA((n_sems,))] if comm else []),
        input_output_aliases=aliases,
        compiler_params=pltpu.CompilerParams(dimension_semantics=("arbitrary",) * len(grid), vmem_limit_bytes=VMEM_LIMIT),
    )(*args, *c_in)
    return tuple(res[:n_out]), tuple(res[n_out:])


def _gather_now(fulls, hows, splits):
    n = len(fulls)
    ici = _GatherIci(fulls, hows, splits)
    split_ids = [a for a in range(n) if splits[a]]
    d2d = _GatherD2d([fulls[a] for a in split_ids], [hows[a] for a in split_ids])

    def body(*refs):
        outs, sems = refs[n : 2 * n], refs[2 * n]
        for op, op_refs, base in ((ici, outs, 0), (d2d, [outs[a] for a in split_ids], ici.n_sems())):
            starts, waits = op.build(None, op_refs, sems, base)
            for start in starts:
                start()
            for wait in waits:
                wait()

    return pl.pallas_call(
        body,
        name="gather_first",
        in_specs=[ANY] * n,
        out_specs=[ANY] * n,
        out_shape=ici.out_shapes(),
        scratch_shapes=[pltpu.SemaphoreType.DMA((ici.n_sems() + d2d.n_sems(),))],
        input_output_aliases={a: a for a in range(n)},
    )(*fulls)


def _place_weights(shards, hows, full_shapes, dtypes):
    n = len(shards)

    def body(*refs):
        ins, outs, bufs, sems = refs[:n], refs[n : 2 * n], refs[2 * n : 3 * n], refs[3 * n]
        x, y, _ = _place()
        copies = []
        for a in range(n):
            bufs[a][...] = ins[a][...].astype(dtypes[a])
            cp = pltpu.make_async_copy(bufs[a], _window(outs[a], hows[a], 2 * x + y), sems.at[a])
            cp.start()
            copies.append(cp)
        for cp in copies:
            cp.wait()

    return pl.pallas_call(
        body,
        name="place_weights",
        in_specs=[pl.BlockSpec(memory_space=pltpu.VMEM)] * n,
        out_specs=[ANY] * n,
        out_shape=[jax.ShapeDtypeStruct(s, d) for s, d in zip(full_shapes, dtypes)],
        scratch_shapes=[pltpu.VMEM(a.shape, d) for a, d in zip(shards, dtypes)] + [pltpu.SemaphoreType.DMA((n,))],
        compiler_params=pltpu.CompilerParams(vmem_limit_bytes=VMEM_LIMIT),
    )(*shards)


def _in_proj(x, g_mix, w_in_b, b_in, comm=()):
    T, D = x.shape
    CI = w_in_b.shape[1]
    tm = _tile(T, 512)

    def body(x_ref, g_ref, w_ref, b_ref, z_ref, xn_ref):
        xv = x_ref[...]
        r = lax.rsqrt(jnp.mean(xv * xv, axis=-1, keepdims=True) + RMS_EPS)
        xn = (xv * r * g_ref[...]).astype(BF16)
        xn_ref[...] = xn
        z_ref[...] = _dot(xn, w_ref[...], NN) + b_ref[...]

    return _call(
        "in_proj",
        body,
        (T // tm,),
        [
            pl.BlockSpec((tm, D), lambda i: (i, 0)),
            pl.BlockSpec((1, D), lambda i: (0, 0)),
            pl.BlockSpec((D, CI), lambda i: (0, 0)),
            pl.BlockSpec((1, CI), lambda i: (0, 0)),
        ],
        [pl.BlockSpec((tm, CI), lambda i: (i, 0)), pl.BlockSpec((tm, D), lambda i: (i, 0))],
        [jax.ShapeDtypeStruct((T, CI), F32), jax.ShapeDtypeStruct((T, D), BF16)],
        (x, g_mix, w_in_b, b_in),
        comm=comm,
    )


def _pool_mean_minus_token(p_scr, cs, w, cnt, tt):
    tok = p_scr[HALO : HALO + tt, cs]
    s = tok
    for d in range(1, w):
        s = s + p_scr[HALO - d : HALO - d + tt, cs]
    return s / cnt - tok


def _seq_fwd(z, w_dw4, b_dw, ln_g, ln_b, w_pool_b, s_pool, comm=()):
    T, CI = z.shape
    CC = ln_g.shape[1]
    n_grp, G = w_pool_b.shape[0], w_pool_b.shape[-1]
    KW = w_dw4.shape[1]
    D = CC + n_grp * G
    tt = _tile(T, 256, HALO)
    per = tt // HALO

    def body(zc_ref, zp_ref, wdw_ref, bdw_ref, lng_ref, lnb_ref, wp_ref, sp_ref, y_ref, v_ref, u_scr, p_scr):
        i = pl.program_id(0)
        first = i == 0
        u_prev = zp_ref[:, 0:CC] * _sigmoid(zp_ref[:, CC : 2 * CC])
        u_scr[0:HALO, :] = jnp.where(first, 0.0, u_prev)
        p_scr[0:HALO, :] = jnp.where(first, 0.0, zp_ref[:, 2 * CC :])
        u_scr[HALO:, :] = zc_ref[:, 0:CC] * _sigmoid(zc_ref[:, CC : 2 * CC])
        p_scr[HALO:, :] = zc_ref[:, 2 * CC :]

        for j in range(CC // LANES):
            cs = slice(LANES * j, LANES * (j + 1))
            for rb in range(tt // CONV_ROWS):
                acc = jnp.zeros((CONV_ROWS, LANES), F32)
                for k in range(KW):
                    off = HALO - (KW - 1) + k + rb * CONV_ROWS
                    acc = acc + u_scr[off : off + CONV_ROWS, cs] * wdw_ref[j, k : k + 1, :]
                v_ref[rb * CONV_ROWS : (rb + 1) * CONV_ROWS, cs] = acc + bdw_ref[:, cs]

        v = v_ref[...]
        mu = jnp.mean(v, axis=-1, keepdims=True)
        d = v - mu
        var = jnp.mean(d * d, axis=-1, keepdims=True)
        ln = d * lax.rsqrt(var + LN_EPS) * lng_ref[...] + lnb_ref[...]
        y_ref[:, 0:CC] = (ln * _sigmoid(ln)).astype(BF16)

        tpos = i * tt + lax.broadcasted_iota(jnp.int32, (tt, 1), 0)
        for gi, w in enumerate(POOL_WINDOWS):
            cs = slice(G * gi, G * (gi + 1))
            cnt = jnp.minimum(tpos + 1, w).astype(F32)
            yi = _pool_mean_minus_token(p_scr, cs, w, cnt, tt)
            q = _dot(yi.astype(BF16), wp_ref[gi], NN)
            y_ref[:, CC + G * gi : CC + G * (gi + 1)] = (q * sp_ref[:, cs]).astype(BF16)

    const2 = lambda i: (0, 0)
    return _call(
        "seq_fwd",
        body,
        (T // tt,),
        [
            pl.BlockSpec((tt, CI), lambda i: (i, 0)),
            pl.BlockSpec((HALO, CI), lambda i: (jnp.maximum(i * per - 1, 0), 0)),
            pl.BlockSpec(w_dw4.shape, lambda i: (0, 0, 0)),
            pl.BlockSpec((1, CC), const2),
            pl.BlockSpec((1, CC), const2),
            pl.BlockSpec((1, CC), const2),
            pl.BlockSpec(w_pool_b.shape, lambda i: (0, 0, 0)),
            pl.BlockSpec((1, n_grp * G), const2),
        ],
        [pl.BlockSpec((tt, D), lambda i: (i, 0)), pl.BlockSpec((tt, CC), lambda i: (i, 0))],
        [jax.ShapeDtypeStruct((T, D), BF16), jax.ShapeDtypeStruct((T, CC), F32)],
        (z, z, w_dw4, b_dw, ln_g, ln_b, w_pool_b, s_pool),
        scratch=[pltpu.VMEM((HALO + tt, CC), F32), pltpu.VMEM((HALO + tt, n_grp * G), F32)],
        comm=comm,
    )


def _out_proj(y_b, x, w_out_b, g_ffn, comm=()):
    T, D = x.shape
    tm = _tile(T, 512)

    def body(y_ref, x_ref, w_ref, g_ref, h1_ref, hn_ref):
        h1 = x_ref[...] + _dot(y_ref[...], w_ref[...], NN)
        h1_ref[...] = h1
        r = lax.rsqrt(jnp.mean(h1 * h1, axis=-1, keepdims=True) + RMS_EPS)
        hn_ref[...] = (h1 * r * g_ref[...]).astype(BF16)

    row = lambda i: (i, 0)
    return _call(
        "out_proj",
        body,
        (T // tm,),
        [
            pl.BlockSpec((tm, y_b.shape[1]), row),
            pl.BlockSpec((tm, D), row),
            pl.BlockSpec(w_out_b.shape, lambda i: (0, 0)),
            pl.BlockSpec((1, D), lambda i: (0, 0)),
        ],
        [pl.BlockSpec((tm, D), row), pl.BlockSpec((tm, D), row)],
        [jax.ShapeDtypeStruct((T, D), F32), jax.ShapeDtypeStruct((T, D), BF16)],
        (y_b, x, w_out_b, g_ffn),
        comm=comm,
    )


def _hidden_tile(F):
    return _tile(F, 1408, LANES)


def _gate_up(hn_b, wgT_b, wuT_b, comm=()):
    T, D = hn_b.shape
    F = wgT_b.shape[0]
    tm, tf = _tile(T, 512), _hidden_tile(F)

    def body(hn_ref, wg_ref, wu_ref, g_ref, u_ref, a_ref):
        hn = hn_ref[...]
        gv = _dot(hn, wg_ref[...], NT)
        uv = _dot(hn, wu_ref[...], NT)
        g_ref[...] = gv.astype(BF16)
        u_ref[...] = uv.astype(BF16)
        a_ref[...] = (gv * _sigmoid(gv) * uv).astype(BF16)

    wspec = pl.BlockSpec((tf, D), lambda j, i: (j, 0))
    ospec = pl.BlockSpec((tm, tf), lambda j, i: (i, j))
    return _call(
        "gate_up",
        body,
        (F // tf, T // tm),
        [pl.BlockSpec((tm, D), lambda j, i: (i, 0)), wspec, wspec],
        [ospec, ospec, ospec],
        [jax.ShapeDtypeStruct((T, F), BF16)] * 3,
        (hn_b, wgT_b, wuT_b),
        comm=comm,
    )


def _down_loss(a_b, wd_b, h1, target, g_final, comm=()):
    T, D = h1.shape
    F = a_b.shape[1]
    tm = _tile(T, 256)
    nt = T // tm

    def body(a_ref, w_ref, h1_ref, t_ref, g_ref, dh2_ref, dh2b_ref, loss_ref, dg_ref):
        i = pl.program_id(0)
        h2 = h1_ref[...] + _dot(a_ref[...], w_ref[...], NN)
        r = lax.rsqrt(jnp.mean(h2 * h2, axis=-1, keepdims=True) + RMS_EPS)
        g = g_ref[...]
        diff = h2 * r * g - t_ref[...]
        loss_ref[...] = jnp.full(loss_ref.shape, jnp.sum(diff * diff), F32)
        dh2, dg_rows = _rms_bwd(h2, g, diff * (1.0 / D))
        dh2_ref[...] = dh2
        dh2b_ref[...] = dh2.astype(BF16)
        _accumulate(dg_ref, i == 0, jnp.sum(dg_rows, axis=0, keepdims=True))

    row = lambda i: (i, 0)
    return _call(
        "down_loss",
        body,
        (nt,),
        [
            pl.BlockSpec((tm, F), row),
            pl.BlockSpec((F, D), lambda i: (0, 0)),
            pl.BlockSpec((tm, D), row),
            pl.BlockSpec((tm, D), row),
            pl.BlockSpec((1, D), lambda i: (0, 0)),
        ],
        [
            pl.BlockSpec((tm, D), row),
            pl.BlockSpec((tm, D), row),
            pl.BlockSpec((1, 1, LANES), lambda i: (i, 0, 0)),
            pl.BlockSpec((1, D), lambda i: (0, 0)),
        ],
        [
            jax.ShapeDtypeStruct((T, D), F32),
            jax.ShapeDtypeStruct((T, D), BF16),
            jax.ShapeDtypeStruct((nt, 1, LANES), F32),
            jax.ShapeDtypeStruct((1, D), F32),
        ],
        (a_b, wd_b, h1, target, g_final),
        comm=comm,
    )


def _ffn_bwd_act(dh2_b, wd_b, g_b, u_b, comm=()):
    T, D = dh2_b.shape
    F = wd_b.shape[0]
    tm, tf = _tile(T, 512), _hidden_tile(F)

    def body(d_ref, w_ref, g_ref, u_ref, dg_ref, du_ref):
        da = _dot(d_ref[...], w_ref[...], NT)
        gv = g_ref[...].astype(F32)
        uv = u_ref[...].astype(F32)
        sg = _sigmoid(gv)
        silu = gv * sg
        dg_ref[...] = (da * uv * (sg * (1.0 + gv * (1.0 - sg)))).astype(BF16)
        du_ref[...] = (da * silu).astype(BF16)

    aspec = pl.BlockSpec((tm, tf), lambda j, i: (i, j))
    return _call(
        "ffn_bwd_act",
        body,
        (F // tf, T // tm),
        [pl.BlockSpec((tm, D), lambda j, i: (i, 0)), pl.BlockSpec((tf, D), lambda j, i: (j, 0)), aspec, aspec],
        [aspec, aspec],
        [jax.ShapeDtypeStruct((T, F), BF16)] * 2,
        (dh2_b, wd_b, g_b, u_b),
        comm=comm,
    )


def _ffn_bwd_in(dg_b, du_b, wgT_b, wuT_b, h1, dh2, g_ffn, w_out_b, comm=()):
    T, D = h1.shape
    F = wgT_b.shape[0]
    DM = w_out_b.shape[0]
    tm = _tile(T, 256)

    def body(dg_ref, du_ref, wg_ref, wu_ref, h1_ref, dh2_ref, g_ref, wo_ref, dh1_ref, dh1b_ref, dy_ref, dgf_ref):
        i = pl.program_id(0)
        dhn = _dot(dg_ref[...], wg_ref[...], NN) + _dot(du_ref[...], wu_ref[...], NN)
        dx, dg_rows = _rms_bwd(h1_ref[...], g_ref[...], dhn)
        dh1 = dh2_ref[...] + dx
        dh1b = dh1.astype(BF16)
        dh1_ref[...] = dh1
        dh1b_ref[...] = dh1b
        dy_ref[...] = _dot(dh1b, wo_ref[...], NT)
        _accumulate(dgf_ref, i == 0, jnp.sum(dg_rows, axis=0, keepdims=True))

    row = lambda i: (i, 0)
    const = lambda i: (0, 0)
    return _call(
        "ffn_bwd_in",
        body,
        (T // tm,),
        [
            pl.BlockSpec((tm, F), row),
            pl.BlockSpec((tm, F), row),
            pl.BlockSpec((F, D), const),
            pl.BlockSpec((F, D), const),
            pl.BlockSpec((tm, D), row),
            pl.BlockSpec((tm, D), row),
            pl.BlockSpec((1, D), const),
            pl.BlockSpec((DM, D), const),
        ],
        [pl.BlockSpec((tm, D), row), pl.BlockSpec((tm, D), row), pl.BlockSpec((tm, DM), row), pl.BlockSpec((1, D), const)],
        [
            jax.ShapeDtypeStruct((T, D), F32),
            jax.ShapeDtypeStruct((T, D), BF16),
            jax.ShapeDtypeStruct((T, DM), F32),
            jax.ShapeDtypeStruct((1, D), F32),
        ],
        (dg_b, du_b, wgT_b, wuT_b, h1, dh2, g_ffn, w_out_b),
        comm=comm,
    )


def _seq_bwd(z, dy, v, w_dw4, ln_g, ln_b, w_pool_b, s_pool, comm=()):
    T, CI = z.shape
    CC = ln_g.shape[1]
    n_grp, G = w_pool_b.shape[0], w_pool_b.shape[-1]
    CP = n_grp * G
    KW = w_dw4.shape[1]
    n_cc = CC // LANES
    D = CC + CP
    tt = _tile(T, 256, HALO)
    per = tt // HALO
    n_tiles = T // tt
    last_halo = T // HALO - 1

    def body(zc_ref, zp_ref, dyc_ref, dyn_ref, vc_ref, vn_ref, wdw_ref, lng_ref, lnb_ref, wp_ref, sp_ref,
             dz_ref, dwdw_ref, dbdw_ref, dlng_ref, dlnb_ref, dwp_ref, dsp_ref, dbin_ref,
             dv_scr, u_scr, p_scr, g_scr, dw_scr):
        i = pl.program_id(0)
        first = i == 0
        last = i == n_tiles - 1
        lng, lnb = lng_ref[...], lnb_ref[...]

        def conv_pre(vv, dyc):
            mu = jnp.mean(vv, axis=-1, keepdims=True)
            d = vv - mu
            rs = lax.rsqrt(jnp.mean(d * d, axis=-1, keepdims=True) + LN_EPS)
            xh = d * rs
            ln = xh * lng + lnb
            sg = _sigmoid(ln)
            dln = dyc * (sg * (1.0 + ln * (1.0 - sg)))
            dxh = dln * lng
            dv = rs * (dxh - jnp.mean(dxh, axis=-1, keepdims=True) - xh * jnp.mean(dxh * xh, axis=-1, keepdims=True))
            return dv, dln, xh

        dv_c, dln_c, xh_c = conv_pre(vc_ref[...], dyc_ref[:, 0:CC])
        dv_scr[0:tt, :] = dv_c
        dv_n, _, _ = conv_pre(vn_ref[...], dyn_ref[:, 0:CC])
        dv_scr[tt:, :] = jnp.where(last, 0.0, dv_n)
        _accumulate(dlng_ref, first, jnp.sum(dln_c * xh_c, axis=0, keepdims=True))
        _accumulate(dlnb_ref, first, jnp.sum(dln_c, axis=0, keepdims=True))
        _accumulate(dbdw_ref, first, jnp.sum(dv_c, axis=0, keepdims=True))

        u_scr[...] = zc_ref[:, 0:CC] * _sigmoid(zc_ref[:, CC : 2 * CC])

        @pl.when(first)
        def _():
            dw_scr[...] = jnp.zeros_like(dw_scr)

        for j in range(n_cc):
            cs = slice(LANES * j, LANES * (j + 1))
            gs = slice(CC + LANES * j, CC + LANES * (j + 1))
            dbin_a = jnp.zeros((1, LANES), F32)
            dbin_g = jnp.zeros((1, LANES), F32)
            for rb in range(tt // CONV_ROWS):
                rows = slice(rb * CONV_ROWS, (rb + 1) * CONV_ROWS)
                u_blk = u_scr[rows, cs]
                du = jnp.zeros((CONV_ROWS, LANES), F32)
                for k in range(KW):
                    off = rb * CONV_ROWS + (KW - 1) - k
                    d = dv_scr[off : off + CONV_ROWS, cs]
                    du = du + d * wdw_ref[j, k : k + 1, :]
                    dw_scr[j * HALO + k] += jnp.sum((u_blk * d).reshape(CONV_ROWS // 8, 8, LANES), axis=0)
                a = zc_ref[rows, cs]
                sg = _sigmoid(zc_ref[rows, gs])
                da = du * sg
                dgate = du * a * sg * (1.0 - sg)
                dz_ref[rows, cs] = da.astype(BF16)
                dz_ref[rows, gs] = dgate.astype(BF16)
                dbin_a = dbin_a + jnp.sum(da, axis=0, keepdims=True)
                dbin_g = dbin_g + jnp.sum(dgate, axis=0, keepdims=True)
            _accumulate(dbin_ref.at[:, cs], first, dbin_a)
            _accumulate(dbin_ref.at[:, gs], first, dbin_g)

        @pl.when(last)
        def _():
            dwdw_ref[...] = jnp.sum(dw_scr[...], axis=1).reshape(dwdw_ref.shape)

        p_scr[0:HALO, :] = jnp.where(first, 0.0, zp_ref[:, 2 * CC :])
        p_scr[HALO:, :] = zc_ref[:, 2 * CC :]
        tpos = i * tt + lax.broadcasted_iota(jnp.int32, (tt, 1), 0)
        for gi, w in enumerate(POOL_WINDOWS):
            cs = slice(G * gi, G * (gi + 1))
            ys = slice(CC + G * gi, CC + G * (gi + 1))
            ps = slice(2 * CC + G * gi, 2 * CC + G * (gi + 1))
            cnt = jnp.minimum(tpos + 1, w).astype(F32)
            yib = _pool_mean_minus_token(p_scr, cs, w, cnt, tt).astype(BF16)
            wp = wp_ref[gi]
            sp = sp_ref[:, cs]
            dyp = dyc_ref[:, ys]
            q = _dot(yib, wp, NN)
            _accumulate(dsp_ref.at[:, cs], first, jnp.sum(dyp * q, axis=0, keepdims=True))
            dq_c = (dyp * sp).astype(BF16)
            dq_n = (jnp.where(last, 0.0, dyn_ref[:, ys]) * sp).astype(BF16)
            _accumulate(dwp_ref.at[gi], first, _dot(yib, dq_c, TN))
            dyi_c = _dot(dq_c, wp, NT)
            g_scr[0:tt, cs] = dyi_c / cnt
            g_scr[tt:, cs] = _dot(dq_n, wp, NT) * (1.0 / w)
            dp = -dyi_c
            for d in range(w):
                dp = dp + g_scr[d : d + tt, cs]
            dz_ref[:, ps] = dp.astype(BF16)
            _accumulate(dbin_ref.at[:, ps], first, jnp.sum(dp, axis=0, keepdims=True))

    cur = lambda i: (i, 0)
    prev = lambda i: (jnp.maximum(i * per - 1, 0), 0)
    nxt = lambda i: (jnp.minimum((i + 1) * per, last_halo), 0)
    c2 = lambda i: (0, 0)
    c3 = lambda i: (0, 0, 0)
    return _call(
        "seq_bwd",
        body,
        (n_tiles,),
        [
            pl.BlockSpec((tt, CI), cur),
            pl.BlockSpec((HALO, CI), prev),
            pl.BlockSpec((tt, D), cur),
            pl.BlockSpec((HALO, D), nxt),
            pl.BlockSpec((tt, CC), cur),
            pl.BlockSpec((HALO, CC), nxt),
            pl.BlockSpec(w_dw4.shape, c3),
            pl.BlockSpec((1, CC), c2),
            pl.BlockSpec((1, CC), c2),
            pl.BlockSpec(w_pool_b.shape, c3),
            pl.BlockSpec((1, CP), c2),
        ],
        [
            pl.BlockSpec((tt, CI), cur),
            pl.BlockSpec((n_cc, HALO, LANES), c3),
            pl.BlockSpec((1, CC), c2),
            pl.BlockSpec((1, CC), c2),
            pl.BlockSpec((1, CC), c2),
            pl.BlockSpec((n_grp, G, G), c3),
            pl.BlockSpec((1, CP), c2),
            pl.BlockSpec((1, CI), c2),
        ],
        [
            jax.ShapeDtypeStruct((T, CI), BF16),
            jax.ShapeDtypeStruct((n_cc, HALO, LANES), F32),
            jax.ShapeDtypeStruct((1, CC), F32),
            jax.ShapeDtypeStruct((1, CC), F32),
            jax.ShapeDtypeStruct((1, CC), F32),
            jax.ShapeDtypeStruct((n_grp, G, G), F32),
            jax.ShapeDtypeStruct((1, CP), F32),
            jax.ShapeDtypeStruct((1, CI), F32),
        ],
        (z, z, dy, dy, v, v, w_dw4, ln_g, ln_b, w_pool_b, s_pool),
        scratch=[
            pltpu.VMEM((tt + HALO, CC), F32),
            pltpu.VMEM((tt, CC), F32),
            pltpu.VMEM((HALO + tt, CP), F32),
            pltpu.VMEM((tt + HALO, CP), F32),
            pltpu.VMEM((n_cc * HALO, 8, LANES), F32),
        ],
        comm=comm,
    )


def _in_proj_bwd(dz_b, w_in_b, x, dh1, g_mix, comm=()):
    T, D = x.shape
    CI = w_in_b.shape[1]
    tm = _tile(T, 512)

    def body(dz_ref, w_ref, x_ref, dh1_ref, g_ref, dx_ref, dg_ref):
        i = pl.program_id(0)
        dxn = _dot(dz_ref[...], w_ref[...], NT)
        dx, dg_rows = _rms_bwd(x_ref[...], g_ref[...], dxn)
        dx_ref[...] = dh1_ref[...] + dx
        _accumulate(dg_ref, i == 0, jnp.sum(dg_rows, axis=0, keepdims=True))

    row = lambda i: (i, 0)
    const = lambda i: (0, 0)
    return _call(
        "in_proj_bwd",
        body,
        (T // tm,),
        [
            pl.BlockSpec((tm, CI), row),
            pl.BlockSpec((D, CI), const),
            pl.BlockSpec((tm, D), row),
            pl.BlockSpec((tm, D), row),
            pl.BlockSpec((1, D), const),
        ],
        [pl.BlockSpec((tm, D), row), pl.BlockSpec((1, D), const)],
        [jax.ShapeDtypeStruct((T, D), F32), jax.ShapeDtypeStruct((1, D), F32)],
        (dz_b, w_in_b, x, dh1, g_mix),
        comm=comm,
    )


def _weight_grad(name, a_b, b_b, comm=()):
    T, N1 = a_b.shape
    N2 = b_b.shape[1]
    t1 = _tile(N1, 1408, LANES)
    tk = _tile(T, 1024)
    nk = T // tk

    def body(a_ref, b_ref, o_ref, acc):
        k = pl.program_id(1)
        _accumulate(acc, k == 0, _dot(a_ref[...], b_ref[...], TN))

        @pl.when(k == nk - 1)
        def _():
            o_ref[...] = acc[...].astype(BF16)

    (out,), rest = _call(
        name,
        body,
        (N1 // t1, nk),
        [pl.BlockSpec((tk, t1), lambda n, k: (k, n)), pl.BlockSpec((tk, N2), lambda n, k: (k, 0))],
        [pl.BlockSpec((t1, N2), lambda n, k: (n, 0))],
        [jax.ShapeDtypeStruct((N1, N2), BF16)],
        (a_b, b_b),
        scratch=[pltpu.VMEM((t1, N2), F32)],
        comm=comm,
    )
    return out, rest


def _sum_parts(name, full, how, parts, me):
    _, R, C = parts.shape
    tr = _tile(R, 512)
    nb = R // tr

    def body(me_ref, own_ref, p_ref, o_ref):
        f = lambda q: p_ref[q].astype(F32)
        o_ref[...] = (own_ref[...].astype(F32) + f(0)) + (f(1) + f(2))

    own_map = {"rows": lambda i, me_ref: (me_ref[0] * nb + i, 0), "cols": lambda i, me_ref: (i, me_ref[0]),
               "all": lambda i, me_ref: (i, 0)}[how]
    return pl.pallas_call(
        body,
        name=name,
        grid_spec=pltpu.PrefetchScalarGridSpec(
            num_scalar_prefetch=1,
            grid=(nb,),
            in_specs=[pl.BlockSpec((tr, C), own_map), pl.BlockSpec((3, tr, C), lambda i, me_ref: (0, i, 0))],
            out_specs=pl.BlockSpec((tr, C), lambda i, me_ref: (i, 0)),
        ),
        out_shape=jax.ShapeDtypeStruct((R, C), F32),
        compiler_params=pltpu.CompilerParams(dimension_semantics=("arbitrary",), vmem_limit_bytes=VMEM_LIMIT),
    )(me, full, parts)


_M_CORR = 1.0 - ADAM_B1**ADAM_STEP
_V_CORR = 1.0 - ADAM_B2**ADAM_STEP


def _adamw_math(w, g, m, v):
    m = ADAM_B1 * m + (1.0 - ADAM_B1) * g
    v = ADAM_B2 * v + (1.0 - ADAM_B2) * (g * g)
    delta = -ADAM_LR * ((m / _M_CORR) / (jnp.sqrt(v / _V_CORR) + ADAM_EPS) + ADAM_WD * w)
    return delta, m, v


def _adamw(name, w, m, v, g_here, g_there, comm=()):
    R, C = w.shape
    tr = _tile(R, 256)

    def body(w_ref, m_ref, v_ref, ga_ref, gb_ref, g_ref, d_ref, nm_ref, nv_ref):
        g = ga_ref[...] + gb_ref[...]
        g_ref[...] = g
        d_ref[...], nm_ref[...], nv_ref[...] = _adamw_math(w_ref[...], g, m_ref[...], v_ref[...])

    spec = pl.BlockSpec((tr, C), lambda i: (i, 0))
    return _call(name, body, (R // tr,), [spec] * 5, [spec] * 4, [jax.ShapeDtypeStruct((R, C), F32)] * 4,
                 (w, m, v, g_here, g_there), comm=comm)


class _PackLayout:
    def __init__(self, n_cc, n_grp, G, widths):
        self.dw_rows = (0, HALO)
        self.wp_rows = (HALO, HALO + G)
        self.n_cc, self.n_grp, self.G = n_cc, n_grp, G
        self.vec = {}
        r = HALO + G
        for name, width in widths:
            self.vec[name] = (r, width)
            r += width // PACK_W
        self.rows = -(-r // 8) * 8


def _pack_small(layout, dwdw, dwp, vecs):
    names = list(vecs)

    def body(*refs):
        dw_ref, wp_ref = refs[0], refs[1]
        vec_refs = refs[2 : 2 + len(names)]
        o_ref = refs[-1]
        o_ref[...] = jnp.zeros_like(o_ref)
        for j in range(layout.n_cc):
            o_ref[layout.dw_rows[0] : layout.dw_rows[1], j * LANES : (j + 1) * LANES] = dw_ref[j]
        for i in range(layout.n_grp):
            o_ref[layout.wp_rows[0] : layout.wp_rows[1], i * layout.G : (i + 1) * layout.G] = wp_ref[i]
        for name, ref in zip(names, vec_refs):
            r, width = layout.vec[name]
            for h in range(width // PACK_W):
                o_ref[r + h : r + h + 1, :] = ref[:, h * PACK_W : (h + 1) * PACK_W]

    return pl.pallas_call(
        body,
        name="pack_small",
        out_shape=jax.ShapeDtypeStruct((layout.rows, PACK_W), F32),
    )(dwdw, dwp, *[vecs[k] for k in names])


def _adamw_small(layout, g_here, g_there, w_dw, m_dw, v_dw, w_pool, m_pool, v_pool, vec_w, vec_m, vec_v):
    names = list(vec_w)
    nv = len(names)

    def body(*refs):
        ga_ref, gb_ref = refs[0], refs[1]
        wdw, mdw, vdw, wp, mp, vp = refs[2:8]
        vw, vm, vv = refs[8 : 8 + nv], refs[8 + nv : 8 + 2 * nv], refs[8 + 2 * nv : 8 + 3 * nv]
        outs = refs[8 + 3 * nv :]
        acc = outs[-1]
        acc[...] = ga_ref[...] + gb_ref[...]

        def emit(o, g, w, m, v, idx=()):
            res = (g,) + _adamw_math(w, g, m, v)
            for ref, val in zip(o, res):
                ref[idx] = val

        me = 2 * lax.axis_index("x") + lax.axis_index("y")
        for j in range(layout.n_cc):

            @pl.when(me == j)
            def _(j=j):
                g = acc[layout.dw_rows[0] : layout.dw_rows[1], j * LANES : (j + 1) * LANES]
                emit(outs[0:4], g, wdw[...], mdw[...], vdw[...], idx=...)

        for i in range(layout.n_grp):
            g = acc[layout.wp_rows[0] : layout.wp_rows[1], i * layout.G : (i + 1) * layout.G]
            emit(outs[4:8], g, wp[i], mp[i], vp[i], idx=i)
        for q, name in enumerate(names):
            r, width = layout.vec[name]
            for h in range(width // PACK_W):
                ls = slice(h * PACK_W, (h + 1) * PACK_W)
                g = acc[r + h : r + h + 1, :]
                emit(outs[8 + 4 * q : 12 + 4 * q], g, vw[q][:, ls], vm[q][:, ls], vv[q][:, ls], idx=(slice(None), ls))

    shapes = [w_dw.shape] * 4 + [w_pool.shape] * 4
    for name in names:
        shapes += [vec_w[name].shape] * 4
    return pl.pallas_call(
        body,
        name="adamw_small",
        out_shape=[jax.ShapeDtypeStruct(s, F32) for s in shapes],
        scratch_shapes=[pltpu.VMEM(g_here.shape, F32)],
    )(g_here, g_there, w_dw, m_dw, v_dw, w_pool, m_pool, v_pool,
      *[vec_w[k] for k in names], *[vec_m[k] for k in names], *[vec_v[k] for k in names])


def _allreduce_adamw_row(g_part, w, m, v, comm=()):
    D = w.shape[1]

    def body(g_ref, w_ref, m_ref, v_ref, go_ref, d_ref, nm_ref, nv_ref, land, sems):
        x, y, c = _place()
        copies = []
        for r in range(1, N_DEV):
            fx, fy, fc = (r >> 2) & 1, (r >> 1) & 1, r & 1
            peer = (1 - x if fx else x, 1 - y if fy else y, 1 - c if fc else c)
            cp = _remote(g_ref, land.at[r], sems, 2 * (r - 1), peer)
            cp.start()
            copies.append(cp)
        for cp in copies:
            cp.wait()
        row = lambda r: g_ref[...] if r == 0 else land[r]
        g = ((row(0) + row(4)) + (row(2) + row(6))) + ((row(1) + row(5)) + (row(3) + row(7)))
        go_ref[...] = g
        d_ref[...], nm_ref[...], nv_ref[...] = _adamw_math(w_ref[...], g, m_ref[...], v_ref[...])

    vm = pl.BlockSpec(memory_space=pltpu.VMEM)
    return _call(
        "allreduce_adamw_g_mix",
        body,
        (),
        [vm] * 4,
        [vm] * 4,
        [jax.ShapeDtypeStruct((1, D), F32)] * 4,
        (g_part, w, m, v),
        scratch=[pltpu.VMEM((N_DEV, 1, D), F32), pltpu.SemaphoreType.DMA((2 * (N_DEV - 1),))],
        comm=comm,
    )


def kernel(x, g_mix, w_in, b_in, w_dw, b_dw, ln_g, ln_b, w_pool, s_pool, w_out, g_ffn, w_gate, w_up, w_down, g_final, loss_target, m_g_mix, m_w_in, m_b_in, m_w_dw, m_b_dw, m_ln_g, m_ln_b, m_w_pool, m_s_pool, m_w_out, m_g_ffn, m_w_gate, m_w_up, m_w_down, m_g_final, v_g_mix, v_w_in, v_b_in, v_w_dw, v_b_dw, v_ln_g, v_ln_b, v_w_pool, v_s_pool, v_w_out, v_g_ffn, v_w_gate, v_w_up, v_w_down, v_g_final):
    x2 = x[0]
    target = loss_target[0]
    T, D = x2.shape
    w_in2, w_out2, w_down2, w_dw2 = w_in[0], w_out[0], w_down[0], w_dw[0]
    w_gateT, w_upT = w_gate[0].T, w_up[0].T
    CI = w_in2.shape[1] * N_CHIPS
    DM = w_out2.shape[0] * N_CHIPS
    F = w_down2.shape[0] * N_CHIPS
    KW, dw_cols = w_dw2.shape
    assert dw_cols == LANES
    n_grp, G = w_pool.shape[1], w_pool.shape[-1]
    w_pool3 = w_pool[0]
    g_final2 = g_final.reshape(1, D)

    me = (2 * lax.axis_index("x") + lax.axis_index("y")).astype(jnp.int32).reshape(1)

    f_in, f_out, f_gate, f_up, f_down, f_dw = _place_weights(
        [w_in2, w_out2, w_gateT, w_upT, w_down2, w_dw2], ["cols", "rows", "rows", "rows", "rows", "lead"],
        [(D, CI), (DM, D), (F, D), (F, D), (F, D), (N_CHIPS, KW, dw_cols)], [BF16] * 5 + [F32])
    w_in_b, w_dw4 = _gather_now([f_in, f_dw], ["cols", "lead"], [True, False])
    w_pool_b = w_pool3.astype(BF16)
    (z, xn_b), (f_out,) = _in_proj(x2, g_mix, w_in_b, b_in, comm=[_GatherIci([f_out], ["rows"], [True])])
    (y_b, v), (w_out_b, f_gate, f_up) = _seq_fwd(
        z, w_dw4, b_dw, ln_g, ln_b, w_pool_b, s_pool,
        comm=[_GatherD2d([f_out], ["rows"]), _GatherIci([f_gate, f_up], ["rows"] * 2, [True] * 2)])
    (h1, hn_b), (wgT_b, wuT_b, f_down) = _out_proj(
        y_b, x2, w_out_b, g_ffn,
        comm=[_GatherD2d([f_gate, f_up], ["rows"] * 2), _GatherIci([f_down], ["rows"], [True])])
    (g_b, u_b, a_b), (wd_b,) = _gate_up(hn_b, wgT_b, wuT_b, comm=[_GatherD2d([f_down], ["rows"])])
    (dh2, dh2_b, loss_parts, d_g_final), _ = _down_loss(a_b, wd_b, h1, target, g_final2)
    loss = lax.psum(jnp.sum(loss_parts[:, 0, 0]) * (0.5 / D), ("x", "y", "c"))

    gw_down, _ = _weight_grad("grad_w_down", a_b, dh2_b)
    (dg_b, du_b), (p_down,) = _ffn_bwd_act(dh2_b, wd_b, g_b, u_b, comm=[_Scatter([gw_down], ["rows"])])
    gw_gateT, _ = _weight_grad("grad_w_gate", dg_b, hn_b)
    gw_upT, _ = _weight_grad("grad_w_up", du_b, hn_b)
    sum_down = _sum_parts("sum_w_down", gw_down, "rows", p_down, me)
    (dh1, dh1_b, dy, d_g_ffn), (p_gate, oth_down) = _ffn_bwd_in(
        dg_b, du_b, wgT_b, wuT_b, h1, dh2, g_ffn, w_out_b, comm=[_Scatter([gw_gateT], ["rows"]), _Swap([sum_down])])
    gw_out, _ = _weight_grad("grad_w_out", y_b, dh1_b)
    sum_gate = _sum_parts("sum_w_gate", gw_gateT, "rows", p_gate, me)
    res = {}
    res["w_down"], _ = _adamw("adamw_w_down", w_down2, m_w_down[0], v_w_down[0], sum_down, oth_down)
    (dz_b, d_wdw, d_bdw, d_lng, d_lnb, d_wp, d_sp, d_bin), (p_up, p_out, oth_gate) = _seq_bwd(
        z, dy, v, w_dw4, ln_g, ln_b, w_pool_b, s_pool,
        comm=[_Scatter([gw_upT, gw_out], ["rows", "rows"]), _Swap([sum_gate])])
    gw_in, _ = _weight_grad("grad_w_in", xn_b, dz_b)
    vec_grads = {"b_dw": d_bdw, "ln_g": d_lng, "ln_b": d_lnb, "s_pool": d_sp, "g_ffn": d_g_ffn, "g_final": d_g_final, "b_in": d_bin}
    layout = _PackLayout(dw_cols * N_CHIPS // LANES, n_grp, G, [(k, a.shape[1]) for k, a in vec_grads.items()])
    pack = _pack_small(layout, d_wdw, d_wp, vec_grads)
    sum_up = _sum_parts("sum_w_up", gw_upT, "rows", p_up, me)
    sum_out = _sum_parts("sum_w_out", gw_out, "rows", p_out, me)
    res["w_gate"], _ = _adamw("adamw_w_gate", w_gateT, m_w_gate[0].T, v_w_gate[0].T, sum_gate, oth_gate)
    (grad_x, d_g_mix), (p_in, p_small, oth_up, oth_out) = _in_proj_bwd(
        dz_b, w_in_b, x2, dh1, g_mix, comm=[_Scatter([gw_in, pack], ["cols", "all"]), _Swap([sum_up, sum_out])])
    sum_in = _sum_parts("sum_w_in", gw_in, "cols", p_in, me)
    sum_small = _sum_parts("sum_small", pack, "all", p_small, me)
    res["g_mix"], (oth_in, oth_small) = _allreduce_adamw_row(d_g_mix, g_mix, m_g_mix, v_g_mix, comm=[_Swap([sum_in, sum_small])])
    res["w_up"], _ = _adamw("adamw_w_up", w_upT, m_w_up[0].T, v_w_up[0].T, sum_up, oth_up)
    res["w_out"], _ = _adamw("adamw_w_out", w_out2, m_w_out[0], v_w_out[0], sum_out, oth_out)
    res["w_in"], _ = _adamw("adamw_w_in", w_in2, m_w_in[0], v_w_in[0], sum_in, oth_in)

    pad_dw = lambda a: jnp.pad(a[0], ((0, HALO - KW), (0, 0)))
    vec_w = {"b_dw": b_dw, "ln_g": ln_g, "ln_b": ln_b, "s_pool": s_pool, "g_ffn": g_ffn, "g_final": g_final2, "b_in": b_in}
    vec_m = {"b_dw": m_b_dw, "ln_g": m_ln_g, "ln_b": m_ln_b, "s_pool": m_s_pool, "g_ffn": m_g_ffn,
             "g_final": m_g_final.reshape(1, D), "b_in": m_b_in}
    vec_v = {"b_dw": v_b_dw, "ln_g": v_ln_g, "ln_b": v_ln_b, "s_pool": v_s_pool, "g_ffn": v_g_ffn,
             "g_final": v_g_final.reshape(1, D), "b_in": v_b_in}
    small = _adamw_small(layout, sum_small, oth_small, pad_dw(w_dw), pad_dw(m_w_dw), pad_dw(v_w_dw),
                         w_pool3, m_w_pool[0], v_w_pool[0], vec_w, vec_m, vec_v)
    res["w_dw"] = [a[:KW][None] for a in small[0:4]]
    res["w_pool"] = [a[None] for a in small[4:8]]
    for q, k in enumerate(vec_w):
        res[k] = list(small[8 + 4 * q : 12 + 4 * q])
    res["g_final"] = [a.reshape(D) for a in res["g_final"]]
    for k in ("w_in", "w_out", "w_down"):
        res[k] = [a[None] for a in res[k]]
    for k in ("w_gate", "w_up"):
        res[k] = [a.T[None] for a in res[k]]

    order = ["g_mix", "w_in", "b_in", "w_dw", "b_dw", "ln_g", "ln_b", "w_pool", "s_pool", "w_out", "g_ffn", "w_gate", "w_up", "w_down", "g_final"]
    outs = [loss, grad_x[None]]
    for q in range(4):
        outs += [res[k][q] for k in order]
    return tuple(outs)
```

```python
import jax
import jax.numpy as jnp
from jax import lax
from jax.experimental import pallas as pl
from jax.experimental.pallas import tpu as pltpu

F32 = jnp.float32
BF16 = jnp.bfloat16
MESH = pl.DeviceIdType.MESH
ANY = pl.BlockSpec(memory_space=pl.ANY)

RMS_EPS = 1e-6
LN_EPS = 1e-5
POOL_WINDOWS = (2, 4, 8, 16)
ADAM_LR = 0.001
ADAM_B1 = 0.9
ADAM_B2 = 0.999
ADAM_EPS = 1e-08
ADAM_WD = 0.01
ADAM_STEP = 10

LANES = 128
SUBLANES = 8
HALO = 32
CONV_ROWS = 64
VMEM_LIMIT = 56 * 1024 * 1024
PACK_W = 512
N_CHIPS = 4
N_DEV = 8


def _tile(n, want, mult=8):
    t = min(n, want)
    while n % t or t % mult:
        t -= 1
    return t


def _sigmoid(x):
    return 1.0 / (1.0 + jnp.exp(-x))


def _dot(a, b, dims):
    return lax.dot_general(a, b, (dims, ((), ())), preferred_element_type=F32)


NN = ((1,), (0,))
NT = ((1,), (1,))
TN = ((0,), (0,))


def _rms_bwd(x, g, dy):
    r = lax.rsqrt(jnp.mean(x * x, axis=-1, keepdims=True) + RMS_EPS)
    xh = x * r
    gy = dy * g
    dx = r * (gy - xh * jnp.mean(gy * xh, axis=-1, keepdims=True))
    return dx, dy * xh


def _accumulate(ref, first, val):
    @pl.when(first)
    def _():
        ref[...] = val

    @pl.when(jnp.logical_not(first))
    def _():
        ref[...] += val


def _place():
    return lax.axis_index("x"), lax.axis_index("y"), lax.axis_index("c")


def _other_chips(x, y):
    return [(1 - x, y), (x, 1 - y), (1 - x, 1 - y)]


def _rows(ref, start, n):
    return ref.at[pl.ds(pl.multiple_of(start, 16), n)]


def _window(ref, how, k, c=None):
    if how == "all":
        return ref
    if how == "lead":
        return ref.at[k]
    if how == "rows":
        n = ref.shape[0] // N_CHIPS
        if c is None:
            return _rows(ref, k * n, n)
        return _rows(ref, k * n + c * (n // 2), n // 2)
    n = ref.shape[1] // N_CHIPS
    cols = pl.ds(pl.multiple_of(k * n, LANES), n)
    if c is None:
        return ref.at[:, cols]
    h = ref.shape[0] // 2
    return ref.at[pl.ds(pl.multiple_of(c * h, 16), h), cols]


def _remote(src, dst, sems, s, device):
    return pltpu.make_async_remote_copy(
        src_ref=src, dst_ref=dst, send_sem=sems.at[s], recv_sem=sems.at[s + 1], device_id=device, device_id_type=MESH)


class _GatherIci:
    aliased = True

    def __init__(self, fulls, hows, splits):
        self.fulls, self.hows, self.splits = list(fulls), list(hows), list(splits)

    def inputs(self):
        return self.fulls

    def out_shapes(self):
        return [jax.ShapeDtypeStruct(a.shape, a.dtype) for a in self.fulls]

    def n_sems(self):
        return 6 * len(self.fulls)

    def build(self, ins, outs, sems, base):
        x, y, c = _place()
        me = 2 * x + y
        starts, waits = [], []
        for a, (how, sp) in enumerate(zip(self.hows, self.splits)):
            half = c if sp else None
            mine = _window(outs[a], how, me, half)
            for j, (px, py) in enumerate(_other_chips(x, y)):
                s = base + 6 * a + 2 * j
                cp = _remote(mine, mine, sems, s, (px, py, c))
                landing = _remote(mine, _window(outs[a], how, 2 * px + py, half), sems, s, (px, py, c))
                starts.append(cp.start)
                waits += [landing.wait_recv, cp.wait_send]
        return starts, waits


class _GatherD2d:
    aliased = True

    def __init__(self, fulls, hows):
        self.fulls, self.hows = list(fulls), list(hows)

    def inputs(self):
        return self.fulls

    def out_shapes(self):
        return [jax.ShapeDtypeStruct(a.shape, a.dtype) for a in self.fulls]

    def n_sems(self):
        return 6 * len(self.fulls)

    def build(self, ins, outs, sems, base):
        x, y, c = _place()
        starts, waits = [], []
        for a, how in enumerate(self.hows):
            for j, (px, py) in enumerate(_other_chips(x, y)):
                s = base + 6 * a + 2 * j
                got = _window(outs[a], how, 2 * px + py, c)
                cp = _remote(got, got, sems, s, (x, y, 1 - c))
                landing = _remote(got, _window(outs[a], how, 2 * px + py, 1 - c), sems, s, (x, y, 1 - c))
                starts.append(cp.start)
                waits += [landing.wait_recv, cp.wait_send]
        return starts, waits


def _part_shape(a, how):
    if how == "all":
        return a.shape
    if how == "rows":
        return (a.shape[0] // N_CHIPS, a.shape[1])
    return (a.shape[0], a.shape[1] // N_CHIPS)


class _Scatter:
    aliased = False

    def __init__(self, fulls, hows):
        self.fulls, self.hows = list(fulls), list(hows)

    def inputs(self):
        return self.fulls

    def out_shapes(self):
        return [jax.ShapeDtypeStruct((3,) + _part_shape(a, h), a.dtype) for a, h in zip(self.fulls, self.hows)]

    def n_sems(self):
        return 6 * len(self.fulls)

    def build(self, ins, outs, sems, base):
        x, y, c = _place()
        starts, waits = [], []
        for a, how in enumerate(self.hows):
            for j, (px, py) in enumerate(_other_chips(x, y)):
                cp = _remote(_window(ins[a], how, 2 * px + py), outs[a].at[j], sems, base + 6 * a + 2 * j, (px, py, c))
                starts.append(cp.start)
                waits += [cp.wait_recv, cp.wait_send]
        return starts, waits


class _Swap:
    aliased = False

    def __init__(self, arrays):
        self.arrays = list(arrays)

    def inputs(self):
        return self.arrays

    def out_shapes(self):
        return [jax.ShapeDtypeStruct(a.shape, a.dtype) for a in self.arrays]

    def n_sems(self):
        return 2 * len(self.arrays)

    def build(self, ins, outs, sems, base):
        x, y, c = _place()
        starts, waits = [], []
        for a in range(len(ins)):
            cp = _remote(ins[a], outs[a], sems, base + 2 * a, (x, y, 1 - c))
            starts.append(cp.start)
            waits += [cp.wait_recv, cp.wait_send]
        return starts, waits


def _call(name, body, grid, in_specs, out_specs, out_shape, args, scratch=(), comm=()):
    comm = list(comm)
    n_in, n_out, n_scr = len(args), len(out_shape), len(scratch)
    c_in = [a for op in comm for a in op.inputs()]
    c_out = [s for op in comm for s in op.out_shapes()]
    n_sems = sum(op.n_sems() for op in comm)
    aliases, i_in, i_out = {}, 0, 0
    for op in comm:
        if op.aliased:
            for q in range(len(op.inputs())):
                aliases[n_in + i_in + q] = n_out + i_out + q
        i_in, i_out = i_in + len(op.inputs()), i_out + len(op.out_shapes())

    def wrapped(*refs):
        ins = refs[:n_in]
        cin = refs[n_in : n_in + len(c_in)]
        o0 = n_in + len(c_in)
        outs = refs[o0 : o0 + n_out]
        cout = refs[o0 + n_out : o0 + n_out + len(c_out)]
        s0 = o0 + n_out + len(c_out)
        scr = refs[s0 : s0 + n_scr]

        def copies():
            sems = refs[s0 + n_scr]
            starts, waits = [], []
            i_in = i_out = base = 0
            for op in comm:
                ni, no = len(op.inputs()), len(op.out_shapes())
                s, w = op.build(cin[i_in : i_in + ni], cout[i_out : i_out + no], sems, base)
                starts += s
                waits += w
                i_in, i_out, base = i_in + ni, i_out + no, base + op.n_sems()
            return starts, waits

        def run_starts():
            for start in copies()[0]:
                start()

        def run_waits():
            for wait in copies()[1]:
                wait()

        if comm and grid:
            first = last = True
            for d, n in enumerate(grid):
                first = jnp.logical_and(first, pl.program_id(d) == 0)
                last = jnp.logical_and(last, pl.program_id(d) == n - 1)
            pl.when(first)(run_starts)
        elif comm:
            run_starts()
        if body is not None:
            body(*ins, *outs, *scr)
        if comm and grid:
            pl.when(last)(run_waits)
        elif comm:
            run_waits()

    res = pl.pallas_call(
        wrapped,
        name=name,
        grid=grid,
        in_specs=list(in_specs) + [ANY] * len(c_in),
        out_specs=list(out_specs) + [ANY] * len(c_out),
        out_shape=list(out_shape) + c_out,
        scratch_shapes=list(scratch) + ([pltpu.SemaphoreType.DMA((n_sems,))] if comm else []),
        input_output_aliases=aliases,
        compiler_params=pltpu.CompilerParams(dimension_semantics=("arbitrary",) * len(grid), vmem_limit_bytes=VMEM_LIMIT),
    )(*args, *c_in)
    return tuple(res[:n_out]), tuple(res[n_out:])


def _gather_now(fulls, hows, splits):
    n = len(fulls)
    ici = _GatherIci(fulls, hows, splits)
    split_ids = [a for a in range(n) if splits[a]]
    d2d = _GatherD2d([fulls[a] for a in split_ids], [hows[a] for a in split_ids])

    def body(*refs):
        outs, sems = refs[n : 2 * n], refs[2 * n]
        for op, op_refs, base in ((ici, outs, 0), (d2d, [outs[a] for a in split_ids], ici.n_sems())):
            starts, waits = op.build(None, op_refs, sems, base)
            for start in starts:
                start()
            for wait in waits:
                wait()

    return pl.pallas_call(
        body,
        name="gather_first",
        in_specs=[ANY] * n,
        out_specs=[ANY] * n,
        out_shape=ici.out_shapes(),
        scratch_shapes=[pltpu.SemaphoreType.DMA((ici.n_sems() + d2d.n_sems(),))],
        input_output_aliases={a: a for a in range(n)},
    )(*fulls)


def _place_weights(shards, hows, full_shapes, dtypes):
    n = len(shards)

    def body(*refs):
        ins, outs, bufs, sems = refs[:n], refs[n : 2 * n], refs[2 * n : 3 * n], refs[3 * n]
        x, y, _ = _place()
        copies = []
        for a in range(n):
            bufs[a][...] = ins[a][...].astype(dtypes[a])
            cp = pltpu.make_async_copy(bufs[a], _window(outs[a], hows[a], 2 * x + y), sems.at[a])
            cp.start()
            copies.append(cp)
        for cp in copies:
            cp.wait()

    return pl.pallas_call(
        body,
        name="place_weights",
        in_specs=[pl.BlockSpec(memory_space=pltpu.VMEM)] * n,
        out_specs=[ANY] * n,
        out_shape=[jax.ShapeDtypeStruct(s, d) for s, d in zip(full_shapes, dtypes)],
        scratch_shapes=[pltpu.VMEM(a.shape, d) for a, d in zip(shards, dtypes)] + [pltpu.SemaphoreType.DMA((n,))],
        compiler_params=pltpu.CompilerParams(vmem_limit_bytes=VMEM_LIMIT),
    )(*shards)


def _in_proj(x, g_mix, w_in_b, b_in, comm=()):
    T, D = x.shape
    CI = w_in_b.shape[1]
    tm = _tile(T, 512)

    def body(x_ref, g_ref, w_ref, b_ref, z_ref, xn_ref):
        xv = x_ref[...]
        r = lax.rsqrt(jnp.mean(xv * xv, axis=-1, keepdims=True) + RMS_EPS)
        xn = (xv * r * g_ref[...]).astype(BF16)
        xn_ref[...] = xn
        z_ref[...] = _dot(xn, w_ref[...], NN) + b_ref[...]

    return _call(
        "in_proj",
        body,
        (T // tm,),
        [
            pl.BlockSpec((tm, D), lambda i: (i, 0)),
            pl.BlockSpec((1, D), lambda i: (0, 0)),
            pl.BlockSpec((D, CI), lambda i: (0, 0)),
            pl.BlockSpec((1, CI), lambda i: (0, 0)),
        ],
        [pl.BlockSpec((tm, CI), lambda i: (i, 0)), pl.BlockSpec((tm, D), lambda i: (i, 0))],
        [jax.ShapeDtypeStruct((T, CI), F32), jax.ShapeDtypeStruct((T, D), BF16)],
        (x, g_mix, w_in_b, b_in),
        comm=comm,
    )


def _fill_shifted(scr):
    n = scr.shape[1] - SUBLANES
    for s in range(1, SUBLANES):
        scr[s, 0:n, :] = scr[0, s : s + n, :]


def _shifted_rows(scr, off, n, cs):
    s = off % SUBLANES
    return scr[s, off - s : off - s + n, cs]


def _pool_mean_minus_token(p_scr, cs, w, cnt, tt):
    tok = p_scr[HALO : HALO + tt, cs]
    s = tok
    for d in range(1, w):
        s = s + p_scr[HALO - d : HALO - d + tt, cs]
    return s / cnt - tok


def _seq_fwd(z, w_dw4, b_dw, ln_g, ln_b, w_pool_b, s_pool, comm=()):
    T, CI = z.shape
    CC = ln_g.shape[1]
    n_grp, G = w_pool_b.shape[0], w_pool_b.shape[-1]
    KW = w_dw4.shape[1]
    D = CC + n_grp * G
    tt = _tile(T, 256, HALO)
    per = tt // HALO

    def body(zc_ref, zp_ref, wdw_ref, bdw_ref, lng_ref, lnb_ref, wp_ref, sp_ref, y_ref, v_ref, u_scr, p_scr):
        i = pl.program_id(0)
        first = i == 0
        u_prev = zp_ref[:, 0:CC] * _sigmoid(zp_ref[:, CC : 2 * CC])
        u_scr[0, 0:HALO, :] = jnp.where(first, 0.0, u_prev)
        p_scr[0:HALO, :] = jnp.where(first, 0.0, zp_ref[:, 2 * CC :])
        u_scr[0, HALO:, :] = zc_ref[:, 0:CC] * _sigmoid(zc_ref[:, CC : 2 * CC])
        p_scr[HALO:, :] = zc_ref[:, 2 * CC :]
        _fill_shifted(u_scr)

        for j in range(CC // LANES):
            cs = slice(LANES * j, LANES * (j + 1))
            for rb in range(tt // CONV_ROWS):
                acc = jnp.zeros((CONV_ROWS, LANES), F32)
                for k in range(KW):
                    off = HALO - (KW - 1) + k + rb * CONV_ROWS
                    acc = acc + _shifted_rows(u_scr, off, CONV_ROWS, cs) * wdw_ref[j, k : k + 1, :]
                v_ref[rb * CONV_ROWS : (rb + 1) * CONV_ROWS, cs] = acc + bdw_ref[:, cs]

        v = v_ref[...]
        mu = jnp.mean(v, axis=-1, keepdims=True)
        d = v - mu
        var = jnp.mean(d * d, axis=-1, keepdims=True)
        ln = d * lax.rsqrt(var + LN_EPS) * lng_ref[...] + lnb_ref[...]
        y_ref[:, 0:CC] = (ln * _sigmoid(ln)).astype(BF16)

        tpos = i * tt + lax.broadcasted_iota(jnp.int32, (tt, 1), 0)
        for gi, w in enumerate(POOL_WINDOWS):
            cs = slice(G * gi, G * (gi + 1))
            cnt = jnp.minimum(tpos + 1, w).astype(F32)
            yi = _pool_mean_minus_token(p_scr, cs, w, cnt, tt)
            q = _dot(yi.astype(BF16), wp_ref[gi], NN)
            y_ref[:, CC + G * gi : CC + G * (gi + 1)] = (q * sp_ref[:, cs]).astype(BF16)

    const2 = lambda i: (0, 0)
    return _call(
        "seq_fwd",
        body,
        (T // tt,),
        [
            pl.BlockSpec((tt, CI), lambda i: (i, 0)),
            pl.BlockSpec((HALO, CI), lambda i: (jnp.maximum(i * per - 1, 0), 0)),
            pl.BlockSpec(w_dw4.shape, lambda i: (0, 0, 0)),
            pl.BlockSpec((1, CC), const2),
            pl.BlockSpec((1, CC), const2),
            pl.BlockSpec((1, CC), const2),
            pl.BlockSpec(w_pool_b.shape, lambda i: (0, 0, 0)),
            pl.BlockSpec((1, n_grp * G), const2),
        ],
        [pl.BlockSpec((tt, D), lambda i: (i, 0)), pl.BlockSpec((tt, CC), lambda i: (i, 0))],
        [jax.ShapeDtypeStruct((T, D), BF16), jax.ShapeDtypeStruct((T, CC), F32)],
        (z, z, w_dw4, b_dw, ln_g, ln_b, w_pool_b, s_pool),
        scratch=[pltpu.VMEM((SUBLANES, HALO + tt, CC), F32), pltpu.VMEM((HALO + tt, n_grp * G), F32)],
        comm=comm,
    )


def _out_proj(y_b, x, w_out_b, g_ffn, comm=()):
    T, D = x.shape
    tm = _tile(T, 512)

    def body(y_ref, x_ref, w_ref, g_ref, h1_ref, hn_ref):
        h1 = x_ref[...] + _dot(y_ref[...], w_ref[...], NN)
        h1_ref[...] = h1
        r = lax.rsqrt(jnp.mean(h1 * h1, axis=-1, keepdims=True) + RMS_EPS)
        hn_ref[...] = (h1 * r * g_ref[...]).astype(BF16)

    row = lambda i: (i, 0)
    return _call(
        "out_proj",
        body,
        (T // tm,),
        [
            pl.BlockSpec((tm, y_b.shape[1]), row),
            pl.BlockSpec((tm, D), row),
            pl.BlockSpec(w_out_b.shape, lambda i: (0, 0)),
            pl.BlockSpec((1, D), lambda i: (0, 0)),
        ],
        [pl.BlockSpec((tm, D), row), pl.BlockSpec((tm, D), row)],
        [jax.ShapeDtypeStruct((T, D), F32), jax.ShapeDtypeStruct((T, D), BF16)],
        (y_b, x, w_out_b, g_ffn),
        comm=comm,
    )


def _hidden_tile(F):
    return _tile(F, 1408, LANES)


def _gate_up(hn_b, wgT_b, wuT_b, comm=()):
    T, D = hn_b.shape
    F = wgT_b.shape[0]
    tm, tf = _tile(T, 512), _hidden_tile(F)

    def body(hn_ref, wg_ref, wu_ref, g_ref, u_ref, a_ref):
        hn = hn_ref[...]
        gv = _dot(hn, wg_ref[...], NT)
        uv = _dot(hn, wu_ref[...], NT)
        g_ref[...] = gv.astype(BF16)
        u_ref[...] = uv.astype(BF16)
        a_ref[...] = (gv * _sigmoid(gv) * uv).astype(BF16)

    wspec = pl.BlockSpec((tf, D), lambda j, i: (j, 0))
    ospec = pl.BlockSpec((tm, tf), lambda j, i: (i, j))
    return _call(
        "gate_up",
        body,
        (F // tf, T // tm),
        [pl.BlockSpec((tm, D), lambda j, i: (i, 0)), wspec, wspec],
        [ospec, ospec, ospec],
        [jax.ShapeDtypeStruct((T, F), BF16)] * 3,
        (hn_b, wgT_b, wuT_b),
        comm=comm,
    )


def _down_loss(a_b, wd_b, h1, target, g_final, comm=()):
    T, D = h1.shape
    F = a_b.shape[1]
    tm = _tile(T, 256)
    nt = T // tm

    def body(a_ref, w_ref, h1_ref, t_ref, g_ref, dh2_ref, dh2b_ref, loss_ref, dg_ref):
        i = pl.program_id(0)
        h2 = h1_ref[...] + _dot(a_ref[...], w_ref[...], NN)
        r = lax.rsqrt(jnp.mean(h2 * h2, axis=-1, keepdims=True) + RMS_EPS)
        g = g_ref[...]
        diff = h2 * r * g - t_ref[...]
        _accumulate(loss_ref, i == 0, jnp.full(loss_ref.shape, jnp.sum(diff * diff) * (0.5 / D), F32))
        dh2, dg_rows = _rms_bwd(h2, g, diff * (1.0 / D))
        dh2_ref[...] = dh2
        dh2b_ref[...] = dh2.astype(BF16)
        _accumulate(dg_ref, i == 0, jnp.sum(dg_rows, axis=0, keepdims=True))

    row = lambda i: (i, 0)
    return _call(
        "down_loss",
        body,
        (nt,),
        [
            pl.BlockSpec((tm, F), row),
            pl.BlockSpec((F, D), lambda i: (0, 0)),
            pl.BlockSpec((tm, D), row),
            pl.BlockSpec((tm, D), row),
            pl.BlockSpec((1, D), lambda i: (0, 0)),
        ],
        [
            pl.BlockSpec((tm, D), row),
            pl.BlockSpec((tm, D), row),
            pl.BlockSpec((1, LANES), lambda i: (0, 0)),
            pl.BlockSpec((1, D), lambda i: (0, 0)),
        ],
        [
            jax.ShapeDtypeStruct((T, D), F32),
            jax.ShapeDtypeStruct((T, D), BF16),
            jax.ShapeDtypeStruct((1, LANES), F32),
            jax.ShapeDtypeStruct((1, D), F32),
        ],
        (a_b, wd_b, h1, target, g_final),
        comm=comm,
    )


def _ffn_bwd_act(dh2_b, wd_b, g_b, u_b, comm=()):
    T, D = dh2_b.shape
    F = wd_b.shape[0]
    tm, tf = _tile(T, 512), _hidden_tile(F)

    def body(d_ref, w_ref, g_ref, u_ref, dg_ref, du_ref):
        da = _dot(d_ref[...], w_ref[...], NT)
        gv = g_ref[...].astype(F32)
        uv = u_ref[...].astype(F32)
        sg = _sigmoid(gv)
        silu = gv * sg
        dg_ref[...] = (da * uv * (sg * (1.0 + gv * (1.0 - sg)))).astype(BF16)
        du_ref[...] = (da * silu).astype(BF16)

    aspec = pl.BlockSpec((tm, tf), lambda j, i: (i, j))
    return _call(
        "ffn_bwd_act",
        body,
        (F // tf, T // tm),
        [pl.BlockSpec((tm, D), lambda j, i: (i, 0)), pl.BlockSpec((tf, D), lambda j, i: (j, 0)), aspec, aspec],
        [aspec, aspec],
        [jax.ShapeDtypeStruct((T, F), BF16)] * 2,
        (dh2_b, wd_b, g_b, u_b),
        comm=comm,
    )


def _ffn_bwd_in(dg_b, du_b, wgT_b, wuT_b, h1, dh2, g_ffn, w_out_b, comm=()):
    T, D = h1.shape
    F = wgT_b.shape[0]
    DM = w_out_b.shape[0]
    tm = _tile(T, 256)

    def body(dg_ref, du_ref, wg_ref, wu_ref, h1_ref, dh2_ref, g_ref, wo_ref, dh1_ref, dh1b_ref, dy_ref, dgf_ref):
        i = pl.program_id(0)
        dhn = _dot(dg_ref[...], wg_ref[...], NN) + _dot(du_ref[...], wu_ref[...], NN)
        dx, dg_rows = _rms_bwd(h1_ref[...], g_ref[...], dhn)
        dh1 = dh2_ref[...] + dx
        dh1b = dh1.astype(BF16)
        dh1_ref[...] = dh1
        dh1b_ref[...] = dh1b
        dy_ref[...] = _dot(dh1b, wo_ref[...], NT)
        _accumulate(dgf_ref, i == 0, jnp.sum(dg_rows, axis=0, keepdims=True))

    row = lambda i: (i, 0)
    const = lambda i: (0, 0)
    return _call(
        "ffn_bwd_in",
        body,
        (T // tm,),
        [
            pl.BlockSpec((tm, F), row),
            pl.BlockSpec((tm, F), row),
            pl.BlockSpec((F, D), const),
            pl.BlockSpec((F, D), const),
            pl.BlockSpec((tm, D), row),
            pl.BlockSpec((tm, D), row),
            pl.BlockSpec((1, D), const),
            pl.BlockSpec((DM, D), const),
        ],
        [pl.BlockSpec((tm, D), row), pl.BlockSpec((tm, D), row), pl.BlockSpec((tm, DM), row), pl.BlockSpec((1, D), const)],
        [
            jax.ShapeDtypeStruct((T, D), F32),
            jax.ShapeDtypeStruct((T, D), BF16),
            jax.ShapeDtypeStruct((T, DM), F32),
            jax.ShapeDtypeStruct((1, D), F32),
        ],
        (dg_b, du_b, wgT_b, wuT_b, h1, dh2, g_ffn, w_out_b),
        comm=comm,
    )


def _seq_bwd(z, dy, v, w_dw4, ln_g, ln_b, w_pool_b, s_pool, comm=()):
    T, CI = z.shape
    CC = ln_g.shape[1]
    n_grp, G = w_pool_b.shape[0], w_pool_b.shape[-1]
    CP = n_grp * G
    KW = w_dw4.shape[1]
    n_cc = CC // LANES
    D = CC + CP
    tt = _tile(T, 256, HALO)
    per = tt // HALO
    n_tiles = T // tt
    last_halo = T // HALO - 1

    def body(zc_ref, zp_ref, dyc_ref, dyn_ref, vc_ref, vn_ref, wdw_ref, lng_ref, lnb_ref, wp_ref, sp_ref,
             dz_ref, dwdw_ref, dbdw_ref, dlng_ref, dlnb_ref, dwp_ref, dsp_ref, dbin_ref,
             dv_scr, u_scr, p_scr, g_scr, dw_scr):
        i = pl.program_id(0)
        first = i == 0
        last = i == n_tiles - 1
        lng, lnb = lng_ref[...], lnb_ref[...]

        def conv_pre(vv, dyc):
            mu = jnp.mean(vv, axis=-1, keepdims=True)
            d = vv - mu
            rs = lax.rsqrt(jnp.mean(d * d, axis=-1, keepdims=True) + LN_EPS)
            xh = d * rs
            ln = xh * lng + lnb
            sg = _sigmoid(ln)
            dln = dyc * (sg * (1.0 + ln * (1.0 - sg)))
            dxh = dln * lng
            dv = rs * (dxh - jnp.mean(dxh, axis=-1, keepdims=True) - xh * jnp.mean(dxh * xh, axis=-1, keepdims=True))
            return dv, dln, xh

        dv_c, dln_c, xh_c = conv_pre(vc_ref[...], dyc_ref[:, 0:CC])
        dv_scr[0, 0:tt, :] = dv_c
        dv_n, _, _ = conv_pre(vn_ref[...], dyn_ref[:, 0:CC])
        dv_scr[0, tt:, :] = jnp.where(last, 0.0, dv_n)
        _fill_shifted(dv_scr)
        _accumulate(dlng_ref, first, jnp.sum(dln_c * xh_c, axis=0, keepdims=True))
        _accumulate(dlnb_ref, first, jnp.sum(dln_c, axis=0, keepdims=True))
        _accumulate(dbdw_ref, first, jnp.sum(dv_c, axis=0, keepdims=True))

        u_scr[...] = zc_ref[:, 0:CC] * _sigmoid(zc_ref[:, CC : 2 * CC])

        @pl.when(first)
        def _():
            dw_scr[...] = jnp.zeros_like(dw_scr)

        for j in range(n_cc):
            cs = slice(LANES * j, LANES * (j + 1))
            gs = slice(CC + LANES * j, CC + LANES * (j + 1))
            dbin_a = jnp.zeros((1, LANES), F32)
            dbin_g = jnp.zeros((1, LANES), F32)
            for rb in range(tt // CONV_ROWS):
                rows = slice(rb * CONV_ROWS, (rb + 1) * CONV_ROWS)
                u_blk = u_scr[rows, cs]
                du = jnp.zeros((CONV_ROWS, LANES), F32)
                for k in range(KW):
                    off = rb * CONV_ROWS + (KW - 1) - k
                    d = _shifted_rows(dv_scr, off, CONV_ROWS, cs)
                    du = du + d * wdw_ref[j, k : k + 1, :]
                    dw_scr[j * HALO + k] += jnp.sum((u_blk * d).reshape(CONV_ROWS // 8, 8, LANES), axis=0)
                a = zc_ref[rows, cs]
                sg = _sigmoid(zc_ref[rows, gs])
                da = du * sg
                dgate = du * a * sg * (1.0 - sg)
                dz_ref[rows, cs] = da.astype(BF16)
                dz_ref[rows, gs] = dgate.astype(BF16)
                dbin_a = dbin_a + jnp.sum(da, axis=0, keepdims=True)
                dbin_g = dbin_g + jnp.sum(dgate, axis=0, keepdims=True)
            _accumulate(dbin_ref.at[:, cs], first, dbin_a)
            _accumulate(dbin_ref.at[:, gs], first, dbin_g)

        @pl.when(last)
        def _():
            dwdw_ref[...] = jnp.sum(dw_scr[...], axis=1).reshape(dwdw_ref.shape)

        p_scr[0:HALO, :] = jnp.where(first, 0.0, zp_ref[:, 2 * CC :])
        p_scr[HALO:, :] = zc_ref[:, 2 * CC :]
        tpos = i * tt + lax.broadcasted_iota(jnp.int32, (tt, 1), 0)
        for gi, w in enumerate(POOL_WINDOWS):
            cs = slice(G * gi, G * (gi + 1))
            ys = slice(CC + G * gi, CC + G * (gi + 1))
            ps = slice(2 * CC + G * gi, 2 * CC + G * (gi + 1))
            cnt = jnp.minimum(tpos + 1, w).astype(F32)
            yib = _pool_mean_minus_token(p_scr, cs, w, cnt, tt).astype(BF16)
            wp = wp_ref[gi]
            sp = sp_ref[:, cs]
            dyp = dyc_ref[:, ys]
            q = _dot(yib, wp, NN)
            _accumulate(dsp_ref.at[:, cs], first, jnp.sum(dyp * q, axis=0, keepdims=True))
            dq_c = (dyp * sp).astype(BF16)
            dq_n = (jnp.where(last, 0.0, dyn_ref[:, ys]) * sp).astype(BF16)
            _accumulate(dwp_ref.at[gi], first, _dot(yib, dq_c, TN))
            dyi_c = _dot(dq_c, wp, NT)
            g_scr[0:tt, cs] = dyi_c / cnt
            g_scr[tt:, cs] = _dot(dq_n, wp, NT) * (1.0 / w)
            dp = -dyi_c
            for d in range(w):
                dp = dp + g_scr[d : d + tt, cs]
            dz_ref[:, ps] = dp.astype(BF16)
            _accumulate(dbin_ref.at[:, ps], first, jnp.sum(dp, axis=0, keepdims=True))

    cur = lambda i: (i, 0)
    prev = lambda i: (jnp.maximum(i * per - 1, 0), 0)
    nxt = lambda i: (jnp.minimum((i + 1) * per, last_halo), 0)
    c2 = lambda i: (0, 0)
    c3 = lambda i: (0, 0, 0)
    return _call(
        "seq_bwd",
        body,
        (n_tiles,),
        [
            pl.BlockSpec((tt, CI), cur),
            pl.BlockSpec((HALO, CI), prev),
            pl.BlockSpec((tt, D), cur),
            pl.BlockSpec((HALO, D), nxt),
            pl.BlockSpec((tt, CC), cur),
            pl.BlockSpec((HALO, CC), nxt),
            pl.BlockSpec(w_dw4.shape, c3),
            pl.BlockSpec((1, CC), c2),
            pl.BlockSpec((1, CC), c2),
            pl.BlockSpec(w_pool_b.shape, c3),
            pl.BlockSpec((1, CP), c2),
        ],
        [
            pl.BlockSpec((tt, CI), cur),
            pl.BlockSpec((n_cc, HALO, LANES), c3),
            pl.BlockSpec((1, CC), c2),
            pl.BlockSpec((1, CC), c2),
            pl.BlockSpec((1, CC), c2),
            pl.BlockSpec((n_grp, G, G), c3),
            pl.BlockSpec((1, CP), c2),
            pl.BlockSpec((1, CI), c2),
        ],
        [
            jax.ShapeDtypeStruct((T, CI), BF16),
            jax.ShapeDtypeStruct((n_cc, HALO, LANES), F32),
            jax.ShapeDtypeStruct((1, CC), F32),
            jax.ShapeDtypeStruct((1, CC), F32),
            jax.ShapeDtypeStruct((1, CC), F32),
            jax.ShapeDtypeStruct((n_grp, G, G), F32),
            jax.ShapeDtypeStruct((1, CP), F32),
            jax.ShapeDtypeStruct((1, CI), F32),
        ],
        (z, z, dy, dy, v, v, w_dw4, ln_g, ln_b, w_pool_b, s_pool),
        scratch=[
            pltpu.VMEM((SUBLANES, tt + HALO, CC), F32),
            pltpu.VMEM((tt, CC), F32),
            pltpu.VMEM((HALO + tt, CP), F32),
            pltpu.VMEM((tt + HALO, CP), F32),
            pltpu.VMEM((n_cc * HALO, 8, LANES), F32),
        ],
        comm=comm,
    )


def _in_proj_bwd(dz_b, w_in_b, x, dh1, g_mix, comm=()):
    T, D = x.shape
    CI = w_in_b.shape[1]
    tm = _tile(T, 512)

    def body(dz_ref, w_ref, x_ref, dh1_ref, g_ref, dx_ref, dg_ref):
        i = pl.program_id(0)
        dxn = _dot(dz_ref[...], w_ref[...], NT)
        dx, dg_rows = _rms_bwd(x_ref[...], g_ref[...], dxn)
        dx_ref[...] = dh1_ref[...] + dx
        _accumulate(dg_ref, i == 0, jnp.sum(dg_rows, axis=0, keepdims=True))

    row = lambda i: (i, 0)
    const = lambda i: (0, 0)
    return _call(
        "in_proj_bwd",
        body,
        (T // tm,),
        [
            pl.BlockSpec((tm, CI), row),
            pl.BlockSpec((D, CI), const),
            pl.BlockSpec((tm, D), row),
            pl.BlockSpec((tm, D), row),
            pl.BlockSpec((1, D), const),
        ],
        [pl.BlockSpec((tm, D), row), pl.BlockSpec((1, D), const)],
        [jax.ShapeDtypeStruct((T, D), F32), jax.ShapeDtypeStruct((1, D), F32)],
        (dz_b, w_in_b, x, dh1, g_mix),
        comm=comm,
    )


def _weight_grad(name, a_b, b_b, comm=()):
    T, N1 = a_b.shape
    N2 = b_b.shape[1]
    t1 = _tile(N1, 1408, LANES)
    tk = _tile(T, 1024)
    nk = T // tk

    def body(a_ref, b_ref, o_ref, acc):
        k = pl.program_id(1)
        _accumulate(acc, k == 0, _dot(a_ref[...], b_ref[...], TN))

        @pl.when(k == nk - 1)
        def _():
            o_ref[...] = acc[...].astype(BF16)

    (out,), rest = _call(
        name,
        body,
        (N1 // t1, nk),
        [pl.BlockSpec((tk, t1), lambda n, k: (k, n)), pl.BlockSpec((tk, N2), lambda n, k: (k, 0))],
        [pl.BlockSpec((t1, N2), lambda n, k: (n, 0))],
        [jax.ShapeDtypeStruct((N1, N2), BF16)],
        (a_b, b_b),
        scratch=[pltpu.VMEM((t1, N2), F32)],
        comm=comm,
    )
    return out, rest


def _sum_parts(name, full, how, parts, me):
    _, R, C = parts.shape
    tr = _tile(R, 512)
    nb = R // tr

    def body(me_ref, own_ref, p_ref, o_ref):
        f = lambda q: p_ref[q].astype(F32)
        o_ref[...] = (own_ref[...].astype(F32) + f(0)) + (f(1) + f(2))

    own_map = {"rows": lambda i, me_ref: (me_ref[0] * nb + i, 0), "cols": lambda i, me_ref: (i, me_ref[0]),
               "all": lambda i, me_ref: (i, 0)}[how]
    return pl.pallas_call(
        body,
        name=name,
        grid_spec=pltpu.PrefetchScalarGridSpec(
            num_scalar_prefetch=1,
            grid=(nb,),
            in_specs=[pl.BlockSpec((tr, C), own_map), pl.BlockSpec((3, tr, C), lambda i, me_ref: (0, i, 0))],
            out_specs=pl.BlockSpec((tr, C), lambda i, me_ref: (i, 0)),
        ),
        out_shape=jax.ShapeDtypeStruct((R, C), F32),
        compiler_params=pltpu.CompilerParams(dimension_semantics=("arbitrary",), vmem_limit_bytes=VMEM_LIMIT),
    )(me, full, parts)


_M_CORR = 1.0 - ADAM_B1**ADAM_STEP
_V_CORR = 1.0 - ADAM_B2**ADAM_STEP


def _adamw_math(w, g, m, v):
    m = ADAM_B1 * m + (1.0 - ADAM_B1) * g
    v = ADAM_B2 * v + (1.0 - ADAM_B2) * (g * g)
    delta = -ADAM_LR * ((m / _M_CORR) / (jnp.sqrt(v / _V_CORR) + ADAM_EPS) + ADAM_WD * w)
    return delta, m, v


def _adamw(name, w, m, v, g_here, g_there, comm=()):
    R, C = w.shape
    tr = _tile(R, 256)

    def body(w_ref, m_ref, v_ref, ga_ref, gb_ref, g_ref, d_ref, nm_ref, nv_ref):
        g = ga_ref[...] + gb_ref[...]
        g_ref[...] = g
        d_ref[...], nm_ref[...], nv_ref[...] = _adamw_math(w_ref[...], g, m_ref[...], v_ref[...])

    spec = pl.BlockSpec((tr, C), lambda i: (i, 0))
    return _call(name, body, (R // tr,), [spec] * 5, [spec] * 4, [jax.ShapeDtypeStruct((R, C), F32)] * 4,
                 (w, m, v, g_here, g_there), comm=comm)


class _PackLayout:
    def __init__(self, n_cc, n_grp, G, widths):
        self.dw_rows = (0, HALO)
        self.wp_rows = (HALO, HALO + G)
        self.n_cc, self.n_grp, self.G = n_cc, n_grp, G
        self.vec = {}
        r = HALO + G
        for name, width in widths:
            self.vec[name] = (r, width)
            r += width // PACK_W
        self.rows = -(-r // 8) * 8


def _pack_small(layout, dwdw, dwp, vecs):
    names = list(vecs)

    def body(*refs):
        dw_ref, wp_ref = refs[0], refs[1]
        vec_refs = refs[2 : 2 + len(names)]
        o_ref = refs[-1]
        o_ref[...] = jnp.zeros_like(o_ref)
        for j in range(layout.n_cc):
            o_ref[layout.dw_rows[0] : layout.dw_rows[1], j * LANES : (j + 1) * LANES] = dw_ref[j]
        for i in range(layout.n_grp):
            o_ref[layout.wp_rows[0] : layout.wp_rows[1], i * layout.G : (i + 1) * layout.G] = wp_ref[i]
        for name, ref in zip(names, vec_refs):
            r, width = layout.vec[name]
            for h in range(width // PACK_W):
                o_ref[r + h : r + h + 1, :] = ref[:, h * PACK_W : (h + 1) * PACK_W]

    return pl.pallas_call(
        body,
        name="pack_small",
        out_shape=jax.ShapeDtypeStruct((layout.rows, PACK_W), F32),
    )(dwdw, dwp, *[vecs[k] for k in names])


def _adamw_small(layout, g_here, g_there, w_dw, m_dw, v_dw, w_pool, m_pool, v_pool, vec_w, vec_m, vec_v):
    names = list(vec_w)
    nv = len(names)

    def body(*refs):
        ga_ref, gb_ref = refs[0], refs[1]
        wdw, mdw, vdw, wp, mp, vp = refs[2:8]
        vw, vm, vv = refs[8 : 8 + nv], refs[8 + nv : 8 + 2 * nv], refs[8 + 2 * nv : 8 + 3 * nv]
        outs = refs[8 + 3 * nv :]
        acc = outs[-1]
        acc[...] = ga_ref[...] + gb_ref[...]

        def emit(o, g, w, m, v, idx=()):
            res = (g,) + _adamw_math(w, g, m, v)
            for ref, val in zip(o, res):
                ref[idx] = val

        me = 2 * lax.axis_index("x") + lax.axis_index("y")
        for j in range(layout.n_cc):

            @pl.when(me == j)
            def _(j=j):
                g = acc[layout.dw_rows[0] : layout.dw_rows[1], j * LANES : (j + 1) * LANES]
                emit(outs[0:4], g, wdw[...], mdw[...], vdw[...], idx=...)

        for i in range(layout.n_grp):
            g = acc[layout.wp_rows[0] : layout.wp_rows[1], i * layout.G : (i + 1) * layout.G]
            emit(outs[4:8], g, wp[i], mp[i], vp[i], idx=i)
        for q, name in enumerate(names):
            r, width = layout.vec[name]
            for h in range(width // PACK_W):
                ls = slice(h * PACK_W, (h + 1) * PACK_W)
                g = acc[r + h : r + h + 1, :]
                emit(outs[8 + 4 * q : 12 + 4 * q], g, vw[q][:, ls], vm[q][:, ls], vv[q][:, ls], idx=(slice(None), ls))

    shapes = [w_dw.shape] * 4 + [w_pool.shape] * 4
    for name in names:
        shapes += [vec_w[name].shape] * 4
    return pl.pallas_call(
        body,
        name="adamw_small",
        out_shape=[jax.ShapeDtypeStruct(s, F32) for s in shapes],
        scratch_shapes=[pltpu.VMEM(g_here.shape, F32)],
    )(g_here, g_there, w_dw, m_dw, v_dw, w_pool, m_pool, v_pool,
      *[vec_w[k] for k in names], *[vec_m[k] for k in names], *[vec_v[k] for k in names])


def _allreduce_adamw_row(g_part, w, m, v, loss_part, comm=()):
    D = w.shape[1]
    n_pairs = N_DEV - 1

    def body(g_ref, w_ref, m_ref, v_ref, l_ref, go_ref, d_ref, nm_ref, nv_ref, lo_ref, land_g, land_l, sems):
        x, y, c = _place()
        copies = []
        for q, (src, land) in enumerate(((g_ref, land_g), (l_ref, land_l))):
            for r in range(1, N_DEV):
                fx, fy, fc = (r >> 2) & 1, (r >> 1) & 1, r & 1
                peer = (1 - x if fx else x, 1 - y if fy else y, 1 - c if fc else c)
                cp = _remote(src, land.at[r], sems, 2 * (q * n_pairs + r - 1), peer)
                cp.start()
                copies.append(cp)
        for cp in copies:
            cp.wait()

        def total(src, land):
            row = lambda r: src[...] if r == 0 else land[r]
            return ((row(0) + row(4)) + (row(2) + row(6))) + ((row(1) + row(5)) + (row(3) + row(7)))

        g = total(g_ref, land_g)
        go_ref[...] = g
        d_ref[...], nm_ref[...], nv_ref[...] = _adamw_math(w_ref[...], g, m_ref[...], v_ref[...])
        lo_ref[...] = total(l_ref, land_l)

    vm = pl.BlockSpec(memory_space=pltpu.VMEM)
    return _call(
        "allreduce_adamw_g_mix",
        body,
        (),
        [vm] * 5,
        [vm] * 5,
        [jax.ShapeDtypeStruct((1, D), F32)] * 4 + [jax.ShapeDtypeStruct(loss_part.shape, F32)],
        (g_part, w, m, v, loss_part),
        scratch=[pltpu.VMEM((N_DEV, 1, D), F32), pltpu.VMEM((N_DEV,) + loss_part.shape, F32),
                 pltpu.SemaphoreType.DMA((4 * n_pairs,))],
        comm=comm,
    )


def kernel(x, g_mix, w_in, b_in, w_dw, b_dw, ln_g, ln_b, w_pool, s_pool, w_out, g_ffn, w_gate, w_up, w_down, g_final, loss_target, m_g_mix, m_w_in, m_b_in, m_w_dw, m_b_dw, m_ln_g, m_ln_b, m_w_pool, m_s_pool, m_w_out, m_g_ffn, m_w_gate, m_w_up, m_w_down, m_g_final, v_g_mix, v_w_in, v_b_in, v_w_dw, v_b_dw, v_ln_g, v_ln_b, v_w_pool, v_s_pool, v_w_out, v_g_ffn, v_w_gate, v_w_up, v_w_down, v_g_final):
    x2 = x[0]
    target = loss_target[0]
    T, D = x2.shape
    w_in2, w_out2, w_down2, w_dw2 = w_in[0], w_out[0], w_down[0], w_dw[0]
    w_gateT, w_upT = w_gate[0].T, w_up[0].T
    CI = w_in2.shape[1] * N_CHIPS
    DM = w_out2.shape[0] * N_CHIPS
    F = w_down2.shape[0] * N_CHIPS
    KW, dw_cols = w_dw2.shape
    assert dw_cols == LANES
    n_grp, G = w_pool.shape[1], w_pool.shape[-1]
    w_pool3 = w_pool[0]
    g_final2 = g_final.reshape(1, D)

    me = (2 * lax.axis_index("x") + lax.axis_index("y")).astype(jnp.int32).reshape(1)

    f_in, f_out, f_gate, f_up, f_down, f_dw = _place_weights(
        [w_in2, w_out2, w_gateT, w_upT, w_down2, w_dw2], ["cols", "rows", "rows", "rows", "rows", "lead"],
        [(D, CI), (DM, D), (F, D), (F, D), (F, D), (N_CHIPS, KW, dw_cols)], [BF16] * 5 + [F32])
    w_in_b, w_dw4 = _gather_now([f_in, f_dw], ["cols", "lead"], [True, False])
    w_pool_b = w_pool3.astype(BF16)
    (z, xn_b), (f_out,) = _in_proj(x2, g_mix, w_in_b, b_in, comm=[_GatherIci([f_out], ["rows"], [True])])
    (y_b, v), (w_out_b, f_gate, f_up) = _seq_fwd(
        z, w_dw4, b_dw, ln_g, ln_b, w_pool_b, s_pool,
        comm=[_GatherD2d([f_out], ["rows"]), _GatherIci([f_gate, f_up], ["rows"] * 2, [True] * 2)])
    (h1, hn_b), (wgT_b, wuT_b, f_down) = _out_proj(
        y_b, x2, w_out_b, g_ffn,
        comm=[_GatherD2d([f_gate, f_up], ["rows"] * 2), _GatherIci([f_down], ["rows"], [True])])
    (g_b, u_b, a_b), (wd_b,) = _gate_up(hn_b, wgT_b, wuT_b, comm=[_GatherD2d([f_down], ["rows"])])
    (dh2, dh2_b, loss_part, d_g_final), _ = _down_loss(a_b, wd_b, h1, target, g_final2)

    gw_down, _ = _weight_grad("grad_w_down", a_b, dh2_b)
    (dg_b, du_b), (p_down,) = _ffn_bwd_act(dh2_b, wd_b, g_b, u_b, comm=[_Scatter([gw_down], ["rows"])])
    gw_gateT, _ = _weight_grad("grad_w_gate", dg_b, hn_b)
    gw_upT, _ = _weight_grad("grad_w_up", du_b, hn_b)
    sum_down = _sum_parts("sum_w_down", gw_down, "rows", p_down, me)
    (dh1, dh1_b, dy, d_g_ffn), (p_gate, oth_down) = _ffn_bwd_in(
        dg_b, du_b, wgT_b, wuT_b, h1, dh2, g_ffn, w_out_b, comm=[_Scatter([gw_gateT], ["rows"]), _Swap([sum_down])])
    gw_out, _ = _weight_grad("grad_w_out", y_b, dh1_b)
    sum_gate = _sum_parts("sum_w_gate", gw_gateT, "rows", p_gate, me)
    res = {}
    res["w_down"], _ = _adamw("adamw_w_down", w_down2, m_w_down[0], v_w_down[0], sum_down, oth_down)
    (dz_b, d_wdw, d_bdw, d_lng, d_lnb, d_wp, d_sp, d_bin), (p_up, p_out, oth_gate) = _seq_bwd(
        z, dy, v, w_dw4, ln_g, ln_b, w_pool_b, s_pool,
        comm=[_Scatter([gw_upT, gw_out], ["rows", "rows"]), _Swap([sum_gate])])
    vec_grads = {"b_dw": d_bdw, "ln_g": d_lng, "ln_b": d_lnb, "s_pool": d_sp, "g_ffn": d_g_ffn, "g_final": d_g_final, "b_in": d_bin}
    layout = _PackLayout(dw_cols * N_CHIPS // LANES, n_grp, G, [(k, a.shape[1]) for k, a in vec_grads.items()])
    pack = _pack_small(layout, d_wdw, d_wp, vec_grads)
    sum_up = _sum_parts("sum_w_up", gw_upT, "rows", p_up, me)
    sum_out = _sum_parts("sum_w_out", gw_out, "rows", p_out, me)
    gw_in, (p_small, oth_up, oth_out) = _weight_grad(
        "grad_w_in", xn_b, dz_b, comm=[_Scatter([pack], ["all"]), _Swap([sum_up, sum_out])])
    res["w_gate"], _ = _adamw("adamw_w_gate", w_gateT, m_w_gate[0].T, v_w_gate[0].T, sum_gate, oth_gate)
    sum_small = _sum_parts("sum_small", pack, "all", p_small, me)
    (grad_x, d_g_mix), (p_in, oth_small) = _in_proj_bwd(
        dz_b, w_in_b, x2, dh1, g_mix, comm=[_Scatter([gw_in], ["cols"]), _Swap([sum_small])])
    sum_in = _sum_parts("sum_w_in", gw_in, "cols", p_in, me)
    (*res["g_mix"], loss_row), (oth_in,) = _allreduce_adamw_row(
        d_g_mix, g_mix, m_g_mix, v_g_mix, loss_part, comm=[_Swap([sum_in])])
    loss = loss_row[0, 0]
    res["w_up"], _ = _adamw("adamw_w_up", w_upT, m_w_up[0].T, v_w_up[0].T, sum_up, oth_up)
    res["w_out"], _ = _adamw("adamw_w_out", w_out2, m_w_out[0], v_w_out[0], sum_out, oth_out)
    res["w_in"], _ = _adamw("adamw_w_in", w_in2, m_w_in[0], v_w_in[0], sum_in, oth_in)

    pad_dw = lambda a: jnp.pad(a[0], ((0, HALO - KW), (0, 0)))
    vec_w = {"b_dw": b_dw, "ln_g": ln_g, "ln_b": ln_b, "s_pool": s_pool, "g_ffn": g_ffn, "g_final": g_final2, "b_in": b_in}
    vec_m = {"b_dw": m_b_dw, "ln_g": m_ln_g, "ln_b": m_ln_b, "s_pool": m_s_pool, "g_ffn": m_g_ffn,
             "g_final": m_g_final.reshape(1, D), "b_in": m_b_in}
    vec_v = {"b_dw": v_b_dw, "ln_g": v_ln_g, "ln_b": v_ln_b, "s_pool": v_s_pool, "g_ffn": v_g_ffn,
             "g_final": v_g_final.reshape(1, D), "b_in": v_b_in}
    small = _adamw_small(layout, sum_small, oth_small, pad_dw(w_dw), pad_dw(m_w_dw), pad_dw(v_w_dw),
                         w_pool3, m_w_pool[0], v_w_pool[0], vec_w, vec_m, vec_v)
    res["w_dw"] = [a[:KW][None] for a in small[0:4]]
    res["w_pool"] = [a[None] for a in small[4:8]]
    for q, k in enumerate(vec_w):
        res[k] = list(small[8 + 4 * q : 12 + 4 * q])
    res["g_final"] = [a.reshape(D) for a in res["g_final"]]
    for k in ("w_in", "w_out", "w_down"):
        res[k] = [a[None] for a in res[k]]
    for k in ("w_gate", "w_up"):
        res[k] = [a.T[None] for a in res[k]]

    order = ["g_mix", "w_in", "b_in", "w_dw", "b_dw", "ln_g", "ln_b", "w_pool", "s_pool", "w_out", "g_ffn", "w_gate", "w_up", "w_down", "g_final"]
    outs = [loss, grad_x[None]]
    for q in range(4):
        outs += [res[k][q] for k in order]
    return tuple(outs)
```

```python
import jax
import jax.numpy as jnp
from jax import lax
from jax.experimental import pallas as pl
from jax.experimental.pallas import tpu as pltpu

F32 = jnp.float32
BF16 = jnp.bfloat16
MESH = pl.DeviceIdType.MESH
ANY = pl.BlockSpec(memory_space=pl.ANY)

RMS_EPS = 1e-6
LN_EPS = 1e-5
POOL_WINDOWS = (2, 4, 8, 16)
ADAM_LR = 0.001
ADAM_B1 = 0.9
ADAM_B2 = 0.999
ADAM_EPS = 1e-08
ADAM_WD = 0.01
ADAM_STEP = 10

LANES = 128
SUBLANES = 8
HALO = 32
CONV_ROWS = 64
VMEM_LIMIT = 56 * 1024 * 1024
PACK_W = 512
N_CHIPS = 4
N_DEV = 8


def _tile(n, want, mult=8):
    t = min(n, want)
    while n % t or t % mult:
        t -= 1
    return t


def _sigmoid(x):
    return 1.0 / (1.0 + jnp.exp(-x))


def _dot(a, b, dims):
    return lax.dot_general(a, b, (dims, ((), ())), preferred_element_type=F32)


NN = ((1,), (0,))
NT = ((1,), (1,))
TN = ((0,), (0,))


def _rms_bwd(x, g, dy):
    r = lax.rsqrt(jnp.mean(x * x, axis=-1, keepdims=True) + RMS_EPS)
    xh = x * r
    gy = dy * g
    dx = r * (gy - xh * jnp.mean(gy * xh, axis=-1, keepdims=True))
    return dx, dy * xh


def _accumulate(ref, first, val):
    @pl.when(first)
    def _():
        ref[...] = val

    @pl.when(jnp.logical_not(first))
    def _():
        ref[...] += val


def _place():
    return lax.axis_index("x"), lax.axis_index("y"), lax.axis_index("c")


def _other_chips(x, y):
    return [(1 - x, y), (x, 1 - y), (1 - x, 1 - y)]


def _rows(ref, start, n):
    return ref.at[pl.ds(pl.multiple_of(start, 16), n)]


def _window(ref, how, k, c=None):
    if how == "all":
        return ref
    if how == "lead":
        return ref.at[k]
    if how == "rows":
        n = ref.shape[0] // N_CHIPS
        if c is None:
            return _rows(ref, k * n, n)
        return _rows(ref, k * n + c * (n // 2), n // 2)
    n = ref.shape[1] // N_CHIPS
    cols = pl.ds(pl.multiple_of(k * n, LANES), n)
    if c is None:
        return ref.at[:, cols]
    h = ref.shape[0] // 2
    return ref.at[pl.ds(pl.multiple_of(c * h, 16), h), cols]


def _remote(src, dst, sems, s, device):
    return pltpu.make_async_remote_copy(
        src_ref=src, dst_ref=dst, send_sem=sems.at[s], recv_sem=sems.at[s + 1], device_id=device, device_id_type=MESH)


class _GatherIci:
    aliased = True

    def __init__(self, fulls, hows, splits, which=(0, 1, 2)):
        self.fulls, self.hows, self.splits, self.which = list(fulls), list(hows), list(splits), tuple(which)

    def inputs(self):
        return self.fulls

    def out_shapes(self):
        return [jax.ShapeDtypeStruct(a.shape, a.dtype) for a in self.fulls]

    def n_sems(self):
        return 6 * len(self.fulls)

    def build(self, ins, outs, sems, base):
        x, y, c = _place()
        me = 2 * x + y
        chips = _other_chips(x, y)
        starts, waits = [], []
        for a, (how, sp) in enumerate(zip(self.hows, self.splits)):
            half = c if sp else None
            mine = _window(outs[a], how, me, half)
            for j in self.which:
                px, py = chips[j]
                s = base + 6 * a + 2 * j
                cp = _remote(mine, mine, sems, s, (px, py, c))
                landing = _remote(mine, _window(outs[a], how, 2 * px + py, half), sems, s, (px, py, c))
                starts.append(cp.start)
                waits += [landing.wait_recv, cp.wait_send]
        return starts, waits


class _GatherD2d:
    aliased = True

    def __init__(self, fulls, hows):
        self.fulls, self.hows = list(fulls), list(hows)

    def inputs(self):
        return self.fulls

    def out_shapes(self):
        return [jax.ShapeDtypeStruct(a.shape, a.dtype) for a in self.fulls]

    def n_sems(self):
        return 6 * len(self.fulls)

    def build(self, ins, outs, sems, base):
        x, y, c = _place()
        starts, waits = [], []
        for a, how in enumerate(self.hows):
            for j, (px, py) in enumerate(_other_chips(x, y)):
                s = base + 6 * a + 2 * j
                got = _window(outs[a], how, 2 * px + py, c)
                cp = _remote(got, got, sems, s, (x, y, 1 - c))
                landing = _remote(got, _window(outs[a], how, 2 * px + py, 1 - c), sems, s, (x, y, 1 - c))
                starts.append(cp.start)
                waits += [landing.wait_recv, cp.wait_send]
        return starts, waits


def _part_shape(a, how):
    if how == "all":
        return a.shape
    if how == "rows":
        return (a.shape[0] // N_CHIPS, a.shape[1])
    return (a.shape[0], a.shape[1] // N_CHIPS)


class _Scatter:
    aliased = False

    def __init__(self, fulls, hows, which=(0, 1, 2)):
        self.fulls, self.hows, self.which = list(fulls), list(hows), tuple(which)

    def inputs(self):
        return self.fulls

    def out_shapes(self):
        return [jax.ShapeDtypeStruct((len(self.which),) + _part_shape(a, h), a.dtype) for a, h in zip(self.fulls, self.hows)]

    def n_sems(self):
        return 6 * len(self.fulls)

    def build(self, ins, outs, sems, base):
        x, y, c = _place()
        chips = _other_chips(x, y)
        starts, waits = [], []
        for a, how in enumerate(self.hows):
            for slot, j in enumerate(self.which):
                px, py = chips[j]
                cp = _remote(_window(ins[a], how, 2 * px + py), outs[a].at[slot], sems, base + 6 * a + 2 * j, (px, py, c))
                starts.append(cp.start)
                waits += [cp.wait_recv, cp.wait_send]
        return starts, waits


class _Swap:
    aliased = False

    def __init__(self, arrays):
        self.arrays = list(arrays)

    def inputs(self):
        return self.arrays

    def out_shapes(self):
        return [jax.ShapeDtypeStruct(a.shape, a.dtype) for a in self.arrays]

    def n_sems(self):
        return 2 * len(self.arrays)

    def build(self, ins, outs, sems, base):
        x, y, c = _place()
        starts, waits = [], []
        for a in range(len(ins)):
            cp = _remote(ins[a], outs[a], sems, base + 2 * a, (x, y, 1 - c))
            starts.append(cp.start)
            waits += [cp.wait_recv, cp.wait_send]
        return starts, waits


def _call(name, body, grid, in_specs, out_specs, out_shape, args, scratch=(), comm=()):
    comm = list(comm)
    n_in, n_out, n_scr = len(args), len(out_shape), len(scratch)
    c_in = [a for op in comm for a in op.inputs()]
    c_out = [s for op in comm for s in op.out_shapes()]
    n_sems = sum(op.n_sems() for op in comm)
    aliases, i_in, i_out = {}, 0, 0
    for op in comm:
        if op.aliased:
            for q in range(len(op.inputs())):
                aliases[n_in + i_in + q] = n_out + i_out + q
        i_in, i_out = i_in + len(op.inputs()), i_out + len(op.out_shapes())

    def wrapped(*refs):
        ins = refs[:n_in]
        cin = refs[n_in : n_in + len(c_in)]
        o0 = n_in + len(c_in)
        outs = refs[o0 : o0 + n_out]
        cout = refs[o0 + n_out : o0 + n_out + len(c_out)]
        s0 = o0 + n_out + len(c_out)
        scr = refs[s0 : s0 + n_scr]

        def copies():
            sems = refs[s0 + n_scr]
            starts, waits = [], []
            i_in = i_out = base = 0
            for op in comm:
                ni, no = len(op.inputs()), len(op.out_shapes())
                s, w = op.build(cin[i_in : i_in + ni], cout[i_out : i_out + no], sems, base)
                starts += s
                waits += w
                i_in, i_out, base = i_in + ni, i_out + no, base + op.n_sems()
            return starts, waits

        def run_starts():
            for start in copies()[0]:
                start()

        def run_waits():
            for wait in copies()[1]:
                wait()

        if comm and grid:
            first = last = True
            for d, n in enumerate(grid):
                first = jnp.logical_and(first, pl.program_id(d) == 0)
                last = jnp.logical_and(last, pl.program_id(d) == n - 1)
            pl.when(first)(run_starts)
        elif comm:
            run_starts()
        if body is not None:
            body(*ins, *outs, *scr)
        if comm and grid:
            pl.when(last)(run_waits)
        elif comm:
            run_waits()

    res = pl.pallas_call(
        wrapped,
        name=name,
        grid=grid,
        in_specs=list(in_specs) + [ANY] * len(c_in),
        out_specs=list(out_specs) + [ANY] * len(c_out),
        out_shape=list(out_shape) + c_out,
        scratch_shapes=list(scratch) + ([pltpu.SemaphoreType.DMA((n_sems,))] if comm else []),
        input_output_aliases=aliases,
        compiler_params=pltpu.CompilerParams(dimension_semantics=("arbitrary",) * len(grid), vmem_limit_bytes=VMEM_LIMIT),
    )(*args, *c_in)
    return tuple(res[:n_out]), tuple(res[n_out:])


def _gather_now(fulls, hows, splits):
    n = len(fulls)
    ici = _GatherIci(fulls, hows, splits)
    split_ids = [a for a in range(n) if splits[a]]
    d2d = _GatherD2d([fulls[a] for a in split_ids], [hows[a] for a in split_ids])

    def body(*refs):
        outs, sems = refs[n : 2 * n], refs[2 * n]
        for op, op_refs, base in ((ici, outs, 0), (d2d, [outs[a] for a in split_ids], ici.n_sems())):
            starts, waits = op.build(None, op_refs, sems, base)
            for start in starts:
                start()
            for wait in waits:
                wait()

    return pl.pallas_call(
        body,
        name="gather_first",
        in_specs=[ANY] * n,
        out_specs=[ANY] * n,
        out_shape=ici.out_shapes(),
        scratch_shapes=[pltpu.SemaphoreType.DMA((ici.n_sems() + d2d.n_sems(),))],
        input_output_aliases={a: a for a in range(n)},
    )(*fulls)


def _place_weights(shards, hows, full_shapes, dtypes, transposed):
    n = len(shards)

    def body(*refs):
        ins, outs, bufs, sems = refs[:n], refs[n : 2 * n], refs[2 * n : 3 * n], refs[3 * n]
        x, y, _ = _place()
        copies = []
        for a in range(n):
            val = ins[a][...].T if transposed[a] else ins[a][...]
            bufs[a][...] = val.astype(dtypes[a])
            cp = pltpu.make_async_copy(bufs[a], _window(outs[a], hows[a], 2 * x + y), sems.at[a])
            cp.start()
            copies.append(cp)
        for cp in copies:
            cp.wait()

    return pl.pallas_call(
        body,
        name="place_weights",
        in_specs=[pl.BlockSpec(memory_space=pltpu.VMEM)] * n,
        out_specs=[ANY] * n,
        out_shape=[jax.ShapeDtypeStruct(s, d) for s, d in zip(full_shapes, dtypes)],
        scratch_shapes=[pltpu.VMEM(a.shape[::-1] if t else a.shape, d) for a, d, t in zip(shards, dtypes, transposed)]
        + [pltpu.SemaphoreType.DMA((n,))],
        compiler_params=pltpu.CompilerParams(vmem_limit_bytes=VMEM_LIMIT),
    )(*shards)


def _in_proj(x, g_mix, w_inT_b, b_in, comm=()):
    T, D = x.shape
    CI = w_inT_b.shape[0]
    tm = _tile(T, 512)

    def body(x_ref, g_ref, w_ref, b_ref, z_ref, xn_ref):
        xv = x_ref[...]
        r = lax.rsqrt(jnp.mean(xv * xv, axis=-1, keepdims=True) + RMS_EPS)
        xn = (xv * r * g_ref[...]).astype(BF16)
        xn_ref[...] = xn
        z_ref[...] = _dot(xn, w_ref[...], NT) + b_ref[...]

    return _call(
        "in_proj",
        body,
        (T // tm,),
        [
            pl.BlockSpec((tm, D), lambda i: (i, 0)),
            pl.BlockSpec((1, D), lambda i: (0, 0)),
            pl.BlockSpec((CI, D), lambda i: (0, 0)),
            pl.BlockSpec((1, CI), lambda i: (0, 0)),
        ],
        [pl.BlockSpec((tm, CI), lambda i: (i, 0)), pl.BlockSpec((tm, D), lambda i: (i, 0))],
        [jax.ShapeDtypeStruct((T, CI), F32), jax.ShapeDtypeStruct((T, D), BF16)],
        (x, g_mix, w_inT_b, b_in),
        comm=comm,
    )


def _fill_shifted(scr):
    n = scr.shape[1] - SUBLANES
    for s in range(1, SUBLANES):
        scr[s, 0:n, :] = scr[0, s : s + n, :]


def _shifted_rows(scr, off, n, cs):
    s = off % SUBLANES
    return scr[s, off - s : off - s + n, cs]


def _pool_mean_minus_token(p_scr, cs, w, cnt, tt):
    tok = p_scr[HALO : HALO + tt, cs]
    s = tok
    for d in range(1, w):
        s = s + p_scr[HALO - d : HALO - d + tt, cs]
    return s / cnt - tok


def _seq_fwd(z, w_dw4, b_dw, ln_g, ln_b, w_pool_b, s_pool, comm=()):
    T, CI = z.shape
    CC = ln_g.shape[1]
    n_grp, G = w_pool_b.shape[0], w_pool_b.shape[-1]
    KW = w_dw4.shape[1]
    D = CC + n_grp * G
    tt = _tile(T, 256, HALO)
    per = tt // HALO

    def body(zc_ref, zp_ref, wdw_ref, bdw_ref, lng_ref, lnb_ref, wp_ref, sp_ref, y_ref, v_ref, u_scr, p_scr):
        i = pl.program_id(0)
        first = i == 0
        u_prev = zp_ref[:, 0:CC] * _sigmoid(zp_ref[:, CC : 2 * CC])
        u_scr[0, 0:HALO, :] = jnp.where(first, 0.0, u_prev)
        p_scr[0:HALO, :] = jnp.where(first, 0.0, zp_ref[:, 2 * CC :])
        u_scr[0, HALO:, :] = zc_ref[:, 0:CC] * _sigmoid(zc_ref[:, CC : 2 * CC])
        p_scr[HALO:, :] = zc_ref[:, 2 * CC :]
        _fill_shifted(u_scr)

        for j in range(CC // LANES):
            cs = slice(LANES * j, LANES * (j + 1))
            for rb in range(tt // CONV_ROWS):
                acc = jnp.zeros((CONV_ROWS, LANES), F32)
                for k in range(KW):
                    off = HALO - (KW - 1) + k + rb * CONV_ROWS
                    acc = acc + _shifted_rows(u_scr, off, CONV_ROWS, cs) * wdw_ref[j, k : k + 1, :]
                v_ref[rb * CONV_ROWS : (rb + 1) * CONV_ROWS, cs] = acc + bdw_ref[:, cs]

        v = v_ref[...]
        mu = jnp.mean(v, axis=-1, keepdims=True)
        d = v - mu
        var = jnp.mean(d * d, axis=-1, keepdims=True)
        ln = d * lax.rsqrt(var + LN_EPS) * lng_ref[...] + lnb_ref[...]
        y_ref[:, 0:CC] = (ln * _sigmoid(ln)).astype(BF16)

        tpos = i * tt + lax.broadcasted_iota(jnp.int32, (tt, 1), 0)
        for gi, w in enumerate(POOL_WINDOWS):
            cs = slice(G * gi, G * (gi + 1))
            cnt = jnp.minimum(tpos + 1, w).astype(F32)
            yi = _pool_mean_minus_token(p_scr, cs, w, cnt, tt)
            q = _dot(yi.astype(BF16), wp_ref[gi], NN)
            y_ref[:, CC + G * gi : CC + G * (gi + 1)] = (q * sp_ref[:, cs]).astype(BF16)

    const2 = lambda i: (0, 0)
    return _call(
        "seq_fwd",
        body,
        (T // tt,),
        [
            pl.BlockSpec((tt, CI), lambda i: (i, 0)),
            pl.BlockSpec((HALO, CI), lambda i: (jnp.maximum(i * per - 1, 0), 0)),
            pl.BlockSpec(w_dw4.shape, lambda i: (0, 0, 0)),
            pl.BlockSpec((1, CC), const2),
            pl.BlockSpec((1, CC), const2),
            pl.BlockSpec((1, CC), const2),
            pl.BlockSpec(w_pool_b.shape, lambda i: (0, 0, 0)),
            pl.BlockSpec((1, n_grp * G), const2),
        ],
        [pl.BlockSpec((tt, D), lambda i: (i, 0)), pl.BlockSpec((tt, CC), lambda i: (i, 0))],
        [jax.ShapeDtypeStruct((T, D), BF16), jax.ShapeDtypeStruct((T, CC), F32)],
        (z, z, w_dw4, b_dw, ln_g, ln_b, w_pool_b, s_pool),
        scratch=[pltpu.VMEM((SUBLANES, HALO + tt, CC), F32), pltpu.VMEM((HALO + tt, n_grp * G), F32)],
        comm=comm,
    )


def _out_proj(y_b, x, w_out_b, g_ffn, comm=()):
    T, D = x.shape
    tm = _tile(T, 512)

    def body(y_ref, x_ref, w_ref, g_ref, h1_ref, hn_ref):
        h1 = x_ref[...] + _dot(y_ref[...], w_ref[...], NN)
        h1_ref[...] = h1
        r = lax.rsqrt(jnp.mean(h1 * h1, axis=-1, keepdims=True) + RMS_EPS)
        hn_ref[...] = (h1 * r * g_ref[...]).astype(BF16)

    row = lambda i: (i, 0)
    return _call(
        "out_proj",
        body,
        (T // tm,),
        [
            pl.BlockSpec((tm, y_b.shape[1]), row),
            pl.BlockSpec((tm, D), row),
            pl.BlockSpec(w_out_b.shape, lambda i: (0, 0)),
            pl.BlockSpec((1, D), lambda i: (0, 0)),
        ],
        [pl.BlockSpec((tm, D), row), pl.BlockSpec((tm, D), row)],
        [jax.ShapeDtypeStruct((T, D), F32), jax.ShapeDtypeStruct((T, D), BF16)],
        (y_b, x, w_out_b, g_ffn),
        comm=comm,
    )


def _hidden_tile(F):
    return _tile(F, 1408, LANES)


def _gate_up(hn_b, wgT_b, wuT_b, comm=()):
    T, D = hn_b.shape
    F = wgT_b.shape[0]
    tm, tf = _tile(T, 512), _hidden_tile(F)

    def body(hn_ref, wg_ref, wu_ref, g_ref, u_ref, a_ref):
        hn = hn_ref[...]
        gv = _dot(hn, wg_ref[...], NT)
        uv = _dot(hn, wu_ref[...], NT)
        g_ref[...] = gv.astype(BF16)
        u_ref[...] = uv.astype(BF16)
        a_ref[...] = (gv * _sigmoid(gv) * uv).astype(BF16)

    wspec = pl.BlockSpec((tf, D), lambda j, i: (j, 0))
    ospec = pl.BlockSpec((tm, tf), lambda j, i: (i, j))
    return _call(
        "gate_up",
        body,
        (F // tf, T // tm),
        [pl.BlockSpec((tm, D), lambda j, i: (i, 0)), wspec, wspec],
        [ospec, ospec, ospec],
        [jax.ShapeDtypeStruct((T, F), BF16)] * 3,
        (hn_b, wgT_b, wuT_b),
        comm=comm,
    )


def _down_loss(a_b, wd_b, h1, target, g_final, comm=()):
    T, D = h1.shape
    F = a_b.shape[1]
    tm = _tile(T, 256)
    nt = T // tm

    def body(a_ref, w_ref, h1_ref, t_ref, g_ref, dh2_ref, dh2b_ref, loss_ref, dg_ref):
        i = pl.program_id(0)
        h2 = h1_ref[...] + _dot(a_ref[...], w_ref[...], NN)
        r = lax.rsqrt(jnp.mean(h2 * h2, axis=-1, keepdims=True) + RMS_EPS)
        g = g_ref[...]
        diff = h2 * r * g - t_ref[...]
        _accumulate(loss_ref, i == 0, jnp.full(loss_ref.shape, jnp.sum(diff * diff) * (0.5 / D), F32))
        dh2, dg_rows = _rms_bwd(h2, g, diff * (1.0 / D))
        dh2_ref[...] = dh2
        dh2b_ref[...] = dh2.astype(BF16)
        _accumulate(dg_ref, i == 0, jnp.sum(dg_rows, axis=0, keepdims=True))

    row = lambda i: (i, 0)
    return _call(
        "down_loss",
        body,
        (nt,),
        [
            pl.BlockSpec((tm, F), row),
            pl.BlockSpec((F, D), lambda i: (0, 0)),
            pl.BlockSpec((tm, D), row),
            pl.BlockSpec((tm, D), row),
            pl.BlockSpec((1, D), lambda i: (0, 0)),
        ],
        [
            pl.BlockSpec((tm, D), row),
            pl.BlockSpec((tm, D), row),
            pl.BlockSpec((1, LANES), lambda i: (0, 0)),
            pl.BlockSpec((1, D), lambda i: (0, 0)),
        ],
        [
            jax.ShapeDtypeStruct((T, D), F32),
            jax.ShapeDtypeStruct((T, D), BF16),
            jax.ShapeDtypeStruct((1, LANES), F32),
            jax.ShapeDtypeStruct((1, D), F32),
        ],
        (a_b, wd_b, h1, target, g_final),
        comm=comm,
    )


def _ffn_bwd_act(dh2_b, wd_b, g_b, u_b, comm=()):
    T, D = dh2_b.shape
    F = wd_b.shape[0]
    tm, tf = _tile(T, 512), _hidden_tile(F)

    def body(d_ref, w_ref, g_ref, u_ref, dg_ref, du_ref):
        da = _dot(d_ref[...], w_ref[...], NT)
        gv = g_ref[...].astype(F32)
        uv = u_ref[...].astype(F32)
        sg = _sigmoid(gv)
        silu = gv * sg
        dg_ref[...] = (da * uv * (sg * (1.0 + gv * (1.0 - sg)))).astype(BF16)
        du_ref[...] = (da * silu).astype(BF16)

    aspec = pl.BlockSpec((tm, tf), lambda j, i: (i, j))
    return _call(
        "ffn_bwd_act",
        body,
        (F // tf, T // tm),
        [pl.BlockSpec((tm, D), lambda j, i: (i, 0)), pl.BlockSpec((tf, D), lambda j, i: (j, 0)), aspec, aspec],
        [aspec, aspec],
        [jax.ShapeDtypeStruct((T, F), BF16)] * 2,
        (dh2_b, wd_b, g_b, u_b),
        comm=comm,
    )


def _ffn_bwd_in(dg_b, du_b, wgT_b, wuT_b, h1, dh2, g_ffn, w_out_b, comm=()):
    T, D = h1.shape
    F = wgT_b.shape[0]
    DM = w_out_b.shape[0]
    tm = _tile(T, 256)

    def body(dg_ref, du_ref, wg_ref, wu_ref, h1_ref, dh2_ref, g_ref, wo_ref, dh1_ref, dh1b_ref, dy_ref, dgf_ref):
        i = pl.program_id(0)
        dhn = _dot(dg_ref[...], wg_ref[...], NN) + _dot(du_ref[...], wu_ref[...], NN)
        dx, dg_rows = _rms_bwd(h1_ref[...], g_ref[...], dhn)
        dh1 = dh2_ref[...] + dx
        dh1b = dh1.astype(BF16)
        dh1_ref[...] = dh1
        dh1b_ref[...] = dh1b
        dy_ref[...] = _dot(dh1b, wo_ref[...], NT)
        _accumulate(dgf_ref, i == 0, jnp.sum(dg_rows, axis=0, keepdims=True))

    row = lambda i: (i, 0)
    const = lambda i: (0, 0)
    return _call(
        "ffn_bwd_in",
        body,
        (T // tm,),
        [
            pl.BlockSpec((tm, F), row),
            pl.BlockSpec((tm, F), row),
            pl.BlockSpec((F, D), const),
            pl.BlockSpec((F, D), const),
            pl.BlockSpec((tm, D), row),
            pl.BlockSpec((tm, D), row),
            pl.BlockSpec((1, D), const),
            pl.BlockSpec((DM, D), const),
        ],
        [pl.BlockSpec((tm, D), row), pl.BlockSpec((tm, D), row), pl.BlockSpec((tm, DM), row), pl.BlockSpec((1, D), const)],
        [
            jax.ShapeDtypeStruct((T, D), F32),
            jax.ShapeDtypeStruct((T, D), BF16),
            jax.ShapeDtypeStruct((T, DM), F32),
            jax.ShapeDtypeStruct((1, D), F32),
        ],
        (dg_b, du_b, wgT_b, wuT_b, h1, dh2, g_ffn, w_out_b),
        comm=comm,
    )


def _seq_bwd(z, dy, v, w_dw4, ln_g, ln_b, w_pool_b, s_pool, comm=()):
    T, CI = z.shape
    CC = ln_g.shape[1]
    n_grp, G = w_pool_b.shape[0], w_pool_b.shape[-1]
    CP = n_grp * G
    KW = w_dw4.shape[1]
    n_cc = CC // LANES
    D = CC + CP
    tt = _tile(T, 256, HALO)
    per = tt // HALO
    n_tiles = T // tt
    last_halo = T // HALO - 1

    def body(zc_ref, zp_ref, dyc_ref, dyn_ref, vc_ref, vn_ref, wdw_ref, lng_ref, lnb_ref, wp_ref, sp_ref,
             dz_ref, dwdw_ref, dbdw_ref, dlng_ref, dlnb_ref, dwp_ref, dsp_ref, dbin_ref,
             dv_scr, u_scr, p_scr, g_scr, dw_scr):
        i = pl.program_id(0)
        first = i == 0
        last = i == n_tiles - 1
        lng, lnb = lng_ref[...], lnb_ref[...]

        def conv_pre(vv, dyc):
            mu = jnp.mean(vv, axis=-1, keepdims=True)
            d = vv - mu
            rs = lax.rsqrt(jnp.mean(d * d, axis=-1, keepdims=True) + LN_EPS)
            xh = d * rs
            ln = xh * lng + lnb
            sg = _sigmoid(ln)
            dln = dyc * (sg * (1.0 + ln * (1.0 - sg)))
            dxh = dln * lng
            dv = rs * (dxh - jnp.mean(dxh, axis=-1, keepdims=True) - xh * jnp.mean(dxh * xh, axis=-1, keepdims=True))
            return dv, dln, xh

        dv_c, dln_c, xh_c = conv_pre(vc_ref[...], dyc_ref[:, 0:CC])
        dv_scr[0, 0:tt, :] = dv_c
        dv_n, _, _ = conv_pre(vn_ref[...], dyn_ref[:, 0:CC])
        dv_scr[0, tt:, :] = jnp.where(last, 0.0, dv_n)
        _fill_shifted(dv_scr)
        _accumulate(dlng_ref, first, jnp.sum(dln_c * xh_c, axis=0, keepdims=True))
        _accumulate(dlnb_ref, first, jnp.sum(dln_c, axis=0, keepdims=True))
        _accumulate(dbdw_ref, first, jnp.sum(dv_c, axis=0, keepdims=True))

        u_scr[...] = zc_ref[:, 0:CC] * _sigmoid(zc_ref[:, CC : 2 * CC])

        @pl.when(first)
        def _():
            dw_scr[...] = jnp.zeros_like(dw_scr)

        for j in range(n_cc):
            cs = slice(LANES * j, LANES * (j + 1))
            gs = slice(CC + LANES * j, CC + LANES * (j + 1))
            dbin_a = jnp.zeros((1, LANES), F32)
            dbin_g = jnp.zeros((1, LANES), F32)
            for rb in range(tt // CONV_ROWS):
                rows = slice(rb * CONV_ROWS, (rb + 1) * CONV_ROWS)
                u_blk = u_scr[rows, cs]
                du = jnp.zeros((CONV_ROWS, LANES), F32)
                for k in range(KW):
                    off = rb * CONV_ROWS + (KW - 1) - k
                    d = _shifted_rows(dv_scr, off, CONV_ROWS, cs)
                    du = du + d * wdw_ref[j, k : k + 1, :]
                    dw_scr[j * HALO + k] += jnp.sum((u_blk * d).reshape(CONV_ROWS // 8, 8, LANES), axis=0)
                a = zc_ref[rows, cs]
                sg = _sigmoid(zc_ref[rows, gs])
                da = du * sg
                dgate = du * a * sg * (1.0 - sg)
                dz_ref[rows, cs] = da.astype(BF16)
                dz_ref[rows, gs] = dgate.astype(BF16)
                dbin_a = dbin_a + jnp.sum(da, axis=0, keepdims=True)
                dbin_g = dbin_g + jnp.sum(dgate, axis=0, keepdims=True)
            _accumulate(dbin_ref.at[:, cs], first, dbin_a)
            _accumulate(dbin_ref.at[:, gs], first, dbin_g)

        @pl.when(last)
        def _():
            dwdw_ref[...] = jnp.sum(dw_scr[...], axis=1).reshape(dwdw_ref.shape)

        p_scr[0:HALO, :] = jnp.where(first, 0.0, zp_ref[:, 2 * CC :])
        p_scr[HALO:, :] = zc_ref[:, 2 * CC :]
        tpos = i * tt + lax.broadcasted_iota(jnp.int32, (tt, 1), 0)
        for gi, w in enumerate(POOL_WINDOWS):
            cs = slice(G * gi, G * (gi + 1))
            ys = slice(CC + G * gi, CC + G * (gi + 1))
            ps = slice(2 * CC + G * gi, 2 * CC + G * (gi + 1))
            cnt = jnp.minimum(tpos + 1, w).astype(F32)
            yib = _pool_mean_minus_token(p_scr, cs, w, cnt, tt).astype(BF16)
            wp = wp_ref[gi]
            sp = sp_ref[:, cs]
            dyp = dyc_ref[:, ys]
            q = _dot(yib, wp, NN)
            _accumulate(dsp_ref.at[:, cs], first, jnp.sum(dyp * q, axis=0, keepdims=True))
            dq_c = (dyp * sp).astype(BF16)
            dq_n = (jnp.where(last, 0.0, dyn_ref[:, ys]) * sp).astype(BF16)
            _accumulate(dwp_ref.at[gi], first, _dot(yib, dq_c, TN))
            dyi_c = _dot(dq_c, wp, NT)
            g_scr[0:tt, cs] = dyi_c / cnt
            g_scr[tt:, cs] = _dot(dq_n, wp, NT) * (1.0 / w)
            dp = -dyi_c
            for d in range(w):
                dp = dp + g_scr[d : d + tt, cs]
            dz_ref[:, ps] = dp.astype(BF16)
            _accumulate(dbin_ref.at[:, ps], first, jnp.sum(dp, axis=0, keepdims=True))

    cur = lambda i: (i, 0)
    prev = lambda i: (jnp.maximum(i * per - 1, 0), 0)
    nxt = lambda i: (jnp.minimum((i + 1) * per, last_halo), 0)
    c2 = lambda i: (0, 0)
    c3 = lambda i: (0, 0, 0)
    return _call(
        "seq_bwd",
        body,
        (n_tiles,),
        [
            pl.BlockSpec((tt, CI), cur),
            pl.BlockSpec((HALO, CI), prev),
            pl.BlockSpec((tt, D), cur),
            pl.BlockSpec((HALO, D), nxt),
            pl.BlockSpec((tt, CC), cur),
            pl.BlockSpec((HALO, CC), nxt),
            pl.BlockSpec(w_dw4.shape, c3),
            pl.BlockSpec((1, CC), c2),
            pl.BlockSpec((1, CC), c2),
            pl.BlockSpec(w_pool_b.shape, c3),
            pl.BlockSpec((1, CP), c2),
        ],
        [
            pl.BlockSpec((tt, CI), cur),
            pl.BlockSpec((n_cc, HALO, LANES), c3),
            pl.BlockSpec((1, CC), c2),
            pl.BlockSpec((1, CC), c2),
            pl.BlockSpec((1, CC), c2),
            pl.BlockSpec((n_grp, G, G), c3),
            pl.BlockSpec((1, CP), c2),
            pl.BlockSpec((1, CI), c2),
        ],
        [
            jax.ShapeDtypeStruct((T, CI), BF16),
            jax.ShapeDtypeStruct((n_cc, HALO, LANES), F32),
            jax.ShapeDtypeStruct((1, CC), F32),
            jax.ShapeDtypeStruct((1, CC), F32),
            jax.ShapeDtypeStruct((1, CC), F32),
            jax.ShapeDtypeStruct((n_grp, G, G), F32),
            jax.ShapeDtypeStruct((1, CP), F32),
            jax.ShapeDtypeStruct((1, CI), F32),
        ],
        (z, z, dy, dy, v, v, w_dw4, ln_g, ln_b, w_pool_b, s_pool),
        scratch=[
            pltpu.VMEM((SUBLANES, tt + HALO, CC), F32),
            pltpu.VMEM((tt, CC), F32),
            pltpu.VMEM((HALO + tt, CP), F32),
            pltpu.VMEM((tt + HALO, CP), F32),
            pltpu.VMEM((n_cc * HALO, 8, LANES), F32),
        ],
        comm=comm,
    )


def _in_proj_bwd(dz_b, w_inT_b, x, dh1, g_mix, comm=()):
    T, D = x.shape
    CI = w_inT_b.shape[0]
    tm = _tile(T, 512)

    def body(dz_ref, w_ref, x_ref, dh1_ref, g_ref, dx_ref, dg_ref):
        i = pl.program_id(0)
        dxn = _dot(dz_ref[...], w_ref[...], NN)
        dx, dg_rows = _rms_bwd(x_ref[...], g_ref[...], dxn)
        dx_ref[...] = dh1_ref[...] + dx
        _accumulate(dg_ref, i == 0, jnp.sum(dg_rows, axis=0, keepdims=True))

    row = lambda i: (i, 0)
    const = lambda i: (0, 0)
    return _call(
        "in_proj_bwd",
        body,
        (T // tm,),
        [
            pl.BlockSpec((tm, CI), row),
            pl.BlockSpec((CI, D), const),
            pl.BlockSpec((tm, D), row),
            pl.BlockSpec((tm, D), row),
            pl.BlockSpec((1, D), const),
        ],
        [pl.BlockSpec((tm, D), row), pl.BlockSpec((1, D), const)],
        [jax.ShapeDtypeStruct((T, D), F32), jax.ShapeDtypeStruct((1, D), F32)],
        (dz_b, w_inT_b, x, dh1, g_mix),
        comm=comm,
    )


def _weight_grad(name, a_b, b_b, comm=()):
    T, N1 = a_b.shape
    N2 = b_b.shape[1]
    t1 = _tile(N1, 1408, LANES)
    tk = _tile(T, 1024)
    nk = T // tk

    def body(a_ref, b_ref, o_ref, acc):
        k = pl.program_id(1)
        _accumulate(acc, k == 0, _dot(a_ref[...], b_ref[...], TN))

        @pl.when(k == nk - 1)
        def _():
            o_ref[...] = acc[...].astype(BF16)

    (out,), rest = _call(
        name,
        body,
        (N1 // t1, nk),
        [pl.BlockSpec((tk, t1), lambda n, k: (k, n)), pl.BlockSpec((tk, N2), lambda n, k: (k, 0))],
        [pl.BlockSpec((t1, N2), lambda n, k: (n, 0))],
        [jax.ShapeDtypeStruct((N1, N2), BF16)],
        (a_b, b_b),
        scratch=[pltpu.VMEM((t1, N2), F32)],
        comm=comm,
    )
    return out, rest


def _sum_parts(name, full, how, parts, me):
    _, R, C = parts[0].shape
    tr = _tile(R, 512)
    nb = R // tr
    where = [(q, r) for q, p in enumerate(parts) for r in range(p.shape[0])]
    assert len(where) == 3

    def body(me_ref, own_ref, *refs):
        o_ref = refs[-1]
        f = lambda j: refs[where[j][0]][where[j][1]].astype(F32)
        o_ref[...] = (own_ref[...].astype(F32) + f(0)) + (f(1) + f(2))

    own_map = {"rows": lambda i, me_ref: (me_ref[0] * nb + i, 0), "cols": lambda i, me_ref: (i, me_ref[0]),
               "all": lambda i, me_ref: (i, 0)}[how]
    return pl.pallas_call(
        body,
        name=name,
        grid_spec=pltpu.PrefetchScalarGridSpec(
            num_scalar_prefetch=1,
            grid=(nb,),
            in_specs=[pl.BlockSpec((tr, C), own_map)]
            + [pl.BlockSpec((p.shape[0], tr, C), lambda i, me_ref: (0, i, 0)) for p in parts],
            out_specs=pl.BlockSpec((tr, C), lambda i, me_ref: (i, 0)),
        ),
        out_shape=jax.ShapeDtypeStruct((R, C), F32),
        compiler_params=pltpu.CompilerParams(dimension_semantics=("arbitrary",), vmem_limit_bytes=VMEM_LIMIT),
    )(me, full, *parts)


_M_CORR = 1.0 - ADAM_B1**ADAM_STEP
_V_CORR = 1.0 - ADAM_B2**ADAM_STEP


def _adamw_math(w, g, m, v):
    m = ADAM_B1 * m + (1.0 - ADAM_B1) * g
    v = ADAM_B2 * v + (1.0 - ADAM_B2) * (g * g)
    delta = -ADAM_LR * ((m / _M_CORR) / (jnp.sqrt(v / _V_CORR) + ADAM_EPS) + ADAM_WD * w)
    return delta, m, v


def _adamw(name, w, m, v, g_here, g_there, g_transposed=False, comm=()):
    R, C = w.shape
    tr = _tile(R, 256, LANES if g_transposed else 8)

    def body(w_ref, m_ref, v_ref, ga_ref, gb_ref, g_ref, d_ref, nm_ref, nv_ref):
        g = ga_ref[...] + gb_ref[...]
        if g_transposed:
            g = g.T
        g_ref[...] = g
        d_ref[...], nm_ref[...], nv_ref[...] = _adamw_math(w_ref[...], g, m_ref[...], v_ref[...])

    spec = pl.BlockSpec((tr, C), lambda i: (i, 0))
    gspec = pl.BlockSpec((C, tr), lambda i: (0, i)) if g_transposed else spec
    return _call(name, body, (R // tr,), [spec] * 3 + [gspec] * 2, [spec] * 4, [jax.ShapeDtypeStruct((R, C), F32)] * 4,
                 (w, m, v, g_here, g_there), comm=comm)


class _PackLayout:
    def __init__(self, n_cc, n_grp, G, widths):
        self.dw_rows = (0, HALO)
        self.wp_rows = (HALO, HALO + G)
        self.n_cc, self.n_grp, self.G = n_cc, n_grp, G
        self.vec = {}
        r = HALO + G
        for name, width in widths:
            self.vec[name] = (r, width)
            r += width // PACK_W
        self.rows = -(-r // 8) * 8


def _pack_small(layout, dwdw, dwp, vecs):
    names = list(vecs)

    def body(*refs):
        dw_ref, wp_ref = refs[0], refs[1]
        vec_refs = refs[2 : 2 + len(names)]
        o_ref = refs[-1]
        o_ref[...] = jnp.zeros_like(o_ref)
        for j in range(layout.n_cc):
            o_ref[layout.dw_rows[0] : layout.dw_rows[1], j * LANES : (j + 1) * LANES] = dw_ref[j]
        for i in range(layout.n_grp):
            o_ref[layout.wp_rows[0] : layout.wp_rows[1], i * layout.G : (i + 1) * layout.G] = wp_ref[i]
        for name, ref in zip(names, vec_refs):
            r, width = layout.vec[name]
            for h in range(width // PACK_W):
                o_ref[r + h : r + h + 1, :] = ref[:, h * PACK_W : (h + 1) * PACK_W]

    return pl.pallas_call(
        body,
        name="pack_small",
        out_shape=jax.ShapeDtypeStruct((layout.rows, PACK_W), F32),
    )(dwdw, dwp, *[vecs[k] for k in names])


def _adamw_small(layout, g_here, g_there, w_dw, m_dw, v_dw, w_pool, m_pool, v_pool, vec_w, vec_m, vec_v):
    names = list(vec_w)
    nv = len(names)

    def body(*refs):
        ga_ref, gb_ref = refs[0], refs[1]
        wdw, mdw, vdw, wp, mp, vp = refs[2:8]
        vw, vm, vv = refs[8 : 8 + nv], refs[8 + nv : 8 + 2 * nv], refs[8 + 2 * nv : 8 + 3 * nv]
        outs = refs[8 + 3 * nv :]
        acc = outs[-1]
        acc[...] = ga_ref[...] + gb_ref[...]

        def emit(o, g, w, m, v, idx=()):
            res = (g,) + _adamw_math(w, g, m, v)
            for ref, val in zip(o, res):
                ref[idx] = val

        me = 2 * lax.axis_index("x") + lax.axis_index("y")
        for j in range(layout.n_cc):

            @pl.when(me == j)
            def _(j=j):
                g = acc[layout.dw_rows[0] : layout.dw_rows[1], j * LANES : (j + 1) * LANES]
                emit(outs[0:4], g, wdw[...], mdw[...], vdw[...], idx=...)

        for i in range(layout.n_grp):
            g = acc[layout.wp_rows[0] : layout.wp_rows[1], i * layout.G : (i + 1) * layout.G]
            emit(outs[4:8], g, wp[i], mp[i], vp[i], idx=i)
        for q, name in enumerate(names):
            r, width = layout.vec[name]
            for h in range(width // PACK_W):
                ls = slice(h * PACK_W, (h + 1) * PACK_W)
                g = acc[r + h : r + h + 1, :]
                emit(outs[8 + 4 * q : 12 + 4 * q], g, vw[q][:, ls], vm[q][:, ls], vv[q][:, ls], idx=(slice(None), ls))

    shapes = [w_dw.shape] * 4 + [w_pool.shape] * 4
    for name in names:
        shapes += [vec_w[name].shape] * 4
    return pl.pallas_call(
        body,
        name="adamw_small",
        out_shape=[jax.ShapeDtypeStruct(s, F32) for s in shapes],
        scratch_shapes=[pltpu.VMEM(g_here.shape, F32)],
    )(g_here, g_there, w_dw, m_dw, v_dw, w_pool, m_pool, v_pool,
      *[vec_w[k] for k in names], *[vec_m[k] for k in names], *[vec_v[k] for k in names])


def _allreduce_adamw_row(g_part, w, m, v, loss_part, comm=()):
    D = w.shape[1]
    n_pairs = N_DEV - 1

    def body(g_ref, w_ref, m_ref, v_ref, l_ref, go_ref, d_ref, nm_ref, nv_ref, lo_ref, land_g, land_l, sems):
        x, y, c = _place()
        copies = []
        for q, (src, land) in enumerate(((g_ref, land_g), (l_ref, land_l))):
            for r in range(1, N_DEV):
                fx, fy, fc = (r >> 2) & 1, (r >> 1) & 1, r & 1
                peer = (1 - x if fx else x, 1 - y if fy else y, 1 - c if fc else c)
                cp = _remote(src, land.at[r], sems, 2 * (q * n_pairs + r - 1), peer)
                cp.start()
                copies.append(cp)
        for cp in copies:
            cp.wait()

        def total(src, land):
            row = lambda r: src[...] if r == 0 else land[r]
            return ((row(0) + row(4)) + (row(2) + row(6))) + ((row(1) + row(5)) + (row(3) + row(7)))

        g = total(g_ref, land_g)
        go_ref[...] = g
        d_ref[...], nm_ref[...], nv_ref[...] = _adamw_math(w_ref[...], g, m_ref[...], v_ref[...])
        lo_ref[...] = total(l_ref, land_l)

    vm = pl.BlockSpec(memory_space=pltpu.VMEM)
    return _call(
        "allreduce_adamw_g_mix",
        body,
        (),
        [vm] * 5,
        [vm] * 5,
        [jax.ShapeDtypeStruct((1, D), F32)] * 4 + [jax.ShapeDtypeStruct(loss_part.shape, F32)],
        (g_part, w, m, v, loss_part),
        scratch=[pltpu.VMEM((N_DEV, 1, D), F32), pltpu.VMEM((N_DEV,) + loss_part.shape, F32),
                 pltpu.SemaphoreType.DMA((4 * n_pairs,))],
        comm=comm,
    )


def kernel(x, g_mix, w_in, b_in, w_dw, b_dw, ln_g, ln_b, w_pool, s_pool, w_out, g_ffn, w_gate, w_up, w_down, g_final, loss_target, m_g_mix, m_w_in, m_b_in, m_w_dw, m_b_dw, m_ln_g, m_ln_b, m_w_pool, m_s_pool, m_w_out, m_g_ffn, m_w_gate, m_w_up, m_w_down, m_g_final, v_g_mix, v_w_in, v_b_in, v_w_dw, v_b_dw, v_ln_g, v_ln_b, v_w_pool, v_s_pool, v_w_out, v_g_ffn, v_w_gate, v_w_up, v_w_down, v_g_final):
    x2 = x[0]
    target = loss_target[0]
    T, D = x2.shape
    w_in2, w_out2, w_down2, w_dw2 = w_in[0], w_out[0], w_down[0], w_dw[0]
    w_gateT, w_upT = w_gate[0].T, w_up[0].T
    CI = w_in2.shape[1] * N_CHIPS
    DM = w_out2.shape[0] * N_CHIPS
    F = w_down2.shape[0] * N_CHIPS
    KW, dw_cols = w_dw2.shape
    assert dw_cols == LANES
    n_grp, G = w_pool.shape[1], w_pool.shape[-1]
    w_pool3 = w_pool[0]
    g_final2 = g_final.reshape(1, D)

    me = (2 * lax.axis_index("x") + lax.axis_index("y")).astype(jnp.int32).reshape(1)

    f_in, f_out, f_gate, f_up, f_down, f_dw = _place_weights(
        [w_in2, w_out2, w_gateT, w_upT, w_down2, w_dw2], ["rows"] * 5 + ["lead"],
        [(CI, D), (DM, D), (F, D), (F, D), (F, D), (N_CHIPS, KW, dw_cols)], [BF16] * 5 + [F32], [True] + [False] * 5)
    w_inT_b, w_dw4 = _gather_now([f_in, f_dw], ["rows", "lead"], [True, False])
    w_pool_b = w_pool3.astype(BF16)
    (z, xn_b), (f_out, f_gate) = _in_proj(
        x2, g_mix, w_inT_b, b_in,
        comm=[_GatherIci([f_out], ["rows"], [True]), _GatherIci([f_gate], ["rows"], [True], which=(2,))])
    (y_b, v), (w_out_b, f_gate, f_up) = _seq_fwd(
        z, w_dw4, b_dw, ln_g, ln_b, w_pool_b, s_pool,
        comm=[_GatherD2d([f_out], ["rows"]), _GatherIci([f_gate], ["rows"], [True], which=(0, 1)),
              _GatherIci([f_up], ["rows"], [True])])
    (h1, hn_b), (wgT_b, wuT_b, f_down) = _out_proj(
        y_b, x2, w_out_b, g_ffn,
        comm=[_GatherD2d([f_gate, f_up], ["rows"] * 2), _GatherIci([f_down], ["rows"], [True])])
    (g_b, u_b, a_b), (wd_b,) = _gate_up(hn_b, wgT_b, wuT_b, comm=[_GatherD2d([f_down], ["rows"])])
    (dh2, dh2_b, loss_part, d_g_final), _ = _down_loss(a_b, wd_b, h1, target, g_final2)

    gw_down, _ = _weight_grad("grad_w_down", a_b, dh2_b)
    (dg_b, du_b), (p_down_xy,) = _ffn_bwd_act(dh2_b, wd_b, g_b, u_b, comm=[_Scatter([gw_down], ["rows"], which=(0, 1))])
    gw_gateT, (p_down_d,) = _weight_grad("grad_w_gate", dg_b, hn_b, comm=[_Scatter([gw_down], ["rows"], which=(2,))])
    gw_upT, _ = _weight_grad("grad_w_up", du_b, hn_b)
    sum_down = _sum_parts("sum_w_down", gw_down, "rows", [p_down_xy, p_down_d], me)
    (dh1, dh1_b, dy, d_g_ffn), (p_gate, oth_down) = _ffn_bwd_in(
        dg_b, du_b, wgT_b, wuT_b, h1, dh2, g_ffn, w_out_b, comm=[_Scatter([gw_gateT], ["rows"]), _Swap([sum_down])])
    gw_out, _ = _weight_grad("grad_w_out", y_b, dh1_b)
    sum_gate = _sum_parts("sum_w_gate", gw_gateT, "rows", [p_gate], me)
    res = {}
    res["w_down"], _ = _adamw("adamw_w_down", w_down2, m_w_down[0], v_w_down[0], sum_down, oth_down)
    (dz_b, d_wdw, d_bdw, d_lng, d_lnb, d_wp, d_sp, d_bin), (p_up, p_out, oth_gate) = _seq_bwd(
        z, dy, v, w_dw4, ln_g, ln_b, w_pool_b, s_pool,
        comm=[_Scatter([gw_upT, gw_out], ["rows", "rows"]), _Swap([sum_gate])])
    vec_grads = {"b_dw": d_bdw, "ln_g": d_lng, "ln_b": d_lnb, "s_pool": d_sp, "g_ffn": d_g_ffn, "g_final": d_g_final, "b_in": d_bin}
    layout = _PackLayout(dw_cols * N_CHIPS // LANES, n_grp, G, [(k, a.shape[1]) for k, a in vec_grads.items()])
    pack = _pack_small(layout, d_wdw, d_wp, vec_grads)
    sum_up = _sum_parts("sum_w_up", gw_upT, "rows", [p_up], me)
    sum_out = _sum_parts("sum_w_out", gw_out, "rows", [p_out], me)
    gw_inT, (p_small, oth_up, oth_out) = _weight_grad(
        "grad_w_in", dz_b, xn_b, comm=[_Scatter([pack], ["all"]), _Swap([sum_up, sum_out])])
    sum_small = _sum_parts("sum_small", pack, "all", [p_small], me)
    res["w_gate"], _ = _adamw("adamw_w_gate", w_gateT, m_w_gate[0].T, v_w_gate[0].T, sum_gate, oth_gate)
    (grad_x, d_g_mix), (p_in, oth_small) = _in_proj_bwd(
        dz_b, w_inT_b, x2, dh1, g_mix, comm=[_Scatter([gw_inT], ["rows"]), _Swap([sum_small])])
    res["w_up"], _ = _adamw("adamw_w_up", w_upT, m_w_up[0].T, v_w_up[0].T, sum_up, oth_up)
    res["w_out"], _ = _adamw("adamw_w_out", w_out2, m_w_out[0], v_w_out[0], sum_out, oth_out)
    sum_in = _sum_parts("sum_w_in", gw_inT, "rows", [p_in], me)
    (*res["g_mix"], loss_row), (oth_in,) = _allreduce_adamw_row(
        d_g_mix, g_mix, m_g_mix, v_g_mix, loss_part, comm=[_Swap([sum_in])])
    loss = loss_row[0, 0]
    res["w_in"], _ = _adamw("adamw_w_in", w_in2, m_w_in[0], v_w_in[0], sum_in, oth_in, g_transposed=True)

    pad_dw = lambda a: jnp.pad(a[0], ((0, HALO - KW), (0, 0)))
    vec_w = {"b_dw": b_dw, "ln_g": ln_g, "ln_b": ln_b, "s_pool": s_pool, "g_ffn": g_ffn, "g_final": g_final2, "b_in": b_in}
    vec_m = {"b_dw": m_b_dw, "ln_g": m_ln_g, "ln_b": m_ln_b, "s_pool": m_s_pool, "g_ffn": m_g_ffn,
             "g_final": m_g_final.reshape(1, D), "b_in": m_b_in}
    vec_v = {"b_dw": v_b_dw, "ln_g": v_ln_g, "ln_b": v_ln_b, "s_pool": v_s_pool, "g_ffn": v_g_ffn,
             "g_final": v_g_final.reshape(1, D), "b_in": v_b_in}
    small = _adamw_small(layout, sum_small, oth_small, pad_dw(w_dw), pad_dw(m_w_dw), pad_dw(v_w_dw),
                         w_pool3, m_w_pool[0], v_w_pool[0], vec_w, vec_m, vec_v)
    res["w_dw"] = [a[:KW][None] for a in small[0:4]]
    res["w_pool"] = [a[None] for a in small[4:8]]
    for q, k in enumerate(vec_w):
        res[k] = list(small[8 + 4 * q : 12 + 4 * q])
    res["g_final"] = [a.reshape(D) for a in res["g_final"]]
    for k in ("w_in", "w_out", "w_down"):
        res[k] = [a[None] for a in res[k]]
    for k in ("w_gate", "w_up"):
        res[k] = [a.T[None] for a in res[k]]

    order = ["g_mix", "w_in", "b_in", "w_dw", "b_dw", "ln_g", "ln_b", "w_pool", "s_pool", "w_out", "g_ffn", "w_gate", "w_up", "w_down", "g_final"]
    outs = [loss, grad_x[None]]
    for q in range(4):
        outs += [res[k][q] for k in order]
    return tuple(outs)
```

```python
import jax
import jax.numpy as jnp
from jax import lax
from jax.experimental import pallas as pl
from jax.experimental.pallas import tpu as pltpu

F32 = jnp.float32
BF16 = jnp.bfloat16
MESH = pl.DeviceIdType.MESH
ANY = pl.BlockSpec(memory_space=pl.ANY)

RMS_EPS = 1e-6
LN_EPS = 1e-5
POOL_WINDOWS = (2, 4, 8, 16)
ADAM_LR = 0.001
ADAM_B1 = 0.9
ADAM_B2 = 0.999
ADAM_EPS = 1e-08
ADAM_WD = 0.01
ADAM_STEP = 10

LANES = 128
SUBLANES = 8
HALO = 32
CONV_ROWS = 64
HIDDEN_CHUNK = 512
VMEM_LIMIT = 56 * 1024 * 1024
PACK_W = 512
N_CHIPS = 4
N_DEV = 8


def _tile(n, want, mult=8):
    t = min(n, want)
    while n % t or t % mult:
        t -= 1
    return t


def _sigmoid(x):
    return 1.0 / (1.0 + jnp.exp(-x))


def _dot(a, b, dims):
    return lax.dot_general(a, b, (dims, ((), ())), preferred_element_type=F32)


NN = ((1,), (0,))
NT = ((1,), (1,))
TN = ((0,), (0,))


def _rms_bwd(x, g, dy):
    r = lax.rsqrt(jnp.mean(x * x, axis=-1, keepdims=True) + RMS_EPS)
    xh = x * r
    gy = dy * g
    dx = r * (gy - xh * jnp.mean(gy * xh, axis=-1, keepdims=True))
    return dx, dy * xh


def _accumulate(ref, first, val):
    @pl.when(first)
    def _():
        ref[...] = val

    @pl.when(jnp.logical_not(first))
    def _():
        ref[...] += val


def _place():
    return lax.axis_index("x"), lax.axis_index("y"), lax.axis_index("c")


def _other_chips(x, y):
    return [(1 - x, y), (x, 1 - y), (1 - x, 1 - y)]


def _rows(ref, start, n):
    return ref.at[pl.ds(pl.multiple_of(start, 16), n)]


def _window(ref, how, k, c=None):
    if how == "all":
        return ref
    if how == "lead":
        return ref.at[k]
    if how == "rows":
        n = ref.shape[0] // N_CHIPS
        if c is None:
            return _rows(ref, k * n, n)
        return _rows(ref, k * n + c * (n // 2), n // 2)
    n = ref.shape[1] // N_CHIPS
    cols = pl.ds(pl.multiple_of(k * n, LANES), n)
    if c is None:
        return ref.at[:, cols]
    h = ref.shape[0] // 2
    return ref.at[pl.ds(pl.multiple_of(c * h, 16), h), cols]


def _remote(src, dst, sems, s, device):
    return pltpu.make_async_remote_copy(
        src_ref=src, dst_ref=dst, send_sem=sems.at[s], recv_sem=sems.at[s + 1], device_id=device, device_id_type=MESH)


class _GatherIci:
    aliased = True

    def __init__(self, fulls, hows, splits, which=(0, 1, 2)):
        self.fulls, self.hows, self.splits, self.which = list(fulls), list(hows), list(splits), tuple(which)

    def inputs(self):
        return self.fulls

    def out_shapes(self):
        return [jax.ShapeDtypeStruct(a.shape, a.dtype) for a in self.fulls]

    def n_sems(self):
        return 6 * len(self.fulls)

    def build(self, ins, outs, sems, base):
        x, y, c = _place()
        me = 2 * x + y
        chips = _other_chips(x, y)
        starts, waits = [], []
        for a, (how, sp) in enumerate(zip(self.hows, self.splits)):
            half = c if sp else None
            mine = _window(outs[a], how, me, half)
            for j in self.which:
                px, py = chips[j]
                s = base + 6 * a + 2 * j
                cp = _remote(mine, mine, sems, s, (px, py, c))
                landing = _remote(mine, _window(outs[a], how, 2 * px + py, half), sems, s, (px, py, c))
                starts.append(cp.start)
                waits += [landing.wait_recv, cp.wait_send]
        return starts, waits


class _GatherD2d:
    aliased = True

    def __init__(self, fulls, hows):
        self.fulls, self.hows = list(fulls), list(hows)

    def inputs(self):
        return self.fulls

    def out_shapes(self):
        return [jax.ShapeDtypeStruct(a.shape, a.dtype) for a in self.fulls]

    def n_sems(self):
        return 6 * len(self.fulls)

    def build(self, ins, outs, sems, base):
        x, y, c = _place()
        starts, waits = [], []
        for a, how in enumerate(self.hows):
            for j, (px, py) in enumerate(_other_chips(x, y)):
                s = base + 6 * a + 2 * j
                got = _window(outs[a], how, 2 * px + py, c)
                cp = _remote(got, got, sems, s, (x, y, 1 - c))
                landing = _remote(got, _window(outs[a], how, 2 * px + py, 1 - c), sems, s, (x, y, 1 - c))
                starts.append(cp.start)
                waits += [landing.wait_recv, cp.wait_send]
        return starts, waits


def _part_shape(a, how):
    if how == "all":
        return a.shape
    if how == "rows":
        return (a.shape[0] // N_CHIPS, a.shape[1])
    return (a.shape[0], a.shape[1] // N_CHIPS)


class _Scatter:
    aliased = False

    def __init__(self, fulls, hows, which=(0, 1, 2)):
        self.fulls, self.hows, self.which = list(fulls), list(hows), tuple(which)

    def inputs(self):
        return self.fulls

    def out_shapes(self):
        return [jax.ShapeDtypeStruct((len(self.which),) + _part_shape(a, h), a.dtype) for a, h in zip(self.fulls, self.hows)]

    def n_sems(self):
        return 6 * len(self.fulls)

    def build(self, ins, outs, sems, base):
        x, y, c = _place()
        chips = _other_chips(x, y)
        starts, waits = [], []
        for a, how in enumerate(self.hows):
            for slot, j in enumerate(self.which):
                px, py = chips[j]
                cp = _remote(_window(ins[a], how, 2 * px + py), outs[a].at[slot], sems, base + 6 * a + 2 * j, (px, py, c))
                starts.append(cp.start)
                waits += [cp.wait_recv, cp.wait_send]
        return starts, waits


class _Swap:
    aliased = False

    def __init__(self, arrays):
        self.arrays = list(arrays)

    def inputs(self):
        return self.arrays

    def out_shapes(self):
        return [jax.ShapeDtypeStruct(a.shape, a.dtype) for a in self.arrays]

    def n_sems(self):
        return 2 * len(self.arrays)

    def build(self, ins, outs, sems, base):
        x, y, c = _place()
        starts, waits = [], []
        for a in range(len(ins)):
            cp = _remote(ins[a], outs[a], sems, base + 2 * a, (x, y, 1 - c))
            starts.append(cp.start)
            waits += [cp.wait_recv, cp.wait_send]
        return starts, waits


def _call(name, body, grid, in_specs, out_specs, out_shape, args, scratch=(), comm=()):
    comm = list(comm)
    n_in, n_out, n_scr = len(args), len(out_shape), len(scratch)
    c_in = [a for op in comm for a in op.inputs()]
    c_out = [s for op in comm for s in op.out_shapes()]
    n_sems = sum(op.n_sems() for op in comm)
    aliases, i_in, i_out = {}, 0, 0
    for op in comm:
        if op.aliased:
            for q in range(len(op.inputs())):
                aliases[n_in + i_in + q] = n_out + i_out + q
        i_in, i_out = i_in + len(op.inputs()), i_out + len(op.out_shapes())

    def wrapped(*refs):
        ins = refs[:n_in]
        cin = refs[n_in : n_in + len(c_in)]
        o0 = n_in + len(c_in)
        outs = refs[o0 : o0 + n_out]
        cout = refs[o0 + n_out : o0 + n_out + len(c_out)]
        s0 = o0 + n_out + len(c_out)
        scr = refs[s0 : s0 + n_scr]

        def copies():
            sems = refs[s0 + n_scr]
            starts, waits = [], []
            i_in = i_out = base = 0
            for op in comm:
                ni, no = len(op.inputs()), len(op.out_shapes())
                s, w = op.build(cin[i_in : i_in + ni], cout[i_out : i_out + no], sems, base)
                starts += s
                waits += w
                i_in, i_out, base = i_in + ni, i_out + no, base + op.n_sems()
            return starts, waits

        def run_starts():
            for start in copies()[0]:
                start()

        def run_waits():
            for wait in copies()[1]:
                wait()

        if comm and grid:
            first = last = True
            for d, n in enumerate(grid):
                first = jnp.logical_and(first, pl.program_id(d) == 0)
                last = jnp.logical_and(last, pl.program_id(d) == n - 1)
            pl.when(first)(run_starts)
        elif comm:
            run_starts()
        if body is not None:
            body(*ins, *outs, *scr)
        if comm and grid:
            pl.when(last)(run_waits)
        elif comm:
            run_waits()

    res = pl.pallas_call(
        wrapped,
        name=name,
        grid=grid,
        in_specs=list(in_specs) + [ANY] * len(c_in),
        out_specs=list(out_specs) + [ANY] * len(c_out),
        out_shape=list(out_shape) + c_out,
        scratch_shapes=list(scratch) + ([pltpu.SemaphoreType.DMA((n_sems,))] if comm else []),
        input_output_aliases=aliases,
        compiler_params=pltpu.CompilerParams(dimension_semantics=("arbitrary",) * len(grid), vmem_limit_bytes=VMEM_LIMIT),
    )(*args, *c_in)
    return tuple(res[:n_out]), tuple(res[n_out:])


def _gather_now(fulls, hows, splits):
    n = len(fulls)
    ici = _GatherIci(fulls, hows, splits)
    split_ids = [a for a in range(n) if splits[a]]
    d2d = _GatherD2d([fulls[a] for a in split_ids], [hows[a] for a in split_ids])

    def body(*refs):
        outs, sems = refs[n : 2 * n], refs[2 * n]
        for op, op_refs, base in ((ici, outs, 0), (d2d, [outs[a] for a in split_ids], ici.n_sems())):
            starts, waits = op.build(None, op_refs, sems, base)
            for start in starts:
                start()
            for wait in waits:
                wait()

    return pl.pallas_call(
        body,
        name="gather_first",
        in_specs=[ANY] * n,
        out_specs=[ANY] * n,
        out_shape=ici.out_shapes(),
        scratch_shapes=[pltpu.SemaphoreType.DMA((ici.n_sems() + d2d.n_sems(),))],
        input_output_aliases={a: a for a in range(n)},
    )(*fulls)


def _place_weights(shards, hows, full_shapes, dtypes, transposed):
    n = len(shards)

    def body(*refs):
        ins, outs, bufs, sems = refs[:n], refs[n : 2 * n], refs[2 * n : 3 * n], refs[3 * n]
        x, y, _ = _place()
        copies = []
        for a in range(n):
            val = ins[a][...].T if transposed[a] else ins[a][...]
            bufs[a][...] = val.astype(dtypes[a])
            cp = pltpu.make_async_copy(bufs[a], _window(outs[a], hows[a], 2 * x + y), sems.at[a])
            cp.start()
            copies.append(cp)
        for cp in copies:
            cp.wait()

    return pl.pallas_call(
        body,
        name="place_weights",
        in_specs=[pl.BlockSpec(memory_space=pltpu.VMEM)] * n,
        out_specs=[ANY] * n,
        out_shape=[jax.ShapeDtypeStruct(s, d) for s, d in zip(full_shapes, dtypes)],
        scratch_shapes=[pltpu.VMEM(a.shape[::-1] if t else a.shape, d) for a, d, t in zip(shards, dtypes, transposed)]
        + [pltpu.SemaphoreType.DMA((n,))],
        compiler_params=pltpu.CompilerParams(vmem_limit_bytes=VMEM_LIMIT),
    )(*shards)


def _in_proj(x, g_mix, w_inT_b, b_in, comm=()):
    T, D = x.shape
    CI = w_inT_b.shape[0]
    tm = _tile(T, 512)

    def body(x_ref, g_ref, w_ref, b_ref, z_ref, xn_ref):
        xv = x_ref[...]
        r = lax.rsqrt(jnp.mean(xv * xv, axis=-1, keepdims=True) + RMS_EPS)
        xn = (xv * r * g_ref[...]).astype(BF16)
        xn_ref[...] = xn
        z_ref[...] = _dot(xn, w_ref[...], NT) + b_ref[...]

    return _call(
        "in_proj",
        body,
        (T // tm,),
        [
            pl.BlockSpec((tm, D), lambda i: (i, 0)),
            pl.BlockSpec((1, D), lambda i: (0, 0)),
            pl.BlockSpec((CI, D), lambda i: (0, 0)),
            pl.BlockSpec((1, CI), lambda i: (0, 0)),
        ],
        [pl.BlockSpec((tm, CI), lambda i: (i, 0)), pl.BlockSpec((tm, D), lambda i: (i, 0))],
        [jax.ShapeDtypeStruct((T, CI), F32), jax.ShapeDtypeStruct((T, D), BF16)],
        (x, g_mix, w_inT_b, b_in),
        comm=comm,
    )


def _fill_shifted(scr):
    n = scr.shape[1] - SUBLANES
    for s in range(1, SUBLANES):
        scr[s, 0:n, :] = scr[0, s : s + n, :]


def _shifted_rows(scr, off, n, cs):
    s = off % SUBLANES
    return scr[s, off - s : off - s + n, cs]


def _pool_mean_minus_token(p_scr, cs, w, cnt, tt):
    tok = p_scr[HALO : HALO + tt, cs]
    s = tok
    for d in range(1, w):
        s = s + p_scr[HALO - d : HALO - d + tt, cs]
    return s / cnt - tok


def _seq_fwd(z, w_dw4, b_dw, ln_g, ln_b, w_pool_b, s_pool, comm=()):
    T, CI = z.shape
    CC = ln_g.shape[1]
    n_grp, G = w_pool_b.shape[0], w_pool_b.shape[-1]
    KW = w_dw4.shape[1]
    D = CC + n_grp * G
    tt = _tile(T, 512, HALO)
    per = tt // HALO

    def body(zc_ref, zp_ref, wdw_ref, bdw_ref, lng_ref, lnb_ref, wp_ref, sp_ref, y_ref, v_ref, u_scr, p_scr):
        i = pl.program_id(0)
        first = i == 0
        u_prev = zp_ref[:, 0:CC] * _sigmoid(zp_ref[:, CC : 2 * CC])
        u_scr[0, 0:HALO, :] = jnp.where(first, 0.0, u_prev)
        p_scr[0:HALO, :] = jnp.where(first, 0.0, zp_ref[:, 2 * CC :])
        u_scr[0, HALO:, :] = zc_ref[:, 0:CC] * _sigmoid(zc_ref[:, CC : 2 * CC])
        p_scr[HALO:, :] = zc_ref[:, 2 * CC :]
        _fill_shifted(u_scr)

        for j in range(CC // LANES):
            cs = slice(LANES * j, LANES * (j + 1))
            for rb in range(tt // CONV_ROWS):
                acc = jnp.zeros((CONV_ROWS, LANES), F32)
                for k in range(KW):
                    off = HALO - (KW - 1) + k + rb * CONV_ROWS
                    acc = acc + _shifted_rows(u_scr, off, CONV_ROWS, cs) * wdw_ref[j, k : k + 1, :]
                v_ref[rb * CONV_ROWS : (rb + 1) * CONV_ROWS, cs] = acc + bdw_ref[:, cs]

        v = v_ref[...]
        mu = jnp.mean(v, axis=-1, keepdims=True)
        d = v - mu
        var = jnp.mean(d * d, axis=-1, keepdims=True)
        ln = d * lax.rsqrt(var + LN_EPS) * lng_ref[...] + lnb_ref[...]
        y_ref[:, 0:CC] = (ln * _sigmoid(ln)).astype(BF16)

        tpos = i * tt + lax.broadcasted_iota(jnp.int32, (tt, 1), 0)
        for gi, w in enumerate(POOL_WINDOWS):
            cs = slice(G * gi, G * (gi + 1))
            cnt = jnp.minimum(tpos + 1, w).astype(F32)
            yi = _pool_mean_minus_token(p_scr, cs, w, cnt, tt)
            q = _dot(yi.astype(BF16), wp_ref[gi], NN)
            y_ref[:, CC + G * gi : CC + G * (gi + 1)] = (q * sp_ref[:, cs]).astype(BF16)

    const2 = lambda i: (0, 0)
    return _call(
        "seq_fwd",
        body,
        (T // tt,),
        [
            pl.BlockSpec((tt, CI), lambda i: (i, 0)),
            pl.BlockSpec((HALO, CI), lambda i: (jnp.maximum(i * per - 1, 0), 0)),
            pl.BlockSpec(w_dw4.shape, lambda i: (0, 0, 0)),
            pl.BlockSpec((1, CC), const2),
            pl.BlockSpec((1, CC), const2),
            pl.BlockSpec((1, CC), const2),
            pl.BlockSpec(w_pool_b.shape, lambda i: (0, 0, 0)),
            pl.BlockSpec((1, n_grp * G), const2),
        ],
        [pl.BlockSpec((tt, D), lambda i: (i, 0)), pl.BlockSpec((tt, CC), lambda i: (i, 0))],
        [jax.ShapeDtypeStruct((T, D), BF16), jax.ShapeDtypeStruct((T, CC), F32)],
        (z, z, w_dw4, b_dw, ln_g, ln_b, w_pool_b, s_pool),
        scratch=[pltpu.VMEM((SUBLANES, HALO + tt, CC), F32), pltpu.VMEM((HALO + tt, n_grp * G), F32)],
        comm=comm,
    )


def _out_proj(y_b, x, w_out_b, g_ffn, comm=()):
    T, D = x.shape
    tm = _tile(T, 512)

    def body(y_ref, x_ref, w_ref, g_ref, h1_ref, hn_ref):
        h1 = x_ref[...] + _dot(y_ref[...], w_ref[...], NN)
        h1_ref[...] = h1
        r = lax.rsqrt(jnp.mean(h1 * h1, axis=-1, keepdims=True) + RMS_EPS)
        hn_ref[...] = (h1 * r * g_ref[...]).astype(BF16)

    row = lambda i: (i, 0)
    return _call(
        "out_proj",
        body,
        (T // tm,),
        [
            pl.BlockSpec((tm, y_b.shape[1]), row),
            pl.BlockSpec((tm, D), row),
            pl.BlockSpec(w_out_b.shape, lambda i: (0, 0)),
            pl.BlockSpec((1, D), lambda i: (0, 0)),
        ],
        [pl.BlockSpec((tm, D), row), pl.BlockSpec((tm, D), row)],
        [jax.ShapeDtypeStruct((T, D), F32), jax.ShapeDtypeStruct((T, D), BF16)],
        (y_b, x, w_out_b, g_ffn),
        comm=comm,
    )


def _hidden_tile(F):
    return _tile(F, 1408, LANES)


def _gate_up(hn_b, wgT_b, wuT_b, comm=()):
    T, D = hn_b.shape
    F = wgT_b.shape[0]
    tm, tf = _tile(T, 1024), _hidden_tile(F)

    def body(hn_ref, wg_ref, wu_ref, g_ref, u_ref, a_ref):
        hn = hn_ref[...]
        for c0 in range(0, tf, HIDDEN_CHUNK):
            cs = slice(c0, min(c0 + HIDDEN_CHUNK, tf))
            gv = _dot(hn, wg_ref[cs, :], NT)
            uv = _dot(hn, wu_ref[cs, :], NT)
            g_ref[:, cs] = gv.astype(BF16)
            u_ref[:, cs] = uv.astype(BF16)
            a_ref[:, cs] = (gv * _sigmoid(gv) * uv).astype(BF16)

    wspec = pl.BlockSpec((tf, D), lambda j, i: (j, 0))
    ospec = pl.BlockSpec((tm, tf), lambda j, i: (i, j))
    return _call(
        "gate_up",
        body,
        (F // tf, T // tm),
        [pl.BlockSpec((tm, D), lambda j, i: (i, 0)), wspec, wspec],
        [ospec, ospec, ospec],
        [jax.ShapeDtypeStruct((T, F), BF16)] * 3,
        (hn_b, wgT_b, wuT_b),
        comm=comm,
    )


def _down_loss(a_b, wd_b, h1, target, g_final, comm=()):
    T, D = h1.shape
    F = a_b.shape[1]
    tm = _tile(T, 512)
    nt = T // tm

    def body(a_ref, w_ref, h1_ref, t_ref, g_ref, dh2_ref, dh2b_ref, loss_ref, dg_ref):
        i = pl.program_id(0)
        h2 = h1_ref[...] + _dot(a_ref[...], w_ref[...], NN)
        r = lax.rsqrt(jnp.mean(h2 * h2, axis=-1, keepdims=True) + RMS_EPS)
        g = g_ref[...]
        diff = h2 * r * g - t_ref[...]
        _accumulate(loss_ref, i == 0, jnp.full(loss_ref.shape, jnp.sum(diff * diff) * (0.5 / D), F32))
        dh2, dg_rows = _rms_bwd(h2, g, diff * (1.0 / D))
        dh2_ref[...] = dh2
        dh2b_ref[...] = dh2.astype(BF16)
        _accumulate(dg_ref, i == 0, jnp.sum(dg_rows, axis=0, keepdims=True))

    row = lambda i: (i, 0)
    return _call(
        "down_loss",
        body,
        (nt,),
        [
            pl.BlockSpec((tm, F), row),
            pl.BlockSpec((F, D), lambda i: (0, 0), pipeline_mode=pl.Buffered(1)),
            pl.BlockSpec((tm, D), row),
            pl.BlockSpec((tm, D), row),
            pl.BlockSpec((1, D), lambda i: (0, 0)),
        ],
        [
            pl.BlockSpec((tm, D), row),
            pl.BlockSpec((tm, D), row),
            pl.BlockSpec((1, LANES), lambda i: (0, 0)),
            pl.BlockSpec((1, D), lambda i: (0, 0)),
        ],
        [
            jax.ShapeDtypeStruct((T, D), F32),
            jax.ShapeDtypeStruct((T, D), BF16),
            jax.ShapeDtypeStruct((1, LANES), F32),
            jax.ShapeDtypeStruct((1, D), F32),
        ],
        (a_b, wd_b, h1, target, g_final),
        comm=comm,
    )


def _ffn_bwd_act(dh2_b, wd_b, g_b, u_b, comm=()):
    T, D = dh2_b.shape
    F = wd_b.shape[0]
    tm, tf = _tile(T, 1024), _hidden_tile(F)

    def body(d_ref, w_ref, g_ref, u_ref, dg_ref, du_ref):
        d = d_ref[...]
        for c0 in range(0, tf, HIDDEN_CHUNK):
            cs = slice(c0, min(c0 + HIDDEN_CHUNK, tf))
            da = _dot(d, w_ref[cs, :], NT)
            gv = g_ref[:, cs].astype(F32)
            uv = u_ref[:, cs].astype(F32)
            sg = _sigmoid(gv)
            silu = gv * sg
            dg_ref[:, cs] = (da * uv * (sg * (1.0 + gv * (1.0 - sg)))).astype(BF16)
            du_ref[:, cs] = (da * silu).astype(BF16)

    aspec = pl.BlockSpec((tm, tf), lambda j, i: (i, j))
    return _call(
        "ffn_bwd_act",
        body,
        (F // tf, T // tm),
        [pl.BlockSpec((tm, D), lambda j, i: (i, 0)), pl.BlockSpec((tf, D), lambda j, i: (j, 0)), aspec, aspec],
        [aspec, aspec],
        [jax.ShapeDtypeStruct((T, F), BF16)] * 2,
        (dh2_b, wd_b, g_b, u_b),
        comm=comm,
    )


def _ffn_bwd_in(dg_b, du_b, wgT_b, wuT_b, h1, dh2, g_ffn, w_out_b, comm=()):
    T, D = h1.shape
    F = wgT_b.shape[0]
    DM = w_out_b.shape[0]
    tm = _tile(T, 512)

    def body(dg_ref, du_ref, wg_ref, wu_ref, h1_ref, dh2_ref, g_ref, wo_ref, dh1_ref, dh1b_ref, dy_ref, dgf_ref):
        i = pl.program_id(0)
        dhn = _dot(dg_ref[...], wg_ref[...], NN) + _dot(du_ref[...], wu_ref[...], NN)
        dx, dg_rows = _rms_bwd(h1_ref[...], g_ref[...], dhn)
        dh1 = dh2_ref[...] + dx
        dh1b = dh1.astype(BF16)
        dh1_ref[...] = dh1
        dh1b_ref[...] = dh1b
        dy_ref[...] = _dot(dh1b, wo_ref[...], NT)
        _accumulate(dgf_ref, i == 0, jnp.sum(dg_rows, axis=0, keepdims=True))

    row = lambda i: (i, 0)
    const = lambda i: (0, 0)
    return _call(
        "ffn_bwd_in",
        body,
        (T // tm,),
        [
            pl.BlockSpec((tm, F), row),
            pl.BlockSpec((tm, F), row),
            pl.BlockSpec((F, D), const, pipeline_mode=pl.Buffered(1)),
            pl.BlockSpec((F, D), const, pipeline_mode=pl.Buffered(1)),
            pl.BlockSpec((tm, D), row),
            pl.BlockSpec((tm, D), row),
            pl.BlockSpec((1, D), const),
            pl.BlockSpec((DM, D), const, pipeline_mode=pl.Buffered(1)),
        ],
        [pl.BlockSpec((tm, D), row), pl.BlockSpec((tm, D), row), pl.BlockSpec((tm, DM), row), pl.BlockSpec((1, D), const)],
        [
            jax.ShapeDtypeStruct((T, D), F32),
            jax.ShapeDtypeStruct((T, D), BF16),
            jax.ShapeDtypeStruct((T, DM), F32),
            jax.ShapeDtypeStruct((1, D), F32),
        ],
        (dg_b, du_b, wgT_b, wuT_b, h1, dh2, g_ffn, w_out_b),
        comm=comm,
    )


def _seq_bwd(z, dy, v, w_dw4, ln_g, ln_b, w_pool_b, s_pool, comm=()):
    T, CI = z.shape
    CC = ln_g.shape[1]
    n_grp, G = w_pool_b.shape[0], w_pool_b.shape[-1]
    CP = n_grp * G
    KW = w_dw4.shape[1]
    n_cc = CC // LANES
    D = CC + CP
    tt = _tile(T, 512, HALO)
    per = tt // HALO
    n_tiles = T // tt
    last_halo = T // HALO - 1

    def body(zc_ref, zp_ref, dyc_ref, dyn_ref, vc_ref, vn_ref, wdw_ref, lng_ref, lnb_ref, wp_ref, sp_ref,
             dz_ref, dwdw_ref, dbdw_ref, dlng_ref, dlnb_ref, dwp_ref, dsp_ref, dbin_ref,
             dv_scr, u_scr, p_scr, g_scr, dw_scr):
        i = pl.program_id(0)
        first = i == 0
        last = i == n_tiles - 1
        lng, lnb = lng_ref[...], lnb_ref[...]

        def conv_pre(vv, dyc):
            mu = jnp.mean(vv, axis=-1, keepdims=True)
            d = vv - mu
            rs = lax.rsqrt(jnp.mean(d * d, axis=-1, keepdims=True) + LN_EPS)
            xh = d * rs
            ln = xh * lng + lnb
            sg = _sigmoid(ln)
            dln = dyc * (sg * (1.0 + ln * (1.0 - sg)))
            dxh = dln * lng
            dv = rs * (dxh - jnp.mean(dxh, axis=-1, keepdims=True) - xh * jnp.mean(dxh * xh, axis=-1, keepdims=True))
            return dv, dln, xh

        dv_c, dln_c, xh_c = conv_pre(vc_ref[...], dyc_ref[:, 0:CC])
        dv_scr[0, 0:tt, :] = dv_c
        dv_n, _, _ = conv_pre(vn_ref[...], dyn_ref[:, 0:CC])
        dv_scr[0, tt:, :] = jnp.where(last, 0.0, dv_n)
        _fill_shifted(dv_scr)
        _accumulate(dlng_ref, first, jnp.sum(dln_c * xh_c, axis=0, keepdims=True))
        _accumulate(dlnb_ref, first, jnp.sum(dln_c, axis=0, keepdims=True))
        _accumulate(dbdw_ref, first, jnp.sum(dv_c, axis=0, keepdims=True))

        u_scr[...] = zc_ref[:, 0:CC] * _sigmoid(zc_ref[:, CC : 2 * CC])

        @pl.when(first)
        def _():
            dw_scr[...] = jnp.zeros_like(dw_scr)

        for j in range(n_cc):
            cs = slice(LANES * j, LANES * (j + 1))
            gs = slice(CC + LANES * j, CC + LANES * (j + 1))
            dbin_a = jnp.zeros((1, LANES), F32)
            dbin_g = jnp.zeros((1, LANES), F32)
            for rb in range(tt // CONV_ROWS):
                rows = slice(rb * CONV_ROWS, (rb + 1) * CONV_ROWS)
                u_blk = u_scr[rows, cs]
                du = jnp.zeros((CONV_ROWS, LANES), F32)
                for k in range(KW):
                    off = rb * CONV_ROWS + (KW - 1) - k
                    d = _shifted_rows(dv_scr, off, CONV_ROWS, cs)
                    du = du + d * wdw_ref[j, k : k + 1, :]
                    dw_scr[j * HALO + k] += jnp.sum((u_blk * d).reshape(CONV_ROWS // 8, 8, LANES), axis=0)
                a = zc_ref[rows, cs]
                sg = _sigmoid(zc_ref[rows, gs])
                da = du * sg
                dgate = du * a * sg * (1.0 - sg)
                dz_ref[rows, cs] = da.astype(BF16)
                dz_ref[rows, gs] = dgate.astype(BF16)
                dbin_a = dbin_a + jnp.sum(da, axis=0, keepdims=True)
                dbin_g = dbin_g + jnp.sum(dgate, axis=0, keepdims=True)
            _accumulate(dbin_ref.at[:, cs], first, dbin_a)
            _accumulate(dbin_ref.at[:, gs], first, dbin_g)

        @pl.when(last)
        def _():
            dwdw_ref[...] = jnp.sum(dw_scr[...], axis=1).reshape(dwdw_ref.shape)

        p_scr[0:HALO, :] = jnp.where(first, 0.0, zp_ref[:, 2 * CC :])
        p_scr[HALO:, :] = zc_ref[:, 2 * CC :]
        tpos = i * tt + lax.broadcasted_iota(jnp.int32, (tt, 1), 0)
        for gi, w in enumerate(POOL_WINDOWS):
            cs = slice(G * gi, G * (gi + 1))
            ys = slice(CC + G * gi, CC + G * (gi + 1))
            ps = slice(2 * CC + G * gi, 2 * CC + G * (gi + 1))
            cnt = jnp.minimum(tpos + 1, w).astype(F32)
            yib = _pool_mean_minus_token(p_scr, cs, w, cnt, tt).astype(BF16)
            wp = wp_ref[gi]
            sp = sp_ref[:, cs]
            dyp = dyc_ref[:, ys]
            q = _dot(yib, wp, NN)
            _accumulate(dsp_ref.at[:, cs], first, jnp.sum(dyp * q, axis=0, keepdims=True))
            dq_c = (dyp * sp).astype(BF16)
            dq_n = (jnp.where(last, 0.0, dyn_ref[:, ys]) * sp).astype(BF16)
            _accumulate(dwp_ref.at[gi], first, _dot(yib, dq_c, TN))
            dyi_c = _dot(dq_c, wp, NT)
            g_scr[0:tt, cs] = dyi_c / cnt
            g_scr[tt:, cs] = _dot(dq_n, wp, NT) * (1.0 / w)
            dp = -dyi_c
            for d in range(w):
                dp = dp + g_scr[d : d + tt, cs]
            dz_ref[:, ps] = dp.astype(BF16)
            _accumulate(dbin_ref.at[:, ps], first, jnp.sum(dp, axis=0, keepdims=True))

    cur = lambda i: (i, 0)
    prev = lambda i: (jnp.maximum(i * per - 1, 0), 0)
    nxt = lambda i: (jnp.minimum((i + 1) * per, last_halo), 0)
    c2 = lambda i: (0, 0)
    c3 = lambda i: (0, 0, 0)
    return _call(
        "seq_bwd",
        body,
        (n_tiles,),
        [
            pl.BlockSpec((tt, CI), cur),
            pl.BlockSpec((HALO, CI), prev),
            pl.BlockSpec((tt, D), cur),
            pl.BlockSpec((HALO, D), nxt),
            pl.BlockSpec((tt, CC), cur),
            pl.BlockSpec((HALO, CC), nxt),
            pl.BlockSpec(w_dw4.shape, c3),
            pl.BlockSpec((1, CC), c2),
            pl.BlockSpec((1, CC), c2),
            pl.BlockSpec(w_pool_b.shape, c3),
            pl.BlockSpec((1, CP), c2),
        ],
        [
            pl.BlockSpec((tt, CI), cur),
            pl.BlockSpec((n_cc, HALO, LANES), c3),
            pl.BlockSpec((1, CC), c2),
            pl.BlockSpec((1, CC), c2),
            pl.BlockSpec((1, CC), c2),
            pl.BlockSpec((n_grp, G, G), c3),
            pl.BlockSpec((1, CP), c2),
            pl.BlockSpec((1, CI), c2),
        ],
        [
            jax.ShapeDtypeStruct((T, CI), BF16),
            jax.ShapeDtypeStruct((n_cc, HALO, LANES), F32),
            jax.ShapeDtypeStruct((1, CC), F32),
            jax.ShapeDtypeStruct((1, CC), F32),
            jax.ShapeDtypeStruct((1, CC), F32),
            jax.ShapeDtypeStruct((n_grp, G, G), F32),
            jax.ShapeDtypeStruct((1, CP), F32),
            jax.ShapeDtypeStruct((1, CI), F32),
        ],
        (z, z, dy, dy, v, v, w_dw4, ln_g, ln_b, w_pool_b, s_pool),
        scratch=[
            pltpu.VMEM((SUBLANES, tt + HALO, CC), F32),
            pltpu.VMEM((tt, CC), F32),
            pltpu.VMEM((HALO + tt, CP), F32),
            pltpu.VMEM((tt + HALO, CP), F32),
            pltpu.VMEM((n_cc * HALO, 8, LANES), F32),
        ],
        comm=comm,
    )


def _in_proj_bwd(dz_b, w_inT_b, x, dh1, g_mix, comm=()):
    T, D = x.shape
    CI = w_inT_b.shape[0]
    tm = _tile(T, 512)

    def body(dz_ref, w_ref, x_ref, dh1_ref, g_ref, dx_ref, dg_ref):
        i = pl.program_id(0)
        dxn = _dot(dz_ref[...], w_ref[...], NN)
        dx, dg_rows = _rms_bwd(x_ref[...], g_ref[...], dxn)
        dx_ref[...] = dh1_ref[...] + dx
        _accumulate(dg_ref, i == 0, jnp.sum(dg_rows, axis=0, keepdims=True))

    row = lambda i: (i, 0)
    const = lambda i: (0, 0)
    return _call(
        "in_proj_bwd",
        body,
        (T // tm,),
        [
            pl.BlockSpec((tm, CI), row),
            pl.BlockSpec((CI, D), const),
            pl.BlockSpec((tm, D), row),
            pl.BlockSpec((tm, D), row),
            pl.BlockSpec((1, D), const),
        ],
        [pl.BlockSpec((tm, D), row), pl.BlockSpec((1, D), const)],
        [jax.ShapeDtypeStruct((T, D), F32), jax.ShapeDtypeStruct((1, D), F32)],
        (dz_b, w_inT_b, x, dh1, g_mix),
        comm=comm,
    )


def _weight_grad(name, a_b, b_b, comm=()):
    T, N1 = a_b.shape
    N2 = b_b.shape[1]
    t1 = _tile(N1, 1408, LANES)
    tk = _tile(T, 2048)
    nk = T // tk

    def body(a_ref, b_ref, o_ref, acc):
        k = pl.program_id(1)
        _accumulate(acc, k == 0, _dot(a_ref[...], b_ref[...], TN))

        @pl.when(k == nk - 1)
        def _():
            o_ref[...] = acc[...].astype(BF16)

    (out,), rest = _call(
        name,
        body,
        (N1 // t1, nk),
        [pl.BlockSpec((tk, t1), lambda n, k: (k, n)), pl.BlockSpec((tk, N2), lambda n, k: (k, 0))],
        [pl.BlockSpec((t1, N2), lambda n, k: (n, 0))],
        [jax.ShapeDtypeStruct((N1, N2), BF16)],
        (a_b, b_b),
        scratch=[pltpu.VMEM((t1, N2), F32)],
        comm=comm,
    )
    return out, rest


def _sum_parts(name, full, how, parts, me):
    _, R, C = parts[0].shape
    tr = _tile(R, 512)
    nb = R // tr
    where = [(q, r) for q, p in enumerate(parts) for r in range(p.shape[0])]
    assert len(where) == 3

    def body(me_ref, own_ref, *refs):
        o_ref = refs[-1]
        f = lambda j: refs[where[j][0]][where[j][1]].astype(F32)
        o_ref[...] = (own_ref[...].astype(F32) + f(0)) + (f(1) + f(2))

    own_map = {"rows": lambda i, me_ref: (me_ref[0] * nb + i, 0), "cols": lambda i, me_ref: (i, me_ref[0]),
               "all": lambda i, me_ref: (i, 0)}[how]
    return pl.pallas_call(
        body,
        name=name,
        grid_spec=pltpu.PrefetchScalarGridSpec(
            num_scalar_prefetch=1,
            grid=(nb,),
            in_specs=[pl.BlockSpec((tr, C), own_map)]
            + [pl.BlockSpec((p.shape[0], tr, C), lambda i, me_ref: (0, i, 0)) for p in parts],
            out_specs=pl.BlockSpec((tr, C), lambda i, me_ref: (i, 0)),
        ),
        out_shape=jax.ShapeDtypeStruct((R, C), F32),
        compiler_params=pltpu.CompilerParams(dimension_semantics=("arbitrary",), vmem_limit_bytes=VMEM_LIMIT),
    )(me, full, *parts)


_M_CORR = 1.0 - ADAM_B1**ADAM_STEP
_V_CORR = 1.0 - ADAM_B2**ADAM_STEP


def _adamw_math(w, g, m, v):
    m = ADAM_B1 * m + (1.0 - ADAM_B1) * g
    v = ADAM_B2 * v + (1.0 - ADAM_B2) * (g * g)
    delta = -ADAM_LR * ((m / _M_CORR) / (jnp.sqrt(v / _V_CORR) + ADAM_EPS) + ADAM_WD * w)
    return delta, m, v


def _adamw(name, w, m, v, g_here, g_there, g_transposed=False, comm=()):
    R, C = w.shape
    tr = _tile(R, 256, LANES if g_transposed else 8)

    def body(w_ref, m_ref, v_ref, ga_ref, gb_ref, g_ref, d_ref, nm_ref, nv_ref):
        g = ga_ref[...] + gb_ref[...]
        if g_transposed:
            g = g.T
        g_ref[...] = g
        d_ref[...], nm_ref[...], nv_ref[...] = _adamw_math(w_ref[...], g, m_ref[...], v_ref[...])

    spec = pl.BlockSpec((tr, C), lambda i: (i, 0))
    gspec = pl.BlockSpec((C, tr), lambda i: (0, i)) if g_transposed else spec
    return _call(name, body, (R // tr,), [spec] * 3 + [gspec] * 2, [spec] * 4, [jax.ShapeDtypeStruct((R, C), F32)] * 4,
                 (w, m, v, g_here, g_there), comm=comm)


class _PackLayout:
    def __init__(self, n_cc, n_grp, G, widths):
        self.dw_rows = (0, HALO)
        self.wp_rows = (HALO, HALO + G)
        self.n_cc, self.n_grp, self.G = n_cc, n_grp, G
        self.vec = {}
        r = HALO + G
        for name, width in widths:
            self.vec[name] = (r, width)
            r += width // PACK_W
        self.rows = -(-r // 8) * 8


def _pack_small(layout, dwdw, dwp, vecs):
    names = list(vecs)

    def body(*refs):
        dw_ref, wp_ref = refs[0], refs[1]
        vec_refs = refs[2 : 2 + len(names)]
        o_ref = refs[-1]
        o_ref[...] = jnp.zeros_like(o_ref)
        for j in range(layout.n_cc):
            o_ref[layout.dw_rows[0] : layout.dw_rows[1], j * LANES : (j + 1) * LANES] = dw_ref[j]
        for i in range(layout.n_grp):
            o_ref[layout.wp_rows[0] : layout.wp_rows[1], i * layout.G : (i + 1) * layout.G] = wp_ref[i]
        for name, ref in zip(names, vec_refs):
            r, width = layout.vec[name]
            for h in range(width // PACK_W):
                o_ref[r + h : r + h + 1, :] = ref[:, h * PACK_W : (h + 1) * PACK_W]

    return pl.pallas_call(
        body,
        name="pack_small",
        out_shape=jax.ShapeDtypeStruct((layout.rows, PACK_W), F32),
    )(dwdw, dwp, *[vecs[k] for k in names])


def _adamw_small(layout, g_here, g_there, w_dw, m_dw, v_dw, w_pool, m_pool, v_pool, vec_w, vec_m, vec_v):
    names = list(vec_w)
    nv = len(names)

    def body(*refs):
        ga_ref, gb_ref = refs[0], refs[1]
        wdw, mdw, vdw, wp, mp, vp = refs[2:8]
        vw, vm, vv = refs[8 : 8 + nv], refs[8 + nv : 8 + 2 * nv], refs[8 + 2 * nv : 8 + 3 * nv]
        outs = refs[8 + 3 * nv :]
        acc = outs[-1]
        acc[...] = ga_ref[...] + gb_ref[...]

        def emit(o, g, w, m, v, idx=()):
            res = (g,) + _adamw_math(w, g, m, v)
            for ref, val in zip(o, res):
                ref[idx] = val

        me = 2 * lax.axis_index("x") + lax.axis_index("y")
        for j in range(layout.n_cc):

            @pl.when(me == j)
            def _(j=j):
                g = acc[layout.dw_rows[0] : layout.dw_rows[1], j * LANES : (j + 1) * LANES]
                emit(outs[0:4], g, wdw[...], mdw[...], vdw[...], idx=...)

        for i in range(layout.n_grp):
            g = acc[layout.wp_rows[0] : layout.wp_rows[1], i * layout.G : (i + 1) * layout.G]
            emit(outs[4:8], g, wp[i], mp[i], vp[i], idx=i)
        for q, name in enumerate(names):
            r, width = layout.vec[name]
            for h in range(width // PACK_W):
                ls = slice(h * PACK_W, (h + 1) * PACK_W)
                g = acc[r + h : r + h + 1, :]
                emit(outs[8 + 4 * q : 12 + 4 * q], g, vw[q][:, ls], vm[q][:, ls], vv[q][:, ls], idx=(slice(None), ls))

    shapes = [w_dw.shape] * 4 + [w_pool.shape] * 4
    for name in names:
        shapes += [vec_w[name].shape] * 4
    return pl.pallas_call(
        body,
        name="adamw_small",
        out_shape=[jax.ShapeDtypeStruct(s, F32) for s in shapes],
        scratch_shapes=[pltpu.VMEM(g_here.shape, F32)],
    )(g_here, g_there, w_dw, m_dw, v_dw, w_pool, m_pool, v_pool,
      *[vec_w[k] for k in names], *[vec_m[k] for k in names], *[vec_v[k] for k in names])


def _allreduce_adamw_row(g_part, w, m, v, loss_part, comm=()):
    D = w.shape[1]
    n_pairs = N_DEV - 1

    def body(g_ref, w_ref, m_ref, v_ref, l_ref, go_ref, d_ref, nm_ref, nv_ref, lo_ref, land_g, land_l, sems):
        x, y, c = _place()
        copies = []
        for q, (src, land) in enumerate(((g_ref, land_g), (l_ref, land_l))):
            for r in range(1, N_DEV):
                fx, fy, fc = (r >> 2) & 1, (r >> 1) & 1, r & 1
                peer = (1 - x if fx else x, 1 - y if fy else y, 1 - c if fc else c)
                cp = _remote(src, land.at[r], sems, 2 * (q * n_pairs + r - 1), peer)
                cp.start()
                copies.append(cp)
        for cp in copies:
            cp.wait()

        def total(src, land):
            row = lambda r: src[...] if r == 0 else land[r]
            return ((row(0) + row(4)) + (row(2) + row(6))) + ((row(1) + row(5)) + (row(3) + row(7)))

        g = total(g_ref, land_g)
        go_ref[...] = g
        d_ref[...], nm_ref[...], nv_ref[...] = _adamw_math(w_ref[...], g, m_ref[...], v_ref[...])
        lo_ref[...] = total(l_ref, land_l)

    vm = pl.BlockSpec(memory_space=pltpu.VMEM)
    return _call(
        "allreduce_adamw_g_mix",
        body,
        (),
        [vm] * 5,
        [vm] * 5,
        [jax.ShapeDtypeStruct((1, D), F32)] * 4 + [jax.ShapeDtypeStruct(loss_part.shape, F32)],
        (g_part, w, m, v, loss_part),
        scratch=[pltpu.VMEM((N_DEV, 1, D), F32), pltpu.VMEM((N_DEV,) + loss_part.shape, F32),
                 pltpu.SemaphoreType.DMA((4 * n_pairs,))],
        comm=comm,
    )


def kernel(x, g_mix, w_in, b_in, w_dw, b_dw, ln_g, ln_b, w_pool, s_pool, w_out, g_ffn, w_gate, w_up, w_down, g_final, loss_target, m_g_mix, m_w_in, m_b_in, m_w_dw, m_b_dw, m_ln_g, m_ln_b, m_w_pool, m_s_pool, m_w_out, m_g_ffn, m_w_gate, m_w_up, m_w_down, m_g_final, v_g_mix, v_w_in, v_b_in, v_w_dw, v_b_dw, v_ln_g, v_ln_b, v_w_pool, v_s_pool, v_w_out, v_g_ffn, v_w_gate, v_w_up, v_w_down, v_g_final):
    x2 = x[0]
    target = loss_target[0]
    T, D = x2.shape
    w_in2, w_out2, w_down2, w_dw2 = w_in[0], w_out[0], w_down[0], w_dw[0]
    w_gateT, w_upT = w_gate[0].T, w_up[0].T
    CI = w_in2.shape[1] * N_CHIPS
    DM = w_out2.shape[0] * N_CHIPS
    F = w_down2.shape[0] * N_CHIPS
    KW, dw_cols = w_dw2.shape
    assert dw_cols == LANES
    n_grp, G = w_pool.shape[1], w_pool.shape[-1]
    w_pool3 = w_pool[0]
    g_final2 = g_final.reshape(1, D)

    me = (2 * lax.axis_index("x") + lax.axis_index("y")).astype(jnp.int32).reshape(1)

    f_in, f_out, f_gate, f_up, f_down, f_dw = _place_weights(
        [w_in2, w_out2, w_gateT, w_upT, w_down2, w_dw2], ["rows"] * 5 + ["lead"],
        [(CI, D), (DM, D), (F, D), (F, D), (F, D), (N_CHIPS, KW, dw_cols)], [BF16] * 5 + [F32], [True] + [False] * 5)
    w_inT_b, w_dw4 = _gather_now([f_in, f_dw], ["rows", "lead"], [True, False])
    w_pool_b = w_pool3.astype(BF16)
    (z, xn_b), (f_out, f_gate) = _in_proj(
        x2, g_mix, w_inT_b, b_in,
        comm=[_GatherIci([f_out], ["rows"], [True]), _GatherIci([f_gate], ["rows"], [True], which=(2,))])
    (y_b, v), (w_out_b, f_gate, f_up) = _seq_fwd(
        z, w_dw4, b_dw, ln_g, ln_b, w_pool_b, s_pool,
        comm=[_GatherD2d([f_out], ["rows"]), _GatherIci([f_gate], ["rows"], [True], which=(0, 1)),
              _GatherIci([f_up], ["rows"], [True])])
    (h1, hn_b), (wgT_b, wuT_b, f_down) = _out_proj(
        y_b, x2, w_out_b, g_ffn,
        comm=[_GatherD2d([f_gate, f_up], ["rows"] * 2), _GatherIci([f_down], ["rows"], [True])])
    (g_b, u_b, a_b), (wd_b,) = _gate_up(hn_b, wgT_b, wuT_b, comm=[_GatherD2d([f_down], ["rows"])])
    (dh2, dh2_b, loss_part, d_g_final), _ = _down_loss(a_b, wd_b, h1, target, g_final2)

    gw_down, _ = _weight_grad("grad_w_down", a_b, dh2_b)
    (dg_b, du_b), (p_down_xy,) = _ffn_bwd_act(dh2_b, wd_b, g_b, u_b, comm=[_Scatter([gw_down], ["rows"], which=(0, 1))])
    gw_gateT, (p_down_d,) = _weight_grad("grad_w_gate", dg_b, hn_b, comm=[_Scatter([gw_down], ["rows"], which=(2,))])
    gw_upT, _ = _weight_grad("grad_w_up", du_b, hn_b)
    sum_down = _sum_parts("sum_w_down", gw_down, "rows", [p_down_xy, p_down_d], me)
    (dh1, dh1_b, dy, d_g_ffn), (p_gate, oth_down) = _ffn_bwd_in(
        dg_b, du_b, wgT_b, wuT_b, h1, dh2, g_ffn, w_out_b, comm=[_Scatter([gw_gateT], ["rows"]), _Swap([sum_down])])
    gw_out, _ = _weight_grad("grad_w_out", y_b, dh1_b)
    sum_gate = _sum_parts("sum_w_gate", gw_gateT, "rows", [p_gate], me)
    res = {}
    res["w_down"], _ = _adamw("adamw_w_down", w_down2, m_w_down[0], v_w_down[0], sum_down, oth_down)
    (dz_b, d_wdw, d_bdw, d_lng, d_lnb, d_wp, d_sp, d_bin), (p_up, p_out, oth_gate) = _seq_bwd(
        z, dy, v, w_dw4, ln_g, ln_b, w_pool_b, s_pool,
        comm=[_Scatter([gw_upT, gw_out], ["rows", "rows"]), _Swap([sum_gate])])
    vec_grads = {"b_dw": d_bdw, "ln_g": d_lng, "ln_b": d_lnb, "s_pool": d_sp, "g_ffn": d_g_ffn, "g_final": d_g_final, "b_in": d_bin}
    layout = _PackLayout(dw_cols * N_CHIPS // LANES, n_grp, G, [(k, a.shape[1]) for k, a in vec_grads.items()])
    pack = _pack_small(layout, d_wdw, d_wp, vec_grads)
    sum_up = _sum_parts("sum_w_up", gw_upT, "rows", [p_up], me)
    sum_out = _sum_parts("sum_w_out", gw_out, "rows", [p_out], me)
    gw_inT, (p_small, oth_up, oth_out) = _weight_grad(
        "grad_w_in", dz_b, xn_b, comm=[_Scatter([pack], ["all"]), _Swap([sum_up, sum_out])])
    sum_small = _sum_parts("sum_small", pack, "all", [p_small], me)
    res["w_gate"], _ = _adamw("adamw_w_gate", w_gateT, m_w_gate[0].T, v_w_gate[0].T, sum_gate, oth_gate)
    (grad_x, d_g_mix), (p_in, oth_small) = _in_proj_bwd(
        dz_b, w_inT_b, x2, dh1, g_mix, comm=[_Scatter([gw_inT], ["rows"]), _Swap([sum_small])])
    res["w_up"], _ = _adamw("adamw_w_up", w_upT, m_w_up[0].T, v_w_up[0].T, sum_up, oth_up)
    res["w_out"], _ = _adamw("adamw_w_out", w_out2, m_w_out[0], v_w_out[0], sum_out, oth_out)
    sum_in = _sum_parts("sum_w_in", gw_inT, "rows", [p_in], me)
    (*res["g_mix"], loss_row), (oth_in,) = _allreduce_adamw_row(
        d_g_mix, g_mix, m_g_mix, v_g_mix, loss_part, comm=[_Swap([sum_in])])
    loss = loss_row[0, 0]
    res["w_in"], _ = _adamw("adamw_w_in", w_in2, m_w_in[0], v_w_in[0], sum_in, oth_in, g_transposed=True)

    pad_dw = lambda a: jnp.pad(a[0], ((0, HALO - KW), (0, 0)))
    vec_w = {"b_dw": b_dw, "ln_g": ln_g, "ln_b": ln_b, "s_pool": s_pool, "g_ffn": g_ffn, "g_final": g_final2, "b_in": b_in}
    vec_m = {"b_dw": m_b_dw, "ln_g": m_ln_g, "ln_b": m_ln_b, "s_pool": m_s_pool, "g_ffn": m_g_ffn,
             "g_final": m_g_final.reshape(1, D), "b_in": m_b_in}
    vec_v = {"b_dw": v_b_dw, "ln_g": v_ln_g, "ln_b": v_ln_b, "s_pool": v_s_pool, "g_ffn": v_g_ffn,
             "g_final": v_g_final.reshape(1, D), "b_in": v_b_in}
    small = _adamw_small(layout, sum_small, oth_small, pad_dw(w_dw), pad_dw(m_w_dw), pad_dw(v_w_dw),
                         w_pool3, m_w_pool[0], v_w_pool[0], vec_w, vec_m, vec_v)
    res["w_dw"] = [a[:KW][None] for a in small[0:4]]
    res["w_pool"] = [a[None] for a in small[4:8]]
    for q, k in enumerate(vec_w):
        res[k] = list(small[8 + 4 * q : 12 + 4 * q])
    res["g_final"] = [a.reshape(D) for a in res["g_final"]]
    for k in ("w_in", "w_out", "w_down"):
        res[k] = [a[None] for a in res[k]]
    for k in ("w_gate", "w_up"):
        res[k] = [a.T[None] for a in res[k]]

    order = ["g_mix", "w_in", "b_in", "w_dw", "b_dw", "ln_g", "ln_b", "w_pool", "s_pool", "w_out", "g_ffn", "w_gate", "w_up", "w_down", "g_final"]
    outs = [loss, grad_x[None]]
    for q in range(4):
        outs += [res[k][q] for k in order]
    return tuple(outs)
```

```python
import jax
import jax.numpy as jnp
from jax import lax
from jax.experimental import pallas as pl
from jax.experimental.pallas import tpu as pltpu

F32 = jnp.float32
BF16 = jnp.bfloat16
MESH = pl.DeviceIdType.MESH
ANY = pl.BlockSpec(memory_space=pl.ANY)

RMS_EPS = 1e-6
LN_EPS = 1e-5
POOL_WINDOWS = (2, 4, 8, 16)
ADAM_LR = 0.001
ADAM_B1 = 0.9
ADAM_B2 = 0.999
ADAM_EPS = 1e-08
ADAM_WD = 0.01
ADAM_STEP = 10

LANES = 128
SUBLANES = 8
HALO = 32
CONV_ROWS = 64
HIDDEN_CHUNK = 512
VMEM_LIMIT = 56 * 1024 * 1024
PACK_W = 512
N_CHIPS = 4
N_DEV = 8


def _tile(n, want, mult=8):
    t = min(n, want)
    while n % t or t % mult:
        t -= 1
    return t


def _sigmoid(x):
    return 1.0 / (1.0 + jnp.exp(-x))


def _dot(a, b, dims):
    return lax.dot_general(a, b, (dims, ((), ())), preferred_element_type=F32)


NN = ((1,), (0,))
NT = ((1,), (1,))
TN = ((0,), (0,))


def _rms_bwd(x, g, dy):
    r = lax.rsqrt(jnp.mean(x * x, axis=-1, keepdims=True) + RMS_EPS)
    xh = x * r
    gy = dy * g
    dx = r * (gy - xh * jnp.mean(gy * xh, axis=-1, keepdims=True))
    return dx, dy * xh


def _accumulate(ref, first, val):
    @pl.when(first)
    def _():
        ref[...] = val

    @pl.when(jnp.logical_not(first))
    def _():
        ref[...] += val


def _place():
    return lax.axis_index("x"), lax.axis_index("y"), lax.axis_index("c")


def _other_chips(x, y):
    return [(1 - x, y), (x, 1 - y), (1 - x, 1 - y)]


def _rows(ref, start, n):
    return ref.at[pl.ds(pl.multiple_of(start, 16), n)]


def _window(ref, how, k, c=None):
    if how == "all":
        return ref
    if how == "lead":
        return ref.at[k]
    if how == "rows":
        n = ref.shape[0] // N_CHIPS
        if c is None:
            return _rows(ref, k * n, n)
        return _rows(ref, k * n + c * (n // 2), n // 2)
    n = ref.shape[1] // N_CHIPS
    cols = pl.ds(pl.multiple_of(k * n, LANES), n)
    if c is None:
        return ref.at[:, cols]
    h = ref.shape[0] // 2
    return ref.at[pl.ds(pl.multiple_of(c * h, 16), h), cols]


def _remote(src, dst, sems, s, device):
    return pltpu.make_async_remote_copy(
        src_ref=src, dst_ref=dst, send_sem=sems.at[s], recv_sem=sems.at[s + 1], device_id=device, device_id_type=MESH)


class _GatherIci:
    aliased = True

    def __init__(self, fulls, hows, splits, which=(0, 1, 2)):
        self.fulls, self.hows, self.splits, self.which = list(fulls), list(hows), list(splits), tuple(which)

    def inputs(self):
        return self.fulls

    def out_shapes(self):
        return [jax.ShapeDtypeStruct(a.shape, a.dtype) for a in self.fulls]

    def n_sems(self):
        return 6 * len(self.fulls)

    def build(self, ins, outs, sems, base):
        x, y, c = _place()
        me = 2 * x + y
        chips = _other_chips(x, y)
        starts, waits = [], []
        for a, (how, sp) in enumerate(zip(self.hows, self.splits)):
            half = c if sp else None
            mine = _window(outs[a], how, me, half)
            for j in self.which:
                px, py = chips[j]
                s = base + 6 * a + 2 * j
                cp = _remote(mine, mine, sems, s, (px, py, c))
                landing = _remote(mine, _window(outs[a], how, 2 * px + py, half), sems, s, (px, py, c))
                starts.append(cp.start)
                waits += [landing.wait_recv, cp.wait_send]
        return starts, waits


class _GatherD2d:
    aliased = True

    def __init__(self, fulls, hows):
        self.fulls, self.hows = list(fulls), list(hows)

    def inputs(self):
        return self.fulls

    def out_shapes(self):
        return [jax.ShapeDtypeStruct(a.shape, a.dtype) for a in self.fulls]

    def n_sems(self):
        return 6 * len(self.fulls)

    def build(self, ins, outs, sems, base):
        x, y, c = _place()
        starts, waits = [], []
        for a, how in enumerate(self.hows):
            for j, (px, py) in enumerate(_other_chips(x, y)):
                s = base + 6 * a + 2 * j
                got = _window(outs[a], how, 2 * px + py, c)
                cp = _remote(got, got, sems, s, (x, y, 1 - c))
                landing = _remote(got, _window(outs[a], how, 2 * px + py, 1 - c), sems, s, (x, y, 1 - c))
                starts.append(cp.start)
                waits += [landing.wait_recv, cp.wait_send]
        return starts, waits


def _part_shape(a, how):
    if how == "all":
        return a.shape
    if how == "lead":
        return a.shape[1:]
    if how == "rows":
        return (a.shape[0] // N_CHIPS, a.shape[1])
    return (a.shape[0], a.shape[1] // N_CHIPS)


class _Scatter:
    aliased = False

    def __init__(self, fulls, hows, which=(0, 1, 2)):
        self.fulls, self.hows, self.which = list(fulls), list(hows), tuple(which)

    def inputs(self):
        return self.fulls

    def out_shapes(self):
        return [jax.ShapeDtypeStruct((len(self.which),) + _part_shape(a, h), a.dtype) for a, h in zip(self.fulls, self.hows)]

    def n_sems(self):
        return 6 * len(self.fulls)

    def build(self, ins, outs, sems, base):
        x, y, c = _place()
        chips = _other_chips(x, y)
        starts, waits = [], []
        for a, how in enumerate(self.hows):
            for slot, j in enumerate(self.which):
                px, py = chips[j]
                cp = _remote(_window(ins[a], how, 2 * px + py), outs[a].at[slot], sems, base + 6 * a + 2 * j, (px, py, c))
                starts.append(cp.start)
                waits += [cp.wait_recv, cp.wait_send]
        return starts, waits


class _Swap:
    aliased = False

    def __init__(self, arrays):
        self.arrays = list(arrays)

    def inputs(self):
        return self.arrays

    def out_shapes(self):
        return [jax.ShapeDtypeStruct(a.shape, a.dtype) for a in self.arrays]

    def n_sems(self):
        return 2 * len(self.arrays)

    def build(self, ins, outs, sems, base):
        x, y, c = _place()
        starts, waits = [], []
        for a in range(len(ins)):
            cp = _remote(ins[a], outs[a], sems, base + 2 * a, (x, y, 1 - c))
            starts.append(cp.start)
            waits += [cp.wait_recv, cp.wait_send]
        return starts, waits


class _HalfSwap:
    aliased = False

    def __init__(self, fulls):
        self.fulls = list(fulls)

    def inputs(self):
        return self.fulls

    def out_shapes(self):
        return [jax.ShapeDtypeStruct((N_CHIPS, a.shape[0] // (2 * N_CHIPS), a.shape[1]), a.dtype) for a in self.fulls]

    def n_sems(self):
        return 2 * N_CHIPS * len(self.fulls)

    def build(self, ins, outs, sems, base):
        x, y, c = _place()
        starts, waits = [], []
        for a in range(len(ins)):
            for k in range(N_CHIPS):
                cp = _remote(_window(ins[a], "rows", k, 1 - c), outs[a].at[k], sems, base + 2 * (N_CHIPS * a + k), (x, y, 1 - c))
                starts.append(cp.start)
                waits += [cp.wait_recv, cp.wait_send]
        return starts, waits


def _call(name, body, grid, in_specs, out_specs, out_shape, args, scratch=(), comm=()):
    comm = list(comm)
    n_in, n_out, n_scr = len(args), len(out_shape), len(scratch)
    c_in = [a for op in comm for a in op.inputs()]
    c_out = [s for op in comm for s in op.out_shapes()]
    n_sems = sum(op.n_sems() for op in comm)
    aliases, i_in, i_out = {}, 0, 0
    for op in comm:
        if op.aliased:
            for q in range(len(op.inputs())):
                aliases[n_in + i_in + q] = n_out + i_out + q
        i_in, i_out = i_in + len(op.inputs()), i_out + len(op.out_shapes())

    def wrapped(*refs):
        ins = refs[:n_in]
        cin = refs[n_in : n_in + len(c_in)]
        o0 = n_in + len(c_in)
        outs = refs[o0 : o0 + n_out]
        cout = refs[o0 + n_out : o0 + n_out + len(c_out)]
        s0 = o0 + n_out + len(c_out)
        scr = refs[s0 : s0 + n_scr]

        def copies():
            sems = refs[s0 + n_scr]
            starts, waits = [], []
            i_in = i_out = base = 0
            for op in comm:
                ni, no = len(op.inputs()), len(op.out_shapes())
                s, w = op.build(cin[i_in : i_in + ni], cout[i_out : i_out + no], sems, base)
                starts += s
                waits += w
                i_in, i_out, base = i_in + ni, i_out + no, base + op.n_sems()
            return starts, waits

        def run_starts():
            for start in copies()[0]:
                start()

        def run_waits():
            for wait in copies()[1]:
                wait()

        if comm and grid:
            first = last = True
            for d, n in enumerate(grid):
                first = jnp.logical_and(first, pl.program_id(d) == 0)
                last = jnp.logical_and(last, pl.program_id(d) == n - 1)
            pl.when(first)(run_starts)
        elif comm:
            run_starts()
        if body is not None:
            body(*ins, *outs, *scr)
        if comm and grid:
            pl.when(last)(run_waits)
        elif comm:
            run_waits()

    res = pl.pallas_call(
        wrapped,
        name=name,
        grid=grid,
        in_specs=list(in_specs) + [ANY] * len(c_in),
        out_specs=list(out_specs) + [ANY] * len(c_out),
        out_shape=list(out_shape) + c_out,
        scratch_shapes=list(scratch) + ([pltpu.SemaphoreType.DMA((n_sems,))] if comm else []),
        input_output_aliases=aliases,
        compiler_params=pltpu.CompilerParams(dimension_semantics=("arbitrary",) * len(grid), vmem_limit_bytes=VMEM_LIMIT),
    )(*args, *c_in)
    return tuple(res[:n_out]), tuple(res[n_out:])


def _gather_now(fulls, hows, splits):
    n = len(fulls)
    ici = _GatherIci(fulls, hows, splits)
    split_ids = [a for a in range(n) if splits[a]]
    d2d = _GatherD2d([fulls[a] for a in split_ids], [hows[a] for a in split_ids])

    def body(*refs):
        outs, sems = refs[n : 2 * n], refs[2 * n]
        for op, op_refs, base in ((ici, outs, 0), (d2d, [outs[a] for a in split_ids], ici.n_sems())):
            starts, waits = op.build(None, op_refs, sems, base)
            for start in starts:
                start()
            for wait in waits:
                wait()

    return pl.pallas_call(
        body,
        name="gather_first",
        in_specs=[ANY] * n,
        out_specs=[ANY] * n,
        out_shape=ici.out_shapes(),
        scratch_shapes=[pltpu.SemaphoreType.DMA((ici.n_sems() + d2d.n_sems(),))],
        input_output_aliases={a: a for a in range(n)},
    )(*fulls)


def _place_weights(shards, hows, full_shapes, dtypes, transposed):
    n = len(shards)

    def body(*refs):
        ins, outs, bufs, sems = refs[:n], refs[n : 2 * n], refs[2 * n : 3 * n], refs[3 * n]
        x, y, _ = _place()
        copies = []
        for a in range(n):
            val = ins[a][...].T if transposed[a] else ins[a][...]
            bufs[a][...] = val.astype(dtypes[a])
            cp = pltpu.make_async_copy(bufs[a], _window(outs[a], hows[a], 2 * x + y), sems.at[a])
            cp.start()
            copies.append(cp)
        for cp in copies:
            cp.wait()

    return pl.pallas_call(
        body,
        name="place_weights",
        in_specs=[pl.BlockSpec(memory_space=pltpu.VMEM)] * n,
        out_specs=[ANY] * n,
        out_shape=[jax.ShapeDtypeStruct(s, d) for s, d in zip(full_shapes, dtypes)],
        scratch_shapes=[pltpu.VMEM(a.shape[::-1] if t else a.shape, d) for a, d, t in zip(shards, dtypes, transposed)]
        + [pltpu.SemaphoreType.DMA((n,))],
        compiler_params=pltpu.CompilerParams(vmem_limit_bytes=VMEM_LIMIT),
    )(*shards)


def _in_proj(x, g_mix, w_inT_b, b_in, comm=()):
    T, D = x.shape
    CI = w_inT_b.shape[0]
    tm = _tile(T, 512)

    def body(x_ref, g_ref, w_ref, b_ref, z_ref, xn_ref):
        xv = x_ref[...]
        r = lax.rsqrt(jnp.mean(xv * xv, axis=-1, keepdims=True) + RMS_EPS)
        xn = (xv * r * g_ref[...]).astype(BF16)
        xn_ref[...] = xn
        z_ref[...] = _dot(xn, w_ref[...], NT) + b_ref[...]

    return _call(
        "in_proj",
        body,
        (T // tm,),
        [
            pl.BlockSpec((tm, D), lambda i: (i, 0)),
            pl.BlockSpec((1, D), lambda i: (0, 0)),
            pl.BlockSpec((CI, D), lambda i: (0, 0)),
            pl.BlockSpec((1, CI), lambda i: (0, 0)),
        ],
        [pl.BlockSpec((tm, CI), lambda i: (i, 0)), pl.BlockSpec((tm, D), lambda i: (i, 0))],
        [jax.ShapeDtypeStruct((T, CI), F32), jax.ShapeDtypeStruct((T, D), BF16)],
        (x, g_mix, w_inT_b, b_in),
        comm=comm,
    )


def _fill_shifted(scr):
    n = scr.shape[1] - SUBLANES
    for s in range(1, SUBLANES):
        scr[s, 0:n, :] = scr[0, s : s + n, :]


def _shifted_rows(scr, off, n, cs):
    s = off % SUBLANES
    return scr[s, off - s : off - s + n, cs]


def _pool_mean_minus_token(p_scr, cs, w, cnt, tt):
    tok = p_scr[HALO : HALO + tt, cs]
    s = tok
    for d in range(1, w):
        s = s + p_scr[HALO - d : HALO - d + tt, cs]
    return s / cnt - tok


def _seq_fwd(z, w_dw4, b_dw, ln_g, ln_b, w_pool_b, s_pool, comm=()):
    T, CI = z.shape
    CC = ln_g.shape[1]
    n_grp, G = w_pool_b.shape[0], w_pool_b.shape[-1]
    KW = w_dw4.shape[1]
    D = CC + n_grp * G
    tt = _tile(T, 512, HALO)
    per = tt // HALO

    def body(zc_ref, zp_ref, wdw_ref, bdw_ref, lng_ref, lnb_ref, wp_ref, sp_ref, y_ref, v_ref, u_scr, p_scr):
        i = pl.program_id(0)
        first = i == 0
        u_prev = zp_ref[:, 0:CC] * _sigmoid(zp_ref[:, CC : 2 * CC])
        u_scr[0, 0:HALO, :] = jnp.where(first, 0.0, u_prev)
        p_scr[0:HALO, :] = jnp.where(first, 0.0, zp_ref[:, 2 * CC :])
        u_scr[0, HALO:, :] = zc_ref[:, 0:CC] * _sigmoid(zc_ref[:, CC : 2 * CC])
        p_scr[HALO:, :] = zc_ref[:, 2 * CC :]
        _fill_shifted(u_scr)

        for j in range(CC // LANES):
            cs = slice(LANES * j, LANES * (j + 1))
            for rb in range(tt // CONV_ROWS):
                acc = jnp.zeros((CONV_ROWS, LANES), F32)
                for k in range(KW):
                    off = HALO - (KW - 1) + k + rb * CONV_ROWS
                    acc = acc + _shifted_rows(u_scr, off, CONV_ROWS, cs) * wdw_ref[j, k : k + 1, :]
                v_ref[rb * CONV_ROWS : (rb + 1) * CONV_ROWS, cs] = acc + bdw_ref[:, cs]

        v = v_ref[...]
        mu = jnp.mean(v, axis=-1, keepdims=True)
        d = v - mu
        var = jnp.mean(d * d, axis=-1, keepdims=True)
        ln = d * lax.rsqrt(var + LN_EPS) * lng_ref[...] + lnb_ref[...]
        y_ref[:, 0:CC] = (ln * _sigmoid(ln)).astype(BF16)

        tpos = i * tt + lax.broadcasted_iota(jnp.int32, (tt, 1), 0)
        for gi, w in enumerate(POOL_WINDOWS):
            cs = slice(G * gi, G * (gi + 1))
            cnt = jnp.minimum(tpos + 1, w).astype(F32)
            yi = _pool_mean_minus_token(p_scr, cs, w, cnt, tt)
            q = _dot(yi.astype(BF16), wp_ref[gi], NN)
            y_ref[:, CC + G * gi : CC + G * (gi + 1)] = (q * sp_ref[:, cs]).astype(BF16)

    const2 = lambda i: (0, 0)
    return _call(
        "seq_fwd",
        body,
        (T // tt,),
        [
            pl.BlockSpec((tt, CI), lambda i: (i, 0)),
            pl.BlockSpec((HALO, CI), lambda i: (jnp.maximum(i * per - 1, 0), 0)),
            pl.BlockSpec(w_dw4.shape, lambda i: (0, 0, 0)),
            pl.BlockSpec((1, CC), const2),
            pl.BlockSpec((1, CC), const2),
            pl.BlockSpec((1, CC), const2),
            pl.BlockSpec(w_pool_b.shape, lambda i: (0, 0, 0)),
            pl.BlockSpec((1, n_grp * G), const2),
        ],
        [pl.BlockSpec((tt, D), lambda i: (i, 0)), pl.BlockSpec((tt, CC), lambda i: (i, 0))],
        [jax.ShapeDtypeStruct((T, D), BF16), jax.ShapeDtypeStruct((T, CC), F32)],
        (z, z, w_dw4, b_dw, ln_g, ln_b, w_pool_b, s_pool),
        scratch=[pltpu.VMEM((SUBLANES, HALO + tt, CC), F32), pltpu.VMEM((HALO + tt, n_grp * G), F32)],
        comm=comm,
    )


def _out_proj(y_b, x, w_out_b, g_ffn, comm=()):
    T, D = x.shape
    tm = _tile(T, 512)

    def body(y_ref, x_ref, w_ref, g_ref, h1_ref, hn_ref):
        h1 = x_ref[...] + _dot(y_ref[...], w_ref[...], NN)
        h1_ref[...] = h1
        r = lax.rsqrt(jnp.mean(h1 * h1, axis=-1, keepdims=True) + RMS_EPS)
        hn_ref[...] = (h1 * r * g_ref[...]).astype(BF16)

    row = lambda i: (i, 0)
    return _call(
        "out_proj",
        body,
        (T // tm,),
        [
            pl.BlockSpec((tm, y_b.shape[1]), row),
            pl.BlockSpec((tm, D), row),
            pl.BlockSpec(w_out_b.shape, lambda i: (0, 0)),
            pl.BlockSpec((1, D), lambda i: (0, 0)),
        ],
        [pl.BlockSpec((tm, D), row), pl.BlockSpec((tm, D), row)],
        [jax.ShapeDtypeStruct((T, D), F32), jax.ShapeDtypeStruct((T, D), BF16)],
        (y_b, x, w_out_b, g_ffn),
        comm=comm,
    )


def _hidden_tile(F):
    return _tile(F, 1408, LANES)


def _gate_up(hn_b, wgT_b, wuT_b, comm=()):
    T, D = hn_b.shape
    F = wgT_b.shape[0]
    tm, tf = _tile(T, 1024), _hidden_tile(F)

    def body(hn_ref, wg_ref, wu_ref, g_ref, u_ref, a_ref):
        hn = hn_ref[...]
        for c0 in range(0, tf, HIDDEN_CHUNK):
            cs = slice(c0, min(c0 + HIDDEN_CHUNK, tf))
            gv = _dot(hn, wg_ref[cs, :], NT)
            uv = _dot(hn, wu_ref[cs, :], NT)
            g_ref[:, cs] = gv.astype(BF16)
            u_ref[:, cs] = uv.astype(BF16)
            a_ref[:, cs] = (gv * _sigmoid(gv) * uv).astype(BF16)

    wspec = pl.BlockSpec((tf, D), lambda j, i: (j, 0))
    ospec = pl.BlockSpec((tm, tf), lambda j, i: (i, j))
    return _call(
        "gate_up",
        body,
        (F // tf, T // tm),
        [pl.BlockSpec((tm, D), lambda j, i: (i, 0)), wspec, wspec],
        [ospec, ospec, ospec],
        [jax.ShapeDtypeStruct((T, F), BF16)] * 3,
        (hn_b, wgT_b, wuT_b),
        comm=comm,
    )


def _down_loss(a_b, wd_b, h1, target, g_final, comm=()):
    T, D = h1.shape
    F = a_b.shape[1]
    tm = _tile(T, 512)
    nt = T // tm

    def body(a_ref, w_ref, h1_ref, t_ref, g_ref, dh2_ref, dh2b_ref, loss_ref, dg_ref):
        i = pl.program_id(0)
        h2 = h1_ref[...] + _dot(a_ref[...], w_ref[...], NN)
        r = lax.rsqrt(jnp.mean(h2 * h2, axis=-1, keepdims=True) + RMS_EPS)
        g = g_ref[...]
        diff = h2 * r * g - t_ref[...]
        _accumulate(loss_ref, i == 0, jnp.full(loss_ref.shape, jnp.sum(diff * diff) * (0.5 / D), F32))
        dh2, dg_rows = _rms_bwd(h2, g, diff * (1.0 / D))
        dh2_ref[...] = dh2
        dh2b_ref[...] = dh2.astype(BF16)
        _accumulate(dg_ref, i == 0, jnp.sum(dg_rows, axis=0, keepdims=True))

    row = lambda i: (i, 0)
    return _call(
        "down_loss",
        body,
        (nt,),
        [
            pl.BlockSpec((tm, F), row),
            pl.BlockSpec((F, D), lambda i: (0, 0), pipeline_mode=pl.Buffered(1)),
            pl.BlockSpec((tm, D), row),
            pl.BlockSpec((tm, D), row),
            pl.BlockSpec((1, D), lambda i: (0, 0)),
        ],
        [
            pl.BlockSpec((tm, D), row),
            pl.BlockSpec((tm, D), row),
            pl.BlockSpec((1, LANES), lambda i: (0, 0)),
            pl.BlockSpec((1, D), lambda i: (0, 0)),
        ],
        [
            jax.ShapeDtypeStruct((T, D), F32),
            jax.ShapeDtypeStruct((T, D), BF16),
            jax.ShapeDtypeStruct((1, LANES), F32),
            jax.ShapeDtypeStruct((1, D), F32),
        ],
        (a_b, wd_b, h1, target, g_final),
        comm=comm,
    )


def _ffn_bwd_act(dh2_b, wd_b, g_b, u_b, comm=()):
    T, D = dh2_b.shape
    F = wd_b.shape[0]
    tm, tf = _tile(T, 1024), _hidden_tile(F)

    def body(d_ref, w_ref, g_ref, u_ref, dg_ref, du_ref):
        d = d_ref[...]
        for c0 in range(0, tf, HIDDEN_CHUNK):
            cs = slice(c0, min(c0 + HIDDEN_CHUNK, tf))
            da = _dot(d, w_ref[cs, :], NT)
            gv = g_ref[:, cs].astype(F32)
            uv = u_ref[:, cs].astype(F32)
            sg = _sigmoid(gv)
            silu = gv * sg
            dg_ref[:, cs] = (da * uv * (sg * (1.0 + gv * (1.0 - sg)))).astype(BF16)
            du_ref[:, cs] = (da * silu).astype(BF16)

    aspec = pl.BlockSpec((tm, tf), lambda j, i: (i, j))
    return _call(
        "ffn_bwd_act",
        body,
        (F // tf, T // tm),
        [pl.BlockSpec((tm, D), lambda j, i: (i, 0)), pl.BlockSpec((tf, D), lambda j, i: (j, 0)), aspec, aspec],
        [aspec, aspec],
        [jax.ShapeDtypeStruct((T, F), BF16)] * 2,
        (dh2_b, wd_b, g_b, u_b),
        comm=comm,
    )


def _ffn_bwd_in(dg_b, du_b, wgT_b, wuT_b, h1, dh2, g_ffn, w_out_b, comm=()):
    T, D = h1.shape
    F = wgT_b.shape[0]
    DM = w_out_b.shape[0]
    tm = _tile(T, 512)

    def body(dg_ref, du_ref, wg_ref, wu_ref, h1_ref, dh2_ref, g_ref, wo_ref, dh1_ref, dh1b_ref, dy_ref, dgf_ref):
        i = pl.program_id(0)
        dhn = _dot(dg_ref[...], wg_ref[...], NN) + _dot(du_ref[...], wu_ref[...], NN)
        dx, dg_rows = _rms_bwd(h1_ref[...], g_ref[...], dhn)
        dh1 = dh2_ref[...] + dx
        dh1b = dh1.astype(BF16)
        dh1_ref[...] = dh1
        dh1b_ref[...] = dh1b
        dy_ref[...] = _dot(dh1b, wo_ref[...], NT)
        _accumulate(dgf_ref, i == 0, jnp.sum(dg_rows, axis=0, keepdims=True))

    row = lambda i: (i, 0)
    const = lambda i: (0, 0)
    return _call(
        "ffn_bwd_in",
        body,
        (T // tm,),
        [
            pl.BlockSpec((tm, F), row),
            pl.BlockSpec((tm, F), row),
            pl.BlockSpec((F, D), const, pipeline_mode=pl.Buffered(1)),
            pl.BlockSpec((F, D), const, pipeline_mode=pl.Buffered(1)),
            pl.BlockSpec((tm, D), row),
            pl.BlockSpec((tm, D), row),
            pl.BlockSpec((1, D), const),
            pl.BlockSpec((DM, D), const, pipeline_mode=pl.Buffered(1)),
        ],
        [pl.BlockSpec((tm, D), row), pl.BlockSpec((tm, D), row), pl.BlockSpec((tm, DM), row), pl.BlockSpec((1, D), const)],
        [
            jax.ShapeDtypeStruct((T, D), F32),
            jax.ShapeDtypeStruct((T, D), BF16),
            jax.ShapeDtypeStruct((T, DM), F32),
            jax.ShapeDtypeStruct((1, D), F32),
        ],
        (dg_b, du_b, wgT_b, wuT_b, h1, dh2, g_ffn, w_out_b),
        comm=comm,
    )


def _seq_bwd(z, dy, v, w_dw4, ln_g, ln_b, w_pool_b, s_pool, comm=()):
    T, CI = z.shape
    CC = ln_g.shape[1]
    n_grp, G = w_pool_b.shape[0], w_pool_b.shape[-1]
    CP = n_grp * G
    KW = w_dw4.shape[1]
    n_cc = CC // LANES
    D = CC + CP
    tt = _tile(T, 512, HALO)
    per = tt // HALO
    n_tiles = T // tt
    last_halo = T // HALO - 1

    def body(zc_ref, zp_ref, dyc_ref, dyn_ref, vc_ref, vn_ref, wdw_ref, lng_ref, lnb_ref, wp_ref, sp_ref,
             dz_ref, dwdw_ref, dbdw_ref, dlng_ref, dlnb_ref, dwp_ref, dsp_ref, dbin_ref,
             dv_scr, u_scr, p_scr, g_scr, dw_scr):
        i = pl.program_id(0)
        first = i == 0
        last = i == n_tiles - 1
        lng, lnb = lng_ref[...], lnb_ref[...]

        def conv_pre(vv, dyc):
            mu = jnp.mean(vv, axis=-1, keepdims=True)
            d = vv - mu
            rs = lax.rsqrt(jnp.mean(d * d, axis=-1, keepdims=True) + LN_EPS)
            xh = d * rs
            ln = xh * lng + lnb
            sg = _sigmoid(ln)
            dln = dyc * (sg * (1.0 + ln * (1.0 - sg)))
            dxh = dln * lng
            dv = rs * (dxh - jnp.mean(dxh, axis=-1, keepdims=True) - xh * jnp.mean(dxh * xh, axis=-1, keepdims=True))
            return dv, dln, xh

        dv_c, dln_c, xh_c = conv_pre(vc_ref[...], dyc_ref[:, 0:CC])
        dv_scr[0, 0:tt, :] = dv_c
        dv_n, _, _ = conv_pre(vn_ref[...], dyn_ref[:, 0:CC])
        dv_scr[0, tt:, :] = jnp.where(last, 0.0, dv_n)
        _fill_shifted(dv_scr)
        _accumulate(dlng_ref, first, jnp.sum(dln_c * xh_c, axis=0, keepdims=True))
        _accumulate(dlnb_ref, first, jnp.sum(dln_c, axis=0, keepdims=True))
        _accumulate(dbdw_ref, first, jnp.sum(dv_c, axis=0, keepdims=True))

        u_scr[...] = zc_ref[:, 0:CC] * _sigmoid(zc_ref[:, CC : 2 * CC])

        @pl.when(first)
        def _():
            dw_scr[...] = jnp.zeros_like(dw_scr)

        for j in range(n_cc):
            cs = slice(LANES * j, LANES * (j + 1))
            gs = slice(CC + LANES * j, CC + LANES * (j + 1))
            dbin_a = jnp.zeros((1, LANES), F32)
            dbin_g = jnp.zeros((1, LANES), F32)
            for rb in range(tt // CONV_ROWS):
                rows = slice(rb * CONV_ROWS, (rb + 1) * CONV_ROWS)
                u_blk = u_scr[rows, cs]
                du = jnp.zeros((CONV_ROWS, LANES), F32)
                for k in range(KW):
                    off = rb * CONV_ROWS + (KW - 1) - k
                    d = _shifted_rows(dv_scr, off, CONV_ROWS, cs)
                    du = du + d * wdw_ref[j, k : k + 1, :]
                    dw_scr[j * HALO + k] += jnp.sum((u_blk * d).reshape(CONV_ROWS // 8, 8, LANES), axis=0)
                a = zc_ref[rows, cs]
                sg = _sigmoid(zc_ref[rows, gs])
                da = du * sg
                dgate = du * a * sg * (1.0 - sg)
                dz_ref[rows, cs] = da.astype(BF16)
                dz_ref[rows, gs] = dgate.astype(BF16)
                dbin_a = dbin_a + jnp.sum(da, axis=0, keepdims=True)
                dbin_g = dbin_g + jnp.sum(dgate, axis=0, keepdims=True)
            _accumulate(dbin_ref.at[:, cs], first, dbin_a)
            _accumulate(dbin_ref.at[:, gs], first, dbin_g)

        @pl.when(last)
        def _():
            dwdw_ref[...] = jnp.sum(dw_scr[...], axis=1).reshape(dwdw_ref.shape)

        p_scr[0:HALO, :] = jnp.where(first, 0.0, zp_ref[:, 2 * CC :])
        p_scr[HALO:, :] = zc_ref[:, 2 * CC :]
        tpos = i * tt + lax.broadcasted_iota(jnp.int32, (tt, 1), 0)
        for gi, w in enumerate(POOL_WINDOWS):
            cs = slice(G * gi, G * (gi + 1))
            ys = slice(CC + G * gi, CC + G * (gi + 1))
            ps = slice(2 * CC + G * gi, 2 * CC + G * (gi + 1))
            cnt = jnp.minimum(tpos + 1, w).astype(F32)
            yib = _pool_mean_minus_token(p_scr, cs, w, cnt, tt).astype(BF16)
            wp = wp_ref[gi]
            sp = sp_ref[:, cs]
            dyp = dyc_ref[:, ys]
            q = _dot(yib, wp, NN)
            _accumulate(dsp_ref.at[:, cs], first, jnp.sum(dyp * q, axis=0, keepdims=True))
            dq_c = (dyp * sp).astype(BF16)
            dq_n = (jnp.where(last, 0.0, dyn_ref[:, ys]) * sp).astype(BF16)
            _accumulate(dwp_ref.at[gi], first, _dot(yib, dq_c, TN))
            dyi_c = _dot(dq_c, wp, NT)
            g_scr[0:tt, cs] = dyi_c / cnt
            g_scr[tt:, cs] = _dot(dq_n, wp, NT) * (1.0 / w)
            dp = -dyi_c
            for d in range(w):
                dp = dp + g_scr[d : d + tt, cs]
            dz_ref[:, ps] = dp.astype(BF16)
            _accumulate(dbin_ref.at[:, ps], first, jnp.sum(dp, axis=0, keepdims=True))

    cur = lambda i: (i, 0)
    prev = lambda i: (jnp.maximum(i * per - 1, 0), 0)
    nxt = lambda i: (jnp.minimum((i + 1) * per, last_halo), 0)
    c2 = lambda i: (0, 0)
    c3 = lambda i: (0, 0, 0)
    return _call(
        "seq_bwd",
        body,
        (n_tiles,),
        [
            pl.BlockSpec((tt, CI), cur),
            pl.BlockSpec((HALO, CI), prev),
            pl.BlockSpec((tt, D), cur),
            pl.BlockSpec((HALO, D), nxt),
            pl.BlockSpec((tt, CC), cur),
            pl.BlockSpec((HALO, CC), nxt),
            pl.BlockSpec(w_dw4.shape, c3),
            pl.BlockSpec((1, CC), c2),
            pl.BlockSpec((1, CC), c2),
            pl.BlockSpec(w_pool_b.shape, c3),
            pl.BlockSpec((1, CP), c2),
        ],
        [
            pl.BlockSpec((tt, CI), cur),
            pl.BlockSpec((n_cc, HALO, LANES), c3),
            pl.BlockSpec((1, CC), c2),
            pl.BlockSpec((1, CC), c2),
            pl.BlockSpec((1, CC), c2),
            pl.BlockSpec((n_grp, G, G), c3),
            pl.BlockSpec((1, CP), c2),
            pl.BlockSpec((1, CI), c2),
        ],
        [
            jax.ShapeDtypeStruct((T, CI), BF16),
            jax.ShapeDtypeStruct((n_cc, HALO, LANES), F32),
            jax.ShapeDtypeStruct((1, CC), F32),
            jax.ShapeDtypeStruct((1, CC), F32),
            jax.ShapeDtypeStruct((1, CC), F32),
            jax.ShapeDtypeStruct((n_grp, G, G), F32),
            jax.ShapeDtypeStruct((1, CP), F32),
            jax.ShapeDtypeStruct((1, CI), F32),
        ],
        (z, z, dy, dy, v, v, w_dw4, ln_g, ln_b, w_pool_b, s_pool),
        scratch=[
            pltpu.VMEM((SUBLANES, tt + HALO, CC), F32),
            pltpu.VMEM((tt, CC), F32),
            pltpu.VMEM((HALO + tt, CP), F32),
            pltpu.VMEM((tt + HALO, CP), F32),
            pltpu.VMEM((n_cc * HALO, 8, LANES), F32),
        ],
        comm=comm,
    )


def _in_proj_bwd(dz_b, w_inT_b, x, dh1, g_mix, comm=()):
    T, D = x.shape
    CI = w_inT_b.shape[0]
    tm = _tile(T, 512)

    def body(dz_ref, w_ref, x_ref, dh1_ref, g_ref, dx_ref, dg_ref):
        i = pl.program_id(0)
        dxn = _dot(dz_ref[...], w_ref[...], NN)
        dx, dg_rows = _rms_bwd(x_ref[...], g_ref[...], dxn)
        dx_ref[...] = dh1_ref[...] + dx
        _accumulate(dg_ref, i == 0, jnp.sum(dg_rows, axis=0, keepdims=True))

    row = lambda i: (i, 0)
    const = lambda i: (0, 0)
    return _call(
        "in_proj_bwd",
        body,
        (T // tm,),
        [
            pl.BlockSpec((tm, CI), row),
            pl.BlockSpec((CI, D), const),
            pl.BlockSpec((tm, D), row),
            pl.BlockSpec((tm, D), row),
            pl.BlockSpec((1, D), const),
        ],
        [pl.BlockSpec((tm, D), row), pl.BlockSpec((1, D), const)],
        [jax.ShapeDtypeStruct((T, D), F32), jax.ShapeDtypeStruct((1, D), F32)],
        (dz_b, w_inT_b, x, dh1, g_mix),
        comm=comm,
    )


def _weight_grad(name, a_b, b_b, comm=()):
    T, N1 = a_b.shape
    N2 = b_b.shape[1]
    t1 = _tile(N1, 1408, LANES)
    tk = _tile(T, 2048)
    nk = T // tk

    def body(a_ref, b_ref, o_ref, acc):
        k = pl.program_id(1)
        _accumulate(acc, k == 0, _dot(a_ref[...], b_ref[...], TN))

        @pl.when(k == nk - 1)
        def _():
            o_ref[...] = acc[...].astype(BF16)

    (out,), rest = _call(
        name,
        body,
        (N1 // t1, nk),
        [pl.BlockSpec((tk, t1), lambda n, k: (k, n)), pl.BlockSpec((tk, N2), lambda n, k: (k, 0))],
        [pl.BlockSpec((t1, N2), lambda n, k: (n, 0))],
        [jax.ShapeDtypeStruct((N1, N2), BF16)],
        (a_b, b_b),
        scratch=[pltpu.VMEM((t1, N2), F32)],
        comm=comm,
    )
    return out, rest


def _pair_sum(name, full, other, core):
    _, n, C = other.shape
    tr = _tile(n, 512)
    nb = n // tr

    def body(c_ref, own_ref, oth_ref, o_ref):
        o_ref[...] = (own_ref[...].astype(F32) + oth_ref[...].astype(F32)).astype(o_ref.dtype)

    return pl.pallas_call(
        body,
        name=name,
        grid_spec=pltpu.PrefetchScalarGridSpec(
            num_scalar_prefetch=1,
            grid=(N_CHIPS, nb),
            in_specs=[pl.BlockSpec((tr, C), lambda k, i, c_ref: ((2 * k + c_ref[0]) * nb + i, 0)),
                      pl.BlockSpec((None, tr, C), lambda k, i, c_ref: (k, i, 0))],
            out_specs=pl.BlockSpec((None, tr, C), lambda k, i, c_ref: (k, i, 0)),
        ),
        out_shape=jax.ShapeDtypeStruct(other.shape, full.dtype),
        compiler_params=pltpu.CompilerParams(dimension_semantics=("arbitrary",) * 2, vmem_limit_bytes=VMEM_LIMIT),
    )(core, full, other)


def _sum_parts(name, full, how, parts, me):
    _, R, C = parts[0].shape
    tr = _tile(R, 512)
    nb = R // tr
    where = [(q, r) for q, p in enumerate(parts) for r in range(p.shape[0])]
    assert len(where) == 3

    def body(me_ref, own_ref, *refs):
        o_ref = refs[-1]
        f = lambda j: refs[where[j][0]][where[j][1]].astype(F32)
        o_ref[...] = (own_ref[...].astype(F32) + f(0)) + (f(1) + f(2))

    own_map = {"rows": lambda i, me_ref: (me_ref[0] * nb + i, 0), "cols": lambda i, me_ref: (i, me_ref[0]),
               "all": lambda i, me_ref: (i, 0), "lead": lambda i, me_ref: (me_ref[0], i, 0)}[how]
    own_block = (None, tr, C) if how == "lead" else (tr, C)
    return pl.pallas_call(
        body,
        name=name,
        grid_spec=pltpu.PrefetchScalarGridSpec(
            num_scalar_prefetch=1,
            grid=(nb,),
            in_specs=[pl.BlockSpec(own_block, own_map)]
            + [pl.BlockSpec((p.shape[0], tr, C), lambda i, me_ref: (0, i, 0)) for p in parts],
            out_specs=pl.BlockSpec((tr, C), lambda i, me_ref: (i, 0)),
        ),
        out_shape=jax.ShapeDtypeStruct((R, C), F32),
        compiler_params=pltpu.CompilerParams(dimension_semantics=("arbitrary",), vmem_limit_bytes=VMEM_LIMIT),
    )(me, full, *parts)


_M_CORR = 1.0 - ADAM_B1**ADAM_STEP
_V_CORR = 1.0 - ADAM_B2**ADAM_STEP


def _adamw_math(w, g, m, v):
    m = ADAM_B1 * m + (1.0 - ADAM_B1) * g
    v = ADAM_B2 * v + (1.0 - ADAM_B2) * (g * g)
    delta = -ADAM_LR * ((m / _M_CORR) / (jnp.sqrt(v / _V_CORR) + ADAM_EPS) + ADAM_WD * w)
    return delta, m, v


def _adamw(name, w, m, v, g_here, g_there, g_transposed=False, comm=()):
    R, C = w.shape
    tr = _tile(R, 256, LANES if g_transposed else 8)

    def body(w_ref, m_ref, v_ref, ga_ref, gb_ref, g_ref, d_ref, nm_ref, nv_ref):
        g = ga_ref[...] + gb_ref[...]
        if g_transposed:
            g = g.T
        g_ref[...] = g
        d_ref[...], nm_ref[...], nv_ref[...] = _adamw_math(w_ref[...], g, m_ref[...], v_ref[...])

    spec = pl.BlockSpec((tr, C), lambda i: (i, 0))
    gspec = pl.BlockSpec((C, tr), lambda i: (0, i)) if g_transposed else spec
    return _call(name, body, (R // tr,), [spec] * 3 + [gspec] * 2, [spec] * 4, [jax.ShapeDtypeStruct((R, C), F32)] * 4,
                 (w, m, v, g_here, g_there), comm=comm)


def _adamw_halves(name, w, m, v, g_here, g_there):
    R, C = w.shape
    tr = _tile(R // 2, 256)
    nb = R // 2 // tr

    def body(w_ref, m_ref, v_ref, ga_ref, gb_ref, g_ref, d_ref, nm_ref, nv_ref):
        mine = pl.program_id(0) == lax.axis_index("c")
        g = jnp.where(mine, ga_ref[...], gb_ref[...])
        g_ref[...] = g
        d_ref[...], nm_ref[...], nv_ref[...] = _adamw_math(w_ref[...], g, m_ref[...], v_ref[...])

    spec = pl.BlockSpec((tr, C), lambda h, i: (h * nb + i, 0))
    gspec = pl.BlockSpec((tr, C), lambda h, i: (i, 0))
    return _call(name, body, (2, nb), [spec] * 3 + [gspec] * 2, [spec] * 4, [jax.ShapeDtypeStruct((R, C), F32)] * 4,
                 (w, m, v, g_here, g_there))


class _PackLayout:
    def __init__(self, n_cc, n_grp, G, widths):
        self.dw_rows = (0, HALO)
        self.wp_rows = (HALO, HALO + G)
        self.n_cc, self.n_grp, self.G = n_cc, n_grp, G
        self.vec = {}
        r = HALO + G
        for name, width in widths:
            self.vec[name] = (r, width)
            r += width // PACK_W
        self.rows = -(-r // 8) * 8


def _pack_small(layout, dwdw, dwp, vecs):
    names = list(vecs)

    def body(*refs):
        dw_ref, wp_ref = refs[0], refs[1]
        vec_refs = refs[2 : 2 + len(names)]
        o_ref = refs[-1]
        o_ref[...] = jnp.zeros_like(o_ref)
        for j in range(layout.n_cc):
            o_ref[layout.dw_rows[0] : layout.dw_rows[1], j * LANES : (j + 1) * LANES] = dw_ref[j]
        for i in range(layout.n_grp):
            o_ref[layout.wp_rows[0] : layout.wp_rows[1], i * layout.G : (i + 1) * layout.G] = wp_ref[i]
        for name, ref in zip(names, vec_refs):
            r, width = layout.vec[name]
            for h in range(width // PACK_W):
                o_ref[r + h : r + h + 1, :] = ref[:, h * PACK_W : (h + 1) * PACK_W]

    return pl.pallas_call(
        body,
        name="pack_small",
        out_shape=jax.ShapeDtypeStruct((layout.rows, PACK_W), F32),
    )(dwdw, dwp, *[vecs[k] for k in names])


def _adamw_small(layout, g_here, g_there, w_dw, m_dw, v_dw, w_pool, m_pool, v_pool, vec_w, vec_m, vec_v):
    names = list(vec_w)
    nv = len(names)

    def body(*refs):
        ga_ref, gb_ref = refs[0], refs[1]
        wdw, mdw, vdw, wp, mp, vp = refs[2:8]
        vw, vm, vv = refs[8 : 8 + nv], refs[8 + nv : 8 + 2 * nv], refs[8 + 2 * nv : 8 + 3 * nv]
        outs = refs[8 + 3 * nv :]
        acc = outs[-1]
        acc[...] = ga_ref[...] + gb_ref[...]

        def emit(o, g, w, m, v, idx=()):
            res = (g,) + _adamw_math(w, g, m, v)
            for ref, val in zip(o, res):
                ref[idx] = val

        me = 2 * lax.axis_index("x") + lax.axis_index("y")
        for j in range(layout.n_cc):

            @pl.when(me == j)
            def _(j=j):
                g = acc[layout.dw_rows[0] : layout.dw_rows[1], j * LANES : (j + 1) * LANES]
                emit(outs[0:4], g, wdw[...], mdw[...], vdw[...], idx=...)

        for i in range(layout.n_grp):
            g = acc[layout.wp_rows[0] : layout.wp_rows[1], i * layout.G : (i + 1) * layout.G]
            emit(outs[4:8], g, wp[i], mp[i], vp[i], idx=i)
        for q, name in enumerate(names):
            r, width = layout.vec[name]
            for h in range(width // PACK_W):
                ls = slice(h * PACK_W, (h + 1) * PACK_W)
                g = acc[r + h : r + h + 1, :]
                emit(outs[8 + 4 * q : 12 + 4 * q], g, vw[q][:, ls], vm[q][:, ls], vv[q][:, ls], idx=(slice(None), ls))

    shapes = [w_dw.shape] * 4 + [w_pool.shape] * 4
    for name in names:
        shapes += [vec_w[name].shape] * 4
    return pl.pallas_call(
        body,
        name="adamw_small",
        out_shape=[jax.ShapeDtypeStruct(s, F32) for s in shapes],
        scratch_shapes=[pltpu.VMEM(g_here.shape, F32)],
    )(g_here, g_there, w_dw, m_dw, v_dw, w_pool, m_pool, v_pool,
      *[vec_w[k] for k in names], *[vec_m[k] for k in names], *[vec_v[k] for k in names])


def _allreduce_adamw_row(g_part, w, m, v, loss_part, comm=()):
    D = w.shape[1]
    n_pairs = N_DEV - 1

    def body(g_ref, w_ref, m_ref, v_ref, l_ref, go_ref, d_ref, nm_ref, nv_ref, lo_ref, land_g, land_l, sems):
        x, y, c = _place()
        copies = []
        for q, (src, land) in enumerate(((g_ref, land_g), (l_ref, land_l))):
            for r in range(1, N_DEV):
                fx, fy, fc = (r >> 2) & 1, (r >> 1) & 1, r & 1
                peer = (1 - x if fx else x, 1 - y if fy else y, 1 - c if fc else c)
                cp = _remote(src, land.at[r], sems, 2 * (q * n_pairs + r - 1), peer)
                cp.start()
                copies.append(cp)
        for cp in copies:
            cp.wait()

        def total(src, land):
            row = lambda r: src[...] if r == 0 else land[r]
            return ((row(0) + row(4)) + (row(2) + row(6))) + ((row(1) + row(5)) + (row(3) + row(7)))

        g = total(g_ref, land_g)
        go_ref[...] = g
        d_ref[...], nm_ref[...], nv_ref[...] = _adamw_math(w_ref[...], g, m_ref[...], v_ref[...])
        lo_ref[...] = total(l_ref, land_l)

    vm = pl.BlockSpec(memory_space=pltpu.VMEM)
    return _call(
        "allreduce_adamw_g_mix",
        body,
        (),
        [vm] * 5,
        [vm] * 5,
        [jax.ShapeDtypeStruct((1, D), F32)] * 4 + [jax.ShapeDtypeStruct(loss_part.shape, F32)],
        (g_part, w, m, v, loss_part),
        scratch=[pltpu.VMEM((N_DEV, 1, D), F32), pltpu.VMEM((N_DEV,) + loss_part.shape, F32),
                 pltpu.SemaphoreType.DMA((4 * n_pairs,))],
        comm=comm,
    )


def kernel(x, g_mix, w_in, b_in, w_dw, b_dw, ln_g, ln_b, w_pool, s_pool, w_out, g_ffn, w_gate, w_up, w_down, g_final, loss_target, m_g_mix, m_w_in, m_b_in, m_w_dw, m_b_dw, m_ln_g, m_ln_b, m_w_pool, m_s_pool, m_w_out, m_g_ffn, m_w_gate, m_w_up, m_w_down, m_g_final, v_g_mix, v_w_in, v_b_in, v_w_dw, v_b_dw, v_ln_g, v_ln_b, v_w_pool, v_s_pool, v_w_out, v_g_ffn, v_w_gate, v_w_up, v_w_down, v_g_final):
    x2 = x[0]
    target = loss_target[0]
    T, D = x2.shape
    w_in2, w_out2, w_down2, w_dw2 = w_in[0], w_out[0], w_down[0], w_dw[0]
    w_gateT, w_upT = w_gate[0].T, w_up[0].T
    CI = w_in2.shape[1] * N_CHIPS
    DM = w_out2.shape[0] * N_CHIPS
    F = w_down2.shape[0] * N_CHIPS
    KW, dw_cols = w_dw2.shape
    assert dw_cols == LANES
    n_grp, G = w_pool.shape[1], w_pool.shape[-1]
    w_pool3 = w_pool[0]
    g_final2 = g_final.reshape(1, D)

    me = (2 * lax.axis_index("x") + lax.axis_index("y")).astype(jnp.int32).reshape(1)

    f_in, f_out, f_gate, f_up, f_down, f_dw = _place_weights(
        [w_in2, w_out2, w_gateT, w_upT, w_down2, w_dw2], ["rows"] * 5 + ["lead"],
        [(CI, D), (DM, D), (F, D), (F, D), (F, D), (N_CHIPS, KW, dw_cols)], [BF16] * 5 + [F32], [True] + [False] * 5)
    w_inT_b, w_dw4 = _gather_now([f_in, f_dw], ["rows", "lead"], [True, False])
    w_pool_b = w_pool3.astype(BF16)
    (z, xn_b), (f_out, f_gate) = _in_proj(
        x2, g_mix, w_inT_b, b_in,
        comm=[_GatherIci([f_out], ["rows"], [True]), _GatherIci([f_gate], ["rows"], [True], which=(2,))])
    (y_b, v), (w_out_b, f_gate, f_up) = _seq_fwd(
        z, w_dw4, b_dw, ln_g, ln_b, w_pool_b, s_pool,
        comm=[_GatherD2d([f_out], ["rows"]), _GatherIci([f_gate], ["rows"], [True], which=(0, 1)),
              _GatherIci([f_up], ["rows"], [True])])
    (h1, hn_b), (wgT_b, wuT_b, f_down) = _out_proj(
        y_b, x2, w_out_b, g_ffn,
        comm=[_GatherD2d([f_gate, f_up], ["rows"] * 2), _GatherIci([f_down], ["rows"], [True])])
    (g_b, u_b, a_b), (wd_b,) = _gate_up(hn_b, wgT_b, wuT_b, comm=[_GatherD2d([f_down], ["rows"])])
    (dh2, dh2_b, loss_part, d_g_final), _ = _down_loss(a_b, wd_b, h1, target, g_final2)

    core = lax.axis_index("c").astype(jnp.int32).reshape(1)
    gw_down, _ = _weight_grad("grad_w_down", a_b, dh2_b)
    (dg_b, du_b), (x_down,) = _ffn_bwd_act(dh2_b, wd_b, g_b, u_b, comm=[_HalfSwap([gw_down])])
    pr_down = _pair_sum("pair_w_down", gw_down, x_down, core)
    gw_gateT, (p_down_xy,) = _weight_grad("grad_w_gate", dg_b, hn_b, comm=[_Scatter([pr_down], ["lead"], which=(0, 1))])
    gw_upT, (p_down_d, x_gate) = _weight_grad(
        "grad_w_up", du_b, hn_b, comm=[_Scatter([pr_down], ["lead"], which=(2,)), _HalfSwap([gw_gateT])])
    pr_gate = _pair_sum("pair_w_gate", gw_gateT, x_gate, core)
    sum_down = _sum_parts("sum_w_down", pr_down, "lead", [p_down_xy, p_down_d], me)
    (dh1, dh1_b, dy, d_g_ffn), (p_gate, x_up, oth_down) = _ffn_bwd_in(
        dg_b, du_b, wgT_b, wuT_b, h1, dh2, g_ffn, w_out_b,
        comm=[_Scatter([pr_gate], ["lead"]), _HalfSwap([gw_upT]), _Swap([sum_down])])
    gw_out, _ = _weight_grad("grad_w_out", y_b, dh1_b)
    pr_up = _pair_sum("pair_w_up", gw_upT, x_up, core)
    sum_gate = _sum_parts("sum_w_gate", pr_gate, "lead", [p_gate], me)
    res = {}
    res["w_down"], _ = _adamw_halves("adamw_w_down", w_down2, m_w_down[0], v_w_down[0], sum_down, oth_down)
    (dz_b, d_wdw, d_bdw, d_lng, d_lnb, d_wp, d_sp, d_bin), (p_up, x_out, oth_gate) = _seq_bwd(
        z, dy, v, w_dw4, ln_g, ln_b, w_pool_b, s_pool,
        comm=[_Scatter([pr_up], ["lead"]), _HalfSwap([gw_out]), _Swap([sum_gate])])
    pr_out = _pair_sum("pair_w_out", gw_out, x_out, core)
    vec_grads = {"b_dw": d_bdw, "ln_g": d_lng, "ln_b": d_lnb, "s_pool": d_sp, "g_ffn": d_g_ffn, "g_final": d_g_final, "b_in": d_bin}
    layout = _PackLayout(dw_cols * N_CHIPS // LANES, n_grp, G, [(k, a.shape[1]) for k, a in vec_grads.items()])
    pack = _pack_small(layout, d_wdw, d_wp, vec_grads)
    sum_up = _sum_parts("sum_w_up", pr_up, "lead", [p_up], me)
    res["w_gate"], _ = _adamw_halves("adamw_w_gate", w_gateT, m_w_gate[0].T, v_w_gate[0].T, sum_gate, oth_gate)
    gw_inT, (p_out, p_small, oth_up) = _weight_grad(
        "grad_w_in", dz_b, xn_b, comm=[_Scatter([pr_out], ["lead"]), _Scatter([pack], ["all"]), _Swap([sum_up])])
    sum_out = _sum_parts("sum_w_out", pr_out, "lead", [p_out], me)
    sum_small = _sum_parts("sum_small", pack, "all", [p_small], me)
    (grad_x, d_g_mix), (p_in, oth_out, oth_small) = _in_proj_bwd(
        dz_b, w_inT_b, x2, dh1, g_mix, comm=[_Scatter([gw_inT], ["rows"]), _Swap([sum_out, sum_small])])
    res["w_up"], _ = _adamw_halves("adamw_w_up", w_upT, m_w_up[0].T, v_w_up[0].T, sum_up, oth_up)
    res["w_out"], _ = _adamw_halves("adamw_w_out", w_out2, m_w_out[0], v_w_out[0], sum_out, oth_out)
    sum_in = _sum_parts("sum_w_in", gw_inT, "rows", [p_in], me)
    (*res["g_mix"], loss_row), (oth_in,) = _allreduce_adamw_row(
        d_g_mix, g_mix, m_g_mix, v_g_mix, loss_part, comm=[_Swap([sum_in])])
    loss = loss_row[0, 0]
    res["w_in"], _ = _adamw("adamw_w_in", w_in2, m_w_in[0], v_w_in[0], sum_in, oth_in, g_transposed=True)

    pad_dw = lambda a: jnp.pad(a[0], ((0, HALO - KW), (0, 0)))
    vec_w = {"b_dw": b_dw, "ln_g": ln_g, "ln_b": ln_b, "s_pool": s_pool, "g_ffn": g_ffn, "g_final": g_final2, "b_in": b_in}
    vec_m = {"b_dw": m_b_dw, "ln_g": m_ln_g, "ln_b": m_ln_b, "s_pool": m_s_pool, "g_ffn": m_g_ffn,
             "g_final": m_g_final.reshape(1, D), "b_in": m_b_in}
    vec_v = {"b_dw": v_b_dw, "ln_g": v_ln_g, "ln_b": v_ln_b, "s_pool": v_s_pool, "g_ffn": v_g_ffn,
             "g_final": v_g_final.reshape(1, D), "b_in": v_b_in}
    small = _adamw_small(layout, sum_small, oth_small, pad_dw(w_dw), pad_dw(m_w_dw), pad_dw(v_w_dw),
                         w_pool3, m_w_pool[0], v_w_pool[0], vec_w, vec_m, vec_v)
    res["w_dw"] = [a[:KW][None] for a in small[0:4]]
    res["w_pool"] = [a[None] for a in small[4:8]]
    for q, k in enumerate(vec_w):
        res[k] = list(small[8 + 4 * q : 12 + 4 * q])
    res["g_final"] = [a.reshape(D) for a in res["g_final"]]
    for k in ("w_in", "w_out", "w_down"):
        res[k] = [a[None] for a in res[k]]
    for k in ("w_gate", "w_up"):
        res[k] = [a.T[None] for a in res[k]]

    order = ["g_mix", "w_in", "b_in", "w_dw", "b_dw", "ln_g", "ln_b", "w_pool", "s_pool", "w_out", "g_ffn", "w_gate", "w_up", "w_down", "g_final"]
    outs = [loss, grad_x[None]]
    for q in range(4):
        outs += [res[k][q] for k in order]
    return tuple(outs)
```

```python
import jax
import jax.numpy as jnp
from jax import lax
from jax.experimental import pallas as pl
from jax.experimental.pallas import tpu as pltpu

F32 = jnp.float32
BF16 = jnp.bfloat16
MESH = pl.DeviceIdType.MESH
ANY = pl.BlockSpec(memory_space=pl.ANY)

RMS_EPS = 1e-6
LN_EPS = 1e-5
POOL_WINDOWS = (2, 4, 8, 16)
ADAM_LR = 0.001
ADAM_B1 = 0.9
ADAM_B2 = 0.999
ADAM_EPS = 1e-08
ADAM_WD = 0.01
ADAM_STEP = 10

LANES = 128
SUBLANES = 8
HALO = 32
CONV_ROWS = 64
HIDDEN_CHUNK = 512
VMEM_LIMIT = 56 * 1024 * 1024
PACK_W = 512
N_CHIPS = 4
N_DEV = 8


def _tile(n, want, mult=8):
    t = min(n, want)
    while n % t or t % mult:
        t -= 1
    return t


def _sigmoid(x):
    return 1.0 / (1.0 + jnp.exp(-x))


def _dot(a, b, dims):
    return lax.dot_general(a, b, (dims, ((), ())), preferred_element_type=F32)


NN = ((1,), (0,))
NT = ((1,), (1,))
TN = ((0,), (0,))


def _rms_bwd(x, g, dy):
    r = lax.rsqrt(jnp.mean(x * x, axis=-1, keepdims=True) + RMS_EPS)
    xh = x * r
    gy = dy * g
    dx = r * (gy - xh * jnp.mean(gy * xh, axis=-1, keepdims=True))
    return dx, dy * xh


def _accumulate(ref, first, val):
    @pl.when(first)
    def _():
        ref[...] = val

    @pl.when(jnp.logical_not(first))
    def _():
        ref[...] += val


def _place():
    return lax.axis_index("x"), lax.axis_index("y"), lax.axis_index("c")


def _other_chips(x, y):
    return [(1 - x, y), (x, 1 - y), (1 - x, 1 - y)]


def _rows(ref, start, n):
    return ref.at[pl.ds(pl.multiple_of(start, 16), n)]


def _window(ref, how, k, c=None):
    if how == "all":
        return ref
    if how == "lead":
        return ref.at[k]
    if how == "rows":
        n = ref.shape[0] // N_CHIPS
        if c is None:
            return _rows(ref, k * n, n)
        return _rows(ref, k * n + c * (n // 2), n // 2)
    n = ref.shape[1] // N_CHIPS
    cols = pl.ds(pl.multiple_of(k * n, LANES), n)
    if c is None:
        return ref.at[:, cols]
    h = ref.shape[0] // 2
    return ref.at[pl.ds(pl.multiple_of(c * h, 16), h), cols]


def _remote(src, dst, sems, s, device):
    return pltpu.make_async_remote_copy(
        src_ref=src, dst_ref=dst, send_sem=sems.at[s], recv_sem=sems.at[s + 1], device_id=device, device_id_type=MESH)


class _GatherIci:
    aliased = True

    def __init__(self, fulls, hows, splits, which=(0, 1, 2)):
        self.fulls, self.hows, self.splits, self.which = list(fulls), list(hows), list(splits), tuple(which)

    def inputs(self):
        return self.fulls

    def out_shapes(self):
        return [jax.ShapeDtypeStruct(a.shape, a.dtype) for a in self.fulls]

    def n_sems(self):
        return 6 * len(self.fulls)

    def build(self, ins, outs, sems, base):
        x, y, c = _place()
        me = 2 * x + y
        chips = _other_chips(x, y)
        starts, waits = [], []
        for a, (how, sp) in enumerate(zip(self.hows, self.splits)):
            half = c if sp else None
            mine = _window(outs[a], how, me, half)
            for j in self.which:
                px, py = chips[j]
                s = base + 6 * a + 2 * j
                cp = _remote(mine, mine, sems, s, (px, py, c))
                landing = _remote(mine, _window(outs[a], how, 2 * px + py, half), sems, s, (px, py, c))
                starts.append(cp.start)
                waits += [landing.wait_recv, cp.wait_send]
        return starts, waits


class _GatherD2d:
    aliased = True

    def __init__(self, fulls, hows):
        self.fulls, self.hows = list(fulls), list(hows)

    def inputs(self):
        return self.fulls

    def out_shapes(self):
        return [jax.ShapeDtypeStruct(a.shape, a.dtype) for a in self.fulls]

    def n_sems(self):
        return 6 * len(self.fulls)

    def build(self, ins, outs, sems, base):
        x, y, c = _place()
        starts, waits = [], []
        for a, how in enumerate(self.hows):
            for j, (px, py) in enumerate(_other_chips(x, y)):
                s = base + 6 * a + 2 * j
                got = _window(outs[a], how, 2 * px + py, c)
                cp = _remote(got, got, sems, s, (x, y, 1 - c))
                landing = _remote(got, _window(outs[a], how, 2 * px + py, 1 - c), sems, s, (x, y, 1 - c))
                starts.append(cp.start)
                waits += [landing.wait_recv, cp.wait_send]
        return starts, waits


def _part_shape(a, how):
    if how == "all":
        return a.shape
    if how == "rows":
        return (a.shape[0] // N_CHIPS, a.shape[1])
    return (a.shape[0], a.shape[1] // N_CHIPS)


class _Scatter:
    aliased = False

    def __init__(self, fulls, hows, which=(0, 1, 2)):
        self.fulls, self.hows, self.which = list(fulls), list(hows), tuple(which)

    def inputs(self):
        return self.fulls

    def out_shapes(self):
        return [jax.ShapeDtypeStruct((len(self.which),) + _part_shape(a, h), a.dtype) for a, h in zip(self.fulls, self.hows)]

    def n_sems(self):
        return 6 * len(self.fulls)

    def build(self, ins, outs, sems, base):
        x, y, c = _place()
        chips = _other_chips(x, y)
        starts, waits = [], []
        for a, how in enumerate(self.hows):
            for slot, j in enumerate(self.which):
                px, py = chips[j]
                cp = _remote(_window(ins[a], how, 2 * px + py), outs[a].at[slot], sems, base + 6 * a + 2 * j, (px, py, c))
                starts.append(cp.start)
                waits += [cp.wait_recv, cp.wait_send]
        return starts, waits


class _Swap:
    aliased = False

    def __init__(self, arrays):
        self.arrays = list(arrays)

    def inputs(self):
        return self.arrays

    def out_shapes(self):
        return [jax.ShapeDtypeStruct(a.shape, a.dtype) for a in self.arrays]

    def n_sems(self):
        return 2 * len(self.arrays)

    def build(self, ins, outs, sems, base):
        x, y, c = _place()
        starts, waits = [], []
        for a in range(len(ins)):
            cp = _remote(ins[a], outs[a], sems, base + 2 * a, (x, y, 1 - c))
            starts.append(cp.start)
            waits += [cp.wait_recv, cp.wait_send]
        return starts, waits


def _call(name, body, grid, in_specs, out_specs, out_shape, args, scratch=(), comm=()):
    comm = list(comm)
    n_in, n_out, n_scr = len(args), len(out_shape), len(scratch)
    c_in = [a for op in comm for a in op.inputs()]
    c_out = [s for op in comm for s in op.out_shapes()]
    n_sems = sum(op.n_sems() for op in comm)
    aliases, i_in, i_out = {}, 0, 0
    for op in comm:
        if op.aliased:
            for q in range(len(op.inputs())):
                aliases[n_in + i_in + q] = n_out + i_out + q
        i_in, i_out = i_in + len(op.inputs()), i_out + len(op.out_shapes())

    def wrapped(*refs):
        ins = refs[:n_in]
        cin = refs[n_in : n_in + len(c_in)]
        o0 = n_in + len(c_in)
        outs = refs[o0 : o0 + n_out]
        cout = refs[o0 + n_out : o0 + n_out + len(c_out)]
        s0 = o0 + n_out + len(c_out)
        scr = refs[s0 : s0 + n_scr]

        def copies():
            sems = refs[s0 + n_scr]
            starts, waits = [], []
            i_in = i_out = base = 0
            for op in comm:
                ni, no = len(op.inputs()), len(op.out_shapes())
                s, w = op.build(cin[i_in : i_in + ni], cout[i_out : i_out + no], sems, base)
                starts += s
                waits += w
                i_in, i_out, base = i_in + ni, i_out + no, base + op.n_sems()
            return starts, waits

        def run_starts():
            for start in copies()[0]:
                start()

        def run_waits():
            for wait in copies()[1]:
                wait()

        if comm and grid:
            first = last = True
            for d, n in enumerate(grid):
                first = jnp.logical_and(first, pl.program_id(d) == 0)
                last = jnp.logical_and(last, pl.program_id(d) == n - 1)
            pl.when(first)(run_starts)
        elif comm:
            run_starts()
        if body is not None:
            body(*ins, *outs, *scr)
        if comm and grid:
            pl.when(last)(run_waits)
        elif comm:
            run_waits()

    res = pl.pallas_call(
        wrapped,
        name=name,
        grid=grid,
        in_specs=list(in_specs) + [ANY] * len(c_in),
        out_specs=list(out_specs) + [ANY] * len(c_out),
        out_shape=list(out_shape) + c_out,
        scratch_shapes=list(scratch) + ([pltpu.SemaphoreType.DMA((n_sems,))] if comm else []),
        input_output_aliases=aliases,
        compiler_params=pltpu.CompilerParams(dimension_semantics=("arbitrary",) * len(grid), vmem_limit_bytes=VMEM_LIMIT),
    )(*args, *c_in)
    return tuple(res[:n_out]), tuple(res[n_out:])


def _place_and_gather(now, later):
    items = list(now) + list(later)
    n, n_now = len(items), len(now)
    buf_shape = lambda it: it[0].shape[::-1] if it[4] else it[0].shape
    split_now = [a for a in range(n_now) if items[a][5]]

    def body(*refs):
        ins, outs = refs[:n], refs[n : 2 * n]
        stage, bufs = refs[2 * n : 3 * n - n_now], refs[3 * n - n_now : 4 * n - n_now]
        sems = refs[4 * n - n_now]
        x, y, c = _place()
        me = 2 * x + y
        chips = _other_chips(x, y)
        loads = [pltpu.make_async_copy(ins[a], stage[a - n_now], sems.at[a]) for a in range(n_now, n)]
        for ld in loads:
            ld.start()
        pending = []

        def place(a, val):
            _, how, _, dtype, transposed, _ = items[a]
            bufs[a][...] = (val.T if transposed else val).astype(dtype)
            cp = pltpu.make_async_copy(bufs[a], _window(outs[a], how, me), sems.at[n + a])
            cp.start()
            pending.append(cp.wait)

        arrivals = []
        for a in range(n_now):
            place(a, ins[a][...])
            how, split = items[a][1], items[a][5]
            half = c if split else None
            src = _rows(bufs[a], c * (bufs[a].shape[0] // 2), bufs[a].shape[0] // 2) if split else bufs[a]
            for j, (px, py) in enumerate(chips):
                s = 2 * n + 6 * a + 2 * j
                cp = _remote(src, _window(outs[a], how, me, half), sems, s, (px, py, c))
                landing = _remote(src, _window(outs[a], how, 2 * px + py, half), sems, s, (px, py, c))
                cp.start()
                arrivals.append(landing.wait_recv)
                pending.append(cp.wait_send)
        for a in range(n_now, n):
            loads[a - n_now].wait()
            place(a, stage[a - n_now][...])
        for wait in arrivals:
            wait()
        d2d = _GatherD2d([None] * len(split_now), [items[a][1] for a in split_now])
        starts, waits = d2d.build(None, [outs[a] for a in split_now], sems, 2 * n + 6 * n_now)
        for start in starts:
            start()
        for wait in waits + pending:
            wait()

    vm = pl.BlockSpec(memory_space=pltpu.VMEM)
    return pl.pallas_call(
        body,
        name="place_and_gather",
        in_specs=[vm] * n_now + [ANY] * (n - n_now),
        out_specs=[ANY] * n,
        out_shape=[jax.ShapeDtypeStruct(it[2], it[3]) for it in items],
        scratch_shapes=[pltpu.VMEM(it[0].shape, it[0].dtype) for it in later]
        + [pltpu.VMEM(buf_shape(it), it[3]) for it in items]
        + [pltpu.SemaphoreType.DMA((2 * n + 6 * n_now + 6 * len(split_now),))],
        compiler_params=pltpu.CompilerParams(vmem_limit_bytes=VMEM_LIMIT),
    )(*[it[0] for it in items])


def _in_proj(x, g_mix, w_inT_b, b_in, comm=()):
    T, D = x.shape
    CI = w_inT_b.shape[0]
    tm = _tile(T, 512)

    def body(x_ref, g_ref, w_ref, b_ref, z_ref, xn_ref):
        xv = x_ref[...]
        r = lax.rsqrt(jnp.mean(xv * xv, axis=-1, keepdims=True) + RMS_EPS)
        xn = (xv * r * g_ref[...]).astype(BF16)
        xn_ref[...] = xn
        z_ref[...] = _dot(xn, w_ref[...], NT) + b_ref[...]

    return _call(
        "in_proj",
        body,
        (T // tm,),
        [
            pl.BlockSpec((tm, D), lambda i: (i, 0)),
            pl.BlockSpec((1, D), lambda i: (0, 0)),
            pl.BlockSpec((CI, D), lambda i: (0, 0)),
            pl.BlockSpec((1, CI), lambda i: (0, 0)),
        ],
        [pl.BlockSpec((tm, CI), lambda i: (i, 0)), pl.BlockSpec((tm, D), lambda i: (i, 0))],
        [jax.ShapeDtypeStruct((T, CI), F32), jax.ShapeDtypeStruct((T, D), BF16)],
        (x, g_mix, w_inT_b, b_in),
        comm=comm,
    )


def _fill_shifted(scr):
    n = scr.shape[1] - SUBLANES
    for s in range(1, SUBLANES):
        scr[s, 0:n, :] = scr[0, s : s + n, :]


def _shifted_rows(scr, off, n, cs):
    s = off % SUBLANES
    return scr[s, off - s : off - s + n, cs]


def _pool_mean_minus_token(p_scr, cs, w, cnt, tt):
    tok = p_scr[HALO : HALO + tt, cs]
    s = tok
    for d in range(1, w):
        s = s + p_scr[HALO - d : HALO - d + tt, cs]
    return s / cnt - tok


def _seq_fwd(z, w_dw4, b_dw, ln_g, ln_b, w_pool_b, s_pool, comm=()):
    T, CI = z.shape
    CC = ln_g.shape[1]
    n_grp, G = w_pool_b.shape[0], w_pool_b.shape[-1]
    KW = w_dw4.shape[1]
    D = CC + n_grp * G
    tt = _tile(T, 512, HALO)
    per = tt // HALO

    def body(zc_ref, zp_ref, wdw_ref, bdw_ref, lng_ref, lnb_ref, wp_ref, sp_ref, y_ref, v_ref, u_scr, p_scr):
        i = pl.program_id(0)
        first = i == 0
        u_prev = zp_ref[:, 0:CC] * _sigmoid(zp_ref[:, CC : 2 * CC])
        u_scr[0, 0:HALO, :] = jnp.where(first, 0.0, u_prev)
        p_scr[0:HALO, :] = jnp.where(first, 0.0, zp_ref[:, 2 * CC :])
        u_scr[0, HALO:, :] = zc_ref[:, 0:CC] * _sigmoid(zc_ref[:, CC : 2 * CC])
        p_scr[HALO:, :] = zc_ref[:, 2 * CC :]
        _fill_shifted(u_scr)

        for j in range(CC // LANES):
            cs = slice(LANES * j, LANES * (j + 1))
            for rb in range(tt // CONV_ROWS):
                acc = jnp.zeros((CONV_ROWS, LANES), F32)
                for k in range(KW):
                    off = HALO - (KW - 1) + k + rb * CONV_ROWS
                    acc = acc + _shifted_rows(u_scr, off, CONV_ROWS, cs) * wdw_ref[j, k : k + 1, :]
                v_ref[rb * CONV_ROWS : (rb + 1) * CONV_ROWS, cs] = acc + bdw_ref[:, cs]

        v = v_ref[...]
        mu = jnp.mean(v, axis=-1, keepdims=True)
        d = v - mu
        var = jnp.mean(d * d, axis=-1, keepdims=True)
        ln = d * lax.rsqrt(var + LN_EPS) * lng_ref[...] + lnb_ref[...]
        y_ref[:, 0:CC] = (ln * _sigmoid(ln)).astype(BF16)

        tpos = i * tt + lax.broadcasted_iota(jnp.int32, (tt, 1), 0)
        for gi, w in enumerate(POOL_WINDOWS):
            cs = slice(G * gi, G * (gi + 1))
            cnt = jnp.minimum(tpos + 1, w).astype(F32)
            yi = _pool_mean_minus_token(p_scr, cs, w, cnt, tt)
            q = _dot(yi.astype(BF16), wp_ref[gi], NN)
            y_ref[:, CC + G * gi : CC + G * (gi + 1)] = (q * sp_ref[:, cs]).astype(BF16)

    const2 = lambda i: (0, 0)
    return _call(
        "seq_fwd",
        body,
        (T // tt,),
        [
            pl.BlockSpec((tt, CI), lambda i: (i, 0)),
            pl.BlockSpec((HALO, CI), lambda i: (jnp.maximum(i * per - 1, 0), 0)),
            pl.BlockSpec(w_dw4.shape, lambda i: (0, 0, 0)),
            pl.BlockSpec((1, CC), const2),
            pl.BlockSpec((1, CC), const2),
            pl.BlockSpec((1, CC), const2),
            pl.BlockSpec(w_pool_b.shape, lambda i: (0, 0, 0)),
            pl.BlockSpec((1, n_grp * G), const2),
        ],
        [pl.BlockSpec((tt, D), lambda i: (i, 0)), pl.BlockSpec((tt, CC), lambda i: (i, 0))],
        [jax.ShapeDtypeStruct((T, D), BF16), jax.ShapeDtypeStruct((T, CC), F32)],
        (z, z, w_dw4, b_dw, ln_g, ln_b, w_pool_b, s_pool),
        scratch=[pltpu.VMEM((SUBLANES, HALO + tt, CC), F32), pltpu.VMEM((HALO + tt, n_grp * G), F32)],
        comm=comm,
    )


def _out_proj(y_b, x, w_out_b, g_ffn, comm=()):
    T, D = x.shape
    tm = _tile(T, 512)

    def body(y_ref, x_ref, w_ref, g_ref, h1_ref, hn_ref):
        h1 = x_ref[...] + _dot(y_ref[...], w_ref[...], NN)
        h1_ref[...] = h1
        r = lax.rsqrt(jnp.mean(h1 * h1, axis=-1, keepdims=True) + RMS_EPS)
        hn_ref[...] = (h1 * r * g_ref[...]).astype(BF16)

    row = lambda i: (i, 0)
    return _call(
        "out_proj",
        body,
        (T // tm,),
        [
            pl.BlockSpec((tm, y_b.shape[1]), row),
            pl.BlockSpec((tm, D), row),
            pl.BlockSpec(w_out_b.shape, lambda i: (0, 0)),
            pl.BlockSpec((1, D), lambda i: (0, 0)),
        ],
        [pl.BlockSpec((tm, D), row), pl.BlockSpec((tm, D), row)],
        [jax.ShapeDtypeStruct((T, D), F32), jax.ShapeDtypeStruct((T, D), BF16)],
        (y_b, x, w_out_b, g_ffn),
        comm=comm,
    )


def _hidden_tile(F):
    return _tile(F, 1408, LANES)


def _gate_up(hn_b, wgT_b, wuT_b, comm=()):
    T, D = hn_b.shape
    F = wgT_b.shape[0]
    tm, tf = _tile(T, 1024), _hidden_tile(F)

    def body(hn_ref, wg_ref, wu_ref, g_ref, u_ref, a_ref):
        hn = hn_ref[...]
        for c0 in range(0, tf, HIDDEN_CHUNK):
            cs = slice(c0, min(c0 + HIDDEN_CHUNK, tf))
            gv = _dot(hn, wg_ref[cs, :], NT)
            uv = _dot(hn, wu_ref[cs, :], NT)
            g_ref[:, cs] = gv.astype(BF16)
            u_ref[:, cs] = uv.astype(BF16)
            a_ref[:, cs] = (gv * _sigmoid(gv) * uv).astype(BF16)

    wspec = pl.BlockSpec((tf, D), lambda j, i: (j, 0))
    ospec = pl.BlockSpec((tm, tf), lambda j, i: (i, j))
    return _call(
        "gate_up",
        body,
        (F // tf, T // tm),
        [pl.BlockSpec((tm, D), lambda j, i: (i, 0)), wspec, wspec],
        [ospec, ospec, ospec],
        [jax.ShapeDtypeStruct((T, F), BF16)] * 3,
        (hn_b, wgT_b, wuT_b),
        comm=comm,
    )


def _down_loss(a_b, wd_b, h1, target, g_final, comm=()):
    T, D = h1.shape
    F = a_b.shape[1]
    tm = _tile(T, 512)
    nt = T // tm

    def body(a_ref, w_ref, h1_ref, t_ref, g_ref, dh2_ref, dh2b_ref, loss_ref, dg_ref):
        i = pl.program_id(0)
        h2 = h1_ref[...] + _dot(a_ref[...], w_ref[...], NN)
        r = lax.rsqrt(jnp.mean(h2 * h2, axis=-1, keepdims=True) + RMS_EPS)
        g = g_ref[...]
        diff = h2 * r * g - t_ref[...]
        _accumulate(loss_ref, i == 0, jnp.full(loss_ref.shape, jnp.sum(diff * diff) * (0.5 / D), F32))
        dh2, dg_rows = _rms_bwd(h2, g, diff * (1.0 / D))
        dh2_ref[...] = dh2
        dh2b_ref[...] = dh2.astype(BF16)
        _accumulate(dg_ref, i == 0, jnp.sum(dg_rows, axis=0, keepdims=True))

    row = lambda i: (i, 0)
    return _call(
        "down_loss",
        body,
        (nt,),
        [
            pl.BlockSpec((tm, F), row),
            pl.BlockSpec((F, D), lambda i: (0, 0), pipeline_mode=pl.Buffered(1)),
            pl.BlockSpec((tm, D), row),
            pl.BlockSpec((tm, D), row),
            pl.BlockSpec((1, D), lambda i: (0, 0)),
        ],
        [
            pl.BlockSpec((tm, D), row),
            pl.BlockSpec((tm, D), row),
            pl.BlockSpec((1, LANES), lambda i: (0, 0)),
            pl.BlockSpec((1, D), lambda i: (0, 0)),
        ],
        [
            jax.ShapeDtypeStruct((T, D), F32),
            jax.ShapeDtypeStruct((T, D), BF16),
            jax.ShapeDtypeStruct((1, LANES), F32),
            jax.ShapeDtypeStruct((1, D), F32),
        ],
        (a_b, wd_b, h1, target, g_final),
        comm=comm,
    )


def _ffn_bwd_act(dh2_b, wd_b, g_b, u_b, comm=()):
    T, D = dh2_b.shape
    F = wd_b.shape[0]
    tm, tf = _tile(T, 1024), _hidden_tile(F)

    def body(d_ref, w_ref, g_ref, u_ref, dg_ref, du_ref):
        d = d_ref[...]
        for c0 in range(0, tf, HIDDEN_CHUNK):
            cs = slice(c0, min(c0 + HIDDEN_CHUNK, tf))
            da = _dot(d, w_ref[cs, :], NT)
            gv = g_ref[:, cs].astype(F32)
            uv = u_ref[:, cs].astype(F32)
            sg = _sigmoid(gv)
            silu = gv * sg
            dg_ref[:, cs] = (da * uv * (sg * (1.0 + gv * (1.0 - sg)))).astype(BF16)
            du_ref[:, cs] = (da * silu).astype(BF16)

    aspec = pl.BlockSpec((tm, tf), lambda j, i: (i, j))
    return _call(
        "ffn_bwd_act",
        body,
        (F // tf, T // tm),
        [pl.BlockSpec((tm, D), lambda j, i: (i, 0)), pl.BlockSpec((tf, D), lambda j, i: (j, 0)), aspec, aspec],
        [aspec, aspec],
        [jax.ShapeDtypeStruct((T, F), BF16)] * 2,
        (dh2_b, wd_b, g_b, u_b),
        comm=comm,
    )


def _ffn_bwd_in(dg_b, du_b, wgT_b, wuT_b, h1, dh2, g_ffn, w_out_b, comm=()):
    T, D = h1.shape
    F = wgT_b.shape[0]
    DM = w_out_b.shape[0]
    tm = _tile(T, 512)

    def body(dg_ref, du_ref, wg_ref, wu_ref, h1_ref, dh2_ref, g_ref, wo_ref, dh1_ref, dh1b_ref, dy_ref, dgf_ref):
        i = pl.program_id(0)
        dhn = _dot(dg_ref[...], wg_ref[...], NN) + _dot(du_ref[...], wu_ref[...], NN)
        dx, dg_rows = _rms_bwd(h1_ref[...], g_ref[...], dhn)
        dh1 = dh2_ref[...] + dx
        dh1b = dh1.astype(BF16)
        dh1_ref[...] = dh1
        dh1b_ref[...] = dh1b
        dy_ref[...] = _dot(dh1b, wo_ref[...], NT)
        _accumulate(dgf_ref, i == 0, jnp.sum(dg_rows, axis=0, keepdims=True))

    row = lambda i: (i, 0)
    const = lambda i: (0, 0)
    return _call(
        "ffn_bwd_in",
        body,
        (T // tm,),
        [
            pl.BlockSpec((tm, F), row),
            pl.BlockSpec((tm, F), row),
            pl.BlockSpec((F, D), const, pipeline_mode=pl.Buffered(1)),
            pl.BlockSpec((F, D), const, pipeline_mode=pl.Buffered(1)),
            pl.BlockSpec((tm, D), row),
            pl.BlockSpec((tm, D), row),
            pl.BlockSpec((1, D), const),
            pl.BlockSpec((DM, D), const, pipeline_mode=pl.Buffered(1)),
        ],
        [pl.BlockSpec((tm, D), row), pl.BlockSpec((tm, D), row), pl.BlockSpec((tm, DM), row), pl.BlockSpec((1, D), const)],
        [
            jax.ShapeDtypeStruct((T, D), F32),
            jax.ShapeDtypeStruct((T, D), BF16),
            jax.ShapeDtypeStruct((T, DM), F32),
            jax.ShapeDtypeStruct((1, D), F32),
        ],
        (dg_b, du_b, wgT_b, wuT_b, h1, dh2, g_ffn, w_out_b),
        comm=comm,
    )


def _seq_bwd(z, dy, v, w_dw4, ln_g, ln_b, w_pool_b, s_pool, comm=()):
    T, CI = z.shape
    CC = ln_g.shape[1]
    n_grp, G = w_pool_b.shape[0], w_pool_b.shape[-1]
    CP = n_grp * G
    KW = w_dw4.shape[1]
    n_cc = CC // LANES
    D = CC + CP
    tt = _tile(T, 512, HALO)
    per = tt // HALO
    n_tiles = T // tt
    last_halo = T // HALO - 1

    def body(zc_ref, zp_ref, dyc_ref, dyn_ref, vc_ref, vn_ref, wdw_ref, lng_ref, lnb_ref, wp_ref, sp_ref,
             dz_ref, dwdw_ref, dbdw_ref, dlng_ref, dlnb_ref, dwp_ref, dsp_ref, dbin_ref,
             dv_scr, u_scr, p_scr, g_scr, dw_scr):
        i = pl.program_id(0)
        first = i == 0
        last = i == n_tiles - 1
        lng, lnb = lng_ref[...], lnb_ref[...]

        def conv_pre(vv, dyc):
            mu = jnp.mean(vv, axis=-1, keepdims=True)
            d = vv - mu
            rs = lax.rsqrt(jnp.mean(d * d, axis=-1, keepdims=True) + LN_EPS)
            xh = d * rs
            ln = xh * lng + lnb
            sg = _sigmoid(ln)
            dln = dyc * (sg * (1.0 + ln * (1.0 - sg)))
            dxh = dln * lng
            dv = rs * (dxh - jnp.mean(dxh, axis=-1, keepdims=True) - xh * jnp.mean(dxh * xh, axis=-1, keepdims=True))
            return dv, dln, xh

        dv_c, dln_c, xh_c = conv_pre(vc_ref[...], dyc_ref[:, 0:CC])
        dv_scr[0, 0:tt, :] = dv_c
        dv_n, _, _ = conv_pre(vn_ref[...], dyn_ref[:, 0:CC])
        dv_scr[0, tt:, :] = jnp.where(last, 0.0, dv_n)
        _fill_shifted(dv_scr)
        _accumulate(dlng_ref, first, jnp.sum(dln_c * xh_c, axis=0, keepdims=True))
        _accumulate(dlnb_ref, first, jnp.sum(dln_c, axis=0, keepdims=True))
        _accumulate(dbdw_ref, first, jnp.sum(dv_c, axis=0, keepdims=True))

        u_scr[...] = zc_ref[:, 0:CC] * _sigmoid(zc_ref[:, CC : 2 * CC])

        @pl.when(first)
        def _():
            dw_scr[...] = jnp.zeros_like(dw_scr)

        for j in range(n_cc):
            cs = slice(LANES * j, LANES * (j + 1))
            gs = slice(CC + LANES * j, CC + LANES * (j + 1))
            dbin_a = jnp.zeros((1, LANES), F32)
            dbin_g = jnp.zeros((1, LANES), F32)
            for rb in range(tt // CONV_ROWS):
                rows = slice(rb * CONV_ROWS, (rb + 1) * CONV_ROWS)
                u_blk = u_scr[rows, cs]
                du = jnp.zeros((CONV_ROWS, LANES), F32)
                for k in range(KW):
                    off = rb * CONV_ROWS + (KW - 1) - k
                    d = _shifted_rows(dv_scr, off, CONV_ROWS, cs)
                    du = du + d * wdw_ref[j, k : k + 1, :]
                    dw_scr[j * HALO + k] += jnp.sum((u_blk * d).reshape(CONV_ROWS // 8, 8, LANES), axis=0)
                a = zc_ref[rows, cs]
                sg = _sigmoid(zc_ref[rows, gs])
                da = du * sg
                dgate = du * a * sg * (1.0 - sg)
                dz_ref[rows, cs] = da.astype(BF16)
                dz_ref[rows, gs] = dgate.astype(BF16)
                dbin_a = dbin_a + jnp.sum(da, axis=0, keepdims=True)
                dbin_g = dbin_g + jnp.sum(dgate, axis=0, keepdims=True)
            _accumulate(dbin_ref.at[:, cs], first, dbin_a)
            _accumulate(dbin_ref.at[:, gs], first, dbin_g)

        @pl.when(last)
        def _():
            dwdw_ref[...] = jnp.sum(dw_scr[...], axis=1).reshape(dwdw_ref.shape)

        p_scr[0:HALO, :] = jnp.where(first, 0.0, zp_ref[:, 2 * CC :])
        p_scr[HALO:, :] = zc_ref[:, 2 * CC :]
        tpos = i * tt + lax.broadcasted_iota(jnp.int32, (tt, 1), 0)
        for gi, w in enumerate(POOL_WINDOWS):
            cs = slice(G * gi, G * (gi + 1))
            ys = slice(CC + G * gi, CC + G * (gi + 1))
            ps = slice(2 * CC + G * gi, 2 * CC + G * (gi + 1))
            cnt = jnp.minimum(tpos + 1, w).astype(F32)
            yib = _pool_mean_minus_token(p_scr, cs, w, cnt, tt).astype(BF16)
            wp = wp_ref[gi]
            sp = sp_ref[:, cs]
            dyp = dyc_ref[:, ys]
            q = _dot(yib, wp, NN)
            _accumulate(dsp_ref.at[:, cs], first, jnp.sum(dyp * q, axis=0, keepdims=True))
            dq_c = (dyp * sp).astype(BF16)
            dq_n = (jnp.where(last, 0.0, dyn_ref[:, ys]) * sp).astype(BF16)
            _accumulate(dwp_ref.at[gi], first, _dot(yib, dq_c, TN))
            dyi_c = _dot(dq_c, wp, NT)
            g_scr[0:tt, cs] = dyi_c / cnt
            g_scr[tt:, cs] = _dot(dq_n, wp, NT) * (1.0 / w)
            dp = -dyi_c
            for d in range(w):
                dp = dp + g_scr[d : d + tt, cs]
            dz_ref[:, ps] = dp.astype(BF16)
            _accumulate(dbin_ref.at[:, ps], first, jnp.sum(dp, axis=0, keepdims=True))

    cur = lambda i: (i, 0)
    prev = lambda i: (jnp.maximum(i * per - 1, 0), 0)
    nxt = lambda i: (jnp.minimum((i + 1) * per, last_halo), 0)
    c2 = lambda i: (0, 0)
    c3 = lambda i: (0, 0, 0)
    return _call(
        "seq_bwd",
        body,
        (n_tiles,),
        [
            pl.BlockSpec((tt, CI), cur),
            pl.BlockSpec((HALO, CI), prev),
            pl.BlockSpec((tt, D), cur),
            pl.BlockSpec((HALO, D), nxt),
            pl.BlockSpec((tt, CC), cur),
            pl.BlockSpec((HALO, CC), nxt),
            pl.BlockSpec(w_dw4.shape, c3),
            pl.BlockSpec((1, CC), c2),
            pl.BlockSpec((1, CC), c2),
            pl.BlockSpec(w_pool_b.shape, c3),
            pl.BlockSpec((1, CP), c2),
        ],
        [
            pl.BlockSpec((tt, CI), cur),
            pl.BlockSpec((n_cc, HALO, LANES), c3),
            pl.BlockSpec((1, CC), c2),
            pl.BlockSpec((1, CC), c2),
            pl.BlockSpec((1, CC), c2),
            pl.BlockSpec((n_grp, G, G), c3),
            pl.BlockSpec((1, CP), c2),
            pl.BlockSpec((1, CI), c2),
        ],
        [
            jax.ShapeDtypeStruct((T, CI), BF16),
            jax.ShapeDtypeStruct((n_cc, HALO, LANES), F32),
            jax.ShapeDtypeStruct((1, CC), F32),
            jax.ShapeDtypeStruct((1, CC), F32),
            jax.ShapeDtypeStruct((1, CC), F32),
            jax.ShapeDtypeStruct((n_grp, G, G), F32),
            jax.ShapeDtypeStruct((1, CP), F32),
            jax.ShapeDtypeStruct((1, CI), F32),
        ],
        (z, z, dy, dy, v, v, w_dw4, ln_g, ln_b, w_pool_b, s_pool),
        scratch=[
            pltpu.VMEM((SUBLANES, tt + HALO, CC), F32),
            pltpu.VMEM((tt, CC), F32),
            pltpu.VMEM((HALO + tt, CP), F32),
            pltpu.VMEM((tt + HALO, CP), F32),
            pltpu.VMEM((n_cc * HALO, 8, LANES), F32),
        ],
        comm=comm,
    )


def _in_proj_bwd(dz_b, w_inT_b, x, dh1, g_mix, comm=()):
    T, D = x.shape
    CI = w_inT_b.shape[0]
    tm = _tile(T, 512)

    def body(dz_ref, w_ref, x_ref, dh1_ref, g_ref, dx_ref, dg_ref):
        i = pl.program_id(0)
        dxn = _dot(dz_ref[...], w_ref[...], NN)
        dx, dg_rows = _rms_bwd(x_ref[...], g_ref[...], dxn)
        dx_ref[...] = dh1_ref[...] + dx
        _accumulate(dg_ref, i == 0, jnp.sum(dg_rows, axis=0, keepdims=True))

    row = lambda i: (i, 0)
    const = lambda i: (0, 0)
    return _call(
        "in_proj_bwd",
        body,
        (T // tm,),
        [
            pl.BlockSpec((tm, CI), row),
            pl.BlockSpec((CI, D), const),
            pl.BlockSpec((tm, D), row),
            pl.BlockSpec((tm, D), row),
            pl.BlockSpec((1, D), const),
        ],
        [pl.BlockSpec((tm, D), row), pl.BlockSpec((1, D), const)],
        [jax.ShapeDtypeStruct((T, D), F32), jax.ShapeDtypeStruct((1, D), F32)],
        (dz_b, w_inT_b, x, dh1, g_mix),
        comm=comm,
    )


def _weight_grad(name, a_b, b_b, comm=()):
    T, N1 = a_b.shape
    N2 = b_b.shape[1]
    t1 = _tile(N1, 1408, LANES)
    tk = _tile(T, 2048)
    nk = T // tk

    def body(a_ref, b_ref, o_ref, acc):
        k = pl.program_id(1)
        _accumulate(acc, k == 0, _dot(a_ref[...], b_ref[...], TN))

        @pl.when(k == nk - 1)
        def _():
            o_ref[...] = acc[...].astype(BF16)

    (out,), rest = _call(
        name,
        body,
        (N1 // t1, nk),
        [pl.BlockSpec((tk, t1), lambda n, k: (k, n)), pl.BlockSpec((tk, N2), lambda n, k: (k, 0))],
        [pl.BlockSpec((t1, N2), lambda n, k: (n, 0))],
        [jax.ShapeDtypeStruct((N1, N2), BF16)],
        (a_b, b_b),
        scratch=[pltpu.VMEM((t1, N2), F32)],
        comm=comm,
    )
    return out, rest


def _sum_parts(name, full, how, parts, me):
    _, R, C = parts[0].shape
    tr = _tile(R, 512)
    nb = R // tr
    where = [(q, r) for q, p in enumerate(parts) for r in range(p.shape[0])]
    assert len(where) == 3

    def body(me_ref, own_ref, *refs):
        o_ref = refs[-1]
        f = lambda j: refs[where[j][0]][where[j][1]].astype(F32)
        o_ref[...] = (own_ref[...].astype(F32) + f(0)) + (f(1) + f(2))

    own_map = {"rows": lambda i, me_ref: (me_ref[0] * nb + i, 0), "cols": lambda i, me_ref: (i, me_ref[0]),
               "all": lambda i, me_ref: (i, 0)}[how]
    return pl.pallas_call(
        body,
        name=name,
        grid_spec=pltpu.PrefetchScalarGridSpec(
            num_scalar_prefetch=1,
            grid=(nb,),
            in_specs=[pl.BlockSpec((tr, C), own_map)]
            + [pl.BlockSpec((p.shape[0], tr, C), lambda i, me_ref: (0, i, 0)) for p in parts],
            out_specs=pl.BlockSpec((tr, C), lambda i, me_ref: (i, 0)),
        ),
        out_shape=jax.ShapeDtypeStruct((R, C), F32),
        compiler_params=pltpu.CompilerParams(dimension_semantics=("arbitrary",), vmem_limit_bytes=VMEM_LIMIT),
    )(me, full, *parts)


_M_CORR = 1.0 - ADAM_B1**ADAM_STEP
_V_CORR = 1.0 - ADAM_B2**ADAM_STEP


def _adamw_math(w, g, m, v):
    m = ADAM_B1 * m + (1.0 - ADAM_B1) * g
    v = ADAM_B2 * v + (1.0 - ADAM_B2) * (g * g)
    delta = -ADAM_LR * ((m / _M_CORR) / (jnp.sqrt(v / _V_CORR) + ADAM_EPS) + ADAM_WD * w)
    return delta, m, v


def _adamw(name, w, m, v, g_here, g_there, g_transposed=False, comm=()):
    R, C = w.shape
    tr = _tile(R, 256, LANES if g_transposed else 8)

    def body(w_ref, m_ref, v_ref, ga_ref, gb_ref, g_ref, d_ref, nm_ref, nv_ref):
        g = ga_ref[...] + gb_ref[...]
        if g_transposed:
            g = g.T
        g_ref[...] = g
        d_ref[...], nm_ref[...], nv_ref[...] = _adamw_math(w_ref[...], g, m_ref[...], v_ref[...])

    spec = pl.BlockSpec((tr, C), lambda i: (i, 0))
    gspec = pl.BlockSpec((C, tr), lambda i: (0, i)) if g_transposed else spec
    return _call(name, body, (R // tr,), [spec] * 3 + [gspec] * 2, [spec] * 4, [jax.ShapeDtypeStruct((R, C), F32)] * 4,
                 (w, m, v, g_here, g_there), comm=comm)


class _PackLayout:
    def __init__(self, n_cc, n_grp, G, widths):
        self.dw_rows = (0, HALO)
        self.wp_rows = (HALO, HALO + G)
        self.n_cc, self.n_grp, self.G = n_cc, n_grp, G
        self.vec = {}
        r = HALO + G
        for name, width in widths:
            self.vec[name] = (r, width)
            r += width // PACK_W
        self.rows = -(-r // 8) * 8


def _pack_small(layout, dwdw, dwp, vecs):
    names = list(vecs)

    def body(*refs):
        dw_ref, wp_ref = refs[0], refs[1]
        vec_refs = refs[2 : 2 + len(names)]
        o_ref = refs[-1]
        o_ref[...] = jnp.zeros_like(o_ref)
        for j in range(layout.n_cc):
            o_ref[layout.dw_rows[0] : layout.dw_rows[1], j * LANES : (j + 1) * LANES] = dw_ref[j]
        for i in range(layout.n_grp):
            o_ref[layout.wp_rows[0] : layout.wp_rows[1], i * layout.G : (i + 1) * layout.G] = wp_ref[i]
        for name, ref in zip(names, vec_refs):
            r, width = layout.vec[name]
            for h in range(width // PACK_W):
                o_ref[r + h : r + h + 1, :] = ref[:, h * PACK_W : (h + 1) * PACK_W]

    return pl.pallas_call(
        body,
        name="pack_small",
        out_shape=jax.ShapeDtypeStruct((layout.rows, PACK_W), F32),
    )(dwdw, dwp, *[vecs[k] for k in names])


def _adamw_small(layout, g_here, g_there, w_dw, m_dw, v_dw, w_pool, m_pool, v_pool, vec_w, vec_m, vec_v):
    names = list(vec_w)
    nv = len(names)

    def body(*refs):
        ga_ref, gb_ref = refs[0], refs[1]
        wdw, mdw, vdw, wp, mp, vp = refs[2:8]
        vw, vm, vv = refs[8 : 8 + nv], refs[8 + nv : 8 + 2 * nv], refs[8 + 2 * nv : 8 + 3 * nv]
        outs = refs[8 + 3 * nv :]
        acc = outs[-1]
        acc[...] = ga_ref[...] + gb_ref[...]

        def emit(o, g, w, m, v, idx=()):
            res = (g,) + _adamw_math(w, g, m, v)
            for ref, val in zip(o, res):
                ref[idx] = val

        me = 2 * lax.axis_index("x") + lax.axis_index("y")
        for j in range(layout.n_cc):

            @pl.when(me == j)
            def _(j=j):
                g = acc[layout.dw_rows[0] : layout.dw_rows[1], j * LANES : (j + 1) * LANES]
                emit(outs[0:4], g, wdw[...], mdw[...], vdw[...], idx=...)

        for i in range(layout.n_grp):
            g = acc[layout.wp_rows[0] : layout.wp_rows[1], i * layout.G : (i + 1) * layout.G]
            emit(outs[4:8], g, wp[i], mp[i], vp[i], idx=i)
        for q, name in enumerate(names):
            r, width = layout.vec[name]
            for h in range(width // PACK_W):
                ls = slice(h * PACK_W, (h + 1) * PACK_W)
                g = acc[r + h : r + h + 1, :]
                emit(outs[8 + 4 * q : 12 + 4 * q], g, vw[q][:, ls], vm[q][:, ls], vv[q][:, ls], idx=(slice(None), ls))

    shapes = [w_dw.shape] * 4 + [w_pool.shape] * 4
    for name in names:
        shapes += [vec_w[name].shape] * 4
    return pl.pallas_call(
        body,
        name="adamw_small",
        out_shape=[jax.ShapeDtypeStruct(s, F32) for s in shapes],
        scratch_shapes=[pltpu.VMEM(g_here.shape, F32)],
    )(g_here, g_there, w_dw, m_dw, v_dw, w_pool, m_pool, v_pool,
      *[vec_w[k] for k in names], *[vec_m[k] for k in names], *[vec_v[k] for k in names])


def _allreduce_adamw_row(g_part, w, m, v, loss_part, comm=()):
    D = w.shape[1]
    n_pairs = N_DEV - 1

    def body(g_ref, w_ref, m_ref, v_ref, l_ref, go_ref, d_ref, nm_ref, nv_ref, lo_ref, land_g, land_l, sems):
        x, y, c = _place()
        copies = []
        for q, (src, land) in enumerate(((g_ref, land_g), (l_ref, land_l))):
            for r in range(1, N_DEV):
                fx, fy, fc = (r >> 2) & 1, (r >> 1) & 1, r & 1
                peer = (1 - x if fx else x, 1 - y if fy else y, 1 - c if fc else c)
                cp = _remote(src, land.at[r], sems, 2 * (q * n_pairs + r - 1), peer)
                cp.start()
                copies.append(cp)
        for cp in copies:
            cp.wait()

        def total(src, land):
            row = lambda r: src[...] if r == 0 else land[r]
            return ((row(0) + row(4)) + (row(2) + row(6))) + ((row(1) + row(5)) + (row(3) + row(7)))

        g = total(g_ref, land_g)
        go_ref[...] = g
        d_ref[...], nm_ref[...], nv_ref[...] = _adamw_math(w_ref[...], g, m_ref[...], v_ref[...])
        lo_ref[...] = total(l_ref, land_l)

    vm = pl.BlockSpec(memory_space=pltpu.VMEM)
    return _call(
        "allreduce_adamw_g_mix",
        body,
        (),
        [vm] * 5,
        [vm] * 5,
        [jax.ShapeDtypeStruct((1, D), F32)] * 4 + [jax.ShapeDtypeStruct(loss_part.shape, F32)],
        (g_part, w, m, v, loss_part),
        scratch=[pltpu.VMEM((N_DEV, 1, D), F32), pltpu.VMEM((N_DEV,) + loss_part.shape, F32),
                 pltpu.SemaphoreType.DMA((4 * n_pairs,))],
        comm=comm,
    )


def kernel(x, g_mix, w_in, b_in, w_dw, b_dw, ln_g, ln_b, w_pool, s_pool, w_out, g_ffn, w_gate, w_up, w_down, g_final, loss_target, m_g_mix, m_w_in, m_b_in, m_w_dw, m_b_dw, m_ln_g, m_ln_b, m_w_pool, m_s_pool, m_w_out, m_g_ffn, m_w_gate, m_w_up, m_w_down, m_g_final, v_g_mix, v_w_in, v_b_in, v_w_dw, v_b_dw, v_ln_g, v_ln_b, v_w_pool, v_s_pool, v_w_out, v_g_ffn, v_w_gate, v_w_up, v_w_down, v_g_final):
    x2 = x[0]
    target = loss_target[0]
    T, D = x2.shape
    w_in2, w_out2, w_down2, w_dw2 = w_in[0], w_out[0], w_down[0], w_dw[0]
    w_gateT, w_upT = w_gate[0].T, w_up[0].T
    CI = w_in2.shape[1] * N_CHIPS
    DM = w_out2.shape[0] * N_CHIPS
    F = w_down2.shape[0] * N_CHIPS
    KW, dw_cols = w_dw2.shape
    assert dw_cols == LANES
    n_grp, G = w_pool.shape[1], w_pool.shape[-1]
    w_pool3 = w_pool[0]
    g_final2 = g_final.reshape(1, D)

    me = (2 * lax.axis_index("x") + lax.axis_index("y")).astype(jnp.int32).reshape(1)

    w_inT_b, w_dw4, f_out, f_gate, f_up, f_down = _place_and_gather(
        [(w_in2, "rows", (CI, D), BF16, True, True), (w_dw2, "lead", (N_CHIPS, KW, dw_cols), F32, False, False)],
        [(w, "rows", shape, BF16, False, True)
         for w, shape in ((w_out2, (DM, D)), (w_gateT, (F, D)), (w_upT, (F, D)), (w_down2, (F, D)))])
    w_pool_b = w_pool3.astype(BF16)
    (z, xn_b), (f_out, f_gate) = _in_proj(
        x2, g_mix, w_inT_b, b_in,
        comm=[_GatherIci([f_out], ["rows"], [True]), _GatherIci([f_gate], ["rows"], [True], which=(2,))])
    (y_b, v), (w_out_b, f_gate, f_up) = _seq_fwd(
        z, w_dw4, b_dw, ln_g, ln_b, w_pool_b, s_pool,
        comm=[_GatherD2d([f_out], ["rows"]), _GatherIci([f_gate], ["rows"], [True], which=(0, 1)),
              _GatherIci([f_up], ["rows"], [True])])
    (h1, hn_b), (wgT_b, wuT_b, f_down) = _out_proj(
        y_b, x2, w_out_b, g_ffn,
        comm=[_GatherD2d([f_gate, f_up], ["rows"] * 2), _GatherIci([f_down], ["rows"], [True])])
    (g_b, u_b, a_b), (wd_b,) = _gate_up(hn_b, wgT_b, wuT_b, comm=[_GatherD2d([f_down], ["rows"])])
    (dh2, dh2_b, loss_part, d_g_final), _ = _down_loss(a_b, wd_b, h1, target, g_final2)

    gw_down, _ = _weight_grad("grad_w_down", a_b, dh2_b)
    (dg_b, du_b), (p_down_xy,) = _ffn_bwd_act(dh2_b, wd_b, g_b, u_b, comm=[_Scatter([gw_down], ["rows"], which=(0, 1))])
    gw_gateT, (p_down_d,) = _weight_grad("grad_w_gate", dg_b, hn_b, comm=[_Scatter([gw_down], ["rows"], which=(2,))])
    gw_upT, _ = _weight_grad("grad_w_up", du_b, hn_b)
    sum_down = _sum_parts("sum_w_down", gw_down, "rows", [p_down_xy, p_down_d], me)
    (dh1, dh1_b, dy, d_g_ffn), (p_gate, oth_down) = _ffn_bwd_in(
        dg_b, du_b, wgT_b, wuT_b, h1, dh2, g_ffn, w_out_b, comm=[_Scatter([gw_gateT], ["rows"]), _Swap([sum_down])])
    gw_out, _ = _weight_grad("grad_w_out", y_b, dh1_b)
    sum_gate = _sum_parts("sum_w_gate", gw_gateT, "rows", [p_gate], me)
    res = {}
    res["w_down"], _ = _adamw("adamw_w_down", w_down2, m_w_down[0], v_w_down[0], sum_down, oth_down)
    (dz_b, d_wdw, d_bdw, d_lng, d_lnb, d_wp, d_sp, d_bin), (p_up, p_out, oth_gate) = _seq_bwd(
        z, dy, v, w_dw4, ln_g, ln_b, w_pool_b, s_pool,
        comm=[_Scatter([gw_upT, gw_out], ["rows", "rows"]), _Swap([sum_gate])])
    vec_grads = {"b_dw": d_bdw, "ln_g": d_lng, "ln_b": d_lnb, "s_pool": d_sp, "g_ffn": d_g_ffn, "g_final": d_g_final, "b_in": d_bin}
    layout = _PackLayout(dw_cols * N_CHIPS // LANES, n_grp, G, [(k, a.shape[1]) for k, a in vec_grads.items()])
    pack = _pack_small(layout, d_wdw, d_wp, vec_grads)
    sum_up = _sum_parts("sum_w_up", gw_upT, "rows", [p_up], me)
    sum_out = _sum_parts("sum_w_out", gw_out, "rows", [p_out], me)
    gw_inT, (p_small, oth_up, oth_out) = _weight_grad(
        "grad_w_in", dz_b, xn_b, comm=[_Scatter([pack], ["all"]), _Swap([sum_up, sum_out])])
    sum_small = _sum_parts("sum_small", pack, "all", [p_small], me)
    res["w_gate"], _ = _adamw("adamw_w_gate", w_gateT, m_w_gate[0].T, v_w_gate[0].T, sum_gate, oth_gate)
    (grad_x, d_g_mix), (p_in, oth_small) = _in_proj_bwd(
        dz_b, w_inT_b, x2, dh1, g_mix, comm=[_Scatter([gw_inT], ["rows"]), _Swap([sum_small])])
    res["w_up"], _ = _adamw("adamw_w_up", w_upT, m_w_up[0].T, v_w_up[0].T, sum_up, oth_up)
    res["w_out"], _ = _adamw("adamw_w_out", w_out2, m_w_out[0], v_w_out[0], sum_out, oth_out)
    sum_in = _sum_parts("sum_w_in", gw_inT, "rows", [p_in], me)
    (*res["g_mix"], loss_row), (oth_in,) = _allreduce_adamw_row(
        d_g_mix, g_mix, m_g_mix, v_g_mix, loss_part, comm=[_Swap([sum_in])])
    loss = loss_row[0, 0]
    res["w_in"], _ = _adamw("adamw_w_in", w_in2, m_w_in[0], v_w_in[0], sum_in, oth_in, g_transposed=True)

    pad_dw = lambda a: jnp.pad(a[0], ((0, HALO - KW), (0, 0)))
    vec_w = {"b_dw": b_dw, "ln_g": ln_g, "ln_b": ln_b, "s_pool": s_pool, "g_ffn": g_ffn, "g_final": g_final2, "b_in": b_in}
    vec_m = {"b_dw": m_b_dw, "ln_g": m_ln_g, "ln_b": m_ln_b, "s_pool": m_s_pool, "g_ffn": m_g_ffn,
             "g_final": m_g_final.reshape(1, D), "b_in": m_b_in}
    vec_v = {"b_dw": v_b_dw, "ln_g": v_ln_g, "ln_b": v_ln_b, "s_pool": v_s_pool, "g_ffn": v_g_ffn,
             "g_final": v_g_final.reshape(1, D), "b_in": v_b_in}
    small = _adamw_small(layout, sum_small, oth_small, pad_dw(w_dw), pad_dw(m_w_dw), pad_dw(v_w_dw),
                         w_pool3, m_w_pool[0], v_w_pool[0], vec_w, vec_m, vec_v)
    res["w_dw"] = [a[:KW][None] for a in small[0:4]]
    res["w_pool"] = [a[None] for a in small[4:8]]
    for q, k in enumerate(vec_w):
        res[k] = list(small[8 + 4 * q : 12 + 4 * q])
    res["g_final"] = [a.reshape(D) for a in res["g_final"]]
    for k in ("w_in", "w_out", "w_down"):
        res[k] = [a[None] for a in res[k]]
    for k in ("w_gate", "w_up"):
        res[k] = [a.T[None] for a in res[k]]

    order = ["g_mix", "w_in", "b_in", "w_dw", "b_dw", "ln_g", "ln_b", "w_pool", "s_pool", "w_out", "g_ffn", "w_gate", "w_up", "w_down", "g_final"]
    outs = [loss, grad_x[None]]
    for q in range(4):
        outs += [res[k][q] for k in order]
    return tuple(outs)
```

```python
import jax
import jax.numpy as jnp
from jax import lax
from jax.experimental import pallas as pl
from jax.experimental.pallas import tpu as pltpu
from jax.experimental.pallas import tpu_sc as plsc

F32 = jnp.float32
BF16 = jnp.bfloat16
MESH = pl.DeviceIdType.MESH
ANY = pl.BlockSpec(memory_space=pl.ANY)

RMS_EPS = 1e-6
LN_EPS = 1e-5
POOL_WINDOWS = (2, 4, 8, 16)
ADAM_LR = 0.001
ADAM_B1 = 0.9
ADAM_B2 = 0.999
ADAM_EPS = 1e-08
ADAM_WD = 0.01
ADAM_STEP = 10

LANES = 128
SUBLANES = 8
HALO = 32
CONV_ROWS = 64
HIDDEN_CHUNK = 512
VMEM_LIMIT = 56 * 1024 * 1024
PACK_W = 512
N_CHIPS = 4
N_DEV = 8
SC_CORES = 2
SC_TILES = 32
SC_LANES = 16


def _tile(n, want, mult=8):
    t = min(n, want)
    while n % t or t % mult:
        t -= 1
    return t


def _sigmoid(x):
    return 1.0 / (1.0 + jnp.exp(-x))


def _dot(a, b, dims):
    return lax.dot_general(a, b, (dims, ((), ())), preferred_element_type=F32)


NN = ((1,), (0,))
NT = ((1,), (1,))
TN = ((0,), (0,))


def _rms_bwd(x, g, dy):
    r = lax.rsqrt(jnp.mean(x * x, axis=-1, keepdims=True) + RMS_EPS)
    xh = x * r
    gy = dy * g
    dx = r * (gy - xh * jnp.mean(gy * xh, axis=-1, keepdims=True))
    return dx, dy * xh


def _accumulate(ref, first, val):
    @pl.when(first)
    def _():
        ref[...] = val

    @pl.when(jnp.logical_not(first))
    def _():
        ref[...] += val


def _place():
    return lax.axis_index("x"), lax.axis_index("y"), lax.axis_index("c")


def _other_chips(x, y):
    return [(1 - x, y), (x, 1 - y), (1 - x, 1 - y)]


def _rows(ref, start, n):
    return ref.at[pl.ds(pl.multiple_of(start, 16), n)]


def _window(ref, how, k, c=None):
    if how == "all":
        return ref
    if how == "lead":
        return ref.at[k]
    if how == "rows":
        n = ref.shape[0] // N_CHIPS
        if c is None:
            return _rows(ref, k * n, n)
        return _rows(ref, k * n + c * (n // 2), n // 2)
    n = ref.shape[1] // N_CHIPS
    cols = pl.ds(pl.multiple_of(k * n, LANES), n)
    if c is None:
        return ref.at[:, cols]
    h = ref.shape[0] // 2
    return ref.at[pl.ds(pl.multiple_of(c * h, 16), h), cols]


def _remote(src, dst, sems, s, device):
    return pltpu.make_async_remote_copy(
        src_ref=src, dst_ref=dst, send_sem=sems.at[s], recv_sem=sems.at[s + 1], device_id=device, device_id_type=MESH)


class _GatherIci:
    aliased = True

    def __init__(self, fulls, hows, splits, which=(0, 1, 2)):
        self.fulls, self.hows, self.splits, self.which = list(fulls), list(hows), list(splits), tuple(which)

    def inputs(self):
        return self.fulls

    def out_shapes(self):
        return [jax.ShapeDtypeStruct(a.shape, a.dtype) for a in self.fulls]

    def n_sems(self):
        return 6 * len(self.fulls)

    def build(self, ins, outs, sems, base):
        x, y, c = _place()
        me = 2 * x + y
        chips = _other_chips(x, y)
        starts, waits = [], []
        for a, (how, sp) in enumerate(zip(self.hows, self.splits)):
            half = c if sp else None
            mine = _window(outs[a], how, me, half)
            for j in self.which:
                px, py = chips[j]
                s = base + 6 * a + 2 * j
                cp = _remote(mine, mine, sems, s, (px, py, c))
                landing = _remote(mine, _window(outs[a], how, 2 * px + py, half), sems, s, (px, py, c))
                starts.append(cp.start)
                waits += [landing.wait_recv, cp.wait_send]
        return starts, waits


class _GatherD2d:
    aliased = True

    def __init__(self, fulls, hows):
        self.fulls, self.hows = list(fulls), list(hows)

    def inputs(self):
        return self.fulls

    def out_shapes(self):
        return [jax.ShapeDtypeStruct(a.shape, a.dtype) for a in self.fulls]

    def n_sems(self):
        return 6 * len(self.fulls)

    def build(self, ins, outs, sems, base):
        x, y, c = _place()
        starts, waits = [], []
        for a, how in enumerate(self.hows):
            for j, (px, py) in enumerate(_other_chips(x, y)):
                s = base + 6 * a + 2 * j
                got = _window(outs[a], how, 2 * px + py, c)
                cp = _remote(got, got, sems, s, (x, y, 1 - c))
                landing = _remote(got, _window(outs[a], how, 2 * px + py, 1 - c), sems, s, (x, y, 1 - c))
                starts.append(cp.start)
                waits += [landing.wait_recv, cp.wait_send]
        return starts, waits


def _part_shape(a, how):
    if how == "all":
        return a.shape
    if how == "rows":
        return (a.shape[0] // N_CHIPS, a.shape[1])
    return (a.shape[0], a.shape[1] // N_CHIPS)


class _Scatter:
    aliased = False

    def __init__(self, fulls, hows, which=(0, 1, 2)):
        self.fulls, self.hows, self.which = list(fulls), list(hows), tuple(which)

    def inputs(self):
        return self.fulls

    def out_shapes(self):
        return [jax.ShapeDtypeStruct((len(self.which),) + _part_shape(a, h), a.dtype) for a, h in zip(self.fulls, self.hows)]

    def n_sems(self):
        return 6 * len(self.fulls)

    def build(self, ins, outs, sems, base):
        x, y, c = _place()
        chips = _other_chips(x, y)
        starts, waits = [], []
        for a, how in enumerate(self.hows):
            for slot, j in enumerate(self.which):
                px, py = chips[j]
                cp = _remote(_window(ins[a], how, 2 * px + py), outs[a].at[slot], sems, base + 6 * a + 2 * j, (px, py, c))
                starts.append(cp.start)
                waits += [cp.wait_recv, cp.wait_send]
        return starts, waits


class _Swap:
    aliased = False

    def __init__(self, arrays):
        self.arrays = list(arrays)

    def inputs(self):
        return self.arrays

    def out_shapes(self):
        return [jax.ShapeDtypeStruct(a.shape, a.dtype) for a in self.arrays]

    def n_sems(self):
        return 2 * len(self.arrays)

    def build(self, ins, outs, sems, base):
        x, y, c = _place()
        starts, waits = [], []
        for a in range(len(ins)):
            cp = _remote(ins[a], outs[a], sems, base + 2 * a, (x, y, 1 - c))
            starts.append(cp.start)
            waits += [cp.wait_recv, cp.wait_send]
        return starts, waits


def _call(name, body, grid, in_specs, out_specs, out_shape, args, scratch=(), comm=()):
    comm = list(comm)
    n_in, n_out, n_scr = len(args), len(out_shape), len(scratch)
    c_in = [a for op in comm for a in op.inputs()]
    c_out = [s for op in comm for s in op.out_shapes()]
    n_sems = sum(op.n_sems() for op in comm)
    aliases, i_in, i_out = {}, 0, 0
    for op in comm:
        if op.aliased:
            for q in range(len(op.inputs())):
                aliases[n_in + i_in + q] = n_out + i_out + q
        i_in, i_out = i_in + len(op.inputs()), i_out + len(op.out_shapes())

    def wrapped(*refs):
        ins = refs[:n_in]
        cin = refs[n_in : n_in + len(c_in)]
        o0 = n_in + len(c_in)
        outs = refs[o0 : o0 + n_out]
        cout = refs[o0 + n_out : o0 + n_out + len(c_out)]
        s0 = o0 + n_out + len(c_out)
        scr = refs[s0 : s0 + n_scr]

        def copies():
            sems = refs[s0 + n_scr]
            starts, waits = [], []
            i_in = i_out = base = 0
            for op in comm:
                ni, no = len(op.inputs()), len(op.out_shapes())
                s, w = op.build(cin[i_in : i_in + ni], cout[i_out : i_out + no], sems, base)
                starts += s
                waits += w
                i_in, i_out, base = i_in + ni, i_out + no, base + op.n_sems()
            return starts, waits

        def run_starts():
            for start in copies()[0]:
                start()

        def run_waits():
            for wait in copies()[1]:
                wait()

        if comm and grid:
            first = last = True
            for d, n in enumerate(grid):
                first = jnp.logical_and(first, pl.program_id(d) == 0)
                last = jnp.logical_and(last, pl.program_id(d) == n - 1)
            pl.when(first)(run_starts)
        elif comm:
            run_starts()
        if body is not None:
            body(*ins, *outs, *scr)
        if comm and grid:
            pl.when(last)(run_waits)
        elif comm:
            run_waits()

    res = pl.pallas_call(
        wrapped,
        name=name,
        grid=grid,
        in_specs=list(in_specs) + [ANY] * len(c_in),
        out_specs=list(out_specs) + [ANY] * len(c_out),
        out_shape=list(out_shape) + c_out,
        scratch_shapes=list(scratch) + ([pltpu.SemaphoreType.DMA((n_sems,))] if comm else []),
        input_output_aliases=aliases,
        compiler_params=pltpu.CompilerParams(dimension_semantics=("arbitrary",) * len(grid), vmem_limit_bytes=VMEM_LIMIT),
    )(*args, *c_in)
    return tuple(res[:n_out]), tuple(res[n_out:])


def _place_and_gather(now, later):
    items = list(now) + list(later)
    n, n_now = len(items), len(now)
    buf_shape = lambda it: it[0].shape[::-1] if it[4] else it[0].shape
    split_now = [a for a in range(n_now) if items[a][5]]

    def body(*refs):
        ins, outs = refs[:n], refs[n : 2 * n]
        stage, bufs = refs[2 * n : 3 * n - n_now], refs[3 * n - n_now : 4 * n - n_now]
        sems = refs[4 * n - n_now]
        x, y, c = _place()
        me = 2 * x + y
        chips = _other_chips(x, y)
        loads = [pltpu.make_async_copy(ins[a], stage[a - n_now], sems.at[a]) for a in range(n_now, n)]
        for ld in loads:
            ld.start()
        pending = []

        def place(a, val):
            _, how, _, dtype, transposed, _ = items[a]
            bufs[a][...] = (val.T if transposed else val).astype(dtype)
            cp = pltpu.make_async_copy(bufs[a], _window(outs[a], how, me), sems.at[n + a])
            cp.start()
            pending.append(cp.wait)

        arrivals = []
        for a in range(n_now):
            place(a, ins[a][...])
            how, split = items[a][1], items[a][5]
            half = c if split else None
            src = _rows(bufs[a], c * (bufs[a].shape[0] // 2), bufs[a].shape[0] // 2) if split else bufs[a]
            for j, (px, py) in enumerate(chips):
                s = 2 * n + 6 * a + 2 * j
                cp = _remote(src, _window(outs[a], how, me, half), sems, s, (px, py, c))
                landing = _remote(src, _window(outs[a], how, 2 * px + py, half), sems, s, (px, py, c))
                cp.start()
                arrivals.append(landing.wait_recv)
                pending.append(cp.wait_send)
        for a in range(n_now, n):
            loads[a - n_now].wait()
            place(a, stage[a - n_now][...])
        for wait in arrivals:
            wait()
        d2d = _GatherD2d([None] * len(split_now), [items[a][1] for a in split_now])
        starts, waits = d2d.build(None, [outs[a] for a in split_now], sems, 2 * n + 6 * n_now)
        for start in starts:
            start()
        for wait in waits + pending:
            wait()

    vm = pl.BlockSpec(memory_space=pltpu.VMEM)
    return pl.pallas_call(
        body,
        name="place_and_gather",
        in_specs=[vm] * n_now + [ANY] * (n - n_now),
        out_specs=[ANY] * n,
        out_shape=[jax.ShapeDtypeStruct(it[2], it[3]) for it in items],
        scratch_shapes=[pltpu.VMEM(it[0].shape, it[0].dtype) for it in later]
        + [pltpu.VMEM(buf_shape(it), it[3]) for it in items]
        + [pltpu.SemaphoreType.DMA((2 * n + 6 * n_now + 6 * len(split_now),))],
        compiler_params=pltpu.CompilerParams(vmem_limit_bytes=VMEM_LIMIT),
    )(*[it[0] for it in items])


def _in_proj(x, g_mix, w_inT_b, b_in, comm=()):
    T, D = x.shape
    CI = w_inT_b.shape[0]
    tm = _tile(T, 512)

    def body(x_ref, g_ref, w_ref, b_ref, z_ref, xn_ref):
        xv = x_ref[...]
        r = lax.rsqrt(jnp.mean(xv * xv, axis=-1, keepdims=True) + RMS_EPS)
        xn = (xv * r * g_ref[...]).astype(BF16)
        xn_ref[...] = xn
        z_ref[...] = _dot(xn, w_ref[...], NT) + b_ref[...]

    return _call(
        "in_proj",
        body,
        (T // tm,),
        [
            pl.BlockSpec((tm, D), lambda i: (i, 0)),
            pl.BlockSpec((1, D), lambda i: (0, 0)),
            pl.BlockSpec((CI, D), lambda i: (0, 0)),
            pl.BlockSpec((1, CI), lambda i: (0, 0)),
        ],
        [pl.BlockSpec((tm, CI), lambda i: (i, 0)), pl.BlockSpec((tm, D), lambda i: (i, 0))],
        [jax.ShapeDtypeStruct((T, CI), F32), jax.ShapeDtypeStruct((T, D), BF16)],
        (x, g_mix, w_inT_b, b_in),
        comm=comm,
    )


def _fill_shifted(scr):
    n = scr.shape[1] - SUBLANES
    for s in range(1, SUBLANES):
        scr[s, 0:n, :] = scr[0, s : s + n, :]


def _shifted_rows(scr, off, n, cs):
    s = off % SUBLANES
    return scr[s, off - s : off - s + n, cs]


def _pool_mean_minus_token(p_scr, cs, w, cnt, tt):
    tok = p_scr[HALO : HALO + tt, cs]
    s = tok
    for d in range(1, w):
        s = s + p_scr[HALO - d : HALO - d + tt, cs]
    return s / cnt - tok


def _seq_fwd(z, w_dw4, b_dw, ln_g, ln_b, w_pool_b, s_pool, comm=()):
    T, CI = z.shape
    CC = ln_g.shape[1]
    n_grp, G = w_pool_b.shape[0], w_pool_b.shape[-1]
    KW = w_dw4.shape[1]
    D = CC + n_grp * G
    tt = _tile(T, 512, HALO)
    per = tt // HALO

    def body(zc_ref, zp_ref, wdw_ref, bdw_ref, lng_ref, lnb_ref, wp_ref, sp_ref, y_ref, v_ref, u_scr, p_scr):
        i = pl.program_id(0)
        first = i == 0
        u_prev = zp_ref[:, 0:CC] * _sigmoid(zp_ref[:, CC : 2 * CC])
        u_scr[0, 0:HALO, :] = jnp.where(first, 0.0, u_prev)
        p_scr[0:HALO, :] = jnp.where(first, 0.0, zp_ref[:, 2 * CC :])
        u_scr[0, HALO:, :] = zc_ref[:, 0:CC] * _sigmoid(zc_ref[:, CC : 2 * CC])
        p_scr[HALO:, :] = zc_ref[:, 2 * CC :]
        _fill_shifted(u_scr)

        for j in range(CC // LANES):
            cs = slice(LANES * j, LANES * (j + 1))
            for rb in range(tt // CONV_ROWS):
                acc = jnp.zeros((CONV_ROWS, LANES), F32)
                for k in range(KW):
                    off = HALO - (KW - 1) + k + rb * CONV_ROWS
                    acc = acc + _shifted_rows(u_scr, off, CONV_ROWS, cs) * wdw_ref[j, k : k + 1, :]
                v_ref[rb * CONV_ROWS : (rb + 1) * CONV_ROWS, cs] = acc + bdw_ref[:, cs]

        v = v_ref[...]
        mu = jnp.mean(v, axis=-1, keepdims=True)
        d = v - mu
        var = jnp.mean(d * d, axis=-1, keepdims=True)
        ln = d * lax.rsqrt(var + LN_EPS) * lng_ref[...] + lnb_ref[...]
        y_ref[:, 0:CC] = (ln * _sigmoid(ln)).astype(BF16)

        tpos = i * tt + lax.broadcasted_iota(jnp.int32, (tt, 1), 0)
        for gi, w in enumerate(POOL_WINDOWS):
            cs = slice(G * gi, G * (gi + 1))
            cnt = jnp.minimum(tpos + 1, w).astype(F32)
            yi = _pool_mean_minus_token(p_scr, cs, w, cnt, tt)
            q = _dot(yi.astype(BF16), wp_ref[gi], NN)
            y_ref[:, CC + G * gi : CC + G * (gi + 1)] = (q * sp_ref[:, cs]).astype(BF16)

    const2 = lambda i: (0, 0)
    return _call(
        "seq_fwd",
        body,
        (T // tt,),
        [
            pl.BlockSpec((tt, CI), lambda i: (i, 0)),
            pl.BlockSpec((HALO, CI), lambda i: (jnp.maximum(i * per - 1, 0), 0)),
            pl.BlockSpec(w_dw4.shape, lambda i: (0, 0, 0)),
            pl.BlockSpec((1, CC), const2),
            pl.BlockSpec((1, CC), const2),
            pl.BlockSpec((1, CC), const2),
            pl.BlockSpec(w_pool_b.shape, lambda i: (0, 0, 0)),
            pl.BlockSpec((1, n_grp * G), const2),
        ],
        [pl.BlockSpec((tt, D), lambda i: (i, 0)), pl.BlockSpec((tt, CC), lambda i: (i, 0))],
        [jax.ShapeDtypeStruct((T, D), BF16), jax.ShapeDtypeStruct((T, CC), F32)],
        (z, z, w_dw4, b_dw, ln_g, ln_b, w_pool_b, s_pool),
        scratch=[pltpu.VMEM((SUBLANES, HALO + tt, CC), F32), pltpu.VMEM((HALO + tt, n_grp * G), F32)],
        comm=comm,
    )


def _out_proj(y_b, x, w_out_b, g_ffn, comm=()):
    T, D = x.shape
    tm = _tile(T, 512)

    def body(y_ref, x_ref, w_ref, g_ref, h1_ref, hn_ref):
        h1 = x_ref[...] + _dot(y_ref[...], w_ref[...], NN)
        h1_ref[...] = h1
        r = lax.rsqrt(jnp.mean(h1 * h1, axis=-1, keepdims=True) + RMS_EPS)
        hn_ref[...] = (h1 * r * g_ref[...]).astype(BF16)

    row = lambda i: (i, 0)
    return _call(
        "out_proj",
        body,
        (T // tm,),
        [
            pl.BlockSpec((tm, y_b.shape[1]), row),
            pl.BlockSpec((tm, D), row),
            pl.BlockSpec(w_out_b.shape, lambda i: (0, 0)),
            pl.BlockSpec((1, D), lambda i: (0, 0)),
        ],
        [pl.BlockSpec((tm, D), row), pl.BlockSpec((tm, D), row)],
        [jax.ShapeDtypeStruct((T, D), F32), jax.ShapeDtypeStruct((T, D), BF16)],
        (y_b, x, w_out_b, g_ffn),
        comm=comm,
    )


def _hidden_tile(F):
    return _tile(F, 1408, LANES)


def _gate_up(hn_b, wgT_b, wuT_b, comm=()):
    T, D = hn_b.shape
    F = wgT_b.shape[0]
    tm, tf = _tile(T, 1024), _hidden_tile(F)

    def body(hn_ref, wg_ref, wu_ref, g_ref, u_ref, a_ref):
        hn = hn_ref[...]
        for c0 in range(0, tf, HIDDEN_CHUNK):
            cs = slice(c0, min(c0 + HIDDEN_CHUNK, tf))
            gv = _dot(hn, wg_ref[cs, :], NT)
            uv = _dot(hn, wu_ref[cs, :], NT)
            g_ref[:, cs] = gv.astype(BF16)
            u_ref[:, cs] = uv.astype(BF16)
            a_ref[:, cs] = (gv * _sigmoid(gv) * uv).astype(BF16)

    wspec = pl.BlockSpec((tf, D), lambda j, i: (j, 0))
    ospec = pl.BlockSpec((tm, tf), lambda j, i: (i, j))
    return _call(
        "gate_up",
        body,
        (F // tf, T // tm),
        [pl.BlockSpec((tm, D), lambda j, i: (i, 0)), wspec, wspec],
        [ospec, ospec, ospec],
        [jax.ShapeDtypeStruct((T, F), BF16)] * 3,
        (hn_b, wgT_b, wuT_b),
        comm=comm,
    )


def _down_loss(a_b, wd_b, h1, target, g_final, comm=()):
    T, D = h1.shape
    F = a_b.shape[1]
    tm = _tile(T, 512)
    nt = T // tm

    def body(a_ref, w_ref, h1_ref, t_ref, g_ref, dh2_ref, dh2b_ref, loss_ref, dg_ref):
        i = pl.program_id(0)
        h2 = h1_ref[...] + _dot(a_ref[...], w_ref[...], NN)
        r = lax.rsqrt(jnp.mean(h2 * h2, axis=-1, keepdims=True) + RMS_EPS)
        g = g_ref[...]
        diff = h2 * r * g - t_ref[...]
        _accumulate(loss_ref, i == 0, jnp.full(loss_ref.shape, jnp.sum(diff * diff) * (0.5 / D), F32))
        dh2, dg_rows = _rms_bwd(h2, g, diff * (1.0 / D))
        dh2_ref[...] = dh2
        dh2b_ref[...] = dh2.astype(BF16)
        _accumulate(dg_ref, i == 0, jnp.sum(dg_rows, axis=0, keepdims=True))

    row = lambda i: (i, 0)
    return _call(
        "down_loss",
        body,
        (nt,),
        [
            pl.BlockSpec((tm, F), row),
            pl.BlockSpec((F, D), lambda i: (0, 0), pipeline_mode=pl.Buffered(1)),
            pl.BlockSpec((tm, D), row),
            pl.BlockSpec((tm, D), row),
            pl.BlockSpec((1, D), lambda i: (0, 0)),
        ],
        [
            pl.BlockSpec((tm, D), row),
            pl.BlockSpec((tm, D), row),
            pl.BlockSpec((1, LANES), lambda i: (0, 0)),
            pl.BlockSpec((1, D), lambda i: (0, 0)),
        ],
        [
            jax.ShapeDtypeStruct((T, D), F32),
            jax.ShapeDtypeStruct((T, D), BF16),
            jax.ShapeDtypeStruct((1, LANES), F32),
            jax.ShapeDtypeStruct((1, D), F32),
        ],
        (a_b, wd_b, h1, target, g_final),
        comm=comm,
    )


def _ffn_bwd_act(dh2_b, wd_b, g_b, u_b, comm=()):
    T, D = dh2_b.shape
    F = wd_b.shape[0]
    tm, tf = _tile(T, 1024), _hidden_tile(F)

    def body(d_ref, w_ref, g_ref, u_ref, dg_ref, du_ref):
        d = d_ref[...]
        for c0 in range(0, tf, HIDDEN_CHUNK):
            cs = slice(c0, min(c0 + HIDDEN_CHUNK, tf))
            da = _dot(d, w_ref[cs, :], NT)
            gv = g_ref[:, cs].astype(F32)
            uv = u_ref[:, cs].astype(F32)
            sg = _sigmoid(gv)
            silu = gv * sg
            dg_ref[:, cs] = (da * uv * (sg * (1.0 + gv * (1.0 - sg)))).astype(BF16)
            du_ref[:, cs] = (da * silu).astype(BF16)

    aspec = pl.BlockSpec((tm, tf), lambda j, i: (i, j))
    return _call(
        "ffn_bwd_act",
        body,
        (F // tf, T // tm),
        [pl.BlockSpec((tm, D), lambda j, i: (i, 0)), pl.BlockSpec((tf, D), lambda j, i: (j, 0)), aspec, aspec],
        [aspec, aspec],
        [jax.ShapeDtypeStruct((T, F), BF16)] * 2,
        (dh2_b, wd_b, g_b, u_b),
        comm=comm,
    )


def _ffn_bwd_in(dg_b, du_b, wgT_b, wuT_b, h1, dh2, g_ffn, w_out_b, comm=()):
    T, D = h1.shape
    F = wgT_b.shape[0]
    DM = w_out_b.shape[0]
    tm = _tile(T, 512)

    def body(dg_ref, du_ref, wg_ref, wu_ref, h1_ref, dh2_ref, g_ref, wo_ref, dh1_ref, dh1b_ref, dy_ref, dgf_ref):
        i = pl.program_id(0)
        dhn = _dot(dg_ref[...], wg_ref[...], NN) + _dot(du_ref[...], wu_ref[...], NN)
        dx, dg_rows = _rms_bwd(h1_ref[...], g_ref[...], dhn)
        dh1 = dh2_ref[...] + dx
        dh1b = dh1.astype(BF16)
        dh1_ref[...] = dh1
        dh1b_ref[...] = dh1b
        dy_ref[...] = _dot(dh1b, wo_ref[...], NT)
        _accumulate(dgf_ref, i == 0, jnp.sum(dg_rows, axis=0, keepdims=True))

    row = lambda i: (i, 0)
    const = lambda i: (0, 0)
    return _call(
        "ffn_bwd_in",
        body,
        (T // tm,),
        [
            pl.BlockSpec((tm, F), row),
            pl.BlockSpec((tm, F), row),
            pl.BlockSpec((F, D), const, pipeline_mode=pl.Buffered(1)),
            pl.BlockSpec((F, D), const, pipeline_mode=pl.Buffered(1)),
            pl.BlockSpec((tm, D), row),
            pl.BlockSpec((tm, D), row),
            pl.BlockSpec((1, D), const),
            pl.BlockSpec((DM, D), const, pipeline_mode=pl.Buffered(1)),
        ],
        [pl.BlockSpec((tm, D), row), pl.BlockSpec((tm, D), row), pl.BlockSpec((tm, DM), row), pl.BlockSpec((1, D), const)],
        [
            jax.ShapeDtypeStruct((T, D), F32),
            jax.ShapeDtypeStruct((T, D), BF16),
            jax.ShapeDtypeStruct((T, DM), F32),
            jax.ShapeDtypeStruct((1, D), F32),
        ],
        (dg_b, du_b, wgT_b, wuT_b, h1, dh2, g_ffn, w_out_b),
        comm=comm,
    )


def _seq_bwd(z, dy, v, w_dw4, ln_g, ln_b, w_pool_b, s_pool, comm=()):
    T, CI = z.shape
    CC = ln_g.shape[1]
    n_grp, G = w_pool_b.shape[0], w_pool_b.shape[-1]
    CP = n_grp * G
    KW = w_dw4.shape[1]
    n_cc = CC // LANES
    D = CC + CP
    tt = _tile(T, 512, HALO)
    per = tt // HALO
    n_tiles = T // tt
    last_halo = T // HALO - 1

    def body(zc_ref, zp_ref, dyc_ref, dyn_ref, vc_ref, vn_ref, wdw_ref, lng_ref, lnb_ref, wp_ref, sp_ref,
             dz_ref, dwdw_ref, dbdw_ref, dlng_ref, dlnb_ref, dwp_ref, dsp_ref, dbin_ref,
             dv_scr, u_scr, p_scr, g_scr, dw_scr):
        i = pl.program_id(0)
        first = i == 0
        last = i == n_tiles - 1
        lng, lnb = lng_ref[...], lnb_ref[...]

        def conv_pre(vv, dyc):
            mu = jnp.mean(vv, axis=-1, keepdims=True)
            d = vv - mu
            rs = lax.rsqrt(jnp.mean(d * d, axis=-1, keepdims=True) + LN_EPS)
            xh = d * rs
            ln = xh * lng + lnb
            sg = _sigmoid(ln)
            dln = dyc * (sg * (1.0 + ln * (1.0 - sg)))
            dxh = dln * lng
            dv = rs * (dxh - jnp.mean(dxh, axis=-1, keepdims=True) - xh * jnp.mean(dxh * xh, axis=-1, keepdims=True))
            return dv, dln, xh

        dv_c, dln_c, xh_c = conv_pre(vc_ref[...], dyc_ref[:, 0:CC])
        dv_scr[0, 0:tt, :] = dv_c
        dv_n, _, _ = conv_pre(vn_ref[...], dyn_ref[:, 0:CC])
        dv_scr[0, tt:, :] = jnp.where(last, 0.0, dv_n)
        _fill_shifted(dv_scr)
        _accumulate(dlng_ref, first, jnp.sum(dln_c * xh_c, axis=0, keepdims=True))
        _accumulate(dlnb_ref, first, jnp.sum(dln_c, axis=0, keepdims=True))
        _accumulate(dbdw_ref, first, jnp.sum(dv_c, axis=0, keepdims=True))

        u_scr[...] = zc_ref[:, 0:CC] * _sigmoid(zc_ref[:, CC : 2 * CC])

        @pl.when(first)
        def _():
            dw_scr[...] = jnp.zeros_like(dw_scr)

        for j in range(n_cc):
            cs = slice(LANES * j, LANES * (j + 1))
            gs = slice(CC + LANES * j, CC + LANES * (j + 1))
            dbin_a = jnp.zeros((1, LANES), F32)
            dbin_g = jnp.zeros((1, LANES), F32)
            for rb in range(tt // CONV_ROWS):
                rows = slice(rb * CONV_ROWS, (rb + 1) * CONV_ROWS)
                u_blk = u_scr[rows, cs]
                du = jnp.zeros((CONV_ROWS, LANES), F32)
                for k in range(KW):
                    off = rb * CONV_ROWS + (KW - 1) - k
                    d = _shifted_rows(dv_scr, off, CONV_ROWS, cs)
                    du = du + d * wdw_ref[j, k : k + 1, :]
                    dw_scr[j * HALO + k] += jnp.sum((u_blk * d).reshape(CONV_ROWS // 8, 8, LANES), axis=0)
                a = zc_ref[rows, cs]
                sg = _sigmoid(zc_ref[rows, gs])
                da = du * sg
                dgate = du * a * sg * (1.0 - sg)
                dz_ref[rows, cs] = da.astype(BF16)
                dz_ref[rows, gs] = dgate.astype(BF16)
                dbin_a = dbin_a + jnp.sum(da, axis=0, keepdims=True)
                dbin_g = dbin_g + jnp.sum(dgate, axis=0, keepdims=True)
            _accumulate(dbin_ref.at[:, cs], first, dbin_a)
            _accumulate(dbin_ref.at[:, gs], first, dbin_g)

        @pl.when(last)
        def _():
            dwdw_ref[...] = jnp.sum(dw_scr[...], axis=1).reshape(dwdw_ref.shape)

        p_scr[0:HALO, :] = jnp.where(first, 0.0, zp_ref[:, 2 * CC :])
        p_scr[HALO:, :] = zc_ref[:, 2 * CC :]
        tpos = i * tt + lax.broadcasted_iota(jnp.int32, (tt, 1), 0)
        for gi, w in enumerate(POOL_WINDOWS):
            cs = slice(G * gi, G * (gi + 1))
            ys = slice(CC + G * gi, CC + G * (gi + 1))
            ps = slice(2 * CC + G * gi, 2 * CC + G * (gi + 1))
            cnt = jnp.minimum(tpos + 1, w).astype(F32)
            yib = _pool_mean_minus_token(p_scr, cs, w, cnt, tt).astype(BF16)
            wp = wp_ref[gi]
            sp = sp_ref[:, cs]
            dyp = dyc_ref[:, ys]
            q = _dot(yib, wp, NN)
            _accumulate(dsp_ref.at[:, cs], first, jnp.sum(dyp * q, axis=0, keepdims=True))
            dq_c = (dyp * sp).astype(BF16)
            dq_n = (jnp.where(last, 0.0, dyn_ref[:, ys]) * sp).astype(BF16)
            _accumulate(dwp_ref.at[gi], first, _dot(yib, dq_c, TN))
            dyi_c = _dot(dq_c, wp, NT)
            g_scr[0:tt, cs] = dyi_c / cnt
            g_scr[tt:, cs] = _dot(dq_n, wp, NT) * (1.0 / w)
            dp = -dyi_c
            for d in range(w):
                dp = dp + g_scr[d : d + tt, cs]
            dz_ref[:, ps] = dp.astype(BF16)
            _accumulate(dbin_ref.at[:, ps], first, jnp.sum(dp, axis=0, keepdims=True))

    cur = lambda i: (i, 0)
    prev = lambda i: (jnp.maximum(i * per - 1, 0), 0)
    nxt = lambda i: (jnp.minimum((i + 1) * per, last_halo), 0)
    c2 = lambda i: (0, 0)
    c3 = lambda i: (0, 0, 0)
    return _call(
        "seq_bwd",
        body,
        (n_tiles,),
        [
            pl.BlockSpec((tt, CI), cur),
            pl.BlockSpec((HALO, CI), prev),
            pl.BlockSpec((tt, D), cur),
            pl.BlockSpec((HALO, D), nxt),
            pl.BlockSpec((tt, CC), cur),
            pl.BlockSpec((HALO, CC), nxt),
            pl.BlockSpec(w_dw4.shape, c3),
            pl.BlockSpec((1, CC), c2),
            pl.BlockSpec((1, CC), c2),
            pl.BlockSpec(w_pool_b.shape, c3),
            pl.BlockSpec((1, CP), c2),
        ],
        [
            pl.BlockSpec((tt, CI), cur),
            pl.BlockSpec((n_cc, HALO, LANES), c3),
            pl.BlockSpec((1, CC), c2),
            pl.BlockSpec((1, CC), c2),
            pl.BlockSpec((1, CC), c2),
            pl.BlockSpec((n_grp, G, G), c3),
            pl.BlockSpec((1, CP), c2),
            pl.BlockSpec((1, CI), c2),
        ],
        [
            jax.ShapeDtypeStruct((T, CI), BF16),
            jax.ShapeDtypeStruct((n_cc, HALO, LANES), F32),
            jax.ShapeDtypeStruct((1, CC), F32),
            jax.ShapeDtypeStruct((1, CC), F32),
            jax.ShapeDtypeStruct((1, CC), F32),
            jax.ShapeDtypeStruct((n_grp, G, G), F32),
            jax.ShapeDtypeStruct((1, CP), F32),
            jax.ShapeDtypeStruct((1, CI), F32),
        ],
        (z, z, dy, dy, v, v, w_dw4, ln_g, ln_b, w_pool_b, s_pool),
        scratch=[
            pltpu.VMEM((SUBLANES, tt + HALO, CC), F32),
            pltpu.VMEM((tt, CC), F32),
            pltpu.VMEM((HALO + tt, CP), F32),
            pltpu.VMEM((tt + HALO, CP), F32),
            pltpu.VMEM((n_cc * HALO, 8, LANES), F32),
        ],
        comm=comm,
    )


def _in_proj_bwd(dz_b, w_inT_b, x, dh1, g_mix, comm=()):
    T, D = x.shape
    CI = w_inT_b.shape[0]
    tm = _tile(T, 512)

    def body(dz_ref, w_ref, x_ref, dh1_ref, g_ref, dx_ref, dg_ref):
        i = pl.program_id(0)
        dxn = _dot(dz_ref[...], w_ref[...], NN)
        dx, dg_rows = _rms_bwd(x_ref[...], g_ref[...], dxn)
        dx_ref[...] = dh1_ref[...] + dx
        _accumulate(dg_ref, i == 0, jnp.sum(dg_rows, axis=0, keepdims=True))

    row = lambda i: (i, 0)
    const = lambda i: (0, 0)
    return _call(
        "in_proj_bwd",
        body,
        (T // tm,),
        [
            pl.BlockSpec((tm, CI), row),
            pl.BlockSpec((CI, D), const),
            pl.BlockSpec((tm, D), row),
            pl.BlockSpec((tm, D), row),
            pl.BlockSpec((1, D), const),
        ],
        [pl.BlockSpec((tm, D), row), pl.BlockSpec((1, D), const)],
        [jax.ShapeDtypeStruct((T, D), F32), jax.ShapeDtypeStruct((1, D), F32)],
        (dz_b, w_inT_b, x, dh1, g_mix),
        comm=comm,
    )


def _weight_grad(name, a_b, b_b, comm=()):
    T, N1 = a_b.shape
    N2 = b_b.shape[1]
    t1 = _tile(N1, 1408, LANES)
    tk = _tile(T, 2048)
    nk = T // tk

    def body(a_ref, b_ref, o_ref, acc):
        k = pl.program_id(1)
        _accumulate(acc, k == 0, _dot(a_ref[...], b_ref[...], TN))

        @pl.when(k == nk - 1)
        def _():
            o_ref[...] = acc[...].astype(BF16)

    (out,), rest = _call(
        name,
        body,
        (N1 // t1, nk),
        [pl.BlockSpec((tk, t1), lambda n, k: (k, n)), pl.BlockSpec((tk, N2), lambda n, k: (k, 0))],
        [pl.BlockSpec((t1, N2), lambda n, k: (n, 0))],
        [jax.ShapeDtypeStruct((N1, N2), BF16)],
        (a_b, b_b),
        scratch=[pltpu.VMEM((t1, N2), F32)],
        comm=comm,
    )
    return out, rest


def _sum_parts(name, full, how, parts, me):
    _, R, C = parts[0].shape
    tr = _tile(R, 512)
    nb = R // tr
    where = [(q, r) for q, p in enumerate(parts) for r in range(p.shape[0])]
    assert len(where) == 3

    def body(me_ref, own_ref, *refs):
        o_ref = refs[-1]
        f = lambda j: refs[where[j][0]][where[j][1]].astype(F32)
        o_ref[...] = (own_ref[...].astype(F32) + f(0)) + (f(1) + f(2))

    own_map = {"rows": lambda i, me_ref: (me_ref[0] * nb + i, 0), "cols": lambda i, me_ref: (i, me_ref[0]),
               "all": lambda i, me_ref: (i, 0)}[how]
    return pl.pallas_call(
        body,
        name=name,
        grid_spec=pltpu.PrefetchScalarGridSpec(
            num_scalar_prefetch=1,
            grid=(nb,),
            in_specs=[pl.BlockSpec((tr, C), own_map)]
            + [pl.BlockSpec((p.shape[0], tr, C), lambda i, me_ref: (0, i, 0)) for p in parts],
            out_specs=pl.BlockSpec((tr, C), lambda i, me_ref: (i, 0)),
        ),
        out_shape=jax.ShapeDtypeStruct((R, C), F32),
        compiler_params=pltpu.CompilerParams(dimension_semantics=("arbitrary",), vmem_limit_bytes=VMEM_LIMIT),
    )(me, full, *parts)


_M_CORR = 1.0 - ADAM_B1**ADAM_STEP
_V_CORR = 1.0 - ADAM_B2**ADAM_STEP


def _adamw_math(w, g, m, v):
    m = ADAM_B1 * m + (1.0 - ADAM_B1) * g
    v = ADAM_B2 * v + (1.0 - ADAM_B2) * (g * g)
    delta = -ADAM_LR * ((m / _M_CORR) / (jnp.sqrt(v / _V_CORR) + ADAM_EPS) + ADAM_WD * w)
    return delta, m, v


def _adamw(name, w, m, v, g_here, g_there, g_transposed=False, comm=()):
    R, C = w.shape
    tr = _tile(R, 256, LANES if g_transposed else 8)

    def body(w_ref, m_ref, v_ref, ga_ref, gb_ref, g_ref, d_ref, nm_ref, nv_ref):
        g = ga_ref[...] + gb_ref[...]
        if g_transposed:
            g = g.T
        g_ref[...] = g
        d_ref[...], nm_ref[...], nv_ref[...] = _adamw_math(w_ref[...], g, m_ref[...], v_ref[...])

    spec = pl.BlockSpec((tr, C), lambda i: (i, 0))
    gspec = pl.BlockSpec((C, tr), lambda i: (0, i)) if g_transposed else spec
    return _call(name, body, (R // tr,), [spec] * 3 + [gspec] * 2, [spec] * 4, [jax.ShapeDtypeStruct((R, C), F32)] * 4,
                 (w, m, v, g_here, g_there), comm=comm)


def _adamw_on_sparsecore(name, w, m, v, g_here, g_there):
    R, C = w.shape
    n_groups = R // SUBLANES

    def body(w_hbm, m_hbm, v_hbm, ga_hbm, gb_hbm, g_out, d_out, nm_out, nv_out, wb, mb, vb, gab, gbb):
        tile = lax.axis_index("subcore") * SC_CORES + lax.axis_index("sparsecore")

        @pl.loop(tile, n_groups, step=SC_TILES)
        def _(group):
            rows = pl.ds(group * SUBLANES, SUBLANES)
            for src, buf in ((w_hbm, wb), (m_hbm, mb), (v_hbm, vb), (ga_hbm, gab), (gb_hbm, gbb)):
                pltpu.sync_copy(src.at[rows, :], buf)

            @pl.loop(0, SUBLANES)
            def _(r):
                @pl.loop(0, C, step=SC_LANES)
                def _(i):
                    at = (r, pl.ds(i, SC_LANES))
                    g = gab[at] + gbb[at]
                    delta, new_m, new_v = _adamw_math(wb[at], g, mb[at], vb[at])
                    gab[at], wb[at], mb[at], vb[at] = g, delta, new_m, new_v

            for buf, dst in ((gab, g_out), (wb, d_out), (mb, nm_out), (vb, nv_out)):
                pltpu.sync_copy(buf, dst.at[rows, :])

    return pl.kernel(
        body,
        name=name,
        out_type=[jax.ShapeDtypeStruct((R, C), F32)] * 4,
        mesh=plsc.VectorSubcoreMesh(core_axis_name="sparsecore", subcore_axis_name="subcore"),
        scratch_types=[pltpu.VMEM((SUBLANES, C), F32)] * 5,
        compiler_params=pltpu.CompilerParams(use_tc_tiling_on_sc=True),
    )(w, m, v, g_here, g_there)


class _PackLayout:
    def __init__(self, n_cc, n_grp, G, widths):
        self.dw_rows = (0, HALO)
        self.wp_rows = (HALO, HALO + G)
        self.n_cc, self.n_grp, self.G = n_cc, n_grp, G
        self.vec = {}
        r = HALO + G
        for name, width in widths:
            self.vec[name] = (r, width)
            r += width // PACK_W
        self.rows = -(-r // 8) * 8


def _pack_small(layout, dwdw, dwp, vecs):
    names = list(vecs)

    def body(*refs):
        dw_ref, wp_ref = refs[0], refs[1]
        vec_refs = refs[2 : 2 + len(names)]
        o_ref = refs[-1]
        o_ref[...] = jnp.zeros_like(o_ref)
        for j in range(layout.n_cc):
            o_ref[layout.dw_rows[0] : layout.dw_rows[1], j * LANES : (j + 1) * LANES] = dw_ref[j]
        for i in range(layout.n_grp):
            o_ref[layout.wp_rows[0] : layout.wp_rows[1], i * layout.G : (i + 1) * layout.G] = wp_ref[i]
        for name, ref in zip(names, vec_refs):
            r, width = layout.vec[name]
            for h in range(width // PACK_W):
                o_ref[r + h : r + h + 1, :] = ref[:, h * PACK_W : (h + 1) * PACK_W]

    return pl.pallas_call(
        body,
        name="pack_small",
        out_shape=jax.ShapeDtypeStruct((layout.rows, PACK_W), F32),
    )(dwdw, dwp, *[vecs[k] for k in names])


def _adamw_small(layout, g_here, g_there, w_dw, m_dw, v_dw, w_pool, m_pool, v_pool, vec_w, vec_m, vec_v):
    names = list(vec_w)
    nv = len(names)

    def body(*refs):
        ga_ref, gb_ref = refs[0], refs[1]
        wdw, mdw, vdw, wp, mp, vp = refs[2:8]
        vw, vm, vv = refs[8 : 8 + nv], refs[8 + nv : 8 + 2 * nv], refs[8 + 2 * nv : 8 + 3 * nv]
        outs = refs[8 + 3 * nv :]
        acc = outs[-1]
        acc[...] = ga_ref[...] + gb_ref[...]

        def emit(o, g, w, m, v, idx=()):
            res = (g,) + _adamw_math(w, g, m, v)
            for ref, val in zip(o, res):
                ref[idx] = val

        me = 2 * lax.axis_index("x") + lax.axis_index("y")
        for j in range(layout.n_cc):

            @pl.when(me == j)
            def _(j=j):
                g = acc[layout.dw_rows[0] : layout.dw_rows[1], j * LANES : (j + 1) * LANES]
                emit(outs[0:4], g, wdw[...], mdw[...], vdw[...], idx=...)

        for i in range(layout.n_grp):
            g = acc[layout.wp_rows[0] : layout.wp_rows[1], i * layout.G : (i + 1) * layout.G]
            emit(outs[4:8], g, wp[i], mp[i], vp[i], idx=i)
        for q, name in enumerate(names):
            r, width = layout.vec[name]
            for h in range(width // PACK_W):
                ls = slice(h * PACK_W, (h + 1) * PACK_W)
                g = acc[r + h : r + h + 1, :]
                emit(outs[8 + 4 * q : 12 + 4 * q], g, vw[q][:, ls], vm[q][:, ls], vv[q][:, ls], idx=(slice(None), ls))

    shapes = [w_dw.shape] * 4 + [w_pool.shape] * 4
    for name in names:
        shapes += [vec_w[name].shape] * 4
    return pl.pallas_call(
        body,
        name="adamw_small",
        out_shape=[jax.ShapeDtypeStruct(s, F32) for s in shapes],
        scratch_shapes=[pltpu.VMEM(g_here.shape, F32)],
    )(g_here, g_there, w_dw, m_dw, v_dw, w_pool, m_pool, v_pool,
      *[vec_w[k] for k in names], *[vec_m[k] for k in names], *[vec_v[k] for k in names])


def _allreduce_adamw_row(g_part, w, m, v, loss_part, comm=()):
    D = w.shape[1]
    n_pairs = N_DEV - 1

    def body(g_ref, w_ref, m_ref, v_ref, l_ref, go_ref, d_ref, nm_ref, nv_ref, lo_ref, land_g, land_l, sems):
        x, y, c = _place()
        copies = []
        for q, (src, land) in enumerate(((g_ref, land_g), (l_ref, land_l))):
            for r in range(1, N_DEV):
                fx, fy, fc = (r >> 2) & 1, (r >> 1) & 1, r & 1
                peer = (1 - x if fx else x, 1 - y if fy else y, 1 - c if fc else c)
                cp = _remote(src, land.at[r], sems, 2 * (q * n_pairs + r - 1), peer)
                cp.start()
                copies.append(cp)
        for cp in copies:
            cp.wait()

        def total(src, land):
            row = lambda r: src[...] if r == 0 else land[r]
            return ((row(0) + row(4)) + (row(2) + row(6))) + ((row(1) + row(5)) + (row(3) + row(7)))

        g = total(g_ref, land_g)
        go_ref[...] = g
        d_ref[...], nm_ref[...], nv_ref[...] = _adamw_math(w_ref[...], g, m_ref[...], v_ref[...])
        lo_ref[...] = total(l_ref, land_l)

    vm = pl.BlockSpec(memory_space=pltpu.VMEM)
    return _call(
        "allreduce_adamw_g_mix",
        body,
        (),
        [vm] * 5,
        [vm] * 5,
        [jax.ShapeDtypeStruct((1, D), F32)] * 4 + [jax.ShapeDtypeStruct(loss_part.shape, F32)],
        (g_part, w, m, v, loss_part),
        scratch=[pltpu.VMEM((N_DEV, 1, D), F32), pltpu.VMEM((N_DEV,) + loss_part.shape, F32),
                 pltpu.SemaphoreType.DMA((4 * n_pairs,))],
        comm=comm,
    )


def kernel(x, g_mix, w_in, b_in, w_dw, b_dw, ln_g, ln_b, w_pool, s_pool, w_out, g_ffn, w_gate, w_up, w_down, g_final, loss_target, m_g_mix, m_w_in, m_b_in, m_w_dw, m_b_dw, m_ln_g, m_ln_b, m_w_pool, m_s_pool, m_w_out, m_g_ffn, m_w_gate, m_w_up, m_w_down, m_g_final, v_g_mix, v_w_in, v_b_in, v_w_dw, v_b_dw, v_ln_g, v_ln_b, v_w_pool, v_s_pool, v_w_out, v_g_ffn, v_w_gate, v_w_up, v_w_down, v_g_final):
    x2 = x[0]
    target = loss_target[0]
    T, D = x2.shape
    w_in2, w_out2, w_down2, w_dw2 = w_in[0], w_out[0], w_down[0], w_dw[0]
    w_gateT, w_upT = w_gate[0].T, w_up[0].T
    CI = w_in2.shape[1] * N_CHIPS
    DM = w_out2.shape[0] * N_CHIPS
    F = w_down2.shape[0] * N_CHIPS
    KW, dw_cols = w_dw2.shape
    assert dw_cols == LANES
    n_grp, G = w_pool.shape[1], w_pool.shape[-1]
    w_pool3 = w_pool[0]
    g_final2 = g_final.reshape(1, D)

    me = (2 * lax.axis_index("x") + lax.axis_index("y")).astype(jnp.int32).reshape(1)

    w_inT_b, w_dw4, f_out, f_gate, f_up, f_down = _place_and_gather(
        [(w_in2, "rows", (CI, D), BF16, True, True), (w_dw2, "lead", (N_CHIPS, KW, dw_cols), F32, False, False)],
        [(w, "rows", shape, BF16, False, True)
         for w, shape in ((w_out2, (DM, D)), (w_gateT, (F, D)), (w_upT, (F, D)), (w_down2, (F, D)))])
    w_pool_b = w_pool3.astype(BF16)
    (z, xn_b), (f_out, f_gate) = _in_proj(
        x2, g_mix, w_inT_b, b_in,
        comm=[_GatherIci([f_out], ["rows"], [True]), _GatherIci([f_gate], ["rows"], [True], which=(2,))])
    (y_b, v), (w_out_b, f_gate, f_up) = _seq_fwd(
        z, w_dw4, b_dw, ln_g, ln_b, w_pool_b, s_pool,
        comm=[_GatherD2d([f_out], ["rows"]), _GatherIci([f_gate], ["rows"], [True], which=(0, 1)),
              _GatherIci([f_up], ["rows"], [True])])
    (h1, hn_b), (wgT_b, wuT_b, f_down) = _out_proj(
        y_b, x2, w_out_b, g_ffn,
        comm=[_GatherD2d([f_gate, f_up], ["rows"] * 2), _GatherIci([f_down], ["rows"], [True])])
    (g_b, u_b, a_b), (wd_b,) = _gate_up(hn_b, wgT_b, wuT_b, comm=[_GatherD2d([f_down], ["rows"])])
    (dh2, dh2_b, loss_part, d_g_final), _ = _down_loss(a_b, wd_b, h1, target, g_final2)

    gw_down, _ = _weight_grad("grad_w_down", a_b, dh2_b)
    (dg_b, du_b), (p_down_xy,) = _ffn_bwd_act(dh2_b, wd_b, g_b, u_b, comm=[_Scatter([gw_down], ["rows"], which=(0, 1))])
    gw_gateT, (p_down_d,) = _weight_grad("grad_w_gate", dg_b, hn_b, comm=[_Scatter([gw_down], ["rows"], which=(2,))])
    gw_upT, _ = _weight_grad("grad_w_up", du_b, hn_b)
    sum_down = _sum_parts("sum_w_down", gw_down, "rows", [p_down_xy, p_down_d], me)
    (dh1, dh1_b, dy, d_g_ffn), (p_gate, oth_down) = _ffn_bwd_in(
        dg_b, du_b, wgT_b, wuT_b, h1, dh2, g_ffn, w_out_b, comm=[_Scatter([gw_gateT], ["rows"]), _Swap([sum_down])])
    gw_out, _ = _weight_grad("grad_w_out", y_b, dh1_b)
    sum_gate = _sum_parts("sum_w_gate", gw_gateT, "rows", [p_gate], me)
    res = {}
    res["w_down"] = _adamw_on_sparsecore("adamw_w_down", w_down2, m_w_down[0], v_w_down[0], sum_down, oth_down)
    (dz_b, d_wdw, d_bdw, d_lng, d_lnb, d_wp, d_sp, d_bin), (p_up, p_out, oth_gate) = _seq_bwd(
        z, dy, v, w_dw4, ln_g, ln_b, w_pool_b, s_pool,
        comm=[_Scatter([gw_upT, gw_out], ["rows", "rows"]), _Swap([sum_gate])])
    vec_grads = {"b_dw": d_bdw, "ln_g": d_lng, "ln_b": d_lnb, "s_pool": d_sp, "g_ffn": d_g_ffn, "g_final": d_g_final, "b_in": d_bin}
    layout = _PackLayout(dw_cols * N_CHIPS // LANES, n_grp, G, [(k, a.shape[1]) for k, a in vec_grads.items()])
    pack = _pack_small(layout, d_wdw, d_wp, vec_grads)
    sum_up = _sum_parts("sum_w_up", gw_upT, "rows", [p_up], me)
    sum_out = _sum_parts("sum_w_out", gw_out, "rows", [p_out], me)
    gw_inT, (p_small, oth_up, oth_out) = _weight_grad(
        "grad_w_in", dz_b, xn_b, comm=[_Scatter([pack], ["all"]), _Swap([sum_up, sum_out])])
    sum_small = _sum_parts("sum_small", pack, "all", [p_small], me)
    res["w_gate"] = _adamw_on_sparsecore("adamw_w_gate", w_gateT, m_w_gate[0].T, v_w_gate[0].T, sum_gate, oth_gate)
    (grad_x, d_g_mix), (p_in, oth_small) = _in_proj_bwd(
        dz_b, w_inT_b, x2, dh1, g_mix, comm=[_Scatter([gw_inT], ["rows"]), _Swap([sum_small])])
    res["w_up"] = _adamw_on_sparsecore("adamw_w_up", w_upT, m_w_up[0].T, v_w_up[0].T, sum_up, oth_up)
    res["w_out"] = _adamw_on_sparsecore("adamw_w_out", w_out2, m_w_out[0], v_w_out[0], sum_out, oth_out)
    sum_in = _sum_parts("sum_w_in", gw_inT, "rows", [p_in], me)
    (*res["g_mix"], loss_row), (oth_in,) = _allreduce_adamw_row(
        d_g_mix, g_mix, m_g_mix, v_g_mix, loss_part, comm=[_Swap([sum_in])])
    loss = loss_row[0, 0]
    res["w_in"], _ = _adamw("adamw_w_in", w_in2, m_w_in[0], v_w_in[0], sum_in, oth_in, g_transposed=True)

    pad_dw = lambda a: jnp.pad(a[0], ((0, HALO - KW), (0, 0)))
    vec_w = {"b_dw": b_dw, "ln_g": ln_g, "ln_b": ln_b, "s_pool": s_pool, "g_ffn": g_ffn, "g_final": g_final2, "b_in": b_in}
    vec_m = {"b_dw": m_b_dw, "ln_g": m_ln_g, "ln_b": m_ln_b, "s_pool": m_s_pool, "g_ffn": m_g_ffn,
             "g_final": m_g_final.reshape(1, D), "b_in": m_b_in}
    vec_v = {"b_dw": v_b_dw, "ln_g": v_ln_g, "ln_b": v_ln_b, "s_pool": v_s_pool, "g_ffn": v_g_ffn,
             "g_final": v_g_final.reshape(1, D), "b_in": v_b_in}
    small = _adamw_small(layout, sum_small, oth_small, pad_dw(w_dw), pad_dw(m_w_dw), pad_dw(v_w_dw),
                         w_pool3, m_w_pool[0], v_w_pool[0], vec_w, vec_m, vec_v)
    res["w_dw"] = [a[:KW][None] for a in small[0:4]]
    res["w_pool"] = [a[None] for a in small[4:8]]
    for q, k in enumerate(vec_w):
        res[k] = list(small[8 + 4 * q : 12 + 4 * q])
    res["g_final"] = [a.reshape(D) for a in res["g_final"]]
    for k in ("w_in", "w_out", "w_down"):
        res[k] = [a[None] for a in res[k]]
    for k in ("w_gate", "w_up"):
        res[k] = [a.T[None] for a in res[k]]

    order = ["g_mix", "w_in", "b_in", "w_dw", "b_dw", "ln_g", "ln_b", "w_pool", "s_pool", "w_out", "g_ffn", "w_gate", "w_up", "w_down", "g_final"]
    outs = [loss, grad_x[None]]
    for q in range(4):
        outs += [res[k][q] for k in order]
    return tuple(outs)
```

```python
import jax
import jax.numpy as jnp
from jax import lax
from jax.experimental import pallas as pl
from jax.experimental.pallas import tpu as pltpu
from jax.experimental.pallas import tpu_sc as plsc

F32 = jnp.float32
BF16 = jnp.bfloat16
MESH = pl.DeviceIdType.MESH
ANY = pl.BlockSpec(memory_space=pl.ANY)

RMS_EPS = 1e-6
LN_EPS = 1e-5
POOL_WINDOWS = (2, 4, 8, 16)
ADAM_LR = 0.001
ADAM_B1 = 0.9
ADAM_B2 = 0.999
ADAM_EPS = 1e-08
ADAM_WD = 0.01
ADAM_STEP = 10

LANES = 128
SUBLANES = 8
HALO = 32
CONV_ROWS = 64
HIDDEN_CHUNK = 512
VMEM_LIMIT = 56 * 1024 * 1024
PACK_W = 512
N_CHIPS = 4
N_DEV = 8
SC_CORES = 2
SC_TILES = 32
SC_LANES = 16


def _tile(n, want, mult=8):
    t = min(n, want)
    while n % t or t % mult:
        t -= 1
    return t


def _sigmoid(x):
    return 1.0 / (1.0 + jnp.exp(-x))


def _dot(a, b, dims):
    return lax.dot_general(a, b, (dims, ((), ())), preferred_element_type=F32)


NN = ((1,), (0,))
NT = ((1,), (1,))
TN = ((0,), (0,))


def _rms_bwd(x, g, dy):
    r = lax.rsqrt(jnp.mean(x * x, axis=-1, keepdims=True) + RMS_EPS)
    xh = x * r
    gy = dy * g
    dx = r * (gy - xh * jnp.mean(gy * xh, axis=-1, keepdims=True))
    return dx, dy * xh


def _accumulate(ref, first, val):
    @pl.when(first)
    def _():
        ref[...] = val

    @pl.when(jnp.logical_not(first))
    def _():
        ref[...] += val


def _place():
    return lax.axis_index("x"), lax.axis_index("y"), lax.axis_index("c")


def _other_chips(x, y):
    return [(1 - x, y), (x, 1 - y), (1 - x, 1 - y)]


def _rows(ref, start, n):
    return ref.at[pl.ds(pl.multiple_of(start, 16), n)]


def _window(ref, how, k, c=None):
    if how == "all":
        return ref
    if how == "lead":
        return ref.at[k]
    if how == "rows":
        n = ref.shape[0] // N_CHIPS
        if c is None:
            return _rows(ref, k * n, n)
        return _rows(ref, k * n + c * (n // 2), n // 2)
    n = ref.shape[1] // N_CHIPS
    cols = pl.ds(pl.multiple_of(k * n, LANES), n)
    if c is None:
        return ref.at[:, cols]
    h = ref.shape[0] // 2
    return ref.at[pl.ds(pl.multiple_of(c * h, 16), h), cols]


def _remote(src, dst, sems, s, device):
    return pltpu.make_async_remote_copy(
        src_ref=src, dst_ref=dst, send_sem=sems.at[s], recv_sem=sems.at[s + 1], device_id=device, device_id_type=MESH)


class _GatherIci:
    aliased = True

    def __init__(self, fulls, hows, splits, which=(0, 1, 2)):
        self.fulls, self.hows, self.splits, self.which = list(fulls), list(hows), list(splits), tuple(which)

    def inputs(self):
        return self.fulls

    def out_shapes(self):
        return [jax.ShapeDtypeStruct(a.shape, a.dtype) for a in self.fulls]

    def n_sems(self):
        return 6 * len(self.fulls)

    def build(self, ins, outs, sems, base):
        x, y, c = _place()
        me = 2 * x + y
        chips = _other_chips(x, y)
        starts, waits = [], []
        for a, (how, sp) in enumerate(zip(self.hows, self.splits)):
            half = c if sp else None
            mine = _window(outs[a], how, me, half)
            for j in self.which:
                px, py = chips[j]
                s = base + 6 * a + 2 * j
                cp = _remote(mine, mine, sems, s, (px, py, c))
                landing = _remote(mine, _window(outs[a], how, 2 * px + py, half), sems, s, (px, py, c))
                starts.append(cp.start)
                waits += [landing.wait_recv, cp.wait_send]
        return starts, waits


class _GatherD2d:
    aliased = True

    def __init__(self, fulls, hows):
        self.fulls, self.hows = list(fulls), list(hows)

    def inputs(self):
        return self.fulls

    def out_shapes(self):
        return [jax.ShapeDtypeStruct(a.shape, a.dtype) for a in self.fulls]

    def n_sems(self):
        return 6 * len(self.fulls)

    def build(self, ins, outs, sems, base):
        x, y, c = _place()
        starts, waits = [], []
        for a, how in enumerate(self.hows):
            for j, (px, py) in enumerate(_other_chips(x, y)):
                s = base + 6 * a + 2 * j
                got = _window(outs[a], how, 2 * px + py, c)
                cp = _remote(got, got, sems, s, (x, y, 1 - c))
                landing = _remote(got, _window(outs[a], how, 2 * px + py, 1 - c), sems, s, (x, y, 1 - c))
                starts.append(cp.start)
                waits += [landing.wait_recv, cp.wait_send]
        return starts, waits


def _part_shape(a, how):
    if how == "all":
        return a.shape
    if how == "rows":
        return (a.shape[0] // N_CHIPS, a.shape[1])
    return (a.shape[0], a.shape[1] // N_CHIPS)


class _Scatter:
    aliased = False

    def __init__(self, fulls, hows, which=(0, 1, 2)):
        self.fulls, self.hows, self.which = list(fulls), list(hows), tuple(which)

    def inputs(self):
        return self.fulls

    def out_shapes(self):
        return [jax.ShapeDtypeStruct((len(self.which),) + _part_shape(a, h), a.dtype) for a, h in zip(self.fulls, self.hows)]

    def n_sems(self):
        return 6 * len(self.fulls)

    def build(self, ins, outs, sems, base):
        x, y, c = _place()
        chips = _other_chips(x, y)
        starts, waits = [], []
        for a, how in enumerate(self.hows):
            for slot, j in enumerate(self.which):
                px, py = chips[j]
                cp = _remote(_window(ins[a], how, 2 * px + py), outs[a].at[slot], sems, base + 6 * a + 2 * j, (px, py, c))
                starts.append(cp.start)
                waits += [cp.wait_recv, cp.wait_send]
        return starts, waits


class _Swap:
    aliased = False

    def __init__(self, arrays):
        self.arrays = list(arrays)

    def inputs(self):
        return self.arrays

    def out_shapes(self):
        return [jax.ShapeDtypeStruct(a.shape, a.dtype) for a in self.arrays]

    def n_sems(self):
        return 2 * len(self.arrays)

    def build(self, ins, outs, sems, base):
        x, y, c = _place()
        starts, waits = [], []
        for a in range(len(ins)):
            cp = _remote(ins[a], outs[a], sems, base + 2 * a, (x, y, 1 - c))
            starts.append(cp.start)
            waits += [cp.wait_recv, cp.wait_send]
        return starts, waits


def _call(name, body, grid, in_specs, out_specs, out_shape, args, scratch=(), comm=()):
    comm = list(comm)
    n_in, n_out, n_scr = len(args), len(out_shape), len(scratch)
    c_in = [a for op in comm for a in op.inputs()]
    c_out = [s for op in comm for s in op.out_shapes()]
    n_sems = sum(op.n_sems() for op in comm)
    aliases, i_in, i_out = {}, 0, 0
    for op in comm:
        if op.aliased:
            for q in range(len(op.inputs())):
                aliases[n_in + i_in + q] = n_out + i_out + q
        i_in, i_out = i_in + len(op.inputs()), i_out + len(op.out_shapes())

    def wrapped(*refs):
        ins = refs[:n_in]
        cin = refs[n_in : n_in + len(c_in)]
        o0 = n_in + len(c_in)
        outs = refs[o0 : o0 + n_out]
        cout = refs[o0 + n_out : o0 + n_out + len(c_out)]
        s0 = o0 + n_out + len(c_out)
        scr = refs[s0 : s0 + n_scr]

        def copies():
            sems = refs[s0 + n_scr]
            starts, waits = [], []
            i_in = i_out = base = 0
            for op in comm:
                ni, no = len(op.inputs()), len(op.out_shapes())
                s, w = op.build(cin[i_in : i_in + ni], cout[i_out : i_out + no], sems, base)
                starts += s
                waits += w
                i_in, i_out, base = i_in + ni, i_out + no, base + op.n_sems()
            return starts, waits

        def run_starts():
            for start in copies()[0]:
                start()

        def run_waits():
            for wait in copies()[1]:
                wait()

        if comm and grid:
            first = last = True
            for d, n in enumerate(grid):
                first = jnp.logical_and(first, pl.program_id(d) == 0)
                last = jnp.logical_and(last, pl.program_id(d) == n - 1)
            pl.when(first)(run_starts)
        elif comm:
            run_starts()
        if body is not None:
            body(*ins, *outs, *scr)
        if comm and grid:
            pl.when(last)(run_waits)
        elif comm:
            run_waits()

    res = pl.pallas_call(
        wrapped,
        name=name,
        grid=grid,
        in_specs=list(in_specs) + [ANY] * len(c_in),
        out_specs=list(out_specs) + [ANY] * len(c_out),
        out_shape=list(out_shape) + c_out,
        scratch_shapes=list(scratch) + ([pltpu.SemaphoreType.DMA((n_sems,))] if comm else []),
        input_output_aliases=aliases,
        compiler_params=pltpu.CompilerParams(dimension_semantics=("arbitrary",) * len(grid), vmem_limit_bytes=VMEM_LIMIT),
    )(*args, *c_in)
    return tuple(res[:n_out]), tuple(res[n_out:])


def _place_and_gather(now, later):
    items = list(now) + list(later)
    n, n_now = len(items), len(now)
    buf_shape = lambda it: it[0].shape[::-1] if it[4] else it[0].shape
    split_now = [a for a in range(n_now) if items[a][5]]

    def body(*refs):
        ins, outs = refs[:n], refs[n : 2 * n]
        stage, bufs = refs[2 * n : 3 * n - n_now], refs[3 * n - n_now : 4 * n - n_now]
        sems = refs[4 * n - n_now]
        x, y, c = _place()
        me = 2 * x + y
        chips = _other_chips(x, y)
        loads = [pltpu.make_async_copy(ins[a], stage[a - n_now], sems.at[a]) for a in range(n_now, n)]
        for ld in loads:
            ld.start()
        pending = []

        def place(a, val):
            _, how, _, dtype, transposed, _ = items[a]
            bufs[a][...] = (val.T if transposed else val).astype(dtype)
            cp = pltpu.make_async_copy(bufs[a], _window(outs[a], how, me), sems.at[n + a])
            cp.start()
            pending.append(cp.wait)

        arrivals = []
        for a in range(n_now):
            place(a, ins[a][...])
            how, split = items[a][1], items[a][5]
            half = c if split else None
            src = _rows(bufs[a], c * (bufs[a].shape[0] // 2), bufs[a].shape[0] // 2) if split else bufs[a]
            for j, (px, py) in enumerate(chips):
                s = 2 * n + 6 * a + 2 * j
                cp = _remote(src, _window(outs[a], how, me, half), sems, s, (px, py, c))
                landing = _remote(src, _window(outs[a], how, 2 * px + py, half), sems, s, (px, py, c))
                cp.start()
                arrivals.append(landing.wait_recv)
                pending.append(cp.wait_send)
        for a in range(n_now, n):
            loads[a - n_now].wait()
            place(a, stage[a - n_now][...])
        for wait in arrivals:
            wait()
        d2d = _GatherD2d([None] * len(split_now), [items[a][1] for a in split_now])
        starts, waits = d2d.build(None, [outs[a] for a in split_now], sems, 2 * n + 6 * n_now)
        for start in starts:
            start()
        for wait in waits + pending:
            wait()

    vm = pl.BlockSpec(memory_space=pltpu.VMEM)
    return pl.pallas_call(
        body,
        name="place_and_gather",
        in_specs=[vm] * n_now + [ANY] * (n - n_now),
        out_specs=[ANY] * n,
        out_shape=[jax.ShapeDtypeStruct(it[2], it[3]) for it in items],
        scratch_shapes=[pltpu.VMEM(it[0].shape, it[0].dtype) for it in later]
        + [pltpu.VMEM(buf_shape(it), it[3]) for it in items]
        + [pltpu.SemaphoreType.DMA((2 * n + 6 * n_now + 6 * len(split_now),))],
        compiler_params=pltpu.CompilerParams(vmem_limit_bytes=VMEM_LIMIT),
    )(*[it[0] for it in items])


def _in_proj(x, g_mix, w_inT_b, b_in, comm=()):
    T, D = x.shape
    CI = w_inT_b.shape[0]
    tm = _tile(T, 512)

    def body(x_ref, g_ref, w_ref, b_ref, z_ref, xn_ref):
        xv = x_ref[...]
        r = lax.rsqrt(jnp.mean(xv * xv, axis=-1, keepdims=True) + RMS_EPS)
        xn = (xv * r * g_ref[...]).astype(BF16)
        xn_ref[...] = xn
        z_ref[...] = _dot(xn, w_ref[...], NT) + b_ref[...]

    return _call(
        "in_proj",
        body,
        (T // tm,),
        [
            pl.BlockSpec((tm, D), lambda i: (i, 0)),
            pl.BlockSpec((1, D), lambda i: (0, 0)),
            pl.BlockSpec((CI, D), lambda i: (0, 0)),
            pl.BlockSpec((1, CI), lambda i: (0, 0)),
        ],
        [pl.BlockSpec((tm, CI), lambda i: (i, 0)), pl.BlockSpec((tm, D), lambda i: (i, 0))],
        [jax.ShapeDtypeStruct((T, CI), F32), jax.ShapeDtypeStruct((T, D), BF16)],
        (x, g_mix, w_inT_b, b_in),
        comm=comm,
    )


def _fill_shifted(scr):
    n = scr.shape[1] - SUBLANES
    for s in range(1, SUBLANES):
        scr[s, 0:n, :] = scr[0, s : s + n, :]


def _shifted_rows(scr, off, n, cs):
    s = off % SUBLANES
    return scr[s, off - s : off - s + n, cs]


def _pool_mean_minus_token(p_scr, cs, w, cnt, tt):
    tok = p_scr[HALO : HALO + tt, cs]
    s = tok
    for d in range(1, w):
        s = s + p_scr[HALO - d : HALO - d + tt, cs]
    return s / cnt - tok


def _seq_fwd(z, w_dw4, b_dw, ln_g, ln_b, w_pool_b, s_pool, comm=()):
    T, CI = z.shape
    CC = ln_g.shape[1]
    n_grp, G = w_pool_b.shape[0], w_pool_b.shape[-1]
    KW = w_dw4.shape[1]
    D = CC + n_grp * G
    tt = _tile(T, 512, HALO)
    per = tt // HALO

    def body(zc_ref, zp_ref, wdw_ref, bdw_ref, lng_ref, lnb_ref, wp_ref, sp_ref, y_ref, v_ref, u_scr, p_scr):
        i = pl.program_id(0)
        first = i == 0
        u_prev = zp_ref[:, 0:CC] * _sigmoid(zp_ref[:, CC : 2 * CC])
        u_scr[0, 0:HALO, :] = jnp.where(first, 0.0, u_prev)
        p_scr[0:HALO, :] = jnp.where(first, 0.0, zp_ref[:, 2 * CC :])
        u_scr[0, HALO:, :] = zc_ref[:, 0:CC] * _sigmoid(zc_ref[:, CC : 2 * CC])
        p_scr[HALO:, :] = zc_ref[:, 2 * CC :]
        _fill_shifted(u_scr)

        for j in range(CC // LANES):
            cs = slice(LANES * j, LANES * (j + 1))
            for rb in range(tt // CONV_ROWS):
                acc = jnp.zeros((CONV_ROWS, LANES), F32)
                for k in range(KW):
                    off = HALO - (KW - 1) + k + rb * CONV_ROWS
                    acc = acc + _shifted_rows(u_scr, off, CONV_ROWS, cs) * wdw_ref[j, k : k + 1, :]
                v_ref[rb * CONV_ROWS : (rb + 1) * CONV_ROWS, cs] = acc + bdw_ref[:, cs]

        v = v_ref[...]
        mu = jnp.mean(v, axis=-1, keepdims=True)
        d = v - mu
        var = jnp.mean(d * d, axis=-1, keepdims=True)
        ln = d * lax.rsqrt(var + LN_EPS) * lng_ref[...] + lnb_ref[...]
        y_ref[:, 0:CC] = (ln * _sigmoid(ln)).astype(BF16)

        tpos = i * tt + lax.broadcasted_iota(jnp.int32, (tt, 1), 0)
        for gi, w in enumerate(POOL_WINDOWS):
            cs = slice(G * gi, G * (gi + 1))
            cnt = jnp.minimum(tpos + 1, w).astype(F32)
            yi = _pool_mean_minus_token(p_scr, cs, w, cnt, tt)
            q = _dot(yi.astype(BF16), wp_ref[gi], NN)
            y_ref[:, CC + G * gi : CC + G * (gi + 1)] = (q * sp_ref[:, cs]).astype(BF16)

    const2 = lambda i: (0, 0)
    return _call(
        "seq_fwd",
        body,
        (T // tt,),
        [
            pl.BlockSpec((tt, CI), lambda i: (i, 0)),
            pl.BlockSpec((HALO, CI), lambda i: (jnp.maximum(i * per - 1, 0), 0)),
            pl.BlockSpec(w_dw4.shape, lambda i: (0, 0, 0)),
            pl.BlockSpec((1, CC), const2),
            pl.BlockSpec((1, CC), const2),
            pl.BlockSpec((1, CC), const2),
            pl.BlockSpec(w_pool_b.shape, lambda i: (0, 0, 0)),
            pl.BlockSpec((1, n_grp * G), const2),
        ],
        [pl.BlockSpec((tt, D), lambda i: (i, 0)), pl.BlockSpec((tt, CC), lambda i: (i, 0))],
        [jax.ShapeDtypeStruct((T, D), BF16), jax.ShapeDtypeStruct((T, CC), F32)],
        (z, z, w_dw4, b_dw, ln_g, ln_b, w_pool_b, s_pool),
        scratch=[pltpu.VMEM((SUBLANES, HALO + tt, CC), F32), pltpu.VMEM((HALO + tt, n_grp * G), F32)],
        comm=comm,
    )


def _out_proj(y_b, x, w_out_b, g_ffn, comm=()):
    T, D = x.shape
    tm = _tile(T, 512)

    def body(y_ref, x_ref, w_ref, g_ref, h1_ref, hn_ref):
        h1 = x_ref[...] + _dot(y_ref[...], w_ref[...], NN)
        h1_ref[...] = h1
        r = lax.rsqrt(jnp.mean(h1 * h1, axis=-1, keepdims=True) + RMS_EPS)
        hn_ref[...] = (h1 * r * g_ref[...]).astype(BF16)

    row = lambda i: (i, 0)
    return _call(
        "out_proj",
        body,
        (T // tm,),
        [
            pl.BlockSpec((tm, y_b.shape[1]), row),
            pl.BlockSpec((tm, D), row),
            pl.BlockSpec(w_out_b.shape, lambda i: (0, 0)),
            pl.BlockSpec((1, D), lambda i: (0, 0)),
        ],
        [pl.BlockSpec((tm, D), row), pl.BlockSpec((tm, D), row)],
        [jax.ShapeDtypeStruct((T, D), F32), jax.ShapeDtypeStruct((T, D), BF16)],
        (y_b, x, w_out_b, g_ffn),
        comm=comm,
    )


def _hidden_tile(F):
    return _tile(F, 1408, LANES)


def _gate_up(hn_b, wgT_b, wuT_b, comm=()):
    T, D = hn_b.shape
    F = wgT_b.shape[0]
    tm, tf = _tile(T, 1024), _hidden_tile(F)

    def body(hn_ref, wg_ref, wu_ref, g_ref, u_ref, a_ref):
        hn = hn_ref[...]
        for c0 in range(0, tf, HIDDEN_CHUNK):
            cs = slice(c0, min(c0 + HIDDEN_CHUNK, tf))
            gv = _dot(hn, wg_ref[cs, :], NT)
            uv = _dot(hn, wu_ref[cs, :], NT)
            g_ref[:, cs] = gv.astype(BF16)
            u_ref[:, cs] = uv.astype(BF16)
            a_ref[:, cs] = (gv * _sigmoid(gv) * uv).astype(BF16)

    wspec = pl.BlockSpec((tf, D), lambda j, i: (j, 0))
    ospec = pl.BlockSpec((tm, tf), lambda j, i: (i, j))
    return _call(
        "gate_up",
        body,
        (F // tf, T // tm),
        [pl.BlockSpec((tm, D), lambda j, i: (i, 0)), wspec, wspec],
        [ospec, ospec, ospec],
        [jax.ShapeDtypeStruct((T, F), BF16)] * 3,
        (hn_b, wgT_b, wuT_b),
        comm=comm,
    )


def _down_loss(a_b, wd_b, h1, target, g_final, comm=()):
    T, D = h1.shape
    F = a_b.shape[1]
    tm = _tile(T, 512)
    nt = T // tm

    def body(a_ref, w_ref, h1_ref, t_ref, g_ref, dh2_ref, dh2b_ref, loss_ref, dg_ref):
        i = pl.program_id(0)
        h2 = h1_ref[...] + _dot(a_ref[...], w_ref[...], NN)
        r = lax.rsqrt(jnp.mean(h2 * h2, axis=-1, keepdims=True) + RMS_EPS)
        g = g_ref[...]
        diff = h2 * r * g - t_ref[...]
        _accumulate(loss_ref, i == 0, jnp.full(loss_ref.shape, jnp.sum(diff * diff) * (0.5 / D), F32))
        dh2, dg_rows = _rms_bwd(h2, g, diff * (1.0 / D))
        dh2_ref[...] = dh2
        dh2b_ref[...] = dh2.astype(BF16)
        _accumulate(dg_ref, i == 0, jnp.sum(dg_rows, axis=0, keepdims=True))

    row = lambda i: (i, 0)
    return _call(
        "down_loss",
        body,
        (nt,),
        [
            pl.BlockSpec((tm, F), row),
            pl.BlockSpec((F, D), lambda i: (0, 0), pipeline_mode=pl.Buffered(1)),
            pl.BlockSpec((tm, D), row),
            pl.BlockSpec((tm, D), row),
            pl.BlockSpec((1, D), lambda i: (0, 0)),
        ],
        [
            pl.BlockSpec((tm, D), row),
            pl.BlockSpec((tm, D), row),
            pl.BlockSpec((1, LANES), lambda i: (0, 0)),
            pl.BlockSpec((1, D), lambda i: (0, 0)),
        ],
        [
            jax.ShapeDtypeStruct((T, D), F32),
            jax.ShapeDtypeStruct((T, D), BF16),
            jax.ShapeDtypeStruct((1, LANES), F32),
            jax.ShapeDtypeStruct((1, D), F32),
        ],
        (a_b, wd_b, h1, target, g_final),
        comm=comm,
    )


def _ffn_bwd_act(dh2_b, wd_b, g_b, u_b, comm=()):
    T, D = dh2_b.shape
    F = wd_b.shape[0]
    tm, tf = _tile(T, 1024), _hidden_tile(F)

    def body(d_ref, w_ref, g_ref, u_ref, dg_ref, du_ref):
        d = d_ref[...]
        for c0 in range(0, tf, HIDDEN_CHUNK):
            cs = slice(c0, min(c0 + HIDDEN_CHUNK, tf))
            da = _dot(d, w_ref[cs, :], NT)
            gv = g_ref[:, cs].astype(F32)
            uv = u_ref[:, cs].astype(F32)
            sg = _sigmoid(gv)
            silu = gv * sg
            dg_ref[:, cs] = (da * uv * (sg * (1.0 + gv * (1.0 - sg)))).astype(BF16)
            du_ref[:, cs] = (da * silu).astype(BF16)

    aspec = pl.BlockSpec((tm, tf), lambda j, i: (i, j))
    return _call(
        "ffn_bwd_act",
        body,
        (F // tf, T // tm),
        [pl.BlockSpec((tm, D), lambda j, i: (i, 0)), pl.BlockSpec((tf, D), lambda j, i: (j, 0)), aspec, aspec],
        [aspec, aspec],
        [jax.ShapeDtypeStruct((T, F), BF16)] * 2,
        (dh2_b, wd_b, g_b, u_b),
        comm=comm,
    )


def _ffn_bwd_in(dg_b, du_b, wgT_b, wuT_b, h1, dh2, g_ffn, w_out_b, comm=()):
    T, D = h1.shape
    F = wgT_b.shape[0]
    DM = w_out_b.shape[0]
    tm = _tile(T, 512)

    def body(dg_ref, du_ref, wg_ref, wu_ref, h1_ref, dh2_ref, g_ref, wo_ref, dh1_ref, dh1b_ref, dy_ref, dgf_ref):
        i = pl.program_id(0)
        dhn = _dot(dg_ref[...], wg_ref[...], NN) + _dot(du_ref[...], wu_ref[...], NN)
        dx, dg_rows = _rms_bwd(h1_ref[...], g_ref[...], dhn)
        dh1 = dh2_ref[...] + dx
        dh1b = dh1.astype(BF16)
        dh1_ref[...] = dh1
        dh1b_ref[...] = dh1b
        dy_ref[...] = _dot(dh1b, wo_ref[...], NT)
        _accumulate(dgf_ref, i == 0, jnp.sum(dg_rows, axis=0, keepdims=True))

    row = lambda i: (i, 0)
    const = lambda i: (0, 0)
    return _call(
        "ffn_bwd_in",
        body,
        (T // tm,),
        [
            pl.BlockSpec((tm, F), row),
            pl.BlockSpec((tm, F), row),
            pl.BlockSpec((F, D), const, pipeline_mode=pl.Buffered(1)),
            pl.BlockSpec((F, D), const, pipeline_mode=pl.Buffered(1)),
            pl.BlockSpec((tm, D), row),
            pl.BlockSpec((tm, D), row),
            pl.BlockSpec((1, D), const),
            pl.BlockSpec((DM, D), const, pipeline_mode=pl.Buffered(1)),
        ],
        [pl.BlockSpec((tm, D), row), pl.BlockSpec((tm, D), row), pl.BlockSpec((tm, DM), row), pl.BlockSpec((1, D), const)],
        [
            jax.ShapeDtypeStruct((T, D), F32),
            jax.ShapeDtypeStruct((T, D), BF16),
            jax.ShapeDtypeStruct((T, DM), F32),
            jax.ShapeDtypeStruct((1, D), F32),
        ],
        (dg_b, du_b, wgT_b, wuT_b, h1, dh2, g_ffn, w_out_b),
        comm=comm,
    )


def _seq_bwd(z, dy, v, w_dw4, ln_g, ln_b, w_pool_b, s_pool, comm=()):
    T, CI = z.shape
    CC = ln_g.shape[1]
    n_grp, G = w_pool_b.shape[0], w_pool_b.shape[-1]
    CP = n_grp * G
    KW = w_dw4.shape[1]
    n_cc = CC // LANES
    D = CC + CP
    tt = _tile(T, 512, HALO)
    per = tt // HALO
    n_tiles = T // tt
    last_halo = T // HALO - 1

    def body(zc_ref, zp_ref, dyc_ref, dyn_ref, vc_ref, vn_ref, wdw_ref, lng_ref, lnb_ref, wp_ref, sp_ref,
             dz_ref, dwdw_ref, dbdw_ref, dlng_ref, dlnb_ref, dwp_ref, dsp_ref, dbin_ref,
             dv_scr, u_scr, p_scr, g_scr, dw_scr):
        i = pl.program_id(0)
        first = i == 0
        last = i == n_tiles - 1
        lng, lnb = lng_ref[...], lnb_ref[...]

        def conv_pre(vv, dyc):
            mu = jnp.mean(vv, axis=-1, keepdims=True)
            d = vv - mu
            rs = lax.rsqrt(jnp.mean(d * d, axis=-1, keepdims=True) + LN_EPS)
            xh = d * rs
            ln = xh * lng + lnb
            sg = _sigmoid(ln)
            dln = dyc * (sg * (1.0 + ln * (1.0 - sg)))
            dxh = dln * lng
            dv = rs * (dxh - jnp.mean(dxh, axis=-1, keepdims=True) - xh * jnp.mean(dxh * xh, axis=-1, keepdims=True))
            return dv, dln, xh

        dv_c, dln_c, xh_c = conv_pre(vc_ref[...], dyc_ref[:, 0:CC])
        dv_scr[0, 0:tt, :] = dv_c
        dv_n, _, _ = conv_pre(vn_ref[...], dyn_ref[:, 0:CC])
        dv_scr[0, tt:, :] = jnp.where(last, 0.0, dv_n)
        _fill_shifted(dv_scr)
        _accumulate(dlng_ref, first, jnp.sum(dln_c * xh_c, axis=0, keepdims=True))
        _accumulate(dlnb_ref, first, jnp.sum(dln_c, axis=0, keepdims=True))
        _accumulate(dbdw_ref, first, jnp.sum(dv_c, axis=0, keepdims=True))

        u_scr[...] = zc_ref[:, 0:CC] * _sigmoid(zc_ref[:, CC : 2 * CC])

        @pl.when(first)
        def _():
            dw_scr[...] = jnp.zeros_like(dw_scr)

        for j in range(n_cc):
            cs = slice(LANES * j, LANES * (j + 1))
            gs = slice(CC + LANES * j, CC + LANES * (j + 1))
            dbin_a = jnp.zeros((1, LANES), F32)
            dbin_g = jnp.zeros((1, LANES), F32)
            for rb in range(tt // CONV_ROWS):
                rows = slice(rb * CONV_ROWS, (rb + 1) * CONV_ROWS)
                u_blk = u_scr[rows, cs]
                du = jnp.zeros((CONV_ROWS, LANES), F32)
                for k in range(KW):
                    off = rb * CONV_ROWS + (KW - 1) - k
                    d = _shifted_rows(dv_scr, off, CONV_ROWS, cs)
                    du = du + d * wdw_ref[j, k : k + 1, :]
                    dw_scr[j * HALO + k] += jnp.sum((u_blk * d).reshape(CONV_ROWS // 8, 8, LANES), axis=0)
                a = zc_ref[rows, cs]
                sg = _sigmoid(zc_ref[rows, gs])
                da = du * sg
                dgate = du * a * sg * (1.0 - sg)
                dz_ref[rows, cs] = da.astype(BF16)
                dz_ref[rows, gs] = dgate.astype(BF16)
                dbin_a = dbin_a + jnp.sum(da, axis=0, keepdims=True)
                dbin_g = dbin_g + jnp.sum(dgate, axis=0, keepdims=True)
            _accumulate(dbin_ref.at[:, cs], first, dbin_a)
            _accumulate(dbin_ref.at[:, gs], first, dbin_g)

        @pl.when(last)
        def _():
            dwdw_ref[...] = jnp.sum(dw_scr[...], axis=1).reshape(dwdw_ref.shape)

        p_scr[0:HALO, :] = jnp.where(first, 0.0, zp_ref[:, 2 * CC :])
        p_scr[HALO:, :] = zc_ref[:, 2 * CC :]
        tpos = i * tt + lax.broadcasted_iota(jnp.int32, (tt, 1), 0)
        for gi, w in enumerate(POOL_WINDOWS):
            cs = slice(G * gi, G * (gi + 1))
            ys = slice(CC + G * gi, CC + G * (gi + 1))
            ps = slice(2 * CC + G * gi, 2 * CC + G * (gi + 1))
            cnt = jnp.minimum(tpos + 1, w).astype(F32)
            yib = _pool_mean_minus_token(p_scr, cs, w, cnt, tt).astype(BF16)
            wp = wp_ref[gi]
            sp = sp_ref[:, cs]
            dyp = dyc_ref[:, ys]
            q = _dot(yib, wp, NN)
            _accumulate(dsp_ref.at[:, cs], first, jnp.sum(dyp * q, axis=0, keepdims=True))
            dq_c = (dyp * sp).astype(BF16)
            dq_n = (jnp.where(last, 0.0, dyn_ref[:, ys]) * sp).astype(BF16)
            _accumulate(dwp_ref.at[gi], first, _dot(yib, dq_c, TN))
            dyi_c = _dot(dq_c, wp, NT)
            g_scr[0:tt, cs] = dyi_c / cnt
            g_scr[tt:, cs] = _dot(dq_n, wp, NT) * (1.0 / w)
            dp = -dyi_c
            for d in range(w):
                dp = dp + g_scr[d : d + tt, cs]
            dz_ref[:, ps] = dp.astype(BF16)
            _accumulate(dbin_ref.at[:, ps], first, jnp.sum(dp, axis=0, keepdims=True))

    cur = lambda i: (i, 0)
    prev = lambda i: (jnp.maximum(i * per - 1, 0), 0)
    nxt = lambda i: (jnp.minimum((i + 1) * per, last_halo), 0)
    c2 = lambda i: (0, 0)
    c3 = lambda i: (0, 0, 0)
    return _call(
        "seq_bwd",
        body,
        (n_tiles,),
        [
            pl.BlockSpec((tt, CI), cur),
            pl.BlockSpec((HALO, CI), prev),
            pl.BlockSpec((tt, D), cur),
            pl.BlockSpec((HALO, D), nxt),
            pl.BlockSpec((tt, CC), cur),
            pl.BlockSpec((HALO, CC), nxt),
            pl.BlockSpec(w_dw4.shape, c3),
            pl.BlockSpec((1, CC), c2),
            pl.BlockSpec((1, CC), c2),
            pl.BlockSpec(w_pool_b.shape, c3),
            pl.BlockSpec((1, CP), c2),
        ],
        [
            pl.BlockSpec((tt, CI), cur),
            pl.BlockSpec((n_cc, HALO, LANES), c3),
            pl.BlockSpec((1, CC), c2),
            pl.BlockSpec((1, CC), c2),
            pl.BlockSpec((1, CC), c2),
            pl.BlockSpec((n_grp, G, G), c3),
            pl.BlockSpec((1, CP), c2),
            pl.BlockSpec((1, CI), c2),
        ],
        [
            jax.ShapeDtypeStruct((T, CI), BF16),
            jax.ShapeDtypeStruct((n_cc, HALO, LANES), F32),
            jax.ShapeDtypeStruct((1, CC), F32),
            jax.ShapeDtypeStruct((1, CC), F32),
            jax.ShapeDtypeStruct((1, CC), F32),
            jax.ShapeDtypeStruct((n_grp, G, G), F32),
            jax.ShapeDtypeStruct((1, CP), F32),
            jax.ShapeDtypeStruct((1, CI), F32),
        ],
        (z, z, dy, dy, v, v, w_dw4, ln_g, ln_b, w_pool_b, s_pool),
        scratch=[
            pltpu.VMEM((SUBLANES, tt + HALO, CC), F32),
            pltpu.VMEM((tt, CC), F32),
            pltpu.VMEM((HALO + tt, CP), F32),
            pltpu.VMEM((tt + HALO, CP), F32),
            pltpu.VMEM((n_cc * HALO, 8, LANES), F32),
        ],
        comm=comm,
    )


def _in_proj_bwd(dz_b, w_inT_b, x, dh1, g_mix, comm=()):
    T, D = x.shape
    CI = w_inT_b.shape[0]
    tm = _tile(T, 512)

    def body(dz_ref, w_ref, x_ref, dh1_ref, g_ref, dx_ref, dg_ref):
        i = pl.program_id(0)
        dxn = _dot(dz_ref[...], w_ref[...], NN)
        dx, dg_rows = _rms_bwd(x_ref[...], g_ref[...], dxn)
        dx_ref[...] = dh1_ref[...] + dx
        _accumulate(dg_ref, i == 0, jnp.sum(dg_rows, axis=0, keepdims=True))

    row = lambda i: (i, 0)
    const = lambda i: (0, 0)
    return _call(
        "in_proj_bwd",
        body,
        (T // tm,),
        [
            pl.BlockSpec((tm, CI), row),
            pl.BlockSpec((CI, D), const),
            pl.BlockSpec((tm, D), row),
            pl.BlockSpec((tm, D), row),
            pl.BlockSpec((1, D), const),
        ],
        [pl.BlockSpec((tm, D), row), pl.BlockSpec((1, D), const)],
        [jax.ShapeDtypeStruct((T, D), F32), jax.ShapeDtypeStruct((1, D), F32)],
        (dz_b, w_inT_b, x, dh1, g_mix),
        comm=comm,
    )


def _weight_grad(name, a_b, b_b, comm=()):
    T, N1 = a_b.shape
    N2 = b_b.shape[1]
    t1 = _tile(N1, 1408, LANES)
    tk = _tile(T, 2048)
    nk = T // tk

    def body(a_ref, b_ref, o_ref, acc):
        k = pl.program_id(1)
        _accumulate(acc, k == 0, _dot(a_ref[...], b_ref[...], TN))

        @pl.when(k == nk - 1)
        def _():
            o_ref[...] = acc[...].astype(BF16)

    (out,), rest = _call(
        name,
        body,
        (N1 // t1, nk),
        [pl.BlockSpec((tk, t1), lambda n, k: (k, n)), pl.BlockSpec((tk, N2), lambda n, k: (k, 0))],
        [pl.BlockSpec((t1, N2), lambda n, k: (n, 0))],
        [jax.ShapeDtypeStruct((N1, N2), BF16)],
        (a_b, b_b),
        scratch=[pltpu.VMEM((t1, N2), F32)],
        comm=comm,
    )
    return out, rest


def _sum_parts(name, full, how, parts, me):
    _, R, C = parts[0].shape
    tr = _tile(R, 512)
    nb = R // tr
    where = [(q, r) for q, p in enumerate(parts) for r in range(p.shape[0])]
    assert len(where) == 3

    def body(me_ref, own_ref, *refs):
        o_ref = refs[-1]
        f = lambda j: refs[where[j][0]][where[j][1]].astype(F32)
        o_ref[...] = (own_ref[...].astype(F32) + f(0)) + (f(1) + f(2))

    own_map = {"rows": lambda i, me_ref: (me_ref[0] * nb + i, 0), "cols": lambda i, me_ref: (i, me_ref[0]),
               "all": lambda i, me_ref: (i, 0)}[how]
    return pl.pallas_call(
        body,
        name=name,
        grid_spec=pltpu.PrefetchScalarGridSpec(
            num_scalar_prefetch=1,
            grid=(nb,),
            in_specs=[pl.BlockSpec((tr, C), own_map)]
            + [pl.BlockSpec((p.shape[0], tr, C), lambda i, me_ref: (0, i, 0)) for p in parts],
            out_specs=pl.BlockSpec((tr, C), lambda i, me_ref: (i, 0)),
        ),
        out_shape=jax.ShapeDtypeStruct((R, C), F32),
        compiler_params=pltpu.CompilerParams(dimension_semantics=("arbitrary",), vmem_limit_bytes=VMEM_LIMIT),
    )(me, full, *parts)


_M_CORR = 1.0 - ADAM_B1**ADAM_STEP
_V_CORR = 1.0 - ADAM_B2**ADAM_STEP


def _adamw_math(w, g, m, v):
    m = ADAM_B1 * m + (1.0 - ADAM_B1) * g
    v = ADAM_B2 * v + (1.0 - ADAM_B2) * (g * g)
    delta = -ADAM_LR * ((m / _M_CORR) / (jnp.sqrt(v / _V_CORR) + ADAM_EPS) + ADAM_WD * w)
    return delta, m, v


def _adamw(name, w, m, v, g_here, g_there, g_transposed=False, comm=()):
    R, C = w.shape
    tr = _tile(R, 256, LANES if g_transposed else 8)

    def body(w_ref, m_ref, v_ref, ga_ref, gb_ref, g_ref, d_ref, nm_ref, nv_ref):
        g = ga_ref[...] + gb_ref[...]
        if g_transposed:
            g = g.T
        g_ref[...] = g
        d_ref[...], nm_ref[...], nv_ref[...] = _adamw_math(w_ref[...], g, m_ref[...], v_ref[...])

    spec = pl.BlockSpec((tr, C), lambda i: (i, 0))
    gspec = pl.BlockSpec((C, tr), lambda i: (0, i)) if g_transposed else spec
    return _call(name, body, (R // tr,), [spec] * 3 + [gspec] * 2, [spec] * 4, [jax.ShapeDtypeStruct((R, C), F32)] * 4,
                 (w, m, v, g_here, g_there), comm=comm)


def _adamw_on_sparsecore(name, w, m, v, g_here, g_there):
    R, C = w.shape
    n_groups = R // SUBLANES
    n_turns = -(-n_groups // SC_TILES)
    n_in, n_out = 5, 4

    def body(w_hbm, m_hbm, v_hbm, ga_hbm, gb_hbm, g_out, d_out, nm_out, nv_out, bufs, sems):
        tile = lax.axis_index("subcore") * SC_CORES + lax.axis_index("sparsecore")
        srcs = (w_hbm, m_hbm, v_hbm, ga_hbm, gb_hbm)
        dsts = (d_out, nm_out, nv_out, g_out)

        def rows(turn):
            return pl.ds((tile + turn * SC_TILES) * SUBLANES, SUBLANES)

        def loads(turn):
            slot = turn % 2
            return [pltpu.make_async_copy(srcs[q].at[rows(turn), :], bufs.at[slot, q], sems.at[slot, q]) for q in range(n_in)]

        def stores(turn):
            slot = turn % 2
            return [pltpu.make_async_copy(bufs.at[slot, q], dsts[q].at[rows(turn), :], sems.at[slot, n_in + q])
                    for q in range(n_out)]

        def when_mine(turn, fn):
            pl.when(tile + turn * SC_TILES < n_groups)(fn)

        def compute(slot):
            wb, mb, vb, gab, gbb = (bufs.at[slot, q] for q in range(n_in))

            @pl.loop(0, SUBLANES)
            def _(r):
                @pl.loop(0, C, step=SC_LANES)
                def _(i):
                    at = (r, pl.ds(i, SC_LANES))
                    g = gab[at] + gbb[at]
                    delta, new_m, new_v = _adamw_math(wb[at], g, mb[at], vb[at])
                    gab[at], wb[at], mb[at], vb[at] = g, delta, new_m, new_v

        def start_loads(turn):
            def fn():
                for cp in loads(turn):
                    cp.start()

            when_mine(turn, fn)

        start_loads(0)
        for turn in range(n_turns):
            def step(turn=turn):
                for cp in loads(turn):
                    cp.wait()
                if turn >= 1:
                    for cp in stores(turn - 1):
                        cp.wait()
                if turn + 1 < n_turns:
                    start_loads(turn + 1)
                compute(turn % 2)
                for cp in stores(turn):
                    cp.start()

            when_mine(turn, step)
        for turn in range(n_turns):
            def drain(turn=turn):
                for cp in stores(turn):
                    cp.wait()

            last_mine = jnp.logical_and(tile + turn * SC_TILES < n_groups, tile + (turn + 1) * SC_TILES >= n_groups)
            pl.when(last_mine)(drain)

    return pl.kernel(
        body,
        name=name,
        out_type=[jax.ShapeDtypeStruct((R, C), F32)] * 4,
        mesh=plsc.VectorSubcoreMesh(core_axis_name="sparsecore", subcore_axis_name="subcore"),
        scratch_types=[pltpu.VMEM((2, n_in, SUBLANES, C), F32), pltpu.SemaphoreType.DMA((2, n_in + n_out))],
        compiler_params=pltpu.CompilerParams(use_tc_tiling_on_sc=True),
    )(w, m, v, g_here, g_there)


class _PackLayout:
    def __init__(self, n_cc, n_grp, G, widths):
        self.dw_rows = (0, HALO)
        self.wp_rows = (HALO, HALO + G)
        self.n_cc, self.n_grp, self.G = n_cc, n_grp, G
        self.vec = {}
        r = HALO + G
        for name, width in widths:
            self.vec[name] = (r, width)
            r += width // PACK_W
        self.rows = -(-r // 8) * 8


def _pack_small(layout, dwdw, dwp, vecs):
    names = list(vecs)

    def body(*refs):
        dw_ref, wp_ref = refs[0], refs[1]
        vec_refs = refs[2 : 2 + len(names)]
        o_ref = refs[-1]
        o_ref[...] = jnp.zeros_like(o_ref)
        for j in range(layout.n_cc):
            o_ref[layout.dw_rows[0] : layout.dw_rows[1], j * LANES : (j + 1) * LANES] = dw_ref[j]
        for i in range(layout.n_grp):
            o_ref[layout.wp_rows[0] : layout.wp_rows[1], i * layout.G : (i + 1) * layout.G] = wp_ref[i]
        for name, ref in zip(names, vec_refs):
            r, width = layout.vec[name]
            for h in range(width // PACK_W):
                o_ref[r + h : r + h + 1, :] = ref[:, h * PACK_W : (h + 1) * PACK_W]

    return pl.pallas_call(
        body,
        name="pack_small",
        out_shape=jax.ShapeDtypeStruct((layout.rows, PACK_W), F32),
    )(dwdw, dwp, *[vecs[k] for k in names])


def _adamw_small(layout, g_here, g_there, w_dw, m_dw, v_dw, w_pool, m_pool, v_pool, vec_w, vec_m, vec_v):
    names = list(vec_w)
    nv = len(names)

    def body(*refs):
        ga_ref, gb_ref = refs[0], refs[1]
        wdw, mdw, vdw, wp, mp, vp = refs[2:8]
        vw, vm, vv = refs[8 : 8 + nv], refs[8 + nv : 8 + 2 * nv], refs[8 + 2 * nv : 8 + 3 * nv]
        outs = refs[8 + 3 * nv :]
        acc = outs[-1]
        acc[...] = ga_ref[...] + gb_ref[...]

        def emit(o, g, w, m, v, idx=()):
            res = (g,) + _adamw_math(w, g, m, v)
            for ref, val in zip(o, res):
                ref[idx] = val

        me = 2 * lax.axis_index("x") + lax.axis_index("y")
        for j in range(layout.n_cc):

            @pl.when(me == j)
            def _(j=j):
                g = acc[layout.dw_rows[0] : layout.dw_rows[1], j * LANES : (j + 1) * LANES]
                emit(outs[0:4], g, wdw[...], mdw[...], vdw[...], idx=...)

        for i in range(layout.n_grp):
            g = acc[layout.wp_rows[0] : layout.wp_rows[1], i * layout.G : (i + 1) * layout.G]
            emit(outs[4:8], g, wp[i], mp[i], vp[i], idx=i)
        for q, name in enumerate(names):
            r, width = layout.vec[name]
            for h in range(width // PACK_W):
                ls = slice(h * PACK_W, (h + 1) * PACK_W)
                g = acc[r + h : r + h + 1, :]
                emit(outs[8 + 4 * q : 12 + 4 * q], g, vw[q][:, ls], vm[q][:, ls], vv[q][:, ls], idx=(slice(None), ls))

    shapes = [w_dw.shape] * 4 + [w_pool.shape] * 4
    for name in names:
        shapes += [vec_w[name].shape] * 4
    return pl.pallas_call(
        body,
        name="adamw_small",
        out_shape=[jax.ShapeDtypeStruct(s, F32) for s in shapes],
        scratch_shapes=[pltpu.VMEM(g_here.shape, F32)],
    )(g_here, g_there, w_dw, m_dw, v_dw, w_pool, m_pool, v_pool,
      *[vec_w[k] for k in names], *[vec_m[k] for k in names], *[vec_v[k] for k in names])


def _allreduce_adamw_row(g_part, w, m, v, loss_part, comm=()):
    D = w.shape[1]
    n_pairs = N_DEV - 1

    def body(g_ref, w_ref, m_ref, v_ref, l_ref, go_ref, d_ref, nm_ref, nv_ref, lo_ref, land_g, land_l, sems):
        x, y, c = _place()
        copies = []
        for q, (src, land) in enumerate(((g_ref, land_g), (l_ref, land_l))):
            for r in range(1, N_DEV):
                fx, fy, fc = (r >> 2) & 1, (r >> 1) & 1, r & 1
                peer = (1 - x if fx else x, 1 - y if fy else y, 1 - c if fc else c)
                cp = _remote(src, land.at[r], sems, 2 * (q * n_pairs + r - 1), peer)
                cp.start()
                copies.append(cp)
        for cp in copies:
            cp.wait()

        def total(src, land):
            row = lambda r: src[...] if r == 0 else land[r]
            return ((row(0) + row(4)) + (row(2) + row(6))) + ((row(1) + row(5)) + (row(3) + row(7)))

        g = total(g_ref, land_g)
        go_ref[...] = g
        d_ref[...], nm_ref[...], nv_ref[...] = _adamw_math(w_ref[...], g, m_ref[...], v_ref[...])
        lo_ref[...] = total(l_ref, land_l)

    vm = pl.BlockSpec(memory_space=pltpu.VMEM)
    return _call(
        "allreduce_adamw_g_mix",
        body,
        (),
        [vm] * 5,
        [vm] * 5,
        [jax.ShapeDtypeStruct((1, D), F32)] * 4 + [jax.ShapeDtypeStruct(loss_part.shape, F32)],
        (g_part, w, m, v, loss_part),
        scratch=[pltpu.VMEM((N_DEV, 1, D), F32), pltpu.VMEM((N_DEV,) + loss_part.shape, F32),
                 pltpu.SemaphoreType.DMA((4 * n_pairs,))],
        comm=comm,
    )


def kernel(x, g_mix, w_in, b_in, w_dw, b_dw, ln_g, ln_b, w_pool, s_pool, w_out, g_ffn, w_gate, w_up, w_down, g_final, loss_target, m_g_mix, m_w_in, m_b_in, m_w_dw, m_b_dw, m_ln_g, m_ln_b, m_w_pool, m_s_pool, m_w_out, m_g_ffn, m_w_gate, m_w_up, m_w_down, m_g_final, v_g_mix, v_w_in, v_b_in, v_w_dw, v_b_dw, v_ln_g, v_ln_b, v_w_pool, v_s_pool, v_w_out, v_g_ffn, v_w_gate, v_w_up, v_w_down, v_g_final):
    x2 = x[0]
    target = loss_target[0]
    T, D = x2.shape
    w_in2, w_out2, w_down2, w_dw2 = w_in[0], w_out[0], w_down[0], w_dw[0]
    w_gateT, w_upT = w_gate[0].T, w_up[0].T
    CI = w_in2.shape[1] * N_CHIPS
    DM = w_out2.shape[0] * N_CHIPS
    F = w_down2.shape[0] * N_CHIPS
    KW, dw_cols = w_dw2.shape
    assert dw_cols == LANES
    n_grp, G = w_pool.shape[1], w_pool.shape[-1]
    w_pool3 = w_pool[0]
    g_final2 = g_final.reshape(1, D)

    me = (2 * lax.axis_index("x") + lax.axis_index("y")).astype(jnp.int32).reshape(1)

    w_inT_b, w_dw4, f_out, f_gate, f_up, f_down = _place_and_gather(
        [(w_in2, "rows", (CI, D), BF16, True, True), (w_dw2, "lead", (N_CHIPS, KW, dw_cols), F32, False, False)],
        [(w, "rows", shape, BF16, False, True)
         for w, shape in ((w_out2, (DM, D)), (w_gateT, (F, D)), (w_upT, (F, D)), (w_down2, (F, D)))])
    w_pool_b = w_pool3.astype(BF16)
    (z, xn_b), (f_out, f_gate) = _in_proj(
        x2, g_mix, w_inT_b, b_in,
        comm=[_GatherIci([f_out], ["rows"], [True]), _GatherIci([f_gate], ["rows"], [True], which=(2,))])
    (y_b, v), (w_out_b, f_gate, f_up) = _seq_fwd(
        z, w_dw4, b_dw, ln_g, ln_b, w_pool_b, s_pool,
        comm=[_GatherD2d([f_out], ["rows"]), _GatherIci([f_gate], ["rows"], [True], which=(0, 1)),
              _GatherIci([f_up], ["rows"], [True])])
    (h1, hn_b), (wgT_b, wuT_b, f_down) = _out_proj(
        y_b, x2, w_out_b, g_ffn,
        comm=[_GatherD2d([f_gate, f_up], ["rows"] * 2), _GatherIci([f_down], ["rows"], [True])])
    (g_b, u_b, a_b), (wd_b,) = _gate_up(hn_b, wgT_b, wuT_b, comm=[_GatherD2d([f_down], ["rows"])])
    (dh2, dh2_b, loss_part, d_g_final), _ = _down_loss(a_b, wd_b, h1, target, g_final2)

    gw_down, _ = _weight_grad("grad_w_down", a_b, dh2_b)
    (dg_b, du_b), (p_down_xy,) = _ffn_bwd_act(dh2_b, wd_b, g_b, u_b, comm=[_Scatter([gw_down], ["rows"], which=(0, 1))])
    gw_gateT, (p_down_d,) = _weight_grad("grad_w_gate", dg_b, hn_b, comm=[_Scatter([gw_down], ["rows"], which=(2,))])
    gw_upT, _ = _weight_grad("grad_w_up", du_b, hn_b)
    sum_down = _sum_parts("sum_w_down", gw_down, "rows", [p_down_xy, p_down_d], me)
    (dh1, dh1_b, dy, d_g_ffn), (p_gate, oth_down) = _ffn_bwd_in(
        dg_b, du_b, wgT_b, wuT_b, h1, dh2, g_ffn, w_out_b, comm=[_Scatter([gw_gateT], ["rows"]), _Swap([sum_down])])
    gw_out, _ = _weight_grad("grad_w_out", y_b, dh1_b)
    sum_gate = _sum_parts("sum_w_gate", gw_gateT, "rows", [p_gate], me)
    res = {}
    res["w_down"] = _adamw_on_sparsecore("adamw_w_down", w_down2, m_w_down[0], v_w_down[0], sum_down, oth_down)
    (dz_b, d_wdw, d_bdw, d_lng, d_lnb, d_wp, d_sp, d_bin), (p_up, p_out, oth_gate) = _seq_bwd(
        z, dy, v, w_dw4, ln_g, ln_b, w_pool_b, s_pool,
        comm=[_Scatter([gw_upT, gw_out], ["rows", "rows"]), _Swap([sum_gate])])
    vec_grads = {"b_dw": d_bdw, "ln_g": d_lng, "ln_b": d_lnb, "s_pool": d_sp, "g_ffn": d_g_ffn, "g_final": d_g_final, "b_in": d_bin}
    layout = _PackLayout(dw_cols * N_CHIPS // LANES, n_grp, G, [(k, a.shape[1]) for k, a in vec_grads.items()])
    pack = _pack_small(layout, d_wdw, d_wp, vec_grads)
    sum_up = _sum_parts("sum_w_up", gw_upT, "rows", [p_up], me)
    sum_out = _sum_parts("sum_w_out", gw_out, "rows", [p_out], me)
    gw_inT, (p_small, oth_up, oth_out) = _weight_grad(
        "grad_w_in", dz_b, xn_b, comm=[_Scatter([pack], ["all"]), _Swap([sum_up, sum_out])])
    sum_small = _sum_parts("sum_small", pack, "all", [p_small], me)
    res["w_gate"] = _adamw_on_sparsecore("adamw_w_gate", w_gateT, m_w_gate[0].T, v_w_gate[0].T, sum_gate, oth_gate)
    (grad_x, d_g_mix), (p_in, oth_small) = _in_proj_bwd(
        dz_b, w_inT_b, x2, dh1, g_mix, comm=[_Scatter([gw_inT], ["rows"]), _Swap([sum_small])])
    res["w_up"] = _adamw_on_sparsecore("adamw_w_up", w_upT, m_w_up[0].T, v_w_up[0].T, sum_up, oth_up)
    res["w_out"] = _adamw_on_sparsecore("adamw_w_out", w_out2, m_w_out[0], v_w_out[0], sum_out, oth_out)
    sum_in = _sum_parts("sum_w_in", gw_inT, "rows", [p_in], me)
    (*res["g_mix"], loss_row), (oth_in,) = _allreduce_adamw_row(
        d_g_mix, g_mix, m_g_mix, v_g_mix, loss_part, comm=[_Swap([sum_in])])
    loss = loss_row[0, 0]
    res["w_in"], _ = _adamw("adamw_w_in", w_in2, m_w_in[0], v_w_in[0], sum_in, oth_in, g_transposed=True)

    pad_dw = lambda a: jnp.pad(a[0], ((0, HALO - KW), (0, 0)))
    vec_w = {"b_dw": b_dw, "ln_g": ln_g, "ln_b": ln_b, "s_pool": s_pool, "g_ffn": g_ffn, "g_final": g_final2, "b_in": b_in}
    vec_m = {"b_dw": m_b_dw, "ln_g": m_ln_g, "ln_b": m_ln_b, "s_pool": m_s_pool, "g_ffn": m_g_ffn,
             "g_final": m_g_final.reshape(1, D), "b_in": m_b_in}
    vec_v = {"b_dw": v_b_dw, "ln_g": v_ln_g, "ln_b": v_ln_b, "s_pool": v_s_pool, "g_ffn": v_g_ffn,
             "g_final": v_g_final.reshape(1, D), "b_in": v_b_in}
    small = _adamw_small(layout, sum_small, oth_small, pad_dw(w_dw), pad_dw(m_w_dw), pad_dw(v_w_dw),
                         w_pool3, m_w_pool[0], v_w_pool[0], vec_w, vec_m, vec_v)
    res["w_dw"] = [a[:KW][None] for a in small[0:4]]
    res["w_pool"] = [a[None] for a in small[4:8]]
    for q, k in enumerate(vec_w):
        res[k] = list(small[8 + 4 * q : 12 + 4 * q])
    res["g_final"] = [a.reshape(D) for a in res["g_final"]]
    for k in ("w_in", "w_out", "w_down"):
        res[k] = [a[None] for a in res[k]]
    for k in ("w_gate", "w_up"):
        res[k] = [a.T[None] for a in res[k]]

    order = ["g_mix", "w_in", "b_in", "w_dw", "b_dw", "ln_g", "ln_b", "w_pool", "s_pool", "w_out", "g_ffn", "w_gate", "w_up", "w_down", "g_final"]
    outs = [loss, grad_x[None]]
    for q in range(4):
        outs += [res[k][q] for k in order]
    return tuple(outs)
```

```python
import jax
import jax.numpy as jnp
from jax import lax
from jax.experimental import pallas as pl
from jax.experimental.pallas import tpu as pltpu
from jax.experimental.pallas import tpu_sc as plsc

F32 = jnp.float32
BF16 = jnp.bfloat16
MESH = pl.DeviceIdType.MESH
ANY = pl.BlockSpec(memory_space=pl.ANY)

RMS_EPS = 1e-6
LN_EPS = 1e-5
POOL_WINDOWS = (2, 4, 8, 16)
ADAM_LR = 0.001
ADAM_B1 = 0.9
ADAM_B2 = 0.999
ADAM_EPS = 1e-08
ADAM_WD = 0.01
ADAM_STEP = 10

LANES = 128
SUBLANES = 8
HALO = 32
CONV_ROWS = 64
HIDDEN_CHUNK = 512
VMEM_LIMIT = 56 * 1024 * 1024
PACK_W = 512
N_CHIPS = 4
N_DEV = 8
SC_CORES = 2
SC_TILES = 32
SC_LANES = 16


def _tile(n, want, mult=8):
    t = min(n, want)
    while n % t or t % mult:
        t -= 1
    return t


def _sigmoid(x):
    return 1.0 / (1.0 + jnp.exp(-x))


def _dot(a, b, dims):
    return lax.dot_general(a, b, (dims, ((), ())), preferred_element_type=F32)


NN = ((1,), (0,))
NT = ((1,), (1,))
TN = ((0,), (0,))


def _rms_bwd(x, g, dy):
    r = lax.rsqrt(jnp.mean(x * x, axis=-1, keepdims=True) + RMS_EPS)
    xh = x * r
    gy = dy * g
    dx = r * (gy - xh * jnp.mean(gy * xh, axis=-1, keepdims=True))
    return dx, dy * xh


def _accumulate(ref, first, val):
    @pl.when(first)
    def _():
        ref[...] = val

    @pl.when(jnp.logical_not(first))
    def _():
        ref[...] += val


def _place():
    return lax.axis_index("x"), lax.axis_index("y"), lax.axis_index("c")


def _other_chips(x, y):
    return [(1 - x, y), (x, 1 - y), (1 - x, 1 - y)]


def _rows(ref, start, n):
    return ref.at[pl.ds(pl.multiple_of(start, 16), n)]


def _window(ref, how, k, c=None):
    if how == "all":
        return ref
    if how == "lead":
        return ref.at[k]
    if how == "rows":
        n = ref.shape[0] // N_CHIPS
        if c is None:
            return _rows(ref, k * n, n)
        return _rows(ref, k * n + c * (n // 2), n // 2)
    n = ref.shape[1] // N_CHIPS
    cols = pl.ds(pl.multiple_of(k * n, LANES), n)
    if c is None:
        return ref.at[:, cols]
    h = ref.shape[0] // 2
    return ref.at[pl.ds(pl.multiple_of(c * h, 16), h), cols]


def _remote(src, dst, sems, s, device):
    return pltpu.make_async_remote_copy(
        src_ref=src, dst_ref=dst, send_sem=sems.at[s], recv_sem=sems.at[s + 1], device_id=device, device_id_type=MESH)


class _GatherIci:
    aliased = True

    def __init__(self, fulls, hows, splits, which=(0, 1, 2)):
        self.fulls, self.hows, self.splits, self.which = list(fulls), list(hows), list(splits), tuple(which)

    def inputs(self):
        return self.fulls

    def out_shapes(self):
        return [jax.ShapeDtypeStruct(a.shape, a.dtype) for a in self.fulls]

    def n_sems(self):
        return 6 * len(self.fulls)

    def build(self, ins, outs, sems, base):
        x, y, c = _place()
        me = 2 * x + y
        chips = _other_chips(x, y)
        starts, waits = [], []
        for a, (how, sp) in enumerate(zip(self.hows, self.splits)):
            half = c if sp else None
            mine = _window(outs[a], how, me, half)
            for j in self.which:
                px, py = chips[j]
                s = base + 6 * a + 2 * j
                cp = _remote(mine, mine, sems, s, (px, py, c))
                landing = _remote(mine, _window(outs[a], how, 2 * px + py, half), sems, s, (px, py, c))
                starts.append(cp.start)
                waits += [landing.wait_recv, cp.wait_send]
        return starts, waits


class _GatherD2d:
    aliased = True

    def __init__(self, fulls, hows):
        self.fulls, self.hows = list(fulls), list(hows)

    def inputs(self):
        return self.fulls

    def out_shapes(self):
        return [jax.ShapeDtypeStruct(a.shape, a.dtype) for a in self.fulls]

    def n_sems(self):
        return 6 * len(self.fulls)

    def build(self, ins, outs, sems, base):
        x, y, c = _place()
        starts, waits = [], []
        for a, how in enumerate(self.hows):
            for j, (px, py) in enumerate(_other_chips(x, y)):
                s = base + 6 * a + 2 * j
                got = _window(outs[a], how, 2 * px + py, c)
                cp = _remote(got, got, sems, s, (x, y, 1 - c))
                landing = _remote(got, _window(outs[a], how, 2 * px + py, 1 - c), sems, s, (x, y, 1 - c))
                starts.append(cp.start)
                waits += [landing.wait_recv, cp.wait_send]
        return starts, waits


def _part_shape(a, how):
    if how == "all":
        return a.shape
    if how == "rows":
        return (a.shape[0] // N_CHIPS, a.shape[1])
    return (a.shape[0], a.shape[1] // N_CHIPS)


class _Scatter:
    aliased = False

    def __init__(self, fulls, hows, which=(0, 1, 2)):
        self.fulls, self.hows, self.which = list(fulls), list(hows), tuple(which)

    def inputs(self):
        return self.fulls

    def out_shapes(self):
        return [jax.ShapeDtypeStruct((len(self.which),) + _part_shape(a, h), a.dtype) for a, h in zip(self.fulls, self.hows)]

    def n_sems(self):
        return 6 * len(self.fulls)

    def build(self, ins, outs, sems, base):
        x, y, c = _place()
        chips = _other_chips(x, y)
        starts, waits = [], []
        for a, how in enumerate(self.hows):
            for slot, j in enumerate(self.which):
                px, py = chips[j]
                cp = _remote(_window(ins[a], how, 2 * px + py), outs[a].at[slot], sems, base + 6 * a + 2 * j, (px, py, c))
                starts.append(cp.start)
                waits += [cp.wait_recv, cp.wait_send]
        return starts, waits


class _Swap:
    aliased = False

    def __init__(self, arrays):
        self.arrays = list(arrays)

    def inputs(self):
        return self.arrays

    def out_shapes(self):
        return [jax.ShapeDtypeStruct(a.shape, a.dtype) for a in self.arrays]

    def n_sems(self):
        return 2 * len(self.arrays)

    def build(self, ins, outs, sems, base):
        x, y, c = _place()
        starts, waits = [], []
        for a in range(len(ins)):
            cp = _remote(ins[a], outs[a], sems, base + 2 * a, (x, y, 1 - c))
            starts.append(cp.start)
            waits += [cp.wait_recv, cp.wait_send]
        return starts, waits


def _call(name, body, grid, in_specs, out_specs, out_shape, args, scratch=(), comm=()):
    comm = list(comm)
    n_in, n_out, n_scr = len(args), len(out_shape), len(scratch)
    c_in = [a for op in comm for a in op.inputs()]
    c_out = [s for op in comm for s in op.out_shapes()]
    n_sems = sum(op.n_sems() for op in comm)
    aliases, i_in, i_out = {}, 0, 0
    for op in comm:
        if op.aliased:
            for q in range(len(op.inputs())):
                aliases[n_in + i_in + q] = n_out + i_out + q
        i_in, i_out = i_in + len(op.inputs()), i_out + len(op.out_shapes())

    def wrapped(*refs):
        ins = refs[:n_in]
        cin = refs[n_in : n_in + len(c_in)]
        o0 = n_in + len(c_in)
        outs = refs[o0 : o0 + n_out]
        cout = refs[o0 + n_out : o0 + n_out + len(c_out)]
        s0 = o0 + n_out + len(c_out)
        scr = refs[s0 : s0 + n_scr]

        def copies():
            sems = refs[s0 + n_scr]
            starts, waits = [], []
            i_in = i_out = base = 0
            for op in comm:
                ni, no = len(op.inputs()), len(op.out_shapes())
                s, w = op.build(cin[i_in : i_in + ni], cout[i_out : i_out + no], sems, base)
                starts += s
                waits += w
                i_in, i_out, base = i_in + ni, i_out + no, base + op.n_sems()
            return starts, waits

        def run_starts():
            for start in copies()[0]:
                start()

        def run_waits():
            for wait in copies()[1]:
                wait()

        if comm and grid:
            first = last = True
            for d, n in enumerate(grid):
                first = jnp.logical_and(first, pl.program_id(d) == 0)
                last = jnp.logical_and(last, pl.program_id(d) == n - 1)
            pl.when(first)(run_starts)
        elif comm:
            run_starts()
        if body is not None:
            body(*ins, *outs, *scr)
        if comm and grid:
            pl.when(last)(run_waits)
        elif comm:
            run_waits()

    res = pl.pallas_call(
        wrapped,
        name=name,
        grid=grid,
        in_specs=list(in_specs) + [ANY] * len(c_in),
        out_specs=list(out_specs) + [ANY] * len(c_out),
        out_shape=list(out_shape) + c_out,
        scratch_shapes=list(scratch) + ([pltpu.SemaphoreType.DMA((n_sems,))] if comm else []),
        input_output_aliases=aliases,
        compiler_params=pltpu.CompilerParams(dimension_semantics=("arbitrary",) * len(grid), vmem_limit_bytes=VMEM_LIMIT),
    )(*args, *c_in)
    return tuple(res[:n_out]), tuple(res[n_out:])


def _place_and_gather(now, later):
    items = list(now) + list(later)
    n, n_now = len(items), len(now)
    buf_shape = lambda it: it[0].shape[::-1] if it[4] else it[0].shape
    split_now = [a for a in range(n_now) if items[a][5]]

    def body(*refs):
        ins, outs = refs[:n], refs[n : 2 * n]
        stage, bufs = refs[2 * n : 3 * n - n_now], refs[3 * n - n_now : 4 * n - n_now]
        sems = refs[4 * n - n_now]
        x, y, c = _place()
        me = 2 * x + y
        chips = _other_chips(x, y)
        loads = [pltpu.make_async_copy(ins[a], stage[a - n_now], sems.at[a]) for a in range(n_now, n)]
        for ld in loads:
            ld.start()
        pending = []

        def place(a, val):
            _, how, _, dtype, transposed, _ = items[a]
            bufs[a][...] = (val.T if transposed else val).astype(dtype)
            cp = pltpu.make_async_copy(bufs[a], _window(outs[a], how, me), sems.at[n + a])
            cp.start()
            pending.append(cp.wait)

        arrivals = []
        for a in range(n_now):
            place(a, ins[a][...])
            how, split = items[a][1], items[a][5]
            half = c if split else None
            src = _rows(bufs[a], c * (bufs[a].shape[0] // 2), bufs[a].shape[0] // 2) if split else bufs[a]
            for j, (px, py) in enumerate(chips):
                s = 2 * n + 6 * a + 2 * j
                cp = _remote(src, _window(outs[a], how, me, half), sems, s, (px, py, c))
                landing = _remote(src, _window(outs[a], how, 2 * px + py, half), sems, s, (px, py, c))
                cp.start()
                arrivals.append(landing.wait_recv)
                pending.append(cp.wait_send)
        for a in range(n_now, n):
            loads[a - n_now].wait()
            place(a, stage[a - n_now][...])
        for wait in arrivals:
            wait()
        d2d = _GatherD2d([None] * len(split_now), [items[a][1] for a in split_now])
        starts, waits = d2d.build(None, [outs[a] for a in split_now], sems, 2 * n + 6 * n_now)
        for start in starts:
            start()
        for wait in waits + pending:
            wait()

    vm = pl.BlockSpec(memory_space=pltpu.VMEM)
    return pl.pallas_call(
        body,
        name="place_and_gather",
        in_specs=[vm] * n_now + [ANY] * (n - n_now),
        out_specs=[ANY] * n,
        out_shape=[jax.ShapeDtypeStruct(it[2], it[3]) for it in items],
        scratch_shapes=[pltpu.VMEM(it[0].shape, it[0].dtype) for it in later]
        + [pltpu.VMEM(buf_shape(it), it[3]) for it in items]
        + [pltpu.SemaphoreType.DMA((2 * n + 6 * n_now + 6 * len(split_now),))],
        compiler_params=pltpu.CompilerParams(vmem_limit_bytes=VMEM_LIMIT),
    )(*[it[0] for it in items])


def _in_proj(x, g_mix, w_inT_b, b_in, comm=()):
    T, D = x.shape
    CI = w_inT_b.shape[0]
    tm = _tile(T, 512)

    def body(x_ref, g_ref, w_ref, b_ref, z_ref, xn_ref):
        xv = x_ref[...]
        r = lax.rsqrt(jnp.mean(xv * xv, axis=-1, keepdims=True) + RMS_EPS)
        xn = (xv * r * g_ref[...]).astype(BF16)
        xn_ref[...] = xn
        z_ref[...] = _dot(xn, w_ref[...], NT) + b_ref[...]

    return _call(
        "in_proj",
        body,
        (T // tm,),
        [
            pl.BlockSpec((tm, D), lambda i: (i, 0)),
            pl.BlockSpec((1, D), lambda i: (0, 0)),
            pl.BlockSpec((CI, D), lambda i: (0, 0)),
            pl.BlockSpec((1, CI), lambda i: (0, 0)),
        ],
        [pl.BlockSpec((tm, CI), lambda i: (i, 0)), pl.BlockSpec((tm, D), lambda i: (i, 0))],
        [jax.ShapeDtypeStruct((T, CI), F32), jax.ShapeDtypeStruct((T, D), BF16)],
        (x, g_mix, w_inT_b, b_in),
        comm=comm,
    )


def _fill_shifted(scr):
    n = scr.shape[1] - SUBLANES
    for s in range(1, SUBLANES):
        scr[s, 0:n, :] = scr[0, s : s + n, :]


def _shifted_rows(scr, off, n, cs):
    s = off % SUBLANES
    return scr[s, off - s : off - s + n, cs]


def _pool_mean_minus_token(p_scr, cs, w, cnt, tt):
    tok = p_scr[HALO : HALO + tt, cs]
    s = tok
    for d in range(1, w):
        s = s + p_scr[HALO - d : HALO - d + tt, cs]
    return s / cnt - tok


def _seq_fwd(z, w_dw4, b_dw, ln_g, ln_b, w_pool_b, s_pool, comm=()):
    T, CI = z.shape
    CC = ln_g.shape[1]
    n_grp, G = w_pool_b.shape[0], w_pool_b.shape[-1]
    KW = w_dw4.shape[1]
    D = CC + n_grp * G
    tt = _tile(T, 512, HALO)
    per = tt // HALO

    def body(zc_ref, zp_ref, wdw_ref, bdw_ref, lng_ref, lnb_ref, wp_ref, sp_ref, y_ref, v_ref, u_scr, p_scr):
        i = pl.program_id(0)
        first = i == 0
        u_prev = zp_ref[:, 0:CC] * _sigmoid(zp_ref[:, CC : 2 * CC])
        u_scr[0, 0:HALO, :] = jnp.where(first, 0.0, u_prev)
        p_scr[0:HALO, :] = jnp.where(first, 0.0, zp_ref[:, 2 * CC :])
        u_scr[0, HALO:, :] = zc_ref[:, 0:CC] * _sigmoid(zc_ref[:, CC : 2 * CC])
        p_scr[HALO:, :] = zc_ref[:, 2 * CC :]
        _fill_shifted(u_scr)

        for j in range(CC // LANES):
            cs = slice(LANES * j, LANES * (j + 1))
            for rb in range(tt // CONV_ROWS):
                acc = jnp.zeros((CONV_ROWS, LANES), F32)
                for k in range(KW):
                    off = HALO - (KW - 1) + k + rb * CONV_ROWS
                    acc = acc + _shifted_rows(u_scr, off, CONV_ROWS, cs) * wdw_ref[j, k]
                v_ref[rb * CONV_ROWS : (rb + 1) * CONV_ROWS, cs] = acc + bdw_ref[:, cs]

        v = v_ref[...]
        mu = jnp.mean(v, axis=-1, keepdims=True)
        d = v - mu
        var = jnp.mean(d * d, axis=-1, keepdims=True)
        ln = d * lax.rsqrt(var + LN_EPS) * lng_ref[...] + lnb_ref[...]
        y_ref[:, 0:CC] = (ln * _sigmoid(ln)).astype(BF16)

        tpos = i * tt + lax.broadcasted_iota(jnp.int32, (tt, 1), 0)
        for gi, w in enumerate(POOL_WINDOWS):
            cs = slice(G * gi, G * (gi + 1))
            cnt = jnp.minimum(tpos + 1, w).astype(F32)
            yi = _pool_mean_minus_token(p_scr, cs, w, cnt, tt)
            q = _dot(yi.astype(BF16), wp_ref[gi], NN)
            y_ref[:, CC + G * gi : CC + G * (gi + 1)] = (q * sp_ref[:, cs]).astype(BF16)

    const2 = lambda i: (0, 0)
    return _call(
        "seq_fwd",
        body,
        (T // tt,),
        [
            pl.BlockSpec((tt, CI), lambda i: (i, 0)),
            pl.BlockSpec((HALO, CI), lambda i: (jnp.maximum(i * per - 1, 0), 0)),
            pl.BlockSpec(w_dw4.shape, lambda i: (0,) * w_dw4.ndim),
            pl.BlockSpec((1, CC), const2),
            pl.BlockSpec((1, CC), const2),
            pl.BlockSpec((1, CC), const2),
            pl.BlockSpec(w_pool_b.shape, lambda i: (0, 0, 0)),
            pl.BlockSpec((1, n_grp * G), const2),
        ],
        [pl.BlockSpec((tt, D), lambda i: (i, 0)), pl.BlockSpec((tt, CC), lambda i: (i, 0))],
        [jax.ShapeDtypeStruct((T, D), BF16), jax.ShapeDtypeStruct((T, CC), F32)],
        (z, z, w_dw4, b_dw, ln_g, ln_b, w_pool_b, s_pool),
        scratch=[pltpu.VMEM((SUBLANES, HALO + tt, CC), F32), pltpu.VMEM((HALO + tt, n_grp * G), F32)],
        comm=comm,
    )


def _out_proj(y_b, x, w_out_b, g_ffn, comm=()):
    T, D = x.shape
    tm = _tile(T, 512)

    def body(y_ref, x_ref, w_ref, g_ref, h1_ref, hn_ref):
        h1 = x_ref[...] + _dot(y_ref[...], w_ref[...], NN)
        h1_ref[...] = h1
        r = lax.rsqrt(jnp.mean(h1 * h1, axis=-1, keepdims=True) + RMS_EPS)
        hn_ref[...] = (h1 * r * g_ref[...]).astype(BF16)

    row = lambda i: (i, 0)
    return _call(
        "out_proj",
        body,
        (T // tm,),
        [
            pl.BlockSpec((tm, y_b.shape[1]), row),
            pl.BlockSpec((tm, D), row),
            pl.BlockSpec(w_out_b.shape, lambda i: (0, 0)),
            pl.BlockSpec((1, D), lambda i: (0, 0)),
        ],
        [pl.BlockSpec((tm, D), row), pl.BlockSpec((tm, D), row)],
        [jax.ShapeDtypeStruct((T, D), F32), jax.ShapeDtypeStruct((T, D), BF16)],
        (y_b, x, w_out_b, g_ffn),
        comm=comm,
    )


def _hidden_tile(F):
    return _tile(F, 1408, LANES)


def _gate_up(hn_b, wgT_b, wuT_b, comm=()):
    T, D = hn_b.shape
    F = wgT_b.shape[0]
    tm, tf = _tile(T, 1024), _hidden_tile(F)

    def body(hn_ref, wg_ref, wu_ref, g_ref, u_ref, a_ref):
        hn = hn_ref[...]
        for c0 in range(0, tf, HIDDEN_CHUNK):
            cs = slice(c0, min(c0 + HIDDEN_CHUNK, tf))
            gv = _dot(hn, wg_ref[cs, :], NT)
            uv = _dot(hn, wu_ref[cs, :], NT)
            g_ref[:, cs] = gv.astype(BF16)
            u_ref[:, cs] = uv.astype(BF16)
            a_ref[:, cs] = (gv * _sigmoid(gv) * uv).astype(BF16)

    wspec = pl.BlockSpec((tf, D), lambda j, i: (j, 0))
    ospec = pl.BlockSpec((tm, tf), lambda j, i: (i, j))
    return _call(
        "gate_up",
        body,
        (F // tf, T // tm),
        [pl.BlockSpec((tm, D), lambda j, i: (i, 0)), wspec, wspec],
        [ospec, ospec, ospec],
        [jax.ShapeDtypeStruct((T, F), BF16)] * 3,
        (hn_b, wgT_b, wuT_b),
        comm=comm,
    )


def _down_loss(a_b, wd_b, h1, target, g_final, comm=()):
    T, D = h1.shape
    F = a_b.shape[1]
    tm = _tile(T, 512)
    nt = T // tm

    def body(a_ref, w_ref, h1_ref, t_ref, g_ref, dh2_ref, dh2b_ref, loss_ref, dg_ref):
        i = pl.program_id(0)
        h2 = h1_ref[...] + _dot(a_ref[...], w_ref[...], NN)
        r = lax.rsqrt(jnp.mean(h2 * h2, axis=-1, keepdims=True) + RMS_EPS)
        g = g_ref[...]
        diff = h2 * r * g - t_ref[...]
        _accumulate(loss_ref, i == 0, jnp.full(loss_ref.shape, jnp.sum(diff * diff) * (0.5 / D), F32))
        dh2, dg_rows = _rms_bwd(h2, g, diff * (1.0 / D))
        dh2_ref[...] = dh2
        dh2b_ref[...] = dh2.astype(BF16)
        _accumulate(dg_ref, i == 0, jnp.sum(dg_rows, axis=0, keepdims=True))

    row = lambda i: (i, 0)
    return _call(
        "down_loss",
        body,
        (nt,),
        [
            pl.BlockSpec((tm, F), row),
            pl.BlockSpec((F, D), lambda i: (0, 0), pipeline_mode=pl.Buffered(1)),
            pl.BlockSpec((tm, D), row),
            pl.BlockSpec((tm, D), row),
            pl.BlockSpec((1, D), lambda i: (0, 0)),
        ],
        [
            pl.BlockSpec((tm, D), row),
            pl.BlockSpec((tm, D), row),
            pl.BlockSpec((1, LANES), lambda i: (0, 0)),
            pl.BlockSpec((1, D), lambda i: (0, 0)),
        ],
        [
            jax.ShapeDtypeStruct((T, D), F32),
            jax.ShapeDtypeStruct((T, D), BF16),
            jax.ShapeDtypeStruct((1, LANES), F32),
            jax.ShapeDtypeStruct((1, D), F32),
        ],
        (a_b, wd_b, h1, target, g_final),
        comm=comm,
    )


def _ffn_bwd_act(dh2_b, wd_b, g_b, u_b, comm=()):
    T, D = dh2_b.shape
    F = wd_b.shape[0]
    tm, tf = _tile(T, 1024), _hidden_tile(F)

    def body(d_ref, w_ref, g_ref, u_ref, dg_ref, du_ref):
        d = d_ref[...]
        for c0 in range(0, tf, HIDDEN_CHUNK):
            cs = slice(c0, min(c0 + HIDDEN_CHUNK, tf))
            da = _dot(d, w_ref[cs, :], NT)
            gv = g_ref[:, cs].astype(F32)
            uv = u_ref[:, cs].astype(F32)
            sg = _sigmoid(gv)
            silu = gv * sg
            dg_ref[:, cs] = (da * uv * (sg * (1.0 + gv * (1.0 - sg)))).astype(BF16)
            du_ref[:, cs] = (da * silu).astype(BF16)

    aspec = pl.BlockSpec((tm, tf), lambda j, i: (i, j))
    return _call(
        "ffn_bwd_act",
        body,
        (F // tf, T // tm),
        [pl.BlockSpec((tm, D), lambda j, i: (i, 0)), pl.BlockSpec((tf, D), lambda j, i: (j, 0)), aspec, aspec],
        [aspec, aspec],
        [jax.ShapeDtypeStruct((T, F), BF16)] * 2,
        (dh2_b, wd_b, g_b, u_b),
        comm=comm,
    )


def _ffn_bwd_in(dg_b, du_b, wgT_b, wuT_b, h1, dh2, g_ffn, w_out_b, comm=()):
    T, D = h1.shape
    F = wgT_b.shape[0]
    DM = w_out_b.shape[0]
    tm = _tile(T, 512)

    def body(dg_ref, du_ref, wg_ref, wu_ref, h1_ref, dh2_ref, g_ref, wo_ref, dh1_ref, dh1b_ref, dy_ref, dgf_ref):
        i = pl.program_id(0)
        dhn = _dot(dg_ref[...], wg_ref[...], NN) + _dot(du_ref[...], wu_ref[...], NN)
        dx, dg_rows = _rms_bwd(h1_ref[...], g_ref[...], dhn)
        dh1 = dh2_ref[...] + dx
        dh1b = dh1.astype(BF16)
        dh1_ref[...] = dh1
        dh1b_ref[...] = dh1b
        dy_ref[...] = _dot(dh1b, wo_ref[...], NT)
        _accumulate(dgf_ref, i == 0, jnp.sum(dg_rows, axis=0, keepdims=True))

    row = lambda i: (i, 0)
    const = lambda i: (0, 0)
    return _call(
        "ffn_bwd_in",
        body,
        (T // tm,),
        [
            pl.BlockSpec((tm, F), row),
            pl.BlockSpec((tm, F), row),
            pl.BlockSpec((F, D), const, pipeline_mode=pl.Buffered(1)),
            pl.BlockSpec((F, D), const, pipeline_mode=pl.Buffered(1)),
            pl.BlockSpec((tm, D), row),
            pl.BlockSpec((tm, D), row),
            pl.BlockSpec((1, D), const),
            pl.BlockSpec((DM, D), const, pipeline_mode=pl.Buffered(1)),
        ],
        [pl.BlockSpec((tm, D), row), pl.BlockSpec((tm, D), row), pl.BlockSpec((tm, DM), row), pl.BlockSpec((1, D), const)],
        [
            jax.ShapeDtypeStruct((T, D), F32),
            jax.ShapeDtypeStruct((T, D), BF16),
            jax.ShapeDtypeStruct((T, DM), F32),
            jax.ShapeDtypeStruct((1, D), F32),
        ],
        (dg_b, du_b, wgT_b, wuT_b, h1, dh2, g_ffn, w_out_b),
        comm=comm,
    )


def _seq_bwd(z, dy, v, w_dw4, ln_g, ln_b, w_pool_b, s_pool, comm=()):
    T, CI = z.shape
    CC = ln_g.shape[1]
    n_grp, G = w_pool_b.shape[0], w_pool_b.shape[-1]
    CP = n_grp * G
    KW = w_dw4.shape[1]
    n_cc = CC // LANES
    D = CC + CP
    tt = _tile(T, 512, HALO)
    per = tt // HALO
    n_tiles = T // tt
    last_halo = T // HALO - 1

    def body(zc_ref, zp_ref, dyc_ref, dyn_ref, vc_ref, vn_ref, wdw_ref, lng_ref, lnb_ref, wp_ref, sp_ref,
             dz_ref, dwdw_ref, dbdw_ref, dlng_ref, dlnb_ref, dwp_ref, dsp_ref, dbin_ref,
             dv_scr, u_scr, p_scr, g_scr, dw_scr):
        i = pl.program_id(0)
        first = i == 0
        last = i == n_tiles - 1
        lng, lnb = lng_ref[...], lnb_ref[...]

        def conv_pre(vv, dyc):
            mu = jnp.mean(vv, axis=-1, keepdims=True)
            d = vv - mu
            rs = lax.rsqrt(jnp.mean(d * d, axis=-1, keepdims=True) + LN_EPS)
            xh = d * rs
            ln = xh * lng + lnb
            sg = _sigmoid(ln)
            dln = dyc * (sg * (1.0 + ln * (1.0 - sg)))
            dxh = dln * lng
            dv = rs * (dxh - jnp.mean(dxh, axis=-1, keepdims=True) - xh * jnp.mean(dxh * xh, axis=-1, keepdims=True))
            return dv, dln, xh

        dv_c, dln_c, xh_c = conv_pre(vc_ref[...], dyc_ref[:, 0:CC])
        dv_scr[0, 0:tt, :] = dv_c
        dv_n, _, _ = conv_pre(vn_ref[...], dyn_ref[:, 0:CC])
        dv_scr[0, tt:, :] = jnp.where(last, 0.0, dv_n)
        _fill_shifted(dv_scr)
        _accumulate(dlng_ref, first, jnp.sum(dln_c * xh_c, axis=0, keepdims=True))
        _accumulate(dlnb_ref, first, jnp.sum(dln_c, axis=0, keepdims=True))
        _accumulate(dbdw_ref, first, jnp.sum(dv_c, axis=0, keepdims=True))

        u_scr[...] = zc_ref[:, 0:CC] * _sigmoid(zc_ref[:, CC : 2 * CC])

        @pl.when(first)
        def _():
            dw_scr[...] = jnp.zeros_like(dw_scr)

        for j in range(n_cc):
            cs = slice(LANES * j, LANES * (j + 1))
            gs = slice(CC + LANES * j, CC + LANES * (j + 1))
            dbin_a = jnp.zeros((1, LANES), F32)
            dbin_g = jnp.zeros((1, LANES), F32)
            for rb in range(tt // CONV_ROWS):
                rows = slice(rb * CONV_ROWS, (rb + 1) * CONV_ROWS)
                u_blk = u_scr[rows, cs]
                du = jnp.zeros((CONV_ROWS, LANES), F32)
                for k in range(KW):
                    off = rb * CONV_ROWS + (KW - 1) - k
                    d = _shifted_rows(dv_scr, off, CONV_ROWS, cs)
                    du = du + d * wdw_ref[j, k]
                    dw_scr[j * HALO + k] += jnp.sum((u_blk * d).reshape(CONV_ROWS // 8, 8, LANES), axis=0)
                a = zc_ref[rows, cs]
                sg = _sigmoid(zc_ref[rows, gs])
                da = du * sg
                dgate = du * a * sg * (1.0 - sg)
                dz_ref[rows, cs] = da.astype(BF16)
                dz_ref[rows, gs] = dgate.astype(BF16)
                dbin_a = dbin_a + jnp.sum(da, axis=0, keepdims=True)
                dbin_g = dbin_g + jnp.sum(dgate, axis=0, keepdims=True)
            _accumulate(dbin_ref.at[:, cs], first, dbin_a)
            _accumulate(dbin_ref.at[:, gs], first, dbin_g)

        @pl.when(last)
        def _():
            dwdw_ref[...] = jnp.sum(dw_scr[...], axis=1).reshape(dwdw_ref.shape)

        p_scr[0:HALO, :] = jnp.where(first, 0.0, zp_ref[:, 2 * CC :])
        p_scr[HALO:, :] = zc_ref[:, 2 * CC :]
        tpos = i * tt + lax.broadcasted_iota(jnp.int32, (tt, 1), 0)
        for gi, w in enumerate(POOL_WINDOWS):
            cs = slice(G * gi, G * (gi + 1))
            ys = slice(CC + G * gi, CC + G * (gi + 1))
            ps = slice(2 * CC + G * gi, 2 * CC + G * (gi + 1))
            cnt = jnp.minimum(tpos + 1, w).astype(F32)
            yib = _pool_mean_minus_token(p_scr, cs, w, cnt, tt).astype(BF16)
            wp = wp_ref[gi]
            sp = sp_ref[:, cs]
            dyp = dyc_ref[:, ys]
            q = _dot(yib, wp, NN)
            _accumulate(dsp_ref.at[:, cs], first, jnp.sum(dyp * q, axis=0, keepdims=True))
            dq_c = (dyp * sp).astype(BF16)
            dq_n = (jnp.where(last, 0.0, dyn_ref[:, ys]) * sp).astype(BF16)
            _accumulate(dwp_ref.at[gi], first, _dot(yib, dq_c, TN))
            dyi_c = _dot(dq_c, wp, NT)
            g_scr[0:tt, cs] = dyi_c / cnt
            g_scr[tt:, cs] = _dot(dq_n, wp, NT) * (1.0 / w)
            dp = -dyi_c
            for d in range(w):
                dp = dp + g_scr[d : d + tt, cs]
            dz_ref[:, ps] = dp.astype(BF16)
            _accumulate(dbin_ref.at[:, ps], first, jnp.sum(dp, axis=0, keepdims=True))

    cur = lambda i: (i, 0)
    prev = lambda i: (jnp.maximum(i * per - 1, 0), 0)
    nxt = lambda i: (jnp.minimum((i + 1) * per, last_halo), 0)
    c2 = lambda i: (0, 0)
    c3 = lambda i: (0, 0, 0)
    return _call(
        "seq_bwd",
        body,
        (n_tiles,),
        [
            pl.BlockSpec((tt, CI), cur),
            pl.BlockSpec((HALO, CI), prev),
            pl.BlockSpec((tt, D), cur),
            pl.BlockSpec((HALO, D), nxt),
            pl.BlockSpec((tt, CC), cur),
            pl.BlockSpec((HALO, CC), nxt),
            pl.BlockSpec(w_dw4.shape, lambda i: (0,) * w_dw4.ndim),
            pl.BlockSpec((1, CC), c2),
            pl.BlockSpec((1, CC), c2),
            pl.BlockSpec(w_pool_b.shape, c3),
            pl.BlockSpec((1, CP), c2),
        ],
        [
            pl.BlockSpec((tt, CI), cur),
            pl.BlockSpec((n_cc, HALO, LANES), c3),
            pl.BlockSpec((1, CC), c2),
            pl.BlockSpec((1, CC), c2),
            pl.BlockSpec((1, CC), c2),
            pl.BlockSpec((n_grp, G, G), c3),
            pl.BlockSpec((1, CP), c2),
            pl.BlockSpec((1, CI), c2),
        ],
        [
            jax.ShapeDtypeStruct((T, CI), BF16),
            jax.ShapeDtypeStruct((n_cc, HALO, LANES), F32),
            jax.ShapeDtypeStruct((1, CC), F32),
            jax.ShapeDtypeStruct((1, CC), F32),
            jax.ShapeDtypeStruct((1, CC), F32),
            jax.ShapeDtypeStruct((n_grp, G, G), F32),
            jax.ShapeDtypeStruct((1, CP), F32),
            jax.ShapeDtypeStruct((1, CI), F32),
        ],
        (z, z, dy, dy, v, v, w_dw4, ln_g, ln_b, w_pool_b, s_pool),
        scratch=[
            pltpu.VMEM((SUBLANES, tt + HALO, CC), F32),
            pltpu.VMEM((tt, CC), F32),
            pltpu.VMEM((HALO + tt, CP), F32),
            pltpu.VMEM((tt + HALO, CP), F32),
            pltpu.VMEM((n_cc * HALO, 8, LANES), F32),
        ],
        comm=comm,
    )


def _in_proj_bwd(dz_b, w_inT_b, x, dh1, g_mix, comm=()):
    T, D = x.shape
    CI = w_inT_b.shape[0]
    tm = _tile(T, 512)

    def body(dz_ref, w_ref, x_ref, dh1_ref, g_ref, dx_ref, dg_ref):
        i = pl.program_id(0)
        dxn = _dot(dz_ref[...], w_ref[...], NN)
        dx, dg_rows = _rms_bwd(x_ref[...], g_ref[...], dxn)
        dx_ref[...] = dh1_ref[...] + dx
        _accumulate(dg_ref, i == 0, jnp.sum(dg_rows, axis=0, keepdims=True))

    row = lambda i: (i, 0)
    const = lambda i: (0, 0)
    return _call(
        "in_proj_bwd",
        body,
        (T // tm,),
        [
            pl.BlockSpec((tm, CI), row),
            pl.BlockSpec((CI, D), const),
            pl.BlockSpec((tm, D), row),
            pl.BlockSpec((tm, D), row),
            pl.BlockSpec((1, D), const),
        ],
        [pl.BlockSpec((tm, D), row), pl.BlockSpec((1, D), const)],
        [jax.ShapeDtypeStruct((T, D), F32), jax.ShapeDtypeStruct((1, D), F32)],
        (dz_b, w_inT_b, x, dh1, g_mix),
        comm=comm,
    )


def _weight_grad(name, a_b, b_b, comm=()):
    T, N1 = a_b.shape
    N2 = b_b.shape[1]
    t1 = _tile(N1, 1408, LANES)
    tk = _tile(T, 2048)
    nk = T // tk

    def body(a_ref, b_ref, o_ref, acc):
        k = pl.program_id(1)
        _accumulate(acc, k == 0, _dot(a_ref[...], b_ref[...], TN))

        @pl.when(k == nk - 1)
        def _():
            o_ref[...] = acc[...].astype(BF16)

    (out,), rest = _call(
        name,
        body,
        (N1 // t1, nk),
        [pl.BlockSpec((tk, t1), lambda n, k: (k, n)), pl.BlockSpec((tk, N2), lambda n, k: (k, 0))],
        [pl.BlockSpec((t1, N2), lambda n, k: (n, 0))],
        [jax.ShapeDtypeStruct((N1, N2), BF16)],
        (a_b, b_b),
        scratch=[pltpu.VMEM((t1, N2), F32)],
        comm=comm,
    )
    return out, rest


def _sum_parts(name, full, how, parts, me):
    _, R, C = parts[0].shape
    tr = _tile(R, 512)
    nb = R // tr
    where = [(q, r) for q, p in enumerate(parts) for r in range(p.shape[0])]
    assert len(where) == 3

    def body(me_ref, own_ref, *refs):
        o_ref = refs[-1]
        f = lambda j: refs[where[j][0]][where[j][1]].astype(F32)
        o_ref[...] = (own_ref[...].astype(F32) + f(0)) + (f(1) + f(2))

    own_map = {"rows": lambda i, me_ref: (me_ref[0] * nb + i, 0), "cols": lambda i, me_ref: (i, me_ref[0]),
               "all": lambda i, me_ref: (i, 0)}[how]
    return pl.pallas_call(
        body,
        name=name,
        grid_spec=pltpu.PrefetchScalarGridSpec(
            num_scalar_prefetch=1,
            grid=(nb,),
            in_specs=[pl.BlockSpec((tr, C), own_map)]
            + [pl.BlockSpec((p.shape[0], tr, C), lambda i, me_ref: (0, i, 0)) for p in parts],
            out_specs=pl.BlockSpec((tr, C), lambda i, me_ref: (i, 0)),
        ),
        out_shape=jax.ShapeDtypeStruct((R, C), F32),
        compiler_params=pltpu.CompilerParams(dimension_semantics=("arbitrary",), vmem_limit_bytes=VMEM_LIMIT),
    )(me, full, *parts)


_M_CORR = 1.0 - ADAM_B1**ADAM_STEP
_V_CORR = 1.0 - ADAM_B2**ADAM_STEP


def _adamw_math(w, g, m, v):
    m = ADAM_B1 * m + (1.0 - ADAM_B1) * g
    v = ADAM_B2 * v + (1.0 - ADAM_B2) * (g * g)
    delta = -ADAM_LR * ((m / _M_CORR) / (jnp.sqrt(v / _V_CORR) + ADAM_EPS) + ADAM_WD * w)
    return delta, m, v


def _adamw(name, w, m, v, g_here, g_there, g_transposed=False, comm=()):
    R, C = w.shape
    tr = _tile(R, 256, LANES if g_transposed else 8)

    def body(w_ref, m_ref, v_ref, ga_ref, gb_ref, g_ref, d_ref, nm_ref, nv_ref):
        g = ga_ref[...] + gb_ref[...]
        if g_transposed:
            g = g.T
        g_ref[...] = g
        d_ref[...], nm_ref[...], nv_ref[...] = _adamw_math(w_ref[...], g, m_ref[...], v_ref[...])

    spec = pl.BlockSpec((tr, C), lambda i: (i, 0))
    gspec = pl.BlockSpec((C, tr), lambda i: (0, i)) if g_transposed else spec
    return _call(name, body, (R // tr,), [spec] * 3 + [gspec] * 2, [spec] * 4, [jax.ShapeDtypeStruct((R, C), F32)] * 4,
                 (w, m, v, g_here, g_there), comm=comm)


def _adamw_on_sparsecore(name, w, m, v, g_here, g_there):
    R, C = w.shape
    n_groups = R // SUBLANES
    n_turns = -(-n_groups // SC_TILES)
    n_in, n_out = 5, 4

    def body(w_hbm, m_hbm, v_hbm, ga_hbm, gb_hbm, g_out, d_out, nm_out, nv_out, bufs, sems):
        tile = lax.axis_index("subcore") * SC_CORES + lax.axis_index("sparsecore")
        srcs = (w_hbm, m_hbm, v_hbm, ga_hbm, gb_hbm)
        dsts = (d_out, nm_out, nv_out, g_out)

        def rows(turn):
            return pl.ds((tile + turn * SC_TILES) * SUBLANES, SUBLANES)

        def loads(turn):
            slot = turn % 2
            return [pltpu.make_async_copy(srcs[q].at[rows(turn), :], bufs.at[slot, q], sems.at[slot, q]) for q in range(n_in)]

        def stores(turn):
            slot = turn % 2
            return [pltpu.make_async_copy(bufs.at[slot, q], dsts[q].at[rows(turn), :], sems.at[slot, n_in + q])
                    for q in range(n_out)]

        def when_mine(turn, fn):
            pl.when(tile + turn * SC_TILES < n_groups)(fn)

        def compute(slot):
            wb, mb, vb, gab, gbb = (bufs.at[slot, q] for q in range(n_in))

            @pl.loop(0, SUBLANES)
            def _(r):
                @pl.loop(0, C, step=SC_LANES)
                def _(i):
                    at = (r, pl.ds(i, SC_LANES))
                    g = gab[at] + gbb[at]
                    delta, new_m, new_v = _adamw_math(wb[at], g, mb[at], vb[at])
                    gab[at], wb[at], mb[at], vb[at] = g, delta, new_m, new_v

        def start_loads(turn):
            def fn():
                for cp in loads(turn):
                    cp.start()

            when_mine(turn, fn)

        start_loads(0)
        for turn in range(n_turns):
            def step(turn=turn):
                for cp in loads(turn):
                    cp.wait()
                if turn >= 1:
                    for cp in stores(turn - 1):
                        cp.wait()
                if turn + 1 < n_turns:
                    start_loads(turn + 1)
                compute(turn % 2)
                for cp in stores(turn):
                    cp.start()

            when_mine(turn, step)
        for turn in range(n_turns):
            def drain(turn=turn):
                for cp in stores(turn):
                    cp.wait()

            last_mine = jnp.logical_and(tile + turn * SC_TILES < n_groups, tile + (turn + 1) * SC_TILES >= n_groups)
            pl.when(last_mine)(drain)

    return pl.kernel(
        body,
        name=name,
        out_type=[jax.ShapeDtypeStruct((R, C), F32)] * 4,
        mesh=plsc.VectorSubcoreMesh(core_axis_name="sparsecore", subcore_axis_name="subcore"),
        scratch_types=[pltpu.VMEM((2, n_in, SUBLANES, C), F32), pltpu.SemaphoreType.DMA((2, n_in + n_out))],
        compiler_params=pltpu.CompilerParams(use_tc_tiling_on_sc=True),
    )(w, m, v, g_here, g_there)


class _PackLayout:
    def __init__(self, n_cc, n_grp, G, widths):
        self.dw_rows = (0, HALO)
        self.wp_rows = (HALO, HALO + G)
        self.n_cc, self.n_grp, self.G = n_cc, n_grp, G
        self.vec = {}
        r = HALO + G
        for name, width in widths:
            self.vec[name] = (r, width)
            r += width // PACK_W
        self.rows = -(-r // 8) * 8


def _pack_small(layout, dwdw, dwp, vecs):
    names = list(vecs)

    def body(*refs):
        dw_ref, wp_ref = refs[0], refs[1]
        vec_refs = refs[2 : 2 + len(names)]
        o_ref = refs[-1]
        o_ref[...] = jnp.zeros_like(o_ref)
        for j in range(layout.n_cc):
            o_ref[layout.dw_rows[0] : layout.dw_rows[1], j * LANES : (j + 1) * LANES] = dw_ref[j]
        for i in range(layout.n_grp):
            o_ref[layout.wp_rows[0] : layout.wp_rows[1], i * layout.G : (i + 1) * layout.G] = wp_ref[i]
        for name, ref in zip(names, vec_refs):
            r, width = layout.vec[name]
            for h in range(width // PACK_W):
                o_ref[r + h : r + h + 1, :] = ref[:, h * PACK_W : (h + 1) * PACK_W]

    return pl.pallas_call(
        body,
        name="pack_small",
        out_shape=jax.ShapeDtypeStruct((layout.rows, PACK_W), F32),
    )(dwdw, dwp, *[vecs[k] for k in names])


def _adamw_small(layout, g_here, g_there, w_dw, m_dw, v_dw, w_pool, m_pool, v_pool, vec_w, vec_m, vec_v):
    names = list(vec_w)
    nv = len(names)

    def body(*refs):
        ga_ref, gb_ref = refs[0], refs[1]
        wdw, mdw, vdw, wp, mp, vp = refs[2:8]
        vw, vm, vv = refs[8 : 8 + nv], refs[8 + nv : 8 + 2 * nv], refs[8 + 2 * nv : 8 + 3 * nv]
        outs = refs[8 + 3 * nv :]
        acc = outs[-1]
        acc[...] = ga_ref[...] + gb_ref[...]

        def emit(o, g, w, m, v, idx=()):
            res = (g,) + _adamw_math(w, g, m, v)
            for ref, val in zip(o, res):
                ref[idx] = val

        me = 2 * lax.axis_index("x") + lax.axis_index("y")
        for j in range(layout.n_cc):

            @pl.when(me == j)
            def _(j=j):
                for k in range(wdw.shape[0]):
                    g = acc[layout.dw_rows[0] + k : layout.dw_rows[0] + k + 1, j * LANES : (j + 1) * LANES]
                    emit(outs[0:4], g, wdw[k], mdw[k], vdw[k], idx=k)

        for i in range(layout.n_grp):
            g = acc[layout.wp_rows[0] : layout.wp_rows[1], i * layout.G : (i + 1) * layout.G]
            emit(outs[4:8], g, wp[i], mp[i], vp[i], idx=i)
        for q, name in enumerate(names):
            r, width = layout.vec[name]
            for h in range(width // PACK_W):
                ls = slice(h * PACK_W, (h + 1) * PACK_W)
                g = acc[r + h : r + h + 1, :]
                emit(outs[8 + 4 * q : 12 + 4 * q], g, vw[q][:, ls], vm[q][:, ls], vv[q][:, ls], idx=(slice(None), ls))

    shapes = [w_dw.shape] * 4 + [w_pool.shape] * 4
    for name in names:
        shapes += [vec_w[name].shape] * 4
    return pl.pallas_call(
        body,
        name="adamw_small",
        out_shape=[jax.ShapeDtypeStruct(s, F32) for s in shapes],
        scratch_shapes=[pltpu.VMEM(g_here.shape, F32)],
    )(g_here, g_there, w_dw, m_dw, v_dw, w_pool, m_pool, v_pool,
      *[vec_w[k] for k in names], *[vec_m[k] for k in names], *[vec_v[k] for k in names])


def _allreduce_adamw_row(g_part, w, m, v, loss_part, comm=()):
    D = w.shape[1]
    n_pairs = N_DEV - 1

    def body(g_ref, w_ref, m_ref, v_ref, l_ref, go_ref, d_ref, nm_ref, nv_ref, lo_ref, land_g, land_l, sems):
        x, y, c = _place()
        copies = []
        for q, (src, land) in enumerate(((g_ref, land_g), (l_ref, land_l))):
            for r in range(1, N_DEV):
                fx, fy, fc = (r >> 2) & 1, (r >> 1) & 1, r & 1
                peer = (1 - x if fx else x, 1 - y if fy else y, 1 - c if fc else c)
                cp = _remote(src, land.at[r], sems, 2 * (q * n_pairs + r - 1), peer)
                cp.start()
                copies.append(cp)
        for cp in copies:
            cp.wait()

        def total(src, land):
            row = lambda r: src[...] if r == 0 else land[r]
            return ((row(0) + row(4)) + (row(2) + row(6))) + ((row(1) + row(5)) + (row(3) + row(7)))

        g = total(g_ref, land_g)
        go_ref[...] = g
        d_ref[...], nm_ref[...], nv_ref[...] = _adamw_math(w_ref[...], g, m_ref[...], v_ref[...])
        lo_ref[...] = total(l_ref, land_l)

    vm = pl.BlockSpec(memory_space=pltpu.VMEM)
    return _call(
        "allreduce_adamw_g_mix",
        body,
        (),
        [vm] * 5,
        [vm] * 5,
        [jax.ShapeDtypeStruct((1, D), F32)] * 4 + [jax.ShapeDtypeStruct(loss_part.shape, F32)],
        (g_part, w, m, v, loss_part),
        scratch=[pltpu.VMEM((N_DEV, 1, D), F32), pltpu.VMEM((N_DEV,) + loss_part.shape, F32),
                 pltpu.SemaphoreType.DMA((4 * n_pairs,))],
        comm=comm,
    )


def kernel(x, g_mix, w_in, b_in, w_dw, b_dw, ln_g, ln_b, w_pool, s_pool, w_out, g_ffn, w_gate, w_up, w_down, g_final, loss_target, m_g_mix, m_w_in, m_b_in, m_w_dw, m_b_dw, m_ln_g, m_ln_b, m_w_pool, m_s_pool, m_w_out, m_g_ffn, m_w_gate, m_w_up, m_w_down, m_g_final, v_g_mix, v_w_in, v_b_in, v_w_dw, v_b_dw, v_ln_g, v_ln_b, v_w_pool, v_s_pool, v_w_out, v_g_ffn, v_w_gate, v_w_up, v_w_down, v_g_final):
    x2 = x[0]
    target = loss_target[0]
    T, D = x2.shape
    w_in2, w_out2, w_down2 = w_in[0], w_out[0], w_down[0]
    taps_first = lambda a: jnp.transpose(a, (1, 0, 2))
    w_dw3 = taps_first(w_dw)
    w_gateT, w_upT = w_gate[0].T, w_up[0].T
    CI = w_in2.shape[1] * N_CHIPS
    DM = w_out2.shape[0] * N_CHIPS
    F = w_down2.shape[0] * N_CHIPS
    KW, _, dw_cols = w_dw3.shape
    assert dw_cols == LANES
    n_grp, G = w_pool.shape[1], w_pool.shape[-1]
    w_pool3 = w_pool[0]
    g_final2 = g_final.reshape(1, D)

    me = (2 * lax.axis_index("x") + lax.axis_index("y")).astype(jnp.int32).reshape(1)

    w_inT_b, w_dw4, f_out, f_gate, f_up, f_down = _place_and_gather(
        [(w_in2, "rows", (CI, D), BF16, True, True), (w_dw3, "lead", (N_CHIPS, KW, 1, dw_cols), F32, False, False)],
        [(w, "rows", shape, BF16, False, True)
         for w, shape in ((w_out2, (DM, D)), (w_gateT, (F, D)), (w_upT, (F, D)), (w_down2, (F, D)))])
    w_pool_b = w_pool3.astype(BF16)
    (z, xn_b), (f_out, f_gate) = _in_proj(
        x2, g_mix, w_inT_b, b_in,
        comm=[_GatherIci([f_out], ["rows"], [True]), _GatherIci([f_gate], ["rows"], [True], which=(2,))])
    (y_b, v), (w_out_b, f_gate, f_up) = _seq_fwd(
        z, w_dw4, b_dw, ln_g, ln_b, w_pool_b, s_pool,
        comm=[_GatherD2d([f_out], ["rows"]), _GatherIci([f_gate], ["rows"], [True], which=(0, 1)),
              _GatherIci([f_up], ["rows"], [True])])
    (h1, hn_b), (wgT_b, wuT_b, f_down) = _out_proj(
        y_b, x2, w_out_b, g_ffn,
        comm=[_GatherD2d([f_gate, f_up], ["rows"] * 2), _GatherIci([f_down], ["rows"], [True])])
    (g_b, u_b, a_b), (wd_b,) = _gate_up(hn_b, wgT_b, wuT_b, comm=[_GatherD2d([f_down], ["rows"])])
    (dh2, dh2_b, loss_part, d_g_final), _ = _down_loss(a_b, wd_b, h1, target, g_final2)

    gw_down, _ = _weight_grad("grad_w_down", a_b, dh2_b)
    (dg_b, du_b), (p_down_xy,) = _ffn_bwd_act(dh2_b, wd_b, g_b, u_b, comm=[_Scatter([gw_down], ["rows"], which=(0, 1))])
    gw_gateT, (p_down_d,) = _weight_grad("grad_w_gate", dg_b, hn_b, comm=[_Scatter([gw_down], ["rows"], which=(2,))])
    gw_upT, _ = _weight_grad("grad_w_up", du_b, hn_b)
    sum_down = _sum_parts("sum_w_down", gw_down, "rows", [p_down_xy, p_down_d], me)
    (dh1, dh1_b, dy, d_g_ffn), (p_gate, oth_down) = _ffn_bwd_in(
        dg_b, du_b, wgT_b, wuT_b, h1, dh2, g_ffn, w_out_b, comm=[_Scatter([gw_gateT], ["rows"]), _Swap([sum_down])])
    gw_out, _ = _weight_grad("grad_w_out", y_b, dh1_b)
    sum_gate = _sum_parts("sum_w_gate", gw_gateT, "rows", [p_gate], me)
    res = {}
    res["w_down"] = _adamw_on_sparsecore("adamw_w_down", w_down2, m_w_down[0], v_w_down[0], sum_down, oth_down)
    (dz_b, d_wdw, d_bdw, d_lng, d_lnb, d_wp, d_sp, d_bin), (p_up, p_out, oth_gate) = _seq_bwd(
        z, dy, v, w_dw4, ln_g, ln_b, w_pool_b, s_pool,
        comm=[_Scatter([gw_upT, gw_out], ["rows", "rows"]), _Swap([sum_gate])])
    vec_grads = {"b_dw": d_bdw, "ln_g": d_lng, "ln_b": d_lnb, "s_pool": d_sp, "g_ffn": d_g_ffn, "g_final": d_g_final, "b_in": d_bin}
    layout = _PackLayout(dw_cols * N_CHIPS // LANES, n_grp, G, [(k, a.shape[1]) for k, a in vec_grads.items()])
    pack = _pack_small(layout, d_wdw, d_wp, vec_grads)
    sum_up = _sum_parts("sum_w_up", gw_upT, "rows", [p_up], me)
    sum_out = _sum_parts("sum_w_out", gw_out, "rows", [p_out], me)
    gw_inT, (p_small, oth_up, oth_out) = _weight_grad(
        "grad_w_in", dz_b, xn_b, comm=[_Scatter([pack], ["all"]), _Swap([sum_up, sum_out])])
    sum_small = _sum_parts("sum_small", pack, "all", [p_small], me)
    res["w_gate"] = _adamw_on_sparsecore("adamw_w_gate", w_gateT, m_w_gate[0].T, v_w_gate[0].T, sum_gate, oth_gate)
    (grad_x, d_g_mix), (p_in, oth_small) = _in_proj_bwd(
        dz_b, w_inT_b, x2, dh1, g_mix, comm=[_Scatter([gw_inT], ["rows"]), _Swap([sum_small])])
    res["w_up"] = _adamw_on_sparsecore("adamw_w_up", w_upT, m_w_up[0].T, v_w_up[0].T, sum_up, oth_up)
    res["w_out"] = _adamw_on_sparsecore("adamw_w_out", w_out2, m_w_out[0], v_w_out[0], sum_out, oth_out)
    sum_in = _sum_parts("sum_w_in", gw_inT, "rows", [p_in], me)
    (*res["g_mix"], loss_row), (oth_in,) = _allreduce_adamw_row(
        d_g_mix, g_mix, m_g_mix, v_g_mix, loss_part, comm=[_Swap([sum_in])])
    loss = loss_row[0, 0]
    res["w_in"], _ = _adamw("adamw_w_in", w_in2, m_w_in[0], v_w_in[0], sum_in, oth_in, g_transposed=True)

    vec_w = {"b_dw": b_dw, "ln_g": ln_g, "ln_b": ln_b, "s_pool": s_pool, "g_ffn": g_ffn, "g_final": g_final2, "b_in": b_in}
    vec_m = {"b_dw": m_b_dw, "ln_g": m_ln_g, "ln_b": m_ln_b, "s_pool": m_s_pool, "g_ffn": m_g_ffn,
             "g_final": m_g_final.reshape(1, D), "b_in": m_b_in}
    vec_v = {"b_dw": v_b_dw, "ln_g": v_ln_g, "ln_b": v_ln_b, "s_pool": v_s_pool, "g_ffn": v_g_ffn,
             "g_final": v_g_final.reshape(1, D), "b_in": v_b_in}
    small = _adamw_small(layout, sum_small, oth_small, w_dw3, taps_first(m_w_dw), taps_first(v_w_dw),
                         w_pool3, m_w_pool[0], v_w_pool[0], vec_w, vec_m, vec_v)
    res["w_dw"] = [taps_first(a) for a in small[0:4]]
    res["w_pool"] = [a[None] for a in small[4:8]]
    for q, k in enumerate(vec_w):
        res[k] = list(small[8 + 4 * q : 12 + 4 * q])
    res["g_final"] = [a.reshape(D) for a in res["g_final"]]
    for k in ("w_in", "w_out", "w_down"):
        res[k] = [a[None] for a in res[k]]
    for k in ("w_gate", "w_up"):
        res[k] = [a.T[None] for a in res[k]]

    order = ["g_mix", "w_in", "b_in", "w_dw", "b_dw", "ln_g", "ln_b", "w_pool", "s_pool", "w_out", "g_ffn", "w_gate", "w_up", "w_down", "g_final"]
    outs = [loss, grad_x[None]]
    for q in range(4):
        outs += [res[k][q] for k in order]
    return tuple(outs)
```

```python
import jax
import jax.numpy as jnp
from jax import lax
from jax.experimental import pallas as pl
from jax.experimental.pallas import tpu as pltpu
from jax.experimental.pallas import tpu_sc as plsc

F32 = jnp.float32
BF16 = jnp.bfloat16
MESH = pl.DeviceIdType.MESH
ANY = pl.BlockSpec(memory_space=pl.ANY)

RMS_EPS = 1e-6
LN_EPS = 1e-5
POOL_WINDOWS = (2, 4, 8, 16)
ADAM_LR = 0.001
ADAM_B1 = 0.9
ADAM_B2 = 0.999
ADAM_EPS = 1e-08
ADAM_WD = 0.01
ADAM_STEP = 10

LANES = 128
SUBLANES = 8
HALO = 32
CONV_ROWS = 64
HIDDEN_CHUNK = 512
VMEM_LIMIT = 56 * 1024 * 1024
PACK_W = 512
N_CHIPS = 4
N_DEV = 8
SC_CORES = 2
SC_TILES = 32
SC_LANES = 16


def _tile(n, want, mult=8):
    t = min(n, want)
    while n % t or t % mult:
        t -= 1
    return t


def _sigmoid(x):
    return 1.0 / (1.0 + jnp.exp(-x))


def _dot(a, b, dims):
    return lax.dot_general(a, b, (dims, ((), ())), preferred_element_type=F32)


NN = ((1,), (0,))
NT = ((1,), (1,))
TN = ((0,), (0,))


def _rms_bwd(x, g, dy):
    r = lax.rsqrt(jnp.mean(x * x, axis=-1, keepdims=True) + RMS_EPS)
    xh = x * r
    gy = dy * g
    dx = r * (gy - xh * jnp.mean(gy * xh, axis=-1, keepdims=True))
    return dx, dy * xh


def _accumulate(ref, first, val):
    @pl.when(first)
    def _():
        ref[...] = val

    @pl.when(jnp.logical_not(first))
    def _():
        ref[...] += val


def _place():
    return lax.axis_index("x"), lax.axis_index("y"), lax.axis_index("c")


def _other_chips(x, y):
    return [(1 - x, y), (x, 1 - y), (1 - x, 1 - y)]


def _rows(ref, start, n):
    return ref.at[pl.ds(pl.multiple_of(start, 16), n)]


def _window(ref, how, k, c=None):
    if how == "all":
        return ref
    if how == "lead":
        return ref.at[k]
    if how == "rows":
        n = ref.shape[0] // N_CHIPS
        if c is None:
            return _rows(ref, k * n, n)
        return _rows(ref, k * n + c * (n // 2), n // 2)
    n = ref.shape[1] // N_CHIPS
    cols = pl.ds(pl.multiple_of(k * n, LANES), n)
    if c is None:
        return ref.at[:, cols]
    h = ref.shape[0] // 2
    return ref.at[pl.ds(pl.multiple_of(c * h, 16), h), cols]


def _remote(src, dst, sems, s, device):
    return pltpu.make_async_remote_copy(
        src_ref=src, dst_ref=dst, send_sem=sems.at[s], recv_sem=sems.at[s + 1], device_id=device, device_id_type=MESH)


class _GatherIci:
    aliased = True

    def __init__(self, fulls, hows, splits, which=(0, 1, 2)):
        self.fulls, self.hows, self.splits, self.which = list(fulls), list(hows), list(splits), tuple(which)

    def inputs(self):
        return self.fulls

    def out_shapes(self):
        return [jax.ShapeDtypeStruct(a.shape, a.dtype) for a in self.fulls]

    def n_sems(self):
        return 6 * len(self.fulls)

    def build(self, ins, outs, sems, base):
        x, y, c = _place()
        me = 2 * x + y
        chips = _other_chips(x, y)
        starts, waits = [], []
        for a, (how, sp) in enumerate(zip(self.hows, self.splits)):
            half = c if sp else None
            mine = _window(outs[a], how, me, half)
            for j in self.which:
                px, py = chips[j]
                s = base + 6 * a + 2 * j
                cp = _remote(mine, mine, sems, s, (px, py, c))
                landing = _remote(mine, _window(outs[a], how, 2 * px + py, half), sems, s, (px, py, c))
                starts.append(cp.start)
                waits += [landing.wait_recv, cp.wait_send]
        return starts, waits


class _GatherD2d:
    aliased = True

    def __init__(self, fulls, hows):
        self.fulls, self.hows = list(fulls), list(hows)

    def inputs(self):
        return self.fulls

    def out_shapes(self):
        return [jax.ShapeDtypeStruct(a.shape, a.dtype) for a in self.fulls]

    def n_sems(self):
        return 6 * len(self.fulls)

    def build(self, ins, outs, sems, base):
        x, y, c = _place()
        starts, waits = [], []
        for a, how in enumerate(self.hows):
            for j, (px, py) in enumerate(_other_chips(x, y)):
                s = base + 6 * a + 2 * j
                got = _window(outs[a], how, 2 * px + py, c)
                cp = _remote(got, got, sems, s, (x, y, 1 - c))
                landing = _remote(got, _window(outs[a], how, 2 * px + py, 1 - c), sems, s, (x, y, 1 - c))
                starts.append(cp.start)
                waits += [landing.wait_recv, cp.wait_send]
        return starts, waits


def _part_shape(a, how):
    if how == "all":
        return a.shape
    if how == "rows":
        return (a.shape[0] // N_CHIPS, a.shape[1])
    return (a.shape[0], a.shape[1] // N_CHIPS)


class _Scatter:
    aliased = False

    def __init__(self, fulls, hows, which=(0, 1, 2)):
        self.fulls, self.hows, self.which = list(fulls), list(hows), tuple(which)

    def inputs(self):
        return self.fulls

    def out_shapes(self):
        return [jax.ShapeDtypeStruct((len(self.which),) + _part_shape(a, h), a.dtype) for a, h in zip(self.fulls, self.hows)]

    def n_sems(self):
        return 6 * len(self.fulls)

    def build(self, ins, outs, sems, base):
        x, y, c = _place()
        chips = _other_chips(x, y)
        starts, waits = [], []
        for a, how in enumerate(self.hows):
            for slot, j in enumerate(self.which):
                px, py = chips[j]
                cp = _remote(_window(ins[a], how, 2 * px + py), outs[a].at[slot], sems, base + 6 * a + 2 * j, (px, py, c))
                starts.append(cp.start)
                waits += [cp.wait_recv, cp.wait_send]
        return starts, waits


class _Swap:
    aliased = False

    def __init__(self, arrays):
        self.arrays = list(arrays)

    def inputs(self):
        return self.arrays

    def out_shapes(self):
        return [jax.ShapeDtypeStruct(a.shape, a.dtype) for a in self.arrays]

    def n_sems(self):
        return 2 * len(self.arrays)

    def build(self, ins, outs, sems, base):
        x, y, c = _place()
        starts, waits = [], []
        for a in range(len(ins)):
            cp = _remote(ins[a], outs[a], sems, base + 2 * a, (x, y, 1 - c))
            starts.append(cp.start)
            waits += [cp.wait_recv, cp.wait_send]
        return starts, waits


def _call(name, body, grid, in_specs, out_specs, out_shape, args, scratch=(), comm=(), aliases=None):
    comm = list(comm)
    n_in, n_out, n_scr = len(args), len(out_shape), len(scratch)
    c_in = [a for op in comm for a in op.inputs()]
    c_out = [s for op in comm for s in op.out_shapes()]
    n_sems = sum(op.n_sems() for op in comm)
    aliases, i_in, i_out = dict(aliases or {}), 0, 0
    for op in comm:
        if op.aliased:
            for q in range(len(op.inputs())):
                aliases[n_in + i_in + q] = n_out + i_out + q
        i_in, i_out = i_in + len(op.inputs()), i_out + len(op.out_shapes())

    def wrapped(*refs):
        ins = refs[:n_in]
        cin = refs[n_in : n_in + len(c_in)]
        o0 = n_in + len(c_in)
        outs = refs[o0 : o0 + n_out]
        cout = refs[o0 + n_out : o0 + n_out + len(c_out)]
        s0 = o0 + n_out + len(c_out)
        scr = refs[s0 : s0 + n_scr]

        def copies():
            sems = refs[s0 + n_scr]
            starts, waits = [], []
            i_in = i_out = base = 0
            for op in comm:
                ni, no = len(op.inputs()), len(op.out_shapes())
                s, w = op.build(cin[i_in : i_in + ni], cout[i_out : i_out + no], sems, base)
                starts += s
                waits += w
                i_in, i_out, base = i_in + ni, i_out + no, base + op.n_sems()
            return starts, waits

        def run_starts():
            for start in copies()[0]:
                start()

        def run_waits():
            for wait in copies()[1]:
                wait()

        if comm and grid:
            first = last = True
            for d, n in enumerate(grid):
                first = jnp.logical_and(first, pl.program_id(d) == 0)
                last = jnp.logical_and(last, pl.program_id(d) == n - 1)
            pl.when(first)(run_starts)
        elif comm:
            run_starts()
        if body is not None:
            body(*ins, *outs, *scr)
        if comm and grid:
            pl.when(last)(run_waits)
        elif comm:
            run_waits()

    res = pl.pallas_call(
        wrapped,
        name=name,
        grid=grid,
        in_specs=list(in_specs) + [ANY] * len(c_in),
        out_specs=list(out_specs) + [ANY] * len(c_out),
        out_shape=list(out_shape) + c_out,
        scratch_shapes=list(scratch) + ([pltpu.SemaphoreType.DMA((n_sems,))] if comm else []),
        input_output_aliases=aliases,
        compiler_params=pltpu.CompilerParams(dimension_semantics=("arbitrary",) * len(grid), vmem_limit_bytes=VMEM_LIMIT),
    )(*args, *c_in)
    return tuple(res[:n_out]), tuple(res[n_out:])


def _place_and_gather(now, later):
    items = list(now) + list(later)
    n, n_now = len(items), len(now)
    buf_shape = lambda it: it[0].shape[::-1] if it[4] else it[0].shape
    split_now = [a for a in range(n_now) if items[a][5]]

    def body(*refs):
        ins, outs = refs[:n], refs[n : 2 * n]
        stage, bufs = refs[2 * n : 3 * n - n_now], refs[3 * n - n_now : 4 * n - n_now]
        sems = refs[4 * n - n_now]
        x, y, c = _place()
        me = 2 * x + y
        chips = _other_chips(x, y)
        loads = [pltpu.make_async_copy(ins[a], stage[a - n_now], sems.at[a]) for a in range(n_now, n)]
        for ld in loads:
            ld.start()
        pending = []

        def place(a, val):
            _, how, _, dtype, transposed, _ = items[a]
            bufs[a][...] = (val.T if transposed else val).astype(dtype)
            cp = pltpu.make_async_copy(bufs[a], _window(outs[a], how, me), sems.at[n + a])
            cp.start()
            pending.append(cp.wait)

        arrivals = []
        for a in range(n_now):
            place(a, ins[a][...])
            how, split = items[a][1], items[a][5]
            half = c if split else None
            src = _rows(bufs[a], c * (bufs[a].shape[0] // 2), bufs[a].shape[0] // 2) if split else bufs[a]
            for j, (px, py) in enumerate(chips):
                s = 2 * n + 6 * a + 2 * j
                cp = _remote(src, _window(outs[a], how, me, half), sems, s, (px, py, c))
                landing = _remote(src, _window(outs[a], how, 2 * px + py, half), sems, s, (px, py, c))
                cp.start()
                arrivals.append(landing.wait_recv)
                pending.append(cp.wait_send)
        for a in range(n_now, n):
            loads[a - n_now].wait()
            place(a, stage[a - n_now][...])
        for wait in arrivals:
            wait()
        d2d = _GatherD2d([None] * len(split_now), [items[a][1] for a in split_now])
        starts, waits = d2d.build(None, [outs[a] for a in split_now], sems, 2 * n + 6 * n_now)
        for start in starts:
            start()
        for wait in waits + pending:
            wait()

    vm = pl.BlockSpec(memory_space=pltpu.VMEM)
    return pl.pallas_call(
        body,
        name="place_and_gather",
        in_specs=[vm] * n_now + [ANY] * (n - n_now),
        out_specs=[ANY] * n,
        out_shape=[jax.ShapeDtypeStruct(it[2], it[3]) for it in items],
        scratch_shapes=[pltpu.VMEM(it[0].shape, it[0].dtype) for it in later]
        + [pltpu.VMEM(buf_shape(it), it[3]) for it in items]
        + [pltpu.SemaphoreType.DMA((2 * n + 6 * n_now + 6 * len(split_now),))],
        compiler_params=pltpu.CompilerParams(vmem_limit_bytes=VMEM_LIMIT),
    )(*[it[0] for it in items])


def _in_proj(x, g_mix, w_inT_b, b_in, comm=()):
    T, D = x.shape
    CI = w_inT_b.shape[0]
    tm = _tile(T, 512)

    def body(x_ref, g_ref, w_ref, b_ref, z_ref, xn_ref):
        xv = x_ref[...]
        r = lax.rsqrt(jnp.mean(xv * xv, axis=-1, keepdims=True) + RMS_EPS)
        xn = (xv * r * g_ref[...]).astype(BF16)
        xn_ref[...] = xn
        z_ref[...] = _dot(xn, w_ref[...], NT) + b_ref[...]

    return _call(
        "in_proj",
        body,
        (T // tm,),
        [
            pl.BlockSpec((tm, D), lambda i: (i, 0)),
            pl.BlockSpec((1, D), lambda i: (0, 0)),
            pl.BlockSpec((CI, D), lambda i: (0, 0)),
            pl.BlockSpec((1, CI), lambda i: (0, 0)),
        ],
        [pl.BlockSpec((tm, CI), lambda i: (i, 0)), pl.BlockSpec((tm, D), lambda i: (i, 0))],
        [jax.ShapeDtypeStruct((T, CI), F32), jax.ShapeDtypeStruct((T, D), BF16)],
        (x, g_mix, w_inT_b, b_in),
        comm=comm,
    )


def _fill_shifted(scr):
    n = scr.shape[1] - SUBLANES
    for s in range(1, SUBLANES):
        scr[s, 0:n, :] = scr[0, s : s + n, :]


def _shifted_rows(scr, off, n, cs):
    s = off % SUBLANES
    return scr[s, off - s : off - s + n, cs]


def _pool_mean_minus_token(p_scr, cs, w, cnt, tt):
    tok = p_scr[HALO : HALO + tt, cs]
    s = tok
    for d in range(1, w):
        s = s + p_scr[HALO - d : HALO - d + tt, cs]
    return s / cnt - tok


def _seq_fwd(z, w_dw4, b_dw, ln_g, ln_b, w_pool_b, s_pool, comm=()):
    T, CI = z.shape
    CC = ln_g.shape[1]
    n_grp, G = w_pool_b.shape[0], w_pool_b.shape[-1]
    KW = w_dw4.shape[1]
    D = CC + n_grp * G
    tt = _tile(T, 512, HALO)
    per = tt // HALO

    def body(zc_ref, zp_ref, wdw_ref, bdw_ref, lng_ref, lnb_ref, wp_ref, sp_ref, y_ref, v_ref, u_scr, p_scr):
        i = pl.program_id(0)
        first = i == 0
        u_prev = zp_ref[:, 0:CC] * _sigmoid(zp_ref[:, CC : 2 * CC])
        u_scr[0, 0:HALO, :] = jnp.where(first, 0.0, u_prev)
        p_scr[0:HALO, :] = jnp.where(first, 0.0, zp_ref[:, 2 * CC :])
        u_scr[0, HALO:, :] = zc_ref[:, 0:CC] * _sigmoid(zc_ref[:, CC : 2 * CC])
        p_scr[HALO:, :] = zc_ref[:, 2 * CC :]
        _fill_shifted(u_scr)

        for j in range(CC // LANES):
            cs = slice(LANES * j, LANES * (j + 1))
            for rb in range(tt // CONV_ROWS):
                acc = jnp.zeros((CONV_ROWS, LANES), F32)
                for k in range(KW):
                    off = HALO - (KW - 1) + k + rb * CONV_ROWS
                    acc = acc + _shifted_rows(u_scr, off, CONV_ROWS, cs) * wdw_ref[j, k]
                v_ref[rb * CONV_ROWS : (rb + 1) * CONV_ROWS, cs] = acc + bdw_ref[:, cs]

        v = v_ref[...]
        mu = jnp.mean(v, axis=-1, keepdims=True)
        d = v - mu
        var = jnp.mean(d * d, axis=-1, keepdims=True)
        ln = d * lax.rsqrt(var + LN_EPS) * lng_ref[...] + lnb_ref[...]
        y_ref[:, 0:CC] = (ln * _sigmoid(ln)).astype(BF16)

        tpos = i * tt + lax.broadcasted_iota(jnp.int32, (tt, 1), 0)
        for gi, w in enumerate(POOL_WINDOWS):
            cs = slice(G * gi, G * (gi + 1))
            cnt = jnp.minimum(tpos + 1, w).astype(F32)
            yi = _pool_mean_minus_token(p_scr, cs, w, cnt, tt)
            q = _dot(yi.astype(BF16), wp_ref[gi], NN)
            y_ref[:, CC + G * gi : CC + G * (gi + 1)] = (q * sp_ref[:, cs]).astype(BF16)

    const2 = lambda i: (0, 0)
    return _call(
        "seq_fwd",
        body,
        (T // tt,),
        [
            pl.BlockSpec((tt, CI), lambda i: (i, 0)),
            pl.BlockSpec((HALO, CI), lambda i: (jnp.maximum(i * per - 1, 0), 0)),
            pl.BlockSpec(w_dw4.shape, lambda i: (0,) * w_dw4.ndim),
            pl.BlockSpec((1, CC), const2),
            pl.BlockSpec((1, CC), const2),
            pl.BlockSpec((1, CC), const2),
            pl.BlockSpec(w_pool_b.shape, lambda i: (0, 0, 0)),
            pl.BlockSpec((1, n_grp * G), const2),
        ],
        [pl.BlockSpec((tt, D), lambda i: (i, 0)), pl.BlockSpec((tt, CC), lambda i: (i, 0))],
        [jax.ShapeDtypeStruct((T, D), BF16), jax.ShapeDtypeStruct((T, CC), F32)],
        (z, z, w_dw4, b_dw, ln_g, ln_b, w_pool_b, s_pool),
        scratch=[pltpu.VMEM((SUBLANES, HALO + tt, CC), F32), pltpu.VMEM((HALO + tt, n_grp * G), F32)],
        comm=comm,
    )


def _out_proj(y_b, x, w_out_b, g_ffn, comm=()):
    T, D = x.shape
    tm = _tile(T, 512)

    def body(y_ref, x_ref, w_ref, g_ref, h1_ref, hn_ref):
        h1 = x_ref[...] + _dot(y_ref[...], w_ref[...], NN)
        h1_ref[...] = h1
        r = lax.rsqrt(jnp.mean(h1 * h1, axis=-1, keepdims=True) + RMS_EPS)
        hn_ref[...] = (h1 * r * g_ref[...]).astype(BF16)

    row = lambda i: (i, 0)
    return _call(
        "out_proj",
        body,
        (T // tm,),
        [
            pl.BlockSpec((tm, y_b.shape[1]), row),
            pl.BlockSpec((tm, D), row),
            pl.BlockSpec(w_out_b.shape, lambda i: (0, 0)),
            pl.BlockSpec((1, D), lambda i: (0, 0)),
        ],
        [pl.BlockSpec((tm, D), row), pl.BlockSpec((tm, D), row)],
        [jax.ShapeDtypeStruct((T, D), F32), jax.ShapeDtypeStruct((T, D), BF16)],
        (y_b, x, w_out_b, g_ffn),
        comm=comm,
    )


def _hidden_tile(F):
    return _tile(F, 1408, LANES)


def _gate_up(hn_b, wgT_b, wuT_b, comm=()):
    T, D = hn_b.shape
    F = wgT_b.shape[0]
    tm, tf = _tile(T, 1024), _hidden_tile(F)

    def body(hn_ref, wg_ref, wu_ref, g_ref, u_ref, a_ref):
        hn = hn_ref[...]
        for c0 in range(0, tf, HIDDEN_CHUNK):
            cs = slice(c0, min(c0 + HIDDEN_CHUNK, tf))
            gv = _dot(hn, wg_ref[cs, :], NT)
            uv = _dot(hn, wu_ref[cs, :], NT)
            g_ref[:, cs] = gv.astype(BF16)
            u_ref[:, cs] = uv.astype(BF16)
            a_ref[:, cs] = (gv * _sigmoid(gv) * uv).astype(BF16)

    wspec = pl.BlockSpec((tf, D), lambda j, i: (j, 0))
    ospec = pl.BlockSpec((tm, tf), lambda j, i: (i, j))
    return _call(
        "gate_up",
        body,
        (F // tf, T // tm),
        [pl.BlockSpec((tm, D), lambda j, i: (i, 0)), wspec, wspec],
        [ospec, ospec, ospec],
        [jax.ShapeDtypeStruct((T, F), BF16)] * 3,
        (hn_b, wgT_b, wuT_b),
        comm=comm,
    )


def _down_loss(a_b, wd_b, h1, target, g_final, comm=()):
    T, D = h1.shape
    F = a_b.shape[1]
    tm = _tile(T, 512)
    nt = T // tm

    def body(a_ref, w_ref, h1_ref, t_ref, g_ref, dh2_ref, dh2b_ref, loss_ref, dg_ref):
        i = pl.program_id(0)
        h2 = h1_ref[...] + _dot(a_ref[...], w_ref[...], NN)
        r = lax.rsqrt(jnp.mean(h2 * h2, axis=-1, keepdims=True) + RMS_EPS)
        g = g_ref[...]
        diff = h2 * r * g - t_ref[...]
        _accumulate(loss_ref, i == 0, jnp.full(loss_ref.shape, jnp.sum(diff * diff) * (0.5 / D), F32))
        dh2, dg_rows = _rms_bwd(h2, g, diff * (1.0 / D))
        dh2_ref[...] = dh2
        dh2b_ref[...] = dh2.astype(BF16)
        _accumulate(dg_ref, i == 0, jnp.sum(dg_rows, axis=0, keepdims=True))

    row = lambda i: (i, 0)
    return _call(
        "down_loss",
        body,
        (nt,),
        [
            pl.BlockSpec((tm, F), row),
            pl.BlockSpec((F, D), lambda i: (0, 0), pipeline_mode=pl.Buffered(1)),
            pl.BlockSpec((tm, D), row),
            pl.BlockSpec((tm, D), row),
            pl.BlockSpec((1, D), lambda i: (0, 0)),
        ],
        [
            pl.BlockSpec((tm, D), row),
            pl.BlockSpec((tm, D), row),
            pl.BlockSpec((1, LANES), lambda i: (0, 0)),
            pl.BlockSpec((1, D), lambda i: (0, 0)),
        ],
        [
            jax.ShapeDtypeStruct((T, D), F32),
            jax.ShapeDtypeStruct((T, D), BF16),
            jax.ShapeDtypeStruct((1, LANES), F32),
            jax.ShapeDtypeStruct((1, D), F32),
        ],
        (a_b, wd_b, h1, target, g_final),
        comm=comm,
    )


def _ffn_bwd_act(dh2_b, wd_b, g_b, u_b, comm=()):
    T, D = dh2_b.shape
    F = wd_b.shape[0]
    tm, tf = _tile(T, 1024), _hidden_tile(F)

    def body(d_ref, w_ref, g_ref, u_ref, dg_ref, du_ref):
        d = d_ref[...]
        for c0 in range(0, tf, HIDDEN_CHUNK):
            cs = slice(c0, min(c0 + HIDDEN_CHUNK, tf))
            da = _dot(d, w_ref[cs, :], NT)
            gv = g_ref[:, cs].astype(F32)
            uv = u_ref[:, cs].astype(F32)
            sg = _sigmoid(gv)
            silu = gv * sg
            dg_ref[:, cs] = (da * uv * (sg * (1.0 + gv * (1.0 - sg)))).astype(BF16)
            du_ref[:, cs] = (da * silu).astype(BF16)

    aspec = pl.BlockSpec((tm, tf), lambda j, i: (i, j))
    return _call(
        "ffn_bwd_act",
        body,
        (F // tf, T // tm),
        [pl.BlockSpec((tm, D), lambda j, i: (i, 0)), pl.BlockSpec((tf, D), lambda j, i: (j, 0)), aspec, aspec],
        [aspec, aspec],
        [jax.ShapeDtypeStruct((T, F), BF16)] * 2,
        (dh2_b, wd_b, g_b, u_b),
        comm=comm,
    )


def _ffn_bwd_in(dg_b, du_b, wgT_b, wuT_b, h1, dh2, g_ffn, w_out_b, comm=()):
    T, D = h1.shape
    F = wgT_b.shape[0]
    DM = w_out_b.shape[0]
    tm = _tile(T, 512)

    def body(dg_ref, du_ref, wg_ref, wu_ref, h1_ref, dh2_ref, g_ref, wo_ref, dh1_ref, dh1b_ref, dy_ref, dgf_ref):
        i = pl.program_id(0)
        dhn = _dot(dg_ref[...], wg_ref[...], NN) + _dot(du_ref[...], wu_ref[...], NN)
        dx, dg_rows = _rms_bwd(h1_ref[...], g_ref[...], dhn)
        dh1 = dh2_ref[...] + dx
        dh1b = dh1.astype(BF16)
        dh1_ref[...] = dh1
        dh1b_ref[...] = dh1b
        dy_ref[...] = _dot(dh1b, wo_ref[...], NT)
        _accumulate(dgf_ref, i == 0, jnp.sum(dg_rows, axis=0, keepdims=True))

    row = lambda i: (i, 0)
    const = lambda i: (0, 0)
    return _call(
        "ffn_bwd_in",
        body,
        (T // tm,),
        [
            pl.BlockSpec((tm, F), row),
            pl.BlockSpec((tm, F), row),
            pl.BlockSpec((F, D), const, pipeline_mode=pl.Buffered(1)),
            pl.BlockSpec((F, D), const, pipeline_mode=pl.Buffered(1)),
            pl.BlockSpec((tm, D), row),
            pl.BlockSpec((tm, D), row),
            pl.BlockSpec((1, D), const),
            pl.BlockSpec((DM, D), const, pipeline_mode=pl.Buffered(1)),
        ],
        [pl.BlockSpec((tm, D), row), pl.BlockSpec((tm, D), row), pl.BlockSpec((tm, DM), row), pl.BlockSpec((1, D), const)],
        [
            jax.ShapeDtypeStruct((T, D), F32),
            jax.ShapeDtypeStruct((T, D), BF16),
            jax.ShapeDtypeStruct((T, DM), F32),
            jax.ShapeDtypeStruct((1, D), F32),
        ],
        (dg_b, du_b, wgT_b, wuT_b, h1, dh2, g_ffn, w_out_b),
        comm=comm,
    )


def _seq_bwd(z, dy, v, w_dw4, ln_g, ln_b, w_pool_b, s_pool, comm=()):
    T, CI = z.shape
    CC = ln_g.shape[1]
    n_grp, G = w_pool_b.shape[0], w_pool_b.shape[-1]
    CP = n_grp * G
    KW = w_dw4.shape[1]
    n_cc = CC // LANES
    D = CC + CP
    tt = _tile(T, 512, HALO)
    per = tt // HALO
    n_tiles = T // tt
    last_halo = T // HALO - 1

    def body(zc_ref, zp_ref, dyc_ref, dyn_ref, vc_ref, vn_ref, wdw_ref, lng_ref, lnb_ref, wp_ref, sp_ref,
             dz_ref, dwdw_ref, dbdw_ref, dlng_ref, dlnb_ref, dwp_ref, dsp_ref, dbin_ref,
             dv_scr, u_scr, p_scr, g_scr, dw_scr):
        i = pl.program_id(0)
        first = i == 0
        last = i == n_tiles - 1
        lng, lnb = lng_ref[...], lnb_ref[...]

        def conv_pre(vv, dyc):
            mu = jnp.mean(vv, axis=-1, keepdims=True)
            d = vv - mu
            rs = lax.rsqrt(jnp.mean(d * d, axis=-1, keepdims=True) + LN_EPS)
            xh = d * rs
            ln = xh * lng + lnb
            sg = _sigmoid(ln)
            dln = dyc * (sg * (1.0 + ln * (1.0 - sg)))
            dxh = dln * lng
            dv = rs * (dxh - jnp.mean(dxh, axis=-1, keepdims=True) - xh * jnp.mean(dxh * xh, axis=-1, keepdims=True))
            return dv, dln, xh

        dv_c, dln_c, xh_c = conv_pre(vc_ref[...], dyc_ref[:, 0:CC])
        dv_scr[0, 0:tt, :] = dv_c
        dv_n, _, _ = conv_pre(vn_ref[...], dyn_ref[:, 0:CC])
        dv_scr[0, tt:, :] = jnp.where(last, 0.0, dv_n)
        _fill_shifted(dv_scr)
        _accumulate(dlng_ref, first, jnp.sum(dln_c * xh_c, axis=0, keepdims=True))
        _accumulate(dlnb_ref, first, jnp.sum(dln_c, axis=0, keepdims=True))
        _accumulate(dbdw_ref, first, jnp.sum(dv_c, axis=0, keepdims=True))

        u_scr[...] = zc_ref[:, 0:CC] * _sigmoid(zc_ref[:, CC : 2 * CC])

        @pl.when(first)
        def _():
            dw_scr[...] = jnp.zeros_like(dw_scr)

        for j in range(n_cc):
            cs = slice(LANES * j, LANES * (j + 1))
            gs = slice(CC + LANES * j, CC + LANES * (j + 1))
            dbin_a = jnp.zeros((1, LANES), F32)
            dbin_g = jnp.zeros((1, LANES), F32)
            for rb in range(tt // CONV_ROWS):
                rows = slice(rb * CONV_ROWS, (rb + 1) * CONV_ROWS)
                u_blk = u_scr[rows, cs]
                du = jnp.zeros((CONV_ROWS, LANES), F32)
                for k in range(KW):
                    off = rb * CONV_ROWS + (KW - 1) - k
                    d = _shifted_rows(dv_scr, off, CONV_ROWS, cs)
                    du = du + d * wdw_ref[j, k]
                    dw_scr[j * HALO + k] += jnp.sum((u_blk * d).reshape(CONV_ROWS // 8, 8, LANES), axis=0)
                a = zc_ref[rows, cs]
                sg = _sigmoid(zc_ref[rows, gs])
                da = du * sg
                dgate = du * a * sg * (1.0 - sg)
                dz_ref[rows, cs] = da.astype(BF16)
                dz_ref[rows, gs] = dgate.astype(BF16)
                dbin_a = dbin_a + jnp.sum(da, axis=0, keepdims=True)
                dbin_g = dbin_g + jnp.sum(dgate, axis=0, keepdims=True)
            _accumulate(dbin_ref.at[:, cs], first, dbin_a)
            _accumulate(dbin_ref.at[:, gs], first, dbin_g)

        @pl.when(last)
        def _():
            dwdw_ref[...] = jnp.sum(dw_scr[...], axis=1).reshape(dwdw_ref.shape)

        p_scr[0:HALO, :] = jnp.where(first, 0.0, zp_ref[:, 2 * CC :])
        p_scr[HALO:, :] = zc_ref[:, 2 * CC :]
        tpos = i * tt + lax.broadcasted_iota(jnp.int32, (tt, 1), 0)
        for gi, w in enumerate(POOL_WINDOWS):
            cs = slice(G * gi, G * (gi + 1))
            ys = slice(CC + G * gi, CC + G * (gi + 1))
            ps = slice(2 * CC + G * gi, 2 * CC + G * (gi + 1))
            cnt = jnp.minimum(tpos + 1, w).astype(F32)
            yib = _pool_mean_minus_token(p_scr, cs, w, cnt, tt).astype(BF16)
            wp = wp_ref[gi]
            sp = sp_ref[:, cs]
            dyp = dyc_ref[:, ys]
            q = _dot(yib, wp, NN)
            _accumulate(dsp_ref.at[:, cs], first, jnp.sum(dyp * q, axis=0, keepdims=True))
            dq_c = (dyp * sp).astype(BF16)
            dq_n = (jnp.where(last, 0.0, dyn_ref[:, ys]) * sp).astype(BF16)
            _accumulate(dwp_ref.at[gi], first, _dot(yib, dq_c, TN))
            dyi_c = _dot(dq_c, wp, NT)
            g_scr[0:tt, cs] = dyi_c / cnt
            g_scr[tt:, cs] = _dot(dq_n, wp, NT) * (1.0 / w)
            dp = -dyi_c
            for d in range(w):
                dp = dp + g_scr[d : d + tt, cs]
            dz_ref[:, ps] = dp.astype(BF16)
            _accumulate(dbin_ref.at[:, ps], first, jnp.sum(dp, axis=0, keepdims=True))

    cur = lambda i: (i, 0)
    prev = lambda i: (jnp.maximum(i * per - 1, 0), 0)
    nxt = lambda i: (jnp.minimum((i + 1) * per, last_halo), 0)
    c2 = lambda i: (0, 0)
    c3 = lambda i: (0, 0, 0)
    return _call(
        "seq_bwd",
        body,
        (n_tiles,),
        [
            pl.BlockSpec((tt, CI), cur),
            pl.BlockSpec((HALO, CI), prev),
            pl.BlockSpec((tt, D), cur),
            pl.BlockSpec((HALO, D), nxt),
            pl.BlockSpec((tt, CC), cur),
            pl.BlockSpec((HALO, CC), nxt),
            pl.BlockSpec(w_dw4.shape, lambda i: (0,) * w_dw4.ndim),
            pl.BlockSpec((1, CC), c2),
            pl.BlockSpec((1, CC), c2),
            pl.BlockSpec(w_pool_b.shape, c3),
            pl.BlockSpec((1, CP), c2),
        ],
        [
            pl.BlockSpec((tt, CI), cur),
            pl.BlockSpec((n_cc, HALO, LANES), c3),
            pl.BlockSpec((1, CC), c2),
            pl.BlockSpec((1, CC), c2),
            pl.BlockSpec((1, CC), c2),
            pl.BlockSpec((n_grp, G, G), c3),
            pl.BlockSpec((1, CP), c2),
            pl.BlockSpec((1, CI), c2),
        ],
        [
            jax.ShapeDtypeStruct((T, CI), BF16),
            jax.ShapeDtypeStruct((n_cc, HALO, LANES), F32),
            jax.ShapeDtypeStruct((1, CC), F32),
            jax.ShapeDtypeStruct((1, CC), F32),
            jax.ShapeDtypeStruct((1, CC), F32),
            jax.ShapeDtypeStruct((n_grp, G, G), F32),
            jax.ShapeDtypeStruct((1, CP), F32),
            jax.ShapeDtypeStruct((1, CI), F32),
        ],
        (z, z, dy, dy, v, v, w_dw4, ln_g, ln_b, w_pool_b, s_pool),
        scratch=[
            pltpu.VMEM((SUBLANES, tt + HALO, CC), F32),
            pltpu.VMEM((tt, CC), F32),
            pltpu.VMEM((HALO + tt, CP), F32),
            pltpu.VMEM((tt + HALO, CP), F32),
            pltpu.VMEM((n_cc * HALO, 8, LANES), F32),
        ],
        comm=comm,
    )


def _in_proj_bwd(name, dz_b, w_inT_b, x, dh1, g_mix, tiles, carry=None, comm=()):
    T, D = x.shape
    CI = w_inT_b.shape[0]
    tm = _tile(T, 512)
    first, count = tiles

    def body(dz_ref, w_ref, x_ref, dh1_ref, g_ref, *rest):
        dx_ref, dg_ref = rest[-2:]
        i = pl.program_id(0)
        dxn = _dot(dz_ref[...], w_ref[...], NN)
        dx, dg_rows = _rms_bwd(x_ref[...], g_ref[...], dxn)
        dx_ref[...] = dh1_ref[...] + dx
        dg = jnp.sum(dg_rows, axis=0, keepdims=True)
        if carry is not None:
            dg = jnp.where(i == 0, rest[1][...], 0.0) + dg
        _accumulate(dg_ref, i == 0, dg)

    row = lambda i: (i + first, 0)
    const = lambda i: (0, 0)
    in_specs = [
        pl.BlockSpec((tm, CI), row),
        pl.BlockSpec((CI, D), const),
        pl.BlockSpec((tm, D), row),
        pl.BlockSpec((tm, D), row),
        pl.BlockSpec((1, D), const),
    ]
    args = (dz_b, w_inT_b, x, dh1, g_mix)
    aliases = {}
    if carry is not None:
        in_specs += [ANY, pl.BlockSpec((1, D), const)]
        args += tuple(carry)
        aliases = {len(args) - 2: 0}
    return _call(
        name,
        body,
        (count,),
        in_specs,
        [pl.BlockSpec((tm, D), row), pl.BlockSpec((1, D), const)],
        [jax.ShapeDtypeStruct((T, D), F32), jax.ShapeDtypeStruct((1, D), F32)],
        args,
        aliases=aliases,
        comm=comm,
    )


def _weight_grad(name, a_b, b_b, comm=()):
    T, N1 = a_b.shape
    N2 = b_b.shape[1]
    t1 = _tile(N1, 1408, LANES)
    tk = _tile(T, 2048)
    nk = T // tk

    def body(a_ref, b_ref, o_ref, acc):
        k = pl.program_id(1)
        _accumulate(acc, k == 0, _dot(a_ref[...], b_ref[...], TN))

        @pl.when(k == nk - 1)
        def _():
            o_ref[...] = acc[...].astype(BF16)

    (out,), rest = _call(
        name,
        body,
        (N1 // t1, nk),
        [pl.BlockSpec((tk, t1), lambda n, k: (k, n)), pl.BlockSpec((tk, N2), lambda n, k: (k, 0))],
        [pl.BlockSpec((t1, N2), lambda n, k: (n, 0))],
        [jax.ShapeDtypeStruct((N1, N2), BF16)],
        (a_b, b_b),
        scratch=[pltpu.VMEM((t1, N2), F32)],
        comm=comm,
    )
    return out, rest


def _sum_parts(name, full, how, parts, me):
    _, R, C = parts[0].shape
    tr = _tile(R, 512)
    nb = R // tr
    where = [(q, r) for q, p in enumerate(parts) for r in range(p.shape[0])]
    assert len(where) == 3

    def body(me_ref, own_ref, *refs):
        o_ref = refs[-1]
        f = lambda j: refs[where[j][0]][where[j][1]].astype(F32)
        o_ref[...] = (own_ref[...].astype(F32) + f(0)) + (f(1) + f(2))

    own_map = {"rows": lambda i, me_ref: (me_ref[0] * nb + i, 0), "cols": lambda i, me_ref: (i, me_ref[0]),
               "all": lambda i, me_ref: (i, 0)}[how]
    return pl.pallas_call(
        body,
        name=name,
        grid_spec=pltpu.PrefetchScalarGridSpec(
            num_scalar_prefetch=1,
            grid=(nb,),
            in_specs=[pl.BlockSpec((tr, C), own_map)]
            + [pl.BlockSpec((p.shape[0], tr, C), lambda i, me_ref: (0, i, 0)) for p in parts],
            out_specs=pl.BlockSpec((tr, C), lambda i, me_ref: (i, 0)),
        ),
        out_shape=jax.ShapeDtypeStruct((R, C), F32),
        compiler_params=pltpu.CompilerParams(dimension_semantics=("arbitrary",), vmem_limit_bytes=VMEM_LIMIT),
    )(me, full, *parts)


_M_CORR = 1.0 - ADAM_B1**ADAM_STEP
_V_CORR = 1.0 - ADAM_B2**ADAM_STEP


def _adamw_math(w, g, m, v):
    m = ADAM_B1 * m + (1.0 - ADAM_B1) * g
    v = ADAM_B2 * v + (1.0 - ADAM_B2) * (g * g)
    delta = -ADAM_LR * ((m / _M_CORR) / (jnp.sqrt(v / _V_CORR) + ADAM_EPS) + ADAM_WD * w)
    return delta, m, v


def _adamw(name, w, m, v, g_here, g_there, g_transposed=False, comm=()):
    R, C = w.shape
    tr = _tile(R, 256, LANES if g_transposed else 8)

    def body(w_ref, m_ref, v_ref, ga_ref, gb_ref, g_ref, d_ref, nm_ref, nv_ref):
        g = ga_ref[...] + gb_ref[...]
        if g_transposed:
            g = g.T
        g_ref[...] = g
        d_ref[...], nm_ref[...], nv_ref[...] = _adamw_math(w_ref[...], g, m_ref[...], v_ref[...])

    spec = pl.BlockSpec((tr, C), lambda i: (i, 0))
    gspec = pl.BlockSpec((C, tr), lambda i: (0, i)) if g_transposed else spec
    return _call(name, body, (R // tr,), [spec] * 3 + [gspec] * 2, [spec] * 4, [jax.ShapeDtypeStruct((R, C), F32)] * 4,
                 (w, m, v, g_here, g_there), comm=comm)


def _adamw_on_sparsecore(name, w, m, v, g_here, g_there):
    R, C = w.shape
    n_groups = R // SUBLANES
    n_turns = -(-n_groups // SC_TILES)
    n_in, n_out = 5, 4

    def body(w_hbm, m_hbm, v_hbm, ga_hbm, gb_hbm, g_out, d_out, nm_out, nv_out, bufs, sems):
        tile = lax.axis_index("subcore") * SC_CORES + lax.axis_index("sparsecore")
        srcs = (w_hbm, m_hbm, v_hbm, ga_hbm, gb_hbm)
        dsts = (d_out, nm_out, nv_out, g_out)

        def rows(turn):
            return pl.ds((tile + turn * SC_TILES) * SUBLANES, SUBLANES)

        def loads(turn):
            slot = turn % 2
            return [pltpu.make_async_copy(srcs[q].at[rows(turn), :], bufs.at[slot, q], sems.at[slot, q]) for q in range(n_in)]

        def stores(turn):
            slot = turn % 2
            return [pltpu.make_async_copy(bufs.at[slot, q], dsts[q].at[rows(turn), :], sems.at[slot, n_in + q])
                    for q in range(n_out)]

        def when_mine(turn, fn):
            pl.when(tile + turn * SC_TILES < n_groups)(fn)

        def compute(slot):
            wb, mb, vb, gab, gbb = (bufs.at[slot, q] for q in range(n_in))

            @pl.loop(0, SUBLANES)
            def _(r):
                @pl.loop(0, C, step=SC_LANES)
                def _(i):
                    at = (r, pl.ds(i, SC_LANES))
                    g = gab[at] + gbb[at]
                    delta, new_m, new_v = _adamw_math(wb[at], g, mb[at], vb[at])
                    gab[at], wb[at], mb[at], vb[at] = g, delta, new_m, new_v

        def start_loads(turn):
            def fn():
                for cp in loads(turn):
                    cp.start()

            when_mine(turn, fn)

        start_loads(0)
        for turn in range(n_turns):
            def step(turn=turn):
                for cp in loads(turn):
                    cp.wait()
                if turn >= 1:
                    for cp in stores(turn - 1):
                        cp.wait()
                if turn + 1 < n_turns:
                    start_loads(turn + 1)
                compute(turn % 2)
                for cp in stores(turn):
                    cp.start()

            when_mine(turn, step)
        for turn in range(n_turns):
            def drain(turn=turn):
                for cp in stores(turn):
                    cp.wait()

            last_mine = jnp.logical_and(tile + turn * SC_TILES < n_groups, tile + (turn + 1) * SC_TILES >= n_groups)
            pl.when(last_mine)(drain)

    return pl.kernel(
        body,
        name=name,
        out_type=[jax.ShapeDtypeStruct((R, C), F32)] * 4,
        mesh=plsc.VectorSubcoreMesh(core_axis_name="sparsecore", subcore_axis_name="subcore"),
        scratch_types=[pltpu.VMEM((2, n_in, SUBLANES, C), F32), pltpu.SemaphoreType.DMA((2, n_in + n_out))],
        compiler_params=pltpu.CompilerParams(use_tc_tiling_on_sc=True),
    )(w, m, v, g_here, g_there)


class _PackLayout:
    def __init__(self, n_cc, n_grp, G, widths):
        self.dw_rows = (0, HALO)
        self.wp_rows = (HALO, HALO + G)
        self.n_cc, self.n_grp, self.G = n_cc, n_grp, G
        self.vec = {}
        r = HALO + G
        for name, width in widths:
            self.vec[name] = (r, width)
            r += width // PACK_W
        self.rows = -(-r // 8) * 8


def _pack_small(layout, dwdw, dwp, vecs):
    names = list(vecs)

    def body(*refs):
        dw_ref, wp_ref = refs[0], refs[1]
        vec_refs = refs[2 : 2 + len(names)]
        o_ref = refs[-1]
        o_ref[...] = jnp.zeros_like(o_ref)
        for j in range(layout.n_cc):
            o_ref[layout.dw_rows[0] : layout.dw_rows[1], j * LANES : (j + 1) * LANES] = dw_ref[j]
        for i in range(layout.n_grp):
            o_ref[layout.wp_rows[0] : layout.wp_rows[1], i * layout.G : (i + 1) * layout.G] = wp_ref[i]
        for name, ref in zip(names, vec_refs):
            r, width = layout.vec[name]
            for h in range(width // PACK_W):
                o_ref[r + h : r + h + 1, :] = ref[:, h * PACK_W : (h + 1) * PACK_W]

    return pl.pallas_call(
        body,
        name="pack_small",
        out_shape=jax.ShapeDtypeStruct((layout.rows, PACK_W), F32),
    )(dwdw, dwp, *[vecs[k] for k in names])


def _adamw_small(layout, g_here, g_there, w_dw, m_dw, v_dw, w_pool, m_pool, v_pool, vec_w, vec_m, vec_v):
    names = list(vec_w)
    nv = len(names)

    def body(*refs):
        ga_ref, gb_ref = refs[0], refs[1]
        wdw, mdw, vdw, wp, mp, vp = refs[2:8]
        vw, vm, vv = refs[8 : 8 + nv], refs[8 + nv : 8 + 2 * nv], refs[8 + 2 * nv : 8 + 3 * nv]
        outs = refs[8 + 3 * nv :]
        acc = outs[-1]
        acc[...] = ga_ref[...] + gb_ref[...]

        def emit(o, g, w, m, v, idx=()):
            res = (g,) + _adamw_math(w, g, m, v)
            for ref, val in zip(o, res):
                ref[idx] = val

        me = 2 * lax.axis_index("x") + lax.axis_index("y")
        for j in range(layout.n_cc):

            @pl.when(me == j)
            def _(j=j):
                for k in range(wdw.shape[0]):
                    g = acc[layout.dw_rows[0] + k : layout.dw_rows[0] + k + 1, j * LANES : (j + 1) * LANES]
                    emit(outs[0:4], g, wdw[k], mdw[k], vdw[k], idx=k)

        for i in range(layout.n_grp):
            g = acc[layout.wp_rows[0] : layout.wp_rows[1], i * layout.G : (i + 1) * layout.G]
            emit(outs[4:8], g, wp[i], mp[i], vp[i], idx=i)
        for q, name in enumerate(names):
            r, width = layout.vec[name]
            for h in range(width // PACK_W):
                ls = slice(h * PACK_W, (h + 1) * PACK_W)
                g = acc[r + h : r + h + 1, :]
                emit(outs[8 + 4 * q : 12 + 4 * q], g, vw[q][:, ls], vm[q][:, ls], vv[q][:, ls], idx=(slice(None), ls))

    shapes = [w_dw.shape] * 4 + [w_pool.shape] * 4
    for name in names:
        shapes += [vec_w[name].shape] * 4
    return pl.pallas_call(
        body,
        name="adamw_small",
        out_shape=[jax.ShapeDtypeStruct(s, F32) for s in shapes],
        scratch_shapes=[pltpu.VMEM(g_here.shape, F32)],
    )(g_here, g_there, w_dw, m_dw, v_dw, w_pool, m_pool, v_pool,
      *[vec_w[k] for k in names], *[vec_m[k] for k in names], *[vec_v[k] for k in names])


def _allreduce_adamw_row(g_part, w, m, v, loss_part, comm=()):
    D = w.shape[1]
    n_pairs = N_DEV - 1

    def body(g_ref, w_ref, m_ref, v_ref, l_ref, go_ref, d_ref, nm_ref, nv_ref, lo_ref, land_g, land_l, sems):
        x, y, c = _place()
        copies = []
        for q, (src, land) in enumerate(((g_ref, land_g), (l_ref, land_l))):
            for r in range(1, N_DEV):
                fx, fy, fc = (r >> 2) & 1, (r >> 1) & 1, r & 1
                peer = (1 - x if fx else x, 1 - y if fy else y, 1 - c if fc else c)
                cp = _remote(src, land.at[r], sems, 2 * (q * n_pairs + r - 1), peer)
                cp.start()
                copies.append(cp)
        for cp in copies:
            cp.wait()

        def total(src, land):
            row = lambda r: src[...] if r == 0 else land[r]
            return ((row(0) + row(4)) + (row(2) + row(6))) + ((row(1) + row(5)) + (row(3) + row(7)))

        g = total(g_ref, land_g)
        go_ref[...] = g
        d_ref[...], nm_ref[...], nv_ref[...] = _adamw_math(w_ref[...], g, m_ref[...], v_ref[...])
        lo_ref[...] = total(l_ref, land_l)

    vm = pl.BlockSpec(memory_space=pltpu.VMEM)
    return _call(
        "allreduce_adamw_g_mix",
        body,
        (),
        [vm] * 5,
        [vm] * 5,
        [jax.ShapeDtypeStruct((1, D), F32)] * 4 + [jax.ShapeDtypeStruct(loss_part.shape, F32)],
        (g_part, w, m, v, loss_part),
        scratch=[pltpu.VMEM((N_DEV, 1, D), F32), pltpu.VMEM((N_DEV,) + loss_part.shape, F32),
                 pltpu.SemaphoreType.DMA((4 * n_pairs,))],
        comm=comm,
    )


def kernel(x, g_mix, w_in, b_in, w_dw, b_dw, ln_g, ln_b, w_pool, s_pool, w_out, g_ffn, w_gate, w_up, w_down, g_final, loss_target, m_g_mix, m_w_in, m_b_in, m_w_dw, m_b_dw, m_ln_g, m_ln_b, m_w_pool, m_s_pool, m_w_out, m_g_ffn, m_w_gate, m_w_up, m_w_down, m_g_final, v_g_mix, v_w_in, v_b_in, v_w_dw, v_b_dw, v_ln_g, v_ln_b, v_w_pool, v_s_pool, v_w_out, v_g_ffn, v_w_gate, v_w_up, v_w_down, v_g_final):
    x2 = x[0]
    target = loss_target[0]
    T, D = x2.shape
    w_in2, w_out2, w_down2 = w_in[0], w_out[0], w_down[0]
    taps_first = lambda a: jnp.transpose(a, (1, 0, 2))
    w_dw3 = taps_first(w_dw)
    w_gateT, w_upT = w_gate[0].T, w_up[0].T
    CI = w_in2.shape[1] * N_CHIPS
    DM = w_out2.shape[0] * N_CHIPS
    F = w_down2.shape[0] * N_CHIPS
    KW, _, dw_cols = w_dw3.shape
    assert dw_cols == LANES
    n_grp, G = w_pool.shape[1], w_pool.shape[-1]
    w_pool3 = w_pool[0]
    g_final2 = g_final.reshape(1, D)

    me = (2 * lax.axis_index("x") + lax.axis_index("y")).astype(jnp.int32).reshape(1)

    w_inT_b, w_dw4, f_out, f_gate, f_up, f_down = _place_and_gather(
        [(w_in2, "rows", (CI, D), BF16, True, True), (w_dw3, "lead", (N_CHIPS, KW, 1, dw_cols), F32, False, False)],
        [(w, "rows", shape, BF16, False, True)
         for w, shape in ((w_out2, (DM, D)), (w_gateT, (F, D)), (w_upT, (F, D)), (w_down2, (F, D)))])
    w_pool_b = w_pool3.astype(BF16)
    (z, xn_b), (f_out, f_gate) = _in_proj(
        x2, g_mix, w_inT_b, b_in,
        comm=[_GatherIci([f_out], ["rows"], [True]), _GatherIci([f_gate], ["rows"], [True], which=(2,))])
    (y_b, v), (w_out_b, f_gate, f_up) = _seq_fwd(
        z, w_dw4, b_dw, ln_g, ln_b, w_pool_b, s_pool,
        comm=[_GatherD2d([f_out], ["rows"]), _GatherIci([f_gate], ["rows"], [True], which=(0, 1)),
              _GatherIci([f_up], ["rows"], [True])])
    (h1, hn_b), (wgT_b, wuT_b, f_down) = _out_proj(
        y_b, x2, w_out_b, g_ffn,
        comm=[_GatherD2d([f_gate, f_up], ["rows"] * 2), _GatherIci([f_down], ["rows"], [True])])
    (g_b, u_b, a_b), (wd_b,) = _gate_up(hn_b, wgT_b, wuT_b, comm=[_GatherD2d([f_down], ["rows"])])
    (dh2, dh2_b, loss_part, d_g_final), _ = _down_loss(a_b, wd_b, h1, target, g_final2)

    gw_down, _ = _weight_grad("grad_w_down", a_b, dh2_b)
    (dg_b, du_b), (p_down_xy,) = _ffn_bwd_act(dh2_b, wd_b, g_b, u_b, comm=[_Scatter([gw_down], ["rows"], which=(0, 1))])
    gw_gateT, (p_down_d,) = _weight_grad("grad_w_gate", dg_b, hn_b, comm=[_Scatter([gw_down], ["rows"], which=(2,))])
    gw_upT, _ = _weight_grad("grad_w_up", du_b, hn_b)
    sum_down = _sum_parts("sum_w_down", gw_down, "rows", [p_down_xy, p_down_d], me)
    (dh1, dh1_b, dy, d_g_ffn), (p_gate, oth_down) = _ffn_bwd_in(
        dg_b, du_b, wgT_b, wuT_b, h1, dh2, g_ffn, w_out_b, comm=[_Scatter([gw_gateT], ["rows"]), _Swap([sum_down])])
    gw_out, _ = _weight_grad("grad_w_out", y_b, dh1_b)
    sum_gate = _sum_parts("sum_w_gate", gw_gateT, "rows", [p_gate], me)
    res = {}
    res["w_down"] = _adamw_on_sparsecore("adamw_w_down", w_down2, m_w_down[0], v_w_down[0], sum_down, oth_down)
    (dz_b, d_wdw, d_bdw, d_lng, d_lnb, d_wp, d_sp, d_bin), (p_up, p_out, oth_gate) = _seq_bwd(
        z, dy, v, w_dw4, ln_g, ln_b, w_pool_b, s_pool,
        comm=[_Scatter([gw_upT, gw_out], ["rows", "rows"]), _Swap([sum_gate])])
    vec_grads = {"b_dw": d_bdw, "ln_g": d_lng, "ln_b": d_lnb, "s_pool": d_sp, "g_ffn": d_g_ffn, "g_final": d_g_final, "b_in": d_bin}
    layout = _PackLayout(dw_cols * N_CHIPS // LANES, n_grp, G, [(k, a.shape[1]) for k, a in vec_grads.items()])
    pack = _pack_small(layout, d_wdw, d_wp, vec_grads)
    sum_up = _sum_parts("sum_w_up", gw_upT, "rows", [p_up], me)
    sum_out = _sum_parts("sum_w_out", gw_out, "rows", [p_out], me)
    gw_inT, (p_small, oth_up, oth_out) = _weight_grad(
        "grad_w_in", dz_b, xn_b, comm=[_Scatter([pack], ["all"]), _Swap([sum_up, sum_out])])
    sum_small = _sum_parts("sum_small", pack, "all", [p_small], me)
    res["w_gate"] = _adamw_on_sparsecore("adamw_w_gate", w_gateT, m_w_gate[0].T, v_w_gate[0].T, sum_gate, oth_gate)
    n_tiles = T // _tile(T, 512)
    assert n_tiles >= 2
    most, (p_in, oth_small) = _in_proj_bwd(
        "in_proj_bwd", dz_b, w_inT_b, x2, dh1, g_mix, (0, n_tiles - 1),
        comm=[_Scatter([gw_inT], ["rows"]), _Swap([sum_small])])
    (grad_x, d_g_mix), _ = _in_proj_bwd("in_proj_bwd_last", dz_b, w_inT_b, x2, dh1, g_mix, (n_tiles - 1, 1), carry=most)
    res["w_up"] = _adamw_on_sparsecore("adamw_w_up", w_upT, m_w_up[0].T, v_w_up[0].T, sum_up, oth_up)
    res["w_out"] = _adamw_on_sparsecore("adamw_w_out", w_out2, m_w_out[0], v_w_out[0], sum_out, oth_out)
    sum_in = _sum_parts("sum_w_in", gw_inT, "rows", [p_in], me)
    (*res["g_mix"], loss_row), (oth_in,) = _allreduce_adamw_row(
        d_g_mix, g_mix, m_g_mix, v_g_mix, loss_part, comm=[_Swap([sum_in])])
    loss = loss_row[0, 0]
    res["w_in"], _ = _adamw("adamw_w_in", w_in2, m_w_in[0], v_w_in[0], sum_in, oth_in, g_transposed=True)

    vec_w = {"b_dw": b_dw, "ln_g": ln_g, "ln_b": ln_b, "s_pool": s_pool, "g_ffn": g_ffn, "g_final": g_final2, "b_in": b_in}
    vec_m = {"b_dw": m_b_dw, "ln_g": m_ln_g, "ln_b": m_ln_b, "s_pool": m_s_pool, "g_ffn": m_g_ffn,
             "g_final": m_g_final.reshape(1, D), "b_in": m_b_in}
    vec_v = {"b_dw": v_b_dw, "ln_g": v_ln_g, "ln_b": v_ln_b, "s_pool": v_s_pool, "g_ffn": v_g_ffn,
             "g_final": v_g_final.reshape(1, D), "b_in": v_b_in}
    small = _adamw_small(layout, sum_small, oth_small, w_dw3, taps_first(m_w_dw), taps_first(v_w_dw),
                         w_pool3, m_w_pool[0], v_w_pool[0], vec_w, vec_m, vec_v)
    res["w_dw"] = [taps_first(a) for a in small[0:4]]
    res["w_pool"] = [a[None] for a in small[4:8]]
    for q, k in enumerate(vec_w):
        res[k] = list(small[8 + 4 * q : 12 + 4 * q])
    res["g_final"] = [a.reshape(D) for a in res["g_final"]]
    for k in ("w_in", "w_out", "w_down"):
        res[k] = [a[None] for a in res[k]]
    for k in ("w_gate", "w_up"):
        res[k] = [a.T[None] for a in res[k]]

    order = ["g_mix", "w_in", "b_in", "w_dw", "b_dw", "ln_g", "ln_b", "w_pool", "s_pool", "w_out", "g_ffn", "w_gate", "w_up", "w_down", "g_final"]
    outs = [loss, grad_x[None]]
    for q in range(4):
        outs += [res[k][q] for k in order]
    return tuple(outs)
```

```python
import jax
import jax.numpy as jnp
from jax import lax
from jax.experimental import pallas as pl
from jax.experimental.pallas import tpu as pltpu
from jax.experimental.pallas import tpu_sc as plsc

F32 = jnp.float32
BF16 = jnp.bfloat16
MESH = pl.DeviceIdType.MESH
ANY = pl.BlockSpec(memory_space=pl.ANY)

RMS_EPS = 1e-6
LN_EPS = 1e-5
POOL_WINDOWS = (2, 4, 8, 16)
ADAM_LR = 0.001
ADAM_B1 = 0.9
ADAM_B2 = 0.999
ADAM_EPS = 1e-08
ADAM_WD = 0.01
ADAM_STEP = 10

LANES = 128
SUBLANES = 8
HALO = 32
CONV_ROWS = 64
HIDDEN_CHUNK = 512
VMEM_LIMIT = 56 * 1024 * 1024
PACK_W = 512
N_CHIPS = 4
N_DEV = 8
SC_CORES = 2
SC_TILES = 32
SC_LANES = 16


def _tile(n, want, mult=8):
    t = min(n, want)
    while n % t or t % mult:
        t -= 1
    return t


def _sigmoid(x):
    return 1.0 / (1.0 + jnp.exp(-x))


def _dot(a, b, dims):
    return lax.dot_general(a, b, (dims, ((), ())), preferred_element_type=F32)


NN = ((1,), (0,))
NT = ((1,), (1,))
TN = ((0,), (0,))


def _rms_bwd(x, g, dy):
    r = lax.rsqrt(jnp.mean(x * x, axis=-1, keepdims=True) + RMS_EPS)
    xh = x * r
    gy = dy * g
    dx = r * (gy - xh * jnp.mean(gy * xh, axis=-1, keepdims=True))
    return dx, dy * xh


def _accumulate(ref, first, val):
    @pl.when(first)
    def _():
        ref[...] = val

    @pl.when(jnp.logical_not(first))
    def _():
        ref[...] += val


def _place():
    return lax.axis_index("x"), lax.axis_index("y"), lax.axis_index("c")


def _other_chips(x, y):
    return [(1 - x, y), (x, 1 - y), (1 - x, 1 - y)]


def _rows(ref, start, n):
    return ref.at[pl.ds(pl.multiple_of(start, 16), n)]


def _window(ref, how, k, c=None):
    if how == "all":
        return ref
    if how == "lead":
        return ref.at[k]
    if how == "rows":
        n = ref.shape[0] // N_CHIPS
        if c is None:
            return _rows(ref, k * n, n)
        return _rows(ref, k * n + c * (n // 2), n // 2)
    n = ref.shape[1] // N_CHIPS
    cols = pl.ds(pl.multiple_of(k * n, LANES), n)
    if c is None:
        return ref.at[:, cols]
    h = ref.shape[0] // 2
    return ref.at[pl.ds(pl.multiple_of(c * h, 16), h), cols]


def _remote(src, dst, sems, s, device):
    return pltpu.make_async_remote_copy(
        src_ref=src, dst_ref=dst, send_sem=sems.at[s], recv_sem=sems.at[s + 1], device_id=device, device_id_type=MESH)


class _GatherIci:
    aliased = True

    def __init__(self, fulls, hows, splits, which=(0, 1, 2)):
        self.fulls, self.hows, self.splits, self.which = list(fulls), list(hows), list(splits), tuple(which)

    def inputs(self):
        return self.fulls

    def out_shapes(self):
        return [jax.ShapeDtypeStruct(a.shape, a.dtype) for a in self.fulls]

    def n_sems(self):
        return 6 * len(self.fulls)

    def build(self, ins, outs, sems, base):
        x, y, c = _place()
        me = 2 * x + y
        chips = _other_chips(x, y)
        starts, waits = [], []
        for a, (how, sp) in enumerate(zip(self.hows, self.splits)):
            half = c if sp else None
            mine = _window(outs[a], how, me, half)
            for j in self.which:
                px, py = chips[j]
                s = base + 6 * a + 2 * j
                cp = _remote(mine, mine, sems, s, (px, py, c))
                landing = _remote(mine, _window(outs[a], how, 2 * px + py, half), sems, s, (px, py, c))
                starts.append(cp.start)
                waits += [landing.wait_recv, cp.wait_send]
        return starts, waits


class _GatherD2d:
    aliased = True

    def __init__(self, fulls, hows):
        self.fulls, self.hows = list(fulls), list(hows)

    def inputs(self):
        return self.fulls

    def out_shapes(self):
        return [jax.ShapeDtypeStruct(a.shape, a.dtype) for a in self.fulls]

    def n_sems(self):
        return 6 * len(self.fulls)

    def build(self, ins, outs, sems, base):
        x, y, c = _place()
        starts, waits = [], []
        for a, how in enumerate(self.hows):
            for j, (px, py) in enumerate(_other_chips(x, y)):
                s = base + 6 * a + 2 * j
                got = _window(outs[a], how, 2 * px + py, c)
                cp = _remote(got, got, sems, s, (x, y, 1 - c))
                landing = _remote(got, _window(outs[a], how, 2 * px + py, 1 - c), sems, s, (x, y, 1 - c))
                starts.append(cp.start)
                waits += [landing.wait_recv, cp.wait_send]
        return starts, waits


def _part_shape(a, how):
    if how == "all":
        return a.shape
    if how == "rows":
        return (a.shape[0] // N_CHIPS, a.shape[1])
    return (a.shape[0], a.shape[1] // N_CHIPS)


class _Scatter:
    aliased = False

    def __init__(self, fulls, hows, which=(0, 1, 2)):
        self.fulls, self.hows, self.which = list(fulls), list(hows), tuple(which)

    def inputs(self):
        return self.fulls

    def out_shapes(self):
        return [jax.ShapeDtypeStruct((len(self.which),) + _part_shape(a, h), a.dtype) for a, h in zip(self.fulls, self.hows)]

    def n_sems(self):
        return 6 * len(self.fulls)

    def build(self, ins, outs, sems, base):
        x, y, c = _place()
        chips = _other_chips(x, y)
        starts, waits = [], []
        for a, how in enumerate(self.hows):
            for slot, j in enumerate(self.which):
                px, py = chips[j]
                cp = _remote(_window(ins[a], how, 2 * px + py), outs[a].at[slot], sems, base + 6 * a + 2 * j, (px, py, c))
                starts.append(cp.start)
                waits += [cp.wait_recv, cp.wait_send]
        return starts, waits


class _Swap:
    aliased = False

    def __init__(self, arrays):
        self.arrays = list(arrays)

    def inputs(self):
        return self.arrays

    def out_shapes(self):
        return [jax.ShapeDtypeStruct(a.shape, a.dtype) for a in self.arrays]

    def n_sems(self):
        return 2 * len(self.arrays)

    def build(self, ins, outs, sems, base):
        x, y, c = _place()
        starts, waits = [], []
        for a in range(len(ins)):
            cp = _remote(ins[a], outs[a], sems, base + 2 * a, (x, y, 1 - c))
            starts.append(cp.start)
            waits += [cp.wait_recv, cp.wait_send]
        return starts, waits


def _call(name, body, grid, in_specs, out_specs, out_shape, args, scratch=(), comm=()):
    comm = list(comm)
    n_in, n_out, n_scr = len(args), len(out_shape), len(scratch)
    c_in = [a for op in comm for a in op.inputs()]
    c_out = [s for op in comm for s in op.out_shapes()]
    n_sems = sum(op.n_sems() for op in comm)
    aliases, i_in, i_out = {}, 0, 0
    for op in comm:
        if op.aliased:
            for q in range(len(op.inputs())):
                aliases[n_in + i_in + q] = n_out + i_out + q
        i_in, i_out = i_in + len(op.inputs()), i_out + len(op.out_shapes())

    def wrapped(*refs):
        ins = refs[:n_in]
        cin = refs[n_in : n_in + len(c_in)]
        o0 = n_in + len(c_in)
        outs = refs[o0 : o0 + n_out]
        cout = refs[o0 + n_out : o0 + n_out + len(c_out)]
        s0 = o0 + n_out + len(c_out)
        scr = refs[s0 : s0 + n_scr]

        def copies():
            sems = refs[s0 + n_scr]
            starts, waits = [], []
            i_in = i_out = base = 0
            for op in comm:
                ni, no = len(op.inputs()), len(op.out_shapes())
                s, w = op.build(cin[i_in : i_in + ni], cout[i_out : i_out + no], sems, base)
                starts += s
                waits += w
                i_in, i_out, base = i_in + ni, i_out + no, base + op.n_sems()
            return starts, waits

        def run_starts():
            for start in copies()[0]:
                start()

        def run_waits():
            for wait in copies()[1]:
                wait()

        if comm and grid:
            first = last = True
            for d, n in enumerate(grid):
                first = jnp.logical_and(first, pl.program_id(d) == 0)
                last = jnp.logical_and(last, pl.program_id(d) == n - 1)
            pl.when(first)(run_starts)
        elif comm:
            run_starts()
        if body is not None:
            body(*ins, *outs, *scr)
        if comm and grid:
            pl.when(last)(run_waits)
        elif comm:
            run_waits()

    res = pl.pallas_call(
        wrapped,
        name=name,
        grid=grid,
        in_specs=list(in_specs) + [ANY] * len(c_in),
        out_specs=list(out_specs) + [ANY] * len(c_out),
        out_shape=list(out_shape) + c_out,
        scratch_shapes=list(scratch) + ([pltpu.SemaphoreType.DMA((n_sems,))] if comm else []),
        input_output_aliases=aliases,
        compiler_params=pltpu.CompilerParams(dimension_semantics=("arbitrary",) * len(grid), vmem_limit_bytes=VMEM_LIMIT),
    )(*args, *c_in)
    return tuple(res[:n_out]), tuple(res[n_out:])


def _place_and_gather(now, later):
    items = list(now) + list(later)
    n, n_now = len(items), len(now)
    buf_shape = lambda it: it[0].shape[::-1] if it[4] else it[0].shape
    split_now = [a for a in range(n_now) if items[a][5]]

    def body(*refs):
        ins, outs = refs[:n], refs[n : 2 * n]
        stage, bufs = refs[2 * n : 3 * n - n_now], refs[3 * n - n_now : 4 * n - n_now]
        sems = refs[4 * n - n_now]
        x, y, c = _place()
        me = 2 * x + y
        chips = _other_chips(x, y)
        loads = [pltpu.make_async_copy(ins[a], stage[a - n_now], sems.at[a]) for a in range(n_now, n)]
        for ld in loads:
            ld.start()
        pending = []

        def place(a, val):
            _, how, _, dtype, transposed, _ = items[a]
            bufs[a][...] = (val.T if transposed else val).astype(dtype)
            cp = pltpu.make_async_copy(bufs[a], _window(outs[a], how, me), sems.at[n + a])
            cp.start()
            pending.append(cp.wait)

        arrivals = []
        for a in range(n_now):
            place(a, ins[a][...])
            how, split = items[a][1], items[a][5]
            half = c if split else None
            src = _rows(bufs[a], c * (bufs[a].shape[0] // 2), bufs[a].shape[0] // 2) if split else bufs[a]
            for j, (px, py) in enumerate(chips):
                s = 2 * n + 6 * a + 2 * j
                cp = _remote(src, _window(outs[a], how, me, half), sems, s, (px, py, c))
                landing = _remote(src, _window(outs[a], how, 2 * px + py, half), sems, s, (px, py, c))
                cp.start()
                arrivals.append(landing.wait_recv)
                pending.append(cp.wait_send)
        for a in range(n_now, n):
            loads[a - n_now].wait()
            place(a, stage[a - n_now][...])
        for wait in arrivals:
            wait()
        d2d = _GatherD2d([None] * len(split_now), [items[a][1] for a in split_now])
        starts, waits = d2d.build(None, [outs[a] for a in split_now], sems, 2 * n + 6 * n_now)
        for start in starts:
            start()
        for wait in waits + pending:
            wait()

    vm = pl.BlockSpec(memory_space=pltpu.VMEM)
    return pl.pallas_call(
        body,
        name="place_and_gather",
        in_specs=[vm] * n_now + [ANY] * (n - n_now),
        out_specs=[ANY] * n,
        out_shape=[jax.ShapeDtypeStruct(it[2], it[3]) for it in items],
        scratch_shapes=[pltpu.VMEM(it[0].shape, it[0].dtype) for it in later]
        + [pltpu.VMEM(buf_shape(it), it[3]) for it in items]
        + [pltpu.SemaphoreType.DMA((2 * n + 6 * n_now + 6 * len(split_now),))],
        compiler_params=pltpu.CompilerParams(vmem_limit_bytes=VMEM_LIMIT),
    )(*[it[0] for it in items])


_HBM = pl.BlockSpec(memory_space=pltpu.HBM)
_SEM = pl.BlockSpec(memory_space=pltpu.SEMAPHORE)
_DATAFLOW = pltpu.SideEffectType.DATAFLOW_SIDE_EFFECTING


def _late_copies(gw_ref, small_ref, parts_ref, oth_ref, send_sems, recv_sems):
    x, y, c = _place()
    copies = [
        pltpu.make_async_remote_copy(
            src_ref=_window(gw_ref, "rows", 2 * px + py), dst_ref=parts_ref.at[j], send_sem=send_sems.at[j],
            recv_sem=recv_sems.at[j], device_id=(px, py, c), device_id_type=MESH)
        for j, (px, py) in enumerate(_other_chips(x, y))
    ]
    copies.append(pltpu.make_async_remote_copy(
        src_ref=small_ref, dst_ref=oth_ref, send_sem=send_sems.at[3], recv_sem=recv_sems.at[3],
        device_id=(x, y, 1 - c), device_id_type=MESH))
    return copies


def _late_exchange_start(gw, small):
    part = (gw.shape[0] // N_CHIPS, gw.shape[1])

    def body(gw_ref, small_ref, parts_ref, oth_ref, send_sems, recv_sems, gw_thru, small_thru, parts_thru, oth_thru, token):
        for cp in _late_copies(gw_ref, small_ref, parts_ref, oth_ref, send_sems, recv_sems):
            cp.start()
        token[...] = jnp.zeros_like(token)

    hbm = lambda v: pltpu.with_memory_space_constraint(v, pltpu.HBM)
    return pl.pallas_call(
        body,
        name="late_exchange_start",
        out_shape=(pltpu.SemaphoreType.DMA((4,)), pltpu.SemaphoreType.DMA((4,)), pltpu.HBM(gw.shape, gw.dtype),
                   pltpu.HBM(small.shape, small.dtype), pltpu.HBM((3,) + part, gw.dtype), pltpu.HBM(small.shape, small.dtype),
                   jax.ShapeDtypeStruct((SUBLANES, LANES), F32)),
        in_specs=(_HBM,) * 4,
        out_specs=(_SEM, _SEM, _HBM, _HBM, _HBM, _HBM, pl.BlockSpec(memory_space=pltpu.VMEM)),
        input_output_aliases={0: 2, 1: 3, 2: 4, 3: 5},
        compiler_params=pltpu.CompilerParams(has_side_effects=_DATAFLOW),
    )(hbm(gw), hbm(small), hbm(lax.empty((3,) + part, gw.dtype)), hbm(lax.empty(small.shape, small.dtype)))


def _late_exchange_wait(send_sems, recv_sems, gw, small, parts, oth, after):
    def body(gw_ref, small_ref, parts_ref, oth_ref, send_sems, recv_sems, after_ref, gw_dead, small_dead, parts_out, oth_out):
        for cp in _late_copies(gw_ref, small_ref, parts_ref, oth_ref, send_sems, recv_sems):
            cp.wait_send()
            cp.wait_recv()

    return pl.pallas_call(
        body,
        name="late_exchange_wait",
        out_shape=(pltpu.HBM(gw.shape, gw.dtype), pltpu.HBM(small.shape, small.dtype), pltpu.HBM(parts.shape, parts.dtype),
                   pltpu.HBM(oth.shape, oth.dtype)),
        in_specs=(_HBM, _HBM, _HBM, _HBM, _SEM, _SEM, ANY),
        out_specs=(_HBM,) * 4,
        input_output_aliases={0: 0, 1: 1, 2: 2, 3: 3},
        compiler_params=pltpu.CompilerParams(has_side_effects=_DATAFLOW),
    )(gw, small, parts, oth, send_sems, recv_sems, after)


def _in_proj(x, g_mix, w_inT_b, b_in, comm=()):
    T, D = x.shape
    CI = w_inT_b.shape[0]
    tm = _tile(T, 512)

    def body(x_ref, g_ref, w_ref, b_ref, z_ref, xn_ref):
        xv = x_ref[...]
        r = lax.rsqrt(jnp.mean(xv * xv, axis=-1, keepdims=True) + RMS_EPS)
        xn = (xv * r * g_ref[...]).astype(BF16)
        xn_ref[...] = xn
        z_ref[...] = _dot(xn, w_ref[...], NT) + b_ref[...]

    return _call(
        "in_proj",
        body,
        (T // tm,),
        [
            pl.BlockSpec((tm, D), lambda i: (i, 0)),
            pl.BlockSpec((1, D), lambda i: (0, 0)),
            pl.BlockSpec((CI, D), lambda i: (0, 0)),
            pl.BlockSpec((1, CI), lambda i: (0, 0)),
        ],
        [pl.BlockSpec((tm, CI), lambda i: (i, 0)), pl.BlockSpec((tm, D), lambda i: (i, 0))],
        [jax.ShapeDtypeStruct((T, CI), F32), jax.ShapeDtypeStruct((T, D), BF16)],
        (x, g_mix, w_inT_b, b_in),
        comm=comm,
    )


def _fill_shifted(scr):
    n = scr.shape[1] - SUBLANES
    for s in range(1, SUBLANES):
        scr[s, 0:n, :] = scr[0, s : s + n, :]


def _shifted_rows(scr, off, n, cs):
    s = off % SUBLANES
    return scr[s, off - s : off - s + n, cs]


def _pool_mean_minus_token(p_scr, cs, w, cnt, tt):
    tok = p_scr[HALO : HALO + tt, cs]
    s = tok
    for d in range(1, w):
        s = s + p_scr[HALO - d : HALO - d + tt, cs]
    return s / cnt - tok


def _seq_fwd(z, w_dw4, b_dw, ln_g, ln_b, w_pool_b, s_pool, comm=()):
    T, CI = z.shape
    CC = ln_g.shape[1]
    n_grp, G = w_pool_b.shape[0], w_pool_b.shape[-1]
    KW = w_dw4.shape[1]
    D = CC + n_grp * G
    tt = _tile(T, 512, HALO)
    per = tt // HALO

    def body(zc_ref, zp_ref, wdw_ref, bdw_ref, lng_ref, lnb_ref, wp_ref, sp_ref, y_ref, v_ref, u_scr, p_scr):
        i = pl.program_id(0)
        first = i == 0
        u_prev = zp_ref[:, 0:CC] * _sigmoid(zp_ref[:, CC : 2 * CC])
        u_scr[0, 0:HALO, :] = jnp.where(first, 0.0, u_prev)
        p_scr[0:HALO, :] = jnp.where(first, 0.0, zp_ref[:, 2 * CC :])
        u_scr[0, HALO:, :] = zc_ref[:, 0:CC] * _sigmoid(zc_ref[:, CC : 2 * CC])
        p_scr[HALO:, :] = zc_ref[:, 2 * CC :]
        _fill_shifted(u_scr)

        for j in range(CC // LANES):
            cs = slice(LANES * j, LANES * (j + 1))
            for rb in range(tt // CONV_ROWS):
                acc = jnp.zeros((CONV_ROWS, LANES), F32)
                for k in range(KW):
                    off = HALO - (KW - 1) + k + rb * CONV_ROWS
                    acc = acc + _shifted_rows(u_scr, off, CONV_ROWS, cs) * wdw_ref[j, k]
                v_ref[rb * CONV_ROWS : (rb + 1) * CONV_ROWS, cs] = acc + bdw_ref[:, cs]

        v = v_ref[...]
        mu = jnp.mean(v, axis=-1, keepdims=True)
        d = v - mu
        var = jnp.mean(d * d, axis=-1, keepdims=True)
        ln = d * lax.rsqrt(var + LN_EPS) * lng_ref[...] + lnb_ref[...]
        y_ref[:, 0:CC] = (ln * _sigmoid(ln)).astype(BF16)

        tpos = i * tt + lax.broadcasted_iota(jnp.int32, (tt, 1), 0)
        for gi, w in enumerate(POOL_WINDOWS):
            cs = slice(G * gi, G * (gi + 1))
            cnt = jnp.minimum(tpos + 1, w).astype(F32)
            yi = _pool_mean_minus_token(p_scr, cs, w, cnt, tt)
            q = _dot(yi.astype(BF16), wp_ref[gi], NN)
            y_ref[:, CC + G * gi : CC + G * (gi + 1)] = (q * sp_ref[:, cs]).astype(BF16)

    const2 = lambda i: (0, 0)
    return _call(
        "seq_fwd",
        body,
        (T // tt,),
        [
            pl.BlockSpec((tt, CI), lambda i: (i, 0)),
            pl.BlockSpec((HALO, CI), lambda i: (jnp.maximum(i * per - 1, 0), 0)),
            pl.BlockSpec(w_dw4.shape, lambda i: (0,) * w_dw4.ndim),
            pl.BlockSpec((1, CC), const2),
            pl.BlockSpec((1, CC), const2),
            pl.BlockSpec((1, CC), const2),
            pl.BlockSpec(w_pool_b.shape, lambda i: (0, 0, 0)),
            pl.BlockSpec((1, n_grp * G), const2),
        ],
        [pl.BlockSpec((tt, D), lambda i: (i, 0)), pl.BlockSpec((tt, CC), lambda i: (i, 0))],
        [jax.ShapeDtypeStruct((T, D), BF16), jax.ShapeDtypeStruct((T, CC), F32)],
        (z, z, w_dw4, b_dw, ln_g, ln_b, w_pool_b, s_pool),
        scratch=[pltpu.VMEM((SUBLANES, HALO + tt, CC), F32), pltpu.VMEM((HALO + tt, n_grp * G), F32)],
        comm=comm,
    )


def _out_proj(y_b, x, w_out_b, g_ffn, comm=()):
    T, D = x.shape
    tm = _tile(T, 512)

    def body(y_ref, x_ref, w_ref, g_ref, h1_ref, hn_ref):
        h1 = x_ref[...] + _dot(y_ref[...], w_ref[...], NN)
        h1_ref[...] = h1
        r = lax.rsqrt(jnp.mean(h1 * h1, axis=-1, keepdims=True) + RMS_EPS)
        hn_ref[...] = (h1 * r * g_ref[...]).astype(BF16)

    row = lambda i: (i, 0)
    return _call(
        "out_proj",
        body,
        (T // tm,),
        [
            pl.BlockSpec((tm, y_b.shape[1]), row),
            pl.BlockSpec((tm, D), row),
            pl.BlockSpec(w_out_b.shape, lambda i: (0, 0)),
            pl.BlockSpec((1, D), lambda i: (0, 0)),
        ],
        [pl.BlockSpec((tm, D), row), pl.BlockSpec((tm, D), row)],
        [jax.ShapeDtypeStruct((T, D), F32), jax.ShapeDtypeStruct((T, D), BF16)],
        (y_b, x, w_out_b, g_ffn),
        comm=comm,
    )


def _hidden_tile(F):
    return _tile(F, 1408, LANES)


def _gate_up(hn_b, wgT_b, wuT_b, comm=()):
    T, D = hn_b.shape
    F = wgT_b.shape[0]
    tm, tf = _tile(T, 1024), _hidden_tile(F)

    def body(hn_ref, wg_ref, wu_ref, g_ref, u_ref, a_ref):
        hn = hn_ref[...]
        for c0 in range(0, tf, HIDDEN_CHUNK):
            cs = slice(c0, min(c0 + HIDDEN_CHUNK, tf))
            gv = _dot(hn, wg_ref[cs, :], NT)
            uv = _dot(hn, wu_ref[cs, :], NT)
            g_ref[:, cs] = gv.astype(BF16)
            u_ref[:, cs] = uv.astype(BF16)
            a_ref[:, cs] = (gv * _sigmoid(gv) * uv).astype(BF16)

    wspec = pl.BlockSpec((tf, D), lambda j, i: (j, 0))
    ospec = pl.BlockSpec((tm, tf), lambda j, i: (i, j))
    return _call(
        "gate_up",
        body,
        (F // tf, T // tm),
        [pl.BlockSpec((tm, D), lambda j, i: (i, 0)), wspec, wspec],
        [ospec, ospec, ospec],
        [jax.ShapeDtypeStruct((T, F), BF16)] * 3,
        (hn_b, wgT_b, wuT_b),
        comm=comm,
    )


def _down_loss(a_b, wd_b, h1, target, g_final, comm=()):
    T, D = h1.shape
    F = a_b.shape[1]
    tm = _tile(T, 512)
    nt = T // tm

    def body(a_ref, w_ref, h1_ref, t_ref, g_ref, dh2_ref, dh2b_ref, loss_ref, dg_ref):
        i = pl.program_id(0)
        h2 = h1_ref[...] + _dot(a_ref[...], w_ref[...], NN)
        r = lax.rsqrt(jnp.mean(h2 * h2, axis=-1, keepdims=True) + RMS_EPS)
        g = g_ref[...]
        diff = h2 * r * g - t_ref[...]
        _accumulate(loss_ref, i == 0, jnp.full(loss_ref.shape, jnp.sum(diff * diff) * (0.5 / D), F32))
        dh2, dg_rows = _rms_bwd(h2, g, diff * (1.0 / D))
        dh2_ref[...] = dh2
        dh2b_ref[...] = dh2.astype(BF16)
        _accumulate(dg_ref, i == 0, jnp.sum(dg_rows, axis=0, keepdims=True))

    row = lambda i: (i, 0)
    return _call(
        "down_loss",
        body,
        (nt,),
        [
            pl.BlockSpec((tm, F), row),
            pl.BlockSpec((F, D), lambda i: (0, 0), pipeline_mode=pl.Buffered(1)),
            pl.BlockSpec((tm, D), row),
            pl.BlockSpec((tm, D), row),
            pl.BlockSpec((1, D), lambda i: (0, 0)),
        ],
        [
            pl.BlockSpec((tm, D), row),
            pl.BlockSpec((tm, D), row),
            pl.BlockSpec((1, LANES), lambda i: (0, 0)),
            pl.BlockSpec((1, D), lambda i: (0, 0)),
        ],
        [
            jax.ShapeDtypeStruct((T, D), F32),
            jax.ShapeDtypeStruct((T, D), BF16),
            jax.ShapeDtypeStruct((1, LANES), F32),
            jax.ShapeDtypeStruct((1, D), F32),
        ],
        (a_b, wd_b, h1, target, g_final),
        comm=comm,
    )


def _ffn_bwd_act(dh2_b, wd_b, g_b, u_b, comm=()):
    T, D = dh2_b.shape
    F = wd_b.shape[0]
    tm, tf = _tile(T, 1024), _hidden_tile(F)

    def body(d_ref, w_ref, g_ref, u_ref, dg_ref, du_ref):
        d = d_ref[...]
        for c0 in range(0, tf, HIDDEN_CHUNK):
            cs = slice(c0, min(c0 + HIDDEN_CHUNK, tf))
            da = _dot(d, w_ref[cs, :], NT)
            gv = g_ref[:, cs].astype(F32)
            uv = u_ref[:, cs].astype(F32)
            sg = _sigmoid(gv)
            silu = gv * sg
            dg_ref[:, cs] = (da * uv * (sg * (1.0 + gv * (1.0 - sg)))).astype(BF16)
            du_ref[:, cs] = (da * silu).astype(BF16)

    aspec = pl.BlockSpec((tm, tf), lambda j, i: (i, j))
    return _call(
        "ffn_bwd_act",
        body,
        (F // tf, T // tm),
        [pl.BlockSpec((tm, D), lambda j, i: (i, 0)), pl.BlockSpec((tf, D), lambda j, i: (j, 0)), aspec, aspec],
        [aspec, aspec],
        [jax.ShapeDtypeStruct((T, F), BF16)] * 2,
        (dh2_b, wd_b, g_b, u_b),
        comm=comm,
    )


def _ffn_bwd_in(dg_b, du_b, wgT_b, wuT_b, h1, dh2, g_ffn, w_out_b, comm=()):
    T, D = h1.shape
    F = wgT_b.shape[0]
    DM = w_out_b.shape[0]
    tm = _tile(T, 512)

    def body(dg_ref, du_ref, wg_ref, wu_ref, h1_ref, dh2_ref, g_ref, wo_ref, dh1_ref, dh1b_ref, dy_ref, dgf_ref):
        i = pl.program_id(0)
        dhn = _dot(dg_ref[...], wg_ref[...], NN) + _dot(du_ref[...], wu_ref[...], NN)
        dx, dg_rows = _rms_bwd(h1_ref[...], g_ref[...], dhn)
        dh1 = dh2_ref[...] + dx
        dh1b = dh1.astype(BF16)
        dh1_ref[...] = dh1
        dh1b_ref[...] = dh1b
        dy_ref[...] = _dot(dh1b, wo_ref[...], NT)
        _accumulate(dgf_ref, i == 0, jnp.sum(dg_rows, axis=0, keepdims=True))

    row = lambda i: (i, 0)
    const = lambda i: (0, 0)
    return _call(
        "ffn_bwd_in",
        body,
        (T // tm,),
        [
            pl.BlockSpec((tm, F), row),
            pl.BlockSpec((tm, F), row),
            pl.BlockSpec((F, D), const, pipeline_mode=pl.Buffered(1)),
            pl.BlockSpec((F, D), const, pipeline_mode=pl.Buffered(1)),
            pl.BlockSpec((tm, D), row),
            pl.BlockSpec((tm, D), row),
            pl.BlockSpec((1, D), const),
            pl.BlockSpec((DM, D), const, pipeline_mode=pl.Buffered(1)),
        ],
        [pl.BlockSpec((tm, D), row), pl.BlockSpec((tm, D), row), pl.BlockSpec((tm, DM), row), pl.BlockSpec((1, D), const)],
        [
            jax.ShapeDtypeStruct((T, D), F32),
            jax.ShapeDtypeStruct((T, D), BF16),
            jax.ShapeDtypeStruct((T, DM), F32),
            jax.ShapeDtypeStruct((1, D), F32),
        ],
        (dg_b, du_b, wgT_b, wuT_b, h1, dh2, g_ffn, w_out_b),
        comm=comm,
    )


def _seq_bwd(z, dy, v, w_dw4, ln_g, ln_b, w_pool_b, s_pool, comm=()):
    T, CI = z.shape
    CC = ln_g.shape[1]
    n_grp, G = w_pool_b.shape[0], w_pool_b.shape[-1]
    CP = n_grp * G
    KW = w_dw4.shape[1]
    n_cc = CC // LANES
    D = CC + CP
    tt = _tile(T, 512, HALO)
    per = tt // HALO
    n_tiles = T // tt
    last_halo = T // HALO - 1

    def body(zc_ref, zp_ref, dyc_ref, dyn_ref, vc_ref, vn_ref, wdw_ref, lng_ref, lnb_ref, wp_ref, sp_ref,
             dz_ref, dwdw_ref, dbdw_ref, dlng_ref, dlnb_ref, dwp_ref, dsp_ref, dbin_ref,
             dv_scr, u_scr, p_scr, g_scr, dw_scr):
        i = pl.program_id(0)
        first = i == 0
        last = i == n_tiles - 1
        lng, lnb = lng_ref[...], lnb_ref[...]

        def conv_pre(vv, dyc):
            mu = jnp.mean(vv, axis=-1, keepdims=True)
            d = vv - mu
            rs = lax.rsqrt(jnp.mean(d * d, axis=-1, keepdims=True) + LN_EPS)
            xh = d * rs
            ln = xh * lng + lnb
            sg = _sigmoid(ln)
            dln = dyc * (sg * (1.0 + ln * (1.0 - sg)))
            dxh = dln * lng
            dv = rs * (dxh - jnp.mean(dxh, axis=-1, keepdims=True) - xh * jnp.mean(dxh * xh, axis=-1, keepdims=True))
            return dv, dln, xh

        dv_c, dln_c, xh_c = conv_pre(vc_ref[...], dyc_ref[:, 0:CC])
        dv_scr[0, 0:tt, :] = dv_c
        dv_n, _, _ = conv_pre(vn_ref[...], dyn_ref[:, 0:CC])
        dv_scr[0, tt:, :] = jnp.where(last, 0.0, dv_n)
        _fill_shifted(dv_scr)
        _accumulate(dlng_ref, first, jnp.sum(dln_c * xh_c, axis=0, keepdims=True))
        _accumulate(dlnb_ref, first, jnp.sum(dln_c, axis=0, keepdims=True))
        _accumulate(dbdw_ref, first, jnp.sum(dv_c, axis=0, keepdims=True))

        u_scr[...] = zc_ref[:, 0:CC] * _sigmoid(zc_ref[:, CC : 2 * CC])

        @pl.when(first)
        def _():
            dw_scr[...] = jnp.zeros_like(dw_scr)

        for j in range(n_cc):
            cs = slice(LANES * j, LANES * (j + 1))
            gs = slice(CC + LANES * j, CC + LANES * (j + 1))
            dbin_a = jnp.zeros((1, LANES), F32)
            dbin_g = jnp.zeros((1, LANES), F32)
            for rb in range(tt // CONV_ROWS):
                rows = slice(rb * CONV_ROWS, (rb + 1) * CONV_ROWS)
                u_blk = u_scr[rows, cs]
                du = jnp.zeros((CONV_ROWS, LANES), F32)
                for k in range(KW):
                    off = rb * CONV_ROWS + (KW - 1) - k
                    d = _shifted_rows(dv_scr, off, CONV_ROWS, cs)
                    du = du + d * wdw_ref[j, k]
                    dw_scr[j * HALO + k] += jnp.sum((u_blk * d).reshape(CONV_ROWS // 8, 8, LANES), axis=0)
                a = zc_ref[rows, cs]
                sg = _sigmoid(zc_ref[rows, gs])
                da = du * sg
                dgate = du * a * sg * (1.0 - sg)
                dz_ref[rows, cs] = da.astype(BF16)
                dz_ref[rows, gs] = dgate.astype(BF16)
                dbin_a = dbin_a + jnp.sum(da, axis=0, keepdims=True)
                dbin_g = dbin_g + jnp.sum(dgate, axis=0, keepdims=True)
            _accumulate(dbin_ref.at[:, cs], first, dbin_a)
            _accumulate(dbin_ref.at[:, gs], first, dbin_g)

        @pl.when(last)
        def _():
            dwdw_ref[...] = jnp.sum(dw_scr[...], axis=1).reshape(dwdw_ref.shape)

        p_scr[0:HALO, :] = jnp.where(first, 0.0, zp_ref[:, 2 * CC :])
        p_scr[HALO:, :] = zc_ref[:, 2 * CC :]
        tpos = i * tt + lax.broadcasted_iota(jnp.int32, (tt, 1), 0)
        for gi, w in enumerate(POOL_WINDOWS):
            cs = slice(G * gi, G * (gi + 1))
            ys = slice(CC + G * gi, CC + G * (gi + 1))
            ps = slice(2 * CC + G * gi, 2 * CC + G * (gi + 1))
            cnt = jnp.minimum(tpos + 1, w).astype(F32)
            yib = _pool_mean_minus_token(p_scr, cs, w, cnt, tt).astype(BF16)
            wp = wp_ref[gi]
            sp = sp_ref[:, cs]
            dyp = dyc_ref[:, ys]
            q = _dot(yib, wp, NN)
            _accumulate(dsp_ref.at[:, cs], first, jnp.sum(dyp * q, axis=0, keepdims=True))
            dq_c = (dyp * sp).astype(BF16)
            dq_n = (jnp.where(last, 0.0, dyn_ref[:, ys]) * sp).astype(BF16)
            _accumulate(dwp_ref.at[gi], first, _dot(yib, dq_c, TN))
            dyi_c = _dot(dq_c, wp, NT)
            g_scr[0:tt, cs] = dyi_c / cnt
            g_scr[tt:, cs] = _dot(dq_n, wp, NT) * (1.0 / w)
            dp = -dyi_c
            for d in range(w):
                dp = dp + g_scr[d : d + tt, cs]
            dz_ref[:, ps] = dp.astype(BF16)
            _accumulate(dbin_ref.at[:, ps], first, jnp.sum(dp, axis=0, keepdims=True))

    cur = lambda i: (i, 0)
    prev = lambda i: (jnp.maximum(i * per - 1, 0), 0)
    nxt = lambda i: (jnp.minimum((i + 1) * per, last_halo), 0)
    c2 = lambda i: (0, 0)
    c3 = lambda i: (0, 0, 0)
    return _call(
        "seq_bwd",
        body,
        (n_tiles,),
        [
            pl.BlockSpec((tt, CI), cur),
            pl.BlockSpec((HALO, CI), prev),
            pl.BlockSpec((tt, D), cur),
            pl.BlockSpec((HALO, D), nxt),
            pl.BlockSpec((tt, CC), cur),
            pl.BlockSpec((HALO, CC), nxt),
            pl.BlockSpec(w_dw4.shape, lambda i: (0,) * w_dw4.ndim),
            pl.BlockSpec((1, CC), c2),
            pl.BlockSpec((1, CC), c2),
            pl.BlockSpec(w_pool_b.shape, c3),
            pl.BlockSpec((1, CP), c2),
        ],
        [
            pl.BlockSpec((tt, CI), cur),
            pl.BlockSpec((n_cc, HALO, LANES), c3),
            pl.BlockSpec((1, CC), c2),
            pl.BlockSpec((1, CC), c2),
            pl.BlockSpec((1, CC), c2),
            pl.BlockSpec((n_grp, G, G), c3),
            pl.BlockSpec((1, CP), c2),
            pl.BlockSpec((1, CI), c2),
        ],
        [
            jax.ShapeDtypeStruct((T, CI), BF16),
            jax.ShapeDtypeStruct((n_cc, HALO, LANES), F32),
            jax.ShapeDtypeStruct((1, CC), F32),
            jax.ShapeDtypeStruct((1, CC), F32),
            jax.ShapeDtypeStruct((1, CC), F32),
            jax.ShapeDtypeStruct((n_grp, G, G), F32),
            jax.ShapeDtypeStruct((1, CP), F32),
            jax.ShapeDtypeStruct((1, CI), F32),
        ],
        (z, z, dy, dy, v, v, w_dw4, ln_g, ln_b, w_pool_b, s_pool),
        scratch=[
            pltpu.VMEM((SUBLANES, tt + HALO, CC), F32),
            pltpu.VMEM((tt, CC), F32),
            pltpu.VMEM((HALO + tt, CP), F32),
            pltpu.VMEM((tt + HALO, CP), F32),
            pltpu.VMEM((n_cc * HALO, 8, LANES), F32),
        ],
        comm=comm,
    )


def _in_proj_bwd(dz_b, w_inT_b, x, dh1, g_mix, comm=()):
    T, D = x.shape
    CI = w_inT_b.shape[0]
    tm = _tile(T, 512)

    def body(dz_ref, w_ref, x_ref, dh1_ref, g_ref, dx_ref, dg_ref):
        i = pl.program_id(0)
        dxn = _dot(dz_ref[...], w_ref[...], NN)
        dx, dg_rows = _rms_bwd(x_ref[...], g_ref[...], dxn)
        dx_ref[...] = dh1_ref[...] + dx
        _accumulate(dg_ref, i == 0, jnp.sum(dg_rows, axis=0, keepdims=True))

    row = lambda i: (i, 0)
    const = lambda i: (0, 0)
    return _call(
        "in_proj_bwd",
        body,
        (T // tm,),
        [
            pl.BlockSpec((tm, CI), row),
            pl.BlockSpec((CI, D), const),
            pl.BlockSpec((tm, D), row),
            pl.BlockSpec((tm, D), row),
            pl.BlockSpec((1, D), const),
        ],
        [pl.BlockSpec((tm, D), row), pl.BlockSpec((1, D), const)],
        [jax.ShapeDtypeStruct((T, D), F32), jax.ShapeDtypeStruct((1, D), F32)],
        (dz_b, w_inT_b, x, dh1, g_mix),
        comm=comm,
    )


def _weight_grad(name, a_b, b_b, comm=()):
    T, N1 = a_b.shape
    N2 = b_b.shape[1]
    t1 = _tile(N1, 1408, LANES)
    tk = _tile(T, 2048)
    nk = T // tk

    def body(a_ref, b_ref, o_ref, acc):
        k = pl.program_id(1)
        _accumulate(acc, k == 0, _dot(a_ref[...], b_ref[...], TN))

        @pl.when(k == nk - 1)
        def _():
            o_ref[...] = acc[...].astype(BF16)

    (out,), rest = _call(
        name,
        body,
        (N1 // t1, nk),
        [pl.BlockSpec((tk, t1), lambda n, k: (k, n)), pl.BlockSpec((tk, N2), lambda n, k: (k, 0))],
        [pl.BlockSpec((t1, N2), lambda n, k: (n, 0))],
        [jax.ShapeDtypeStruct((N1, N2), BF16)],
        (a_b, b_b),
        scratch=[pltpu.VMEM((t1, N2), F32)],
        comm=comm,
    )
    return out, rest


def _sum_parts(name, full, how, parts, me):
    _, R, C = parts[0].shape
    tr = _tile(R, 512)
    nb = R // tr
    where = [(q, r) for q, p in enumerate(parts) for r in range(p.shape[0])]
    assert len(where) == 3

    def body(me_ref, own_ref, *refs):
        o_ref = refs[-1]
        f = lambda j: refs[where[j][0]][where[j][1]].astype(F32)
        o_ref[...] = (own_ref[...].astype(F32) + f(0)) + (f(1) + f(2))

    own_map = {"rows": lambda i, me_ref: (me_ref[0] * nb + i, 0), "cols": lambda i, me_ref: (i, me_ref[0]),
               "all": lambda i, me_ref: (i, 0)}[how]
    return pl.pallas_call(
        body,
        name=name,
        grid_spec=pltpu.PrefetchScalarGridSpec(
            num_scalar_prefetch=1,
            grid=(nb,),
            in_specs=[pl.BlockSpec((tr, C), own_map)]
            + [pl.BlockSpec((p.shape[0], tr, C), lambda i, me_ref: (0, i, 0)) for p in parts],
            out_specs=pl.BlockSpec((tr, C), lambda i, me_ref: (i, 0)),
        ),
        out_shape=jax.ShapeDtypeStruct((R, C), F32),
        compiler_params=pltpu.CompilerParams(dimension_semantics=("arbitrary",), vmem_limit_bytes=VMEM_LIMIT),
    )(me, full, *parts)


_M_CORR = 1.0 - ADAM_B1**ADAM_STEP
_V_CORR = 1.0 - ADAM_B2**ADAM_STEP


def _adamw_math(w, g, m, v):
    m = ADAM_B1 * m + (1.0 - ADAM_B1) * g
    v = ADAM_B2 * v + (1.0 - ADAM_B2) * (g * g)
    delta = -ADAM_LR * ((m / _M_CORR) / (jnp.sqrt(v / _V_CORR) + ADAM_EPS) + ADAM_WD * w)
    return delta, m, v


def _adamw(name, w, m, v, g_here, g_there, g_transposed=False, comm=()):
    R, C = w.shape
    tr = _tile(R, 256, LANES if g_transposed else 8)

    def body(w_ref, m_ref, v_ref, ga_ref, gb_ref, g_ref, d_ref, nm_ref, nv_ref):
        g = ga_ref[...] + gb_ref[...]
        if g_transposed:
            g = g.T
        g_ref[...] = g
        d_ref[...], nm_ref[...], nv_ref[...] = _adamw_math(w_ref[...], g, m_ref[...], v_ref[...])

    spec = pl.BlockSpec((tr, C), lambda i: (i, 0))
    gspec = pl.BlockSpec((C, tr), lambda i: (0, i)) if g_transposed else spec
    return _call(name, body, (R // tr,), [spec] * 3 + [gspec] * 2, [spec] * 4, [jax.ShapeDtypeStruct((R, C), F32)] * 4,
                 (w, m, v, g_here, g_there), comm=comm)


def _adamw_on_sparsecore(name, w, m, v, g_here, g_there):
    R, C = w.shape
    n_groups = R // SUBLANES
    n_turns = -(-n_groups // SC_TILES)
    n_in, n_out = 5, 4

    def body(w_hbm, m_hbm, v_hbm, ga_hbm, gb_hbm, g_out, d_out, nm_out, nv_out, bufs, sems):
        tile = lax.axis_index("subcore") * SC_CORES + lax.axis_index("sparsecore")
        srcs = (w_hbm, m_hbm, v_hbm, ga_hbm, gb_hbm)
        dsts = (d_out, nm_out, nv_out, g_out)

        def rows(turn):
            return pl.ds((tile + turn * SC_TILES) * SUBLANES, SUBLANES)

        def loads(turn):
            slot = turn % 2
            return [pltpu.make_async_copy(srcs[q].at[rows(turn), :], bufs.at[slot, q], sems.at[slot, q]) for q in range(n_in)]

        def stores(turn):
            slot = turn % 2
            return [pltpu.make_async_copy(bufs.at[slot, q], dsts[q].at[rows(turn), :], sems.at[slot, n_in + q])
                    for q in range(n_out)]

        def when_mine(turn, fn):
            pl.when(tile + turn * SC_TILES < n_groups)(fn)

        def compute(slot):
            wb, mb, vb, gab, gbb = (bufs.at[slot, q] for q in range(n_in))

            @pl.loop(0, SUBLANES)
            def _(r):
                @pl.loop(0, C, step=SC_LANES)
                def _(i):
                    at = (r, pl.ds(i, SC_LANES))
                    g = gab[at] + gbb[at]
                    delta, new_m, new_v = _adamw_math(wb[at], g, mb[at], vb[at])
                    gab[at], wb[at], mb[at], vb[at] = g, delta, new_m, new_v

        def start_loads(turn):
            def fn():
                for cp in loads(turn):
                    cp.start()

            when_mine(turn, fn)

        start_loads(0)
        for turn in range(n_turns):
            def step(turn=turn):
                for cp in loads(turn):
                    cp.wait()
                if turn >= 1:
                    for cp in stores(turn - 1):
                        cp.wait()
                if turn + 1 < n_turns:
                    start_loads(turn + 1)
                compute(turn % 2)
                for cp in stores(turn):
                    cp.start()

            when_mine(turn, step)
        for turn in range(n_turns):
            def drain(turn=turn):
                for cp in stores(turn):
                    cp.wait()

            last_mine = jnp.logical_and(tile + turn * SC_TILES < n_groups, tile + (turn + 1) * SC_TILES >= n_groups)
            pl.when(last_mine)(drain)

    return pl.kernel(
        body,
        name=name,
        out_type=[jax.ShapeDtypeStruct((R, C), F32)] * 4,
        mesh=plsc.VectorSubcoreMesh(core_axis_name="sparsecore", subcore_axis_name="subcore"),
        scratch_types=[pltpu.VMEM((2, n_in, SUBLANES, C), F32), pltpu.SemaphoreType.DMA((2, n_in + n_out))],
        compiler_params=pltpu.CompilerParams(use_tc_tiling_on_sc=True),
    )(w, m, v, g_here, g_there)


class _PackLayout:
    def __init__(self, n_cc, n_grp, G, widths):
        self.dw_rows = (0, HALO)
        self.wp_rows = (HALO, HALO + G)
        self.n_cc, self.n_grp, self.G = n_cc, n_grp, G
        self.vec = {}
        r = HALO + G
        for name, width in widths:
            self.vec[name] = (r, width)
            r += width // PACK_W
        self.rows = -(-r // 8) * 8


def _pack_small(layout, dwdw, dwp, vecs):
    names = list(vecs)

    def body(*refs):
        dw_ref, wp_ref = refs[0], refs[1]
        vec_refs = refs[2 : 2 + len(names)]
        o_ref = refs[-1]
        o_ref[...] = jnp.zeros_like(o_ref)
        for j in range(layout.n_cc):
            o_ref[layout.dw_rows[0] : layout.dw_rows[1], j * LANES : (j + 1) * LANES] = dw_ref[j]
        for i in range(layout.n_grp):
            o_ref[layout.wp_rows[0] : layout.wp_rows[1], i * layout.G : (i + 1) * layout.G] = wp_ref[i]
        for name, ref in zip(names, vec_refs):
            r, width = layout.vec[name]
            for h in range(width // PACK_W):
                o_ref[r + h : r + h + 1, :] = ref[:, h * PACK_W : (h + 1) * PACK_W]

    return pl.pallas_call(
        body,
        name="pack_small",
        out_shape=jax.ShapeDtypeStruct((layout.rows, PACK_W), F32),
    )(dwdw, dwp, *[vecs[k] for k in names])


def _adamw_small(layout, g_here, g_there, w_dw, m_dw, v_dw, w_pool, m_pool, v_pool, vec_w, vec_m, vec_v):
    names = list(vec_w)
    nv = len(names)

    def body(*refs):
        ga_ref, gb_ref = refs[0], refs[1]
        wdw, mdw, vdw, wp, mp, vp = refs[2:8]
        vw, vm, vv = refs[8 : 8 + nv], refs[8 + nv : 8 + 2 * nv], refs[8 + 2 * nv : 8 + 3 * nv]
        outs = refs[8 + 3 * nv :]
        acc = outs[-1]
        acc[...] = ga_ref[...] + gb_ref[...]

        def emit(o, g, w, m, v, idx=()):
            res = (g,) + _adamw_math(w, g, m, v)
            for ref, val in zip(o, res):
                ref[idx] = val

        me = 2 * lax.axis_index("x") + lax.axis_index("y")
        for j in range(layout.n_cc):

            @pl.when(me == j)
            def _(j=j):
                for k in range(wdw.shape[0]):
                    g = acc[layout.dw_rows[0] + k : layout.dw_rows[0] + k + 1, j * LANES : (j + 1) * LANES]
                    emit(outs[0:4], g, wdw[k], mdw[k], vdw[k], idx=k)

        for i in range(layout.n_grp):
            g = acc[layout.wp_rows[0] : layout.wp_rows[1], i * layout.G : (i + 1) * layout.G]
            emit(outs[4:8], g, wp[i], mp[i], vp[i], idx=i)
        for q, name in enumerate(names):
            r, width = layout.vec[name]
            for h in range(width // PACK_W):
                ls = slice(h * PACK_W, (h + 1) * PACK_W)
                g = acc[r + h : r + h + 1, :]
                emit(outs[8 + 4 * q : 12 + 4 * q], g, vw[q][:, ls], vm[q][:, ls], vv[q][:, ls], idx=(slice(None), ls))

    shapes = [w_dw.shape] * 4 + [w_pool.shape] * 4
    for name in names:
        shapes += [vec_w[name].shape] * 4
    return pl.pallas_call(
        body,
        name="adamw_small",
        out_shape=[jax.ShapeDtypeStruct(s, F32) for s in shapes],
        scratch_shapes=[pltpu.VMEM(g_here.shape, F32)],
    )(g_here, g_there, w_dw, m_dw, v_dw, w_pool, m_pool, v_pool,
      *[vec_w[k] for k in names], *[vec_m[k] for k in names], *[vec_v[k] for k in names])


def _allreduce_adamw_row(g_part, w, m, v, loss_part, comm=()):
    D = w.shape[1]
    n_pairs = N_DEV - 1

    def body(g_ref, w_ref, m_ref, v_ref, l_ref, go_ref, d_ref, nm_ref, nv_ref, lo_ref, land_g, land_l, sems):
        x, y, c = _place()
        copies = []
        for q, (src, land) in enumerate(((g_ref, land_g), (l_ref, land_l))):
            for r in range(1, N_DEV):
                fx, fy, fc = (r >> 2) & 1, (r >> 1) & 1, r & 1
                peer = (1 - x if fx else x, 1 - y if fy else y, 1 - c if fc else c)
                cp = _remote(src, land.at[r], sems, 2 * (q * n_pairs + r - 1), peer)
                cp.start()
                copies.append(cp)
        for cp in copies:
            cp.wait()

        def total(src, land):
            row = lambda r: src[...] if r == 0 else land[r]
            return ((row(0) + row(4)) + (row(2) + row(6))) + ((row(1) + row(5)) + (row(3) + row(7)))

        g = total(g_ref, land_g)
        go_ref[...] = g
        d_ref[...], nm_ref[...], nv_ref[...] = _adamw_math(w_ref[...], g, m_ref[...], v_ref[...])
        lo_ref[...] = total(l_ref, land_l)

    vm = pl.BlockSpec(memory_space=pltpu.VMEM)
    return _call(
        "allreduce_adamw_g_mix",
        body,
        (),
        [vm] * 5,
        [vm] * 5,
        [jax.ShapeDtypeStruct((1, D), F32)] * 4 + [jax.ShapeDtypeStruct(loss_part.shape, F32)],
        (g_part, w, m, v, loss_part),
        scratch=[pltpu.VMEM((N_DEV, 1, D), F32), pltpu.VMEM((N_DEV,) + loss_part.shape, F32),
                 pltpu.SemaphoreType.DMA((4 * n_pairs,))],
        comm=comm,
    )


def kernel(x, g_mix, w_in, b_in, w_dw, b_dw, ln_g, ln_b, w_pool, s_pool, w_out, g_ffn, w_gate, w_up, w_down, g_final, loss_target, m_g_mix, m_w_in, m_b_in, m_w_dw, m_b_dw, m_ln_g, m_ln_b, m_w_pool, m_s_pool, m_w_out, m_g_ffn, m_w_gate, m_w_up, m_w_down, m_g_final, v_g_mix, v_w_in, v_b_in, v_w_dw, v_b_dw, v_ln_g, v_ln_b, v_w_pool, v_s_pool, v_w_out, v_g_ffn, v_w_gate, v_w_up, v_w_down, v_g_final):
    x2 = x[0]
    target = loss_target[0]
    T, D = x2.shape
    w_in2, w_out2, w_down2 = w_in[0], w_out[0], w_down[0]
    taps_first = lambda a: jnp.transpose(a, (1, 0, 2))
    w_dw3 = taps_first(w_dw)
    w_gateT, w_upT = w_gate[0].T, w_up[0].T
    CI = w_in2.shape[1] * N_CHIPS
    DM = w_out2.shape[0] * N_CHIPS
    F = w_down2.shape[0] * N_CHIPS
    KW, _, dw_cols = w_dw3.shape
    assert dw_cols == LANES
    n_grp, G = w_pool.shape[1], w_pool.shape[-1]
    w_pool3 = w_pool[0]
    g_final2 = g_final.reshape(1, D)

    me = (2 * lax.axis_index("x") + lax.axis_index("y")).astype(jnp.int32).reshape(1)

    w_inT_b, w_dw4, f_out, f_gate, f_up, f_down = _place_and_gather(
        [(w_in2, "rows", (CI, D), BF16, True, True), (w_dw3, "lead", (N_CHIPS, KW, 1, dw_cols), F32, False, False)],
        [(w, "rows", shape, BF16, False, True)
         for w, shape in ((w_out2, (DM, D)), (w_gateT, (F, D)), (w_upT, (F, D)), (w_down2, (F, D)))])
    w_pool_b = w_pool3.astype(BF16)
    (z, xn_b), (f_out, f_gate) = _in_proj(
        x2, g_mix, w_inT_b, b_in,
        comm=[_GatherIci([f_out], ["rows"], [True]), _GatherIci([f_gate], ["rows"], [True], which=(2,))])
    (y_b, v), (w_out_b, f_gate, f_up) = _seq_fwd(
        z, w_dw4, b_dw, ln_g, ln_b, w_pool_b, s_pool,
        comm=[_GatherD2d([f_out], ["rows"]), _GatherIci([f_gate], ["rows"], [True], which=(0, 1)),
              _GatherIci([f_up], ["rows"], [True])])
    (h1, hn_b), (wgT_b, wuT_b, f_down) = _out_proj(
        y_b, x2, w_out_b, g_ffn,
        comm=[_GatherD2d([f_gate, f_up], ["rows"] * 2), _GatherIci([f_down], ["rows"], [True])])
    (g_b, u_b, a_b), (wd_b,) = _gate_up(hn_b, wgT_b, wuT_b, comm=[_GatherD2d([f_down], ["rows"])])
    (dh2, dh2_b, loss_part, d_g_final), _ = _down_loss(a_b, wd_b, h1, target, g_final2)

    gw_down, _ = _weight_grad("grad_w_down", a_b, dh2_b)
    (dg_b, du_b), (p_down_xy,) = _ffn_bwd_act(dh2_b, wd_b, g_b, u_b, comm=[_Scatter([gw_down], ["rows"], which=(0, 1))])
    gw_gateT, (p_down_d,) = _weight_grad("grad_w_gate", dg_b, hn_b, comm=[_Scatter([gw_down], ["rows"], which=(2,))])
    gw_upT, _ = _weight_grad("grad_w_up", du_b, hn_b)
    sum_down = _sum_parts("sum_w_down", gw_down, "rows", [p_down_xy, p_down_d], me)
    (dh1, dh1_b, dy, d_g_ffn), (p_gate, oth_down) = _ffn_bwd_in(
        dg_b, du_b, wgT_b, wuT_b, h1, dh2, g_ffn, w_out_b, comm=[_Scatter([gw_gateT], ["rows"]), _Swap([sum_down])])
    gw_out, _ = _weight_grad("grad_w_out", y_b, dh1_b)
    sum_gate = _sum_parts("sum_w_gate", gw_gateT, "rows", [p_gate], me)
    res = {}
    res["w_down"] = _adamw_on_sparsecore("adamw_w_down", w_down2, m_w_down[0], v_w_down[0], sum_down, oth_down)
    (dz_b, d_wdw, d_bdw, d_lng, d_lnb, d_wp, d_sp, d_bin), (p_up, p_out, oth_gate) = _seq_bwd(
        z, dy, v, w_dw4, ln_g, ln_b, w_pool_b, s_pool,
        comm=[_Scatter([gw_upT, gw_out], ["rows", "rows"]), _Swap([sum_gate])])
    vec_grads = {"b_dw": d_bdw, "ln_g": d_lng, "ln_b": d_lnb, "s_pool": d_sp, "g_ffn": d_g_ffn, "g_final": d_g_final, "b_in": d_bin}
    layout = _PackLayout(dw_cols * N_CHIPS // LANES, n_grp, G, [(k, a.shape[1]) for k, a in vec_grads.items()])
    pack = _pack_small(layout, d_wdw, d_wp, vec_grads)
    sum_up = _sum_parts("sum_w_up", gw_upT, "rows", [p_up], me)
    sum_out = _sum_parts("sum_w_out", gw_out, "rows", [p_out], me)
    gw_inT, (p_small, oth_up, oth_out) = _weight_grad(
        "grad_w_in", dz_b, xn_b, comm=[_Scatter([pack], ["all"]), _Swap([sum_up, sum_out])])
    sum_small = _sum_parts("sum_small", pack, "all", [p_small], me)
    res["w_gate"] = _adamw_on_sparsecore("adamw_w_gate", w_gateT, m_w_gate[0].T, v_w_gate[0].T, sum_gate, oth_gate)
    send_sems, recv_sems, gw_thru, small_thru, p_in, oth_small, token = _late_exchange_start(gw_inT, sum_small)
    (grad_x, d_g_mix), _ = _in_proj_bwd(dz_b, w_inT_b, x2, dh1, g_mix + token[0, 0])
    gw_inT, sum_small, p_in, oth_small = _late_exchange_wait(
        send_sems, recv_sems, gw_thru, small_thru, p_in, oth_small, d_g_mix)
    res["w_up"] = _adamw_on_sparsecore("adamw_w_up", w_upT, m_w_up[0].T, v_w_up[0].T, sum_up, oth_up)
    res["w_out"] = _adamw_on_sparsecore("adamw_w_out", w_out2, m_w_out[0], v_w_out[0], sum_out, oth_out)
    sum_in = _sum_parts("sum_w_in", gw_inT, "rows", [p_in], me)
    (*res["g_mix"], loss_row), (oth_in,) = _allreduce_adamw_row(
        d_g_mix, g_mix, m_g_mix, v_g_mix, loss_part, comm=[_Swap([sum_in])])
    loss = loss_row[0, 0]
    res["w_in"], _ = _adamw("adamw_w_in", w_in2, m_w_in[0], v_w_in[0], sum_in, oth_in, g_transposed=True)

    vec_w = {"b_dw": b_dw, "ln_g": ln_g, "ln_b": ln_b, "s_pool": s_pool, "g_ffn": g_ffn, "g_final": g_final2, "b_in": b_in}
    vec_m = {"b_dw": m_b_dw, "ln_g": m_ln_g, "ln_b": m_ln_b, "s_pool": m_s_pool, "g_ffn": m_g_ffn,
             "g_final": m_g_final.reshape(1, D), "b_in": m_b_in}
    vec_v = {"b_dw": v_b_dw, "ln_g": v_ln_g, "ln_b": v_ln_b, "s_pool": v_s_pool, "g_ffn": v_g_ffn,
             "g_final": v_g_final.reshape(1, D), "b_in": v_b_in}
    small = _adamw_small(layout, sum_small, oth_small, w_dw3, taps_first(m_w_dw), taps_first(v_w_dw),
                         w_pool3, m_w_pool[0], v_w_pool[0], vec_w, vec_m, vec_v)
    res["w_dw"] = [taps_first(a) for a in small[0:4]]
    res["w_pool"] = [a[None] for a in small[4:8]]
    for q, k in enumerate(vec_w):
        res[k] = list(small[8 + 4 * q : 12 + 4 * q])
    res["g_final"] = [a.reshape(D) for a in res["g_final"]]
    for k in ("w_in", "w_out", "w_down"):
        res[k] = [a[None] for a in res[k]]
    for k in ("w_gate", "w_up"):
        res[k] = [a.T[None] for a in res[k]]

    order = ["g_mix", "w_in", "b_in", "w_dw", "b_dw", "ln_g", "ln_b", "w_pool", "s_pool", "w_out", "g_ffn", "w_gate", "w_up", "w_down", "g_final"]
    outs = [loss, grad_x[None]]
    for q in range(4):
        outs += [res[k][q] for k in order]
    return tuple(outs)
```

```python
import jax
import jax.numpy as jnp
from jax import lax
from jax.experimental import pallas as pl
from jax.experimental.pallas import tpu as pltpu
from jax.experimental.pallas import tpu_sc as plsc

F32 = jnp.float32
BF16 = jnp.bfloat16
MESH = pl.DeviceIdType.MESH
ANY = pl.BlockSpec(memory_space=pl.ANY)

RMS_EPS = 1e-6
LN_EPS = 1e-5
POOL_WINDOWS = (2, 4, 8, 16)
ADAM_LR = 0.001
ADAM_B1 = 0.9
ADAM_B2 = 0.999
ADAM_EPS = 1e-08
ADAM_WD = 0.01
ADAM_STEP = 10

LANES = 128
SUBLANES = 8
HALO = 32
CONV_ROWS = 64
HIDDEN_CHUNK = 512
VMEM_LIMIT = 56 * 1024 * 1024
PACK_W = 512
N_CHIPS = 4
N_DEV = 8
SC_CORES = 2
SC_TILES = 32
SC_LANES = 16


def _tile(n, want, mult=8):
    t = min(n, want)
    while n % t or t % mult:
        t -= 1
    return t


def _sigmoid(x):
    return 1.0 / (1.0 + jnp.exp(-x))


def _dot(a, b, dims):
    return lax.dot_general(a, b, (dims, ((), ())), preferred_element_type=F32)


NN = ((1,), (0,))
NT = ((1,), (1,))
TN = ((0,), (0,))


def _rms_bwd(x, g, dy):
    r = lax.rsqrt(jnp.mean(x * x, axis=-1, keepdims=True) + RMS_EPS)
    xh = x * r
    gy = dy * g
    dx = r * (gy - xh * jnp.mean(gy * xh, axis=-1, keepdims=True))
    return dx, dy * xh


def _accumulate(ref, first, val):
    @pl.when(first)
    def _():
        ref[...] = val

    @pl.when(jnp.logical_not(first))
    def _():
        ref[...] += val


def _place():
    return lax.axis_index("x"), lax.axis_index("y"), lax.axis_index("c")


def _other_chips(x, y):
    return [(1 - x, y), (x, 1 - y), (1 - x, 1 - y)]


def _rows(ref, start, n):
    return ref.at[pl.ds(pl.multiple_of(start, 16), n)]


def _window(ref, how, k, c=None):
    if how == "all":
        return ref
    if how == "lead":
        return ref.at[k]
    if how == "rows":
        n = ref.shape[0] // N_CHIPS
        if c is None:
            return _rows(ref, k * n, n)
        return _rows(ref, k * n + c * (n // 2), n // 2)
    n = ref.shape[1] // N_CHIPS
    cols = pl.ds(pl.multiple_of(k * n, LANES), n)
    if c is None:
        return ref.at[:, cols]
    h = ref.shape[0] // 2
    return ref.at[pl.ds(pl.multiple_of(c * h, 16), h), cols]


def _remote(src, dst, sems, s, device):
    return pltpu.make_async_remote_copy(
        src_ref=src, dst_ref=dst, send_sem=sems.at[s], recv_sem=sems.at[s + 1], device_id=device, device_id_type=MESH)


class _GatherIci:
    aliased = True

    def __init__(self, fulls, hows, splits, which=(0, 1, 2)):
        self.fulls, self.hows, self.splits, self.which = list(fulls), list(hows), list(splits), tuple(which)

    def inputs(self):
        return self.fulls

    def out_shapes(self):
        return [jax.ShapeDtypeStruct(a.shape, a.dtype) for a in self.fulls]

    def n_sems(self):
        return 6 * len(self.fulls)

    def build(self, ins, outs, sems, base):
        x, y, c = _place()
        me = 2 * x + y
        chips = _other_chips(x, y)
        starts, waits = [], []
        for a, (how, sp) in enumerate(zip(self.hows, self.splits)):
            half = c if sp else None
            mine = _window(outs[a], how, me, half)
            for j in self.which:
                px, py = chips[j]
                s = base + 6 * a + 2 * j
                cp = _remote(mine, mine, sems, s, (px, py, c))
                landing = _remote(mine, _window(outs[a], how, 2 * px + py, half), sems, s, (px, py, c))
                starts.append(cp.start)
                waits += [landing.wait_recv, cp.wait_send]
        return starts, waits


class _GatherD2d:
    aliased = True

    def __init__(self, fulls, hows):
        self.fulls, self.hows = list(fulls), list(hows)

    def inputs(self):
        return self.fulls

    def out_shapes(self):
        return [jax.ShapeDtypeStruct(a.shape, a.dtype) for a in self.fulls]

    def n_sems(self):
        return 6 * len(self.fulls)

    def build(self, ins, outs, sems, base):
        x, y, c = _place()
        starts, waits = [], []
        for a, how in enumerate(self.hows):
            for j, (px, py) in enumerate(_other_chips(x, y)):
                s = base + 6 * a + 2 * j
                got = _window(outs[a], how, 2 * px + py, c)
                cp = _remote(got, got, sems, s, (x, y, 1 - c))
                landing = _remote(got, _window(outs[a], how, 2 * px + py, 1 - c), sems, s, (x, y, 1 - c))
                starts.append(cp.start)
                waits += [landing.wait_recv, cp.wait_send]
        return starts, waits


def _part_shape(a, how):
    if how == "all":
        return a.shape
    if how == "rows":
        return (a.shape[0] // N_CHIPS, a.shape[1])
    return (a.shape[0], a.shape[1] // N_CHIPS)


class _Scatter:
    aliased = False

    def __init__(self, fulls, hows, which=(0, 1, 2)):
        self.fulls, self.hows, self.which = list(fulls), list(hows), tuple(which)

    def inputs(self):
        return self.fulls

    def out_shapes(self):
        return [jax.ShapeDtypeStruct((len(self.which),) + _part_shape(a, h), a.dtype) for a, h in zip(self.fulls, self.hows)]

    def n_sems(self):
        return 6 * len(self.fulls)

    def build(self, ins, outs, sems, base):
        x, y, c = _place()
        chips = _other_chips(x, y)
        starts, waits = [], []
        for a, how in enumerate(self.hows):
            for slot, j in enumerate(self.which):
                px, py = chips[j]
                cp = _remote(_window(ins[a], how, 2 * px + py), outs[a].at[slot], sems, base + 6 * a + 2 * j, (px, py, c))
                starts.append(cp.start)
                waits += [cp.wait_recv, cp.wait_send]
        return starts, waits


class _Swap:
    aliased = False

    def __init__(self, arrays):
        self.arrays = list(arrays)

    def inputs(self):
        return self.arrays

    def out_shapes(self):
        return [jax.ShapeDtypeStruct(a.shape, a.dtype) for a in self.arrays]

    def n_sems(self):
        return 2 * len(self.arrays)

    def build(self, ins, outs, sems, base):
        x, y, c = _place()
        starts, waits = [], []
        for a in range(len(ins)):
            cp = _remote(ins[a], outs[a], sems, base + 2 * a, (x, y, 1 - c))
            starts.append(cp.start)
            waits += [cp.wait_recv, cp.wait_send]
        return starts, waits


def _call(name, body, grid, in_specs, out_specs, out_shape, args, scratch=(), comm=(), after=()):
    comm, after = list(comm), list(after)
    n_in, n_out, n_scr, n_after = len(args), len(out_shape), len(scratch), len(after)
    c_in = [a for op in comm for a in op.inputs()]
    c_out = [s for op in comm for s in op.out_shapes()]
    n_sems = sum(op.n_sems() for op in comm)
    aliases, i_in, i_out = {}, 0, 0
    for op in comm:
        if op.aliased:
            for q in range(len(op.inputs())):
                aliases[n_in + n_after + i_in + q] = n_out + i_out + q
        i_in, i_out = i_in + len(op.inputs()), i_out + len(op.out_shapes())

    def wrapped(*refs):
        ins = refs[:n_in]
        cin = refs[n_in + n_after : n_in + n_after + len(c_in)]
        o0 = n_in + n_after + len(c_in)
        outs = refs[o0 : o0 + n_out]
        cout = refs[o0 + n_out : o0 + n_out + len(c_out)]
        s0 = o0 + n_out + len(c_out)
        scr = refs[s0 : s0 + n_scr]

        def copies():
            sems = refs[s0 + n_scr]
            starts, waits = [], []
            i_in = i_out = base = 0
            for op in comm:
                ni, no = len(op.inputs()), len(op.out_shapes())
                s, w = op.build(cin[i_in : i_in + ni], cout[i_out : i_out + no], sems, base)
                starts += s
                waits += w
                i_in, i_out, base = i_in + ni, i_out + no, base + op.n_sems()
            return starts, waits

        def run_starts():
            for start in copies()[0]:
                start()

        def run_waits():
            for wait in copies()[1]:
                wait()

        if comm and grid:
            first = last = True
            for d, n in enumerate(grid):
                first = jnp.logical_and(first, pl.program_id(d) == 0)
                last = jnp.logical_and(last, pl.program_id(d) == n - 1)
            pl.when(first)(run_starts)
        elif comm:
            run_starts()
        if body is not None:
            body(*ins, *outs, *scr)
        if comm and grid:
            pl.when(last)(run_waits)
        elif comm:
            run_waits()

    res = pl.pallas_call(
        wrapped,
        name=name,
        grid=grid,
        in_specs=list(in_specs) + [ANY] * (n_after + len(c_in)),
        out_specs=list(out_specs) + [ANY] * len(c_out),
        out_shape=list(out_shape) + c_out,
        scratch_shapes=list(scratch) + ([pltpu.SemaphoreType.DMA((n_sems,))] if comm else []),
        input_output_aliases=aliases,
        compiler_params=pltpu.CompilerParams(dimension_semantics=("arbitrary",) * len(grid), vmem_limit_bytes=VMEM_LIMIT),
    )(*args, *after, *c_in)
    return tuple(res[:n_out]), tuple(res[n_out:])


def _place_and_gather(now, later):
    items = list(now) + list(later)
    n, n_now = len(items), len(now)
    buf_shape = lambda it: it[0].shape[::-1] if it[4] else it[0].shape
    split_now = [a for a in range(n_now) if items[a][5]]

    def body(*refs):
        ins, outs = refs[:n], refs[n : 2 * n]
        stage, bufs = refs[2 * n : 3 * n - n_now], refs[3 * n - n_now : 4 * n - n_now]
        sems = refs[4 * n - n_now]
        x, y, c = _place()
        me = 2 * x + y
        chips = _other_chips(x, y)
        loads = [pltpu.make_async_copy(ins[a], stage[a - n_now], sems.at[a]) for a in range(n_now, n)]
        for ld in loads:
            ld.start()
        pending = []

        def place(a, val):
            _, how, _, dtype, transposed, _ = items[a]
            bufs[a][...] = (val.T if transposed else val).astype(dtype)
            cp = pltpu.make_async_copy(bufs[a], _window(outs[a], how, me), sems.at[n + a])
            cp.start()
            pending.append(cp.wait)

        arrivals = []
        for a in range(n_now):
            place(a, ins[a][...])
            how, split = items[a][1], items[a][5]
            half = c if split else None
            src = _rows(bufs[a], c * (bufs[a].shape[0] // 2), bufs[a].shape[0] // 2) if split else bufs[a]
            for j, (px, py) in enumerate(chips):
                s = 2 * n + 6 * a + 2 * j
                cp = _remote(src, _window(outs[a], how, me, half), sems, s, (px, py, c))
                landing = _remote(src, _window(outs[a], how, 2 * px + py, half), sems, s, (px, py, c))
                cp.start()
                arrivals.append(landing.wait_recv)
                pending.append(cp.wait_send)
        for a in range(n_now, n):
            loads[a - n_now].wait()
            place(a, stage[a - n_now][...])
        for wait in arrivals:
            wait()
        d2d = _GatherD2d([None] * len(split_now), [items[a][1] for a in split_now])
        starts, waits = d2d.build(None, [outs[a] for a in split_now], sems, 2 * n + 6 * n_now)
        for start in starts:
            start()
        for wait in waits + pending:
            wait()

    vm = pl.BlockSpec(memory_space=pltpu.VMEM)
    return pl.pallas_call(
        body,
        name="place_and_gather",
        in_specs=[vm] * n_now + [ANY] * (n - n_now),
        out_specs=[ANY] * n,
        out_shape=[jax.ShapeDtypeStruct(it[2], it[3]) for it in items],
        scratch_shapes=[pltpu.VMEM(it[0].shape, it[0].dtype) for it in later]
        + [pltpu.VMEM(buf_shape(it), it[3]) for it in items]
        + [pltpu.SemaphoreType.DMA((2 * n + 6 * n_now + 6 * len(split_now),))],
        compiler_params=pltpu.CompilerParams(vmem_limit_bytes=VMEM_LIMIT),
    )(*[it[0] for it in items])


_HBM = pl.BlockSpec(memory_space=pltpu.HBM)
_SEM = pl.BlockSpec(memory_space=pltpu.SEMAPHORE)
_DATAFLOW = pltpu.SideEffectType.DATAFLOW_SIDE_EFFECTING


class _Pending:
    def __init__(self, op, sems, arrays, token):
        self.op, self.sems, self.arrays, self.token = op, sems, arrays, token


def _op_refs(op, refs):
    n_src = len(op.inputs())
    return refs[:n_src], (refs[:n_src] if op.aliased else refs[n_src:])


def _start(name, op, after=()):
    srcs = list(op.inputs())
    lands = [] if op.aliased else [lax.empty(sd.shape, sd.dtype) for sd in op.out_shapes()]
    arrays = srcs + lands
    n, after = len(arrays), list(after)

    def body(*refs):
        sems, token = refs[n + len(after)], refs[-1]
        starts, _ = op.build(*_op_refs(op, refs[:n]), sems, 0)
        for start in starts:
            start()
        token[...] = jnp.zeros_like(token)

    res = pl.pallas_call(
        body,
        name=name,
        out_shape=(pltpu.SemaphoreType.DMA((op.n_sems(),)),) + tuple(pltpu.HBM(a.shape, a.dtype) for a in arrays)
        + (jax.ShapeDtypeStruct((SUBLANES, LANES), F32),),
        in_specs=(_HBM,) * n + (ANY,) * len(after),
        out_specs=(_SEM,) + (_HBM,) * n + (pl.BlockSpec(memory_space=pltpu.VMEM),),
        input_output_aliases={i: 1 + i for i in range(n)},
        compiler_params=pltpu.CompilerParams(has_side_effects=_DATAFLOW),
    )(*[pltpu.with_memory_space_constraint(a, pltpu.HBM) for a in arrays], *after)
    return _Pending(op, res[0], list(res[1 : 1 + n]), res[-1])


def _wait(name, pending, after):
    op, arrays = pending.op, pending.arrays
    n = len(arrays)

    def body(*refs):
        _, waits = op.build(*_op_refs(op, refs[:n]), refs[n], 0)
        for wait in waits:
            wait()

    return pl.pallas_call(
        body,
        name=name,
        out_shape=tuple(pltpu.HBM(a.shape, a.dtype) for a in arrays),
        in_specs=(_HBM,) * n + (_SEM, ANY),
        out_specs=(_HBM,) * n,
        input_output_aliases={i: i for i in range(n)},
        compiler_params=pltpu.CompilerParams(has_side_effects=_DATAFLOW),
    )(*arrays, pending.sems, after)


def _late_copies(gw_ref, small_ref, parts_ref, oth_ref, send_sems, recv_sems):
    x, y, c = _place()
    copies = [
        pltpu.make_async_remote_copy(
            src_ref=_window(gw_ref, "rows", 2 * px + py), dst_ref=parts_ref.at[j], send_sem=send_sems.at[j],
            recv_sem=recv_sems.at[j], device_id=(px, py, c), device_id_type=MESH)
        for j, (px, py) in enumerate(_other_chips(x, y))
    ]
    copies.append(pltpu.make_async_remote_copy(
        src_ref=small_ref, dst_ref=oth_ref, send_sem=send_sems.at[3], recv_sem=recv_sems.at[3],
        device_id=(x, y, 1 - c), device_id_type=MESH))
    return copies


def _late_exchange_start(gw, small):
    part = (gw.shape[0] // N_CHIPS, gw.shape[1])

    def body(gw_ref, small_ref, parts_ref, oth_ref, send_sems, recv_sems, gw_thru, small_thru, parts_thru, oth_thru, token):
        for cp in _late_copies(gw_ref, small_ref, parts_ref, oth_ref, send_sems, recv_sems):
            cp.start()
        token[...] = jnp.zeros_like(token)

    hbm = lambda v: pltpu.with_memory_space_constraint(v, pltpu.HBM)
    return pl.pallas_call(
        body,
        name="late_exchange_start",
        out_shape=(pltpu.SemaphoreType.DMA((4,)), pltpu.SemaphoreType.DMA((4,)), pltpu.HBM(gw.shape, gw.dtype),
                   pltpu.HBM(small.shape, small.dtype), pltpu.HBM((3,) + part, gw.dtype), pltpu.HBM(small.shape, small.dtype),
                   jax.ShapeDtypeStruct((SUBLANES, LANES), F32)),
        in_specs=(_HBM,) * 4,
        out_specs=(_SEM, _SEM, _HBM, _HBM, _HBM, _HBM, pl.BlockSpec(memory_space=pltpu.VMEM)),
        input_output_aliases={0: 2, 1: 3, 2: 4, 3: 5},
        compiler_params=pltpu.CompilerParams(has_side_effects=_DATAFLOW),
    )(hbm(gw), hbm(small), hbm(lax.empty((3,) + part, gw.dtype)), hbm(lax.empty(small.shape, small.dtype)))


def _late_exchange_wait(send_sems, recv_sems, gw, small, parts, oth, after):
    def body(gw_ref, small_ref, parts_ref, oth_ref, send_sems, recv_sems, after_ref, gw_dead, small_dead, parts_out, oth_out):
        for cp in _late_copies(gw_ref, small_ref, parts_ref, oth_ref, send_sems, recv_sems):
            cp.wait_send()
            cp.wait_recv()

    return pl.pallas_call(
        body,
        name="late_exchange_wait",
        out_shape=(pltpu.HBM(gw.shape, gw.dtype), pltpu.HBM(small.shape, small.dtype), pltpu.HBM(parts.shape, parts.dtype),
                   pltpu.HBM(oth.shape, oth.dtype)),
        in_specs=(_HBM, _HBM, _HBM, _HBM, _SEM, _SEM, ANY),
        out_specs=(_HBM,) * 4,
        input_output_aliases={0: 0, 1: 1, 2: 2, 3: 3},
        compiler_params=pltpu.CompilerParams(has_side_effects=_DATAFLOW),
    )(gw, small, parts, oth, send_sems, recv_sems, after)


def _in_proj(x, g_mix, w_inT_b, b_in, comm=(), after=()):
    T, D = x.shape
    CI = w_inT_b.shape[0]
    tm = _tile(T, 512)

    def body(x_ref, g_ref, w_ref, b_ref, z_ref, xn_ref):
        xv = x_ref[...]
        r = lax.rsqrt(jnp.mean(xv * xv, axis=-1, keepdims=True) + RMS_EPS)
        xn = (xv * r * g_ref[...]).astype(BF16)
        xn_ref[...] = xn
        z_ref[...] = _dot(xn, w_ref[...], NT) + b_ref[...]

    return _call(
        "in_proj",
        body,
        (T // tm,),
        [
            pl.BlockSpec((tm, D), lambda i: (i, 0)),
            pl.BlockSpec((1, D), lambda i: (0, 0)),
            pl.BlockSpec((CI, D), lambda i: (0, 0)),
            pl.BlockSpec((1, CI), lambda i: (0, 0)),
        ],
        [pl.BlockSpec((tm, CI), lambda i: (i, 0)), pl.BlockSpec((tm, D), lambda i: (i, 0))],
        [jax.ShapeDtypeStruct((T, CI), F32), jax.ShapeDtypeStruct((T, D), BF16)],
        (x, g_mix, w_inT_b, b_in),
        comm=comm,
        after=after,
    )


def _fill_shifted(scr):
    n = scr.shape[1] - SUBLANES
    for s in range(1, SUBLANES):
        scr[s, 0:n, :] = scr[0, s : s + n, :]


def _shifted_rows(scr, off, n, cs):
    s = off % SUBLANES
    return scr[s, off - s : off - s + n, cs]


def _pool_mean_minus_token(p_scr, cs, w, cnt, tt):
    tok = p_scr[HALO : HALO + tt, cs]
    s = tok
    for d in range(1, w):
        s = s + p_scr[HALO - d : HALO - d + tt, cs]
    return s / cnt - tok


def _seq_fwd(z, w_dw4, b_dw, ln_g, ln_b, w_pool_b, s_pool, comm=(), after=()):
    T, CI = z.shape
    CC = ln_g.shape[1]
    n_grp, G = w_pool_b.shape[0], w_pool_b.shape[-1]
    KW = w_dw4.shape[1]
    D = CC + n_grp * G
    tt = _tile(T, 512, HALO)
    per = tt // HALO

    def body(zc_ref, zp_ref, wdw_ref, bdw_ref, lng_ref, lnb_ref, wp_ref, sp_ref, y_ref, v_ref, u_scr, p_scr):
        i = pl.program_id(0)
        first = i == 0
        u_prev = zp_ref[:, 0:CC] * _sigmoid(zp_ref[:, CC : 2 * CC])
        u_scr[0, 0:HALO, :] = jnp.where(first, 0.0, u_prev)
        p_scr[0:HALO, :] = jnp.where(first, 0.0, zp_ref[:, 2 * CC :])
        u_scr[0, HALO:, :] = zc_ref[:, 0:CC] * _sigmoid(zc_ref[:, CC : 2 * CC])
        p_scr[HALO:, :] = zc_ref[:, 2 * CC :]
        _fill_shifted(u_scr)

        for j in range(CC // LANES):
            cs = slice(LANES * j, LANES * (j + 1))
            for rb in range(tt // CONV_ROWS):
                acc = jnp.zeros((CONV_ROWS, LANES), F32)
                for k in range(KW):
                    off = HALO - (KW - 1) + k + rb * CONV_ROWS
                    acc = acc + _shifted_rows(u_scr, off, CONV_ROWS, cs) * wdw_ref[j, k]
                v_ref[rb * CONV_ROWS : (rb + 1) * CONV_ROWS, cs] = acc + bdw_ref[:, cs]

        v = v_ref[...]
        mu = jnp.mean(v, axis=-1, keepdims=True)
        d = v - mu
        var = jnp.mean(d * d, axis=-1, keepdims=True)
        ln = d * lax.rsqrt(var + LN_EPS) * lng_ref[...] + lnb_ref[...]
        y_ref[:, 0:CC] = (ln * _sigmoid(ln)).astype(BF16)

        tpos = i * tt + lax.broadcasted_iota(jnp.int32, (tt, 1), 0)
        for gi, w in enumerate(POOL_WINDOWS):
            cs = slice(G * gi, G * (gi + 1))
            cnt = jnp.minimum(tpos + 1, w).astype(F32)
            yi = _pool_mean_minus_token(p_scr, cs, w, cnt, tt)
            q = _dot(yi.astype(BF16), wp_ref[gi], NN)
            y_ref[:, CC + G * gi : CC + G * (gi + 1)] = (q * sp_ref[:, cs]).astype(BF16)

    const2 = lambda i: (0, 0)
    return _call(
        "seq_fwd",
        body,
        (T // tt,),
        [
            pl.BlockSpec((tt, CI), lambda i: (i, 0)),
            pl.BlockSpec((HALO, CI), lambda i: (jnp.maximum(i * per - 1, 0), 0)),
            pl.BlockSpec(w_dw4.shape, lambda i: (0,) * w_dw4.ndim),
            pl.BlockSpec((1, CC), const2),
            pl.BlockSpec((1, CC), const2),
            pl.BlockSpec((1, CC), const2),
            pl.BlockSpec(w_pool_b.shape, lambda i: (0, 0, 0)),
            pl.BlockSpec((1, n_grp * G), const2),
        ],
        [pl.BlockSpec((tt, D), lambda i: (i, 0)), pl.BlockSpec((tt, CC), lambda i: (i, 0))],
        [jax.ShapeDtypeStruct((T, D), BF16), jax.ShapeDtypeStruct((T, CC), F32)],
        (z, z, w_dw4, b_dw, ln_g, ln_b, w_pool_b, s_pool),
        scratch=[pltpu.VMEM((SUBLANES, HALO + tt, CC), F32), pltpu.VMEM((HALO + tt, n_grp * G), F32)],
        comm=comm,
        after=after,
    )


def _out_proj(y_b, x, w_out_b, g_ffn, comm=(), after=()):
    T, D = x.shape
    tm = _tile(T, 512)

    def body(y_ref, x_ref, w_ref, g_ref, h1_ref, hn_ref):
        h1 = x_ref[...] + _dot(y_ref[...], w_ref[...], NN)
        h1_ref[...] = h1
        r = lax.rsqrt(jnp.mean(h1 * h1, axis=-1, keepdims=True) + RMS_EPS)
        hn_ref[...] = (h1 * r * g_ref[...]).astype(BF16)

    row = lambda i: (i, 0)
    return _call(
        "out_proj",
        body,
        (T // tm,),
        [
            pl.BlockSpec((tm, y_b.shape[1]), row),
            pl.BlockSpec((tm, D), row),
            pl.BlockSpec(w_out_b.shape, lambda i: (0, 0)),
            pl.BlockSpec((1, D), lambda i: (0, 0)),
        ],
        [pl.BlockSpec((tm, D), row), pl.BlockSpec((tm, D), row)],
        [jax.ShapeDtypeStruct((T, D), F32), jax.ShapeDtypeStruct((T, D), BF16)],
        (y_b, x, w_out_b, g_ffn),
        comm=comm,
        after=after,
    )


def _hidden_tile(F):
    return _tile(F, 1408, LANES)


def _gate_up(hn_b, wgT_b, wuT_b, comm=()):
    T, D = hn_b.shape
    F = wgT_b.shape[0]
    tm, tf = _tile(T, 1024), _hidden_tile(F)

    def body(hn_ref, wg_ref, wu_ref, g_ref, u_ref, a_ref):
        hn = hn_ref[...]
        for c0 in range(0, tf, HIDDEN_CHUNK):
            cs = slice(c0, min(c0 + HIDDEN_CHUNK, tf))
            gv = _dot(hn, wg_ref[cs, :], NT)
            uv = _dot(hn, wu_ref[cs, :], NT)
            g_ref[:, cs] = gv.astype(BF16)
            u_ref[:, cs] = uv.astype(BF16)
            a_ref[:, cs] = (gv * _sigmoid(gv) * uv).astype(BF16)

    wspec = pl.BlockSpec((tf, D), lambda j, i: (j, 0))
    ospec = pl.BlockSpec((tm, tf), lambda j, i: (i, j))
    return _call(
        "gate_up",
        body,
        (F // tf, T // tm),
        [pl.BlockSpec((tm, D), lambda j, i: (i, 0)), wspec, wspec],
        [ospec, ospec, ospec],
        [jax.ShapeDtypeStruct((T, F), BF16)] * 3,
        (hn_b, wgT_b, wuT_b),
        comm=comm,
    )


def _down_loss(a_b, wd_b, h1, target, g_final, comm=()):
    T, D = h1.shape
    F = a_b.shape[1]
    tm = _tile(T, 512)
    nt = T // tm

    def body(a_ref, w_ref, h1_ref, t_ref, g_ref, dh2_ref, dh2b_ref, loss_ref, dg_ref):
        i = pl.program_id(0)
        h2 = h1_ref[...] + _dot(a_ref[...], w_ref[...], NN)
        r = lax.rsqrt(jnp.mean(h2 * h2, axis=-1, keepdims=True) + RMS_EPS)
        g = g_ref[...]
        diff = h2 * r * g - t_ref[...]
        _accumulate(loss_ref, i == 0, jnp.full(loss_ref.shape, jnp.sum(diff * diff) * (0.5 / D), F32))
        dh2, dg_rows = _rms_bwd(h2, g, diff * (1.0 / D))
        dh2_ref[...] = dh2
        dh2b_ref[...] = dh2.astype(BF16)
        _accumulate(dg_ref, i == 0, jnp.sum(dg_rows, axis=0, keepdims=True))

    row = lambda i: (i, 0)
    return _call(
        "down_loss",
        body,
        (nt,),
        [
            pl.BlockSpec((tm, F), row),
            pl.BlockSpec((F, D), lambda i: (0, 0), pipeline_mode=pl.Buffered(1)),
            pl.BlockSpec((tm, D), row),
            pl.BlockSpec((tm, D), row),
            pl.BlockSpec((1, D), lambda i: (0, 0)),
        ],
        [
            pl.BlockSpec((tm, D), row),
            pl.BlockSpec((tm, D), row),
            pl.BlockSpec((1, LANES), lambda i: (0, 0)),
            pl.BlockSpec((1, D), lambda i: (0, 0)),
        ],
        [
            jax.ShapeDtypeStruct((T, D), F32),
            jax.ShapeDtypeStruct((T, D), BF16),
            jax.ShapeDtypeStruct((1, LANES), F32),
            jax.ShapeDtypeStruct((1, D), F32),
        ],
        (a_b, wd_b, h1, target, g_final),
        comm=comm,
    )


def _ffn_bwd_act(dh2_b, wd_b, g_b, u_b, comm=()):
    T, D = dh2_b.shape
    F = wd_b.shape[0]
    tm, tf = _tile(T, 1024), _hidden_tile(F)

    def body(d_ref, w_ref, g_ref, u_ref, dg_ref, du_ref):
        d = d_ref[...]
        for c0 in range(0, tf, HIDDEN_CHUNK):
            cs = slice(c0, min(c0 + HIDDEN_CHUNK, tf))
            da = _dot(d, w_ref[cs, :], NT)
            gv = g_ref[:, cs].astype(F32)
            uv = u_ref[:, cs].astype(F32)
            sg = _sigmoid(gv)
            silu = gv * sg
            dg_ref[:, cs] = (da * uv * (sg * (1.0 + gv * (1.0 - sg)))).astype(BF16)
            du_ref[:, cs] = (da * silu).astype(BF16)

    aspec = pl.BlockSpec((tm, tf), lambda j, i: (i, j))
    return _call(
        "ffn_bwd_act",
        body,
        (F // tf, T // tm),
        [pl.BlockSpec((tm, D), lambda j, i: (i, 0)), pl.BlockSpec((tf, D), lambda j, i: (j, 0)), aspec, aspec],
        [aspec, aspec],
        [jax.ShapeDtypeStruct((T, F), BF16)] * 2,
        (dh2_b, wd_b, g_b, u_b),
        comm=comm,
    )


def _ffn_bwd_in(dg_b, du_b, wgT_b, wuT_b, h1, dh2, g_ffn, w_out_b, comm=()):
    T, D = h1.shape
    F = wgT_b.shape[0]
    DM = w_out_b.shape[0]
    tm = _tile(T, 512)

    def body(dg_ref, du_ref, wg_ref, wu_ref, h1_ref, dh2_ref, g_ref, wo_ref, dh1_ref, dh1b_ref, dy_ref, dgf_ref):
        i = pl.program_id(0)
        dhn = _dot(dg_ref[...], wg_ref[...], NN) + _dot(du_ref[...], wu_ref[...], NN)
        dx, dg_rows = _rms_bwd(h1_ref[...], g_ref[...], dhn)
        dh1 = dh2_ref[...] + dx
        dh1b = dh1.astype(BF16)
        dh1_ref[...] = dh1
        dh1b_ref[...] = dh1b
        dy_ref[...] = _dot(dh1b, wo_ref[...], NT)
        _accumulate(dgf_ref, i == 0, jnp.sum(dg_rows, axis=0, keepdims=True))

    row = lambda i: (i, 0)
    const = lambda i: (0, 0)
    return _call(
        "ffn_bwd_in",
        body,
        (T // tm,),
        [
            pl.BlockSpec((tm, F), row),
            pl.BlockSpec((tm, F), row),
            pl.BlockSpec((F, D), const, pipeline_mode=pl.Buffered(1)),
            pl.BlockSpec((F, D), const, pipeline_mode=pl.Buffered(1)),
            pl.BlockSpec((tm, D), row),
            pl.BlockSpec((tm, D), row),
            pl.BlockSpec((1, D), const),
            pl.BlockSpec((DM, D), const, pipeline_mode=pl.Buffered(1)),
        ],
        [pl.BlockSpec((tm, D), row), pl.BlockSpec((tm, D), row), pl.BlockSpec((tm, DM), row), pl.BlockSpec((1, D), const)],
        [
            jax.ShapeDtypeStruct((T, D), F32),
            jax.ShapeDtypeStruct((T, D), BF16),
            jax.ShapeDtypeStruct((T, DM), F32),
            jax.ShapeDtypeStruct((1, D), F32),
        ],
        (dg_b, du_b, wgT_b, wuT_b, h1, dh2, g_ffn, w_out_b),
        comm=comm,
    )


def _seq_bwd(z, dy, v, w_dw4, ln_g, ln_b, w_pool_b, s_pool, comm=()):
    T, CI = z.shape
    CC = ln_g.shape[1]
    n_grp, G = w_pool_b.shape[0], w_pool_b.shape[-1]
    CP = n_grp * G
    KW = w_dw4.shape[1]
    n_cc = CC // LANES
    D = CC + CP
    tt = _tile(T, 512, HALO)
    per = tt // HALO
    n_tiles = T // tt
    last_halo = T // HALO - 1

    def body(zc_ref, zp_ref, dyc_ref, dyn_ref, vc_ref, vn_ref, wdw_ref, lng_ref, lnb_ref, wp_ref, sp_ref,
             dz_ref, dwdw_ref, dbdw_ref, dlng_ref, dlnb_ref, dwp_ref, dsp_ref, dbin_ref,
             dv_scr, u_scr, p_scr, g_scr, dw_scr):
        i = pl.program_id(0)
        first = i == 0
        last = i == n_tiles - 1
        lng, lnb = lng_ref[...], lnb_ref[...]

        def conv_pre(vv, dyc):
            mu = jnp.mean(vv, axis=-1, keepdims=True)
            d = vv - mu
            rs = lax.rsqrt(jnp.mean(d * d, axis=-1, keepdims=True) + LN_EPS)
            xh = d * rs
            ln = xh * lng + lnb
            sg = _sigmoid(ln)
            dln = dyc * (sg * (1.0 + ln * (1.0 - sg)))
            dxh = dln * lng
            dv = rs * (dxh - jnp.mean(dxh, axis=-1, keepdims=True) - xh * jnp.mean(dxh * xh, axis=-1, keepdims=True))
            return dv, dln, xh

        dv_c, dln_c, xh_c = conv_pre(vc_ref[...], dyc_ref[:, 0:CC])
        dv_scr[0, 0:tt, :] = dv_c
        dv_n, _, _ = conv_pre(vn_ref[...], dyn_ref[:, 0:CC])
        dv_scr[0, tt:, :] = jnp.where(last, 0.0, dv_n)
        _fill_shifted(dv_scr)
        _accumulate(dlng_ref, first, jnp.sum(dln_c * xh_c, axis=0, keepdims=True))
        _accumulate(dlnb_ref, first, jnp.sum(dln_c, axis=0, keepdims=True))
        _accumulate(dbdw_ref, first, jnp.sum(dv_c, axis=0, keepdims=True))

        u_scr[...] = zc_ref[:, 0:CC] * _sigmoid(zc_ref[:, CC : 2 * CC])

        @pl.when(first)
        def _():
            dw_scr[...] = jnp.zeros_like(dw_scr)

        for j in range(n_cc):
            cs = slice(LANES * j, LANES * (j + 1))
            gs = slice(CC + LANES * j, CC + LANES * (j + 1))
            dbin_a = jnp.zeros((1, LANES), F32)
            dbin_g = jnp.zeros((1, LANES), F32)
            for rb in range(tt // CONV_ROWS):
                rows = slice(rb * CONV_ROWS, (rb + 1) * CONV_ROWS)
                u_blk = u_scr[rows, cs]
                du = jnp.zeros((CONV_ROWS, LANES), F32)
                for k in range(KW):
                    off = rb * CONV_ROWS + (KW - 1) - k
                    d = _shifted_rows(dv_scr, off, CONV_ROWS, cs)
                    du = du + d * wdw_ref[j, k]
                    dw_scr[j * HALO + k] += jnp.sum((u_blk * d).reshape(CONV_ROWS // 8, 8, LANES), axis=0)
                a = zc_ref[rows, cs]
                sg = _sigmoid(zc_ref[rows, gs])
                da = du * sg
                dgate = du * a * sg * (1.0 - sg)
                dz_ref[rows, cs] = da.astype(BF16)
                dz_ref[rows, gs] = dgate.astype(BF16)
                dbin_a = dbin_a + jnp.sum(da, axis=0, keepdims=True)
                dbin_g = dbin_g + jnp.sum(dgate, axis=0, keepdims=True)
            _accumulate(dbin_ref.at[:, cs], first, dbin_a)
            _accumulate(dbin_ref.at[:, gs], first, dbin_g)

        @pl.when(last)
        def _():
            dwdw_ref[...] = jnp.sum(dw_scr[...], axis=1).reshape(dwdw_ref.shape)

        p_scr[0:HALO, :] = jnp.where(first, 0.0, zp_ref[:, 2 * CC :])
        p_scr[HALO:, :] = zc_ref[:, 2 * CC :]
        tpos = i * tt + lax.broadcasted_iota(jnp.int32, (tt, 1), 0)
        for gi, w in enumerate(POOL_WINDOWS):
            cs = slice(G * gi, G * (gi + 1))
            ys = slice(CC + G * gi, CC + G * (gi + 1))
            ps = slice(2 * CC + G * gi, 2 * CC + G * (gi + 1))
            cnt = jnp.minimum(tpos + 1, w).astype(F32)
            yib = _pool_mean_minus_token(p_scr, cs, w, cnt, tt).astype(BF16)
            wp = wp_ref[gi]
            sp = sp_ref[:, cs]
            dyp = dyc_ref[:, ys]
            q = _dot(yib, wp, NN)
            _accumulate(dsp_ref.at[:, cs], first, jnp.sum(dyp * q, axis=0, keepdims=True))
            dq_c = (dyp * sp).astype(BF16)
            dq_n = (jnp.where(last, 0.0, dyn_ref[:, ys]) * sp).astype(BF16)
            _accumulate(dwp_ref.at[gi], first, _dot(yib, dq_c, TN))
            dyi_c = _dot(dq_c, wp, NT)
            g_scr[0:tt, cs] = dyi_c / cnt
            g_scr[tt:, cs] = _dot(dq_n, wp, NT) * (1.0 / w)
            dp = -dyi_c
            for d in range(w):
                dp = dp + g_scr[d : d + tt, cs]
            dz_ref[:, ps] = dp.astype(BF16)
            _accumulate(dbin_ref.at[:, ps], first, jnp.sum(dp, axis=0, keepdims=True))

    cur = lambda i: (i, 0)
    prev = lambda i: (jnp.maximum(i * per - 1, 0), 0)
    nxt = lambda i: (jnp.minimum((i + 1) * per, last_halo), 0)
    c2 = lambda i: (0, 0)
    c3 = lambda i: (0, 0, 0)
    return _call(
        "seq_bwd",
        body,
        (n_tiles,),
        [
            pl.BlockSpec((tt, CI), cur),
            pl.BlockSpec((HALO, CI), prev),
            pl.BlockSpec((tt, D), cur),
            pl.BlockSpec((HALO, D), nxt),
            pl.BlockSpec((tt, CC), cur),
            pl.BlockSpec((HALO, CC), nxt),
            pl.BlockSpec(w_dw4.shape, lambda i: (0,) * w_dw4.ndim),
            pl.BlockSpec((1, CC), c2),
            pl.BlockSpec((1, CC), c2),
            pl.BlockSpec(w_pool_b.shape, c3),
            pl.BlockSpec((1, CP), c2),
        ],
        [
            pl.BlockSpec((tt, CI), cur),
            pl.BlockSpec((n_cc, HALO, LANES), c3),
            pl.BlockSpec((1, CC), c2),
            pl.BlockSpec((1, CC), c2),
            pl.BlockSpec((1, CC), c2),
            pl.BlockSpec((n_grp, G, G), c3),
            pl.BlockSpec((1, CP), c2),
            pl.BlockSpec((1, CI), c2),
        ],
        [
            jax.ShapeDtypeStruct((T, CI), BF16),
            jax.ShapeDtypeStruct((n_cc, HALO, LANES), F32),
            jax.ShapeDtypeStruct((1, CC), F32),
            jax.ShapeDtypeStruct((1, CC), F32),
            jax.ShapeDtypeStruct((1, CC), F32),
            jax.ShapeDtypeStruct((n_grp, G, G), F32),
            jax.ShapeDtypeStruct((1, CP), F32),
            jax.ShapeDtypeStruct((1, CI), F32),
        ],
        (z, z, dy, dy, v, v, w_dw4, ln_g, ln_b, w_pool_b, s_pool),
        scratch=[
            pltpu.VMEM((SUBLANES, tt + HALO, CC), F32),
            pltpu.VMEM((tt, CC), F32),
            pltpu.VMEM((HALO + tt, CP), F32),
            pltpu.VMEM((tt + HALO, CP), F32),
            pltpu.VMEM((n_cc * HALO, 8, LANES), F32),
        ],
        comm=comm,
    )


def _in_proj_bwd(dz_b, w_inT_b, x, dh1, g_mix, comm=()):
    T, D = x.shape
    CI = w_inT_b.shape[0]
    tm = _tile(T, 512)

    def body(dz_ref, w_ref, x_ref, dh1_ref, g_ref, dx_ref, dg_ref):
        i = pl.program_id(0)
        dxn = _dot(dz_ref[...], w_ref[...], NN)
        dx, dg_rows = _rms_bwd(x_ref[...], g_ref[...], dxn)
        dx_ref[...] = dh1_ref[...] + dx
        _accumulate(dg_ref, i == 0, jnp.sum(dg_rows, axis=0, keepdims=True))

    row = lambda i: (i, 0)
    const = lambda i: (0, 0)
    return _call(
        "in_proj_bwd",
        body,
        (T // tm,),
        [
            pl.BlockSpec((tm, CI), row),
            pl.BlockSpec((CI, D), const),
            pl.BlockSpec((tm, D), row),
            pl.BlockSpec((tm, D), row),
            pl.BlockSpec((1, D), const),
        ],
        [pl.BlockSpec((tm, D), row), pl.BlockSpec((1, D), const)],
        [jax.ShapeDtypeStruct((T, D), F32), jax.ShapeDtypeStruct((1, D), F32)],
        (dz_b, w_inT_b, x, dh1, g_mix),
        comm=comm,
    )


def _weight_grad(name, a_b, b_b, comm=()):
    T, N1 = a_b.shape
    N2 = b_b.shape[1]
    t1 = _tile(N1, 1408, LANES)
    tk = _tile(T, 2048)
    nk = T // tk

    def body(a_ref, b_ref, o_ref, acc):
        k = pl.program_id(1)
        _accumulate(acc, k == 0, _dot(a_ref[...], b_ref[...], TN))

        @pl.when(k == nk - 1)
        def _():
            o_ref[...] = acc[...].astype(BF16)

    (out,), rest = _call(
        name,
        body,
        (N1 // t1, nk),
        [pl.BlockSpec((tk, t1), lambda n, k: (k, n)), pl.BlockSpec((tk, N2), lambda n, k: (k, 0))],
        [pl.BlockSpec((t1, N2), lambda n, k: (n, 0))],
        [jax.ShapeDtypeStruct((N1, N2), BF16)],
        (a_b, b_b),
        scratch=[pltpu.VMEM((t1, N2), F32)],
        comm=comm,
    )
    return out, rest


def _sum_parts(name, full, how, parts, me):
    _, R, C = parts[0].shape
    tr = _tile(R, 512)
    nb = R // tr
    where = [(q, r) for q, p in enumerate(parts) for r in range(p.shape[0])]
    assert len(where) == 3

    def body(me_ref, own_ref, *refs):
        o_ref = refs[-1]
        f = lambda j: refs[where[j][0]][where[j][1]].astype(F32)
        o_ref[...] = (own_ref[...].astype(F32) + f(0)) + (f(1) + f(2))

    own_map = {"rows": lambda i, me_ref: (me_ref[0] * nb + i, 0), "cols": lambda i, me_ref: (i, me_ref[0]),
               "all": lambda i, me_ref: (i, 0)}[how]
    return pl.pallas_call(
        body,
        name=name,
        grid_spec=pltpu.PrefetchScalarGridSpec(
            num_scalar_prefetch=1,
            grid=(nb,),
            in_specs=[pl.BlockSpec((tr, C), own_map)]
            + [pl.BlockSpec((p.shape[0], tr, C), lambda i, me_ref: (0, i, 0)) for p in parts],
            out_specs=pl.BlockSpec((tr, C), lambda i, me_ref: (i, 0)),
        ),
        out_shape=jax.ShapeDtypeStruct((R, C), F32),
        compiler_params=pltpu.CompilerParams(dimension_semantics=("arbitrary",), vmem_limit_bytes=VMEM_LIMIT),
    )(me, full, *parts)


_M_CORR = 1.0 - ADAM_B1**ADAM_STEP
_V_CORR = 1.0 - ADAM_B2**ADAM_STEP


def _adamw_math(w, g, m, v):
    m = ADAM_B1 * m + (1.0 - ADAM_B1) * g
    v = ADAM_B2 * v + (1.0 - ADAM_B2) * (g * g)
    delta = -ADAM_LR * ((m / _M_CORR) / (jnp.sqrt(v / _V_CORR) + ADAM_EPS) + ADAM_WD * w)
    return delta, m, v


def _adamw(name, w, m, v, g_here, g_there, g_transposed=False, comm=()):
    R, C = w.shape
    tr = _tile(R, 256, LANES if g_transposed else 8)

    def body(w_ref, m_ref, v_ref, ga_ref, gb_ref, g_ref, d_ref, nm_ref, nv_ref):
        g = ga_ref[...] + gb_ref[...]
        if g_transposed:
            g = g.T
        g_ref[...] = g
        d_ref[...], nm_ref[...], nv_ref[...] = _adamw_math(w_ref[...], g, m_ref[...], v_ref[...])

    spec = pl.BlockSpec((tr, C), lambda i: (i, 0))
    gspec = pl.BlockSpec((C, tr), lambda i: (0, i)) if g_transposed else spec
    return _call(name, body, (R // tr,), [spec] * 3 + [gspec] * 2, [spec] * 4, [jax.ShapeDtypeStruct((R, C), F32)] * 4,
                 (w, m, v, g_here, g_there), comm=comm)


def _adamw_on_sparsecore(name, w, m, v, g_here, g_there):
    R, C = w.shape
    n_groups = R // SUBLANES
    n_turns = -(-n_groups // SC_TILES)
    n_in, n_out = 5, 4

    def body(w_hbm, m_hbm, v_hbm, ga_hbm, gb_hbm, g_out, d_out, nm_out, nv_out, bufs, sems):
        tile = lax.axis_index("subcore") * SC_CORES + lax.axis_index("sparsecore")
        srcs = (w_hbm, m_hbm, v_hbm, ga_hbm, gb_hbm)
        dsts = (d_out, nm_out, nv_out, g_out)

        def rows(turn):
            return pl.ds((tile + turn * SC_TILES) * SUBLANES, SUBLANES)

        def loads(turn):
            slot = turn % 2
            return [pltpu.make_async_copy(srcs[q].at[rows(turn), :], bufs.at[slot, q], sems.at[slot, q]) for q in range(n_in)]

        def stores(turn):
            slot = turn % 2
            return [pltpu.make_async_copy(bufs.at[slot, q], dsts[q].at[rows(turn), :], sems.at[slot, n_in + q])
                    for q in range(n_out)]

        def when_mine(turn, fn):
            pl.when(tile + turn * SC_TILES < n_groups)(fn)

        def compute(slot):
            wb, mb, vb, gab, gbb = (bufs.at[slot, q] for q in range(n_in))

            @pl.loop(0, SUBLANES)
            def _(r):
                @pl.loop(0, C, step=SC_LANES)
                def _(i):
                    at = (r, pl.ds(i, SC_LANES))
                    g = gab[at] + gbb[at]
                    delta, new_m, new_v = _adamw_math(wb[at], g, mb[at], vb[at])
                    gab[at], wb[at], mb[at], vb[at] = g, delta, new_m, new_v

        def start_loads(turn):
            def fn():
                for cp in loads(turn):
                    cp.start()

            when_mine(turn, fn)

        start_loads(0)
        for turn in range(n_turns):
            def step(turn=turn):
                for cp in loads(turn):
                    cp.wait()
                if turn >= 1:
                    for cp in stores(turn - 1):
                        cp.wait()
                if turn + 1 < n_turns:
                    start_loads(turn + 1)
                compute(turn % 2)
                for cp in stores(turn):
                    cp.start()

            when_mine(turn, step)
        for turn in range(n_turns):
            def drain(turn=turn):
                for cp in stores(turn):
                    cp.wait()

            last_mine = jnp.logical_and(tile + turn * SC_TILES < n_groups, tile + (turn + 1) * SC_TILES >= n_groups)
            pl.when(last_mine)(drain)

    return pl.kernel(
        body,
        name=name,
        out_type=[jax.ShapeDtypeStruct((R, C), F32)] * 4,
        mesh=plsc.VectorSubcoreMesh(core_axis_name="sparsecore", subcore_axis_name="subcore"),
        scratch_types=[pltpu.VMEM((2, n_in, SUBLANES, C), F32), pltpu.SemaphoreType.DMA((2, n_in + n_out))],
        compiler_params=pltpu.CompilerParams(use_tc_tiling_on_sc=True),
    )(w, m, v, g_here, g_there)


class _PackLayout:
    def __init__(self, n_cc, n_grp, G, widths):
        self.dw_rows = (0, HALO)
        self.wp_rows = (HALO, HALO + G)
        self.n_cc, self.n_grp, self.G = n_cc, n_grp, G
        self.vec = {}
        r = HALO + G
        for name, width in widths:
            self.vec[name] = (r, width)
            r += width // PACK_W
        self.rows = -(-r // 8) * 8


def _pack_small(layout, dwdw, dwp, vecs):
    names = list(vecs)

    def body(*refs):
        dw_ref, wp_ref = refs[0], refs[1]
        vec_refs = refs[2 : 2 + len(names)]
        o_ref = refs[-1]
        o_ref[...] = jnp.zeros_like(o_ref)
        for j in range(layout.n_cc):
            o_ref[layout.dw_rows[0] : layout.dw_rows[1], j * LANES : (j + 1) * LANES] = dw_ref[j]
        for i in range(layout.n_grp):
            o_ref[layout.wp_rows[0] : layout.wp_rows[1], i * layout.G : (i + 1) * layout.G] = wp_ref[i]
        for name, ref in zip(names, vec_refs):
            r, width = layout.vec[name]
            for h in range(width // PACK_W):
                o_ref[r + h : r + h + 1, :] = ref[:, h * PACK_W : (h + 1) * PACK_W]

    return pl.pallas_call(
        body,
        name="pack_small",
        out_shape=jax.ShapeDtypeStruct((layout.rows, PACK_W), F32),
    )(dwdw, dwp, *[vecs[k] for k in names])


def _adamw_small(layout, g_here, g_there, w_dw, m_dw, v_dw, w_pool, m_pool, v_pool, vec_w, vec_m, vec_v):
    names = list(vec_w)
    nv = len(names)

    def body(*refs):
        ga_ref, gb_ref = refs[0], refs[1]
        wdw, mdw, vdw, wp, mp, vp = refs[2:8]
        vw, vm, vv = refs[8 : 8 + nv], refs[8 + nv : 8 + 2 * nv], refs[8 + 2 * nv : 8 + 3 * nv]
        outs = refs[8 + 3 * nv :]
        acc = outs[-1]
        acc[...] = ga_ref[...] + gb_ref[...]

        def emit(o, g, w, m, v, idx=()):
            res = (g,) + _adamw_math(w, g, m, v)
            for ref, val in zip(o, res):
                ref[idx] = val

        me = 2 * lax.axis_index("x") + lax.axis_index("y")
        for j in range(layout.n_cc):

            @pl.when(me == j)
            def _(j=j):
                for k in range(wdw.shape[0]):
                    g = acc[layout.dw_rows[0] + k : layout.dw_rows[0] + k + 1, j * LANES : (j + 1) * LANES]
                    emit(outs[0:4], g, wdw[k], mdw[k], vdw[k], idx=k)

        for i in range(layout.n_grp):
            g = acc[layout.wp_rows[0] : layout.wp_rows[1], i * layout.G : (i + 1) * layout.G]
            emit(outs[4:8], g, wp[i], mp[i], vp[i], idx=i)
        for q, name in enumerate(names):
            r, width = layout.vec[name]
            for h in range(width // PACK_W):
                ls = slice(h * PACK_W, (h + 1) * PACK_W)
                g = acc[r + h : r + h + 1, :]
                emit(outs[8 + 4 * q : 12 + 4 * q], g, vw[q][:, ls], vm[q][:, ls], vv[q][:, ls], idx=(slice(None), ls))

    shapes = [w_dw.shape] * 4 + [w_pool.shape] * 4
    for name in names:
        shapes += [vec_w[name].shape] * 4
    return pl.pallas_call(
        body,
        name="adamw_small",
        out_shape=[jax.ShapeDtypeStruct(s, F32) for s in shapes],
        scratch_shapes=[pltpu.VMEM(g_here.shape, F32)],
    )(g_here, g_there, w_dw, m_dw, v_dw, w_pool, m_pool, v_pool,
      *[vec_w[k] for k in names], *[vec_m[k] for k in names], *[vec_v[k] for k in names])


def _allreduce_adamw_row(g_part, w, m, v, loss_part, comm=()):
    D = w.shape[1]
    n_pairs = N_DEV - 1

    def body(g_ref, w_ref, m_ref, v_ref, l_ref, go_ref, d_ref, nm_ref, nv_ref, lo_ref, land_g, land_l, sems):
        x, y, c = _place()
        copies = []
        for q, (src, land) in enumerate(((g_ref, land_g), (l_ref, land_l))):
            for r in range(1, N_DEV):
                fx, fy, fc = (r >> 2) & 1, (r >> 1) & 1, r & 1
                peer = (1 - x if fx else x, 1 - y if fy else y, 1 - c if fc else c)
                cp = _remote(src, land.at[r], sems, 2 * (q * n_pairs + r - 1), peer)
                cp.start()
                copies.append(cp)
        for cp in copies:
            cp.wait()

        def total(src, land):
            row = lambda r: src[...] if r == 0 else land[r]
            return ((row(0) + row(4)) + (row(2) + row(6))) + ((row(1) + row(5)) + (row(3) + row(7)))

        g = total(g_ref, land_g)
        go_ref[...] = g
        d_ref[...], nm_ref[...], nv_ref[...] = _adamw_math(w_ref[...], g, m_ref[...], v_ref[...])
        lo_ref[...] = total(l_ref, land_l)

    vm = pl.BlockSpec(memory_space=pltpu.VMEM)
    return _call(
        "allreduce_adamw_g_mix",
        body,
        (),
        [vm] * 5,
        [vm] * 5,
        [jax.ShapeDtypeStruct((1, D), F32)] * 4 + [jax.ShapeDtypeStruct(loss_part.shape, F32)],
        (g_part, w, m, v, loss_part),
        scratch=[pltpu.VMEM((N_DEV, 1, D), F32), pltpu.VMEM((N_DEV,) + loss_part.shape, F32),
                 pltpu.SemaphoreType.DMA((4 * n_pairs,))],
        comm=comm,
    )


def kernel(x, g_mix, w_in, b_in, w_dw, b_dw, ln_g, ln_b, w_pool, s_pool, w_out, g_ffn, w_gate, w_up, w_down, g_final, loss_target, m_g_mix, m_w_in, m_b_in, m_w_dw, m_b_dw, m_ln_g, m_ln_b, m_w_pool, m_s_pool, m_w_out, m_g_ffn, m_w_gate, m_w_up, m_w_down, m_g_final, v_g_mix, v_w_in, v_b_in, v_w_dw, v_b_dw, v_ln_g, v_ln_b, v_w_pool, v_s_pool, v_w_out, v_g_ffn, v_w_gate, v_w_up, v_w_down, v_g_final):
    x2 = x[0]
    target = loss_target[0]
    T, D = x2.shape
    w_in2, w_out2, w_down2 = w_in[0], w_out[0], w_down[0]
    taps_first = lambda a: jnp.transpose(a, (1, 0, 2))
    w_dw3 = taps_first(w_dw)
    w_gateT, w_upT = w_gate[0].T, w_up[0].T
    CI = w_in2.shape[1] * N_CHIPS
    DM = w_out2.shape[0] * N_CHIPS
    F = w_down2.shape[0] * N_CHIPS
    KW, _, dw_cols = w_dw3.shape
    assert dw_cols == LANES
    n_grp, G = w_pool.shape[1], w_pool.shape[-1]
    w_pool3 = w_pool[0]
    g_final2 = g_final.reshape(1, D)

    me = (2 * lax.axis_index("x") + lax.axis_index("y")).astype(jnp.int32).reshape(1)

    w_inT_b, w_dw4, f_out, f_gate, f_up, f_down = _place_and_gather(
        [(w_in2, "rows", (CI, D), BF16, True, True), (w_dw3, "lead", (N_CHIPS, KW, 1, dw_cols), F32, False, False)],
        [(w, "rows", shape, BF16, False, True)
         for w, shape in ((w_out2, (DM, D)), (w_gateT, (F, D)), (w_upT, (F, D)), (w_down2, (F, D)))])
    w_pool_b = w_pool3.astype(BF16)
    ici = lambda f: _GatherIci([f], ["rows"], [True])
    d2d = lambda f: _GatherD2d([f], ["rows"])
    g_out = _start("gather_out_start", ici(f_out))
    g_gate = _start("gather_gate_start", ici(f_gate), after=[g_out.token])
    g_up = _start("gather_up_start", ici(f_up), after=[g_gate.token])
    g_down = _start("gather_down_start", ici(f_down), after=[g_up.token])
    (z, xn_b), _ = _in_proj(x2, g_mix, w_inT_b, b_in, after=[g_down.token])
    (f_out,) = _wait("gather_out_wait", g_out, xn_b)
    s_out = _start("share_out_start", d2d(f_out))
    (y_b, v), _ = _seq_fwd(z, w_dw4, b_dw, ln_g, ln_b, w_pool_b, s_pool, after=[s_out.token])
    (w_out_b,) = _wait("share_out_wait", s_out, y_b)
    (f_gate,) = _wait("gather_gate_wait", g_gate, y_b)
    s_gate = _start("share_gate_start", d2d(f_gate))
    (h1, hn_b), _ = _out_proj(y_b, x2, w_out_b, g_ffn, after=[s_gate.token])
    (f_up,) = _wait("gather_up_wait", g_up, hn_b)
    s_up = _start("share_up_start", d2d(f_up))
    (wgT_b,) = _wait("share_gate_wait", s_gate, hn_b)
    (wuT_b,) = _wait("share_up_wait", s_up, hn_b)
    (g_b, u_b, a_b), _ = _gate_up(hn_b, wgT_b, wuT_b)
    (f_down,) = _wait("gather_down_wait", g_down, a_b)
    s_down = _start("share_down_start", d2d(f_down))
    (wd_b,) = _wait("share_down_wait", s_down, a_b)
    (dh2, dh2_b, loss_part, d_g_final), _ = _down_loss(a_b, wd_b, h1, target, g_final2)

    gw_down, _ = _weight_grad("grad_w_down", a_b, dh2_b)
    (dg_b, du_b), (p_down_xy,) = _ffn_bwd_act(dh2_b, wd_b, g_b, u_b, comm=[_Scatter([gw_down], ["rows"], which=(0, 1))])
    gw_gateT, (p_down_d,) = _weight_grad("grad_w_gate", dg_b, hn_b, comm=[_Scatter([gw_down], ["rows"], which=(2,))])
    gw_upT, _ = _weight_grad("grad_w_up", du_b, hn_b)
    sum_down = _sum_parts("sum_w_down", gw_down, "rows", [p_down_xy, p_down_d], me)
    (dh1, dh1_b, dy, d_g_ffn), (p_gate, oth_down) = _ffn_bwd_in(
        dg_b, du_b, wgT_b, wuT_b, h1, dh2, g_ffn, w_out_b, comm=[_Scatter([gw_gateT], ["rows"]), _Swap([sum_down])])
    gw_out, _ = _weight_grad("grad_w_out", y_b, dh1_b)
    sum_gate = _sum_parts("sum_w_gate", gw_gateT, "rows", [p_gate], me)
    res = {}
    res["w_down"] = _adamw_on_sparsecore("adamw_w_down", w_down2, m_w_down[0], v_w_down[0], sum_down, oth_down)
    (dz_b, d_wdw, d_bdw, d_lng, d_lnb, d_wp, d_sp, d_bin), (p_up, p_out, oth_gate) = _seq_bwd(
        z, dy, v, w_dw4, ln_g, ln_b, w_pool_b, s_pool,
        comm=[_Scatter([gw_upT, gw_out], ["rows", "rows"]), _Swap([sum_gate])])
    vec_grads = {"b_dw": d_bdw, "ln_g": d_lng, "ln_b": d_lnb, "s_pool": d_sp, "g_ffn": d_g_ffn, "g_final": d_g_final, "b_in": d_bin}
    layout = _PackLayout(dw_cols * N_CHIPS // LANES, n_grp, G, [(k, a.shape[1]) for k, a in vec_grads.items()])
    pack = _pack_small(layout, d_wdw, d_wp, vec_grads)
    sum_up = _sum_parts("sum_w_up", gw_upT, "rows", [p_up], me)
    sum_out = _sum_parts("sum_w_out", gw_out, "rows", [p_out], me)
    gw_inT, (p_small, oth_up, oth_out) = _weight_grad(
        "grad_w_in", dz_b, xn_b, comm=[_Scatter([pack], ["all"]), _Swap([sum_up, sum_out])])
    sum_small = _sum_parts("sum_small", pack, "all", [p_small], me)
    res["w_gate"] = _adamw_on_sparsecore("adamw_w_gate", w_gateT, m_w_gate[0].T, v_w_gate[0].T, sum_gate, oth_gate)
    send_sems, recv_sems, gw_thru, small_thru, p_in, oth_small, token = _late_exchange_start(gw_inT, sum_small)
    (grad_x, d_g_mix), _ = _in_proj_bwd(dz_b, w_inT_b, x2, dh1, g_mix + token[0, 0])
    gw_inT, sum_small, p_in, oth_small = _late_exchange_wait(
        send_sems, recv_sems, gw_thru, small_thru, p_in, oth_small, d_g_mix)
    res["w_up"] = _adamw_on_sparsecore("adamw_w_up", w_upT, m_w_up[0].T, v_w_up[0].T, sum_up, oth_up)
    res["w_out"] = _adamw_on_sparsecore("adamw_w_out", w_out2, m_w_out[0], v_w_out[0], sum_out, oth_out)
    sum_in = _sum_parts("sum_w_in", gw_inT, "rows", [p_in], me)
    (*res["g_mix"], loss_row), (oth_in,) = _allreduce_adamw_row(
        d_g_mix, g_mix, m_g_mix, v_g_mix, loss_part, comm=[_Swap([sum_in])])
    loss = loss_row[0, 0]
    res["w_in"], _ = _adamw("adamw_w_in", w_in2, m_w_in[0], v_w_in[0], sum_in, oth_in, g_transposed=True)

    vec_w = {"b_dw": b_dw, "ln_g": ln_g, "ln_b": ln_b, "s_pool": s_pool, "g_ffn": g_ffn, "g_final": g_final2, "b_in": b_in}
    vec_m = {"b_dw": m_b_dw, "ln_g": m_ln_g, "ln_b": m_ln_b, "s_pool": m_s_pool, "g_ffn": m_g_ffn,
             "g_final": m_g_final.reshape(1, D), "b_in": m_b_in}
    vec_v = {"b_dw": v_b_dw, "ln_g": v_ln_g, "ln_b": v_ln_b, "s_pool": v_s_pool, "g_ffn": v_g_ffn,
             "g_final": v_g_final.reshape(1, D), "b_in": v_b_in}
    small = _adamw_small(layout, sum_small, oth_small, w_dw3, taps_first(m_w_dw), taps_first(v_w_dw),
                         w_pool3, m_w_pool[0], v_w_pool[0], vec_w, vec_m, vec_v)
    res["w_dw"] = [taps_first(a) for a in small[0:4]]
    res["w_pool"] = [a[None] for a in small[4:8]]
    for q, k in enumerate(vec_w):
        res[k] = list(small[8 + 4 * q : 12 + 4 * q])
    res["g_final"] = [a.reshape(D) for a in res["g_final"]]
    for k in ("w_in", "w_out", "w_down"):
        res[k] = [a[None] for a in res[k]]
    for k in ("w_gate", "w_up"):
        res[k] = [a.T[None] for a in res[k]]

    order = ["g_mix", "w_in", "b_in", "w_dw", "b_dw", "ln_g", "ln_b", "w_pool", "s_pool", "w_out", "g_ffn", "w_gate", "w_up", "w_down", "g_final"]
    outs = [loss, grad_x[None]]
    for q in range(4):
        outs += [res[k][q] for k in order]
    return tuple(outs)
```

```python
import jax
import jax.numpy as jnp
from jax import lax
from jax.experimental import pallas as pl
from jax.experimental.pallas import tpu as pltpu
from jax.experimental.pallas import tpu_sc as plsc

F32 = jnp.float32
BF16 = jnp.bfloat16
MESH = pl.DeviceIdType.MESH
ANY = pl.BlockSpec(memory_space=pl.ANY)

RMS_EPS = 1e-6
LN_EPS = 1e-5
POOL_WINDOWS = (2, 4, 8, 16)
ADAM_LR = 0.001
ADAM_B1 = 0.9
ADAM_B2 = 0.999
ADAM_EPS = 1e-08
ADAM_WD = 0.01
ADAM_STEP = 10

LANES = 128
SUBLANES = 8
HALO = 32
CONV_ROWS = 64
HIDDEN_CHUNK = 512
VMEM_LIMIT = 56 * 1024 * 1024
PACK_W = 512
N_CHIPS = 4
N_DEV = 8
SIBLING_BARRIER_ID = 0
SC_CORES = 2
SC_TILES = 32
SC_LANES = 16


def _tile(n, want, mult=8):
    t = min(n, want)
    while n % t or t % mult:
        t -= 1
    return t


def _sigmoid(x):
    return 1.0 / (1.0 + jnp.exp(-x))


def _dot(a, b, dims):
    return lax.dot_general(a, b, (dims, ((), ())), preferred_element_type=F32)


NN = ((1,), (0,))
NT = ((1,), (1,))
TN = ((0,), (0,))


def _rms_bwd(x, g, dy):
    r = lax.rsqrt(jnp.mean(x * x, axis=-1, keepdims=True) + RMS_EPS)
    xh = x * r
    gy = dy * g
    dx = r * (gy - xh * jnp.mean(gy * xh, axis=-1, keepdims=True))
    return dx, dy * xh


def _accumulate(ref, first, val):
    @pl.when(first)
    def _():
        ref[...] = val

    @pl.when(jnp.logical_not(first))
    def _():
        ref[...] += val


def _place():
    return lax.axis_index("x"), lax.axis_index("y"), lax.axis_index("c")


def _other_chips(x, y):
    return [(1 - x, y), (x, 1 - y), (1 - x, 1 - y)]


def _rows(ref, start, n):
    return ref.at[pl.ds(pl.multiple_of(start, 16), n)]


def _window(ref, how, k, c=None):
    if how == "all":
        return ref
    if how == "lead":
        return ref.at[k]
    if how == "rows":
        n = ref.shape[0] // N_CHIPS
        if c is None:
            return _rows(ref, k * n, n)
        return _rows(ref, k * n + c * (n // 2), n // 2)
    n = ref.shape[1] // N_CHIPS
    cols = pl.ds(pl.multiple_of(k * n, LANES), n)
    if c is None:
        return ref.at[:, cols]
    h = ref.shape[0] // 2
    return ref.at[pl.ds(pl.multiple_of(c * h, 16), h), cols]


def _remote(src, dst, sems, s, device):
    return pltpu.make_async_remote_copy(
        src_ref=src, dst_ref=dst, send_sem=sems.at[s], recv_sem=sems.at[s + 1], device_id=device, device_id_type=MESH)


class _GatherIci:
    aliased = True

    def __init__(self, fulls, hows, splits, which=(0, 1, 2)):
        self.fulls, self.hows, self.splits, self.which = list(fulls), list(hows), list(splits), tuple(which)

    def inputs(self):
        return self.fulls

    def out_shapes(self):
        return [jax.ShapeDtypeStruct(a.shape, a.dtype) for a in self.fulls]

    def n_sems(self):
        return 6 * len(self.fulls)

    def build(self, ins, outs, sems, base):
        x, y, c = _place()
        me = 2 * x + y
        chips = _other_chips(x, y)
        starts, waits = [], []
        for a, (how, sp) in enumerate(zip(self.hows, self.splits)):
            half = c if sp else None
            mine = _window(outs[a], how, me, half)
            for j in self.which:
                px, py = chips[j]
                s = base + 6 * a + 2 * j
                cp = _remote(mine, mine, sems, s, (px, py, c))
                landing = _remote(mine, _window(outs[a], how, 2 * px + py, half), sems, s, (px, py, c))
                starts.append(cp.start)
                waits += [landing.wait_recv, cp.wait_send]
        return starts, waits


class _GatherD2d:
    aliased = True

    def __init__(self, fulls, hows):
        self.fulls, self.hows = list(fulls), list(hows)

    def inputs(self):
        return self.fulls

    def out_shapes(self):
        return [jax.ShapeDtypeStruct(a.shape, a.dtype) for a in self.fulls]

    def n_sems(self):
        return 6 * len(self.fulls)

    def build(self, ins, outs, sems, base):
        x, y, c = _place()
        starts, waits = [], []
        for a, how in enumerate(self.hows):
            for j, (px, py) in enumerate(_other_chips(x, y)):
                s = base + 6 * a + 2 * j
                got = _window(outs[a], how, 2 * px + py, c)
                cp = _remote(got, got, sems, s, (x, y, 1 - c))
                landing = _remote(got, _window(outs[a], how, 2 * px + py, 1 - c), sems, s, (x, y, 1 - c))
                starts.append(cp.start)
                waits += [landing.wait_recv, cp.wait_send]
        return starts, waits


def _part_shape(a, how):
    if how == "all":
        return a.shape
    if how == "rows":
        return (a.shape[0] // N_CHIPS, a.shape[1])
    return (a.shape[0], a.shape[1] // N_CHIPS)


class _Scatter:
    aliased = False

    def __init__(self, fulls, hows, which=(0, 1, 2)):
        self.fulls, self.hows, self.which = list(fulls), list(hows), tuple(which)

    def inputs(self):
        return self.fulls

    def out_shapes(self):
        return [jax.ShapeDtypeStruct((len(self.which),) + _part_shape(a, h), a.dtype) for a, h in zip(self.fulls, self.hows)]

    def n_sems(self):
        return 6 * len(self.fulls)

    def build(self, ins, outs, sems, base):
        x, y, c = _place()
        chips = _other_chips(x, y)
        starts, waits = [], []
        for a, how in enumerate(self.hows):
            for slot, j in enumerate(self.which):
                px, py = chips[j]
                cp = _remote(_window(ins[a], how, 2 * px + py), outs[a].at[slot], sems, base + 6 * a + 2 * j, (px, py, c))
                starts.append(cp.start)
                waits += [cp.wait_recv, cp.wait_send]
        return starts, waits


class _Swap:
    aliased = False

    def __init__(self, arrays):
        self.arrays = list(arrays)

    def inputs(self):
        return self.arrays

    def out_shapes(self):
        return [jax.ShapeDtypeStruct(a.shape, a.dtype) for a in self.arrays]

    def n_sems(self):
        return 2 * len(self.arrays)

    def build(self, ins, outs, sems, base):
        x, y, c = _place()
        starts, waits = [], []
        for a in range(len(ins)):
            cp = _remote(ins[a], outs[a], sems, base + 2 * a, (x, y, 1 - c))
            starts.append(cp.start)
            waits += [cp.wait_recv, cp.wait_send]
        return starts, waits


def _call(name, body, grid, in_specs, out_specs, out_shape, args, scratch=(), comm=(), after=()):
    comm, after = list(comm), list(after)
    n_in, n_out, n_scr, n_after = len(args), len(out_shape), len(scratch), len(after)
    c_in = [a for op in comm for a in op.inputs()]
    c_out = [s for op in comm for s in op.out_shapes()]
    n_sems = sum(op.n_sems() for op in comm)
    aliases, i_in, i_out = {}, 0, 0
    for op in comm:
        if op.aliased:
            for q in range(len(op.inputs())):
                aliases[n_in + n_after + i_in + q] = n_out + i_out + q
        i_in, i_out = i_in + len(op.inputs()), i_out + len(op.out_shapes())

    def wrapped(*refs):
        ins = refs[:n_in]
        cin = refs[n_in + n_after : n_in + n_after + len(c_in)]
        o0 = n_in + n_after + len(c_in)
        outs = refs[o0 : o0 + n_out]
        cout = refs[o0 + n_out : o0 + n_out + len(c_out)]
        s0 = o0 + n_out + len(c_out)
        scr = refs[s0 : s0 + n_scr]

        def copies():
            sems = refs[s0 + n_scr]
            starts, waits = [], []
            i_in = i_out = base = 0
            for op in comm:
                ni, no = len(op.inputs()), len(op.out_shapes())
                s, w = op.build(cin[i_in : i_in + ni], cout[i_out : i_out + no], sems, base)
                starts += s
                waits += w
                i_in, i_out, base = i_in + ni, i_out + no, base + op.n_sems()
            return starts, waits

        def run_starts():
            for start in copies()[0]:
                start()

        def run_waits():
            for wait in copies()[1]:
                wait()

        if comm and grid:
            first = last = True
            for d, n in enumerate(grid):
                first = jnp.logical_and(first, pl.program_id(d) == 0)
                last = jnp.logical_and(last, pl.program_id(d) == n - 1)
            pl.when(first)(run_starts)
        elif comm:
            run_starts()
        if body is not None:
            body(*ins, *outs, *scr)
        if comm and grid:
            pl.when(last)(run_waits)
        elif comm:
            run_waits()

    res = pl.pallas_call(
        wrapped,
        name=name,
        grid=grid,
        in_specs=list(in_specs) + [ANY] * (n_after + len(c_in)),
        out_specs=list(out_specs) + [ANY] * len(c_out),
        out_shape=list(out_shape) + c_out,
        scratch_shapes=list(scratch) + ([pltpu.SemaphoreType.DMA((n_sems,))] if comm else []),
        input_output_aliases=aliases,
        compiler_params=pltpu.CompilerParams(dimension_semantics=("arbitrary",) * len(grid), vmem_limit_bytes=VMEM_LIMIT),
    )(*args, *after, *c_in)
    return tuple(res[:n_out]), tuple(res[n_out:])


def _place_and_gather(now, later):
    items = list(now) + list(later)
    n, n_now = len(items), len(now)
    buf_shape = lambda it: it[0].shape[::-1] if it[4] else it[0].shape
    split_now = [a for a in range(n_now) if items[a][5]]

    def body(*refs):
        ins, outs = refs[:n], refs[n : 2 * n]
        stage, bufs = refs[2 * n : 3 * n - n_now], refs[3 * n - n_now : 4 * n - n_now]
        sems = refs[4 * n - n_now]
        x, y, c = _place()
        me = 2 * x + y
        chips = _other_chips(x, y)
        loads = [pltpu.make_async_copy(ins[a], stage[a - n_now], sems.at[a]) for a in range(n_now, n)]
        for ld in loads:
            ld.start()
        pending = []

        def place(a, val):
            _, how, _, dtype, transposed, _ = items[a]
            bufs[a][...] = (val.T if transposed else val).astype(dtype)
            cp = pltpu.make_async_copy(bufs[a], _window(outs[a], how, me), sems.at[n + a])
            cp.start()
            pending.append(cp.wait)

        arrivals = []
        for a in range(n_now):
            place(a, ins[a][...])
            how, split = items[a][1], items[a][5]
            half = c if split else None
            src = _rows(bufs[a], c * (bufs[a].shape[0] // 2), bufs[a].shape[0] // 2) if split else bufs[a]
            for j, (px, py) in enumerate(chips):
                s = 2 * n + 6 * a + 2 * j
                cp = _remote(src, _window(outs[a], how, me, half), sems, s, (px, py, c))
                landing = _remote(src, _window(outs[a], how, 2 * px + py, half), sems, s, (px, py, c))
                cp.start()
                arrivals.append(landing.wait_recv)
                pending.append(cp.wait_send)
        for a in range(n_now, n):
            loads[a - n_now].wait()
            place(a, stage[a - n_now][...])
        for wait in arrivals:
            wait()
        d2d = _GatherD2d([None] * len(split_now), [items[a][1] for a in split_now])
        starts, waits = d2d.build(None, [outs[a] for a in split_now], sems, 2 * n + 6 * n_now)
        for start in starts:
            start()
        for wait in waits + pending:
            wait()

    vm = pl.BlockSpec(memory_space=pltpu.VMEM)
    return pl.pallas_call(
        body,
        name="place_and_gather",
        in_specs=[vm] * n_now + [ANY] * (n - n_now),
        out_specs=[ANY] * n,
        out_shape=[jax.ShapeDtypeStruct(it[2], it[3]) for it in items],
        scratch_shapes=[pltpu.VMEM(it[0].shape, it[0].dtype) for it in later]
        + [pltpu.VMEM(buf_shape(it), it[3]) for it in items]
        + [pltpu.SemaphoreType.DMA((2 * n + 6 * n_now + 6 * len(split_now),))],
        compiler_params=pltpu.CompilerParams(vmem_limit_bytes=VMEM_LIMIT),
    )(*[it[0] for it in items])


_HBM = pl.BlockSpec(memory_space=pltpu.HBM)
_SEM = pl.BlockSpec(memory_space=pltpu.SEMAPHORE)
_DATAFLOW = pltpu.SideEffectType.DATAFLOW_SIDE_EFFECTING


class _Pending:
    def __init__(self, ops, bases, sems, arrays, token):
        self.ops, self.bases, self.sems, self.arrays, self.token = ops, bases, sems, arrays, token


def _op_refs(op, refs):
    n_src = len(op.inputs())
    return refs[:n_src], (refs[:n_src] if op.aliased else refs[n_src:])


def _start(name, ops, after=(), sibling_only=False):
    per_op = [list(op.inputs()) + ([] if op.aliased else [lax.empty(sd.shape, sd.dtype) for sd in op.out_shapes()])
              for op in ops]
    arrays = [a for group in per_op for a in group]
    bases = [sum(op.n_sems() for op in ops[:k]) for k in range(len(ops))]
    n, after = len(arrays), list(after)

    def body(*refs):
        sems, token = refs[n + len(after)], refs[-1]
        if sibling_only:
            x, y, c = _place()
            barrier = pltpu.get_barrier_semaphore()
            pl.semaphore_signal(barrier, inc=1, device_id=(x, y, 1 - c), device_id_type=MESH)
            pl.semaphore_wait(barrier, 1)
        at = 0
        for op, group, base in zip(ops, per_op, bases):
            starts, _ = op.build(*_op_refs(op, refs[at : at + len(group)]), sems, base)
            for start in starts:
                start()
            at += len(group)
        token[...] = jnp.zeros_like(token)

    res = pl.pallas_call(
        body,
        name=name,
        out_shape=(pltpu.SemaphoreType.DMA((sum(op.n_sems() for op in ops),)),)
        + tuple(pltpu.HBM(a.shape, a.dtype) for a in arrays) + (jax.ShapeDtypeStruct((SUBLANES, LANES), F32),),
        in_specs=(_HBM,) * n + (ANY,) * len(after),
        out_specs=(_SEM,) + (_HBM,) * n + (pl.BlockSpec(memory_space=pltpu.VMEM),),
        input_output_aliases={i: 1 + i for i in range(n)},
        compiler_params=pltpu.CompilerParams(
            has_side_effects=_DATAFLOW, collective_id=SIBLING_BARRIER_ID if sibling_only else None),
    )(*[pltpu.with_memory_space_constraint(a, pltpu.HBM) for a in arrays], *after)
    thru, at, groups = list(res[1 : 1 + n]), 0, []
    for group in per_op:
        groups.append(thru[at : at + len(group)])
        at += len(group)
    return _Pending(list(ops), bases, res[0], groups, res[-1])


def _wait(name, pending, k, after):
    op, arrays = pending.ops[k], pending.arrays[k]
    n = len(arrays)

    def body(*refs):
        _, waits = op.build(*_op_refs(op, refs[:n]), refs[n], pending.bases[k])
        for wait in waits:
            wait()

    return pl.pallas_call(
        body,
        name=name,
        out_shape=tuple(pltpu.HBM(a.shape, a.dtype) for a in arrays),
        in_specs=(_HBM,) * n + (_SEM, ANY),
        out_specs=(_HBM,) * n,
        input_output_aliases={i: i for i in range(n)},
        compiler_params=pltpu.CompilerParams(has_side_effects=_DATAFLOW),
    )(*arrays, pending.sems, after)


def _in_proj(x, g_mix, w_inT_b, b_in, comm=(), after=()):
    T, D = x.shape
    CI = w_inT_b.shape[0]
    tm = _tile(T, 512)

    def body(x_ref, g_ref, w_ref, b_ref, z_ref, xn_ref):
        xv = x_ref[...]
        r = lax.rsqrt(jnp.mean(xv * xv, axis=-1, keepdims=True) + RMS_EPS)
        xn = (xv * r * g_ref[...]).astype(BF16)
        xn_ref[...] = xn
        z_ref[...] = _dot(xn, w_ref[...], NT) + b_ref[...]

    return _call(
        "in_proj",
        body,
        (T // tm,),
        [
            pl.BlockSpec((tm, D), lambda i: (i, 0)),
            pl.BlockSpec((1, D), lambda i: (0, 0)),
            pl.BlockSpec((CI, D), lambda i: (0, 0)),
            pl.BlockSpec((1, CI), lambda i: (0, 0)),
        ],
        [pl.BlockSpec((tm, CI), lambda i: (i, 0)), pl.BlockSpec((tm, D), lambda i: (i, 0))],
        [jax.ShapeDtypeStruct((T, CI), F32), jax.ShapeDtypeStruct((T, D), BF16)],
        (x, g_mix, w_inT_b, b_in),
        comm=comm,
        after=after,
    )


def _fill_shifted(scr):
    n = scr.shape[1] - SUBLANES
    for s in range(1, SUBLANES):
        scr[s, 0:n, :] = scr[0, s : s + n, :]


def _shifted_rows(scr, off, n, cs):
    s = off % SUBLANES
    return scr[s, off - s : off - s + n, cs]


def _pool_mean_minus_token(p_scr, cs, w, cnt, tt):
    tok = p_scr[HALO : HALO + tt, cs]
    s = tok
    for d in range(1, w):
        s = s + p_scr[HALO - d : HALO - d + tt, cs]
    return s / cnt - tok


def _seq_fwd(z, w_dw4, b_dw, ln_g, ln_b, w_pool_b, s_pool, comm=(), after=()):
    T, CI = z.shape
    CC = ln_g.shape[1]
    n_grp, G = w_pool_b.shape[0], w_pool_b.shape[-1]
    KW = w_dw4.shape[1]
    D = CC + n_grp * G
    tt = _tile(T, 512, HALO)
    per = tt // HALO

    def body(zc_ref, zp_ref, wdw_ref, bdw_ref, lng_ref, lnb_ref, wp_ref, sp_ref, y_ref, v_ref, u_scr, p_scr):
        i = pl.program_id(0)
        first = i == 0
        u_prev = zp_ref[:, 0:CC] * _sigmoid(zp_ref[:, CC : 2 * CC])
        u_scr[0, 0:HALO, :] = jnp.where(first, 0.0, u_prev)
        p_scr[0:HALO, :] = jnp.where(first, 0.0, zp_ref[:, 2 * CC :])
        u_scr[0, HALO:, :] = zc_ref[:, 0:CC] * _sigmoid(zc_ref[:, CC : 2 * CC])
        p_scr[HALO:, :] = zc_ref[:, 2 * CC :]
        _fill_shifted(u_scr)

        for j in range(CC // LANES):
            cs = slice(LANES * j, LANES * (j + 1))
            for rb in range(tt // CONV_ROWS):
                acc = jnp.zeros((CONV_ROWS, LANES), F32)
                for k in range(KW):
                    off = HALO - (KW - 1) + k + rb * CONV_ROWS
                    acc = acc + _shifted_rows(u_scr, off, CONV_ROWS, cs) * wdw_ref[j, k]
                v_ref[rb * CONV_ROWS : (rb + 1) * CONV_ROWS, cs] = acc + bdw_ref[:, cs]

        v = v_ref[...]
        mu = jnp.mean(v, axis=-1, keepdims=True)
        d = v - mu
        var = jnp.mean(d * d, axis=-1, keepdims=True)
        ln = d * lax.rsqrt(var + LN_EPS) * lng_ref[...] + lnb_ref[...]
        y_ref[:, 0:CC] = (ln * _sigmoid(ln)).astype(BF16)

        tpos = i * tt + lax.broadcasted_iota(jnp.int32, (tt, 1), 0)
        for gi, w in enumerate(POOL_WINDOWS):
            cs = slice(G * gi, G * (gi + 1))
            cnt = jnp.minimum(tpos + 1, w).astype(F32)
            yi = _pool_mean_minus_token(p_scr, cs, w, cnt, tt)
            q = _dot(yi.astype(BF16), wp_ref[gi], NN)
            y_ref[:, CC + G * gi : CC + G * (gi + 1)] = (q * sp_ref[:, cs]).astype(BF16)

    const2 = lambda i: (0, 0)
    return _call(
        "seq_fwd",
        body,
        (T // tt,),
        [
            pl.BlockSpec((tt, CI), lambda i: (i, 0)),
            pl.BlockSpec((HALO, CI), lambda i: (jnp.maximum(i * per - 1, 0), 0)),
            pl.BlockSpec(w_dw4.shape, lambda i: (0,) * w_dw4.ndim),
            pl.BlockSpec((1, CC), const2),
            pl.BlockSpec((1, CC), const2),
            pl.BlockSpec((1, CC), const2),
            pl.BlockSpec(w_pool_b.shape, lambda i: (0, 0, 0)),
            pl.BlockSpec((1, n_grp * G), const2),
        ],
        [pl.BlockSpec((tt, D), lambda i: (i, 0)), pl.BlockSpec((tt, CC), lambda i: (i, 0))],
        [jax.ShapeDtypeStruct((T, D), BF16), jax.ShapeDtypeStruct((T, CC), F32)],
        (z, z, w_dw4, b_dw, ln_g, ln_b, w_pool_b, s_pool),
        scratch=[pltpu.VMEM((SUBLANES, HALO + tt, CC), F32), pltpu.VMEM((HALO + tt, n_grp * G), F32)],
        comm=comm,
        after=after,
    )


def _out_proj(y_b, x, w_out_b, g_ffn, comm=(), after=()):
    T, D = x.shape
    tm = _tile(T, 512)

    def body(y_ref, x_ref, w_ref, g_ref, h1_ref, hn_ref):
        h1 = x_ref[...] + _dot(y_ref[...], w_ref[...], NN)
        h1_ref[...] = h1
        r = lax.rsqrt(jnp.mean(h1 * h1, axis=-1, keepdims=True) + RMS_EPS)
        hn_ref[...] = (h1 * r * g_ref[...]).astype(BF16)

    row = lambda i: (i, 0)
    return _call(
        "out_proj",
        body,
        (T // tm,),
        [
            pl.BlockSpec((tm, y_b.shape[1]), row),
            pl.BlockSpec((tm, D), row),
            pl.BlockSpec(w_out_b.shape, lambda i: (0, 0)),
            pl.BlockSpec((1, D), lambda i: (0, 0)),
        ],
        [pl.BlockSpec((tm, D), row), pl.BlockSpec((tm, D), row)],
        [jax.ShapeDtypeStruct((T, D), F32), jax.ShapeDtypeStruct((T, D), BF16)],
        (y_b, x, w_out_b, g_ffn),
        comm=comm,
        after=after,
    )


def _hidden_tile(F):
    return _tile(F, 1408, LANES)


def _gate_up(hn_b, wgT_b, wuT_b, comm=()):
    T, D = hn_b.shape
    F = wgT_b.shape[0]
    tm, tf = _tile(T, 1024), _hidden_tile(F)

    def body(hn_ref, wg_ref, wu_ref, g_ref, u_ref, a_ref):
        hn = hn_ref[...]
        for c0 in range(0, tf, HIDDEN_CHUNK):
            cs = slice(c0, min(c0 + HIDDEN_CHUNK, tf))
            gv = _dot(hn, wg_ref[cs, :], NT)
            uv = _dot(hn, wu_ref[cs, :], NT)
            g_ref[:, cs] = gv.astype(BF16)
            u_ref[:, cs] = uv.astype(BF16)
            a_ref[:, cs] = (gv * _sigmoid(gv) * uv).astype(BF16)

    wspec = pl.BlockSpec((tf, D), lambda j, i: (j, 0))
    ospec = pl.BlockSpec((tm, tf), lambda j, i: (i, j))
    return _call(
        "gate_up",
        body,
        (F // tf, T // tm),
        [pl.BlockSpec((tm, D), lambda j, i: (i, 0)), wspec, wspec],
        [ospec, ospec, ospec],
        [jax.ShapeDtypeStruct((T, F), BF16)] * 3,
        (hn_b, wgT_b, wuT_b),
        comm=comm,
    )


def _down_loss(a_b, wd_b, h1, target, g_final, comm=()):
    T, D = h1.shape
    F = a_b.shape[1]
    tm = _tile(T, 512)
    nt = T // tm

    def body(a_ref, w_ref, h1_ref, t_ref, g_ref, dh2_ref, dh2b_ref, loss_ref, dg_ref):
        i = pl.program_id(0)
        h2 = h1_ref[...] + _dot(a_ref[...], w_ref[...], NN)
        r = lax.rsqrt(jnp.mean(h2 * h2, axis=-1, keepdims=True) + RMS_EPS)
        g = g_ref[...]
        diff = h2 * r * g - t_ref[...]
        _accumulate(loss_ref, i == 0, jnp.full(loss_ref.shape, jnp.sum(diff * diff) * (0.5 / D), F32))
        dh2, dg_rows = _rms_bwd(h2, g, diff * (1.0 / D))
        dh2_ref[...] = dh2
        dh2b_ref[...] = dh2.astype(BF16)
        _accumulate(dg_ref, i == 0, jnp.sum(dg_rows, axis=0, keepdims=True))

    row = lambda i: (i, 0)
    return _call(
        "down_loss",
        body,
        (nt,),
        [
            pl.BlockSpec((tm, F), row),
            pl.BlockSpec((F, D), lambda i: (0, 0), pipeline_mode=pl.Buffered(1)),
            pl.BlockSpec((tm, D), row),
            pl.BlockSpec((tm, D), row),
            pl.BlockSpec((1, D), lambda i: (0, 0)),
        ],
        [
            pl.BlockSpec((tm, D), row),
            pl.BlockSpec((tm, D), row),
            pl.BlockSpec((1, LANES), lambda i: (0, 0)),
            pl.BlockSpec((1, D), lambda i: (0, 0)),
        ],
        [
            jax.ShapeDtypeStruct((T, D), F32),
            jax.ShapeDtypeStruct((T, D), BF16),
            jax.ShapeDtypeStruct((1, LANES), F32),
            jax.ShapeDtypeStruct((1, D), F32),
        ],
        (a_b, wd_b, h1, target, g_final),
        comm=comm,
    )


def _ffn_bwd_act(dh2_b, wd_b, g_b, u_b, comm=()):
    T, D = dh2_b.shape
    F = wd_b.shape[0]
    tm, tf = _tile(T, 1024), _hidden_tile(F)

    def body(d_ref, w_ref, g_ref, u_ref, dg_ref, du_ref):
        d = d_ref[...]
        for c0 in range(0, tf, HIDDEN_CHUNK):
            cs = slice(c0, min(c0 + HIDDEN_CHUNK, tf))
            da = _dot(d, w_ref[cs, :], NT)
            gv = g_ref[:, cs].astype(F32)
            uv = u_ref[:, cs].astype(F32)
            sg = _sigmoid(gv)
            silu = gv * sg
            dg_ref[:, cs] = (da * uv * (sg * (1.0 + gv * (1.0 - sg)))).astype(BF16)
            du_ref[:, cs] = (da * silu).astype(BF16)

    aspec = pl.BlockSpec((tm, tf), lambda j, i: (i, j))
    return _call(
        "ffn_bwd_act",
        body,
        (F // tf, T // tm),
        [pl.BlockSpec((tm, D), lambda j, i: (i, 0)), pl.BlockSpec((tf, D), lambda j, i: (j, 0)), aspec, aspec],
        [aspec, aspec],
        [jax.ShapeDtypeStruct((T, F), BF16)] * 2,
        (dh2_b, wd_b, g_b, u_b),
        comm=comm,
    )


def _ffn_bwd_in(dg_b, du_b, wgT_b, wuT_b, h1, dh2, g_ffn, w_out_b, comm=()):
    T, D = h1.shape
    F = wgT_b.shape[0]
    DM = w_out_b.shape[0]
    tm = _tile(T, 512)

    def body(dg_ref, du_ref, wg_ref, wu_ref, h1_ref, dh2_ref, g_ref, wo_ref, dh1_ref, dh1b_ref, dy_ref, dgf_ref):
        i = pl.program_id(0)
        dhn = _dot(dg_ref[...], wg_ref[...], NN) + _dot(du_ref[...], wu_ref[...], NN)
        dx, dg_rows = _rms_bwd(h1_ref[...], g_ref[...], dhn)
        dh1 = dh2_ref[...] + dx
        dh1b = dh1.astype(BF16)
        dh1_ref[...] = dh1
        dh1b_ref[...] = dh1b
        dy_ref[...] = _dot(dh1b, wo_ref[...], NT)
        _accumulate(dgf_ref, i == 0, jnp.sum(dg_rows, axis=0, keepdims=True))

    row = lambda i: (i, 0)
    const = lambda i: (0, 0)
    return _call(
        "ffn_bwd_in",
        body,
        (T // tm,),
        [
            pl.BlockSpec((tm, F), row),
            pl.BlockSpec((tm, F), row),
            pl.BlockSpec((F, D), const, pipeline_mode=pl.Buffered(1)),
            pl.BlockSpec((F, D), const, pipeline_mode=pl.Buffered(1)),
            pl.BlockSpec((tm, D), row),
            pl.BlockSpec((tm, D), row),
            pl.BlockSpec((1, D), const),
            pl.BlockSpec((DM, D), const, pipeline_mode=pl.Buffered(1)),
        ],
        [pl.BlockSpec((tm, D), row), pl.BlockSpec((tm, D), row), pl.BlockSpec((tm, DM), row), pl.BlockSpec((1, D), const)],
        [
            jax.ShapeDtypeStruct((T, D), F32),
            jax.ShapeDtypeStruct((T, D), BF16),
            jax.ShapeDtypeStruct((T, DM), F32),
            jax.ShapeDtypeStruct((1, D), F32),
        ],
        (dg_b, du_b, wgT_b, wuT_b, h1, dh2, g_ffn, w_out_b),
        comm=comm,
    )


def _seq_bwd(z, dy, v, w_dw4, ln_g, ln_b, w_pool_b, s_pool, comm=()):
    T, CI = z.shape
    CC = ln_g.shape[1]
    n_grp, G = w_pool_b.shape[0], w_pool_b.shape[-1]
    CP = n_grp * G
    KW = w_dw4.shape[1]
    n_cc = CC // LANES
    D = CC + CP
    tt = _tile(T, 512, HALO)
    per = tt // HALO
    n_tiles = T // tt
    last_halo = T // HALO - 1

    def body(zc_ref, zp_ref, dyc_ref, dyn_ref, vc_ref, vn_ref, wdw_ref, lng_ref, lnb_ref, wp_ref, sp_ref,
             dz_ref, dwdw_ref, dbdw_ref, dlng_ref, dlnb_ref, dwp_ref, dsp_ref, dbin_ref,
             dv_scr, u_scr, p_scr, g_scr, dw_scr):
        i = pl.program_id(0)
        first = i == 0
        last = i == n_tiles - 1
        lng, lnb = lng_ref[...], lnb_ref[...]

        def conv_pre(vv, dyc):
            mu = jnp.mean(vv, axis=-1, keepdims=True)
            d = vv - mu
            rs = lax.rsqrt(jnp.mean(d * d, axis=-1, keepdims=True) + LN_EPS)
            xh = d * rs
            ln = xh * lng + lnb
            sg = _sigmoid(ln)
            dln = dyc * (sg * (1.0 + ln * (1.0 - sg)))
            dxh = dln * lng
            dv = rs * (dxh - jnp.mean(dxh, axis=-1, keepdims=True) - xh * jnp.mean(dxh * xh, axis=-1, keepdims=True))
            return dv, dln, xh

        dv_c, dln_c, xh_c = conv_pre(vc_ref[...], dyc_ref[:, 0:CC])
        dv_scr[0, 0:tt, :] = dv_c
        dv_n, _, _ = conv_pre(vn_ref[...], dyn_ref[:, 0:CC])
        dv_scr[0, tt:, :] = jnp.where(last, 0.0, dv_n)
        _fill_shifted(dv_scr)
        _accumulate(dlng_ref, first, jnp.sum(dln_c * xh_c, axis=0, keepdims=True))
        _accumulate(dlnb_ref, first, jnp.sum(dln_c, axis=0, keepdims=True))
        _accumulate(dbdw_ref, first, jnp.sum(dv_c, axis=0, keepdims=True))

        u_scr[...] = zc_ref[:, 0:CC] * _sigmoid(zc_ref[:, CC : 2 * CC])

        @pl.when(first)
        def _():
            dw_scr[...] = jnp.zeros_like(dw_scr)

        for j in range(n_cc):
            cs = slice(LANES * j, LANES * (j + 1))
            gs = slice(CC + LANES * j, CC + LANES * (j + 1))
            dbin_a = jnp.zeros((1, LANES), F32)
            dbin_g = jnp.zeros((1, LANES), F32)
            for rb in range(tt // CONV_ROWS):
                rows = slice(rb * CONV_ROWS, (rb + 1) * CONV_ROWS)
                u_blk = u_scr[rows, cs]
                du = jnp.zeros((CONV_ROWS, LANES), F32)
                for k in range(KW):
                    off = rb * CONV_ROWS + (KW - 1) - k
                    d = _shifted_rows(dv_scr, off, CONV_ROWS, cs)
                    du = du + d * wdw_ref[j, k]
                    dw_scr[j * HALO + k] += jnp.sum((u_blk * d).reshape(CONV_ROWS // 8, 8, LANES), axis=0)
                a = zc_ref[rows, cs]
                sg = _sigmoid(zc_ref[rows, gs])
                da = du * sg
                dgate = du * a * sg * (1.0 - sg)
                dz_ref[rows, cs] = da.astype(BF16)
                dz_ref[rows, gs] = dgate.astype(BF16)
                dbin_a = dbin_a + jnp.sum(da, axis=0, keepdims=True)
                dbin_g = dbin_g + jnp.sum(dgate, axis=0, keepdims=True)
            _accumulate(dbin_ref.at[:, cs], first, dbin_a)
            _accumulate(dbin_ref.at[:, gs], first, dbin_g)

        @pl.when(last)
        def _():
            dwdw_ref[...] = jnp.sum(dw_scr[...], axis=1).reshape(dwdw_ref.shape)

        p_scr[0:HALO, :] = jnp.where(first, 0.0, zp_ref[:, 2 * CC :])
        p_scr[HALO:, :] = zc_ref[:, 2 * CC :]
        tpos = i * tt + lax.broadcasted_iota(jnp.int32, (tt, 1), 0)
        for gi, w in enumerate(POOL_WINDOWS):
            cs = slice(G * gi, G * (gi + 1))
            ys = slice(CC + G * gi, CC + G * (gi + 1))
            ps = slice(2 * CC + G * gi, 2 * CC + G * (gi + 1))
            cnt = jnp.minimum(tpos + 1, w).astype(F32)
            yib = _pool_mean_minus_token(p_scr, cs, w, cnt, tt).astype(BF16)
            wp = wp_ref[gi]
            sp = sp_ref[:, cs]
            dyp = dyc_ref[:, ys]
            q = _dot(yib, wp, NN)
            _accumulate(dsp_ref.at[:, cs], first, jnp.sum(dyp * q, axis=0, keepdims=True))
            dq_c = (dyp * sp).astype(BF16)
            dq_n = (jnp.where(last, 0.0, dyn_ref[:, ys]) * sp).astype(BF16)
            _accumulate(dwp_ref.at[gi], first, _dot(yib, dq_c, TN))
            dyi_c = _dot(dq_c, wp, NT)
            g_scr[0:tt, cs] = dyi_c / cnt
            g_scr[tt:, cs] = _dot(dq_n, wp, NT) * (1.0 / w)
            dp = -dyi_c
            for d in range(w):
                dp = dp + g_scr[d : d + tt, cs]
            dz_ref[:, ps] = dp.astype(BF16)
            _accumulate(dbin_ref.at[:, ps], first, jnp.sum(dp, axis=0, keepdims=True))

    cur = lambda i: (i, 0)
    prev = lambda i: (jnp.maximum(i * per - 1, 0), 0)
    nxt = lambda i: (jnp.minimum((i + 1) * per, last_halo), 0)
    c2 = lambda i: (0, 0)
    c3 = lambda i: (0, 0, 0)
    return _call(
        "seq_bwd",
        body,
        (n_tiles,),
        [
            pl.BlockSpec((tt, CI), cur),
            pl.BlockSpec((HALO, CI), prev),
            pl.BlockSpec((tt, D), cur),
            pl.BlockSpec((HALO, D), nxt),
            pl.BlockSpec((tt, CC), cur),
            pl.BlockSpec((HALO, CC), nxt),
            pl.BlockSpec(w_dw4.shape, lambda i: (0,) * w_dw4.ndim),
            pl.BlockSpec((1, CC), c2),
            pl.BlockSpec((1, CC), c2),
            pl.BlockSpec(w_pool_b.shape, c3),
            pl.BlockSpec((1, CP), c2),
        ],
        [
            pl.BlockSpec((tt, CI), cur),
            pl.BlockSpec((n_cc, HALO, LANES), c3),
            pl.BlockSpec((1, CC), c2),
            pl.BlockSpec((1, CC), c2),
            pl.BlockSpec((1, CC), c2),
            pl.BlockSpec((n_grp, G, G), c3),
            pl.BlockSpec((1, CP), c2),
            pl.BlockSpec((1, CI), c2),
        ],
        [
            jax.ShapeDtypeStruct((T, CI), BF16),
            jax.ShapeDtypeStruct((n_cc, HALO, LANES), F32),
            jax.ShapeDtypeStruct((1, CC), F32),
            jax.ShapeDtypeStruct((1, CC), F32),
            jax.ShapeDtypeStruct((1, CC), F32),
            jax.ShapeDtypeStruct((n_grp, G, G), F32),
            jax.ShapeDtypeStruct((1, CP), F32),
            jax.ShapeDtypeStruct((1, CI), F32),
        ],
        (z, z, dy, dy, v, v, w_dw4, ln_g, ln_b, w_pool_b, s_pool),
        scratch=[
            pltpu.VMEM((SUBLANES, tt + HALO, CC), F32),
            pltpu.VMEM((tt, CC), F32),
            pltpu.VMEM((HALO + tt, CP), F32),
            pltpu.VMEM((tt + HALO, CP), F32),
            pltpu.VMEM((n_cc * HALO, 8, LANES), F32),
        ],
        comm=comm,
    )


def _in_proj_bwd(dz_b, w_inT_b, x, dh1, g_mix, comm=(), after=()):
    T, D = x.shape
    CI = w_inT_b.shape[0]
    tm = _tile(T, 512)

    def body(dz_ref, w_ref, x_ref, dh1_ref, g_ref, dx_ref, dg_ref):
        i = pl.program_id(0)
        dxn = _dot(dz_ref[...], w_ref[...], NN)
        dx, dg_rows = _rms_bwd(x_ref[...], g_ref[...], dxn)
        dx_ref[...] = dh1_ref[...] + dx
        _accumulate(dg_ref, i == 0, jnp.sum(dg_rows, axis=0, keepdims=True))

    row = lambda i: (i, 0)
    const = lambda i: (0, 0)
    return _call(
        "in_proj_bwd",
        body,
        (T // tm,),
        [
            pl.BlockSpec((tm, CI), row),
            pl.BlockSpec((CI, D), const),
            pl.BlockSpec((tm, D), row),
            pl.BlockSpec((tm, D), row),
            pl.BlockSpec((1, D), const),
        ],
        [pl.BlockSpec((tm, D), row), pl.BlockSpec((1, D), const)],
        [jax.ShapeDtypeStruct((T, D), F32), jax.ShapeDtypeStruct((1, D), F32)],
        (dz_b, w_inT_b, x, dh1, g_mix),
        comm=comm,
        after=after,
    )


def _weight_grad(name, a_b, b_b, comm=()):
    T, N1 = a_b.shape
    N2 = b_b.shape[1]
    t1 = _tile(N1, 1408, LANES)
    tk = _tile(T, 2048)
    nk = T // tk

    def body(a_ref, b_ref, o_ref, acc):
        k = pl.program_id(1)
        _accumulate(acc, k == 0, _dot(a_ref[...], b_ref[...], TN))

        @pl.when(k == nk - 1)
        def _():
            o_ref[...] = acc[...].astype(BF16)

    (out,), rest = _call(
        name,
        body,
        (N1 // t1, nk),
        [pl.BlockSpec((tk, t1), lambda n, k: (k, n)), pl.BlockSpec((tk, N2), lambda n, k: (k, 0))],
        [pl.BlockSpec((t1, N2), lambda n, k: (n, 0))],
        [jax.ShapeDtypeStruct((N1, N2), BF16)],
        (a_b, b_b),
        scratch=[pltpu.VMEM((t1, N2), F32)],
        comm=comm,
    )
    return out, rest


def _sum_parts(name, full, how, parts, me):
    _, R, C = parts[0].shape
    tr = _tile(R, 512)
    nb = R // tr
    where = [(q, r) for q, p in enumerate(parts) for r in range(p.shape[0])]
    assert len(where) == 3

    def body(me_ref, own_ref, *refs):
        o_ref = refs[-1]
        f = lambda j: refs[where[j][0]][where[j][1]].astype(F32)
        o_ref[...] = (own_ref[...].astype(F32) + f(0)) + (f(1) + f(2))

    own_map = {"rows": lambda i, me_ref: (me_ref[0] * nb + i, 0), "cols": lambda i, me_ref: (i, me_ref[0]),
               "all": lambda i, me_ref: (i, 0)}[how]
    return pl.pallas_call(
        body,
        name=name,
        grid_spec=pltpu.PrefetchScalarGridSpec(
            num_scalar_prefetch=1,
            grid=(nb,),
            in_specs=[pl.BlockSpec((tr, C), own_map)]
            + [pl.BlockSpec((p.shape[0], tr, C), lambda i, me_ref: (0, i, 0)) for p in parts],
            out_specs=pl.BlockSpec((tr, C), lambda i, me_ref: (i, 0)),
        ),
        out_shape=jax.ShapeDtypeStruct((R, C), F32),
        compiler_params=pltpu.CompilerParams(dimension_semantics=("arbitrary",), vmem_limit_bytes=VMEM_LIMIT),
    )(me, full, *parts)


_M_CORR = 1.0 - ADAM_B1**ADAM_STEP
_V_CORR = 1.0 - ADAM_B2**ADAM_STEP


def _adamw_math(w, g, m, v):
    m = ADAM_B1 * m + (1.0 - ADAM_B1) * g
    v = ADAM_B2 * v + (1.0 - ADAM_B2) * (g * g)
    delta = -ADAM_LR * ((m / _M_CORR) / (jnp.sqrt(v / _V_CORR) + ADAM_EPS) + ADAM_WD * w)
    return delta, m, v


def _adamw(name, w, m, v, g_here, g_there, g_transposed=False, comm=()):
    R, C = w.shape
    tr = _tile(R, 256, LANES if g_transposed else 8)

    def body(w_ref, m_ref, v_ref, ga_ref, gb_ref, g_ref, d_ref, nm_ref, nv_ref):
        g = ga_ref[...] + gb_ref[...]
        if g_transposed:
            g = g.T
        g_ref[...] = g
        d_ref[...], nm_ref[...], nv_ref[...] = _adamw_math(w_ref[...], g, m_ref[...], v_ref[...])

    spec = pl.BlockSpec((tr, C), lambda i: (i, 0))
    gspec = pl.BlockSpec((C, tr), lambda i: (0, i)) if g_transposed else spec
    return _call(name, body, (R // tr,), [spec] * 3 + [gspec] * 2, [spec] * 4, [jax.ShapeDtypeStruct((R, C), F32)] * 4,
                 (w, m, v, g_here, g_there), comm=comm)


def _adamw_on_sparsecore(name, w, m, v, g_here, g_there):
    R, C = w.shape
    n_groups = R // SUBLANES
    n_turns = -(-n_groups // SC_TILES)
    n_in, n_out = 5, 4

    def body(w_hbm, m_hbm, v_hbm, ga_hbm, gb_hbm, g_out, d_out, nm_out, nv_out, bufs, sems):
        tile = lax.axis_index("subcore") * SC_CORES + lax.axis_index("sparsecore")
        srcs = (w_hbm, m_hbm, v_hbm, ga_hbm, gb_hbm)
        dsts = (d_out, nm_out, nv_out, g_out)

        def rows(turn):
            return pl.ds((tile + turn * SC_TILES) * SUBLANES, SUBLANES)

        def loads(turn):
            slot = turn % 2
            return [pltpu.make_async_copy(srcs[q].at[rows(turn), :], bufs.at[slot, q], sems.at[slot, q]) for q in range(n_in)]

        def stores(turn):
            slot = turn % 2
            return [pltpu.make_async_copy(bufs.at[slot, q], dsts[q].at[rows(turn), :], sems.at[slot, n_in + q])
                    for q in range(n_out)]

        def when_mine(turn, fn):
            pl.when(tile + turn * SC_TILES < n_groups)(fn)

        def compute(slot):
            wb, mb, vb, gab, gbb = (bufs.at[slot, q] for q in range(n_in))

            @pl.loop(0, SUBLANES)
            def _(r):
                @pl.loop(0, C, step=SC_LANES)
                def _(i):
                    at = (r, pl.ds(i, SC_LANES))
                    g = gab[at] + gbb[at]
                    delta, new_m, new_v = _adamw_math(wb[at], g, mb[at], vb[at])
                    gab[at], wb[at], mb[at], vb[at] = g, delta, new_m, new_v

        def start_loads(turn):
            def fn():
                for cp in loads(turn):
                    cp.start()

            when_mine(turn, fn)

        start_loads(0)
        for turn in range(n_turns):
            def step(turn=turn):
                for cp in loads(turn):
                    cp.wait()
                if turn >= 1:
                    for cp in stores(turn - 1):
                        cp.wait()
                if turn + 1 < n_turns:
                    start_loads(turn + 1)
                compute(turn % 2)
                for cp in stores(turn):
                    cp.start()

            when_mine(turn, step)
        for turn in range(n_turns):
            def drain(turn=turn):
                for cp in stores(turn):
                    cp.wait()

            last_mine = jnp.logical_and(tile + turn * SC_TILES < n_groups, tile + (turn + 1) * SC_TILES >= n_groups)
            pl.when(last_mine)(drain)

    return pl.kernel(
        body,
        name=name,
        out_type=[jax.ShapeDtypeStruct((R, C), F32)] * 4,
        mesh=plsc.VectorSubcoreMesh(core_axis_name="sparsecore", subcore_axis_name="subcore"),
        scratch_types=[pltpu.VMEM((2, n_in, SUBLANES, C), F32), pltpu.SemaphoreType.DMA((2, n_in + n_out))],
        compiler_params=pltpu.CompilerParams(use_tc_tiling_on_sc=True),
    )(w, m, v, g_here, g_there)


class _PackLayout:
    def __init__(self, n_cc, n_grp, G, widths):
        self.dw_rows = (0, HALO)
        self.wp_rows = (HALO, HALO + G)
        self.n_cc, self.n_grp, self.G = n_cc, n_grp, G
        self.vec = {}
        r = HALO + G
        for name, width in widths:
            self.vec[name] = (r, width)
            r += width // PACK_W
        self.rows = -(-r // 8) * 8


def _pack_small(layout, dwdw, dwp, vecs):
    names = list(vecs)

    def body(*refs):
        dw_ref, wp_ref = refs[0], refs[1]
        vec_refs = refs[2 : 2 + len(names)]
        o_ref = refs[-1]
        o_ref[...] = jnp.zeros_like(o_ref)
        for j in range(layout.n_cc):
            o_ref[layout.dw_rows[0] : layout.dw_rows[1], j * LANES : (j + 1) * LANES] = dw_ref[j]
        for i in range(layout.n_grp):
            o_ref[layout.wp_rows[0] : layout.wp_rows[1], i * layout.G : (i + 1) * layout.G] = wp_ref[i]
        for name, ref in zip(names, vec_refs):
            r, width = layout.vec[name]
            for h in range(width // PACK_W):
                o_ref[r + h : r + h + 1, :] = ref[:, h * PACK_W : (h + 1) * PACK_W]

    return pl.pallas_call(
        body,
        name="pack_small",
        out_shape=jax.ShapeDtypeStruct((layout.rows, PACK_W), F32),
    )(dwdw, dwp, *[vecs[k] for k in names])


def _adamw_small(layout, g_here, g_there, w_dw, m_dw, v_dw, w_pool, m_pool, v_pool, vec_w, vec_m, vec_v):
    names = list(vec_w)
    nv = len(names)

    def body(*refs):
        ga_ref, gb_ref = refs[0], refs[1]
        wdw, mdw, vdw, wp, mp, vp = refs[2:8]
        vw, vm, vv = refs[8 : 8 + nv], refs[8 + nv : 8 + 2 * nv], refs[8 + 2 * nv : 8 + 3 * nv]
        outs = refs[8 + 3 * nv :]
        acc = outs[-1]
        acc[...] = ga_ref[...] + gb_ref[...]

        def emit(o, g, w, m, v, idx=()):
            res = (g,) + _adamw_math(w, g, m, v)
            for ref, val in zip(o, res):
                ref[idx] = val

        me = 2 * lax.axis_index("x") + lax.axis_index("y")
        for j in range(layout.n_cc):

            @pl.when(me == j)
            def _(j=j):
                for k in range(wdw.shape[0]):
                    g = acc[layout.dw_rows[0] + k : layout.dw_rows[0] + k + 1, j * LANES : (j + 1) * LANES]
                    emit(outs[0:4], g, wdw[k], mdw[k], vdw[k], idx=k)

        for i in range(layout.n_grp):
            g = acc[layout.wp_rows[0] : layout.wp_rows[1], i * layout.G : (i + 1) * layout.G]
            emit(outs[4:8], g, wp[i], mp[i], vp[i], idx=i)
        for q, name in enumerate(names):
            r, width = layout.vec[name]
            for h in range(width // PACK_W):
                ls = slice(h * PACK_W, (h + 1) * PACK_W)
                g = acc[r + h : r + h + 1, :]
                emit(outs[8 + 4 * q : 12 + 4 * q], g, vw[q][:, ls], vm[q][:, ls], vv[q][:, ls], idx=(slice(None), ls))

    shapes = [w_dw.shape] * 4 + [w_pool.shape] * 4
    for name in names:
        shapes += [vec_w[name].shape] * 4
    return pl.pallas_call(
        body,
        name="adamw_small",
        out_shape=[jax.ShapeDtypeStruct(s, F32) for s in shapes],
        scratch_shapes=[pltpu.VMEM(g_here.shape, F32)],
    )(g_here, g_there, w_dw, m_dw, v_dw, w_pool, m_pool, v_pool,
      *[vec_w[k] for k in names], *[vec_m[k] for k in names], *[vec_v[k] for k in names])


def _allreduce_adamw_row(g_part, w, m, v, loss_part, comm=()):
    D = w.shape[1]
    n_pairs = N_DEV - 1

    def body(g_ref, w_ref, m_ref, v_ref, l_ref, go_ref, d_ref, nm_ref, nv_ref, lo_ref, land_g, land_l, sems):
        x, y, c = _place()
        copies = []
        for q, (src, land) in enumerate(((g_ref, land_g), (l_ref, land_l))):
            for r in range(1, N_DEV):
                fx, fy, fc = (r >> 2) & 1, (r >> 1) & 1, r & 1
                peer = (1 - x if fx else x, 1 - y if fy else y, 1 - c if fc else c)
                cp = _remote(src, land.at[r], sems, 2 * (q * n_pairs + r - 1), peer)
                cp.start()
                copies.append(cp)
        for cp in copies:
            cp.wait()

        def total(src, land):
            row = lambda r: src[...] if r == 0 else land[r]
            return ((row(0) + row(4)) + (row(2) + row(6))) + ((row(1) + row(5)) + (row(3) + row(7)))

        g = total(g_ref, land_g)
        go_ref[...] = g
        d_ref[...], nm_ref[...], nv_ref[...] = _adamw_math(w_ref[...], g, m_ref[...], v_ref[...])
        lo_ref[...] = total(l_ref, land_l)

    vm = pl.BlockSpec(memory_space=pltpu.VMEM)
    return _call(
        "allreduce_adamw_g_mix",
        body,
        (),
        [vm] * 5,
        [vm] * 5,
        [jax.ShapeDtypeStruct((1, D), F32)] * 4 + [jax.ShapeDtypeStruct(loss_part.shape, F32)],
        (g_part, w, m, v, loss_part),
        scratch=[pltpu.VMEM((N_DEV, 1, D), F32), pltpu.VMEM((N_DEV,) + loss_part.shape, F32),
                 pltpu.SemaphoreType.DMA((4 * n_pairs,))],
        comm=comm,
    )


def kernel(x, g_mix, w_in, b_in, w_dw, b_dw, ln_g, ln_b, w_pool, s_pool, w_out, g_ffn, w_gate, w_up, w_down, g_final, loss_target, m_g_mix, m_w_in, m_b_in, m_w_dw, m_b_dw, m_ln_g, m_ln_b, m_w_pool, m_s_pool, m_w_out, m_g_ffn, m_w_gate, m_w_up, m_w_down, m_g_final, v_g_mix, v_w_in, v_b_in, v_w_dw, v_b_dw, v_ln_g, v_ln_b, v_w_pool, v_s_pool, v_w_out, v_g_ffn, v_w_gate, v_w_up, v_w_down, v_g_final):
    x2 = x[0]
    target = loss_target[0]
    T, D = x2.shape
    w_in2, w_out2, w_down2 = w_in[0], w_out[0], w_down[0]
    taps_first = lambda a: jnp.transpose(a, (1, 0, 2))
    w_dw3 = taps_first(w_dw)
    w_gateT, w_upT = w_gate[0].T, w_up[0].T
    CI = w_in2.shape[1] * N_CHIPS
    DM = w_out2.shape[0] * N_CHIPS
    F = w_down2.shape[0] * N_CHIPS
    KW, _, dw_cols = w_dw3.shape
    assert dw_cols == LANES
    n_grp, G = w_pool.shape[1], w_pool.shape[-1]
    w_pool3 = w_pool[0]
    g_final2 = g_final.reshape(1, D)

    me = (2 * lax.axis_index("x") + lax.axis_index("y")).astype(jnp.int32).reshape(1)

    w_inT_b, w_dw4, f_out, f_gate, f_up, f_down = _place_and_gather(
        [(w_in2, "rows", (CI, D), BF16, True, True), (w_dw3, "lead", (N_CHIPS, KW, 1, dw_cols), F32, False, False)],
        [(w, "rows", shape, BF16, False, True)
         for w, shape in ((w_out2, (DM, D)), (w_gateT, (F, D)), (w_upT, (F, D)), (w_down2, (F, D)))])
    w_pool_b = w_pool3.astype(BF16)
    ici = lambda f: _GatherIci([f], ["rows"], [True])
    d2d = lambda f: _GatherD2d([f], ["rows"])
    gather = _start("gather_start", [ici(f_out), ici(f_gate), ici(f_up), ici(f_down)])
    (z, xn_b), _ = _in_proj(x2, g_mix, w_inT_b, b_in, after=[gather.token])
    (f_out,) = _wait("gather_out_wait", gather, 0, xn_b)
    s_out = _start("share_out_start", [d2d(f_out)], sibling_only=True)
    (y_b, v), _ = _seq_fwd(z, w_dw4, b_dw, ln_g, ln_b, w_pool_b, s_pool, after=[s_out.token])
    (w_out_b,) = _wait("share_out_wait", s_out, 0, y_b)
    (f_gate,) = _wait("gather_gate_wait", gather, 1, y_b)
    s_gate = _start("share_gate_start", [d2d(f_gate)], sibling_only=True)
    (h1, hn_b), _ = _out_proj(y_b, x2, w_out_b, g_ffn, after=[s_gate.token])
    (f_up,) = _wait("gather_up_wait", gather, 2, hn_b)
    s_up = _start("share_up_start", [d2d(f_up)], sibling_only=True)
    (wgT_b,) = _wait("share_gate_wait", s_gate, 0, hn_b)
    (wuT_b,) = _wait("share_up_wait", s_up, 0, hn_b)
    (g_b, u_b, a_b), _ = _gate_up(hn_b, wgT_b, wuT_b)
    (f_down,) = _wait("gather_down_wait", gather, 3, a_b)
    s_down = _start("share_down_start", [d2d(f_down)], sibling_only=True)
    (wd_b,) = _wait("share_down_wait", s_down, 0, a_b)
    (dh2, dh2_b, loss_part, d_g_final), _ = _down_loss(a_b, wd_b, h1, target, g_final2)

    gw_down, _ = _weight_grad("grad_w_down", a_b, dh2_b)
    (dg_b, du_b), (p_down_xy,) = _ffn_bwd_act(dh2_b, wd_b, g_b, u_b, comm=[_Scatter([gw_down], ["rows"], which=(0, 1))])
    gw_gateT, (p_down_d,) = _weight_grad("grad_w_gate", dg_b, hn_b, comm=[_Scatter([gw_down], ["rows"], which=(2,))])
    gw_upT, _ = _weight_grad("grad_w_up", du_b, hn_b)
    sum_down = _sum_parts("sum_w_down", gw_down, "rows", [p_down_xy, p_down_d], me)
    (dh1, dh1_b, dy, d_g_ffn), (p_gate, oth_down) = _ffn_bwd_in(
        dg_b, du_b, wgT_b, wuT_b, h1, dh2, g_ffn, w_out_b, comm=[_Scatter([gw_gateT], ["rows"]), _Swap([sum_down])])
    gw_out, _ = _weight_grad("grad_w_out", y_b, dh1_b)
    sum_gate = _sum_parts("sum_w_gate", gw_gateT, "rows", [p_gate], me)
    res = {}
    res["w_down"] = _adamw_on_sparsecore("adamw_w_down", w_down2, m_w_down[0], v_w_down[0], sum_down, oth_down)
    (dz_b, d_wdw, d_bdw, d_lng, d_lnb, d_wp, d_sp, d_bin), (p_up, p_out, oth_gate) = _seq_bwd(
        z, dy, v, w_dw4, ln_g, ln_b, w_pool_b, s_pool,
        comm=[_Scatter([gw_upT, gw_out], ["rows", "rows"]), _Swap([sum_gate])])
    vec_grads = {"b_dw": d_bdw, "ln_g": d_lng, "ln_b": d_lnb, "s_pool": d_sp, "g_ffn": d_g_ffn, "g_final": d_g_final, "b_in": d_bin}
    layout = _PackLayout(dw_cols * N_CHIPS // LANES, n_grp, G, [(k, a.shape[1]) for k, a in vec_grads.items()])
    pack = _pack_small(layout, d_wdw, d_wp, vec_grads)
    sum_up = _sum_parts("sum_w_up", gw_upT, "rows", [p_up], me)
    sum_out = _sum_parts("sum_w_out", gw_out, "rows", [p_out], me)
    gw_inT, (p_small, oth_up, oth_out) = _weight_grad(
        "grad_w_in", dz_b, xn_b, comm=[_Scatter([pack], ["all"]), _Swap([sum_up, sum_out])])
    sum_small = _sum_parts("sum_small", pack, "all", [p_small], me)
    res["w_gate"] = _adamw_on_sparsecore("adamw_w_gate", w_gateT, m_w_gate[0].T, v_w_gate[0].T, sum_gate, oth_gate)
    late = _start("late_start", [_Scatter([gw_inT], ["rows"]), _Swap([sum_small])])
    (grad_x, d_g_mix), _ = _in_proj_bwd(dz_b, w_inT_b, x2, dh1, g_mix, after=[late.token])
    gw_inT, p_in = _wait("late_w_in_wait", late, 0, d_g_mix)
    sum_small, oth_small = _wait("late_small_wait", late, 1, d_g_mix)
    res["w_up"] = _adamw_on_sparsecore("adamw_w_up", w_upT, m_w_up[0].T, v_w_up[0].T, sum_up, oth_up)
    res["w_out"] = _adamw_on_sparsecore("adamw_w_out", w_out2, m_w_out[0], v_w_out[0], sum_out, oth_out)
    sum_in = _sum_parts("sum_w_in", gw_inT, "rows", [p_in], me)
    (*res["g_mix"], loss_row), (oth_in,) = _allreduce_adamw_row(
        d_g_mix, g_mix, m_g_mix, v_g_mix, loss_part, comm=[_Swap([sum_in])])
    loss = loss_row[0, 0]
    res["w_in"], _ = _adamw("adamw_w_in", w_in2, m_w_in[0], v_w_in[0], sum_in, oth_in, g_transposed=True)

    vec_w = {"b_dw": b_dw, "ln_g": ln_g, "ln_b": ln_b, "s_pool": s_pool, "g_ffn": g_ffn, "g_final": g_final2, "b_in": b_in}
    vec_m = {"b_dw": m_b_dw, "ln_g": m_ln_g, "ln_b": m_ln_b, "s_pool": m_s_pool, "g_ffn": m_g_ffn,
             "g_final": m_g_final.reshape(1, D), "b_in": m_b_in}
    vec_v = {"b_dw": v_b_dw, "ln_g": v_ln_g, "ln_b": v_ln_b, "s_pool": v_s_pool, "g_ffn": v_g_ffn,
             "g_final": v_g_final.reshape(1, D), "b_in": v_b_in}
    small = _adamw_small(layout, sum_small, oth_small, w_dw3, taps_first(m_w_dw), taps_first(v_w_dw),
                         w_pool3, m_w_pool[0], v_w_pool[0], vec_w, vec_m, vec_v)
    res["w_dw"] = [taps_first(a) for a in small[0:4]]
    res["w_pool"] = [a[None] for a in small[4:8]]
    for q, k in enumerate(vec_w):
        res[k] = list(small[8 + 4 * q : 12 + 4 * q])
    res["g_final"] = [a.reshape(D) for a in res["g_final"]]
    for k in ("w_in", "w_out", "w_down"):
        res[k] = [a[None] for a in res[k]]
    for k in ("w_gate", "w_up"):
        res[k] = [a.T[None] for a in res[k]]

    order = ["g_mix", "w_in", "b_in", "w_dw", "b_dw", "ln_g", "ln_b", "w_pool", "s_pool", "w_out", "g_ffn", "w_gate", "w_up", "w_down", "g_final"]
    outs = [loss, grad_x[None]]
    for q in range(4):
        outs += [res[k][q] for k in order]
    return tuple(outs)
```

```python
import jax
import jax.numpy as jnp
from jax import lax
from jax.experimental import pallas as pl
from jax.experimental.pallas import tpu as pltpu
from jax.experimental.pallas import tpu_sc as plsc

F32 = jnp.float32
BF16 = jnp.bfloat16
MESH = pl.DeviceIdType.MESH
ANY = pl.BlockSpec(memory_space=pl.ANY)

RMS_EPS = 1e-6
LN_EPS = 1e-5
POOL_WINDOWS = (2, 4, 8, 16)
ADAM_LR = 0.001
ADAM_B1 = 0.9
ADAM_B2 = 0.999
ADAM_EPS = 1e-08
ADAM_WD = 0.01
ADAM_STEP = 10

LANES = 128
SUBLANES = 8
HALO = 32
CONV_ROWS = 64
HIDDEN_CHUNK = 512
VMEM_LIMIT = 56 * 1024 * 1024
PACK_W = 512
N_CHIPS = 4
N_DEV = 8
SIBLING_BARRIER_ID = 0
SC_CORES = 2
SC_TILES = 32
SC_LANES = 16


def _tile(n, want, mult=8):
    t = min(n, want)
    while n % t or t % mult:
        t -= 1
    return t


def _sigmoid(x):
    return 1.0 / (1.0 + jnp.exp(-x))


def _dot(a, b, dims):
    return lax.dot_general(a, b, (dims, ((), ())), preferred_element_type=F32)


NN = ((1,), (0,))
NT = ((1,), (1,))
TN = ((0,), (0,))


def _rms_bwd(x, g, dy):
    r = lax.rsqrt(jnp.mean(x * x, axis=-1, keepdims=True) + RMS_EPS)
    xh = x * r
    gy = dy * g
    dx = r * (gy - xh * jnp.mean(gy * xh, axis=-1, keepdims=True))
    return dx, dy * xh


def _accumulate(ref, first, val):
    @pl.when(first)
    def _():
        ref[...] = val

    @pl.when(jnp.logical_not(first))
    def _():
        ref[...] += val


def _place():
    return lax.axis_index("x"), lax.axis_index("y"), lax.axis_index("c")


def _other_chips(x, y):
    return [(1 - x, y), (x, 1 - y), (1 - x, 1 - y)]


def _rows(ref, start, n):
    return ref.at[pl.ds(pl.multiple_of(start, 16), n)]


def _window(ref, how, k, c=None):
    if how == "all":
        return ref
    if how == "lead":
        return ref.at[k]
    if how == "rows":
        n = ref.shape[0] // N_CHIPS
        if c is None:
            return _rows(ref, k * n, n)
        return _rows(ref, k * n + c * (n // 2), n // 2)
    n = ref.shape[1] // N_CHIPS
    cols = pl.ds(pl.multiple_of(k * n, LANES), n)
    if c is None:
        return ref.at[:, cols]
    h = ref.shape[0] // 2
    return ref.at[pl.ds(pl.multiple_of(c * h, 16), h), cols]


def _remote(src, dst, sems, s, device):
    return pltpu.make_async_remote_copy(
        src_ref=src, dst_ref=dst, send_sem=sems.at[s], recv_sem=sems.at[s + 1], device_id=device, device_id_type=MESH)


class _GatherIci:
    aliased = True

    def __init__(self, fulls, hows, splits, which=(0, 1, 2)):
        self.fulls, self.hows, self.splits, self.which = list(fulls), list(hows), list(splits), tuple(which)

    def inputs(self):
        return self.fulls

    def out_shapes(self):
        return [jax.ShapeDtypeStruct(a.shape, a.dtype) for a in self.fulls]

    def n_sems(self):
        return 6 * len(self.fulls)

    def build(self, ins, outs, sems, base):
        x, y, c = _place()
        me = 2 * x + y
        chips = _other_chips(x, y)
        starts, waits = [], []
        for a, (how, sp) in enumerate(zip(self.hows, self.splits)):
            half = c if sp else None
            mine = _window(outs[a], how, me, half)
            for j in self.which:
                px, py = chips[j]
                s = base + 6 * a + 2 * j
                cp = _remote(mine, mine, sems, s, (px, py, c))
                landing = _remote(mine, _window(outs[a], how, 2 * px + py, half), sems, s, (px, py, c))
                starts.append(cp.start)
                waits += [landing.wait_recv, cp.wait_send]
        return starts, waits


class _GatherD2d:
    aliased = True

    def __init__(self, fulls, hows):
        self.fulls, self.hows = list(fulls), list(hows)

    def inputs(self):
        return self.fulls

    def out_shapes(self):
        return [jax.ShapeDtypeStruct(a.shape, a.dtype) for a in self.fulls]

    def n_sems(self):
        return 6 * len(self.fulls)

    def build(self, ins, outs, sems, base):
        x, y, c = _place()
        starts, waits = [], []
        for a, how in enumerate(self.hows):
            for j, (px, py) in enumerate(_other_chips(x, y)):
                s = base + 6 * a + 2 * j
                got = _window(outs[a], how, 2 * px + py, c)
                cp = _remote(got, got, sems, s, (x, y, 1 - c))
                landing = _remote(got, _window(outs[a], how, 2 * px + py, 1 - c), sems, s, (x, y, 1 - c))
                starts.append(cp.start)
                waits += [landing.wait_recv, cp.wait_send]
        return starts, waits


def _part_shape(a, how):
    if how == "all":
        return a.shape
    if how == "rows":
        return (a.shape[0] // N_CHIPS, a.shape[1])
    return (a.shape[0], a.shape[1] // N_CHIPS)


class _Scatter:
    aliased = False

    def __init__(self, fulls, hows, which=(0, 1, 2)):
        self.fulls, self.hows, self.which = list(fulls), list(hows), tuple(which)

    def inputs(self):
        return self.fulls

    def out_shapes(self):
        return [jax.ShapeDtypeStruct((len(self.which),) + _part_shape(a, h), a.dtype) for a, h in zip(self.fulls, self.hows)]

    def n_sems(self):
        return 6 * len(self.fulls)

    def build(self, ins, outs, sems, base):
        x, y, c = _place()
        chips = _other_chips(x, y)
        starts, waits = [], []
        for a, how in enumerate(self.hows):
            for slot, j in enumerate(self.which):
                px, py = chips[j]
                cp = _remote(_window(ins[a], how, 2 * px + py), outs[a].at[slot], sems, base + 6 * a + 2 * j, (px, py, c))
                starts.append(cp.start)
                waits += [cp.wait_recv, cp.wait_send]
        return starts, waits


class _Swap:
    aliased = False

    def __init__(self, arrays):
        self.arrays = list(arrays)

    def inputs(self):
        return self.arrays

    def out_shapes(self):
        return [jax.ShapeDtypeStruct(a.shape, a.dtype) for a in self.arrays]

    def n_sems(self):
        return 2 * len(self.arrays)

    def build(self, ins, outs, sems, base):
        x, y, c = _place()
        starts, waits = [], []
        for a in range(len(ins)):
            cp = _remote(ins[a], outs[a], sems, base + 2 * a, (x, y, 1 - c))
            starts.append(cp.start)
            waits += [cp.wait_recv, cp.wait_send]
        return starts, waits


def _call(name, body, grid, in_specs, out_specs, out_shape, args, scratch=(), comm=(), after=()):
    comm, after = list(comm), list(after)
    n_in, n_out, n_scr, n_after = len(args), len(out_shape), len(scratch), len(after)
    c_in = [a for op in comm for a in op.inputs()]
    c_out = [s for op in comm for s in op.out_shapes()]
    n_sems = sum(op.n_sems() for op in comm)
    aliases, i_in, i_out = {}, 0, 0
    for op in comm:
        if op.aliased:
            for q in range(len(op.inputs())):
                aliases[n_in + n_after + i_in + q] = n_out + i_out + q
        i_in, i_out = i_in + len(op.inputs()), i_out + len(op.out_shapes())

    def wrapped(*refs):
        ins = refs[:n_in]
        cin = refs[n_in + n_after : n_in + n_after + len(c_in)]
        o0 = n_in + n_after + len(c_in)
        outs = refs[o0 : o0 + n_out]
        cout = refs[o0 + n_out : o0 + n_out + len(c_out)]
        s0 = o0 + n_out + len(c_out)
        scr = refs[s0 : s0 + n_scr]

        def copies():
            sems = refs[s0 + n_scr]
            starts, waits = [], []
            i_in = i_out = base = 0
            for op in comm:
                ni, no = len(op.inputs()), len(op.out_shapes())
                s, w = op.build(cin[i_in : i_in + ni], cout[i_out : i_out + no], sems, base)
                starts += s
                waits += w
                i_in, i_out, base = i_in + ni, i_out + no, base + op.n_sems()
            return starts, waits

        def run_starts():
            for start in copies()[0]:
                start()

        def run_waits():
            for wait in copies()[1]:
                wait()

        if comm and grid:
            first = last = True
            for d, n in enumerate(grid):
                first = jnp.logical_and(first, pl.program_id(d) == 0)
                last = jnp.logical_and(last, pl.program_id(d) == n - 1)
            pl.when(first)(run_starts)
        elif comm:
            run_starts()
        if body is not None:
            body(*ins, *outs, *scr)
        if comm and grid:
            pl.when(last)(run_waits)
        elif comm:
            run_waits()

    res = pl.pallas_call(
        wrapped,
        name=name,
        grid=grid,
        in_specs=list(in_specs) + [ANY] * (n_after + len(c_in)),
        out_specs=list(out_specs) + [ANY] * len(c_out),
        out_shape=list(out_shape) + c_out,
        scratch_shapes=list(scratch) + ([pltpu.SemaphoreType.DMA((n_sems,))] if comm else []),
        input_output_aliases=aliases,
        compiler_params=pltpu.CompilerParams(dimension_semantics=("arbitrary",) * len(grid), vmem_limit_bytes=VMEM_LIMIT),
    )(*args, *after, *c_in)
    return tuple(res[:n_out]), tuple(res[n_out:])


def _place_and_gather(now, later):
    items = list(now) + list(later)
    n, n_now = len(items), len(now)
    buf_shape = lambda it: it[0].shape[::-1] if it[4] else it[0].shape
    split_now = [a for a in range(n_now) if items[a][5]]

    def body(*refs):
        ins, outs = refs[:n], refs[n : 2 * n]
        stage, bufs = refs[2 * n : 3 * n - n_now], refs[3 * n - n_now : 4 * n - n_now]
        sems = refs[4 * n - n_now]
        x, y, c = _place()
        me = 2 * x + y
        chips = _other_chips(x, y)
        loads = [pltpu.make_async_copy(ins[a], stage[a - n_now], sems.at[a]) for a in range(n_now, n)]
        for ld in loads:
            ld.start()
        pending = []

        def place(a, val):
            _, how, _, dtype, transposed, _ = items[a]
            bufs[a][...] = (val.T if transposed else val).astype(dtype)
            cp = pltpu.make_async_copy(bufs[a], _window(outs[a], how, me), sems.at[n + a])
            cp.start()
            pending.append(cp.wait)

        arrivals = []
        for a in range(n_now):
            place(a, ins[a][...])
            how, split = items[a][1], items[a][5]
            half = c if split else None
            src = _rows(bufs[a], c * (bufs[a].shape[0] // 2), bufs[a].shape[0] // 2) if split else bufs[a]
            for j, (px, py) in enumerate(chips):
                s = 2 * n + 6 * a + 2 * j
                cp = _remote(src, _window(outs[a], how, me, half), sems, s, (px, py, c))
                landing = _remote(src, _window(outs[a], how, 2 * px + py, half), sems, s, (px, py, c))
                cp.start()
                arrivals.append(landing.wait_recv)
                pending.append(cp.wait_send)
        for a in range(n_now, n):
            loads[a - n_now].wait()
            place(a, stage[a - n_now][...])
        for wait in arrivals:
            wait()
        d2d = _GatherD2d([None] * len(split_now), [items[a][1] for a in split_now])
        starts, waits = d2d.build(None, [outs[a] for a in split_now], sems, 2 * n + 6 * n_now)
        for start in starts:
            start()
        for wait in waits + pending:
            wait()

    vm = pl.BlockSpec(memory_space=pltpu.VMEM)
    return pl.pallas_call(
        body,
        name="place_and_gather",
        in_specs=[vm] * n_now + [ANY] * (n - n_now),
        out_specs=[ANY] * n,
        out_shape=[jax.ShapeDtypeStruct(it[2], it[3]) for it in items],
        scratch_shapes=[pltpu.VMEM(it[0].shape, it[0].dtype) for it in later]
        + [pltpu.VMEM(buf_shape(it), it[3]) for it in items]
        + [pltpu.SemaphoreType.DMA((2 * n + 6 * n_now + 6 * len(split_now),))],
        compiler_params=pltpu.CompilerParams(vmem_limit_bytes=VMEM_LIMIT),
    )(*[it[0] for it in items])


_HBM = pl.BlockSpec(memory_space=pltpu.HBM)
_SEM = pl.BlockSpec(memory_space=pltpu.SEMAPHORE)
_DATAFLOW = pltpu.SideEffectType.DATAFLOW_SIDE_EFFECTING


class _Pending:
    def __init__(self, ops, bases, sems, arrays, token):
        self.ops, self.bases, self.sems, self.arrays, self.token = ops, bases, sems, arrays, token


def _op_refs(op, refs):
    n_src = len(op.inputs())
    return refs[:n_src], (refs[:n_src] if op.aliased else refs[n_src:])


def _start(name, ops, after=(), sibling_only=False):
    per_op = [list(op.inputs()) + ([] if op.aliased else [lax.empty(sd.shape, sd.dtype) for sd in op.out_shapes()])
              for op in ops]
    arrays = [a for group in per_op for a in group]
    bases = [sum(op.n_sems() for op in ops[:k]) for k in range(len(ops))]
    n, after = len(arrays), list(after)

    def body(*refs):
        sems, token = refs[n + len(after)], refs[-1]
        if sibling_only:
            x, y, c = _place()
            barrier = pltpu.get_barrier_semaphore()
            pl.semaphore_signal(barrier, inc=1, device_id=(x, y, 1 - c), device_id_type=MESH)
            pl.semaphore_wait(barrier, 1)
        at = 0
        for op, group, base in zip(ops, per_op, bases):
            starts, _ = op.build(*_op_refs(op, refs[at : at + len(group)]), sems, base)
            for start in starts:
                start()
            at += len(group)
        token[...] = jnp.zeros_like(token)

    res = pl.pallas_call(
        body,
        name=name,
        out_shape=(pltpu.SemaphoreType.DMA((sum(op.n_sems() for op in ops),)),)
        + tuple(pltpu.HBM(a.shape, a.dtype) for a in arrays) + (jax.ShapeDtypeStruct((SUBLANES, LANES), F32),),
        in_specs=(_HBM,) * n + (ANY,) * len(after),
        out_specs=(_SEM,) + (_HBM,) * n + (pl.BlockSpec(memory_space=pltpu.VMEM),),
        input_output_aliases={i: 1 + i for i in range(n)},
        compiler_params=pltpu.CompilerParams(
            has_side_effects=_DATAFLOW, collective_id=SIBLING_BARRIER_ID if sibling_only else None),
    )(*[pltpu.with_memory_space_constraint(a, pltpu.HBM) for a in arrays], *after)
    thru, at, groups = list(res[1 : 1 + n]), 0, []
    for group in per_op:
        groups.append(thru[at : at + len(group)])
        at += len(group)
    return _Pending(list(ops), bases, res[0], groups, res[-1])


def _wait(name, pending, k, after):
    op, arrays = pending.ops[k], pending.arrays[k]
    n = len(arrays)

    def body(*refs):
        _, waits = op.build(*_op_refs(op, refs[:n]), refs[n], pending.bases[k])
        for wait in waits:
            wait()

    return pl.pallas_call(
        body,
        name=name,
        out_shape=tuple(pltpu.HBM(a.shape, a.dtype) for a in arrays),
        in_specs=(_HBM,) * n + (_SEM, ANY),
        out_specs=(_HBM,) * n,
        input_output_aliases={i: i for i in range(n)},
        compiler_params=pltpu.CompilerParams(has_side_effects=_DATAFLOW),
    )(*arrays, pending.sems, after)


def _in_proj(x, g_mix, w_inT_b, b_in, comm=(), after=()):
    T, D = x.shape
    CI = w_inT_b.shape[0]
    tm = _tile(T, 512)

    def body(x_ref, g_ref, w_ref, b_ref, z_ref, xn_ref):
        xv = x_ref[...]
        r = lax.rsqrt(jnp.mean(xv * xv, axis=-1, keepdims=True) + RMS_EPS)
        xn = (xv * r * g_ref[...]).astype(BF16)
        xn_ref[...] = xn
        z_ref[...] = _dot(xn, w_ref[...], NT) + b_ref[...]

    return _call(
        "in_proj",
        body,
        (T // tm,),
        [
            pl.BlockSpec((tm, D), lambda i: (i, 0)),
            pl.BlockSpec((1, D), lambda i: (0, 0)),
            pl.BlockSpec((CI, D), lambda i: (0, 0)),
            pl.BlockSpec((1, CI), lambda i: (0, 0)),
        ],
        [pl.BlockSpec((tm, CI), lambda i: (i, 0)), pl.BlockSpec((tm, D), lambda i: (i, 0))],
        [jax.ShapeDtypeStruct((T, CI), F32), jax.ShapeDtypeStruct((T, D), BF16)],
        (x, g_mix, w_inT_b, b_in),
        comm=comm,
        after=after,
    )


def _fill_shifted(scr):
    n = scr.shape[1] - SUBLANES
    for s in range(1, SUBLANES):
        scr[s, 0:n, :] = scr[0, s : s + n, :]


def _shifted_rows(scr, off, n, cs):
    s = off % SUBLANES
    return scr[s, off - s : off - s + n, cs]


def _pool_mean_minus_token(p_scr, cs, w, cnt, tt):
    tok = p_scr[HALO : HALO + tt, cs]
    s = tok
    for d in range(1, w):
        s = s + p_scr[HALO - d : HALO - d + tt, cs]
    return s / cnt - tok


def _seq_fwd(z, w_dw4, b_dw, ln_g, ln_b, w_pool_b, s_pool, comm=(), after=()):
    T, CI = z.shape
    CC = ln_g.shape[1]
    n_grp, G = w_pool_b.shape[0], w_pool_b.shape[-1]
    KW = w_dw4.shape[1]
    D = CC + n_grp * G
    tt = _tile(T, 512, HALO)
    per = tt // HALO

    def body(zc_ref, zp_ref, wdw_ref, bdw_ref, lng_ref, lnb_ref, wp_ref, sp_ref, y_ref, v_ref, u_scr, p_scr):
        i = pl.program_id(0)
        first = i == 0
        u_prev = zp_ref[:, 0:CC] * _sigmoid(zp_ref[:, CC : 2 * CC])
        u_scr[0, 0:HALO, :] = jnp.where(first, 0.0, u_prev)
        p_scr[0:HALO, :] = jnp.where(first, 0.0, zp_ref[:, 2 * CC :])
        u_scr[0, HALO:, :] = zc_ref[:, 0:CC] * _sigmoid(zc_ref[:, CC : 2 * CC])
        p_scr[HALO:, :] = zc_ref[:, 2 * CC :]
        _fill_shifted(u_scr)

        for j in range(CC // LANES):
            cs = slice(LANES * j, LANES * (j + 1))
            for rb in range(tt // CONV_ROWS):
                acc = jnp.zeros((CONV_ROWS, LANES), F32)
                for k in range(KW):
                    off = HALO - (KW - 1) + k + rb * CONV_ROWS
                    acc = acc + _shifted_rows(u_scr, off, CONV_ROWS, cs) * wdw_ref[j, k]
                v_ref[rb * CONV_ROWS : (rb + 1) * CONV_ROWS, cs] = acc + bdw_ref[:, cs]

        v = v_ref[...]
        mu = jnp.mean(v, axis=-1, keepdims=True)
        d = v - mu
        var = jnp.mean(d * d, axis=-1, keepdims=True)
        ln = d * lax.rsqrt(var + LN_EPS) * lng_ref[...] + lnb_ref[...]
        y_ref[:, 0:CC] = (ln * _sigmoid(ln)).astype(BF16)

        tpos = i * tt + lax.broadcasted_iota(jnp.int32, (tt, 1), 0)
        for gi, w in enumerate(POOL_WINDOWS):
            cs = slice(G * gi, G * (gi + 1))
            cnt = jnp.minimum(tpos + 1, w).astype(F32)
            yi = _pool_mean_minus_token(p_scr, cs, w, cnt, tt)
            q = _dot(yi.astype(BF16), wp_ref[gi], NN)
            y_ref[:, CC + G * gi : CC + G * (gi + 1)] = (q * sp_ref[:, cs]).astype(BF16)

    const2 = lambda i: (0, 0)
    return _call(
        "seq_fwd",
        body,
        (T // tt,),
        [
            pl.BlockSpec((tt, CI), lambda i: (i, 0)),
            pl.BlockSpec((HALO, CI), lambda i: (jnp.maximum(i * per - 1, 0), 0)),
            pl.BlockSpec(w_dw4.shape, lambda i: (0,) * w_dw4.ndim),
            pl.BlockSpec((1, CC), const2),
            pl.BlockSpec((1, CC), const2),
            pl.BlockSpec((1, CC), const2),
            pl.BlockSpec(w_pool_b.shape, lambda i: (0, 0, 0)),
            pl.BlockSpec((1, n_grp * G), const2),
        ],
        [pl.BlockSpec((tt, D), lambda i: (i, 0)), pl.BlockSpec((tt, CC), lambda i: (i, 0))],
        [jax.ShapeDtypeStruct((T, D), BF16), jax.ShapeDtypeStruct((T, CC), F32)],
        (z, z, w_dw4, b_dw, ln_g, ln_b, w_pool_b, s_pool),
        scratch=[pltpu.VMEM((SUBLANES, HALO + tt, CC), F32), pltpu.VMEM((HALO + tt, n_grp * G), F32)],
        comm=comm,
        after=after,
    )


def _out_proj(y_b, x, w_out_b, g_ffn, comm=(), after=()):
    T, D = x.shape
    tm = _tile(T, 512)

    def body(y_ref, x_ref, w_ref, g_ref, h1_ref, hn_ref):
        h1 = x_ref[...] + _dot(y_ref[...], w_ref[...], NN)
        h1_ref[...] = h1
        r = lax.rsqrt(jnp.mean(h1 * h1, axis=-1, keepdims=True) + RMS_EPS)
        hn_ref[...] = (h1 * r * g_ref[...]).astype(BF16)

    row = lambda i: (i, 0)
    return _call(
        "out_proj",
        body,
        (T // tm,),
        [
            pl.BlockSpec((tm, y_b.shape[1]), row),
            pl.BlockSpec((tm, D), row),
            pl.BlockSpec(w_out_b.shape, lambda i: (0, 0)),
            pl.BlockSpec((1, D), lambda i: (0, 0)),
        ],
        [pl.BlockSpec((tm, D), row), pl.BlockSpec((tm, D), row)],
        [jax.ShapeDtypeStruct((T, D), F32), jax.ShapeDtypeStruct((T, D), BF16)],
        (y_b, x, w_out_b, g_ffn),
        comm=comm,
        after=after,
    )


def _hidden_tile(F):
    return _tile(F, 1408, LANES)


def _gate_up(hn_b, wgT_b, wuT_b, comm=(), after=()):
    T, D = hn_b.shape
    F = wgT_b.shape[0]
    tm, tf = _tile(T, 1024), _hidden_tile(F)

    def body(hn_ref, wg_ref, wu_ref, g_ref, u_ref, a_ref):
        hn = hn_ref[...]
        for c0 in range(0, tf, HIDDEN_CHUNK):
            cs = slice(c0, min(c0 + HIDDEN_CHUNK, tf))
            gv = _dot(hn, wg_ref[cs, :], NT)
            uv = _dot(hn, wu_ref[cs, :], NT)
            g_ref[:, cs] = gv.astype(BF16)
            u_ref[:, cs] = uv.astype(BF16)
            a_ref[:, cs] = (gv * _sigmoid(gv) * uv).astype(BF16)

    wspec = pl.BlockSpec((tf, D), lambda j, i: (j, 0))
    ospec = pl.BlockSpec((tm, tf), lambda j, i: (i, j))
    return _call(
        "gate_up",
        body,
        (F // tf, T // tm),
        [pl.BlockSpec((tm, D), lambda j, i: (i, 0)), wspec, wspec],
        [ospec, ospec, ospec],
        [jax.ShapeDtypeStruct((T, F), BF16)] * 3,
        (hn_b, wgT_b, wuT_b),
        comm=comm,
        after=after,
    )


def _down_loss(a_b, wd_b, h1, target, g_final, comm=(), after=()):
    T, D = h1.shape
    F = a_b.shape[1]
    tm = _tile(T, 512)
    nt = T // tm

    def body(a_ref, w_ref, h1_ref, t_ref, g_ref, dh2_ref, dh2b_ref, loss_ref, dg_ref):
        i = pl.program_id(0)
        h2 = h1_ref[...] + _dot(a_ref[...], w_ref[...], NN)
        r = lax.rsqrt(jnp.mean(h2 * h2, axis=-1, keepdims=True) + RMS_EPS)
        g = g_ref[...]
        diff = h2 * r * g - t_ref[...]
        _accumulate(loss_ref, i == 0, jnp.full(loss_ref.shape, jnp.sum(diff * diff) * (0.5 / D), F32))
        dh2, dg_rows = _rms_bwd(h2, g, diff * (1.0 / D))
        dh2_ref[...] = dh2
        dh2b_ref[...] = dh2.astype(BF16)
        _accumulate(dg_ref, i == 0, jnp.sum(dg_rows, axis=0, keepdims=True))

    row = lambda i: (i, 0)
    return _call(
        "down_loss",
        body,
        (nt,),
        [
            pl.BlockSpec((tm, F), row),
            pl.BlockSpec((F, D), lambda i: (0, 0), pipeline_mode=pl.Buffered(1)),
            pl.BlockSpec((tm, D), row),
            pl.BlockSpec((tm, D), row),
            pl.BlockSpec((1, D), lambda i: (0, 0)),
        ],
        [
            pl.BlockSpec((tm, D), row),
            pl.BlockSpec((tm, D), row),
            pl.BlockSpec((1, LANES), lambda i: (0, 0)),
            pl.BlockSpec((1, D), lambda i: (0, 0)),
        ],
        [
            jax.ShapeDtypeStruct((T, D), F32),
            jax.ShapeDtypeStruct((T, D), BF16),
            jax.ShapeDtypeStruct((1, LANES), F32),
            jax.ShapeDtypeStruct((1, D), F32),
        ],
        (a_b, wd_b, h1, target, g_final),
        comm=comm,
        after=after,
    )


def _ffn_bwd_act(dh2_b, wd_b, g_b, u_b, comm=(), after=()):
    T, D = dh2_b.shape
    F = wd_b.shape[0]
    tm, tf = _tile(T, 1024), _hidden_tile(F)

    def body(d_ref, w_ref, g_ref, u_ref, dg_ref, du_ref):
        d = d_ref[...]
        for c0 in range(0, tf, HIDDEN_CHUNK):
            cs = slice(c0, min(c0 + HIDDEN_CHUNK, tf))
            da = _dot(d, w_ref[cs, :], NT)
            gv = g_ref[:, cs].astype(F32)
            uv = u_ref[:, cs].astype(F32)
            sg = _sigmoid(gv)
            silu = gv * sg
            dg_ref[:, cs] = (da * uv * (sg * (1.0 + gv * (1.0 - sg)))).astype(BF16)
            du_ref[:, cs] = (da * silu).astype(BF16)

    aspec = pl.BlockSpec((tm, tf), lambda j, i: (i, j))
    return _call(
        "ffn_bwd_act",
        body,
        (F // tf, T // tm),
        [pl.BlockSpec((tm, D), lambda j, i: (i, 0)), pl.BlockSpec((tf, D), lambda j, i: (j, 0)), aspec, aspec],
        [aspec, aspec],
        [jax.ShapeDtypeStruct((T, F), BF16)] * 2,
        (dh2_b, wd_b, g_b, u_b),
        comm=comm,
        after=after,
    )


def _ffn_bwd_in(dg_b, du_b, wgT_b, wuT_b, h1, dh2, g_ffn, w_out_b, comm=(), after=()):
    T, D = h1.shape
    F = wgT_b.shape[0]
    DM = w_out_b.shape[0]
    tm = _tile(T, 512)

    def body(dg_ref, du_ref, wg_ref, wu_ref, h1_ref, dh2_ref, g_ref, wo_ref, dh1_ref, dh1b_ref, dy_ref, dgf_ref):
        i = pl.program_id(0)
        dhn = _dot(dg_ref[...], wg_ref[...], NN) + _dot(du_ref[...], wu_ref[...], NN)
        dx, dg_rows = _rms_bwd(h1_ref[...], g_ref[...], dhn)
        dh1 = dh2_ref[...] + dx
        dh1b = dh1.astype(BF16)
        dh1_ref[...] = dh1
        dh1b_ref[...] = dh1b
        dy_ref[...] = _dot(dh1b, wo_ref[...], NT)
        _accumulate(dgf_ref, i == 0, jnp.sum(dg_rows, axis=0, keepdims=True))

    row = lambda i: (i, 0)
    const = lambda i: (0, 0)
    return _call(
        "ffn_bwd_in",
        body,
        (T // tm,),
        [
            pl.BlockSpec((tm, F), row),
            pl.BlockSpec((tm, F), row),
            pl.BlockSpec((F, D), const, pipeline_mode=pl.Buffered(1)),
            pl.BlockSpec((F, D), const, pipeline_mode=pl.Buffered(1)),
            pl.BlockSpec((tm, D), row),
            pl.BlockSpec((tm, D), row),
            pl.BlockSpec((1, D), const),
            pl.BlockSpec((DM, D), const, pipeline_mode=pl.Buffered(1)),
        ],
        [pl.BlockSpec((tm, D), row), pl.BlockSpec((tm, D), row), pl.BlockSpec((tm, DM), row), pl.BlockSpec((1, D), const)],
        [
            jax.ShapeDtypeStruct((T, D), F32),
            jax.ShapeDtypeStruct((T, D), BF16),
            jax.ShapeDtypeStruct((T, DM), F32),
            jax.ShapeDtypeStruct((1, D), F32),
        ],
        (dg_b, du_b, wgT_b, wuT_b, h1, dh2, g_ffn, w_out_b),
        comm=comm,
        after=after,
    )


def _seq_bwd(z, dy, v, w_dw4, ln_g, ln_b, w_pool_b, s_pool, comm=(), after=()):
    T, CI = z.shape
    CC = ln_g.shape[1]
    n_grp, G = w_pool_b.shape[0], w_pool_b.shape[-1]
    CP = n_grp * G
    KW = w_dw4.shape[1]
    n_cc = CC // LANES
    D = CC + CP
    tt = _tile(T, 512, HALO)
    per = tt // HALO
    n_tiles = T // tt
    last_halo = T // HALO - 1

    def body(zc_ref, zp_ref, dyc_ref, dyn_ref, vc_ref, vn_ref, wdw_ref, lng_ref, lnb_ref, wp_ref, sp_ref,
             dz_ref, dwdw_ref, dbdw_ref, dlng_ref, dlnb_ref, dwp_ref, dsp_ref, dbin_ref,
             dv_scr, u_scr, p_scr, g_scr, dw_scr):
        i = pl.program_id(0)
        first = i == 0
        last = i == n_tiles - 1
        lng, lnb = lng_ref[...], lnb_ref[...]

        def conv_pre(vv, dyc):
            mu = jnp.mean(vv, axis=-1, keepdims=True)
            d = vv - mu
            rs = lax.rsqrt(jnp.mean(d * d, axis=-1, keepdims=True) + LN_EPS)
            xh = d * rs
            ln = xh * lng + lnb
            sg = _sigmoid(ln)
            dln = dyc * (sg * (1.0 + ln * (1.0 - sg)))
            dxh = dln * lng
            dv = rs * (dxh - jnp.mean(dxh, axis=-1, keepdims=True) - xh * jnp.mean(dxh * xh, axis=-1, keepdims=True))
            return dv, dln, xh

        dv_c, dln_c, xh_c = conv_pre(vc_ref[...], dyc_ref[:, 0:CC])
        dv_scr[0, 0:tt, :] = dv_c
        dv_n, _, _ = conv_pre(vn_ref[...], dyn_ref[:, 0:CC])
        dv_scr[0, tt:, :] = jnp.where(last, 0.0, dv_n)
        _fill_shifted(dv_scr)
        _accumulate(dlng_ref, first, jnp.sum(dln_c * xh_c, axis=0, keepdims=True))
        _accumulate(dlnb_ref, first, jnp.sum(dln_c, axis=0, keepdims=True))
        _accumulate(dbdw_ref, first, jnp.sum(dv_c, axis=0, keepdims=True))

        u_scr[...] = zc_ref[:, 0:CC] * _sigmoid(zc_ref[:, CC : 2 * CC])

        @pl.when(first)
        def _():
            dw_scr[...] = jnp.zeros_like(dw_scr)

        for j in range(n_cc):
            cs = slice(LANES * j, LANES * (j + 1))
            gs = slice(CC + LANES * j, CC + LANES * (j + 1))
            dbin_a = jnp.zeros((1, LANES), F32)
            dbin_g = jnp.zeros((1, LANES), F32)
            for rb in range(tt // CONV_ROWS):
                rows = slice(rb * CONV_ROWS, (rb + 1) * CONV_ROWS)
                u_blk = u_scr[rows, cs]
                du = jnp.zeros((CONV_ROWS, LANES), F32)
                for k in range(KW):
                    off = rb * CONV_ROWS + (KW - 1) - k
                    d = _shifted_rows(dv_scr, off, CONV_ROWS, cs)
                    du = du + d * wdw_ref[j, k]
                    dw_scr[j * HALO + k] += jnp.sum((u_blk * d).reshape(CONV_ROWS // 8, 8, LANES), axis=0)
                a = zc_ref[rows, cs]
                sg = _sigmoid(zc_ref[rows, gs])
                da = du * sg
                dgate = du * a * sg * (1.0 - sg)
                dz_ref[rows, cs] = da.astype(BF16)
                dz_ref[rows, gs] = dgate.astype(BF16)
                dbin_a = dbin_a + jnp.sum(da, axis=0, keepdims=True)
                dbin_g = dbin_g + jnp.sum(dgate, axis=0, keepdims=True)
            _accumulate(dbin_ref.at[:, cs], first, dbin_a)
            _accumulate(dbin_ref.at[:, gs], first, dbin_g)

        @pl.when(last)
        def _():
            dwdw_ref[...] = jnp.sum(dw_scr[...], axis=1).reshape(dwdw_ref.shape)

        p_scr[0:HALO, :] = jnp.where(first, 0.0, zp_ref[:, 2 * CC :])
        p_scr[HALO:, :] = zc_ref[:, 2 * CC :]
        tpos = i * tt + lax.broadcasted_iota(jnp.int32, (tt, 1), 0)
        for gi, w in enumerate(POOL_WINDOWS):
            cs = slice(G * gi, G * (gi + 1))
            ys = slice(CC + G * gi, CC + G * (gi + 1))
            ps = slice(2 * CC + G * gi, 2 * CC + G * (gi + 1))
            cnt = jnp.minimum(tpos + 1, w).astype(F32)
            yib = _pool_mean_minus_token(p_scr, cs, w, cnt, tt).astype(BF16)
            wp = wp_ref[gi]
            sp = sp_ref[:, cs]
            dyp = dyc_ref[:, ys]
            q = _dot(yib, wp, NN)
            _accumulate(dsp_ref.at[:, cs], first, jnp.sum(dyp * q, axis=0, keepdims=True))
            dq_c = (dyp * sp).astype(BF16)
            dq_n = (jnp.where(last, 0.0, dyn_ref[:, ys]) * sp).astype(BF16)
            _accumulate(dwp_ref.at[gi], first, _dot(yib, dq_c, TN))
            dyi_c = _dot(dq_c, wp, NT)
            g_scr[0:tt, cs] = dyi_c / cnt
            g_scr[tt:, cs] = _dot(dq_n, wp, NT) * (1.0 / w)
            dp = -dyi_c
            for d in range(w):
                dp = dp + g_scr[d : d + tt, cs]
            dz_ref[:, ps] = dp.astype(BF16)
            _accumulate(dbin_ref.at[:, ps], first, jnp.sum(dp, axis=0, keepdims=True))

    cur = lambda i: (i, 0)
    prev = lambda i: (jnp.maximum(i * per - 1, 0), 0)
    nxt = lambda i: (jnp.minimum((i + 1) * per, last_halo), 0)
    c2 = lambda i: (0, 0)
    c3 = lambda i: (0, 0, 0)
    return _call(
        "seq_bwd",
        body,
        (n_tiles,),
        [
            pl.BlockSpec((tt, CI), cur),
            pl.BlockSpec((HALO, CI), prev),
            pl.BlockSpec((tt, D), cur),
            pl.BlockSpec((HALO, D), nxt),
            pl.BlockSpec((tt, CC), cur),
            pl.BlockSpec((HALO, CC), nxt),
            pl.BlockSpec(w_dw4.shape, lambda i: (0,) * w_dw4.ndim),
            pl.BlockSpec((1, CC), c2),
            pl.BlockSpec((1, CC), c2),
            pl.BlockSpec(w_pool_b.shape, c3),
            pl.BlockSpec((1, CP), c2),
        ],
        [
            pl.BlockSpec((tt, CI), cur),
            pl.BlockSpec((n_cc, HALO, LANES), c3),
            pl.BlockSpec((1, CC), c2),
            pl.BlockSpec((1, CC), c2),
            pl.BlockSpec((1, CC), c2),
            pl.BlockSpec((n_grp, G, G), c3),
            pl.BlockSpec((1, CP), c2),
            pl.BlockSpec((1, CI), c2),
        ],
        [
            jax.ShapeDtypeStruct((T, CI), BF16),
            jax.ShapeDtypeStruct((n_cc, HALO, LANES), F32),
            jax.ShapeDtypeStruct((1, CC), F32),
            jax.ShapeDtypeStruct((1, CC), F32),
            jax.ShapeDtypeStruct((1, CC), F32),
            jax.ShapeDtypeStruct((n_grp, G, G), F32),
            jax.ShapeDtypeStruct((1, CP), F32),
            jax.ShapeDtypeStruct((1, CI), F32),
        ],
        (z, z, dy, dy, v, v, w_dw4, ln_g, ln_b, w_pool_b, s_pool),
        scratch=[
            pltpu.VMEM((SUBLANES, tt + HALO, CC), F32),
            pltpu.VMEM((tt, CC), F32),
            pltpu.VMEM((HALO + tt, CP), F32),
            pltpu.VMEM((tt + HALO, CP), F32),
            pltpu.VMEM((n_cc * HALO, 8, LANES), F32),
        ],
        comm=comm,
        after=after,
    )


def _in_proj_bwd(dz_b, w_inT_b, x, dh1, g_mix, comm=(), after=()):
    T, D = x.shape
    CI = w_inT_b.shape[0]
    tm = _tile(T, 512)

    def body(dz_ref, w_ref, x_ref, dh1_ref, g_ref, dx_ref, dg_ref):
        i = pl.program_id(0)
        dxn = _dot(dz_ref[...], w_ref[...], NN)
        dx, dg_rows = _rms_bwd(x_ref[...], g_ref[...], dxn)
        dx_ref[...] = dh1_ref[...] + dx
        _accumulate(dg_ref, i == 0, jnp.sum(dg_rows, axis=0, keepdims=True))

    row = lambda i: (i, 0)
    const = lambda i: (0, 0)
    return _call(
        "in_proj_bwd",
        body,
        (T // tm,),
        [
            pl.BlockSpec((tm, CI), row),
            pl.BlockSpec((CI, D), const),
            pl.BlockSpec((tm, D), row),
            pl.BlockSpec((tm, D), row),
            pl.BlockSpec((1, D), const),
        ],
        [pl.BlockSpec((tm, D), row), pl.BlockSpec((1, D), const)],
        [jax.ShapeDtypeStruct((T, D), F32), jax.ShapeDtypeStruct((1, D), F32)],
        (dz_b, w_inT_b, x, dh1, g_mix),
        comm=comm,
        after=after,
    )


def _weight_grad(name, a_b, b_b, comm=(), after=()):
    T, N1 = a_b.shape
    N2 = b_b.shape[1]
    t1 = _tile(N1, 1408, LANES)
    tk = _tile(T, 2048)
    nk = T // tk

    def body(a_ref, b_ref, o_ref, acc):
        k = pl.program_id(1)
        _accumulate(acc, k == 0, _dot(a_ref[...], b_ref[...], TN))

        @pl.when(k == nk - 1)
        def _():
            o_ref[...] = acc[...].astype(BF16)

    (out,), rest = _call(
        name,
        body,
        (N1 // t1, nk),
        [pl.BlockSpec((tk, t1), lambda n, k: (k, n)), pl.BlockSpec((tk, N2), lambda n, k: (k, 0))],
        [pl.BlockSpec((t1, N2), lambda n, k: (n, 0))],
        [jax.ShapeDtypeStruct((N1, N2), BF16)],
        (a_b, b_b),
        scratch=[pltpu.VMEM((t1, N2), F32)],
        comm=comm,
        after=after,
    )
    return out, rest


def _sum_parts(name, full, how, parts, me):
    _, R, C = parts[0].shape
    tr = _tile(R, 512)
    nb = R // tr
    where = [(q, r) for q, p in enumerate(parts) for r in range(p.shape[0])]
    assert len(where) == 3

    def body(me_ref, own_ref, *refs):
        o_ref = refs[-1]
        f = lambda j: refs[where[j][0]][where[j][1]].astype(F32)
        o_ref[...] = (own_ref[...].astype(F32) + f(0)) + (f(1) + f(2))

    own_map = {"rows": lambda i, me_ref: (me_ref[0] * nb + i, 0), "cols": lambda i, me_ref: (i, me_ref[0]),
               "all": lambda i, me_ref: (i, 0)}[how]
    return pl.pallas_call(
        body,
        name=name,
        grid_spec=pltpu.PrefetchScalarGridSpec(
            num_scalar_prefetch=1,
            grid=(nb,),
            in_specs=[pl.BlockSpec((tr, C), own_map)]
            + [pl.BlockSpec((p.shape[0], tr, C), lambda i, me_ref: (0, i, 0)) for p in parts],
            out_specs=pl.BlockSpec((tr, C), lambda i, me_ref: (i, 0)),
        ),
        out_shape=jax.ShapeDtypeStruct((R, C), F32),
        compiler_params=pltpu.CompilerParams(dimension_semantics=("arbitrary",), vmem_limit_bytes=VMEM_LIMIT),
    )(me, full, *parts)


_M_CORR = 1.0 - ADAM_B1**ADAM_STEP
_V_CORR = 1.0 - ADAM_B2**ADAM_STEP


def _adamw_math(w, g, m, v):
    m = ADAM_B1 * m + (1.0 - ADAM_B1) * g
    v = ADAM_B2 * v + (1.0 - ADAM_B2) * (g * g)
    delta = -ADAM_LR * ((m / _M_CORR) / (jnp.sqrt(v / _V_CORR) + ADAM_EPS) + ADAM_WD * w)
    return delta, m, v


def _adamw(name, w, m, v, g_here, g_there, g_transposed=False, comm=()):
    R, C = w.shape
    tr = _tile(R, 256, LANES if g_transposed else 8)

    def body(w_ref, m_ref, v_ref, ga_ref, gb_ref, g_ref, d_ref, nm_ref, nv_ref):
        g = ga_ref[...] + gb_ref[...]
        if g_transposed:
            g = g.T
        g_ref[...] = g
        d_ref[...], nm_ref[...], nv_ref[...] = _adamw_math(w_ref[...], g, m_ref[...], v_ref[...])

    spec = pl.BlockSpec((tr, C), lambda i: (i, 0))
    gspec = pl.BlockSpec((C, tr), lambda i: (0, i)) if g_transposed else spec
    return _call(name, body, (R // tr,), [spec] * 3 + [gspec] * 2, [spec] * 4, [jax.ShapeDtypeStruct((R, C), F32)] * 4,
                 (w, m, v, g_here, g_there), comm=comm)


def _adamw_on_sparsecore(name, w, m, v, g_here, g_there):
    R, C = w.shape
    n_groups = R // SUBLANES
    n_turns = -(-n_groups // SC_TILES)
    n_in, n_out = 5, 4

    def body(w_hbm, m_hbm, v_hbm, ga_hbm, gb_hbm, g_out, d_out, nm_out, nv_out, bufs, sems):
        tile = lax.axis_index("subcore") * SC_CORES + lax.axis_index("sparsecore")
        srcs = (w_hbm, m_hbm, v_hbm, ga_hbm, gb_hbm)
        dsts = (d_out, nm_out, nv_out, g_out)

        def rows(turn):
            return pl.ds((tile + turn * SC_TILES) * SUBLANES, SUBLANES)

        def loads(turn):
            slot = turn % 2
            return [pltpu.make_async_copy(srcs[q].at[rows(turn), :], bufs.at[slot, q], sems.at[slot, q]) for q in range(n_in)]

        def stores(turn):
            slot = turn % 2
            return [pltpu.make_async_copy(bufs.at[slot, q], dsts[q].at[rows(turn), :], sems.at[slot, n_in + q])
                    for q in range(n_out)]

        def when_mine(turn, fn):
            pl.when(tile + turn * SC_TILES < n_groups)(fn)

        def compute(slot):
            wb, mb, vb, gab, gbb = (bufs.at[slot, q] for q in range(n_in))

            @pl.loop(0, SUBLANES)
            def _(r):
                @pl.loop(0, C, step=SC_LANES)
                def _(i):
                    at = (r, pl.ds(i, SC_LANES))
                    g = gab[at] + gbb[at]
                    delta, new_m, new_v = _adamw_math(wb[at], g, mb[at], vb[at])
                    gab[at], wb[at], mb[at], vb[at] = g, delta, new_m, new_v

        def start_loads(turn):
            def fn():
                for cp in loads(turn):
                    cp.start()

            when_mine(turn, fn)

        start_loads(0)
        for turn in range(n_turns):
            def step(turn=turn):
                for cp in loads(turn):
                    cp.wait()
                if turn >= 1:
                    for cp in stores(turn - 1):
                        cp.wait()
                if turn + 1 < n_turns:
                    start_loads(turn + 1)
                compute(turn % 2)
                for cp in stores(turn):
                    cp.start()

            when_mine(turn, step)
        for turn in range(n_turns):
            def drain(turn=turn):
                for cp in stores(turn):
                    cp.wait()

            last_mine = jnp.logical_and(tile + turn * SC_TILES < n_groups, tile + (turn + 1) * SC_TILES >= n_groups)
            pl.when(last_mine)(drain)

    return pl.kernel(
        body,
        name=name,
        out_type=[jax.ShapeDtypeStruct((R, C), F32)] * 4,
        mesh=plsc.VectorSubcoreMesh(core_axis_name="sparsecore", subcore_axis_name="subcore"),
        scratch_types=[pltpu.VMEM((2, n_in, SUBLANES, C), F32), pltpu.SemaphoreType.DMA((2, n_in + n_out))],
        compiler_params=pltpu.CompilerParams(use_tc_tiling_on_sc=True),
    )(w, m, v, g_here, g_there)


class _PackLayout:
    def __init__(self, n_cc, n_grp, G, widths):
        self.dw_rows = (0, HALO)
        self.wp_rows = (HALO, HALO + G)
        self.n_cc, self.n_grp, self.G = n_cc, n_grp, G
        self.vec = {}
        r = HALO + G
        for name, width in widths:
            self.vec[name] = (r, width)
            r += width // PACK_W
        self.rows = -(-r // 8) * 8


def _pack_small(layout, dwdw, dwp, vecs):
    names = list(vecs)

    def body(*refs):
        dw_ref, wp_ref = refs[0], refs[1]
        vec_refs = refs[2 : 2 + len(names)]
        o_ref = refs[-1]
        o_ref[...] = jnp.zeros_like(o_ref)
        for j in range(layout.n_cc):
            o_ref[layout.dw_rows[0] : layout.dw_rows[1], j * LANES : (j + 1) * LANES] = dw_ref[j]
        for i in range(layout.n_grp):
            o_ref[layout.wp_rows[0] : layout.wp_rows[1], i * layout.G : (i + 1) * layout.G] = wp_ref[i]
        for name, ref in zip(names, vec_refs):
            r, width = layout.vec[name]
            for h in range(width // PACK_W):
                o_ref[r + h : r + h + 1, :] = ref[:, h * PACK_W : (h + 1) * PACK_W]

    return pl.pallas_call(
        body,
        name="pack_small",
        out_shape=jax.ShapeDtypeStruct((layout.rows, PACK_W), F32),
    )(dwdw, dwp, *[vecs[k] for k in names])


def _adamw_small(layout, g_here, g_there, w_dw, m_dw, v_dw, w_pool, m_pool, v_pool, vec_w, vec_m, vec_v):
    names = list(vec_w)
    nv = len(names)

    def body(*refs):
        ga_ref, gb_ref = refs[0], refs[1]
        wdw, mdw, vdw, wp, mp, vp = refs[2:8]
        vw, vm, vv = refs[8 : 8 + nv], refs[8 + nv : 8 + 2 * nv], refs[8 + 2 * nv : 8 + 3 * nv]
        outs = refs[8 + 3 * nv :]
        acc = outs[-1]
        acc[...] = ga_ref[...] + gb_ref[...]

        def emit(o, g, w, m, v, idx=()):
            res = (g,) + _adamw_math(w, g, m, v)
            for ref, val in zip(o, res):
                ref[idx] = val

        me = 2 * lax.axis_index("x") + lax.axis_index("y")
        for j in range(layout.n_cc):

            @pl.when(me == j)
            def _(j=j):
                for k in range(wdw.shape[0]):
                    g = acc[layout.dw_rows[0] + k : layout.dw_rows[0] + k + 1, j * LANES : (j + 1) * LANES]
                    emit(outs[0:4], g, wdw[k], mdw[k], vdw[k], idx=k)

        for i in range(layout.n_grp):
            g = acc[layout.wp_rows[0] : layout.wp_rows[1], i * layout.G : (i + 1) * layout.G]
            emit(outs[4:8], g, wp[i], mp[i], vp[i], idx=i)
        for q, name in enumerate(names):
            r, width = layout.vec[name]
            for h in range(width // PACK_W):
                ls = slice(h * PACK_W, (h + 1) * PACK_W)
                g = acc[r + h : r + h + 1, :]
                emit(outs[8 + 4 * q : 12 + 4 * q], g, vw[q][:, ls], vm[q][:, ls], vv[q][:, ls], idx=(slice(None), ls))

    shapes = [w_dw.shape] * 4 + [w_pool.shape] * 4
    for name in names:
        shapes += [vec_w[name].shape] * 4
    return pl.pallas_call(
        body,
        name="adamw_small",
        out_shape=[jax.ShapeDtypeStruct(s, F32) for s in shapes],
        scratch_shapes=[pltpu.VMEM(g_here.shape, F32)],
    )(g_here, g_there, w_dw, m_dw, v_dw, w_pool, m_pool, v_pool,
      *[vec_w[k] for k in names], *[vec_m[k] for k in names], *[vec_v[k] for k in names])


def _allreduce_adamw_row(g_part, w, m, v, loss_part, comm=()):
    D = w.shape[1]
    n_pairs = N_DEV - 1

    def body(g_ref, w_ref, m_ref, v_ref, l_ref, go_ref, d_ref, nm_ref, nv_ref, lo_ref, land_g, land_l, sems):
        x, y, c = _place()
        copies = []
        for q, (src, land) in enumerate(((g_ref, land_g), (l_ref, land_l))):
            for r in range(1, N_DEV):
                fx, fy, fc = (r >> 2) & 1, (r >> 1) & 1, r & 1
                peer = (1 - x if fx else x, 1 - y if fy else y, 1 - c if fc else c)
                cp = _remote(src, land.at[r], sems, 2 * (q * n_pairs + r - 1), peer)
                cp.start()
                copies.append(cp)
        for cp in copies:
            cp.wait()

        def total(src, land):
            row = lambda r: src[...] if r == 0 else land[r]
            return ((row(0) + row(4)) + (row(2) + row(6))) + ((row(1) + row(5)) + (row(3) + row(7)))

        g = total(g_ref, land_g)
        go_ref[...] = g
        d_ref[...], nm_ref[...], nv_ref[...] = _adamw_math(w_ref[...], g, m_ref[...], v_ref[...])
        lo_ref[...] = total(l_ref, land_l)

    vm = pl.BlockSpec(memory_space=pltpu.VMEM)
    return _call(
        "allreduce_adamw_g_mix",
        body,
        (),
        [vm] * 5,
        [vm] * 5,
        [jax.ShapeDtypeStruct((1, D), F32)] * 4 + [jax.ShapeDtypeStruct(loss_part.shape, F32)],
        (g_part, w, m, v, loss_part),
        scratch=[pltpu.VMEM((N_DEV, 1, D), F32), pltpu.VMEM((N_DEV,) + loss_part.shape, F32),
                 pltpu.SemaphoreType.DMA((4 * n_pairs,))],
        comm=comm,
    )


def kernel(x, g_mix, w_in, b_in, w_dw, b_dw, ln_g, ln_b, w_pool, s_pool, w_out, g_ffn, w_gate, w_up, w_down, g_final, loss_target, m_g_mix, m_w_in, m_b_in, m_w_dw, m_b_dw, m_ln_g, m_ln_b, m_w_pool, m_s_pool, m_w_out, m_g_ffn, m_w_gate, m_w_up, m_w_down, m_g_final, v_g_mix, v_w_in, v_b_in, v_w_dw, v_b_dw, v_ln_g, v_ln_b, v_w_pool, v_s_pool, v_w_out, v_g_ffn, v_w_gate, v_w_up, v_w_down, v_g_final):
    x2 = x[0]
    target = loss_target[0]
    T, D = x2.shape
    w_in2, w_out2, w_down2 = w_in[0], w_out[0], w_down[0]
    taps_first = lambda a: jnp.transpose(a, (1, 0, 2))
    w_dw3 = taps_first(w_dw)
    w_gateT, w_upT = w_gate[0].T, w_up[0].T
    CI = w_in2.shape[1] * N_CHIPS
    DM = w_out2.shape[0] * N_CHIPS
    F = w_down2.shape[0] * N_CHIPS
    KW, _, dw_cols = w_dw3.shape
    assert dw_cols == LANES
    n_grp, G = w_pool.shape[1], w_pool.shape[-1]
    w_pool3 = w_pool[0]
    g_final2 = g_final.reshape(1, D)

    me = (2 * lax.axis_index("x") + lax.axis_index("y")).astype(jnp.int32).reshape(1)

    w_inT_b, w_dw4, f_out, f_gate, f_up, f_down = _place_and_gather(
        [(w_in2, "rows", (CI, D), BF16, True, True), (w_dw3, "lead", (N_CHIPS, KW, 1, dw_cols), F32, False, False)],
        [(w, "rows", shape, BF16, False, True)
         for w, shape in ((w_out2, (DM, D)), (w_gateT, (F, D)), (w_upT, (F, D)), (w_down2, (F, D)))])
    w_pool_b = w_pool3.astype(BF16)
    ici = lambda f: _GatherIci([f], ["rows"], [True])
    d2d = lambda f: _GatherD2d([f], ["rows"])
    gather = _start("gather_start", [ici(f_out), ici(f_gate), ici(f_up), ici(f_down)])
    (z, xn_b), _ = _in_proj(x2, g_mix, w_inT_b, b_in, after=[gather.token])
    (f_out,) = _wait("gather_out_wait", gather, 0, xn_b)
    s_out = _start("share_out_start", [d2d(f_out)], sibling_only=True)
    (y_b, v), _ = _seq_fwd(z, w_dw4, b_dw, ln_g, ln_b, w_pool_b, s_pool, after=[s_out.token])
    (w_out_b,) = _wait("share_out_wait", s_out, 0, y_b)
    (f_gate,) = _wait("gather_gate_wait", gather, 1, y_b)
    s_gate = _start("share_gate_start", [d2d(f_gate)], sibling_only=True)
    (h1, hn_b), _ = _out_proj(y_b, x2, w_out_b, g_ffn, after=[s_gate.token])
    (f_up,) = _wait("gather_up_wait", gather, 2, hn_b)
    s_up = _start("share_up_start", [d2d(f_up)], sibling_only=True)
    (wgT_b,) = _wait("share_gate_wait", s_gate, 0, hn_b)
    (wuT_b,) = _wait("share_up_wait", s_up, 0, hn_b)
    (g_b, u_b, a_b), _ = _gate_up(hn_b, wgT_b, wuT_b)
    (f_down,) = _wait("gather_down_wait", gather, 3, a_b)
    s_down = _start("share_down_start", [d2d(f_down)], sibling_only=True)
    (wd_b,) = _wait("share_down_wait", s_down, 0, a_b)
    (dh2, dh2_b, loss_part, d_g_final), _ = _down_loss(a_b, wd_b, h1, target, g_final2)

    scatter = lambda g: _Scatter([g], ["rows"])
    swap = lambda *sums: [_Swap([sm]) for sm in sums]
    gw_down, _ = _weight_grad("grad_w_down", a_b, dh2_b)
    x_down = _start("scatter_down_start", [scatter(gw_down)])
    (dg_b, du_b), _ = _ffn_bwd_act(dh2_b, wd_b, g_b, u_b, after=[x_down.token])
    gw_gateT, _ = _weight_grad("grad_w_gate", dg_b, hn_b)
    x_gate = _start("scatter_gate_start", [scatter(gw_gateT)])
    gw_upT, _ = _weight_grad("grad_w_up", du_b, hn_b, after=[x_gate.token])
    x_up = _start("scatter_up_start", [scatter(gw_upT)])
    gw_down, p_down = _wait("scatter_down_wait", x_down, 0, gw_upT)
    sum_down = _sum_parts("sum_w_down", gw_down, "rows", [p_down], me)
    y_down = _start("swap_down_start", swap(sum_down), after=[x_up.token], sibling_only=True)
    (dh1, dh1_b, dy, d_g_ffn), _ = _ffn_bwd_in(dg_b, du_b, wgT_b, wuT_b, h1, dh2, g_ffn, w_out_b, after=[y_down.token])
    sum_down, oth_down = _wait("swap_down_wait", y_down, 0, dy)
    res = {}
    res["w_down"] = _adamw_on_sparsecore("adamw_w_down", w_down2, m_w_down[0], v_w_down[0], sum_down, oth_down)
    gw_out, _ = _weight_grad("grad_w_out", y_b, dh1_b)
    x_out = _start("scatter_out_start", [scatter(gw_out)])
    gw_gateT, p_gate = _wait("scatter_gate_wait", x_gate, 0, gw_out)
    sum_gate = _sum_parts("sum_w_gate", gw_gateT, "rows", [p_gate], me)
    y_gate = _start("swap_gate_start", swap(sum_gate), after=[x_out.token], sibling_only=True)
    (dz_b, d_wdw, d_bdw, d_lng, d_lnb, d_wp, d_sp, d_bin), _ = _seq_bwd(
        z, dy, v, w_dw4, ln_g, ln_b, w_pool_b, s_pool, after=[y_gate.token])
    sum_gate, oth_gate = _wait("swap_gate_wait", y_gate, 0, dz_b)
    res["w_gate"] = _adamw_on_sparsecore("adamw_w_gate", w_gateT, m_w_gate[0].T, v_w_gate[0].T, sum_gate, oth_gate)
    vec_grads = {"b_dw": d_bdw, "ln_g": d_lng, "ln_b": d_lnb, "s_pool": d_sp, "g_ffn": d_g_ffn, "g_final": d_g_final, "b_in": d_bin}
    layout = _PackLayout(dw_cols * N_CHIPS // LANES, n_grp, G, [(k, a.shape[1]) for k, a in vec_grads.items()])
    pack = _pack_small(layout, d_wdw, d_wp, vec_grads)
    x_small = _start("scatter_small_start", [_Scatter([pack], ["all"])])
    gw_upT, p_up = _wait("scatter_up_wait", x_up, 0, dz_b)
    gw_out, p_out = _wait("scatter_out_wait", x_out, 0, dz_b)
    sum_up = _sum_parts("sum_w_up", gw_upT, "rows", [p_up], me)
    sum_out = _sum_parts("sum_w_out", gw_out, "rows", [p_out], me)
    y_up_out = _start("swap_up_out_start", swap(sum_up, sum_out), after=[x_small.token], sibling_only=True)
    gw_inT, _ = _weight_grad("grad_w_in", dz_b, xn_b, after=[y_up_out.token])
    sum_up, oth_up = _wait("swap_up_wait", y_up_out, 0, gw_inT)
    sum_out, oth_out = _wait("swap_out_wait", y_up_out, 1, gw_inT)
    res["w_up"] = _adamw_on_sparsecore("adamw_w_up", w_upT, m_w_up[0].T, v_w_up[0].T, sum_up, oth_up)
    res["w_out"] = _adamw_on_sparsecore("adamw_w_out", w_out2, m_w_out[0], v_w_out[0], sum_out, oth_out)
    pack, p_small = _wait("scatter_small_wait", x_small, 0, gw_inT)
    sum_small = _sum_parts("sum_small", pack, "all", [p_small], me)
    late = _start("late_start", [_Scatter([gw_inT], ["rows"]), _Swap([sum_small])])
    (grad_x, d_g_mix), _ = _in_proj_bwd(dz_b, w_inT_b, x2, dh1, g_mix, after=[late.token])
    gw_inT, p_in = _wait("late_w_in_wait", late, 0, d_g_mix)
    sum_small, oth_small = _wait("late_small_wait", late, 1, d_g_mix)
    sum_in = _sum_parts("sum_w_in", gw_inT, "rows", [p_in], me)
    (*res["g_mix"], loss_row), (oth_in,) = _allreduce_adamw_row(
        d_g_mix, g_mix, m_g_mix, v_g_mix, loss_part, comm=[_Swap([sum_in])])
    loss = loss_row[0, 0]
    res["w_in"], _ = _adamw("adamw_w_in", w_in2, m_w_in[0], v_w_in[0], sum_in, oth_in, g_transposed=True)

    vec_w = {"b_dw": b_dw, "ln_g": ln_g, "ln_b": ln_b, "s_pool": s_pool, "g_ffn": g_ffn, "g_final": g_final2, "b_in": b_in}
    vec_m = {"b_dw": m_b_dw, "ln_g": m_ln_g, "ln_b": m_ln_b, "s_pool": m_s_pool, "g_ffn": m_g_ffn,
             "g_final": m_g_final.reshape(1, D), "b_in": m_b_in}
    vec_v = {"b_dw": v_b_dw, "ln_g": v_ln_g, "ln_b": v_ln_b, "s_pool": v_s_pool, "g_ffn": v_g_ffn,
             "g_final": v_g_final.reshape(1, D), "b_in": v_b_in}
    small = _adamw_small(layout, sum_small, oth_small, w_dw3, taps_first(m_w_dw), taps_first(v_w_dw),
                         w_pool3, m_w_pool[0], v_w_pool[0], vec_w, vec_m, vec_v)
    res["w_dw"] = [taps_first(a) for a in small[0:4]]
    res["w_pool"] = [a[None] for a in small[4:8]]
    for q, k in enumerate(vec_w):
        res[k] = list(small[8 + 4 * q : 12 + 4 * q])
    res["g_final"] = [a.reshape(D) for a in res["g_final"]]
    for k in ("w_in", "w_out", "w_down"):
        res[k] = [a[None] for a in res[k]]
    for k in ("w_gate", "w_up"):
        res[k] = [a.T[None] for a in res[k]]

    order = ["g_mix", "w_in", "b_in", "w_dw", "b_dw", "ln_g", "ln_b", "w_pool", "s_pool", "w_out", "g_ffn", "w_gate", "w_up", "w_down", "g_final"]
    outs = [loss, grad_x[None]]
    for q in range(4):
        outs += [res[k][q] for k in order]
    return tuple(outs)
```

```python
import jax
import jax.numpy as jnp
from jax import lax
from jax.experimental import pallas as pl
from jax.experimental.pallas import tpu as pltpu
from jax.experimental.pallas import tpu_sc as plsc

F32 = jnp.float32
BF16 = jnp.bfloat16
MESH = pl.DeviceIdType.MESH
ANY = pl.BlockSpec(memory_space=pl.ANY)

RMS_EPS = 1e-6
LN_EPS = 1e-5
POOL_WINDOWS = (2, 4, 8, 16)
ADAM_LR = 0.001
ADAM_B1 = 0.9
ADAM_B2 = 0.999
ADAM_EPS = 1e-08
ADAM_WD = 0.01
ADAM_STEP = 10

LANES = 128
SUBLANES = 8
HALO = 32
CONV_ROWS = 64
HIDDEN_CHUNK = 512
VMEM_LIMIT = 56 * 1024 * 1024
PACK_W = 512
N_CHIPS = 4
N_DEV = 8
SIBLING_BARRIER_ID = 0
SC_CORES = 2
SC_TILES = 32
SC_LANES = 16


def _tile(n, want, mult=8):
    t = min(n, want)
    while n % t or t % mult:
        t -= 1
    return t


def _sigmoid(x):
    return 1.0 / (1.0 + jnp.exp(-x))


def _dot(a, b, dims):
    return lax.dot_general(a, b, (dims, ((), ())), preferred_element_type=F32)


NN = ((1,), (0,))
NT = ((1,), (1,))
TN = ((0,), (0,))


def _rms_bwd(x, g, dy):
    r = lax.rsqrt(jnp.mean(x * x, axis=-1, keepdims=True) + RMS_EPS)
    xh = x * r
    gy = dy * g
    dx = r * (gy - xh * jnp.mean(gy * xh, axis=-1, keepdims=True))
    return dx, dy * xh


def _accumulate(ref, first, val):
    @pl.when(first)
    def _():
        ref[...] = val

    @pl.when(jnp.logical_not(first))
    def _():
        ref[...] += val


def _place():
    return lax.axis_index("x"), lax.axis_index("y"), lax.axis_index("c")


def _other_chips(x, y):
    return [(1 - x, y), (x, 1 - y), (1 - x, 1 - y)]


def _rows(ref, start, n):
    return ref.at[pl.ds(pl.multiple_of(start, 16), n)]


def _window(ref, how, k, c=None):
    if how == "all":
        return ref
    if how == "lead":
        return ref.at[k]
    if how == "rows":
        n = ref.shape[0] // N_CHIPS
        if c is None:
            return _rows(ref, k * n, n)
        return _rows(ref, k * n + c * (n // 2), n // 2)
    n = ref.shape[1] // N_CHIPS
    cols = pl.ds(pl.multiple_of(k * n, LANES), n)
    if c is None:
        return ref.at[:, cols]
    h = ref.shape[0] // 2
    return ref.at[pl.ds(pl.multiple_of(c * h, 16), h), cols]


def _remote(src, dst, sems, s, device):
    return pltpu.make_async_remote_copy(
        src_ref=src, dst_ref=dst, send_sem=sems.at[s], recv_sem=sems.at[s + 1], device_id=device, device_id_type=MESH)


class _GatherIci:
    aliased = True

    def __init__(self, fulls, hows, splits, which=(0, 1, 2)):
        self.fulls, self.hows, self.splits, self.which = list(fulls), list(hows), list(splits), tuple(which)

    def inputs(self):
        return self.fulls

    def out_shapes(self):
        return [jax.ShapeDtypeStruct(a.shape, a.dtype) for a in self.fulls]

    def n_sems(self):
        return 6 * len(self.fulls)

    def build(self, ins, outs, sems, base):
        x, y, c = _place()
        me = 2 * x + y
        chips = _other_chips(x, y)
        starts, waits = [], []
        for a, (how, sp) in enumerate(zip(self.hows, self.splits)):
            half = c if sp else None
            mine = _window(outs[a], how, me, half)
            for j in self.which:
                px, py = chips[j]
                s = base + 6 * a + 2 * j
                cp = _remote(mine, mine, sems, s, (px, py, c))
                landing = _remote(mine, _window(outs[a], how, 2 * px + py, half), sems, s, (px, py, c))
                starts.append(cp.start)
                waits += [landing.wait_recv, cp.wait_send]
        return starts, waits


class _GatherD2d:
    aliased = True

    def __init__(self, fulls, hows):
        self.fulls, self.hows = list(fulls), list(hows)

    def inputs(self):
        return self.fulls

    def out_shapes(self):
        return [jax.ShapeDtypeStruct(a.shape, a.dtype) for a in self.fulls]

    def n_sems(self):
        return 6 * len(self.fulls)

    def build(self, ins, outs, sems, base):
        x, y, c = _place()
        starts, waits = [], []
        for a, how in enumerate(self.hows):
            for j, (px, py) in enumerate(_other_chips(x, y)):
                s = base + 6 * a + 2 * j
                got = _window(outs[a], how, 2 * px + py, c)
                cp = _remote(got, got, sems, s, (x, y, 1 - c))
                landing = _remote(got, _window(outs[a], how, 2 * px + py, 1 - c), sems, s, (x, y, 1 - c))
                starts.append(cp.start)
                waits += [landing.wait_recv, cp.wait_send]
        return starts, waits


def _part_shape(a, how):
    if how == "all":
        return a.shape
    if how == "rows":
        return (a.shape[0] // N_CHIPS, a.shape[1])
    return (a.shape[0], a.shape[1] // N_CHIPS)


class _Scatter:
    aliased = False

    def __init__(self, fulls, hows, which=(0, 1, 2)):
        self.fulls, self.hows, self.which = list(fulls), list(hows), tuple(which)

    def inputs(self):
        return self.fulls

    def out_shapes(self):
        return [jax.ShapeDtypeStruct((len(self.which),) + _part_shape(a, h), a.dtype) for a, h in zip(self.fulls, self.hows)]

    def n_sems(self):
        return 6 * len(self.fulls)

    def build(self, ins, outs, sems, base):
        x, y, c = _place()
        chips = _other_chips(x, y)
        starts, waits = [], []
        for a, how in enumerate(self.hows):
            for slot, j in enumerate(self.which):
                px, py = chips[j]
                cp = _remote(_window(ins[a], how, 2 * px + py), outs[a].at[slot], sems, base + 6 * a + 2 * j, (px, py, c))
                starts.append(cp.start)
                waits += [cp.wait_recv, cp.wait_send]
        return starts, waits


class _Swap:
    aliased = False

    def __init__(self, arrays):
        self.arrays = list(arrays)

    def inputs(self):
        return self.arrays

    def out_shapes(self):
        return [jax.ShapeDtypeStruct(a.shape, a.dtype) for a in self.arrays]

    def n_sems(self):
        return 2 * len(self.arrays)

    def build(self, ins, outs, sems, base):
        x, y, c = _place()
        starts, waits = [], []
        for a in range(len(ins)):
            cp = _remote(ins[a], outs[a], sems, base + 2 * a, (x, y, 1 - c))
            starts.append(cp.start)
            waits += [cp.wait_recv, cp.wait_send]
        return starts, waits


def _call(name, body, grid, in_specs, out_specs, out_shape, args, scratch=(), comm=(), after=()):
    comm, after = list(comm), list(after)
    n_in, n_out, n_scr, n_after = len(args), len(out_shape), len(scratch), len(after)
    c_in = [a for op in comm for a in op.inputs()]
    c_out = [s for op in comm for s in op.out_shapes()]
    n_sems = sum(op.n_sems() for op in comm)
    aliases, i_in, i_out = {}, 0, 0
    for op in comm:
        if op.aliased:
            for q in range(len(op.inputs())):
                aliases[n_in + n_after + i_in + q] = n_out + i_out + q
        i_in, i_out = i_in + len(op.inputs()), i_out + len(op.out_shapes())

    def wrapped(*refs):
        ins = refs[:n_in]
        cin = refs[n_in + n_after : n_in + n_after + len(c_in)]
        o0 = n_in + n_after + len(c_in)
        outs = refs[o0 : o0 + n_out]
        cout = refs[o0 + n_out : o0 + n_out + len(c_out)]
        s0 = o0 + n_out + len(c_out)
        scr = refs[s0 : s0 + n_scr]

        def copies():
            sems = refs[s0 + n_scr]
            starts, waits = [], []
            i_in = i_out = base = 0
            for op in comm:
                ni, no = len(op.inputs()), len(op.out_shapes())
                s, w = op.build(cin[i_in : i_in + ni], cout[i_out : i_out + no], sems, base)
                starts += s
                waits += w
                i_in, i_out, base = i_in + ni, i_out + no, base + op.n_sems()
            return starts, waits

        def run_starts():
            for start in copies()[0]:
                start()

        def run_waits():
            for wait in copies()[1]:
                wait()

        if comm and grid:
            first = last = True
            for d, n in enumerate(grid):
                first = jnp.logical_and(first, pl.program_id(d) == 0)
                last = jnp.logical_and(last, pl.program_id(d) == n - 1)
            pl.when(first)(run_starts)
        elif comm:
            run_starts()
        if body is not None:
            body(*ins, *outs, *scr)
        if comm and grid:
            pl.when(last)(run_waits)
        elif comm:
            run_waits()

    res = pl.pallas_call(
        wrapped,
        name=name,
        grid=grid,
        in_specs=list(in_specs) + [ANY] * (n_after + len(c_in)),
        out_specs=list(out_specs) + [ANY] * len(c_out),
        out_shape=list(out_shape) + c_out,
        scratch_shapes=list(scratch) + ([pltpu.SemaphoreType.DMA((n_sems,))] if comm else []),
        input_output_aliases=aliases,
        compiler_params=pltpu.CompilerParams(dimension_semantics=("arbitrary",) * len(grid), vmem_limit_bytes=VMEM_LIMIT),
    )(*args, *after, *c_in)
    return tuple(res[:n_out]), tuple(res[n_out:])


def _place_and_gather(now, later):
    items = list(now) + list(later)
    n, n_now = len(items), len(now)
    buf_shape = lambda it: it[0].shape[::-1] if it[4] else it[0].shape
    split_now = [a for a in range(n_now) if items[a][5]]

    def body(*refs):
        ins, outs = refs[:n], refs[n : 2 * n]
        stage, bufs = refs[2 * n : 3 * n - n_now], refs[3 * n - n_now : 4 * n - n_now]
        sems = refs[4 * n - n_now]
        x, y, c = _place()
        me = 2 * x + y
        chips = _other_chips(x, y)
        loads = [pltpu.make_async_copy(ins[a], stage[a - n_now], sems.at[a]) for a in range(n_now, n)]
        for ld in loads:
            ld.start()
        pending = []

        def place(a, val):
            _, how, _, dtype, transposed, _ = items[a]
            bufs[a][...] = (val.T if transposed else val).astype(dtype)
            cp = pltpu.make_async_copy(bufs[a], _window(outs[a], how, me), sems.at[n + a])
            cp.start()
            pending.append(cp.wait)

        arrivals = []
        for a in range(n_now):
            place(a, ins[a][...])
            how, split = items[a][1], items[a][5]
            half = c if split else None
            src = _rows(bufs[a], c * (bufs[a].shape[0] // 2), bufs[a].shape[0] // 2) if split else bufs[a]
            for j, (px, py) in enumerate(chips):
                s = 2 * n + 6 * a + 2 * j
                cp = _remote(src, _window(outs[a], how, me, half), sems, s, (px, py, c))
                landing = _remote(src, _window(outs[a], how, 2 * px + py, half), sems, s, (px, py, c))
                cp.start()
                arrivals.append(landing.wait_recv)
                pending.append(cp.wait_send)
        for a in range(n_now, n):
            loads[a - n_now].wait()
            place(a, stage[a - n_now][...])
        for wait in arrivals:
            wait()
        d2d = _GatherD2d([None] * len(split_now), [items[a][1] for a in split_now])
        starts, waits = d2d.build(None, [outs[a] for a in split_now], sems, 2 * n + 6 * n_now)
        for start in starts:
            start()
        for wait in waits + pending:
            wait()

    vm = pl.BlockSpec(memory_space=pltpu.VMEM)
    return pl.pallas_call(
        body,
        name="place_and_gather",
        in_specs=[vm] * n_now + [ANY] * (n - n_now),
        out_specs=[ANY] * n,
        out_shape=[jax.ShapeDtypeStruct(it[2], it[3]) for it in items],
        scratch_shapes=[pltpu.VMEM(it[0].shape, it[0].dtype) for it in later]
        + [pltpu.VMEM(buf_shape(it), it[3]) for it in items]
        + [pltpu.SemaphoreType.DMA((2 * n + 6 * n_now + 6 * len(split_now),))],
        compiler_params=pltpu.CompilerParams(vmem_limit_bytes=VMEM_LIMIT),
    )(*[it[0] for it in items])


_HBM = pl.BlockSpec(memory_space=pltpu.HBM)
_SEM = pl.BlockSpec(memory_space=pltpu.SEMAPHORE)
_DATAFLOW = pltpu.SideEffectType.DATAFLOW_SIDE_EFFECTING


class _Pending:
    def __init__(self, ops, bases, sems, arrays, token):
        self.ops, self.bases, self.sems, self.arrays, self.token = ops, bases, sems, arrays, token


def _op_refs(op, refs):
    n_src = len(op.inputs())
    return refs[:n_src], (refs[:n_src] if op.aliased else refs[n_src:])


def _start(name, ops, after=(), sibling_only=False):
    per_op = [list(op.inputs()) + ([] if op.aliased else [lax.empty(sd.shape, sd.dtype) for sd in op.out_shapes()])
              for op in ops]
    arrays = [a for group in per_op for a in group]
    bases = [sum(op.n_sems() for op in ops[:k]) for k in range(len(ops))]
    n, after = len(arrays), list(after)

    def body(*refs):
        sems, token = refs[n + len(after)], refs[-1]
        if sibling_only:
            x, y, c = _place()
            barrier = pltpu.get_barrier_semaphore()
            pl.semaphore_signal(barrier, inc=1, device_id=(x, y, 1 - c), device_id_type=MESH)
            pl.semaphore_wait(barrier, 1)
        at = 0
        for op, group, base in zip(ops, per_op, bases):
            starts, _ = op.build(*_op_refs(op, refs[at : at + len(group)]), sems, base)
            for start in starts:
                start()
            at += len(group)
        token[...] = jnp.zeros_like(token)

    res = pl.pallas_call(
        body,
        name=name,
        out_shape=(pltpu.SemaphoreType.DMA((sum(op.n_sems() for op in ops),)),)
        + tuple(pltpu.HBM(a.shape, a.dtype) for a in arrays) + (jax.ShapeDtypeStruct((SUBLANES, LANES), F32),),
        in_specs=(_HBM,) * n + (ANY,) * len(after),
        out_specs=(_SEM,) + (_HBM,) * n + (pl.BlockSpec(memory_space=pltpu.VMEM),),
        input_output_aliases={i: 1 + i for i in range(n)},
        compiler_params=pltpu.CompilerParams(
            has_side_effects=_DATAFLOW, collective_id=SIBLING_BARRIER_ID if sibling_only else None),
    )(*[pltpu.with_memory_space_constraint(a, pltpu.HBM) for a in arrays], *after)
    thru, at, groups = list(res[1 : 1 + n]), 0, []
    for group in per_op:
        groups.append(thru[at : at + len(group)])
        at += len(group)
    return _Pending(list(ops), bases, res[0], groups, res[-1])


def _wait(name, pending, k, after):
    op, arrays = pending.ops[k], pending.arrays[k]
    n = len(arrays)

    def body(*refs):
        _, waits = op.build(*_op_refs(op, refs[:n]), refs[n], pending.bases[k])
        for wait in waits:
            wait()

    return pl.pallas_call(
        body,
        name=name,
        out_shape=tuple(pltpu.HBM(a.shape, a.dtype) for a in arrays),
        in_specs=(_HBM,) * n + (_SEM, ANY),
        out_specs=(_HBM,) * n,
        input_output_aliases={i: i for i in range(n)},
        compiler_params=pltpu.CompilerParams(has_side_effects=_DATAFLOW),
    )(*arrays, pending.sems, after)


def _in_proj(x, g_mix, w_inT_b, b_in, comm=(), after=()):
    T, D = x.shape
    CI = w_inT_b.shape[0]
    tm = _tile(T, 512)

    def body(x_ref, g_ref, w_ref, b_ref, z_ref, xn_ref):
        xv = x_ref[...]
        r = lax.rsqrt(jnp.mean(xv * xv, axis=-1, keepdims=True) + RMS_EPS)
        xn = (xv * r * g_ref[...]).astype(BF16)
        xn_ref[...] = xn
        z_ref[...] = _dot(xn, w_ref[...], NT) + b_ref[...]

    return _call(
        "in_proj",
        body,
        (T // tm,),
        [
            pl.BlockSpec((tm, D), lambda i: (i, 0)),
            pl.BlockSpec((1, D), lambda i: (0, 0)),
            pl.BlockSpec((CI, D), lambda i: (0, 0)),
            pl.BlockSpec((1, CI), lambda i: (0, 0)),
        ],
        [pl.BlockSpec((tm, CI), lambda i: (i, 0)), pl.BlockSpec((tm, D), lambda i: (i, 0))],
        [jax.ShapeDtypeStruct((T, CI), F32), jax.ShapeDtypeStruct((T, D), BF16)],
        (x, g_mix, w_inT_b, b_in),
        comm=comm,
        after=after,
    )


def _fill_shifted(scr):
    n = scr.shape[1] - SUBLANES
    for s in range(1, SUBLANES):
        scr[s, 0:n, :] = scr[0, s : s + n, :]


def _shifted_rows(scr, off, n, cs):
    s = off % SUBLANES
    return scr[s, off - s : off - s + n, cs]


def _pool_mean_minus_token(p_scr, cs, w, cnt, tt):
    tok = p_scr[HALO : HALO + tt, cs]
    s = tok
    for d in range(1, w):
        s = s + p_scr[HALO - d : HALO - d + tt, cs]
    return s / cnt - tok


def _seq_fwd(z, w_dw4, b_dw, ln_g, ln_b, w_pool_b, s_pool, comm=(), after=()):
    T, CI = z.shape
    CC = ln_g.shape[1]
    n_grp, G = w_pool_b.shape[0], w_pool_b.shape[-1]
    KW = w_dw4.shape[1]
    D = CC + n_grp * G
    tt = _tile(T, 512, HALO)
    per = tt // HALO

    def body(zc_ref, zp_ref, wdw_ref, bdw_ref, lng_ref, lnb_ref, wp_ref, sp_ref, y_ref, v_ref, u_scr, p_scr):
        i = pl.program_id(0)
        first = i == 0
        u_prev = zp_ref[:, 0:CC] * _sigmoid(zp_ref[:, CC : 2 * CC])
        u_scr[0, 0:HALO, :] = jnp.where(first, 0.0, u_prev)
        p_scr[0:HALO, :] = jnp.where(first, 0.0, zp_ref[:, 2 * CC :])
        u_scr[0, HALO:, :] = zc_ref[:, 0:CC] * _sigmoid(zc_ref[:, CC : 2 * CC])
        p_scr[HALO:, :] = zc_ref[:, 2 * CC :]
        _fill_shifted(u_scr)

        for j in range(CC // LANES):
            cs = slice(LANES * j, LANES * (j + 1))
            for rb in range(tt // CONV_ROWS):
                acc = jnp.zeros((CONV_ROWS, LANES), F32)
                for k in range(KW):
                    off = HALO - (KW - 1) + k + rb * CONV_ROWS
                    acc = acc + _shifted_rows(u_scr, off, CONV_ROWS, cs) * wdw_ref[j, k]
                v_ref[rb * CONV_ROWS : (rb + 1) * CONV_ROWS, cs] = acc + bdw_ref[:, cs]

        v = v_ref[...]
        mu = jnp.mean(v, axis=-1, keepdims=True)
        d = v - mu
        var = jnp.mean(d * d, axis=-1, keepdims=True)
        ln = d * lax.rsqrt(var + LN_EPS) * lng_ref[...] + lnb_ref[...]
        y_ref[:, 0:CC] = (ln * _sigmoid(ln)).astype(BF16)

        tpos = i * tt + lax.broadcasted_iota(jnp.int32, (tt, 1), 0)
        for gi, w in enumerate(POOL_WINDOWS):
            cs = slice(G * gi, G * (gi + 1))
            cnt = jnp.minimum(tpos + 1, w).astype(F32)
            yi = _pool_mean_minus_token(p_scr, cs, w, cnt, tt)
            q = _dot(yi.astype(BF16), wp_ref[gi], NN)
            y_ref[:, CC + G * gi : CC + G * (gi + 1)] = (q * sp_ref[:, cs]).astype(BF16)

    const2 = lambda i: (0, 0)
    return _call(
        "seq_fwd",
        body,
        (T // tt,),
        [
            pl.BlockSpec((tt, CI), lambda i: (i, 0)),
            pl.BlockSpec((HALO, CI), lambda i: (jnp.maximum(i * per - 1, 0), 0)),
            pl.BlockSpec(w_dw4.shape, lambda i: (0,) * w_dw4.ndim),
            pl.BlockSpec((1, CC), const2),
            pl.BlockSpec((1, CC), const2),
            pl.BlockSpec((1, CC), const2),
            pl.BlockSpec(w_pool_b.shape, lambda i: (0, 0, 0)),
            pl.BlockSpec((1, n_grp * G), const2),
        ],
        [pl.BlockSpec((tt, D), lambda i: (i, 0)), pl.BlockSpec((tt, CC), lambda i: (i, 0))],
        [jax.ShapeDtypeStruct((T, D), BF16), jax.ShapeDtypeStruct((T, CC), F32)],
        (z, z, w_dw4, b_dw, ln_g, ln_b, w_pool_b, s_pool),
        scratch=[pltpu.VMEM((SUBLANES, HALO + tt, CC), F32), pltpu.VMEM((HALO + tt, n_grp * G), F32)],
        comm=comm,
        after=after,
    )


def _out_proj(y_b, x, w_out_b, g_ffn, comm=(), after=()):
    T, D = x.shape
    tm = _tile(T, 512)

    def body(y_ref, x_ref, w_ref, g_ref, h1_ref, hn_ref):
        h1 = x_ref[...] + _dot(y_ref[...], w_ref[...], NN)
        h1_ref[...] = h1
        r = lax.rsqrt(jnp.mean(h1 * h1, axis=-1, keepdims=True) + RMS_EPS)
        hn_ref[...] = (h1 * r * g_ref[...]).astype(BF16)

    row = lambda i: (i, 0)
    return _call(
        "out_proj",
        body,
        (T // tm,),
        [
            pl.BlockSpec((tm, y_b.shape[1]), row),
            pl.BlockSpec((tm, D), row),
            pl.BlockSpec(w_out_b.shape, lambda i: (0, 0)),
            pl.BlockSpec((1, D), lambda i: (0, 0)),
        ],
        [pl.BlockSpec((tm, D), row), pl.BlockSpec((tm, D), row)],
        [jax.ShapeDtypeStruct((T, D), F32), jax.ShapeDtypeStruct((T, D), BF16)],
        (y_b, x, w_out_b, g_ffn),
        comm=comm,
        after=after,
    )


def _hidden_tile(F):
    return _tile(F, 1408, LANES)


def _gate_up(hn_b, wgT_b, wuT_b, comm=(), after=()):
    T, D = hn_b.shape
    F = wgT_b.shape[0]
    tm, tf = _tile(T, 1024), _hidden_tile(F)

    def body(hn_ref, wg_ref, wu_ref, g_ref, u_ref, a_ref):
        hn = hn_ref[...]
        for c0 in range(0, tf, HIDDEN_CHUNK):
            cs = slice(c0, min(c0 + HIDDEN_CHUNK, tf))
            gv = _dot(hn, wg_ref[cs, :], NT)
            uv = _dot(hn, wu_ref[cs, :], NT)
            g_ref[:, cs] = gv.astype(BF16)
            u_ref[:, cs] = uv.astype(BF16)
            a_ref[:, cs] = (gv * _sigmoid(gv) * uv).astype(BF16)

    wspec = pl.BlockSpec((tf, D), lambda j, i: (j, 0))
    ospec = pl.BlockSpec((tm, tf), lambda j, i: (i, j))
    return _call(
        "gate_up",
        body,
        (F // tf, T // tm),
        [pl.BlockSpec((tm, D), lambda j, i: (i, 0)), wspec, wspec],
        [ospec, ospec, ospec],
        [jax.ShapeDtypeStruct((T, F), BF16)] * 3,
        (hn_b, wgT_b, wuT_b),
        comm=comm,
        after=after,
    )


def _down_loss(a_b, wd_b, h1, target, g_final, comm=(), after=()):
    T, D = h1.shape
    F = a_b.shape[1]
    tm = _tile(T, 512)
    nt = T // tm

    def body(a_ref, w_ref, h1_ref, t_ref, g_ref, dh2_ref, dh2b_ref, loss_ref, dg_ref):
        i = pl.program_id(0)
        h2 = h1_ref[...] + _dot(a_ref[...], w_ref[...], NN)
        r = lax.rsqrt(jnp.mean(h2 * h2, axis=-1, keepdims=True) + RMS_EPS)
        g = g_ref[...]
        diff = h2 * r * g - t_ref[...]
        _accumulate(loss_ref, i == 0, jnp.full(loss_ref.shape, jnp.sum(diff * diff) * (0.5 / D), F32))
        dh2, dg_rows = _rms_bwd(h2, g, diff * (1.0 / D))
        dh2_ref[...] = dh2
        dh2b_ref[...] = dh2.astype(BF16)
        _accumulate(dg_ref, i == 0, jnp.sum(dg_rows, axis=0, keepdims=True))

    row = lambda i: (i, 0)
    return _call(
        "down_loss",
        body,
        (nt,),
        [
            pl.BlockSpec((tm, F), row),
            pl.BlockSpec((F, D), lambda i: (0, 0), pipeline_mode=pl.Buffered(1)),
            pl.BlockSpec((tm, D), row),
            pl.BlockSpec((tm, D), row),
            pl.BlockSpec((1, D), lambda i: (0, 0)),
        ],
        [
            pl.BlockSpec((tm, D), row),
            pl.BlockSpec((tm, D), row),
            pl.BlockSpec((1, LANES), lambda i: (0, 0)),
            pl.BlockSpec((1, D), lambda i: (0, 0)),
        ],
        [
            jax.ShapeDtypeStruct((T, D), F32),
            jax.ShapeDtypeStruct((T, D), BF16),
            jax.ShapeDtypeStruct((1, LANES), F32),
            jax.ShapeDtypeStruct((1, D), F32),
        ],
        (a_b, wd_b, h1, target, g_final),
        comm=comm,
        after=after,
    )


def _ffn_bwd_act(dh2_b, wd_b, g_b, u_b, comm=(), after=()):
    T, D = dh2_b.shape
    F = wd_b.shape[0]
    tm, tf = _tile(T, 1024), _hidden_tile(F)

    def body(d_ref, w_ref, g_ref, u_ref, dg_ref, du_ref):
        d = d_ref[...]
        for c0 in range(0, tf, HIDDEN_CHUNK):
            cs = slice(c0, min(c0 + HIDDEN_CHUNK, tf))
            da = _dot(d, w_ref[cs, :], NT)
            gv = g_ref[:, cs].astype(F32)
            uv = u_ref[:, cs].astype(F32)
            sg = _sigmoid(gv)
            silu = gv * sg
            dg_ref[:, cs] = (da * uv * (sg * (1.0 + gv * (1.0 - sg)))).astype(BF16)
            du_ref[:, cs] = (da * silu).astype(BF16)

    aspec = pl.BlockSpec((tm, tf), lambda j, i: (i, j))
    return _call(
        "ffn_bwd_act",
        body,
        (F // tf, T // tm),
        [pl.BlockSpec((tm, D), lambda j, i: (i, 0)), pl.BlockSpec((tf, D), lambda j, i: (j, 0)), aspec, aspec],
        [aspec, aspec],
        [jax.ShapeDtypeStruct((T, F), BF16)] * 2,
        (dh2_b, wd_b, g_b, u_b),
        comm=comm,
        after=after,
    )


def _ffn_bwd_in(dg_b, du_b, wgT_b, wuT_b, h1, dh2, g_ffn, w_out_b, comm=(), after=()):
    T, D = h1.shape
    F = wgT_b.shape[0]
    DM = w_out_b.shape[0]
    tm = _tile(T, 512)

    def body(dg_ref, du_ref, wg_ref, wu_ref, h1_ref, dh2_ref, g_ref, wo_ref, dh1_ref, dh1b_ref, dy_ref, dgf_ref):
        i = pl.program_id(0)
        dhn = _dot(dg_ref[...], wg_ref[...], NN) + _dot(du_ref[...], wu_ref[...], NN)
        dx, dg_rows = _rms_bwd(h1_ref[...], g_ref[...], dhn)
        dh1 = dh2_ref[...] + dx
        dh1b = dh1.astype(BF16)
        dh1_ref[...] = dh1
        dh1b_ref[...] = dh1b
        dy_ref[...] = _dot(dh1b, wo_ref[...], NT)
        _accumulate(dgf_ref, i == 0, jnp.sum(dg_rows, axis=0, keepdims=True))

    row = lambda i: (i, 0)
    const = lambda i: (0, 0)
    return _call(
        "ffn_bwd_in",
        body,
        (T // tm,),
        [
            pl.BlockSpec((tm, F), row),
            pl.BlockSpec((tm, F), row),
            pl.BlockSpec((F, D), const, pipeline_mode=pl.Buffered(1)),
            pl.BlockSpec((F, D), const, pipeline_mode=pl.Buffered(1)),
            pl.BlockSpec((tm, D), row),
            pl.BlockSpec((tm, D), row),
            pl.BlockSpec((1, D), const),
            pl.BlockSpec((DM, D), const, pipeline_mode=pl.Buffered(1)),
        ],
        [pl.BlockSpec((tm, D), row), pl.BlockSpec((tm, D), row), pl.BlockSpec((tm, DM), row), pl.BlockSpec((1, D), const)],
        [
            jax.ShapeDtypeStruct((T, D), F32),
            jax.ShapeDtypeStruct((T, D), BF16),
            jax.ShapeDtypeStruct((T, DM), F32),
            jax.ShapeDtypeStruct((1, D), F32),
        ],
        (dg_b, du_b, wgT_b, wuT_b, h1, dh2, g_ffn, w_out_b),
        comm=comm,
        after=after,
    )


def _seq_bwd(z, dy, v, w_dw4, ln_g, ln_b, w_pool_b, s_pool, comm=(), after=()):
    T, CI = z.shape
    CC = ln_g.shape[1]
    n_grp, G = w_pool_b.shape[0], w_pool_b.shape[-1]
    CP = n_grp * G
    KW = w_dw4.shape[1]
    n_cc = CC // LANES
    D = CC + CP
    tt = _tile(T, 512, HALO)
    per = tt // HALO
    n_tiles = T // tt
    last_halo = T // HALO - 1

    def body(zc_ref, zp_ref, dyc_ref, dyn_ref, vc_ref, vn_ref, wdw_ref, lng_ref, lnb_ref, wp_ref, sp_ref,
             dz_ref, dwdw_ref, dbdw_ref, dlng_ref, dlnb_ref, dwp_ref, dsp_ref, dbin_ref,
             dv_scr, u_scr, p_scr, g_scr, dw_scr):
        i = pl.program_id(0)
        first = i == 0
        last = i == n_tiles - 1
        lng, lnb = lng_ref[...], lnb_ref[...]

        def conv_pre(vv, dyc):
            mu = jnp.mean(vv, axis=-1, keepdims=True)
            d = vv - mu
            rs = lax.rsqrt(jnp.mean(d * d, axis=-1, keepdims=True) + LN_EPS)
            xh = d * rs
            ln = xh * lng + lnb
            sg = _sigmoid(ln)
            dln = dyc * (sg * (1.0 + ln * (1.0 - sg)))
            dxh = dln * lng
            dv = rs * (dxh - jnp.mean(dxh, axis=-1, keepdims=True) - xh * jnp.mean(dxh * xh, axis=-1, keepdims=True))
            return dv, dln, xh

        dv_c, dln_c, xh_c = conv_pre(vc_ref[...], dyc_ref[:, 0:CC])
        dv_scr[0, 0:tt, :] = dv_c
        dv_n, _, _ = conv_pre(vn_ref[...], dyn_ref[:, 0:CC])
        dv_scr[0, tt:, :] = jnp.where(last, 0.0, dv_n)
        _fill_shifted(dv_scr)
        _accumulate(dlng_ref, first, jnp.sum(dln_c * xh_c, axis=0, keepdims=True))
        _accumulate(dlnb_ref, first, jnp.sum(dln_c, axis=0, keepdims=True))
        _accumulate(dbdw_ref, first, jnp.sum(dv_c, axis=0, keepdims=True))

        u_scr[...] = zc_ref[:, 0:CC] * _sigmoid(zc_ref[:, CC : 2 * CC])

        @pl.when(first)
        def _():
            dw_scr[...] = jnp.zeros_like(dw_scr)

        for j in range(n_cc):
            cs = slice(LANES * j, LANES * (j + 1))
            gs = slice(CC + LANES * j, CC + LANES * (j + 1))
            dbin_a = jnp.zeros((1, LANES), F32)
            dbin_g = jnp.zeros((1, LANES), F32)
            for rb in range(tt // CONV_ROWS):
                rows = slice(rb * CONV_ROWS, (rb + 1) * CONV_ROWS)
                u_blk = u_scr[rows, cs]
                du = jnp.zeros((CONV_ROWS, LANES), F32)
                for k in range(KW):
                    off = rb * CONV_ROWS + (KW - 1) - k
                    d = _shifted_rows(dv_scr, off, CONV_ROWS, cs)
                    du = du + d * wdw_ref[j, k]
                    dw_scr[j * HALO + k] += jnp.sum((u_blk * d).reshape(CONV_ROWS // 8, 8, LANES), axis=0)
                a = zc_ref[rows, cs]
                sg = _sigmoid(zc_ref[rows, gs])
                da = du * sg
                dgate = du * a * sg * (1.0 - sg)
                dz_ref[rows, cs] = da.astype(BF16)
                dz_ref[rows, gs] = dgate.astype(BF16)
                dbin_a = dbin_a + jnp.sum(da, axis=0, keepdims=True)
                dbin_g = dbin_g + jnp.sum(dgate, axis=0, keepdims=True)
            _accumulate(dbin_ref.at[:, cs], first, dbin_a)
            _accumulate(dbin_ref.at[:, gs], first, dbin_g)

        @pl.when(last)
        def _():
            dwdw_ref[...] = jnp.sum(dw_scr[...], axis=1).reshape(dwdw_ref.shape)

        p_scr[0:HALO, :] = jnp.where(first, 0.0, zp_ref[:, 2 * CC :])
        p_scr[HALO:, :] = zc_ref[:, 2 * CC :]
        tpos = i * tt + lax.broadcasted_iota(jnp.int32, (tt, 1), 0)
        for gi, w in enumerate(POOL_WINDOWS):
            cs = slice(G * gi, G * (gi + 1))
            ys = slice(CC + G * gi, CC + G * (gi + 1))
            ps = slice(2 * CC + G * gi, 2 * CC + G * (gi + 1))
            cnt = jnp.minimum(tpos + 1, w).astype(F32)
            yib = _pool_mean_minus_token(p_scr, cs, w, cnt, tt).astype(BF16)
            wp = wp_ref[gi]
            sp = sp_ref[:, cs]
            dyp = dyc_ref[:, ys]
            q = _dot(yib, wp, NN)
            _accumulate(dsp_ref.at[:, cs], first, jnp.sum(dyp * q, axis=0, keepdims=True))
            dq_c = (dyp * sp).astype(BF16)
            dq_n = (jnp.where(last, 0.0, dyn_ref[:, ys]) * sp).astype(BF16)
            _accumulate(dwp_ref.at[gi], first, _dot(yib, dq_c, TN))
            dyi_c = _dot(dq_c, wp, NT)
            g_scr[0:tt, cs] = dyi_c / cnt
            g_scr[tt:, cs] = _dot(dq_n, wp, NT) * (1.0 / w)
            dp = -dyi_c
            for d in range(w):
                dp = dp + g_scr[d : d + tt, cs]
            dz_ref[:, ps] = dp.astype(BF16)
            _accumulate(dbin_ref.at[:, ps], first, jnp.sum(dp, axis=0, keepdims=True))

    cur = lambda i: (i, 0)
    prev = lambda i: (jnp.maximum(i * per - 1, 0), 0)
    nxt = lambda i: (jnp.minimum((i + 1) * per, last_halo), 0)
    c2 = lambda i: (0, 0)
    c3 = lambda i: (0, 0, 0)
    return _call(
        "seq_bwd",
        body,
        (n_tiles,),
        [
            pl.BlockSpec((tt, CI), cur),
            pl.BlockSpec((HALO, CI), prev),
            pl.BlockSpec((tt, D), cur),
            pl.BlockSpec((HALO, D), nxt),
            pl.BlockSpec((tt, CC), cur),
            pl.BlockSpec((HALO, CC), nxt),
            pl.BlockSpec(w_dw4.shape, lambda i: (0,) * w_dw4.ndim),
            pl.BlockSpec((1, CC), c2),
            pl.BlockSpec((1, CC), c2),
            pl.BlockSpec(w_pool_b.shape, c3),
            pl.BlockSpec((1, CP), c2),
        ],
        [
            pl.BlockSpec((tt, CI), cur),
            pl.BlockSpec((n_cc, HALO, LANES), c3),
            pl.BlockSpec((1, CC), c2),
            pl.BlockSpec((1, CC), c2),
            pl.BlockSpec((1, CC), c2),
            pl.BlockSpec((n_grp, G, G), c3),
            pl.BlockSpec((1, CP), c2),
            pl.BlockSpec((1, CI), c2),
        ],
        [
            jax.ShapeDtypeStruct((T, CI), BF16),
            jax.ShapeDtypeStruct((n_cc, HALO, LANES), F32),
            jax.ShapeDtypeStruct((1, CC), F32),
            jax.ShapeDtypeStruct((1, CC), F32),
            jax.ShapeDtypeStruct((1, CC), F32),
            jax.ShapeDtypeStruct((n_grp, G, G), F32),
            jax.ShapeDtypeStruct((1, CP), F32),
            jax.ShapeDtypeStruct((1, CI), F32),
        ],
        (z, z, dy, dy, v, v, w_dw4, ln_g, ln_b, w_pool_b, s_pool),
        scratch=[
            pltpu.VMEM((SUBLANES, tt + HALO, CC), F32),
            pltpu.VMEM((tt, CC), F32),
            pltpu.VMEM((HALO + tt, CP), F32),
            pltpu.VMEM((tt + HALO, CP), F32),
            pltpu.VMEM((n_cc * HALO, 8, LANES), F32),
        ],
        comm=comm,
        after=after,
    )


def _in_proj_bwd(dz_b, w_inT_b, x, dh1, g_mix, comm=(), after=()):
    T, D = x.shape
    CI = w_inT_b.shape[0]
    tm = _tile(T, 512)

    def body(dz_ref, w_ref, x_ref, dh1_ref, g_ref, dx_ref, dg_ref):
        i = pl.program_id(0)
        dxn = _dot(dz_ref[...], w_ref[...], NN)
        dx, dg_rows = _rms_bwd(x_ref[...], g_ref[...], dxn)
        dx_ref[...] = dh1_ref[...] + dx
        _accumulate(dg_ref, i == 0, jnp.sum(dg_rows, axis=0, keepdims=True))

    row = lambda i: (i, 0)
    const = lambda i: (0, 0)
    return _call(
        "in_proj_bwd",
        body,
        (T // tm,),
        [
            pl.BlockSpec((tm, CI), row),
            pl.BlockSpec((CI, D), const),
            pl.BlockSpec((tm, D), row),
            pl.BlockSpec((tm, D), row),
            pl.BlockSpec((1, D), const),
        ],
        [pl.BlockSpec((tm, D), row), pl.BlockSpec((1, D), const)],
        [jax.ShapeDtypeStruct((T, D), F32), jax.ShapeDtypeStruct((1, D), F32)],
        (dz_b, w_inT_b, x, dh1, g_mix),
        comm=comm,
        after=after,
    )


def _weight_grad(name, a_b, b_b, comm=(), after=()):
    T, N1 = a_b.shape
    N2 = b_b.shape[1]
    t1 = _tile(N1, 1408, LANES)
    tk = _tile(T, 2048)
    nk = T // tk

    def body(a_ref, b_ref, o_ref, acc):
        k = pl.program_id(1)
        _accumulate(acc, k == 0, _dot(a_ref[...], b_ref[...], TN))

        @pl.when(k == nk - 1)
        def _():
            o_ref[...] = acc[...].astype(BF16)

    (out,), rest = _call(
        name,
        body,
        (N1 // t1, nk),
        [pl.BlockSpec((tk, t1), lambda n, k: (k, n)), pl.BlockSpec((tk, N2), lambda n, k: (k, 0))],
        [pl.BlockSpec((t1, N2), lambda n, k: (n, 0))],
        [jax.ShapeDtypeStruct((N1, N2), BF16)],
        (a_b, b_b),
        scratch=[pltpu.VMEM((t1, N2), F32)],
        comm=comm,
        after=after,
    )
    return out, rest


def _sum_parts(name, full, how, parts, me):
    _, R, C = parts[0].shape
    tr = _tile(R, 512)
    nb = R // tr
    where = [(q, r) for q, p in enumerate(parts) for r in range(p.shape[0])]
    assert len(where) == 3

    def body(me_ref, own_ref, *refs):
        o_ref = refs[-1]
        f = lambda j: refs[where[j][0]][where[j][1]].astype(F32)
        o_ref[...] = (own_ref[...].astype(F32) + f(0)) + (f(1) + f(2))

    own_map = {"rows": lambda i, me_ref: (me_ref[0] * nb + i, 0), "cols": lambda i, me_ref: (i, me_ref[0]),
               "all": lambda i, me_ref: (i, 0)}[how]
    return pl.pallas_call(
        body,
        name=name,
        grid_spec=pltpu.PrefetchScalarGridSpec(
            num_scalar_prefetch=1,
            grid=(nb,),
            in_specs=[pl.BlockSpec((tr, C), own_map)]
            + [pl.BlockSpec((p.shape[0], tr, C), lambda i, me_ref: (0, i, 0)) for p in parts],
            out_specs=pl.BlockSpec((tr, C), lambda i, me_ref: (i, 0)),
        ),
        out_shape=jax.ShapeDtypeStruct((R, C), F32),
        compiler_params=pltpu.CompilerParams(dimension_semantics=("arbitrary",), vmem_limit_bytes=VMEM_LIMIT),
    )(me, full, *parts)


_M_CORR = 1.0 - ADAM_B1**ADAM_STEP
_V_CORR = 1.0 - ADAM_B2**ADAM_STEP


def _adamw_math(w, g, m, v):
    m = ADAM_B1 * m + (1.0 - ADAM_B1) * g
    v = ADAM_B2 * v + (1.0 - ADAM_B2) * (g * g)
    delta = -ADAM_LR * ((m / _M_CORR) / (jnp.sqrt(v / _V_CORR) + ADAM_EPS) + ADAM_WD * w)
    return delta, m, v


def _adamw(name, w, m, v, g_here, g_there, g_transposed=False, comm=()):
    R, C = w.shape
    tr = _tile(R, 256, LANES if g_transposed else 8)

    def body(w_ref, m_ref, v_ref, ga_ref, gb_ref, g_ref, d_ref, nm_ref, nv_ref):
        g = ga_ref[...] + gb_ref[...]
        if g_transposed:
            g = g.T
        g_ref[...] = g
        d_ref[...], nm_ref[...], nv_ref[...] = _adamw_math(w_ref[...], g, m_ref[...], v_ref[...])

    spec = pl.BlockSpec((tr, C), lambda i: (i, 0))
    gspec = pl.BlockSpec((C, tr), lambda i: (0, i)) if g_transposed else spec
    return _call(name, body, (R // tr,), [spec] * 3 + [gspec] * 2, [spec] * 4, [jax.ShapeDtypeStruct((R, C), F32)] * 4,
                 (w, m, v, g_here, g_there), comm=comm)


def _adamw_on_sparsecore(name, w, m, v, g_here, g_there):
    R, C = w.shape
    n_groups = R // SUBLANES
    n_turns = -(-n_groups // SC_TILES)
    n_in, n_out = 5, 4

    def body(w_hbm, m_hbm, v_hbm, ga_hbm, gb_hbm, g_out, d_out, nm_out, nv_out, bufs, sems):
        tile = lax.axis_index("subcore") * SC_CORES + lax.axis_index("sparsecore")
        srcs = (w_hbm, m_hbm, v_hbm, ga_hbm, gb_hbm)
        dsts = (d_out, nm_out, nv_out, g_out)

        def rows(turn):
            return pl.ds((tile + turn * SC_TILES) * SUBLANES, SUBLANES)

        def loads(turn):
            slot = turn % 2
            return [pltpu.make_async_copy(srcs[q].at[rows(turn), :], bufs.at[slot, q], sems.at[slot, q]) for q in range(n_in)]

        def stores(turn):
            slot = turn % 2
            return [pltpu.make_async_copy(bufs.at[slot, q], dsts[q].at[rows(turn), :], sems.at[slot, n_in + q])
                    for q in range(n_out)]

        def when_mine(turn, fn):
            pl.when(tile + turn * SC_TILES < n_groups)(fn)

        def compute(slot):
            wb, mb, vb, gab, gbb = (bufs.at[slot, q] for q in range(n_in))

            @pl.loop(0, SUBLANES)
            def _(r):
                @pl.loop(0, C, step=SC_LANES)
                def _(i):
                    at = (r, pl.ds(i, SC_LANES))
                    g = gab[at] + gbb[at]
                    delta, new_m, new_v = _adamw_math(wb[at], g, mb[at], vb[at])
                    gab[at], wb[at], mb[at], vb[at] = g, delta, new_m, new_v

        def start_loads(turn):
            def fn():
                for cp in loads(turn):
                    cp.start()

            when_mine(turn, fn)

        start_loads(0)
        for turn in range(n_turns):
            def step(turn=turn):
                for cp in loads(turn):
                    cp.wait()
                if turn >= 1:
                    for cp in stores(turn - 1):
                        cp.wait()
                if turn + 1 < n_turns:
                    start_loads(turn + 1)
                compute(turn % 2)
                for cp in stores(turn):
                    cp.start()

            when_mine(turn, step)
        for turn in range(n_turns):
            def drain(turn=turn):
                for cp in stores(turn):
                    cp.wait()

            last_mine = jnp.logical_and(tile + turn * SC_TILES < n_groups, tile + (turn + 1) * SC_TILES >= n_groups)
            pl.when(last_mine)(drain)

    return pl.kernel(
        body,
        name=name,
        out_type=[jax.ShapeDtypeStruct((R, C), F32)] * 4,
        mesh=plsc.VectorSubcoreMesh(core_axis_name="sparsecore", subcore_axis_name="subcore"),
        scratch_types=[pltpu.VMEM((2, n_in, SUBLANES, C), F32), pltpu.SemaphoreType.DMA((2, n_in + n_out))],
        compiler_params=pltpu.CompilerParams(use_tc_tiling_on_sc=True),
    )(w, m, v, g_here, g_there)


class _PackLayout:
    def __init__(self, n_cc, n_grp, G, widths):
        self.dw_rows = (0, HALO)
        self.wp_rows = (HALO, HALO + G)
        self.n_cc, self.n_grp, self.G = n_cc, n_grp, G
        self.vec = {}
        r = HALO + G
        for name, width in widths:
            self.vec[name] = (r, width)
            r += width // PACK_W
        self.rows = -(-r // 8) * 8


def _pack_small(layout, dwdw, dwp, vecs):
    names = list(vecs)

    def body(*refs):
        dw_ref, wp_ref = refs[0], refs[1]
        vec_refs = refs[2 : 2 + len(names)]
        o_ref = refs[-1]
        o_ref[...] = jnp.zeros_like(o_ref)
        for j in range(layout.n_cc):
            o_ref[layout.dw_rows[0] : layout.dw_rows[1], j * LANES : (j + 1) * LANES] = dw_ref[j]
        for i in range(layout.n_grp):
            o_ref[layout.wp_rows[0] : layout.wp_rows[1], i * layout.G : (i + 1) * layout.G] = wp_ref[i]
        for name, ref in zip(names, vec_refs):
            r, width = layout.vec[name]
            for h in range(width // PACK_W):
                o_ref[r + h : r + h + 1, :] = ref[:, h * PACK_W : (h + 1) * PACK_W]

    return pl.pallas_call(
        body,
        name="pack_small",
        out_shape=jax.ShapeDtypeStruct((layout.rows, PACK_W), F32),
    )(dwdw, dwp, *[vecs[k] for k in names])


def _adamw_small(layout, g_here, g_there, w_dw, m_dw, v_dw, w_pool, m_pool, v_pool, vec_w, vec_m, vec_v):
    names = list(vec_w)
    nv = len(names)

    def body(*refs):
        ga_ref, gb_ref = refs[0], refs[1]
        wdw, mdw, vdw, wp, mp, vp = refs[2:8]
        vw, vm, vv = refs[8 : 8 + nv], refs[8 + nv : 8 + 2 * nv], refs[8 + 2 * nv : 8 + 3 * nv]
        outs = refs[8 + 3 * nv :]
        acc = outs[-1]
        acc[...] = ga_ref[...] + gb_ref[...]

        def emit(o, g, w, m, v, idx=()):
            res = (g,) + _adamw_math(w, g, m, v)
            for ref, val in zip(o, res):
                ref[idx] = val

        me = 2 * lax.axis_index("x") + lax.axis_index("y")
        for j in range(layout.n_cc):

            @pl.when(me == j)
            def _(j=j):
                for k in range(wdw.shape[0]):
                    g = acc[layout.dw_rows[0] + k : layout.dw_rows[0] + k + 1, j * LANES : (j + 1) * LANES]
                    emit(outs[0:4], g, wdw[k], mdw[k], vdw[k], idx=k)

        for i in range(layout.n_grp):
            g = acc[layout.wp_rows[0] : layout.wp_rows[1], i * layout.G : (i + 1) * layout.G]
            emit(outs[4:8], g, wp[i], mp[i], vp[i], idx=i)
        for q, name in enumerate(names):
            r, width = layout.vec[name]
            for h in range(width // PACK_W):
                ls = slice(h * PACK_W, (h + 1) * PACK_W)
                g = acc[r + h : r + h + 1, :]
                emit(outs[8 + 4 * q : 12 + 4 * q], g, vw[q][:, ls], vm[q][:, ls], vv[q][:, ls], idx=(slice(None), ls))

    shapes = [w_dw.shape] * 4 + [w_pool.shape] * 4
    for name in names:
        shapes += [vec_w[name].shape] * 4
    return pl.pallas_call(
        body,
        name="adamw_small",
        out_shape=[jax.ShapeDtypeStruct(s, F32) for s in shapes],
        scratch_shapes=[pltpu.VMEM(g_here.shape, F32)],
    )(g_here, g_there, w_dw, m_dw, v_dw, w_pool, m_pool, v_pool,
      *[vec_w[k] for k in names], *[vec_m[k] for k in names], *[vec_v[k] for k in names])


def _allreduce_adamw_row(g_part, w, m, v, loss_part, comm=()):
    D = w.shape[1]
    n_pairs = N_DEV - 1

    def body(g_ref, w_ref, m_ref, v_ref, l_ref, go_ref, d_ref, nm_ref, nv_ref, lo_ref, land_g, land_l, sems):
        x, y, c = _place()
        copies = []
        for q, (src, land) in enumerate(((g_ref, land_g), (l_ref, land_l))):
            for r in range(1, N_DEV):
                fx, fy, fc = (r >> 2) & 1, (r >> 1) & 1, r & 1
                peer = (1 - x if fx else x, 1 - y if fy else y, 1 - c if fc else c)
                cp = _remote(src, land.at[r], sems, 2 * (q * n_pairs + r - 1), peer)
                cp.start()
                copies.append(cp)
        for cp in copies:
            cp.wait()

        def total(src, land):
            row = lambda r: src[...] if r == 0 else land[r]
            return ((row(0) + row(4)) + (row(2) + row(6))) + ((row(1) + row(5)) + (row(3) + row(7)))

        g = total(g_ref, land_g)
        go_ref[...] = g
        d_ref[...], nm_ref[...], nv_ref[...] = _adamw_math(w_ref[...], g, m_ref[...], v_ref[...])
        lo_ref[...] = total(l_ref, land_l)

    vm = pl.BlockSpec(memory_space=pltpu.VMEM)
    return _call(
        "allreduce_adamw_g_mix",
        body,
        (),
        [vm] * 5,
        [vm] * 5,
        [jax.ShapeDtypeStruct((1, D), F32)] * 4 + [jax.ShapeDtypeStruct(loss_part.shape, F32)],
        (g_part, w, m, v, loss_part),
        scratch=[pltpu.VMEM((N_DEV, 1, D), F32), pltpu.VMEM((N_DEV,) + loss_part.shape, F32),
                 pltpu.SemaphoreType.DMA((4 * n_pairs,))],
        comm=comm,
    )


def kernel(x, g_mix, w_in, b_in, w_dw, b_dw, ln_g, ln_b, w_pool, s_pool, w_out, g_ffn, w_gate, w_up, w_down, g_final, loss_target, m_g_mix, m_w_in, m_b_in, m_w_dw, m_b_dw, m_ln_g, m_ln_b, m_w_pool, m_s_pool, m_w_out, m_g_ffn, m_w_gate, m_w_up, m_w_down, m_g_final, v_g_mix, v_w_in, v_b_in, v_w_dw, v_b_dw, v_ln_g, v_ln_b, v_w_pool, v_s_pool, v_w_out, v_g_ffn, v_w_gate, v_w_up, v_w_down, v_g_final):
    x2 = x[0]
    target = loss_target[0]
    T, D = x2.shape
    w_in2, w_out2, w_down2 = w_in[0], w_out[0], w_down[0]
    taps_first = lambda a: jnp.transpose(a, (1, 0, 2))
    w_dw3 = taps_first(w_dw)
    w_gateT, w_upT = w_gate[0].T, w_up[0].T
    CI = w_in2.shape[1] * N_CHIPS
    DM = w_out2.shape[0] * N_CHIPS
    F = w_down2.shape[0] * N_CHIPS
    KW, _, dw_cols = w_dw3.shape
    assert dw_cols == LANES
    n_grp, G = w_pool.shape[1], w_pool.shape[-1]
    w_pool3 = w_pool[0]
    g_final2 = g_final.reshape(1, D)

    me = (2 * lax.axis_index("x") + lax.axis_index("y")).astype(jnp.int32).reshape(1)

    w_inT_b, w_dw4, f_out, f_gate, f_up, f_down = _place_and_gather(
        [(w_in2, "rows", (CI, D), BF16, True, True), (w_dw3, "lead", (N_CHIPS, KW, 1, dw_cols), F32, False, False)],
        [(w, "rows", shape, BF16, False, True)
         for w, shape in ((w_out2, (DM, D)), (w_gateT, (F, D)), (w_upT, (F, D)), (w_down2, (F, D)))])
    w_pool_b = w_pool3.astype(BF16)
    ici = lambda f: _GatherIci([f], ["rows"], [True])
    d2d = lambda f: _GatherD2d([f], ["rows"])
    gather = _start("gather_start", [ici(f_out), ici(f_gate), ici(f_up), ici(f_down)])
    (z, xn_b), _ = _in_proj(x2, g_mix, w_inT_b, b_in, after=[gather.token])
    (f_out,) = _wait("gather_out_wait", gather, 0, xn_b)
    s_out = _start("share_out_start", [d2d(f_out)], sibling_only=True)
    (y_b, v), _ = _seq_fwd(z, w_dw4, b_dw, ln_g, ln_b, w_pool_b, s_pool, after=[s_out.token])
    (w_out_b,) = _wait("share_out_wait", s_out, 0, y_b)
    (f_gate,) = _wait("gather_gate_wait", gather, 1, y_b)
    s_gate = _start("share_gate_start", [d2d(f_gate)], sibling_only=True)
    (h1, hn_b), _ = _out_proj(y_b, x2, w_out_b, g_ffn, after=[s_gate.token])
    (f_up,) = _wait("gather_up_wait", gather, 2, hn_b)
    s_up = _start("share_up_start", [d2d(f_up)], sibling_only=True)
    (wgT_b,) = _wait("share_gate_wait", s_gate, 0, hn_b)
    (wuT_b,) = _wait("share_up_wait", s_up, 0, hn_b)
    (g_b, u_b, a_b), _ = _gate_up(hn_b, wgT_b, wuT_b)
    (f_down,) = _wait("gather_down_wait", gather, 3, a_b)
    s_down = _start("share_down_start", [d2d(f_down)], sibling_only=True)
    (wd_b,) = _wait("share_down_wait", s_down, 0, a_b)
    (dh2, dh2_b, loss_part, d_g_final), _ = _down_loss(a_b, wd_b, h1, target, g_final2)

    gw_down, _ = _weight_grad("grad_w_down", a_b, dh2_b)
    x_down = _start("scatter_down_diag_start", [_Scatter([gw_down], ["rows"], which=(2,))])
    gw_down = x_down.arrays[0][0]
    (dg_b, du_b), (p_down_xy,) = _ffn_bwd_act(
        dh2_b, wd_b, g_b, u_b, comm=[_Scatter([gw_down], ["rows"], which=(0, 1))], after=[x_down.token])
    gw_gateT, _ = _weight_grad("grad_w_gate", dg_b, hn_b)
    gw_upT, _ = _weight_grad("grad_w_up", du_b, hn_b)
    gw_down, p_down_d = _wait("scatter_down_diag_wait", x_down, 0, gw_upT)
    sum_down = _sum_parts("sum_w_down", gw_down, "rows", [p_down_xy, p_down_d], me)
    (dh1, dh1_b, dy, d_g_ffn), (p_gate, oth_down) = _ffn_bwd_in(
        dg_b, du_b, wgT_b, wuT_b, h1, dh2, g_ffn, w_out_b, comm=[_Scatter([gw_gateT], ["rows"]), _Swap([sum_down])])
    gw_out, _ = _weight_grad("grad_w_out", y_b, dh1_b)
    sum_gate = _sum_parts("sum_w_gate", gw_gateT, "rows", [p_gate], me)
    res = {}
    res["w_down"] = _adamw_on_sparsecore("adamw_w_down", w_down2, m_w_down[0], v_w_down[0], sum_down, oth_down)
    (dz_b, d_wdw, d_bdw, d_lng, d_lnb, d_wp, d_sp, d_bin), (p_up, p_out, oth_gate) = _seq_bwd(
        z, dy, v, w_dw4, ln_g, ln_b, w_pool_b, s_pool,
        comm=[_Scatter([gw_upT, gw_out], ["rows", "rows"]), _Swap([sum_gate])])
    vec_grads = {"b_dw": d_bdw, "ln_g": d_lng, "ln_b": d_lnb, "s_pool": d_sp, "g_ffn": d_g_ffn, "g_final": d_g_final, "b_in": d_bin}
    layout = _PackLayout(dw_cols * N_CHIPS // LANES, n_grp, G, [(k, a.shape[1]) for k, a in vec_grads.items()])
    pack = _pack_small(layout, d_wdw, d_wp, vec_grads)
    sum_up = _sum_parts("sum_w_up", gw_upT, "rows", [p_up], me)
    sum_out = _sum_parts("sum_w_out", gw_out, "rows", [p_out], me)
    gw_inT, (p_small, oth_up, oth_out) = _weight_grad(
        "grad_w_in", dz_b, xn_b, comm=[_Scatter([pack], ["all"]), _Swap([sum_up, sum_out])])
    sum_small = _sum_parts("sum_small", pack, "all", [p_small], me)
    res["w_gate"] = _adamw_on_sparsecore("adamw_w_gate", w_gateT, m_w_gate[0].T, v_w_gate[0].T, sum_gate, oth_gate)
    late = _start("late_start", [_Scatter([gw_inT], ["rows"]), _Swap([sum_small])])
    (grad_x, d_g_mix), _ = _in_proj_bwd(dz_b, w_inT_b, x2, dh1, g_mix, after=[late.token])
    gw_inT, p_in = _wait("late_w_in_wait", late, 0, d_g_mix)
    sum_small, oth_small = _wait("late_small_wait", late, 1, d_g_mix)
    res["w_up"] = _adamw_on_sparsecore("adamw_w_up", w_upT, m_w_up[0].T, v_w_up[0].T, sum_up, oth_up)
    res["w_out"] = _adamw_on_sparsecore("adamw_w_out", w_out2, m_w_out[0], v_w_out[0], sum_out, oth_out)
    sum_in = _sum_parts("sum_w_in", gw_inT, "rows", [p_in], me)
    (*res["g_mix"], loss_row), (oth_in,) = _allreduce_adamw_row(
        d_g_mix, g_mix, m_g_mix, v_g_mix, loss_part, comm=[_Swap([sum_in])])
    loss = loss_row[0, 0]
    res["w_in"], _ = _adamw("adamw_w_in", w_in2, m_w_in[0], v_w_in[0], sum_in, oth_in, g_transposed=True)

    vec_w = {"b_dw": b_dw, "ln_g": ln_g, "ln_b": ln_b, "s_pool": s_pool, "g_ffn": g_ffn, "g_final": g_final2, "b_in": b_in}
    vec_m = {"b_dw": m_b_dw, "ln_g": m_ln_g, "ln_b": m_ln_b, "s_pool": m_s_pool, "g_ffn": m_g_ffn,
             "g_final": m_g_final.reshape(1, D), "b_in": m_b_in}
    vec_v = {"b_dw": v_b_dw, "ln_g": v_ln_g, "ln_b": v_ln_b, "s_pool": v_s_pool, "g_ffn": v_g_ffn,
             "g_final": v_g_final.reshape(1, D), "b_in": v_b_in}
    small = _adamw_small(layout, sum_small, oth_small, w_dw3, taps_first(m_w_dw), taps_first(v_w_dw),
                         w_pool3, m_w_pool[0], v_w_pool[0], vec_w, vec_m, vec_v)
    res["w_dw"] = [taps_first(a) for a in small[0:4]]
    res["w_pool"] = [a[None] for a in small[4:8]]
    for q, k in enumerate(vec_w):
        res[k] = list(small[8 + 4 * q : 12 + 4 * q])
    res["g_final"] = [a.reshape(D) for a in res["g_final"]]
    for k in ("w_in", "w_out", "w_down"):
        res[k] = [a[None] for a in res[k]]
    for k in ("w_gate", "w_up"):
        res[k] = [a.T[None] for a in res[k]]

    order = ["g_mix", "w_in", "b_in", "w_dw", "b_dw", "ln_g", "ln_b", "w_pool", "s_pool", "w_out", "g_ffn", "w_gate", "w_up", "w_down", "g_final"]
    outs = [loss, grad_x[None]]
    for q in range(4):
        outs += [res[k][q] for k in order]
    return tuple(outs)
```

```python
import jax
import jax.numpy as jnp
from jax import lax
from jax.experimental import pallas as pl
from jax.experimental.pallas import tpu as pltpu
from jax.experimental.pallas import tpu_sc as plsc

F32 = jnp.float32
BF16 = jnp.bfloat16
MESH = pl.DeviceIdType.MESH
ANY = pl.BlockSpec(memory_space=pl.ANY)

RMS_EPS = 1e-6
LN_EPS = 1e-5
POOL_WINDOWS = (2, 4, 8, 16)
ADAM_LR = 0.001
ADAM_B1 = 0.9
ADAM_B2 = 0.999
ADAM_EPS = 1e-08
ADAM_WD = 0.01
ADAM_STEP = 10

LANES = 128
SUBLANES = 8
BF16_ROWS = 16
HALO = 32
CONV_ROWS = 64
HIDDEN_CHUNK = 512
VMEM_LIMIT = 56 * 1024 * 1024
PACK_W = 512
N_CHIPS = 4
N_DEV = 8
SIBLING_BARRIER_ID = 0
SC_CORES = 2
SC_TILES = 32
SC_LANES = 16


def _tile(n, want, mult=8):
    t = min(n, want)
    while n % t or t % mult:
        t -= 1
    return t


def _sigmoid(x):
    return 1.0 / (1.0 + jnp.exp(-x))


def _dot(a, b, dims):
    return lax.dot_general(a, b, (dims, ((), ())), preferred_element_type=F32)


NN = ((1,), (0,))
NT = ((1,), (1,))
TN = ((0,), (0,))


def _rms_bwd(x, g, dy):
    r = lax.rsqrt(jnp.mean(x * x, axis=-1, keepdims=True) + RMS_EPS)
    xh = x * r
    gy = dy * g
    dx = r * (gy - xh * jnp.mean(gy * xh, axis=-1, keepdims=True))
    return dx, dy * xh


def _accumulate(ref, first, val):
    @pl.when(first)
    def _():
        ref[...] = val

    @pl.when(jnp.logical_not(first))
    def _():
        ref[...] += val


def _place():
    return lax.axis_index("x"), lax.axis_index("y"), lax.axis_index("c")


def _other_chips(x, y):
    return [(1 - x, y), (x, 1 - y), (1 - x, 1 - y)]


def _rows(ref, start, n):
    return ref.at[pl.ds(pl.multiple_of(start, BF16_ROWS), n)]


def _window(ref, how, k, c=None):
    if how == "all":
        return ref
    if how == "lead":
        return ref.at[k]
    assert how == "rows"
    n = ref.shape[0] // N_CHIPS
    if c is None:
        return _rows(ref, k * n, n)
    return _rows(ref, k * n + c * (n // 2), n // 2)


def _remote(src, dst, sems, s, device):
    return pltpu.make_async_remote_copy(
        src_ref=src, dst_ref=dst, send_sem=sems.at[s], recv_sem=sems.at[s + 1], device_id=device, device_id_type=MESH)


class _GatherIci:
    aliased = True

    def __init__(self, fulls, hows, splits, which=(0, 1, 2)):
        self.fulls, self.hows, self.splits, self.which = list(fulls), list(hows), list(splits), tuple(which)

    def inputs(self):
        return self.fulls

    def out_shapes(self):
        return [jax.ShapeDtypeStruct(a.shape, a.dtype) for a in self.fulls]

    def n_sems(self):
        return 6 * len(self.fulls)

    def build(self, ins, outs, sems, base):
        x, y, c = _place()
        me = 2 * x + y
        chips = _other_chips(x, y)
        starts, waits = [], []
        for a, (how, sp) in enumerate(zip(self.hows, self.splits)):
            half = c if sp else None
            mine = _window(outs[a], how, me, half)
            for j in self.which:
                px, py = chips[j]
                s = base + 6 * a + 2 * j
                cp = _remote(mine, mine, sems, s, (px, py, c))
                landing = _remote(mine, _window(outs[a], how, 2 * px + py, half), sems, s, (px, py, c))
                starts.append(cp.start)
                waits += [landing.wait_recv, cp.wait_send]
        return starts, waits


class _GatherD2d:
    aliased = True

    def __init__(self, fulls, hows):
        self.fulls, self.hows = list(fulls), list(hows)

    def inputs(self):
        return self.fulls

    def out_shapes(self):
        return [jax.ShapeDtypeStruct(a.shape, a.dtype) for a in self.fulls]

    def n_sems(self):
        return 6 * len(self.fulls)

    def build(self, ins, outs, sems, base):
        x, y, c = _place()
        starts, waits = [], []
        for a, how in enumerate(self.hows):
            for j, (px, py) in enumerate(_other_chips(x, y)):
                s = base + 6 * a + 2 * j
                got = _window(outs[a], how, 2 * px + py, c)
                cp = _remote(got, got, sems, s, (x, y, 1 - c))
                landing = _remote(got, _window(outs[a], how, 2 * px + py, 1 - c), sems, s, (x, y, 1 - c))
                starts.append(cp.start)
                waits += [landing.wait_recv, cp.wait_send]
        return starts, waits


def _part_shape(a, how):
    if how == "all":
        return a.shape
    assert how == "rows"
    return (a.shape[0] // N_CHIPS, a.shape[1])


class _Scatter:
    aliased = False

    def __init__(self, fulls, hows, which=(0, 1, 2)):
        self.fulls, self.hows, self.which = list(fulls), list(hows), tuple(which)

    def inputs(self):
        return self.fulls

    def out_shapes(self):
        return [jax.ShapeDtypeStruct((len(self.which),) + _part_shape(a, h), a.dtype) for a, h in zip(self.fulls, self.hows)]

    def n_sems(self):
        return 6 * len(self.fulls)

    def build(self, ins, outs, sems, base):
        x, y, c = _place()
        chips = _other_chips(x, y)
        starts, waits = [], []
        for a, how in enumerate(self.hows):
            for slot, j in enumerate(self.which):
                px, py = chips[j]
                cp = _remote(_window(ins[a], how, 2 * px + py), outs[a].at[slot], sems, base + 6 * a + 2 * j, (px, py, c))
                starts.append(cp.start)
                waits += [cp.wait_recv, cp.wait_send]
        return starts, waits


class _Swap:
    aliased = False

    def __init__(self, arrays):
        self.arrays = list(arrays)

    def inputs(self):
        return self.arrays

    def out_shapes(self):
        return [jax.ShapeDtypeStruct(a.shape, a.dtype) for a in self.arrays]

    def n_sems(self):
        return 2 * len(self.arrays)

    def build(self, ins, outs, sems, base):
        x, y, c = _place()
        starts, waits = [], []
        for a in range(len(ins)):
            cp = _remote(ins[a], outs[a], sems, base + 2 * a, (x, y, 1 - c))
            starts.append(cp.start)
            waits += [cp.wait_recv, cp.wait_send]
        return starts, waits


def _call(name, body, grid, in_specs, out_specs, out_shape, args, scratch=(), comm=(), after=()):
    comm, after = list(comm), list(after)
    n_in, n_out, n_scr, n_after = len(args), len(out_shape), len(scratch), len(after)
    c_in = [a for op in comm for a in op.inputs()]
    c_out = [s for op in comm for s in op.out_shapes()]
    n_sems = sum(op.n_sems() for op in comm)
    aliases, i_in, i_out = {}, 0, 0
    for op in comm:
        if op.aliased:
            for q in range(len(op.inputs())):
                aliases[n_in + n_after + i_in + q] = n_out + i_out + q
        i_in, i_out = i_in + len(op.inputs()), i_out + len(op.out_shapes())

    def wrapped(*refs):
        ins = refs[:n_in]
        cin = refs[n_in + n_after : n_in + n_after + len(c_in)]
        o0 = n_in + n_after + len(c_in)
        outs = refs[o0 : o0 + n_out]
        cout = refs[o0 + n_out : o0 + n_out + len(c_out)]
        s0 = o0 + n_out + len(c_out)
        scr = refs[s0 : s0 + n_scr]

        def copies():
            sems = refs[s0 + n_scr]
            starts, waits = [], []
            i_in = i_out = base = 0
            for op in comm:
                ni, no = len(op.inputs()), len(op.out_shapes())
                s, w = op.build(cin[i_in : i_in + ni], cout[i_out : i_out + no], sems, base)
                starts += s
                waits += w
                i_in, i_out, base = i_in + ni, i_out + no, base + op.n_sems()
            return starts, waits

        def run_starts():
            for start in copies()[0]:
                start()

        def run_waits():
            for wait in copies()[1]:
                wait()

        if comm and grid:
            first = last = True
            for d, n in enumerate(grid):
                first = jnp.logical_and(first, pl.program_id(d) == 0)
                last = jnp.logical_and(last, pl.program_id(d) == n - 1)
            pl.when(first)(run_starts)
        elif comm:
            run_starts()
        if body is not None:
            body(*ins, *outs, *scr)
        if comm and grid:
            pl.when(last)(run_waits)
        elif comm:
            run_waits()

    res = pl.pallas_call(
        wrapped,
        name=name,
        grid=grid,
        in_specs=list(in_specs) + [ANY] * (n_after + len(c_in)),
        out_specs=list(out_specs) + [ANY] * len(c_out),
        out_shape=list(out_shape) + c_out,
        scratch_shapes=list(scratch) + ([pltpu.SemaphoreType.DMA((n_sems,))] if comm else []),
        input_output_aliases=aliases,
        compiler_params=pltpu.CompilerParams(dimension_semantics=("arbitrary",) * len(grid), vmem_limit_bytes=VMEM_LIMIT),
    )(*args, *after, *c_in)
    return tuple(res[:n_out]), tuple(res[n_out:])


def _place_and_gather(now, later):
    items = list(now) + list(later)
    n, n_now = len(items), len(now)
    buf_shape = lambda it: it[0].shape[::-1] if it[4] else it[0].shape
    split_now = [a for a in range(n_now) if items[a][5]]

    def body(*refs):
        ins, outs = refs[:n], refs[n : 2 * n]
        stage, bufs = refs[2 * n : 3 * n - n_now], refs[3 * n - n_now : 4 * n - n_now]
        sems = refs[4 * n - n_now]
        x, y, c = _place()
        me = 2 * x + y
        chips = _other_chips(x, y)
        loads = [pltpu.make_async_copy(ins[a], stage[a - n_now], sems.at[a]) for a in range(n_now, n)]
        for ld in loads:
            ld.start()
        pending = []

        def place(a, val):
            _, how, _, dtype, transposed, _ = items[a]
            bufs[a][...] = (val.T if transposed else val).astype(dtype)
            cp = pltpu.make_async_copy(bufs[a], _window(outs[a], how, me), sems.at[n + a])
            cp.start()
            pending.append(cp.wait)

        arrivals = []
        for a in range(n_now):
            place(a, ins[a][...])
            how, split = items[a][1], items[a][5]
            half = c if split else None
            src = _rows(bufs[a], c * (bufs[a].shape[0] // 2), bufs[a].shape[0] // 2) if split else bufs[a]
            for j, (px, py) in enumerate(chips):
                s = 2 * n + 6 * a + 2 * j
                cp = _remote(src, _window(outs[a], how, me, half), sems, s, (px, py, c))
                landing = _remote(src, _window(outs[a], how, 2 * px + py, half), sems, s, (px, py, c))
                cp.start()
                arrivals.append(landing.wait_recv)
                pending.append(cp.wait_send)
        for a in range(n_now, n):
            loads[a - n_now].wait()
            place(a, stage[a - n_now][...])
        for wait in arrivals:
            wait()
        d2d = _GatherD2d([None] * len(split_now), [items[a][1] for a in split_now])
        starts, waits = d2d.build(None, [outs[a] for a in split_now], sems, 2 * n + 6 * n_now)
        for start in starts:
            start()
        for wait in waits + pending:
            wait()

    vm = pl.BlockSpec(memory_space=pltpu.VMEM)
    return pl.pallas_call(
        body,
        name="place_and_gather",
        in_specs=[vm] * n_now + [ANY] * (n - n_now),
        out_specs=[ANY] * n,
        out_shape=[jax.ShapeDtypeStruct(it[2], it[3]) for it in items],
        scratch_shapes=[pltpu.VMEM(it[0].shape, it[0].dtype) for it in later]
        + [pltpu.VMEM(buf_shape(it), it[3]) for it in items]
        + [pltpu.SemaphoreType.DMA((2 * n + 6 * n_now + 6 * len(split_now),))],
        compiler_params=pltpu.CompilerParams(vmem_limit_bytes=VMEM_LIMIT),
    )(*[it[0] for it in items])


_HBM = pl.BlockSpec(memory_space=pltpu.HBM)
_SEM = pl.BlockSpec(memory_space=pltpu.SEMAPHORE)
_DATAFLOW = pltpu.SideEffectType.DATAFLOW_SIDE_EFFECTING


class _Pending:
    def __init__(self, ops, bases, sems, arrays, token):
        self.ops, self.bases, self.sems, self.arrays, self.token = ops, bases, sems, arrays, token


def _op_refs(op, refs):
    n_src = len(op.inputs())
    return refs[:n_src], (refs[:n_src] if op.aliased else refs[n_src:])


def _start(name, ops, sibling_only=False):
    per_op = [list(op.inputs()) + ([] if op.aliased else [lax.empty(sd.shape, sd.dtype) for sd in op.out_shapes()])
              for op in ops]
    arrays = [a for group in per_op for a in group]
    bases = [sum(op.n_sems() for op in ops[:k]) for k in range(len(ops))]
    n = len(arrays)

    def body(*refs):
        sems, token = refs[n], refs[-1]
        if sibling_only:
            x, y, c = _place()
            barrier = pltpu.get_barrier_semaphore()
            pl.semaphore_signal(barrier, inc=1, device_id=(x, y, 1 - c), device_id_type=MESH)
            pl.semaphore_wait(barrier, 1)
        at = 0
        for op, group, base in zip(ops, per_op, bases):
            starts, _ = op.build(*_op_refs(op, refs[at : at + len(group)]), sems, base)
            for start in starts:
                start()
            at += len(group)
        token[...] = jnp.zeros_like(token)

    res = pl.pallas_call(
        body,
        name=name,
        out_shape=(pltpu.SemaphoreType.DMA((sum(op.n_sems() for op in ops),)),)
        + tuple(pltpu.HBM(a.shape, a.dtype) for a in arrays) + (jax.ShapeDtypeStruct((SUBLANES, LANES), F32),),
        in_specs=(_HBM,) * n,
        out_specs=(_SEM,) + (_HBM,) * n + (pl.BlockSpec(memory_space=pltpu.VMEM),),
        input_output_aliases={i: 1 + i for i in range(n)},
        compiler_params=pltpu.CompilerParams(
            has_side_effects=_DATAFLOW, collective_id=SIBLING_BARRIER_ID if sibling_only else None),
    )(*[pltpu.with_memory_space_constraint(a, pltpu.HBM) for a in arrays])
    thru, at, groups = list(res[1 : 1 + n]), 0, []
    for group in per_op:
        groups.append(thru[at : at + len(group)])
        at += len(group)
    return _Pending(list(ops), bases, res[0], groups, res[-1])


def _wait(name, pending, k, after):
    op, arrays = pending.ops[k], pending.arrays[k]
    n = len(arrays)

    def body(*refs):
        _, waits = op.build(*_op_refs(op, refs[:n]), refs[n], pending.bases[k])
        for wait in waits:
            wait()

    return pl.pallas_call(
        body,
        name=name,
        out_shape=tuple(pltpu.HBM(a.shape, a.dtype) for a in arrays),
        in_specs=(_HBM,) * n + (_SEM, ANY),
        out_specs=(_HBM,) * n,
        input_output_aliases={i: i for i in range(n)},
        compiler_params=pltpu.CompilerParams(has_side_effects=_DATAFLOW),
    )(*arrays, pending.sems, after)


def _in_proj(x, g_mix, w_inT_b, b_in, after=()):
    T, D = x.shape
    CI = w_inT_b.shape[0]
    tm = _tile(T, 512)

    def body(x_ref, g_ref, w_ref, b_ref, z_ref, xn_ref):
        xv = x_ref[...]
        r = lax.rsqrt(jnp.mean(xv * xv, axis=-1, keepdims=True) + RMS_EPS)
        xn = (xv * r * g_ref[...]).astype(BF16)
        xn_ref[...] = xn
        z_ref[...] = _dot(xn, w_ref[...], NT) + b_ref[...]

    return _call(
        "in_proj",
        body,
        (T // tm,),
        [
            pl.BlockSpec((tm, D), lambda i: (i, 0)),
            pl.BlockSpec((1, D), lambda i: (0, 0)),
            pl.BlockSpec((CI, D), lambda i: (0, 0)),
            pl.BlockSpec((1, CI), lambda i: (0, 0)),
        ],
        [pl.BlockSpec((tm, CI), lambda i: (i, 0)), pl.BlockSpec((tm, D), lambda i: (i, 0))],
        [jax.ShapeDtypeStruct((T, CI), F32), jax.ShapeDtypeStruct((T, D), BF16)],
        (x, g_mix, w_inT_b, b_in),
        after=after,
    )


def _fill_shifted(scr):
    n = scr.shape[1] - SUBLANES
    for s in range(1, SUBLANES):
        scr[s, 0:n, :] = scr[0, s : s + n, :]


def _shifted_rows(scr, off, n, cs):
    s = off % SUBLANES
    return scr[s, off - s : off - s + n, cs]


def _pool_mean_minus_token(p_scr, cs, w, cnt, tt):
    tok = p_scr[HALO : HALO + tt, cs]
    s = tok
    for d in range(1, w):
        s = s + p_scr[HALO - d : HALO - d + tt, cs]
    return s / cnt - tok


def _seq_fwd(z, w_dw4, b_dw, ln_g, ln_b, w_pool_b, s_pool, after=()):
    T, CI = z.shape
    CC = ln_g.shape[1]
    n_grp, G = w_pool_b.shape[0], w_pool_b.shape[-1]
    KW = w_dw4.shape[1]
    D = CC + n_grp * G
    tt = _tile(T, 512, HALO)
    per = tt // HALO

    def body(zc_ref, zp_ref, wdw_ref, bdw_ref, lng_ref, lnb_ref, wp_ref, sp_ref, y_ref, v_ref, u_scr, p_scr):
        i = pl.program_id(0)
        first = i == 0
        u_prev = zp_ref[:, 0:CC] * _sigmoid(zp_ref[:, CC : 2 * CC])
        u_scr[0, 0:HALO, :] = jnp.where(first, 0.0, u_prev)
        p_scr[0:HALO, :] = jnp.where(first, 0.0, zp_ref[:, 2 * CC :])
        u_scr[0, HALO:, :] = zc_ref[:, 0:CC] * _sigmoid(zc_ref[:, CC : 2 * CC])
        p_scr[HALO:, :] = zc_ref[:, 2 * CC :]
        _fill_shifted(u_scr)

        for j in range(CC // LANES):
            cs = slice(LANES * j, LANES * (j + 1))
            for rb in range(tt // CONV_ROWS):
                acc = jnp.zeros((CONV_ROWS, LANES), F32)
                for k in range(KW):
                    off = HALO - (KW - 1) + k + rb * CONV_ROWS
                    acc = acc + _shifted_rows(u_scr, off, CONV_ROWS, cs) * wdw_ref[j, k]
                v_ref[rb * CONV_ROWS : (rb + 1) * CONV_ROWS, cs] = acc + bdw_ref[:, cs]

        v = v_ref[...]
        mu = jnp.mean(v, axis=-1, keepdims=True)
        d = v - mu
        var = jnp.mean(d * d, axis=-1, keepdims=True)
        ln = d * lax.rsqrt(var + LN_EPS) * lng_ref[...] + lnb_ref[...]
        y_ref[:, 0:CC] = (ln * _sigmoid(ln)).astype(BF16)

        tpos = i * tt + lax.broadcasted_iota(jnp.int32, (tt, 1), 0)
        for gi, w in enumerate(POOL_WINDOWS):
            cs = slice(G * gi, G * (gi + 1))
            cnt = jnp.minimum(tpos + 1, w).astype(F32)
            yi = _pool_mean_minus_token(p_scr, cs, w, cnt, tt)
            q = _dot(yi.astype(BF16), wp_ref[gi], NN)
            y_ref[:, CC + G * gi : CC + G * (gi + 1)] = (q * sp_ref[:, cs]).astype(BF16)

    const2 = lambda i: (0, 0)
    return _call(
        "seq_fwd",
        body,
        (T // tt,),
        [
            pl.BlockSpec((tt, CI), lambda i: (i, 0)),
            pl.BlockSpec((HALO, CI), lambda i: (jnp.maximum(i * per - 1, 0), 0)),
            pl.BlockSpec(w_dw4.shape, lambda i: (0,) * w_dw4.ndim),
            pl.BlockSpec((1, CC), const2),
            pl.BlockSpec((1, CC), const2),
            pl.BlockSpec((1, CC), const2),
            pl.BlockSpec(w_pool_b.shape, lambda i: (0, 0, 0)),
            pl.BlockSpec((1, n_grp * G), const2),
        ],
        [pl.BlockSpec((tt, D), lambda i: (i, 0)), pl.BlockSpec((tt, CC), lambda i: (i, 0))],
        [jax.ShapeDtypeStruct((T, D), BF16), jax.ShapeDtypeStruct((T, CC), F32)],
        (z, z, w_dw4, b_dw, ln_g, ln_b, w_pool_b, s_pool),
        scratch=[pltpu.VMEM((SUBLANES, HALO + tt, CC), F32), pltpu.VMEM((HALO + tt, n_grp * G), F32)],
        after=after,
    )


def _out_proj(y_b, x, w_out_b, g_ffn, after=()):
    T, D = x.shape
    tm = _tile(T, 512)

    def body(y_ref, x_ref, w_ref, g_ref, h1_ref, hn_ref):
        h1 = x_ref[...] + _dot(y_ref[...], w_ref[...], NN)
        h1_ref[...] = h1
        r = lax.rsqrt(jnp.mean(h1 * h1, axis=-1, keepdims=True) + RMS_EPS)
        hn_ref[...] = (h1 * r * g_ref[...]).astype(BF16)

    row = lambda i: (i, 0)
    return _call(
        "out_proj",
        body,
        (T // tm,),
        [
            pl.BlockSpec((tm, y_b.shape[1]), row),
            pl.BlockSpec((tm, D), row),
            pl.BlockSpec(w_out_b.shape, lambda i: (0, 0)),
            pl.BlockSpec((1, D), lambda i: (0, 0)),
        ],
        [pl.BlockSpec((tm, D), row), pl.BlockSpec((tm, D), row)],
        [jax.ShapeDtypeStruct((T, D), F32), jax.ShapeDtypeStruct((T, D), BF16)],
        (y_b, x, w_out_b, g_ffn),
        after=after,
    )


def _hidden_tile(F):
    return _tile(F, 1408, LANES)


def _gate_up(hn_b, wgT_b, wuT_b):
    T, D = hn_b.shape
    F = wgT_b.shape[0]
    tm, tf = _tile(T, 1024), _hidden_tile(F)

    def body(hn_ref, wg_ref, wu_ref, g_ref, u_ref, a_ref):
        hn = hn_ref[...]
        for c0 in range(0, tf, HIDDEN_CHUNK):
            cs = slice(c0, min(c0 + HIDDEN_CHUNK, tf))
            gv = _dot(hn, wg_ref[cs, :], NT)
            uv = _dot(hn, wu_ref[cs, :], NT)
            g_ref[:, cs] = gv.astype(BF16)
            u_ref[:, cs] = uv.astype(BF16)
            a_ref[:, cs] = (gv * _sigmoid(gv) * uv).astype(BF16)

    wspec = pl.BlockSpec((tf, D), lambda j, i: (j, 0))
    ospec = pl.BlockSpec((tm, tf), lambda j, i: (i, j))
    return _call(
        "gate_up",
        body,
        (F // tf, T // tm),
        [pl.BlockSpec((tm, D), lambda j, i: (i, 0)), wspec, wspec],
        [ospec, ospec, ospec],
        [jax.ShapeDtypeStruct((T, F), BF16)] * 3,
        (hn_b, wgT_b, wuT_b),
    )


def _down_loss(a_b, wd_b, h1, target, g_final):
    T, D = h1.shape
    F = a_b.shape[1]
    tm = _tile(T, 512)
    nt = T // tm

    def body(a_ref, w_ref, h1_ref, t_ref, g_ref, dh2_ref, dh2b_ref, loss_ref, dg_ref):
        i = pl.program_id(0)
        h2 = h1_ref[...] + _dot(a_ref[...], w_ref[...], NN)
        r = lax.rsqrt(jnp.mean(h2 * h2, axis=-1, keepdims=True) + RMS_EPS)
        g = g_ref[...]
        diff = h2 * r * g - t_ref[...]
        _accumulate(loss_ref, i == 0, jnp.full(loss_ref.shape, jnp.sum(diff * diff) * (0.5 / D), F32))
        dh2, dg_rows = _rms_bwd(h2, g, diff * (1.0 / D))
        dh2_ref[...] = dh2
        dh2b_ref[...] = dh2.astype(BF16)
        _accumulate(dg_ref, i == 0, jnp.sum(dg_rows, axis=0, keepdims=True))

    row = lambda i: (i, 0)
    return _call(
        "down_loss",
        body,
        (nt,),
        [
            pl.BlockSpec((tm, F), row),
            pl.BlockSpec((F, D), lambda i: (0, 0), pipeline_mode=pl.Buffered(1)),
            pl.BlockSpec((tm, D), row),
            pl.BlockSpec((tm, D), row),
            pl.BlockSpec((1, D), lambda i: (0, 0)),
        ],
        [
            pl.BlockSpec((tm, D), row),
            pl.BlockSpec((tm, D), row),
            pl.BlockSpec((1, LANES), lambda i: (0, 0)),
            pl.BlockSpec((1, D), lambda i: (0, 0)),
        ],
        [
            jax.ShapeDtypeStruct((T, D), F32),
            jax.ShapeDtypeStruct((T, D), BF16),
            jax.ShapeDtypeStruct((1, LANES), F32),
            jax.ShapeDtypeStruct((1, D), F32),
        ],
        (a_b, wd_b, h1, target, g_final),
    )


def _ffn_bwd_act(dh2_b, wd_b, g_b, u_b, comm=()):
    T, D = dh2_b.shape
    F = wd_b.shape[0]
    tm, tf = _tile(T, 1024), _hidden_tile(F)

    def body(d_ref, w_ref, g_ref, u_ref, dg_ref, du_ref):
        d = d_ref[...]
        for c0 in range(0, tf, HIDDEN_CHUNK):
            cs = slice(c0, min(c0 + HIDDEN_CHUNK, tf))
            da = _dot(d, w_ref[cs, :], NT)
            gv = g_ref[:, cs].astype(F32)
            uv = u_ref[:, cs].astype(F32)
            sg = _sigmoid(gv)
            silu = gv * sg
            dg_ref[:, cs] = (da * uv * (sg * (1.0 + gv * (1.0 - sg)))).astype(BF16)
            du_ref[:, cs] = (da * silu).astype(BF16)

    aspec = pl.BlockSpec((tm, tf), lambda j, i: (i, j))
    return _call(
        "ffn_bwd_act",
        body,
        (F // tf, T // tm),
        [pl.BlockSpec((tm, D), lambda j, i: (i, 0)), pl.BlockSpec((tf, D), lambda j, i: (j, 0)), aspec, aspec],
        [aspec, aspec],
        [jax.ShapeDtypeStruct((T, F), BF16)] * 2,
        (dh2_b, wd_b, g_b, u_b),
        comm=comm,
    )


def _ffn_bwd_in(dg_b, du_b, wgT_b, wuT_b, h1, dh2, g_ffn, w_out_b, comm=()):
    T, D = h1.shape
    F = wgT_b.shape[0]
    DM = w_out_b.shape[0]
    tm = _tile(T, 512)

    def body(dg_ref, du_ref, wg_ref, wu_ref, h1_ref, dh2_ref, g_ref, wo_ref, dh1_ref, dh1b_ref, dy_ref, dgf_ref):
        i = pl.program_id(0)
        dhn = _dot(dg_ref[...], wg_ref[...], NN) + _dot(du_ref[...], wu_ref[...], NN)
        dx, dg_rows = _rms_bwd(h1_ref[...], g_ref[...], dhn)
        dh1 = dh2_ref[...] + dx
        dh1b = dh1.astype(BF16)
        dh1_ref[...] = dh1
        dh1b_ref[...] = dh1b
        dy_ref[...] = _dot(dh1b, wo_ref[...], NT)
        _accumulate(dgf_ref, i == 0, jnp.sum(dg_rows, axis=0, keepdims=True))

    row = lambda i: (i, 0)
    const = lambda i: (0, 0)
    return _call(
        "ffn_bwd_in",
        body,
        (T // tm,),
        [
            pl.BlockSpec((tm, F), row),
            pl.BlockSpec((tm, F), row),
            pl.BlockSpec((F, D), const, pipeline_mode=pl.Buffered(1)),
            pl.BlockSpec((F, D), const, pipeline_mode=pl.Buffered(1)),
            pl.BlockSpec((tm, D), row),
            pl.BlockSpec((tm, D), row),
            pl.BlockSpec((1, D), const),
            pl.BlockSpec((DM, D), const, pipeline_mode=pl.Buffered(1)),
        ],
        [pl.BlockSpec((tm, D), row), pl.BlockSpec((tm, D), row), pl.BlockSpec((tm, DM), row), pl.BlockSpec((1, D), const)],
        [
            jax.ShapeDtypeStruct((T, D), F32),
            jax.ShapeDtypeStruct((T, D), BF16),
            jax.ShapeDtypeStruct((T, DM), F32),
            jax.ShapeDtypeStruct((1, D), F32),
        ],
        (dg_b, du_b, wgT_b, wuT_b, h1, dh2, g_ffn, w_out_b),
        comm=comm,
    )


def _seq_bwd(z, dy, v, w_dw4, ln_g, ln_b, w_pool_b, s_pool, comm=()):
    T, CI = z.shape
    CC = ln_g.shape[1]
    n_grp, G = w_pool_b.shape[0], w_pool_b.shape[-1]
    CP = n_grp * G
    KW = w_dw4.shape[1]
    n_cc = CC // LANES
    D = CC + CP
    tt = _tile(T, 512, HALO)
    per = tt // HALO
    n_tiles = T // tt
    last_halo = T // HALO - 1

    def body(zc_ref, zp_ref, dyc_ref, dyn_ref, vc_ref, vn_ref, wdw_ref, lng_ref, lnb_ref, wp_ref, sp_ref,
             dz_ref, dwdw_ref, dbdw_ref, dlng_ref, dlnb_ref, dwp_ref, dsp_ref, dbin_ref,
             dv_scr, u_scr, p_scr, g_scr, dw_scr):
        i = pl.program_id(0)
        first = i == 0
        last = i == n_tiles - 1
        lng, lnb = lng_ref[...], lnb_ref[...]

        def conv_pre(vv, dyc):
            mu = jnp.mean(vv, axis=-1, keepdims=True)
            d = vv - mu
            rs = lax.rsqrt(jnp.mean(d * d, axis=-1, keepdims=True) + LN_EPS)
            xh = d * rs
            ln = xh * lng + lnb
            sg = _sigmoid(ln)
            dln = dyc * (sg * (1.0 + ln * (1.0 - sg)))
            dxh = dln * lng
            dv = rs * (dxh - jnp.mean(dxh, axis=-1, keepdims=True) - xh * jnp.mean(dxh * xh, axis=-1, keepdims=True))
            return dv, dln, xh

        dv_c, dln_c, xh_c = conv_pre(vc_ref[...], dyc_ref[:, 0:CC])
        dv_scr[0, 0:tt, :] = dv_c
        dv_n, _, _ = conv_pre(vn_ref[...], dyn_ref[:, 0:CC])
        dv_scr[0, tt:, :] = jnp.where(last, 0.0, dv_n)
        _fill_shifted(dv_scr)
        _accumulate(dlng_ref, first, jnp.sum(dln_c * xh_c, axis=0, keepdims=True))
        _accumulate(dlnb_ref, first, jnp.sum(dln_c, axis=0, keepdims=True))
        _accumulate(dbdw_ref, first, jnp.sum(dv_c, axis=0, keepdims=True))

        u_scr[...] = zc_ref[:, 0:CC] * _sigmoid(zc_ref[:, CC : 2 * CC])

        @pl.when(first)
        def _():
            dw_scr[...] = jnp.zeros_like(dw_scr)

        for j in range(n_cc):
            cs = slice(LANES * j, LANES * (j + 1))
            gs = slice(CC + LANES * j, CC + LANES * (j + 1))
            dbin_a = jnp.zeros((1, LANES), F32)
            dbin_g = jnp.zeros((1, LANES), F32)
            for rb in range(tt // CONV_ROWS):
                rows = slice(rb * CONV_ROWS, (rb + 1) * CONV_ROWS)
                u_blk = u_scr[rows, cs]
                du = jnp.zeros((CONV_ROWS, LANES), F32)
                for k in range(KW):
                    off = rb * CONV_ROWS + (KW - 1) - k
                    d = _shifted_rows(dv_scr, off, CONV_ROWS, cs)
                    du = du + d * wdw_ref[j, k]
                    dw_scr[j * HALO + k] += jnp.sum((u_blk * d).reshape(CONV_ROWS // 8, 8, LANES), axis=0)
                a = zc_ref[rows, cs]
                sg = _sigmoid(zc_ref[rows, gs])
                da = du * sg
                dgate = du * a * sg * (1.0 - sg)
                dz_ref[rows, cs] = da.astype(BF16)
                dz_ref[rows, gs] = dgate.astype(BF16)
                dbin_a = dbin_a + jnp.sum(da, axis=0, keepdims=True)
                dbin_g = dbin_g + jnp.sum(dgate, axis=0, keepdims=True)
            _accumulate(dbin_ref.at[:, cs], first, dbin_a)
            _accumulate(dbin_ref.at[:, gs], first, dbin_g)

        @pl.when(last)
        def _():
            dwdw_ref[...] = jnp.sum(dw_scr[...], axis=1).reshape(dwdw_ref.shape)

        p_scr[0:HALO, :] = jnp.where(first, 0.0, zp_ref[:, 2 * CC :])
        p_scr[HALO:, :] = zc_ref[:, 2 * CC :]
        tpos = i * tt + lax.broadcasted_iota(jnp.int32, (tt, 1), 0)
        for gi, w in enumerate(POOL_WINDOWS):
            cs = slice(G * gi, G * (gi + 1))
            ys = slice(CC + G * gi, CC + G * (gi + 1))
            ps = slice(2 * CC + G * gi, 2 * CC + G * (gi + 1))
            cnt = jnp.minimum(tpos + 1, w).astype(F32)
            yib = _pool_mean_minus_token(p_scr, cs, w, cnt, tt).astype(BF16)
            wp = wp_ref[gi]
            sp = sp_ref[:, cs]
            dyp = dyc_ref[:, ys]
            q = _dot(yib, wp, NN)
            _accumulate(dsp_ref.at[:, cs], first, jnp.sum(dyp * q, axis=0, keepdims=True))
            dq_c = (dyp * sp).astype(BF16)
            dq_n = (jnp.where(last, 0.0, dyn_ref[:, ys]) * sp).astype(BF16)
            _accumulate(dwp_ref.at[gi], first, _dot(yib, dq_c, TN))
            dyi_c = _dot(dq_c, wp, NT)
            g_scr[0:tt, cs] = dyi_c / cnt
            g_scr[tt:, cs] = _dot(dq_n, wp, NT) * (1.0 / w)
            dp = -dyi_c
            for d in range(w):
                dp = dp + g_scr[d : d + tt, cs]
            dz_ref[:, ps] = dp.astype(BF16)
            _accumulate(dbin_ref.at[:, ps], first, jnp.sum(dp, axis=0, keepdims=True))

    cur = lambda i: (i, 0)
    prev = lambda i: (jnp.maximum(i * per - 1, 0), 0)
    nxt = lambda i: (jnp.minimum((i + 1) * per, last_halo), 0)
    c2 = lambda i: (0, 0)
    c3 = lambda i: (0, 0, 0)
    return _call(
        "seq_bwd",
        body,
        (n_tiles,),
        [
            pl.BlockSpec((tt, CI), cur),
            pl.BlockSpec((HALO, CI), prev),
            pl.BlockSpec((tt, D), cur),
            pl.BlockSpec((HALO, D), nxt),
            pl.BlockSpec((tt, CC), cur),
            pl.BlockSpec((HALO, CC), nxt),
            pl.BlockSpec(w_dw4.shape, lambda i: (0,) * w_dw4.ndim),
            pl.BlockSpec((1, CC), c2),
            pl.BlockSpec((1, CC), c2),
            pl.BlockSpec(w_pool_b.shape, c3),
            pl.BlockSpec((1, CP), c2),
        ],
        [
            pl.BlockSpec((tt, CI), cur),
            pl.BlockSpec((n_cc, HALO, LANES), c3),
            pl.BlockSpec((1, CC), c2),
            pl.BlockSpec((1, CC), c2),
            pl.BlockSpec((1, CC), c2),
            pl.BlockSpec((n_grp, G, G), c3),
            pl.BlockSpec((1, CP), c2),
            pl.BlockSpec((1, CI), c2),
        ],
        [
            jax.ShapeDtypeStruct((T, CI), BF16),
            jax.ShapeDtypeStruct((n_cc, HALO, LANES), F32),
            jax.ShapeDtypeStruct((1, CC), F32),
            jax.ShapeDtypeStruct((1, CC), F32),
            jax.ShapeDtypeStruct((1, CC), F32),
            jax.ShapeDtypeStruct((n_grp, G, G), F32),
            jax.ShapeDtypeStruct((1, CP), F32),
            jax.ShapeDtypeStruct((1, CI), F32),
        ],
        (z, z, dy, dy, v, v, w_dw4, ln_g, ln_b, w_pool_b, s_pool),
        scratch=[
            pltpu.VMEM((SUBLANES, tt + HALO, CC), F32),
            pltpu.VMEM((tt, CC), F32),
            pltpu.VMEM((HALO + tt, CP), F32),
            pltpu.VMEM((tt + HALO, CP), F32),
            pltpu.VMEM((n_cc * HALO, 8, LANES), F32),
        ],
        comm=comm,
    )


def _in_proj_bwd(dz_b, w_inT_b, x, dh1, g_mix, after=()):
    T, D = x.shape
    CI = w_inT_b.shape[0]
    tm = _tile(T, 512)

    def body(dz_ref, w_ref, x_ref, dh1_ref, g_ref, dx_ref, dg_ref):
        i = pl.program_id(0)
        dxn = _dot(dz_ref[...], w_ref[...], NN)
        dx, dg_rows = _rms_bwd(x_ref[...], g_ref[...], dxn)
        dx_ref[...] = dh1_ref[...] + dx
        _accumulate(dg_ref, i == 0, jnp.sum(dg_rows, axis=0, keepdims=True))

    row = lambda i: (i, 0)
    const = lambda i: (0, 0)
    return _call(
        "in_proj_bwd",
        body,
        (T // tm,),
        [
            pl.BlockSpec((tm, CI), row),
            pl.BlockSpec((CI, D), const),
            pl.BlockSpec((tm, D), row),
            pl.BlockSpec((tm, D), row),
            pl.BlockSpec((1, D), const),
        ],
        [pl.BlockSpec((tm, D), row), pl.BlockSpec((1, D), const)],
        [jax.ShapeDtypeStruct((T, D), F32), jax.ShapeDtypeStruct((1, D), F32)],
        (dz_b, w_inT_b, x, dh1, g_mix),
        after=after,
    )


def _weight_grad(name, a_b, b_b, comm=()):
    T, N1 = a_b.shape
    N2 = b_b.shape[1]
    t1 = _tile(N1, 1408, LANES)
    tk = _tile(T, 2048)
    nk = T // tk

    def body(a_ref, b_ref, o_ref, acc):
        k = pl.program_id(1)
        _accumulate(acc, k == 0, _dot(a_ref[...], b_ref[...], TN))

        @pl.when(k == nk - 1)
        def _():
            o_ref[...] = acc[...].astype(BF16)

    (out,), rest = _call(
        name,
        body,
        (N1 // t1, nk),
        [pl.BlockSpec((tk, t1), lambda n, k: (k, n)), pl.BlockSpec((tk, N2), lambda n, k: (k, 0))],
        [pl.BlockSpec((t1, N2), lambda n, k: (n, 0))],
        [jax.ShapeDtypeStruct((N1, N2), BF16)],
        (a_b, b_b),
        scratch=[pltpu.VMEM((t1, N2), F32)],
        comm=comm,
    )
    return out, rest


def _sum_parts(name, full, how, parts, me):
    _, R, C = parts[0].shape
    tr = _tile(R, 512)
    nb = R // tr
    where = [(q, r) for q, p in enumerate(parts) for r in range(p.shape[0])]
    assert len(where) == 3

    def body(me_ref, own_ref, *refs):
        o_ref = refs[-1]
        f = lambda j: refs[where[j][0]][where[j][1]].astype(F32)
        o_ref[...] = (own_ref[...].astype(F32) + f(0)) + (f(1) + f(2))

    own_map = {"rows": lambda i, me_ref: (me_ref[0] * nb + i, 0), "all": lambda i, me_ref: (i, 0)}[how]
    return pl.pallas_call(
        body,
        name=name,
        grid_spec=pltpu.PrefetchScalarGridSpec(
            num_scalar_prefetch=1,
            grid=(nb,),
            in_specs=[pl.BlockSpec((tr, C), own_map)]
            + [pl.BlockSpec((p.shape[0], tr, C), lambda i, me_ref: (0, i, 0)) for p in parts],
            out_specs=pl.BlockSpec((tr, C), lambda i, me_ref: (i, 0)),
        ),
        out_shape=jax.ShapeDtypeStruct((R, C), F32),
        compiler_params=pltpu.CompilerParams(dimension_semantics=("arbitrary",), vmem_limit_bytes=VMEM_LIMIT),
    )(me, full, *parts)


_M_CORR = 1.0 - ADAM_B1**ADAM_STEP
_V_CORR = 1.0 - ADAM_B2**ADAM_STEP


def _adamw_math(w, g, m, v):
    m = ADAM_B1 * m + (1.0 - ADAM_B1) * g
    v = ADAM_B2 * v + (1.0 - ADAM_B2) * (g * g)
    delta = -ADAM_LR * ((m / _M_CORR) / (jnp.sqrt(v / _V_CORR) + ADAM_EPS) + ADAM_WD * w)
    return delta, m, v


def _adamw(name, w, m, v, g_here, g_there, g_transposed=False):
    R, C = w.shape
    tr = _tile(R, 256, LANES if g_transposed else 8)

    def body(w_ref, m_ref, v_ref, ga_ref, gb_ref, g_ref, d_ref, nm_ref, nv_ref):
        g = ga_ref[...] + gb_ref[...]
        if g_transposed:
            g = g.T
        g_ref[...] = g
        d_ref[...], nm_ref[...], nv_ref[...] = _adamw_math(w_ref[...], g, m_ref[...], v_ref[...])

    spec = pl.BlockSpec((tr, C), lambda i: (i, 0))
    gspec = pl.BlockSpec((C, tr), lambda i: (0, i)) if g_transposed else spec
    return _call(name, body, (R // tr,), [spec] * 3 + [gspec] * 2, [spec] * 4, [jax.ShapeDtypeStruct((R, C), F32)] * 4,
                 (w, m, v, g_here, g_there))


def _adamw_on_sparsecore(name, w, m, v, g_here, g_there, after):
    R, C = w.shape
    n_groups = R // SUBLANES
    n_turns = -(-n_groups // SC_TILES)
    n_in, n_out = 5, 4

    def body(w_hbm, m_hbm, v_hbm, ga_hbm, gb_hbm, after_hbm, g_out, d_out, nm_out, nv_out, bufs, sems):
        tile = lax.axis_index("subcore") * SC_CORES + lax.axis_index("sparsecore")
        srcs = (w_hbm, m_hbm, v_hbm, ga_hbm, gb_hbm)
        dsts = (d_out, nm_out, nv_out, g_out)

        def rows(turn):
            return pl.ds((tile + turn * SC_TILES) * SUBLANES, SUBLANES)

        def loads(turn):
            slot = turn % 2
            return [pltpu.make_async_copy(srcs[q].at[rows(turn), :], bufs.at[slot, q], sems.at[slot, q]) for q in range(n_in)]

        def stores(turn):
            slot = turn % 2
            return [pltpu.make_async_copy(bufs.at[slot, q], dsts[q].at[rows(turn), :], sems.at[slot, n_in + q])
                    for q in range(n_out)]

        def when_mine(turn, fn):
            pl.when(tile + turn * SC_TILES < n_groups)(fn)

        def compute(slot):
            wb, mb, vb, gab, gbb = (bufs.at[slot, q] for q in range(n_in))

            @pl.loop(0, SUBLANES)
            def _(r):
                @pl.loop(0, C, step=SC_LANES)
                def _(i):
                    at = (r, pl.ds(i, SC_LANES))
                    g = gab[at] + gbb[at]
                    delta, new_m, new_v = _adamw_math(wb[at], g, mb[at], vb[at])
                    gab[at], wb[at], mb[at], vb[at] = g, delta, new_m, new_v

        def start_loads(turn):
            def fn():
                for cp in loads(turn):
                    cp.start()

            when_mine(turn, fn)

        start_loads(0)
        for turn in range(n_turns):
            def step(turn=turn):
                for cp in loads(turn):
                    cp.wait()
                if turn >= 1:
                    for cp in stores(turn - 1):
                        cp.wait()
                if turn + 1 < n_turns:
                    start_loads(turn + 1)
                compute(turn % 2)
                for cp in stores(turn):
                    cp.start()

            when_mine(turn, step)
        for turn in range(n_turns):
            def drain(turn=turn):
                for cp in stores(turn):
                    cp.wait()

            last_mine = jnp.logical_and(tile + turn * SC_TILES < n_groups, tile + (turn + 1) * SC_TILES >= n_groups)
            pl.when(last_mine)(drain)

    return pl.kernel(
        body,
        name=name,
        out_type=[jax.ShapeDtypeStruct((R, C), F32)] * 4,
        mesh=plsc.VectorSubcoreMesh(core_axis_name="sparsecore", subcore_axis_name="subcore"),
        scratch_types=[pltpu.VMEM((2, n_in, SUBLANES, C), F32), pltpu.SemaphoreType.DMA((2, n_in + n_out))],
        compiler_params=pltpu.CompilerParams(use_tc_tiling_on_sc=True),
    )(w, m, v, g_here, g_there, after)


class _PackLayout:
    def __init__(self, n_cc, n_grp, G, widths):
        self.dw_rows = (0, HALO)
        self.wp_rows = (HALO, HALO + G)
        self.n_cc, self.n_grp, self.G = n_cc, n_grp, G
        self.vec = {}
        r = HALO + G
        for name, width in widths:
            self.vec[name] = (r, width)
            r += width // PACK_W
        self.rows = -(-r // 8) * 8


def _pack_small(layout, dwdw, dwp, vecs):
    names = list(vecs)

    def body(*refs):
        dw_ref, wp_ref = refs[0], refs[1]
        vec_refs = refs[2 : 2 + len(names)]
        o_ref = refs[-1]
        o_ref[...] = jnp.zeros_like(o_ref)
        for j in range(layout.n_cc):
            o_ref[layout.dw_rows[0] : layout.dw_rows[1], j * LANES : (j + 1) * LANES] = dw_ref[j]
        for i in range(layout.n_grp):
            o_ref[layout.wp_rows[0] : layout.wp_rows[1], i * layout.G : (i + 1) * layout.G] = wp_ref[i]
        for name, ref in zip(names, vec_refs):
            r, width = layout.vec[name]
            for h in range(width // PACK_W):
                o_ref[r + h : r + h + 1, :] = ref[:, h * PACK_W : (h + 1) * PACK_W]

    return pl.pallas_call(
        body,
        name="pack_small",
        out_shape=jax.ShapeDtypeStruct((layout.rows, PACK_W), F32),
    )(dwdw, dwp, *[vecs[k] for k in names])


def _adamw_small(layout, g_here, g_there, w_dw, m_dw, v_dw, w_pool, m_pool, v_pool, vec_w, vec_m, vec_v):
    names = list(vec_w)
    nv = len(names)

    def body(*refs):
        ga_ref, gb_ref = refs[0], refs[1]
        wdw, mdw, vdw, wp, mp, vp = refs[2:8]
        vw, vm, vv = refs[8 : 8 + nv], refs[8 + nv : 8 + 2 * nv], refs[8 + 2 * nv : 8 + 3 * nv]
        outs = refs[8 + 3 * nv :]
        acc = outs[-1]
        acc[...] = ga_ref[...] + gb_ref[...]

        def emit(o, g, w, m, v, idx=()):
            res = (g,) + _adamw_math(w, g, m, v)
            for ref, val in zip(o, res):
                ref[idx] = val

        me = 2 * lax.axis_index("x") + lax.axis_index("y")
        for j in range(layout.n_cc):

            @pl.when(me == j)
            def _(j=j):
                for k in range(wdw.shape[0]):
                    g = acc[layout.dw_rows[0] + k : layout.dw_rows[0] + k + 1, j * LANES : (j + 1) * LANES]
                    emit(outs[0:4], g, wdw[k], mdw[k], vdw[k], idx=k)

        for i in range(layout.n_grp):
            g = acc[layout.wp_rows[0] : layout.wp_rows[1], i * layout.G : (i + 1) * layout.G]
            emit(outs[4:8], g, wp[i], mp[i], vp[i], idx=i)
        for q, name in enumerate(names):
            r, width = layout.vec[name]
            for h in range(width // PACK_W):
                ls = slice(h * PACK_W, (h + 1) * PACK_W)
                g = acc[r + h : r + h + 1, :]
                emit(outs[8 + 4 * q : 12 + 4 * q], g, vw[q][:, ls], vm[q][:, ls], vv[q][:, ls], idx=(slice(None), ls))

    shapes = [w_dw.shape] * 4 + [w_pool.shape] * 4
    for name in names:
        shapes += [vec_w[name].shape] * 4
    return pl.pallas_call(
        body,
        name="adamw_small",
        out_shape=[jax.ShapeDtypeStruct(s, F32) for s in shapes],
        scratch_shapes=[pltpu.VMEM(g_here.shape, F32)],
    )(g_here, g_there, w_dw, m_dw, v_dw, w_pool, m_pool, v_pool,
      *[vec_w[k] for k in names], *[vec_m[k] for k in names], *[vec_v[k] for k in names])


def _allreduce_adamw_row(g_part, w, m, v, loss_part, comm=()):
    D = w.shape[1]
    n_pairs = N_DEV - 1

    def body(g_ref, w_ref, m_ref, v_ref, l_ref, go_ref, d_ref, nm_ref, nv_ref, lo_ref, land_g, land_l, sems):
        x, y, c = _place()
        copies = []
        for q, (src, land) in enumerate(((g_ref, land_g), (l_ref, land_l))):
            for r in range(1, N_DEV):
                fx, fy, fc = (r >> 2) & 1, (r >> 1) & 1, r & 1
                peer = (1 - x if fx else x, 1 - y if fy else y, 1 - c if fc else c)
                cp = _remote(src, land.at[r], sems, 2 * (q * n_pairs + r - 1), peer)
                cp.start()
                copies.append(cp)
        for cp in copies:
            cp.wait()

        def total(src, land):
            row = lambda r: src[...] if r == 0 else land[r]
            return ((row(0) + row(4)) + (row(2) + row(6))) + ((row(1) + row(5)) + (row(3) + row(7)))

        g = total(g_ref, land_g)
        go_ref[...] = g
        d_ref[...], nm_ref[...], nv_ref[...] = _adamw_math(w_ref[...], g, m_ref[...], v_ref[...])
        lo_ref[...] = total(l_ref, land_l)

    vm = pl.BlockSpec(memory_space=pltpu.VMEM)
    return _call(
        "allreduce_adamw_g_mix",
        body,
        (),
        [vm] * 5,
        [vm] * 5,
        [jax.ShapeDtypeStruct((1, D), F32)] * 4 + [jax.ShapeDtypeStruct(loss_part.shape, F32)],
        (g_part, w, m, v, loss_part),
        scratch=[pltpu.VMEM((N_DEV, 1, D), F32), pltpu.VMEM((N_DEV,) + loss_part.shape, F32),
                 pltpu.SemaphoreType.DMA((4 * n_pairs,))],
        comm=comm,
    )


def kernel(x, g_mix, w_in, b_in, w_dw, b_dw, ln_g, ln_b, w_pool, s_pool, w_out, g_ffn, w_gate, w_up, w_down, g_final, loss_target, m_g_mix, m_w_in, m_b_in, m_w_dw, m_b_dw, m_ln_g, m_ln_b, m_w_pool, m_s_pool, m_w_out, m_g_ffn, m_w_gate, m_w_up, m_w_down, m_g_final, v_g_mix, v_w_in, v_b_in, v_w_dw, v_b_dw, v_ln_g, v_ln_b, v_w_pool, v_s_pool, v_w_out, v_g_ffn, v_w_gate, v_w_up, v_w_down, v_g_final):
    x2 = x[0]
    target = loss_target[0]
    T, D = x2.shape
    w_in2, w_out2, w_down2 = w_in[0], w_out[0], w_down[0]
    taps_first = lambda a: jnp.transpose(a, (1, 0, 2))
    w_dw3 = taps_first(w_dw)
    w_gateT, w_upT = w_gate[0].T, w_up[0].T
    CI = w_in2.shape[1] * N_CHIPS
    DM = w_out2.shape[0] * N_CHIPS
    F = w_down2.shape[0] * N_CHIPS
    KW, _, dw_cols = w_dw3.shape
    assert dw_cols == LANES
    n_grp, G = w_pool.shape[1], w_pool.shape[-1]
    w_pool3 = w_pool[0]
    g_final2 = g_final.reshape(1, D)

    me = (2 * lax.axis_index("x") + lax.axis_index("y")).astype(jnp.int32).reshape(1)

    w_inT_b, w_dw4, f_out, f_gate, f_up, f_down = _place_and_gather(
        [(w_in2, "rows", (CI, D), BF16, True, True), (w_dw3, "lead", (N_CHIPS, KW, 1, dw_cols), F32, False, False)],
        [(w, "rows", shape, BF16, False, True)
         for w, shape in ((w_out2, (DM, D)), (w_gateT, (F, D)), (w_upT, (F, D)), (w_down2, (F, D)))])
    w_pool_b = w_pool3.astype(BF16)
    ici = lambda f: _GatherIci([f], ["rows"], [True])
    d2d = lambda f: _GatherD2d([f], ["rows"])
    gather = _start("gather_start", [ici(f_out), ici(f_gate), ici(f_up), ici(f_down)])
    (z, xn_b), _ = _in_proj(x2, g_mix, w_inT_b, b_in, after=[gather.token])
    (f_out,) = _wait("gather_out_wait", gather, 0, xn_b)
    s_out = _start("share_out_start", [d2d(f_out)], sibling_only=True)
    (y_b, v), _ = _seq_fwd(z, w_dw4, b_dw, ln_g, ln_b, w_pool_b, s_pool, after=[s_out.token])
    (w_out_b,) = _wait("share_out_wait", s_out, 0, y_b)
    (f_gate,) = _wait("gather_gate_wait", gather, 1, y_b)
    s_gate = _start("share_gate_start", [d2d(f_gate)], sibling_only=True)
    (h1, hn_b), _ = _out_proj(y_b, x2, w_out_b, g_ffn, after=[s_gate.token])
    (f_up,) = _wait("gather_up_wait", gather, 2, hn_b)
    s_up = _start("share_up_start", [d2d(f_up)], sibling_only=True)
    (wgT_b,) = _wait("share_gate_wait", s_gate, 0, hn_b)
    (wuT_b,) = _wait("share_up_wait", s_up, 0, hn_b)
    (g_b, u_b, a_b), _ = _gate_up(hn_b, wgT_b, wuT_b)
    (f_down,) = _wait("gather_down_wait", gather, 3, a_b)
    s_down = _start("share_down_start", [d2d(f_down)], sibling_only=True)
    (wd_b,) = _wait("share_down_wait", s_down, 0, a_b)
    (dh2, dh2_b, loss_part, d_g_final), _ = _down_loss(a_b, wd_b, h1, target, g_final2)

    gw_down, _ = _weight_grad("grad_w_down", a_b, dh2_b)
    (dg_b, du_b), (p_down_xy,) = _ffn_bwd_act(dh2_b, wd_b, g_b, u_b, comm=[_Scatter([gw_down], ["rows"], which=(0, 1))])
    gw_gateT, (p_down_d,) = _weight_grad("grad_w_gate", dg_b, hn_b, comm=[_Scatter([gw_down], ["rows"], which=(2,))])
    gw_upT, _ = _weight_grad("grad_w_up", du_b, hn_b)
    sum_down = _sum_parts("sum_w_down", gw_down, "rows", [p_down_xy, p_down_d], me)
    (dh1, dh1_b, dy, d_g_ffn), (p_gate, oth_down) = _ffn_bwd_in(
        dg_b, du_b, wgT_b, wuT_b, h1, dh2, g_ffn, w_out_b, comm=[_Scatter([gw_gateT], ["rows"]), _Swap([sum_down])])
    gw_out, _ = _weight_grad("grad_w_out", y_b, dh1_b)
    sum_gate = _sum_parts("sum_w_gate", gw_gateT, "rows", [p_gate], me)
    res = {}
    res["w_down"] = _adamw_on_sparsecore("adamw_w_down", w_down2, m_w_down[0], v_w_down[0], sum_down, oth_down, sum_down)
    (dz_b, d_wdw, d_bdw, d_lng, d_lnb, d_wp, d_sp, d_bin), (p_up, p_out, oth_gate) = _seq_bwd(
        z, dy, v, w_dw4, ln_g, ln_b, w_pool_b, s_pool,
        comm=[_Scatter([gw_upT, gw_out], ["rows", "rows"]), _Swap([sum_gate])])
    res["w_gate"] = _adamw_on_sparsecore(
        "adamw_w_gate", w_gateT, m_w_gate[0].T, v_w_gate[0].T, sum_gate, oth_gate, res["w_down"][0])
    vec_grads ={"b_dw": d_bdw, "ln_g": d_lng, "ln_b": d_lnb, "s_pool": d_sp, "g_ffn": d_g_ffn, "g_final": d_g_final, "b_in": d_bin}
    layout = _PackLayout(dw_cols * N_CHIPS // LANES, n_grp, G, [(k, a.shape[1]) for k, a in vec_grads.items()])
    pack = _pack_small(layout, d_wdw, d_wp, vec_grads)
    sum_up = _sum_parts("sum_w_up", gw_upT, "rows", [p_up], me)
    sum_out = _sum_parts("sum_w_out", gw_out, "rows", [p_out], me)
    gw_inT, (p_small, oth_up, oth_out) = _weight_grad(
        "grad_w_in", dz_b, xn_b, comm=[_Scatter([pack], ["all"]), _Swap([sum_up, sum_out])])
    sum_small = _sum_parts("sum_small", pack, "all", [p_small], me)
    late = _start("late_start", [_Scatter([gw_inT], ["rows"]), _Swap([sum_small])])
    (grad_x, d_g_mix), _ = _in_proj_bwd(dz_b, w_inT_b, x2, dh1, g_mix, after=[late.token])
    gw_inT, p_in = _wait("late_w_in_wait", late, 0, d_g_mix)
    sum_small, oth_small = _wait("late_small_wait", late, 1, d_g_mix)
    res["w_up"] = _adamw_on_sparsecore("adamw_w_up", w_upT, m_w_up[0].T, v_w_up[0].T, sum_up, oth_up, res["w_gate"][0])
    res["w_out"] = _adamw_on_sparsecore("adamw_w_out", w_out2, m_w_out[0], v_w_out[0], sum_out, oth_out, res["w_gate"][0])
    sum_in = _sum_parts("sum_w_in", gw_inT, "rows", [p_in], me)
    (*res["g_mix"], loss_row), (oth_in,) = _allreduce_adamw_row(
        d_g_mix, g_mix, m_g_mix, v_g_mix, loss_part, comm=[_Swap([sum_in])])
    loss = loss_row[0, 0]
    res["w_in"], _ = _adamw("adamw_w_in", w_in2, m_w_in[0], v_w_in[0], sum_in, oth_in, g_transposed=True)

    vec_w = {"b_dw": b_dw, "ln_g": ln_g, "ln_b": ln_b, "s_pool": s_pool, "g_ffn": g_ffn, "g_final": g_final2, "b_in": b_in}
    vec_m = {"b_dw": m_b_dw, "ln_g": m_ln_g, "ln_b": m_ln_b, "s_pool": m_s_pool, "g_ffn": m_g_ffn,
             "g_final": m_g_final.reshape(1, D), "b_in": m_b_in}
    vec_v = {"b_dw": v_b_dw, "ln_g": v_ln_g, "ln_b": v_ln_b, "s_pool": v_s_pool, "g_ffn": v_g_ffn,
             "g_final": v_g_final.reshape(1, D), "b_in": v_b_in}
    small = _adamw_small(layout, sum_small, oth_small, w_dw3, taps_first(m_w_dw), taps_first(v_w_dw),
                         w_pool3, m_w_pool[0], v_w_pool[0], vec_w, vec_m, vec_v)
    res["w_dw"] = [taps_first(a) for a in small[0:4]]
    res["w_pool"] = [a[None] for a in small[4:8]]
    for q, k in enumerate(vec_w):
        res[k] = list(small[8 + 4 * q : 12 + 4 * q])
    res["g_final"] = [a.reshape(D) for a in res["g_final"]]
    for k in ("w_in", "w_out", "w_down"):
        res[k] = [a[None] for a in res[k]]
    for k in ("w_gate", "w_up"):
        res[k] = [a.T[None] for a in res[k]]

    order = ["g_mix", "w_in", "b_in", "w_dw", "b_dw", "ln_g", "ln_b", "w_pool", "s_pool", "w_out", "g_ffn", "w_gate", "w_up", "w_down", "g_final"]
    outs = [loss, grad_x[None]]
    for q in range(4):
        outs += [res[k][q] for k in order]
    return tuple(outs)
```

```python
import jax
import jax.numpy as jnp
from jax import lax
from jax.experimental import pallas as pl
from jax.experimental.pallas import tpu as pltpu
from jax.experimental.pallas import tpu_sc as plsc

F32 = jnp.float32
BF16 = jnp.bfloat16
MESH = pl.DeviceIdType.MESH
ANY = pl.BlockSpec(memory_space=pl.ANY)

RMS_EPS = 1e-6
LN_EPS = 1e-5
POOL_WINDOWS = (2, 4, 8, 16)
ADAM_LR = 0.001
ADAM_B1 = 0.9
ADAM_B2 = 0.999
ADAM_EPS = 1e-08
ADAM_WD = 0.01
ADAM_STEP = 10

LANES = 128
SUBLANES = 8
BF16_ROWS = 16
HALO = 32
CONV_ROWS = 64
HIDDEN_CHUNK = 512
VMEM_LIMIT = 56 * 1024 * 1024
PACK_W = 512
N_CHIPS = 4
N_DEV = 8
SIBLING_BARRIER_ID = 0
SC_CORES = 2
SC_TILES = 32
SC_LANES = 16


def _tile(n, want, mult=8):
    t = min(n, want)
    while n % t or t % mult:
        t -= 1
    return t


def _sigmoid(x):
    return 1.0 / (1.0 + jnp.exp(-x))


def _dot(a, b, dims):
    return lax.dot_general(a, b, (dims, ((), ())), preferred_element_type=F32)


NN = ((1,), (0,))
NT = ((1,), (1,))
TN = ((0,), (0,))


def _rms_bwd(x, g, dy):
    r = lax.rsqrt(jnp.mean(x * x, axis=-1, keepdims=True) + RMS_EPS)
    xh = x * r
    gy = dy * g
    dx = r * (gy - xh * jnp.mean(gy * xh, axis=-1, keepdims=True))
    return dx, dy * xh


def _accumulate(ref, first, val):
    @pl.when(first)
    def _():
        ref[...] = val

    @pl.when(jnp.logical_not(first))
    def _():
        ref[...] += val


def _place():
    return lax.axis_index("x"), lax.axis_index("y"), lax.axis_index("c")


def _other_chips(x, y):
    return [(1 - x, y), (x, 1 - y), (1 - x, 1 - y)]


def _rows(ref, start, n):
    return ref.at[pl.ds(pl.multiple_of(start, BF16_ROWS), n)]


def _window(ref, how, k, c=None):
    if how == "all":
        return ref
    if how == "lead":
        return ref.at[k]
    assert how == "rows"
    n = ref.shape[0] // N_CHIPS
    if c is None:
        return _rows(ref, k * n, n)
    return _rows(ref, k * n + c * (n // 2), n // 2)


def _remote(src, dst, sems, s, device):
    return pltpu.make_async_remote_copy(
        src_ref=src, dst_ref=dst, send_sem=sems.at[s], recv_sem=sems.at[s + 1], device_id=device, device_id_type=MESH)


class _GatherIci:
    aliased = True

    def __init__(self, fulls, hows, splits, which=(0, 1, 2)):
        self.fulls, self.hows, self.splits, self.which = list(fulls), list(hows), list(splits), tuple(which)

    def inputs(self):
        return self.fulls

    def out_shapes(self):
        return [jax.ShapeDtypeStruct(a.shape, a.dtype) for a in self.fulls]

    def n_sems(self):
        return 6 * len(self.fulls)

    def build(self, ins, outs, sems, base):
        x, y, c = _place()
        me = 2 * x + y
        chips = _other_chips(x, y)
        starts, waits = [], []
        for a, (how, sp) in enumerate(zip(self.hows, self.splits)):
            half = c if sp else None
            mine = _window(outs[a], how, me, half)
            for j in self.which:
                px, py = chips[j]
                s = base + 6 * a + 2 * j
                cp = _remote(mine, mine, sems, s, (px, py, c))
                landing = _remote(mine, _window(outs[a], how, 2 * px + py, half), sems, s, (px, py, c))
                starts.append(cp.start)
                waits += [landing.wait_recv, cp.wait_send]
        return starts, waits


class _GatherD2d:
    aliased = True

    def __init__(self, fulls, hows):
        self.fulls, self.hows = list(fulls), list(hows)

    def inputs(self):
        return self.fulls

    def out_shapes(self):
        return [jax.ShapeDtypeStruct(a.shape, a.dtype) for a in self.fulls]

    def n_sems(self):
        return 6 * len(self.fulls)

    def build(self, ins, outs, sems, base):
        x, y, c = _place()
        starts, waits = [], []
        for a, how in enumerate(self.hows):
            for j, (px, py) in enumerate(_other_chips(x, y)):
                s = base + 6 * a + 2 * j
                got = _window(outs[a], how, 2 * px + py, c)
                cp = _remote(got, got, sems, s, (x, y, 1 - c))
                landing = _remote(got, _window(outs[a], how, 2 * px + py, 1 - c), sems, s, (x, y, 1 - c))
                starts.append(cp.start)
                waits += [landing.wait_recv, cp.wait_send]
        return starts, waits


def _part_shape(a, how):
    if how == "all":
        return a.shape
    assert how == "rows"
    return (a.shape[0] // N_CHIPS, a.shape[1])


class _Scatter:
    aliased = False

    def __init__(self, fulls, hows, which=(0, 1, 2)):
        self.fulls, self.hows, self.which = list(fulls), list(hows), tuple(which)

    def inputs(self):
        return self.fulls

    def out_shapes(self):
        return [jax.ShapeDtypeStruct((len(self.which),) + _part_shape(a, h), a.dtype) for a, h in zip(self.fulls, self.hows)]

    def n_sems(self):
        return 6 * len(self.fulls)

    def build(self, ins, outs, sems, base):
        x, y, c = _place()
        chips = _other_chips(x, y)
        starts, waits = [], []
        for a, how in enumerate(self.hows):
            for slot, j in enumerate(self.which):
                px, py = chips[j]
                cp = _remote(_window(ins[a], how, 2 * px + py), outs[a].at[slot], sems, base + 6 * a + 2 * j, (px, py, c))
                starts.append(cp.start)
                waits += [cp.wait_recv, cp.wait_send]
        return starts, waits


class _Swap:
    aliased = False

    def __init__(self, arrays):
        self.arrays = list(arrays)

    def inputs(self):
        return self.arrays

    def out_shapes(self):
        return [jax.ShapeDtypeStruct(a.shape, a.dtype) for a in self.arrays]

    def n_sems(self):
        return 2 * len(self.arrays)

    def build(self, ins, outs, sems, base):
        x, y, c = _place()
        starts, waits = [], []
        for a in range(len(ins)):
            cp = _remote(ins[a], outs[a], sems, base + 2 * a, (x, y, 1 - c))
            starts.append(cp.start)
            waits += [cp.wait_recv, cp.wait_send]
        return starts, waits


def _call(name, body, grid, in_specs, out_specs, out_shape, args, scratch=(), comm=(), after=()):
    comm, after = list(comm), list(after)
    n_in, n_out, n_scr, n_after = len(args), len(out_shape), len(scratch), len(after)
    c_in = [a for op in comm for a in op.inputs()]
    c_out = [s for op in comm for s in op.out_shapes()]
    n_sems = sum(op.n_sems() for op in comm)
    aliases, i_in, i_out = {}, 0, 0
    for op in comm:
        if op.aliased:
            for q in range(len(op.inputs())):
                aliases[n_in + n_after + i_in + q] = n_out + i_out + q
        i_in, i_out = i_in + len(op.inputs()), i_out + len(op.out_shapes())

    def wrapped(*refs):
        ins = refs[:n_in]
        cin = refs[n_in + n_after : n_in + n_after + len(c_in)]
        o0 = n_in + n_after + len(c_in)
        outs = refs[o0 : o0 + n_out]
        cout = refs[o0 + n_out : o0 + n_out + len(c_out)]
        s0 = o0 + n_out + len(c_out)
        scr = refs[s0 : s0 + n_scr]

        def copies():
            sems = refs[s0 + n_scr]
            starts, waits = [], []
            i_in = i_out = base = 0
            for op in comm:
                ni, no = len(op.inputs()), len(op.out_shapes())
                s, w = op.build(cin[i_in : i_in + ni], cout[i_out : i_out + no], sems, base)
                starts += s
                waits += w
                i_in, i_out, base = i_in + ni, i_out + no, base + op.n_sems()
            return starts, waits

        def run_starts():
            for start in copies()[0]:
                start()

        def run_waits():
            for wait in copies()[1]:
                wait()

        if comm and grid:
            first = last = True
            for d, n in enumerate(grid):
                first = jnp.logical_and(first, pl.program_id(d) == 0)
                last = jnp.logical_and(last, pl.program_id(d) == n - 1)
            pl.when(first)(run_starts)
        elif comm:
            run_starts()
        if body is not None:
            body(*ins, *outs, *scr)
        if comm and grid:
            pl.when(last)(run_waits)
        elif comm:
            run_waits()

    res = pl.pallas_call(
        wrapped,
        name=name,
        grid=grid,
        in_specs=list(in_specs) + [ANY] * (n_after + len(c_in)),
        out_specs=list(out_specs) + [ANY] * len(c_out),
        out_shape=list(out_shape) + c_out,
        scratch_shapes=list(scratch) + ([pltpu.SemaphoreType.DMA((n_sems,))] if comm else []),
        input_output_aliases=aliases,
        compiler_params=pltpu.CompilerParams(dimension_semantics=("arbitrary",) * len(grid), vmem_limit_bytes=VMEM_LIMIT),
    )(*args, *after, *c_in)
    return tuple(res[:n_out]), tuple(res[n_out:])


def _place_and_gather(now, later):
    items = list(now) + list(later)
    n, n_now = len(items), len(now)
    buf_shape = lambda it: it[0].shape[::-1] if it[4] else it[0].shape
    split_now = [a for a in range(n_now) if items[a][5]]

    def body(*refs):
        ins, outs = refs[:n], refs[n : 2 * n]
        stage, bufs = refs[2 * n : 3 * n - n_now], refs[3 * n - n_now : 4 * n - n_now]
        sems = refs[4 * n - n_now]
        x, y, c = _place()
        me = 2 * x + y
        chips = _other_chips(x, y)
        loads = [pltpu.make_async_copy(ins[a], stage[a - n_now], sems.at[a]) for a in range(n_now, n)]
        for ld in loads:
            ld.start()
        pending = []

        def place(a, val):
            _, how, _, dtype, transposed, _ = items[a]
            bufs[a][...] = (val.T if transposed else val).astype(dtype)
            cp = pltpu.make_async_copy(bufs[a], _window(outs[a], how, me), sems.at[n + a])
            cp.start()
            pending.append(cp.wait)

        arrivals = []
        for a in range(n_now):
            place(a, ins[a][...])
            how, split = items[a][1], items[a][5]
            half = c if split else None
            src = _rows(bufs[a], c * (bufs[a].shape[0] // 2), bufs[a].shape[0] // 2) if split else bufs[a]
            for j, (px, py) in enumerate(chips):
                s = 2 * n + 6 * a + 2 * j
                cp = _remote(src, _window(outs[a], how, me, half), sems, s, (px, py, c))
                landing = _remote(src, _window(outs[a], how, 2 * px + py, half), sems, s, (px, py, c))
                cp.start()
                arrivals.append(landing.wait_recv)
                pending.append(cp.wait_send)
        for a in range(n_now, n):
            loads[a - n_now].wait()
            place(a, stage[a - n_now][...])
        for wait in arrivals:
            wait()
        d2d = _GatherD2d([None] * len(split_now), [items[a][1] for a in split_now])
        starts, waits = d2d.build(None, [outs[a] for a in split_now], sems, 2 * n + 6 * n_now)
        for start in starts:
            start()
        for wait in waits + pending:
            wait()

    vm = pl.BlockSpec(memory_space=pltpu.VMEM)
    return pl.pallas_call(
        body,
        name="place_and_gather",
        in_specs=[vm] * n_now + [ANY] * (n - n_now),
        out_specs=[ANY] * n,
        out_shape=[jax.ShapeDtypeStruct(it[2], it[3]) for it in items],
        scratch_shapes=[pltpu.VMEM(it[0].shape, it[0].dtype) for it in later]
        + [pltpu.VMEM(buf_shape(it), it[3]) for it in items]
        + [pltpu.SemaphoreType.DMA((2 * n + 6 * n_now + 6 * len(split_now),))],
        compiler_params=pltpu.CompilerParams(vmem_limit_bytes=VMEM_LIMIT),
    )(*[it[0] for it in items])


_HBM = pl.BlockSpec(memory_space=pltpu.HBM)
_SEM = pl.BlockSpec(memory_space=pltpu.SEMAPHORE)
_DATAFLOW = pltpu.SideEffectType.DATAFLOW_SIDE_EFFECTING


class _Pending:
    def __init__(self, ops, bases, sems, arrays, token):
        self.ops, self.bases, self.sems, self.arrays, self.token = ops, bases, sems, arrays, token


def _op_refs(op, refs):
    n_src = len(op.inputs())
    return refs[:n_src], (refs[:n_src] if op.aliased else refs[n_src:])


def _start(name, ops, sibling_only=False):
    per_op = [list(op.inputs()) + ([] if op.aliased else [lax.empty(sd.shape, sd.dtype) for sd in op.out_shapes()])
              for op in ops]
    arrays = [a for group in per_op for a in group]
    bases = [sum(op.n_sems() for op in ops[:k]) for k in range(len(ops))]
    n = len(arrays)

    def body(*refs):
        sems, token = refs[n], refs[-1]
        if sibling_only:
            x, y, c = _place()
            barrier = pltpu.get_barrier_semaphore()
            pl.semaphore_signal(barrier, inc=1, device_id=(x, y, 1 - c), device_id_type=MESH)
            pl.semaphore_wait(barrier, 1)
        at = 0
        for op, group, base in zip(ops, per_op, bases):
            starts, _ = op.build(*_op_refs(op, refs[at : at + len(group)]), sems, base)
            for start in starts:
                start()
            at += len(group)
        token[...] = jnp.zeros_like(token)

    res = pl.pallas_call(
        body,
        name=name,
        out_shape=(pltpu.SemaphoreType.DMA((sum(op.n_sems() for op in ops),)),)
        + tuple(pltpu.HBM(a.shape, a.dtype) for a in arrays) + (jax.ShapeDtypeStruct((SUBLANES, LANES), F32),),
        in_specs=(_HBM,) * n,
        out_specs=(_SEM,) + (_HBM,) * n + (pl.BlockSpec(memory_space=pltpu.VMEM),),
        input_output_aliases={i: 1 + i for i in range(n)},
        compiler_params=pltpu.CompilerParams(
            has_side_effects=_DATAFLOW, collective_id=SIBLING_BARRIER_ID if sibling_only else None),
    )(*[pltpu.with_memory_space_constraint(a, pltpu.HBM) for a in arrays])
    thru, at, groups = list(res[1 : 1 + n]), 0, []
    for group in per_op:
        groups.append(thru[at : at + len(group)])
        at += len(group)
    return _Pending(list(ops), bases, res[0], groups, res[-1])


def _wait(name, pending, k, after):
    op, arrays = pending.ops[k], pending.arrays[k]
    n = len(arrays)

    def body(*refs):
        _, waits = op.build(*_op_refs(op, refs[:n]), refs[n], pending.bases[k])
        for wait in waits:
            wait()

    return pl.pallas_call(
        body,
        name=name,
        out_shape=tuple(pltpu.HBM(a.shape, a.dtype) for a in arrays),
        in_specs=(_HBM,) * n + (_SEM, ANY),
        out_specs=(_HBM,) * n,
        input_output_aliases={i: i for i in range(n)},
        compiler_params=pltpu.CompilerParams(has_side_effects=_DATAFLOW),
    )(*arrays, pending.sems, after)


def _in_proj(x, g_mix, w_inT_b, b_in, after=()):
    T, D = x.shape
    CI = w_inT_b.shape[0]
    tm = _tile(T, 512)

    def body(x_ref, g_ref, w_ref, b_ref, z_ref, xn_ref):
        xv = x_ref[...]
        r = lax.rsqrt(jnp.mean(xv * xv, axis=-1, keepdims=True) + RMS_EPS)
        xn = (xv * r * g_ref[...]).astype(BF16)
        xn_ref[...] = xn
        z_ref[...] = _dot(xn, w_ref[...], NT) + b_ref[...]

    return _call(
        "in_proj",
        body,
        (T // tm,),
        [
            pl.BlockSpec((tm, D), lambda i: (i, 0)),
            pl.BlockSpec((1, D), lambda i: (0, 0)),
            pl.BlockSpec((CI, D), lambda i: (0, 0)),
            pl.BlockSpec((1, CI), lambda i: (0, 0)),
        ],
        [pl.BlockSpec((tm, CI), lambda i: (i, 0)), pl.BlockSpec((tm, D), lambda i: (i, 0))],
        [jax.ShapeDtypeStruct((T, CI), F32), jax.ShapeDtypeStruct((T, D), BF16)],
        (x, g_mix, w_inT_b, b_in),
        after=after,
    )


def _fill_shifted(scr):
    n = scr.shape[1] - SUBLANES
    for s in range(1, SUBLANES):
        scr[s, 0:n, :] = scr[0, s : s + n, :]


def _shifted_rows(scr, off, n, cs):
    s = off % SUBLANES
    return scr[s, off - s : off - s + n, cs]


def _pool_mean_minus_token(p_scr, cs, w, cnt, tt):
    tok = p_scr[HALO : HALO + tt, cs]
    s = tok
    for d in range(1, w):
        s = s + p_scr[HALO - d : HALO - d + tt, cs]
    return s / cnt - tok


def _seq_fwd(z, w_dw4, b_dw, ln_g, ln_b, w_pool_b, s_pool, after=()):
    T, CI = z.shape
    CC = ln_g.shape[1]
    n_grp, G = w_pool_b.shape[0], w_pool_b.shape[-1]
    KW = w_dw4.shape[1]
    D = CC + n_grp * G
    tt = _tile(T, 512, HALO)
    per = tt // HALO

    def body(zc_ref, zp_ref, wdw_ref, bdw_ref, lng_ref, lnb_ref, wp_ref, sp_ref, y_ref, v_ref, u_scr, p_scr):
        i = pl.program_id(0)
        first = i == 0
        u_prev = zp_ref[:, 0:CC] * _sigmoid(zp_ref[:, CC : 2 * CC])
        u_scr[0, 0:HALO, :] = jnp.where(first, 0.0, u_prev)
        p_scr[0:HALO, :] = jnp.where(first, 0.0, zp_ref[:, 2 * CC :])
        u_scr[0, HALO:, :] = zc_ref[:, 0:CC] * _sigmoid(zc_ref[:, CC : 2 * CC])
        p_scr[HALO:, :] = zc_ref[:, 2 * CC :]
        _fill_shifted(u_scr)

        for j in range(CC // LANES):
            cs = slice(LANES * j, LANES * (j + 1))
            for rb in range(tt // CONV_ROWS):
                acc = jnp.zeros((CONV_ROWS, LANES), F32)
                for k in range(KW):
                    off = HALO - (KW - 1) + k + rb * CONV_ROWS
                    acc = acc + _shifted_rows(u_scr, off, CONV_ROWS, cs) * wdw_ref[j, k]
                v_ref[rb * CONV_ROWS : (rb + 1) * CONV_ROWS, cs] = acc + bdw_ref[:, cs]

        v = v_ref[...]
        mu = jnp.mean(v, axis=-1, keepdims=True)
        d = v - mu
        var = jnp.mean(d * d, axis=-1, keepdims=True)
        ln = d * lax.rsqrt(var + LN_EPS) * lng_ref[...] + lnb_ref[...]
        y_ref[:, 0:CC] = (ln * _sigmoid(ln)).astype(BF16)

        tpos = i * tt + lax.broadcasted_iota(jnp.int32, (tt, 1), 0)
        for gi, w in enumerate(POOL_WINDOWS):
            cs = slice(G * gi, G * (gi + 1))
            cnt = jnp.minimum(tpos + 1, w).astype(F32)
            yi = _pool_mean_minus_token(p_scr, cs, w, cnt, tt)
            q = _dot(yi.astype(BF16), wp_ref[gi], NN)
            y_ref[:, CC + G * gi : CC + G * (gi + 1)] = (q * sp_ref[:, cs]).astype(BF16)

    const2 = lambda i: (0, 0)
    return _call(
        "seq_fwd",
        body,
        (T // tt,),
        [
            pl.BlockSpec((tt, CI), lambda i: (i, 0)),
            pl.BlockSpec((HALO, CI), lambda i: (jnp.maximum(i * per - 1, 0), 0)),
            pl.BlockSpec(w_dw4.shape, lambda i: (0,) * w_dw4.ndim),
            pl.BlockSpec((1, CC), const2),
            pl.BlockSpec((1, CC), const2),
            pl.BlockSpec((1, CC), const2),
            pl.BlockSpec(w_pool_b.shape, lambda i: (0, 0, 0)),
            pl.BlockSpec((1, n_grp * G), const2),
        ],
        [pl.BlockSpec((tt, D), lambda i: (i, 0)), pl.BlockSpec((tt, CC), lambda i: (i, 0))],
        [jax.ShapeDtypeStruct((T, D), BF16), jax.ShapeDtypeStruct((T, CC), F32)],
        (z, z, w_dw4, b_dw, ln_g, ln_b, w_pool_b, s_pool),
        scratch=[pltpu.VMEM((SUBLANES, HALO + tt, CC), F32), pltpu.VMEM((HALO + tt, n_grp * G), F32)],
        after=after,
    )


def _out_proj(y_b, x, w_out_b, g_ffn, after=()):
    T, D = x.shape
    tm = _tile(T, 512)

    def body(y_ref, x_ref, w_ref, g_ref, h1_ref, hn_ref):
        h1 = x_ref[...] + _dot(y_ref[...], w_ref[...], NN)
        h1_ref[...] = h1
        r = lax.rsqrt(jnp.mean(h1 * h1, axis=-1, keepdims=True) + RMS_EPS)
        hn_ref[...] = (h1 * r * g_ref[...]).astype(BF16)

    row = lambda i: (i, 0)
    return _call(
        "out_proj",
        body,
        (T // tm,),
        [
            pl.BlockSpec((tm, y_b.shape[1]), row),
            pl.BlockSpec((tm, D), row),
            pl.BlockSpec(w_out_b.shape, lambda i: (0, 0)),
            pl.BlockSpec((1, D), lambda i: (0, 0)),
        ],
        [pl.BlockSpec((tm, D), row), pl.BlockSpec((tm, D), row)],
        [jax.ShapeDtypeStruct((T, D), F32), jax.ShapeDtypeStruct((T, D), BF16)],
        (y_b, x, w_out_b, g_ffn),
        after=after,
    )


def _hidden_tile(F):
    return _tile(F, 1408, LANES)


def _gate_up(hn_b, wgT_b, wuT_b):
    T, D = hn_b.shape
    F = wgT_b.shape[0]
    tm, tf = _tile(T, 1024), _hidden_tile(F)

    def body(hn_ref, wg_ref, wu_ref, g_ref, u_ref, a_ref):
        hn = hn_ref[...]
        for c0 in range(0, tf, HIDDEN_CHUNK):
            cs = slice(c0, min(c0 + HIDDEN_CHUNK, tf))
            gv = _dot(hn, wg_ref[cs, :], NT)
            uv = _dot(hn, wu_ref[cs, :], NT)
            g_ref[:, cs] = gv.astype(BF16)
            u_ref[:, cs] = uv.astype(BF16)
            a_ref[:, cs] = (gv * _sigmoid(gv) * uv).astype(BF16)

    wspec = pl.BlockSpec((tf, D), lambda j, i: (j, 0))
    ospec = pl.BlockSpec((tm, tf), lambda j, i: (i, j))
    return _call(
        "gate_up",
        body,
        (F // tf, T // tm),
        [pl.BlockSpec((tm, D), lambda j, i: (i, 0)), wspec, wspec],
        [ospec, ospec, ospec],
        [jax.ShapeDtypeStruct((T, F), BF16)] * 3,
        (hn_b, wgT_b, wuT_b),
    )


def _down_loss(a_b, wd_b, h1, target, g_final):
    T, D = h1.shape
    F = a_b.shape[1]
    tm = _tile(T, 512)
    nt = T // tm

    def body(a_ref, w_ref, h1_ref, t_ref, g_ref, dh2_ref, dh2b_ref, loss_ref, dg_ref):
        i = pl.program_id(0)
        h2 = h1_ref[...] + _dot(a_ref[...], w_ref[...], NN)
        r = lax.rsqrt(jnp.mean(h2 * h2, axis=-1, keepdims=True) + RMS_EPS)
        g = g_ref[...]
        diff = h2 * r * g - t_ref[...]
        _accumulate(loss_ref, i == 0, jnp.full(loss_ref.shape, jnp.sum(diff * diff) * (0.5 / D), F32))
        dh2, dg_rows = _rms_bwd(h2, g, diff * (1.0 / D))
        dh2_ref[...] = dh2
        dh2b_ref[...] = dh2.astype(BF16)
        _accumulate(dg_ref, i == 0, jnp.sum(dg_rows, axis=0, keepdims=True))

    row = lambda i: (i, 0)
    return _call(
        "down_loss",
        body,
        (nt,),
        [
            pl.BlockSpec((tm, F), row),
            pl.BlockSpec((F, D), lambda i: (0, 0), pipeline_mode=pl.Buffered(1)),
            pl.BlockSpec((tm, D), row),
            pl.BlockSpec((tm, D), row),
            pl.BlockSpec((1, D), lambda i: (0, 0)),
        ],
        [
            pl.BlockSpec((tm, D), row),
            pl.BlockSpec((tm, D), row),
            pl.BlockSpec((1, LANES), lambda i: (0, 0)),
            pl.BlockSpec((1, D), lambda i: (0, 0)),
        ],
        [
            jax.ShapeDtypeStruct((T, D), F32),
            jax.ShapeDtypeStruct((T, D), BF16),
            jax.ShapeDtypeStruct((1, LANES), F32),
            jax.ShapeDtypeStruct((1, D), F32),
        ],
        (a_b, wd_b, h1, target, g_final),
    )


def _ffn_bwd_act(dh2_b, wd_b, g_b, u_b, comm=()):
    T, D = dh2_b.shape
    F = wd_b.shape[0]
    tm, tf = _tile(T, 1024), _hidden_tile(F)

    def body(d_ref, w_ref, g_ref, u_ref, dg_ref, du_ref):
        d = d_ref[...]
        for c0 in range(0, tf, HIDDEN_CHUNK):
            cs = slice(c0, min(c0 + HIDDEN_CHUNK, tf))
            da = _dot(d, w_ref[cs, :], NT)
            gv = g_ref[:, cs].astype(F32)
            uv = u_ref[:, cs].astype(F32)
            sg = _sigmoid(gv)
            silu = gv * sg
            dg_ref[:, cs] = (da * uv * (sg * (1.0 + gv * (1.0 - sg)))).astype(BF16)
            du_ref[:, cs] = (da * silu).astype(BF16)

    aspec = pl.BlockSpec((tm, tf), lambda j, i: (i, j))
    return _call(
        "ffn_bwd_act",
        body,
        (F // tf, T // tm),
        [pl.BlockSpec((tm, D), lambda j, i: (i, 0)), pl.BlockSpec((tf, D), lambda j, i: (j, 0)), aspec, aspec],
        [aspec, aspec],
        [jax.ShapeDtypeStruct((T, F), BF16)] * 2,
        (dh2_b, wd_b, g_b, u_b),
        comm=comm,
    )


def _ffn_bwd_in(dg_b, du_b, wgT_b, wuT_b, h1, dh2, g_ffn, w_out_b, comm=()):
    T, D = h1.shape
    F = wgT_b.shape[0]
    DM = w_out_b.shape[0]
    tm = _tile(T, 512)

    def body(dg_ref, du_ref, wg_ref, wu_ref, h1_ref, dh2_ref, g_ref, wo_ref, dh1_ref, dh1b_ref, dy_ref, dgf_ref):
        i = pl.program_id(0)
        dhn = _dot(dg_ref[...], wg_ref[...], NN) + _dot(du_ref[...], wu_ref[...], NN)
        dx, dg_rows = _rms_bwd(h1_ref[...], g_ref[...], dhn)
        dh1 = dh2_ref[...] + dx
        dh1b = dh1.astype(BF16)
        dh1_ref[...] = dh1
        dh1b_ref[...] = dh1b
        dy_ref[...] = _dot(dh1b, wo_ref[...], NT)
        _accumulate(dgf_ref, i == 0, jnp.sum(dg_rows, axis=0, keepdims=True))

    row = lambda i: (i, 0)
    const = lambda i: (0, 0)
    return _call(
        "ffn_bwd_in",
        body,
        (T // tm,),
        [
            pl.BlockSpec((tm, F), row),
            pl.BlockSpec((tm, F), row),
            pl.BlockSpec((F, D), const, pipeline_mode=pl.Buffered(1)),
            pl.BlockSpec((F, D), const, pipeline_mode=pl.Buffered(1)),
            pl.BlockSpec((tm, D), row),
            pl.BlockSpec((tm, D), row),
            pl.BlockSpec((1, D), const),
            pl.BlockSpec((DM, D), const, pipeline_mode=pl.Buffered(1)),
        ],
        [pl.BlockSpec((tm, D), row), pl.BlockSpec((tm, D), row), pl.BlockSpec((tm, DM), row), pl.BlockSpec((1, D), const)],
        [
            jax.ShapeDtypeStruct((T, D), F32),
            jax.ShapeDtypeStruct((T, D), BF16),
            jax.ShapeDtypeStruct((T, DM), F32),
            jax.ShapeDtypeStruct((1, D), F32),
        ],
        (dg_b, du_b, wgT_b, wuT_b, h1, dh2, g_ffn, w_out_b),
        comm=comm,
    )


def _seq_bwd(z, dy, v, w_dw4, ln_g, ln_b, w_pool_b, s_pool, comm=()):
    T, CI = z.shape
    CC = ln_g.shape[1]
    n_grp, G = w_pool_b.shape[0], w_pool_b.shape[-1]
    CP = n_grp * G
    KW = w_dw4.shape[1]
    n_cc = CC // LANES
    D = CC + CP
    tt = _tile(T, 512, HALO)
    per = tt // HALO
    n_tiles = T // tt
    last_halo = T // HALO - 1

    def body(zc_ref, zp_ref, dyc_ref, dyn_ref, vc_ref, vn_ref, wdw_ref, lng_ref, lnb_ref, wp_ref, sp_ref,
             dz_ref, dwdw_ref, dbdw_ref, dlng_ref, dlnb_ref, dwp_ref, dsp_ref, dbin_ref,
             dv_scr, u_scr, p_scr, g_scr, dw_scr):
        i = pl.program_id(0)
        first = i == 0
        last = i == n_tiles - 1
        lng, lnb = lng_ref[...], lnb_ref[...]

        def conv_pre(vv, dyc):
            mu = jnp.mean(vv, axis=-1, keepdims=True)
            d = vv - mu
            rs = lax.rsqrt(jnp.mean(d * d, axis=-1, keepdims=True) + LN_EPS)
            xh = d * rs
            ln = xh * lng + lnb
            sg = _sigmoid(ln)
            dln = dyc * (sg * (1.0 + ln * (1.0 - sg)))
            dxh = dln * lng
            dv = rs * (dxh - jnp.mean(dxh, axis=-1, keepdims=True) - xh * jnp.mean(dxh * xh, axis=-1, keepdims=True))
            return dv, dln, xh

        dv_c, dln_c, xh_c = conv_pre(vc_ref[...], dyc_ref[:, 0:CC])
        dv_scr[0, 0:tt, :] = dv_c
        dv_n, _, _ = conv_pre(vn_ref[...], dyn_ref[:, 0:CC])
        dv_scr[0, tt:, :] = jnp.where(last, 0.0, dv_n)
        _fill_shifted(dv_scr)
        _accumulate(dlng_ref, first, jnp.sum(dln_c * xh_c, axis=0, keepdims=True))
        _accumulate(dlnb_ref, first, jnp.sum(dln_c, axis=0, keepdims=True))
        _accumulate(dbdw_ref, first, jnp.sum(dv_c, axis=0, keepdims=True))

        u_scr[...] = zc_ref[:, 0:CC] * _sigmoid(zc_ref[:, CC : 2 * CC])

        @pl.when(first)
        def _():
            dw_scr[...] = jnp.zeros_like(dw_scr)

        for j in range(n_cc):
            cs = slice(LANES * j, LANES * (j + 1))
            gs = slice(CC + LANES * j, CC + LANES * (j + 1))
            dbin_a = jnp.zeros((1, LANES), F32)
            dbin_g = jnp.zeros((1, LANES), F32)
            for rb in range(tt // CONV_ROWS):
                rows = slice(rb * CONV_ROWS, (rb + 1) * CONV_ROWS)
                u_blk = u_scr[rows, cs]
                du = jnp.zeros((CONV_ROWS, LANES), F32)
                for k in range(KW):
                    off = rb * CONV_ROWS + (KW - 1) - k
                    d = _shifted_rows(dv_scr, off, CONV_ROWS, cs)
                    du = du + d * wdw_ref[j, k]
                    dw_scr[j * HALO + k] += jnp.sum((u_blk * d).reshape(CONV_ROWS // 8, 8, LANES), axis=0)
                a = zc_ref[rows, cs]
                sg = _sigmoid(zc_ref[rows, gs])
                da = du * sg
                dgate = du * a * sg * (1.0 - sg)
                dz_ref[rows, cs] = da.astype(BF16)
                dz_ref[rows, gs] = dgate.astype(BF16)
                dbin_a = dbin_a + jnp.sum(da, axis=0, keepdims=True)
                dbin_g = dbin_g + jnp.sum(dgate, axis=0, keepdims=True)
            _accumulate(dbin_ref.at[:, cs], first, dbin_a)
            _accumulate(dbin_ref.at[:, gs], first, dbin_g)

        @pl.when(last)
        def _():
            dwdw_ref[...] = jnp.sum(dw_scr[...], axis=1).reshape(dwdw_ref.shape)

        p_scr[0:HALO, :] = jnp.where(first, 0.0, zp_ref[:, 2 * CC :])
        p_scr[HALO:, :] = zc_ref[:, 2 * CC :]
        tpos = i * tt + lax.broadcasted_iota(jnp.int32, (tt, 1), 0)
        for gi, w in enumerate(POOL_WINDOWS):
            cs = slice(G * gi, G * (gi + 1))
            ys = slice(CC + G * gi, CC + G * (gi + 1))
            ps = slice(2 * CC + G * gi, 2 * CC + G * (gi + 1))
            cnt = jnp.minimum(tpos + 1, w).astype(F32)
            yib = _pool_mean_minus_token(p_scr, cs, w, cnt, tt).astype(BF16)
            wp = wp_ref[gi]
            sp = sp_ref[:, cs]
            dyp = dyc_ref[:, ys]
            q = _dot(yib, wp, NN)
            _accumulate(dsp_ref.at[:, cs], first, jnp.sum(dyp * q, axis=0, keepdims=True))
            dq_c = (dyp * sp).astype(BF16)
            dq_n = (jnp.where(last, 0.0, dyn_ref[:, ys]) * sp).astype(BF16)
            _accumulate(dwp_ref.at[gi], first, _dot(yib, dq_c, TN))
            dyi_c = _dot(dq_c, wp, NT)
            g_scr[0:tt, cs] = dyi_c / cnt
            g_scr[tt:, cs] = _dot(dq_n, wp, NT) * (1.0 / w)
            dp = -dyi_c
            for d in range(w):
                dp = dp + g_scr[d : d + tt, cs]
            dz_ref[:, ps] = dp.astype(BF16)
            _accumulate(dbin_ref.at[:, ps], first, jnp.sum(dp, axis=0, keepdims=True))

    cur = lambda i: (i, 0)
    prev = lambda i: (jnp.maximum(i * per - 1, 0), 0)
    nxt = lambda i: (jnp.minimum((i + 1) * per, last_halo), 0)
    c2 = lambda i: (0, 0)
    c3 = lambda i: (0, 0, 0)
    return _call(
        "seq_bwd",
        body,
        (n_tiles,),
        [
            pl.BlockSpec((tt, CI), cur),
            pl.BlockSpec((HALO, CI), prev),
            pl.BlockSpec((tt, D), cur),
            pl.BlockSpec((HALO, D), nxt),
            pl.BlockSpec((tt, CC), cur),
            pl.BlockSpec((HALO, CC), nxt),
            pl.BlockSpec(w_dw4.shape, lambda i: (0,) * w_dw4.ndim),
            pl.BlockSpec((1, CC), c2),
            pl.BlockSpec((1, CC), c2),
            pl.BlockSpec(w_pool_b.shape, c3),
            pl.BlockSpec((1, CP), c2),
        ],
        [
            pl.BlockSpec((tt, CI), cur),
            pl.BlockSpec((n_cc, HALO, LANES), c3),
            pl.BlockSpec((1, CC), c2),
            pl.BlockSpec((1, CC), c2),
            pl.BlockSpec((1, CC), c2),
            pl.BlockSpec((n_grp, G, G), c3),
            pl.BlockSpec((1, CP), c2),
            pl.BlockSpec((1, CI), c2),
        ],
        [
            jax.ShapeDtypeStruct((T, CI), BF16),
            jax.ShapeDtypeStruct((n_cc, HALO, LANES), F32),
            jax.ShapeDtypeStruct((1, CC), F32),
            jax.ShapeDtypeStruct((1, CC), F32),
            jax.ShapeDtypeStruct((1, CC), F32),
            jax.ShapeDtypeStruct((n_grp, G, G), F32),
            jax.ShapeDtypeStruct((1, CP), F32),
            jax.ShapeDtypeStruct((1, CI), F32),
        ],
        (z, z, dy, dy, v, v, w_dw4, ln_g, ln_b, w_pool_b, s_pool),
        scratch=[
            pltpu.VMEM((SUBLANES, tt + HALO, CC), F32),
            pltpu.VMEM((tt, CC), F32),
            pltpu.VMEM((HALO + tt, CP), F32),
            pltpu.VMEM((tt + HALO, CP), F32),
            pltpu.VMEM((n_cc * HALO, 8, LANES), F32),
        ],
        comm=comm,
    )


def _in_proj_bwd(dz_b, w_inT_b, x, dh1, g_mix, after=()):
    T, D = x.shape
    CI = w_inT_b.shape[0]
    tm = _tile(T, 512)

    def body(dz_ref, w_ref, x_ref, dh1_ref, g_ref, dx_ref, dg_ref):
        i = pl.program_id(0)
        dxn = _dot(dz_ref[...], w_ref[...], NN)
        dx, dg_rows = _rms_bwd(x_ref[...], g_ref[...], dxn)
        dx_ref[...] = dh1_ref[...] + dx
        _accumulate(dg_ref, i == 0, jnp.sum(dg_rows, axis=0, keepdims=True))

    row = lambda i: (i, 0)
    const = lambda i: (0, 0)
    return _call(
        "in_proj_bwd",
        body,
        (T // tm,),
        [
            pl.BlockSpec((tm, CI), row),
            pl.BlockSpec((CI, D), const),
            pl.BlockSpec((tm, D), row),
            pl.BlockSpec((tm, D), row),
            pl.BlockSpec((1, D), const),
        ],
        [pl.BlockSpec((tm, D), row), pl.BlockSpec((1, D), const)],
        [jax.ShapeDtypeStruct((T, D), F32), jax.ShapeDtypeStruct((1, D), F32)],
        (dz_b, w_inT_b, x, dh1, g_mix),
        after=after,
    )


def _weight_grad(name, a_b, b_b, comm=(), after=()):
    T, N1 = a_b.shape
    N2 = b_b.shape[1]
    t1 = _tile(N1, 1408, LANES)
    tk = _tile(T, 2048)
    nk = T // tk

    def body(a_ref, b_ref, o_ref, acc):
        k = pl.program_id(1)
        _accumulate(acc, k == 0, _dot(a_ref[...], b_ref[...], TN))

        @pl.when(k == nk - 1)
        def _():
            o_ref[...] = acc[...].astype(BF16)

    (out,), rest = _call(
        name,
        body,
        (N1 // t1, nk),
        [pl.BlockSpec((tk, t1), lambda n, k: (k, n)), pl.BlockSpec((tk, N2), lambda n, k: (k, 0))],
        [pl.BlockSpec((t1, N2), lambda n, k: (n, 0))],
        [jax.ShapeDtypeStruct((N1, N2), BF16)],
        (a_b, b_b),
        scratch=[pltpu.VMEM((t1, N2), F32)],
        comm=comm,
        after=after,
    )
    return out, rest


def _sum_parts(name, full, how, parts, me):
    _, R, C = parts[0].shape
    tr = _tile(R, 512)
    nb = R // tr
    where = [(q, r) for q, p in enumerate(parts) for r in range(p.shape[0])]
    assert len(where) == 3

    def body(me_ref, own_ref, *refs):
        o_ref = refs[-1]
        f = lambda j: refs[where[j][0]][where[j][1]].astype(F32)
        o_ref[...] = (own_ref[...].astype(F32) + f(0)) + (f(1) + f(2))

    own_map = {"rows": lambda i, me_ref: (me_ref[0] * nb + i, 0), "all": lambda i, me_ref: (i, 0)}[how]
    return pl.pallas_call(
        body,
        name=name,
        grid_spec=pltpu.PrefetchScalarGridSpec(
            num_scalar_prefetch=1,
            grid=(nb,),
            in_specs=[pl.BlockSpec((tr, C), own_map)]
            + [pl.BlockSpec((p.shape[0], tr, C), lambda i, me_ref: (0, i, 0)) for p in parts],
            out_specs=pl.BlockSpec((tr, C), lambda i, me_ref: (i, 0)),
        ),
        out_shape=jax.ShapeDtypeStruct((R, C), F32),
        compiler_params=pltpu.CompilerParams(dimension_semantics=("arbitrary",), vmem_limit_bytes=VMEM_LIMIT),
    )(me, full, *parts)


_M_CORR = 1.0 - ADAM_B1**ADAM_STEP
_V_CORR = 1.0 - ADAM_B2**ADAM_STEP


def _adamw_math(w, g, m, v):
    m = ADAM_B1 * m + (1.0 - ADAM_B1) * g
    v = ADAM_B2 * v + (1.0 - ADAM_B2) * (g * g)
    delta = -ADAM_LR * ((m / _M_CORR) / (jnp.sqrt(v / _V_CORR) + ADAM_EPS) + ADAM_WD * w)
    return delta, m, v


def _adamw(name, w, m, v, g_here, g_there, g_transposed=False):
    R, C = w.shape
    tr = _tile(R, 256, LANES if g_transposed else 8)

    def body(w_ref, m_ref, v_ref, ga_ref, gb_ref, g_ref, d_ref, nm_ref, nv_ref):
        g = ga_ref[...] + gb_ref[...]
        if g_transposed:
            g = g.T
        g_ref[...] = g
        d_ref[...], nm_ref[...], nv_ref[...] = _adamw_math(w_ref[...], g, m_ref[...], v_ref[...])

    spec = pl.BlockSpec((tr, C), lambda i: (i, 0))
    gspec = pl.BlockSpec((C, tr), lambda i: (0, i)) if g_transposed else spec
    return _call(name, body, (R // tr,), [spec] * 3 + [gspec] * 2, [spec] * 4, [jax.ShapeDtypeStruct((R, C), F32)] * 4,
                 (w, m, v, g_here, g_there))


def _adamw_on_sparsecore(name, w, m, v, g_here, g_there, after):
    R, C = w.shape
    n_groups = R // SUBLANES
    n_turns = -(-n_groups // SC_TILES)
    n_in, n_out = 5, 4

    def body(w_hbm, m_hbm, v_hbm, ga_hbm, gb_hbm, after_hbm, g_out, d_out, nm_out, nv_out, bufs, sems):
        tile = lax.axis_index("subcore") * SC_CORES + lax.axis_index("sparsecore")
        srcs = (w_hbm, m_hbm, v_hbm, ga_hbm, gb_hbm)
        dsts = (d_out, nm_out, nv_out, g_out)

        def rows(turn):
            return pl.ds((tile + turn * SC_TILES) * SUBLANES, SUBLANES)

        def loads(turn):
            slot = turn % 2
            return [pltpu.make_async_copy(srcs[q].at[rows(turn), :], bufs.at[slot, q], sems.at[slot, q]) for q in range(n_in)]

        def stores(turn):
            slot = turn % 2
            return [pltpu.make_async_copy(bufs.at[slot, q], dsts[q].at[rows(turn), :], sems.at[slot, n_in + q])
                    for q in range(n_out)]

        def when_mine(turn, fn):
            pl.when(tile + turn * SC_TILES < n_groups)(fn)

        def compute(slot):
            wb, mb, vb, gab, gbb = (bufs.at[slot, q] for q in range(n_in))

            @pl.loop(0, SUBLANES)
            def _(r):
                @pl.loop(0, C, step=SC_LANES)
                def _(i):
                    at = (r, pl.ds(i, SC_LANES))
                    g = gab[at] + gbb[at]
                    delta, new_m, new_v = _adamw_math(wb[at], g, mb[at], vb[at])
                    gab[at], wb[at], mb[at], vb[at] = g, delta, new_m, new_v

        def start_loads(turn):
            def fn():
                for cp in loads(turn):
                    cp.start()

            when_mine(turn, fn)

        start_loads(0)
        for turn in range(n_turns):
            def step(turn=turn):
                for cp in loads(turn):
                    cp.wait()
                if turn >= 1:
                    for cp in stores(turn - 1):
                        cp.wait()
                if turn + 1 < n_turns:
                    start_loads(turn + 1)
                compute(turn % 2)
                for cp in stores(turn):
                    cp.start()

            when_mine(turn, step)
        for turn in range(n_turns):
            def drain(turn=turn):
                for cp in stores(turn):
                    cp.wait()

            last_mine = jnp.logical_and(tile + turn * SC_TILES < n_groups, tile + (turn + 1) * SC_TILES >= n_groups)
            pl.when(last_mine)(drain)

    return pl.kernel(
        body,
        name=name,
        out_type=[jax.ShapeDtypeStruct((R, C), F32)] * 4,
        mesh=plsc.VectorSubcoreMesh(core_axis_name="sparsecore", subcore_axis_name="subcore"),
        scratch_types=[pltpu.VMEM((2, n_in, SUBLANES, C), F32), pltpu.SemaphoreType.DMA((2, n_in + n_out))],
        compiler_params=pltpu.CompilerParams(use_tc_tiling_on_sc=True),
    )(w, m, v, g_here, g_there, after)


class _PackLayout:
    def __init__(self, n_cc, n_grp, G, widths):
        self.dw_rows = (0, HALO)
        self.wp_rows = (HALO, HALO + G)
        self.n_cc, self.n_grp, self.G = n_cc, n_grp, G
        self.vec = {}
        r = HALO + G
        for name, width in widths:
            self.vec[name] = (r, width)
            r += width // PACK_W
        self.rows = -(-r // 8) * 8


def _pack_small(layout, dwdw, dwp, vecs):
    names = list(vecs)

    def body(*refs):
        dw_ref, wp_ref = refs[0], refs[1]
        vec_refs = refs[2 : 2 + len(names)]
        o_ref = refs[-1]
        o_ref[...] = jnp.zeros_like(o_ref)
        for j in range(layout.n_cc):
            o_ref[layout.dw_rows[0] : layout.dw_rows[1], j * LANES : (j + 1) * LANES] = dw_ref[j]
        for i in range(layout.n_grp):
            o_ref[layout.wp_rows[0] : layout.wp_rows[1], i * layout.G : (i + 1) * layout.G] = wp_ref[i]
        for name, ref in zip(names, vec_refs):
            r, width = layout.vec[name]
            for h in range(width // PACK_W):
                o_ref[r + h : r + h + 1, :] = ref[:, h * PACK_W : (h + 1) * PACK_W]

    return pl.pallas_call(
        body,
        name="pack_small",
        out_shape=jax.ShapeDtypeStruct((layout.rows, PACK_W), F32),
    )(dwdw, dwp, *[vecs[k] for k in names])


def _adamw_small(layout, g_here, g_there, w_dw, m_dw, v_dw, w_pool, m_pool, v_pool, vec_w, vec_m, vec_v):
    names = list(vec_w)
    nv = len(names)

    def body(*refs):
        ga_ref, gb_ref = refs[0], refs[1]
        wdw, mdw, vdw, wp, mp, vp = refs[2:8]
        vw, vm, vv = refs[8 : 8 + nv], refs[8 + nv : 8 + 2 * nv], refs[8 + 2 * nv : 8 + 3 * nv]
        outs = refs[8 + 3 * nv :]
        acc = outs[-1]
        acc[...] = ga_ref[...] + gb_ref[...]

        def emit(o, g, w, m, v, idx=()):
            res = (g,) + _adamw_math(w, g, m, v)
            for ref, val in zip(o, res):
                ref[idx] = val

        me = 2 * lax.axis_index("x") + lax.axis_index("y")
        for j in range(layout.n_cc):

            @pl.when(me == j)
            def _(j=j):
                for k in range(wdw.shape[0]):
                    g = acc[layout.dw_rows[0] + k : layout.dw_rows[0] + k + 1, j * LANES : (j + 1) * LANES]
                    emit(outs[0:4], g, wdw[k], mdw[k], vdw[k], idx=k)

        for i in range(layout.n_grp):
            g = acc[layout.wp_rows[0] : layout.wp_rows[1], i * layout.G : (i + 1) * layout.G]
            emit(outs[4:8], g, wp[i], mp[i], vp[i], idx=i)
        for q, name in enumerate(names):
            r, width = layout.vec[name]
            for h in range(width // PACK_W):
                ls = slice(h * PACK_W, (h + 1) * PACK_W)
                g = acc[r + h : r + h + 1, :]
                emit(outs[8 + 4 * q : 12 + 4 * q], g, vw[q][:, ls], vm[q][:, ls], vv[q][:, ls], idx=(slice(None), ls))

    shapes = [w_dw.shape] * 4 + [w_pool.shape] * 4
    for name in names:
        shapes += [vec_w[name].shape] * 4
    return pl.pallas_call(
        body,
        name="adamw_small",
        out_shape=[jax.ShapeDtypeStruct(s, F32) for s in shapes],
        scratch_shapes=[pltpu.VMEM(g_here.shape, F32)],
    )(g_here, g_there, w_dw, m_dw, v_dw, w_pool, m_pool, v_pool,
      *[vec_w[k] for k in names], *[vec_m[k] for k in names], *[vec_v[k] for k in names])


def _allreduce_adamw_row(g_part, w, m, v, loss_part, comm=()):
    D = w.shape[1]
    n_pairs = N_DEV - 1

    def body(g_ref, w_ref, m_ref, v_ref, l_ref, go_ref, d_ref, nm_ref, nv_ref, lo_ref, land_g, land_l, sems):
        x, y, c = _place()
        copies = []
        for q, (src, land) in enumerate(((g_ref, land_g), (l_ref, land_l))):
            for r in range(1, N_DEV):
                fx, fy, fc = (r >> 2) & 1, (r >> 1) & 1, r & 1
                peer = (1 - x if fx else x, 1 - y if fy else y, 1 - c if fc else c)
                cp = _remote(src, land.at[r], sems, 2 * (q * n_pairs + r - 1), peer)
                cp.start()
                copies.append(cp)
        for cp in copies:
            cp.wait()

        def total(src, land):
            row = lambda r: src[...] if r == 0 else land[r]
            return ((row(0) + row(4)) + (row(2) + row(6))) + ((row(1) + row(5)) + (row(3) + row(7)))

        g = total(g_ref, land_g)
        go_ref[...] = g
        d_ref[...], nm_ref[...], nv_ref[...] = _adamw_math(w_ref[...], g, m_ref[...], v_ref[...])
        lo_ref[...] = total(l_ref, land_l)

    vm = pl.BlockSpec(memory_space=pltpu.VMEM)
    return _call(
        "allreduce_adamw_g_mix",
        body,
        (),
        [vm] * 5,
        [vm] * 5,
        [jax.ShapeDtypeStruct((1, D), F32)] * 4 + [jax.ShapeDtypeStruct(loss_part.shape, F32)],
        (g_part, w, m, v, loss_part),
        scratch=[pltpu.VMEM((N_DEV, 1, D), F32), pltpu.VMEM((N_DEV,) + loss_part.shape, F32),
                 pltpu.SemaphoreType.DMA((4 * n_pairs,))],
        comm=comm,
    )


def kernel(x, g_mix, w_in, b_in, w_dw, b_dw, ln_g, ln_b, w_pool, s_pool, w_out, g_ffn, w_gate, w_up, w_down, g_final, loss_target, m_g_mix, m_w_in, m_b_in, m_w_dw, m_b_dw, m_ln_g, m_ln_b, m_w_pool, m_s_pool, m_w_out, m_g_ffn, m_w_gate, m_w_up, m_w_down, m_g_final, v_g_mix, v_w_in, v_b_in, v_w_dw, v_b_dw, v_ln_g, v_ln_b, v_w_pool, v_s_pool, v_w_out, v_g_ffn, v_w_gate, v_w_up, v_w_down, v_g_final):
    x2 = x[0]
    target = loss_target[0]
    T, D = x2.shape
    w_in2, w_out2, w_down2 = w_in[0], w_out[0], w_down[0]
    taps_first = lambda a: jnp.transpose(a, (1, 0, 2))
    w_dw3 = taps_first(w_dw)
    w_gateT, w_upT = w_gate[0].T, w_up[0].T
    CI = w_in2.shape[1] * N_CHIPS
    DM = w_out2.shape[0] * N_CHIPS
    F = w_down2.shape[0] * N_CHIPS
    KW, _, dw_cols = w_dw3.shape
    assert dw_cols == LANES
    n_grp, G = w_pool.shape[1], w_pool.shape[-1]
    w_pool3 = w_pool[0]
    g_final2 = g_final.reshape(1, D)

    me = (2 * lax.axis_index("x") + lax.axis_index("y")).astype(jnp.int32).reshape(1)

    w_inT_b, w_dw4, f_out, f_gate, f_up, f_down = _place_and_gather(
        [(w_in2, "rows", (CI, D), BF16, True, True), (w_dw3, "lead", (N_CHIPS, KW, 1, dw_cols), F32, False, False)],
        [(w, "rows", shape, BF16, False, True)
         for w, shape in ((w_out2, (DM, D)), (w_gateT, (F, D)), (w_upT, (F, D)), (w_down2, (F, D)))])
    w_pool_b = w_pool3.astype(BF16)
    ici = lambda f: _GatherIci([f], ["rows"], [True])
    d2d = lambda f: _GatherD2d([f], ["rows"])
    gather = _start("gather_start", [ici(f_out), ici(f_gate), ici(f_up), ici(f_down)])
    (z, xn_b), _ = _in_proj(x2, g_mix, w_inT_b, b_in, after=[gather.token])
    (f_out,) = _wait("gather_out_wait", gather, 0, xn_b)
    s_out = _start("share_out_start", [d2d(f_out)], sibling_only=True)
    (y_b, v), _ = _seq_fwd(z, w_dw4, b_dw, ln_g, ln_b, w_pool_b, s_pool, after=[s_out.token])
    (w_out_b,) = _wait("share_out_wait", s_out, 0, y_b)
    (f_gate,) = _wait("gather_gate_wait", gather, 1, y_b)
    s_gate = _start("share_gate_start", [d2d(f_gate)], sibling_only=True)
    (h1, hn_b), _ = _out_proj(y_b, x2, w_out_b, g_ffn, after=[s_gate.token])
    (f_up,) = _wait("gather_up_wait", gather, 2, hn_b)
    s_up = _start("share_up_start", [d2d(f_up)], sibling_only=True)
    (wgT_b,) = _wait("share_gate_wait", s_gate, 0, hn_b)
    (wuT_b,) = _wait("share_up_wait", s_up, 0, hn_b)
    (g_b, u_b, a_b), _ = _gate_up(hn_b, wgT_b, wuT_b)
    (f_down,) = _wait("gather_down_wait", gather, 3, a_b)
    s_down = _start("share_down_start", [d2d(f_down)], sibling_only=True)
    (wd_b,) = _wait("share_down_wait", s_down, 0, a_b)
    (dh2, dh2_b, loss_part, d_g_final), _ = _down_loss(a_b, wd_b, h1, target, g_final2)

    gw_down, _ = _weight_grad("grad_w_down", a_b, dh2_b)
    (dg_b, du_b), (p_down_xy,) = _ffn_bwd_act(dh2_b, wd_b, g_b, u_b, comm=[_Scatter([gw_down], ["rows"], which=(0, 1))])
    gw_gateT, (p_down_d,) = _weight_grad("grad_w_gate", dg_b, hn_b, comm=[_Scatter([gw_down], ["rows"], which=(2,))])
    x_gate = _start("scatter_gate_start", [_Scatter([gw_gateT], ["rows"])])
    gw_upT, _ = _weight_grad("grad_w_up", du_b, hn_b, after=[x_gate.token])
    sum_down = _sum_parts("sum_w_down", gw_down, "rows", [p_down_xy, p_down_d], me)
    (dh1, dh1_b, dy, d_g_ffn), (oth_down,) = _ffn_bwd_in(
        dg_b, du_b, wgT_b, wuT_b, h1, dh2, g_ffn, w_out_b, comm=[_Swap([sum_down])])
    gw_gateT, p_gate = _wait("scatter_gate_wait", x_gate, 0, dy)
    gw_out, _ = _weight_grad("grad_w_out", y_b, dh1_b)
    sum_gate = _sum_parts("sum_w_gate", gw_gateT, "rows", [p_gate], me)
    res = {}
    res["w_down"] = _adamw_on_sparsecore("adamw_w_down", w_down2, m_w_down[0], v_w_down[0], sum_down, oth_down, sum_down)
    (dz_b, d_wdw, d_bdw, d_lng, d_lnb, d_wp, d_sp, d_bin), (p_up, p_out, oth_gate) = _seq_bwd(
        z, dy, v, w_dw4, ln_g, ln_b, w_pool_b, s_pool,
        comm=[_Scatter([gw_upT, gw_out], ["rows", "rows"]), _Swap([sum_gate])])
    res["w_gate"] = _adamw_on_sparsecore(
        "adamw_w_gate", w_gateT, m_w_gate[0].T, v_w_gate[0].T, sum_gate, oth_gate, res["w_down"][0])
    vec_grads ={"b_dw": d_bdw, "ln_g": d_lng, "ln_b": d_lnb, "s_pool": d_sp, "g_ffn": d_g_ffn, "g_final": d_g_final, "b_in": d_bin}
    layout = _PackLayout(dw_cols * N_CHIPS // LANES, n_grp, G, [(k, a.shape[1]) for k, a in vec_grads.items()])
    pack = _pack_small(layout, d_wdw, d_wp, vec_grads)
    sum_up = _sum_parts("sum_w_up", gw_upT, "rows", [p_up], me)
    sum_out = _sum_parts("sum_w_out", gw_out, "rows", [p_out], me)
    gw_inT, (p_small, oth_up, oth_out) = _weight_grad(
        "grad_w_in", dz_b, xn_b, comm=[_Scatter([pack], ["all"]), _Swap([sum_up, sum_out])])
    sum_small = _sum_parts("sum_small", pack, "all", [p_small], me)
    late = _start("late_start", [_Scatter([gw_inT], ["rows"]), _Swap([sum_small])])
    (grad_x, d_g_mix), _ = _in_proj_bwd(dz_b, w_inT_b, x2, dh1, g_mix, after=[late.token])
    gw_inT, p_in = _wait("late_w_in_wait", late, 0, d_g_mix)
    sum_small, oth_small = _wait("late_small_wait", late, 1, d_g_mix)
    res["w_up"] = _adamw_on_sparsecore("adamw_w_up", w_upT, m_w_up[0].T, v_w_up[0].T, sum_up, oth_up, res["w_gate"][0])
    res["w_out"] = _adamw_on_sparsecore("adamw_w_out", w_out2, m_w_out[0], v_w_out[0], sum_out, oth_out, res["w_gate"][0])
    sum_in = _sum_parts("sum_w_in", gw_inT, "rows", [p_in], me)
    (*res["g_mix"], loss_row), (oth_in,) = _allreduce_adamw_row(
        d_g_mix, g_mix, m_g_mix, v_g_mix, loss_part, comm=[_Swap([sum_in])])
    loss = loss_row[0, 0]
    res["w_in"], _ = _adamw("adamw_w_in", w_in2, m_w_in[0], v_w_in[0], sum_in, oth_in, g_transposed=True)

    vec_w = {"b_dw": b_dw, "ln_g": ln_g, "ln_b": ln_b, "s_pool": s_pool, "g_ffn": g_ffn, "g_final": g_final2, "b_in": b_in}
    vec_m = {"b_dw": m_b_dw, "ln_g": m_ln_g, "ln_b": m_ln_b, "s_pool": m_s_pool, "g_ffn": m_g_ffn,
             "g_final": m_g_final.reshape(1, D), "b_in": m_b_in}
    vec_v = {"b_dw": v_b_dw, "ln_g": v_ln_g, "ln_b": v_ln_b, "s_pool": v_s_pool, "g_ffn": v_g_ffn,
             "g_final": v_g_final.reshape(1, D), "b_in": v_b_in}
    small = _adamw_small(layout, sum_small, oth_small, w_dw3, taps_first(m_w_dw), taps_first(v_w_dw),
                         w_pool3, m_w_pool[0], v_w_pool[0], vec_w, vec_m, vec_v)
    res["w_dw"] = [taps_first(a) for a in small[0:4]]
    res["w_pool"] = [a[None] for a in small[4:8]]
    for q, k in enumerate(vec_w):
        res[k] = list(small[8 + 4 * q : 12 + 4 * q])
    res["g_final"] = [a.reshape(D) for a in res["g_final"]]
    for k in ("w_in", "w_out", "w_down"):
        res[k] = [a[None] for a in res[k]]
    for k in ("w_gate", "w_up"):
        res[k] = [a.T[None] for a in res[k]]

    order = ["g_mix", "w_in", "b_in", "w_dw", "b_dw", "ln_g", "ln_b", "w_pool", "s_pool", "w_out", "g_ffn", "w_gate", "w_up", "w_down", "g_final"]
    outs = [loss, grad_x[None]]
    for q in range(4):
        outs += [res[k][q] for k in order]
    return tuple(outs)
```

```python
import jax
import jax.numpy as jnp
from jax import lax
from jax.experimental import pallas as pl
from jax.experimental.pallas import tpu as pltpu
from jax.experimental.pallas import tpu_sc as plsc

F32 = jnp.float32
BF16 = jnp.bfloat16
MESH = pl.DeviceIdType.MESH
ANY = pl.BlockSpec(memory_space=pl.ANY)

RMS_EPS = 1e-6
LN_EPS = 1e-5
POOL_WINDOWS = (2, 4, 8, 16)
ADAM_LR = 0.001
ADAM_B1 = 0.9
ADAM_B2 = 0.999
ADAM_EPS = 1e-08
ADAM_WD = 0.01
ADAM_STEP = 10

LANES = 128
SUBLANES = 8
BF16_ROWS = 16
HALO = 32
CONV_ROWS = 64
SUM_ROWS = 176
HIDDEN_CHUNK = 512
VMEM_LIMIT = 56 * 1024 * 1024
PACK_W = 512
N_CHIPS = 4
N_DEV = 8
SIBLING_BARRIER_ID = 0
SC_CORES = 2
SC_TILES = 32
SC_LANES = 16


def _tile(n, want, mult=8):
    t = min(n, want)
    while n % t or t % mult:
        t -= 1
    return t


def _sigmoid(x):
    return 1.0 / (1.0 + jnp.exp(-x))


def _dot(a, b, dims):
    return lax.dot_general(a, b, (dims, ((), ())), preferred_element_type=F32)


NN = ((1,), (0,))
NT = ((1,), (1,))
TN = ((0,), (0,))


def _rms_bwd(x, g, dy):
    r = lax.rsqrt(jnp.mean(x * x, axis=-1, keepdims=True) + RMS_EPS)
    xh = x * r
    gy = dy * g
    dx = r * (gy - xh * jnp.mean(gy * xh, axis=-1, keepdims=True))
    return dx, dy * xh


def _accumulate(ref, first, val):
    @pl.when(first)
    def _():
        ref[...] = val

    @pl.when(jnp.logical_not(first))
    def _():
        ref[...] += val


def _place():
    return lax.axis_index("x"), lax.axis_index("y"), lax.axis_index("c")


def _other_chips(x, y):
    return [(1 - x, y), (x, 1 - y), (1 - x, 1 - y)]


def _rows(ref, start, n):
    return ref.at[pl.ds(pl.multiple_of(start, BF16_ROWS), n)]


def _window(ref, how, k, c=None):
    if how == "all":
        return ref
    if how == "lead":
        return ref.at[k]
    assert how == "rows"
    n = ref.shape[0] // N_CHIPS
    if c is None:
        return _rows(ref, k * n, n)
    return _rows(ref, k * n + c * (n // 2), n // 2)


def _remote(src, dst, sems, s, device):
    return pltpu.make_async_remote_copy(
        src_ref=src, dst_ref=dst, send_sem=sems.at[s], recv_sem=sems.at[s + 1], device_id=device, device_id_type=MESH)


class _GatherIci:
    aliased = True

    def __init__(self, fulls, hows, splits, which=(0, 1, 2)):
        self.fulls, self.hows, self.splits, self.which = list(fulls), list(hows), list(splits), tuple(which)

    def inputs(self):
        return self.fulls

    def out_shapes(self):
        return [jax.ShapeDtypeStruct(a.shape, a.dtype) for a in self.fulls]

    def n_sems(self):
        return 6 * len(self.fulls)

    def build(self, ins, outs, sems, base):
        x, y, c = _place()
        me = 2 * x + y
        chips = _other_chips(x, y)
        starts, waits = [], []
        for a, (how, sp) in enumerate(zip(self.hows, self.splits)):
            half = c if sp else None
            mine = _window(outs[a], how, me, half)
            for j in self.which:
                px, py = chips[j]
                s = base + 6 * a + 2 * j
                cp = _remote(mine, mine, sems, s, (px, py, c))
                landing = _remote(mine, _window(outs[a], how, 2 * px + py, half), sems, s, (px, py, c))
                starts.append(cp.start)
                waits += [landing.wait_recv, cp.wait_send]
        return starts, waits


class _GatherD2d:
    aliased = True

    def __init__(self, fulls, hows):
        self.fulls, self.hows = list(fulls), list(hows)

    def inputs(self):
        return self.fulls

    def out_shapes(self):
        return [jax.ShapeDtypeStruct(a.shape, a.dtype) for a in self.fulls]

    def n_sems(self):
        return 6 * len(self.fulls)

    def build(self, ins, outs, sems, base):
        x, y, c = _place()
        starts, waits = [], []
        for a, how in enumerate(self.hows):
            for j, (px, py) in enumerate(_other_chips(x, y)):
                s = base + 6 * a + 2 * j
                got = _window(outs[a], how, 2 * px + py, c)
                cp = _remote(got, got, sems, s, (x, y, 1 - c))
                landing = _remote(got, _window(outs[a], how, 2 * px + py, 1 - c), sems, s, (x, y, 1 - c))
                starts.append(cp.start)
                waits += [landing.wait_recv, cp.wait_send]
        return starts, waits


def _part_shape(a, how):
    if how == "all":
        return a.shape
    assert how == "rows"
    return (a.shape[0] // N_CHIPS, a.shape[1])


class _Scatter:
    aliased = False

    def __init__(self, fulls, hows, which=(0, 1, 2)):
        self.fulls, self.hows, self.which = list(fulls), list(hows), tuple(which)

    def inputs(self):
        return self.fulls

    def out_shapes(self):
        return [jax.ShapeDtypeStruct((len(self.which),) + _part_shape(a, h), a.dtype) for a, h in zip(self.fulls, self.hows)]

    def n_sems(self):
        return 6 * len(self.fulls)

    def build(self, ins, outs, sems, base):
        x, y, c = _place()
        chips = _other_chips(x, y)
        starts, waits = [], []
        for a, how in enumerate(self.hows):
            for slot, j in enumerate(self.which):
                px, py = chips[j]
                cp = _remote(_window(ins[a], how, 2 * px + py), outs[a].at[slot], sems, base + 6 * a + 2 * j, (px, py, c))
                starts.append(cp.start)
                waits += [cp.wait_recv, cp.wait_send]
        return starts, waits


class _Swap:
    aliased = False

    def __init__(self, arrays):
        self.arrays = list(arrays)

    def inputs(self):
        return self.arrays

    def out_shapes(self):
        return [jax.ShapeDtypeStruct(a.shape, a.dtype) for a in self.arrays]

    def n_sems(self):
        return 2 * len(self.arrays)

    def build(self, ins, outs, sems, base):
        x, y, c = _place()
        starts, waits = [], []
        for a in range(len(ins)):
            cp = _remote(ins[a], outs[a], sems, base + 2 * a, (x, y, 1 - c))
            starts.append(cp.start)
            waits += [cp.wait_recv, cp.wait_send]
        return starts, waits


def _call(name, body, grid, in_specs, out_specs, out_shape, args, scratch=(), comm=(), after=()):
    comm, after = list(comm), list(after)
    n_in, n_out, n_scr, n_after = len(args), len(out_shape), len(scratch), len(after)
    c_in = [a for op in comm for a in op.inputs()]
    c_out = [s for op in comm for s in op.out_shapes()]
    n_sems = sum(op.n_sems() for op in comm)
    aliases, i_in, i_out = {}, 0, 0
    for op in comm:
        if op.aliased:
            for q in range(len(op.inputs())):
                aliases[n_in + n_after + i_in + q] = n_out + i_out + q
        i_in, i_out = i_in + len(op.inputs()), i_out + len(op.out_shapes())

    def wrapped(*refs):
        ins = refs[:n_in]
        cin = refs[n_in + n_after : n_in + n_after + len(c_in)]
        o0 = n_in + n_after + len(c_in)
        outs = refs[o0 : o0 + n_out]
        cout = refs[o0 + n_out : o0 + n_out + len(c_out)]
        s0 = o0 + n_out + len(c_out)
        scr = refs[s0 : s0 + n_scr]

        def copies():
            sems = refs[s0 + n_scr]
            starts, waits = [], []
            i_in = i_out = base = 0
            for op in comm:
                ni, no = len(op.inputs()), len(op.out_shapes())
                s, w = op.build(cin[i_in : i_in + ni], cout[i_out : i_out + no], sems, base)
                starts += s
                waits += w
                i_in, i_out, base = i_in + ni, i_out + no, base + op.n_sems()
            return starts, waits

        def run_starts():
            for start in copies()[0]:
                start()

        def run_waits():
            for wait in copies()[1]:
                wait()

        if comm and grid:
            first = last = True
            for d, n in enumerate(grid):
                first = jnp.logical_and(first, pl.program_id(d) == 0)
                last = jnp.logical_and(last, pl.program_id(d) == n - 1)
            pl.when(first)(run_starts)
        elif comm:
            run_starts()
        if body is not None:
            body(*ins, *outs, *scr)
        if comm and grid:
            pl.when(last)(run_waits)
        elif comm:
            run_waits()

    res = pl.pallas_call(
        wrapped,
        name=name,
        grid=grid,
        in_specs=list(in_specs) + [ANY] * (n_after + len(c_in)),
        out_specs=list(out_specs) + [ANY] * len(c_out),
        out_shape=list(out_shape) + c_out,
        scratch_shapes=list(scratch) + ([pltpu.SemaphoreType.DMA((n_sems,))] if comm else []),
        input_output_aliases=aliases,
        compiler_params=pltpu.CompilerParams(dimension_semantics=("arbitrary",) * len(grid), vmem_limit_bytes=VMEM_LIMIT),
    )(*args, *after, *c_in)
    return tuple(res[:n_out]), tuple(res[n_out:])


def _place_and_gather(now, later):
    items = list(now) + list(later)
    n, n_now = len(items), len(now)
    buf_shape = lambda it: it[0].shape[::-1] if it[4] else it[0].shape
    split_now = [a for a in range(n_now) if items[a][5]]

    def body(*refs):
        ins, outs = refs[:n], refs[n : 2 * n]
        stage, bufs = refs[2 * n : 3 * n - n_now], refs[3 * n - n_now : 4 * n - n_now]
        sems = refs[4 * n - n_now]
        x, y, c = _place()
        me = 2 * x + y
        chips = _other_chips(x, y)
        loads = [pltpu.make_async_copy(ins[a], stage[a - n_now], sems.at[a]) for a in range(n_now, n)]
        for ld in loads:
            ld.start()
        pending = []

        def place(a, val):
            _, how, _, dtype, transposed, _ = items[a]
            bufs[a][...] = (val.T if transposed else val).astype(dtype)
            cp = pltpu.make_async_copy(bufs[a], _window(outs[a], how, me), sems.at[n + a])
            cp.start()
            pending.append(cp.wait)

        arrivals = []
        for a in range(n_now):
            place(a, ins[a][...])
            how, split = items[a][1], items[a][5]
            half = c if split else None
            src = _rows(bufs[a], c * (bufs[a].shape[0] // 2), bufs[a].shape[0] // 2) if split else bufs[a]
            for j, (px, py) in enumerate(chips):
                s = 2 * n + 6 * a + 2 * j
                cp = _remote(src, _window(outs[a], how, me, half), sems, s, (px, py, c))
                landing = _remote(src, _window(outs[a], how, 2 * px + py, half), sems, s, (px, py, c))
                cp.start()
                arrivals.append(landing.wait_recv)
                pending.append(cp.wait_send)
        for a in range(n_now, n):
            loads[a - n_now].wait()
            place(a, stage[a - n_now][...])
        for wait in arrivals:
            wait()
        d2d = _GatherD2d([None] * len(split_now), [items[a][1] for a in split_now])
        starts, waits = d2d.build(None, [outs[a] for a in split_now], sems, 2 * n + 6 * n_now)
        for start in starts:
            start()
        for wait in waits + pending:
            wait()

    vm = pl.BlockSpec(memory_space=pltpu.VMEM)
    return pl.pallas_call(
        body,
        name="place_and_gather",
        in_specs=[vm] * n_now + [ANY] * (n - n_now),
        out_specs=[ANY] * n,
        out_shape=[jax.ShapeDtypeStruct(it[2], it[3]) for it in items],
        scratch_shapes=[pltpu.VMEM(it[0].shape, it[0].dtype) for it in later]
        + [pltpu.VMEM(buf_shape(it), it[3]) for it in items]
        + [pltpu.SemaphoreType.DMA((2 * n + 6 * n_now + 6 * len(split_now),))],
        compiler_params=pltpu.CompilerParams(vmem_limit_bytes=VMEM_LIMIT),
    )(*[it[0] for it in items])


_HBM = pl.BlockSpec(memory_space=pltpu.HBM)
_SEM = pl.BlockSpec(memory_space=pltpu.SEMAPHORE)
_DATAFLOW = pltpu.SideEffectType.DATAFLOW_SIDE_EFFECTING


class _Pending:
    def __init__(self, ops, bases, sems, arrays, token):
        self.ops, self.bases, self.sems, self.arrays, self.token = ops, bases, sems, arrays, token


def _op_refs(op, refs):
    n_src = len(op.inputs())
    return refs[:n_src], (refs[:n_src] if op.aliased else refs[n_src:])


def _start(name, ops, sibling_only=False):
    per_op = [list(op.inputs()) + ([] if op.aliased else [lax.empty(sd.shape, sd.dtype) for sd in op.out_shapes()])
              for op in ops]
    arrays = [a for group in per_op for a in group]
    bases = [sum(op.n_sems() for op in ops[:k]) for k in range(len(ops))]
    n = len(arrays)

    def body(*refs):
        sems, token = refs[n], refs[-1]
        if sibling_only:
            x, y, c = _place()
            barrier = pltpu.get_barrier_semaphore()
            pl.semaphore_signal(barrier, inc=1, device_id=(x, y, 1 - c), device_id_type=MESH)
            pl.semaphore_wait(barrier, 1)
        at = 0
        for op, group, base in zip(ops, per_op, bases):
            starts, _ = op.build(*_op_refs(op, refs[at : at + len(group)]), sems, base)
            for start in starts:
                start()
            at += len(group)
        token[...] = jnp.zeros_like(token)

    res = pl.pallas_call(
        body,
        name=name,
        out_shape=(pltpu.SemaphoreType.DMA((sum(op.n_sems() for op in ops),)),)
        + tuple(pltpu.HBM(a.shape, a.dtype) for a in arrays) + (jax.ShapeDtypeStruct((SUBLANES, LANES), F32),),
        in_specs=(_HBM,) * n,
        out_specs=(_SEM,) + (_HBM,) * n + (pl.BlockSpec(memory_space=pltpu.VMEM),),
        input_output_aliases={i: 1 + i for i in range(n)},
        compiler_params=pltpu.CompilerParams(
            has_side_effects=_DATAFLOW, collective_id=SIBLING_BARRIER_ID if sibling_only else None),
    )(*[pltpu.with_memory_space_constraint(a, pltpu.HBM) for a in arrays])
    thru, at, groups = list(res[1 : 1 + n]), 0, []
    for group in per_op:
        groups.append(thru[at : at + len(group)])
        at += len(group)
    return _Pending(list(ops), bases, res[0], groups, res[-1])


def _wait(name, pending, k, after):
    op, arrays = pending.ops[k], pending.arrays[k]
    n = len(arrays)

    def body(*refs):
        _, waits = op.build(*_op_refs(op, refs[:n]), refs[n], pending.bases[k])
        for wait in waits:
            wait()

    return pl.pallas_call(
        body,
        name=name,
        out_shape=tuple(pltpu.HBM(a.shape, a.dtype) for a in arrays),
        in_specs=(_HBM,) * n + (_SEM, ANY),
        out_specs=(_HBM,) * n,
        input_output_aliases={i: i for i in range(n)},
        compiler_params=pltpu.CompilerParams(has_side_effects=_DATAFLOW),
    )(*arrays, pending.sems, after)


def _in_proj(x, g_mix, w_inT_b, b_in, after=()):
    T, D = x.shape
    CI = w_inT_b.shape[0]
    tm = _tile(T, 512)

    def body(x_ref, g_ref, w_ref, b_ref, z_ref, xn_ref):
        xv = x_ref[...]
        r = lax.rsqrt(jnp.mean(xv * xv, axis=-1, keepdims=True) + RMS_EPS)
        xn = (xv * r * g_ref[...]).astype(BF16)
        xn_ref[...] = xn
        z_ref[...] = _dot(xn, w_ref[...], NT) + b_ref[...]

    return _call(
        "in_proj",
        body,
        (T // tm,),
        [
            pl.BlockSpec((tm, D), lambda i: (i, 0)),
            pl.BlockSpec((1, D), lambda i: (0, 0)),
            pl.BlockSpec((CI, D), lambda i: (0, 0)),
            pl.BlockSpec((1, CI), lambda i: (0, 0)),
        ],
        [pl.BlockSpec((tm, CI), lambda i: (i, 0)), pl.BlockSpec((tm, D), lambda i: (i, 0))],
        [jax.ShapeDtypeStruct((T, CI), F32), jax.ShapeDtypeStruct((T, D), BF16)],
        (x, g_mix, w_inT_b, b_in),
        after=after,
    )


def _fill_shifted(scr):
    n = scr.shape[1] - SUBLANES
    for s in range(1, SUBLANES):
        scr[s, 0:n, :] = scr[0, s : s + n, :]


def _shifted_rows(scr, off, n, cs):
    s = off % SUBLANES
    return scr[s, off - s : off - s + n, cs]


def _pool_mean_minus_token(p_scr, cs, w, cnt, tt):
    tok = p_scr[HALO : HALO + tt, cs]
    s = tok
    for d in range(1, w):
        s = s + p_scr[HALO - d : HALO - d + tt, cs]
    return s / cnt - tok


def _seq_fwd(z, w_dw4, b_dw, ln_g, ln_b, w_pool_b, s_pool, after=()):
    T, CI = z.shape
    CC = ln_g.shape[1]
    n_grp, G = w_pool_b.shape[0], w_pool_b.shape[-1]
    KW = w_dw4.shape[1]
    D = CC + n_grp * G
    tt = _tile(T, 512, HALO)
    per = tt // HALO

    def body(zc_ref, zp_ref, wdw_ref, bdw_ref, lng_ref, lnb_ref, wp_ref, sp_ref, y_ref, v_ref, u_scr, p_scr):
        i = pl.program_id(0)
        first = i == 0
        u_prev = zp_ref[:, 0:CC] * _sigmoid(zp_ref[:, CC : 2 * CC])
        u_scr[0, 0:HALO, :] = jnp.where(first, 0.0, u_prev)
        p_scr[0:HALO, :] = jnp.where(first, 0.0, zp_ref[:, 2 * CC :])
        u_scr[0, HALO:, :] = zc_ref[:, 0:CC] * _sigmoid(zc_ref[:, CC : 2 * CC])
        p_scr[HALO:, :] = zc_ref[:, 2 * CC :]
        _fill_shifted(u_scr)

        for j in range(CC // LANES):
            cs = slice(LANES * j, LANES * (j + 1))
            for rb in range(tt // CONV_ROWS):
                acc = jnp.zeros((CONV_ROWS, LANES), F32)
                for k in range(KW):
                    off = HALO - (KW - 1) + k + rb * CONV_ROWS
                    acc = acc + _shifted_rows(u_scr, off, CONV_ROWS, cs) * wdw_ref[j, k]
                v_ref[rb * CONV_ROWS : (rb + 1) * CONV_ROWS, cs] = acc + bdw_ref[:, cs]

        v = v_ref[...]
        mu = jnp.mean(v, axis=-1, keepdims=True)
        d = v - mu
        var = jnp.mean(d * d, axis=-1, keepdims=True)
        ln = d * lax.rsqrt(var + LN_EPS) * lng_ref[...] + lnb_ref[...]
        y_ref[:, 0:CC] = (ln * _sigmoid(ln)).astype(BF16)

        tpos = i * tt + lax.broadcasted_iota(jnp.int32, (tt, 1), 0)
        for gi, w in enumerate(POOL_WINDOWS):
            cs = slice(G * gi, G * (gi + 1))
            cnt = jnp.minimum(tpos + 1, w).astype(F32)
            yi = _pool_mean_minus_token(p_scr, cs, w, cnt, tt)
            q = _dot(yi.astype(BF16), wp_ref[gi], NN)
            y_ref[:, CC + G * gi : CC + G * (gi + 1)] = (q * sp_ref[:, cs]).astype(BF16)

    const2 = lambda i: (0, 0)
    return _call(
        "seq_fwd",
        body,
        (T // tt,),
        [
            pl.BlockSpec((tt, CI), lambda i: (i, 0)),
            pl.BlockSpec((HALO, CI), lambda i: (jnp.maximum(i * per - 1, 0), 0)),
            pl.BlockSpec(w_dw4.shape, lambda i: (0,) * w_dw4.ndim),
            pl.BlockSpec((1, CC), const2),
            pl.BlockSpec((1, CC), const2),
            pl.BlockSpec((1, CC), const2),
            pl.BlockSpec(w_pool_b.shape, lambda i: (0, 0, 0)),
            pl.BlockSpec((1, n_grp * G), const2),
        ],
        [pl.BlockSpec((tt, D), lambda i: (i, 0)), pl.BlockSpec((tt, CC), lambda i: (i, 0))],
        [jax.ShapeDtypeStruct((T, D), BF16), jax.ShapeDtypeStruct((T, CC), F32)],
        (z, z, w_dw4, b_dw, ln_g, ln_b, w_pool_b, s_pool),
        scratch=[pltpu.VMEM((SUBLANES, HALO + tt, CC), F32), pltpu.VMEM((HALO + tt, n_grp * G), F32)],
        after=after,
    )


def _out_proj(y_b, x, w_out_b, g_ffn, after=()):
    T, D = x.shape
    tm = _tile(T, 512)

    def body(y_ref, x_ref, w_ref, g_ref, h1_ref, hn_ref):
        h1 = x_ref[...] + _dot(y_ref[...], w_ref[...], NN)
        h1_ref[...] = h1
        r = lax.rsqrt(jnp.mean(h1 * h1, axis=-1, keepdims=True) + RMS_EPS)
        hn_ref[...] = (h1 * r * g_ref[...]).astype(BF16)

    row = lambda i: (i, 0)
    return _call(
        "out_proj",
        body,
        (T // tm,),
        [
            pl.BlockSpec((tm, y_b.shape[1]), row),
            pl.BlockSpec((tm, D), row),
            pl.BlockSpec(w_out_b.shape, lambda i: (0, 0)),
            pl.BlockSpec((1, D), lambda i: (0, 0)),
        ],
        [pl.BlockSpec((tm, D), row), pl.BlockSpec((tm, D), row)],
        [jax.ShapeDtypeStruct((T, D), F32), jax.ShapeDtypeStruct((T, D), BF16)],
        (y_b, x, w_out_b, g_ffn),
        after=after,
    )


def _hidden_tile(F):
    return _tile(F, 1408, LANES)


def _gate_up(hn_b, wgT_b, wuT_b):
    T, D = hn_b.shape
    F = wgT_b.shape[0]
    tm, tf = _tile(T, 1024), _hidden_tile(F)

    def body(hn_ref, wg_ref, wu_ref, g_ref, u_ref, a_ref):
        hn = hn_ref[...]
        for c0 in range(0, tf, HIDDEN_CHUNK):
            cs = slice(c0, min(c0 + HIDDEN_CHUNK, tf))
            gv = _dot(hn, wg_ref[cs, :], NT)
            uv = _dot(hn, wu_ref[cs, :], NT)
            g_ref[:, cs] = gv.astype(BF16)
            u_ref[:, cs] = uv.astype(BF16)
            a_ref[:, cs] = (gv * _sigmoid(gv) * uv).astype(BF16)

    wspec = pl.BlockSpec((tf, D), lambda j, i: (j, 0))
    ospec = pl.BlockSpec((tm, tf), lambda j, i: (i, j))
    return _call(
        "gate_up",
        body,
        (F // tf, T // tm),
        [pl.BlockSpec((tm, D), lambda j, i: (i, 0)), wspec, wspec],
        [ospec, ospec, ospec],
        [jax.ShapeDtypeStruct((T, F), BF16)] * 3,
        (hn_b, wgT_b, wuT_b),
    )


def _down_loss(a_b, wd_b, h1, target, g_final):
    T, D = h1.shape
    F = a_b.shape[1]
    tm = _tile(T, 512)
    nt = T // tm

    def body(a_ref, w_ref, h1_ref, t_ref, g_ref, dh2_ref, dh2b_ref, loss_ref, dg_ref):
        i = pl.program_id(0)
        h2 = h1_ref[...] + _dot(a_ref[...], w_ref[...], NN)
        r = lax.rsqrt(jnp.mean(h2 * h2, axis=-1, keepdims=True) + RMS_EPS)
        g = g_ref[...]
        diff = h2 * r * g - t_ref[...]
        _accumulate(loss_ref, i == 0, jnp.full(loss_ref.shape, jnp.sum(diff * diff) * (0.5 / D), F32))
        dh2, dg_rows = _rms_bwd(h2, g, diff * (1.0 / D))
        dh2_ref[...] = dh2
        dh2b_ref[...] = dh2.astype(BF16)
        _accumulate(dg_ref, i == 0, jnp.sum(dg_rows, axis=0, keepdims=True))

    row = lambda i: (i, 0)
    return _call(
        "down_loss",
        body,
        (nt,),
        [
            pl.BlockSpec((tm, F), row),
            pl.BlockSpec((F, D), lambda i: (0, 0), pipeline_mode=pl.Buffered(1)),
            pl.BlockSpec((tm, D), row),
            pl.BlockSpec((tm, D), row),
            pl.BlockSpec((1, D), lambda i: (0, 0)),
        ],
        [
            pl.BlockSpec((tm, D), row),
            pl.BlockSpec((tm, D), row),
            pl.BlockSpec((1, LANES), lambda i: (0, 0)),
            pl.BlockSpec((1, D), lambda i: (0, 0)),
        ],
        [
            jax.ShapeDtypeStruct((T, D), F32),
            jax.ShapeDtypeStruct((T, D), BF16),
            jax.ShapeDtypeStruct((1, LANES), F32),
            jax.ShapeDtypeStruct((1, D), F32),
        ],
        (a_b, wd_b, h1, target, g_final),
    )


def _ffn_bwd_act(dh2_b, wd_b, g_b, u_b, comm=()):
    T, D = dh2_b.shape
    F = wd_b.shape[0]
    tm, tf = _tile(T, 1024), _hidden_tile(F)

    def body(d_ref, w_ref, g_ref, u_ref, dg_ref, du_ref):
        d = d_ref[...]
        for c0 in range(0, tf, HIDDEN_CHUNK):
            cs = slice(c0, min(c0 + HIDDEN_CHUNK, tf))
            da = _dot(d, w_ref[cs, :], NT)
            gv = g_ref[:, cs].astype(F32)
            uv = u_ref[:, cs].astype(F32)
            sg = _sigmoid(gv)
            silu = gv * sg
            dg_ref[:, cs] = (da * uv * (sg * (1.0 + gv * (1.0 - sg)))).astype(BF16)
            du_ref[:, cs] = (da * silu).astype(BF16)

    aspec = pl.BlockSpec((tm, tf), lambda j, i: (i, j))
    return _call(
        "ffn_bwd_act",
        body,
        (F // tf, T // tm),
        [pl.BlockSpec((tm, D), lambda j, i: (i, 0)), pl.BlockSpec((tf, D), lambda j, i: (j, 0)), aspec, aspec],
        [aspec, aspec],
        [jax.ShapeDtypeStruct((T, F), BF16)] * 2,
        (dh2_b, wd_b, g_b, u_b),
        comm=comm,
    )


def _ffn_bwd_in(dg_b, du_b, wgT_b, wuT_b, h1, dh2, g_ffn, w_out_b, comm=()):
    T, D = h1.shape
    F = wgT_b.shape[0]
    DM = w_out_b.shape[0]
    tm = _tile(T, 512)

    def body(dg_ref, du_ref, wg_ref, wu_ref, h1_ref, dh2_ref, g_ref, wo_ref, dh1_ref, dh1b_ref, dy_ref, dgf_ref):
        i = pl.program_id(0)
        dhn = _dot(dg_ref[...], wg_ref[...], NN) + _dot(du_ref[...], wu_ref[...], NN)
        dx, dg_rows = _rms_bwd(h1_ref[...], g_ref[...], dhn)
        dh1 = dh2_ref[...] + dx
        dh1b = dh1.astype(BF16)
        dh1_ref[...] = dh1
        dh1b_ref[...] = dh1b
        dy_ref[...] = _dot(dh1b, wo_ref[...], NT)
        _accumulate(dgf_ref, i == 0, jnp.sum(dg_rows, axis=0, keepdims=True))

    row = lambda i: (i, 0)
    const = lambda i: (0, 0)
    return _call(
        "ffn_bwd_in",
        body,
        (T // tm,),
        [
            pl.BlockSpec((tm, F), row),
            pl.BlockSpec((tm, F), row),
            pl.BlockSpec((F, D), const, pipeline_mode=pl.Buffered(1)),
            pl.BlockSpec((F, D), const, pipeline_mode=pl.Buffered(1)),
            pl.BlockSpec((tm, D), row),
            pl.BlockSpec((tm, D), row),
            pl.BlockSpec((1, D), const),
            pl.BlockSpec((DM, D), const, pipeline_mode=pl.Buffered(1)),
        ],
        [pl.BlockSpec((tm, D), row), pl.BlockSpec((tm, D), row), pl.BlockSpec((tm, DM), row), pl.BlockSpec((1, D), const)],
        [
            jax.ShapeDtypeStruct((T, D), F32),
            jax.ShapeDtypeStruct((T, D), BF16),
            jax.ShapeDtypeStruct((T, DM), F32),
            jax.ShapeDtypeStruct((1, D), F32),
        ],
        (dg_b, du_b, wgT_b, wuT_b, h1, dh2, g_ffn, w_out_b),
        comm=comm,
    )


def _seq_bwd(z, dy, v, w_dw4, ln_g, ln_b, w_pool_b, s_pool, comm=()):
    T, CI = z.shape
    CC = ln_g.shape[1]
    n_grp, G = w_pool_b.shape[0], w_pool_b.shape[-1]
    CP = n_grp * G
    KW = w_dw4.shape[1]
    n_cc = CC // LANES
    D = CC + CP
    tt = _tile(T, 512, HALO)
    per = tt // HALO
    n_tiles = T // tt
    last_halo = T // HALO - 1

    def body(zc_ref, zp_ref, dyc_ref, dyn_ref, vc_ref, vn_ref, wdw_ref, lng_ref, lnb_ref, wp_ref, sp_ref,
             dz_ref, dwdw_ref, dbdw_ref, dlng_ref, dlnb_ref, dwp_ref, dsp_ref, dbin_ref,
             dv_scr, u_scr, p_scr, g_scr, dw_scr):
        i = pl.program_id(0)
        first = i == 0
        last = i == n_tiles - 1
        lng, lnb = lng_ref[...], lnb_ref[...]

        def conv_pre(vv, dyc):
            mu = jnp.mean(vv, axis=-1, keepdims=True)
            d = vv - mu
            rs = lax.rsqrt(jnp.mean(d * d, axis=-1, keepdims=True) + LN_EPS)
            xh = d * rs
            ln = xh * lng + lnb
            sg = _sigmoid(ln)
            dln = dyc * (sg * (1.0 + ln * (1.0 - sg)))
            dxh = dln * lng
            dv = rs * (dxh - jnp.mean(dxh, axis=-1, keepdims=True) - xh * jnp.mean(dxh * xh, axis=-1, keepdims=True))
            return dv, dln, xh

        dv_c, dln_c, xh_c = conv_pre(vc_ref[...], dyc_ref[:, 0:CC])
        dv_scr[0, 0:tt, :] = dv_c
        dv_n, _, _ = conv_pre(vn_ref[...], dyn_ref[:, 0:CC])
        dv_scr[0, tt:, :] = jnp.where(last, 0.0, dv_n)
        _fill_shifted(dv_scr)
        _accumulate(dlng_ref, first, jnp.sum(dln_c * xh_c, axis=0, keepdims=True))
        _accumulate(dlnb_ref, first, jnp.sum(dln_c, axis=0, keepdims=True))
        _accumulate(dbdw_ref, first, jnp.sum(dv_c, axis=0, keepdims=True))

        u_scr[...] = zc_ref[:, 0:CC] * _sigmoid(zc_ref[:, CC : 2 * CC])

        @pl.when(first)
        def _():
            dw_scr[...] = jnp.zeros_like(dw_scr)

        for j in range(n_cc):
            cs = slice(LANES * j, LANES * (j + 1))
            gs = slice(CC + LANES * j, CC + LANES * (j + 1))
            dbin_a = jnp.zeros((1, LANES), F32)
            dbin_g = jnp.zeros((1, LANES), F32)
            for rb in range(tt // CONV_ROWS):
                rows = slice(rb * CONV_ROWS, (rb + 1) * CONV_ROWS)
                u_blk = u_scr[rows, cs]
                du = jnp.zeros((CONV_ROWS, LANES), F32)
                for k in range(KW):
                    off = rb * CONV_ROWS + (KW - 1) - k
                    d = _shifted_rows(dv_scr, off, CONV_ROWS, cs)
                    du = du + d * wdw_ref[j, k]
                    dw_scr[j * HALO + k] += jnp.sum((u_blk * d).reshape(CONV_ROWS // 8, 8, LANES), axis=0)
                a = zc_ref[rows, cs]
                sg = _sigmoid(zc_ref[rows, gs])
                da = du * sg
                dgate = du * a * sg * (1.0 - sg)
                dz_ref[rows, cs] = da.astype(BF16)
                dz_ref[rows, gs] = dgate.astype(BF16)
                dbin_a = dbin_a + jnp.sum(da, axis=0, keepdims=True)
                dbin_g = dbin_g + jnp.sum(dgate, axis=0, keepdims=True)
            _accumulate(dbin_ref.at[:, cs], first, dbin_a)
            _accumulate(dbin_ref.at[:, gs], first, dbin_g)

        @pl.when(last)
        def _():
            dwdw_ref[...] = jnp.sum(dw_scr[...], axis=1).reshape(dwdw_ref.shape)

        p_scr[0:HALO, :] = jnp.where(first, 0.0, zp_ref[:, 2 * CC :])
        p_scr[HALO:, :] = zc_ref[:, 2 * CC :]
        tpos = i * tt + lax.broadcasted_iota(jnp.int32, (tt, 1), 0)
        for gi, w in enumerate(POOL_WINDOWS):
            cs = slice(G * gi, G * (gi + 1))
            ys = slice(CC + G * gi, CC + G * (gi + 1))
            ps = slice(2 * CC + G * gi, 2 * CC + G * (gi + 1))
            cnt = jnp.minimum(tpos + 1, w).astype(F32)
            yib = _pool_mean_minus_token(p_scr, cs, w, cnt, tt).astype(BF16)
            wp = wp_ref[gi]
            sp = sp_ref[:, cs]
            dyp = dyc_ref[:, ys]
            q = _dot(yib, wp, NN)
            _accumulate(dsp_ref.at[:, cs], first, jnp.sum(dyp * q, axis=0, keepdims=True))
            dq_c = (dyp * sp).astype(BF16)
            dq_n = (jnp.where(last, 0.0, dyn_ref[:, ys]) * sp).astype(BF16)
            _accumulate(dwp_ref.at[gi], first, _dot(yib, dq_c, TN))
            dyi_c = _dot(dq_c, wp, NT)
            g_scr[0:tt, cs] = dyi_c / cnt
            g_scr[tt:, cs] = _dot(dq_n, wp, NT) * (1.0 / w)
            dp = -dyi_c
            for d in range(w):
                dp = dp + g_scr[d : d + tt, cs]
            dz_ref[:, ps] = dp.astype(BF16)
            _accumulate(dbin_ref.at[:, ps], first, jnp.sum(dp, axis=0, keepdims=True))

    cur = lambda i: (i, 0)
    prev = lambda i: (jnp.maximum(i * per - 1, 0), 0)
    nxt = lambda i: (jnp.minimum((i + 1) * per, last_halo), 0)
    c2 = lambda i: (0, 0)
    c3 = lambda i: (0, 0, 0)
    return _call(
        "seq_bwd",
        body,
        (n_tiles,),
        [
            pl.BlockSpec((tt, CI), cur),
            pl.BlockSpec((HALO, CI), prev),
            pl.BlockSpec((tt, D), cur),
            pl.BlockSpec((HALO, D), nxt),
            pl.BlockSpec((tt, CC), cur),
            pl.BlockSpec((HALO, CC), nxt),
            pl.BlockSpec(w_dw4.shape, lambda i: (0,) * w_dw4.ndim),
            pl.BlockSpec((1, CC), c2),
            pl.BlockSpec((1, CC), c2),
            pl.BlockSpec(w_pool_b.shape, c3),
            pl.BlockSpec((1, CP), c2),
        ],
        [
            pl.BlockSpec((tt, CI), cur),
            pl.BlockSpec((n_cc, HALO, LANES), c3),
            pl.BlockSpec((1, CC), c2),
            pl.BlockSpec((1, CC), c2),
            pl.BlockSpec((1, CC), c2),
            pl.BlockSpec((n_grp, G, G), c3),
            pl.BlockSpec((1, CP), c2),
            pl.BlockSpec((1, CI), c2),
        ],
        [
            jax.ShapeDtypeStruct((T, CI), BF16),
            jax.ShapeDtypeStruct((n_cc, HALO, LANES), F32),
            jax.ShapeDtypeStruct((1, CC), F32),
            jax.ShapeDtypeStruct((1, CC), F32),
            jax.ShapeDtypeStruct((1, CC), F32),
            jax.ShapeDtypeStruct((n_grp, G, G), F32),
            jax.ShapeDtypeStruct((1, CP), F32),
            jax.ShapeDtypeStruct((1, CI), F32),
        ],
        (z, z, dy, dy, v, v, w_dw4, ln_g, ln_b, w_pool_b, s_pool),
        scratch=[
            pltpu.VMEM((SUBLANES, tt + HALO, CC), F32),
            pltpu.VMEM((tt, CC), F32),
            pltpu.VMEM((HALO + tt, CP), F32),
            pltpu.VMEM((tt + HALO, CP), F32),
            pltpu.VMEM((n_cc * HALO, 8, LANES), F32),
        ],
        comm=comm,
    )


def _in_proj_bwd(dz_b, w_inT_b, x, dh1, g_mix, after=()):
    T, D = x.shape
    CI = w_inT_b.shape[0]
    tm = _tile(T, 512)

    def body(dz_ref, w_ref, x_ref, dh1_ref, g_ref, dx_ref, dg_ref):
        i = pl.program_id(0)
        dxn = _dot(dz_ref[...], w_ref[...], NN)
        dx, dg_rows = _rms_bwd(x_ref[...], g_ref[...], dxn)
        dx_ref[...] = dh1_ref[...] + dx
        _accumulate(dg_ref, i == 0, jnp.sum(dg_rows, axis=0, keepdims=True))

    row = lambda i: (i, 0)
    const = lambda i: (0, 0)
    return _call(
        "in_proj_bwd",
        body,
        (T // tm,),
        [
            pl.BlockSpec((tm, CI), row),
            pl.BlockSpec((CI, D), const),
            pl.BlockSpec((tm, D), row),
            pl.BlockSpec((tm, D), row),
            pl.BlockSpec((1, D), const),
        ],
        [pl.BlockSpec((tm, D), row), pl.BlockSpec((1, D), const)],
        [jax.ShapeDtypeStruct((T, D), F32), jax.ShapeDtypeStruct((1, D), F32)],
        (dz_b, w_inT_b, x, dh1, g_mix),
        after=after,
    )


def _weight_grad(name, a_b, b_b, comm=()):
    T, N1 = a_b.shape
    N2 = b_b.shape[1]
    t1 = _tile(N1, 1408, LANES)
    tk = _tile(T, 2048)
    nk = T // tk

    def body(a_ref, b_ref, o_ref, acc):
        k = pl.program_id(1)
        _accumulate(acc, k == 0, _dot(a_ref[...], b_ref[...], TN))

        @pl.when(k == nk - 1)
        def _():
            o_ref[...] = acc[...].astype(BF16)

    (out,), rest = _call(
        name,
        body,
        (N1 // t1, nk),
        [pl.BlockSpec((tk, t1), lambda n, k: (k, n)), pl.BlockSpec((tk, N2), lambda n, k: (k, 0))],
        [pl.BlockSpec((t1, N2), lambda n, k: (n, 0))],
        [jax.ShapeDtypeStruct((N1, N2), BF16)],
        (a_b, b_b),
        scratch=[pltpu.VMEM((t1, N2), F32)],
        comm=comm,
    )
    return out, rest


def _sum_parts(name, full, how, parts, me):
    _, R, C = parts[0].shape
    tr = _tile(R, SUM_ROWS, BF16_ROWS)
    nb = R // tr
    where = [(q, r) for q, p in enumerate(parts) for r in range(p.shape[0])]
    assert len(where) == 3

    def body(me_ref, own_ref, *refs):
        o_ref = refs[-1]
        f = lambda j: refs[where[j][0]][where[j][1]].astype(F32)
        o_ref[...] = (own_ref[...].astype(F32) + f(0)) + (f(1) + f(2))

    own_map = {"rows": lambda i, me_ref: (me_ref[0] * nb + i, 0), "all": lambda i, me_ref: (i, 0)}[how]
    return pl.pallas_call(
        body,
        name=name,
        grid_spec=pltpu.PrefetchScalarGridSpec(
            num_scalar_prefetch=1,
            grid=(nb,),
            in_specs=[pl.BlockSpec((tr, C), own_map)]
            + [pl.BlockSpec((p.shape[0], tr, C), lambda i, me_ref: (0, i, 0)) for p in parts],
            out_specs=pl.BlockSpec((tr, C), lambda i, me_ref: (i, 0)),
        ),
        out_shape=jax.ShapeDtypeStruct((R, C), F32),
        compiler_params=pltpu.CompilerParams(dimension_semantics=("arbitrary",), vmem_limit_bytes=VMEM_LIMIT),
    )(me, full, *parts)


_M_CORR = 1.0 - ADAM_B1**ADAM_STEP
_V_CORR = 1.0 - ADAM_B2**ADAM_STEP


def _adamw_math(w, g, m, v):
    m = ADAM_B1 * m + (1.0 - ADAM_B1) * g
    v = ADAM_B2 * v + (1.0 - ADAM_B2) * (g * g)
    delta = -ADAM_LR * ((m / _M_CORR) / (jnp.sqrt(v / _V_CORR) + ADAM_EPS) + ADAM_WD * w)
    return delta, m, v


def _adamw(name, w, m, v, g_here, g_there, g_transposed=False):
    R, C = w.shape
    tr = _tile(R, 256, LANES if g_transposed else 8)

    def body(w_ref, m_ref, v_ref, ga_ref, gb_ref, g_ref, d_ref, nm_ref, nv_ref):
        g = ga_ref[...] + gb_ref[...]
        if g_transposed:
            g = g.T
        g_ref[...] = g
        d_ref[...], nm_ref[...], nv_ref[...] = _adamw_math(w_ref[...], g, m_ref[...], v_ref[...])

    spec = pl.BlockSpec((tr, C), lambda i: (i, 0))
    gspec = pl.BlockSpec((C, tr), lambda i: (0, i)) if g_transposed else spec
    return _call(name, body, (R // tr,), [spec] * 3 + [gspec] * 2, [spec] * 4, [jax.ShapeDtypeStruct((R, C), F32)] * 4,
                 (w, m, v, g_here, g_there))


def _adamw_on_sparsecore(name, w, m, v, g_here, g_there, after):
    R, C = w.shape
    n_groups = R // SUBLANES
    n_turns = -(-n_groups // SC_TILES)
    n_in, n_out = 5, 4

    def body(w_hbm, m_hbm, v_hbm, ga_hbm, gb_hbm, after_hbm, g_out, d_out, nm_out, nv_out, bufs, sems):
        tile = lax.axis_index("subcore") * SC_CORES + lax.axis_index("sparsecore")
        srcs = (w_hbm, m_hbm, v_hbm, ga_hbm, gb_hbm)
        dsts = (d_out, nm_out, nv_out, g_out)

        def rows(turn):
            return pl.ds((tile + turn * SC_TILES) * SUBLANES, SUBLANES)

        def loads(turn):
            slot = turn % 2
            return [pltpu.make_async_copy(srcs[q].at[rows(turn), :], bufs.at[slot, q], sems.at[slot, q]) for q in range(n_in)]

        def stores(turn):
            slot = turn % 2
            return [pltpu.make_async_copy(bufs.at[slot, q], dsts[q].at[rows(turn), :], sems.at[slot, n_in + q])
                    for q in range(n_out)]

        def when_mine(turn, fn):
            pl.when(tile + turn * SC_TILES < n_groups)(fn)

        def compute(slot):
            wb, mb, vb, gab, gbb = (bufs.at[slot, q] for q in range(n_in))

            @pl.loop(0, SUBLANES)
            def _(r):
                @pl.loop(0, C, step=SC_LANES)
                def _(i):
                    at = (r, pl.ds(i, SC_LANES))
                    g = gab[at] + gbb[at]
                    delta, new_m, new_v = _adamw_math(wb[at], g, mb[at], vb[at])
                    gab[at], wb[at], mb[at], vb[at] = g, delta, new_m, new_v

        def start_loads(turn):
            def fn():
                for cp in loads(turn):
                    cp.start()

            when_mine(turn, fn)

        start_loads(0)
        for turn in range(n_turns):
            def step(turn=turn):
                for cp in loads(turn):
                    cp.wait()
                if turn >= 1:
                    for cp in stores(turn - 1):
                        cp.wait()
                if turn + 1 < n_turns:
                    start_loads(turn + 1)
                compute(turn % 2)
                for cp in stores(turn):
                    cp.start()

            when_mine(turn, step)
        for turn in range(n_turns):
            def drain(turn=turn):
                for cp in stores(turn):
                    cp.wait()

            last_mine = jnp.logical_and(tile + turn * SC_TILES < n_groups, tile + (turn + 1) * SC_TILES >= n_groups)
            pl.when(last_mine)(drain)

    return pl.kernel(
        body,
        name=name,
        out_type=[jax.ShapeDtypeStruct((R, C), F32)] * 4,
        mesh=plsc.VectorSubcoreMesh(core_axis_name="sparsecore", subcore_axis_name="subcore"),
        scratch_types=[pltpu.VMEM((2, n_in, SUBLANES, C), F32), pltpu.SemaphoreType.DMA((2, n_in + n_out))],
        compiler_params=pltpu.CompilerParams(use_tc_tiling_on_sc=True),
    )(w, m, v, g_here, g_there, after)


class _PackLayout:
    def __init__(self, n_cc, n_grp, G, widths):
        self.dw_rows = (0, HALO)
        self.wp_rows = (HALO, HALO + G)
        self.n_cc, self.n_grp, self.G = n_cc, n_grp, G
        self.vec = {}
        r = HALO + G
        for name, width in widths:
            self.vec[name] = (r, width)
            r += width // PACK_W
        self.rows = -(-r // 8) * 8


def _pack_small(layout, dwdw, dwp, vecs):
    names = list(vecs)

    def body(*refs):
        dw_ref, wp_ref = refs[0], refs[1]
        vec_refs = refs[2 : 2 + len(names)]
        o_ref = refs[-1]
        o_ref[...] = jnp.zeros_like(o_ref)
        for j in range(layout.n_cc):
            o_ref[layout.dw_rows[0] : layout.dw_rows[1], j * LANES : (j + 1) * LANES] = dw_ref[j]
        for i in range(layout.n_grp):
            o_ref[layout.wp_rows[0] : layout.wp_rows[1], i * layout.G : (i + 1) * layout.G] = wp_ref[i]
        for name, ref in zip(names, vec_refs):
            r, width = layout.vec[name]
            for h in range(width // PACK_W):
                o_ref[r + h : r + h + 1, :] = ref[:, h * PACK_W : (h + 1) * PACK_W]

    return pl.pallas_call(
        body,
        name="pack_small",
        out_shape=jax.ShapeDtypeStruct((layout.rows, PACK_W), F32),
    )(dwdw, dwp, *[vecs[k] for k in names])


def _adamw_small(layout, g_here, g_there, w_dw, m_dw, v_dw, w_pool, m_pool, v_pool, vec_w, vec_m, vec_v):
    names = list(vec_w)
    nv = len(names)

    def body(*refs):
        ga_ref, gb_ref = refs[0], refs[1]
        wdw, mdw, vdw, wp, mp, vp = refs[2:8]
        vw, vm, vv = refs[8 : 8 + nv], refs[8 + nv : 8 + 2 * nv], refs[8 + 2 * nv : 8 + 3 * nv]
        outs = refs[8 + 3 * nv :]
        acc = outs[-1]
        acc[...] = ga_ref[...] + gb_ref[...]

        def emit(o, g, w, m, v, idx=()):
            res = (g,) + _adamw_math(w, g, m, v)
            for ref, val in zip(o, res):
                ref[idx] = val

        me = 2 * lax.axis_index("x") + lax.axis_index("y")
        for j in range(layout.n_cc):

            @pl.when(me == j)
            def _(j=j):
                for k in range(wdw.shape[0]):
                    g = acc[layout.dw_rows[0] + k : layout.dw_rows[0] + k + 1, j * LANES : (j + 1) * LANES]
                    emit(outs[0:4], g, wdw[k], mdw[k], vdw[k], idx=k)

        for i in range(layout.n_grp):
            g = acc[layout.wp_rows[0] : layout.wp_rows[1], i * layout.G : (i + 1) * layout.G]
            emit(outs[4:8], g, wp[i], mp[i], vp[i], idx=i)
        for q, name in enumerate(names):
            r, width = layout.vec[name]
            for h in range(width // PACK_W):
                ls = slice(h * PACK_W, (h + 1) * PACK_W)
                g = acc[r + h : r + h + 1, :]
                emit(outs[8 + 4 * q : 12 + 4 * q], g, vw[q][:, ls], vm[q][:, ls], vv[q][:, ls], idx=(slice(None), ls))

    shapes = [w_dw.shape] * 4 + [w_pool.shape] * 4
    for name in names:
        shapes += [vec_w[name].shape] * 4
    return pl.pallas_call(
        body,
        name="adamw_small",
        out_shape=[jax.ShapeDtypeStruct(s, F32) for s in shapes],
        scratch_shapes=[pltpu.VMEM(g_here.shape, F32)],
    )(g_here, g_there, w_dw, m_dw, v_dw, w_pool, m_pool, v_pool,
      *[vec_w[k] for k in names], *[vec_m[k] for k in names], *[vec_v[k] for k in names])


def _allreduce_adamw_row(g_part, w, m, v, loss_part, comm=()):
    D = w.shape[1]
    n_pairs = N_DEV - 1

    def body(g_ref, w_ref, m_ref, v_ref, l_ref, go_ref, d_ref, nm_ref, nv_ref, lo_ref, land_g, land_l, sems):
        x, y, c = _place()
        copies = []
        for q, (src, land) in enumerate(((g_ref, land_g), (l_ref, land_l))):
            for r in range(1, N_DEV):
                fx, fy, fc = (r >> 2) & 1, (r >> 1) & 1, r & 1
                peer = (1 - x if fx else x, 1 - y if fy else y, 1 - c if fc else c)
                cp = _remote(src, land.at[r], sems, 2 * (q * n_pairs + r - 1), peer)
                cp.start()
                copies.append(cp)
        for cp in copies:
            cp.wait()

        def total(src, land):
            row = lambda r: src[...] if r == 0 else land[r]
            return ((row(0) + row(4)) + (row(2) + row(6))) + ((row(1) + row(5)) + (row(3) + row(7)))

        g = total(g_ref, land_g)
        go_ref[...] = g
        d_ref[...], nm_ref[...], nv_ref[...] = _adamw_math(w_ref[...], g, m_ref[...], v_ref[...])
        lo_ref[...] = total(l_ref, land_l)

    vm = pl.BlockSpec(memory_space=pltpu.VMEM)
    return _call(
        "allreduce_adamw_g_mix",
        body,
        (),
        [vm] * 5,
        [vm] * 5,
        [jax.ShapeDtypeStruct((1, D), F32)] * 4 + [jax.ShapeDtypeStruct(loss_part.shape, F32)],
        (g_part, w, m, v, loss_part),
        scratch=[pltpu.VMEM((N_DEV, 1, D), F32), pltpu.VMEM((N_DEV,) + loss_part.shape, F32),
                 pltpu.SemaphoreType.DMA((4 * n_pairs,))],
        comm=comm,
    )


def kernel(x, g_mix, w_in, b_in, w_dw, b_dw, ln_g, ln_b, w_pool, s_pool, w_out, g_ffn, w_gate, w_up, w_down, g_final, loss_target, m_g_mix, m_w_in, m_b_in, m_w_dw, m_b_dw, m_ln_g, m_ln_b, m_w_pool, m_s_pool, m_w_out, m_g_ffn, m_w_gate, m_w_up, m_w_down, m_g_final, v_g_mix, v_w_in, v_b_in, v_w_dw, v_b_dw, v_ln_g, v_ln_b, v_w_pool, v_s_pool, v_w_out, v_g_ffn, v_w_gate, v_w_up, v_w_down, v_g_final):
    x2 = x[0]
    target = loss_target[0]
    T, D = x2.shape
    w_in2, w_out2, w_down2 = w_in[0], w_out[0], w_down[0]
    taps_first = lambda a: jnp.transpose(a, (1, 0, 2))
    w_dw3 = taps_first(w_dw)
    w_gateT, w_upT = w_gate[0].T, w_up[0].T
    CI = w_in2.shape[1] * N_CHIPS
    DM = w_out2.shape[0] * N_CHIPS
    F = w_down2.shape[0] * N_CHIPS
    KW, _, dw_cols = w_dw3.shape
    assert dw_cols == LANES
    n_grp, G = w_pool.shape[1], w_pool.shape[-1]
    w_pool3 = w_pool[0]
    g_final2 = g_final.reshape(1, D)

    me = (2 * lax.axis_index("x") + lax.axis_index("y")).astype(jnp.int32).reshape(1)

    w_inT_b, w_dw4, f_out, f_gate, f_up, f_down = _place_and_gather(
        [(w_in2, "rows", (CI, D), BF16, True, True), (w_dw3, "lead", (N_CHIPS, KW, 1, dw_cols), F32, False, False)],
        [(w, "rows", shape, BF16, False, True)
         for w, shape in ((w_out2, (DM, D)), (w_gateT, (F, D)), (w_upT, (F, D)), (w_down2, (F, D)))])
    w_pool_b = w_pool3.astype(BF16)
    ici = lambda f: _GatherIci([f], ["rows"], [True])
    d2d = lambda f: _GatherD2d([f], ["rows"])
    gather = _start("gather_start", [ici(f_out), ici(f_gate), ici(f_up), ici(f_down)])
    (z, xn_b), _ = _in_proj(x2, g_mix, w_inT_b, b_in, after=[gather.token])
    (f_out,) = _wait("gather_out_wait", gather, 0, xn_b)
    s_out = _start("share_out_start", [d2d(f_out)], sibling_only=True)
    (y_b, v), _ = _seq_fwd(z, w_dw4, b_dw, ln_g, ln_b, w_pool_b, s_pool, after=[s_out.token])
    (w_out_b,) = _wait("share_out_wait", s_out, 0, y_b)
    (f_gate,) = _wait("gather_gate_wait", gather, 1, y_b)
    s_gate = _start("share_gate_start", [d2d(f_gate)], sibling_only=True)
    (h1, hn_b), _ = _out_proj(y_b, x2, w_out_b, g_ffn, after=[s_gate.token])
    (f_up,) = _wait("gather_up_wait", gather, 2, hn_b)
    s_up = _start("share_up_start", [d2d(f_up)], sibling_only=True)
    (wgT_b,) = _wait("share_gate_wait", s_gate, 0, hn_b)
    (wuT_b,) = _wait("share_up_wait", s_up, 0, hn_b)
    (g_b, u_b, a_b), _ = _gate_up(hn_b, wgT_b, wuT_b)
    (f_down,) = _wait("gather_down_wait", gather, 3, a_b)
    s_down = _start("share_down_start", [d2d(f_down)], sibling_only=True)
    (wd_b,) = _wait("share_down_wait", s_down, 0, a_b)
    (dh2, dh2_b, loss_part, d_g_final), _ = _down_loss(a_b, wd_b, h1, target, g_final2)

    gw_down, _ = _weight_grad("grad_w_down", a_b, dh2_b)
    (dg_b, du_b), (p_down_xy,) = _ffn_bwd_act(dh2_b, wd_b, g_b, u_b, comm=[_Scatter([gw_down], ["rows"], which=(0, 1))])
    gw_gateT, (p_down_d,) = _weight_grad("grad_w_gate", dg_b, hn_b, comm=[_Scatter([gw_down], ["rows"], which=(2,))])
    gw_upT, _ = _weight_grad("grad_w_up", du_b, hn_b)
    sum_down = _sum_parts("sum_w_down", gw_down, "rows", [p_down_xy, p_down_d], me)
    (dh1, dh1_b, dy, d_g_ffn), (p_gate, oth_down) = _ffn_bwd_in(
        dg_b, du_b, wgT_b, wuT_b, h1, dh2, g_ffn, w_out_b, comm=[_Scatter([gw_gateT], ["rows"]), _Swap([sum_down])])
    gw_out, _ = _weight_grad("grad_w_out", y_b, dh1_b)
    sum_gate = _sum_parts("sum_w_gate", gw_gateT, "rows", [p_gate], me)
    res = {}
    res["w_down"] = _adamw_on_sparsecore("adamw_w_down", w_down2, m_w_down[0], v_w_down[0], sum_down, oth_down, sum_down)
    (dz_b, d_wdw, d_bdw, d_lng, d_lnb, d_wp, d_sp, d_bin), (p_up, p_out, oth_gate) = _seq_bwd(
        z, dy, v, w_dw4, ln_g, ln_b, w_pool_b, s_pool,
        comm=[_Scatter([gw_upT, gw_out], ["rows", "rows"]), _Swap([sum_gate])])
    res["w_gate"] = _adamw_on_sparsecore(
        "adamw_w_gate", w_gateT, m_w_gate[0].T, v_w_gate[0].T, sum_gate, oth_gate, res["w_down"][0])
    vec_grads ={"b_dw": d_bdw, "ln_g": d_lng, "ln_b": d_lnb, "s_pool": d_sp, "g_ffn": d_g_ffn, "g_final": d_g_final, "b_in": d_bin}
    layout = _PackLayout(dw_cols * N_CHIPS // LANES, n_grp, G, [(k, a.shape[1]) for k, a in vec_grads.items()])
    pack = _pack_small(layout, d_wdw, d_wp, vec_grads)
    sum_up = _sum_parts("sum_w_up", gw_upT, "rows", [p_up], me)
    sum_out = _sum_parts("sum_w_out", gw_out, "rows", [p_out], me)
    gw_inT, (p_small, oth_up, oth_out) = _weight_grad(
        "grad_w_in", dz_b, xn_b, comm=[_Scatter([pack], ["all"]), _Swap([sum_up, sum_out])])
    sum_small = _sum_parts("sum_small", pack, "all", [p_small], me)
    late = _start("late_start", [_Scatter([gw_inT], ["rows"]), _Swap([sum_small])])
    (grad_x, d_g_mix), _ = _in_proj_bwd(dz_b, w_inT_b, x2, dh1, g_mix, after=[late.token])
    gw_inT, p_in = _wait("late_w_in_wait", late, 0, d_g_mix)
    sum_small, oth_small = _wait("late_small_wait", late, 1, d_g_mix)
    res["w_up"] = _adamw_on_sparsecore("adamw_w_up", w_upT, m_w_up[0].T, v_w_up[0].T, sum_up, oth_up, res["w_gate"][0])
    res["w_out"] = _adamw_on_sparsecore("adamw_w_out", w_out2, m_w_out[0], v_w_out[0], sum_out, oth_out, res["w_gate"][0])
    sum_in = _sum_parts("sum_w_in", gw_inT, "rows", [p_in], me)
    (*res["g_mix"], loss_row), (oth_in,) = _allreduce_adamw_row(
        d_g_mix, g_mix, m_g_mix, v_g_mix, loss_part, comm=[_Swap([sum_in])])
    loss = loss_row[0, 0]
    res["w_in"], _ = _adamw("adamw_w_in", w_in2, m_w_in[0], v_w_in[0], sum_in, oth_in, g_transposed=True)

    vec_w = {"b_dw": b_dw, "ln_g": ln_g, "ln_b": ln_b, "s_pool": s_pool, "g_ffn": g_ffn, "g_final": g_final2, "b_in": b_in}
    vec_m = {"b_dw": m_b_dw, "ln_g": m_ln_g, "ln_b": m_ln_b, "s_pool": m_s_pool, "g_ffn": m_g_ffn,
             "g_final": m_g_final.reshape(1, D), "b_in": m_b_in}
    vec_v = {"b_dw": v_b_dw, "ln_g": v_ln_g, "ln_b": v_ln_b, "s_pool": v_s_pool, "g_ffn": v_g_ffn,
             "g_final": v_g_final.reshape(1, D), "b_in": v_b_in}
    small = _adamw_small(layout, sum_small, oth_small, w_dw3, taps_first(m_w_dw), taps_first(v_w_dw),
                         w_pool3, m_w_pool[0], v_w_pool[0], vec_w, vec_m, vec_v)
    res["w_dw"] = [taps_first(a) for a in small[0:4]]
    res["w_pool"] = [a[None] for a in small[4:8]]
    for q, k in enumerate(vec_w):
        res[k] = list(small[8 + 4 * q : 12 + 4 * q])
    res["g_final"] = [a.reshape(D) for a in res["g_final"]]
    for k in ("w_in", "w_out", "w_down"):
        res[k] = [a[None] for a in res[k]]
    for k in ("w_gate", "w_up"):
        res[k] = [a.T[None] for a in res[k]]

    order = ["g_mix", "w_in", "b_in", "w_dw", "b_dw", "ln_g", "ln_b", "w_pool", "s_pool", "w_out", "g_ffn", "w_gate", "w_up", "w_down", "g_final"]
    outs = [loss, grad_x[None]]
    for q in range(4):
        outs += [res[k][q] for k in order]
    return tuple(outs)
```

```python
import jax
import jax.numpy as jnp
from jax import lax
from jax.experimental import pallas as pl
from jax.experimental.pallas import tpu as pltpu
from jax.experimental.pallas import tpu_sc as plsc

F32 = jnp.float32
BF16 = jnp.bfloat16
MESH = pl.DeviceIdType.MESH
ANY = pl.BlockSpec(memory_space=pl.ANY)

RMS_EPS = 1e-6
LN_EPS = 1e-5
POOL_WINDOWS = (2, 4, 8, 16)
ADAM_LR = 0.001
ADAM_B1 = 0.9
ADAM_B2 = 0.999
ADAM_EPS = 1e-08
ADAM_WD = 0.01
ADAM_STEP = 10

LANES = 128
SUBLANES = 8
BF16_ROWS = 16
HALO = 32
CONV_ROWS = 64
HIDDEN_CHUNK = 512
VMEM_LIMIT = 56 * 1024 * 1024
PACK_W = 512
N_CHIPS = 4
N_DEV = 8
SIBLING_BARRIER_ID = 0
SC_CORES = 2
SC_TILES = 32
SC_LANES = 16


def _tile(n, want, mult=8):
    t = min(n, want)
    while n % t or t % mult:
        t -= 1
    return t


def _sigmoid(x):
    return 1.0 / (1.0 + jnp.exp(-x))


def _dot(a, b, dims):
    return lax.dot_general(a, b, (dims, ((), ())), preferred_element_type=F32)


NN = ((1,), (0,))
NT = ((1,), (1,))
TN = ((0,), (0,))


def _rms_bwd(x, g, dy):
    r = lax.rsqrt(jnp.mean(x * x, axis=-1, keepdims=True) + RMS_EPS)
    xh = x * r
    gy = dy * g
    dx = r * (gy - xh * jnp.mean(gy * xh, axis=-1, keepdims=True))
    return dx, dy * xh


def _accumulate(ref, first, val):
    @pl.when(first)
    def _():
        ref[...] = val

    @pl.when(jnp.logical_not(first))
    def _():
        ref[...] += val


def _place():
    return lax.axis_index("x"), lax.axis_index("y"), lax.axis_index("c")


def _other_chips(x, y):
    return [(1 - x, y), (x, 1 - y), (1 - x, 1 - y)]


def _rows(ref, start, n):
    return ref.at[pl.ds(pl.multiple_of(start, BF16_ROWS), n)]


def _window(ref, how, k, c=None):
    if how == "all":
        return ref
    if how == "lead":
        return ref.at[k]
    assert how == "rows"
    n = ref.shape[0] // N_CHIPS
    if c is None:
        return _rows(ref, k * n, n)
    return _rows(ref, k * n + c * (n // 2), n // 2)


def _remote(src, dst, sems, s, device):
    return pltpu.make_async_remote_copy(
        src_ref=src, dst_ref=dst, send_sem=sems.at[s], recv_sem=sems.at[s + 1], device_id=device, device_id_type=MESH)


class _GatherIci:
    aliased = True

    def __init__(self, fulls, hows, splits, which=(0, 1, 2)):
        self.fulls, self.hows, self.splits, self.which = list(fulls), list(hows), list(splits), tuple(which)

    def inputs(self):
        return self.fulls

    def out_shapes(self):
        return [jax.ShapeDtypeStruct(a.shape, a.dtype) for a in self.fulls]

    def n_sems(self):
        return 6 * len(self.fulls)

    def build(self, ins, outs, sems, base):
        x, y, c = _place()
        me = 2 * x + y
        chips = _other_chips(x, y)
        starts, waits = [], []
        for a, (how, sp) in enumerate(zip(self.hows, self.splits)):
            half = c if sp else None
            mine = _window(outs[a], how, me, half)
            for j in self.which:
                px, py = chips[j]
                s = base + 6 * a + 2 * j
                cp = _remote(mine, mine, sems, s, (px, py, c))
                landing = _remote(mine, _window(outs[a], how, 2 * px + py, half), sems, s, (px, py, c))
                starts.append(cp.start)
                waits += [landing.wait_recv, cp.wait_send]
        return starts, waits


class _GatherD2d:
    aliased = True

    def __init__(self, fulls, hows):
        self.fulls, self.hows = list(fulls), list(hows)

    def inputs(self):
        return self.fulls

    def out_shapes(self):
        return [jax.ShapeDtypeStruct(a.shape, a.dtype) for a in self.fulls]

    def n_sems(self):
        return 6 * len(self.fulls)

    def build(self, ins, outs, sems, base):
        x, y, c = _place()
        starts, waits = [], []
        for a, how in enumerate(self.hows):
            for j, (px, py) in enumerate(_other_chips(x, y)):
                s = base + 6 * a + 2 * j
                got = _window(outs[a], how, 2 * px + py, c)
                cp = _remote(got, got, sems, s, (x, y, 1 - c))
                landing = _remote(got, _window(outs[a], how, 2 * px + py, 1 - c), sems, s, (x, y, 1 - c))
                starts.append(cp.start)
                waits += [landing.wait_recv, cp.wait_send]
        return starts, waits


def _part_shape(a, how):
    if how == "all":
        return a.shape
    assert how == "rows"
    return (a.shape[0] // N_CHIPS, a.shape[1])


class _Scatter:
    aliased = False

    def __init__(self, fulls, hows, which=(0, 1, 2)):
        self.fulls, self.hows, self.which = list(fulls), list(hows), tuple(which)

    def inputs(self):
        return self.fulls

    def out_shapes(self):
        return [jax.ShapeDtypeStruct((len(self.which),) + _part_shape(a, h), a.dtype) for a, h in zip(self.fulls, self.hows)]

    def n_sems(self):
        return 6 * len(self.fulls)

    def build(self, ins, outs, sems, base):
        x, y, c = _place()
        chips = _other_chips(x, y)
        starts, waits = [], []
        for a, how in enumerate(self.hows):
            for slot, j in enumerate(self.which):
                px, py = chips[j]
                cp = _remote(_window(ins[a], how, 2 * px + py), outs[a].at[slot], sems, base + 6 * a + 2 * j, (px, py, c))
                starts.append(cp.start)
                waits += [cp.wait_recv, cp.wait_send]
        return starts, waits


class _Swap:
    aliased = False

    def __init__(self, arrays):
        self.arrays = list(arrays)

    def inputs(self):
        return self.arrays

    def out_shapes(self):
        return [jax.ShapeDtypeStruct(a.shape, a.dtype) for a in self.arrays]

    def n_sems(self):
        return 2 * len(self.arrays)

    def build(self, ins, outs, sems, base):
        x, y, c = _place()
        starts, waits = [], []
        for a in range(len(ins)):
            cp = _remote(ins[a], outs[a], sems, base + 2 * a, (x, y, 1 - c))
            starts.append(cp.start)
            waits += [cp.wait_recv, cp.wait_send]
        return starts, waits


def _call(name, body, grid, in_specs, out_specs, out_shape, args, scratch=(), comm=(), after=()):
    comm, after = list(comm), list(after)
    n_in, n_out, n_scr, n_after = len(args), len(out_shape), len(scratch), len(after)
    c_in = [a for op in comm for a in op.inputs()]
    c_out = [s for op in comm for s in op.out_shapes()]
    n_sems = sum(op.n_sems() for op in comm)
    aliases, i_in, i_out = {}, 0, 0
    for op in comm:
        if op.aliased:
            for q in range(len(op.inputs())):
                aliases[n_in + n_after + i_in + q] = n_out + i_out + q
        i_in, i_out = i_in + len(op.inputs()), i_out + len(op.out_shapes())

    def wrapped(*refs):
        ins = refs[:n_in]
        cin = refs[n_in + n_after : n_in + n_after + len(c_in)]
        o0 = n_in + n_after + len(c_in)
        outs = refs[o0 : o0 + n_out]
        cout = refs[o0 + n_out : o0 + n_out + len(c_out)]
        s0 = o0 + n_out + len(c_out)
        scr = refs[s0 : s0 + n_scr]

        def copies():
            sems = refs[s0 + n_scr]
            starts, waits = [], []
            i_in = i_out = base = 0
            for op in comm:
                ni, no = len(op.inputs()), len(op.out_shapes())
                s, w = op.build(cin[i_in : i_in + ni], cout[i_out : i_out + no], sems, base)
                starts += s
                waits += w
                i_in, i_out, base = i_in + ni, i_out + no, base + op.n_sems()
            return starts, waits

        def run_starts():
            for start in copies()[0]:
                start()

        def run_waits():
            for wait in copies()[1]:
                wait()

        if comm and grid:
            first = last = True
            for d, n in enumerate(grid):
                first = jnp.logical_and(first, pl.program_id(d) == 0)
                last = jnp.logical_and(last, pl.program_id(d) == n - 1)
            pl.when(first)(run_starts)
        elif comm:
            run_starts()
        if body is not None:
            body(*ins, *outs, *scr)
        if comm and grid:
            pl.when(last)(run_waits)
        elif comm:
            run_waits()

    res = pl.pallas_call(
        wrapped,
        name=name,
        grid=grid,
        in_specs=list(in_specs) + [ANY] * (n_after + len(c_in)),
        out_specs=list(out_specs) + [ANY] * len(c_out),
        out_shape=list(out_shape) + c_out,
        scratch_shapes=list(scratch) + ([pltpu.SemaphoreType.DMA((n_sems,))] if comm else []),
        input_output_aliases=aliases,
        compiler_params=pltpu.CompilerParams(dimension_semantics=("arbitrary",) * len(grid), vmem_limit_bytes=VMEM_LIMIT),
    )(*args, *after, *c_in)
    return tuple(res[:n_out]), tuple(res[n_out:])


def _place_and_gather(now, later):
    items = list(now) + list(later)
    n, n_now = len(items), len(now)
    buf_shape = lambda it: it[0].shape[::-1] if it[4] else it[0].shape
    split_now = [a for a in range(n_now) if items[a][5]]

    def body(*refs):
        ins, outs = refs[:n], refs[n : 2 * n]
        stage, bufs = refs[2 * n : 3 * n - n_now], refs[3 * n - n_now : 4 * n - n_now]
        sems = refs[4 * n - n_now]
        x, y, c = _place()
        me = 2 * x + y
        chips = _other_chips(x, y)
        loads = [pltpu.make_async_copy(ins[a], stage[a - n_now], sems.at[a]) for a in range(n_now, n)]
        for ld in loads:
            ld.start()
        pending = []

        def place(a, val):
            _, how, _, dtype, transposed, _ = items[a]
            bufs[a][...] = (val.T if transposed else val).astype(dtype)
            cp = pltpu.make_async_copy(bufs[a], _window(outs[a], how, me), sems.at[n + a])
            cp.start()
            pending.append(cp.wait)

        arrivals = []
        for a in range(n_now):
            place(a, ins[a][...])
            how, split = items[a][1], items[a][5]
            half = c if split else None
            src = _rows(bufs[a], c * (bufs[a].shape[0] // 2), bufs[a].shape[0] // 2) if split else bufs[a]
            for j, (px, py) in enumerate(chips):
                s = 2 * n + 6 * a + 2 * j
                cp = _remote(src, _window(outs[a], how, me, half), sems, s, (px, py, c))
                landing = _remote(src, _window(outs[a], how, 2 * px + py, half), sems, s, (px, py, c))
                cp.start()
                arrivals.append(landing.wait_recv)
                pending.append(cp.wait_send)
        for a in range(n_now, n):
            loads[a - n_now].wait()
            place(a, stage[a - n_now][...])
        for wait in arrivals:
            wait()
        d2d = _GatherD2d([None] * len(split_now), [items[a][1] for a in split_now])
        starts, waits = d2d.build(None, [outs[a] for a in split_now], sems, 2 * n + 6 * n_now)
        for start in starts:
            start()
        for wait in waits + pending:
            wait()

    vm = pl.BlockSpec(memory_space=pltpu.VMEM)
    return pl.pallas_call(
        body,
        name="place_and_gather",
        in_specs=[vm] * n_now + [ANY] * (n - n_now),
        out_specs=[ANY] * n,
        out_shape=[jax.ShapeDtypeStruct(it[2], it[3]) for it in items],
        scratch_shapes=[pltpu.VMEM(it[0].shape, it[0].dtype) for it in later]
        + [pltpu.VMEM(buf_shape(it), it[3]) for it in items]
        + [pltpu.SemaphoreType.DMA((2 * n + 6 * n_now + 6 * len(split_now),))],
        compiler_params=pltpu.CompilerParams(vmem_limit_bytes=VMEM_LIMIT),
    )(*[it[0] for it in items])


_HBM = pl.BlockSpec(memory_space=pltpu.HBM)
_SEM = pl.BlockSpec(memory_space=pltpu.SEMAPHORE)
_DATAFLOW = pltpu.SideEffectType.DATAFLOW_SIDE_EFFECTING


class _Pending:
    def __init__(self, ops, bases, sems, arrays, token):
        self.ops, self.bases, self.sems, self.arrays, self.token = ops, bases, sems, arrays, token


def _op_refs(op, refs):
    n_src = len(op.inputs())
    return refs[:n_src], (refs[:n_src] if op.aliased else refs[n_src:])


def _start(name, ops, sibling_only=False):
    per_op = [list(op.inputs()) + ([] if op.aliased else [lax.empty(sd.shape, sd.dtype) for sd in op.out_shapes()])
              for op in ops]
    arrays = [a for group in per_op for a in group]
    bases = [sum(op.n_sems() for op in ops[:k]) for k in range(len(ops))]
    n = len(arrays)

    def body(*refs):
        sems, token = refs[n], refs[-1]
        if sibling_only:
            x, y, c = _place()
            barrier = pltpu.get_barrier_semaphore()
            pl.semaphore_signal(barrier, inc=1, device_id=(x, y, 1 - c), device_id_type=MESH)
            pl.semaphore_wait(barrier, 1)
        at = 0
        for op, group, base in zip(ops, per_op, bases):
            starts, _ = op.build(*_op_refs(op, refs[at : at + len(group)]), sems, base)
            for start in starts:
                start()
            at += len(group)
        token[...] = jnp.zeros_like(token)

    res = pl.pallas_call(
        body,
        name=name,
        out_shape=(pltpu.SemaphoreType.DMA((sum(op.n_sems() for op in ops),)),)
        + tuple(pltpu.HBM(a.shape, a.dtype) for a in arrays) + (jax.ShapeDtypeStruct((SUBLANES, LANES), F32),),
        in_specs=(_HBM,) * n,
        out_specs=(_SEM,) + (_HBM,) * n + (pl.BlockSpec(memory_space=pltpu.VMEM),),
        input_output_aliases={i: 1 + i for i in range(n)},
        compiler_params=pltpu.CompilerParams(
            has_side_effects=_DATAFLOW, collective_id=SIBLING_BARRIER_ID if sibling_only else None),
    )(*[pltpu.with_memory_space_constraint(a, pltpu.HBM) for a in arrays])
    thru, at, groups = list(res[1 : 1 + n]), 0, []
    for group in per_op:
        groups.append(thru[at : at + len(group)])
        at += len(group)
    return _Pending(list(ops), bases, res[0], groups, res[-1])


def _wait(name, pending, k, after):
    op, arrays = pending.ops[k], pending.arrays[k]
    n = len(arrays)

    def body(*refs):
        _, waits = op.build(*_op_refs(op, refs[:n]), refs[n], pending.bases[k])
        for wait in waits:
            wait()

    return pl.pallas_call(
        body,
        name=name,
        out_shape=tuple(pltpu.HBM(a.shape, a.dtype) for a in arrays),
        in_specs=(_HBM,) * n + (_SEM, ANY),
        out_specs=(_HBM,) * n,
        input_output_aliases={i: i for i in range(n)},
        compiler_params=pltpu.CompilerParams(has_side_effects=_DATAFLOW),
    )(*arrays, pending.sems, after)


def _in_proj(x, g_mix, w_inT_b, b_in, after=()):
    T, D = x.shape
    CI = w_inT_b.shape[0]
    tm = _tile(T, 512)

    def body(x_ref, g_ref, w_ref, b_ref, z_ref, xn_ref):
        xv = x_ref[...]
        r = lax.rsqrt(jnp.mean(xv * xv, axis=-1, keepdims=True) + RMS_EPS)
        xn = (xv * r * g_ref[...]).astype(BF16)
        xn_ref[...] = xn
        z_ref[...] = _dot(xn, w_ref[...], NT) + b_ref[...]

    return _call(
        "in_proj",
        body,
        (T // tm,),
        [
            pl.BlockSpec((tm, D), lambda i: (i, 0)),
            pl.BlockSpec((1, D), lambda i: (0, 0)),
            pl.BlockSpec((CI, D), lambda i: (0, 0)),
            pl.BlockSpec((1, CI), lambda i: (0, 0)),
        ],
        [pl.BlockSpec((tm, CI), lambda i: (i, 0)), pl.BlockSpec((tm, D), lambda i: (i, 0))],
        [jax.ShapeDtypeStruct((T, CI), F32), jax.ShapeDtypeStruct((T, D), BF16)],
        (x, g_mix, w_inT_b, b_in),
        after=after,
    )


def _fill_shifted(scr):
    n = scr.shape[1] - SUBLANES
    for s in range(1, SUBLANES):
        scr[s, 0:n, :] = scr[0, s : s + n, :]


def _shifted_rows(scr, off, n, cs):
    s = off % SUBLANES
    return scr[s, off - s : off - s + n, cs]


def _pool_mean_minus_token(p_scr, cs, w, cnt, tt):
    tok = p_scr[HALO : HALO + tt, cs]
    s = tok
    for d in range(1, w):
        s = s + p_scr[HALO - d : HALO - d + tt, cs]
    return s / cnt - tok


def _seq_fwd(z, w_dw4, b_dw, ln_g, ln_b, w_pool_b, s_pool, after=()):
    T, CI = z.shape
    CC = ln_g.shape[1]
    n_grp, G = w_pool_b.shape[0], w_pool_b.shape[-1]
    KW = w_dw4.shape[1]
    D = CC + n_grp * G
    tt = _tile(T, 512, HALO)
    per = tt // HALO

    def body(zc_ref, zp_ref, wdw_ref, bdw_ref, lng_ref, lnb_ref, wp_ref, sp_ref, y_ref, v_ref, u_scr, p_scr):
        i = pl.program_id(0)
        first = i == 0
        u_prev = zp_ref[:, 0:CC] * _sigmoid(zp_ref[:, CC : 2 * CC])
        u_scr[0, 0:HALO, :] = jnp.where(first, 0.0, u_prev)
        p_scr[0:HALO, :] = jnp.where(first, 0.0, zp_ref[:, 2 * CC :])
        u_scr[0, HALO:, :] = zc_ref[:, 0:CC] * _sigmoid(zc_ref[:, CC : 2 * CC])
        p_scr[HALO:, :] = zc_ref[:, 2 * CC :]
        _fill_shifted(u_scr)

        for j in range(CC // LANES):
            cs = slice(LANES * j, LANES * (j + 1))
            for rb in range(tt // CONV_ROWS):
                acc = jnp.zeros((CONV_ROWS, LANES), F32)
                for k in range(KW):
                    off = HALO - (KW - 1) + k + rb * CONV_ROWS
                    acc = acc + _shifted_rows(u_scr, off, CONV_ROWS, cs) * wdw_ref[j, k]
                v_ref[rb * CONV_ROWS : (rb + 1) * CONV_ROWS, cs] = acc + bdw_ref[:, cs]

        v = v_ref[...]
        mu = jnp.mean(v, axis=-1, keepdims=True)
        d = v - mu
        var = jnp.mean(d * d, axis=-1, keepdims=True)
        ln = d * lax.rsqrt(var + LN_EPS) * lng_ref[...] + lnb_ref[...]
        y_ref[:, 0:CC] = (ln * _sigmoid(ln)).astype(BF16)

        tpos = i * tt + lax.broadcasted_iota(jnp.int32, (tt, 1), 0)
        for gi, w in enumerate(POOL_WINDOWS):
            cs = slice(G * gi, G * (gi + 1))
            cnt = jnp.minimum(tpos + 1, w).astype(F32)
            yi = _pool_mean_minus_token(p_scr, cs, w, cnt, tt)
            q = _dot(yi.astype(BF16), wp_ref[gi], NN)
            y_ref[:, CC + G * gi : CC + G * (gi + 1)] = (q * sp_ref[:, cs]).astype(BF16)

    const2 = lambda i: (0, 0)
    return _call(
        "seq_fwd",
        body,
        (T // tt,),
        [
            pl.BlockSpec((tt, CI), lambda i: (i, 0)),
            pl.BlockSpec((HALO, CI), lambda i: (jnp.maximum(i * per - 1, 0), 0)),
            pl.BlockSpec(w_dw4.shape, lambda i: (0,) * w_dw4.ndim),
            pl.BlockSpec((1, CC), const2),
            pl.BlockSpec((1, CC), const2),
            pl.BlockSpec((1, CC), const2),
            pl.BlockSpec(w_pool_b.shape, lambda i: (0, 0, 0)),
            pl.BlockSpec((1, n_grp * G), const2),
        ],
        [pl.BlockSpec((tt, D), lambda i: (i, 0)), pl.BlockSpec((tt, CC), lambda i: (i, 0))],
        [jax.ShapeDtypeStruct((T, D), BF16), jax.ShapeDtypeStruct((T, CC), F32)],
        (z, z, w_dw4, b_dw, ln_g, ln_b, w_pool_b, s_pool),
        scratch=[pltpu.VMEM((SUBLANES, HALO + tt, CC), F32), pltpu.VMEM((HALO + tt, n_grp * G), F32)],
        after=after,
    )


def _out_proj(y_b, x, w_out_b, g_ffn, after=()):
    T, D = x.shape
    tm = _tile(T, 512)

    def body(y_ref, x_ref, w_ref, g_ref, h1_ref, hn_ref):
        h1 = x_ref[...] + _dot(y_ref[...], w_ref[...], NN)
        h1_ref[...] = h1
        r = lax.rsqrt(jnp.mean(h1 * h1, axis=-1, keepdims=True) + RMS_EPS)
        hn_ref[...] = (h1 * r * g_ref[...]).astype(BF16)

    row = lambda i: (i, 0)
    return _call(
        "out_proj",
        body,
        (T // tm,),
        [
            pl.BlockSpec((tm, y_b.shape[1]), row),
            pl.BlockSpec((tm, D), row),
            pl.BlockSpec(w_out_b.shape, lambda i: (0, 0)),
            pl.BlockSpec((1, D), lambda i: (0, 0)),
        ],
        [pl.BlockSpec((tm, D), row), pl.BlockSpec((tm, D), row)],
        [jax.ShapeDtypeStruct((T, D), F32), jax.ShapeDtypeStruct((T, D), BF16)],
        (y_b, x, w_out_b, g_ffn),
        after=after,
    )


def _hidden_tile(F):
    return _tile(F, 1408, LANES)


def _gate_up(hn_b, wgT_b, wuT_b):
    T, D = hn_b.shape
    F = wgT_b.shape[0]
    tm, tf = _tile(T, 1024), _hidden_tile(F)

    def body(hn_ref, wg_ref, wu_ref, silu_ref, uds_ref, a_ref):
        hn = hn_ref[...]
        for c0 in range(0, tf, HIDDEN_CHUNK):
            cs = slice(c0, min(c0 + HIDDEN_CHUNK, tf))
            gv = _dot(hn, wg_ref[cs, :], NT)
            uv = _dot(hn, wu_ref[cs, :], NT)
            sg = _sigmoid(gv)
            silu = gv * sg
            silu_ref[:, cs] = silu.astype(BF16)
            uds_ref[:, cs] = (uv * (sg * (1.0 + gv * (1.0 - sg)))).astype(BF16)
            a_ref[:, cs] = (silu * uv).astype(BF16)

    wspec = pl.BlockSpec((tf, D), lambda j, i: (j, 0))
    ospec = pl.BlockSpec((tm, tf), lambda j, i: (i, j))
    return _call(
        "gate_up",
        body,
        (F // tf, T // tm),
        [pl.BlockSpec((tm, D), lambda j, i: (i, 0)), wspec, wspec],
        [ospec, ospec, ospec],
        [jax.ShapeDtypeStruct((T, F), BF16)] * 3,
        (hn_b, wgT_b, wuT_b),
    )


def _down_loss(a_b, wd_b, h1, target, g_final):
    T, D = h1.shape
    F = a_b.shape[1]
    tm = _tile(T, 512)
    nt = T // tm

    def body(a_ref, w_ref, h1_ref, t_ref, g_ref, dh2_ref, dh2b_ref, loss_ref, dg_ref):
        i = pl.program_id(0)
        h2 = h1_ref[...] + _dot(a_ref[...], w_ref[...], NN)
        r = lax.rsqrt(jnp.mean(h2 * h2, axis=-1, keepdims=True) + RMS_EPS)
        g = g_ref[...]
        diff = h2 * r * g - t_ref[...]
        _accumulate(loss_ref, i == 0, jnp.full(loss_ref.shape, jnp.sum(diff * diff) * (0.5 / D), F32))
        dh2, dg_rows = _rms_bwd(h2, g, diff * (1.0 / D))
        dh2_ref[...] = dh2
        dh2b_ref[...] = dh2.astype(BF16)
        _accumulate(dg_ref, i == 0, jnp.sum(dg_rows, axis=0, keepdims=True))

    row = lambda i: (i, 0)
    return _call(
        "down_loss",
        body,
        (nt,),
        [
            pl.BlockSpec((tm, F), row),
            pl.BlockSpec((F, D), lambda i: (0, 0), pipeline_mode=pl.Buffered(1)),
            pl.BlockSpec((tm, D), row),
            pl.BlockSpec((tm, D), row),
            pl.BlockSpec((1, D), lambda i: (0, 0)),
        ],
        [
            pl.BlockSpec((tm, D), row),
            pl.BlockSpec((tm, D), row),
            pl.BlockSpec((1, LANES), lambda i: (0, 0)),
            pl.BlockSpec((1, D), lambda i: (0, 0)),
        ],
        [
            jax.ShapeDtypeStruct((T, D), F32),
            jax.ShapeDtypeStruct((T, D), BF16),
            jax.ShapeDtypeStruct((1, LANES), F32),
            jax.ShapeDtypeStruct((1, D), F32),
        ],
        (a_b, wd_b, h1, target, g_final),
    )


def _ffn_bwd_act(dh2_b, wd_b, silu_b, uds_b, comm=()):
    T, D = dh2_b.shape
    F = wd_b.shape[0]
    tm, tf = _tile(T, 1024), _hidden_tile(F)

    def body(d_ref, w_ref, silu_ref, uds_ref, dg_ref, du_ref):
        d = d_ref[...]
        for c0 in range(0, tf, HIDDEN_CHUNK):
            cs = slice(c0, min(c0 + HIDDEN_CHUNK, tf))
            da = _dot(d, w_ref[cs, :], NT)
            dg_ref[:, cs] = (da * uds_ref[:, cs].astype(F32)).astype(BF16)
            du_ref[:, cs] = (da * silu_ref[:, cs].astype(F32)).astype(BF16)

    aspec = pl.BlockSpec((tm, tf), lambda j, i: (i, j))
    return _call(
        "ffn_bwd_act",
        body,
        (F // tf, T // tm),
        [pl.BlockSpec((tm, D), lambda j, i: (i, 0)), pl.BlockSpec((tf, D), lambda j, i: (j, 0)), aspec, aspec],
        [aspec, aspec],
        [jax.ShapeDtypeStruct((T, F), BF16)] * 2,
        (dh2_b, wd_b, silu_b, uds_b),
        comm=comm,
    )


def _ffn_bwd_in(dg_b, du_b, wgT_b, wuT_b, h1, dh2, g_ffn, w_out_b, comm=()):
    T, D = h1.shape
    F = wgT_b.shape[0]
    DM = w_out_b.shape[0]
    tm = _tile(T, 512)

    def body(dg_ref, du_ref, wg_ref, wu_ref, h1_ref, dh2_ref, g_ref, wo_ref, dh1_ref, dh1b_ref, dy_ref, dgf_ref):
        i = pl.program_id(0)
        dhn = _dot(dg_ref[...], wg_ref[...], NN) + _dot(du_ref[...], wu_ref[...], NN)
        dx, dg_rows = _rms_bwd(h1_ref[...], g_ref[...], dhn)
        dh1 = dh2_ref[...] + dx
        dh1b = dh1.astype(BF16)
        dh1_ref[...] = dh1
        dh1b_ref[...] = dh1b
        dy_ref[...] = _dot(dh1b, wo_ref[...], NT)
        _accumulate(dgf_ref, i == 0, jnp.sum(dg_rows, axis=0, keepdims=True))

    row = lambda i: (i, 0)
    const = lambda i: (0, 0)
    return _call(
        "ffn_bwd_in",
        body,
        (T // tm,),
        [
            pl.BlockSpec((tm, F), row),
            pl.BlockSpec((tm, F), row),
            pl.BlockSpec((F, D), const, pipeline_mode=pl.Buffered(1)),
            pl.BlockSpec((F, D), const, pipeline_mode=pl.Buffered(1)),
            pl.BlockSpec((tm, D), row),
            pl.BlockSpec((tm, D), row),
            pl.BlockSpec((1, D), const),
            pl.BlockSpec((DM, D), const, pipeline_mode=pl.Buffered(1)),
        ],
        [pl.BlockSpec((tm, D), row), pl.BlockSpec((tm, D), row), pl.BlockSpec((tm, DM), row), pl.BlockSpec((1, D), const)],
        [
            jax.ShapeDtypeStruct((T, D), F32),
            jax.ShapeDtypeStruct((T, D), BF16),
            jax.ShapeDtypeStruct((T, DM), F32),
            jax.ShapeDtypeStruct((1, D), F32),
        ],
        (dg_b, du_b, wgT_b, wuT_b, h1, dh2, g_ffn, w_out_b),
        comm=comm,
    )


def _seq_bwd(z, dy, v, w_dw4, ln_g, ln_b, w_pool_b, s_pool, comm=()):
    T, CI = z.shape
    CC = ln_g.shape[1]
    n_grp, G = w_pool_b.shape[0], w_pool_b.shape[-1]
    CP = n_grp * G
    KW = w_dw4.shape[1]
    n_cc = CC // LANES
    D = CC + CP
    tt = _tile(T, 512, HALO)
    per = tt // HALO
    n_tiles = T // tt
    last_halo = T // HALO - 1

    def body(zc_ref, zp_ref, dyc_ref, dyn_ref, vc_ref, vn_ref, wdw_ref, lng_ref, lnb_ref, wp_ref, sp_ref,
             dz_ref, dwdw_ref, dbdw_ref, dlng_ref, dlnb_ref, dwp_ref, dsp_ref, dbin_ref,
             dv_scr, u_scr, p_scr, g_scr, dw_scr):
        i = pl.program_id(0)
        first = i == 0
        last = i == n_tiles - 1
        lng, lnb = lng_ref[...], lnb_ref[...]

        def conv_pre(vv, dyc):
            mu = jnp.mean(vv, axis=-1, keepdims=True)
            d = vv - mu
            rs = lax.rsqrt(jnp.mean(d * d, axis=-1, keepdims=True) + LN_EPS)
            xh = d * rs
            ln = xh * lng + lnb
            sg = _sigmoid(ln)
            dln = dyc * (sg * (1.0 + ln * (1.0 - sg)))
            dxh = dln * lng
            dv = rs * (dxh - jnp.mean(dxh, axis=-1, keepdims=True) - xh * jnp.mean(dxh * xh, axis=-1, keepdims=True))
            return dv, dln, xh

        dv_c, dln_c, xh_c = conv_pre(vc_ref[...], dyc_ref[:, 0:CC])
        dv_scr[0, 0:tt, :] = dv_c
        dv_n, _, _ = conv_pre(vn_ref[...], dyn_ref[:, 0:CC])
        dv_scr[0, tt:, :] = jnp.where(last, 0.0, dv_n)
        _fill_shifted(dv_scr)
        _accumulate(dlng_ref, first, jnp.sum(dln_c * xh_c, axis=0, keepdims=True))
        _accumulate(dlnb_ref, first, jnp.sum(dln_c, axis=0, keepdims=True))
        _accumulate(dbdw_ref, first, jnp.sum(dv_c, axis=0, keepdims=True))

        u_scr[...] = zc_ref[:, 0:CC] * _sigmoid(zc_ref[:, CC : 2 * CC])

        @pl.when(first)
        def _():
            dw_scr[...] = jnp.zeros_like(dw_scr)

        for j in range(n_cc):
            cs = slice(LANES * j, LANES * (j + 1))
            gs = slice(CC + LANES * j, CC + LANES * (j + 1))
            dbin_a = jnp.zeros((1, LANES), F32)
            dbin_g = jnp.zeros((1, LANES), F32)
            for rb in range(tt // CONV_ROWS):
                rows = slice(rb * CONV_ROWS, (rb + 1) * CONV_ROWS)
                u_blk = u_scr[rows, cs]
                du = jnp.zeros((CONV_ROWS, LANES), F32)
                for k in range(KW):
                    off = rb * CONV_ROWS + (KW - 1) - k
                    d = _shifted_rows(dv_scr, off, CONV_ROWS, cs)
                    du = du + d * wdw_ref[j, k]
                    dw_scr[j * HALO + k] += jnp.sum((u_blk * d).reshape(CONV_ROWS // 8, 8, LANES), axis=0)
                a = zc_ref[rows, cs]
                sg = _sigmoid(zc_ref[rows, gs])
                da = du * sg
                dgate = du * a * sg * (1.0 - sg)
                dz_ref[rows, cs] = da.astype(BF16)
                dz_ref[rows, gs] = dgate.astype(BF16)
                dbin_a = dbin_a + jnp.sum(da, axis=0, keepdims=True)
                dbin_g = dbin_g + jnp.sum(dgate, axis=0, keepdims=True)
            _accumulate(dbin_ref.at[:, cs], first, dbin_a)
            _accumulate(dbin_ref.at[:, gs], first, dbin_g)

        @pl.when(last)
        def _():
            dwdw_ref[...] = jnp.sum(dw_scr[...], axis=1).reshape(dwdw_ref.shape)

        p_scr[0:HALO, :] = jnp.where(first, 0.0, zp_ref[:, 2 * CC :])
        p_scr[HALO:, :] = zc_ref[:, 2 * CC :]
        tpos = i * tt + lax.broadcasted_iota(jnp.int32, (tt, 1), 0)
        for gi, w in enumerate(POOL_WINDOWS):
            cs = slice(G * gi, G * (gi + 1))
            ys = slice(CC + G * gi, CC + G * (gi + 1))
            ps = slice(2 * CC + G * gi, 2 * CC + G * (gi + 1))
            cnt = jnp.minimum(tpos + 1, w).astype(F32)
            yib = _pool_mean_minus_token(p_scr, cs, w, cnt, tt).astype(BF16)
            wp = wp_ref[gi]
            sp = sp_ref[:, cs]
            dyp = dyc_ref[:, ys]
            q = _dot(yib, wp, NN)
            _accumulate(dsp_ref.at[:, cs], first, jnp.sum(dyp * q, axis=0, keepdims=True))
            dq_c = (dyp * sp).astype(BF16)
            dq_n = (jnp.where(last, 0.0, dyn_ref[:, ys]) * sp).astype(BF16)
            _accumulate(dwp_ref.at[gi], first, _dot(yib, dq_c, TN))
            dyi_c = _dot(dq_c, wp, NT)
            g_scr[0:tt, cs] = dyi_c / cnt
            g_scr[tt:, cs] = _dot(dq_n, wp, NT) * (1.0 / w)
            dp = -dyi_c
            for d in range(w):
                dp = dp + g_scr[d : d + tt, cs]
            dz_ref[:, ps] = dp.astype(BF16)
            _accumulate(dbin_ref.at[:, ps], first, jnp.sum(dp, axis=0, keepdims=True))

    cur = lambda i: (i, 0)
    prev = lambda i: (jnp.maximum(i * per - 1, 0), 0)
    nxt = lambda i: (jnp.minimum((i + 1) * per, last_halo), 0)
    c2 = lambda i: (0, 0)
    c3 = lambda i: (0, 0, 0)
    return _call(
        "seq_bwd",
        body,
        (n_tiles,),
        [
            pl.BlockSpec((tt, CI), cur),
            pl.BlockSpec((HALO, CI), prev),
            pl.BlockSpec((tt, D), cur),
            pl.BlockSpec((HALO, D), nxt),
            pl.BlockSpec((tt, CC), cur),
            pl.BlockSpec((HALO, CC), nxt),
            pl.BlockSpec(w_dw4.shape, lambda i: (0,) * w_dw4.ndim),
            pl.BlockSpec((1, CC), c2),
            pl.BlockSpec((1, CC), c2),
            pl.BlockSpec(w_pool_b.shape, c3),
            pl.BlockSpec((1, CP), c2),
        ],
        [
            pl.BlockSpec((tt, CI), cur),
            pl.BlockSpec((n_cc, HALO, LANES), c3),
            pl.BlockSpec((1, CC), c2),
            pl.BlockSpec((1, CC), c2),
            pl.BlockSpec((1, CC), c2),
            pl.BlockSpec((n_grp, G, G), c3),
            pl.BlockSpec((1, CP), c2),
            pl.BlockSpec((1, CI), c2),
        ],
        [
            jax.ShapeDtypeStruct((T, CI), BF16),
            jax.ShapeDtypeStruct((n_cc, HALO, LANES), F32),
            jax.ShapeDtypeStruct((1, CC), F32),
            jax.ShapeDtypeStruct((1, CC), F32),
            jax.ShapeDtypeStruct((1, CC), F32),
            jax.ShapeDtypeStruct((n_grp, G, G), F32),
            jax.ShapeDtypeStruct((1, CP), F32),
            jax.ShapeDtypeStruct((1, CI), F32),
        ],
        (z, z, dy, dy, v, v, w_dw4, ln_g, ln_b, w_pool_b, s_pool),
        scratch=[
            pltpu.VMEM((SUBLANES, tt + HALO, CC), F32),
            pltpu.VMEM((tt, CC), F32),
            pltpu.VMEM((HALO + tt, CP), F32),
            pltpu.VMEM((tt + HALO, CP), F32),
            pltpu.VMEM((n_cc * HALO, 8, LANES), F32),
        ],
        comm=comm,
    )


def _in_proj_bwd(dz_b, w_inT_b, x, dh1, g_mix, after=()):
    T, D = x.shape
    CI = w_inT_b.shape[0]
    tm = _tile(T, 512)

    def body(dz_ref, w_ref, x_ref, dh1_ref, g_ref, dx_ref, dg_ref):
        i = pl.program_id(0)
        dxn = _dot(dz_ref[...], w_ref[...], NN)
        dx, dg_rows = _rms_bwd(x_ref[...], g_ref[...], dxn)
        dx_ref[...] = dh1_ref[...] + dx
        _accumulate(dg_ref, i == 0, jnp.sum(dg_rows, axis=0, keepdims=True))

    row = lambda i: (i, 0)
    const = lambda i: (0, 0)
    return _call(
        "in_proj_bwd",
        body,
        (T // tm,),
        [
            pl.BlockSpec((tm, CI), row),
            pl.BlockSpec((CI, D), const),
            pl.BlockSpec((tm, D), row),
            pl.BlockSpec((tm, D), row),
            pl.BlockSpec((1, D), const),
        ],
        [pl.BlockSpec((tm, D), row), pl.BlockSpec((1, D), const)],
        [jax.ShapeDtypeStruct((T, D), F32), jax.ShapeDtypeStruct((1, D), F32)],
        (dz_b, w_inT_b, x, dh1, g_mix),
        after=after,
    )


def _weight_grad(name, a_b, b_b, comm=()):
    T, N1 = a_b.shape
    N2 = b_b.shape[1]
    t1 = _tile(N1, 1408, LANES)
    tk = _tile(T, 2048)
    nk = T // tk

    def body(a_ref, b_ref, o_ref, acc):
        k = pl.program_id(1)
        _accumulate(acc, k == 0, _dot(a_ref[...], b_ref[...], TN))

        @pl.when(k == nk - 1)
        def _():
            o_ref[...] = acc[...].astype(BF16)

    (out,), rest = _call(
        name,
        body,
        (N1 // t1, nk),
        [pl.BlockSpec((tk, t1), lambda n, k: (k, n)), pl.BlockSpec((tk, N2), lambda n, k: (k, 0))],
        [pl.BlockSpec((t1, N2), lambda n, k: (n, 0))],
        [jax.ShapeDtypeStruct((N1, N2), BF16)],
        (a_b, b_b),
        scratch=[pltpu.VMEM((t1, N2), F32)],
        comm=comm,
    )
    return out, rest


def _sum_parts(name, full, how, parts, me):
    _, R, C = parts[0].shape
    tr = _tile(R, 512)
    nb = R // tr
    where = [(q, r) for q, p in enumerate(parts) for r in range(p.shape[0])]
    assert len(where) == 3

    def body(me_ref, own_ref, *refs):
        o_ref = refs[-1]
        f = lambda j: refs[where[j][0]][where[j][1]].astype(F32)
        o_ref[...] = (own_ref[...].astype(F32) + f(0)) + (f(1) + f(2))

    own_map = {"rows": lambda i, me_ref: (me_ref[0] * nb + i, 0), "all": lambda i, me_ref: (i, 0)}[how]
    return pl.pallas_call(
        body,
        name=name,
        grid_spec=pltpu.PrefetchScalarGridSpec(
            num_scalar_prefetch=1,
            grid=(nb,),
            in_specs=[pl.BlockSpec((tr, C), own_map)]
            + [pl.BlockSpec((p.shape[0], tr, C), lambda i, me_ref: (0, i, 0)) for p in parts],
            out_specs=pl.BlockSpec((tr, C), lambda i, me_ref: (i, 0)),
        ),
        out_shape=jax.ShapeDtypeStruct((R, C), F32),
        compiler_params=pltpu.CompilerParams(dimension_semantics=("arbitrary",), vmem_limit_bytes=VMEM_LIMIT),
    )(me, full, *parts)


_M_CORR = 1.0 - ADAM_B1**ADAM_STEP
_V_CORR = 1.0 - ADAM_B2**ADAM_STEP


def _adamw_math(w, g, m, v):
    m = ADAM_B1 * m + (1.0 - ADAM_B1) * g
    v = ADAM_B2 * v + (1.0 - ADAM_B2) * (g * g)
    delta = -ADAM_LR * ((m / _M_CORR) / (jnp.sqrt(v / _V_CORR) + ADAM_EPS) + ADAM_WD * w)
    return delta, m, v


def _adamw(name, w, m, v, g_here, g_there, g_transposed=False):
    R, C = w.shape
    tr = _tile(R, 256, LANES if g_transposed else 8)

    def body(w_ref, m_ref, v_ref, ga_ref, gb_ref, g_ref, d_ref, nm_ref, nv_ref):
        g = ga_ref[...] + gb_ref[...]
        if g_transposed:
            g = g.T
        g_ref[...] = g
        d_ref[...], nm_ref[...], nv_ref[...] = _adamw_math(w_ref[...], g, m_ref[...], v_ref[...])

    spec = pl.BlockSpec((tr, C), lambda i: (i, 0))
    gspec = pl.BlockSpec((C, tr), lambda i: (0, i)) if g_transposed else spec
    return _call(name, body, (R // tr,), [spec] * 3 + [gspec] * 2, [spec] * 4, [jax.ShapeDtypeStruct((R, C), F32)] * 4,
                 (w, m, v, g_here, g_there))


def _adamw_on_sparsecore(name, w, m, v, g_here, g_there, after):
    R, C = w.shape
    n_groups = R // SUBLANES
    n_turns = -(-n_groups // SC_TILES)
    n_in, n_out = 5, 4

    def body(w_hbm, m_hbm, v_hbm, ga_hbm, gb_hbm, after_hbm, g_out, d_out, nm_out, nv_out, bufs, sems):
        tile = lax.axis_index("subcore") * SC_CORES + lax.axis_index("sparsecore")
        srcs = (w_hbm, m_hbm, v_hbm, ga_hbm, gb_hbm)
        dsts = (d_out, nm_out, nv_out, g_out)

        def rows(turn):
            return pl.ds((tile + turn * SC_TILES) * SUBLANES, SUBLANES)

        def loads(turn):
            slot = turn % 2
            return [pltpu.make_async_copy(srcs[q].at[rows(turn), :], bufs.at[slot, q], sems.at[slot, q]) for q in range(n_in)]

        def stores(turn):
            slot = turn % 2
            return [pltpu.make_async_copy(bufs.at[slot, q], dsts[q].at[rows(turn), :], sems.at[slot, n_in + q])
                    for q in range(n_out)]

        def when_mine(turn, fn):
            pl.when(tile + turn * SC_TILES < n_groups)(fn)

        def compute(slot):
            wb, mb, vb, gab, gbb = (bufs.at[slot, q] for q in range(n_in))

            @pl.loop(0, SUBLANES)
            def _(r):
                @pl.loop(0, C, step=SC_LANES)
                def _(i):
                    at = (r, pl.ds(i, SC_LANES))
                    g = gab[at] + gbb[at]
                    delta, new_m, new_v = _adamw_math(wb[at], g, mb[at], vb[at])
                    gab[at], wb[at], mb[at], vb[at] = g, delta, new_m, new_v

        def start_loads(turn):
            def fn():
                for cp in loads(turn):
                    cp.start()

            when_mine(turn, fn)

        start_loads(0)
        for turn in range(n_turns):
            def step(turn=turn):
                for cp in loads(turn):
                    cp.wait()
                if turn >= 1:
                    for cp in stores(turn - 1):
                        cp.wait()
                if turn + 1 < n_turns:
                    start_loads(turn + 1)
                compute(turn % 2)
                for cp in stores(turn):
                    cp.start()

            when_mine(turn, step)
        for turn in range(n_turns):
            def drain(turn=turn):
                for cp in stores(turn):
                    cp.wait()

            last_mine = jnp.logical_and(tile + turn * SC_TILES < n_groups, tile + (turn + 1) * SC_TILES >= n_groups)
            pl.when(last_mine)(drain)

    return pl.kernel(
        body,
        name=name,
        out_type=[jax.ShapeDtypeStruct((R, C), F32)] * 4,
        mesh=plsc.VectorSubcoreMesh(core_axis_name="sparsecore", subcore_axis_name="subcore"),
        scratch_types=[pltpu.VMEM((2, n_in, SUBLANES, C), F32), pltpu.SemaphoreType.DMA((2, n_in + n_out))],
        compiler_params=pltpu.CompilerParams(use_tc_tiling_on_sc=True),
    )(w, m, v, g_here, g_there, after)


class _PackLayout:
    def __init__(self, n_cc, n_grp, G, widths):
        self.dw_rows = (0, HALO)
        self.wp_rows = (HALO, HALO + G)
        self.n_cc, self.n_grp, self.G = n_cc, n_grp, G
        self.vec = {}
        r = HALO + G
        for name, width in widths:
            self.vec[name] = (r, width)
            r += width // PACK_W
        self.rows = -(-r // 8) * 8


def _pack_small(layout, dwdw, dwp, vecs):
    names = list(vecs)

    def body(*refs):
        dw_ref, wp_ref = refs[0], refs[1]
        vec_refs = refs[2 : 2 + len(names)]
        o_ref = refs[-1]
        o_ref[...] = jnp.zeros_like(o_ref)
        for j in range(layout.n_cc):
            o_ref[layout.dw_rows[0] : layout.dw_rows[1], j * LANES : (j + 1) * LANES] = dw_ref[j]
        for i in range(layout.n_grp):
            o_ref[layout.wp_rows[0] : layout.wp_rows[1], i * layout.G : (i + 1) * layout.G] = wp_ref[i]
        for name, ref in zip(names, vec_refs):
            r, width = layout.vec[name]
            for h in range(width // PACK_W):
                o_ref[r + h : r + h + 1, :] = ref[:, h * PACK_W : (h + 1) * PACK_W]

    return pl.pallas_call(
        body,
        name="pack_small",
        out_shape=jax.ShapeDtypeStruct((layout.rows, PACK_W), F32),
    )(dwdw, dwp, *[vecs[k] for k in names])


def _adamw_small(layout, g_here, g_there, w_dw, m_dw, v_dw, w_pool, m_pool, v_pool, vec_w, vec_m, vec_v):
    names = list(vec_w)
    nv = len(names)

    def body(*refs):
        ga_ref, gb_ref = refs[0], refs[1]
        wdw, mdw, vdw, wp, mp, vp = refs[2:8]
        vw, vm, vv = refs[8 : 8 + nv], refs[8 + nv : 8 + 2 * nv], refs[8 + 2 * nv : 8 + 3 * nv]
        outs = refs[8 + 3 * nv :]
        acc = outs[-1]
        acc[...] = ga_ref[...] + gb_ref[...]

        def emit(o, g, w, m, v, idx=()):
            res = (g,) + _adamw_math(w, g, m, v)
            for ref, val in zip(o, res):
                ref[idx] = val

        me = 2 * lax.axis_index("x") + lax.axis_index("y")
        for j in range(layout.n_cc):

            @pl.when(me == j)
            def _(j=j):
                for k in range(wdw.shape[0]):
                    g = acc[layout.dw_rows[0] + k : layout.dw_rows[0] + k + 1, j * LANES : (j + 1) * LANES]
                    emit(outs[0:4], g, wdw[k], mdw[k], vdw[k], idx=k)

        for i in range(layout.n_grp):
            g = acc[layout.wp_rows[0] : layout.wp_rows[1], i * layout.G : (i + 1) * layout.G]
            emit(outs[4:8], g, wp[i], mp[i], vp[i], idx=i)
        for q, name in enumerate(names):
            r, width = layout.vec[name]
            for h in range(width // PACK_W):
                ls = slice(h * PACK_W, (h + 1) * PACK_W)
                g = acc[r + h : r + h + 1, :]
                emit(outs[8 + 4 * q : 12 + 4 * q], g, vw[q][:, ls], vm[q][:, ls], vv[q][:, ls], idx=(slice(None), ls))

    shapes = [w_dw.shape] * 4 + [w_pool.shape] * 4
    for name in names:
        shapes += [vec_w[name].shape] * 4
    return pl.pallas_call(
        body,
        name="adamw_small",
        out_shape=[jax.ShapeDtypeStruct(s, F32) for s in shapes],
        scratch_shapes=[pltpu.VMEM(g_here.shape, F32)],
    )(g_here, g_there, w_dw, m_dw, v_dw, w_pool, m_pool, v_pool,
      *[vec_w[k] for k in names], *[vec_m[k] for k in names], *[vec_v[k] for k in names])


def _allreduce_adamw_row(g_part, w, m, v, loss_part, comm=()):
    D = w.shape[1]
    n_pairs = N_DEV - 1

    def body(g_ref, w_ref, m_ref, v_ref, l_ref, go_ref, d_ref, nm_ref, nv_ref, lo_ref, land_g, land_l, sems):
        x, y, c = _place()
        copies = []
        for q, (src, land) in enumerate(((g_ref, land_g), (l_ref, land_l))):
            for r in range(1, N_DEV):
                fx, fy, fc = (r >> 2) & 1, (r >> 1) & 1, r & 1
                peer = (1 - x if fx else x, 1 - y if fy else y, 1 - c if fc else c)
                cp = _remote(src, land.at[r], sems, 2 * (q * n_pairs + r - 1), peer)
                cp.start()
                copies.append(cp)
        for cp in copies:
            cp.wait()

        def total(src, land):
            row = lambda r: src[...] if r == 0 else land[r]
            return ((row(0) + row(4)) + (row(2) + row(6))) + ((row(1) + row(5)) + (row(3) + row(7)))

        g = total(g_ref, land_g)
        go_ref[...] = g
        d_ref[...], nm_ref[...], nv_ref[...] = _adamw_math(w_ref[...], g, m_ref[...], v_ref[...])
        lo_ref[...] = total(l_ref, land_l)

    vm = pl.BlockSpec(memory_space=pltpu.VMEM)
    return _call(
        "allreduce_adamw_g_mix",
        body,
        (),
        [vm] * 5,
        [vm] * 5,
        [jax.ShapeDtypeStruct((1, D), F32)] * 4 + [jax.ShapeDtypeStruct(loss_part.shape, F32)],
        (g_part, w, m, v, loss_part),
        scratch=[pltpu.VMEM((N_DEV, 1, D), F32), pltpu.VMEM((N_DEV,) + loss_part.shape, F32),
                 pltpu.SemaphoreType.DMA((4 * n_pairs,))],
        comm=comm,
    )


def kernel(x, g_mix, w_in, b_in, w_dw, b_dw, ln_g, ln_b, w_pool, s_pool, w_out, g_ffn, w_gate, w_up, w_down, g_final, loss_target, m_g_mix, m_w_in, m_b_in, m_w_dw, m_b_dw, m_ln_g, m_ln_b, m_w_pool, m_s_pool, m_w_out, m_g_ffn, m_w_gate, m_w_up, m_w_down, m_g_final, v_g_mix, v_w_in, v_b_in, v_w_dw, v_b_dw, v_ln_g, v_ln_b, v_w_pool, v_s_pool, v_w_out, v_g_ffn, v_w_gate, v_w_up, v_w_down, v_g_final):
    x2 = x[0]
    target = loss_target[0]
    T, D = x2.shape
    w_in2, w_out2, w_down2 = w_in[0], w_out[0], w_down[0]
    taps_first = lambda a: jnp.transpose(a, (1, 0, 2))
    w_dw3 = taps_first(w_dw)
    w_gateT, w_upT = w_gate[0].T, w_up[0].T
    CI = w_in2.shape[1] * N_CHIPS
    DM = w_out2.shape[0] * N_CHIPS
    F = w_down2.shape[0] * N_CHIPS
    KW, _, dw_cols = w_dw3.shape
    assert dw_cols == LANES
    n_grp, G = w_pool.shape[1], w_pool.shape[-1]
    w_pool3 = w_pool[0]
    g_final2 = g_final.reshape(1, D)

    me = (2 * lax.axis_index("x") + lax.axis_index("y")).astype(jnp.int32).reshape(1)

    w_inT_b, w_dw4, f_out, f_gate, f_up, f_down = _place_and_gather(
        [(w_in2, "rows", (CI, D), BF16, True, True), (w_dw3, "lead", (N_CHIPS, KW, 1, dw_cols), F32, False, False)],
        [(w, "rows", shape, BF16, False, True)
         for w, shape in ((w_out2, (DM, D)), (w_gateT, (F, D)), (w_upT, (F, D)), (w_down2, (F, D)))])
    w_pool_b = w_pool3.astype(BF16)
    ici = lambda f: _GatherIci([f], ["rows"], [True])
    d2d = lambda f: _GatherD2d([f], ["rows"])
    gather = _start("gather_start", [ici(f_out), ici(f_gate), ici(f_up), ici(f_down)])
    (z, xn_b), _ = _in_proj(x2, g_mix, w_inT_b, b_in, after=[gather.token])
    (f_out,) = _wait("gather_out_wait", gather, 0, xn_b)
    s_out = _start("share_out_start", [d2d(f_out)], sibling_only=True)
    (y_b, v), _ = _seq_fwd(z, w_dw4, b_dw, ln_g, ln_b, w_pool_b, s_pool, after=[s_out.token])
    (w_out_b,) = _wait("share_out_wait", s_out, 0, y_b)
    (f_gate,) = _wait("gather_gate_wait", gather, 1, y_b)
    s_gate = _start("share_gate_start", [d2d(f_gate)], sibling_only=True)
    (h1, hn_b), _ = _out_proj(y_b, x2, w_out_b, g_ffn, after=[s_gate.token])
    (f_up,) = _wait("gather_up_wait", gather, 2, hn_b)
    s_up = _start("share_up_start", [d2d(f_up)], sibling_only=True)
    (wgT_b,) = _wait("share_gate_wait", s_gate, 0, hn_b)
    (wuT_b,) = _wait("share_up_wait", s_up, 0, hn_b)
    (silu_b, uds_b, a_b), _ = _gate_up(hn_b, wgT_b, wuT_b)
    (f_down,) = _wait("gather_down_wait", gather, 3, a_b)
    s_down = _start("share_down_start", [d2d(f_down)], sibling_only=True)
    (wd_b,) = _wait("share_down_wait", s_down, 0, a_b)
    (dh2, dh2_b, loss_part, d_g_final), _ = _down_loss(a_b, wd_b, h1, target, g_final2)

    gw_down, _ = _weight_grad("grad_w_down", a_b, dh2_b)
    (dg_b, du_b), (p_down_xy,) = _ffn_bwd_act(
        dh2_b, wd_b, silu_b, uds_b, comm=[_Scatter([gw_down], ["rows"], which=(0, 1))])
    gw_gateT, (p_down_d,) = _weight_grad("grad_w_gate", dg_b, hn_b, comm=[_Scatter([gw_down], ["rows"], which=(2,))])
    gw_upT, _ = _weight_grad("grad_w_up", du_b, hn_b)
    sum_down = _sum_parts("sum_w_down", gw_down, "rows", [p_down_xy, p_down_d], me)
    (dh1, dh1_b, dy, d_g_ffn), (p_gate, oth_down) = _ffn_bwd_in(
        dg_b, du_b, wgT_b, wuT_b, h1, dh2, g_ffn, w_out_b, comm=[_Scatter([gw_gateT], ["rows"]), _Swap([sum_down])])
    gw_out, _ = _weight_grad("grad_w_out", y_b, dh1_b)
    sum_gate = _sum_parts("sum_w_gate", gw_gateT, "rows", [p_gate], me)
    res = {}
    res["w_down"] = _adamw_on_sparsecore("adamw_w_down", w_down2, m_w_down[0], v_w_down[0], sum_down, oth_down, sum_down)
    (dz_b, d_wdw, d_bdw, d_lng, d_lnb, d_wp, d_sp, d_bin), (p_up, p_out, oth_gate) = _seq_bwd(
        z, dy, v, w_dw4, ln_g, ln_b, w_pool_b, s_pool,
        comm=[_Scatter([gw_upT, gw_out], ["rows", "rows"]), _Swap([sum_gate])])
    res["w_gate"] = _adamw_on_sparsecore(
        "adamw_w_gate", w_gateT, m_w_gate[0].T, v_w_gate[0].T, sum_gate, oth_gate, res["w_down"][0])
    vec_grads ={"b_dw": d_bdw, "ln_g": d_lng, "ln_b": d_lnb, "s_pool": d_sp, "g_ffn": d_g_ffn, "g_final": d_g_final, "b_in": d_bin}
    layout = _PackLayout(dw_cols * N_CHIPS // LANES, n_grp, G, [(k, a.shape[1]) for k, a in vec_grads.items()])
    pack = _pack_small(layout, d_wdw, d_wp, vec_grads)
    sum_up = _sum_parts("sum_w_up", gw_upT, "rows", [p_up], me)
    sum_out = _sum_parts("sum_w_out", gw_out, "rows", [p_out], me)
    gw_inT, (p_small, oth_up, oth_out) = _weight_grad(
        "grad_w_in", dz_b, xn_b, comm=[_Scatter([pack], ["all"]), _Swap([sum_up, sum_out])])
    sum_small = _sum_parts("sum_small", pack, "all", [p_small], me)
    late = _start("late_start", [_Scatter([gw_inT], ["rows"]), _Swap([sum_small])])
    (grad_x, d_g_mix), _ = _in_proj_bwd(dz_b, w_inT_b, x2, dh1, g_mix, after=[late.token])
    gw_inT, p_in = _wait("late_w_in_wait", late, 0, d_g_mix)
    sum_small, oth_small = _wait("late_small_wait", late, 1, d_g_mix)
    res["w_up"] = _adamw_on_sparsecore("adamw_w_up", w_upT, m_w_up[0].T, v_w_up[0].T, sum_up, oth_up, res["w_gate"][0])
    res["w_out"] = _adamw_on_sparsecore("adamw_w_out", w_out2, m_w_out[0], v_w_out[0], sum_out, oth_out, res["w_gate"][0])
    sum_in = _sum_parts("sum_w_in", gw_inT, "rows", [p_in], me)
    (*res["g_mix"], loss_row), (oth_in,) = _allreduce_adamw_row(
        d_g_mix, g_mix, m_g_mix, v_g_mix, loss_part, comm=[_Swap([sum_in])])
    loss = loss_row[0, 0]
    res["w_in"], _ = _adamw("adamw_w_in", w_in2, m_w_in[0], v_w_in[0], sum_in, oth_in, g_transposed=True)

    vec_w = {"b_dw": b_dw, "ln_g": ln_g, "ln_b": ln_b, "s_pool": s_pool, "g_ffn": g_ffn, "g_final": g_final2, "b_in": b_in}
    vec_m = {"b_dw": m_b_dw, "ln_g": m_ln_g, "ln_b": m_ln_b, "s_pool": m_s_pool, "g_ffn": m_g_ffn,
             "g_final": m_g_final.reshape(1, D), "b_in": m_b_in}
    vec_v = {"b_dw": v_b_dw, "ln_g": v_ln_g, "ln_b": v_ln_b, "s_pool": v_s_pool, "g_ffn": v_g_ffn,
             "g_final": v_g_final.reshape(1, D), "b_in": v_b_in}
    small = _adamw_small(layout, sum_small, oth_small, w_dw3, taps_first(m_w_dw), taps_first(v_w_dw),
                         w_pool3, m_w_pool[0], v_w_pool[0], vec_w, vec_m, vec_v)
    res["w_dw"] = [taps_first(a) for a in small[0:4]]
    res["w_pool"] = [a[None] for a in small[4:8]]
    for q, k in enumerate(vec_w):
        res[k] = list(small[8 + 4 * q : 12 + 4 * q])
    res["g_final"] = [a.reshape(D) for a in res["g_final"]]
    for k in ("w_in", "w_out", "w_down"):
        res[k] = [a[None] for a in res[k]]
    for k in ("w_gate", "w_up"):
        res[k] = [a.T[None] for a in res[k]]

    order = ["g_mix", "w_in", "b_in", "w_dw", "b_dw", "ln_g", "ln_b", "w_pool", "s_pool", "w_out", "g_ffn", "w_gate", "w_up", "w_down", "g_final"]
    outs = [loss, grad_x[None]]
    for q in range(4):
        outs += [res[k][q] for k in order]
    return tuple(outs)
```

```python
import jax
import jax.numpy as jnp
from jax import lax
from jax.experimental import pallas as pl
from jax.experimental.pallas import tpu as pltpu
from jax.experimental.pallas import tpu_sc as plsc

F32 = jnp.float32
BF16 = jnp.bfloat16
MESH = pl.DeviceIdType.MESH
ANY = pl.BlockSpec(memory_space=pl.ANY)

RMS_EPS = 1e-6
LN_EPS = 1e-5
POOL_WINDOWS = (2, 4, 8, 16)
ADAM_LR = 0.001
ADAM_B1 = 0.9
ADAM_B2 = 0.999
ADAM_EPS = 1e-08
ADAM_WD = 0.01
ADAM_STEP = 10

LANES = 128
SUBLANES = 8
BF16_ROWS = 16
HALO = 32
CONV_ROWS = 64
HIDDEN_CHUNK = 512
VMEM_LIMIT = 56 * 1024 * 1024
PACK_W = 512
N_CHIPS = 4
N_DEV = 8
SIBLING_BARRIER_ID = 0
SC_CORES = 2
SC_TILES = 32
SC_LANES = 16


def _tile(n, want, mult=8):
    t = min(n, want)
    while n % t or t % mult:
        t -= 1
    return t


def _sigmoid(x):
    return 1.0 / (1.0 + jnp.exp(-x))


def _dot(a, b, dims):
    return lax.dot_general(a, b, (dims, ((), ())), preferred_element_type=F32)


NN = ((1,), (0,))
NT = ((1,), (1,))
TN = ((0,), (0,))


def _rms_bwd(x, g, dy):
    r = lax.rsqrt(jnp.mean(x * x, axis=-1, keepdims=True) + RMS_EPS)
    xh = x * r
    gy = dy * g
    dx = r * (gy - xh * jnp.mean(gy * xh, axis=-1, keepdims=True))
    return dx, dy * xh


def _accumulate(ref, first, val):
    @pl.when(first)
    def _():
        ref[...] = val

    @pl.when(jnp.logical_not(first))
    def _():
        ref[...] += val


def _place():
    return lax.axis_index("x"), lax.axis_index("y"), lax.axis_index("c")


def _other_chips(x, y):
    return [(1 - x, y), (x, 1 - y), (1 - x, 1 - y)]


def _rows(ref, start, n):
    return ref.at[pl.ds(pl.multiple_of(start, BF16_ROWS), n)]


def _window(ref, how, k, c=None):
    if how == "all":
        return ref
    if how == "lead":
        return ref.at[k]
    assert how == "rows"
    n = ref.shape[0] // N_CHIPS
    if c is None:
        return _rows(ref, k * n, n)
    return _rows(ref, k * n + c * (n // 2), n // 2)


def _remote(src, dst, sems, s, device):
    return pltpu.make_async_remote_copy(
        src_ref=src, dst_ref=dst, send_sem=sems.at[s], recv_sem=sems.at[s + 1], device_id=device, device_id_type=MESH)


class _GatherIci:
    aliased = True

    def __init__(self, fulls, hows, splits, which=(0, 1, 2)):
        self.fulls, self.hows, self.splits, self.which = list(fulls), list(hows), list(splits), tuple(which)

    def inputs(self):
        return self.fulls

    def out_shapes(self):
        return [jax.ShapeDtypeStruct(a.shape, a.dtype) for a in self.fulls]

    def n_sems(self):
        return 6 * len(self.fulls)

    def build(self, ins, outs, sems, base):
        x, y, c = _place()
        me = 2 * x + y
        chips = _other_chips(x, y)
        starts, waits = [], []
        for a, (how, sp) in enumerate(zip(self.hows, self.splits)):
            half = c if sp else None
            mine = _window(outs[a], how, me, half)
            for j in self.which:
                px, py = chips[j]
                s = base + 6 * a + 2 * j
                cp = _remote(mine, mine, sems, s, (px, py, c))
                landing = _remote(mine, _window(outs[a], how, 2 * px + py, half), sems, s, (px, py, c))
                starts.append(cp.start)
                waits += [landing.wait_recv, cp.wait_send]
        return starts, waits


class _GatherD2d:
    aliased = True

    def __init__(self, fulls, hows):
        self.fulls, self.hows = list(fulls), list(hows)

    def inputs(self):
        return self.fulls

    def out_shapes(self):
        return [jax.ShapeDtypeStruct(a.shape, a.dtype) for a in self.fulls]

    def n_sems(self):
        return 6 * len(self.fulls)

    def build(self, ins, outs, sems, base):
        x, y, c = _place()
        starts, waits = [], []
        for a, how in enumerate(self.hows):
            for j, (px, py) in enumerate(_other_chips(x, y)):
                s = base + 6 * a + 2 * j
                got = _window(outs[a], how, 2 * px + py, c)
                cp = _remote(got, got, sems, s, (x, y, 1 - c))
                landing = _remote(got, _window(outs[a], how, 2 * px + py, 1 - c), sems, s, (x, y, 1 - c))
                starts.append(cp.start)
                waits += [landing.wait_recv, cp.wait_send]
        return starts, waits


def _part_shape(a, how):
    if how == "all":
        return a.shape
    assert how == "rows"
    return (a.shape[0] // N_CHIPS, a.shape[1])


class _Scatter:
    aliased = False

    def __init__(self, fulls, hows, which=(0, 1, 2), block=(0, 1)):
        self.fulls, self.hows, self.which, self.block = list(fulls), list(hows), tuple(which), block

    def inputs(self):
        return self.fulls

    def _shape(self, a, how):
        rows, cols = _part_shape(a, how)
        return (rows // self.block[1], cols)

    def out_shapes(self):
        return [jax.ShapeDtypeStruct((len(self.which),) + self._shape(a, h), a.dtype) for a, h in zip(self.fulls, self.hows)]

    def n_sems(self):
        return 6 * len(self.fulls)

    def build(self, ins, outs, sems, base):
        x, y, c = _place()
        chips = _other_chips(x, y)
        starts, waits = [], []
        for a, how in enumerate(self.hows):
            rows = outs[a].shape[1]
            for slot, j in enumerate(self.which):
                px, py = chips[j]
                src = _window(ins[a], how, 2 * px + py)
                if self.block[1] > 1:
                    src = src.at[pl.ds(self.block[0] * rows, rows)]
                cp = _remote(src, outs[a].at[slot], sems, base + 6 * a + 2 * j, (px, py, c))
                starts.append(cp.start)
                waits += [cp.wait_recv, cp.wait_send]
        return starts, waits


class _Swap:
    aliased = False

    def __init__(self, arrays):
        self.arrays = list(arrays)

    def inputs(self):
        return self.arrays

    def out_shapes(self):
        return [jax.ShapeDtypeStruct(a.shape, a.dtype) for a in self.arrays]

    def n_sems(self):
        return 2 * len(self.arrays)

    def build(self, ins, outs, sems, base):
        x, y, c = _place()
        starts, waits = [], []
        for a in range(len(ins)):
            cp = _remote(ins[a], outs[a], sems, base + 2 * a, (x, y, 1 - c))
            starts.append(cp.start)
            waits += [cp.wait_recv, cp.wait_send]
        return starts, waits


def _call(name, body, grid, in_specs, out_specs, out_shape, args, scratch=(), comm=(), after=()):
    comm, after = list(comm), list(after)
    n_in, n_out, n_scr, n_after = len(args), len(out_shape), len(scratch), len(after)
    c_in = [a for op in comm for a in op.inputs()]
    c_out = [s for op in comm for s in op.out_shapes()]
    n_sems = sum(op.n_sems() for op in comm)
    aliases, i_in, i_out = {}, 0, 0
    for op in comm:
        if op.aliased:
            for q in range(len(op.inputs())):
                aliases[n_in + n_after + i_in + q] = n_out + i_out + q
        i_in, i_out = i_in + len(op.inputs()), i_out + len(op.out_shapes())

    def wrapped(*refs):
        ins = refs[:n_in]
        cin = refs[n_in + n_after : n_in + n_after + len(c_in)]
        o0 = n_in + n_after + len(c_in)
        outs = refs[o0 : o0 + n_out]
        cout = refs[o0 + n_out : o0 + n_out + len(c_out)]
        s0 = o0 + n_out + len(c_out)
        scr = refs[s0 : s0 + n_scr]

        def copies():
            sems = refs[s0 + n_scr]
            starts, waits = [], []
            i_in = i_out = base = 0
            for op in comm:
                ni, no = len(op.inputs()), len(op.out_shapes())
                s, w = op.build(cin[i_in : i_in + ni], cout[i_out : i_out + no], sems, base)
                starts += s
                waits += w
                i_in, i_out, base = i_in + ni, i_out + no, base + op.n_sems()
            return starts, waits

        def run_starts():
            for start in copies()[0]:
                start()

        def run_waits():
            for wait in copies()[1]:
                wait()

        if comm and grid:
            first = last = True
            for d, n in enumerate(grid):
                first = jnp.logical_and(first, pl.program_id(d) == 0)
                last = jnp.logical_and(last, pl.program_id(d) == n - 1)
            pl.when(first)(run_starts)
        elif comm:
            run_starts()
        if body is not None:
            body(*ins, *outs, *scr)
        if comm and grid:
            pl.when(last)(run_waits)
        elif comm:
            run_waits()

    res = pl.pallas_call(
        wrapped,
        name=name,
        grid=grid,
        in_specs=list(in_specs) + [ANY] * (n_after + len(c_in)),
        out_specs=list(out_specs) + [ANY] * len(c_out),
        out_shape=list(out_shape) + c_out,
        scratch_shapes=list(scratch) + ([pltpu.SemaphoreType.DMA((n_sems,))] if comm else []),
        input_output_aliases=aliases,
        compiler_params=pltpu.CompilerParams(dimension_semantics=("arbitrary",) * len(grid), vmem_limit_bytes=VMEM_LIMIT),
    )(*args, *after, *c_in)
    return tuple(res[:n_out]), tuple(res[n_out:])


def _place_and_gather(now, later):
    items = list(now) + list(later)
    n, n_now = len(items), len(now)
    buf_shape = lambda it: it[0].shape[::-1] if it[4] else it[0].shape
    split_now = [a for a in range(n_now) if items[a][5]]

    def body(*refs):
        ins, outs = refs[:n], refs[n : 2 * n]
        stage, bufs = refs[2 * n : 3 * n - n_now], refs[3 * n - n_now : 4 * n - n_now]
        sems = refs[4 * n - n_now]
        x, y, c = _place()
        me = 2 * x + y
        chips = _other_chips(x, y)
        loads = [pltpu.make_async_copy(ins[a], stage[a - n_now], sems.at[a]) for a in range(n_now, n)]
        for ld in loads:
            ld.start()
        pending = []

        def place(a, val):
            _, how, _, dtype, transposed, _ = items[a]
            bufs[a][...] = (val.T if transposed else val).astype(dtype)
            cp = pltpu.make_async_copy(bufs[a], _window(outs[a], how, me), sems.at[n + a])
            cp.start()
            pending.append(cp.wait)

        arrivals = []
        for a in range(n_now):
            place(a, ins[a][...])
            how, split = items[a][1], items[a][5]
            half = c if split else None
            src = _rows(bufs[a], c * (bufs[a].shape[0] // 2), bufs[a].shape[0] // 2) if split else bufs[a]
            for j, (px, py) in enumerate(chips):
                s = 2 * n + 6 * a + 2 * j
                cp = _remote(src, _window(outs[a], how, me, half), sems, s, (px, py, c))
                landing = _remote(src, _window(outs[a], how, 2 * px + py, half), sems, s, (px, py, c))
                cp.start()
                arrivals.append(landing.wait_recv)
                pending.append(cp.wait_send)
        for a in range(n_now, n):
            loads[a - n_now].wait()
            place(a, stage[a - n_now][...])
        for wait in arrivals:
            wait()
        d2d = _GatherD2d([None] * len(split_now), [items[a][1] for a in split_now])
        starts, waits = d2d.build(None, [outs[a] for a in split_now], sems, 2 * n + 6 * n_now)
        for start in starts:
            start()
        for wait in waits + pending:
            wait()

    vm = pl.BlockSpec(memory_space=pltpu.VMEM)
    return pl.pallas_call(
        body,
        name="place_and_gather",
        in_specs=[vm] * n_now + [ANY] * (n - n_now),
        out_specs=[ANY] * n,
        out_shape=[jax.ShapeDtypeStruct(it[2], it[3]) for it in items],
        scratch_shapes=[pltpu.VMEM(it[0].shape, it[0].dtype) for it in later]
        + [pltpu.VMEM(buf_shape(it), it[3]) for it in items]
        + [pltpu.SemaphoreType.DMA((2 * n + 6 * n_now + 6 * len(split_now),))],
        compiler_params=pltpu.CompilerParams(vmem_limit_bytes=VMEM_LIMIT),
    )(*[it[0] for it in items])


_HBM = pl.BlockSpec(memory_space=pltpu.HBM)
_SEM = pl.BlockSpec(memory_space=pltpu.SEMAPHORE)
_DATAFLOW = pltpu.SideEffectType.DATAFLOW_SIDE_EFFECTING


class _Pending:
    def __init__(self, ops, bases, sems, arrays, token):
        self.ops, self.bases, self.sems, self.arrays, self.token = ops, bases, sems, arrays, token


def _op_refs(op, refs):
    n_src = len(op.inputs())
    return refs[:n_src], (refs[:n_src] if op.aliased else refs[n_src:])


def _start(name, ops, sibling_only=False):
    per_op = [list(op.inputs()) + ([] if op.aliased else [lax.empty(sd.shape, sd.dtype) for sd in op.out_shapes()])
              for op in ops]
    arrays = [a for group in per_op for a in group]
    bases = [sum(op.n_sems() for op in ops[:k]) for k in range(len(ops))]
    n = len(arrays)

    def body(*refs):
        sems, token = refs[n], refs[-1]
        if sibling_only:
            x, y, c = _place()
            barrier = pltpu.get_barrier_semaphore()
            pl.semaphore_signal(barrier, inc=1, device_id=(x, y, 1 - c), device_id_type=MESH)
            pl.semaphore_wait(barrier, 1)
        at = 0
        for op, group, base in zip(ops, per_op, bases):
            starts, _ = op.build(*_op_refs(op, refs[at : at + len(group)]), sems, base)
            for start in starts:
                start()
            at += len(group)
        token[...] = jnp.zeros_like(token)

    res = pl.pallas_call(
        body,
        name=name,
        out_shape=(pltpu.SemaphoreType.DMA((sum(op.n_sems() for op in ops),)),)
        + tuple(pltpu.HBM(a.shape, a.dtype) for a in arrays) + (jax.ShapeDtypeStruct((SUBLANES, LANES), F32),),
        in_specs=(_HBM,) * n,
        out_specs=(_SEM,) + (_HBM,) * n + (pl.BlockSpec(memory_space=pltpu.VMEM),),
        input_output_aliases={i: 1 + i for i in range(n)},
        compiler_params=pltpu.CompilerParams(
            has_side_effects=_DATAFLOW, collective_id=SIBLING_BARRIER_ID if sibling_only else None),
    )(*[pltpu.with_memory_space_constraint(a, pltpu.HBM) for a in arrays])
    thru, at, groups = list(res[1 : 1 + n]), 0, []
    for group in per_op:
        groups.append(thru[at : at + len(group)])
        at += len(group)
    return _Pending(list(ops), bases, res[0], groups, res[-1])


def _wait(name, pending, k, after):
    op, arrays = pending.ops[k], pending.arrays[k]
    n = len(arrays)

    def body(*refs):
        _, waits = op.build(*_op_refs(op, refs[:n]), refs[n], pending.bases[k])
        for wait in waits:
            wait()

    return pl.pallas_call(
        body,
        name=name,
        out_shape=tuple(pltpu.HBM(a.shape, a.dtype) for a in arrays),
        in_specs=(_HBM,) * n + (_SEM, ANY),
        out_specs=(_HBM,) * n,
        input_output_aliases={i: i for i in range(n)},
        compiler_params=pltpu.CompilerParams(has_side_effects=_DATAFLOW),
    )(*arrays, pending.sems, after)


def _in_proj(x, g_mix, w_inT_b, b_in, after=()):
    T, D = x.shape
    CI = w_inT_b.shape[0]
    tm = _tile(T, 512)

    def body(x_ref, g_ref, w_ref, b_ref, z_ref, xn_ref):
        xv = x_ref[...]
        r = lax.rsqrt(jnp.mean(xv * xv, axis=-1, keepdims=True) + RMS_EPS)
        xn = (xv * r * g_ref[...]).astype(BF16)
        xn_ref[...] = xn
        z_ref[...] = _dot(xn, w_ref[...], NT) + b_ref[...]

    return _call(
        "in_proj",
        body,
        (T // tm,),
        [
            pl.BlockSpec((tm, D), lambda i: (i, 0)),
            pl.BlockSpec((1, D), lambda i: (0, 0)),
            pl.BlockSpec((CI, D), lambda i: (0, 0)),
            pl.BlockSpec((1, CI), lambda i: (0, 0)),
        ],
        [pl.BlockSpec((tm, CI), lambda i: (i, 0)), pl.BlockSpec((tm, D), lambda i: (i, 0))],
        [jax.ShapeDtypeStruct((T, CI), F32), jax.ShapeDtypeStruct((T, D), BF16)],
        (x, g_mix, w_inT_b, b_in),
        after=after,
    )


def _fill_shifted(scr):
    n = scr.shape[1] - SUBLANES
    for s in range(1, SUBLANES):
        scr[s, 0:n, :] = scr[0, s : s + n, :]


def _shifted_rows(scr, off, n, cs):
    s = off % SUBLANES
    return scr[s, off - s : off - s + n, cs]


def _pool_mean_minus_token(p_scr, cs, w, cnt, tt):
    tok = p_scr[HALO : HALO + tt, cs]
    s = tok
    for d in range(1, w):
        s = s + p_scr[HALO - d : HALO - d + tt, cs]
    return s / cnt - tok


def _seq_fwd(z, w_dw4, b_dw, ln_g, ln_b, w_pool_b, s_pool, after=()):
    T, CI = z.shape
    CC = ln_g.shape[1]
    n_grp, G = w_pool_b.shape[0], w_pool_b.shape[-1]
    KW = w_dw4.shape[1]
    D = CC + n_grp * G
    tt = _tile(T, 512, HALO)
    per = tt // HALO

    def body(zc_ref, zp_ref, wdw_ref, bdw_ref, lng_ref, lnb_ref, wp_ref, sp_ref, y_ref, v_ref, u_scr, p_scr):
        i = pl.program_id(0)
        first = i == 0
        u_prev = zp_ref[:, 0:CC] * _sigmoid(zp_ref[:, CC : 2 * CC])
        u_scr[0, 0:HALO, :] = jnp.where(first, 0.0, u_prev)
        p_scr[0:HALO, :] = jnp.where(first, 0.0, zp_ref[:, 2 * CC :])
        u_scr[0, HALO:, :] = zc_ref[:, 0:CC] * _sigmoid(zc_ref[:, CC : 2 * CC])
        p_scr[HALO:, :] = zc_ref[:, 2 * CC :]
        _fill_shifted(u_scr)

        for j in range(CC // LANES):
            cs = slice(LANES * j, LANES * (j + 1))
            for rb in range(tt // CONV_ROWS):
                acc = jnp.zeros((CONV_ROWS, LANES), F32)
                for k in range(KW):
                    off = HALO - (KW - 1) + k + rb * CONV_ROWS
                    acc = acc + _shifted_rows(u_scr, off, CONV_ROWS, cs) * wdw_ref[j, k]
                v_ref[rb * CONV_ROWS : (rb + 1) * CONV_ROWS, cs] = acc + bdw_ref[:, cs]

        v = v_ref[...]
        mu = jnp.mean(v, axis=-1, keepdims=True)
        d = v - mu
        var = jnp.mean(d * d, axis=-1, keepdims=True)
        ln = d * lax.rsqrt(var + LN_EPS) * lng_ref[...] + lnb_ref[...]
        y_ref[:, 0:CC] = (ln * _sigmoid(ln)).astype(BF16)

        tpos = i * tt + lax.broadcasted_iota(jnp.int32, (tt, 1), 0)
        for gi, w in enumerate(POOL_WINDOWS):
            cs = slice(G * gi, G * (gi + 1))
            cnt = jnp.minimum(tpos + 1, w).astype(F32)
            yi = _pool_mean_minus_token(p_scr, cs, w, cnt, tt)
            q = _dot(yi.astype(BF16), wp_ref[gi], NN)
            y_ref[:, CC + G * gi : CC + G * (gi + 1)] = (q * sp_ref[:, cs]).astype(BF16)

    const2 = lambda i: (0, 0)
    return _call(
        "seq_fwd",
        body,
        (T // tt,),
        [
            pl.BlockSpec((tt, CI), lambda i: (i, 0)),
            pl.BlockSpec((HALO, CI), lambda i: (jnp.maximum(i * per - 1, 0), 0)),
            pl.BlockSpec(w_dw4.shape, lambda i: (0,) * w_dw4.ndim),
            pl.BlockSpec((1, CC), const2),
            pl.BlockSpec((1, CC), const2),
            pl.BlockSpec((1, CC), const2),
            pl.BlockSpec(w_pool_b.shape, lambda i: (0, 0, 0)),
            pl.BlockSpec((1, n_grp * G), const2),
        ],
        [pl.BlockSpec((tt, D), lambda i: (i, 0)), pl.BlockSpec((tt, CC), lambda i: (i, 0))],
        [jax.ShapeDtypeStruct((T, D), BF16), jax.ShapeDtypeStruct((T, CC), F32)],
        (z, z, w_dw4, b_dw, ln_g, ln_b, w_pool_b, s_pool),
        scratch=[pltpu.VMEM((SUBLANES, HALO + tt, CC), F32), pltpu.VMEM((HALO + tt, n_grp * G), F32)],
        after=after,
    )


def _out_proj(y_b, x, w_out_b, g_ffn, after=()):
    T, D = x.shape
    tm = _tile(T, 512)

    def body(y_ref, x_ref, w_ref, g_ref, h1_ref, hn_ref):
        h1 = x_ref[...] + _dot(y_ref[...], w_ref[...], NN)
        h1_ref[...] = h1
        r = lax.rsqrt(jnp.mean(h1 * h1, axis=-1, keepdims=True) + RMS_EPS)
        hn_ref[...] = (h1 * r * g_ref[...]).astype(BF16)

    row = lambda i: (i, 0)
    return _call(
        "out_proj",
        body,
        (T // tm,),
        [
            pl.BlockSpec((tm, y_b.shape[1]), row),
            pl.BlockSpec((tm, D), row),
            pl.BlockSpec(w_out_b.shape, lambda i: (0, 0)),
            pl.BlockSpec((1, D), lambda i: (0, 0)),
        ],
        [pl.BlockSpec((tm, D), row), pl.BlockSpec((tm, D), row)],
        [jax.ShapeDtypeStruct((T, D), F32), jax.ShapeDtypeStruct((T, D), BF16)],
        (y_b, x, w_out_b, g_ffn),
        after=after,
    )


def _hidden_tile(F):
    return _tile(F, 1408, LANES)


def _gate_up(hn_b, wgT_b, wuT_b):
    T, D = hn_b.shape
    F = wgT_b.shape[0]
    tm, tf = _tile(T, 1024), _hidden_tile(F)

    def body(hn_ref, wg_ref, wu_ref, silu_ref, uds_ref, a_ref):
        hn = hn_ref[...]
        for c0 in range(0, tf, HIDDEN_CHUNK):
            cs = slice(c0, min(c0 + HIDDEN_CHUNK, tf))
            gv = _dot(hn, wg_ref[cs, :], NT)
            uv = _dot(hn, wu_ref[cs, :], NT)
            sg = _sigmoid(gv)
            silu = gv * sg
            silu_ref[:, cs] = silu.astype(BF16)
            uds_ref[:, cs] = (uv * (sg * (1.0 + gv * (1.0 - sg)))).astype(BF16)
            a_ref[:, cs] = (silu * uv).astype(BF16)

    wspec = pl.BlockSpec((tf, D), lambda j, i: (j, 0))
    ospec = pl.BlockSpec((tm, tf), lambda j, i: (i, j))
    return _call(
        "gate_up",
        body,
        (F // tf, T // tm),
        [pl.BlockSpec((tm, D), lambda j, i: (i, 0)), wspec, wspec],
        [ospec, ospec, ospec],
        [jax.ShapeDtypeStruct((T, F), BF16)] * 3,
        (hn_b, wgT_b, wuT_b),
    )


def _down_loss(a_b, wd_b, h1, target, g_final):
    T, D = h1.shape
    F = a_b.shape[1]
    tm = _tile(T, 512)
    nt = T // tm

    def body(a_ref, w_ref, h1_ref, t_ref, g_ref, dh2_ref, dh2b_ref, loss_ref, dg_ref):
        i = pl.program_id(0)
        h2 = h1_ref[...] + _dot(a_ref[...], w_ref[...], NN)
        r = lax.rsqrt(jnp.mean(h2 * h2, axis=-1, keepdims=True) + RMS_EPS)
        g = g_ref[...]
        diff = h2 * r * g - t_ref[...]
        _accumulate(loss_ref, i == 0, jnp.full(loss_ref.shape, jnp.sum(diff * diff) * (0.5 / D), F32))
        dh2, dg_rows = _rms_bwd(h2, g, diff * (1.0 / D))
        dh2_ref[...] = dh2
        dh2b_ref[...] = dh2.astype(BF16)
        _accumulate(dg_ref, i == 0, jnp.sum(dg_rows, axis=0, keepdims=True))

    row = lambda i: (i, 0)
    return _call(
        "down_loss",
        body,
        (nt,),
        [
            pl.BlockSpec((tm, F), row),
            pl.BlockSpec((F, D), lambda i: (0, 0), pipeline_mode=pl.Buffered(1)),
            pl.BlockSpec((tm, D), row),
            pl.BlockSpec((tm, D), row),
            pl.BlockSpec((1, D), lambda i: (0, 0)),
        ],
        [
            pl.BlockSpec((tm, D), row),
            pl.BlockSpec((tm, D), row),
            pl.BlockSpec((1, LANES), lambda i: (0, 0)),
            pl.BlockSpec((1, D), lambda i: (0, 0)),
        ],
        [
            jax.ShapeDtypeStruct((T, D), F32),
            jax.ShapeDtypeStruct((T, D), BF16),
            jax.ShapeDtypeStruct((1, LANES), F32),
            jax.ShapeDtypeStruct((1, D), F32),
        ],
        (a_b, wd_b, h1, target, g_final),
    )


def _ffn_bwd_act(dh2_b, wd_b, silu_b, uds_b, comm=()):
    T, D = dh2_b.shape
    F = wd_b.shape[0]
    tm, tf = _tile(T, 1024), _hidden_tile(F)

    def body(d_ref, w_ref, silu_ref, uds_ref, dg_ref, du_ref):
        d = d_ref[...]
        for c0 in range(0, tf, HIDDEN_CHUNK):
            cs = slice(c0, min(c0 + HIDDEN_CHUNK, tf))
            da = _dot(d, w_ref[cs, :], NT)
            dg_ref[:, cs] = (da * uds_ref[:, cs].astype(F32)).astype(BF16)
            du_ref[:, cs] = (da * silu_ref[:, cs].astype(F32)).astype(BF16)

    aspec = pl.BlockSpec((tm, tf), lambda j, i: (i, j))
    return _call(
        "ffn_bwd_act",
        body,
        (F // tf, T // tm),
        [pl.BlockSpec((tm, D), lambda j, i: (i, 0)), pl.BlockSpec((tf, D), lambda j, i: (j, 0)), aspec, aspec],
        [aspec, aspec],
        [jax.ShapeDtypeStruct((T, F), BF16)] * 2,
        (dh2_b, wd_b, silu_b, uds_b),
        comm=comm,
    )


def _ffn_bwd_in(dg_b, du_b, wgT_b, wuT_b, h1, dh2, g_ffn, w_out_b, comm=()):
    T, D = h1.shape
    F = wgT_b.shape[0]
    DM = w_out_b.shape[0]
    tm = _tile(T, 512)

    def body(dg_ref, du_ref, wg_ref, wu_ref, h1_ref, dh2_ref, g_ref, wo_ref, dh1_ref, dh1b_ref, dy_ref, dgf_ref):
        i = pl.program_id(0)
        dhn = _dot(dg_ref[...], wg_ref[...], NN) + _dot(du_ref[...], wu_ref[...], NN)
        dx, dg_rows = _rms_bwd(h1_ref[...], g_ref[...], dhn)
        dh1 = dh2_ref[...] + dx
        dh1b = dh1.astype(BF16)
        dh1_ref[...] = dh1
        dh1b_ref[...] = dh1b
        dy_ref[...] = _dot(dh1b, wo_ref[...], NT)
        _accumulate(dgf_ref, i == 0, jnp.sum(dg_rows, axis=0, keepdims=True))

    row = lambda i: (i, 0)
    const = lambda i: (0, 0)
    return _call(
        "ffn_bwd_in",
        body,
        (T // tm,),
        [
            pl.BlockSpec((tm, F), row),
            pl.BlockSpec((tm, F), row),
            pl.BlockSpec((F, D), const, pipeline_mode=pl.Buffered(1)),
            pl.BlockSpec((F, D), const, pipeline_mode=pl.Buffered(1)),
            pl.BlockSpec((tm, D), row),
            pl.BlockSpec((tm, D), row),
            pl.BlockSpec((1, D), const),
            pl.BlockSpec((DM, D), const, pipeline_mode=pl.Buffered(1)),
        ],
        [pl.BlockSpec((tm, D), row), pl.BlockSpec((tm, D), row), pl.BlockSpec((tm, DM), row), pl.BlockSpec((1, D), const)],
        [
            jax.ShapeDtypeStruct((T, D), F32),
            jax.ShapeDtypeStruct((T, D), BF16),
            jax.ShapeDtypeStruct((T, DM), F32),
            jax.ShapeDtypeStruct((1, D), F32),
        ],
        (dg_b, du_b, wgT_b, wuT_b, h1, dh2, g_ffn, w_out_b),
        comm=comm,
    )


def _seq_bwd(z, dy, v, w_dw4, ln_g, ln_b, w_pool_b, s_pool, comm=()):
    T, CI = z.shape
    CC = ln_g.shape[1]
    n_grp, G = w_pool_b.shape[0], w_pool_b.shape[-1]
    CP = n_grp * G
    KW = w_dw4.shape[1]
    n_cc = CC // LANES
    D = CC + CP
    tt = _tile(T, 512, HALO)
    per = tt // HALO
    n_tiles = T // tt
    last_halo = T // HALO - 1

    def body(zc_ref, zp_ref, dyc_ref, dyn_ref, vc_ref, vn_ref, wdw_ref, lng_ref, lnb_ref, wp_ref, sp_ref,
             dz_ref, dwdw_ref, dbdw_ref, dlng_ref, dlnb_ref, dwp_ref, dsp_ref, dbin_ref,
             dv_scr, u_scr, p_scr, g_scr, dw_scr):
        i = pl.program_id(0)
        first = i == 0
        last = i == n_tiles - 1
        lng, lnb = lng_ref[...], lnb_ref[...]

        def conv_pre(vv, dyc):
            mu = jnp.mean(vv, axis=-1, keepdims=True)
            d = vv - mu
            rs = lax.rsqrt(jnp.mean(d * d, axis=-1, keepdims=True) + LN_EPS)
            xh = d * rs
            ln = xh * lng + lnb
            sg = _sigmoid(ln)
            dln = dyc * (sg * (1.0 + ln * (1.0 - sg)))
            dxh = dln * lng
            dv = rs * (dxh - jnp.mean(dxh, axis=-1, keepdims=True) - xh * jnp.mean(dxh * xh, axis=-1, keepdims=True))
            return dv, dln, xh

        dv_c, dln_c, xh_c = conv_pre(vc_ref[...], dyc_ref[:, 0:CC])
        dv_scr[0, 0:tt, :] = dv_c
        dv_n, _, _ = conv_pre(vn_ref[...], dyn_ref[:, 0:CC])
        dv_scr[0, tt:, :] = jnp.where(last, 0.0, dv_n)
        _fill_shifted(dv_scr)
        _accumulate(dlng_ref, first, jnp.sum(dln_c * xh_c, axis=0, keepdims=True))
        _accumulate(dlnb_ref, first, jnp.sum(dln_c, axis=0, keepdims=True))
        _accumulate(dbdw_ref, first, jnp.sum(dv_c, axis=0, keepdims=True))

        u_scr[...] = zc_ref[:, 0:CC] * _sigmoid(zc_ref[:, CC : 2 * CC])

        @pl.when(first)
        def _():
            dw_scr[...] = jnp.zeros_like(dw_scr)

        for j in range(n_cc):
            cs = slice(LANES * j, LANES * (j + 1))
            gs = slice(CC + LANES * j, CC + LANES * (j + 1))
            dbin_a = jnp.zeros((1, LANES), F32)
            dbin_g = jnp.zeros((1, LANES), F32)
            for rb in range(tt // CONV_ROWS):
                rows = slice(rb * CONV_ROWS, (rb + 1) * CONV_ROWS)
                u_blk = u_scr[rows, cs]
                du = jnp.zeros((CONV_ROWS, LANES), F32)
                for k in range(KW):
                    off = rb * CONV_ROWS + (KW - 1) - k
                    d = _shifted_rows(dv_scr, off, CONV_ROWS, cs)
                    du = du + d * wdw_ref[j, k]
                    dw_scr[j * HALO + k] += jnp.sum((u_blk * d).reshape(CONV_ROWS // 8, 8, LANES), axis=0)
                a = zc_ref[rows, cs]
                sg = _sigmoid(zc_ref[rows, gs])
                da = du * sg
                dgate = du * a * sg * (1.0 - sg)
                dz_ref[rows, cs] = da.astype(BF16)
                dz_ref[rows, gs] = dgate.astype(BF16)
                dbin_a = dbin_a + jnp.sum(da, axis=0, keepdims=True)
                dbin_g = dbin_g + jnp.sum(dgate, axis=0, keepdims=True)
            _accumulate(dbin_ref.at[:, cs], first, dbin_a)
            _accumulate(dbin_ref.at[:, gs], first, dbin_g)

        @pl.when(last)
        def _():
            dwdw_ref[...] = jnp.sum(dw_scr[...], axis=1).reshape(dwdw_ref.shape)

        p_scr[0:HALO, :] = jnp.where(first, 0.0, zp_ref[:, 2 * CC :])
        p_scr[HALO:, :] = zc_ref[:, 2 * CC :]
        tpos = i * tt + lax.broadcasted_iota(jnp.int32, (tt, 1), 0)
        for gi, w in enumerate(POOL_WINDOWS):
            cs = slice(G * gi, G * (gi + 1))
            ys = slice(CC + G * gi, CC + G * (gi + 1))
            ps = slice(2 * CC + G * gi, 2 * CC + G * (gi + 1))
            cnt = jnp.minimum(tpos + 1, w).astype(F32)
            yib = _pool_mean_minus_token(p_scr, cs, w, cnt, tt).astype(BF16)
            wp = wp_ref[gi]
            sp = sp_ref[:, cs]
            dyp = dyc_ref[:, ys]
            q = _dot(yib, wp, NN)
            _accumulate(dsp_ref.at[:, cs], first, jnp.sum(dyp * q, axis=0, keepdims=True))
            dq_c = (dyp * sp).astype(BF16)
            dq_n = (jnp.where(last, 0.0, dyn_ref[:, ys]) * sp).astype(BF16)
            _accumulate(dwp_ref.at[gi], first, _dot(yib, dq_c, TN))
            dyi_c = _dot(dq_c, wp, NT)
            g_scr[0:tt, cs] = dyi_c / cnt
            g_scr[tt:, cs] = _dot(dq_n, wp, NT) * (1.0 / w)
            dp = -dyi_c
            for d in range(w):
                dp = dp + g_scr[d : d + tt, cs]
            dz_ref[:, ps] = dp.astype(BF16)
            _accumulate(dbin_ref.at[:, ps], first, jnp.sum(dp, axis=0, keepdims=True))

    cur = lambda i: (i, 0)
    prev = lambda i: (jnp.maximum(i * per - 1, 0), 0)
    nxt = lambda i: (jnp.minimum((i + 1) * per, last_halo), 0)
    c2 = lambda i: (0, 0)
    c3 = lambda i: (0, 0, 0)
    return _call(
        "seq_bwd",
        body,
        (n_tiles,),
        [
            pl.BlockSpec((tt, CI), cur),
            pl.BlockSpec((HALO, CI), prev),
            pl.BlockSpec((tt, D), cur),
            pl.BlockSpec((HALO, D), nxt),
            pl.BlockSpec((tt, CC), cur),
            pl.BlockSpec((HALO, CC), nxt),
            pl.BlockSpec(w_dw4.shape, lambda i: (0,) * w_dw4.ndim),
            pl.BlockSpec((1, CC), c2),
            pl.BlockSpec((1, CC), c2),
            pl.BlockSpec(w_pool_b.shape, c3),
            pl.BlockSpec((1, CP), c2),
        ],
        [
            pl.BlockSpec((tt, CI), cur),
            pl.BlockSpec((n_cc, HALO, LANES), c3),
            pl.BlockSpec((1, CC), c2),
            pl.BlockSpec((1, CC), c2),
            pl.BlockSpec((1, CC), c2),
            pl.BlockSpec((n_grp, G, G), c3),
            pl.BlockSpec((1, CP), c2),
            pl.BlockSpec((1, CI), c2),
        ],
        [
            jax.ShapeDtypeStruct((T, CI), BF16),
            jax.ShapeDtypeStruct((n_cc, HALO, LANES), F32),
            jax.ShapeDtypeStruct((1, CC), F32),
            jax.ShapeDtypeStruct((1, CC), F32),
            jax.ShapeDtypeStruct((1, CC), F32),
            jax.ShapeDtypeStruct((n_grp, G, G), F32),
            jax.ShapeDtypeStruct((1, CP), F32),
            jax.ShapeDtypeStruct((1, CI), F32),
        ],
        (z, z, dy, dy, v, v, w_dw4, ln_g, ln_b, w_pool_b, s_pool),
        scratch=[
            pltpu.VMEM((SUBLANES, tt + HALO, CC), F32),
            pltpu.VMEM((tt, CC), F32),
            pltpu.VMEM((HALO + tt, CP), F32),
            pltpu.VMEM((tt + HALO, CP), F32),
            pltpu.VMEM((n_cc * HALO, 8, LANES), F32),
        ],
        comm=comm,
    )


def _in_proj_bwd(dz_b, w_inT_b, x, dh1, g_mix, after=()):
    T, D = x.shape
    CI = w_inT_b.shape[0]
    tm = _tile(T, 512)

    def body(dz_ref, w_ref, x_ref, dh1_ref, g_ref, dx_ref, dg_ref):
        i = pl.program_id(0)
        dxn = _dot(dz_ref[...], w_ref[...], NN)
        dx, dg_rows = _rms_bwd(x_ref[...], g_ref[...], dxn)
        dx_ref[...] = dh1_ref[...] + dx
        _accumulate(dg_ref, i == 0, jnp.sum(dg_rows, axis=0, keepdims=True))

    row = lambda i: (i, 0)
    const = lambda i: (0, 0)
    return _call(
        "in_proj_bwd",
        body,
        (T // tm,),
        [
            pl.BlockSpec((tm, CI), row),
            pl.BlockSpec((CI, D), const),
            pl.BlockSpec((tm, D), row),
            pl.BlockSpec((tm, D), row),
            pl.BlockSpec((1, D), const),
        ],
        [pl.BlockSpec((tm, D), row), pl.BlockSpec((1, D), const)],
        [jax.ShapeDtypeStruct((T, D), F32), jax.ShapeDtypeStruct((1, D), F32)],
        (dz_b, w_inT_b, x, dh1, g_mix),
        after=after,
    )


def _weight_grad(name, a_b, b_b, comm=()):
    T, N1 = a_b.shape
    N2 = b_b.shape[1]
    t1 = _tile(N1, 1408, LANES)
    tk = _tile(T, 2048)
    nk = T // tk

    def body(a_ref, b_ref, o_ref, acc):
        k = pl.program_id(1)
        _accumulate(acc, k == 0, _dot(a_ref[...], b_ref[...], TN))

        @pl.when(k == nk - 1)
        def _():
            o_ref[...] = acc[...].astype(BF16)

    (out,), rest = _call(
        name,
        body,
        (N1 // t1, nk),
        [pl.BlockSpec((tk, t1), lambda n, k: (k, n)), pl.BlockSpec((tk, N2), lambda n, k: (k, 0))],
        [pl.BlockSpec((t1, N2), lambda n, k: (n, 0))],
        [jax.ShapeDtypeStruct((N1, N2), BF16)],
        (a_b, b_b),
        scratch=[pltpu.VMEM((t1, N2), F32)],
        comm=comm,
    )
    return out, rest


def _sum_parts(name, full, how, parts, me):
    _, R, C = parts[0].shape
    tr = _tile(R, 512)
    nb = R // tr
    where = [(q, r) for q, p in enumerate(parts) for r in range(p.shape[0])]
    assert len(where) == 3

    def body(me_ref, own_ref, *refs):
        o_ref = refs[-1]
        f = lambda j: refs[where[j][0]][where[j][1]].astype(F32)
        o_ref[...] = (own_ref[...].astype(F32) + f(0)) + (f(1) + f(2))

    own_map = {"rows": lambda i, me_ref: (me_ref[0] * nb + i, 0), "all": lambda i, me_ref: (i, 0)}[how]
    return pl.pallas_call(
        body,
        name=name,
        grid_spec=pltpu.PrefetchScalarGridSpec(
            num_scalar_prefetch=1,
            grid=(nb,),
            in_specs=[pl.BlockSpec((tr, C), own_map)]
            + [pl.BlockSpec((p.shape[0], tr, C), lambda i, me_ref: (0, i, 0)) for p in parts],
            out_specs=pl.BlockSpec((tr, C), lambda i, me_ref: (i, 0)),
        ),
        out_shape=jax.ShapeDtypeStruct((R, C), F32),
        compiler_params=pltpu.CompilerParams(dimension_semantics=("arbitrary",), vmem_limit_bytes=VMEM_LIMIT),
    )(me, full, *parts)


_M_CORR = 1.0 - ADAM_B1**ADAM_STEP
_V_CORR = 1.0 - ADAM_B2**ADAM_STEP


def _adamw_math(w, g, m, v):
    m = ADAM_B1 * m + (1.0 - ADAM_B1) * g
    v = ADAM_B2 * v + (1.0 - ADAM_B2) * (g * g)
    delta = -ADAM_LR * ((m / _M_CORR) / (jnp.sqrt(v / _V_CORR) + ADAM_EPS) + ADAM_WD * w)
    return delta, m, v


def _adamw(name, w, m, v, g_here, g_there, g_transposed=False):
    R, C = w.shape
    tr = _tile(R, 256, LANES if g_transposed else 8)

    def body(w_ref, m_ref, v_ref, ga_ref, gb_ref, g_ref, d_ref, nm_ref, nv_ref):
        g = ga_ref[...] + gb_ref[...]
        if g_transposed:
            g = g.T
        g_ref[...] = g
        d_ref[...], nm_ref[...], nv_ref[...] = _adamw_math(w_ref[...], g, m_ref[...], v_ref[...])

    spec = pl.BlockSpec((tr, C), lambda i: (i, 0))
    gspec = pl.BlockSpec((C, tr), lambda i: (0, i)) if g_transposed else spec
    return _call(name, body, (R // tr,), [spec] * 3 + [gspec] * 2, [spec] * 4, [jax.ShapeDtypeStruct((R, C), F32)] * 4,
                 (w, m, v, g_here, g_there))


def _adamw_on_sparsecore(name, w, m, v, g_here, g_there, after):
    R, C = w.shape
    n_groups = R // SUBLANES
    n_turns = -(-n_groups // SC_TILES)
    n_in, n_out = 5, 4

    def body(w_hbm, m_hbm, v_hbm, ga_hbm, gb_hbm, after_hbm, g_out, d_out, nm_out, nv_out, bufs, sems):
        tile = lax.axis_index("subcore") * SC_CORES + lax.axis_index("sparsecore")
        srcs = (w_hbm, m_hbm, v_hbm, ga_hbm, gb_hbm)
        dsts = (d_out, nm_out, nv_out, g_out)

        def rows(turn):
            return pl.ds((tile + turn * SC_TILES) * SUBLANES, SUBLANES)

        def loads(turn):
            slot = turn % 2
            return [pltpu.make_async_copy(srcs[q].at[rows(turn), :], bufs.at[slot, q], sems.at[slot, q]) for q in range(n_in)]

        def stores(turn):
            slot = turn % 2
            return [pltpu.make_async_copy(bufs.at[slot, q], dsts[q].at[rows(turn), :], sems.at[slot, n_in + q])
                    for q in range(n_out)]

        def when_mine(turn, fn):
            pl.when(tile + turn * SC_TILES < n_groups)(fn)

        def compute(slot):
            wb, mb, vb, gab, gbb = (bufs.at[slot, q] for q in range(n_in))

            @pl.loop(0, SUBLANES)
            def _(r):
                @pl.loop(0, C, step=SC_LANES)
                def _(i):
                    at = (r, pl.ds(i, SC_LANES))
                    g = gab[at] + gbb[at]
                    delta, new_m, new_v = _adamw_math(wb[at], g, mb[at], vb[at])
                    gab[at], wb[at], mb[at], vb[at] = g, delta, new_m, new_v

        def start_loads(turn):
            def fn():
                for cp in loads(turn):
                    cp.start()

            when_mine(turn, fn)

        start_loads(0)
        for turn in range(n_turns):
            def step(turn=turn):
                for cp in loads(turn):
                    cp.wait()
                if turn >= 1:
                    for cp in stores(turn - 1):
                        cp.wait()
                if turn + 1 < n_turns:
                    start_loads(turn + 1)
                compute(turn % 2)
                for cp in stores(turn):
                    cp.start()

            when_mine(turn, step)
        for turn in range(n_turns):
            def drain(turn=turn):
                for cp in stores(turn):
                    cp.wait()

            last_mine = jnp.logical_and(tile + turn * SC_TILES < n_groups, tile + (turn + 1) * SC_TILES >= n_groups)
            pl.when(last_mine)(drain)

    return pl.kernel(
        body,
        name=name,
        out_type=[jax.ShapeDtypeStruct((R, C), F32)] * 4,
        mesh=plsc.VectorSubcoreMesh(core_axis_name="sparsecore", subcore_axis_name="subcore"),
        scratch_types=[pltpu.VMEM((2, n_in, SUBLANES, C), F32), pltpu.SemaphoreType.DMA((2, n_in + n_out))],
        compiler_params=pltpu.CompilerParams(use_tc_tiling_on_sc=True),
    )(w, m, v, g_here, g_there, after)


class _PackLayout:
    def __init__(self, n_cc, n_grp, G, widths):
        self.dw_rows = (0, HALO)
        self.wp_rows = (HALO, HALO + G)
        self.n_cc, self.n_grp, self.G = n_cc, n_grp, G
        self.vec = {}
        r = HALO + G
        for name, width in widths:
            self.vec[name] = (r, width)
            r += width // PACK_W
        self.rows = -(-r // 8) * 8


def _pack_small(layout, dwdw, dwp, vecs):
    names = list(vecs)

    def body(*refs):
        dw_ref, wp_ref = refs[0], refs[1]
        vec_refs = refs[2 : 2 + len(names)]
        o_ref = refs[-1]
        o_ref[...] = jnp.zeros_like(o_ref)
        for j in range(layout.n_cc):
            o_ref[layout.dw_rows[0] : layout.dw_rows[1], j * LANES : (j + 1) * LANES] = dw_ref[j]
        for i in range(layout.n_grp):
            o_ref[layout.wp_rows[0] : layout.wp_rows[1], i * layout.G : (i + 1) * layout.G] = wp_ref[i]
        for name, ref in zip(names, vec_refs):
            r, width = layout.vec[name]
            for h in range(width // PACK_W):
                o_ref[r + h : r + h + 1, :] = ref[:, h * PACK_W : (h + 1) * PACK_W]

    return pl.pallas_call(
        body,
        name="pack_small",
        out_shape=jax.ShapeDtypeStruct((layout.rows, PACK_W), F32),
    )(dwdw, dwp, *[vecs[k] for k in names])


def _adamw_small(layout, g_here, g_there, w_dw, m_dw, v_dw, w_pool, m_pool, v_pool, vec_w, vec_m, vec_v):
    names = list(vec_w)
    nv = len(names)

    def body(*refs):
        ga_ref, gb_ref = refs[0], refs[1]
        wdw, mdw, vdw, wp, mp, vp = refs[2:8]
        vw, vm, vv = refs[8 : 8 + nv], refs[8 + nv : 8 + 2 * nv], refs[8 + 2 * nv : 8 + 3 * nv]
        outs = refs[8 + 3 * nv :]
        acc = outs[-1]
        acc[...] = ga_ref[...] + gb_ref[...]

        def emit(o, g, w, m, v, idx=()):
            res = (g,) + _adamw_math(w, g, m, v)
            for ref, val in zip(o, res):
                ref[idx] = val

        me = 2 * lax.axis_index("x") + lax.axis_index("y")
        for j in range(layout.n_cc):

            @pl.when(me == j)
            def _(j=j):
                for k in range(wdw.shape[0]):
                    g = acc[layout.dw_rows[0] + k : layout.dw_rows[0] + k + 1, j * LANES : (j + 1) * LANES]
                    emit(outs[0:4], g, wdw[k], mdw[k], vdw[k], idx=k)

        for i in range(layout.n_grp):
            g = acc[layout.wp_rows[0] : layout.wp_rows[1], i * layout.G : (i + 1) * layout.G]
            emit(outs[4:8], g, wp[i], mp[i], vp[i], idx=i)
        for q, name in enumerate(names):
            r, width = layout.vec[name]
            for h in range(width // PACK_W):
                ls = slice(h * PACK_W, (h + 1) * PACK_W)
                g = acc[r + h : r + h + 1, :]
                emit(outs[8 + 4 * q : 12 + 4 * q], g, vw[q][:, ls], vm[q][:, ls], vv[q][:, ls], idx=(slice(None), ls))

    shapes = [w_dw.shape] * 4 + [w_pool.shape] * 4
    for name in names:
        shapes += [vec_w[name].shape] * 4
    return pl.pallas_call(
        body,
        name="adamw_small",
        out_shape=[jax.ShapeDtypeStruct(s, F32) for s in shapes],
        scratch_shapes=[pltpu.VMEM(g_here.shape, F32)],
    )(g_here, g_there, w_dw, m_dw, v_dw, w_pool, m_pool, v_pool,
      *[vec_w[k] for k in names], *[vec_m[k] for k in names], *[vec_v[k] for k in names])


def _allreduce_adamw_row(g_part, w, m, v, loss_part, comm=()):
    D = w.shape[1]
    n_pairs = N_DEV - 1

    def body(g_ref, w_ref, m_ref, v_ref, l_ref, go_ref, d_ref, nm_ref, nv_ref, lo_ref, land_g, land_l, sems):
        x, y, c = _place()
        copies = []
        for q, (src, land) in enumerate(((g_ref, land_g), (l_ref, land_l))):
            for r in range(1, N_DEV):
                fx, fy, fc = (r >> 2) & 1, (r >> 1) & 1, r & 1
                peer = (1 - x if fx else x, 1 - y if fy else y, 1 - c if fc else c)
                cp = _remote(src, land.at[r], sems, 2 * (q * n_pairs + r - 1), peer)
                cp.start()
                copies.append(cp)
        for cp in copies:
            cp.wait()

        def total(src, land):
            row = lambda r: src[...] if r == 0 else land[r]
            return ((row(0) + row(4)) + (row(2) + row(6))) + ((row(1) + row(5)) + (row(3) + row(7)))

        g = total(g_ref, land_g)
        go_ref[...] = g
        d_ref[...], nm_ref[...], nv_ref[...] = _adamw_math(w_ref[...], g, m_ref[...], v_ref[...])
        lo_ref[...] = total(l_ref, land_l)

    vm = pl.BlockSpec(memory_space=pltpu.VMEM)
    return _call(
        "allreduce_adamw_g_mix",
        body,
        (),
        [vm] * 5,
        [vm] * 5,
        [jax.ShapeDtypeStruct((1, D), F32)] * 4 + [jax.ShapeDtypeStruct(loss_part.shape, F32)],
        (g_part, w, m, v, loss_part),
        scratch=[pltpu.VMEM((N_DEV, 1, D), F32), pltpu.VMEM((N_DEV,) + loss_part.shape, F32),
                 pltpu.SemaphoreType.DMA((4 * n_pairs,))],
        comm=comm,
    )


def kernel(x, g_mix, w_in, b_in, w_dw, b_dw, ln_g, ln_b, w_pool, s_pool, w_out, g_ffn, w_gate, w_up, w_down, g_final, loss_target, m_g_mix, m_w_in, m_b_in, m_w_dw, m_b_dw, m_ln_g, m_ln_b, m_w_pool, m_s_pool, m_w_out, m_g_ffn, m_w_gate, m_w_up, m_w_down, m_g_final, v_g_mix, v_w_in, v_b_in, v_w_dw, v_b_dw, v_ln_g, v_ln_b, v_w_pool, v_s_pool, v_w_out, v_g_ffn, v_w_gate, v_w_up, v_w_down, v_g_final):
    x2 = x[0]
    target = loss_target[0]
    T, D = x2.shape
    w_in2, w_out2, w_down2 = w_in[0], w_out[0], w_down[0]
    taps_first = lambda a: jnp.transpose(a, (1, 0, 2))
    w_dw3 = taps_first(w_dw)
    w_gateT, w_upT = w_gate[0].T, w_up[0].T
    CI = w_in2.shape[1] * N_CHIPS
    DM = w_out2.shape[0] * N_CHIPS
    F = w_down2.shape[0] * N_CHIPS
    KW, _, dw_cols = w_dw3.shape
    assert dw_cols == LANES
    n_grp, G = w_pool.shape[1], w_pool.shape[-1]
    w_pool3 = w_pool[0]
    g_final2 = g_final.reshape(1, D)

    me = (2 * lax.axis_index("x") + lax.axis_index("y")).astype(jnp.int32).reshape(1)

    w_inT_b, w_dw4, f_out, f_gate, f_up, f_down = _place_and_gather(
        [(w_in2, "rows", (CI, D), BF16, True, True), (w_dw3, "lead", (N_CHIPS, KW, 1, dw_cols), F32, False, False)],
        [(w, "rows", shape, BF16, False, True)
         for w, shape in ((w_out2, (DM, D)), (w_gateT, (F, D)), (w_upT, (F, D)), (w_down2, (F, D)))])
    w_pool_b = w_pool3.astype(BF16)
    ici = lambda f: _GatherIci([f], ["rows"], [True])
    d2d = lambda f: _GatherD2d([f], ["rows"])
    gather = _start("gather_start", [ici(f_out), ici(f_gate), ici(f_up), ici(f_down)])
    (z, xn_b), _ = _in_proj(x2, g_mix, w_inT_b, b_in, after=[gather.token])
    (f_out,) = _wait("gather_out_wait", gather, 0, xn_b)
    s_out = _start("share_out_start", [d2d(f_out)], sibling_only=True)
    (y_b, v), _ = _seq_fwd(z, w_dw4, b_dw, ln_g, ln_b, w_pool_b, s_pool, after=[s_out.token])
    (w_out_b,) = _wait("share_out_wait", s_out, 0, y_b)
    (f_gate,) = _wait("gather_gate_wait", gather, 1, y_b)
    s_gate = _start("share_gate_start", [d2d(f_gate)], sibling_only=True)
    (h1, hn_b), _ = _out_proj(y_b, x2, w_out_b, g_ffn, after=[s_gate.token])
    (f_up,) = _wait("gather_up_wait", gather, 2, hn_b)
    s_up = _start("share_up_start", [d2d(f_up)], sibling_only=True)
    (wgT_b,) = _wait("share_gate_wait", s_gate, 0, hn_b)
    (wuT_b,) = _wait("share_up_wait", s_up, 0, hn_b)
    (silu_b, uds_b, a_b), _ = _gate_up(hn_b, wgT_b, wuT_b)
    (f_down,) = _wait("gather_down_wait", gather, 3, a_b)
    s_down = _start("share_down_start", [d2d(f_down)], sibling_only=True)
    (wd_b,) = _wait("share_down_wait", s_down, 0, a_b)
    (dh2, dh2_b, loss_part, d_g_final), _ = _down_loss(a_b, wd_b, h1, target, g_final2)

    gw_down, _ = _weight_grad("grad_w_down", a_b, dh2_b)
    (dg_b, du_b), (p_down_xy,) = _ffn_bwd_act(
        dh2_b, wd_b, silu_b, uds_b, comm=[_Scatter([gw_down], ["rows"], which=(0, 1))])
    diag_half = lambda i: [_Scatter([gw_down], ["rows"], which=(2,), block=(i, 2))]
    gw_gateT, (p_down_d0,) = _weight_grad("grad_w_gate", dg_b, hn_b, comm=diag_half(0))
    gw_upT, (p_down_d1,) = _weight_grad("grad_w_up", du_b, hn_b, comm=diag_half(1))
    p_down_d = jnp.concatenate([p_down_d0, p_down_d1], axis=1)
    sum_down = _sum_parts("sum_w_down", gw_down, "rows", [p_down_xy, p_down_d], me)
    (dh1, dh1_b, dy, d_g_ffn), (p_gate, oth_down) = _ffn_bwd_in(
        dg_b, du_b, wgT_b, wuT_b, h1, dh2, g_ffn, w_out_b, comm=[_Scatter([gw_gateT], ["rows"]), _Swap([sum_down])])
    gw_out, _ = _weight_grad("grad_w_out", y_b, dh1_b)
    sum_gate = _sum_parts("sum_w_gate", gw_gateT, "rows", [p_gate], me)
    res = {}
    res["w_down"] = _adamw_on_sparsecore("adamw_w_down", w_down2, m_w_down[0], v_w_down[0], sum_down, oth_down, sum_down)
    (dz_b, d_wdw, d_bdw, d_lng, d_lnb, d_wp, d_sp, d_bin), (p_up, p_out, oth_gate) = _seq_bwd(
        z, dy, v, w_dw4, ln_g, ln_b, w_pool_b, s_pool,
        comm=[_Scatter([gw_upT, gw_out], ["rows", "rows"]), _Swap([sum_gate])])
    res["w_gate"] = _adamw_on_sparsecore(
        "adamw_w_gate", w_gateT, m_w_gate[0].T, v_w_gate[0].T, sum_gate, oth_gate, res["w_down"][0])
    vec_grads ={"b_dw": d_bdw, "ln_g": d_lng, "ln_b": d_lnb, "s_pool": d_sp, "g_ffn": d_g_ffn, "g_final": d_g_final, "b_in": d_bin}
    layout = _PackLayout(dw_cols * N_CHIPS // LANES, n_grp, G, [(k, a.shape[1]) for k, a in vec_grads.items()])
    pack = _pack_small(layout, d_wdw, d_wp, vec_grads)
    sum_up = _sum_parts("sum_w_up", gw_upT, "rows", [p_up], me)
    sum_out = _sum_parts("sum_w_out", gw_out, "rows", [p_out], me)
    gw_inT, (p_small, oth_up, oth_out) = _weight_grad(
        "grad_w_in", dz_b, xn_b, comm=[_Scatter([pack], ["all"]), _Swap([sum_up, sum_out])])
    sum_small = _sum_parts("sum_small", pack, "all", [p_small], me)
    late = _start("late_start", [_Scatter([gw_inT], ["rows"]), _Swap([sum_small])])
    (grad_x, d_g_mix), _ = _in_proj_bwd(dz_b, w_inT_b, x2, dh1, g_mix, after=[late.token])
    gw_inT, p_in = _wait("late_w_in_wait", late, 0, d_g_mix)
    sum_small, oth_small = _wait("late_small_wait", late, 1, d_g_mix)
    res["w_up"] = _adamw_on_sparsecore("adamw_w_up", w_upT, m_w_up[0].T, v_w_up[0].T, sum_up, oth_up, res["w_gate"][0])
    res["w_out"] = _adamw_on_sparsecore("adamw_w_out", w_out2, m_w_out[0], v_w_out[0], sum_out, oth_out, res["w_gate"][0])
    sum_in = _sum_parts("sum_w_in", gw_inT, "rows", [p_in], me)
    (*res["g_mix"], loss_row), (oth_in,) = _allreduce_adamw_row(
        d_g_mix, g_mix, m_g_mix, v_g_mix, loss_part, comm=[_Swap([sum_in])])
    loss = loss_row[0, 0]
    res["w_in"], _ = _adamw("adamw_w_in", w_in2, m_w_in[0], v_w_in[0], sum_in, oth_in, g_transposed=True)

    vec_w = {"b_dw": b_dw, "ln_g": ln_g, "ln_b": ln_b, "s_pool": s_pool, "g_ffn": g_ffn, "g_final": g_final2, "b_in": b_in}
    vec_m = {"b_dw": m_b_dw, "ln_g": m_ln_g, "ln_b": m_ln_b, "s_pool": m_s_pool, "g_ffn": m_g_ffn,
             "g_final": m_g_final.reshape(1, D), "b_in": m_b_in}
    vec_v = {"b_dw": v_b_dw, "ln_g": v_ln_g, "ln_b": v_ln_b, "s_pool": v_s_pool, "g_ffn": v_g_ffn,
             "g_final": v_g_final.reshape(1, D), "b_in": v_b_in}
    small = _adamw_small(layout, sum_small, oth_small, w_dw3, taps_first(m_w_dw), taps_first(v_w_dw),
                         w_pool3, m_w_pool[0], v_w_pool[0], vec_w, vec_m, vec_v)
    res["w_dw"] = [taps_first(a) for a in small[0:4]]
    res["w_pool"] = [a[None] for a in small[4:8]]
    for q, k in enumerate(vec_w):
        res[k] = list(small[8 + 4 * q : 12 + 4 * q])
    res["g_final"] = [a.reshape(D) for a in res["g_final"]]
    for k in ("w_in", "w_out", "w_down"):
        res[k] = [a[None] for a in res[k]]
    for k in ("w_gate", "w_up"):
        res[k] = [a.T[None] for a in res[k]]

    order = ["g_mix", "w_in", "b_in", "w_dw", "b_dw", "ln_g", "ln_b", "w_pool", "s_pool", "w_out", "g_ffn", "w_gate", "w_up", "w_down", "g_final"]
    outs = [loss, grad_x[None]]
    for q in range(4):
        outs += [res[k][q] for k in order]
    return tuple(outs)
```

```python
import jax
import jax.numpy as jnp
from jax import lax
from jax.experimental import pallas as pl
from jax.experimental.pallas import tpu as pltpu
from jax.experimental.pallas import tpu_sc as plsc

F32 = jnp.float32
BF16 = jnp.bfloat16
MESH = pl.DeviceIdType.MESH
ANY = pl.BlockSpec(memory_space=pl.ANY)

RMS_EPS = 1e-6
LN_EPS = 1e-5
POOL_WINDOWS = (2, 4, 8, 16)
ADAM_LR = 0.001
ADAM_B1 = 0.9
ADAM_B2 = 0.999
ADAM_EPS = 1e-08
ADAM_WD = 0.01
ADAM_STEP = 10

LANES = 128
SUBLANES = 8
BF16_ROWS = 16
HALO = 32
CONV_ROWS = 64
HIDDEN_CHUNK = 512
VMEM_LIMIT = 56 * 1024 * 1024
PACK_W = 512
N_CHIPS = 4
N_DEV = 8
SIBLING_BARRIER_ID = 0
SC_CORES = 2
SC_TILES = 32
SC_LANES = 16


def _tile(n, want, mult=8):
    t = min(n, want)
    while n % t or t % mult:
        t -= 1
    return t


def _sigmoid(x):
    return 1.0 / (1.0 + jnp.exp(-x))


def _dot(a, b, dims):
    return lax.dot_general(a, b, (dims, ((), ())), preferred_element_type=F32)


NN = ((1,), (0,))
NT = ((1,), (1,))
TN = ((0,), (0,))


def _rms_bwd(x, g, dy):
    r = lax.rsqrt(jnp.mean(x * x, axis=-1, keepdims=True) + RMS_EPS)
    xh = x * r
    gy = dy * g
    dx = r * (gy - xh * jnp.mean(gy * xh, axis=-1, keepdims=True))
    return dx, dy * xh


def _accumulate(ref, first, val):
    @pl.when(first)
    def _():
        ref[...] = val

    @pl.when(jnp.logical_not(first))
    def _():
        ref[...] += val


def _place():
    return lax.axis_index("x"), lax.axis_index("y"), lax.axis_index("c")


def _other_chips(x, y):
    return [(1 - x, y), (x, 1 - y), (1 - x, 1 - y)]


def _rows(ref, start, n):
    return ref.at[pl.ds(pl.multiple_of(start, BF16_ROWS), n)]


def _window(ref, how, k, c=None):
    if how == "all":
        return ref
    if how == "lead":
        return ref.at[k]
    assert how == "rows"
    n = ref.shape[0] // N_CHIPS
    if c is None:
        return _rows(ref, k * n, n)
    return _rows(ref, k * n + c * (n // 2), n // 2)


def _remote(src, dst, sems, s, device):
    return pltpu.make_async_remote_copy(
        src_ref=src, dst_ref=dst, send_sem=sems.at[s], recv_sem=sems.at[s + 1], device_id=device, device_id_type=MESH)


class _GatherIci:
    aliased = True

    def __init__(self, fulls, hows, splits, which=(0, 1, 2)):
        self.fulls, self.hows, self.splits, self.which = list(fulls), list(hows), list(splits), tuple(which)

    def inputs(self):
        return self.fulls

    def out_shapes(self):
        return [jax.ShapeDtypeStruct(a.shape, a.dtype) for a in self.fulls]

    def n_sems(self):
        return 6 * len(self.fulls)

    def build(self, ins, outs, sems, base):
        x, y, c = _place()
        me = 2 * x + y
        chips = _other_chips(x, y)
        starts, waits = [], []
        for a, (how, sp) in enumerate(zip(self.hows, self.splits)):
            half = c if sp else None
            mine = _window(outs[a], how, me, half)
            for j in self.which:
                px, py = chips[j]
                s = base + 6 * a + 2 * j
                cp = _remote(mine, mine, sems, s, (px, py, c))
                landing = _remote(mine, _window(outs[a], how, 2 * px + py, half), sems, s, (px, py, c))
                starts.append(cp.start)
                waits += [landing.wait_recv, cp.wait_send]
        return starts, waits


class _GatherD2d:
    aliased = True

    def __init__(self, fulls, hows):
        self.fulls, self.hows = list(fulls), list(hows)

    def inputs(self):
        return self.fulls

    def out_shapes(self):
        return [jax.ShapeDtypeStruct(a.shape, a.dtype) for a in self.fulls]

    def n_sems(self):
        return 6 * len(self.fulls)

    def build(self, ins, outs, sems, base):
        x, y, c = _place()
        starts, waits = [], []
        for a, how in enumerate(self.hows):
            for j, (px, py) in enumerate(_other_chips(x, y)):
                s = base + 6 * a + 2 * j
                got = _window(outs[a], how, 2 * px + py, c)
                cp = _remote(got, got, sems, s, (x, y, 1 - c))
                landing = _remote(got, _window(outs[a], how, 2 * px + py, 1 - c), sems, s, (x, y, 1 - c))
                starts.append(cp.start)
                waits += [landing.wait_recv, cp.wait_send]
        return starts, waits


def _part_shape(a, how):
    if how == "all":
        return a.shape
    assert how == "rows"
    return (a.shape[0] // N_CHIPS, a.shape[1])


class _Scatter:
    aliased = False

    def __init__(self, fulls, hows, which=(0, 1, 2)):
        self.fulls, self.hows, self.which = list(fulls), list(hows), tuple(which)

    def inputs(self):
        return self.fulls

    def out_shapes(self):
        return [jax.ShapeDtypeStruct((len(self.which),) + _part_shape(a, h), a.dtype) for a, h in zip(self.fulls, self.hows)]

    def n_sems(self):
        return 6 * len(self.fulls)

    def build(self, ins, outs, sems, base):
        x, y, c = _place()
        chips = _other_chips(x, y)
        starts, waits = [], []
        for a, how in enumerate(self.hows):
            for slot, j in enumerate(self.which):
                px, py = chips[j]
                cp = _remote(_window(ins[a], how, 2 * px + py), outs[a].at[slot], sems, base + 6 * a + 2 * j, (px, py, c))
                starts.append(cp.start)
                waits += [cp.wait_recv, cp.wait_send]
        return starts, waits


class _Swap:
    aliased = False

    def __init__(self, arrays):
        self.arrays = list(arrays)

    def inputs(self):
        return self.arrays

    def out_shapes(self):
        return [jax.ShapeDtypeStruct(a.shape, a.dtype) for a in self.arrays]

    def n_sems(self):
        return 2 * len(self.arrays)

    def build(self, ins, outs, sems, base):
        x, y, c = _place()
        starts, waits = [], []
        for a in range(len(ins)):
            cp = _remote(ins[a], outs[a], sems, base + 2 * a, (x, y, 1 - c))
            starts.append(cp.start)
            waits += [cp.wait_recv, cp.wait_send]
        return starts, waits


def _call(name, body, grid, in_specs, out_specs, out_shape, args, scratch=(), comm=(), after=()):
    comm, after = list(comm), list(after)
    n_in, n_out, n_scr, n_after = len(args), len(out_shape), len(scratch), len(after)
    c_in = [a for op in comm for a in op.inputs()]
    c_out = [s for op in comm for s in op.out_shapes()]
    n_sems = sum(op.n_sems() for op in comm)
    aliases, i_in, i_out = {}, 0, 0
    for op in comm:
        if op.aliased:
            for q in range(len(op.inputs())):
                aliases[n_in + n_after + i_in + q] = n_out + i_out + q
        i_in, i_out = i_in + len(op.inputs()), i_out + len(op.out_shapes())

    def wrapped(*refs):
        ins = refs[:n_in]
        cin = refs[n_in + n_after : n_in + n_after + len(c_in)]
        o0 = n_in + n_after + len(c_in)
        outs = refs[o0 : o0 + n_out]
        cout = refs[o0 + n_out : o0 + n_out + len(c_out)]
        s0 = o0 + n_out + len(c_out)
        scr = refs[s0 : s0 + n_scr]

        def copies():
            sems = refs[s0 + n_scr]
            starts, waits = [], []
            i_in = i_out = base = 0
            for op in comm:
                ni, no = len(op.inputs()), len(op.out_shapes())
                s, w = op.build(cin[i_in : i_in + ni], cout[i_out : i_out + no], sems, base)
                starts += s
                waits += w
                i_in, i_out, base = i_in + ni, i_out + no, base + op.n_sems()
            return starts, waits

        def run_starts():
            for start in copies()[0]:
                start()

        def run_waits():
            for wait in copies()[1]:
                wait()

        if comm and grid:
            first = last = True
            for d, n in enumerate(grid):
                first = jnp.logical_and(first, pl.program_id(d) == 0)
                last = jnp.logical_and(last, pl.program_id(d) == n - 1)
            pl.when(first)(run_starts)
        elif comm:
            run_starts()
        if body is not None:
            body(*ins, *outs, *scr)
        if comm and grid:
            pl.when(last)(run_waits)
        elif comm:
            run_waits()

    res = pl.pallas_call(
        wrapped,
        name=name,
        grid=grid,
        in_specs=list(in_specs) + [ANY] * (n_after + len(c_in)),
        out_specs=list(out_specs) + [ANY] * len(c_out),
        out_shape=list(out_shape) + c_out,
        scratch_shapes=list(scratch) + ([pltpu.SemaphoreType.DMA((n_sems,))] if comm else []),
        input_output_aliases=aliases,
        compiler_params=pltpu.CompilerParams(dimension_semantics=("arbitrary",) * len(grid), vmem_limit_bytes=VMEM_LIMIT),
    )(*args, *after, *c_in)
    return tuple(res[:n_out]), tuple(res[n_out:])


def _place_and_gather(now, later):
    items = list(now) + list(later)
    n, n_now = len(items), len(now)
    buf_shape = lambda it: it[0].shape[::-1] if it[4] else it[0].shape
    split_now = [a for a in range(n_now) if items[a][5]]

    def body(*refs):
        ins, outs = refs[:n], refs[n : 2 * n]
        stage, bufs = refs[2 * n : 3 * n - n_now], refs[3 * n - n_now : 4 * n - n_now]
        sems = refs[4 * n - n_now]
        x, y, c = _place()
        me = 2 * x + y
        chips = _other_chips(x, y)
        loads = [pltpu.make_async_copy(ins[a], stage[a - n_now], sems.at[a]) for a in range(n_now, n)]
        for ld in loads:
            ld.start()
        pending = []

        def place(a, val):
            _, how, _, dtype, transposed, _ = items[a]
            bufs[a][...] = (val.T if transposed else val).astype(dtype)
            cp = pltpu.make_async_copy(bufs[a], _window(outs[a], how, me), sems.at[n + a])
            cp.start()
            pending.append(cp.wait)

        arrivals = []
        for a in range(n_now):
            place(a, ins[a][...])
            how, split = items[a][1], items[a][5]
            half = c if split else None
            src = _rows(bufs[a], c * (bufs[a].shape[0] // 2), bufs[a].shape[0] // 2) if split else bufs[a]
            for j, (px, py) in enumerate(chips):
                s = 2 * n + 6 * a + 2 * j
                cp = _remote(src, _window(outs[a], how, me, half), sems, s, (px, py, c))
                landing = _remote(src, _window(outs[a], how, 2 * px + py, half), sems, s, (px, py, c))
                cp.start()
                arrivals.append(landing.wait_recv)
                pending.append(cp.wait_send)
        for a in range(n_now, n):
            loads[a - n_now].wait()
            place(a, stage[a - n_now][...])
        for wait in arrivals:
            wait()
        d2d = _GatherD2d([None] * len(split_now), [items[a][1] for a in split_now])
        starts, waits = d2d.build(None, [outs[a] for a in split_now], sems, 2 * n + 6 * n_now)
        for start in starts:
            start()
        for wait in waits + pending:
            wait()

    vm = pl.BlockSpec(memory_space=pltpu.VMEM)
    return pl.pallas_call(
        body,
        name="place_and_gather",
        in_specs=[vm] * n_now + [ANY] * (n - n_now),
        out_specs=[ANY] * n,
        out_shape=[jax.ShapeDtypeStruct(it[2], it[3]) for it in items],
        scratch_shapes=[pltpu.VMEM(it[0].shape, it[0].dtype) for it in later]
        + [pltpu.VMEM(buf_shape(it), it[3]) for it in items]
        + [pltpu.SemaphoreType.DMA((2 * n + 6 * n_now + 6 * len(split_now),))],
        compiler_params=pltpu.CompilerParams(vmem_limit_bytes=VMEM_LIMIT),
    )(*[it[0] for it in items])


_HBM = pl.BlockSpec(memory_space=pltpu.HBM)
_SEM = pl.BlockSpec(memory_space=pltpu.SEMAPHORE)
_DATAFLOW = pltpu.SideEffectType.DATAFLOW_SIDE_EFFECTING


class _Pending:
    def __init__(self, ops, bases, sems, arrays, token):
        self.ops, self.bases, self.sems, self.arrays, self.token = ops, bases, sems, arrays, token


def _op_refs(op, refs):
    n_src = len(op.inputs())
    return refs[:n_src], (refs[:n_src] if op.aliased else refs[n_src:])


def _start(name, ops, sibling_only=False):
    per_op = [list(op.inputs()) + ([] if op.aliased else [lax.empty(sd.shape, sd.dtype) for sd in op.out_shapes()])
              for op in ops]
    arrays = [a for group in per_op for a in group]
    bases = [sum(op.n_sems() for op in ops[:k]) for k in range(len(ops))]
    n = len(arrays)

    def body(*refs):
        sems, token = refs[n], refs[-1]
        if sibling_only:
            x, y, c = _place()
            barrier = pltpu.get_barrier_semaphore()
            pl.semaphore_signal(barrier, inc=1, device_id=(x, y, 1 - c), device_id_type=MESH)
            pl.semaphore_wait(barrier, 1)
        at = 0
        for op, group, base in zip(ops, per_op, bases):
            starts, _ = op.build(*_op_refs(op, refs[at : at + len(group)]), sems, base)
            for start in starts:
                start()
            at += len(group)
        token[...] = jnp.zeros_like(token)

    res = pl.pallas_call(
        body,
        name=name,
        out_shape=(pltpu.SemaphoreType.DMA((sum(op.n_sems() for op in ops),)),)
        + tuple(pltpu.HBM(a.shape, a.dtype) for a in arrays) + (jax.ShapeDtypeStruct((SUBLANES, LANES), F32),),
        in_specs=(_HBM,) * n,
        out_specs=(_SEM,) + (_HBM,) * n + (pl.BlockSpec(memory_space=pltpu.VMEM),),
        input_output_aliases={i: 1 + i for i in range(n)},
        compiler_params=pltpu.CompilerParams(
            has_side_effects=_DATAFLOW, collective_id=SIBLING_BARRIER_ID if sibling_only else None),
    )(*[pltpu.with_memory_space_constraint(a, pltpu.HBM) for a in arrays])
    thru, at, groups = list(res[1 : 1 + n]), 0, []
    for group in per_op:
        groups.append(thru[at : at + len(group)])
        at += len(group)
    return _Pending(list(ops), bases, res[0], groups, res[-1])


def _wait(name, pending, k, after):
    op, arrays = pending.ops[k], pending.arrays[k]
    n = len(arrays)

    def body(*refs):
        _, waits = op.build(*_op_refs(op, refs[:n]), refs[n], pending.bases[k])
        for wait in waits:
            wait()

    return pl.pallas_call(
        body,
        name=name,
        out_shape=tuple(pltpu.HBM(a.shape, a.dtype) for a in arrays),
        in_specs=(_HBM,) * n + (_SEM, ANY),
        out_specs=(_HBM,) * n,
        input_output_aliases={i: i for i in range(n)},
        compiler_params=pltpu.CompilerParams(has_side_effects=_DATAFLOW),
    )(*arrays, pending.sems, after)


def _in_proj(x, g_mix, w_inT_b, b_in, after=()):
    T, D = x.shape
    CI = w_inT_b.shape[0]
    tm = _tile(T, 512)

    def body(x_ref, g_ref, w_ref, b_ref, z_ref, xn_ref):
        xv = x_ref[...]
        r = lax.rsqrt(jnp.mean(xv * xv, axis=-1, keepdims=True) + RMS_EPS)
        xn = (xv * r * g_ref[...]).astype(BF16)
        xn_ref[...] = xn
        z_ref[...] = _dot(xn, w_ref[...], NT) + b_ref[...]

    return _call(
        "in_proj",
        body,
        (T // tm,),
        [
            pl.BlockSpec((tm, D), lambda i: (i, 0)),
            pl.BlockSpec((1, D), lambda i: (0, 0)),
            pl.BlockSpec((CI, D), lambda i: (0, 0)),
            pl.BlockSpec((1, CI), lambda i: (0, 0)),
        ],
        [pl.BlockSpec((tm, CI), lambda i: (i, 0)), pl.BlockSpec((tm, D), lambda i: (i, 0))],
        [jax.ShapeDtypeStruct((T, CI), F32), jax.ShapeDtypeStruct((T, D), BF16)],
        (x, g_mix, w_inT_b, b_in),
        after=after,
    )


def _fill_shifted(scr):
    n = scr.shape[1] - SUBLANES
    for s in range(1, SUBLANES):
        scr[s, 0:n, :] = scr[0, s : s + n, :]


def _shifted_rows(scr, off, n, cs):
    s = off % SUBLANES
    return scr[s, off - s : off - s + n, cs]


def _pool_mean_minus_token(p_scr, cs, w, cnt, tt):
    tok = p_scr[HALO : HALO + tt, cs]
    s = tok
    for d in range(1, w):
        s = s + p_scr[HALO - d : HALO - d + tt, cs]
    return s / cnt - tok


def _seq_fwd(z, w_dw4, b_dw, ln_g, ln_b, w_pool_b, s_pool, after=()):
    T, CI = z.shape
    CC = ln_g.shape[1]
    n_grp, G = w_pool_b.shape[0], w_pool_b.shape[-1]
    KW = w_dw4.shape[1]
    D = CC + n_grp * G
    tt = _tile(T, 512, HALO)
    per = tt // HALO

    def body(zc_ref, zp_ref, wdw_ref, bdw_ref, lng_ref, lnb_ref, wp_ref, sp_ref, y_ref, v_ref, u_scr, p_scr):
        i = pl.program_id(0)
        first = i == 0
        u_prev = zp_ref[:, 0:CC] * _sigmoid(zp_ref[:, CC : 2 * CC])
        u_scr[0, 0:HALO, :] = jnp.where(first, 0.0, u_prev)
        p_scr[0:HALO, :] = jnp.where(first, 0.0, zp_ref[:, 2 * CC :])
        u_scr[0, HALO:, :] = zc_ref[:, 0:CC] * _sigmoid(zc_ref[:, CC : 2 * CC])
        p_scr[HALO:, :] = zc_ref[:, 2 * CC :]
        _fill_shifted(u_scr)

        for j in range(CC // LANES):
            cs = slice(LANES * j, LANES * (j + 1))
            for rb in range(tt // CONV_ROWS):
                acc = jnp.zeros((CONV_ROWS, LANES), F32)
                for k in range(KW):
                    off = HALO - (KW - 1) + k + rb * CONV_ROWS
                    acc = acc + _shifted_rows(u_scr, off, CONV_ROWS, cs) * wdw_ref[j, k]
                v_ref[rb * CONV_ROWS : (rb + 1) * CONV_ROWS, cs] = acc + bdw_ref[:, cs]

        v = v_ref[...]
        mu = jnp.mean(v, axis=-1, keepdims=True)
        d = v - mu
        var = jnp.mean(d * d, axis=-1, keepdims=True)
        ln = d * lax.rsqrt(var + LN_EPS) * lng_ref[...] + lnb_ref[...]
        y_ref[:, 0:CC] = (ln * _sigmoid(ln)).astype(BF16)

        tpos = i * tt + lax.broadcasted_iota(jnp.int32, (tt, 1), 0)
        for gi, w in enumerate(POOL_WINDOWS):
            cs = slice(G * gi, G * (gi + 1))
            cnt = jnp.minimum(tpos + 1, w).astype(F32)
            yi = _pool_mean_minus_token(p_scr, cs, w, cnt, tt)
            q = _dot(yi.astype(BF16), wp_ref[gi], NN)
            y_ref[:, CC + G * gi : CC + G * (gi + 1)] = (q * sp_ref[:, cs]).astype(BF16)

    const2 = lambda i: (0, 0)
    return _call(
        "seq_fwd",
        body,
        (T // tt,),
        [
            pl.BlockSpec((tt, CI), lambda i: (i, 0)),
            pl.BlockSpec((HALO, CI), lambda i: (jnp.maximum(i * per - 1, 0), 0)),
            pl.BlockSpec(w_dw4.shape, lambda i: (0,) * w_dw4.ndim),
            pl.BlockSpec((1, CC), const2),
            pl.BlockSpec((1, CC), const2),
            pl.BlockSpec((1, CC), const2),
            pl.BlockSpec(w_pool_b.shape, lambda i: (0, 0, 0)),
            pl.BlockSpec((1, n_grp * G), const2),
        ],
        [pl.BlockSpec((tt, D), lambda i: (i, 0)), pl.BlockSpec((tt, CC), lambda i: (i, 0))],
        [jax.ShapeDtypeStruct((T, D), BF16), jax.ShapeDtypeStruct((T, CC), F32)],
        (z, z, w_dw4, b_dw, ln_g, ln_b, w_pool_b, s_pool),
        scratch=[pltpu.VMEM((SUBLANES, HALO + tt, CC), F32), pltpu.VMEM((HALO + tt, n_grp * G), F32)],
        after=after,
    )


def _out_proj(y_b, x, w_out_b, g_ffn, after=()):
    T, D = x.shape
    tm = _tile(T, 512)

    def body(y_ref, x_ref, w_ref, g_ref, h1_ref, hn_ref):
        h1 = x_ref[...] + _dot(y_ref[...], w_ref[...], NN)
        h1_ref[...] = h1
        r = lax.rsqrt(jnp.mean(h1 * h1, axis=-1, keepdims=True) + RMS_EPS)
        hn_ref[...] = (h1 * r * g_ref[...]).astype(BF16)

    row = lambda i: (i, 0)
    return _call(
        "out_proj",
        body,
        (T // tm,),
        [
            pl.BlockSpec((tm, y_b.shape[1]), row),
            pl.BlockSpec((tm, D), row),
            pl.BlockSpec(w_out_b.shape, lambda i: (0, 0)),
            pl.BlockSpec((1, D), lambda i: (0, 0)),
        ],
        [pl.BlockSpec((tm, D), row), pl.BlockSpec((tm, D), row)],
        [jax.ShapeDtypeStruct((T, D), F32), jax.ShapeDtypeStruct((T, D), BF16)],
        (y_b, x, w_out_b, g_ffn),
        after=after,
    )


def _hidden_tile(F):
    return _tile(F, 1408, LANES)


def _gate_up(hn_b, wgT_b, wuT_b):
    T, D = hn_b.shape
    F = wgT_b.shape[0]
    tm, tf = _tile(T, 1024), _hidden_tile(F)

    def body(hn_ref, wg_ref, wu_ref, silu_ref, uds_ref, a_ref):
        hn = hn_ref[...]
        for c0 in range(0, tf, HIDDEN_CHUNK):
            cs = slice(c0, min(c0 + HIDDEN_CHUNK, tf))
            gv = _dot(hn, wg_ref[cs, :], NT)
            uv = _dot(hn, wu_ref[cs, :], NT)
            sg = _sigmoid(gv)
            silu = gv * sg
            silu_ref[:, cs] = silu.astype(BF16)
            uds_ref[:, cs] = (uv * (sg * (1.0 + gv * (1.0 - sg)))).astype(BF16)
            a_ref[:, cs] = (silu * uv).astype(BF16)

    wspec = pl.BlockSpec((tf, D), lambda j, i: (j, 0))
    ospec = pl.BlockSpec((tm, tf), lambda j, i: (i, j))
    return _call(
        "gate_up",
        body,
        (F // tf, T // tm),
        [pl.BlockSpec((tm, D), lambda j, i: (i, 0)), wspec, wspec],
        [ospec, ospec, ospec],
        [jax.ShapeDtypeStruct((T, F), BF16)] * 3,
        (hn_b, wgT_b, wuT_b),
    )


def _down_loss(a_b, wd_b, h1, target, g_final):
    T, D = h1.shape
    F = a_b.shape[1]
    tm = _tile(T, 512)
    nt = T // tm

    def body(a_ref, w_ref, h1_ref, t_ref, g_ref, dh2_ref, dh2b_ref, loss_ref, dg_ref):
        i = pl.program_id(0)
        h2 = h1_ref[...] + _dot(a_ref[...], w_ref[...], NN)
        r = lax.rsqrt(jnp.mean(h2 * h2, axis=-1, keepdims=True) + RMS_EPS)
        g = g_ref[...]
        diff = h2 * r * g - t_ref[...]
        _accumulate(loss_ref, i == 0, jnp.full(loss_ref.shape, jnp.sum(diff * diff) * (0.5 / D), F32))
        dh2, dg_rows = _rms_bwd(h2, g, diff * (1.0 / D))
        dh2_ref[...] = dh2
        dh2b_ref[...] = dh2.astype(BF16)
        _accumulate(dg_ref, i == 0, jnp.sum(dg_rows, axis=0, keepdims=True))

    row = lambda i: (i, 0)
    return _call(
        "down_loss",
        body,
        (nt,),
        [
            pl.BlockSpec((tm, F), row),
            pl.BlockSpec((F, D), lambda i: (0, 0), pipeline_mode=pl.Buffered(1)),
            pl.BlockSpec((tm, D), row),
            pl.BlockSpec((tm, D), row),
            pl.BlockSpec((1, D), lambda i: (0, 0)),
        ],
        [
            pl.BlockSpec((tm, D), row),
            pl.BlockSpec((tm, D), row),
            pl.BlockSpec((1, LANES), lambda i: (0, 0)),
            pl.BlockSpec((1, D), lambda i: (0, 0)),
        ],
        [
            jax.ShapeDtypeStruct((T, D), F32),
            jax.ShapeDtypeStruct((T, D), BF16),
            jax.ShapeDtypeStruct((1, LANES), F32),
            jax.ShapeDtypeStruct((1, D), F32),
        ],
        (a_b, wd_b, h1, target, g_final),
    )


def _ffn_bwd_act(dh2_b, wd_b, silu_b, uds_b, after=()):
    T, D = dh2_b.shape
    F = wd_b.shape[0]
    tm, tf = _tile(T, 1024), _hidden_tile(F)

    def body(d_ref, w_ref, silu_ref, uds_ref, dg_ref, du_ref):
        d = d_ref[...]
        for c0 in range(0, tf, HIDDEN_CHUNK):
            cs = slice(c0, min(c0 + HIDDEN_CHUNK, tf))
            da = _dot(d, w_ref[cs, :], NT)
            dg_ref[:, cs] = (da * uds_ref[:, cs].astype(F32)).astype(BF16)
            du_ref[:, cs] = (da * silu_ref[:, cs].astype(F32)).astype(BF16)

    aspec = pl.BlockSpec((tm, tf), lambda j, i: (i, j))
    return _call(
        "ffn_bwd_act",
        body,
        (F // tf, T // tm),
        [pl.BlockSpec((tm, D), lambda j, i: (i, 0)), pl.BlockSpec((tf, D), lambda j, i: (j, 0)), aspec, aspec],
        [aspec, aspec],
        [jax.ShapeDtypeStruct((T, F), BF16)] * 2,
        (dh2_b, wd_b, silu_b, uds_b),
        after=after,
    )


def _ffn_bwd_in(dg_b, du_b, wgT_b, wuT_b, h1, dh2, g_ffn, w_out_b, comm=()):
    T, D = h1.shape
    F = wgT_b.shape[0]
    DM = w_out_b.shape[0]
    tm = _tile(T, 512)

    def body(dg_ref, du_ref, wg_ref, wu_ref, h1_ref, dh2_ref, g_ref, wo_ref, dh1_ref, dh1b_ref, dy_ref, dgf_ref):
        i = pl.program_id(0)
        dhn = _dot(dg_ref[...], wg_ref[...], NN) + _dot(du_ref[...], wu_ref[...], NN)
        dx, dg_rows = _rms_bwd(h1_ref[...], g_ref[...], dhn)
        dh1 = dh2_ref[...] + dx
        dh1b = dh1.astype(BF16)
        dh1_ref[...] = dh1
        dh1b_ref[...] = dh1b
        dy_ref[...] = _dot(dh1b, wo_ref[...], NT)
        _accumulate(dgf_ref, i == 0, jnp.sum(dg_rows, axis=0, keepdims=True))

    row = lambda i: (i, 0)
    const = lambda i: (0, 0)
    return _call(
        "ffn_bwd_in",
        body,
        (T // tm,),
        [
            pl.BlockSpec((tm, F), row),
            pl.BlockSpec((tm, F), row),
            pl.BlockSpec((F, D), const, pipeline_mode=pl.Buffered(1)),
            pl.BlockSpec((F, D), const, pipeline_mode=pl.Buffered(1)),
            pl.BlockSpec((tm, D), row),
            pl.BlockSpec((tm, D), row),
            pl.BlockSpec((1, D), const),
            pl.BlockSpec((DM, D), const, pipeline_mode=pl.Buffered(1)),
        ],
        [pl.BlockSpec((tm, D), row), pl.BlockSpec((tm, D), row), pl.BlockSpec((tm, DM), row), pl.BlockSpec((1, D), const)],
        [
            jax.ShapeDtypeStruct((T, D), F32),
            jax.ShapeDtypeStruct((T, D), BF16),
            jax.ShapeDtypeStruct((T, DM), F32),
            jax.ShapeDtypeStruct((1, D), F32),
        ],
        (dg_b, du_b, wgT_b, wuT_b, h1, dh2, g_ffn, w_out_b),
        comm=comm,
    )


def _seq_bwd(z, dy, v, w_dw4, ln_g, ln_b, w_pool_b, s_pool, comm=()):
    T, CI = z.shape
    CC = ln_g.shape[1]
    n_grp, G = w_pool_b.shape[0], w_pool_b.shape[-1]
    CP = n_grp * G
    KW = w_dw4.shape[1]
    n_cc = CC // LANES
    D = CC + CP
    tt = _tile(T, 512, HALO)
    per = tt // HALO
    n_tiles = T // tt
    last_halo = T // HALO - 1

    def body(zc_ref, zp_ref, dyc_ref, dyn_ref, vc_ref, vn_ref, wdw_ref, lng_ref, lnb_ref, wp_ref, sp_ref,
             dz_ref, dwdw_ref, dbdw_ref, dlng_ref, dlnb_ref, dwp_ref, dsp_ref, dbin_ref,
             dv_scr, u_scr, p_scr, g_scr, dw_scr):
        i = pl.program_id(0)
        first = i == 0
        last = i == n_tiles - 1
        lng, lnb = lng_ref[...], lnb_ref[...]

        def conv_pre(vv, dyc):
            mu = jnp.mean(vv, axis=-1, keepdims=True)
            d = vv - mu
            rs = lax.rsqrt(jnp.mean(d * d, axis=-1, keepdims=True) + LN_EPS)
            xh = d * rs
            ln = xh * lng + lnb
            sg = _sigmoid(ln)
            dln = dyc * (sg * (1.0 + ln * (1.0 - sg)))
            dxh = dln * lng
            dv = rs * (dxh - jnp.mean(dxh, axis=-1, keepdims=True) - xh * jnp.mean(dxh * xh, axis=-1, keepdims=True))
            return dv, dln, xh

        dv_c, dln_c, xh_c = conv_pre(vc_ref[...], dyc_ref[:, 0:CC])
        dv_scr[0, 0:tt, :] = dv_c
        dv_n, _, _ = conv_pre(vn_ref[...], dyn_ref[:, 0:CC])
        dv_scr[0, tt:, :] = jnp.where(last, 0.0, dv_n)
        _fill_shifted(dv_scr)
        _accumulate(dlng_ref, first, jnp.sum(dln_c * xh_c, axis=0, keepdims=True))
        _accumulate(dlnb_ref, first, jnp.sum(dln_c, axis=0, keepdims=True))
        _accumulate(dbdw_ref, first, jnp.sum(dv_c, axis=0, keepdims=True))

        u_scr[...] = zc_ref[:, 0:CC] * _sigmoid(zc_ref[:, CC : 2 * CC])

        @pl.when(first)
        def _():
            dw_scr[...] = jnp.zeros_like(dw_scr)

        for j in range(n_cc):
            cs = slice(LANES * j, LANES * (j + 1))
            gs = slice(CC + LANES * j, CC + LANES * (j + 1))
            dbin_a = jnp.zeros((1, LANES), F32)
            dbin_g = jnp.zeros((1, LANES), F32)
            for rb in range(tt // CONV_ROWS):
                rows = slice(rb * CONV_ROWS, (rb + 1) * CONV_ROWS)
                u_blk = u_scr[rows, cs]
                du = jnp.zeros((CONV_ROWS, LANES), F32)
                for k in range(KW):
                    off = rb * CONV_ROWS + (KW - 1) - k
                    d = _shifted_rows(dv_scr, off, CONV_ROWS, cs)
                    du = du + d * wdw_ref[j, k]
                    dw_scr[j * HALO + k] += jnp.sum((u_blk * d).reshape(CONV_ROWS // 8, 8, LANES), axis=0)
                a = zc_ref[rows, cs]
                sg = _sigmoid(zc_ref[rows, gs])
                da = du * sg
                dgate = du * a * sg * (1.0 - sg)
                dz_ref[rows, cs] = da.astype(BF16)
                dz_ref[rows, gs] = dgate.astype(BF16)
                dbin_a = dbin_a + jnp.sum(da, axis=0, keepdims=True)
                dbin_g = dbin_g + jnp.sum(dgate, axis=0, keepdims=True)
            _accumulate(dbin_ref.at[:, cs], first, dbin_a)
            _accumulate(dbin_ref.at[:, gs], first, dbin_g)

        @pl.when(last)
        def _():
            dwdw_ref[...] = jnp.sum(dw_scr[...], axis=1).reshape(dwdw_ref.shape)

        p_scr[0:HALO, :] = jnp.where(first, 0.0, zp_ref[:, 2 * CC :])
        p_scr[HALO:, :] = zc_ref[:, 2 * CC :]
        tpos = i * tt + lax.broadcasted_iota(jnp.int32, (tt, 1), 0)
        for gi, w in enumerate(POOL_WINDOWS):
            cs = slice(G * gi, G * (gi + 1))
            ys = slice(CC + G * gi, CC + G * (gi + 1))
            ps = slice(2 * CC + G * gi, 2 * CC + G * (gi + 1))
            cnt = jnp.minimum(tpos + 1, w).astype(F32)
            yib = _pool_mean_minus_token(p_scr, cs, w, cnt, tt).astype(BF16)
            wp = wp_ref[gi]
            sp = sp_ref[:, cs]
            dyp = dyc_ref[:, ys]
            q = _dot(yib, wp, NN)
            _accumulate(dsp_ref.at[:, cs], first, jnp.sum(dyp * q, axis=0, keepdims=True))
            dq_c = (dyp * sp).astype(BF16)
            dq_n = (jnp.where(last, 0.0, dyn_ref[:, ys]) * sp).astype(BF16)
            _accumulate(dwp_ref.at[gi], first, _dot(yib, dq_c, TN))
            dyi_c = _dot(dq_c, wp, NT)
            g_scr[0:tt, cs] = dyi_c / cnt
            g_scr[tt:, cs] = _dot(dq_n, wp, NT) * (1.0 / w)
            dp = -dyi_c
            for d in range(w):
                dp = dp + g_scr[d : d + tt, cs]
            dz_ref[:, ps] = dp.astype(BF16)
            _accumulate(dbin_ref.at[:, ps], first, jnp.sum(dp, axis=0, keepdims=True))

    cur = lambda i: (i, 0)
    prev = lambda i: (jnp.maximum(i * per - 1, 0), 0)
    nxt = lambda i: (jnp.minimum((i + 1) * per, last_halo), 0)
    c2 = lambda i: (0, 0)
    c3 = lambda i: (0, 0, 0)
    return _call(
        "seq_bwd",
        body,
        (n_tiles,),
        [
            pl.BlockSpec((tt, CI), cur),
            pl.BlockSpec((HALO, CI), prev),
            pl.BlockSpec((tt, D), cur),
            pl.BlockSpec((HALO, D), nxt),
            pl.BlockSpec((tt, CC), cur),
            pl.BlockSpec((HALO, CC), nxt),
            pl.BlockSpec(w_dw4.shape, lambda i: (0,) * w_dw4.ndim),
            pl.BlockSpec((1, CC), c2),
            pl.BlockSpec((1, CC), c2),
            pl.BlockSpec(w_pool_b.shape, c3),
            pl.BlockSpec((1, CP), c2),
        ],
        [
            pl.BlockSpec((tt, CI), cur),
            pl.BlockSpec((n_cc, HALO, LANES), c3),
            pl.BlockSpec((1, CC), c2),
            pl.BlockSpec((1, CC), c2),
            pl.BlockSpec((1, CC), c2),
            pl.BlockSpec((n_grp, G, G), c3),
            pl.BlockSpec((1, CP), c2),
            pl.BlockSpec((1, CI), c2),
        ],
        [
            jax.ShapeDtypeStruct((T, CI), BF16),
            jax.ShapeDtypeStruct((n_cc, HALO, LANES), F32),
            jax.ShapeDtypeStruct((1, CC), F32),
            jax.ShapeDtypeStruct((1, CC), F32),
            jax.ShapeDtypeStruct((1, CC), F32),
            jax.ShapeDtypeStruct((n_grp, G, G), F32),
            jax.ShapeDtypeStruct((1, CP), F32),
            jax.ShapeDtypeStruct((1, CI), F32),
        ],
        (z, z, dy, dy, v, v, w_dw4, ln_g, ln_b, w_pool_b, s_pool),
        scratch=[
            pltpu.VMEM((SUBLANES, tt + HALO, CC), F32),
            pltpu.VMEM((tt, CC), F32),
            pltpu.VMEM((HALO + tt, CP), F32),
            pltpu.VMEM((tt + HALO, CP), F32),
            pltpu.VMEM((n_cc * HALO, 8, LANES), F32),
        ],
        comm=comm,
    )


def _in_proj_bwd(dz_b, w_inT_b, x, dh1, g_mix, after=()):
    T, D = x.shape
    CI = w_inT_b.shape[0]
    tm = _tile(T, 512)

    def body(dz_ref, w_ref, x_ref, dh1_ref, g_ref, dx_ref, dg_ref):
        i = pl.program_id(0)
        dxn = _dot(dz_ref[...], w_ref[...], NN)
        dx, dg_rows = _rms_bwd(x_ref[...], g_ref[...], dxn)
        dx_ref[...] = dh1_ref[...] + dx
        _accumulate(dg_ref, i == 0, jnp.sum(dg_rows, axis=0, keepdims=True))

    row = lambda i: (i, 0)
    const = lambda i: (0, 0)
    return _call(
        "in_proj_bwd",
        body,
        (T // tm,),
        [
            pl.BlockSpec((tm, CI), row),
            pl.BlockSpec((CI, D), const),
            pl.BlockSpec((tm, D), row),
            pl.BlockSpec((tm, D), row),
            pl.BlockSpec((1, D), const),
        ],
        [pl.BlockSpec((tm, D), row), pl.BlockSpec((1, D), const)],
        [jax.ShapeDtypeStruct((T, D), F32), jax.ShapeDtypeStruct((1, D), F32)],
        (dz_b, w_inT_b, x, dh1, g_mix),
        after=after,
    )


def _weight_grad(name, a_b, b_b, comm=()):
    T, N1 = a_b.shape
    N2 = b_b.shape[1]
    t1 = _tile(N1, 1408, LANES)
    tk = _tile(T, 2048)
    nk = T // tk

    def body(a_ref, b_ref, o_ref, acc):
        k = pl.program_id(1)
        _accumulate(acc, k == 0, _dot(a_ref[...], b_ref[...], TN))

        @pl.when(k == nk - 1)
        def _():
            o_ref[...] = acc[...].astype(BF16)

    (out,), rest = _call(
        name,
        body,
        (N1 // t1, nk),
        [pl.BlockSpec((tk, t1), lambda n, k: (k, n)), pl.BlockSpec((tk, N2), lambda n, k: (k, 0))],
        [pl.BlockSpec((t1, N2), lambda n, k: (n, 0))],
        [jax.ShapeDtypeStruct((N1, N2), BF16)],
        (a_b, b_b),
        scratch=[pltpu.VMEM((t1, N2), F32)],
        comm=comm,
    )
    return out, rest


def _sum_parts(name, full, how, parts, me):
    _, R, C = parts[0].shape
    tr = _tile(R, 512)
    nb = R // tr
    where = [(q, r) for q, p in enumerate(parts) for r in range(p.shape[0])]
    assert len(where) == 3

    def body(me_ref, own_ref, *refs):
        o_ref = refs[-1]
        f = lambda j: refs[where[j][0]][where[j][1]].astype(F32)
        o_ref[...] = (own_ref[...].astype(F32) + f(0)) + (f(1) + f(2))

    own_map = {"rows": lambda i, me_ref: (me_ref[0] * nb + i, 0), "all": lambda i, me_ref: (i, 0)}[how]
    return pl.pallas_call(
        body,
        name=name,
        grid_spec=pltpu.PrefetchScalarGridSpec(
            num_scalar_prefetch=1,
            grid=(nb,),
            in_specs=[pl.BlockSpec((tr, C), own_map)]
            + [pl.BlockSpec((p.shape[0], tr, C), lambda i, me_ref: (0, i, 0)) for p in parts],
            out_specs=pl.BlockSpec((tr, C), lambda i, me_ref: (i, 0)),
        ),
        out_shape=jax.ShapeDtypeStruct((R, C), F32),
        compiler_params=pltpu.CompilerParams(dimension_semantics=("arbitrary",), vmem_limit_bytes=VMEM_LIMIT),
    )(me, full, *parts)


_M_CORR = 1.0 - ADAM_B1**ADAM_STEP
_V_CORR = 1.0 - ADAM_B2**ADAM_STEP


def _adamw_math(w, g, m, v):
    m = ADAM_B1 * m + (1.0 - ADAM_B1) * g
    v = ADAM_B2 * v + (1.0 - ADAM_B2) * (g * g)
    delta = -ADAM_LR * ((m / _M_CORR) / (jnp.sqrt(v / _V_CORR) + ADAM_EPS) + ADAM_WD * w)
    return delta, m, v


def _adamw(name, w, m, v, g_here, g_there, g_transposed=False):
    R, C = w.shape
    tr = _tile(R, 256, LANES if g_transposed else 8)

    def body(w_ref, m_ref, v_ref, ga_ref, gb_ref, g_ref, d_ref, nm_ref, nv_ref):
        g = ga_ref[...] + gb_ref[...]
        if g_transposed:
            g = g.T
        g_ref[...] = g
        d_ref[...], nm_ref[...], nv_ref[...] = _adamw_math(w_ref[...], g, m_ref[...], v_ref[...])

    spec = pl.BlockSpec((tr, C), lambda i: (i, 0))
    gspec = pl.BlockSpec((C, tr), lambda i: (0, i)) if g_transposed else spec
    return _call(name, body, (R // tr,), [spec] * 3 + [gspec] * 2, [spec] * 4, [jax.ShapeDtypeStruct((R, C), F32)] * 4,
                 (w, m, v, g_here, g_there))


def _adamw_on_sparsecore(name, w, m, v, g_here, g_there, after):
    R, C = w.shape
    n_groups = R // SUBLANES
    n_turns = -(-n_groups // SC_TILES)
    n_in, n_out = 5, 4

    def body(w_hbm, m_hbm, v_hbm, ga_hbm, gb_hbm, after_hbm, g_out, d_out, nm_out, nv_out, bufs, sems):
        tile = lax.axis_index("subcore") * SC_CORES + lax.axis_index("sparsecore")
        srcs = (w_hbm, m_hbm, v_hbm, ga_hbm, gb_hbm)
        dsts = (d_out, nm_out, nv_out, g_out)

        def rows(turn):
            return pl.ds((tile + turn * SC_TILES) * SUBLANES, SUBLANES)

        def loads(turn):
            slot = turn % 2
            return [pltpu.make_async_copy(srcs[q].at[rows(turn), :], bufs.at[slot, q], sems.at[slot, q]) for q in range(n_in)]

        def stores(turn):
            slot = turn % 2
            return [pltpu.make_async_copy(bufs.at[slot, q], dsts[q].at[rows(turn), :], sems.at[slot, n_in + q])
                    for q in range(n_out)]

        def when_mine(turn, fn):
            pl.when(tile + turn * SC_TILES < n_groups)(fn)

        def compute(slot):
            wb, mb, vb, gab, gbb = (bufs.at[slot, q] for q in range(n_in))

            @pl.loop(0, SUBLANES)
            def _(r):
                @pl.loop(0, C, step=SC_LANES)
                def _(i):
                    at = (r, pl.ds(i, SC_LANES))
                    g = gab[at] + gbb[at]
                    delta, new_m, new_v = _adamw_math(wb[at], g, mb[at], vb[at])
                    gab[at], wb[at], mb[at], vb[at] = g, delta, new_m, new_v

        def start_loads(turn):
            def fn():
                for cp in loads(turn):
                    cp.start()

            when_mine(turn, fn)

        start_loads(0)
        for turn in range(n_turns):
            def step(turn=turn):
                for cp in loads(turn):
                    cp.wait()
                if turn >= 1:
                    for cp in stores(turn - 1):
                        cp.wait()
                if turn + 1 < n_turns:
                    start_loads(turn + 1)
                compute(turn % 2)
                for cp in stores(turn):
                    cp.start()

            when_mine(turn, step)
        for turn in range(n_turns):
            def drain(turn=turn):
                for cp in stores(turn):
                    cp.wait()

            last_mine = jnp.logical_and(tile + turn * SC_TILES < n_groups, tile + (turn + 1) * SC_TILES >= n_groups)
            pl.when(last_mine)(drain)

    return pl.kernel(
        body,
        name=name,
        out_type=[jax.ShapeDtypeStruct((R, C), F32)] * 4,
        mesh=plsc.VectorSubcoreMesh(core_axis_name="sparsecore", subcore_axis_name="subcore"),
        scratch_types=[pltpu.VMEM((2, n_in, SUBLANES, C), F32), pltpu.SemaphoreType.DMA((2, n_in + n_out))],
        compiler_params=pltpu.CompilerParams(use_tc_tiling_on_sc=True),
    )(w, m, v, g_here, g_there, after)


class _PackLayout:
    def __init__(self, n_cc, n_grp, G, widths):
        self.dw_rows = (0, HALO)
        self.wp_rows = (HALO, HALO + G)
        self.n_cc, self.n_grp, self.G = n_cc, n_grp, G
        self.vec = {}
        r = HALO + G
        for name, width in widths:
            self.vec[name] = (r, width)
            r += width // PACK_W
        self.rows = -(-r // 8) * 8


def _pack_small(layout, dwdw, dwp, vecs):
    names = list(vecs)

    def body(*refs):
        dw_ref, wp_ref = refs[0], refs[1]
        vec_refs = refs[2 : 2 + len(names)]
        o_ref = refs[-1]
        o_ref[...] = jnp.zeros_like(o_ref)
        for j in range(layout.n_cc):
            o_ref[layout.dw_rows[0] : layout.dw_rows[1], j * LANES : (j + 1) * LANES] = dw_ref[j]
        for i in range(layout.n_grp):
            o_ref[layout.wp_rows[0] : layout.wp_rows[1], i * layout.G : (i + 1) * layout.G] = wp_ref[i]
        for name, ref in zip(names, vec_refs):
            r, width = layout.vec[name]
            for h in range(width // PACK_W):
                o_ref[r + h : r + h + 1, :] = ref[:, h * PACK_W : (h + 1) * PACK_W]

    return pl.pallas_call(
        body,
        name="pack_small",
        out_shape=jax.ShapeDtypeStruct((layout.rows, PACK_W), F32),
    )(dwdw, dwp, *[vecs[k] for k in names])


def _adamw_small(layout, g_here, g_there, w_dw, m_dw, v_dw, w_pool, m_pool, v_pool, vec_w, vec_m, vec_v):
    names = list(vec_w)
    nv = len(names)

    def body(*refs):
        ga_ref, gb_ref = refs[0], refs[1]
        wdw, mdw, vdw, wp, mp, vp = refs[2:8]
        vw, vm, vv = refs[8 : 8 + nv], refs[8 + nv : 8 + 2 * nv], refs[8 + 2 * nv : 8 + 3 * nv]
        outs = refs[8 + 3 * nv :]
        acc = outs[-1]
        acc[...] = ga_ref[...] + gb_ref[...]

        def emit(o, g, w, m, v, idx=()):
            res = (g,) + _adamw_math(w, g, m, v)
            for ref, val in zip(o, res):
                ref[idx] = val

        me = 2 * lax.axis_index("x") + lax.axis_index("y")
        for j in range(layout.n_cc):

            @pl.when(me == j)
            def _(j=j):
                for k in range(wdw.shape[0]):
                    g = acc[layout.dw_rows[0] + k : layout.dw_rows[0] + k + 1, j * LANES : (j + 1) * LANES]
                    emit(outs[0:4], g, wdw[k], mdw[k], vdw[k], idx=k)

        for i in range(layout.n_grp):
            g = acc[layout.wp_rows[0] : layout.wp_rows[1], i * layout.G : (i + 1) * layout.G]
            emit(outs[4:8], g, wp[i], mp[i], vp[i], idx=i)
        for q, name in enumerate(names):
            r, width = layout.vec[name]
            for h in range(width // PACK_W):
                ls = slice(h * PACK_W, (h + 1) * PACK_W)
                g = acc[r + h : r + h + 1, :]
                emit(outs[8 + 4 * q : 12 + 4 * q], g, vw[q][:, ls], vm[q][:, ls], vv[q][:, ls], idx=(slice(None), ls))

    shapes = [w_dw.shape] * 4 + [w_pool.shape] * 4
    for name in names:
        shapes += [vec_w[name].shape] * 4
    return pl.pallas_call(
        body,
        name="adamw_small",
        out_shape=[jax.ShapeDtypeStruct(s, F32) for s in shapes],
        scratch_shapes=[pltpu.VMEM(g_here.shape, F32)],
    )(g_here, g_there, w_dw, m_dw, v_dw, w_pool, m_pool, v_pool,
      *[vec_w[k] for k in names], *[vec_m[k] for k in names], *[vec_v[k] for k in names])


def _allreduce_adamw_row(g_part, w, m, v, loss_part, comm=()):
    D = w.shape[1]
    n_pairs = N_DEV - 1

    def body(g_ref, w_ref, m_ref, v_ref, l_ref, go_ref, d_ref, nm_ref, nv_ref, lo_ref, land_g, land_l, sems):
        x, y, c = _place()
        copies = []
        for q, (src, land) in enumerate(((g_ref, land_g), (l_ref, land_l))):
            for r in range(1, N_DEV):
                fx, fy, fc = (r >> 2) & 1, (r >> 1) & 1, r & 1
                peer = (1 - x if fx else x, 1 - y if fy else y, 1 - c if fc else c)
                cp = _remote(src, land.at[r], sems, 2 * (q * n_pairs + r - 1), peer)
                cp.start()
                copies.append(cp)
        for cp in copies:
            cp.wait()

        def total(src, land):
            row = lambda r: src[...] if r == 0 else land[r]
            return ((row(0) + row(4)) + (row(2) + row(6))) + ((row(1) + row(5)) + (row(3) + row(7)))

        g = total(g_ref, land_g)
        go_ref[...] = g
        d_ref[...], nm_ref[...], nv_ref[...] = _adamw_math(w_ref[...], g, m_ref[...], v_ref[...])
        lo_ref[...] = total(l_ref, land_l)

    vm = pl.BlockSpec(memory_space=pltpu.VMEM)
    return _call(
        "allreduce_adamw_g_mix",
        body,
        (),
        [vm] * 5,
        [vm] * 5,
        [jax.ShapeDtypeStruct((1, D), F32)] * 4 + [jax.ShapeDtypeStruct(loss_part.shape, F32)],
        (g_part, w, m, v, loss_part),
        scratch=[pltpu.VMEM((N_DEV, 1, D), F32), pltpu.VMEM((N_DEV,) + loss_part.shape, F32),
                 pltpu.SemaphoreType.DMA((4 * n_pairs,))],
        comm=comm,
    )


def kernel(x, g_mix, w_in, b_in, w_dw, b_dw, ln_g, ln_b, w_pool, s_pool, w_out, g_ffn, w_gate, w_up, w_down, g_final, loss_target, m_g_mix, m_w_in, m_b_in, m_w_dw, m_b_dw, m_ln_g, m_ln_b, m_w_pool, m_s_pool, m_w_out, m_g_ffn, m_w_gate, m_w_up, m_w_down, m_g_final, v_g_mix, v_w_in, v_b_in, v_w_dw, v_b_dw, v_ln_g, v_ln_b, v_w_pool, v_s_pool, v_w_out, v_g_ffn, v_w_gate, v_w_up, v_w_down, v_g_final):
    x2 = x[0]
    target = loss_target[0]
    T, D = x2.shape
    w_in2, w_out2, w_down2 = w_in[0], w_out[0], w_down[0]
    taps_first = lambda a: jnp.transpose(a, (1, 0, 2))
    w_dw3 = taps_first(w_dw)
    w_gateT, w_upT = w_gate[0].T, w_up[0].T
    CI = w_in2.shape[1] * N_CHIPS
    DM = w_out2.shape[0] * N_CHIPS
    F = w_down2.shape[0] * N_CHIPS
    KW, _, dw_cols = w_dw3.shape
    assert dw_cols == LANES
    n_grp, G = w_pool.shape[1], w_pool.shape[-1]
    w_pool3 = w_pool[0]
    g_final2 = g_final.reshape(1, D)

    me = (2 * lax.axis_index("x") + lax.axis_index("y")).astype(jnp.int32).reshape(1)

    w_inT_b, w_dw4, f_out, f_gate, f_up, f_down = _place_and_gather(
        [(w_in2, "rows", (CI, D), BF16, True, True), (w_dw3, "lead", (N_CHIPS, KW, 1, dw_cols), F32, False, False)],
        [(w, "rows", shape, BF16, False, True)
         for w, shape in ((w_out2, (DM, D)), (w_gateT, (F, D)), (w_upT, (F, D)), (w_down2, (F, D)))])
    w_pool_b = w_pool3.astype(BF16)
    ici = lambda f: _GatherIci([f], ["rows"], [True])
    d2d = lambda f: _GatherD2d([f], ["rows"])
    gather = _start("gather_start", [ici(f_out), ici(f_gate), ici(f_up), ici(f_down)])
    (z, xn_b), _ = _in_proj(x2, g_mix, w_inT_b, b_in, after=[gather.token])
    (f_out,) = _wait("gather_out_wait", gather, 0, xn_b)
    s_out = _start("share_out_start", [d2d(f_out)], sibling_only=True)
    (y_b, v), _ = _seq_fwd(z, w_dw4, b_dw, ln_g, ln_b, w_pool_b, s_pool, after=[s_out.token])
    (w_out_b,) = _wait("share_out_wait", s_out, 0, y_b)
    (f_gate,) = _wait("gather_gate_wait", gather, 1, y_b)
    s_gate = _start("share_gate_start", [d2d(f_gate)], sibling_only=True)
    (h1, hn_b), _ = _out_proj(y_b, x2, w_out_b, g_ffn, after=[s_gate.token])
    (f_up,) = _wait("gather_up_wait", gather, 2, hn_b)
    s_up = _start("share_up_start", [d2d(f_up)], sibling_only=True)
    (wgT_b,) = _wait("share_gate_wait", s_gate, 0, hn_b)
    (wuT_b,) = _wait("share_up_wait", s_up, 0, hn_b)
    (silu_b, uds_b, a_b), _ = _gate_up(hn_b, wgT_b, wuT_b)
    (f_down,) = _wait("gather_down_wait", gather, 3, a_b)
    s_down = _start("share_down_start", [d2d(f_down)], sibling_only=True)
    (wd_b,) = _wait("share_down_wait", s_down, 0, a_b)
    (dh2, dh2_b, loss_part, d_g_final), _ = _down_loss(a_b, wd_b, h1, target, g_final2)

    gw_down, _ = _weight_grad("grad_w_down", a_b, dh2_b)
    x_down = _start("scatter_down_start", [_Scatter([gw_down], ["rows"])])
    (dg_b, du_b), _ = _ffn_bwd_act(dh2_b, wd_b, silu_b, uds_b, after=[x_down.token])
    gw_gateT, _ = _weight_grad("grad_w_gate", dg_b, hn_b)
    gw_upT, _ = _weight_grad("grad_w_up", du_b, hn_b)
    gw_down, p_down = _wait("scatter_down_wait", x_down, 0, gw_upT)
    sum_down = _sum_parts("sum_w_down", gw_down, "rows", [p_down], me)
    (dh1, dh1_b, dy, d_g_ffn), (p_gate, oth_down) = _ffn_bwd_in(
        dg_b, du_b, wgT_b, wuT_b, h1, dh2, g_ffn, w_out_b, comm=[_Scatter([gw_gateT], ["rows"]), _Swap([sum_down])])
    gw_out, _ = _weight_grad("grad_w_out", y_b, dh1_b)
    sum_gate = _sum_parts("sum_w_gate", gw_gateT, "rows", [p_gate], me)
    res = {}
    res["w_down"] = _adamw_on_sparsecore("adamw_w_down", w_down2, m_w_down[0], v_w_down[0], sum_down, oth_down, sum_down)
    (dz_b, d_wdw, d_bdw, d_lng, d_lnb, d_wp, d_sp, d_bin), (p_up, p_out, oth_gate) = _seq_bwd(
        z, dy, v, w_dw4, ln_g, ln_b, w_pool_b, s_pool,
        comm=[_Scatter([gw_upT, gw_out], ["rows", "rows"]), _Swap([sum_gate])])
    res["w_gate"] = _adamw_on_sparsecore(
        "adamw_w_gate", w_gateT, m_w_gate[0].T, v_w_gate[0].T, sum_gate, oth_gate, res["w_down"][0])
    vec_grads ={"b_dw": d_bdw, "ln_g": d_lng, "ln_b": d_lnb, "s_pool": d_sp, "g_ffn": d_g_ffn, "g_final": d_g_final, "b_in": d_bin}
    layout = _PackLayout(dw_cols * N_CHIPS // LANES, n_grp, G, [(k, a.shape[1]) for k, a in vec_grads.items()])
    pack = _pack_small(layout, d_wdw, d_wp, vec_grads)
    sum_up = _sum_parts("sum_w_up", gw_upT, "rows", [p_up], me)
    sum_out = _sum_parts("sum_w_out", gw_out, "rows", [p_out], me)
    gw_inT, (p_small, oth_up, oth_out) = _weight_grad(
        "grad_w_in", dz_b, xn_b, comm=[_Scatter([pack], ["all"]), _Swap([sum_up, sum_out])])
    sum_small = _sum_parts("sum_small", pack, "all", [p_small], me)
    late = _start("late_start", [_Scatter([gw_inT], ["rows"]), _Swap([sum_small])])
    (grad_x, d_g_mix), _ = _in_proj_bwd(dz_b, w_inT_b, x2, dh1, g_mix, after=[late.token])
    gw_inT, p_in = _wait("late_w_in_wait", late, 0, d_g_mix)
    sum_small, oth_small = _wait("late_small_wait", late, 1, d_g_mix)
    res["w_up"] = _adamw_on_sparsecore("adamw_w_up", w_upT, m_w_up[0].T, v_w_up[0].T, sum_up, oth_up, res["w_gate"][0])
    res["w_out"] = _adamw_on_sparsecore("adamw_w_out", w_out2, m_w_out[0], v_w_out[0], sum_out, oth_out, res["w_gate"][0])
    sum_in = _sum_parts("sum_w_in", gw_inT, "rows", [p_in], me)
    (*res["g_mix"], loss_row), (oth_in,) = _allreduce_adamw_row(
        d_g_mix, g_mix, m_g_mix, v_g_mix, loss_part, comm=[_Swap([sum_in])])
    loss = loss_row[0, 0]
    res["w_in"], _ = _adamw("adamw_w_in", w_in2, m_w_in[0], v_w_in[0], sum_in, oth_in, g_transposed=True)

    vec_w = {"b_dw": b_dw, "ln_g": ln_g, "ln_b": ln_b, "s_pool": s_pool, "g_ffn": g_ffn, "g_final": g_final2, "b_in": b_in}
    vec_m = {"b_dw": m_b_dw, "ln_g": m_ln_g, "ln_b": m_ln_b, "s_pool": m_s_pool, "g_ffn": m_g_ffn,
             "g_final": m_g_final.reshape(1, D), "b_in": m_b_in}
    vec_v = {"b_dw": v_b_dw, "ln_g": v_ln_g, "ln_b": v_ln_b, "s_pool": v_s_pool, "g_ffn": v_g_ffn,
             "g_final": v_g_final.reshape(1, D), "b_in": v_b_in}
    small = _adamw_small(layout, sum_small, oth_small, w_dw3, taps_first(m_w_dw), taps_first(v_w_dw),
                         w_pool3, m_w_pool[0], v_w_pool[0], vec_w, vec_m, vec_v)
    res["w_dw"] = [taps_first(a) for a in small[0:4]]
    res["w_pool"] = [a[None] for a in small[4:8]]
    for q, k in enumerate(vec_w):
        res[k] = list(small[8 + 4 * q : 12 + 4 * q])
    res["g_final"] = [a.reshape(D) for a in res["g_final"]]
    for k in ("w_in", "w_out", "w_down"):
        res[k] = [a[None] for a in res[k]]
    for k in ("w_gate", "w_up"):
        res[k] = [a.T[None] for a in res[k]]

    order = ["g_mix", "w_in", "b_in", "w_dw", "b_dw", "ln_g", "ln_b", "w_pool", "s_pool", "w_out", "g_ffn", "w_gate", "w_up", "w_down", "g_final"]
    outs = [loss, grad_x[None]]
    for q in range(4):
        outs += [res[k][q] for k in order]
    return tuple(outs)
```

```python
import jax
import jax.numpy as jnp
from jax import lax
from jax.experimental import pallas as pl
from jax.experimental.pallas import tpu as pltpu
from jax.experimental.pallas import tpu_sc as plsc

F32 = jnp.float32
BF16 = jnp.bfloat16
MESH = pl.DeviceIdType.MESH
ANY = pl.BlockSpec(memory_space=pl.ANY)

RMS_EPS = 1e-6
LN_EPS = 1e-5
POOL_WINDOWS = (2, 4, 8, 16)
ADAM_LR = 0.001
ADAM_B1 = 0.9
ADAM_B2 = 0.999
ADAM_EPS = 1e-08
ADAM_WD = 0.01
ADAM_STEP = 10

LANES = 128
SUBLANES = 8
BF16_ROWS = 16
HALO = 32
CONV_ROWS = 64
HIDDEN_CHUNK = 512
VMEM_LIMIT = 56 * 1024 * 1024
PACK_W = 512
N_CHIPS = 4
N_DEV = 8
SIBLING_BARRIER_ID = 0
SC_CORES = 2
SC_TILES = 32
SC_LANES = 16


def _tile(n, want, mult=8):
    t = min(n, want)
    while n % t or t % mult:
        t -= 1
    return t


def _sigmoid(x):
    return 1.0 / (1.0 + jnp.exp(-x))


def _dot(a, b, dims):
    return lax.dot_general(a, b, (dims, ((), ())), preferred_element_type=F32)


NN = ((1,), (0,))
NT = ((1,), (1,))
TN = ((0,), (0,))


def _rms_bwd(x, g, dy):
    r = lax.rsqrt(jnp.mean(x * x, axis=-1, keepdims=True) + RMS_EPS)
    xh = x * r
    gy = dy * g
    dx = r * (gy - xh * jnp.mean(gy * xh, axis=-1, keepdims=True))
    return dx, dy * xh


def _accumulate(ref, first, val):
    @pl.when(first)
    def _():
        ref[...] = val

    @pl.when(jnp.logical_not(first))
    def _():
        ref[...] += val


def _place():
    return lax.axis_index("x"), lax.axis_index("y"), lax.axis_index("c")


def _other_chips(x, y):
    return [(1 - x, y), (x, 1 - y), (1 - x, 1 - y)]


def _rows(ref, start, n):
    return ref.at[pl.ds(pl.multiple_of(start, BF16_ROWS), n)]


def _window(ref, how, k, c=None):
    if how == "all":
        return ref
    if how == "lead":
        return ref.at[k]
    assert how == "rows"
    n = ref.shape[0] // N_CHIPS
    if c is None:
        return _rows(ref, k * n, n)
    return _rows(ref, k * n + c * (n // 2), n // 2)


def _remote(src, dst, sems, s, device):
    return pltpu.make_async_remote_copy(
        src_ref=src, dst_ref=dst, send_sem=sems.at[s], recv_sem=sems.at[s + 1], device_id=device, device_id_type=MESH)


class _GatherIci:
    aliased = True

    def __init__(self, fulls, hows, splits, which=(0, 1, 2)):
        self.fulls, self.hows, self.splits, self.which = list(fulls), list(hows), list(splits), tuple(which)

    def inputs(self):
        return self.fulls

    def out_shapes(self):
        return [jax.ShapeDtypeStruct(a.shape, a.dtype) for a in self.fulls]

    def n_sems(self):
        return 6 * len(self.fulls)

    def build(self, ins, outs, sems, base):
        x, y, c = _place()
        me = 2 * x + y
        chips = _other_chips(x, y)
        starts, waits = [], []
        for a, (how, sp) in enumerate(zip(self.hows, self.splits)):
            half = c if sp else None
            mine = _window(outs[a], how, me, half)
            for j in self.which:
                px, py = chips[j]
                s = base + 6 * a + 2 * j
                cp = _remote(mine, mine, sems, s, (px, py, c))
                landing = _remote(mine, _window(outs[a], how, 2 * px + py, half), sems, s, (px, py, c))
                starts.append(cp.start)
                waits += [landing.wait_recv, cp.wait_send]
        return starts, waits


class _GatherD2d:
    aliased = True

    def __init__(self, fulls, hows):
        self.fulls, self.hows = list(fulls), list(hows)

    def inputs(self):
        return self.fulls

    def out_shapes(self):
        return [jax.ShapeDtypeStruct(a.shape, a.dtype) for a in self.fulls]

    def n_sems(self):
        return 6 * len(self.fulls)

    def build(self, ins, outs, sems, base):
        x, y, c = _place()
        starts, waits = [], []
        for a, how in enumerate(self.hows):
            for j, (px, py) in enumerate(_other_chips(x, y)):
                s = base + 6 * a + 2 * j
                got = _window(outs[a], how, 2 * px + py, c)
                cp = _remote(got, got, sems, s, (x, y, 1 - c))
                landing = _remote(got, _window(outs[a], how, 2 * px + py, 1 - c), sems, s, (x, y, 1 - c))
                starts.append(cp.start)
                waits += [landing.wait_recv, cp.wait_send]
        return starts, waits


def _part_shape(a, how):
    if how == "all":
        return a.shape
    assert how == "rows"
    return (a.shape[0] // N_CHIPS, a.shape[1])


class _Scatter:
    aliased = False

    def __init__(self, fulls, hows, which=(0, 1, 2)):
        self.fulls, self.hows, self.which = list(fulls), list(hows), tuple(which)

    def inputs(self):
        return self.fulls

    def out_shapes(self):
        return [jax.ShapeDtypeStruct((len(self.which),) + _part_shape(a, h), a.dtype) for a, h in zip(self.fulls, self.hows)]

    def n_sems(self):
        return 6 * len(self.fulls)

    def build(self, ins, outs, sems, base):
        x, y, c = _place()
        chips = _other_chips(x, y)
        starts, waits = [], []
        for a, how in enumerate(self.hows):
            for slot, j in enumerate(self.which):
                px, py = chips[j]
                cp = _remote(_window(ins[a], how, 2 * px + py), outs[a].at[slot], sems, base + 6 * a + 2 * j, (px, py, c))
                starts.append(cp.start)
                waits += [cp.wait_recv, cp.wait_send]
        return starts, waits


class _Swap:
    aliased = False

    def __init__(self, arrays):
        self.arrays = list(arrays)

    def inputs(self):
        return self.arrays

    def out_shapes(self):
        return [jax.ShapeDtypeStruct(a.shape, a.dtype) for a in self.arrays]

    def n_sems(self):
        return 2 * len(self.arrays)

    def build(self, ins, outs, sems, base):
        x, y, c = _place()
        starts, waits = [], []
        for a in range(len(ins)):
            cp = _remote(ins[a], outs[a], sems, base + 2 * a, (x, y, 1 - c))
            starts.append(cp.start)
            waits += [cp.wait_recv, cp.wait_send]
        return starts, waits


def _call(name, body, grid, in_specs, out_specs, out_shape, args, scratch=(), comm=(), after=()):
    comm, after = list(comm), list(after)
    n_in, n_out, n_scr, n_after = len(args), len(out_shape), len(scratch), len(after)
    c_in = [a for op in comm for a in op.inputs()]
    c_out = [s for op in comm for s in op.out_shapes()]
    n_sems = sum(op.n_sems() for op in comm)
    aliases, i_in, i_out = {}, 0, 0
    for op in comm:
        if op.aliased:
            for q in range(len(op.inputs())):
                aliases[n_in + n_after + i_in + q] = n_out + i_out + q
        i_in, i_out = i_in + len(op.inputs()), i_out + len(op.out_shapes())

    def wrapped(*refs):
        ins = refs[:n_in]
        cin = refs[n_in + n_after : n_in + n_after + len(c_in)]
        o0 = n_in + n_after + len(c_in)
        outs = refs[o0 : o0 + n_out]
        cout = refs[o0 + n_out : o0 + n_out + len(c_out)]
        s0 = o0 + n_out + len(c_out)
        scr = refs[s0 : s0 + n_scr]

        def copies():
            sems = refs[s0 + n_scr]
            starts, waits = [], []
            i_in = i_out = base = 0
            for op in comm:
                ni, no = len(op.inputs()), len(op.out_shapes())
                s, w = op.build(cin[i_in : i_in + ni], cout[i_out : i_out + no], sems, base)
                starts += s
                waits += w
                i_in, i_out, base = i_in + ni, i_out + no, base + op.n_sems()
            return starts, waits

        def run_starts():
            for start in copies()[0]:
                start()

        def run_waits():
            for wait in copies()[1]:
                wait()

        if comm and grid:
            first = last = True
            for d, n in enumerate(grid):
                first = jnp.logical_and(first, pl.program_id(d) == 0)
                last = jnp.logical_and(last, pl.program_id(d) == n - 1)
            pl.when(first)(run_starts)
        elif comm:
            run_starts()
        if body is not None:
            body(*ins, *outs, *scr)
        if comm and grid:
            pl.when(last)(run_waits)
        elif comm:
            run_waits()

    res = pl.pallas_call(
        wrapped,
        name=name,
        grid=grid,
        in_specs=list(in_specs) + [ANY] * (n_after + len(c_in)),
        out_specs=list(out_specs) + [ANY] * len(c_out),
        out_shape=list(out_shape) + c_out,
        scratch_shapes=list(scratch) + ([pltpu.SemaphoreType.DMA((n_sems,))] if comm else []),
        input_output_aliases=aliases,
        compiler_params=pltpu.CompilerParams(dimension_semantics=("arbitrary",) * len(grid), vmem_limit_bytes=VMEM_LIMIT),
    )(*args, *after, *c_in)
    return tuple(res[:n_out]), tuple(res[n_out:])


def _place_and_gather(now, later):
    items = list(now) + list(later)
    n, n_now = len(items), len(now)
    buf_shape = lambda it: it[0].shape[::-1] if it[4] else it[0].shape
    split_now = [a for a in range(n_now) if items[a][5]]

    def body(*refs):
        ins, outs = refs[:n], refs[n : 2 * n]
        stage, bufs = refs[2 * n : 3 * n - n_now], refs[3 * n - n_now : 4 * n - n_now]
        sems = refs[4 * n - n_now]
        x, y, c = _place()
        me = 2 * x + y
        chips = _other_chips(x, y)
        loads = [pltpu.make_async_copy(ins[a], stage[a - n_now], sems.at[a]) for a in range(n_now, n)]
        for ld in loads:
            ld.start()
        pending = []

        def place(a, val):
            _, how, _, dtype, transposed, _ = items[a]
            bufs[a][...] = (val.T if transposed else val).astype(dtype)
            cp = pltpu.make_async_copy(bufs[a], _window(outs[a], how, me), sems.at[n + a])
            cp.start()
            pending.append(cp.wait)

        arrivals = []
        for a in range(n_now):
            place(a, ins[a][...])
            how, split = items[a][1], items[a][5]
            half = c if split else None
            src = _rows(bufs[a], c * (bufs[a].shape[0] // 2), bufs[a].shape[0] // 2) if split else bufs[a]
            for j, (px, py) in enumerate(chips):
                s = 2 * n + 6 * a + 2 * j
                cp = _remote(src, _window(outs[a], how, me, half), sems, s, (px, py, c))
                landing = _remote(src, _window(outs[a], how, 2 * px + py, half), sems, s, (px, py, c))
                cp.start()
                arrivals.append(landing.wait_recv)
                pending.append(cp.wait_send)
        for a in range(n_now, n):
            loads[a - n_now].wait()
            place(a, stage[a - n_now][...])
        for wait in arrivals:
            wait()
        d2d = _GatherD2d([None] * len(split_now), [items[a][1] for a in split_now])
        starts, waits = d2d.build(None, [outs[a] for a in split_now], sems, 2 * n + 6 * n_now)
        for start in starts:
            start()
        for wait in waits + pending:
            wait()

    vm = pl.BlockSpec(memory_space=pltpu.VMEM)
    return pl.pallas_call(
        body,
        name="place_and_gather",
        in_specs=[vm] * n_now + [ANY] * (n - n_now),
        out_specs=[ANY] * n,
        out_shape=[jax.ShapeDtypeStruct(it[2], it[3]) for it in items],
        scratch_shapes=[pltpu.VMEM(it[0].shape, it[0].dtype) for it in later]
        + [pltpu.VMEM(buf_shape(it), it[3]) for it in items]
        + [pltpu.SemaphoreType.DMA((2 * n + 6 * n_now + 6 * len(split_now),))],
        compiler_params=pltpu.CompilerParams(vmem_limit_bytes=VMEM_LIMIT),
    )(*[it[0] for it in items])


_HBM = pl.BlockSpec(memory_space=pltpu.HBM)
_SEM = pl.BlockSpec(memory_space=pltpu.SEMAPHORE)
_DATAFLOW = pltpu.SideEffectType.DATAFLOW_SIDE_EFFECTING


class _Pending:
    def __init__(self, ops, bases, sems, arrays, token):
        self.ops, self.bases, self.sems, self.arrays, self.token = ops, bases, sems, arrays, token


def _op_refs(op, refs):
    n_src = len(op.inputs())
    return refs[:n_src], (refs[:n_src] if op.aliased else refs[n_src:])


def _start(name, ops, sibling_only=False):
    per_op = [list(op.inputs()) + ([] if op.aliased else [lax.empty(sd.shape, sd.dtype) for sd in op.out_shapes()])
              for op in ops]
    arrays = [a for group in per_op for a in group]
    bases = [sum(op.n_sems() for op in ops[:k]) for k in range(len(ops))]
    n = len(arrays)

    def body(*refs):
        sems, token = refs[n], refs[-1]
        if sibling_only:
            x, y, c = _place()
            barrier = pltpu.get_barrier_semaphore()
            pl.semaphore_signal(barrier, inc=1, device_id=(x, y, 1 - c), device_id_type=MESH)
            pl.semaphore_wait(barrier, 1)
        at = 0
        for op, group, base in zip(ops, per_op, bases):
            starts, _ = op.build(*_op_refs(op, refs[at : at + len(group)]), sems, base)
            for start in starts:
                start()
            at += len(group)
        token[...] = jnp.zeros_like(token)

    res = pl.pallas_call(
        body,
        name=name,
        out_shape=(pltpu.SemaphoreType.DMA((sum(op.n_sems() for op in ops),)),)
        + tuple(pltpu.HBM(a.shape, a.dtype) for a in arrays) + (jax.ShapeDtypeStruct((SUBLANES, LANES), F32),),
        in_specs=(_HBM,) * n,
        out_specs=(_SEM,) + (_HBM,) * n + (pl.BlockSpec(memory_space=pltpu.VMEM),),
        input_output_aliases={i: 1 + i for i in range(n)},
        compiler_params=pltpu.CompilerParams(
            has_side_effects=_DATAFLOW, collective_id=SIBLING_BARRIER_ID if sibling_only else None),
    )(*[pltpu.with_memory_space_constraint(a, pltpu.HBM) for a in arrays])
    thru, at, groups = list(res[1 : 1 + n]), 0, []
    for group in per_op:
        groups.append(thru[at : at + len(group)])
        at += len(group)
    return _Pending(list(ops), bases, res[0], groups, res[-1])


def _wait(name, pending, k, after):
    op, arrays = pending.ops[k], pending.arrays[k]
    n = len(arrays)

    def body(*refs):
        _, waits = op.build(*_op_refs(op, refs[:n]), refs[n], pending.bases[k])
        for wait in waits:
            wait()

    return pl.pallas_call(
        body,
        name=name,
        out_shape=tuple(pltpu.HBM(a.shape, a.dtype) for a in arrays),
        in_specs=(_HBM,) * n + (_SEM, ANY),
        out_specs=(_HBM,) * n,
        input_output_aliases={i: i for i in range(n)},
        compiler_params=pltpu.CompilerParams(has_side_effects=_DATAFLOW),
    )(*arrays, pending.sems, after)


def _in_proj(x, g_mix, w_inT_b, b_in, after=()):
    T, D = x.shape
    CI = w_inT_b.shape[0]
    tm = _tile(T, 512)

    def body(x_ref, g_ref, w_ref, b_ref, z_ref, xn_ref):
        xv = x_ref[...]
        r = lax.rsqrt(jnp.mean(xv * xv, axis=-1, keepdims=True) + RMS_EPS)
        xn = (xv * r * g_ref[...]).astype(BF16)
        xn_ref[...] = xn
        z_ref[...] = _dot(xn, w_ref[...], NT) + b_ref[...]

    return _call(
        "in_proj",
        body,
        (T // tm,),
        [
            pl.BlockSpec((tm, D), lambda i: (i, 0)),
            pl.BlockSpec((1, D), lambda i: (0, 0)),
            pl.BlockSpec((CI, D), lambda i: (0, 0)),
            pl.BlockSpec((1, CI), lambda i: (0, 0)),
        ],
        [pl.BlockSpec((tm, CI), lambda i: (i, 0)), pl.BlockSpec((tm, D), lambda i: (i, 0))],
        [jax.ShapeDtypeStruct((T, CI), F32), jax.ShapeDtypeStruct((T, D), BF16)],
        (x, g_mix, w_inT_b, b_in),
        after=after,
    )


def _fill_shifted(scr):
    n = scr.shape[1] - SUBLANES
    for s in range(1, SUBLANES):
        scr[s, 0:n, :] = scr[0, s : s + n, :]


def _shifted_rows(scr, off, n, cs):
    s = off % SUBLANES
    return scr[s, off - s : off - s + n, cs]


def _pool_mean_minus_token(p_scr, cs, w, cnt, tt):
    tok = p_scr[HALO : HALO + tt, cs]
    s = tok
    for d in range(1, w):
        s = s + p_scr[HALO - d : HALO - d + tt, cs]
    return s / cnt - tok


def _seq_fwd(z, w_dw4, b_dw, ln_g, ln_b, w_pool_b, s_pool, after=()):
    T, CI = z.shape
    CC = ln_g.shape[1]
    n_grp, G = w_pool_b.shape[0], w_pool_b.shape[-1]
    KW = w_dw4.shape[1]
    D = CC + n_grp * G
    tt = _tile(T, 512, HALO)
    per = tt // HALO

    def body(zc_ref, zp_ref, wdw_ref, bdw_ref, lng_ref, lnb_ref, wp_ref, sp_ref, y_ref, v_ref, u_scr, p_scr):
        i = pl.program_id(0)
        first = i == 0
        u_prev = zp_ref[:, 0:CC] * _sigmoid(zp_ref[:, CC : 2 * CC])
        u_scr[0, 0:HALO, :] = jnp.where(first, 0.0, u_prev)
        p_scr[0:HALO, :] = jnp.where(first, 0.0, zp_ref[:, 2 * CC :])
        u_scr[0, HALO:, :] = zc_ref[:, 0:CC] * _sigmoid(zc_ref[:, CC : 2 * CC])
        p_scr[HALO:, :] = zc_ref[:, 2 * CC :]
        _fill_shifted(u_scr)

        for j in range(CC // LANES):
            cs = slice(LANES * j, LANES * (j + 1))
            for rb in range(tt // CONV_ROWS):
                acc = jnp.zeros((CONV_ROWS, LANES), F32)
                for k in range(KW):
                    off = HALO - (KW - 1) + k + rb * CONV_ROWS
                    acc = acc + _shifted_rows(u_scr, off, CONV_ROWS, cs) * wdw_ref[j, k]
                v_ref[rb * CONV_ROWS : (rb + 1) * CONV_ROWS, cs] = acc + bdw_ref[:, cs]

        v = v_ref[...]
        mu = jnp.mean(v, axis=-1, keepdims=True)
        d = v - mu
        var = jnp.mean(d * d, axis=-1, keepdims=True)
        ln = d * lax.rsqrt(var + LN_EPS) * lng_ref[...] + lnb_ref[...]
        y_ref[:, 0:CC] = (ln * _sigmoid(ln)).astype(BF16)

        tpos = i * tt + lax.broadcasted_iota(jnp.int32, (tt, 1), 0)
        for gi, w in enumerate(POOL_WINDOWS):
            cs = slice(G * gi, G * (gi + 1))
            cnt = jnp.minimum(tpos + 1, w).astype(F32)
            yi = _pool_mean_minus_token(p_scr, cs, w, cnt, tt)
            q = _dot(yi.astype(BF16), wp_ref[gi], NN)
            y_ref[:, CC + G * gi : CC + G * (gi + 1)] = (q * sp_ref[:, cs]).astype(BF16)

    const2 = lambda i: (0, 0)
    return _call(
        "seq_fwd",
        body,
        (T // tt,),
        [
            pl.BlockSpec((tt, CI), lambda i: (i, 0)),
            pl.BlockSpec((HALO, CI), lambda i: (jnp.maximum(i * per - 1, 0), 0)),
            pl.BlockSpec(w_dw4.shape, lambda i: (0,) * w_dw4.ndim),
            pl.BlockSpec((1, CC), const2),
            pl.BlockSpec((1, CC), const2),
            pl.BlockSpec((1, CC), const2),
            pl.BlockSpec(w_pool_b.shape, lambda i: (0, 0, 0)),
            pl.BlockSpec((1, n_grp * G), const2),
        ],
        [pl.BlockSpec((tt, D), lambda i: (i, 0)), pl.BlockSpec((tt, CC), lambda i: (i, 0))],
        [jax.ShapeDtypeStruct((T, D), BF16), jax.ShapeDtypeStruct((T, CC), F32)],
        (z, z, w_dw4, b_dw, ln_g, ln_b, w_pool_b, s_pool),
        scratch=[pltpu.VMEM((SUBLANES, HALO + tt, CC), F32), pltpu.VMEM((HALO + tt, n_grp * G), F32)],
        after=after,
    )


def _out_proj(y_b, x, w_out_b, g_ffn, after=()):
    T, D = x.shape
    tm = _tile(T, 512)

    def body(y_ref, x_ref, w_ref, g_ref, h1_ref, hn_ref):
        h1 = x_ref[...] + _dot(y_ref[...], w_ref[...], NN)
        h1_ref[...] = h1
        r = lax.rsqrt(jnp.mean(h1 * h1, axis=-1, keepdims=True) + RMS_EPS)
        hn_ref[...] = (h1 * r * g_ref[...]).astype(BF16)

    row = lambda i: (i, 0)
    return _call(
        "out_proj",
        body,
        (T // tm,),
        [
            pl.BlockSpec((tm, y_b.shape[1]), row),
            pl.BlockSpec((tm, D), row),
            pl.BlockSpec(w_out_b.shape, lambda i: (0, 0)),
            pl.BlockSpec((1, D), lambda i: (0, 0)),
        ],
        [pl.BlockSpec((tm, D), row), pl.BlockSpec((tm, D), row)],
        [jax.ShapeDtypeStruct((T, D), F32), jax.ShapeDtypeStruct((T, D), BF16)],
        (y_b, x, w_out_b, g_ffn),
        after=after,
    )


def _hidden_tile(F):
    return _tile(F, 1408, LANES)


def _gate_up(hn_b, wgT_b, wuT_b):
    T, D = hn_b.shape
    F = wgT_b.shape[0]
    tm, tf = _tile(T, 1024), _hidden_tile(F)

    def body(hn_ref, wg_ref, wu_ref, silu_ref, uds_ref, a_ref):
        hn = hn_ref[...]
        for c0 in range(0, tf, HIDDEN_CHUNK):
            cs = slice(c0, min(c0 + HIDDEN_CHUNK, tf))
            gv = _dot(hn, wg_ref[cs, :], NT)
            uv = _dot(hn, wu_ref[cs, :], NT)
            sg = _sigmoid(gv)
            silu = gv * sg
            silu_ref[:, cs] = silu.astype(BF16)
            uds_ref[:, cs] = (uv * (sg * (1.0 + gv * (1.0 - sg)))).astype(BF16)
            a_ref[:, cs] = (silu * uv).astype(BF16)

    wspec = pl.BlockSpec((tf, D), lambda j, i: (j, 0))
    ospec = pl.BlockSpec((tm, tf), lambda j, i: (i, j))
    return _call(
        "gate_up",
        body,
        (F // tf, T // tm),
        [pl.BlockSpec((tm, D), lambda j, i: (i, 0)), wspec, wspec],
        [ospec, ospec, ospec],
        [jax.ShapeDtypeStruct((T, F), BF16)] * 3,
        (hn_b, wgT_b, wuT_b),
    )


def _down_loss(a_b, wd_b, h1, target, g_final):
    T, D = h1.shape
    F = a_b.shape[1]
    tm = _tile(T, 512)
    nt = T // tm

    def body(a_ref, w_ref, h1_ref, t_ref, g_ref, dh2_ref, dh2b_ref, loss_ref, dg_ref):
        i = pl.program_id(0)
        h2 = h1_ref[...] + _dot(a_ref[...], w_ref[...], NN)
        r = lax.rsqrt(jnp.mean(h2 * h2, axis=-1, keepdims=True) + RMS_EPS)
        g = g_ref[...]
        diff = h2 * r * g - t_ref[...]
        _accumulate(loss_ref, i == 0, jnp.full(loss_ref.shape, jnp.sum(diff * diff) * (0.5 / D), F32))
        dh2, dg_rows = _rms_bwd(h2, g, diff * (1.0 / D))
        dh2_ref[...] = dh2
        dh2b_ref[...] = dh2.astype(BF16)
        _accumulate(dg_ref, i == 0, jnp.sum(dg_rows, axis=0, keepdims=True))

    row = lambda i: (i, 0)
    return _call(
        "down_loss",
        body,
        (nt,),
        [
            pl.BlockSpec((tm, F), row),
            pl.BlockSpec((F, D), lambda i: (0, 0), pipeline_mode=pl.Buffered(1)),
            pl.BlockSpec((tm, D), row),
            pl.BlockSpec((tm, D), row),
            pl.BlockSpec((1, D), lambda i: (0, 0)),
        ],
        [
            pl.BlockSpec((tm, D), row),
            pl.BlockSpec((tm, D), row),
            pl.BlockSpec((1, LANES), lambda i: (0, 0)),
            pl.BlockSpec((1, D), lambda i: (0, 0)),
        ],
        [
            jax.ShapeDtypeStruct((T, D), F32),
            jax.ShapeDtypeStruct((T, D), BF16),
            jax.ShapeDtypeStruct((1, LANES), F32),
            jax.ShapeDtypeStruct((1, D), F32),
        ],
        (a_b, wd_b, h1, target, g_final),
    )


def _ffn_bwd_act(dh2_b, wd_b, silu_b, uds_b, after=()):
    T, D = dh2_b.shape
    F = wd_b.shape[0]
    tm, tf = _tile(T, 1024), _hidden_tile(F)

    def body(d_ref, w_ref, silu_ref, uds_ref, dg_ref, du_ref):
        d = d_ref[...]
        for c0 in range(0, tf, HIDDEN_CHUNK):
            cs = slice(c0, min(c0 + HIDDEN_CHUNK, tf))
            da = _dot(d, w_ref[cs, :], NT)
            dg_ref[:, cs] = (da * uds_ref[:, cs].astype(F32)).astype(BF16)
            du_ref[:, cs] = (da * silu_ref[:, cs].astype(F32)).astype(BF16)

    aspec = pl.BlockSpec((tm, tf), lambda j, i: (i, j))
    return _call(
        "ffn_bwd_act",
        body,
        (F // tf, T // tm),
        [pl.BlockSpec((tm, D), lambda j, i: (i, 0)), pl.BlockSpec((tf, D), lambda j, i: (j, 0)), aspec, aspec],
        [aspec, aspec],
        [jax.ShapeDtypeStruct((T, F), BF16)] * 2,
        (dh2_b, wd_b, silu_b, uds_b),
        after=after,
    )


def _ffn_bwd_in(dg_b, du_b, wgT_b, wuT_b, h1, dh2, g_ffn, w_out_b, comm=()):
    T, D = h1.shape
    F = wgT_b.shape[0]
    DM = w_out_b.shape[0]
    tm = _tile(T, 512)

    def body(dg_ref, du_ref, wg_ref, wu_ref, h1_ref, dh2_ref, g_ref, wo_ref, dh1_ref, dh1b_ref, dy_ref, dgf_ref):
        i = pl.program_id(0)
        dhn = _dot(dg_ref[...], wg_ref[...], NN) + _dot(du_ref[...], wu_ref[...], NN)
        dx, dg_rows = _rms_bwd(h1_ref[...], g_ref[...], dhn)
        dh1 = dh2_ref[...] + dx
        dh1b = dh1.astype(BF16)
        dh1_ref[...] = dh1
        dh1b_ref[...] = dh1b
        dy_ref[...] = _dot(dh1b, wo_ref[...], NT)
        _accumulate(dgf_ref, i == 0, jnp.sum(dg_rows, axis=0, keepdims=True))

    row = lambda i: (i, 0)
    const = lambda i: (0, 0)
    return _call(
        "ffn_bwd_in",
        body,
        (T // tm,),
        [
            pl.BlockSpec((tm, F), row),
            pl.BlockSpec((tm, F), row),
            pl.BlockSpec((F, D), const, pipeline_mode=pl.Buffered(1)),
            pl.BlockSpec((F, D), const, pipeline_mode=pl.Buffered(1)),
            pl.BlockSpec((tm, D), row),
            pl.BlockSpec((tm, D), row),
            pl.BlockSpec((1, D), const),
            pl.BlockSpec((DM, D), const, pipeline_mode=pl.Buffered(1)),
        ],
        [pl.BlockSpec((tm, D), row), pl.BlockSpec((tm, D), row), pl.BlockSpec((tm, DM), row), pl.BlockSpec((1, D), const)],
        [
            jax.ShapeDtypeStruct((T, D), F32),
            jax.ShapeDtypeStruct((T, D), BF16),
            jax.ShapeDtypeStruct((T, DM), F32),
            jax.ShapeDtypeStruct((1, D), F32),
        ],
        (dg_b, du_b, wgT_b, wuT_b, h1, dh2, g_ffn, w_out_b),
        comm=comm,
    )


def _seq_bwd(z, dy, v, w_dw4, ln_g, ln_b, w_pool_b, s_pool, comm=()):
    T, CI = z.shape
    CC = ln_g.shape[1]
    n_grp, G = w_pool_b.shape[0], w_pool_b.shape[-1]
    CP = n_grp * G
    KW = w_dw4.shape[1]
    n_cc = CC // LANES
    D = CC + CP
    tt = _tile(T, 512, HALO)
    per = tt // HALO
    n_tiles = T // tt
    last_halo = T // HALO - 1

    def body(zc_ref, zp_ref, dyc_ref, dyn_ref, vc_ref, vn_ref, wdw_ref, lng_ref, lnb_ref, wp_ref, sp_ref,
             dz_ref, dwdw_ref, dbdw_ref, dlng_ref, dlnb_ref, dwp_ref, dsp_ref, dbin_ref,
             dv_scr, u_scr, p_scr, g_scr, dw_scr):
        i = pl.program_id(0)
        first = i == 0
        last = i == n_tiles - 1
        lng, lnb = lng_ref[...], lnb_ref[...]

        def conv_pre(vv, dyc):
            mu = jnp.mean(vv, axis=-1, keepdims=True)
            d = vv - mu
            rs = lax.rsqrt(jnp.mean(d * d, axis=-1, keepdims=True) + LN_EPS)
            xh = d * rs
            ln = xh * lng + lnb
            sg = _sigmoid(ln)
            dln = dyc * (sg * (1.0 + ln * (1.0 - sg)))
            dxh = dln * lng
            dv = rs * (dxh - jnp.mean(dxh, axis=-1, keepdims=True) - xh * jnp.mean(dxh * xh, axis=-1, keepdims=True))
            return dv, dln, xh

        dv_c, dln_c, xh_c = conv_pre(vc_ref[...], dyc_ref[:, 0:CC])
        dv_scr[0, 0:tt, :] = dv_c
        dv_n, _, _ = conv_pre(vn_ref[...], dyn_ref[:, 0:CC])
        dv_scr[0, tt:, :] = jnp.where(last, 0.0, dv_n)
        _fill_shifted(dv_scr)
        _accumulate(dlng_ref, first, jnp.sum(dln_c * xh_c, axis=0, keepdims=True))
        _accumulate(dlnb_ref, first, jnp.sum(dln_c, axis=0, keepdims=True))
        _accumulate(dbdw_ref, first, jnp.sum(dv_c, axis=0, keepdims=True))

        u_scr[...] = zc_ref[:, 0:CC] * _sigmoid(zc_ref[:, CC : 2 * CC])

        @pl.when(first)
        def _():
            dw_scr[...] = jnp.zeros_like(dw_scr)

        for j in range(n_cc):
            cs = slice(LANES * j, LANES * (j + 1))
            gs = slice(CC + LANES * j, CC + LANES * (j + 1))
            dbin_a = jnp.zeros((1, LANES), F32)
            dbin_g = jnp.zeros((1, LANES), F32)
            for rb in range(tt // CONV_ROWS):
                rows = slice(rb * CONV_ROWS, (rb + 1) * CONV_ROWS)
                u_blk = u_scr[rows, cs]
                du = jnp.zeros((CONV_ROWS, LANES), F32)
                for k in range(KW):
                    off = rb * CONV_ROWS + (KW - 1) - k
                    d = _shifted_rows(dv_scr, off, CONV_ROWS, cs)
                    du = du + d * wdw_ref[j, k]
                    dw_scr[j * HALO + k] += jnp.sum((u_blk * d).reshape(CONV_ROWS // 8, 8, LANES), axis=0)
                a = zc_ref[rows, cs]
                sg = _sigmoid(zc_ref[rows, gs])
                da = du * sg
                dgate = du * a * sg * (1.0 - sg)
                dz_ref[rows, cs] = da.astype(BF16)
                dz_ref[rows, gs] = dgate.astype(BF16)
                dbin_a = dbin_a + jnp.sum(da, axis=0, keepdims=True)
                dbin_g = dbin_g + jnp.sum(dgate, axis=0, keepdims=True)
            _accumulate(dbin_ref.at[:, cs], first, dbin_a)
            _accumulate(dbin_ref.at[:, gs], first, dbin_g)

        @pl.when(last)
        def _():
            dwdw_ref[...] = jnp.sum(dw_scr[...], axis=1).reshape(dwdw_ref.shape)

        p_scr[0:HALO, :] = jnp.where(first, 0.0, zp_ref[:, 2 * CC :])
        p_scr[HALO:, :] = zc_ref[:, 2 * CC :]
        tpos = i * tt + lax.broadcasted_iota(jnp.int32, (tt, 1), 0)
        for gi, w in enumerate(POOL_WINDOWS):
            cs = slice(G * gi, G * (gi + 1))
            ys = slice(CC + G * gi, CC + G * (gi + 1))
            ps = slice(2 * CC + G * gi, 2 * CC + G * (gi + 1))
            cnt = jnp.minimum(tpos + 1, w).astype(F32)
            yib = _pool_mean_minus_token(p_scr, cs, w, cnt, tt).astype(BF16)
            wp = wp_ref[gi]
            sp = sp_ref[:, cs]
            dyp = dyc_ref[:, ys]
            q = _dot(yib, wp, NN)
            _accumulate(dsp_ref.at[:, cs], first, jnp.sum(dyp * q, axis=0, keepdims=True))
            dq_c = (dyp * sp).astype(BF16)
            dq_n = (jnp.where(last, 0.0, dyn_ref[:, ys]) * sp).astype(BF16)
            _accumulate(dwp_ref.at[gi], first, _dot(yib, dq_c, TN))
            dyi_c = _dot(dq_c, wp, NT)
            g_scr[0:tt, cs] = dyi_c / cnt
            g_scr[tt:, cs] = _dot(dq_n, wp, NT) * (1.0 / w)
            dp = -dyi_c
            for d in range(w):
                dp = dp + g_scr[d : d + tt, cs]
            dz_ref[:, ps] = dp.astype(BF16)
            _accumulate(dbin_ref.at[:, ps], first, jnp.sum(dp, axis=0, keepdims=True))

    cur = lambda i: (i, 0)
    prev = lambda i: (jnp.maximum(i * per - 1, 0), 0)
    nxt = lambda i: (jnp.minimum((i + 1) * per, last_halo), 0)
    c2 = lambda i: (0, 0)
    c3 = lambda i: (0, 0, 0)
    return _call(
        "seq_bwd",
        body,
        (n_tiles,),
        [
            pl.BlockSpec((tt, CI), cur),
            pl.BlockSpec((HALO, CI), prev),
            pl.BlockSpec((tt, D), cur),
            pl.BlockSpec((HALO, D), nxt),
            pl.BlockSpec((tt, CC), cur),
            pl.BlockSpec((HALO, CC), nxt),
            pl.BlockSpec(w_dw4.shape, lambda i: (0,) * w_dw4.ndim),
            pl.BlockSpec((1, CC), c2),
            pl.BlockSpec((1, CC), c2),
            pl.BlockSpec(w_pool_b.shape, c3),
            pl.BlockSpec((1, CP), c2),
        ],
        [
            pl.BlockSpec((tt, CI), cur),
            pl.BlockSpec((n_cc, HALO, LANES), c3),
            pl.BlockSpec((1, CC), c2),
            pl.BlockSpec((1, CC), c2),
            pl.BlockSpec((1, CC), c2),
            pl.BlockSpec((n_grp, G, G), c3),
            pl.BlockSpec((1, CP), c2),
            pl.BlockSpec((1, CI), c2),
        ],
        [
            jax.ShapeDtypeStruct((T, CI), BF16),
            jax.ShapeDtypeStruct((n_cc, HALO, LANES), F32),
            jax.ShapeDtypeStruct((1, CC), F32),
            jax.ShapeDtypeStruct((1, CC), F32),
            jax.ShapeDtypeStruct((1, CC), F32),
            jax.ShapeDtypeStruct((n_grp, G, G), F32),
            jax.ShapeDtypeStruct((1, CP), F32),
            jax.ShapeDtypeStruct((1, CI), F32),
        ],
        (z, z, dy, dy, v, v, w_dw4, ln_g, ln_b, w_pool_b, s_pool),
        scratch=[
            pltpu.VMEM((SUBLANES, tt + HALO, CC), F32),
            pltpu.VMEM((tt, CC), F32),
            pltpu.VMEM((HALO + tt, CP), F32),
            pltpu.VMEM((tt + HALO, CP), F32),
            pltpu.VMEM((n_cc * HALO, 8, LANES), F32),
        ],
        comm=comm,
    )


def _in_proj_bwd(dz_b, w_inT_b, x, dh1, g_mix, after=()):
    T, D = x.shape
    CI = w_inT_b.shape[0]
    tm = _tile(T, 512)

    def body(dz_ref, w_ref, x_ref, dh1_ref, g_ref, dx_ref, dg_ref):
        i = pl.program_id(0)
        dxn = _dot(dz_ref[...], w_ref[...], NN)
        dx, dg_rows = _rms_bwd(x_ref[...], g_ref[...], dxn)
        dx_ref[...] = dh1_ref[...] + dx
        _accumulate(dg_ref, i == 0, jnp.sum(dg_rows, axis=0, keepdims=True))

    row = lambda i: (i, 0)
    const = lambda i: (0, 0)
    return _call(
        "in_proj_bwd",
        body,
        (T // tm,),
        [
            pl.BlockSpec((tm, CI), row),
            pl.BlockSpec((CI, D), const),
            pl.BlockSpec((tm, D), row),
            pl.BlockSpec((tm, D), row),
            pl.BlockSpec((1, D), const),
        ],
        [pl.BlockSpec((tm, D), row), pl.BlockSpec((1, D), const)],
        [jax.ShapeDtypeStruct((T, D), F32), jax.ShapeDtypeStruct((1, D), F32)],
        (dz_b, w_inT_b, x, dh1, g_mix),
        after=after,
    )


def _weight_grad(name, a_b, b_b, after=()):
    T, N1 = a_b.shape
    N2 = b_b.shape[1]
    t1 = _tile(N1, 1408, LANES)
    tk = _tile(T, 2048)
    nk = T // tk

    def body(a_ref, b_ref, o_ref, acc):
        k = pl.program_id(1)
        _accumulate(acc, k == 0, _dot(a_ref[...], b_ref[...], TN))

        @pl.when(k == nk - 1)
        def _():
            o_ref[...] = acc[...].astype(BF16)

    (out,), _ = _call(
        name,
        body,
        (N1 // t1, nk),
        [pl.BlockSpec((tk, t1), lambda n, k: (k, n)), pl.BlockSpec((tk, N2), lambda n, k: (k, 0))],
        [pl.BlockSpec((t1, N2), lambda n, k: (n, 0))],
        [jax.ShapeDtypeStruct((N1, N2), BF16)],
        (a_b, b_b),
        scratch=[pltpu.VMEM((t1, N2), F32)],
        after=after,
    )
    return out


def _sum_parts(name, full, how, parts, me):
    _, R, C = parts[0].shape
    tr = _tile(R, 512)
    nb = R // tr
    where = [(q, r) for q, p in enumerate(parts) for r in range(p.shape[0])]
    assert len(where) == 3

    def body(me_ref, own_ref, *refs):
        o_ref = refs[-1]
        f = lambda j: refs[where[j][0]][where[j][1]].astype(F32)
        o_ref[...] = (own_ref[...].astype(F32) + f(0)) + (f(1) + f(2))

    own_map = {"rows": lambda i, me_ref: (me_ref[0] * nb + i, 0), "all": lambda i, me_ref: (i, 0)}[how]
    return pl.pallas_call(
        body,
        name=name,
        grid_spec=pltpu.PrefetchScalarGridSpec(
            num_scalar_prefetch=1,
            grid=(nb,),
            in_specs=[pl.BlockSpec((tr, C), own_map)]
            + [pl.BlockSpec((p.shape[0], tr, C), lambda i, me_ref: (0, i, 0)) for p in parts],
            out_specs=pl.BlockSpec((tr, C), lambda i, me_ref: (i, 0)),
        ),
        out_shape=jax.ShapeDtypeStruct((R, C), F32),
        compiler_params=pltpu.CompilerParams(dimension_semantics=("arbitrary",), vmem_limit_bytes=VMEM_LIMIT),
    )(me, full, *parts)


_M_CORR = 1.0 - ADAM_B1**ADAM_STEP
_V_CORR = 1.0 - ADAM_B2**ADAM_STEP


def _adamw_math(w, g, m, v):
    m = ADAM_B1 * m + (1.0 - ADAM_B1) * g
    v = ADAM_B2 * v + (1.0 - ADAM_B2) * (g * g)
    delta = -ADAM_LR * ((m / _M_CORR) / (jnp.sqrt(v / _V_CORR) + ADAM_EPS) + ADAM_WD * w)
    return delta, m, v


def _adamw(name, w, m, v, g_here, g_there, g_transposed=False):
    R, C = w.shape
    tr = _tile(R, 256, LANES if g_transposed else 8)

    def body(w_ref, m_ref, v_ref, ga_ref, gb_ref, g_ref, d_ref, nm_ref, nv_ref):
        g = ga_ref[...] + gb_ref[...]
        if g_transposed:
            g = g.T
        g_ref[...] = g
        d_ref[...], nm_ref[...], nv_ref[...] = _adamw_math(w_ref[...], g, m_ref[...], v_ref[...])

    spec = pl.BlockSpec((tr, C), lambda i: (i, 0))
    gspec = pl.BlockSpec((C, tr), lambda i: (0, i)) if g_transposed else spec
    return _call(name, body, (R // tr,), [spec] * 3 + [gspec] * 2, [spec] * 4, [jax.ShapeDtypeStruct((R, C), F32)] * 4,
                 (w, m, v, g_here, g_there))


def _adamw_on_sparsecore(name, w, m, v, g_here, g_there, after):
    R, C = w.shape
    n_groups = R // SUBLANES
    n_turns = -(-n_groups // SC_TILES)
    n_in, n_out = 5, 4

    def body(w_hbm, m_hbm, v_hbm, ga_hbm, gb_hbm, after_hbm, g_out, d_out, nm_out, nv_out, bufs, sems):
        tile = lax.axis_index("subcore") * SC_CORES + lax.axis_index("sparsecore")
        srcs = (w_hbm, m_hbm, v_hbm, ga_hbm, gb_hbm)
        dsts = (d_out, nm_out, nv_out, g_out)

        def rows(turn):
            return pl.ds((tile + turn * SC_TILES) * SUBLANES, SUBLANES)

        def loads(turn):
            slot = turn % 2
            return [pltpu.make_async_copy(srcs[q].at[rows(turn), :], bufs.at[slot, q], sems.at[slot, q]) for q in range(n_in)]

        def stores(turn):
            slot = turn % 2
            return [pltpu.make_async_copy(bufs.at[slot, q], dsts[q].at[rows(turn), :], sems.at[slot, n_in + q])
                    for q in range(n_out)]

        def when_mine(turn, fn):
            pl.when(tile + turn * SC_TILES < n_groups)(fn)

        def compute(slot):
            wb, mb, vb, gab, gbb = (bufs.at[slot, q] for q in range(n_in))

            @pl.loop(0, SUBLANES)
            def _(r):
                @pl.loop(0, C, step=SC_LANES)
                def _(i):
                    at = (r, pl.ds(i, SC_LANES))
                    g = gab[at] + gbb[at]
                    delta, new_m, new_v = _adamw_math(wb[at], g, mb[at], vb[at])
                    gab[at], wb[at], mb[at], vb[at] = g, delta, new_m, new_v

        def start_loads(turn):
            def fn():
                for cp in loads(turn):
                    cp.start()

            when_mine(turn, fn)

        start_loads(0)
        for turn in range(n_turns):
            def step(turn=turn):
                for cp in loads(turn):
                    cp.wait()
                if turn >= 1:
                    for cp in stores(turn - 1):
                        cp.wait()
                if turn + 1 < n_turns:
                    start_loads(turn + 1)
                compute(turn % 2)
                for cp in stores(turn):
                    cp.start()

            when_mine(turn, step)
        for turn in range(n_turns):
            def drain(turn=turn):
                for cp in stores(turn):
                    cp.wait()

            last_mine = jnp.logical_and(tile + turn * SC_TILES < n_groups, tile + (turn + 1) * SC_TILES >= n_groups)
            pl.when(last_mine)(drain)

    return pl.kernel(
        body,
        name=name,
        out_type=[jax.ShapeDtypeStruct((R, C), F32)] * 4,
        mesh=plsc.VectorSubcoreMesh(core_axis_name="sparsecore", subcore_axis_name="subcore"),
        scratch_types=[pltpu.VMEM((2, n_in, SUBLANES, C), F32), pltpu.SemaphoreType.DMA((2, n_in + n_out))],
        compiler_params=pltpu.CompilerParams(use_tc_tiling_on_sc=True),
    )(w, m, v, g_here, g_there, after)


class _PackLayout:
    def __init__(self, n_cc, n_grp, G, widths):
        self.dw_rows = (0, HALO)
        self.wp_rows = (HALO, HALO + G)
        self.n_cc, self.n_grp, self.G = n_cc, n_grp, G
        self.vec = {}
        r = HALO + G
        for name, width in widths:
            self.vec[name] = (r, width)
            r += width // PACK_W
        self.rows = -(-r // 8) * 8


def _pack_small(layout, dwdw, dwp, vecs):
    names = list(vecs)

    def body(*refs):
        dw_ref, wp_ref = refs[0], refs[1]
        vec_refs = refs[2 : 2 + len(names)]
        o_ref = refs[-1]
        o_ref[...] = jnp.zeros_like(o_ref)
        for j in range(layout.n_cc):
            o_ref[layout.dw_rows[0] : layout.dw_rows[1], j * LANES : (j + 1) * LANES] = dw_ref[j]
        for i in range(layout.n_grp):
            o_ref[layout.wp_rows[0] : layout.wp_rows[1], i * layout.G : (i + 1) * layout.G] = wp_ref[i]
        for name, ref in zip(names, vec_refs):
            r, width = layout.vec[name]
            for h in range(width // PACK_W):
                o_ref[r + h : r + h + 1, :] = ref[:, h * PACK_W : (h + 1) * PACK_W]

    return pl.pallas_call(
        body,
        name="pack_small",
        out_shape=jax.ShapeDtypeStruct((layout.rows, PACK_W), F32),
    )(dwdw, dwp, *[vecs[k] for k in names])


def _adamw_small(layout, g_here, g_there, w_dw, m_dw, v_dw, w_pool, m_pool, v_pool, vec_w, vec_m, vec_v):
    names = list(vec_w)
    nv = len(names)

    def body(*refs):
        ga_ref, gb_ref = refs[0], refs[1]
        wdw, mdw, vdw, wp, mp, vp = refs[2:8]
        vw, vm, vv = refs[8 : 8 + nv], refs[8 + nv : 8 + 2 * nv], refs[8 + 2 * nv : 8 + 3 * nv]
        outs = refs[8 + 3 * nv :]
        acc = outs[-1]
        acc[...] = ga_ref[...] + gb_ref[...]

        def emit(o, g, w, m, v, idx=()):
            res = (g,) + _adamw_math(w, g, m, v)
            for ref, val in zip(o, res):
                ref[idx] = val

        me = 2 * lax.axis_index("x") + lax.axis_index("y")
        for j in range(layout.n_cc):

            @pl.when(me == j)
            def _(j=j):
                for k in range(wdw.shape[0]):
                    g = acc[layout.dw_rows[0] + k : layout.dw_rows[0] + k + 1, j * LANES : (j + 1) * LANES]
                    emit(outs[0:4], g, wdw[k], mdw[k], vdw[k], idx=k)

        for i in range(layout.n_grp):
            g = acc[layout.wp_rows[0] : layout.wp_rows[1], i * layout.G : (i + 1) * layout.G]
            emit(outs[4:8], g, wp[i], mp[i], vp[i], idx=i)
        for q, name in enumerate(names):
            r, width = layout.vec[name]
            for h in range(width // PACK_W):
                ls = slice(h * PACK_W, (h + 1) * PACK_W)
                g = acc[r + h : r + h + 1, :]
                emit(outs[8 + 4 * q : 12 + 4 * q], g, vw[q][:, ls], vm[q][:, ls], vv[q][:, ls], idx=(slice(None), ls))

    shapes = [w_dw.shape] * 4 + [w_pool.shape] * 4
    for name in names:
        shapes += [vec_w[name].shape] * 4
    return pl.pallas_call(
        body,
        name="adamw_small",
        out_shape=[jax.ShapeDtypeStruct(s, F32) for s in shapes],
        scratch_shapes=[pltpu.VMEM(g_here.shape, F32)],
    )(g_here, g_there, w_dw, m_dw, v_dw, w_pool, m_pool, v_pool,
      *[vec_w[k] for k in names], *[vec_m[k] for k in names], *[vec_v[k] for k in names])


def _allreduce_adamw_row(g_part, w, m, v, loss_part, comm=()):
    D = w.shape[1]
    n_pairs = N_DEV - 1

    def body(g_ref, w_ref, m_ref, v_ref, l_ref, go_ref, d_ref, nm_ref, nv_ref, lo_ref, land_g, land_l, sems):
        x, y, c = _place()
        copies = []
        for q, (src, land) in enumerate(((g_ref, land_g), (l_ref, land_l))):
            for r in range(1, N_DEV):
                fx, fy, fc = (r >> 2) & 1, (r >> 1) & 1, r & 1
                peer = (1 - x if fx else x, 1 - y if fy else y, 1 - c if fc else c)
                cp = _remote(src, land.at[r], sems, 2 * (q * n_pairs + r - 1), peer)
                cp.start()
                copies.append(cp)
        for cp in copies:
            cp.wait()

        def total(src, land):
            row = lambda r: src[...] if r == 0 else land[r]
            return ((row(0) + row(4)) + (row(2) + row(6))) + ((row(1) + row(5)) + (row(3) + row(7)))

        g = total(g_ref, land_g)
        go_ref[...] = g
        d_ref[...], nm_ref[...], nv_ref[...] = _adamw_math(w_ref[...], g, m_ref[...], v_ref[...])
        lo_ref[...] = total(l_ref, land_l)

    vm = pl.BlockSpec(memory_space=pltpu.VMEM)
    return _call(
        "allreduce_adamw_g_mix",
        body,
        (),
        [vm] * 5,
        [vm] * 5,
        [jax.ShapeDtypeStruct((1, D), F32)] * 4 + [jax.ShapeDtypeStruct(loss_part.shape, F32)],
        (g_part, w, m, v, loss_part),
        scratch=[pltpu.VMEM((N_DEV, 1, D), F32), pltpu.VMEM((N_DEV,) + loss_part.shape, F32),
                 pltpu.SemaphoreType.DMA((4 * n_pairs,))],
        comm=comm,
    )


def kernel(x, g_mix, w_in, b_in, w_dw, b_dw, ln_g, ln_b, w_pool, s_pool, w_out, g_ffn, w_gate, w_up, w_down, g_final, loss_target, m_g_mix, m_w_in, m_b_in, m_w_dw, m_b_dw, m_ln_g, m_ln_b, m_w_pool, m_s_pool, m_w_out, m_g_ffn, m_w_gate, m_w_up, m_w_down, m_g_final, v_g_mix, v_w_in, v_b_in, v_w_dw, v_b_dw, v_ln_g, v_ln_b, v_w_pool, v_s_pool, v_w_out, v_g_ffn, v_w_gate, v_w_up, v_w_down, v_g_final):
    x2 = x[0]
    target = loss_target[0]
    T, D = x2.shape
    w_in2, w_out2, w_down2 = w_in[0], w_out[0], w_down[0]
    taps_first = lambda a: jnp.transpose(a, (1, 0, 2))
    w_dw3 = taps_first(w_dw)
    w_gateT, w_upT = w_gate[0].T, w_up[0].T
    CI = w_in2.shape[1] * N_CHIPS
    DM = w_out2.shape[0] * N_CHIPS
    F = w_down2.shape[0] * N_CHIPS
    KW, _, dw_cols = w_dw3.shape
    assert dw_cols == LANES
    n_grp, G = w_pool.shape[1], w_pool.shape[-1]
    w_pool3 = w_pool[0]
    g_final2 = g_final.reshape(1, D)

    me = (2 * lax.axis_index("x") + lax.axis_index("y")).astype(jnp.int32).reshape(1)

    w_inT_b, w_dw4, f_out, f_gate, f_up, f_down = _place_and_gather(
        [(w_in2, "rows", (CI, D), BF16, True, True), (w_dw3, "lead", (N_CHIPS, KW, 1, dw_cols), F32, False, False)],
        [(w, "rows", shape, BF16, False, True)
         for w, shape in ((w_out2, (DM, D)), (w_gateT, (F, D)), (w_upT, (F, D)), (w_down2, (F, D)))])
    w_pool_b = w_pool3.astype(BF16)
    ici = lambda f: _GatherIci([f], ["rows"], [True])
    d2d = lambda f: _GatherD2d([f], ["rows"])
    gather = _start("gather_start", [ici(f_out), ici(f_gate), ici(f_up), ici(f_down)])
    (z, xn_b), _ = _in_proj(x2, g_mix, w_inT_b, b_in, after=[gather.token])
    (f_out,) = _wait("gather_out_wait", gather, 0, xn_b)
    s_out = _start("share_out_start", [d2d(f_out)], sibling_only=True)
    (y_b, v), _ = _seq_fwd(z, w_dw4, b_dw, ln_g, ln_b, w_pool_b, s_pool, after=[s_out.token])
    (w_out_b,) = _wait("share_out_wait", s_out, 0, y_b)
    (f_gate,) = _wait("gather_gate_wait", gather, 1, y_b)
    s_gate = _start("share_gate_start", [d2d(f_gate)], sibling_only=True)
    (h1, hn_b), _ = _out_proj(y_b, x2, w_out_b, g_ffn, after=[s_gate.token])
    (f_up,) = _wait("gather_up_wait", gather, 2, hn_b)
    s_up = _start("share_up_start", [d2d(f_up)], sibling_only=True)
    (wgT_b,) = _wait("share_gate_wait", s_gate, 0, hn_b)
    (wuT_b,) = _wait("share_up_wait", s_up, 0, hn_b)
    (silu_b, uds_b, a_b), _ = _gate_up(hn_b, wgT_b, wuT_b)
    (f_down,) = _wait("gather_down_wait", gather, 3, a_b)
    s_down = _start("share_down_start", [d2d(f_down)], sibling_only=True)
    (wd_b,) = _wait("share_down_wait", s_down, 0, a_b)
    (dh2, dh2_b, loss_part, d_g_final), _ = _down_loss(a_b, wd_b, h1, target, g_final2)

    gw_down = _weight_grad("grad_w_down", a_b, dh2_b)
    x_down = _start("scatter_down_start", [_Scatter([gw_down], ["rows"])])
    (dg_b, du_b), _ = _ffn_bwd_act(dh2_b, wd_b, silu_b, uds_b, after=[x_down.token])
    gw_gateT = _weight_grad("grad_w_gate", dg_b, hn_b)
    gw_upT = _weight_grad("grad_w_up", du_b, hn_b)
    gw_down, p_down = _wait("scatter_down_wait", x_down, 0, gw_upT)
    sum_down = _sum_parts("sum_w_down", gw_down, "rows", [p_down], me)
    (dh1, dh1_b, dy, d_g_ffn), (p_gate, oth_down) = _ffn_bwd_in(
        dg_b, du_b, wgT_b, wuT_b, h1, dh2, g_ffn, w_out_b, comm=[_Scatter([gw_gateT], ["rows"]), _Swap([sum_down])])
    gw_out = _weight_grad("grad_w_out", y_b, dh1_b)
    sum_gate = _sum_parts("sum_w_gate", gw_gateT, "rows", [p_gate], me)
    res = {}
    res["w_down"] = _adamw_on_sparsecore("adamw_w_down", w_down2, m_w_down[0], v_w_down[0], sum_down, oth_down, sum_down)
    (dz_b, d_wdw, d_bdw, d_lng, d_lnb, d_wp, d_sp, d_bin), (p_up, p_out, oth_gate) = _seq_bwd(
        z, dy, v, w_dw4, ln_g, ln_b, w_pool_b, s_pool,
        comm=[_Scatter([gw_upT, gw_out], ["rows", "rows"]), _Swap([sum_gate])])
    res["w_gate"] = _adamw_on_sparsecore(
        "adamw_w_gate", w_gateT, m_w_gate[0].T, v_w_gate[0].T, sum_gate, oth_gate, res["w_down"][0])
    vec_grads ={"b_dw": d_bdw, "ln_g": d_lng, "ln_b": d_lnb, "s_pool": d_sp, "g_ffn": d_g_ffn, "g_final": d_g_final, "b_in": d_bin}
    layout = _PackLayout(dw_cols * N_CHIPS // LANES, n_grp, G, [(k, a.shape[1]) for k, a in vec_grads.items()])
    pack = _pack_small(layout, d_wdw, d_wp, vec_grads)
    sum_up = _sum_parts("sum_w_up", gw_upT, "rows", [p_up], me)
    sum_out = _sum_parts("sum_w_out", gw_out, "rows", [p_out], me)
    mid = _start("mid_start", [_Swap([sum_up, sum_out]), _Scatter([pack], ["all"])])
    gw_inT = _weight_grad("grad_w_in", dz_b, xn_b, after=[mid.token])
    sum_up, sum_out, oth_up, oth_out = _wait("mid_swap_wait", mid, 0, gw_inT)
    late = _start("late_start", [_Scatter([gw_inT], ["rows"])])
    (grad_x, d_g_mix), _ = _in_proj_bwd(dz_b, w_inT_b, x2, dh1, g_mix, after=[late.token])
    pack, p_small = _wait("mid_small_wait", mid, 1, d_g_mix)
    gw_inT, p_in = _wait("late_w_in_wait", late, 0, d_g_mix)
    sum_small = _sum_parts("sum_small", pack, "all", [p_small], me)
    res["w_up"] = _adamw_on_sparsecore("adamw_w_up", w_upT, m_w_up[0].T, v_w_up[0].T, sum_up, oth_up, res["w_gate"][0])
    res["w_out"] = _adamw_on_sparsecore("adamw_w_out", w_out2, m_w_out[0], v_w_out[0], sum_out, oth_out, res["w_gate"][0])
    sum_in = _sum_parts("sum_w_in", gw_inT, "rows", [p_in], me)
    (*res["g_mix"], loss_row), (oth_in, oth_small) = _allreduce_adamw_row(
        d_g_mix, g_mix, m_g_mix, v_g_mix, loss_part, comm=[_Swap([sum_in, sum_small])])
    loss = loss_row[0, 0]
    res["w_in"], _ = _adamw("adamw_w_in", w_in2, m_w_in[0], v_w_in[0], sum_in, oth_in, g_transposed=True)

    vec_w = {"b_dw": b_dw, "ln_g": ln_g, "ln_b": ln_b, "s_pool": s_pool, "g_ffn": g_ffn, "g_final": g_final2, "b_in": b_in}
    vec_m = {"b_dw": m_b_dw, "ln_g": m_ln_g, "ln_b": m_ln_b, "s_pool": m_s_pool, "g_ffn": m_g_ffn,
             "g_final": m_g_final.reshape(1, D), "b_in": m_b_in}
    vec_v = {"b_dw": v_b_dw, "ln_g": v_ln_g, "ln_b": v_ln_b, "s_pool": v_s_pool, "g_ffn": v_g_ffn,
             "g_final": v_g_final.reshape(1, D), "b_in": v_b_in}
    small = _adamw_small(layout, sum_small, oth_small, w_dw3, taps_first(m_w_dw), taps_first(v_w_dw),
                         w_pool3, m_w_pool[0], v_w_pool[0], vec_w, vec_m, vec_v)
    res["w_dw"] = [taps_first(a) for a in small[0:4]]
    res["w_pool"] = [a[None] for a in small[4:8]]
    for q, k in enumerate(vec_w):
        res[k] = list(small[8 + 4 * q : 12 + 4 * q])
    res["g_final"] = [a.reshape(D) for a in res["g_final"]]
    for k in ("w_in", "w_out", "w_down"):
        res[k] = [a[None] for a in res[k]]
    for k in ("w_gate", "w_up"):
        res[k] = [a.T[None] for a in res[k]]

    order = ["g_mix", "w_in", "b_in", "w_dw", "b_dw", "ln_g", "ln_b", "w_pool", "s_pool", "w_out", "g_ffn", "w_gate", "w_up", "w_down", "g_final"]
    outs = [loss, grad_x[None]]
    for q in range(4):
        outs += [res[k][q] for k in order]
    return tuple(outs)
```

```python
import jax
import jax.numpy as jnp
from jax import lax
from jax.experimental import pallas as pl
from jax.experimental.pallas import tpu as pltpu
from jax.experimental.pallas import tpu_sc as plsc

F32 = jnp.float32
BF16 = jnp.bfloat16
MESH = pl.DeviceIdType.MESH
ANY = pl.BlockSpec(memory_space=pl.ANY)

RMS_EPS = 1e-6
LN_EPS = 1e-5
POOL_WINDOWS = (2, 4, 8, 16)
ADAM_LR = 0.001
ADAM_B1 = 0.9
ADAM_B2 = 0.999
ADAM_EPS = 1e-08
ADAM_WD = 0.01
ADAM_STEP = 10

LANES = 128
SUBLANES = 8
BF16_ROWS = 16
HALO = 32
CONV_ROWS = 64
HIDDEN_CHUNK = 512
VMEM_LIMIT = 56 * 1024 * 1024
PACK_W = 512
N_CHIPS = 4
N_DEV = 8
SIBLING_BARRIER_ID = 0
SC_CORES = 2
SC_TILES = 32
SC_LANES = 16


def _tile(n, want, mult=8):
    t = min(n, want)
    while n % t or t % mult:
        t -= 1
    return t


def _sigmoid(x):
    return 1.0 / (1.0 + jnp.exp(-x))


def _dot(a, b, dims):
    return lax.dot_general(a, b, (dims, ((), ())), preferred_element_type=F32)


NN = ((1,), (0,))
NT = ((1,), (1,))
TN = ((0,), (0,))


def _rms_bwd(x, g, dy):
    r = lax.rsqrt(jnp.mean(x * x, axis=-1, keepdims=True) + RMS_EPS)
    xh = x * r
    gy = dy * g
    dx = r * (gy - xh * jnp.mean(gy * xh, axis=-1, keepdims=True))
    return dx, dy * xh


def _accumulate(ref, first, val):
    @pl.when(first)
    def _():
        ref[...] = val

    @pl.when(jnp.logical_not(first))
    def _():
        ref[...] += val


def _place():
    return lax.axis_index("x"), lax.axis_index("y"), lax.axis_index("c")


def _other_chips(x, y):
    return [(1 - x, y), (x, 1 - y), (1 - x, 1 - y)]


def _rows(ref, start, n):
    return ref.at[pl.ds(pl.multiple_of(start, BF16_ROWS), n)]


def _window(ref, how, k, c=None):
    if how == "all":
        return ref
    if how == "lead":
        return ref.at[k]
    assert how == "rows"
    n = ref.shape[0] // N_CHIPS
    if c is None:
        return _rows(ref, k * n, n)
    return _rows(ref, k * n + c * (n // 2), n // 2)


def _remote(src, dst, sems, s, device):
    return pltpu.make_async_remote_copy(
        src_ref=src, dst_ref=dst, send_sem=sems.at[s], recv_sem=sems.at[s + 1], device_id=device, device_id_type=MESH)


class _GatherIci:
    aliased = True

    def __init__(self, fulls, hows, splits, which=(0, 1, 2)):
        self.fulls, self.hows, self.splits, self.which = list(fulls), list(hows), list(splits), tuple(which)

    def inputs(self):
        return self.fulls

    def out_shapes(self):
        return [jax.ShapeDtypeStruct(a.shape, a.dtype) for a in self.fulls]

    def n_sems(self):
        return 6 * len(self.fulls)

    def build(self, ins, outs, sems, base):
        x, y, c = _place()
        me = 2 * x + y
        chips = _other_chips(x, y)
        starts, waits = [], []
        for a, (how, sp) in enumerate(zip(self.hows, self.splits)):
            half = c if sp else None
            mine = _window(outs[a], how, me, half)
            for j in self.which:
                px, py = chips[j]
                s = base + 6 * a + 2 * j
                cp = _remote(mine, mine, sems, s, (px, py, c))
                landing = _remote(mine, _window(outs[a], how, 2 * px + py, half), sems, s, (px, py, c))
                starts.append(cp.start)
                waits += [landing.wait_recv, cp.wait_send]
        return starts, waits


class _GatherD2d:
    aliased = True

    def __init__(self, fulls, hows):
        self.fulls, self.hows = list(fulls), list(hows)

    def inputs(self):
        return self.fulls

    def out_shapes(self):
        return [jax.ShapeDtypeStruct(a.shape, a.dtype) for a in self.fulls]

    def n_sems(self):
        return 6 * len(self.fulls)

    def build(self, ins, outs, sems, base):
        x, y, c = _place()
        starts, waits = [], []
        for a, how in enumerate(self.hows):
            for j, (px, py) in enumerate(_other_chips(x, y)):
                s = base + 6 * a + 2 * j
                got = _window(outs[a], how, 2 * px + py, c)
                cp = _remote(got, got, sems, s, (x, y, 1 - c))
                landing = _remote(got, _window(outs[a], how, 2 * px + py, 1 - c), sems, s, (x, y, 1 - c))
                starts.append(cp.start)
                waits += [landing.wait_recv, cp.wait_send]
        return starts, waits


def _part_shape(a, how):
    if how == "all":
        return a.shape
    assert how == "rows"
    return (a.shape[0] // N_CHIPS, a.shape[1])


class _Scatter:
    aliased = False

    def __init__(self, fulls, hows, which=(0, 1, 2)):
        self.fulls, self.hows, self.which = list(fulls), list(hows), tuple(which)

    def inputs(self):
        return self.fulls

    def out_shapes(self):
        return [jax.ShapeDtypeStruct((len(self.which),) + _part_shape(a, h), a.dtype) for a, h in zip(self.fulls, self.hows)]

    def n_sems(self):
        return 6 * len(self.fulls)

    def build(self, ins, outs, sems, base):
        x, y, c = _place()
        chips = _other_chips(x, y)
        starts, waits = [], []
        for a, how in enumerate(self.hows):
            for slot, j in enumerate(self.which):
                px, py = chips[j]
                cp = _remote(_window(ins[a], how, 2 * px + py), outs[a].at[slot], sems, base + 6 * a + 2 * j, (px, py, c))
                starts.append(cp.start)
                waits += [cp.wait_recv, cp.wait_send]
        return starts, waits


class _Swap:
    aliased = False

    def __init__(self, arrays):
        self.arrays = list(arrays)

    def inputs(self):
        return self.arrays

    def out_shapes(self):
        return [jax.ShapeDtypeStruct(a.shape, a.dtype) for a in self.arrays]

    def n_sems(self):
        return 2 * len(self.arrays)

    def build(self, ins, outs, sems, base):
        x, y, c = _place()
        starts, waits = [], []
        for a in range(len(ins)):
            cp = _remote(ins[a], outs[a], sems, base + 2 * a, (x, y, 1 - c))
            starts.append(cp.start)
            waits += [cp.wait_recv, cp.wait_send]
        return starts, waits


def _call(name, body, grid, in_specs, out_specs, out_shape, args, scratch=(), comm=(), after=()):
    comm, after = list(comm), list(after)
    n_in, n_out, n_scr, n_after = len(args), len(out_shape), len(scratch), len(after)
    c_in = [a for op in comm for a in op.inputs()]
    c_out = [s for op in comm for s in op.out_shapes()]
    n_sems = sum(op.n_sems() for op in comm)
    aliases, i_in, i_out = {}, 0, 0
    for op in comm:
        if op.aliased:
            for q in range(len(op.inputs())):
                aliases[n_in + n_after + i_in + q] = n_out + i_out + q
        i_in, i_out = i_in + len(op.inputs()), i_out + len(op.out_shapes())

    def wrapped(*refs):
        ins = refs[:n_in]
        cin = refs[n_in + n_after : n_in + n_after + len(c_in)]
        o0 = n_in + n_after + len(c_in)
        outs = refs[o0 : o0 + n_out]
        cout = refs[o0 + n_out : o0 + n_out + len(c_out)]
        s0 = o0 + n_out + len(c_out)
        scr = refs[s0 : s0 + n_scr]

        def copies():
            sems = refs[s0 + n_scr]
            starts, waits = [], []
            i_in = i_out = base = 0
            for op in comm:
                ni, no = len(op.inputs()), len(op.out_shapes())
                s, w = op.build(cin[i_in : i_in + ni], cout[i_out : i_out + no], sems, base)
                starts += s
                waits += w
                i_in, i_out, base = i_in + ni, i_out + no, base + op.n_sems()
            return starts, waits

        def run_starts():
            for start in copies()[0]:
                start()

        def run_waits():
            for wait in copies()[1]:
                wait()

        if comm and grid:
            first = last = True
            for d, n in enumerate(grid):
                first = jnp.logical_and(first, pl.program_id(d) == 0)
                last = jnp.logical_and(last, pl.program_id(d) == n - 1)
            pl.when(first)(run_starts)
        elif comm:
            run_starts()
        if body is not None:
            body(*ins, *outs, *scr)
        if comm and grid:
            pl.when(last)(run_waits)
        elif comm:
            run_waits()

    res = pl.pallas_call(
        wrapped,
        name=name,
        grid=grid,
        in_specs=list(in_specs) + [ANY] * (n_after + len(c_in)),
        out_specs=list(out_specs) + [ANY] * len(c_out),
        out_shape=list(out_shape) + c_out,
        scratch_shapes=list(scratch) + ([pltpu.SemaphoreType.DMA((n_sems,))] if comm else []),
        input_output_aliases=aliases,
        compiler_params=pltpu.CompilerParams(dimension_semantics=("arbitrary",) * len(grid), vmem_limit_bytes=VMEM_LIMIT),
    )(*args, *after, *c_in)
    return tuple(res[:n_out]), tuple(res[n_out:])


def _place_and_gather(now, later):
    items = list(now) + list(later)
    n, n_now = len(items), len(now)
    buf_shape = lambda it: it[0].shape[::-1] if it[4] else it[0].shape
    split_now = [a for a in range(n_now) if items[a][5]]

    def body(*refs):
        ins, outs = refs[:n], refs[n : 2 * n]
        stage, bufs = refs[2 * n : 3 * n - n_now], refs[3 * n - n_now : 4 * n - n_now]
        sems = refs[4 * n - n_now]
        x, y, c = _place()
        me = 2 * x + y
        chips = _other_chips(x, y)
        loads = [pltpu.make_async_copy(ins[a], stage[a - n_now], sems.at[a]) for a in range(n_now, n)]
        for ld in loads:
            ld.start()
        pending = []

        def place(a, val):
            _, how, _, dtype, transposed, _ = items[a]
            bufs[a][...] = (val.T if transposed else val).astype(dtype)
            cp = pltpu.make_async_copy(bufs[a], _window(outs[a], how, me), sems.at[n + a])
            cp.start()
            pending.append(cp.wait)

        arrivals = []
        for a in range(n_now):
            place(a, ins[a][...])
            how, split = items[a][1], items[a][5]
            half = c if split else None
            src = _rows(bufs[a], c * (bufs[a].shape[0] // 2), bufs[a].shape[0] // 2) if split else bufs[a]
            for j, (px, py) in enumerate(chips):
                s = 2 * n + 6 * a + 2 * j
                cp = _remote(src, _window(outs[a], how, me, half), sems, s, (px, py, c))
                landing = _remote(src, _window(outs[a], how, 2 * px + py, half), sems, s, (px, py, c))
                cp.start()
                arrivals.append(landing.wait_recv)
                pending.append(cp.wait_send)
        for a in range(n_now, n):
            loads[a - n_now].wait()
            place(a, stage[a - n_now][...])
        for wait in arrivals:
            wait()
        d2d = _GatherD2d([None] * len(split_now), [items[a][1] for a in split_now])
        starts, waits = d2d.build(None, [outs[a] for a in split_now], sems, 2 * n + 6 * n_now)
        for start in starts:
            start()
        for wait in waits + pending:
            wait()

    vm = pl.BlockSpec(memory_space=pltpu.VMEM)
    return pl.pallas_call(
        body,
        name="place_and_gather",
        in_specs=[vm] * n_now + [ANY] * (n - n_now),
        out_specs=[ANY] * n,
        out_shape=[jax.ShapeDtypeStruct(it[2], it[3]) for it in items],
        scratch_shapes=[pltpu.VMEM(it[0].shape, it[0].dtype) for it in later]
        + [pltpu.VMEM(buf_shape(it), it[3]) for it in items]
        + [pltpu.SemaphoreType.DMA((2 * n + 6 * n_now + 6 * len(split_now),))],
        compiler_params=pltpu.CompilerParams(vmem_limit_bytes=VMEM_LIMIT),
    )(*[it[0] for it in items])


_HBM = pl.BlockSpec(memory_space=pltpu.HBM)
_SEM = pl.BlockSpec(memory_space=pltpu.SEMAPHORE)
_DATAFLOW = pltpu.SideEffectType.DATAFLOW_SIDE_EFFECTING


class _Pending:
    def __init__(self, ops, bases, sems, arrays, token):
        self.ops, self.bases, self.sems, self.arrays, self.token = ops, bases, sems, arrays, token


def _op_refs(op, refs):
    n_src = len(op.inputs())
    return refs[:n_src], (refs[:n_src] if op.aliased else refs[n_src:])


def _start(name, ops, sibling_only=False):
    per_op = [list(op.inputs()) + ([] if op.aliased else [lax.empty(sd.shape, sd.dtype) for sd in op.out_shapes()])
              for op in ops]
    arrays = [a for group in per_op for a in group]
    bases = [sum(op.n_sems() for op in ops[:k]) for k in range(len(ops))]
    n = len(arrays)

    def body(*refs):
        sems, token = refs[n], refs[-1]
        if sibling_only:
            x, y, c = _place()
            barrier = pltpu.get_barrier_semaphore()
            pl.semaphore_signal(barrier, inc=1, device_id=(x, y, 1 - c), device_id_type=MESH)
            pl.semaphore_wait(barrier, 1)
        at = 0
        for op, group, base in zip(ops, per_op, bases):
            starts, _ = op.build(*_op_refs(op, refs[at : at + len(group)]), sems, base)
            for start in starts:
                start()
            at += len(group)
        token[...] = jnp.zeros_like(token)

    res = pl.pallas_call(
        body,
        name=name,
        out_shape=(pltpu.SemaphoreType.DMA((sum(op.n_sems() for op in ops),)),)
        + tuple(pltpu.HBM(a.shape, a.dtype) for a in arrays) + (jax.ShapeDtypeStruct((SUBLANES, LANES), F32),),
        in_specs=(_HBM,) * n,
        out_specs=(_SEM,) + (_HBM,) * n + (pl.BlockSpec(memory_space=pltpu.VMEM),),
        input_output_aliases={i: 1 + i for i in range(n)},
        compiler_params=pltpu.CompilerParams(
            has_side_effects=_DATAFLOW, collective_id=SIBLING_BARRIER_ID if sibling_only else None),
    )(*[pltpu.with_memory_space_constraint(a, pltpu.HBM) for a in arrays])
    thru, at, groups = list(res[1 : 1 + n]), 0, []
    for group in per_op:
        groups.append(thru[at : at + len(group)])
        at += len(group)
    return _Pending(list(ops), bases, res[0], groups, res[-1])


def _wait(name, pending, k, after):
    op, arrays = pending.ops[k], pending.arrays[k]
    n = len(arrays)

    def body(*refs):
        _, waits = op.build(*_op_refs(op, refs[:n]), refs[n], pending.bases[k])
        for wait in waits:
            wait()

    return pl.pallas_call(
        body,
        name=name,
        out_shape=tuple(pltpu.HBM(a.shape, a.dtype) for a in arrays),
        in_specs=(_HBM,) * n + (_SEM, ANY),
        out_specs=(_HBM,) * n,
        input_output_aliases={i: i for i in range(n)},
        compiler_params=pltpu.CompilerParams(has_side_effects=_DATAFLOW),
    )(*arrays, pending.sems, after)


def _in_proj(x, g_mix, w_inT_b, b_in, after=()):
    T, D = x.shape
    CI = w_inT_b.shape[0]
    tm = _tile(T, 512)

    def body(x_ref, g_ref, w_ref, b_ref, z_ref, xn_ref):
        xv = x_ref[...]
        r = lax.rsqrt(jnp.mean(xv * xv, axis=-1, keepdims=True) + RMS_EPS)
        xn = (xv * r * g_ref[...]).astype(BF16)
        xn_ref[...] = xn
        z_ref[...] = _dot(xn, w_ref[...], NT) + b_ref[...]

    return _call(
        "in_proj",
        body,
        (T // tm,),
        [
            pl.BlockSpec((tm, D), lambda i: (i, 0)),
            pl.BlockSpec((1, D), lambda i: (0, 0)),
            pl.BlockSpec((CI, D), lambda i: (0, 0)),
            pl.BlockSpec((1, CI), lambda i: (0, 0)),
        ],
        [pl.BlockSpec((tm, CI), lambda i: (i, 0)), pl.BlockSpec((tm, D), lambda i: (i, 0))],
        [jax.ShapeDtypeStruct((T, CI), F32), jax.ShapeDtypeStruct((T, D), BF16)],
        (x, g_mix, w_inT_b, b_in),
        after=after,
    )


def _fill_shifted(scr):
    n = scr.shape[1] - SUBLANES
    for s in range(1, SUBLANES):
        scr[s, 0:n, :] = scr[0, s : s + n, :]


def _shifted_rows(scr, off, n, cs):
    s = off % SUBLANES
    return scr[s, off - s : off - s + n, cs]


def _pool_mean_minus_token(p_scr, cs, w, cnt, tt):
    tok = p_scr[HALO : HALO + tt, cs]
    s = tok
    for d in range(1, w):
        s = s + p_scr[HALO - d : HALO - d + tt, cs]
    return s / cnt - tok


def _seq_fwd(z, w_dw4, b_dw, ln_g, ln_b, w_pool_b, s_pool, after=()):
    T, CI = z.shape
    CC = ln_g.shape[1]
    n_grp, G = w_pool_b.shape[0], w_pool_b.shape[-1]
    KW = w_dw4.shape[1]
    D = CC + n_grp * G
    tt = _tile(T, 512, HALO)
    per = tt // HALO

    def body(zc_ref, zp_ref, wdw_ref, bdw_ref, lng_ref, lnb_ref, wp_ref, sp_ref, y_ref, v_ref, u_scr, p_scr):
        i = pl.program_id(0)
        first = i == 0
        u_prev = zp_ref[:, 0:CC] * _sigmoid(zp_ref[:, CC : 2 * CC])
        u_scr[0, 0:HALO, :] = jnp.where(first, 0.0, u_prev)
        p_scr[0:HALO, :] = jnp.where(first, 0.0, zp_ref[:, 2 * CC :])
        u_scr[0, HALO:, :] = zc_ref[:, 0:CC] * _sigmoid(zc_ref[:, CC : 2 * CC])
        p_scr[HALO:, :] = zc_ref[:, 2 * CC :]
        _fill_shifted(u_scr)

        for j in range(CC // LANES):
            cs = slice(LANES * j, LANES * (j + 1))
            for rb in range(tt // CONV_ROWS):
                acc = jnp.zeros((CONV_ROWS, LANES), F32)
                for k in range(KW):
                    off = HALO - (KW - 1) + k + rb * CONV_ROWS
                    acc = acc + _shifted_rows(u_scr, off, CONV_ROWS, cs) * wdw_ref[j, k]
                v_ref[rb * CONV_ROWS : (rb + 1) * CONV_ROWS, cs] = acc + bdw_ref[:, cs]

        v = v_ref[...]
        mu = jnp.mean(v, axis=-1, keepdims=True)
        d = v - mu
        var = jnp.mean(d * d, axis=-1, keepdims=True)
        ln = d * lax.rsqrt(var + LN_EPS) * lng_ref[...] + lnb_ref[...]
        y_ref[:, 0:CC] = (ln * _sigmoid(ln)).astype(BF16)

        tpos = i * tt + lax.broadcasted_iota(jnp.int32, (tt, 1), 0)
        for gi, w in enumerate(POOL_WINDOWS):
            cs = slice(G * gi, G * (gi + 1))
            cnt = jnp.minimum(tpos + 1, w).astype(F32)
            yi = _pool_mean_minus_token(p_scr, cs, w, cnt, tt)
            q = _dot(yi.astype(BF16), wp_ref[gi], NN)
            y_ref[:, CC + G * gi : CC + G * (gi + 1)] = (q * sp_ref[:, cs]).astype(BF16)

    const2 = lambda i: (0, 0)
    return _call(
        "seq_fwd",
        body,
        (T // tt,),
        [
            pl.BlockSpec((tt, CI), lambda i: (i, 0)),
            pl.BlockSpec((HALO, CI), lambda i: (jnp.maximum(i * per - 1, 0), 0)),
            pl.BlockSpec(w_dw4.shape, lambda i: (0,) * w_dw4.ndim),
            pl.BlockSpec((1, CC), const2),
            pl.BlockSpec((1, CC), const2),
            pl.BlockSpec((1, CC), const2),
            pl.BlockSpec(w_pool_b.shape, lambda i: (0, 0, 0)),
            pl.BlockSpec((1, n_grp * G), const2),
        ],
        [pl.BlockSpec((tt, D), lambda i: (i, 0)), pl.BlockSpec((tt, CC), lambda i: (i, 0))],
        [jax.ShapeDtypeStruct((T, D), BF16), jax.ShapeDtypeStruct((T, CC), F32)],
        (z, z, w_dw4, b_dw, ln_g, ln_b, w_pool_b, s_pool),
        scratch=[pltpu.VMEM((SUBLANES, HALO + tt, CC), F32), pltpu.VMEM((HALO + tt, n_grp * G), F32)],
        after=after,
    )


def _out_proj(y_b, x, w_out_b, g_ffn, after=()):
    T, D = x.shape
    tm = _tile(T, 512)

    def body(y_ref, x_ref, w_ref, g_ref, h1_ref, hn_ref):
        h1 = x_ref[...] + _dot(y_ref[...], w_ref[...], NN)
        h1_ref[...] = h1
        r = lax.rsqrt(jnp.mean(h1 * h1, axis=-1, keepdims=True) + RMS_EPS)
        hn_ref[...] = (h1 * r * g_ref[...]).astype(BF16)

    row = lambda i: (i, 0)
    return _call(
        "out_proj",
        body,
        (T // tm,),
        [
            pl.BlockSpec((tm, y_b.shape[1]), row),
            pl.BlockSpec((tm, D), row),
            pl.BlockSpec(w_out_b.shape, lambda i: (0, 0)),
            pl.BlockSpec((1, D), lambda i: (0, 0)),
        ],
        [pl.BlockSpec((tm, D), row), pl.BlockSpec((tm, D), row)],
        [jax.ShapeDtypeStruct((T, D), F32), jax.ShapeDtypeStruct((T, D), BF16)],
        (y_b, x, w_out_b, g_ffn),
        after=after,
    )


def _hidden_tile(F):
    return _tile(F, 1408, LANES)


def _gate_up(hn_b, wgT_b, wuT_b):
    T, D = hn_b.shape
    F = wgT_b.shape[0]
    tm, tf = _tile(T, 1024), _hidden_tile(F)

    def body(hn_ref, wg_ref, wu_ref, silu_ref, uds_ref, a_ref):
        hn = hn_ref[...]
        for c0 in range(0, tf, HIDDEN_CHUNK):
            cs = slice(c0, min(c0 + HIDDEN_CHUNK, tf))
            gv = _dot(hn, wg_ref[cs, :], NT)
            uv = _dot(hn, wu_ref[cs, :], NT)
            sg = _sigmoid(gv)
            silu = gv * sg
            silu_ref[:, cs] = silu.astype(BF16)
            uds_ref[:, cs] = (uv * (sg * (1.0 + gv * (1.0 - sg)))).astype(BF16)
            a_ref[:, cs] = (silu * uv).astype(BF16)

    wspec = pl.BlockSpec((tf, D), lambda j, i: (j, 0))
    ospec = pl.BlockSpec((tm, tf), lambda j, i: (i, j))
    return _call(
        "gate_up",
        body,
        (F // tf, T // tm),
        [pl.BlockSpec((tm, D), lambda j, i: (i, 0)), wspec, wspec],
        [ospec, ospec, ospec],
        [jax.ShapeDtypeStruct((T, F), BF16)] * 3,
        (hn_b, wgT_b, wuT_b),
    )


def _down_loss(a_b, wd_b, h1, target, g_final):
    T, D = h1.shape
    F = a_b.shape[1]
    tm = _tile(T, 512)
    nt = T // tm

    def body(a_ref, w_ref, h1_ref, t_ref, g_ref, dh2_ref, dh2b_ref, loss_ref, dg_ref):
        i = pl.program_id(0)
        h2 = h1_ref[...] + _dot(a_ref[...], w_ref[...], NN)
        r = lax.rsqrt(jnp.mean(h2 * h2, axis=-1, keepdims=True) + RMS_EPS)
        g = g_ref[...]
        diff = h2 * r * g - t_ref[...]
        _accumulate(loss_ref, i == 0, jnp.full(loss_ref.shape, jnp.sum(diff * diff) * (0.5 / D), F32))
        dh2, dg_rows = _rms_bwd(h2, g, diff * (1.0 / D))
        dh2_ref[...] = dh2
        dh2b_ref[...] = dh2.astype(BF16)
        _accumulate(dg_ref, i == 0, jnp.sum(dg_rows, axis=0, keepdims=True))

    row = lambda i: (i, 0)
    return _call(
        "down_loss",
        body,
        (nt,),
        [
            pl.BlockSpec((tm, F), row),
            pl.BlockSpec((F, D), lambda i: (0, 0), pipeline_mode=pl.Buffered(1)),
            pl.BlockSpec((tm, D), row),
            pl.BlockSpec((tm, D), row),
            pl.BlockSpec((1, D), lambda i: (0, 0)),
        ],
        [
            pl.BlockSpec((tm, D), row),
            pl.BlockSpec((tm, D), row),
            pl.BlockSpec((1, LANES), lambda i: (0, 0)),
            pl.BlockSpec((1, D), lambda i: (0, 0)),
        ],
        [
            jax.ShapeDtypeStruct((T, D), F32),
            jax.ShapeDtypeStruct((T, D), BF16),
            jax.ShapeDtypeStruct((1, LANES), F32),
            jax.ShapeDtypeStruct((1, D), F32),
        ],
        (a_b, wd_b, h1, target, g_final),
    )


def _ffn_bwd_act(dh2_b, wd_b, silu_b, uds_b, after=()):
    T, D = dh2_b.shape
    F = wd_b.shape[0]
    tm, tf = _tile(T, 1024), _hidden_tile(F)

    def body(d_ref, w_ref, silu_ref, uds_ref, dg_ref, du_ref):
        d = d_ref[...]
        for c0 in range(0, tf, HIDDEN_CHUNK):
            cs = slice(c0, min(c0 + HIDDEN_CHUNK, tf))
            da = _dot(d, w_ref[cs, :], NT)
            dg_ref[:, cs] = (da * uds_ref[:, cs].astype(F32)).astype(BF16)
            du_ref[:, cs] = (da * silu_ref[:, cs].astype(F32)).astype(BF16)

    aspec = pl.BlockSpec((tm, tf), lambda j, i: (i, j))
    return _call(
        "ffn_bwd_act",
        body,
        (F // tf, T // tm),
        [pl.BlockSpec((tm, D), lambda j, i: (i, 0)), pl.BlockSpec((tf, D), lambda j, i: (j, 0)), aspec, aspec],
        [aspec, aspec],
        [jax.ShapeDtypeStruct((T, F), BF16)] * 2,
        (dh2_b, wd_b, silu_b, uds_b),
        after=after,
    )


def _ffn_bwd_in(dg_b, du_b, wgT_b, wuT_b, h1, dh2, g_ffn, w_out_b, comm=()):
    T, D = h1.shape
    F = wgT_b.shape[0]
    DM = w_out_b.shape[0]
    tm = _tile(T, 512)

    def body(dg_ref, du_ref, wg_ref, wu_ref, h1_ref, dh2_ref, g_ref, wo_ref, dh1_ref, dh1b_ref, dy_ref, dgf_ref):
        i = pl.program_id(0)
        dhn = _dot(dg_ref[...], wg_ref[...], NN) + _dot(du_ref[...], wu_ref[...], NN)
        dx, dg_rows = _rms_bwd(h1_ref[...], g_ref[...], dhn)
        dh1 = dh2_ref[...] + dx
        dh1b = dh1.astype(BF16)
        dh1_ref[...] = dh1
        dh1b_ref[...] = dh1b
        dy_ref[...] = _dot(dh1b, wo_ref[...], NT)
        _accumulate(dgf_ref, i == 0, jnp.sum(dg_rows, axis=0, keepdims=True))

    row = lambda i: (i, 0)
    const = lambda i: (0, 0)
    return _call(
        "ffn_bwd_in",
        body,
        (T // tm,),
        [
            pl.BlockSpec((tm, F), row),
            pl.BlockSpec((tm, F), row),
            pl.BlockSpec((F, D), const, pipeline_mode=pl.Buffered(1)),
            pl.BlockSpec((F, D), const, pipeline_mode=pl.Buffered(1)),
            pl.BlockSpec((tm, D), row),
            pl.BlockSpec((tm, D), row),
            pl.BlockSpec((1, D), const),
            pl.BlockSpec((DM, D), const, pipeline_mode=pl.Buffered(1)),
        ],
        [pl.BlockSpec((tm, D), row), pl.BlockSpec((tm, D), row), pl.BlockSpec((tm, DM), row), pl.BlockSpec((1, D), const)],
        [
            jax.ShapeDtypeStruct((T, D), F32),
            jax.ShapeDtypeStruct((T, D), BF16),
            jax.ShapeDtypeStruct((T, DM), F32),
            jax.ShapeDtypeStruct((1, D), F32),
        ],
        (dg_b, du_b, wgT_b, wuT_b, h1, dh2, g_ffn, w_out_b),
        comm=comm,
    )


def _seq_bwd(z, dy, v, w_dw4, ln_g, ln_b, w_pool_b, s_pool, comm=()):
    T, CI = z.shape
    CC = ln_g.shape[1]
    n_grp, G = w_pool_b.shape[0], w_pool_b.shape[-1]
    CP = n_grp * G
    KW = w_dw4.shape[1]
    n_cc = CC // LANES
    D = CC + CP
    tt = _tile(T, 512, HALO)
    per = tt // HALO
    n_tiles = T // tt
    last_halo = T // HALO - 1

    def body(zc_ref, zp_ref, dyc_ref, dyn_ref, vc_ref, vn_ref, wdw_ref, lng_ref, lnb_ref, wp_ref, sp_ref,
             dz_ref, dwdw_ref, dbdw_ref, dlng_ref, dlnb_ref, dwp_ref, dsp_ref, dbin_ref,
             dv_scr, u_scr, p_scr, g_scr, dw_scr):
        i = pl.program_id(0)
        first = i == 0
        last = i == n_tiles - 1
        lng, lnb = lng_ref[...], lnb_ref[...]

        def conv_pre(vv, dyc):
            mu = jnp.mean(vv, axis=-1, keepdims=True)
            d = vv - mu
            rs = lax.rsqrt(jnp.mean(d * d, axis=-1, keepdims=True) + LN_EPS)
            xh = d * rs
            ln = xh * lng + lnb
            sg = _sigmoid(ln)
            dln = dyc * (sg * (1.0 + ln * (1.0 - sg)))
            dxh = dln * lng
            dv = rs * (dxh - jnp.mean(dxh, axis=-1, keepdims=True) - xh * jnp.mean(dxh * xh, axis=-1, keepdims=True))
            return dv, dln, xh

        dv_c, dln_c, xh_c = conv_pre(vc_ref[...], dyc_ref[:, 0:CC])
        dv_scr[0, 0:tt, :] = dv_c
        dv_n, _, _ = conv_pre(vn_ref[...], dyn_ref[:, 0:CC])
        dv_scr[0, tt:, :] = jnp.where(last, 0.0, dv_n)
        _fill_shifted(dv_scr)
        _accumulate(dlng_ref, first, jnp.sum(dln_c * xh_c, axis=0, keepdims=True))
        _accumulate(dlnb_ref, first, jnp.sum(dln_c, axis=0, keepdims=True))
        _accumulate(dbdw_ref, first, jnp.sum(dv_c, axis=0, keepdims=True))

        u_scr[...] = zc_ref[:, 0:CC] * _sigmoid(zc_ref[:, CC : 2 * CC])

        @pl.when(first)
        def _():
            dw_scr[...] = jnp.zeros_like(dw_scr)

        for j in range(n_cc):
            cs = slice(LANES * j, LANES * (j + 1))
            gs = slice(CC + LANES * j, CC + LANES * (j + 1))
            dbin_a = jnp.zeros((1, LANES), F32)
            dbin_g = jnp.zeros((1, LANES), F32)
            for rb in range(tt // CONV_ROWS):
                rows = slice(rb * CONV_ROWS, (rb + 1) * CONV_ROWS)
                u_blk = u_scr[rows, cs]
                du = jnp.zeros((CONV_ROWS, LANES), F32)
                for k in range(KW):
                    off = rb * CONV_ROWS + (KW - 1) - k
                    d = _shifted_rows(dv_scr, off, CONV_ROWS, cs)
                    du = du + d * wdw_ref[j, k]
                    dw_scr[j * HALO + k] += jnp.sum((u_blk * d).reshape(CONV_ROWS // 8, 8, LANES), axis=0)
                a = zc_ref[rows, cs]
                sg = _sigmoid(zc_ref[rows, gs])
                da = du * sg
                dgate = du * a * sg * (1.0 - sg)
                dz_ref[rows, cs] = da.astype(BF16)
                dz_ref[rows, gs] = dgate.astype(BF16)
                dbin_a = dbin_a + jnp.sum(da, axis=0, keepdims=True)
                dbin_g = dbin_g + jnp.sum(dgate, axis=0, keepdims=True)
            _accumulate(dbin_ref.at[:, cs], first, dbin_a)
            _accumulate(dbin_ref.at[:, gs], first, dbin_g)

        @pl.when(last)
        def _():
            dwdw_ref[...] = jnp.sum(dw_scr[...], axis=1).reshape(dwdw_ref.shape)

        p_scr[0:HALO, :] = jnp.where(first, 0.0, zp_ref[:, 2 * CC :])
        p_scr[HALO:, :] = zc_ref[:, 2 * CC :]
        tpos = i * tt + lax.broadcasted_iota(jnp.int32, (tt, 1), 0)
        for gi, w in enumerate(POOL_WINDOWS):
            cs = slice(G * gi, G * (gi + 1))
            ys = slice(CC + G * gi, CC + G * (gi + 1))
            ps = slice(2 * CC + G * gi, 2 * CC + G * (gi + 1))
            cnt = jnp.minimum(tpos + 1, w).astype(F32)
            yib = _pool_mean_minus_token(p_scr, cs, w, cnt, tt).astype(BF16)
            wp = wp_ref[gi]
            sp = sp_ref[:, cs]
            dyp = dyc_ref[:, ys]
            q = _dot(yib, wp, NN)
            _accumulate(dsp_ref.at[:, cs], first, jnp.sum(dyp * q, axis=0, keepdims=True))
            dq_c = (dyp * sp).astype(BF16)
            dq_n = (jnp.where(last, 0.0, dyn_ref[:, ys]) * sp).astype(BF16)
            _accumulate(dwp_ref.at[gi], first, _dot(yib, dq_c, TN))
            dyi_c = _dot(dq_c, wp, NT)
            g_scr[0:tt, cs] = dyi_c / cnt
            g_scr[tt:, cs] = _dot(dq_n, wp, NT) * (1.0 / w)
            dp = -dyi_c
            for d in range(w):
                dp = dp + g_scr[d : d + tt, cs]
            dz_ref[:, ps] = dp.astype(BF16)
            _accumulate(dbin_ref.at[:, ps], first, jnp.sum(dp, axis=0, keepdims=True))

    cur = lambda i: (i, 0)
    prev = lambda i: (jnp.maximum(i * per - 1, 0), 0)
    nxt = lambda i: (jnp.minimum((i + 1) * per, last_halo), 0)
    c2 = lambda i: (0, 0)
    c3 = lambda i: (0, 0, 0)
    return _call(
        "seq_bwd",
        body,
        (n_tiles,),
        [
            pl.BlockSpec((tt, CI), cur),
            pl.BlockSpec((HALO, CI), prev),
            pl.BlockSpec((tt, D), cur),
            pl.BlockSpec((HALO, D), nxt),
            pl.BlockSpec((tt, CC), cur),
            pl.BlockSpec((HALO, CC), nxt),
            pl.BlockSpec(w_dw4.shape, lambda i: (0,) * w_dw4.ndim),
            pl.BlockSpec((1, CC), c2),
            pl.BlockSpec((1, CC), c2),
            pl.BlockSpec(w_pool_b.shape, c3),
            pl.BlockSpec((1, CP), c2),
        ],
        [
            pl.BlockSpec((tt, CI), cur),
            pl.BlockSpec((n_cc, HALO, LANES), c3),
            pl.BlockSpec((1, CC), c2),
            pl.BlockSpec((1, CC), c2),
            pl.BlockSpec((1, CC), c2),
            pl.BlockSpec((n_grp, G, G), c3),
            pl.BlockSpec((1, CP), c2),
            pl.BlockSpec((1, CI), c2),
        ],
        [
            jax.ShapeDtypeStruct((T, CI), BF16),
            jax.ShapeDtypeStruct((n_cc, HALO, LANES), F32),
            jax.ShapeDtypeStruct((1, CC), F32),
            jax.ShapeDtypeStruct((1, CC), F32),
            jax.ShapeDtypeStruct((1, CC), F32),
            jax.ShapeDtypeStruct((n_grp, G, G), F32),
            jax.ShapeDtypeStruct((1, CP), F32),
            jax.ShapeDtypeStruct((1, CI), F32),
        ],
        (z, z, dy, dy, v, v, w_dw4, ln_g, ln_b, w_pool_b, s_pool),
        scratch=[
            pltpu.VMEM((SUBLANES, tt + HALO, CC), F32),
            pltpu.VMEM((tt, CC), F32),
            pltpu.VMEM((HALO + tt, CP), F32),
            pltpu.VMEM((tt + HALO, CP), F32),
            pltpu.VMEM((n_cc * HALO, 8, LANES), F32),
        ],
        comm=comm,
    )


def _in_proj_bwd(dz_b, w_inT_b, x, dh1, g_mix, after=()):
    T, D = x.shape
    CI = w_inT_b.shape[0]
    tm = _tile(T, 512)

    def body(dz_ref, w_ref, x_ref, dh1_ref, g_ref, dx_ref, dg_ref):
        i = pl.program_id(0)
        dxn = _dot(dz_ref[...], w_ref[...], NN)
        dx, dg_rows = _rms_bwd(x_ref[...], g_ref[...], dxn)
        dx_ref[...] = dh1_ref[...] + dx
        _accumulate(dg_ref, i == 0, jnp.sum(dg_rows, axis=0, keepdims=True))

    row = lambda i: (i, 0)
    const = lambda i: (0, 0)
    return _call(
        "in_proj_bwd",
        body,
        (T // tm,),
        [
            pl.BlockSpec((tm, CI), row),
            pl.BlockSpec((CI, D), const),
            pl.BlockSpec((tm, D), row),
            pl.BlockSpec((tm, D), row),
            pl.BlockSpec((1, D), const),
        ],
        [pl.BlockSpec((tm, D), row), pl.BlockSpec((1, D), const)],
        [jax.ShapeDtypeStruct((T, D), F32), jax.ShapeDtypeStruct((1, D), F32)],
        (dz_b, w_inT_b, x, dh1, g_mix),
        after=after,
    )


def _weight_grad(name, a_b, b_b, comm=(), after=()):
    T, N1 = a_b.shape
    N2 = b_b.shape[1]
    t1 = _tile(N1, 1408, LANES)
    tk = _tile(T, 2048)
    nk = T // tk

    def body(a_ref, b_ref, o_ref, acc):
        k = pl.program_id(1)
        _accumulate(acc, k == 0, _dot(a_ref[...], b_ref[...], TN))

        @pl.when(k == nk - 1)
        def _():
            o_ref[...] = acc[...].astype(BF16)

    (out,), rest = _call(
        name,
        body,
        (N1 // t1, nk),
        [pl.BlockSpec((tk, t1), lambda n, k: (k, n)), pl.BlockSpec((tk, N2), lambda n, k: (k, 0))],
        [pl.BlockSpec((t1, N2), lambda n, k: (n, 0))],
        [jax.ShapeDtypeStruct((N1, N2), BF16)],
        (a_b, b_b),
        scratch=[pltpu.VMEM((t1, N2), F32)],
        comm=comm,
        after=after,
    )
    return out, rest


def _sum_parts(name, full, how, parts, me):
    _, R, C = parts[0].shape
    tr = _tile(R, 512)
    nb = R // tr
    where = [(q, r) for q, p in enumerate(parts) for r in range(p.shape[0])]
    assert len(where) == 3

    def body(me_ref, own_ref, *refs):
        o_ref = refs[-1]
        f = lambda j: refs[where[j][0]][where[j][1]].astype(F32)
        o_ref[...] = (own_ref[...].astype(F32) + f(0)) + (f(1) + f(2))

    own_map = {"rows": lambda i, me_ref: (me_ref[0] * nb + i, 0), "all": lambda i, me_ref: (i, 0)}[how]
    return pl.pallas_call(
        body,
        name=name,
        grid_spec=pltpu.PrefetchScalarGridSpec(
            num_scalar_prefetch=1,
            grid=(nb,),
            in_specs=[pl.BlockSpec((tr, C), own_map)]
            + [pl.BlockSpec((p.shape[0], tr, C), lambda i, me_ref: (0, i, 0)) for p in parts],
            out_specs=pl.BlockSpec((tr, C), lambda i, me_ref: (i, 0)),
        ),
        out_shape=jax.ShapeDtypeStruct((R, C), F32),
        compiler_params=pltpu.CompilerParams(dimension_semantics=("arbitrary",), vmem_limit_bytes=VMEM_LIMIT),
    )(me, full, *parts)


_M_CORR = 1.0 - ADAM_B1**ADAM_STEP
_V_CORR = 1.0 - ADAM_B2**ADAM_STEP


def _adamw_math(w, g, m, v):
    m = ADAM_B1 * m + (1.0 - ADAM_B1) * g
    v = ADAM_B2 * v + (1.0 - ADAM_B2) * (g * g)
    delta = -ADAM_LR * ((m / _M_CORR) / (jnp.sqrt(v / _V_CORR) + ADAM_EPS) + ADAM_WD * w)
    return delta, m, v


def _adamw(name, w, m, v, g_here, g_there, g_transposed=False):
    R, C = w.shape
    tr = _tile(R, 256, LANES if g_transposed else 8)

    def body(w_ref, m_ref, v_ref, ga_ref, gb_ref, g_ref, d_ref, nm_ref, nv_ref):
        g = ga_ref[...] + gb_ref[...]
        if g_transposed:
            g = g.T
        g_ref[...] = g
        d_ref[...], nm_ref[...], nv_ref[...] = _adamw_math(w_ref[...], g, m_ref[...], v_ref[...])

    spec = pl.BlockSpec((tr, C), lambda i: (i, 0))
    gspec = pl.BlockSpec((C, tr), lambda i: (0, i)) if g_transposed else spec
    return _call(name, body, (R // tr,), [spec] * 3 + [gspec] * 2, [spec] * 4, [jax.ShapeDtypeStruct((R, C), F32)] * 4,
                 (w, m, v, g_here, g_there))


def _adamw_on_sparsecore(name, w, m, v, g_here, g_there, after):
    R, C = w.shape
    n_groups = R // SUBLANES
    n_turns = -(-n_groups // SC_TILES)
    n_in, n_out = 5, 4

    def body(w_hbm, m_hbm, v_hbm, ga_hbm, gb_hbm, after_hbm, g_out, d_out, nm_out, nv_out, bufs, sems):
        tile = lax.axis_index("subcore") * SC_CORES + lax.axis_index("sparsecore")
        srcs = (w_hbm, m_hbm, v_hbm, ga_hbm, gb_hbm)
        dsts = (d_out, nm_out, nv_out, g_out)

        def rows(turn):
            return pl.ds((tile + turn * SC_TILES) * SUBLANES, SUBLANES)

        def loads(turn):
            slot = turn % 2
            return [pltpu.make_async_copy(srcs[q].at[rows(turn), :], bufs.at[slot, q], sems.at[slot, q]) for q in range(n_in)]

        def stores(turn):
            slot = turn % 2
            return [pltpu.make_async_copy(bufs.at[slot, q], dsts[q].at[rows(turn), :], sems.at[slot, n_in + q])
                    for q in range(n_out)]

        def when_mine(turn, fn):
            pl.when(tile + turn * SC_TILES < n_groups)(fn)

        def compute(slot):
            wb, mb, vb, gab, gbb = (bufs.at[slot, q] for q in range(n_in))

            @pl.loop(0, SUBLANES)
            def _(r):
                @pl.loop(0, C, step=SC_LANES)
                def _(i):
                    at = (r, pl.ds(i, SC_LANES))
                    g = gab[at] + gbb[at]
                    delta, new_m, new_v = _adamw_math(wb[at], g, mb[at], vb[at])
                    gab[at], wb[at], mb[at], vb[at] = g, delta, new_m, new_v

        def start_loads(turn):
            def fn():
                for cp in loads(turn):
                    cp.start()

            when_mine(turn, fn)

        start_loads(0)
        for turn in range(n_turns):
            def step(turn=turn):
                for cp in loads(turn):
                    cp.wait()
                if turn >= 1:
                    for cp in stores(turn - 1):
                        cp.wait()
                if turn + 1 < n_turns:
                    start_loads(turn + 1)
                compute(turn % 2)
                for cp in stores(turn):
                    cp.start()

            when_mine(turn, step)
        for turn in range(n_turns):
            def drain(turn=turn):
                for cp in stores(turn):
                    cp.wait()

            last_mine = jnp.logical_and(tile + turn * SC_TILES < n_groups, tile + (turn + 1) * SC_TILES >= n_groups)
            pl.when(last_mine)(drain)

    return pl.kernel(
        body,
        name=name,
        out_type=[jax.ShapeDtypeStruct((R, C), F32)] * 4,
        mesh=plsc.VectorSubcoreMesh(core_axis_name="sparsecore", subcore_axis_name="subcore"),
        scratch_types=[pltpu.VMEM((2, n_in, SUBLANES, C), F32), pltpu.SemaphoreType.DMA((2, n_in + n_out))],
        compiler_params=pltpu.CompilerParams(use_tc_tiling_on_sc=True),
    )(w, m, v, g_here, g_there, after)


class _PackLayout:
    def __init__(self, n_cc, n_grp, G, widths):
        self.dw_rows = (0, HALO)
        self.wp_rows = (HALO, HALO + G)
        self.n_cc, self.n_grp, self.G = n_cc, n_grp, G
        self.vec = {}
        r = HALO + G
        for name, width in widths:
            self.vec[name] = (r, width)
            r += width // PACK_W
        self.rows = -(-r // 8) * 8


def _pack_small(layout, dwdw, dwp, vecs):
    names = list(vecs)

    def body(*refs):
        dw_ref, wp_ref = refs[0], refs[1]
        vec_refs = refs[2 : 2 + len(names)]
        o_ref = refs[-1]
        o_ref[...] = jnp.zeros_like(o_ref)
        for j in range(layout.n_cc):
            o_ref[layout.dw_rows[0] : layout.dw_rows[1], j * LANES : (j + 1) * LANES] = dw_ref[j]
        for i in range(layout.n_grp):
            o_ref[layout.wp_rows[0] : layout.wp_rows[1], i * layout.G : (i + 1) * layout.G] = wp_ref[i]
        for name, ref in zip(names, vec_refs):
            r, width = layout.vec[name]
            for h in range(width // PACK_W):
                o_ref[r + h : r + h + 1, :] = ref[:, h * PACK_W : (h + 1) * PACK_W]

    return pl.pallas_call(
        body,
        name="pack_small",
        out_shape=jax.ShapeDtypeStruct((layout.rows, PACK_W), F32),
    )(dwdw, dwp, *[vecs[k] for k in names])


def _adamw_small(layout, g_here, g_there, w_dw, m_dw, v_dw, w_pool, m_pool, v_pool, vec_w, vec_m, vec_v):
    names = list(vec_w)
    nv = len(names)

    def body(*refs):
        ga_ref, gb_ref = refs[0], refs[1]
        wdw, mdw, vdw, wp, mp, vp = refs[2:8]
        vw, vm, vv = refs[8 : 8 + nv], refs[8 + nv : 8 + 2 * nv], refs[8 + 2 * nv : 8 + 3 * nv]
        outs = refs[8 + 3 * nv :]
        acc = outs[-1]
        acc[...] = ga_ref[...] + gb_ref[...]

        def emit(o, g, w, m, v, idx=()):
            res = (g,) + _adamw_math(w, g, m, v)
            for ref, val in zip(o, res):
                ref[idx] = val

        me = 2 * lax.axis_index("x") + lax.axis_index("y")
        for j in range(layout.n_cc):

            @pl.when(me == j)
            def _(j=j):
                for k in range(wdw.shape[0]):
                    g = acc[layout.dw_rows[0] + k : layout.dw_rows[0] + k + 1, j * LANES : (j + 1) * LANES]
                    emit(outs[0:4], g, wdw[k], mdw[k], vdw[k], idx=k)

        for i in range(layout.n_grp):
            g = acc[layout.wp_rows[0] : layout.wp_rows[1], i * layout.G : (i + 1) * layout.G]
            emit(outs[4:8], g, wp[i], mp[i], vp[i], idx=i)
        for q, name in enumerate(names):
            r, width = layout.vec[name]
            for h in range(width // PACK_W):
                ls = slice(h * PACK_W, (h + 1) * PACK_W)
                g = acc[r + h : r + h + 1, :]
                emit(outs[8 + 4 * q : 12 + 4 * q], g, vw[q][:, ls], vm[q][:, ls], vv[q][:, ls], idx=(slice(None), ls))

    shapes = [w_dw.shape] * 4 + [w_pool.shape] * 4
    for name in names:
        shapes += [vec_w[name].shape] * 4
    return pl.pallas_call(
        body,
        name="adamw_small",
        out_shape=[jax.ShapeDtypeStruct(s, F32) for s in shapes],
        scratch_shapes=[pltpu.VMEM(g_here.shape, F32)],
    )(g_here, g_there, w_dw, m_dw, v_dw, w_pool, m_pool, v_pool,
      *[vec_w[k] for k in names], *[vec_m[k] for k in names], *[vec_v[k] for k in names])


def _allreduce_adamw_row(g_part, w, m, v, loss_part, comm=()):
    D = w.shape[1]
    n_pairs = N_DEV - 1

    def body(g_ref, w_ref, m_ref, v_ref, l_ref, go_ref, d_ref, nm_ref, nv_ref, lo_ref, land_g, land_l, sems):
        x, y, c = _place()
        copies = []
        for q, (src, land) in enumerate(((g_ref, land_g), (l_ref, land_l))):
            for r in range(1, N_DEV):
                fx, fy, fc = (r >> 2) & 1, (r >> 1) & 1, r & 1
                peer = (1 - x if fx else x, 1 - y if fy else y, 1 - c if fc else c)
                cp = _remote(src, land.at[r], sems, 2 * (q * n_pairs + r - 1), peer)
                cp.start()
                copies.append(cp)
        for cp in copies:
            cp.wait()

        def total(src, land):
            row = lambda r: src[...] if r == 0 else land[r]
            return ((row(0) + row(4)) + (row(2) + row(6))) + ((row(1) + row(5)) + (row(3) + row(7)))

        g = total(g_ref, land_g)
        go_ref[...] = g
        d_ref[...], nm_ref[...], nv_ref[...] = _adamw_math(w_ref[...], g, m_ref[...], v_ref[...])
        lo_ref[...] = total(l_ref, land_l)

    vm = pl.BlockSpec(memory_space=pltpu.VMEM)
    return _call(
        "allreduce_adamw_g_mix",
        body,
        (),
        [vm] * 5,
        [vm] * 5,
        [jax.ShapeDtypeStruct((1, D), F32)] * 4 + [jax.ShapeDtypeStruct(loss_part.shape, F32)],
        (g_part, w, m, v, loss_part),
        scratch=[pltpu.VMEM((N_DEV, 1, D), F32), pltpu.VMEM((N_DEV,) + loss_part.shape, F32),
                 pltpu.SemaphoreType.DMA((4 * n_pairs,))],
        comm=comm,
    )


def kernel(x, g_mix, w_in, b_in, w_dw, b_dw, ln_g, ln_b, w_pool, s_pool, w_out, g_ffn, w_gate, w_up, w_down, g_final, loss_target, m_g_mix, m_w_in, m_b_in, m_w_dw, m_b_dw, m_ln_g, m_ln_b, m_w_pool, m_s_pool, m_w_out, m_g_ffn, m_w_gate, m_w_up, m_w_down, m_g_final, v_g_mix, v_w_in, v_b_in, v_w_dw, v_b_dw, v_ln_g, v_ln_b, v_w_pool, v_s_pool, v_w_out, v_g_ffn, v_w_gate, v_w_up, v_w_down, v_g_final):
    x2 = x[0]
    target = loss_target[0]
    T, D = x2.shape
    w_in2, w_out2, w_down2 = w_in[0], w_out[0], w_down[0]
    taps_first = lambda a: jnp.transpose(a, (1, 0, 2))
    w_dw3 = taps_first(w_dw)
    w_gateT, w_upT = w_gate[0].T, w_up[0].T
    CI = w_in2.shape[1] * N_CHIPS
    DM = w_out2.shape[0] * N_CHIPS
    F = w_down2.shape[0] * N_CHIPS
    KW, _, dw_cols = w_dw3.shape
    assert dw_cols == LANES
    n_grp, G = w_pool.shape[1], w_pool.shape[-1]
    w_pool3 = w_pool[0]
    g_final2 = g_final.reshape(1, D)

    me = (2 * lax.axis_index("x") + lax.axis_index("y")).astype(jnp.int32).reshape(1)

    w_inT_b, w_dw4, f_out, f_gate, f_up, f_down = _place_and_gather(
        [(w_in2, "rows", (CI, D), BF16, True, True), (w_dw3, "lead", (N_CHIPS, KW, 1, dw_cols), F32, False, False)],
        [(w, "rows", shape, BF16, False, True)
         for w, shape in ((w_out2, (DM, D)), (w_gateT, (F, D)), (w_upT, (F, D)), (w_down2, (F, D)))])
    w_pool_b = w_pool3.astype(BF16)
    ici = lambda f: _GatherIci([f], ["rows"], [True])
    d2d = lambda f: _GatherD2d([f], ["rows"])
    gather = _start("gather_start", [ici(f_out), ici(f_gate), ici(f_up), ici(f_down)])
    (z, xn_b), _ = _in_proj(x2, g_mix, w_inT_b, b_in, after=[gather.token])
    (f_out,) = _wait("gather_out_wait", gather, 0, xn_b)
    s_out = _start("share_out_start", [d2d(f_out)], sibling_only=True)
    (y_b, v), _ = _seq_fwd(z, w_dw4, b_dw, ln_g, ln_b, w_pool_b, s_pool, after=[s_out.token])
    (w_out_b,) = _wait("share_out_wait", s_out, 0, y_b)
    (f_gate,) = _wait("gather_gate_wait", gather, 1, y_b)
    s_gate = _start("share_gate_start", [d2d(f_gate)], sibling_only=True)
    (h1, hn_b), _ = _out_proj(y_b, x2, w_out_b, g_ffn, after=[s_gate.token])
    (f_up,) = _wait("gather_up_wait", gather, 2, hn_b)
    s_up = _start("share_up_start", [d2d(f_up)], sibling_only=True)
    (wgT_b,) = _wait("share_gate_wait", s_gate, 0, hn_b)
    (wuT_b,) = _wait("share_up_wait", s_up, 0, hn_b)
    (silu_b, uds_b, a_b), _ = _gate_up(hn_b, wgT_b, wuT_b)
    (f_down,) = _wait("gather_down_wait", gather, 3, a_b)
    s_down = _start("share_down_start", [d2d(f_down)], sibling_only=True)
    (wd_b,) = _wait("share_down_wait", s_down, 0, a_b)
    (dh2, dh2_b, loss_part, d_g_final), _ = _down_loss(a_b, wd_b, h1, target, g_final2)

    gw_down, _ = _weight_grad("grad_w_down", a_b, dh2_b)
    x_down = _start("scatter_down_start", [_Scatter([gw_down], ["rows"])])
    (dg_b, du_b), _ = _ffn_bwd_act(dh2_b, wd_b, silu_b, uds_b, after=[x_down.token])
    gw_gateT, _ = _weight_grad("grad_w_gate", dg_b, hn_b)
    gw_upT, _ = _weight_grad("grad_w_up", du_b, hn_b)
    gw_down, p_down = _wait("scatter_down_wait", x_down, 0, gw_upT)
    sum_down = _sum_parts("sum_w_down", gw_down, "rows", [p_down], me)
    (dh1, dh1_b, dy, d_g_ffn), (p_gate, oth_down) = _ffn_bwd_in(
        dg_b, du_b, wgT_b, wuT_b, h1, dh2, g_ffn, w_out_b, comm=[_Scatter([gw_gateT], ["rows"]), _Swap([sum_down])])
    gw_out, _ = _weight_grad("grad_w_out", y_b, dh1_b)
    sum_gate = _sum_parts("sum_w_gate", gw_gateT, "rows", [p_gate], me)
    res = {}
    res["w_down"] = _adamw_on_sparsecore("adamw_w_down", w_down2, m_w_down[0], v_w_down[0], sum_down, oth_down, sum_down)
    (dz_b, d_wdw, d_bdw, d_lng, d_lnb, d_wp, d_sp, d_bin), (p_up, p_out, oth_gate) = _seq_bwd(
        z, dy, v, w_dw4, ln_g, ln_b, w_pool_b, s_pool,
        comm=[_Scatter([gw_upT, gw_out], ["rows", "rows"]), _Swap([sum_gate])])
    res["w_gate"] = _adamw_on_sparsecore(
        "adamw_w_gate", w_gateT, m_w_gate[0].T, v_w_gate[0].T, sum_gate, oth_gate, res["w_down"][0])
    vec_grads ={"b_dw": d_bdw, "ln_g": d_lng, "ln_b": d_lnb, "s_pool": d_sp, "g_ffn": d_g_ffn, "g_final": d_g_final, "b_in": d_bin}
    layout = _PackLayout(dw_cols * N_CHIPS // LANES, n_grp, G, [(k, a.shape[1]) for k, a in vec_grads.items()])
    pack = _pack_small(layout, d_wdw, d_wp, vec_grads)
    sum_up = _sum_parts("sum_w_up", gw_upT, "rows", [p_up], me)
    sum_out = _sum_parts("sum_w_out", gw_out, "rows", [p_out], me)
    gw_inT, (oth_up, oth_out) = _weight_grad("grad_w_in", dz_b, xn_b, comm=[_Swap([sum_up, sum_out])], after=[pack])
    late = _start("late_start", [_Scatter([gw_inT], ["rows"]), _Scatter([pack], ["all"])])
    (grad_x, d_g_mix), _ = _in_proj_bwd(dz_b, w_inT_b, x2, dh1, g_mix, after=[late.token])
    gw_inT, p_in = _wait("late_w_in_wait", late, 0, d_g_mix)
    pack, p_small = _wait("late_small_wait", late, 1, d_g_mix)
    sum_small = _sum_parts("sum_small", pack, "all", [p_small], me)
    res["w_up"] = _adamw_on_sparsecore("adamw_w_up", w_upT, m_w_up[0].T, v_w_up[0].T, sum_up, oth_up, res["w_gate"][0])
    res["w_out"] = _adamw_on_sparsecore("adamw_w_out", w_out2, m_w_out[0], v_w_out[0], sum_out, oth_out, res["w_gate"][0])
    sum_in = _sum_parts("sum_w_in", gw_inT, "rows", [p_in], me)
    (*res["g_mix"], loss_row), (oth_in, oth_small) = _allreduce_adamw_row(
        d_g_mix, g_mix, m_g_mix, v_g_mix, loss_part, comm=[_Swap([sum_in, sum_small])])
    loss = loss_row[0, 0]
    res["w_in"], _ = _adamw("adamw_w_in", w_in2, m_w_in[0], v_w_in[0], sum_in, oth_in, g_transposed=True)

    vec_w = {"b_dw": b_dw, "ln_g": ln_g, "ln_b": ln_b, "s_pool": s_pool, "g_ffn": g_ffn, "g_final": g_final2, "b_in": b_in}
    vec_m = {"b_dw": m_b_dw, "ln_g": m_ln_g, "ln_b": m_ln_b, "s_pool": m_s_pool, "g_ffn": m_g_ffn,
             "g_final": m_g_final.reshape(1, D), "b_in": m_b_in}
    vec_v = {"b_dw": v_b_dw, "ln_g": v_ln_g, "ln_b": v_ln_b, "s_pool": v_s_pool, "g_ffn": v_g_ffn,
             "g_final": v_g_final.reshape(1, D), "b_in": v_b_in}
    small = _adamw_small(layout, sum_small, oth_small, w_dw3, taps_first(m_w_dw), taps_first(v_w_dw),
                         w_pool3, m_w_pool[0], v_w_pool[0], vec_w, vec_m, vec_v)
    res["w_dw"] = [taps_first(a) for a in small[0:4]]
    res["w_pool"] = [a[None] for a in small[4:8]]
    for q, k in enumerate(vec_w):
        res[k] = list(small[8 + 4 * q : 12 + 4 * q])
    res["g_final"] = [a.reshape(D) for a in res["g_final"]]
    for k in ("w_in", "w_out", "w_down"):
        res[k] = [a[None] for a in res[k]]
    for k in ("w_gate", "w_up"):
        res[k] = [a.T[None] for a in res[k]]

    order = ["g_mix", "w_in", "b_in", "w_dw", "b_dw", "ln_g", "ln_b", "w_pool", "s_pool", "w_out", "g_ffn", "w_gate", "w_up", "w_down", "g_final"]
    outs = [loss, grad_x[None]]
    for q in range(4):
        outs += [res[k][q] for k in order]
    return tuple(outs)
```

```python
import jax
import jax.numpy as jnp
from jax import lax
from jax.experimental import pallas as pl
from jax.experimental.pallas import tpu as pltpu
from jax.experimental.pallas import tpu_sc as plsc

F32 = jnp.float32
BF16 = jnp.bfloat16
MESH = pl.DeviceIdType.MESH
ANY = pl.BlockSpec(memory_space=pl.ANY)

RMS_EPS = 1e-6
LN_EPS = 1e-5
POOL_WINDOWS = (2, 4, 8, 16)
ADAM_LR = 0.001
ADAM_B1 = 0.9
ADAM_B2 = 0.999
ADAM_EPS = 1e-08
ADAM_WD = 0.01
ADAM_STEP = 10

LANES = 128
SUBLANES = 8
BF16_ROWS = 16
HALO = 32
CONV_ROWS = 64
HIDDEN_CHUNK = 512
VMEM_LIMIT = 56 * 1024 * 1024
PACK_W = 512
N_CHIPS = 4
N_DEV = 8
SIBLING_BARRIER_ID = 0
SC_CORES = 2
SC_TILES = 32
SC_LANES = 16


def _tile(n, want, mult=8):
    t = min(n, want)
    while n % t or t % mult:
        t -= 1
    return t


def _sigmoid(x):
    return 1.0 / (1.0 + jnp.exp(-x))


def _dot(a, b, dims):
    return lax.dot_general(a, b, (dims, ((), ())), preferred_element_type=F32)


NN = ((1,), (0,))
NT = ((1,), (1,))
TN = ((0,), (0,))


def _rms_bwd(x, g, dy):
    r = lax.rsqrt(jnp.mean(x * x, axis=-1, keepdims=True) + RMS_EPS)
    xh = x * r
    gy = dy * g
    dx = r * (gy - xh * jnp.mean(gy * xh, axis=-1, keepdims=True))
    return dx, dy * xh


def _accumulate(ref, first, val):
    @pl.when(first)
    def _():
        ref[...] = val

    @pl.when(jnp.logical_not(first))
    def _():
        ref[...] += val


def _place():
    return lax.axis_index("x"), lax.axis_index("y"), lax.axis_index("c")


def _other_chips(x, y):
    return [(1 - x, y), (x, 1 - y), (1 - x, 1 - y)]


def _rows(ref, start, n):
    return ref.at[pl.ds(pl.multiple_of(start, BF16_ROWS), n)]


def _window(ref, how, k, c=None):
    if how == "all":
        return ref
    if how == "lead":
        return ref.at[k]
    assert how == "rows"
    n = ref.shape[0] // N_CHIPS
    if c is None:
        return _rows(ref, k * n, n)
    return _rows(ref, k * n + c * (n // 2), n // 2)


def _remote(src, dst, sems, s, device):
    return pltpu.make_async_remote_copy(
        src_ref=src, dst_ref=dst, send_sem=sems.at[s], recv_sem=sems.at[s + 1], device_id=device, device_id_type=MESH)


class _GatherIci:
    aliased = True

    def __init__(self, fulls, hows, splits, which=(0, 1, 2)):
        self.fulls, self.hows, self.splits, self.which = list(fulls), list(hows), list(splits), tuple(which)

    def inputs(self):
        return self.fulls

    def out_shapes(self):
        return [jax.ShapeDtypeStruct(a.shape, a.dtype) for a in self.fulls]

    def n_sems(self):
        return 6 * len(self.fulls)

    def build(self, ins, outs, sems, base):
        x, y, c = _place()
        me = 2 * x + y
        chips = _other_chips(x, y)
        starts, waits = [], []
        for a, (how, sp) in enumerate(zip(self.hows, self.splits)):
            half = c if sp else None
            mine = _window(outs[a], how, me, half)
            for j in self.which:
                px, py = chips[j]
                s = base + 6 * a + 2 * j
                cp = _remote(mine, mine, sems, s, (px, py, c))
                landing = _remote(mine, _window(outs[a], how, 2 * px + py, half), sems, s, (px, py, c))
                starts.append(cp.start)
                waits += [landing.wait_recv, cp.wait_send]
        return starts, waits


class _GatherD2d:
    aliased = True

    def __init__(self, fulls, hows):
        self.fulls, self.hows = list(fulls), list(hows)

    def inputs(self):
        return self.fulls

    def out_shapes(self):
        return [jax.ShapeDtypeStruct(a.shape, a.dtype) for a in self.fulls]

    def n_sems(self):
        return 6 * len(self.fulls)

    def build(self, ins, outs, sems, base):
        x, y, c = _place()
        starts, waits = [], []
        for a, how in enumerate(self.hows):
            for j, (px, py) in enumerate(_other_chips(x, y)):
                s = base + 6 * a + 2 * j
                got = _window(outs[a], how, 2 * px + py, c)
                cp = _remote(got, got, sems, s, (x, y, 1 - c))
                landing = _remote(got, _window(outs[a], how, 2 * px + py, 1 - c), sems, s, (x, y, 1 - c))
                starts.append(cp.start)
                waits += [landing.wait_recv, cp.wait_send]
        return starts, waits


def _part_shape(a, how):
    if how == "all":
        return a.shape
    assert how == "rows"
    return (a.shape[0] // N_CHIPS, a.shape[1])


class _Scatter:
    aliased = False

    def __init__(self, fulls, hows, which=(0, 1, 2)):
        self.fulls, self.hows, self.which = list(fulls), list(hows), tuple(which)

    def inputs(self):
        return self.fulls

    def out_shapes(self):
        return [jax.ShapeDtypeStruct((len(self.which),) + _part_shape(a, h), a.dtype) for a, h in zip(self.fulls, self.hows)]

    def n_sems(self):
        return 6 * len(self.fulls)

    def build(self, ins, outs, sems, base):
        x, y, c = _place()
        chips = _other_chips(x, y)
        starts, waits = [], []
        for a, how in enumerate(self.hows):
            for slot, j in enumerate(self.which):
                px, py = chips[j]
                cp = _remote(_window(ins[a], how, 2 * px + py), outs[a].at[slot], sems, base + 6 * a + 2 * j, (px, py, c))
                starts.append(cp.start)
                waits += [cp.wait_recv, cp.wait_send]
        return starts, waits


class _Swap:
    aliased = False

    def __init__(self, arrays):
        self.arrays = list(arrays)

    def inputs(self):
        return self.arrays

    def out_shapes(self):
        return [jax.ShapeDtypeStruct(a.shape, a.dtype) for a in self.arrays]

    def n_sems(self):
        return 2 * len(self.arrays)

    def build(self, ins, outs, sems, base):
        x, y, c = _place()
        starts, waits = [], []
        for a in range(len(ins)):
            cp = _remote(ins[a], outs[a], sems, base + 2 * a, (x, y, 1 - c))
            starts.append(cp.start)
            waits += [cp.wait_recv, cp.wait_send]
        return starts, waits


def _call(name, body, grid, in_specs, out_specs, out_shape, args, scratch=(), comm=(), after=()):
    comm, after = list(comm), list(after)
    n_in, n_out, n_scr, n_after = len(args), len(out_shape), len(scratch), len(after)
    c_in = [a for op in comm for a in op.inputs()]
    c_out = [s for op in comm for s in op.out_shapes()]
    n_sems = sum(op.n_sems() for op in comm)
    aliases, i_in, i_out = {}, 0, 0
    for op in comm:
        if op.aliased:
            for q in range(len(op.inputs())):
                aliases[n_in + n_after + i_in + q] = n_out + i_out + q
        i_in, i_out = i_in + len(op.inputs()), i_out + len(op.out_shapes())

    def wrapped(*refs):
        ins = refs[:n_in]
        cin = refs[n_in + n_after : n_in + n_after + len(c_in)]
        o0 = n_in + n_after + len(c_in)
        outs = refs[o0 : o0 + n_out]
        cout = refs[o0 + n_out : o0 + n_out + len(c_out)]
        s0 = o0 + n_out + len(c_out)
        scr = refs[s0 : s0 + n_scr]

        def copies():
            sems = refs[s0 + n_scr]
            starts, waits = [], []
            i_in = i_out = base = 0
            for op in comm:
                ni, no = len(op.inputs()), len(op.out_shapes())
                s, w = op.build(cin[i_in : i_in + ni], cout[i_out : i_out + no], sems, base)
                starts += s
                waits += w
                i_in, i_out, base = i_in + ni, i_out + no, base + op.n_sems()
            return starts, waits

        def run_starts():
            for start in copies()[0]:
                start()

        def run_waits():
            for wait in copies()[1]:
                wait()

        if comm and grid:
            first = last = True
            for d, n in enumerate(grid):
                first = jnp.logical_and(first, pl.program_id(d) == 0)
                last = jnp.logical_and(last, pl.program_id(d) == n - 1)
            pl.when(first)(run_starts)
        elif comm:
            run_starts()
        if body is not None:
            body(*ins, *outs, *scr)
        if comm and grid:
            pl.when(last)(run_waits)
        elif comm:
            run_waits()

    res = pl.pallas_call(
        wrapped,
        name=name,
        grid=grid,
        in_specs=list(in_specs) + [ANY] * (n_after + len(c_in)),
        out_specs=list(out_specs) + [ANY] * len(c_out),
        out_shape=list(out_shape) + c_out,
        scratch_shapes=list(scratch) + ([pltpu.SemaphoreType.DMA((n_sems,))] if comm else []),
        input_output_aliases=aliases,
        compiler_params=pltpu.CompilerParams(dimension_semantics=("arbitrary",) * len(grid), vmem_limit_bytes=VMEM_LIMIT),
    )(*args, *after, *c_in)
    return tuple(res[:n_out]), tuple(res[n_out:])


def _place_and_gather(now, later):
    items = list(now) + list(later)
    n, n_now = len(items), len(now)
    buf_shape = lambda it: it[0].shape[::-1] if it[4] else it[0].shape
    split_now = [a for a in range(n_now) if items[a][5]]

    def body(*refs):
        ins, outs = refs[:n], refs[n : 2 * n]
        stage, bufs = refs[2 * n : 3 * n - n_now], refs[3 * n - n_now : 4 * n - n_now]
        sems = refs[4 * n - n_now]
        x, y, c = _place()
        me = 2 * x + y
        chips = _other_chips(x, y)
        loads = [pltpu.make_async_copy(ins[a], stage[a - n_now], sems.at[a]) for a in range(n_now, n)]
        for ld in loads:
            ld.start()
        pending = []

        def place(a, val):
            _, how, _, dtype, transposed, _ = items[a]
            bufs[a][...] = (val.T if transposed else val).astype(dtype)
            cp = pltpu.make_async_copy(bufs[a], _window(outs[a], how, me), sems.at[n + a])
            cp.start()
            pending.append(cp.wait)

        arrivals = []
        for a in range(n_now):
            place(a, ins[a][...])
            how, split = items[a][1], items[a][5]
            half = c if split else None
            src = _rows(bufs[a], c * (bufs[a].shape[0] // 2), bufs[a].shape[0] // 2) if split else bufs[a]
            for j, (px, py) in enumerate(chips):
                s = 2 * n + 6 * a + 2 * j
                cp = _remote(src, _window(outs[a], how, me, half), sems, s, (px, py, c))
                landing = _remote(src, _window(outs[a], how, 2 * px + py, half), sems, s, (px, py, c))
                cp.start()
                arrivals.append(landing.wait_recv)
                pending.append(cp.wait_send)
        for a in range(n_now, n):
            loads[a - n_now].wait()
            place(a, stage[a - n_now][...])
        for wait in arrivals:
            wait()
        d2d = _GatherD2d([None] * len(split_now), [items[a][1] for a in split_now])
        starts, waits = d2d.build(None, [outs[a] for a in split_now], sems, 2 * n + 6 * n_now)
        for start in starts:
            start()
        for wait in waits + pending:
            wait()

    vm = pl.BlockSpec(memory_space=pltpu.VMEM)
    return pl.pallas_call(
        body,
        name="place_and_gather",
        in_specs=[vm] * n_now + [ANY] * (n - n_now),
        out_specs=[ANY] * n,
        out_shape=[jax.ShapeDtypeStruct(it[2], it[3]) for it in items],
        scratch_shapes=[pltpu.VMEM(it[0].shape, it[0].dtype) for it in later]
        + [pltpu.VMEM(buf_shape(it), it[3]) for it in items]
        + [pltpu.SemaphoreType.DMA((2 * n + 6 * n_now + 6 * len(split_now),))],
        compiler_params=pltpu.CompilerParams(vmem_limit_bytes=VMEM_LIMIT),
    )(*[it[0] for it in items])


_HBM = pl.BlockSpec(memory_space=pltpu.HBM)
_SEM = pl.BlockSpec(memory_space=pltpu.SEMAPHORE)
_DATAFLOW = pltpu.SideEffectType.DATAFLOW_SIDE_EFFECTING


class _Pending:
    def __init__(self, ops, bases, sems, arrays, token):
        self.ops, self.bases, self.sems, self.arrays, self.token = ops, bases, sems, arrays, token


def _op_refs(op, refs):
    n_src = len(op.inputs())
    return refs[:n_src], (refs[:n_src] if op.aliased else refs[n_src:])


def _start(name, ops, sibling_only=False):
    per_op = [list(op.inputs()) + ([] if op.aliased else [lax.empty(sd.shape, sd.dtype) for sd in op.out_shapes()])
              for op in ops]
    arrays = [a for group in per_op for a in group]
    bases = [sum(op.n_sems() for op in ops[:k]) for k in range(len(ops))]
    n = len(arrays)

    def body(*refs):
        sems, token = refs[n], refs[-1]
        if sibling_only:
            x, y, c = _place()
            barrier = pltpu.get_barrier_semaphore()
            pl.semaphore_signal(barrier, inc=1, device_id=(x, y, 1 - c), device_id_type=MESH)
            pl.semaphore_wait(barrier, 1)
        at = 0
        for op, group, base in zip(ops, per_op, bases):
            starts, _ = op.build(*_op_refs(op, refs[at : at + len(group)]), sems, base)
            for start in starts:
                start()
            at += len(group)
        token[...] = jnp.zeros_like(token)

    res = pl.pallas_call(
        body,
        name=name,
        out_shape=(pltpu.SemaphoreType.DMA((sum(op.n_sems() for op in ops),)),)
        + tuple(pltpu.HBM(a.shape, a.dtype) for a in arrays) + (jax.ShapeDtypeStruct((SUBLANES, LANES), F32),),
        in_specs=(_HBM,) * n,
        out_specs=(_SEM,) + (_HBM,) * n + (pl.BlockSpec(memory_space=pltpu.VMEM),),
        input_output_aliases={i: 1 + i for i in range(n)},
        compiler_params=pltpu.CompilerParams(
            has_side_effects=_DATAFLOW, collective_id=SIBLING_BARRIER_ID if sibling_only else None),
    )(*[pltpu.with_memory_space_constraint(a, pltpu.HBM) for a in arrays])
    thru, at, groups = list(res[1 : 1 + n]), 0, []
    for group in per_op:
        groups.append(thru[at : at + len(group)])
        at += len(group)
    return _Pending(list(ops), bases, res[0], groups, res[-1])


def _wait(name, pending, k, after):
    op, arrays = pending.ops[k], pending.arrays[k]
    n = len(arrays)

    def body(*refs):
        _, waits = op.build(*_op_refs(op, refs[:n]), refs[n], pending.bases[k])
        for wait in waits:
            wait()

    return pl.pallas_call(
        body,
        name=name,
        out_shape=tuple(pltpu.HBM(a.shape, a.dtype) for a in arrays),
        in_specs=(_HBM,) * n + (_SEM, ANY),
        out_specs=(_HBM,) * n,
        input_output_aliases={i: i for i in range(n)},
        compiler_params=pltpu.CompilerParams(has_side_effects=_DATAFLOW),
    )(*arrays, pending.sems, after)


def _in_proj(x, g_mix, w_inT_b, b_in, after=()):
    T, D = x.shape
    CI = w_inT_b.shape[0]
    tm = _tile(T, 512)

    def body(x_ref, g_ref, w_ref, b_ref, z_ref, xn_ref):
        xv = x_ref[...]
        r = lax.rsqrt(jnp.mean(xv * xv, axis=-1, keepdims=True) + RMS_EPS)
        xn = (xv * r * g_ref[...]).astype(BF16)
        xn_ref[...] = xn
        z_ref[...] = _dot(xn, w_ref[...], NT) + b_ref[...]

    return _call(
        "in_proj",
        body,
        (T // tm,),
        [
            pl.BlockSpec((tm, D), lambda i: (i, 0)),
            pl.BlockSpec((1, D), lambda i: (0, 0)),
            pl.BlockSpec((CI, D), lambda i: (0, 0)),
            pl.BlockSpec((1, CI), lambda i: (0, 0)),
        ],
        [pl.BlockSpec((tm, CI), lambda i: (i, 0)), pl.BlockSpec((tm, D), lambda i: (i, 0))],
        [jax.ShapeDtypeStruct((T, CI), F32), jax.ShapeDtypeStruct((T, D), BF16)],
        (x, g_mix, w_inT_b, b_in),
        after=after,
    )


def _fill_shifted(scr):
    n = scr.shape[1] - SUBLANES
    for s in range(1, SUBLANES):
        scr[s, 0:n, :] = scr[0, s : s + n, :]


def _shifted_rows(scr, off, n, cs):
    s = off % SUBLANES
    return scr[s, off - s : off - s + n, cs]


def _pool_mean_minus_token(p_scr, cs, w, cnt, tt):
    tok = p_scr[HALO : HALO + tt, cs]
    s = tok
    for d in range(1, w):
        s = s + p_scr[HALO - d : HALO - d + tt, cs]
    return s / cnt - tok


def _seq_fwd(z, w_dw4, b_dw, ln_g, ln_b, w_pool_b, s_pool, after=()):
    T, CI = z.shape
    CC = ln_g.shape[1]
    n_grp, G = w_pool_b.shape[0], w_pool_b.shape[-1]
    KW = w_dw4.shape[1]
    D = CC + n_grp * G
    tt = _tile(T, 512, HALO)
    per = tt // HALO

    def body(zc_ref, zp_ref, wdw_ref, bdw_ref, lng_ref, lnb_ref, wp_ref, sp_ref, y_ref, v_ref, u_scr, p_scr):
        i = pl.program_id(0)
        first = i == 0
        u_prev = zp_ref[:, 0:CC] * _sigmoid(zp_ref[:, CC : 2 * CC])
        u_scr[0, 0:HALO, :] = jnp.where(first, 0.0, u_prev)
        p_scr[0:HALO, :] = jnp.where(first, 0.0, zp_ref[:, 2 * CC :])
        u_scr[0, HALO:, :] = zc_ref[:, 0:CC] * _sigmoid(zc_ref[:, CC : 2 * CC])
        p_scr[HALO:, :] = zc_ref[:, 2 * CC :]
        _fill_shifted(u_scr)

        for j in range(CC // LANES):
            cs = slice(LANES * j, LANES * (j + 1))
            for rb in range(tt // CONV_ROWS):
                acc = jnp.zeros((CONV_ROWS, LANES), F32)
                for k in range(KW):
                    off = HALO - (KW - 1) + k + rb * CONV_ROWS
                    acc = acc + _shifted_rows(u_scr, off, CONV_ROWS, cs) * wdw_ref[j, k]
                v_ref[rb * CONV_ROWS : (rb + 1) * CONV_ROWS, cs] = acc + bdw_ref[:, cs]

        v = v_ref[...]
        mu = jnp.mean(v, axis=-1, keepdims=True)
        d = v - mu
        var = jnp.mean(d * d, axis=-1, keepdims=True)
        ln = d * lax.rsqrt(var + LN_EPS) * lng_ref[...] + lnb_ref[...]
        y_ref[:, 0:CC] = (ln * _sigmoid(ln)).astype(BF16)

        tpos = i * tt + lax.broadcasted_iota(jnp.int32, (tt, 1), 0)
        for gi, w in enumerate(POOL_WINDOWS):
            cs = slice(G * gi, G * (gi + 1))
            cnt = jnp.minimum(tpos + 1, w).astype(F32)
            yi = _pool_mean_minus_token(p_scr, cs, w, cnt, tt)
            q = _dot(yi.astype(BF16), wp_ref[gi], NN)
            y_ref[:, CC + G * gi : CC + G * (gi + 1)] = (q * sp_ref[:, cs]).astype(BF16)

    const2 = lambda i: (0, 0)
    return _call(
        "seq_fwd",
        body,
        (T // tt,),
        [
            pl.BlockSpec((tt, CI), lambda i: (i, 0)),
            pl.BlockSpec((HALO, CI), lambda i: (jnp.maximum(i * per - 1, 0), 0)),
            pl.BlockSpec(w_dw4.shape, lambda i: (0,) * w_dw4.ndim),
            pl.BlockSpec((1, CC), const2),
            pl.BlockSpec((1, CC), const2),
            pl.BlockSpec((1, CC), const2),
            pl.BlockSpec(w_pool_b.shape, lambda i: (0, 0, 0)),
            pl.BlockSpec((1, n_grp * G), const2),
        ],
        [pl.BlockSpec((tt, D), lambda i: (i, 0)), pl.BlockSpec((tt, CC), lambda i: (i, 0))],
        [jax.ShapeDtypeStruct((T, D), BF16), jax.ShapeDtypeStruct((T, CC), F32)],
        (z, z, w_dw4, b_dw, ln_g, ln_b, w_pool_b, s_pool),
        scratch=[pltpu.VMEM((SUBLANES, HALO + tt, CC), F32), pltpu.VMEM((HALO + tt, n_grp * G), F32)],
        after=after,
    )


def _out_proj(y_b, x, w_out_b, g_ffn, after=()):
    T, D = x.shape
    tm = _tile(T, 512)

    def body(y_ref, x_ref, w_ref, g_ref, h1_ref, hn_ref):
        h1 = x_ref[...] + _dot(y_ref[...], w_ref[...], NN)
        h1_ref[...] = h1
        r = lax.rsqrt(jnp.mean(h1 * h1, axis=-1, keepdims=True) + RMS_EPS)
        hn_ref[...] = (h1 * r * g_ref[...]).astype(BF16)

    row = lambda i: (i, 0)
    return _call(
        "out_proj",
        body,
        (T // tm,),
        [
            pl.BlockSpec((tm, y_b.shape[1]), row),
            pl.BlockSpec((tm, D), row),
            pl.BlockSpec(w_out_b.shape, lambda i: (0, 0)),
            pl.BlockSpec((1, D), lambda i: (0, 0)),
        ],
        [pl.BlockSpec((tm, D), row), pl.BlockSpec((tm, D), row)],
        [jax.ShapeDtypeStruct((T, D), F32), jax.ShapeDtypeStruct((T, D), BF16)],
        (y_b, x, w_out_b, g_ffn),
        after=after,
    )


def _hidden_tile(F):
    return _tile(F, 1408, LANES)


def _gate_up(hn_b, wgT_b, wuT_b):
    T, D = hn_b.shape
    F = wgT_b.shape[0]
    tm, tf = _tile(T, 1024), _hidden_tile(F)

    def body(hn_ref, wg_ref, wu_ref, silu_ref, uds_ref, a_ref):
        hn = hn_ref[...]
        for c0 in range(0, tf, HIDDEN_CHUNK):
            cs = slice(c0, min(c0 + HIDDEN_CHUNK, tf))
            gv = _dot(hn, wg_ref[cs, :], NT)
            uv = _dot(hn, wu_ref[cs, :], NT)
            sg = _sigmoid(gv)
            silu = gv * sg
            silu_ref[:, cs] = silu.astype(BF16)
            uds_ref[:, cs] = (uv * (sg * (1.0 + gv * (1.0 - sg)))).astype(BF16)
            a_ref[:, cs] = (silu * uv).astype(BF16)

    wspec = pl.BlockSpec((tf, D), lambda j, i: (j, 0))
    ospec = pl.BlockSpec((tm, tf), lambda j, i: (i, j))
    return _call(
        "gate_up",
        body,
        (F // tf, T // tm),
        [pl.BlockSpec((tm, D), lambda j, i: (i, 0)), wspec, wspec],
        [ospec, ospec, ospec],
        [jax.ShapeDtypeStruct((T, F), BF16)] * 3,
        (hn_b, wgT_b, wuT_b),
    )


def _down_loss(a_b, wd_b, h1, target, g_final):
    T, D = h1.shape
    F = a_b.shape[1]
    tm = _tile(T, 512)
    nt = T // tm

    def body(a_ref, w_ref, h1_ref, t_ref, g_ref, dh2_ref, dh2b_ref, loss_ref, dg_ref):
        i = pl.program_id(0)
        h2 = h1_ref[...] + _dot(a_ref[...], w_ref[...], NN)
        r = lax.rsqrt(jnp.mean(h2 * h2, axis=-1, keepdims=True) + RMS_EPS)
        g = g_ref[...]
        diff = h2 * r * g - t_ref[...]
        _accumulate(loss_ref, i == 0, jnp.full(loss_ref.shape, jnp.sum(diff * diff) * (0.5 / D), F32))
        dh2, dg_rows = _rms_bwd(h2, g, diff * (1.0 / D))
        dh2_ref[...] = dh2
        dh2b_ref[...] = dh2.astype(BF16)
        _accumulate(dg_ref, i == 0, jnp.sum(dg_rows, axis=0, keepdims=True))

    row = lambda i: (i, 0)
    return _call(
        "down_loss",
        body,
        (nt,),
        [
            pl.BlockSpec((tm, F), row),
            pl.BlockSpec((F, D), lambda i: (0, 0), pipeline_mode=pl.Buffered(1)),
            pl.BlockSpec((tm, D), row),
            pl.BlockSpec((tm, D), row),
            pl.BlockSpec((1, D), lambda i: (0, 0)),
        ],
        [
            pl.BlockSpec((tm, D), row),
            pl.BlockSpec((tm, D), row),
            pl.BlockSpec((1, LANES), lambda i: (0, 0)),
            pl.BlockSpec((1, D), lambda i: (0, 0)),
        ],
        [
            jax.ShapeDtypeStruct((T, D), F32),
            jax.ShapeDtypeStruct((T, D), BF16),
            jax.ShapeDtypeStruct((1, LANES), F32),
            jax.ShapeDtypeStruct((1, D), F32),
        ],
        (a_b, wd_b, h1, target, g_final),
    )


def _ffn_bwd_act(dh2_b, wd_b, silu_b, uds_b, after=()):
    T, D = dh2_b.shape
    F = wd_b.shape[0]
    tm, tf = _tile(T, 1024), _hidden_tile(F)

    def body(d_ref, w_ref, silu_ref, uds_ref, dg_ref, du_ref):
        d = d_ref[...]
        for c0 in range(0, tf, HIDDEN_CHUNK):
            cs = slice(c0, min(c0 + HIDDEN_CHUNK, tf))
            da = _dot(d, w_ref[cs, :], NT)
            dg_ref[:, cs] = (da * uds_ref[:, cs].astype(F32)).astype(BF16)
            du_ref[:, cs] = (da * silu_ref[:, cs].astype(F32)).astype(BF16)

    aspec = pl.BlockSpec((tm, tf), lambda j, i: (i, j))
    return _call(
        "ffn_bwd_act",
        body,
        (F // tf, T // tm),
        [pl.BlockSpec((tm, D), lambda j, i: (i, 0)), pl.BlockSpec((tf, D), lambda j, i: (j, 0)), aspec, aspec],
        [aspec, aspec],
        [jax.ShapeDtypeStruct((T, F), BF16)] * 2,
        (dh2_b, wd_b, silu_b, uds_b),
        after=after,
    )


def _ffn_bwd_in(dg_b, du_b, wgT_b, wuT_b, h1, dh2, g_ffn, w_out_b, comm=()):
    T, D = h1.shape
    F = wgT_b.shape[0]
    DM = w_out_b.shape[0]
    tm = _tile(T, 512)

    def body(dg_ref, du_ref, wg_ref, wu_ref, h1_ref, dh2_ref, g_ref, wo_ref, dh1_ref, dh1b_ref, dy_ref, dgf_ref):
        i = pl.program_id(0)
        dhn = _dot(dg_ref[...], wg_ref[...], NN) + _dot(du_ref[...], wu_ref[...], NN)
        dx, dg_rows = _rms_bwd(h1_ref[...], g_ref[...], dhn)
        dh1 = dh2_ref[...] + dx
        dh1b = dh1.astype(BF16)
        dh1_ref[...] = dh1
        dh1b_ref[...] = dh1b
        dy_ref[...] = _dot(dh1b, wo_ref[...], NT)
        _accumulate(dgf_ref, i == 0, jnp.sum(dg_rows, axis=0, keepdims=True))

    row = lambda i: (i, 0)
    const = lambda i: (0, 0)
    return _call(
        "ffn_bwd_in",
        body,
        (T // tm,),
        [
            pl.BlockSpec((tm, F), row),
            pl.BlockSpec((tm, F), row),
            pl.BlockSpec((F, D), const, pipeline_mode=pl.Buffered(1)),
            pl.BlockSpec((F, D), const, pipeline_mode=pl.Buffered(1)),
            pl.BlockSpec((tm, D), row),
            pl.BlockSpec((tm, D), row),
            pl.BlockSpec((1, D), const),
            pl.BlockSpec((DM, D), const, pipeline_mode=pl.Buffered(1)),
        ],
        [pl.BlockSpec((tm, D), row), pl.BlockSpec((tm, D), row), pl.BlockSpec((tm, DM), row), pl.BlockSpec((1, D), const)],
        [
            jax.ShapeDtypeStruct((T, D), F32),
            jax.ShapeDtypeStruct((T, D), BF16),
            jax.ShapeDtypeStruct((T, DM), F32),
            jax.ShapeDtypeStruct((1, D), F32),
        ],
        (dg_b, du_b, wgT_b, wuT_b, h1, dh2, g_ffn, w_out_b),
        comm=comm,
    )


def _seq_bwd(z, dy, v, w_dw4, ln_g, ln_b, w_pool_b, s_pool, comm=()):
    T, CI = z.shape
    CC = ln_g.shape[1]
    n_grp, G = w_pool_b.shape[0], w_pool_b.shape[-1]
    CP = n_grp * G
    KW = w_dw4.shape[1]
    n_cc = CC // LANES
    D = CC + CP
    tt = _tile(T, 512, HALO)
    per = tt // HALO
    n_tiles = T // tt
    last_halo = T // HALO - 1

    def body(zc_ref, zp_ref, dyc_ref, dyn_ref, vc_ref, vn_ref, wdw_ref, lng_ref, lnb_ref, wp_ref, sp_ref,
             dz_ref, dwdw_ref, dbdw_ref, dlng_ref, dlnb_ref, dwp_ref, dsp_ref, dbin_ref,
             dv_scr, u_scr, p_scr, g_scr, dw_scr):
        i = pl.program_id(0)
        first = i == 0
        last = i == n_tiles - 1
        lng, lnb = lng_ref[...], lnb_ref[...]

        def conv_pre(vv, dyc):
            mu = jnp.mean(vv, axis=-1, keepdims=True)
            d = vv - mu
            rs = lax.rsqrt(jnp.mean(d * d, axis=-1, keepdims=True) + LN_EPS)
            xh = d * rs
            ln = xh * lng + lnb
            sg = _sigmoid(ln)
            dln = dyc * (sg * (1.0 + ln * (1.0 - sg)))
            dxh = dln * lng
            dv = rs * (dxh - jnp.mean(dxh, axis=-1, keepdims=True) - xh * jnp.mean(dxh * xh, axis=-1, keepdims=True))
            return dv, dln, xh

        dv_c, dln_c, xh_c = conv_pre(vc_ref[...], dyc_ref[:, 0:CC])
        dv_scr[0, 0:tt, :] = dv_c
        dv_n, _, _ = conv_pre(vn_ref[...], dyn_ref[:, 0:CC])
        dv_scr[0, tt:, :] = jnp.where(last, 0.0, dv_n)
        _fill_shifted(dv_scr)
        _accumulate(dlng_ref, first, jnp.sum(dln_c * xh_c, axis=0, keepdims=True))
        _accumulate(dlnb_ref, first, jnp.sum(dln_c, axis=0, keepdims=True))
        _accumulate(dbdw_ref, first, jnp.sum(dv_c, axis=0, keepdims=True))

        u_scr[...] = zc_ref[:, 0:CC] * _sigmoid(zc_ref[:, CC : 2 * CC])

        @pl.when(first)
        def _():
            dw_scr[...] = jnp.zeros_like(dw_scr)

        for j in range(n_cc):
            cs = slice(LANES * j, LANES * (j + 1))
            gs = slice(CC + LANES * j, CC + LANES * (j + 1))
            dbin_a = jnp.zeros((1, LANES), F32)
            dbin_g = jnp.zeros((1, LANES), F32)
            for rb in range(tt // CONV_ROWS):
                rows = slice(rb * CONV_ROWS, (rb + 1) * CONV_ROWS)
                u_blk = u_scr[rows, cs]
                du = jnp.zeros((CONV_ROWS, LANES), F32)
                for k in range(KW):
                    off = rb * CONV_ROWS + (KW - 1) - k
                    d = _shifted_rows(dv_scr, off, CONV_ROWS, cs)
                    du = du + d * wdw_ref[j, k]
                    dw_scr[j * HALO + k] += jnp.sum((u_blk * d).reshape(CONV_ROWS // 8, 8, LANES), axis=0)
                a = zc_ref[rows, cs]
                sg = _sigmoid(zc_ref[rows, gs])
                da = du * sg
                dgate = du * a * sg * (1.0 - sg)
                dz_ref[rows, cs] = da.astype(BF16)
                dz_ref[rows, gs] = dgate.astype(BF16)
                dbin_a = dbin_a + jnp.sum(da, axis=0, keepdims=True)
                dbin_g = dbin_g + jnp.sum(dgate, axis=0, keepdims=True)
            _accumulate(dbin_ref.at[:, cs], first, dbin_a)
            _accumulate(dbin_ref.at[:, gs], first, dbin_g)

        @pl.when(last)
        def _():
            dwdw_ref[...] = jnp.sum(dw_scr[...], axis=1).reshape(dwdw_ref.shape)

        p_scr[0:HALO, :] = jnp.where(first, 0.0, zp_ref[:, 2 * CC :])
        p_scr[HALO:, :] = zc_ref[:, 2 * CC :]
        tpos = i * tt + lax.broadcasted_iota(jnp.int32, (tt, 1), 0)
        for gi, w in enumerate(POOL_WINDOWS):
            cs = slice(G * gi, G * (gi + 1))
            ys = slice(CC + G * gi, CC + G * (gi + 1))
            ps = slice(2 * CC + G * gi, 2 * CC + G * (gi + 1))
            cnt = jnp.minimum(tpos + 1, w).astype(F32)
            yib = _pool_mean_minus_token(p_scr, cs, w, cnt, tt).astype(BF16)
            wp = wp_ref[gi]
            sp = sp_ref[:, cs]
            dyp = dyc_ref[:, ys]
            q = _dot(yib, wp, NN)
            _accumulate(dsp_ref.at[:, cs], first, jnp.sum(dyp * q, axis=0, keepdims=True))
            dq_c = (dyp * sp).astype(BF16)
            dq_n = (jnp.where(last, 0.0, dyn_ref[:, ys]) * sp).astype(BF16)
            _accumulate(dwp_ref.at[gi], first, _dot(yib, dq_c, TN))
            dyi_c = _dot(dq_c, wp, NT)
            g_scr[0:tt, cs] = dyi_c / cnt
            g_scr[tt:, cs] = _dot(dq_n, wp, NT) * (1.0 / w)
            dp = -dyi_c
            for d in range(w):
                dp = dp + g_scr[d : d + tt, cs]
            dz_ref[:, ps] = dp.astype(BF16)
            _accumulate(dbin_ref.at[:, ps], first, jnp.sum(dp, axis=0, keepdims=True))

    cur = lambda i: (i, 0)
    prev = lambda i: (jnp.maximum(i * per - 1, 0), 0)
    nxt = lambda i: (jnp.minimum((i + 1) * per, last_halo), 0)
    c2 = lambda i: (0, 0)
    c3 = lambda i: (0, 0, 0)
    return _call(
        "seq_bwd",
        body,
        (n_tiles,),
        [
            pl.BlockSpec((tt, CI), cur),
            pl.BlockSpec((HALO, CI), prev),
            pl.BlockSpec((tt, D), cur),
            pl.BlockSpec((HALO, D), nxt),
            pl.BlockSpec((tt, CC), cur),
            pl.BlockSpec((HALO, CC), nxt),
            pl.BlockSpec(w_dw4.shape, lambda i: (0,) * w_dw4.ndim),
            pl.BlockSpec((1, CC), c2),
            pl.BlockSpec((1, CC), c2),
            pl.BlockSpec(w_pool_b.shape, c3),
            pl.BlockSpec((1, CP), c2),
        ],
        [
            pl.BlockSpec((tt, CI), cur),
            pl.BlockSpec((n_cc, HALO, LANES), c3),
            pl.BlockSpec((1, CC), c2),
            pl.BlockSpec((1, CC), c2),
            pl.BlockSpec((1, CC), c2),
            pl.BlockSpec((n_grp, G, G), c3),
            pl.BlockSpec((1, CP), c2),
            pl.BlockSpec((1, CI), c2),
        ],
        [
            jax.ShapeDtypeStruct((T, CI), BF16),
            jax.ShapeDtypeStruct((n_cc, HALO, LANES), F32),
            jax.ShapeDtypeStruct((1, CC), F32),
            jax.ShapeDtypeStruct((1, CC), F32),
            jax.ShapeDtypeStruct((1, CC), F32),
            jax.ShapeDtypeStruct((n_grp, G, G), F32),
            jax.ShapeDtypeStruct((1, CP), F32),
            jax.ShapeDtypeStruct((1, CI), F32),
        ],
        (z, z, dy, dy, v, v, w_dw4, ln_g, ln_b, w_pool_b, s_pool),
        scratch=[
            pltpu.VMEM((SUBLANES, tt + HALO, CC), F32),
            pltpu.VMEM((tt, CC), F32),
            pltpu.VMEM((HALO + tt, CP), F32),
            pltpu.VMEM((tt + HALO, CP), F32),
            pltpu.VMEM((n_cc * HALO, 8, LANES), F32),
        ],
        comm=comm,
    )


def _in_proj_bwd(dz_b, w_inT_b, x, dh1, g_mix, after=()):
    T, D = x.shape
    CI = w_inT_b.shape[0]
    tm = _tile(T, 512)

    def body(dz_ref, w_ref, x_ref, dh1_ref, g_ref, dx_ref, dg_ref):
        i = pl.program_id(0)
        dxn = _dot(dz_ref[...], w_ref[...], NN)
        dx, dg_rows = _rms_bwd(x_ref[...], g_ref[...], dxn)
        dx_ref[...] = dh1_ref[...] + dx
        _accumulate(dg_ref, i == 0, jnp.sum(dg_rows, axis=0, keepdims=True))

    row = lambda i: (i, 0)
    const = lambda i: (0, 0)
    return _call(
        "in_proj_bwd",
        body,
        (T // tm,),
        [
            pl.BlockSpec((tm, CI), row),
            pl.BlockSpec((CI, D), const),
            pl.BlockSpec((tm, D), row),
            pl.BlockSpec((tm, D), row),
            pl.BlockSpec((1, D), const),
        ],
        [pl.BlockSpec((tm, D), row), pl.BlockSpec((1, D), const)],
        [jax.ShapeDtypeStruct((T, D), F32), jax.ShapeDtypeStruct((1, D), F32)],
        (dz_b, w_inT_b, x, dh1, g_mix),
        after=after,
    )


def _weight_grad(name, a_b, b_b, after=()):
    T, N1 = a_b.shape
    N2 = b_b.shape[1]
    t1 = _tile(N1, 1408, LANES)
    tk = _tile(T, 2048)
    nk = T // tk

    def body(a_ref, b_ref, o_ref, acc):
        k = pl.program_id(1)
        _accumulate(acc, k == 0, _dot(a_ref[...], b_ref[...], TN))

        @pl.when(k == nk - 1)
        def _():
            o_ref[...] = acc[...].astype(BF16)

    (out,), _ = _call(
        name,
        body,
        (N1 // t1, nk),
        [pl.BlockSpec((tk, t1), lambda n, k: (k, n)), pl.BlockSpec((tk, N2), lambda n, k: (k, 0))],
        [pl.BlockSpec((t1, N2), lambda n, k: (n, 0))],
        [jax.ShapeDtypeStruct((N1, N2), BF16)],
        (a_b, b_b),
        scratch=[pltpu.VMEM((t1, N2), F32)],
        after=after,
    )
    return out


def _sum_parts(name, full, how, parts, me):
    _, R, C = parts[0].shape
    tr = _tile(R, 512)
    nb = R // tr
    where = [(q, r) for q, p in enumerate(parts) for r in range(p.shape[0])]
    assert len(where) == 3

    def body(me_ref, own_ref, *refs):
        o_ref = refs[-1]
        f = lambda j: refs[where[j][0]][where[j][1]].astype(F32)
        o_ref[...] = (own_ref[...].astype(F32) + f(0)) + (f(1) + f(2))

    own_map = {"rows": lambda i, me_ref: (me_ref[0] * nb + i, 0), "all": lambda i, me_ref: (i, 0)}[how]
    return pl.pallas_call(
        body,
        name=name,
        grid_spec=pltpu.PrefetchScalarGridSpec(
            num_scalar_prefetch=1,
            grid=(nb,),
            in_specs=[pl.BlockSpec((tr, C), own_map)]
            + [pl.BlockSpec((p.shape[0], tr, C), lambda i, me_ref: (0, i, 0)) for p in parts],
            out_specs=pl.BlockSpec((tr, C), lambda i, me_ref: (i, 0)),
        ),
        out_shape=jax.ShapeDtypeStruct((R, C), F32),
        compiler_params=pltpu.CompilerParams(dimension_semantics=("arbitrary",), vmem_limit_bytes=VMEM_LIMIT),
    )(me, full, *parts)


_M_CORR = 1.0 - ADAM_B1**ADAM_STEP
_V_CORR = 1.0 - ADAM_B2**ADAM_STEP


def _adamw_math(w, g, m, v):
    m = ADAM_B1 * m + (1.0 - ADAM_B1) * g
    v = ADAM_B2 * v + (1.0 - ADAM_B2) * (g * g)
    delta = -ADAM_LR * ((m / _M_CORR) / (jnp.sqrt(v / _V_CORR) + ADAM_EPS) + ADAM_WD * w)
    return delta, m, v


def _adamw(name, w, m, v, g_here, g_there, g_transposed=False):
    R, C = w.shape
    tr = _tile(R, 256, LANES if g_transposed else 8)

    def body(w_ref, m_ref, v_ref, ga_ref, gb_ref, g_ref, d_ref, nm_ref, nv_ref):
        g = ga_ref[...] + gb_ref[...]
        if g_transposed:
            g = g.T
        g_ref[...] = g
        d_ref[...], nm_ref[...], nv_ref[...] = _adamw_math(w_ref[...], g, m_ref[...], v_ref[...])

    spec = pl.BlockSpec((tr, C), lambda i: (i, 0))
    gspec = pl.BlockSpec((C, tr), lambda i: (0, i)) if g_transposed else spec
    return _call(name, body, (R // tr,), [spec] * 3 + [gspec] * 2, [spec] * 4, [jax.ShapeDtypeStruct((R, C), F32)] * 4,
                 (w, m, v, g_here, g_there))


def _adamw_on_sparsecore(name, w, m, v, g_here, g_there, after):
    R, C = w.shape
    n_groups = R // SUBLANES
    n_turns = -(-n_groups // SC_TILES)
    n_in, n_out = 5, 4

    def body(w_hbm, m_hbm, v_hbm, ga_hbm, gb_hbm, after_hbm, g_out, d_out, nm_out, nv_out, bufs, sems):
        tile = lax.axis_index("subcore") * SC_CORES + lax.axis_index("sparsecore")
        srcs = (w_hbm, m_hbm, v_hbm, ga_hbm, gb_hbm)
        dsts = (d_out, nm_out, nv_out, g_out)

        def rows(turn):
            return pl.ds((tile + turn * SC_TILES) * SUBLANES, SUBLANES)

        def loads(turn):
            slot = turn % 2
            return [pltpu.make_async_copy(srcs[q].at[rows(turn), :], bufs.at[slot, q], sems.at[slot, q]) for q in range(n_in)]

        def stores(turn):
            slot = turn % 2
            return [pltpu.make_async_copy(bufs.at[slot, q], dsts[q].at[rows(turn), :], sems.at[slot, n_in + q])
                    for q in range(n_out)]

        def when_mine(turn, fn):
            pl.when(tile + turn * SC_TILES < n_groups)(fn)

        def compute(slot):
            wb, mb, vb, gab, gbb = (bufs.at[slot, q] for q in range(n_in))

            @pl.loop(0, SUBLANES)
            def _(r):
                @pl.loop(0, C, step=SC_LANES)
                def _(i):
                    at = (r, pl.ds(i, SC_LANES))
                    g = gab[at] + gbb[at]
                    delta, new_m, new_v = _adamw_math(wb[at], g, mb[at], vb[at])
                    gab[at], wb[at], mb[at], vb[at] = g, delta, new_m, new_v

        def start_loads(turn):
            def fn():
                for cp in loads(turn):
                    cp.start()

            when_mine(turn, fn)

        start_loads(0)
        for turn in range(n_turns):
            def step(turn=turn):
                for cp in loads(turn):
                    cp.wait()
                if turn >= 1:
                    for cp in stores(turn - 1):
                        cp.wait()
                if turn + 1 < n_turns:
                    start_loads(turn + 1)
                compute(turn % 2)
                for cp in stores(turn):
                    cp.start()

            when_mine(turn, step)
        for turn in range(n_turns):
            def drain(turn=turn):
                for cp in stores(turn):
                    cp.wait()

            last_mine = jnp.logical_and(tile + turn * SC_TILES < n_groups, tile + (turn + 1) * SC_TILES >= n_groups)
            pl.when(last_mine)(drain)

    return pl.kernel(
        body,
        name=name,
        out_type=[jax.ShapeDtypeStruct((R, C), F32)] * 4,
        mesh=plsc.VectorSubcoreMesh(core_axis_name="sparsecore", subcore_axis_name="subcore"),
        scratch_types=[pltpu.VMEM((2, n_in, SUBLANES, C), F32), pltpu.SemaphoreType.DMA((2, n_in + n_out))],
        compiler_params=pltpu.CompilerParams(use_tc_tiling_on_sc=True),
    )(w, m, v, g_here, g_there, after)


class _PackLayout:
    def __init__(self, n_cc, n_grp, G, widths):
        self.dw_rows = (0, HALO)
        self.wp_rows = (HALO, HALO + G)
        self.n_cc, self.n_grp, self.G = n_cc, n_grp, G
        self.vec = {}
        r = HALO + G
        for name, width in widths:
            self.vec[name] = (r, width)
            r += width // PACK_W
        self.rows = -(-r // 8) * 8


def _pack_small(layout, dwdw, dwp, vecs):
    names = list(vecs)

    def body(*refs):
        dw_ref, wp_ref = refs[0], refs[1]
        vec_refs = refs[2 : 2 + len(names)]
        o_ref = refs[-1]
        o_ref[...] = jnp.zeros_like(o_ref)
        for j in range(layout.n_cc):
            o_ref[layout.dw_rows[0] : layout.dw_rows[1], j * LANES : (j + 1) * LANES] = dw_ref[j]
        for i in range(layout.n_grp):
            o_ref[layout.wp_rows[0] : layout.wp_rows[1], i * layout.G : (i + 1) * layout.G] = wp_ref[i]
        for name, ref in zip(names, vec_refs):
            r, width = layout.vec[name]
            for h in range(width // PACK_W):
                o_ref[r + h : r + h + 1, :] = ref[:, h * PACK_W : (h + 1) * PACK_W]

    return pl.pallas_call(
        body,
        name="pack_small",
        out_shape=jax.ShapeDtypeStruct((layout.rows, PACK_W), F32),
    )(dwdw, dwp, *[vecs[k] for k in names])


def _adamw_small(layout, g_here, g_there, w_dw, m_dw, v_dw, w_pool, m_pool, v_pool, vec_w, vec_m, vec_v, row):
    names = list(vec_w)
    nv = len(names)

    def body(*refs):
        ga_ref, gb_ref = refs[0], refs[1]
        wdw, mdw, vdw, wp, mp, vp = refs[2:8]
        vw, vm, vv = refs[8 : 8 + nv], refs[8 + nv : 8 + 2 * nv], refs[8 + 2 * nv : 8 + 3 * nv]
        row_g, row_w, row_m, row_v = refs[8 + 3 * nv : 12 + 3 * nv]
        outs = refs[12 + 3 * nv :]
        acc = outs[-1]
        acc[...] = ga_ref[...] + gb_ref[...]

        def emit(o, g, w, m, v, idx=()):
            res = (g,) + _adamw_math(w, g, m, v)
            for ref, val in zip(o, res):
                ref[idx] = val

        me = 2 * lax.axis_index("x") + lax.axis_index("y")
        for j in range(layout.n_cc):

            @pl.when(me == j)
            def _(j=j):
                for k in range(wdw.shape[0]):
                    g = acc[layout.dw_rows[0] + k : layout.dw_rows[0] + k + 1, j * LANES : (j + 1) * LANES]
                    emit(outs[0:4], g, wdw[k], mdw[k], vdw[k], idx=k)

        for i in range(layout.n_grp):
            g = acc[layout.wp_rows[0] : layout.wp_rows[1], i * layout.G : (i + 1) * layout.G]
            emit(outs[4:8], g, wp[i], mp[i], vp[i], idx=i)
        for q, name in enumerate(names):
            r, width = layout.vec[name]
            for h in range(width // PACK_W):
                ls = slice(h * PACK_W, (h + 1) * PACK_W)
                g = acc[r + h : r + h + 1, :]
                emit(outs[8 + 4 * q : 12 + 4 * q], g, vw[q][:, ls], vm[q][:, ls], vv[q][:, ls], idx=(slice(None), ls))
        emit(outs[8 + 4 * nv : 12 + 4 * nv], row_g[...], row_w[...], row_m[...], row_v[...], idx=...)

    shapes = [w_dw.shape] * 4 + [w_pool.shape] * 4
    for name in names:
        shapes += [vec_w[name].shape] * 4
    shapes += [row[1].shape] * 4
    return pl.pallas_call(
        body,
        name="adamw_small",
        out_shape=[jax.ShapeDtypeStruct(s, F32) for s in shapes],
        scratch_shapes=[pltpu.VMEM(g_here.shape, F32)],
    )(g_here, g_there, w_dw, m_dw, v_dw, w_pool, m_pool, v_pool,
      *[vec_w[k] for k in names], *[vec_m[k] for k in names], *[vec_v[k] for k in names], *row)


def _allreduce_rows(g_part, loss_part, comm=()):
    n_pairs = N_DEV - 1

    def body(g_ref, l_ref, go_ref, lo_ref, land_g, land_l, sems):
        x, y, c = _place()
        copies = []
        for q, (src, land) in enumerate(((g_ref, land_g), (l_ref, land_l))):
            for r in range(1, N_DEV):
                fx, fy, fc = (r >> 2) & 1, (r >> 1) & 1, r & 1
                peer = (1 - x if fx else x, 1 - y if fy else y, 1 - c if fc else c)
                cp = _remote(src, land.at[r], sems, 2 * (q * n_pairs + r - 1), peer)
                cp.start()
                copies.append(cp)
        for cp in copies:
            cp.wait()

        def total(src, land):
            row = lambda r: src[...] if r == 0 else land[r]
            return ((row(0) + row(4)) + (row(2) + row(6))) + ((row(1) + row(5)) + (row(3) + row(7)))

        go_ref[...] = total(g_ref, land_g)
        lo_ref[...] = total(l_ref, land_l)

    vm = pl.BlockSpec(memory_space=pltpu.VMEM)
    return _call(
        "allreduce_rows",
        body,
        (),
        [vm] * 2,
        [vm] * 2,
        [jax.ShapeDtypeStruct(g_part.shape, F32), jax.ShapeDtypeStruct(loss_part.shape, F32)],
        (g_part, loss_part),
        scratch=[pltpu.VMEM((N_DEV,) + g_part.shape, F32), pltpu.VMEM((N_DEV,) + loss_part.shape, F32),
                 pltpu.SemaphoreType.DMA((4 * n_pairs,))],
        comm=comm,
    )


def kernel(x, g_mix, w_in, b_in, w_dw, b_dw, ln_g, ln_b, w_pool, s_pool, w_out, g_ffn, w_gate, w_up, w_down, g_final, loss_target, m_g_mix, m_w_in, m_b_in, m_w_dw, m_b_dw, m_ln_g, m_ln_b, m_w_pool, m_s_pool, m_w_out, m_g_ffn, m_w_gate, m_w_up, m_w_down, m_g_final, v_g_mix, v_w_in, v_b_in, v_w_dw, v_b_dw, v_ln_g, v_ln_b, v_w_pool, v_s_pool, v_w_out, v_g_ffn, v_w_gate, v_w_up, v_w_down, v_g_final):
    x2 = x[0]
    target = loss_target[0]
    T, D = x2.shape
    w_in2, w_out2, w_down2 = w_in[0], w_out[0], w_down[0]
    taps_first = lambda a: jnp.transpose(a, (1, 0, 2))
    w_dw3 = taps_first(w_dw)
    w_gateT, w_upT = w_gate[0].T, w_up[0].T
    CI = w_in2.shape[1] * N_CHIPS
    DM = w_out2.shape[0] * N_CHIPS
    F = w_down2.shape[0] * N_CHIPS
    KW, _, dw_cols = w_dw3.shape
    assert dw_cols == LANES
    n_grp, G = w_pool.shape[1], w_pool.shape[-1]
    w_pool3 = w_pool[0]
    g_final2 = g_final.reshape(1, D)

    me = (2 * lax.axis_index("x") + lax.axis_index("y")).astype(jnp.int32).reshape(1)

    w_inT_b, w_dw4, f_out, f_gate, f_up, f_down = _place_and_gather(
        [(w_in2, "rows", (CI, D), BF16, True, True), (w_dw3, "lead", (N_CHIPS, KW, 1, dw_cols), F32, False, False)],
        [(w, "rows", shape, BF16, False, True)
         for w, shape in ((w_out2, (DM, D)), (w_gateT, (F, D)), (w_upT, (F, D)), (w_down2, (F, D)))])
    w_pool_b = w_pool3.astype(BF16)
    ici = lambda f: _GatherIci([f], ["rows"], [True])
    d2d = lambda f: _GatherD2d([f], ["rows"])
    gather = _start("gather_start", [ici(f_out), ici(f_gate), ici(f_up), ici(f_down)])
    (z, xn_b), _ = _in_proj(x2, g_mix, w_inT_b, b_in, after=[gather.token])
    (f_out,) = _wait("gather_out_wait", gather, 0, xn_b)
    s_out = _start("share_out_start", [d2d(f_out)], sibling_only=True)
    (y_b, v), _ = _seq_fwd(z, w_dw4, b_dw, ln_g, ln_b, w_pool_b, s_pool, after=[s_out.token])
    (w_out_b,) = _wait("share_out_wait", s_out, 0, y_b)
    (f_gate,) = _wait("gather_gate_wait", gather, 1, y_b)
    s_gate = _start("share_gate_start", [d2d(f_gate)], sibling_only=True)
    (h1, hn_b), _ = _out_proj(y_b, x2, w_out_b, g_ffn, after=[s_gate.token])
    (f_up,) = _wait("gather_up_wait", gather, 2, hn_b)
    s_up = _start("share_up_start", [d2d(f_up)], sibling_only=True)
    (wgT_b,) = _wait("share_gate_wait", s_gate, 0, hn_b)
    (wuT_b,) = _wait("share_up_wait", s_up, 0, hn_b)
    (silu_b, uds_b, a_b), _ = _gate_up(hn_b, wgT_b, wuT_b)
    (f_down,) = _wait("gather_down_wait", gather, 3, a_b)
    s_down = _start("share_down_start", [d2d(f_down)], sibling_only=True)
    (wd_b,) = _wait("share_down_wait", s_down, 0, a_b)
    (dh2, dh2_b, loss_part, d_g_final), _ = _down_loss(a_b, wd_b, h1, target, g_final2)

    gw_down = _weight_grad("grad_w_down", a_b, dh2_b)
    x_down = _start("scatter_down_start", [_Scatter([gw_down], ["rows"])])
    (dg_b, du_b), _ = _ffn_bwd_act(dh2_b, wd_b, silu_b, uds_b, after=[x_down.token])
    gw_gateT = _weight_grad("grad_w_gate", dg_b, hn_b)
    gw_upT = _weight_grad("grad_w_up", du_b, hn_b)
    gw_down, p_down = _wait("scatter_down_wait", x_down, 0, gw_upT)
    sum_down = _sum_parts("sum_w_down", gw_down, "rows", [p_down], me)
    (dh1, dh1_b, dy, d_g_ffn), (p_gate, oth_down) = _ffn_bwd_in(
        dg_b, du_b, wgT_b, wuT_b, h1, dh2, g_ffn, w_out_b, comm=[_Scatter([gw_gateT], ["rows"]), _Swap([sum_down])])
    gw_out = _weight_grad("grad_w_out", y_b, dh1_b)
    sum_gate = _sum_parts("sum_w_gate", gw_gateT, "rows", [p_gate], me)
    res = {}
    res["w_down"] = _adamw_on_sparsecore("adamw_w_down", w_down2, m_w_down[0], v_w_down[0], sum_down, oth_down, sum_down)
    (dz_b, d_wdw, d_bdw, d_lng, d_lnb, d_wp, d_sp, d_bin), (p_up, p_out, oth_gate) = _seq_bwd(
        z, dy, v, w_dw4, ln_g, ln_b, w_pool_b, s_pool,
        comm=[_Scatter([gw_upT, gw_out], ["rows", "rows"]), _Swap([sum_gate])])
    res["w_gate"] = _adamw_on_sparsecore(
        "adamw_w_gate", w_gateT, m_w_gate[0].T, v_w_gate[0].T, sum_gate, oth_gate, res["w_down"][0])
    vec_grads ={"b_dw": d_bdw, "ln_g": d_lng, "ln_b": d_lnb, "s_pool": d_sp, "g_ffn": d_g_ffn, "g_final": d_g_final, "b_in": d_bin}
    layout = _PackLayout(dw_cols * N_CHIPS // LANES, n_grp, G, [(k, a.shape[1]) for k, a in vec_grads.items()])
    pack = _pack_small(layout, d_wdw, d_wp, vec_grads)
    sum_up = _sum_parts("sum_w_up", gw_upT, "rows", [p_up], me)
    sum_out = _sum_parts("sum_w_out", gw_out, "rows", [p_out], me)
    mid = _start("mid_start", [_Swap([sum_up, sum_out]), _Scatter([pack], ["all"])])
    gw_inT = _weight_grad("grad_w_in", dz_b, xn_b, after=[mid.token])
    sum_up, sum_out, oth_up, oth_out = _wait("mid_swap_wait", mid, 0, gw_inT)
    late = _start("late_start", [_Scatter([gw_inT], ["rows"])])
    (grad_x, d_g_mix), _ = _in_proj_bwd(dz_b, w_inT_b, x2, dh1, g_mix, after=[late.token])
    pack, p_small = _wait("mid_small_wait", mid, 1, d_g_mix)
    gw_inT, p_in = _wait("late_w_in_wait", late, 0, d_g_mix)
    sum_small = _sum_parts("sum_small", pack, "all", [p_small], me)
    res["w_up"] = _adamw_on_sparsecore("adamw_w_up", w_upT, m_w_up[0].T, v_w_up[0].T, sum_up, oth_up, res["w_gate"][0])
    res["w_out"] = _adamw_on_sparsecore("adamw_w_out", w_out2, m_w_out[0], v_w_out[0], sum_out, oth_out, res["w_gate"][0])
    sum_in = _sum_parts("sum_w_in", gw_inT, "rows", [p_in], me)
    (d_g_mix, loss_row), (oth_in, oth_small) = _allreduce_rows(d_g_mix, loss_part, comm=[_Swap([sum_in, sum_small])])
    loss = loss_row[0, 0]
    res["w_in"], _ = _adamw("adamw_w_in", w_in2, m_w_in[0], v_w_in[0], sum_in, oth_in, g_transposed=True)

    vec_w = {"b_dw": b_dw, "ln_g": ln_g, "ln_b": ln_b, "s_pool": s_pool, "g_ffn": g_ffn, "g_final": g_final2, "b_in": b_in}
    vec_m = {"b_dw": m_b_dw, "ln_g": m_ln_g, "ln_b": m_ln_b, "s_pool": m_s_pool, "g_ffn": m_g_ffn,
             "g_final": m_g_final.reshape(1, D), "b_in": m_b_in}
    vec_v = {"b_dw": v_b_dw, "ln_g": v_ln_g, "ln_b": v_ln_b, "s_pool": v_s_pool, "g_ffn": v_g_ffn,
             "g_final": v_g_final.reshape(1, D), "b_in": v_b_in}
    small = _adamw_small(layout, sum_small, oth_small, w_dw3, taps_first(m_w_dw), taps_first(v_w_dw),
                         w_pool3, m_w_pool[0], v_w_pool[0], vec_w, vec_m, vec_v, (d_g_mix, g_mix, m_g_mix, v_g_mix))
    res["w_dw"] = [taps_first(a) for a in small[0:4]]
    res["w_pool"] = [a[None] for a in small[4:8]]
    for q, k in enumerate(vec_w):
        res[k] = list(small[8 + 4 * q : 12 + 4 * q])
    res["g_mix"] = list(small[-4:])
    res["g_final"] = [a.reshape(D) for a in res["g_final"]]
    for k in ("w_in", "w_out", "w_down"):
        res[k] = [a[None] for a in res[k]]
    for k in ("w_gate", "w_up"):
        res[k] = [a.T[None] for a in res[k]]

    order = ["g_mix", "w_in", "b_in", "w_dw", "b_dw", "ln_g", "ln_b", "w_pool", "s_pool", "w_out", "g_ffn", "w_gate", "w_up", "w_down", "g_final"]
    outs = [loss, grad_x[None]]
    for q in range(4):
        outs += [res[k][q] for k in order]
    return tuple(outs)
```

```python
import jax
import jax.numpy as jnp
from jax import lax
from jax.experimental import pallas as pl
from jax.experimental.pallas import tpu as pltpu
from jax.experimental.pallas import tpu_sc as plsc

F32 = jnp.float32
BF16 = jnp.bfloat16
MESH = pl.DeviceIdType.MESH
ANY = pl.BlockSpec(memory_space=pl.ANY)

RMS_EPS = 1e-6
LN_EPS = 1e-5
POOL_WINDOWS = (2, 4, 8, 16)
ADAM_LR = 0.001
ADAM_B1 = 0.9
ADAM_B2 = 0.999
ADAM_EPS = 1e-08
ADAM_WD = 0.01
ADAM_STEP = 10

LANES = 128
SUBLANES = 8
BF16_ROWS = 16
HALO = 32
CONV_ROWS = 64
HIDDEN_CHUNK = 512
VMEM_LIMIT = 56 * 1024 * 1024
PACK_W = 512
N_CHIPS = 4
N_DEV = 8
SIBLING_BARRIER_ID = 0
SC_CORES = 2
SC_TILES = 32
SC_LANES = 16


def _tile(n, want, mult=8):
    t = min(n, want)
    while n % t or t % mult:
        t -= 1
    return t


def _sigmoid(x):
    return 1.0 / (1.0 + jnp.exp(-x))


def _dot(a, b, dims):
    return lax.dot_general(a, b, (dims, ((), ())), preferred_element_type=F32)


NN = ((1,), (0,))
NT = ((1,), (1,))
TN = ((0,), (0,))


def _rms_bwd(x, g, dy):
    r = lax.rsqrt(jnp.mean(x * x, axis=-1, keepdims=True) + RMS_EPS)
    xh = x * r
    gy = dy * g
    dx = r * (gy - xh * jnp.mean(gy * xh, axis=-1, keepdims=True))
    return dx, dy * xh


def _accumulate(ref, first, val):
    @pl.when(first)
    def _():
        ref[...] = val

    @pl.when(jnp.logical_not(first))
    def _():
        ref[...] += val


def _place():
    return lax.axis_index("x"), lax.axis_index("y"), lax.axis_index("c")


def _other_chips(x, y):
    return [(1 - x, y), (x, 1 - y), (1 - x, 1 - y)]


def _rows(ref, start, n):
    return ref.at[pl.ds(pl.multiple_of(start, BF16_ROWS), n)]


def _window(ref, how, k, c=None):
    if how == "all":
        return ref
    if how == "lead":
        return ref.at[k]
    assert how == "rows"
    n = ref.shape[0] // N_CHIPS
    if c is None:
        return _rows(ref, k * n, n)
    return _rows(ref, k * n + c * (n // 2), n // 2)


def _remote(src, dst, sems, s, device):
    return pltpu.make_async_remote_copy(
        src_ref=src, dst_ref=dst, send_sem=sems.at[s], recv_sem=sems.at[s + 1], device_id=device, device_id_type=MESH)


class _GatherIci:
    aliased = True

    def __init__(self, fulls, hows, splits, which=(0, 1, 2)):
        self.fulls, self.hows, self.splits, self.which = list(fulls), list(hows), list(splits), tuple(which)

    def inputs(self):
        return self.fulls

    def out_shapes(self):
        return [jax.ShapeDtypeStruct(a.shape, a.dtype) for a in self.fulls]

    def n_sems(self):
        return 6 * len(self.fulls)

    def build(self, ins, outs, sems, base):
        x, y, c = _place()
        me = 2 * x + y
        chips = _other_chips(x, y)
        starts, waits = [], []
        for a, (how, sp) in enumerate(zip(self.hows, self.splits)):
            half = c if sp else None
            mine = _window(outs[a], how, me, half)
            for j in self.which:
                px, py = chips[j]
                s = base + 6 * a + 2 * j
                cp = _remote(mine, mine, sems, s, (px, py, c))
                landing = _remote(mine, _window(outs[a], how, 2 * px + py, half), sems, s, (px, py, c))
                starts.append(cp.start)
                waits += [landing.wait_recv, cp.wait_send]
        return starts, waits


class _GatherD2d:
    aliased = True

    def __init__(self, fulls, hows):
        self.fulls, self.hows = list(fulls), list(hows)

    def inputs(self):
        return self.fulls

    def out_shapes(self):
        return [jax.ShapeDtypeStruct(a.shape, a.dtype) for a in self.fulls]

    def n_sems(self):
        return 6 * len(self.fulls)

    def build(self, ins, outs, sems, base):
        x, y, c = _place()
        starts, waits = [], []
        for a, how in enumerate(self.hows):
            for j, (px, py) in enumerate(_other_chips(x, y)):
                s = base + 6 * a + 2 * j
                got = _window(outs[a], how, 2 * px + py, c)
                cp = _remote(got, got, sems, s, (x, y, 1 - c))
                landing = _remote(got, _window(outs[a], how, 2 * px + py, 1 - c), sems, s, (x, y, 1 - c))
                starts.append(cp.start)
                waits += [landing.wait_recv, cp.wait_send]
        return starts, waits


def _part_shape(a, how):
    if how == "all":
        return a.shape
    assert how == "rows"
    return (a.shape[0] // N_CHIPS, a.shape[1])


class _Scatter:
    aliased = False

    def __init__(self, fulls, hows, which=(0, 1, 2)):
        self.fulls, self.hows, self.which = list(fulls), list(hows), tuple(which)

    def inputs(self):
        return self.fulls

    def out_shapes(self):
        return [jax.ShapeDtypeStruct((len(self.which),) + _part_shape(a, h), a.dtype) for a, h in zip(self.fulls, self.hows)]

    def n_sems(self):
        return 6 * len(self.fulls)

    def build(self, ins, outs, sems, base):
        x, y, c = _place()
        chips = _other_chips(x, y)
        starts, waits = [], []
        for a, how in enumerate(self.hows):
            for slot, j in enumerate(self.which):
                px, py = chips[j]
                cp = _remote(_window(ins[a], how, 2 * px + py), outs[a].at[slot], sems, base + 6 * a + 2 * j, (px, py, c))
                starts.append(cp.start)
                waits += [cp.wait_recv, cp.wait_send]
        return starts, waits


class _Swap:
    aliased = False

    def __init__(self, arrays):
        self.arrays = list(arrays)

    def inputs(self):
        return self.arrays

    def out_shapes(self):
        return [jax.ShapeDtypeStruct(a.shape, a.dtype) for a in self.arrays]

    def n_sems(self):
        return 2 * len(self.arrays)

    def build(self, ins, outs, sems, base):
        x, y, c = _place()
        starts, waits = [], []
        for a in range(len(ins)):
            cp = _remote(ins[a], outs[a], sems, base + 2 * a, (x, y, 1 - c))
            starts.append(cp.start)
            waits += [cp.wait_recv, cp.wait_send]
        return starts, waits


def _call(name, body, grid, in_specs, out_specs, out_shape, args, scratch=(), comm=(), after=()):
    comm, after = list(comm), list(after)
    n_in, n_out, n_scr, n_after = len(args), len(out_shape), len(scratch), len(after)
    c_in = [a for op in comm for a in op.inputs()]
    c_out = [s for op in comm for s in op.out_shapes()]
    n_sems = sum(op.n_sems() for op in comm)
    aliases, i_in, i_out = {}, 0, 0
    for op in comm:
        if op.aliased:
            for q in range(len(op.inputs())):
                aliases[n_in + n_after + i_in + q] = n_out + i_out + q
        i_in, i_out = i_in + len(op.inputs()), i_out + len(op.out_shapes())

    def wrapped(*refs):
        ins = refs[:n_in]
        cin = refs[n_in + n_after : n_in + n_after + len(c_in)]
        o0 = n_in + n_after + len(c_in)
        outs = refs[o0 : o0 + n_out]
        cout = refs[o0 + n_out : o0 + n_out + len(c_out)]
        s0 = o0 + n_out + len(c_out)
        scr = refs[s0 : s0 + n_scr]

        def copies():
            sems = refs[s0 + n_scr]
            starts, waits = [], []
            i_in = i_out = base = 0
            for op in comm:
                ni, no = len(op.inputs()), len(op.out_shapes())
                s, w = op.build(cin[i_in : i_in + ni], cout[i_out : i_out + no], sems, base)
                starts += s
                waits += w
                i_in, i_out, base = i_in + ni, i_out + no, base + op.n_sems()
            return starts, waits

        def run_starts():
            for start in copies()[0]:
                start()

        def run_waits():
            for wait in copies()[1]:
                wait()

        if comm and grid:
            first = last = True
            for d, n in enumerate(grid):
                first = jnp.logical_and(first, pl.program_id(d) == 0)
                last = jnp.logical_and(last, pl.program_id(d) == n - 1)
            pl.when(first)(run_starts)
        elif comm:
            run_starts()
        if body is not None:
            body(*ins, *outs, *scr)
        if comm and grid:
            pl.when(last)(run_waits)
        elif comm:
            run_waits()

    res = pl.pallas_call(
        wrapped,
        name=name,
        grid=grid,
        in_specs=list(in_specs) + [ANY] * (n_after + len(c_in)),
        out_specs=list(out_specs) + [ANY] * len(c_out),
        out_shape=list(out_shape) + c_out,
        scratch_shapes=list(scratch) + ([pltpu.SemaphoreType.DMA((n_sems,))] if comm else []),
        input_output_aliases=aliases,
        compiler_params=pltpu.CompilerParams(dimension_semantics=("arbitrary",) * len(grid), vmem_limit_bytes=VMEM_LIMIT),
    )(*args, *after, *c_in)
    return tuple(res[:n_out]), tuple(res[n_out:])


def _place_and_gather(now, later):
    items = list(now) + list(later)
    n, n_now = len(items), len(now)
    buf_shape = lambda it: it[0].shape[::-1] if it[4] else it[0].shape
    split_now = [a for a in range(n_now) if items[a][5]]

    def body(*refs):
        ins, outs = refs[:n], refs[n : 2 * n]
        stage, bufs = refs[2 * n : 3 * n - n_now], refs[3 * n - n_now : 4 * n - n_now]
        sems = refs[4 * n - n_now]
        x, y, c = _place()
        me = 2 * x + y
        chips = _other_chips(x, y)
        loads = [pltpu.make_async_copy(ins[a], stage[a - n_now], sems.at[a]) for a in range(n_now, n)]
        for ld in loads:
            ld.start()
        pending = []

        def place(a, val):
            _, how, _, dtype, transposed, _ = items[a]
            bufs[a][...] = (val.T if transposed else val).astype(dtype)
            cp = pltpu.make_async_copy(bufs[a], _window(outs[a], how, me), sems.at[n + a])
            cp.start()
            pending.append(cp.wait)

        arrivals = []
        for a in range(n_now):
            place(a, ins[a][...])
            how, split = items[a][1], items[a][5]
            half = c if split else None
            src = _rows(bufs[a], c * (bufs[a].shape[0] // 2), bufs[a].shape[0] // 2) if split else bufs[a]
            for j, (px, py) in enumerate(chips):
                s = 2 * n + 6 * a + 2 * j
                cp = _remote(src, _window(outs[a], how, me, half), sems, s, (px, py, c))
                landing = _remote(src, _window(outs[a], how, 2 * px + py, half), sems, s, (px, py, c))
                cp.start()
                arrivals.append(landing.wait_recv)
                pending.append(cp.wait_send)
        for a in range(n_now, n):
            loads[a - n_now].wait()
            place(a, stage[a - n_now][...])
        for wait in arrivals:
            wait()
        d2d = _GatherD2d([None] * len(split_now), [items[a][1] for a in split_now])
        starts, waits = d2d.build(None, [outs[a] for a in split_now], sems, 2 * n + 6 * n_now)
        for start in starts:
            start()
        for wait in waits + pending:
            wait()

    vm = pl.BlockSpec(memory_space=pltpu.VMEM)
    return pl.pallas_call(
        body,
        name="place_and_gather",
        in_specs=[vm] * n_now + [ANY] * (n - n_now),
        out_specs=[ANY] * n,
        out_shape=[jax.ShapeDtypeStruct(it[2], it[3]) for it in items],
        scratch_shapes=[pltpu.VMEM(it[0].shape, it[0].dtype) for it in later]
        + [pltpu.VMEM(buf_shape(it), it[3]) for it in items]
        + [pltpu.SemaphoreType.DMA((2 * n + 6 * n_now + 6 * len(split_now),))],
        compiler_params=pltpu.CompilerParams(vmem_limit_bytes=VMEM_LIMIT),
    )(*[it[0] for it in items])


_HBM = pl.BlockSpec(memory_space=pltpu.HBM)
_SEM = pl.BlockSpec(memory_space=pltpu.SEMAPHORE)
_DATAFLOW = pltpu.SideEffectType.DATAFLOW_SIDE_EFFECTING


class _Pending:
    def __init__(self, ops, bases, sems, arrays, token):
        self.ops, self.bases, self.sems, self.arrays, self.token = ops, bases, sems, arrays, token


def _op_refs(op, refs):
    n_src = len(op.inputs())
    return refs[:n_src], (refs[:n_src] if op.aliased else refs[n_src:])


def _start(name, ops, sibling_only=False):
    per_op = [list(op.inputs()) + ([] if op.aliased else [lax.empty(sd.shape, sd.dtype) for sd in op.out_shapes()])
              for op in ops]
    arrays = [a for group in per_op for a in group]
    bases = [sum(op.n_sems() for op in ops[:k]) for k in range(len(ops))]
    n = len(arrays)

    def body(*refs):
        sems, token = refs[n], refs[-1]
        if sibling_only:
            x, y, c = _place()
            barrier = pltpu.get_barrier_semaphore()
            pl.semaphore_signal(barrier, inc=1, device_id=(x, y, 1 - c), device_id_type=MESH)
            pl.semaphore_wait(barrier, 1)
        at = 0
        for op, group, base in zip(ops, per_op, bases):
            starts, _ = op.build(*_op_refs(op, refs[at : at + len(group)]), sems, base)
            for start in starts:
                start()
            at += len(group)
        token[...] = jnp.zeros_like(token)

    res = pl.pallas_call(
        body,
        name=name,
        out_shape=(pltpu.SemaphoreType.DMA((sum(op.n_sems() for op in ops),)),)
        + tuple(pltpu.HBM(a.shape, a.dtype) for a in arrays) + (jax.ShapeDtypeStruct((SUBLANES, LANES), F32),),
        in_specs=(_HBM,) * n,
        out_specs=(_SEM,) + (_HBM,) * n + (pl.BlockSpec(memory_space=pltpu.VMEM),),
        input_output_aliases={i: 1 + i for i in range(n)},
        compiler_params=pltpu.CompilerParams(
            has_side_effects=_DATAFLOW, collective_id=SIBLING_BARRIER_ID if sibling_only else None),
    )(*[pltpu.with_memory_space_constraint(a, pltpu.HBM) for a in arrays])
    thru, at, groups = list(res[1 : 1 + n]), 0, []
    for group in per_op:
        groups.append(thru[at : at + len(group)])
        at += len(group)
    return _Pending(list(ops), bases, res[0], groups, res[-1])


def _wait(name, pending, k, after):
    op, arrays = pending.ops[k], pending.arrays[k]
    n = len(arrays)

    def body(*refs):
        _, waits = op.build(*_op_refs(op, refs[:n]), refs[n], pending.bases[k])
        for wait in waits:
            wait()

    return pl.pallas_call(
        body,
        name=name,
        out_shape=tuple(pltpu.HBM(a.shape, a.dtype) for a in arrays),
        in_specs=(_HBM,) * n + (_SEM, ANY),
        out_specs=(_HBM,) * n,
        input_output_aliases={i: i for i in range(n)},
        compiler_params=pltpu.CompilerParams(has_side_effects=_DATAFLOW),
    )(*arrays, pending.sems, after)


def _in_proj(x, g_mix, w_inT_b, b_in, after=()):
    T, D = x.shape
    CI = w_inT_b.shape[0]
    tm = _tile(T, 512)

    def body(x_ref, g_ref, w_ref, b_ref, z_ref, xn_ref):
        xv = x_ref[...]
        r = lax.rsqrt(jnp.mean(xv * xv, axis=-1, keepdims=True) + RMS_EPS)
        xn = (xv * r * g_ref[...]).astype(BF16)
        xn_ref[...] = xn
        z_ref[...] = _dot(xn, w_ref[...], NT) + b_ref[...]

    return _call(
        "in_proj",
        body,
        (T // tm,),
        [
            pl.BlockSpec((tm, D), lambda i: (i, 0)),
            pl.BlockSpec((1, D), lambda i: (0, 0)),
            pl.BlockSpec((CI, D), lambda i: (0, 0)),
            pl.BlockSpec((1, CI), lambda i: (0, 0)),
        ],
        [pl.BlockSpec((tm, CI), lambda i: (i, 0)), pl.BlockSpec((tm, D), lambda i: (i, 0))],
        [jax.ShapeDtypeStruct((T, CI), F32), jax.ShapeDtypeStruct((T, D), BF16)],
        (x, g_mix, w_inT_b, b_in),
        after=after,
    )


def _fill_shifted(scr):
    n = scr.shape[1] - SUBLANES
    for s in range(1, SUBLANES):
        scr[s, 0:n, :] = scr[0, s : s + n, :]


def _shifted_rows(scr, off, n, cs):
    s = off % SUBLANES
    return scr[s, off - s : off - s + n, cs]


def _pool_mean_minus_token(p_scr, cs, w, cnt, tt):
    tok = p_scr[HALO : HALO + tt, cs]
    s = tok
    for d in range(1, w):
        s = s + p_scr[HALO - d : HALO - d + tt, cs]
    return s / cnt - tok


def _seq_fwd(z, w_dw4, b_dw, ln_g, ln_b, w_pool, s_pool, after=()):
    T, CI = z.shape
    CC = ln_g.shape[1]
    n_grp, G = w_pool.shape[0], w_pool.shape[-1]
    KW = w_dw4.shape[1]
    D = CC + n_grp * G
    tt = _tile(T, 512, HALO)
    per = tt // HALO

    def body(zc_ref, zp_ref, wdw_ref, bdw_ref, lng_ref, lnb_ref, wp_ref, sp_ref, y_ref, v_ref, u_scr, p_scr):
        i = pl.program_id(0)
        first = i == 0
        u_prev = zp_ref[:, 0:CC] * _sigmoid(zp_ref[:, CC : 2 * CC])
        u_scr[0, 0:HALO, :] = jnp.where(first, 0.0, u_prev)
        p_scr[0:HALO, :] = jnp.where(first, 0.0, zp_ref[:, 2 * CC :])
        u_scr[0, HALO:, :] = zc_ref[:, 0:CC] * _sigmoid(zc_ref[:, CC : 2 * CC])
        p_scr[HALO:, :] = zc_ref[:, 2 * CC :]
        _fill_shifted(u_scr)

        for j in range(CC // LANES):
            cs = slice(LANES * j, LANES * (j + 1))
            for rb in range(tt // CONV_ROWS):
                acc = jnp.zeros((CONV_ROWS, LANES), F32)
                for k in range(KW):
                    off = HALO - (KW - 1) + k + rb * CONV_ROWS
                    acc = acc + _shifted_rows(u_scr, off, CONV_ROWS, cs) * wdw_ref[j, k]
                v_ref[rb * CONV_ROWS : (rb + 1) * CONV_ROWS, cs] = acc + bdw_ref[:, cs]

        v = v_ref[...]
        mu = jnp.mean(v, axis=-1, keepdims=True)
        d = v - mu
        var = jnp.mean(d * d, axis=-1, keepdims=True)
        ln = d * lax.rsqrt(var + LN_EPS) * lng_ref[...] + lnb_ref[...]
        y_ref[:, 0:CC] = (ln * _sigmoid(ln)).astype(BF16)

        tpos = i * tt + lax.broadcasted_iota(jnp.int32, (tt, 1), 0)
        for gi, w in enumerate(POOL_WINDOWS):
            cs = slice(G * gi, G * (gi + 1))
            cnt = jnp.minimum(tpos + 1, w).astype(F32)
            yi = _pool_mean_minus_token(p_scr, cs, w, cnt, tt)
            q = _dot(yi.astype(BF16), wp_ref[gi].astype(BF16), NN)
            y_ref[:, CC + G * gi : CC + G * (gi + 1)] = (q * sp_ref[:, cs]).astype(BF16)

    const2 = lambda i: (0, 0)
    return _call(
        "seq_fwd",
        body,
        (T // tt,),
        [
            pl.BlockSpec((tt, CI), lambda i: (i, 0)),
            pl.BlockSpec((HALO, CI), lambda i: (jnp.maximum(i * per - 1, 0), 0)),
            pl.BlockSpec(w_dw4.shape, lambda i: (0,) * w_dw4.ndim),
            pl.BlockSpec((1, CC), const2),
            pl.BlockSpec((1, CC), const2),
            pl.BlockSpec((1, CC), const2),
            pl.BlockSpec(w_pool.shape, lambda i: (0, 0, 0)),
            pl.BlockSpec((1, n_grp * G), const2),
        ],
        [pl.BlockSpec((tt, D), lambda i: (i, 0)), pl.BlockSpec((tt, CC), lambda i: (i, 0))],
        [jax.ShapeDtypeStruct((T, D), BF16), jax.ShapeDtypeStruct((T, CC), F32)],
        (z, z, w_dw4, b_dw, ln_g, ln_b, w_pool, s_pool),
        scratch=[pltpu.VMEM((SUBLANES, HALO + tt, CC), F32), pltpu.VMEM((HALO + tt, n_grp * G), F32)],
        after=after,
    )


def _out_proj(y_b, x, w_out_b, g_ffn, after=()):
    T, D = x.shape
    tm = _tile(T, 512)

    def body(y_ref, x_ref, w_ref, g_ref, h1_ref, hn_ref):
        h1 = x_ref[...] + _dot(y_ref[...], w_ref[...], NN)
        h1_ref[...] = h1
        r = lax.rsqrt(jnp.mean(h1 * h1, axis=-1, keepdims=True) + RMS_EPS)
        hn_ref[...] = (h1 * r * g_ref[...]).astype(BF16)

    row = lambda i: (i, 0)
    return _call(
        "out_proj",
        body,
        (T // tm,),
        [
            pl.BlockSpec((tm, y_b.shape[1]), row),
            pl.BlockSpec((tm, D), row),
            pl.BlockSpec(w_out_b.shape, lambda i: (0, 0)),
            pl.BlockSpec((1, D), lambda i: (0, 0)),
        ],
        [pl.BlockSpec((tm, D), row), pl.BlockSpec((tm, D), row)],
        [jax.ShapeDtypeStruct((T, D), F32), jax.ShapeDtypeStruct((T, D), BF16)],
        (y_b, x, w_out_b, g_ffn),
        after=after,
    )


def _hidden_tile(F):
    return _tile(F, 1408, LANES)


def _gate_up(hn_b, wgT_b, wuT_b):
    T, D = hn_b.shape
    F = wgT_b.shape[0]
    tm, tf = _tile(T, 1024), _hidden_tile(F)

    def body(hn_ref, wg_ref, wu_ref, silu_ref, uds_ref, a_ref):
        hn = hn_ref[...]
        for c0 in range(0, tf, HIDDEN_CHUNK):
            cs = slice(c0, min(c0 + HIDDEN_CHUNK, tf))
            gv = _dot(hn, wg_ref[cs, :], NT)
            uv = _dot(hn, wu_ref[cs, :], NT)
            sg = _sigmoid(gv)
            silu = gv * sg
            silu_ref[:, cs] = silu.astype(BF16)
            uds_ref[:, cs] = (uv * (sg * (1.0 + gv * (1.0 - sg)))).astype(BF16)
            a_ref[:, cs] = (silu * uv).astype(BF16)

    wspec = pl.BlockSpec((tf, D), lambda j, i: (j, 0))
    ospec = pl.BlockSpec((tm, tf), lambda j, i: (i, j))
    return _call(
        "gate_up",
        body,
        (F // tf, T // tm),
        [pl.BlockSpec((tm, D), lambda j, i: (i, 0)), wspec, wspec],
        [ospec, ospec, ospec],
        [jax.ShapeDtypeStruct((T, F), BF16)] * 3,
        (hn_b, wgT_b, wuT_b),
    )


def _down_loss(a_b, wd_b, h1, target, g_final):
    T, D = h1.shape
    F = a_b.shape[1]
    tm = _tile(T, 512)
    nt = T // tm

    def body(a_ref, w_ref, h1_ref, t_ref, g_ref, dh2_ref, dh2b_ref, loss_ref, dg_ref):
        i = pl.program_id(0)
        h2 = h1_ref[...] + _dot(a_ref[...], w_ref[...], NN)
        r = lax.rsqrt(jnp.mean(h2 * h2, axis=-1, keepdims=True) + RMS_EPS)
        g = g_ref[...]
        diff = h2 * r * g - t_ref[...]
        _accumulate(loss_ref, i == 0, jnp.full(loss_ref.shape, jnp.sum(diff * diff) * (0.5 / D), F32))
        dh2, dg_rows = _rms_bwd(h2, g, diff * (1.0 / D))
        dh2_ref[...] = dh2
        dh2b_ref[...] = dh2.astype(BF16)
        _accumulate(dg_ref, i == 0, jnp.sum(dg_rows, axis=0, keepdims=True))

    row = lambda i: (i, 0)
    return _call(
        "down_loss",
        body,
        (nt,),
        [
            pl.BlockSpec((tm, F), row),
            pl.BlockSpec((F, D), lambda i: (0, 0), pipeline_mode=pl.Buffered(1)),
            pl.BlockSpec((tm, D), row),
            pl.BlockSpec((tm, D), row),
            pl.BlockSpec((1, D), lambda i: (0, 0)),
        ],
        [
            pl.BlockSpec((tm, D), row),
            pl.BlockSpec((tm, D), row),
            pl.BlockSpec((1, LANES), lambda i: (0, 0)),
            pl.BlockSpec((1, D), lambda i: (0, 0)),
        ],
        [
            jax.ShapeDtypeStruct((T, D), F32),
            jax.ShapeDtypeStruct((T, D), BF16),
            jax.ShapeDtypeStruct((1, LANES), F32),
            jax.ShapeDtypeStruct((1, D), F32),
        ],
        (a_b, wd_b, h1, target, g_final),
    )


def _ffn_bwd_act(dh2_b, wd_b, silu_b, uds_b, after=()):
    T, D = dh2_b.shape
    F = wd_b.shape[0]
    tm, tf = _tile(T, 1024), _hidden_tile(F)

    def body(d_ref, w_ref, silu_ref, uds_ref, dg_ref, du_ref):
        d = d_ref[...]
        for c0 in range(0, tf, HIDDEN_CHUNK):
            cs = slice(c0, min(c0 + HIDDEN_CHUNK, tf))
            da = _dot(d, w_ref[cs, :], NT)
            dg_ref[:, cs] = (da * uds_ref[:, cs].astype(F32)).astype(BF16)
            du_ref[:, cs] = (da * silu_ref[:, cs].astype(F32)).astype(BF16)

    aspec = pl.BlockSpec((tm, tf), lambda j, i: (i, j))
    return _call(
        "ffn_bwd_act",
        body,
        (F // tf, T // tm),
        [pl.BlockSpec((tm, D), lambda j, i: (i, 0)), pl.BlockSpec((tf, D), lambda j, i: (j, 0)), aspec, aspec],
        [aspec, aspec],
        [jax.ShapeDtypeStruct((T, F), BF16)] * 2,
        (dh2_b, wd_b, silu_b, uds_b),
        after=after,
    )


def _ffn_bwd_in(dg_b, du_b, wgT_b, wuT_b, h1, dh2, g_ffn, w_out_b, comm=()):
    T, D = h1.shape
    F = wgT_b.shape[0]
    DM = w_out_b.shape[0]
    tm = _tile(T, 512)

    def body(dg_ref, du_ref, wg_ref, wu_ref, h1_ref, dh2_ref, g_ref, wo_ref, dh1_ref, dh1b_ref, dy_ref, dgf_ref):
        i = pl.program_id(0)
        dhn = _dot(dg_ref[...], wg_ref[...], NN) + _dot(du_ref[...], wu_ref[...], NN)
        dx, dg_rows = _rms_bwd(h1_ref[...], g_ref[...], dhn)
        dh1 = dh2_ref[...] + dx
        dh1b = dh1.astype(BF16)
        dh1_ref[...] = dh1
        dh1b_ref[...] = dh1b
        dy_ref[...] = _dot(dh1b, wo_ref[...], NT)
        _accumulate(dgf_ref, i == 0, jnp.sum(dg_rows, axis=0, keepdims=True))

    row = lambda i: (i, 0)
    const = lambda i: (0, 0)
    return _call(
        "ffn_bwd_in",
        body,
        (T // tm,),
        [
            pl.BlockSpec((tm, F), row),
            pl.BlockSpec((tm, F), row),
            pl.BlockSpec((F, D), const, pipeline_mode=pl.Buffered(1)),
            pl.BlockSpec((F, D), const, pipeline_mode=pl.Buffered(1)),
            pl.BlockSpec((tm, D), row),
            pl.BlockSpec((tm, D), row),
            pl.BlockSpec((1, D), const),
            pl.BlockSpec((DM, D), const, pipeline_mode=pl.Buffered(1)),
        ],
        [pl.BlockSpec((tm, D), row), pl.BlockSpec((tm, D), row), pl.BlockSpec((tm, DM), row), pl.BlockSpec((1, D), const)],
        [
            jax.ShapeDtypeStruct((T, D), F32),
            jax.ShapeDtypeStruct((T, D), BF16),
            jax.ShapeDtypeStruct((T, DM), F32),
            jax.ShapeDtypeStruct((1, D), F32),
        ],
        (dg_b, du_b, wgT_b, wuT_b, h1, dh2, g_ffn, w_out_b),
        comm=comm,
    )


def _seq_bwd(z, dy, v, w_dw4, ln_g, ln_b, w_pool, s_pool, comm=()):
    T, CI = z.shape
    CC = ln_g.shape[1]
    n_grp, G = w_pool.shape[0], w_pool.shape[-1]
    CP = n_grp * G
    KW = w_dw4.shape[1]
    n_cc = CC // LANES
    D = CC + CP
    tt = _tile(T, 512, HALO)
    per = tt // HALO
    n_tiles = T // tt
    last_halo = T // HALO - 1

    def body(zc_ref, zp_ref, dyc_ref, dyn_ref, vc_ref, vn_ref, wdw_ref, lng_ref, lnb_ref, wp_ref, sp_ref,
             dz_ref, dwdw_ref, dbdw_ref, dlng_ref, dlnb_ref, dwp_ref, dsp_ref, dbin_ref,
             dv_scr, u_scr, p_scr, g_scr, dw_scr):
        i = pl.program_id(0)
        first = i == 0
        last = i == n_tiles - 1
        lng, lnb = lng_ref[...], lnb_ref[...]

        def conv_pre(vv, dyc):
            mu = jnp.mean(vv, axis=-1, keepdims=True)
            d = vv - mu
            rs = lax.rsqrt(jnp.mean(d * d, axis=-1, keepdims=True) + LN_EPS)
            xh = d * rs
            ln = xh * lng + lnb
            sg = _sigmoid(ln)
            dln = dyc * (sg * (1.0 + ln * (1.0 - sg)))
            dxh = dln * lng
            dv = rs * (dxh - jnp.mean(dxh, axis=-1, keepdims=True) - xh * jnp.mean(dxh * xh, axis=-1, keepdims=True))
            return dv, dln, xh

        dv_c, dln_c, xh_c = conv_pre(vc_ref[...], dyc_ref[:, 0:CC])
        dv_scr[0, 0:tt, :] = dv_c
        dv_n, _, _ = conv_pre(vn_ref[...], dyn_ref[:, 0:CC])
        dv_scr[0, tt:, :] = jnp.where(last, 0.0, dv_n)
        _fill_shifted(dv_scr)
        _accumulate(dlng_ref, first, jnp.sum(dln_c * xh_c, axis=0, keepdims=True))
        _accumulate(dlnb_ref, first, jnp.sum(dln_c, axis=0, keepdims=True))
        _accumulate(dbdw_ref, first, jnp.sum(dv_c, axis=0, keepdims=True))

        u_scr[...] = zc_ref[:, 0:CC] * _sigmoid(zc_ref[:, CC : 2 * CC])

        @pl.when(first)
        def _():
            dw_scr[...] = jnp.zeros_like(dw_scr)

        for j in range(n_cc):
            cs = slice(LANES * j, LANES * (j + 1))
            gs = slice(CC + LANES * j, CC + LANES * (j + 1))
            dbin_a = jnp.zeros((1, LANES), F32)
            dbin_g = jnp.zeros((1, LANES), F32)
            for rb in range(tt // CONV_ROWS):
                rows = slice(rb * CONV_ROWS, (rb + 1) * CONV_ROWS)
                u_blk = u_scr[rows, cs]
                du = jnp.zeros((CONV_ROWS, LANES), F32)
                for k in range(KW):
                    off = rb * CONV_ROWS + (KW - 1) - k
                    d = _shifted_rows(dv_scr, off, CONV_ROWS, cs)
                    du = du + d * wdw_ref[j, k]
                    dw_scr[j * HALO + k] += jnp.sum((u_blk * d).reshape(CONV_ROWS // 8, 8, LANES), axis=0)
                a = zc_ref[rows, cs]
                sg = _sigmoid(zc_ref[rows, gs])
                da = du * sg
                dgate = du * a * sg * (1.0 - sg)
                dz_ref[rows, cs] = da.astype(BF16)
                dz_ref[rows, gs] = dgate.astype(BF16)
                dbin_a = dbin_a + jnp.sum(da, axis=0, keepdims=True)
                dbin_g = dbin_g + jnp.sum(dgate, axis=0, keepdims=True)
            _accumulate(dbin_ref.at[:, cs], first, dbin_a)
            _accumulate(dbin_ref.at[:, gs], first, dbin_g)

        @pl.when(last)
        def _():
            dwdw_ref[...] = jnp.sum(dw_scr[...], axis=1).reshape(dwdw_ref.shape)

        p_scr[0:HALO, :] = jnp.where(first, 0.0, zp_ref[:, 2 * CC :])
        p_scr[HALO:, :] = zc_ref[:, 2 * CC :]
        tpos = i * tt + lax.broadcasted_iota(jnp.int32, (tt, 1), 0)
        for gi, w in enumerate(POOL_WINDOWS):
            cs = slice(G * gi, G * (gi + 1))
            ys = slice(CC + G * gi, CC + G * (gi + 1))
            ps = slice(2 * CC + G * gi, 2 * CC + G * (gi + 1))
            cnt = jnp.minimum(tpos + 1, w).astype(F32)
            yib = _pool_mean_minus_token(p_scr, cs, w, cnt, tt).astype(BF16)
            wp = wp_ref[gi].astype(BF16)
            sp = sp_ref[:, cs]
            dyp = dyc_ref[:, ys]
            q = _dot(yib, wp, NN)
            _accumulate(dsp_ref.at[:, cs], first, jnp.sum(dyp * q, axis=0, keepdims=True))
            dq_c = (dyp * sp).astype(BF16)
            dq_n = (jnp.where(last, 0.0, dyn_ref[:, ys]) * sp).astype(BF16)
            _accumulate(dwp_ref.at[gi], first, _dot(yib, dq_c, TN))
            dyi_c = _dot(dq_c, wp, NT)
            g_scr[0:tt, cs] = dyi_c / cnt
            g_scr[tt:, cs] = _dot(dq_n, wp, NT) * (1.0 / w)
            dp = -dyi_c
            for d in range(w):
                dp = dp + g_scr[d : d + tt, cs]
            dz_ref[:, ps] = dp.astype(BF16)
            _accumulate(dbin_ref.at[:, ps], first, jnp.sum(dp, axis=0, keepdims=True))

    cur = lambda i: (i, 0)
    prev = lambda i: (jnp.maximum(i * per - 1, 0), 0)
    nxt = lambda i: (jnp.minimum((i + 1) * per, last_halo), 0)
    c2 = lambda i: (0, 0)
    c3 = lambda i: (0, 0, 0)
    return _call(
        "seq_bwd",
        body,
        (n_tiles,),
        [
            pl.BlockSpec((tt, CI), cur),
            pl.BlockSpec((HALO, CI), prev),
            pl.BlockSpec((tt, D), cur),
            pl.BlockSpec((HALO, D), nxt),
            pl.BlockSpec((tt, CC), cur),
            pl.BlockSpec((HALO, CC), nxt),
            pl.BlockSpec(w_dw4.shape, lambda i: (0,) * w_dw4.ndim),
            pl.BlockSpec((1, CC), c2),
            pl.BlockSpec((1, CC), c2),
            pl.BlockSpec(w_pool.shape, c3),
            pl.BlockSpec((1, CP), c2),
        ],
        [
            pl.BlockSpec((tt, CI), cur),
            pl.BlockSpec((n_cc, HALO, LANES), c3),
            pl.BlockSpec((1, CC), c2),
            pl.BlockSpec((1, CC), c2),
            pl.BlockSpec((1, CC), c2),
            pl.BlockSpec((n_grp, G, G), c3),
            pl.BlockSpec((1, CP), c2),
            pl.BlockSpec((1, CI), c2),
        ],
        [
            jax.ShapeDtypeStruct((T, CI), BF16),
            jax.ShapeDtypeStruct((n_cc, HALO, LANES), F32),
            jax.ShapeDtypeStruct((1, CC), F32),
            jax.ShapeDtypeStruct((1, CC), F32),
            jax.ShapeDtypeStruct((1, CC), F32),
            jax.ShapeDtypeStruct((n_grp, G, G), F32),
            jax.ShapeDtypeStruct((1, CP), F32),
            jax.ShapeDtypeStruct((1, CI), F32),
        ],
        (z, z, dy, dy, v, v, w_dw4, ln_g, ln_b, w_pool, s_pool),
        scratch=[
            pltpu.VMEM((SUBLANES, tt + HALO, CC), F32),
            pltpu.VMEM((tt, CC), F32),
            pltpu.VMEM((HALO + tt, CP), F32),
            pltpu.VMEM((tt + HALO, CP), F32),
            pltpu.VMEM((n_cc * HALO, 8, LANES), F32),
        ],
        comm=comm,
    )


def _in_proj_bwd(dz_b, w_inT_b, x, dh1, g_mix, after=()):
    T, D = x.shape
    CI = w_inT_b.shape[0]
    tm = _tile(T, 512)

    def body(dz_ref, w_ref, x_ref, dh1_ref, g_ref, dx_ref, dg_ref):
        i = pl.program_id(0)
        dxn = _dot(dz_ref[...], w_ref[...], NN)
        dx, dg_rows = _rms_bwd(x_ref[...], g_ref[...], dxn)
        dx_ref[...] = dh1_ref[...] + dx
        _accumulate(dg_ref, i == 0, jnp.sum(dg_rows, axis=0, keepdims=True))

    row = lambda i: (i, 0)
    const = lambda i: (0, 0)
    return _call(
        "in_proj_bwd",
        body,
        (T // tm,),
        [
            pl.BlockSpec((tm, CI), row),
            pl.BlockSpec((CI, D), const),
            pl.BlockSpec((tm, D), row),
            pl.BlockSpec((tm, D), row),
            pl.BlockSpec((1, D), const),
        ],
        [pl.BlockSpec((tm, D), row), pl.BlockSpec((1, D), const)],
        [jax.ShapeDtypeStruct((T, D), F32), jax.ShapeDtypeStruct((1, D), F32)],
        (dz_b, w_inT_b, x, dh1, g_mix),
        after=after,
    )


def _weight_grad(name, a_b, b_b, after=()):
    T, N1 = a_b.shape
    N2 = b_b.shape[1]
    t1 = _tile(N1, 1408, LANES)
    tk = _tile(T, 2048)
    nk = T // tk

    def body(a_ref, b_ref, o_ref, acc):
        k = pl.program_id(1)
        _accumulate(acc, k == 0, _dot(a_ref[...], b_ref[...], TN))

        @pl.when(k == nk - 1)
        def _():
            o_ref[...] = acc[...].astype(BF16)

    (out,), _ = _call(
        name,
        body,
        (N1 // t1, nk),
        [pl.BlockSpec((tk, t1), lambda n, k: (k, n)), pl.BlockSpec((tk, N2), lambda n, k: (k, 0))],
        [pl.BlockSpec((t1, N2), lambda n, k: (n, 0))],
        [jax.ShapeDtypeStruct((N1, N2), BF16)],
        (a_b, b_b),
        scratch=[pltpu.VMEM((t1, N2), F32)],
        after=after,
    )
    return out


def _sum_parts(name, full, how, parts, me):
    _, R, C = parts[0].shape
    tr = _tile(R, 512)
    nb = R // tr
    where = [(q, r) for q, p in enumerate(parts) for r in range(p.shape[0])]
    assert len(where) == 3

    def body(me_ref, own_ref, *refs):
        o_ref = refs[-1]
        f = lambda j: refs[where[j][0]][where[j][1]].astype(F32)
        o_ref[...] = (own_ref[...].astype(F32) + f(0)) + (f(1) + f(2))

    own_map = {"rows": lambda i, me_ref: (me_ref[0] * nb + i, 0), "all": lambda i, me_ref: (i, 0)}[how]
    return pl.pallas_call(
        body,
        name=name,
        grid_spec=pltpu.PrefetchScalarGridSpec(
            num_scalar_prefetch=1,
            grid=(nb,),
            in_specs=[pl.BlockSpec((tr, C), own_map)]
            + [pl.BlockSpec((p.shape[0], tr, C), lambda i, me_ref: (0, i, 0)) for p in parts],
            out_specs=pl.BlockSpec((tr, C), lambda i, me_ref: (i, 0)),
        ),
        out_shape=jax.ShapeDtypeStruct((R, C), F32),
        compiler_params=pltpu.CompilerParams(dimension_semantics=("arbitrary",), vmem_limit_bytes=VMEM_LIMIT),
    )(me, full, *parts)


_M_CORR = 1.0 - ADAM_B1**ADAM_STEP
_V_CORR = 1.0 - ADAM_B2**ADAM_STEP


def _adamw_math(w, g, m, v):
    m = ADAM_B1 * m + (1.0 - ADAM_B1) * g
    v = ADAM_B2 * v + (1.0 - ADAM_B2) * (g * g)
    delta = -ADAM_LR * ((m / _M_CORR) / (jnp.sqrt(v / _V_CORR) + ADAM_EPS) + ADAM_WD * w)
    return delta, m, v


def _adamw(name, w, m, v, g_here, g_there, g_transposed=False):
    R, C = w.shape
    tr = _tile(R, 256, LANES if g_transposed else 8)

    def body(w_ref, m_ref, v_ref, ga_ref, gb_ref, g_ref, d_ref, nm_ref, nv_ref):
        g = ga_ref[...] + gb_ref[...]
        if g_transposed:
            g = g.T
        g_ref[...] = g
        d_ref[...], nm_ref[...], nv_ref[...] = _adamw_math(w_ref[...], g, m_ref[...], v_ref[...])

    spec = pl.BlockSpec((tr, C), lambda i: (i, 0))
    gspec = pl.BlockSpec((C, tr), lambda i: (0, i)) if g_transposed else spec
    return _call(name, body, (R // tr,), [spec] * 3 + [gspec] * 2, [spec] * 4, [jax.ShapeDtypeStruct((R, C), F32)] * 4,
                 (w, m, v, g_here, g_there))


def _adamw_on_sparsecore(name, w, m, v, g_here, g_there, after):
    R, C = w.shape
    n_groups = R // SUBLANES
    n_turns = -(-n_groups // SC_TILES)
    n_in, n_out = 5, 4

    def body(w_hbm, m_hbm, v_hbm, ga_hbm, gb_hbm, after_hbm, g_out, d_out, nm_out, nv_out, bufs, sems):
        tile = lax.axis_index("subcore") * SC_CORES + lax.axis_index("sparsecore")
        srcs = (w_hbm, m_hbm, v_hbm, ga_hbm, gb_hbm)
        dsts = (d_out, nm_out, nv_out, g_out)

        def rows(turn):
            return pl.ds((tile + turn * SC_TILES) * SUBLANES, SUBLANES)

        def loads(turn):
            slot = turn % 2
            return [pltpu.make_async_copy(srcs[q].at[rows(turn), :], bufs.at[slot, q], sems.at[slot, q]) for q in range(n_in)]

        def stores(turn):
            slot = turn % 2
            return [pltpu.make_async_copy(bufs.at[slot, q], dsts[q].at[rows(turn), :], sems.at[slot, n_in + q])
                    for q in range(n_out)]

        def when_mine(turn, fn):
            pl.when(tile + turn * SC_TILES < n_groups)(fn)

        def compute(slot):
            wb, mb, vb, gab, gbb = (bufs.at[slot, q] for q in range(n_in))

            @pl.loop(0, SUBLANES)
            def _(r):
                @pl.loop(0, C, step=SC_LANES)
                def _(i):
                    at = (r, pl.ds(i, SC_LANES))
                    g = gab[at] + gbb[at]
                    delta, new_m, new_v = _adamw_math(wb[at], g, mb[at], vb[at])
                    gab[at], wb[at], mb[at], vb[at] = g, delta, new_m, new_v

        def start_loads(turn):
            def fn():
                for cp in loads(turn):
                    cp.start()

            when_mine(turn, fn)

        start_loads(0)
        for turn in range(n_turns):
            def step(turn=turn):
                for cp in loads(turn):
                    cp.wait()
                if turn >= 1:
                    for cp in stores(turn - 1):
                        cp.wait()
                if turn + 1 < n_turns:
                    start_loads(turn + 1)
                compute(turn % 2)
                for cp in stores(turn):
                    cp.start()

            when_mine(turn, step)
        for turn in range(n_turns):
            def drain(turn=turn):
                for cp in stores(turn):
                    cp.wait()

            last_mine = jnp.logical_and(tile + turn * SC_TILES < n_groups, tile + (turn + 1) * SC_TILES >= n_groups)
            pl.when(last_mine)(drain)

    return pl.kernel(
        body,
        name=name,
        out_type=[jax.ShapeDtypeStruct((R, C), F32)] * 4,
        mesh=plsc.VectorSubcoreMesh(core_axis_name="sparsecore", subcore_axis_name="subcore"),
        scratch_types=[pltpu.VMEM((2, n_in, SUBLANES, C), F32), pltpu.SemaphoreType.DMA((2, n_in + n_out))],
        compiler_params=pltpu.CompilerParams(use_tc_tiling_on_sc=True),
    )(w, m, v, g_here, g_there, after)


class _PackLayout:
    def __init__(self, n_cc, n_grp, G, widths):
        self.dw_rows = (0, HALO)
        self.wp_rows = (HALO, HALO + G)
        self.n_cc, self.n_grp, self.G = n_cc, n_grp, G
        self.vec = {}
        r = HALO + G
        for name, width in widths:
            self.vec[name] = (r, width)
            r += width // PACK_W
        self.rows = -(-r // 8) * 8


def _pack_small(layout, dwdw, dwp, vecs):
    names = list(vecs)

    def body(*refs):
        dw_ref, wp_ref = refs[0], refs[1]
        vec_refs = refs[2 : 2 + len(names)]
        o_ref = refs[-1]
        o_ref[...] = jnp.zeros_like(o_ref)
        for j in range(layout.n_cc):
            o_ref[layout.dw_rows[0] : layout.dw_rows[1], j * LANES : (j + 1) * LANES] = dw_ref[j]
        for i in range(layout.n_grp):
            o_ref[layout.wp_rows[0] : layout.wp_rows[1], i * layout.G : (i + 1) * layout.G] = wp_ref[i]
        for name, ref in zip(names, vec_refs):
            r, width = layout.vec[name]
            for h in range(width // PACK_W):
                o_ref[r + h : r + h + 1, :] = ref[:, h * PACK_W : (h + 1) * PACK_W]

    return pl.pallas_call(
        body,
        name="pack_small",
        out_shape=jax.ShapeDtypeStruct((layout.rows, PACK_W), F32),
    )(dwdw, dwp, *[vecs[k] for k in names])


def _adamw_small(layout, g_here, g_there, w_dw, m_dw, v_dw, w_pool, m_pool, v_pool, vec_w, vec_m, vec_v, row):
    names = list(vec_w)
    nv = len(names)

    def body(*refs):
        ga_ref, gb_ref = refs[0], refs[1]
        wdw, mdw, vdw, wp, mp, vp = refs[2:8]
        vw, vm, vv = refs[8 : 8 + nv], refs[8 + nv : 8 + 2 * nv], refs[8 + 2 * nv : 8 + 3 * nv]
        row_g, row_w, row_m, row_v = refs[8 + 3 * nv : 12 + 3 * nv]
        outs = refs[12 + 3 * nv :]
        acc = outs[-1]
        acc[...] = ga_ref[...] + gb_ref[...]

        def emit(o, g, w, m, v, idx=()):
            res = (g,) + _adamw_math(w, g, m, v)
            for ref, val in zip(o, res):
                ref[idx] = val

        me = 2 * lax.axis_index("x") + lax.axis_index("y")
        for j in range(layout.n_cc):

            @pl.when(me == j)
            def _(j=j):
                for k in range(wdw.shape[0]):
                    g = acc[layout.dw_rows[0] + k : layout.dw_rows[0] + k + 1, j * LANES : (j + 1) * LANES]
                    emit(outs[0:4], g, wdw[k], mdw[k], vdw[k], idx=k)

        for i in range(layout.n_grp):
            g = acc[layout.wp_rows[0] : layout.wp_rows[1], i * layout.G : (i + 1) * layout.G]
            emit(outs[4:8], g, wp[i], mp[i], vp[i], idx=i)
        for q, name in enumerate(names):
            r, width = layout.vec[name]
            for h in range(width // PACK_W):
                ls = slice(h * PACK_W, (h + 1) * PACK_W)
                g = acc[r + h : r + h + 1, :]
                emit(outs[8 + 4 * q : 12 + 4 * q], g, vw[q][:, ls], vm[q][:, ls], vv[q][:, ls], idx=(slice(None), ls))
        emit(outs[8 + 4 * nv : 12 + 4 * nv], row_g[...], row_w[...], row_m[...], row_v[...], idx=...)

    shapes = [w_dw.shape] * 4 + [w_pool.shape] * 4
    for name in names:
        shapes += [vec_w[name].shape] * 4
    shapes += [row[1].shape] * 4
    return pl.pallas_call(
        body,
        name="adamw_small",
        out_shape=[jax.ShapeDtypeStruct(s, F32) for s in shapes],
        scratch_shapes=[pltpu.VMEM(g_here.shape, F32)],
    )(g_here, g_there, w_dw, m_dw, v_dw, w_pool, m_pool, v_pool,
      *[vec_w[k] for k in names], *[vec_m[k] for k in names], *[vec_v[k] for k in names], *row)


def _allreduce_rows(g_part, loss_part, comm=()):
    n_pairs = N_DEV - 1

    def body(g_ref, l_ref, go_ref, lo_ref, land_g, land_l, sems):
        x, y, c = _place()
        copies = []
        for q, (src, land) in enumerate(((g_ref, land_g), (l_ref, land_l))):
            for r in range(1, N_DEV):
                fx, fy, fc = (r >> 2) & 1, (r >> 1) & 1, r & 1
                peer = (1 - x if fx else x, 1 - y if fy else y, 1 - c if fc else c)
                cp = _remote(src, land.at[r], sems, 2 * (q * n_pairs + r - 1), peer)
                cp.start()
                copies.append(cp)
        for cp in copies:
            cp.wait()

        def total(src, land):
            row = lambda r: src[...] if r == 0 else land[r]
            return ((row(0) + row(4)) + (row(2) + row(6))) + ((row(1) + row(5)) + (row(3) + row(7)))

        go_ref[...] = total(g_ref, land_g)
        lo_ref[...] = total(l_ref, land_l)

    vm = pl.BlockSpec(memory_space=pltpu.VMEM)
    return _call(
        "allreduce_rows",
        body,
        (),
        [vm] * 2,
        [vm] * 2,
        [jax.ShapeDtypeStruct(g_part.shape, F32), jax.ShapeDtypeStruct(loss_part.shape, F32)],
        (g_part, loss_part),
        scratch=[pltpu.VMEM((N_DEV,) + g_part.shape, F32), pltpu.VMEM((N_DEV,) + loss_part.shape, F32),
                 pltpu.SemaphoreType.DMA((4 * n_pairs,))],
        comm=comm,
    )


def kernel(x, g_mix, w_in, b_in, w_dw, b_dw, ln_g, ln_b, w_pool, s_pool, w_out, g_ffn, w_gate, w_up, w_down, g_final, loss_target, m_g_mix, m_w_in, m_b_in, m_w_dw, m_b_dw, m_ln_g, m_ln_b, m_w_pool, m_s_pool, m_w_out, m_g_ffn, m_w_gate, m_w_up, m_w_down, m_g_final, v_g_mix, v_w_in, v_b_in, v_w_dw, v_b_dw, v_ln_g, v_ln_b, v_w_pool, v_s_pool, v_w_out, v_g_ffn, v_w_gate, v_w_up, v_w_down, v_g_final):
    x2 = x[0]
    target = loss_target[0]
    T, D = x2.shape
    w_in2, w_out2, w_down2 = w_in[0], w_out[0], w_down[0]
    taps_first = lambda a: jnp.transpose(a, (1, 0, 2))
    w_dw3 = taps_first(w_dw)
    w_gateT, w_upT = w_gate[0].T, w_up[0].T
    CI = w_in2.shape[1] * N_CHIPS
    DM = w_out2.shape[0] * N_CHIPS
    F = w_down2.shape[0] * N_CHIPS
    KW, _, dw_cols = w_dw3.shape
    assert dw_cols == LANES
    n_grp, G = w_pool.shape[1], w_pool.shape[-1]
    w_pool3 = w_pool[0]
    g_final2 = g_final.reshape(1, D)

    me = (2 * lax.axis_index("x") + lax.axis_index("y")).astype(jnp.int32).reshape(1)

    w_inT_b, w_dw4, f_out, f_gate, f_up, f_down = _place_and_gather(
        [(w_in2, "rows", (CI, D), BF16, True, True), (w_dw3, "lead", (N_CHIPS, KW, 1, dw_cols), F32, False, False)],
        [(w, "rows", shape, BF16, False, True)
         for w, shape in ((w_out2, (DM, D)), (w_gateT, (F, D)), (w_upT, (F, D)), (w_down2, (F, D)))])
    ici = lambda f: _GatherIci([f], ["rows"], [True])
    d2d = lambda f: _GatherD2d([f], ["rows"])
    gather = _start("gather_start", [ici(f_out), ici(f_gate), ici(f_up), ici(f_down)])
    (z, xn_b), _ = _in_proj(x2, g_mix, w_inT_b, b_in, after=[gather.token])
    (f_out,) = _wait("gather_out_wait", gather, 0, xn_b)
    s_out = _start("share_out_start", [d2d(f_out)], sibling_only=True)
    (y_b, v), _ = _seq_fwd(z, w_dw4, b_dw, ln_g, ln_b, w_pool3, s_pool, after=[s_out.token])
    (w_out_b,) = _wait("share_out_wait", s_out, 0, y_b)
    (f_gate,) = _wait("gather_gate_wait", gather, 1, y_b)
    s_gate = _start("share_gate_start", [d2d(f_gate)], sibling_only=True)
    (h1, hn_b), _ = _out_proj(y_b, x2, w_out_b, g_ffn, after=[s_gate.token])
    (f_up,) = _wait("gather_up_wait", gather, 2, hn_b)
    s_up = _start("share_up_start", [d2d(f_up)], sibling_only=True)
    (wgT_b,) = _wait("share_gate_wait", s_gate, 0, hn_b)
    (wuT_b,) = _wait("share_up_wait", s_up, 0, hn_b)
    (silu_b, uds_b, a_b), _ = _gate_up(hn_b, wgT_b, wuT_b)
    (f_down,) = _wait("gather_down_wait", gather, 3, a_b)
    s_down = _start("share_down_start", [d2d(f_down)], sibling_only=True)
    (wd_b,) = _wait("share_down_wait", s_down, 0, a_b)
    (dh2, dh2_b, loss_part, d_g_final), _ = _down_loss(a_b, wd_b, h1, target, g_final2)

    gw_down = _weight_grad("grad_w_down", a_b, dh2_b)
    x_down = _start("scatter_down_start", [_Scatter([gw_down], ["rows"])])
    (dg_b, du_b), _ = _ffn_bwd_act(dh2_b, wd_b, silu_b, uds_b, after=[x_down.token])
    gw_gateT = _weight_grad("grad_w_gate", dg_b, hn_b)
    gw_upT = _weight_grad("grad_w_up", du_b, hn_b)
    gw_down, p_down = _wait("scatter_down_wait", x_down, 0, gw_upT)
    sum_down = _sum_parts("sum_w_down", gw_down, "rows", [p_down], me)
    (dh1, dh1_b, dy, d_g_ffn), (p_gate, oth_down) = _ffn_bwd_in(
        dg_b, du_b, wgT_b, wuT_b, h1, dh2, g_ffn, w_out_b, comm=[_Scatter([gw_gateT], ["rows"]), _Swap([sum_down])])
    gw_out = _weight_grad("grad_w_out", y_b, dh1_b)
    sum_gate = _sum_parts("sum_w_gate", gw_gateT, "rows", [p_gate], me)
    res = {}
    res["w_down"] = _adamw_on_sparsecore("adamw_w_down", w_down2, m_w_down[0], v_w_down[0], sum_down, oth_down, sum_down)
    (dz_b, d_wdw, d_bdw, d_lng, d_lnb, d_wp, d_sp, d_bin), (p_up, p_out, oth_gate) = _seq_bwd(
        z, dy, v, w_dw4, ln_g, ln_b, w_pool3, s_pool,
        comm=[_Scatter([gw_upT, gw_out], ["rows", "rows"]), _Swap([sum_gate])])
    res["w_gate"] = _adamw_on_sparsecore(
        "adamw_w_gate", w_gateT, m_w_gate[0].T, v_w_gate[0].T, sum_gate, oth_gate, res["w_down"][0])
    vec_grads ={"b_dw": d_bdw, "ln_g": d_lng, "ln_b": d_lnb, "s_pool": d_sp, "g_ffn": d_g_ffn, "g_final": d_g_final, "b_in": d_bin}
    layout = _PackLayout(dw_cols * N_CHIPS // LANES, n_grp, G, [(k, a.shape[1]) for k, a in vec_grads.items()])
    pack = _pack_small(layout, d_wdw, d_wp, vec_grads)
    sum_up = _sum_parts("sum_w_up", gw_upT, "rows", [p_up], me)
    sum_out = _sum_parts("sum_w_out", gw_out, "rows", [p_out], me)
    mid = _start("mid_start", [_Swap([sum_up, sum_out]), _Scatter([pack], ["all"])])
    gw_inT = _weight_grad("grad_w_in", dz_b, xn_b, after=[mid.token])
    sum_up, sum_out, oth_up, oth_out = _wait("mid_swap_wait", mid, 0, gw_inT)
    late = _start("late_start", [_Scatter([gw_inT], ["rows"])])
    (grad_x, d_g_mix), _ = _in_proj_bwd(dz_b, w_inT_b, x2, dh1, g_mix, after=[late.token])
    pack, p_small = _wait("mid_small_wait", mid, 1, d_g_mix)
    gw_inT, p_in = _wait("late_w_in_wait", late, 0, d_g_mix)
    sum_small = _sum_parts("sum_small", pack, "all", [p_small], me)
    res["w_up"] = _adamw_on_sparsecore("adamw_w_up", w_upT, m_w_up[0].T, v_w_up[0].T, sum_up, oth_up, res["w_gate"][0])
    res["w_out"] = _adamw_on_sparsecore("adamw_w_out", w_out2, m_w_out[0], v_w_out[0], sum_out, oth_out, res["w_gate"][0])
    sum_in = _sum_parts("sum_w_in", gw_inT, "rows", [p_in], me)
    (d_g_mix, loss_row), (oth_in, oth_small) = _allreduce_rows(d_g_mix, loss_part, comm=[_Swap([sum_in, sum_small])])
    loss = loss_row[0, 0]
    res["w_in"], _ = _adamw("adamw_w_in", w_in2, m_w_in[0], v_w_in[0], sum_in, oth_in, g_transposed=True)

    vec_w = {"b_dw": b_dw, "ln_g": ln_g, "ln_b": ln_b, "s_pool": s_pool, "g_ffn": g_ffn, "g_final": g_final2, "b_in": b_in}
    vec_m = {"b_dw": m_b_dw, "ln_g": m_ln_g, "ln_b": m_ln_b, "s_pool": m_s_pool, "g_ffn": m_g_ffn,
             "g_final": m_g_final.reshape(1, D), "b_in": m_b_in}
    vec_v = {"b_dw": v_b_dw, "ln_g": v_ln_g, "ln_b": v_ln_b, "s_pool": v_s_pool, "g_ffn": v_g_ffn,
             "g_final": v_g_final.reshape(1, D), "b_in": v_b_in}
    small = _adamw_small(layout, sum_small, oth_small, w_dw3, taps_first(m_w_dw), taps_first(v_w_dw),
                         w_pool3, m_w_pool[0], v_w_pool[0], vec_w, vec_m, vec_v, (d_g_mix, g_mix, m_g_mix, v_g_mix))
    res["w_dw"] = [taps_first(a) for a in small[0:4]]
    res["w_pool"] = [a[None] for a in small[4:8]]
    for q, k in enumerate(vec_w):
        res[k] = list(small[8 + 4 * q : 12 + 4 * q])
    res["g_mix"] = list(small[-4:])
    res["g_final"] = [a.reshape(D) for a in res["g_final"]]
    for k in ("w_in", "w_out", "w_down"):
        res[k] = [a[None] for a in res[k]]
    for k in ("w_gate", "w_up"):
        res[k] = [a.T[None] for a in res[k]]

    order = ["g_mix", "w_in", "b_in", "w_dw", "b_dw", "ln_g", "ln_b", "w_pool", "s_pool", "w_out", "g_ffn", "w_gate", "w_up", "w_down", "g_final"]
    outs = [loss, grad_x[None]]
    for q in range(4):
        outs += [res[k][q] for k in order]
    return tuple(outs)
```

```python
import jax
import jax.numpy as jnp
from jax import lax
from jax.experimental import pallas as pl
from jax.experimental.pallas import tpu as pltpu
from jax.experimental.pallas import tpu_sc as plsc

F32 = jnp.float32
BF16 = jnp.bfloat16
MESH = pl.DeviceIdType.MESH
ANY = pl.BlockSpec(memory_space=pl.ANY)

RMS_EPS = 1e-6
LN_EPS = 1e-5
POOL_WINDOWS = (2, 4, 8, 16)
ADAM_LR = 0.001
ADAM_B1 = 0.9
ADAM_B2 = 0.999
ADAM_EPS = 1e-08
ADAM_WD = 0.01
ADAM_STEP = 10

LANES = 128
SUBLANES = 8
BF16_ROWS = 16
HALO = 32
CONV_ROWS = 64
HIDDEN_CHUNK = 512
VMEM_LIMIT = 56 * 1024 * 1024
PACK_W = 512
N_CHIPS = 4
N_DEV = 8
SIBLING_BARRIER_ID = 0
SC_CORES = 2
SC_TILES = 32
SC_LANES = 16


def _tile(n, want, mult=8):
    t = min(n, want)
    while n % t or t % mult:
        t -= 1
    return t


def _sigmoid(x):
    return 1.0 / (1.0 + jnp.exp(-x))


def _dot(a, b, dims):
    return lax.dot_general(a, b, (dims, ((), ())), preferred_element_type=F32)


NN = ((1,), (0,))
NT = ((1,), (1,))
TN = ((0,), (0,))


def _rms_bwd(x, g, dy):
    r = lax.rsqrt(jnp.mean(x * x, axis=-1, keepdims=True) + RMS_EPS)
    xh = x * r
    gy = dy * g
    dx = r * (gy - xh * jnp.mean(gy * xh, axis=-1, keepdims=True))
    return dx, dy * xh


def _accumulate(ref, first, val):
    @pl.when(first)
    def _():
        ref[...] = val

    @pl.when(jnp.logical_not(first))
    def _():
        ref[...] += val


def _place():
    return lax.axis_index("x"), lax.axis_index("y"), lax.axis_index("c")


def _other_chips(x, y):
    return [(1 - x, y), (x, 1 - y), (1 - x, 1 - y)]


def _rows(ref, start, n):
    return ref.at[pl.ds(pl.multiple_of(start, BF16_ROWS), n)]


def _window(ref, how, k, c=None):
    if how == "all":
        return ref
    if how == "lead":
        return ref.at[k]
    assert how == "rows"
    n = ref.shape[0] // N_CHIPS
    if c is None:
        return _rows(ref, k * n, n)
    return _rows(ref, k * n + c * (n // 2), n // 2)


def _remote(src, dst, sems, s, device):
    return pltpu.make_async_remote_copy(
        src_ref=src, dst_ref=dst, send_sem=sems.at[s], recv_sem=sems.at[s + 1], device_id=device, device_id_type=MESH)


class _GatherIci:
    aliased = True

    def __init__(self, fulls, hows, splits, which=(0, 1, 2)):
        self.fulls, self.hows, self.splits, self.which = list(fulls), list(hows), list(splits), tuple(which)

    def inputs(self):
        return self.fulls

    def out_shapes(self):
        return [jax.ShapeDtypeStruct(a.shape, a.dtype) for a in self.fulls]

    def n_sems(self):
        return 6 * len(self.fulls)

    def build(self, ins, outs, sems, base):
        x, y, c = _place()
        me = 2 * x + y
        chips = _other_chips(x, y)
        starts, waits = [], []
        for a, (how, sp) in enumerate(zip(self.hows, self.splits)):
            half = c if sp else None
            mine = _window(outs[a], how, me, half)
            for j in self.which:
                px, py = chips[j]
                s = base + 6 * a + 2 * j
                cp = _remote(mine, mine, sems, s, (px, py, c))
                landing = _remote(mine, _window(outs[a], how, 2 * px + py, half), sems, s, (px, py, c))
                starts.append(cp.start)
                waits += [landing.wait_recv, cp.wait_send]
        return starts, waits


class _GatherD2d:
    aliased = True

    def __init__(self, fulls, hows):
        self.fulls, self.hows = list(fulls), list(hows)

    def inputs(self):
        return self.fulls

    def out_shapes(self):
        return [jax.ShapeDtypeStruct(a.shape, a.dtype) for a in self.fulls]

    def n_sems(self):
        return 6 * len(self.fulls)

    def build(self, ins, outs, sems, base):
        x, y, c = _place()
        starts, waits = [], []
        for a, how in enumerate(self.hows):
            for j, (px, py) in enumerate(_other_chips(x, y)):
                s = base + 6 * a + 2 * j
                got = _window(outs[a], how, 2 * px + py, c)
                cp = _remote(got, got, sems, s, (x, y, 1 - c))
                landing = _remote(got, _window(outs[a], how, 2 * px + py, 1 - c), sems, s, (x, y, 1 - c))
                starts.append(cp.start)
                waits += [landing.wait_recv, cp.wait_send]
        return starts, waits


def _part_shape(a, how):
    if how == "all":
        return a.shape
    assert how == "rows"
    return (a.shape[0] // N_CHIPS, a.shape[1])


class _Scatter:
    aliased = False

    def __init__(self, fulls, hows, which=(0, 1, 2)):
        self.fulls, self.hows, self.which = list(fulls), list(hows), tuple(which)

    def inputs(self):
        return self.fulls

    def out_shapes(self):
        return [jax.ShapeDtypeStruct((len(self.which),) + _part_shape(a, h), a.dtype) for a, h in zip(self.fulls, self.hows)]

    def n_sems(self):
        return 6 * len(self.fulls)

    def build(self, ins, outs, sems, base):
        x, y, c = _place()
        chips = _other_chips(x, y)
        starts, waits = [], []
        for a, how in enumerate(self.hows):
            for slot, j in enumerate(self.which):
                px, py = chips[j]
                cp = _remote(_window(ins[a], how, 2 * px + py), outs[a].at[slot], sems, base + 6 * a + 2 * j, (px, py, c))
                starts.append(cp.start)
                waits += [cp.wait_recv, cp.wait_send]
        return starts, waits


class _Swap:
    aliased = False

    def __init__(self, arrays):
        self.arrays = list(arrays)

    def inputs(self):
        return self.arrays

    def out_shapes(self):
        return [jax.ShapeDtypeStruct(a.shape, a.dtype) for a in self.arrays]

    def n_sems(self):
        return 2 * len(self.arrays)

    def build(self, ins, outs, sems, base):
        x, y, c = _place()
        starts, waits = [], []
        for a in range(len(ins)):
            cp = _remote(ins[a], outs[a], sems, base + 2 * a, (x, y, 1 - c))
            starts.append(cp.start)
            waits += [cp.wait_recv, cp.wait_send]
        return starts, waits


def _call(name, body, grid, in_specs, out_specs, out_shape, args, scratch=(), comm=(), after=()):
    comm, after = list(comm), list(after)
    n_in, n_out, n_scr, n_after = len(args), len(out_shape), len(scratch), len(after)
    c_in = [a for op in comm for a in op.inputs()]
    c_out = [s for op in comm for s in op.out_shapes()]
    n_sems = sum(op.n_sems() for op in comm)
    aliases, i_in, i_out = {}, 0, 0
    for op in comm:
        if op.aliased:
            for q in range(len(op.inputs())):
                aliases[n_in + n_after + i_in + q] = n_out + i_out + q
        i_in, i_out = i_in + len(op.inputs()), i_out + len(op.out_shapes())

    def wrapped(*refs):
        ins = refs[:n_in]
        cin = refs[n_in + n_after : n_in + n_after + len(c_in)]
        o0 = n_in + n_after + len(c_in)
        outs = refs[o0 : o0 + n_out]
        cout = refs[o0 + n_out : o0 + n_out + len(c_out)]
        s0 = o0 + n_out + len(c_out)
        scr = refs[s0 : s0 + n_scr]

        def copies():
            sems = refs[s0 + n_scr]
            starts, waits = [], []
            i_in = i_out = base = 0
            for op in comm:
                ni, no = len(op.inputs()), len(op.out_shapes())
                s, w = op.build(cin[i_in : i_in + ni], cout[i_out : i_out + no], sems, base)
                starts += s
                waits += w
                i_in, i_out, base = i_in + ni, i_out + no, base + op.n_sems()
            return starts, waits

        def run_starts():
            for start in copies()[0]:
                start()

        def run_waits():
            for wait in copies()[1]:
                wait()

        if comm and grid:
            first = last = True
            for d, n in enumerate(grid):
                first = jnp.logical_and(first, pl.program_id(d) == 0)
                last = jnp.logical_and(last, pl.program_id(d) == n - 1)
            pl.when(first)(run_starts)
        elif comm:
            run_starts()
        if body is not None:
            body(*ins, *outs, *scr)
        if comm and grid:
            pl.when(last)(run_waits)
        elif comm:
            run_waits()

    res = pl.pallas_call(
        wrapped,
        name=name,
        grid=grid,
        in_specs=list(in_specs) + [ANY] * (n_after + len(c_in)),
        out_specs=list(out_specs) + [ANY] * len(c_out),
        out_shape=list(out_shape) + c_out,
        scratch_shapes=list(scratch) + ([pltpu.SemaphoreType.DMA((n_sems,))] if comm else []),
        input_output_aliases=aliases,
        compiler_params=pltpu.CompilerParams(dimension_semantics=("arbitrary",) * len(grid), vmem_limit_bytes=VMEM_LIMIT),
    )(*args, *after, *c_in)
    return tuple(res[:n_out]), tuple(res[n_out:])


def _place_and_gather(now, later):
    items = list(now) + list(later)
    n, n_now = len(items), len(now)
    buf_shape = lambda it: it[0].shape[::-1] if it[4] else it[0].shape
    split_now = [a for a in range(n_now) if items[a][5]]

    def body(*refs):
        ins, outs = refs[:n], refs[n : 2 * n]
        stage, bufs = refs[2 * n : 3 * n - n_now], refs[3 * n - n_now : 4 * n - n_now]
        sems = refs[4 * n - n_now]
        x, y, c = _place()
        me = 2 * x + y
        chips = _other_chips(x, y)
        loads = [pltpu.make_async_copy(ins[a], stage[a - n_now], sems.at[a]) for a in range(n_now, n)]
        for ld in loads:
            ld.start()
        pending = []

        def place(a, val):
            _, how, _, dtype, transposed, _ = items[a]
            bufs[a][...] = (val.T if transposed else val).astype(dtype)
            cp = pltpu.make_async_copy(bufs[a], _window(outs[a], how, me), sems.at[n + a])
            cp.start()
            pending.append(cp.wait)

        arrivals = []
        for a in range(n_now):
            place(a, ins[a][...])
            how, split = items[a][1], items[a][5]
            half = c if split else None
            src = _rows(bufs[a], c * (bufs[a].shape[0] // 2), bufs[a].shape[0] // 2) if split else bufs[a]
            for j, (px, py) in enumerate(chips):
                s = 2 * n + 6 * a + 2 * j
                cp = _remote(src, _window(outs[a], how, me, half), sems, s, (px, py, c))
                landing = _remote(src, _window(outs[a], how, 2 * px + py, half), sems, s, (px, py, c))
                cp.start()
                arrivals.append(landing.wait_recv)
                pending.append(cp.wait_send)
        for a in range(n_now, n):
            loads[a - n_now].wait()
            place(a, stage[a - n_now][...])
        for wait in arrivals:
            wait()
        d2d = _GatherD2d([None] * len(split_now), [items[a][1] for a in split_now])
        starts, waits = d2d.build(None, [outs[a] for a in split_now], sems, 2 * n + 6 * n_now)
        for start in starts:
            start()
        for wait in waits + pending:
            wait()

    vm = pl.BlockSpec(memory_space=pltpu.VMEM)
    return pl.pallas_call(
        body,
        name="place_and_gather",
        in_specs=[vm] * n_now + [ANY] * (n - n_now),
        out_specs=[ANY] * n,
        out_shape=[jax.ShapeDtypeStruct(it[2], it[3]) for it in items],
        scratch_shapes=[pltpu.VMEM(it[0].shape, it[0].dtype) for it in later]
        + [pltpu.VMEM(buf_shape(it), it[3]) for it in items]
        + [pltpu.SemaphoreType.DMA((2 * n + 6 * n_now + 6 * len(split_now),))],
        compiler_params=pltpu.CompilerParams(vmem_limit_bytes=VMEM_LIMIT),
    )(*[it[0] for it in items])


_HBM = pl.BlockSpec(memory_space=pltpu.HBM)
_SEM = pl.BlockSpec(memory_space=pltpu.SEMAPHORE)
_DATAFLOW = pltpu.SideEffectType.DATAFLOW_SIDE_EFFECTING


class _Pending:
    def __init__(self, ops, bases, sems, arrays, token):
        self.ops, self.bases, self.sems, self.arrays, self.token = ops, bases, sems, arrays, token


def _op_refs(op, refs):
    n_src = len(op.inputs())
    return refs[:n_src], (refs[:n_src] if op.aliased else refs[n_src:])


def _start(name, ops, sibling_only=False, after=()):
    per_op = [list(op.inputs()) + ([] if op.aliased else [lax.empty(sd.shape, sd.dtype) for sd in op.out_shapes()])
              for op in ops]
    arrays = [a for group in per_op for a in group]
    bases = [sum(op.n_sems() for op in ops[:k]) for k in range(len(ops))]
    n = len(arrays)

    def body(*refs):
        sems, token = refs[n + len(after)], refs[-1]
        if sibling_only:
            x, y, c = _place()
            barrier = pltpu.get_barrier_semaphore()
            pl.semaphore_signal(barrier, inc=1, device_id=(x, y, 1 - c), device_id_type=MESH)
            pl.semaphore_wait(barrier, 1)
        at = 0
        for op, group, base in zip(ops, per_op, bases):
            starts, _ = op.build(*_op_refs(op, refs[at : at + len(group)]), sems, base)
            for start in starts:
                start()
            at += len(group)
        token[...] = jnp.zeros_like(token)

    res = pl.pallas_call(
        body,
        name=name,
        out_shape=(pltpu.SemaphoreType.DMA((sum(op.n_sems() for op in ops),)),)
        + tuple(pltpu.HBM(a.shape, a.dtype) for a in arrays) + (jax.ShapeDtypeStruct((SUBLANES, LANES), F32),),
        in_specs=(_HBM,) * n + (ANY,) * len(after),
        out_specs=(_SEM,) + (_HBM,) * n + (pl.BlockSpec(memory_space=pltpu.VMEM),),
        input_output_aliases={i: 1 + i for i in range(n)},
        compiler_params=pltpu.CompilerParams(
            has_side_effects=_DATAFLOW, collective_id=SIBLING_BARRIER_ID if sibling_only else None),
    )(*[pltpu.with_memory_space_constraint(a, pltpu.HBM) for a in arrays], *after)
    thru, at, groups = list(res[1 : 1 + n]), 0, []
    for group in per_op:
        groups.append(thru[at : at + len(group)])
        at += len(group)
    return _Pending(list(ops), bases, res[0], groups, res[-1])


def _wait(name, pending, k, after):
    op, arrays = pending.ops[k], pending.arrays[k]
    n = len(arrays)

    def body(*refs):
        _, waits = op.build(*_op_refs(op, refs[:n]), refs[n], pending.bases[k])
        for wait in waits:
            wait()

    return pl.pallas_call(
        body,
        name=name,
        out_shape=tuple(pltpu.HBM(a.shape, a.dtype) for a in arrays),
        in_specs=(_HBM,) * n + (_SEM, ANY),
        out_specs=(_HBM,) * n,
        input_output_aliases={i: i for i in range(n)},
        compiler_params=pltpu.CompilerParams(has_side_effects=_DATAFLOW),
    )(*arrays, pending.sems, after)


def _in_proj(x, g_mix, w_inT_b, b_in, after=()):
    T, D = x.shape
    CI = w_inT_b.shape[0]
    tm = _tile(T, 512)

    def body(x_ref, g_ref, w_ref, b_ref, z_ref, xn_ref):
        xv = x_ref[...]
        r = lax.rsqrt(jnp.mean(xv * xv, axis=-1, keepdims=True) + RMS_EPS)
        xn = (xv * r * g_ref[...]).astype(BF16)
        xn_ref[...] = xn
        z_ref[...] = _dot(xn, w_ref[...], NT) + b_ref[...]

    return _call(
        "in_proj",
        body,
        (T // tm,),
        [
            pl.BlockSpec((tm, D), lambda i: (i, 0)),
            pl.BlockSpec((1, D), lambda i: (0, 0)),
            pl.BlockSpec((CI, D), lambda i: (0, 0)),
            pl.BlockSpec((1, CI), lambda i: (0, 0)),
        ],
        [pl.BlockSpec((tm, CI), lambda i: (i, 0)), pl.BlockSpec((tm, D), lambda i: (i, 0))],
        [jax.ShapeDtypeStruct((T, CI), F32), jax.ShapeDtypeStruct((T, D), BF16)],
        (x, g_mix, w_inT_b, b_in),
        after=after,
    )


def _fill_shifted(scr):
    n = scr.shape[1] - SUBLANES
    for s in range(1, SUBLANES):
        scr[s, 0:n, :] = scr[0, s : s + n, :]


def _shifted_rows(scr, off, n, cs):
    s = off % SUBLANES
    return scr[s, off - s : off - s + n, cs]


def _pool_mean_minus_token(p_scr, cs, w, cnt, tt):
    tok = p_scr[HALO : HALO + tt, cs]
    s = tok
    for d in range(1, w):
        s = s + p_scr[HALO - d : HALO - d + tt, cs]
    return s / cnt - tok


def _seq_fwd(z, w_dw4, b_dw, ln_g, ln_b, w_pool, s_pool, after=()):
    T, CI = z.shape
    CC = ln_g.shape[1]
    n_grp, G = w_pool.shape[0], w_pool.shape[-1]
    KW = w_dw4.shape[1]
    D = CC + n_grp * G
    tt = _tile(T, 512, HALO)
    per = tt // HALO

    def body(zc_ref, zp_ref, wdw_ref, bdw_ref, lng_ref, lnb_ref, wp_ref, sp_ref, y_ref, v_ref, u_scr, p_scr):
        i = pl.program_id(0)
        first = i == 0
        u_prev = zp_ref[:, 0:CC] * _sigmoid(zp_ref[:, CC : 2 * CC])
        u_scr[0, 0:HALO, :] = jnp.where(first, 0.0, u_prev)
        p_scr[0:HALO, :] = jnp.where(first, 0.0, zp_ref[:, 2 * CC :])
        u_scr[0, HALO:, :] = zc_ref[:, 0:CC] * _sigmoid(zc_ref[:, CC : 2 * CC])
        p_scr[HALO:, :] = zc_ref[:, 2 * CC :]
        _fill_shifted(u_scr)

        for j in range(CC // LANES):
            cs = slice(LANES * j, LANES * (j + 1))
            for rb in range(tt // CONV_ROWS):
                acc = jnp.zeros((CONV_ROWS, LANES), F32)
                for k in range(KW):
                    off = HALO - (KW - 1) + k + rb * CONV_ROWS
                    acc = acc + _shifted_rows(u_scr, off, CONV_ROWS, cs) * wdw_ref[j, k]
                v_ref[rb * CONV_ROWS : (rb + 1) * CONV_ROWS, cs] = acc + bdw_ref[:, cs]

        v = v_ref[...]
        mu = jnp.mean(v, axis=-1, keepdims=True)
        d = v - mu
        var = jnp.mean(d * d, axis=-1, keepdims=True)
        ln = d * lax.rsqrt(var + LN_EPS) * lng_ref[...] + lnb_ref[...]
        y_ref[:, 0:CC] = (ln * _sigmoid(ln)).astype(BF16)

        tpos = i * tt + lax.broadcasted_iota(jnp.int32, (tt, 1), 0)
        for gi, w in enumerate(POOL_WINDOWS):
            cs = slice(G * gi, G * (gi + 1))
            cnt = jnp.minimum(tpos + 1, w).astype(F32)
            yi = _pool_mean_minus_token(p_scr, cs, w, cnt, tt)
            q = _dot(yi.astype(BF16), wp_ref[gi].astype(BF16), NN)
            y_ref[:, CC + G * gi : CC + G * (gi + 1)] = (q * sp_ref[:, cs]).astype(BF16)

    const2 = lambda i: (0, 0)
    return _call(
        "seq_fwd",
        body,
        (T // tt,),
        [
            pl.BlockSpec((tt, CI), lambda i: (i, 0)),
            pl.BlockSpec((HALO, CI), lambda i: (jnp.maximum(i * per - 1, 0), 0)),
            pl.BlockSpec(w_dw4.shape, lambda i: (0,) * w_dw4.ndim),
            pl.BlockSpec((1, CC), const2),
            pl.BlockSpec((1, CC), const2),
            pl.BlockSpec((1, CC), const2),
            pl.BlockSpec(w_pool.shape, lambda i: (0, 0, 0)),
            pl.BlockSpec((1, n_grp * G), const2),
        ],
        [pl.BlockSpec((tt, D), lambda i: (i, 0)), pl.BlockSpec((tt, CC), lambda i: (i, 0))],
        [jax.ShapeDtypeStruct((T, D), BF16), jax.ShapeDtypeStruct((T, CC), F32)],
        (z, z, w_dw4, b_dw, ln_g, ln_b, w_pool, s_pool),
        scratch=[pltpu.VMEM((SUBLANES, HALO + tt, CC), F32), pltpu.VMEM((HALO + tt, n_grp * G), F32)],
        after=after,
    )


def _out_proj(y_b, x, w_out_b, g_ffn, after=()):
    T, D = x.shape
    tm = _tile(T, 512)

    def body(y_ref, x_ref, w_ref, g_ref, h1_ref, hn_ref):
        h1 = x_ref[...] + _dot(y_ref[...], w_ref[...], NN)
        h1_ref[...] = h1
        r = lax.rsqrt(jnp.mean(h1 * h1, axis=-1, keepdims=True) + RMS_EPS)
        hn_ref[...] = (h1 * r * g_ref[...]).astype(BF16)

    row = lambda i: (i, 0)
    return _call(
        "out_proj",
        body,
        (T // tm,),
        [
            pl.BlockSpec((tm, y_b.shape[1]), row),
            pl.BlockSpec((tm, D), row),
            pl.BlockSpec(w_out_b.shape, lambda i: (0, 0)),
            pl.BlockSpec((1, D), lambda i: (0, 0)),
        ],
        [pl.BlockSpec((tm, D), row), pl.BlockSpec((tm, D), row)],
        [jax.ShapeDtypeStruct((T, D), F32), jax.ShapeDtypeStruct((T, D), BF16)],
        (y_b, x, w_out_b, g_ffn),
        after=after,
    )


def _hidden_tile(F):
    return _tile(F, 1408, LANES)


def _gate_up(hn_b, wgT_b, wuT_b):
    T, D = hn_b.shape
    F = wgT_b.shape[0]
    tm, tf = _tile(T, 1024), _hidden_tile(F)

    def body(hn_ref, wg_ref, wu_ref, silu_ref, uds_ref, a_ref):
        hn = hn_ref[...]
        for c0 in range(0, tf, HIDDEN_CHUNK):
            cs = slice(c0, min(c0 + HIDDEN_CHUNK, tf))
            gv = _dot(hn, wg_ref[cs, :], NT)
            uv = _dot(hn, wu_ref[cs, :], NT)
            sg = _sigmoid(gv)
            silu = gv * sg
            silu_ref[:, cs] = silu.astype(BF16)
            uds_ref[:, cs] = (uv * (sg * (1.0 + gv * (1.0 - sg)))).astype(BF16)
            a_ref[:, cs] = (silu * uv).astype(BF16)

    wspec = pl.BlockSpec((tf, D), lambda j, i: (j, 0))
    ospec = pl.BlockSpec((tm, tf), lambda j, i: (i, j))
    return _call(
        "gate_up",
        body,
        (F // tf, T // tm),
        [pl.BlockSpec((tm, D), lambda j, i: (i, 0)), wspec, wspec],
        [ospec, ospec, ospec],
        [jax.ShapeDtypeStruct((T, F), BF16)] * 3,
        (hn_b, wgT_b, wuT_b),
    )


def _down_loss(a_b, wd_b, h1, target, g_final):
    T, D = h1.shape
    F = a_b.shape[1]
    tm = _tile(T, 512)
    nt = T // tm

    def body(a_ref, w_ref, h1_ref, t_ref, g_ref, dh2_ref, dh2b_ref, loss_ref, dg_ref):
        i = pl.program_id(0)
        h2 = h1_ref[...] + _dot(a_ref[...], w_ref[...], NN)
        r = lax.rsqrt(jnp.mean(h2 * h2, axis=-1, keepdims=True) + RMS_EPS)
        g = g_ref[...]
        diff = h2 * r * g - t_ref[...]
        _accumulate(loss_ref, i == 0, jnp.full(loss_ref.shape, jnp.sum(diff * diff) * (0.5 / D), F32))
        dh2, dg_rows = _rms_bwd(h2, g, diff * (1.0 / D))
        dh2_ref[...] = dh2
        dh2b_ref[...] = dh2.astype(BF16)
        _accumulate(dg_ref, i == 0, jnp.sum(dg_rows, axis=0, keepdims=True))

    row = lambda i: (i, 0)
    return _call(
        "down_loss",
        body,
        (nt,),
        [
            pl.BlockSpec((tm, F), row),
            pl.BlockSpec((F, D), lambda i: (0, 0), pipeline_mode=pl.Buffered(1)),
            pl.BlockSpec((tm, D), row),
            pl.BlockSpec((tm, D), row),
            pl.BlockSpec((1, D), lambda i: (0, 0)),
        ],
        [
            pl.BlockSpec((tm, D), row),
            pl.BlockSpec((tm, D), row),
            pl.BlockSpec((1, LANES), lambda i: (0, 0)),
            pl.BlockSpec((1, D), lambda i: (0, 0)),
        ],
        [
            jax.ShapeDtypeStruct((T, D), F32),
            jax.ShapeDtypeStruct((T, D), BF16),
            jax.ShapeDtypeStruct((1, LANES), F32),
            jax.ShapeDtypeStruct((1, D), F32),
        ],
        (a_b, wd_b, h1, target, g_final),
    )


def _ffn_bwd_act(dh2_b, wd_b, silu_b, uds_b, after=()):
    T, D = dh2_b.shape
    F = wd_b.shape[0]
    tm, tf = _tile(T, 1024), _hidden_tile(F)

    def body(d_ref, w_ref, silu_ref, uds_ref, dg_ref, du_ref):
        d = d_ref[...]
        for c0 in range(0, tf, HIDDEN_CHUNK):
            cs = slice(c0, min(c0 + HIDDEN_CHUNK, tf))
            da = _dot(d, w_ref[cs, :], NT)
            dg_ref[:, cs] = (da * uds_ref[:, cs].astype(F32)).astype(BF16)
            du_ref[:, cs] = (da * silu_ref[:, cs].astype(F32)).astype(BF16)

    aspec = pl.BlockSpec((tm, tf), lambda j, i: (i, j))
    return _call(
        "ffn_bwd_act",
        body,
        (F // tf, T // tm),
        [pl.BlockSpec((tm, D), lambda j, i: (i, 0)), pl.BlockSpec((tf, D), lambda j, i: (j, 0)), aspec, aspec],
        [aspec, aspec],
        [jax.ShapeDtypeStruct((T, F), BF16)] * 2,
        (dh2_b, wd_b, silu_b, uds_b),
        after=after,
    )


def _ffn_bwd_in(dg_b, du_b, wgT_b, wuT_b, h1, dh2, g_ffn, w_out_b, comm=()):
    T, D = h1.shape
    F = wgT_b.shape[0]
    DM = w_out_b.shape[0]
    tm = _tile(T, 512)

    def body(dg_ref, du_ref, wg_ref, wu_ref, h1_ref, dh2_ref, g_ref, wo_ref, dh1_ref, dh1b_ref, dy_ref, dgf_ref):
        i = pl.program_id(0)
        dhn = _dot(dg_ref[...], wg_ref[...], NN) + _dot(du_ref[...], wu_ref[...], NN)
        dx, dg_rows = _rms_bwd(h1_ref[...], g_ref[...], dhn)
        dh1 = dh2_ref[...] + dx
        dh1b = dh1.astype(BF16)
        dh1_ref[...] = dh1
        dh1b_ref[...] = dh1b
        dy_ref[...] = _dot(dh1b, wo_ref[...], NT)
        _accumulate(dgf_ref, i == 0, jnp.sum(dg_rows, axis=0, keepdims=True))

    row = lambda i: (i, 0)
    const = lambda i: (0, 0)
    return _call(
        "ffn_bwd_in",
        body,
        (T // tm,),
        [
            pl.BlockSpec((tm, F), row),
            pl.BlockSpec((tm, F), row),
            pl.BlockSpec((F, D), const, pipeline_mode=pl.Buffered(1)),
            pl.BlockSpec((F, D), const, pipeline_mode=pl.Buffered(1)),
            pl.BlockSpec((tm, D), row),
            pl.BlockSpec((tm, D), row),
            pl.BlockSpec((1, D), const),
            pl.BlockSpec((DM, D), const, pipeline_mode=pl.Buffered(1)),
        ],
        [pl.BlockSpec((tm, D), row), pl.BlockSpec((tm, D), row), pl.BlockSpec((tm, DM), row), pl.BlockSpec((1, D), const)],
        [
            jax.ShapeDtypeStruct((T, D), F32),
            jax.ShapeDtypeStruct((T, D), BF16),
            jax.ShapeDtypeStruct((T, DM), F32),
            jax.ShapeDtypeStruct((1, D), F32),
        ],
        (dg_b, du_b, wgT_b, wuT_b, h1, dh2, g_ffn, w_out_b),
        comm=comm,
    )


def _seq_bwd(z, dy, v, w_dw4, ln_g, ln_b, w_pool, s_pool, comm=()):
    T, CI = z.shape
    CC = ln_g.shape[1]
    n_grp, G = w_pool.shape[0], w_pool.shape[-1]
    CP = n_grp * G
    KW = w_dw4.shape[1]
    n_cc = CC // LANES
    D = CC + CP
    tt = _tile(T, 512, HALO)
    per = tt // HALO
    n_tiles = T // tt
    last_halo = T // HALO - 1

    def body(zc_ref, zp_ref, dyc_ref, dyn_ref, vc_ref, vn_ref, wdw_ref, lng_ref, lnb_ref, wp_ref, sp_ref,
             dz_ref, dwdw_ref, dbdw_ref, dlng_ref, dlnb_ref, dwp_ref, dsp_ref, dbin_ref,
             dv_scr, u_scr, p_scr, g_scr, dw_scr):
        i = pl.program_id(0)
        first = i == 0
        last = i == n_tiles - 1
        lng, lnb = lng_ref[...], lnb_ref[...]

        def conv_pre(vv, dyc):
            mu = jnp.mean(vv, axis=-1, keepdims=True)
            d = vv - mu
            rs = lax.rsqrt(jnp.mean(d * d, axis=-1, keepdims=True) + LN_EPS)
            xh = d * rs
            ln = xh * lng + lnb
            sg = _sigmoid(ln)
            dln = dyc * (sg * (1.0 + ln * (1.0 - sg)))
            dxh = dln * lng
            dv = rs * (dxh - jnp.mean(dxh, axis=-1, keepdims=True) - xh * jnp.mean(dxh * xh, axis=-1, keepdims=True))
            return dv, dln, xh

        dv_c, dln_c, xh_c = conv_pre(vc_ref[...], dyc_ref[:, 0:CC])
        dv_scr[0, 0:tt, :] = dv_c
        dv_n, _, _ = conv_pre(vn_ref[...], dyn_ref[:, 0:CC])
        dv_scr[0, tt:, :] = jnp.where(last, 0.0, dv_n)
        _fill_shifted(dv_scr)
        _accumulate(dlng_ref, first, jnp.sum(dln_c * xh_c, axis=0, keepdims=True))
        _accumulate(dlnb_ref, first, jnp.sum(dln_c, axis=0, keepdims=True))
        _accumulate(dbdw_ref, first, jnp.sum(dv_c, axis=0, keepdims=True))

        u_scr[...] = zc_ref[:, 0:CC] * _sigmoid(zc_ref[:, CC : 2 * CC])

        @pl.when(first)
        def _():
            dw_scr[...] = jnp.zeros_like(dw_scr)

        for j in range(n_cc):
            cs = slice(LANES * j, LANES * (j + 1))
            gs = slice(CC + LANES * j, CC + LANES * (j + 1))
            dbin_a = jnp.zeros((1, LANES), F32)
            dbin_g = jnp.zeros((1, LANES), F32)
            for rb in range(tt // CONV_ROWS):
                rows = slice(rb * CONV_ROWS, (rb + 1) * CONV_ROWS)
                u_blk = u_scr[rows, cs]
                du = jnp.zeros((CONV_ROWS, LANES), F32)
                for k in range(KW):
                    off = rb * CONV_ROWS + (KW - 1) - k
                    d = _shifted_rows(dv_scr, off, CONV_ROWS, cs)
                    du = du + d * wdw_ref[j, k]
                    dw_scr[j * HALO + k] += jnp.sum((u_blk * d).reshape(CONV_ROWS // 8, 8, LANES), axis=0)
                a = zc_ref[rows, cs]
                sg = _sigmoid(zc_ref[rows, gs])
                da = du * sg
                dgate = du * a * sg * (1.0 - sg)
                dz_ref[rows, cs] = da.astype(BF16)
                dz_ref[rows, gs] = dgate.astype(BF16)
                dbin_a = dbin_a + jnp.sum(da, axis=0, keepdims=True)
                dbin_g = dbin_g + jnp.sum(dgate, axis=0, keepdims=True)
            _accumulate(dbin_ref.at[:, cs], first, dbin_a)
            _accumulate(dbin_ref.at[:, gs], first, dbin_g)

        @pl.when(last)
        def _():
            dwdw_ref[...] = jnp.sum(dw_scr[...], axis=1).reshape(dwdw_ref.shape)

        p_scr[0:HALO, :] = jnp.where(first, 0.0, zp_ref[:, 2 * CC :])
        p_scr[HALO:, :] = zc_ref[:, 2 * CC :]
        tpos = i * tt + lax.broadcasted_iota(jnp.int32, (tt, 1), 0)
        for gi, w in enumerate(POOL_WINDOWS):
            cs = slice(G * gi, G * (gi + 1))
            ys = slice(CC + G * gi, CC + G * (gi + 1))
            ps = slice(2 * CC + G * gi, 2 * CC + G * (gi + 1))
            cnt = jnp.minimum(tpos + 1, w).astype(F32)
            yib = _pool_mean_minus_token(p_scr, cs, w, cnt, tt).astype(BF16)
            wp = wp_ref[gi].astype(BF16)
            sp = sp_ref[:, cs]
            dyp = dyc_ref[:, ys]
            q = _dot(yib, wp, NN)
            _accumulate(dsp_ref.at[:, cs], first, jnp.sum(dyp * q, axis=0, keepdims=True))
            dq_c = (dyp * sp).astype(BF16)
            dq_n = (jnp.where(last, 0.0, dyn_ref[:, ys]) * sp).astype(BF16)
            _accumulate(dwp_ref.at[gi], first, _dot(yib, dq_c, TN))
            dyi_c = _dot(dq_c, wp, NT)
            g_scr[0:tt, cs] = dyi_c / cnt
            g_scr[tt:, cs] = _dot(dq_n, wp, NT) * (1.0 / w)
            dp = -dyi_c
            for d in range(w):
                dp = dp + g_scr[d : d + tt, cs]
            dz_ref[:, ps] = dp.astype(BF16)
            _accumulate(dbin_ref.at[:, ps], first, jnp.sum(dp, axis=0, keepdims=True))

    cur = lambda i: (i, 0)
    prev = lambda i: (jnp.maximum(i * per - 1, 0), 0)
    nxt = lambda i: (jnp.minimum((i + 1) * per, last_halo), 0)
    c2 = lambda i: (0, 0)
    c3 = lambda i: (0, 0, 0)
    return _call(
        "seq_bwd",
        body,
        (n_tiles,),
        [
            pl.BlockSpec((tt, CI), cur),
            pl.BlockSpec((HALO, CI), prev),
            pl.BlockSpec((tt, D), cur),
            pl.BlockSpec((HALO, D), nxt),
            pl.BlockSpec((tt, CC), cur),
            pl.BlockSpec((HALO, CC), nxt),
            pl.BlockSpec(w_dw4.shape, lambda i: (0,) * w_dw4.ndim),
            pl.BlockSpec((1, CC), c2),
            pl.BlockSpec((1, CC), c2),
            pl.BlockSpec(w_pool.shape, c3),
            pl.BlockSpec((1, CP), c2),
        ],
        [
            pl.BlockSpec((tt, CI), cur),
            pl.BlockSpec((n_cc, HALO, LANES), c3),
            pl.BlockSpec((1, CC), c2),
            pl.BlockSpec((1, CC), c2),
            pl.BlockSpec((1, CC), c2),
            pl.BlockSpec((n_grp, G, G), c3),
            pl.BlockSpec((1, CP), c2),
            pl.BlockSpec((1, CI), c2),
        ],
        [
            jax.ShapeDtypeStruct((T, CI), BF16),
            jax.ShapeDtypeStruct((n_cc, HALO, LANES), F32),
            jax.ShapeDtypeStruct((1, CC), F32),
            jax.ShapeDtypeStruct((1, CC), F32),
            jax.ShapeDtypeStruct((1, CC), F32),
            jax.ShapeDtypeStruct((n_grp, G, G), F32),
            jax.ShapeDtypeStruct((1, CP), F32),
            jax.ShapeDtypeStruct((1, CI), F32),
        ],
        (z, z, dy, dy, v, v, w_dw4, ln_g, ln_b, w_pool, s_pool),
        scratch=[
            pltpu.VMEM((SUBLANES, tt + HALO, CC), F32),
            pltpu.VMEM((tt, CC), F32),
            pltpu.VMEM((HALO + tt, CP), F32),
            pltpu.VMEM((tt + HALO, CP), F32),
            pltpu.VMEM((n_cc * HALO, 8, LANES), F32),
        ],
        comm=comm,
    )


def _in_proj_bwd(dz_b, w_inT_b, x, dh1, g_mix, after=()):
    T, D = x.shape
    CI = w_inT_b.shape[0]
    tm = _tile(T, 512)

    def body(dz_ref, w_ref, x_ref, dh1_ref, g_ref, dx_ref, dg_ref):
        i = pl.program_id(0)
        dxn = _dot(dz_ref[...], w_ref[...], NN)
        dx, dg_rows = _rms_bwd(x_ref[...], g_ref[...], dxn)
        dx_ref[...] = dh1_ref[...] + dx
        _accumulate(dg_ref, i == 0, jnp.sum(dg_rows, axis=0, keepdims=True))

    row = lambda i: (i, 0)
    const = lambda i: (0, 0)
    return _call(
        "in_proj_bwd",
        body,
        (T // tm,),
        [
            pl.BlockSpec((tm, CI), row),
            pl.BlockSpec((CI, D), const),
            pl.BlockSpec((tm, D), row),
            pl.BlockSpec((tm, D), row),
            pl.BlockSpec((1, D), const),
        ],
        [pl.BlockSpec((tm, D), row), pl.BlockSpec((1, D), const)],
        [jax.ShapeDtypeStruct((T, D), F32), jax.ShapeDtypeStruct((1, D), F32)],
        (dz_b, w_inT_b, x, dh1, g_mix),
        after=after,
    )


def _weight_grad(name, a_b, b_b, after=()):
    T, N1 = a_b.shape
    N2 = b_b.shape[1]
    t1 = _tile(N1, 1408, LANES)
    tk = _tile(T, 2048)
    nk = T // tk

    def body(a_ref, b_ref, o_ref, acc):
        k = pl.program_id(1)
        _accumulate(acc, k == 0, _dot(a_ref[...], b_ref[...], TN))

        @pl.when(k == nk - 1)
        def _():
            o_ref[...] = acc[...].astype(BF16)

    (out,), _ = _call(
        name,
        body,
        (N1 // t1, nk),
        [pl.BlockSpec((tk, t1), lambda n, k: (k, n)), pl.BlockSpec((tk, N2), lambda n, k: (k, 0))],
        [pl.BlockSpec((t1, N2), lambda n, k: (n, 0))],
        [jax.ShapeDtypeStruct((N1, N2), BF16)],
        (a_b, b_b),
        scratch=[pltpu.VMEM((t1, N2), F32)],
        after=after,
    )
    return out


def _sum_parts(name, full, how, parts, me):
    _, R, C = parts[0].shape
    tr = _tile(R, 512)
    nb = R // tr
    where = [(q, r) for q, p in enumerate(parts) for r in range(p.shape[0])]
    assert len(where) == 3

    def body(me_ref, own_ref, *refs):
        o_ref = refs[-1]
        f = lambda j: refs[where[j][0]][where[j][1]].astype(F32)
        o_ref[...] = (own_ref[...].astype(F32) + f(0)) + (f(1) + f(2))

    own_map = {"rows": lambda i, me_ref: (me_ref[0] * nb + i, 0), "all": lambda i, me_ref: (i, 0)}[how]
    return pl.pallas_call(
        body,
        name=name,
        grid_spec=pltpu.PrefetchScalarGridSpec(
            num_scalar_prefetch=1,
            grid=(nb,),
            in_specs=[pl.BlockSpec((tr, C), own_map)]
            + [pl.BlockSpec((p.shape[0], tr, C), lambda i, me_ref: (0, i, 0)) for p in parts],
            out_specs=pl.BlockSpec((tr, C), lambda i, me_ref: (i, 0)),
        ),
        out_shape=jax.ShapeDtypeStruct((R, C), F32),
        compiler_params=pltpu.CompilerParams(dimension_semantics=("arbitrary",), vmem_limit_bytes=VMEM_LIMIT),
    )(me, full, *parts)


_M_CORR = 1.0 - ADAM_B1**ADAM_STEP
_V_CORR = 1.0 - ADAM_B2**ADAM_STEP


def _adamw_math(w, g, m, v):
    m = ADAM_B1 * m + (1.0 - ADAM_B1) * g
    v = ADAM_B2 * v + (1.0 - ADAM_B2) * (g * g)
    delta = -ADAM_LR * ((m / _M_CORR) / (jnp.sqrt(v / _V_CORR) + ADAM_EPS) + ADAM_WD * w)
    return delta, m, v


def _adamw(name, w, m, v, g_here, g_there, g_transposed=False):
    R, C = w.shape
    tr = _tile(R, 256, LANES if g_transposed else 8)

    def body(w_ref, m_ref, v_ref, ga_ref, gb_ref, g_ref, d_ref, nm_ref, nv_ref):
        g = ga_ref[...] + gb_ref[...]
        if g_transposed:
            g = g.T
        g_ref[...] = g
        d_ref[...], nm_ref[...], nv_ref[...] = _adamw_math(w_ref[...], g, m_ref[...], v_ref[...])

    spec = pl.BlockSpec((tr, C), lambda i: (i, 0))
    gspec = pl.BlockSpec((C, tr), lambda i: (0, i)) if g_transposed else spec
    return _call(name, body, (R // tr,), [spec] * 3 + [gspec] * 2, [spec] * 4, [jax.ShapeDtypeStruct((R, C), F32)] * 4,
                 (w, m, v, g_here, g_there))


def _adamw_on_sparsecore(name, w, m, v, g_here, g_there, after):
    R, C = w.shape
    n_groups = R // SUBLANES
    n_turns = -(-n_groups // SC_TILES)
    n_in, n_out = 5, 4

    def body(w_hbm, m_hbm, v_hbm, ga_hbm, gb_hbm, after_hbm, g_out, d_out, nm_out, nv_out, bufs, sems):
        tile = lax.axis_index("subcore") * SC_CORES + lax.axis_index("sparsecore")
        srcs = (w_hbm, m_hbm, v_hbm, ga_hbm, gb_hbm)
        dsts = (d_out, nm_out, nv_out, g_out)

        def rows(turn):
            return pl.ds((tile + turn * SC_TILES) * SUBLANES, SUBLANES)

        def loads(turn):
            slot = turn % 2
            return [pltpu.make_async_copy(srcs[q].at[rows(turn), :], bufs.at[slot, q], sems.at[slot, q]) for q in range(n_in)]

        def stores(turn):
            slot = turn % 2
            return [pltpu.make_async_copy(bufs.at[slot, q], dsts[q].at[rows(turn), :], sems.at[slot, n_in + q])
                    for q in range(n_out)]

        def when_mine(turn, fn):
            pl.when(tile + turn * SC_TILES < n_groups)(fn)

        def compute(slot):
            wb, mb, vb, gab, gbb = (bufs.at[slot, q] for q in range(n_in))

            @pl.loop(0, SUBLANES)
            def _(r):
                @pl.loop(0, C, step=SC_LANES)
                def _(i):
                    at = (r, pl.ds(i, SC_LANES))
                    g = gab[at] + gbb[at]
                    delta, new_m, new_v = _adamw_math(wb[at], g, mb[at], vb[at])
                    gab[at], wb[at], mb[at], vb[at] = g, delta, new_m, new_v

        def start_loads(turn):
            def fn():
                for cp in loads(turn):
                    cp.start()

            when_mine(turn, fn)

        start_loads(0)
        for turn in range(n_turns):
            def step(turn=turn):
                for cp in loads(turn):
                    cp.wait()
                if turn >= 1:
                    for cp in stores(turn - 1):
                        cp.wait()
                if turn + 1 < n_turns:
                    start_loads(turn + 1)
                compute(turn % 2)
                for cp in stores(turn):
                    cp.start()

            when_mine(turn, step)
        for turn in range(n_turns):
            def drain(turn=turn):
                for cp in stores(turn):
                    cp.wait()

            last_mine = jnp.logical_and(tile + turn * SC_TILES < n_groups, tile + (turn + 1) * SC_TILES >= n_groups)
            pl.when(last_mine)(drain)

    return pl.kernel(
        body,
        name=name,
        out_type=[jax.ShapeDtypeStruct((R, C), F32)] * 4,
        mesh=plsc.VectorSubcoreMesh(core_axis_name="sparsecore", subcore_axis_name="subcore"),
        scratch_types=[pltpu.VMEM((2, n_in, SUBLANES, C), F32), pltpu.SemaphoreType.DMA((2, n_in + n_out))],
        compiler_params=pltpu.CompilerParams(use_tc_tiling_on_sc=True),
    )(w, m, v, g_here, g_there, after)


class _PackLayout:
    def __init__(self, n_cc, n_grp, G, widths):
        self.dw_rows = (0, HALO)
        self.wp_rows = (HALO, HALO + G)
        self.n_cc, self.n_grp, self.G = n_cc, n_grp, G
        self.vec = {}
        r = HALO + G
        for name, width in widths:
            self.vec[name] = (r, width)
            r += width // PACK_W
        self.rows = -(-r // 8) * 8


def _pack_small(layout, dwdw, dwp, vecs):
    names = list(vecs)

    def body(*refs):
        dw_ref, wp_ref = refs[0], refs[1]
        vec_refs = refs[2 : 2 + len(names)]
        o_ref = refs[-1]
        o_ref[...] = jnp.zeros_like(o_ref)
        for j in range(layout.n_cc):
            o_ref[layout.dw_rows[0] : layout.dw_rows[1], j * LANES : (j + 1) * LANES] = dw_ref[j]
        for i in range(layout.n_grp):
            o_ref[layout.wp_rows[0] : layout.wp_rows[1], i * layout.G : (i + 1) * layout.G] = wp_ref[i]
        for name, ref in zip(names, vec_refs):
            r, width = layout.vec[name]
            for h in range(width // PACK_W):
                o_ref[r + h : r + h + 1, :] = ref[:, h * PACK_W : (h + 1) * PACK_W]

    return pl.pallas_call(
        body,
        name="pack_small",
        out_shape=jax.ShapeDtypeStruct((layout.rows, PACK_W), F32),
    )(dwdw, dwp, *[vecs[k] for k in names])


def _adamw_small(layout, g_here, g_there, w_dw, m_dw, v_dw, w_pool, m_pool, v_pool, vec_w, vec_m, vec_v, row):
    names = list(vec_w)
    nv = len(names)

    def body(*refs):
        ga_ref, gb_ref = refs[0], refs[1]
        wdw, mdw, vdw, wp, mp, vp = refs[2:8]
        vw, vm, vv = refs[8 : 8 + nv], refs[8 + nv : 8 + 2 * nv], refs[8 + 2 * nv : 8 + 3 * nv]
        row_g, row_w, row_m, row_v = refs[8 + 3 * nv : 12 + 3 * nv]
        outs = refs[12 + 3 * nv :]
        acc = outs[-1]
        acc[...] = ga_ref[...] + gb_ref[...]

        def emit(o, g, w, m, v, idx=()):
            res = (g,) + _adamw_math(w, g, m, v)
            for ref, val in zip(o, res):
                ref[idx] = val

        me = 2 * lax.axis_index("x") + lax.axis_index("y")
        for j in range(layout.n_cc):

            @pl.when(me == j)
            def _(j=j):
                for k in range(wdw.shape[0]):
                    g = acc[layout.dw_rows[0] + k : layout.dw_rows[0] + k + 1, j * LANES : (j + 1) * LANES]
                    emit(outs[0:4], g, wdw[k], mdw[k], vdw[k], idx=k)

        for i in range(layout.n_grp):
            g = acc[layout.wp_rows[0] : layout.wp_rows[1], i * layout.G : (i + 1) * layout.G]
            emit(outs[4:8], g, wp[i], mp[i], vp[i], idx=i)
        for q, name in enumerate(names):
            r, width = layout.vec[name]
            for h in range(width // PACK_W):
                ls = slice(h * PACK_W, (h + 1) * PACK_W)
                g = acc[r + h : r + h + 1, :]
                emit(outs[8 + 4 * q : 12 + 4 * q], g, vw[q][:, ls], vm[q][:, ls], vv[q][:, ls], idx=(slice(None), ls))
        emit(outs[8 + 4 * nv : 12 + 4 * nv], row_g[...], row_w[...], row_m[...], row_v[...], idx=...)

    shapes = [w_dw.shape] * 4 + [w_pool.shape] * 4
    for name in names:
        shapes += [vec_w[name].shape] * 4
    shapes += [row[1].shape] * 4
    return pl.pallas_call(
        body,
        name="adamw_small",
        out_shape=[jax.ShapeDtypeStruct(s, F32) for s in shapes],
        scratch_shapes=[pltpu.VMEM(g_here.shape, F32)],
    )(g_here, g_there, w_dw, m_dw, v_dw, w_pool, m_pool, v_pool,
      *[vec_w[k] for k in names], *[vec_m[k] for k in names], *[vec_v[k] for k in names], *row)


def _allreduce_rows(g_part, loss_part, comm=()):
    n_pairs = N_DEV - 1

    def body(g_ref, l_ref, go_ref, lo_ref, land_g, land_l, sems):
        x, y, c = _place()
        copies = []
        for q, (src, land) in enumerate(((g_ref, land_g), (l_ref, land_l))):
            for r in range(1, N_DEV):
                fx, fy, fc = (r >> 2) & 1, (r >> 1) & 1, r & 1
                peer = (1 - x if fx else x, 1 - y if fy else y, 1 - c if fc else c)
                cp = _remote(src, land.at[r], sems, 2 * (q * n_pairs + r - 1), peer)
                cp.start()
                copies.append(cp)
        for cp in copies:
            cp.wait()

        def total(src, land):
            row = lambda r: src[...] if r == 0 else land[r]
            return ((row(0) + row(4)) + (row(2) + row(6))) + ((row(1) + row(5)) + (row(3) + row(7)))

        go_ref[...] = total(g_ref, land_g)
        lo_ref[...] = total(l_ref, land_l)

    vm = pl.BlockSpec(memory_space=pltpu.VMEM)
    return _call(
        "allreduce_rows",
        body,
        (),
        [vm] * 2,
        [vm] * 2,
        [jax.ShapeDtypeStruct(g_part.shape, F32), jax.ShapeDtypeStruct(loss_part.shape, F32)],
        (g_part, loss_part),
        scratch=[pltpu.VMEM((N_DEV,) + g_part.shape, F32), pltpu.VMEM((N_DEV,) + loss_part.shape, F32),
                 pltpu.SemaphoreType.DMA((4 * n_pairs,))],
        comm=comm,
    )


def kernel(x, g_mix, w_in, b_in, w_dw, b_dw, ln_g, ln_b, w_pool, s_pool, w_out, g_ffn, w_gate, w_up, w_down, g_final, loss_target, m_g_mix, m_w_in, m_b_in, m_w_dw, m_b_dw, m_ln_g, m_ln_b, m_w_pool, m_s_pool, m_w_out, m_g_ffn, m_w_gate, m_w_up, m_w_down, m_g_final, v_g_mix, v_w_in, v_b_in, v_w_dw, v_b_dw, v_ln_g, v_ln_b, v_w_pool, v_s_pool, v_w_out, v_g_ffn, v_w_gate, v_w_up, v_w_down, v_g_final):
    x2 = x[0]
    target = loss_target[0]
    T, D = x2.shape
    w_in2, w_out2, w_down2 = w_in[0], w_out[0], w_down[0]
    taps_first = lambda a: jnp.transpose(a, (1, 0, 2))
    w_dw3 = taps_first(w_dw)
    w_gateT, w_upT = w_gate[0].T, w_up[0].T
    CI = w_in2.shape[1] * N_CHIPS
    DM = w_out2.shape[0] * N_CHIPS
    F = w_down2.shape[0] * N_CHIPS
    KW, _, dw_cols = w_dw3.shape
    assert dw_cols == LANES
    n_grp, G = w_pool.shape[1], w_pool.shape[-1]
    w_pool3 = w_pool[0]
    g_final2 = g_final.reshape(1, D)

    me = (2 * lax.axis_index("x") + lax.axis_index("y")).astype(jnp.int32).reshape(1)

    w_inT_b, w_dw4, f_out, f_gate, f_up, f_down = _place_and_gather(
        [(w_in2, "rows", (CI, D), BF16, True, True), (w_dw3, "lead", (N_CHIPS, KW, 1, dw_cols), F32, False, False)],
        [(w, "rows", shape, BF16, False, True)
         for w, shape in ((w_out2, (DM, D)), (w_gateT, (F, D)), (w_upT, (F, D)), (w_down2, (F, D)))])
    ici = lambda f: _GatherIci([f], ["rows"], [True])
    d2d = lambda f: _GatherD2d([f], ["rows"])
    gather = _start("gather_start", [ici(f_out), ici(f_gate), ici(f_up), ici(f_down)])
    (z, xn_b), _ = _in_proj(x2, g_mix, w_inT_b, b_in, after=[gather.token])
    (f_out,) = _wait("gather_out_wait", gather, 0, xn_b)
    s_out = _start("share_out_start", [d2d(f_out)], sibling_only=True)
    (y_b, v), _ = _seq_fwd(z, w_dw4, b_dw, ln_g, ln_b, w_pool3, s_pool, after=[s_out.token])
    (w_out_b,) = _wait("share_out_wait", s_out, 0, y_b)
    (f_gate,) = _wait("gather_gate_wait", gather, 1, y_b)
    s_gate = _start("share_gate_start", [d2d(f_gate)], sibling_only=True)
    (h1, hn_b), _ = _out_proj(y_b, x2, w_out_b, g_ffn, after=[s_gate.token])
    (f_up,) = _wait("gather_up_wait", gather, 2, hn_b)
    s_up = _start("share_up_start", [d2d(f_up)], sibling_only=True)
    (wgT_b,) = _wait("share_gate_wait", s_gate, 0, hn_b)
    (wuT_b,) = _wait("share_up_wait", s_up, 0, hn_b)
    (silu_b, uds_b, a_b), _ = _gate_up(hn_b, wgT_b, wuT_b)
    (f_down,) = _wait("gather_down_wait", gather, 3, a_b)
    s_down = _start("share_down_start", [d2d(f_down)], sibling_only=True)
    (wd_b,) = _wait("share_down_wait", s_down, 0, a_b)
    (dh2, dh2_b, loss_part, d_g_final), _ = _down_loss(a_b, wd_b, h1, target, g_final2)

    gw_down = _weight_grad("grad_w_down", a_b, dh2_b)
    x_down = _start("scatter_down_start", [_Scatter([gw_down], ["rows"])])
    (dg_b, du_b), _ = _ffn_bwd_act(dh2_b, wd_b, silu_b, uds_b, after=[x_down.token])
    gw_gateT = _weight_grad("grad_w_gate", dg_b, hn_b)
    gw_upT = _weight_grad("grad_w_up", du_b, hn_b)
    gw_down, p_down = _wait("scatter_down_wait", x_down, 0, gw_upT)
    sum_down = _sum_parts("sum_w_down", gw_down, "rows", [p_down], me)
    (dh1, dh1_b, dy, d_g_ffn), (p_gate, oth_down) = _ffn_bwd_in(
        dg_b, du_b, wgT_b, wuT_b, h1, dh2, g_ffn, w_out_b, comm=[_Scatter([gw_gateT], ["rows"]), _Swap([sum_down])])
    gw_out = _weight_grad("grad_w_out", y_b, dh1_b)
    sum_gate = _sum_parts("sum_w_gate", gw_gateT, "rows", [p_gate], me)
    res = {}
    res["w_down"] = _adamw_on_sparsecore("adamw_w_down", w_down2, m_w_down[0], v_w_down[0], sum_down, oth_down, sum_down)
    (dz_b, d_wdw, d_bdw, d_lng, d_lnb, d_wp, d_sp, d_bin), (p_up, p_out, oth_gate) = _seq_bwd(
        z, dy, v, w_dw4, ln_g, ln_b, w_pool3, s_pool,
        comm=[_Scatter([gw_upT, gw_out], ["rows", "rows"]), _Swap([sum_gate])])
    res["w_gate"] = _adamw_on_sparsecore(
        "adamw_w_gate", w_gateT, m_w_gate[0].T, v_w_gate[0].T, sum_gate, oth_gate, res["w_down"][0])
    vec_grads ={"b_dw": d_bdw, "ln_g": d_lng, "ln_b": d_lnb, "s_pool": d_sp, "g_ffn": d_g_ffn, "g_final": d_g_final, "b_in": d_bin}
    layout = _PackLayout(dw_cols * N_CHIPS // LANES, n_grp, G, [(k, a.shape[1]) for k, a in vec_grads.items()])
    pack = _pack_small(layout, d_wdw, d_wp, vec_grads)
    sum_up = _sum_parts("sum_w_up", gw_upT, "rows", [p_up], me)
    sum_out = _sum_parts("sum_w_out", gw_out, "rows", [p_out], me)
    swap = _start("swap_start", [_Swap([sum_up, sum_out])], sibling_only=True)
    mid = _start("mid_start", [_Scatter([pack], ["all"])], after=[swap.token])
    sum_up, sum_out, oth_up, oth_out = _wait("swap_wait", swap, 0, mid.token)
    gw_inT = _weight_grad("grad_w_in", dz_b, xn_b, after=[mid.token, oth_up])
    late = _start("late_start", [_Scatter([gw_inT], ["rows"])])
    (grad_x, d_g_mix), _ = _in_proj_bwd(dz_b, w_inT_b, x2, dh1, g_mix, after=[late.token])
    pack, p_small = _wait("mid_small_wait", mid, 0, d_g_mix)
    gw_inT, p_in = _wait("late_w_in_wait", late, 0, d_g_mix)
    sum_small = _sum_parts("sum_small", pack, "all", [p_small], me)
    res["w_up"] = _adamw_on_sparsecore("adamw_w_up", w_upT, m_w_up[0].T, v_w_up[0].T, sum_up, oth_up, res["w_gate"][0])
    res["w_out"] = _adamw_on_sparsecore("adamw_w_out", w_out2, m_w_out[0], v_w_out[0], sum_out, oth_out, res["w_gate"][0])
    sum_in = _sum_parts("sum_w_in", gw_inT, "rows", [p_in], me)
    (d_g_mix, loss_row), (oth_in, oth_small) = _allreduce_rows(d_g_mix, loss_part, comm=[_Swap([sum_in, sum_small])])
    loss = loss_row[0, 0]
    res["w_in"], _ = _adamw("adamw_w_in", w_in2, m_w_in[0], v_w_in[0], sum_in, oth_in, g_transposed=True)

    vec_w = {"b_dw": b_dw, "ln_g": ln_g, "ln_b": ln_b, "s_pool": s_pool, "g_ffn": g_ffn, "g_final": g_final2, "b_in": b_in}
    vec_m = {"b_dw": m_b_dw, "ln_g": m_ln_g, "ln_b": m_ln_b, "s_pool": m_s_pool, "g_ffn": m_g_ffn,
             "g_final": m_g_final.reshape(1, D), "b_in": m_b_in}
    vec_v = {"b_dw": v_b_dw, "ln_g": v_ln_g, "ln_b": v_ln_b, "s_pool": v_s_pool, "g_ffn": v_g_ffn,
             "g_final": v_g_final.reshape(1, D), "b_in": v_b_in}
    small = _adamw_small(layout, sum_small, oth_small, w_dw3, taps_first(m_w_dw), taps_first(v_w_dw),
                         w_pool3, m_w_pool[0], v_w_pool[0], vec_w, vec_m, vec_v, (d_g_mix, g_mix, m_g_mix, v_g_mix))
    res["w_dw"] = [taps_first(a) for a in small[0:4]]
    res["w_pool"] = [a[None] for a in small[4:8]]
    for q, k in enumerate(vec_w):
        res[k] = list(small[8 + 4 * q : 12 + 4 * q])
    res["g_mix"] = list(small[-4:])
    res["g_final"] = [a.reshape(D) for a in res["g_final"]]
    for k in ("w_in", "w_out", "w_down"):
        res[k] = [a[None] for a in res[k]]
    for k in ("w_gate", "w_up"):
        res[k] = [a.T[None] for a in res[k]]

    order = ["g_mix", "w_in", "b_in", "w_dw", "b_dw", "ln_g", "ln_b", "w_pool", "s_pool", "w_out", "g_ffn", "w_gate", "w_up", "w_down", "g_final"]
    outs = [loss, grad_x[None]]
    for q in range(4):
        outs += [res[k][q] for k in order]
    return tuple(outs)
```

```python
import jax
import jax.numpy as jnp
from jax import lax
from jax.experimental import pallas as pl
from jax.experimental.pallas import tpu as pltpu
from jax.experimental.pallas import tpu_sc as plsc

F32 = jnp.float32
BF16 = jnp.bfloat16
MESH = pl.DeviceIdType.MESH
ANY = pl.BlockSpec(memory_space=pl.ANY)

RMS_EPS = 1e-6
LN_EPS = 1e-5
POOL_WINDOWS = (2, 4, 8, 16)
ADAM_LR = 0.001
ADAM_B1 = 0.9
ADAM_B2 = 0.999
ADAM_EPS = 1e-08
ADAM_WD = 0.01
ADAM_STEP = 10

LANES = 128
SUBLANES = 8
BF16_ROWS = 16
HALO = 32
CONV_ROWS = 64
HIDDEN_CHUNK = 512
VMEM_LIMIT = 56 * 1024 * 1024
PACK_W = 512
N_CHIPS = 4
N_DEV = 8
SIBLING_BARRIER_ID = 0
SC_CORES = 2
SC_TILES = 32
SC_LANES = 16


def _tile(n, want, mult=8):
    t = min(n, want)
    while n % t or t % mult:
        t -= 1
    return t


def _sigmoid(x):
    return 1.0 / (1.0 + jnp.exp(-x))


def _dot(a, b, dims):
    return lax.dot_general(a, b, (dims, ((), ())), preferred_element_type=F32)


NN = ((1,), (0,))
NT = ((1,), (1,))
TN = ((0,), (0,))


def _rms_bwd(x, g, dy):
    r = lax.rsqrt(jnp.mean(x * x, axis=-1, keepdims=True) + RMS_EPS)
    xh = x * r
    gy = dy * g
    dx = r * (gy - xh * jnp.mean(gy * xh, axis=-1, keepdims=True))
    return dx, dy * xh


def _accumulate(ref, first, val):
    @pl.when(first)
    def _():
        ref[...] = val

    @pl.when(jnp.logical_not(first))
    def _():
        ref[...] += val


def _place():
    return lax.axis_index("x"), lax.axis_index("y"), lax.axis_index("c")


def _other_chips(x, y):
    return [(1 - x, y), (x, 1 - y), (1 - x, 1 - y)]


def _rows(ref, start, n):
    return ref.at[pl.ds(pl.multiple_of(start, BF16_ROWS), n)]


def _window(ref, how, k, c=None):
    if how == "all":
        return ref
    if how == "lead":
        return ref.at[k]
    assert how == "rows"
    n = ref.shape[0] // N_CHIPS
    if c is None:
        return _rows(ref, k * n, n)
    return _rows(ref, k * n + c * (n // 2), n // 2)


def _remote(src, dst, sems, s, device):
    return pltpu.make_async_remote_copy(
        src_ref=src, dst_ref=dst, send_sem=sems.at[s], recv_sem=sems.at[s + 1], device_id=device, device_id_type=MESH)


class _GatherIci:
    aliased = True

    def __init__(self, fulls, hows, splits, which=(0, 1, 2)):
        self.fulls, self.hows, self.splits, self.which = list(fulls), list(hows), list(splits), tuple(which)

    def inputs(self):
        return self.fulls

    def out_shapes(self):
        return [jax.ShapeDtypeStruct(a.shape, a.dtype) for a in self.fulls]

    def n_sems(self):
        return 6 * len(self.fulls)

    def build(self, ins, outs, sems, base):
        x, y, c = _place()
        me = 2 * x + y
        chips = _other_chips(x, y)
        starts, waits = [], []
        for a, (how, sp) in enumerate(zip(self.hows, self.splits)):
            half = c if sp else None
            mine = _window(outs[a], how, me, half)
            for j in self.which:
                px, py = chips[j]
                s = base + 6 * a + 2 * j
                cp = _remote(mine, mine, sems, s, (px, py, c))
                landing = _remote(mine, _window(outs[a], how, 2 * px + py, half), sems, s, (px, py, c))
                starts.append(cp.start)
                waits += [landing.wait_recv, cp.wait_send]
        return starts, waits


class _GatherD2d:
    aliased = True

    def __init__(self, fulls, hows):
        self.fulls, self.hows = list(fulls), list(hows)

    def inputs(self):
        return self.fulls

    def out_shapes(self):
        return [jax.ShapeDtypeStruct(a.shape, a.dtype) for a in self.fulls]

    def n_sems(self):
        return 6 * len(self.fulls)

    def build(self, ins, outs, sems, base):
        x, y, c = _place()
        starts, waits = [], []
        for a, how in enumerate(self.hows):
            for j, (px, py) in enumerate(_other_chips(x, y)):
                s = base + 6 * a + 2 * j
                got = _window(outs[a], how, 2 * px + py, c)
                cp = _remote(got, got, sems, s, (x, y, 1 - c))
                landing = _remote(got, _window(outs[a], how, 2 * px + py, 1 - c), sems, s, (x, y, 1 - c))
                starts.append(cp.start)
                waits += [landing.wait_recv, cp.wait_send]
        return starts, waits


def _part_shape(a, how):
    if how == "all":
        return a.shape
    assert how == "rows"
    return (a.shape[0] // N_CHIPS, a.shape[1])


class _Scatter:
    aliased = False

    def __init__(self, fulls, hows, which=(0, 1, 2)):
        self.fulls, self.hows, self.which = list(fulls), list(hows), tuple(which)

    def inputs(self):
        return self.fulls

    def out_shapes(self):
        return [jax.ShapeDtypeStruct((len(self.which),) + _part_shape(a, h), a.dtype) for a, h in zip(self.fulls, self.hows)]

    def n_sems(self):
        return 6 * len(self.fulls)

    def build(self, ins, outs, sems, base):
        x, y, c = _place()
        chips = _other_chips(x, y)
        starts, waits = [], []
        for a, how in enumerate(self.hows):
            for slot, j in enumerate(self.which):
                px, py = chips[j]
                cp = _remote(_window(ins[a], how, 2 * px + py), outs[a].at[slot], sems, base + 6 * a + 2 * j, (px, py, c))
                starts.append(cp.start)
                waits += [cp.wait_recv, cp.wait_send]
        return starts, waits


class _Swap:
    aliased = False

    def __init__(self, arrays):
        self.arrays = list(arrays)

    def inputs(self):
        return self.arrays

    def out_shapes(self):
        return [jax.ShapeDtypeStruct(a.shape, a.dtype) for a in self.arrays]

    def n_sems(self):
        return 2 * len(self.arrays)

    def build(self, ins, outs, sems, base):
        x, y, c = _place()
        starts, waits = [], []
        for a in range(len(ins)):
            cp = _remote(ins[a], outs[a], sems, base + 2 * a, (x, y, 1 - c))
            starts.append(cp.start)
            waits += [cp.wait_recv, cp.wait_send]
        return starts, waits


def _call(name, body, grid, in_specs, out_specs, out_shape, args, scratch=(), comm=(), after=()):
    comm, after = list(comm), list(after)
    n_in, n_out, n_scr, n_after = len(args), len(out_shape), len(scratch), len(after)
    c_in = [a for op in comm for a in op.inputs()]
    c_out = [s for op in comm for s in op.out_shapes()]
    n_sems = sum(op.n_sems() for op in comm)
    aliases, i_in, i_out = {}, 0, 0
    for op in comm:
        if op.aliased:
            for q in range(len(op.inputs())):
                aliases[n_in + n_after + i_in + q] = n_out + i_out + q
        i_in, i_out = i_in + len(op.inputs()), i_out + len(op.out_shapes())

    def wrapped(*refs):
        ins = refs[:n_in]
        cin = refs[n_in + n_after : n_in + n_after + len(c_in)]
        o0 = n_in + n_after + len(c_in)
        outs = refs[o0 : o0 + n_out]
        cout = refs[o0 + n_out : o0 + n_out + len(c_out)]
        s0 = o0 + n_out + len(c_out)
        scr = refs[s0 : s0 + n_scr]

        def copies():
            sems = refs[s0 + n_scr]
            starts, waits = [], []
            i_in = i_out = base = 0
            for op in comm:
                ni, no = len(op.inputs()), len(op.out_shapes())
                s, w = op.build(cin[i_in : i_in + ni], cout[i_out : i_out + no], sems, base)
                starts += s
                waits += w
                i_in, i_out, base = i_in + ni, i_out + no, base + op.n_sems()
            return starts, waits

        def run_starts():
            for start in copies()[0]:
                start()

        def run_waits():
            for wait in copies()[1]:
                wait()

        if comm and grid:
            first = last = True
            for d, n in enumerate(grid):
                first = jnp.logical_and(first, pl.program_id(d) == 0)
                last = jnp.logical_and(last, pl.program_id(d) == n - 1)
            pl.when(first)(run_starts)
        elif comm:
            run_starts()
        if body is not None:
            body(*ins, *outs, *scr)
        if comm and grid:
            pl.when(last)(run_waits)
        elif comm:
            run_waits()

    res = pl.pallas_call(
        wrapped,
        name=name,
        grid=grid,
        in_specs=list(in_specs) + [ANY] * (n_after + len(c_in)),
        out_specs=list(out_specs) + [ANY] * len(c_out),
        out_shape=list(out_shape) + c_out,
        scratch_shapes=list(scratch) + ([pltpu.SemaphoreType.DMA((n_sems,))] if comm else []),
        input_output_aliases=aliases,
        compiler_params=pltpu.CompilerParams(dimension_semantics=("arbitrary",) * len(grid), vmem_limit_bytes=VMEM_LIMIT),
    )(*args, *after, *c_in)
    return tuple(res[:n_out]), tuple(res[n_out:])


def _place_and_gather(now, later):
    items = list(now) + list(later)
    n, n_now = len(items), len(now)
    buf_shape = lambda it: it[0].shape[::-1] if it[4] else it[0].shape
    split_now = [a for a in range(n_now) if items[a][5]]

    def body(*refs):
        ins, outs = refs[:n], refs[n : 2 * n]
        stage, bufs = refs[2 * n : 3 * n - n_now], refs[3 * n - n_now : 4 * n - n_now]
        sems = refs[4 * n - n_now]
        x, y, c = _place()
        me = 2 * x + y
        chips = _other_chips(x, y)
        loads = [pltpu.make_async_copy(ins[a], stage[a - n_now], sems.at[a]) for a in range(n_now, n)]
        for ld in loads:
            ld.start()
        pending = []

        def place(a, val):
            _, how, _, dtype, transposed, _ = items[a]
            bufs[a][...] = (val.T if transposed else val).astype(dtype)
            cp = pltpu.make_async_copy(bufs[a], _window(outs[a], how, me), sems.at[n + a])
            cp.start()
            pending.append(cp.wait)

        arrivals = []
        for a in range(n_now):
            place(a, ins[a][...])
            how, split = items[a][1], items[a][5]
            half = c if split else None
            src = _rows(bufs[a], c * (bufs[a].shape[0] // 2), bufs[a].shape[0] // 2) if split else bufs[a]
            for j, (px, py) in enumerate(chips):
                s = 2 * n + 6 * a + 2 * j
                cp = _remote(src, _window(outs[a], how, me, half), sems, s, (px, py, c))
                landing = _remote(src, _window(outs[a], how, 2 * px + py, half), sems, s, (px, py, c))
                cp.start()
                arrivals.append(landing.wait_recv)
                pending.append(cp.wait_send)
        for a in range(n_now, n):
            loads[a - n_now].wait()
            place(a, stage[a - n_now][...])
        for wait in arrivals:
            wait()
        d2d = _GatherD2d([None] * len(split_now), [items[a][1] for a in split_now])
        starts, waits = d2d.build(None, [outs[a] for a in split_now], sems, 2 * n + 6 * n_now)
        for start in starts:
            start()
        for wait in waits + pending:
            wait()

    vm = pl.BlockSpec(memory_space=pltpu.VMEM)
    return pl.pallas_call(
        body,
        name="place_and_gather",
        in_specs=[vm] * n_now + [ANY] * (n - n_now),
        out_specs=[ANY] * n,
        out_shape=[jax.ShapeDtypeStruct(it[2], it[3]) for it in items],
        scratch_shapes=[pltpu.VMEM(it[0].shape, it[0].dtype) for it in later]
        + [pltpu.VMEM(buf_shape(it), it[3]) for it in items]
        + [pltpu.SemaphoreType.DMA((2 * n + 6 * n_now + 6 * len(split_now),))],
        compiler_params=pltpu.CompilerParams(vmem_limit_bytes=VMEM_LIMIT),
    )(*[it[0] for it in items])


_HBM = pl.BlockSpec(memory_space=pltpu.HBM)
_SEM = pl.BlockSpec(memory_space=pltpu.SEMAPHORE)
_DATAFLOW = pltpu.SideEffectType.DATAFLOW_SIDE_EFFECTING


class _Pending:
    def __init__(self, ops, bases, sems, arrays, token):
        self.ops, self.bases, self.sems, self.arrays, self.token = ops, bases, sems, arrays, token


def _op_refs(op, refs):
    n_src = len(op.inputs())
    return refs[:n_src], (refs[:n_src] if op.aliased else refs[n_src:])


def _start(name, ops, sibling_only=False):
    per_op = [list(op.inputs()) + ([] if op.aliased else [lax.empty(sd.shape, sd.dtype) for sd in op.out_shapes()])
              for op in ops]
    arrays = [a for group in per_op for a in group]
    bases = [sum(op.n_sems() for op in ops[:k]) for k in range(len(ops))]
    n = len(arrays)

    def body(*refs):
        sems, token = refs[n], refs[-1]
        if sibling_only:
            x, y, c = _place()
            barrier = pltpu.get_barrier_semaphore()
            pl.semaphore_signal(barrier, inc=1, device_id=(x, y, 1 - c), device_id_type=MESH)
            pl.semaphore_wait(barrier, 1)
        at = 0
        for op, group, base in zip(ops, per_op, bases):
            starts, _ = op.build(*_op_refs(op, refs[at : at + len(group)]), sems, base)
            for start in starts:
                start()
            at += len(group)
        token[...] = jnp.zeros_like(token)

    res = pl.pallas_call(
        body,
        name=name,
        out_shape=(pltpu.SemaphoreType.DMA((sum(op.n_sems() for op in ops),)),)
        + tuple(pltpu.HBM(a.shape, a.dtype) for a in arrays) + (jax.ShapeDtypeStruct((SUBLANES, LANES), F32),),
        in_specs=(_HBM,) * n,
        out_specs=(_SEM,) + (_HBM,) * n + (pl.BlockSpec(memory_space=pltpu.VMEM),),
        input_output_aliases={i: 1 + i for i in range(n)},
        compiler_params=pltpu.CompilerParams(
            has_side_effects=_DATAFLOW, collective_id=SIBLING_BARRIER_ID if sibling_only else None),
    )(*[pltpu.with_memory_space_constraint(a, pltpu.HBM) for a in arrays])
    thru, at, groups = list(res[1 : 1 + n]), 0, []
    for group in per_op:
        groups.append(thru[at : at + len(group)])
        at += len(group)
    return _Pending(list(ops), bases, res[0], groups, res[-1])


def _wait(name, pending, k, after):
    op, arrays = pending.ops[k], pending.arrays[k]
    n = len(arrays)

    def body(*refs):
        _, waits = op.build(*_op_refs(op, refs[:n]), refs[n], pending.bases[k])
        for wait in waits:
            wait()

    return pl.pallas_call(
        body,
        name=name,
        out_shape=tuple(pltpu.HBM(a.shape, a.dtype) for a in arrays),
        in_specs=(_HBM,) * n + (_SEM, ANY),
        out_specs=(_HBM,) * n,
        input_output_aliases={i: i for i in range(n)},
        compiler_params=pltpu.CompilerParams(has_side_effects=_DATAFLOW),
    )(*arrays, pending.sems, after)


def _in_proj(x, g_mix, w_inT_b, b_in, after=()):
    T, D = x.shape
    CI = w_inT_b.shape[0]
    tm = _tile(T, 512)

    def body(x_ref, g_ref, w_ref, b_ref, z_ref, xn_ref):
        xv = x_ref[...]
        r = lax.rsqrt(jnp.mean(xv * xv, axis=-1, keepdims=True) + RMS_EPS)
        xn = (xv * r * g_ref[...]).astype(BF16)
        xn_ref[...] = xn
        z_ref[...] = _dot(xn, w_ref[...], NT) + b_ref[...]

    return _call(
        "in_proj",
        body,
        (T // tm,),
        [
            pl.BlockSpec((tm, D), lambda i: (i, 0)),
            pl.BlockSpec((1, D), lambda i: (0, 0)),
            pl.BlockSpec((CI, D), lambda i: (0, 0)),
            pl.BlockSpec((1, CI), lambda i: (0, 0)),
        ],
        [pl.BlockSpec((tm, CI), lambda i: (i, 0)), pl.BlockSpec((tm, D), lambda i: (i, 0))],
        [jax.ShapeDtypeStruct((T, CI), F32), jax.ShapeDtypeStruct((T, D), BF16)],
        (x, g_mix, w_inT_b, b_in),
        after=after,
    )


def _fill_shifted(scr):
    n = scr.shape[1] - SUBLANES
    for s in range(1, SUBLANES):
        scr[s, 0:n, :] = scr[0, s : s + n, :]


def _shifted_rows(scr, off, n, cs):
    s = off % SUBLANES
    return scr[s, off - s : off - s + n, cs]


def _pool_mean_minus_token(p_scr, cs, w, cnt, tt):
    tok = p_scr[HALO : HALO + tt, cs]
    s = tok
    for d in range(1, w):
        s = s + p_scr[HALO - d : HALO - d + tt, cs]
    return s / cnt - tok


def _seq_fwd(z, w_dw4, b_dw, ln_g, ln_b, w_pool, s_pool, after=()):
    T, CI = z.shape
    CC = ln_g.shape[1]
    n_grp, G = w_pool.shape[0], w_pool.shape[-1]
    KW = w_dw4.shape[1]
    D = CC + n_grp * G
    tt = _tile(T, 512, HALO)
    per = tt // HALO

    def body(zc_ref, zp_ref, wdw_ref, bdw_ref, lng_ref, lnb_ref, wp_ref, sp_ref, y_ref, v_ref, u_scr, p_scr):
        i = pl.program_id(0)
        first = i == 0
        u_prev = zp_ref[:, 0:CC] * _sigmoid(zp_ref[:, CC : 2 * CC])
        u_scr[0, 0:HALO, :] = jnp.where(first, 0.0, u_prev)
        p_scr[0:HALO, :] = jnp.where(first, 0.0, zp_ref[:, 2 * CC :])
        u_scr[0, HALO:, :] = zc_ref[:, 0:CC] * _sigmoid(zc_ref[:, CC : 2 * CC])
        p_scr[HALO:, :] = zc_ref[:, 2 * CC :]
        _fill_shifted(u_scr)

        for j in range(CC // LANES):
            cs = slice(LANES * j, LANES * (j + 1))
            for rb in range(tt // CONV_ROWS):
                acc = jnp.zeros((CONV_ROWS, LANES), F32)
                for k in range(KW):
                    off = HALO - (KW - 1) + k + rb * CONV_ROWS
                    acc = acc + _shifted_rows(u_scr, off, CONV_ROWS, cs) * wdw_ref[j, k]
                v_ref[rb * CONV_ROWS : (rb + 1) * CONV_ROWS, cs] = acc + bdw_ref[:, cs]

        v = v_ref[...]
        mu = jnp.mean(v, axis=-1, keepdims=True)
        d = v - mu
        var = jnp.mean(d * d, axis=-1, keepdims=True)
        ln = d * lax.rsqrt(var + LN_EPS) * lng_ref[...] + lnb_ref[...]
        y_ref[:, 0:CC] = (ln * _sigmoid(ln)).astype(BF16)

        tpos = i * tt + lax.broadcasted_iota(jnp.int32, (tt, 1), 0)
        for gi, w in enumerate(POOL_WINDOWS):
            cs = slice(G * gi, G * (gi + 1))
            cnt = jnp.minimum(tpos + 1, w).astype(F32)
            yi = _pool_mean_minus_token(p_scr, cs, w, cnt, tt)
            q = _dot(yi.astype(BF16), wp_ref[gi].astype(BF16), NN)
            y_ref[:, CC + G * gi : CC + G * (gi + 1)] = (q * sp_ref[:, cs]).astype(BF16)

    const2 = lambda i: (0, 0)
    return _call(
        "seq_fwd",
        body,
        (T // tt,),
        [
            pl.BlockSpec((tt, CI), lambda i: (i, 0)),
            pl.BlockSpec((HALO, CI), lambda i: (jnp.maximum(i * per - 1, 0), 0)),
            pl.BlockSpec(w_dw4.shape, lambda i: (0,) * w_dw4.ndim),
            pl.BlockSpec((1, CC), const2),
            pl.BlockSpec((1, CC), const2),
            pl.BlockSpec((1, CC), const2),
            pl.BlockSpec(w_pool.shape, lambda i: (0, 0, 0)),
            pl.BlockSpec((1, n_grp * G), const2),
        ],
        [pl.BlockSpec((tt, D), lambda i: (i, 0)), pl.BlockSpec((tt, CC), lambda i: (i, 0))],
        [jax.ShapeDtypeStruct((T, D), BF16), jax.ShapeDtypeStruct((T, CC), F32)],
        (z, z, w_dw4, b_dw, ln_g, ln_b, w_pool, s_pool),
        scratch=[pltpu.VMEM((SUBLANES, HALO + tt, CC), F32), pltpu.VMEM((HALO + tt, n_grp * G), F32)],
        after=after,
    )


def _out_proj(y_b, x, w_out_b, g_ffn, after=()):
    T, D = x.shape
    tm = _tile(T, 512)

    def body(y_ref, x_ref, w_ref, g_ref, h1_ref, hn_ref):
        h1 = x_ref[...] + _dot(y_ref[...], w_ref[...], NN)
        h1_ref[...] = h1
        r = lax.rsqrt(jnp.mean(h1 * h1, axis=-1, keepdims=True) + RMS_EPS)
        hn_ref[...] = (h1 * r * g_ref[...]).astype(BF16)

    row = lambda i: (i, 0)
    return _call(
        "out_proj",
        body,
        (T // tm,),
        [
            pl.BlockSpec((tm, y_b.shape[1]), row),
            pl.BlockSpec((tm, D), row),
            pl.BlockSpec(w_out_b.shape, lambda i: (0, 0)),
            pl.BlockSpec((1, D), lambda i: (0, 0)),
        ],
        [pl.BlockSpec((tm, D), row), pl.BlockSpec((tm, D), row)],
        [jax.ShapeDtypeStruct((T, D), F32), jax.ShapeDtypeStruct((T, D), BF16)],
        (y_b, x, w_out_b, g_ffn),
        after=after,
    )


def _hidden_tile(F):
    return _tile(F, 1408, LANES)


def _gate_up(hn_b, wgT_b, wuT_b):
    T, D = hn_b.shape
    F = wgT_b.shape[0]
    tm, tf = _tile(T, 1024), _hidden_tile(F)

    def body(hn_ref, wg_ref, wu_ref, silu_ref, uds_ref, a_ref):
        hn = hn_ref[...]
        for c0 in range(0, tf, HIDDEN_CHUNK):
            cs = slice(c0, min(c0 + HIDDEN_CHUNK, tf))
            gv = _dot(hn, wg_ref[cs, :], NT)
            uv = _dot(hn, wu_ref[cs, :], NT)
            sg = _sigmoid(gv)
            silu = gv * sg
            silu_ref[:, cs] = silu.astype(BF16)
            uds_ref[:, cs] = (uv * (sg * (1.0 + gv * (1.0 - sg)))).astype(BF16)
            a_ref[:, cs] = (silu * uv).astype(BF16)

    wspec = pl.BlockSpec((tf, D), lambda j, i: (j, 0))
    ospec = pl.BlockSpec((tm, tf), lambda j, i: (i, j))
    return _call(
        "gate_up",
        body,
        (F // tf, T // tm),
        [pl.BlockSpec((tm, D), lambda j, i: (i, 0)), wspec, wspec],
        [ospec, ospec, ospec],
        [jax.ShapeDtypeStruct((T, F), BF16)] * 3,
        (hn_b, wgT_b, wuT_b),
    )


def _down_loss(a_b, wd_b, h1, target, g_final):
    T, D = h1.shape
    F = a_b.shape[1]
    tm = _tile(T, 512)
    nt = T // tm

    def body(a_ref, w_ref, h1_ref, t_ref, g_ref, dh2_ref, dh2b_ref, loss_ref, dg_ref):
        i = pl.program_id(0)
        h2 = h1_ref[...] + _dot(a_ref[...], w_ref[...], NN)
        r = lax.rsqrt(jnp.mean(h2 * h2, axis=-1, keepdims=True) + RMS_EPS)
        g = g_ref[...]
        diff = h2 * r * g - t_ref[...]
        _accumulate(loss_ref, i == 0, jnp.full(loss_ref.shape, jnp.sum(diff * diff) * (0.5 / D), F32))
        dh2, dg_rows = _rms_bwd(h2, g, diff * (1.0 / D))
        dh2_ref[...] = dh2
        dh2b_ref[...] = dh2.astype(BF16)
        _accumulate(dg_ref, i == 0, jnp.sum(dg_rows, axis=0, keepdims=True))

    row = lambda i: (i, 0)
    return _call(
        "down_loss",
        body,
        (nt,),
        [
            pl.BlockSpec((tm, F), row),
            pl.BlockSpec((F, D), lambda i: (0, 0), pipeline_mode=pl.Buffered(1)),
            pl.BlockSpec((tm, D), row),
            pl.BlockSpec((tm, D), row),
            pl.BlockSpec((1, D), lambda i: (0, 0)),
        ],
        [
            pl.BlockSpec((tm, D), row),
            pl.BlockSpec((tm, D), row),
            pl.BlockSpec((1, LANES), lambda i: (0, 0)),
            pl.BlockSpec((1, D), lambda i: (0, 0)),
        ],
        [
            jax.ShapeDtypeStruct((T, D), F32),
            jax.ShapeDtypeStruct((T, D), BF16),
            jax.ShapeDtypeStruct((1, LANES), F32),
            jax.ShapeDtypeStruct((1, D), F32),
        ],
        (a_b, wd_b, h1, target, g_final),
    )


def _ffn_bwd_act(dh2_b, wd_b, silu_b, uds_b, after=()):
    T, D = dh2_b.shape
    F = wd_b.shape[0]
    tm, tf = _tile(T, 1024), _hidden_tile(F)

    def body(d_ref, w_ref, silu_ref, uds_ref, dg_ref, du_ref):
        d = d_ref[...]
        for c0 in range(0, tf, HIDDEN_CHUNK):
            cs = slice(c0, min(c0 + HIDDEN_CHUNK, tf))
            da = _dot(d, w_ref[cs, :], NT)
            dg_ref[:, cs] = (da * uds_ref[:, cs].astype(F32)).astype(BF16)
            du_ref[:, cs] = (da * silu_ref[:, cs].astype(F32)).astype(BF16)

    aspec = pl.BlockSpec((tm, tf), lambda j, i: (i, j))
    return _call(
        "ffn_bwd_act",
        body,
        (F // tf, T // tm),
        [pl.BlockSpec((tm, D), lambda j, i: (i, 0)), pl.BlockSpec((tf, D), lambda j, i: (j, 0)), aspec, aspec],
        [aspec, aspec],
        [jax.ShapeDtypeStruct((T, F), BF16)] * 2,
        (dh2_b, wd_b, silu_b, uds_b),
        after=after,
    )


def _ffn_bwd_in(dg_b, du_b, wgT_b, wuT_b, h1, dh2, g_ffn, w_out_b, comm=()):
    T, D = h1.shape
    F = wgT_b.shape[0]
    DM = w_out_b.shape[0]
    tm = _tile(T, 512)

    def body(dg_ref, du_ref, wg_ref, wu_ref, h1_ref, dh2_ref, g_ref, wo_ref, dh1_ref, dh1b_ref, dy_ref, dgf_ref):
        i = pl.program_id(0)
        dhn = _dot(dg_ref[...], wg_ref[...], NN) + _dot(du_ref[...], wu_ref[...], NN)
        dx, dg_rows = _rms_bwd(h1_ref[...], g_ref[...], dhn)
        dh1 = dh2_ref[...] + dx
        dh1b = dh1.astype(BF16)
        dh1_ref[...] = dh1
        dh1b_ref[...] = dh1b
        dy_ref[...] = _dot(dh1b, wo_ref[...], NT)
        _accumulate(dgf_ref, i == 0, jnp.sum(dg_rows, axis=0, keepdims=True))

    row = lambda i: (i, 0)
    const = lambda i: (0, 0)
    return _call(
        "ffn_bwd_in",
        body,
        (T // tm,),
        [
            pl.BlockSpec((tm, F), row),
            pl.BlockSpec((tm, F), row),
            pl.BlockSpec((F, D), const, pipeline_mode=pl.Buffered(1)),
            pl.BlockSpec((F, D), const, pipeline_mode=pl.Buffered(1)),
            pl.BlockSpec((tm, D), row),
            pl.BlockSpec((tm, D), row),
            pl.BlockSpec((1, D), const),
            pl.BlockSpec((DM, D), const, pipeline_mode=pl.Buffered(1)),
        ],
        [pl.BlockSpec((tm, D), row), pl.BlockSpec((tm, D), row), pl.BlockSpec((tm, DM), row), pl.BlockSpec((1, D), const)],
        [
            jax.ShapeDtypeStruct((T, D), F32),
            jax.ShapeDtypeStruct((T, D), BF16),
            jax.ShapeDtypeStruct((T, DM), F32),
            jax.ShapeDtypeStruct((1, D), F32),
        ],
        (dg_b, du_b, wgT_b, wuT_b, h1, dh2, g_ffn, w_out_b),
        comm=comm,
    )


def _seq_bwd(z, dy, v, w_dw4, ln_g, ln_b, w_pool, s_pool, comm=()):
    T, CI = z.shape
    CC = ln_g.shape[1]
    n_grp, G = w_pool.shape[0], w_pool.shape[-1]
    CP = n_grp * G
    KW = w_dw4.shape[1]
    n_cc = CC // LANES
    D = CC + CP
    tt = _tile(T, 512, HALO)
    per = tt // HALO
    n_tiles = T // tt
    last_halo = T // HALO - 1

    def body(zc_ref, zp_ref, dyc_ref, dyn_ref, vc_ref, vn_ref, wdw_ref, lng_ref, lnb_ref, wp_ref, sp_ref,
             dz_ref, dwdw_ref, dbdw_ref, dlng_ref, dlnb_ref, dwp_ref, dsp_ref, dbin_ref,
             dv_scr, u_scr, p_scr, g_scr, dw_scr):
        i = pl.program_id(0)
        first = i == 0
        last = i == n_tiles - 1
        lng, lnb = lng_ref[...], lnb_ref[...]

        def conv_pre(vv, dyc):
            mu = jnp.mean(vv, axis=-1, keepdims=True)
            d = vv - mu
            rs = lax.rsqrt(jnp.mean(d * d, axis=-1, keepdims=True) + LN_EPS)
            xh = d * rs
            ln = xh * lng + lnb
            sg = _sigmoid(ln)
            dln = dyc * (sg * (1.0 + ln * (1.0 - sg)))
            dxh = dln * lng
            dv = rs * (dxh - jnp.mean(dxh, axis=-1, keepdims=True) - xh * jnp.mean(dxh * xh, axis=-1, keepdims=True))
            return dv, dln, xh

        dv_c, dln_c, xh_c = conv_pre(vc_ref[...], dyc_ref[:, 0:CC])
        dv_scr[0, 0:tt, :] = dv_c
        dv_n, _, _ = conv_pre(vn_ref[...], dyn_ref[:, 0:CC])
        dv_scr[0, tt:, :] = jnp.where(last, 0.0, dv_n)
        _fill_shifted(dv_scr)
        _accumulate(dlng_ref, first, jnp.sum(dln_c * xh_c, axis=0, keepdims=True))
        _accumulate(dlnb_ref, first, jnp.sum(dln_c, axis=0, keepdims=True))
        _accumulate(dbdw_ref, first, jnp.sum(dv_c, axis=0, keepdims=True))

        u_scr[...] = zc_ref[:, 0:CC] * _sigmoid(zc_ref[:, CC : 2 * CC])

        @pl.when(first)
        def _():
            dw_scr[...] = jnp.zeros_like(dw_scr)

        for j in range(n_cc):
            cs = slice(LANES * j, LANES * (j + 1))
            gs = slice(CC + LANES * j, CC + LANES * (j + 1))
            dbin_a = jnp.zeros((1, LANES), F32)
            dbin_g = jnp.zeros((1, LANES), F32)
            for rb in range(tt // CONV_ROWS):
                rows = slice(rb * CONV_ROWS, (rb + 1) * CONV_ROWS)
                u_blk = u_scr[rows, cs]
                du = jnp.zeros((CONV_ROWS, LANES), F32)
                for k in range(KW):
                    off = rb * CONV_ROWS + (KW - 1) - k
                    d = _shifted_rows(dv_scr, off, CONV_ROWS, cs)
                    du = du + d * wdw_ref[j, k]
                    dw_scr[j * HALO + k] += jnp.sum((u_blk * d).reshape(CONV_ROWS // 8, 8, LANES), axis=0)
                a = zc_ref[rows, cs]
                sg = _sigmoid(zc_ref[rows, gs])
                da = du * sg
                dgate = du * a * sg * (1.0 - sg)
                dz_ref[rows, cs] = da.astype(BF16)
                dz_ref[rows, gs] = dgate.astype(BF16)
                dbin_a = dbin_a + jnp.sum(da, axis=0, keepdims=True)
                dbin_g = dbin_g + jnp.sum(dgate, axis=0, keepdims=True)
            _accumulate(dbin_ref.at[:, cs], first, dbin_a)
            _accumulate(dbin_ref.at[:, gs], first, dbin_g)

        @pl.when(last)
        def _():
            dwdw_ref[...] = jnp.sum(dw_scr[...], axis=1).reshape(dwdw_ref.shape)

        p_scr[0:HALO, :] = jnp.where(first, 0.0, zp_ref[:, 2 * CC :])
        p_scr[HALO:, :] = zc_ref[:, 2 * CC :]
        tpos = i * tt + lax.broadcasted_iota(jnp.int32, (tt, 1), 0)
        for gi, w in enumerate(POOL_WINDOWS):
            cs = slice(G * gi, G * (gi + 1))
            ys = slice(CC + G * gi, CC + G * (gi + 1))
            ps = slice(2 * CC + G * gi, 2 * CC + G * (gi + 1))
            cnt = jnp.minimum(tpos + 1, w).astype(F32)
            yib = _pool_mean_minus_token(p_scr, cs, w, cnt, tt).astype(BF16)
            wp = wp_ref[gi].astype(BF16)
            sp = sp_ref[:, cs]
            dyp = dyc_ref[:, ys]
            q = _dot(yib, wp, NN)
            _accumulate(dsp_ref.at[:, cs], first, jnp.sum(dyp * q, axis=0, keepdims=True))
            dq_c = (dyp * sp).astype(BF16)
            dq_n = (jnp.where(last, 0.0, dyn_ref[:, ys]) * sp).astype(BF16)
            _accumulate(dwp_ref.at[gi], first, _dot(yib, dq_c, TN))
            dyi_c = _dot(dq_c, wp, NT)
            g_scr[0:tt, cs] = dyi_c / cnt
            g_scr[tt:, cs] = _dot(dq_n, wp, NT) * (1.0 / w)
            dp = -dyi_c
            for d in range(w):
                dp = dp + g_scr[d : d + tt, cs]
            dz_ref[:, ps] = dp.astype(BF16)
            _accumulate(dbin_ref.at[:, ps], first, jnp.sum(dp, axis=0, keepdims=True))

    cur = lambda i: (i, 0)
    prev = lambda i: (jnp.maximum(i * per - 1, 0), 0)
    nxt = lambda i: (jnp.minimum((i + 1) * per, last_halo), 0)
    c2 = lambda i: (0, 0)
    c3 = lambda i: (0, 0, 0)
    return _call(
        "seq_bwd",
        body,
        (n_tiles,),
        [
            pl.BlockSpec((tt, CI), cur),
            pl.BlockSpec((HALO, CI), prev),
            pl.BlockSpec((tt, D), cur),
            pl.BlockSpec((HALO, D), nxt),
            pl.BlockSpec((tt, CC), cur),
            pl.BlockSpec((HALO, CC), nxt),
            pl.BlockSpec(w_dw4.shape, lambda i: (0,) * w_dw4.ndim),
            pl.BlockSpec((1, CC), c2),
            pl.BlockSpec((1, CC), c2),
            pl.BlockSpec(w_pool.shape, c3),
            pl.BlockSpec((1, CP), c2),
        ],
        [
            pl.BlockSpec((tt, CI), cur),
            pl.BlockSpec((n_cc, HALO, LANES), c3),
            pl.BlockSpec((1, CC), c2),
            pl.BlockSpec((1, CC), c2),
            pl.BlockSpec((1, CC), c2),
            pl.BlockSpec((n_grp, G, G), c3),
            pl.BlockSpec((1, CP), c2),
            pl.BlockSpec((1, CI), c2),
        ],
        [
            jax.ShapeDtypeStruct((T, CI), BF16),
            jax.ShapeDtypeStruct((n_cc, HALO, LANES), F32),
            jax.ShapeDtypeStruct((1, CC), F32),
            jax.ShapeDtypeStruct((1, CC), F32),
            jax.ShapeDtypeStruct((1, CC), F32),
            jax.ShapeDtypeStruct((n_grp, G, G), F32),
            jax.ShapeDtypeStruct((1, CP), F32),
            jax.ShapeDtypeStruct((1, CI), F32),
        ],
        (z, z, dy, dy, v, v, w_dw4, ln_g, ln_b, w_pool, s_pool),
        scratch=[
            pltpu.VMEM((SUBLANES, tt + HALO, CC), F32),
            pltpu.VMEM((tt, CC), F32),
            pltpu.VMEM((HALO + tt, CP), F32),
            pltpu.VMEM((tt + HALO, CP), F32),
            pltpu.VMEM((n_cc * HALO, 8, LANES), F32),
        ],
        comm=comm,
    )


def _in_proj_bwd(dz_b, w_inT_b, x, dh1, g_mix, after=()):
    T, D = x.shape
    CI = w_inT_b.shape[0]
    tm = _tile(T, 512)

    def body(dz_ref, w_ref, x_ref, dh1_ref, g_ref, dx_ref, dg_ref):
        i = pl.program_id(0)
        dxn = _dot(dz_ref[...], w_ref[...], NN)
        dx, dg_rows = _rms_bwd(x_ref[...], g_ref[...], dxn)
        dx_ref[...] = dh1_ref[...] + dx
        _accumulate(dg_ref, i == 0, jnp.sum(dg_rows, axis=0, keepdims=True))

    row = lambda i: (i, 0)
    const = lambda i: (0, 0)
    return _call(
        "in_proj_bwd",
        body,
        (T // tm,),
        [
            pl.BlockSpec((tm, CI), row),
            pl.BlockSpec((CI, D), const),
            pl.BlockSpec((tm, D), row),
            pl.BlockSpec((tm, D), row),
            pl.BlockSpec((1, D), const),
        ],
        [pl.BlockSpec((tm, D), row), pl.BlockSpec((1, D), const)],
        [jax.ShapeDtypeStruct((T, D), F32), jax.ShapeDtypeStruct((1, D), F32)],
        (dz_b, w_inT_b, x, dh1, g_mix),
        after=after,
    )


def _weight_grad(name, a_b, b_b, after=()):
    T, N1 = a_b.shape
    N2 = b_b.shape[1]
    t1 = _tile(N1, 1408, LANES)
    tk = _tile(T, 2048)
    nk = T // tk

    def body(a_ref, b_ref, o_ref, acc):
        k = pl.program_id(1)
        _accumulate(acc, k == 0, _dot(a_ref[...], b_ref[...], TN))

        @pl.when(k == nk - 1)
        def _():
            o_ref[...] = acc[...].astype(BF16)

    (out,), _ = _call(
        name,
        body,
        (N1 // t1, nk),
        [pl.BlockSpec((tk, t1), lambda n, k: (k, n)), pl.BlockSpec((tk, N2), lambda n, k: (k, 0))],
        [pl.BlockSpec((t1, N2), lambda n, k: (n, 0))],
        [jax.ShapeDtypeStruct((N1, N2), BF16)],
        (a_b, b_b),
        scratch=[pltpu.VMEM((t1, N2), F32)],
        after=after,
    )
    return out


def _sum_parts(name, full, how, parts, me):
    _, R, C = parts[0].shape
    tr = _tile(R, 512)
    nb = R // tr
    where = [(q, r) for q, p in enumerate(parts) for r in range(p.shape[0])]
    assert len(where) == 3

    def body(me_ref, own_ref, *refs):
        o_ref = refs[-1]
        f = lambda j: refs[where[j][0]][where[j][1]].astype(F32)
        o_ref[...] = (own_ref[...].astype(F32) + f(0)) + (f(1) + f(2))

    own_map = {"rows": lambda i, me_ref: (me_ref[0] * nb + i, 0), "all": lambda i, me_ref: (i, 0)}[how]
    return pl.pallas_call(
        body,
        name=name,
        grid_spec=pltpu.PrefetchScalarGridSpec(
            num_scalar_prefetch=1,
            grid=(nb,),
            in_specs=[pl.BlockSpec((tr, C), own_map)]
            + [pl.BlockSpec((p.shape[0], tr, C), lambda i, me_ref: (0, i, 0)) for p in parts],
            out_specs=pl.BlockSpec((tr, C), lambda i, me_ref: (i, 0)),
        ),
        out_shape=jax.ShapeDtypeStruct((R, C), F32),
        compiler_params=pltpu.CompilerParams(dimension_semantics=("arbitrary",), vmem_limit_bytes=VMEM_LIMIT),
    )(me, full, *parts)


_M_CORR = 1.0 - ADAM_B1**ADAM_STEP
_V_CORR = 1.0 - ADAM_B2**ADAM_STEP


def _adamw_math(w, g, m, v):
    m = ADAM_B1 * m + (1.0 - ADAM_B1) * g
    v = ADAM_B2 * v + (1.0 - ADAM_B2) * (g * g)
    delta = -ADAM_LR * ((m / _M_CORR) / (jnp.sqrt(v / _V_CORR) + ADAM_EPS) + ADAM_WD * w)
    return delta, m, v


def _adamw(name, w, m, v, g_here, g_there, g_transposed=False):
    R, C = w.shape
    tr = _tile(R, 256, LANES if g_transposed else 8)

    def body(w_ref, m_ref, v_ref, ga_ref, gb_ref, g_ref, d_ref, nm_ref, nv_ref):
        g = ga_ref[...] + gb_ref[...]
        if g_transposed:
            g = g.T
        g_ref[...] = g
        d_ref[...], nm_ref[...], nv_ref[...] = _adamw_math(w_ref[...], g, m_ref[...], v_ref[...])

    spec = pl.BlockSpec((tr, C), lambda i: (i, 0))
    gspec = pl.BlockSpec((C, tr), lambda i: (0, i)) if g_transposed else spec
    return _call(name, body, (R // tr,), [spec] * 3 + [gspec] * 2, [spec] * 4, [jax.ShapeDtypeStruct((R, C), F32)] * 4,
                 (w, m, v, g_here, g_there))


def _adamw_on_sparsecore(name, w, m, v, g_here, g_there, after):
    R, C = w.shape
    n_groups = R // SUBLANES
    n_turns = -(-n_groups // SC_TILES)
    n_in, n_out = 5, 4

    def body(w_hbm, m_hbm, v_hbm, ga_hbm, gb_hbm, after_hbm, g_out, d_out, nm_out, nv_out, bufs, sems):
        tile = lax.axis_index("subcore") * SC_CORES + lax.axis_index("sparsecore")
        srcs = (w_hbm, m_hbm, v_hbm, ga_hbm, gb_hbm)
        dsts = (d_out, nm_out, nv_out, g_out)

        def rows(turn):
            return pl.ds((tile + turn * SC_TILES) * SUBLANES, SUBLANES)

        def loads(turn):
            slot = turn % 2
            return [pltpu.make_async_copy(srcs[q].at[rows(turn), :], bufs.at[slot, q], sems.at[slot, q]) for q in range(n_in)]

        def stores(turn):
            slot = turn % 2
            return [pltpu.make_async_copy(bufs.at[slot, q], dsts[q].at[rows(turn), :], sems.at[slot, n_in + q])
                    for q in range(n_out)]

        def when_mine(turn, fn):
            pl.when(tile + turn * SC_TILES < n_groups)(fn)

        def compute(slot):
            wb, mb, vb, gab, gbb = (bufs.at[slot, q] for q in range(n_in))

            @pl.loop(0, SUBLANES)
            def _(r):
                @pl.loop(0, C, step=SC_LANES)
                def _(i):
                    at = (r, pl.ds(i, SC_LANES))
                    g = gab[at] + gbb[at]
                    delta, new_m, new_v = _adamw_math(wb[at], g, mb[at], vb[at])
                    gab[at], wb[at], mb[at], vb[at] = g, delta, new_m, new_v

        def start_loads(turn):
            def fn():
                for cp in loads(turn):
                    cp.start()

            when_mine(turn, fn)

        start_loads(0)
        for turn in range(n_turns):
            def step(turn=turn):
                for cp in loads(turn):
                    cp.wait()
                if turn >= 1:
                    for cp in stores(turn - 1):
                        cp.wait()
                if turn + 1 < n_turns:
                    start_loads(turn + 1)
                compute(turn % 2)
                for cp in stores(turn):
                    cp.start()

            when_mine(turn, step)
        for turn in range(n_turns):
            def drain(turn=turn):
                for cp in stores(turn):
                    cp.wait()

            last_mine = jnp.logical_and(tile + turn * SC_TILES < n_groups, tile + (turn + 1) * SC_TILES >= n_groups)
            pl.when(last_mine)(drain)

    return pl.kernel(
        body,
        name=name,
        out_type=[jax.ShapeDtypeStruct((R, C), F32)] * 4,
        mesh=plsc.VectorSubcoreMesh(core_axis_name="sparsecore", subcore_axis_name="subcore"),
        scratch_types=[pltpu.VMEM((2, n_in, SUBLANES, C), F32), pltpu.SemaphoreType.DMA((2, n_in + n_out))],
        compiler_params=pltpu.CompilerParams(use_tc_tiling_on_sc=True),
    )(w, m, v, g_here, g_there, after)


class _PackLayout:
    def __init__(self, n_cc, n_grp, G, widths):
        self.dw_rows = (0, HALO)
        self.wp_rows = (HALO, HALO + G)
        self.n_cc, self.n_grp, self.G = n_cc, n_grp, G
        self.vec = {}
        r = HALO + G
        for name, width in widths:
            self.vec[name] = (r, width)
            r += width // PACK_W
        self.rows = -(-r // 8) * 8


def _pack_small(layout, dwdw, dwp, vecs):
    names = list(vecs)

    def body(*refs):
        dw_ref, wp_ref = refs[0], refs[1]
        vec_refs = refs[2 : 2 + len(names)]
        o_ref = refs[-1]
        o_ref[...] = jnp.zeros_like(o_ref)
        for j in range(layout.n_cc):
            o_ref[layout.dw_rows[0] : layout.dw_rows[1], j * LANES : (j + 1) * LANES] = dw_ref[j]
        for i in range(layout.n_grp):
            o_ref[layout.wp_rows[0] : layout.wp_rows[1], i * layout.G : (i + 1) * layout.G] = wp_ref[i]
        for name, ref in zip(names, vec_refs):
            r, width = layout.vec[name]
            for h in range(width // PACK_W):
                o_ref[r + h : r + h + 1, :] = ref[:, h * PACK_W : (h + 1) * PACK_W]

    return pl.pallas_call(
        body,
        name="pack_small",
        out_shape=jax.ShapeDtypeStruct((layout.rows, PACK_W), F32),
    )(dwdw, dwp, *[vecs[k] for k in names])


def _adamw_small(layout, g_here, g_there, w_dw, m_dw, v_dw, w_pool, m_pool, v_pool, vec_w, vec_m, vec_v, row):
    names = list(vec_w)
    nv = len(names)

    def body(*refs):
        ga_ref, gb_ref = refs[0], refs[1]
        wdw, mdw, vdw, wp, mp, vp = refs[2:8]
        vw, vm, vv = refs[8 : 8 + nv], refs[8 + nv : 8 + 2 * nv], refs[8 + 2 * nv : 8 + 3 * nv]
        row_g, row_w, row_m, row_v = refs[8 + 3 * nv : 12 + 3 * nv]
        outs = refs[12 + 3 * nv :]
        acc = outs[-1]
        acc[...] = ga_ref[...] + gb_ref[...]

        def emit(o, g, w, m, v, idx=()):
            res = (g,) + _adamw_math(w, g, m, v)
            for ref, val in zip(o, res):
                ref[idx] = val

        me = 2 * lax.axis_index("x") + lax.axis_index("y")
        for j in range(layout.n_cc):

            @pl.when(me == j)
            def _(j=j):
                for k in range(wdw.shape[0]):
                    g = acc[layout.dw_rows[0] + k : layout.dw_rows[0] + k + 1, j * LANES : (j + 1) * LANES]
                    emit(outs[0:4], g, wdw[k], mdw[k], vdw[k], idx=k)

        for i in range(layout.n_grp):
            g = acc[layout.wp_rows[0] : layout.wp_rows[1], i * layout.G : (i + 1) * layout.G]
            emit(outs[4:8], g, wp[i], mp[i], vp[i], idx=i)
        for q, name in enumerate(names):
            r, width = layout.vec[name]
            for h in range(width // PACK_W):
                ls = slice(h * PACK_W, (h + 1) * PACK_W)
                g = acc[r + h : r + h + 1, :]
                emit(outs[8 + 4 * q : 12 + 4 * q], g, vw[q][:, ls], vm[q][:, ls], vv[q][:, ls], idx=(slice(None), ls))
        emit(outs[8 + 4 * nv : 12 + 4 * nv], row_g[...], row_w[...], row_m[...], row_v[...], idx=...)

    shapes = [w_dw.shape] * 4 + [w_pool.shape] * 4
    for name in names:
        shapes += [vec_w[name].shape] * 4
    shapes += [row[1].shape] * 4
    return pl.pallas_call(
        body,
        name="adamw_small",
        out_shape=[jax.ShapeDtypeStruct(s, F32) for s in shapes],
        scratch_shapes=[pltpu.VMEM(g_here.shape, F32)],
    )(g_here, g_there, w_dw, m_dw, v_dw, w_pool, m_pool, v_pool,
      *[vec_w[k] for k in names], *[vec_m[k] for k in names], *[vec_v[k] for k in names], *row)


def _allreduce_rows(g_part, loss_part, comm=()):
    n_pairs = N_DEV - 1

    def body(g_ref, l_ref, go_ref, lo_ref, land_g, land_l, sems):
        x, y, c = _place()
        copies = []
        for q, (src, land) in enumerate(((g_ref, land_g), (l_ref, land_l))):
            for r in range(1, N_DEV):
                fx, fy, fc = (r >> 2) & 1, (r >> 1) & 1, r & 1
                peer = (1 - x if fx else x, 1 - y if fy else y, 1 - c if fc else c)
                cp = _remote(src, land.at[r], sems, 2 * (q * n_pairs + r - 1), peer)
                cp.start()
                copies.append(cp)
        for cp in copies:
            cp.wait()

        def total(src, land):
            row = lambda r: src[...] if r == 0 else land[r]
            return ((row(0) + row(4)) + (row(2) + row(6))) + ((row(1) + row(5)) + (row(3) + row(7)))

        go_ref[...] = total(g_ref, land_g)
        lo_ref[...] = total(l_ref, land_l)

    vm = pl.BlockSpec(memory_space=pltpu.VMEM)
    return _call(
        "allreduce_rows",
        body,
        (),
        [vm] * 2,
        [vm] * 2,
        [jax.ShapeDtypeStruct(g_part.shape, F32), jax.ShapeDtypeStruct(loss_part.shape, F32)],
        (g_part, loss_part),
        scratch=[pltpu.VMEM((N_DEV,) + g_part.shape, F32), pltpu.VMEM((N_DEV,) + loss_part.shape, F32),
                 pltpu.SemaphoreType.DMA((4 * n_pairs,))],
        comm=comm,
    )


def kernel(x, g_mix, w_in, b_in, w_dw, b_dw, ln_g, ln_b, w_pool, s_pool, w_out, g_ffn, w_gate, w_up, w_down, g_final, loss_target, m_g_mix, m_w_in, m_b_in, m_w_dw, m_b_dw, m_ln_g, m_ln_b, m_w_pool, m_s_pool, m_w_out, m_g_ffn, m_w_gate, m_w_up, m_w_down, m_g_final, v_g_mix, v_w_in, v_b_in, v_w_dw, v_b_dw, v_ln_g, v_ln_b, v_w_pool, v_s_pool, v_w_out, v_g_ffn, v_w_gate, v_w_up, v_w_down, v_g_final):
    x2 = x[0]
    target = loss_target[0]
    T, D = x2.shape
    w_in2, w_out2, w_down2 = w_in[0], w_out[0], w_down[0]
    taps_first = lambda a: jnp.transpose(a, (1, 0, 2))
    w_dw3 = taps_first(w_dw)
    w_gateT, w_upT = w_gate[0].T, w_up[0].T
    CI = w_in2.shape[1] * N_CHIPS
    DM = w_out2.shape[0] * N_CHIPS
    F = w_down2.shape[0] * N_CHIPS
    KW, _, dw_cols = w_dw3.shape
    assert dw_cols == LANES
    n_grp, G = w_pool.shape[1], w_pool.shape[-1]
    w_pool3 = w_pool[0]
    g_final2 = g_final.reshape(1, D)

    me = (2 * lax.axis_index("x") + lax.axis_index("y")).astype(jnp.int32).reshape(1)

    w_inT_b, w_dw4, f_out, f_gate, f_up, f_down = _place_and_gather(
        [(w_in2, "rows", (CI, D), BF16, True, True), (w_dw3, "lead", (N_CHIPS, KW, 1, dw_cols), F32, False, False)],
        [(w, "rows", shape, BF16, False, True)
         for w, shape in ((w_out2, (DM, D)), (w_gateT, (F, D)), (w_upT, (F, D)), (w_down2, (F, D)))])
    ici = lambda f: _GatherIci([f], ["rows"], [True])
    d2d = lambda f: _GatherD2d([f], ["rows"])
    gather = _start("gather_start", [ici(f_out), ici(f_gate), ici(f_up), ici(f_down)])
    (z, xn_b), _ = _in_proj(x2, g_mix, w_inT_b, b_in, after=[gather.token])
    (f_out,) = _wait("gather_out_wait", gather, 0, xn_b)
    s_out = _start("share_out_start", [d2d(f_out)], sibling_only=True)
    (y_b, v), _ = _seq_fwd(z, w_dw4, b_dw, ln_g, ln_b, w_pool3, s_pool, after=[s_out.token])
    (w_out_b,) = _wait("share_out_wait", s_out, 0, y_b)
    (f_gate,) = _wait("gather_gate_wait", gather, 1, y_b)
    s_gate = _start("share_gate_start", [d2d(f_gate)], sibling_only=True)
    (h1, hn_b), _ = _out_proj(y_b, x2, w_out_b, g_ffn, after=[s_gate.token])
    (f_up,) = _wait("gather_up_wait", gather, 2, hn_b)
    s_up = _start("share_up_start", [d2d(f_up)], sibling_only=True)
    (wgT_b,) = _wait("share_gate_wait", s_gate, 0, hn_b)
    (wuT_b,) = _wait("share_up_wait", s_up, 0, hn_b)
    (silu_b, uds_b, a_b), _ = _gate_up(hn_b, wgT_b, wuT_b)
    (f_down,) = _wait("gather_down_wait", gather, 3, a_b)
    s_down = _start("share_down_start", [d2d(f_down)], sibling_only=True)
    (wd_b,) = _wait("share_down_wait", s_down, 0, a_b)
    (dh2, dh2_b, loss_part, d_g_final), _ = _down_loss(a_b, wd_b, h1, target, g_final2)

    gw_down = _weight_grad("grad_w_down", a_b, dh2_b)
    x_down = _start("scatter_down_start", [_Scatter([gw_down], ["rows"])])
    (dg_b, du_b), _ = _ffn_bwd_act(dh2_b, wd_b, silu_b, uds_b, after=[x_down.token])
    gw_gateT = _weight_grad("grad_w_gate", dg_b, hn_b)
    gw_upT = _weight_grad("grad_w_up", du_b, hn_b)
    gw_down, p_down = _wait("scatter_down_wait", x_down, 0, gw_upT)
    sum_down = _sum_parts("sum_w_down", gw_down, "rows", [p_down], me)
    (dh1, dh1_b, dy, d_g_ffn), (p_gate, oth_down) = _ffn_bwd_in(
        dg_b, du_b, wgT_b, wuT_b, h1, dh2, g_ffn, w_out_b, comm=[_Scatter([gw_gateT], ["rows"]), _Swap([sum_down])])
    gw_out = _weight_grad("grad_w_out", y_b, dh1_b)
    sum_gate = _sum_parts("sum_w_gate", gw_gateT, "rows", [p_gate], me)
    res = {}
    res["w_down"] = _adamw_on_sparsecore("adamw_w_down", w_down2, m_w_down[0], v_w_down[0], sum_down, oth_down, sum_down)
    (dz_b, d_wdw, d_bdw, d_lng, d_lnb, d_wp, d_sp, d_bin), (p_up, p_out, oth_gate) = _seq_bwd(
        z, dy, v, w_dw4, ln_g, ln_b, w_pool3, s_pool,
        comm=[_Scatter([gw_upT, gw_out], ["rows", "rows"]), _Swap([sum_gate])])
    vec_grads ={"b_dw": d_bdw, "ln_g": d_lng, "ln_b": d_lnb, "s_pool": d_sp, "g_ffn": d_g_ffn, "g_final": d_g_final, "b_in": d_bin}
    layout = _PackLayout(dw_cols * N_CHIPS // LANES, n_grp, G, [(k, a.shape[1]) for k, a in vec_grads.items()])
    pack = _pack_small(layout, d_wdw, d_wp, vec_grads)
    sum_up = _sum_parts("sum_w_up", gw_upT, "rows", [p_up], me)
    sum_out = _sum_parts("sum_w_out", gw_out, "rows", [p_out], me)
    res["w_gate"] = _adamw_on_sparsecore(
        "adamw_w_gate", w_gateT, m_w_gate[0].T, v_w_gate[0].T, sum_gate, oth_gate, sum_up)
    mid = _start("mid_start", [_Swap([sum_up, sum_out]), _Scatter([pack], ["all"])])
    gw_inT = _weight_grad("grad_w_in", dz_b, xn_b, after=[mid.token])
    sum_up, sum_out, oth_up, oth_out = _wait("mid_swap_wait", mid, 0, gw_inT)
    late = _start("late_start", [_Scatter([gw_inT], ["rows"])])
    (grad_x, d_g_mix), _ = _in_proj_bwd(dz_b, w_inT_b, x2, dh1, g_mix, after=[late.token])
    pack, p_small = _wait("mid_small_wait", mid, 1, d_g_mix)
    gw_inT, p_in = _wait("late_w_in_wait", late, 0, d_g_mix)
    sum_small = _sum_parts("sum_small", pack, "all", [p_small], me)
    res["w_up"] = _adamw_on_sparsecore("adamw_w_up", w_upT, m_w_up[0].T, v_w_up[0].T, sum_up, oth_up, res["w_gate"][0])
    res["w_out"] = _adamw_on_sparsecore("adamw_w_out", w_out2, m_w_out[0], v_w_out[0], sum_out, oth_out, res["w_gate"][0])
    sum_in = _sum_parts("sum_w_in", gw_inT, "rows", [p_in], me)
    (d_g_mix, loss_row), (oth_in, oth_small) = _allreduce_rows(d_g_mix, loss_part, comm=[_Swap([sum_in, sum_small])])
    loss = loss_row[0, 0]
    res["w_in"], _ = _adamw("adamw_w_in", w_in2, m_w_in[0], v_w_in[0], sum_in, oth_in, g_transposed=True)

    vec_w = {"b_dw": b_dw, "ln_g": ln_g, "ln_b": ln_b, "s_pool": s_pool, "g_ffn": g_ffn, "g_final": g_final2, "b_in": b_in}
    vec_m = {"b_dw": m_b_dw, "ln_g": m_ln_g, "ln_b": m_ln_b, "s_pool": m_s_pool, "g_ffn": m_g_ffn,
             "g_final": m_g_final.reshape(1, D), "b_in": m_b_in}
    vec_v = {"b_dw": v_b_dw, "ln_g": v_ln_g, "ln_b": v_ln_b, "s_pool": v_s_pool, "g_ffn": v_g_ffn,
             "g_final": v_g_final.reshape(1, D), "b_in": v_b_in}
    small = _adamw_small(layout, sum_small, oth_small, w_dw3, taps_first(m_w_dw), taps_first(v_w_dw),
                         w_pool3, m_w_pool[0], v_w_pool[0], vec_w, vec_m, vec_v, (d_g_mix, g_mix, m_g_mix, v_g_mix))
    res["w_dw"] = [taps_first(a) for a in small[0:4]]
    res["w_pool"] = [a[None] for a in small[4:8]]
    for q, k in enumerate(vec_w):
        res[k] = list(small[8 + 4 * q : 12 + 4 * q])
    res["g_mix"] = list(small[-4:])
    res["g_final"] = [a.reshape(D) for a in res["g_final"]]
    for k in ("w_in", "w_out", "w_down"):
        res[k] = [a[None] for a in res[k]]
    for k in ("w_gate", "w_up"):
        res[k] = [a.T[None] for a in res[k]]

    order = ["g_mix", "w_in", "b_in", "w_dw", "b_dw", "ln_g", "ln_b", "w_pool", "s_pool", "w_out", "g_ffn", "w_gate", "w_up", "w_down", "g_final"]
    outs = [loss, grad_x[None]]
    for q in range(4):
        outs += [res[k][q] for k in order]
    return tuple(outs)
```

```python
import jax
import jax.numpy as jnp
from jax import lax
from jax.experimental import pallas as pl
from jax.experimental.pallas import tpu as pltpu
from jax.experimental.pallas import tpu_sc as plsc

F32 = jnp.float32
BF16 = jnp.bfloat16
MESH = pl.DeviceIdType.MESH
ANY = pl.BlockSpec(memory_space=pl.ANY)

RMS_EPS = 1e-6
LN_EPS = 1e-5
POOL_WINDOWS = (2, 4, 8, 16)
ADAM_LR = 0.001
ADAM_B1 = 0.9
ADAM_B2 = 0.999
ADAM_EPS = 1e-08
ADAM_WD = 0.01
ADAM_STEP = 10

LANES = 128
SUBLANES = 8
BF16_ROWS = 16
HALO = 32
CONV_ROWS = 64
HIDDEN_CHUNK = 512
VMEM_LIMIT = 56 * 1024 * 1024
PACK_W = 512
N_CHIPS = 4
N_DEV = 8
SIBLING_BARRIER_ID = 0
SC_CORES = 2
SC_TILES = 32
SC_LANES = 16


def _tile(n, want, mult=8):
    t = min(n, want)
    while n % t or t % mult:
        t -= 1
    return t


def _sigmoid(x):
    return 1.0 / (1.0 + jnp.exp(-x))


def _dot(a, b, dims):
    return lax.dot_general(a, b, (dims, ((), ())), preferred_element_type=F32)


NN = ((1,), (0,))
NT = ((1,), (1,))
TN = ((0,), (0,))


def _rms_bwd(x, g, dy):
    r = lax.rsqrt(jnp.mean(x * x, axis=-1, keepdims=True) + RMS_EPS)
    xh = x * r
    gy = dy * g
    dx = r * (gy - xh * jnp.mean(gy * xh, axis=-1, keepdims=True))
    return dx, dy * xh


def _accumulate(ref, first, val):
    @pl.when(first)
    def _():
        ref[...] = val

    @pl.when(jnp.logical_not(first))
    def _():
        ref[...] += val


def _place():
    return lax.axis_index("x"), lax.axis_index("y"), lax.axis_index("c")


def _other_chips(x, y):
    return [(1 - x, y), (x, 1 - y), (1 - x, 1 - y)]


def _rows(ref, start, n):
    return ref.at[pl.ds(pl.multiple_of(start, BF16_ROWS), n)]


def _window(ref, how, k, c=None):
    if how == "all":
        return ref
    if how == "lead":
        return ref.at[k]
    assert how == "rows"
    n = ref.shape[0] // N_CHIPS
    if c is None:
        return _rows(ref, k * n, n)
    return _rows(ref, k * n + c * (n // 2), n // 2)


def _remote(src, dst, sems, s, device):
    return pltpu.make_async_remote_copy(
        src_ref=src, dst_ref=dst, send_sem=sems.at[s], recv_sem=sems.at[s + 1], device_id=device, device_id_type=MESH)


class _GatherIci:
    aliased = True

    def __init__(self, fulls, hows, splits):
        self.fulls, self.hows, self.splits = list(fulls), list(hows), list(splits)

    def inputs(self):
        return self.fulls

    def out_shapes(self):
        return [jax.ShapeDtypeStruct(a.shape, a.dtype) for a in self.fulls]

    def n_sems(self):
        return 6 * len(self.fulls)

    def build(self, ins, outs, sems, base):
        x, y, c = _place()
        me = 2 * x + y
        chips = _other_chips(x, y)
        starts, waits = [], []
        for a, (how, sp) in enumerate(zip(self.hows, self.splits)):
            half = c if sp else None
            mine = _window(outs[a], how, me, half)
            for j, (px, py) in enumerate(chips):
                s = base + 6 * a + 2 * j
                cp = _remote(mine, mine, sems, s, (px, py, c))
                landing = _remote(mine, _window(outs[a], how, 2 * px + py, half), sems, s, (px, py, c))
                starts.append(cp.start)
                waits += [landing.wait_recv, cp.wait_send]
        return starts, waits


class _GatherD2d:
    aliased = True

    def __init__(self, fulls, hows):
        self.fulls, self.hows = list(fulls), list(hows)

    def inputs(self):
        return self.fulls

    def out_shapes(self):
        return [jax.ShapeDtypeStruct(a.shape, a.dtype) for a in self.fulls]

    def n_sems(self):
        return 6 * len(self.fulls)

    def build(self, ins, outs, sems, base):
        x, y, c = _place()
        starts, waits = [], []
        for a, how in enumerate(self.hows):
            for j, (px, py) in enumerate(_other_chips(x, y)):
                s = base + 6 * a + 2 * j
                got = _window(outs[a], how, 2 * px + py, c)
                cp = _remote(got, got, sems, s, (x, y, 1 - c))
                landing = _remote(got, _window(outs[a], how, 2 * px + py, 1 - c), sems, s, (x, y, 1 - c))
                starts.append(cp.start)
                waits += [landing.wait_recv, cp.wait_send]
        return starts, waits


def _part_shape(a, how):
    if how == "all":
        return a.shape
    assert how == "rows"
    return (a.shape[0] // N_CHIPS, a.shape[1])


class _Scatter:
    aliased = False

    def __init__(self, fulls, hows):
        self.fulls, self.hows = list(fulls), list(hows)

    def inputs(self):
        return self.fulls

    def out_shapes(self):
        return [jax.ShapeDtypeStruct((N_CHIPS - 1,) + _part_shape(a, h), a.dtype) for a, h in zip(self.fulls, self.hows)]

    def n_sems(self):
        return 6 * len(self.fulls)

    def build(self, ins, outs, sems, base):
        x, y, c = _place()
        chips = _other_chips(x, y)
        starts, waits = [], []
        for a, how in enumerate(self.hows):
            for j, (px, py) in enumerate(chips):
                cp = _remote(_window(ins[a], how, 2 * px + py), outs[a].at[j], sems, base + 6 * a + 2 * j, (px, py, c))
                starts.append(cp.start)
                waits += [cp.wait_recv, cp.wait_send]
        return starts, waits


class _Swap:
    aliased = False

    def __init__(self, arrays):
        self.arrays = list(arrays)

    def inputs(self):
        return self.arrays

    def out_shapes(self):
        return [jax.ShapeDtypeStruct(a.shape, a.dtype) for a in self.arrays]

    def n_sems(self):
        return 2 * len(self.arrays)

    def build(self, ins, outs, sems, base):
        x, y, c = _place()
        starts, waits = [], []
        for a in range(len(ins)):
            cp = _remote(ins[a], outs[a], sems, base + 2 * a, (x, y, 1 - c))
            starts.append(cp.start)
            waits += [cp.wait_recv, cp.wait_send]
        return starts, waits


def _call(name, body, grid, in_specs, out_specs, out_shape, args, scratch=(), comm=(), after=()):
    comm, after = list(comm), list(after)
    n_in, n_out, n_scr, n_after = len(args), len(out_shape), len(scratch), len(after)
    c_in = [a for op in comm for a in op.inputs()]
    c_out = [s for op in comm for s in op.out_shapes()]
    n_sems = sum(op.n_sems() for op in comm)
    aliases, i_in, i_out = {}, 0, 0
    for op in comm:
        if op.aliased:
            for q in range(len(op.inputs())):
                aliases[n_in + n_after + i_in + q] = n_out + i_out + q
        i_in, i_out = i_in + len(op.inputs()), i_out + len(op.out_shapes())

    def wrapped(*refs):
        ins = refs[:n_in]
        cin = refs[n_in + n_after : n_in + n_after + len(c_in)]
        o0 = n_in + n_after + len(c_in)
        outs = refs[o0 : o0 + n_out]
        cout = refs[o0 + n_out : o0 + n_out + len(c_out)]
        s0 = o0 + n_out + len(c_out)
        scr = refs[s0 : s0 + n_scr]

        def copies():
            sems = refs[s0 + n_scr]
            starts, waits = [], []
            i_in = i_out = base = 0
            for op in comm:
                ni, no = len(op.inputs()), len(op.out_shapes())
                s, w = op.build(cin[i_in : i_in + ni], cout[i_out : i_out + no], sems, base)
                starts += s
                waits += w
                i_in, i_out, base = i_in + ni, i_out + no, base + op.n_sems()
            return starts, waits

        def run_starts():
            for start in copies()[0]:
                start()

        def run_waits():
            for wait in copies()[1]:
                wait()

        if comm and grid:
            first = last = True
            for d, n in enumerate(grid):
                first = jnp.logical_and(first, pl.program_id(d) == 0)
                last = jnp.logical_and(last, pl.program_id(d) == n - 1)
            pl.when(first)(run_starts)
        elif comm:
            run_starts()
        if body is not None:
            body(*ins, *outs, *scr)
        if comm and grid:
            pl.when(last)(run_waits)
        elif comm:
            run_waits()

    res = pl.pallas_call(
        wrapped,
        name=name,
        grid=grid,
        in_specs=list(in_specs) + [ANY] * (n_after + len(c_in)),
        out_specs=list(out_specs) + [ANY] * len(c_out),
        out_shape=list(out_shape) + c_out,
        scratch_shapes=list(scratch) + ([pltpu.SemaphoreType.DMA((n_sems,))] if comm else []),
        input_output_aliases=aliases,
        compiler_params=pltpu.CompilerParams(dimension_semantics=("arbitrary",) * len(grid), vmem_limit_bytes=VMEM_LIMIT),
    )(*args, *after, *c_in)
    return tuple(res[:n_out]), tuple(res[n_out:])


def _place_and_gather(now, later):
    items = list(now) + list(later)
    n, n_now = len(items), len(now)
    buf_shape = lambda it: it[0].shape[::-1] if it[4] else it[0].shape
    split_now = [a for a in range(n_now) if items[a][5]]

    def body(*refs):
        ins, outs = refs[:n], refs[n : 2 * n]
        stage, bufs = refs[2 * n : 3 * n - n_now], refs[3 * n - n_now : 4 * n - n_now]
        sems = refs[4 * n - n_now]
        x, y, c = _place()
        me = 2 * x + y
        chips = _other_chips(x, y)
        loads = [pltpu.make_async_copy(ins[a], stage[a - n_now], sems.at[a]) for a in range(n_now, n)]
        for ld in loads:
            ld.start()
        pending = []

        def place(a, val):
            _, how, _, dtype, transposed, _ = items[a]
            bufs[a][...] = (val.T if transposed else val).astype(dtype)
            cp = pltpu.make_async_copy(bufs[a], _window(outs[a], how, me), sems.at[n + a])
            cp.start()
            pending.append(cp.wait)

        arrivals = []
        for a in range(n_now):
            place(a, ins[a][...])
            how, split = items[a][1], items[a][5]
            half = c if split else None
            src = _rows(bufs[a], c * (bufs[a].shape[0] // 2), bufs[a].shape[0] // 2) if split else bufs[a]
            for j, (px, py) in enumerate(chips):
                s = 2 * n + 6 * a + 2 * j
                cp = _remote(src, _window(outs[a], how, me, half), sems, s, (px, py, c))
                landing = _remote(src, _window(outs[a], how, 2 * px + py, half), sems, s, (px, py, c))
                cp.start()
                arrivals.append(landing.wait_recv)
                pending.append(cp.wait_send)
        for a in range(n_now, n):
            loads[a - n_now].wait()
            place(a, stage[a - n_now][...])
        for wait in arrivals:
            wait()
        d2d = _GatherD2d([None] * len(split_now), [items[a][1] for a in split_now])
        starts, waits = d2d.build(None, [outs[a] for a in split_now], sems, 2 * n + 6 * n_now)
        for start in starts:
            start()
        for wait in waits + pending:
            wait()

    vm = pl.BlockSpec(memory_space=pltpu.VMEM)
    return pl.pallas_call(
        body,
        name="place_and_gather",
        in_specs=[vm] * n_now + [ANY] * (n - n_now),
        out_specs=[ANY] * n,
        out_shape=[jax.ShapeDtypeStruct(it[2], it[3]) for it in items],
        scratch_shapes=[pltpu.VMEM(it[0].shape, it[0].dtype) for it in later]
        + [pltpu.VMEM(buf_shape(it), it[3]) for it in items]
        + [pltpu.SemaphoreType.DMA((2 * n + 6 * n_now + 6 * len(split_now),))],
        compiler_params=pltpu.CompilerParams(vmem_limit_bytes=VMEM_LIMIT),
    )(*[it[0] for it in items])


_HBM = pl.BlockSpec(memory_space=pltpu.HBM)
_SEM = pl.BlockSpec(memory_space=pltpu.SEMAPHORE)
_DATAFLOW = pltpu.SideEffectType.DATAFLOW_SIDE_EFFECTING


class _Pending:
    def __init__(self, ops, bases, sems, arrays, token):
        self.ops, self.bases, self.sems, self.arrays, self.token = ops, bases, sems, arrays, token


def _op_refs(op, refs):
    n_src = len(op.inputs())
    return refs[:n_src], (refs[:n_src] if op.aliased else refs[n_src:])


def _start(name, ops, sibling_only=False):
    per_op = [list(op.inputs()) + ([] if op.aliased else [lax.empty(sd.shape, sd.dtype) for sd in op.out_shapes()])
              for op in ops]
    arrays = [a for group in per_op for a in group]
    bases = [sum(op.n_sems() for op in ops[:k]) for k in range(len(ops))]
    n = len(arrays)

    def body(*refs):
        sems, token = refs[n], refs[-1]
        if sibling_only:
            x, y, c = _place()
            barrier = pltpu.get_barrier_semaphore()
            pl.semaphore_signal(barrier, inc=1, device_id=(x, y, 1 - c), device_id_type=MESH)
            pl.semaphore_wait(barrier, 1)
        at = 0
        for op, group, base in zip(ops, per_op, bases):
            starts, _ = op.build(*_op_refs(op, refs[at : at + len(group)]), sems, base)
            for start in starts:
                start()
            at += len(group)
        token[...] = jnp.zeros_like(token)

    res = pl.pallas_call(
        body,
        name=name,
        out_shape=(pltpu.SemaphoreType.DMA((sum(op.n_sems() for op in ops),)),)
        + tuple(pltpu.HBM(a.shape, a.dtype) for a in arrays) + (jax.ShapeDtypeStruct((SUBLANES, LANES), F32),),
        in_specs=(_HBM,) * n,
        out_specs=(_SEM,) + (_HBM,) * n + (pl.BlockSpec(memory_space=pltpu.VMEM),),
        input_output_aliases={i: 1 + i for i in range(n)},
        compiler_params=pltpu.CompilerParams(
            has_side_effects=_DATAFLOW, collective_id=SIBLING_BARRIER_ID if sibling_only else None),
    )(*[pltpu.with_memory_space_constraint(a, pltpu.HBM) for a in arrays])
    thru, at, groups = list(res[1 : 1 + n]), 0, []
    for group in per_op:
        groups.append(thru[at : at + len(group)])
        at += len(group)
    return _Pending(list(ops), bases, res[0], groups, res[-1])


def _wait(name, pending, k, after):
    op, arrays = pending.ops[k], pending.arrays[k]
    n = len(arrays)

    def body(*refs):
        _, waits = op.build(*_op_refs(op, refs[:n]), refs[n], pending.bases[k])
        for wait in waits:
            wait()

    return pl.pallas_call(
        body,
        name=name,
        out_shape=tuple(pltpu.HBM(a.shape, a.dtype) for a in arrays),
        in_specs=(_HBM,) * n + (_SEM, ANY),
        out_specs=(_HBM,) * n,
        input_output_aliases={i: i for i in range(n)},
        compiler_params=pltpu.CompilerParams(has_side_effects=_DATAFLOW),
    )(*arrays, pending.sems, after)


def _in_proj(x, g_mix, w_inT_b, b_in, after=()):
    T, D = x.shape
    CI = w_inT_b.shape[0]
    tm = _tile(T, 512)

    def body(x_ref, g_ref, w_ref, b_ref, z_ref, xn_ref):
        xv = x_ref[...]
        r = lax.rsqrt(jnp.mean(xv * xv, axis=-1, keepdims=True) + RMS_EPS)
        xn = (xv * r * g_ref[...]).astype(BF16)
        xn_ref[...] = xn
        z_ref[...] = _dot(xn, w_ref[...], NT) + b_ref[...]

    return _call(
        "in_proj",
        body,
        (T // tm,),
        [
            pl.BlockSpec((tm, D), lambda i: (i, 0)),
            pl.BlockSpec((1, D), lambda i: (0, 0)),
            pl.BlockSpec((CI, D), lambda i: (0, 0)),
            pl.BlockSpec((1, CI), lambda i: (0, 0)),
        ],
        [pl.BlockSpec((tm, CI), lambda i: (i, 0)), pl.BlockSpec((tm, D), lambda i: (i, 0))],
        [jax.ShapeDtypeStruct((T, CI), F32), jax.ShapeDtypeStruct((T, D), BF16)],
        (x, g_mix, w_inT_b, b_in),
        after=after,
    )


def _fill_shifted(scr):
    n = scr.shape[1] - SUBLANES
    for s in range(1, SUBLANES):
        scr[s, 0:n, :] = scr[0, s : s + n, :]


def _shifted_rows(scr, off, n, cs):
    s = off % SUBLANES
    return scr[s, off - s : off - s + n, cs]


def _pool_mean_minus_token(p_scr, cs, w, cnt, tt):
    tok = p_scr[HALO : HALO + tt, cs]
    s = tok
    for d in range(1, w):
        s = s + p_scr[HALO - d : HALO - d + tt, cs]
    return s / cnt - tok


def _seq_fwd(z, w_dw4, b_dw, ln_g, ln_b, w_pool, s_pool, after=()):
    T, CI = z.shape
    CC = ln_g.shape[1]
    n_grp, G = w_pool.shape[0], w_pool.shape[-1]
    KW = w_dw4.shape[1]
    D = CC + n_grp * G
    tt = _tile(T, 512, HALO)
    per = tt // HALO

    def body(zc_ref, zp_ref, wdw_ref, bdw_ref, lng_ref, lnb_ref, wp_ref, sp_ref, y_ref, v_ref, u_scr, p_scr):
        i = pl.program_id(0)
        first = i == 0
        u_prev = zp_ref[:, 0:CC] * _sigmoid(zp_ref[:, CC : 2 * CC])
        u_scr[0, 0:HALO, :] = jnp.where(first, 0.0, u_prev)
        p_scr[0:HALO, :] = jnp.where(first, 0.0, zp_ref[:, 2 * CC :])
        u_scr[0, HALO:, :] = zc_ref[:, 0:CC] * _sigmoid(zc_ref[:, CC : 2 * CC])
        p_scr[HALO:, :] = zc_ref[:, 2 * CC :]
        _fill_shifted(u_scr)

        for j in range(CC // LANES):
            cs = slice(LANES * j, LANES * (j + 1))
            for rb in range(tt // CONV_ROWS):
                acc = jnp.zeros((CONV_ROWS, LANES), F32)
                for k in range(KW):
                    off = HALO - (KW - 1) + k + rb * CONV_ROWS
                    acc = acc + _shifted_rows(u_scr, off, CONV_ROWS, cs) * wdw_ref[j, k]
                v_ref[rb * CONV_ROWS : (rb + 1) * CONV_ROWS, cs] = acc + bdw_ref[:, cs]

        v = v_ref[...]
        mu = jnp.mean(v, axis=-1, keepdims=True)
        d = v - mu
        var = jnp.mean(d * d, axis=-1, keepdims=True)
        ln = d * lax.rsqrt(var + LN_EPS) * lng_ref[...] + lnb_ref[...]
        y_ref[:, 0:CC] = (ln * _sigmoid(ln)).astype(BF16)

        tpos = i * tt + lax.broadcasted_iota(jnp.int32, (tt, 1), 0)
        for gi, w in enumerate(POOL_WINDOWS):
            cs = slice(G * gi, G * (gi + 1))
            cnt = jnp.minimum(tpos + 1, w).astype(F32)
            yi = _pool_mean_minus_token(p_scr, cs, w, cnt, tt)
            q = _dot(yi.astype(BF16), wp_ref[gi].astype(BF16), NN)
            y_ref[:, CC + G * gi : CC + G * (gi + 1)] = (q * sp_ref[:, cs]).astype(BF16)

    const2 = lambda i: (0, 0)
    return _call(
        "seq_fwd",
        body,
        (T // tt,),
        [
            pl.BlockSpec((tt, CI), lambda i: (i, 0)),
            pl.BlockSpec((HALO, CI), lambda i: (jnp.maximum(i * per - 1, 0), 0)),
            pl.BlockSpec(w_dw4.shape, lambda i: (0,) * w_dw4.ndim),
            pl.BlockSpec((1, CC), const2),
            pl.BlockSpec((1, CC), const2),
            pl.BlockSpec((1, CC), const2),
            pl.BlockSpec(w_pool.shape, lambda i: (0, 0, 0)),
            pl.BlockSpec((1, n_grp * G), const2),
        ],
        [pl.BlockSpec((tt, D), lambda i: (i, 0)), pl.BlockSpec((tt, CC), lambda i: (i, 0))],
        [jax.ShapeDtypeStruct((T, D), BF16), jax.ShapeDtypeStruct((T, CC), F32)],
        (z, z, w_dw4, b_dw, ln_g, ln_b, w_pool, s_pool),
        scratch=[pltpu.VMEM((SUBLANES, HALO + tt, CC), F32), pltpu.VMEM((HALO + tt, n_grp * G), F32)],
        after=after,
    )


def _out_proj(y_b, x, w_out_b, g_ffn, after=()):
    T, D = x.shape
    tm = _tile(T, 512)

    def body(y_ref, x_ref, w_ref, g_ref, h1_ref, hn_ref):
        h1 = x_ref[...] + _dot(y_ref[...], w_ref[...], NN)
        h1_ref[...] = h1
        r = lax.rsqrt(jnp.mean(h1 * h1, axis=-1, keepdims=True) + RMS_EPS)
        hn_ref[...] = (h1 * r * g_ref[...]).astype(BF16)

    row = lambda i: (i, 0)
    return _call(
        "out_proj",
        body,
        (T // tm,),
        [
            pl.BlockSpec((tm, y_b.shape[1]), row),
            pl.BlockSpec((tm, D), row),
            pl.BlockSpec(w_out_b.shape, lambda i: (0, 0)),
            pl.BlockSpec((1, D), lambda i: (0, 0)),
        ],
        [pl.BlockSpec((tm, D), row), pl.BlockSpec((tm, D), row)],
        [jax.ShapeDtypeStruct((T, D), F32), jax.ShapeDtypeStruct((T, D), BF16)],
        (y_b, x, w_out_b, g_ffn),
        after=after,
    )


def _hidden_tile(F):
    return _tile(F, 1408, LANES)


def _gate_up(hn_b, wgT_b, wuT_b):
    T, D = hn_b.shape
    F = wgT_b.shape[0]
    tm, tf = _tile(T, 1024), _hidden_tile(F)

    def body(hn_ref, wg_ref, wu_ref, silu_ref, uds_ref, a_ref):
        hn = hn_ref[...]
        for c0 in range(0, tf, HIDDEN_CHUNK):
            cs = slice(c0, min(c0 + HIDDEN_CHUNK, tf))
            gv = _dot(hn, wg_ref[cs, :], NT)
            uv = _dot(hn, wu_ref[cs, :], NT)
            sg = _sigmoid(gv)
            silu = gv * sg
            silu_ref[:, cs] = silu.astype(BF16)
            uds_ref[:, cs] = (uv * (sg * (1.0 + gv * (1.0 - sg)))).astype(BF16)
            a_ref[:, cs] = (silu * uv).astype(BF16)

    wspec = pl.BlockSpec((tf, D), lambda j, i: (j, 0))
    ospec = pl.BlockSpec((tm, tf), lambda j, i: (i, j))
    return _call(
        "gate_up",
        body,
        (F // tf, T // tm),
        [pl.BlockSpec((tm, D), lambda j, i: (i, 0)), wspec, wspec],
        [ospec, ospec, ospec],
        [jax.ShapeDtypeStruct((T, F), BF16)] * 3,
        (hn_b, wgT_b, wuT_b),
    )


def _down_loss(a_b, wd_b, h1, target, g_final):
    T, D = h1.shape
    F = a_b.shape[1]
    tm = _tile(T, 512)
    nt = T // tm

    def body(a_ref, w_ref, h1_ref, t_ref, g_ref, dh2_ref, dh2b_ref, loss_ref, dg_ref):
        i = pl.program_id(0)
        h2 = h1_ref[...] + _dot(a_ref[...], w_ref[...], NN)
        r = lax.rsqrt(jnp.mean(h2 * h2, axis=-1, keepdims=True) + RMS_EPS)
        g = g_ref[...]
        diff = h2 * r * g - t_ref[...]
        _accumulate(loss_ref, i == 0, jnp.full(loss_ref.shape, jnp.sum(diff * diff) * (0.5 / D), F32))
        dh2, dg_rows = _rms_bwd(h2, g, diff * (1.0 / D))
        dh2_ref[...] = dh2
        dh2b_ref[...] = dh2.astype(BF16)
        _accumulate(dg_ref, i == 0, jnp.sum(dg_rows, axis=0, keepdims=True))

    row = lambda i: (i, 0)
    return _call(
        "down_loss",
        body,
        (nt,),
        [
            pl.BlockSpec((tm, F), row),
            pl.BlockSpec((F, D), lambda i: (0, 0), pipeline_mode=pl.Buffered(1)),
            pl.BlockSpec((tm, D), row),
            pl.BlockSpec((tm, D), row),
            pl.BlockSpec((1, D), lambda i: (0, 0)),
        ],
        [
            pl.BlockSpec((tm, D), row),
            pl.BlockSpec((tm, D), row),
            pl.BlockSpec((1, LANES), lambda i: (0, 0)),
            pl.BlockSpec((1, D), lambda i: (0, 0)),
        ],
        [
            jax.ShapeDtypeStruct((T, D), F32),
            jax.ShapeDtypeStruct((T, D), BF16),
            jax.ShapeDtypeStruct((1, LANES), F32),
            jax.ShapeDtypeStruct((1, D), F32),
        ],
        (a_b, wd_b, h1, target, g_final),
    )


def _ffn_bwd_act(dh2_b, wd_b, silu_b, uds_b, after=()):
    T, D = dh2_b.shape
    F = wd_b.shape[0]
    tm, tf = _tile(T, 1024), _hidden_tile(F)

    def body(d_ref, w_ref, silu_ref, uds_ref, dg_ref, du_ref):
        d = d_ref[...]
        f0 = pl.multiple_of(pl.program_id(1) * tf, LANES)
        for c0 in range(0, tf, HIDDEN_CHUNK):
            cs = slice(c0, min(c0 + HIDDEN_CHUNK, tf))
            da = _dot(d, w_ref[pl.ds(f0 + c0, cs.stop - c0), :], NT)
            dg_ref[:, cs] = (da * uds_ref[:, cs].astype(F32)).astype(BF16)
            du_ref[:, cs] = (da * silu_ref[:, cs].astype(F32)).astype(BF16)

    aspec = pl.BlockSpec((tm, tf), lambda i, j: (i, j))
    return _call(
        "ffn_bwd_act",
        body,
        (T // tm, F // tf),
        [pl.BlockSpec((tm, D), lambda i, j: (i, 0)),
         pl.BlockSpec((F, D), lambda i, j: (0, 0), pipeline_mode=pl.Buffered(1)), aspec, aspec],
        [aspec, aspec],
        [jax.ShapeDtypeStruct((T, F), BF16)] * 2,
        (dh2_b, wd_b, silu_b, uds_b),
        after=after,
    )


def _ffn_bwd_in(dg_b, du_b, wgT_b, wuT_b, h1, dh2, g_ffn, w_out_b, comm=()):
    T, D = h1.shape
    F = wgT_b.shape[0]
    DM = w_out_b.shape[0]
    tm = _tile(T, 512)

    def body(dg_ref, du_ref, wg_ref, wu_ref, h1_ref, dh2_ref, g_ref, wo_ref, dh1_ref, dh1b_ref, dy_ref, dgf_ref):
        i = pl.program_id(0)
        dhn = _dot(dg_ref[...], wg_ref[...], NN) + _dot(du_ref[...], wu_ref[...], NN)
        dx, dg_rows = _rms_bwd(h1_ref[...], g_ref[...], dhn)
        dh1 = dh2_ref[...] + dx
        dh1b = dh1.astype(BF16)
        dh1_ref[...] = dh1
        dh1b_ref[...] = dh1b
        dy_ref[...] = _dot(dh1b, wo_ref[...], NT)
        _accumulate(dgf_ref, i == 0, jnp.sum(dg_rows, axis=0, keepdims=True))

    row = lambda i: (i, 0)
    const = lambda i: (0, 0)
    return _call(
        "ffn_bwd_in",
        body,
        (T // tm,),
        [
            pl.BlockSpec((tm, F), row),
            pl.BlockSpec((tm, F), row),
            pl.BlockSpec((F, D), const, pipeline_mode=pl.Buffered(1)),
            pl.BlockSpec((F, D), const, pipeline_mode=pl.Buffered(1)),
            pl.BlockSpec((tm, D), row),
            pl.BlockSpec((tm, D), row),
            pl.BlockSpec((1, D), const),
            pl.BlockSpec((DM, D), const, pipeline_mode=pl.Buffered(1)),
        ],
        [pl.BlockSpec((tm, D), row), pl.BlockSpec((tm, D), row), pl.BlockSpec((tm, DM), row), pl.BlockSpec((1, D), const)],
        [
            jax.ShapeDtypeStruct((T, D), F32),
            jax.ShapeDtypeStruct((T, D), BF16),
            jax.ShapeDtypeStruct((T, DM), F32),
            jax.ShapeDtypeStruct((1, D), F32),
        ],
        (dg_b, du_b, wgT_b, wuT_b, h1, dh2, g_ffn, w_out_b),
        comm=comm,
    )


def _seq_bwd(z, dy, v, w_dw4, ln_g, ln_b, w_pool, s_pool, comm=()):
    T, CI = z.shape
    CC = ln_g.shape[1]
    n_grp, G = w_pool.shape[0], w_pool.shape[-1]
    CP = n_grp * G
    KW = w_dw4.shape[1]
    n_cc = CC // LANES
    D = CC + CP
    tt = _tile(T, 512, HALO)
    per = tt // HALO
    n_tiles = T // tt
    last_halo = T // HALO - 1

    def body(zc_ref, zp_ref, dyc_ref, dyn_ref, vc_ref, vn_ref, wdw_ref, lng_ref, lnb_ref, wp_ref, sp_ref,
             dz_ref, dwdw_ref, dbdw_ref, dlng_ref, dlnb_ref, dwp_ref, dsp_ref, dbin_ref,
             dv_scr, u_scr, p_scr, g_scr, dw_scr):
        i = pl.program_id(0)
        first = i == 0
        last = i == n_tiles - 1
        lng, lnb = lng_ref[...], lnb_ref[...]

        def conv_pre(vv, dyc):
            mu = jnp.mean(vv, axis=-1, keepdims=True)
            d = vv - mu
            rs = lax.rsqrt(jnp.mean(d * d, axis=-1, keepdims=True) + LN_EPS)
            xh = d * rs
            ln = xh * lng + lnb
            sg = _sigmoid(ln)
            dln = dyc * (sg * (1.0 + ln * (1.0 - sg)))
            dxh = dln * lng
            dv = rs * (dxh - jnp.mean(dxh, axis=-1, keepdims=True) - xh * jnp.mean(dxh * xh, axis=-1, keepdims=True))
            return dv, dln, xh

        dv_c, dln_c, xh_c = conv_pre(vc_ref[...], dyc_ref[:, 0:CC])
        dv_scr[0, 0:tt, :] = dv_c
        dv_n, _, _ = conv_pre(vn_ref[...], dyn_ref[:, 0:CC])
        dv_scr[0, tt:, :] = jnp.where(last, 0.0, dv_n)
        _fill_shifted(dv_scr)
        _accumulate(dlng_ref, first, jnp.sum(dln_c * xh_c, axis=0, keepdims=True))
        _accumulate(dlnb_ref, first, jnp.sum(dln_c, axis=0, keepdims=True))
        _accumulate(dbdw_ref, first, jnp.sum(dv_c, axis=0, keepdims=True))

        u_scr[...] = zc_ref[:, 0:CC] * _sigmoid(zc_ref[:, CC : 2 * CC])

        @pl.when(first)
        def _():
            dw_scr[...] = jnp.zeros_like(dw_scr)

        for j in range(n_cc):
            cs = slice(LANES * j, LANES * (j + 1))
            gs = slice(CC + LANES * j, CC + LANES * (j + 1))
            dbin_a = jnp.zeros((1, LANES), F32)
            dbin_g = jnp.zeros((1, LANES), F32)
            for rb in range(tt // CONV_ROWS):
                rows = slice(rb * CONV_ROWS, (rb + 1) * CONV_ROWS)
                u_blk = u_scr[rows, cs]
                du = jnp.zeros((CONV_ROWS, LANES), F32)
                for k in range(KW):
                    off = rb * CONV_ROWS + (KW - 1) - k
                    d = _shifted_rows(dv_scr, off, CONV_ROWS, cs)
                    du = du + d * wdw_ref[j, k]
                    dw_scr[j * HALO + k] += jnp.sum((u_blk * d).reshape(CONV_ROWS // 8, 8, LANES), axis=0)
                a = zc_ref[rows, cs]
                sg = _sigmoid(zc_ref[rows, gs])
                da = du * sg
                dgate = du * a * sg * (1.0 - sg)
                dz_ref[rows, cs] = da.astype(BF16)
                dz_ref[rows, gs] = dgate.astype(BF16)
                dbin_a = dbin_a + jnp.sum(da, axis=0, keepdims=True)
                dbin_g = dbin_g + jnp.sum(dgate, axis=0, keepdims=True)
            _accumulate(dbin_ref.at[:, cs], first, dbin_a)
            _accumulate(dbin_ref.at[:, gs], first, dbin_g)

        @pl.when(last)
        def _():
            dwdw_ref[...] = jnp.sum(dw_scr[...], axis=1).reshape(dwdw_ref.shape)

        p_scr[0:HALO, :] = jnp.where(first, 0.0, zp_ref[:, 2 * CC :])
        p_scr[HALO:, :] = zc_ref[:, 2 * CC :]
        tpos = i * tt + lax.broadcasted_iota(jnp.int32, (tt, 1), 0)
        for gi, w in enumerate(POOL_WINDOWS):
            cs = slice(G * gi, G * (gi + 1))
            ys = slice(CC + G * gi, CC + G * (gi + 1))
            ps = slice(2 * CC + G * gi, 2 * CC + G * (gi + 1))
            cnt = jnp.minimum(tpos + 1, w).astype(F32)
            yib = _pool_mean_minus_token(p_scr, cs, w, cnt, tt).astype(BF16)
            wp = wp_ref[gi].astype(BF16)
            sp = sp_ref[:, cs]
            dyp = dyc_ref[:, ys]
            q = _dot(yib, wp, NN)
            _accumulate(dsp_ref.at[:, cs], first, jnp.sum(dyp * q, axis=0, keepdims=True))
            dq_c = (dyp * sp).astype(BF16)
            dq_n = (jnp.where(last, 0.0, dyn_ref[:, ys]) * sp).astype(BF16)
            _accumulate(dwp_ref.at[gi], first, _dot(yib, dq_c, TN))
            dyi_c = _dot(dq_c, wp, NT)
            g_scr[0:tt, cs] = dyi_c / cnt
            g_scr[tt:, cs] = _dot(dq_n, wp, NT) * (1.0 / w)
            dp = -dyi_c
            for d in range(w):
                dp = dp + g_scr[d : d + tt, cs]
            dz_ref[:, ps] = dp.astype(BF16)
            _accumulate(dbin_ref.at[:, ps], first, jnp.sum(dp, axis=0, keepdims=True))

    cur = lambda i: (i, 0)
    prev = lambda i: (jnp.maximum(i * per - 1, 0), 0)
    nxt = lambda i: (jnp.minimum((i + 1) * per, last_halo), 0)
    c2 = lambda i: (0, 0)
    c3 = lambda i: (0, 0, 0)
    return _call(
        "seq_bwd",
        body,
        (n_tiles,),
        [
            pl.BlockSpec((tt, CI), cur),
            pl.BlockSpec((HALO, CI), prev),
            pl.BlockSpec((tt, D), cur),
            pl.BlockSpec((HALO, D), nxt),
            pl.BlockSpec((tt, CC), cur),
            pl.BlockSpec((HALO, CC), nxt),
            pl.BlockSpec(w_dw4.shape, lambda i: (0,) * w_dw4.ndim),
            pl.BlockSpec((1, CC), c2),
            pl.BlockSpec((1, CC), c2),
            pl.BlockSpec(w_pool.shape, c3),
            pl.BlockSpec((1, CP), c2),
        ],
        [
            pl.BlockSpec((tt, CI), cur),
            pl.BlockSpec((n_cc, HALO, LANES), c3),
            pl.BlockSpec((1, CC), c2),
            pl.BlockSpec((1, CC), c2),
            pl.BlockSpec((1, CC), c2),
            pl.BlockSpec((n_grp, G, G), c3),
            pl.BlockSpec((1, CP), c2),
            pl.BlockSpec((1, CI), c2),
        ],
        [
            jax.ShapeDtypeStruct((T, CI), BF16),
            jax.ShapeDtypeStruct((n_cc, HALO, LANES), F32),
            jax.ShapeDtypeStruct((1, CC), F32),
            jax.ShapeDtypeStruct((1, CC), F32),
            jax.ShapeDtypeStruct((1, CC), F32),
            jax.ShapeDtypeStruct((n_grp, G, G), F32),
            jax.ShapeDtypeStruct((1, CP), F32),
            jax.ShapeDtypeStruct((1, CI), F32),
        ],
        (z, z, dy, dy, v, v, w_dw4, ln_g, ln_b, w_pool, s_pool),
        scratch=[
            pltpu.VMEM((SUBLANES, tt + HALO, CC), F32),
            pltpu.VMEM((tt, CC), F32),
            pltpu.VMEM((HALO + tt, CP), F32),
            pltpu.VMEM((tt + HALO, CP), F32),
            pltpu.VMEM((n_cc * HALO, 8, LANES), F32),
        ],
        comm=comm,
    )


def _in_proj_bwd(dz_b, w_inT_b, x, dh1, g_mix, after=()):
    T, D = x.shape
    CI = w_inT_b.shape[0]
    tm = _tile(T, 512)

    def body(dz_ref, w_ref, x_ref, dh1_ref, g_ref, dx_ref, dg_ref):
        i = pl.program_id(0)
        dxn = _dot(dz_ref[...], w_ref[...], NN)
        dx, dg_rows = _rms_bwd(x_ref[...], g_ref[...], dxn)
        dx_ref[...] = dh1_ref[...] + dx
        _accumulate(dg_ref, i == 0, jnp.sum(dg_rows, axis=0, keepdims=True))

    row = lambda i: (i, 0)
    const = lambda i: (0, 0)
    return _call(
        "in_proj_bwd",
        body,
        (T // tm,),
        [
            pl.BlockSpec((tm, CI), row),
            pl.BlockSpec((CI, D), const),
            pl.BlockSpec((tm, D), row),
            pl.BlockSpec((tm, D), row),
            pl.BlockSpec((1, D), const),
        ],
        [pl.BlockSpec((tm, D), row), pl.BlockSpec((1, D), const)],
        [jax.ShapeDtypeStruct((T, D), F32), jax.ShapeDtypeStruct((1, D), F32)],
        (dz_b, w_inT_b, x, dh1, g_mix),
        after=after,
    )


def _weight_grad(name, a_b, b_b, after=()):
    T, N1 = a_b.shape
    N2 = b_b.shape[1]
    t1 = _tile(N1, 1408, LANES)
    tk = _tile(T, 2048)
    nk = T // tk

    def body(a_ref, b_ref, o_ref, acc):
        k = pl.program_id(1)
        _accumulate(acc, k == 0, _dot(a_ref[...], b_ref[...], TN))

        @pl.when(k == nk - 1)
        def _():
            o_ref[...] = acc[...].astype(BF16)

    (out,), _ = _call(
        name,
        body,
        (N1 // t1, nk),
        [pl.BlockSpec((tk, t1), lambda n, k: (k, n)), pl.BlockSpec((tk, N2), lambda n, k: (k, 0))],
        [pl.BlockSpec((t1, N2), lambda n, k: (n, 0))],
        [jax.ShapeDtypeStruct((N1, N2), BF16)],
        (a_b, b_b),
        scratch=[pltpu.VMEM((t1, N2), F32)],
        after=after,
    )
    return out


def _sum_parts(name, full, how, parts, me):
    _, R, C = parts[0].shape
    tr = _tile(R, 512)
    nb = R // tr
    where = [(q, r) for q, p in enumerate(parts) for r in range(p.shape[0])]
    assert len(where) == 3

    def body(me_ref, own_ref, *refs):
        o_ref = refs[-1]
        f = lambda j: refs[where[j][0]][where[j][1]].astype(F32)
        o_ref[...] = (own_ref[...].astype(F32) + f(0)) + (f(1) + f(2))

    own_map = {"rows": lambda i, me_ref: (me_ref[0] * nb + i, 0), "all": lambda i, me_ref: (i, 0)}[how]
    return pl.pallas_call(
        body,
        name=name,
        grid_spec=pltpu.PrefetchScalarGridSpec(
            num_scalar_prefetch=1,
            grid=(nb,),
            in_specs=[pl.BlockSpec((tr, C), own_map)]
            + [pl.BlockSpec((p.shape[0], tr, C), lambda i, me_ref: (0, i, 0)) for p in parts],
            out_specs=pl.BlockSpec((tr, C), lambda i, me_ref: (i, 0)),
        ),
        out_shape=jax.ShapeDtypeStruct((R, C), F32),
        compiler_params=pltpu.CompilerParams(dimension_semantics=("arbitrary",), vmem_limit_bytes=VMEM_LIMIT),
    )(me, full, *parts)


_M_CORR = 1.0 - ADAM_B1**ADAM_STEP
_V_CORR = 1.0 - ADAM_B2**ADAM_STEP


def _adamw_math(w, g, m, v):
    m = ADAM_B1 * m + (1.0 - ADAM_B1) * g
    v = ADAM_B2 * v + (1.0 - ADAM_B2) * (g * g)
    delta = -ADAM_LR * ((m / _M_CORR) / (jnp.sqrt(v / _V_CORR) + ADAM_EPS) + ADAM_WD * w)
    return delta, m, v


def _adamw(name, w, m, v, g_here, g_there, g_transposed=False):
    R, C = w.shape
    tr = _tile(R, 256, LANES if g_transposed else 8)

    def body(w_ref, m_ref, v_ref, ga_ref, gb_ref, g_ref, d_ref, nm_ref, nv_ref):
        g = ga_ref[...] + gb_ref[...]
        if g_transposed:
            g = g.T
        g_ref[...] = g
        d_ref[...], nm_ref[...], nv_ref[...] = _adamw_math(w_ref[...], g, m_ref[...], v_ref[...])

    spec = pl.BlockSpec((tr, C), lambda i: (i, 0))
    gspec = pl.BlockSpec((C, tr), lambda i: (0, i)) if g_transposed else spec
    return _call(name, body, (R // tr,), [spec] * 3 + [gspec] * 2, [spec] * 4, [jax.ShapeDtypeStruct((R, C), F32)] * 4,
                 (w, m, v, g_here, g_there))


def _adamw_on_sparsecore(name, w, m, v, g_here, g_there, after):
    R, C = w.shape
    n_groups = R // SUBLANES
    n_turns = -(-n_groups // SC_TILES)
    n_in, n_out = 5, 4

    def body(w_hbm, m_hbm, v_hbm, ga_hbm, gb_hbm, after_hbm, g_out, d_out, nm_out, nv_out, bufs, sems):
        tile = lax.axis_index("subcore") * SC_CORES + lax.axis_index("sparsecore")
        srcs = (w_hbm, m_hbm, v_hbm, ga_hbm, gb_hbm)
        dsts = (d_out, nm_out, nv_out, g_out)

        def rows(turn):
            return pl.ds((tile + turn * SC_TILES) * SUBLANES, SUBLANES)

        def loads(turn):
            slot = turn % 2
            return [pltpu.make_async_copy(srcs[q].at[rows(turn), :], bufs.at[slot, q], sems.at[slot, q]) for q in range(n_in)]

        def stores(turn):
            slot = turn % 2
            return [pltpu.make_async_copy(bufs.at[slot, q], dsts[q].at[rows(turn), :], sems.at[slot, n_in + q])
                    for q in range(n_out)]

        def when_mine(turn, fn):
            pl.when(tile + turn * SC_TILES < n_groups)(fn)

        def compute(slot):
            wb, mb, vb, gab, gbb = (bufs.at[slot, q] for q in range(n_in))

            @pl.loop(0, SUBLANES)
            def _(r):
                @pl.loop(0, C, step=SC_LANES)
                def _(i):
                    at = (r, pl.ds(i, SC_LANES))
                    g = gab[at] + gbb[at]
                    delta, new_m, new_v = _adamw_math(wb[at], g, mb[at], vb[at])
                    gab[at], wb[at], mb[at], vb[at] = g, delta, new_m, new_v

        def start_loads(turn):
            def fn():
                for cp in loads(turn):
                    cp.start()

            when_mine(turn, fn)

        start_loads(0)
        for turn in range(n_turns):
            def step(turn=turn):
                for cp in loads(turn):
                    cp.wait()
                if turn >= 1:
                    for cp in stores(turn - 1):
                        cp.wait()
                if turn + 1 < n_turns:
                    start_loads(turn + 1)
                compute(turn % 2)
                for cp in stores(turn):
                    cp.start()

            when_mine(turn, step)
        for turn in range(n_turns):
            def drain(turn=turn):
                for cp in stores(turn):
                    cp.wait()

            last_mine = jnp.logical_and(tile + turn * SC_TILES < n_groups, tile + (turn + 1) * SC_TILES >= n_groups)
            pl.when(last_mine)(drain)

    return pl.kernel(
        body,
        name=name,
        out_type=[jax.ShapeDtypeStruct((R, C), F32)] * 4,
        mesh=plsc.VectorSubcoreMesh(core_axis_name="sparsecore", subcore_axis_name="subcore"),
        scratch_types=[pltpu.VMEM((2, n_in, SUBLANES, C), F32), pltpu.SemaphoreType.DMA((2, n_in + n_out))],
        compiler_params=pltpu.CompilerParams(use_tc_tiling_on_sc=True),
    )(w, m, v, g_here, g_there, after)


class _PackLayout:
    def __init__(self, n_cc, n_grp, G, widths):
        self.dw_rows = (0, HALO)
        self.wp_rows = (HALO, HALO + G)
        self.n_cc, self.n_grp, self.G = n_cc, n_grp, G
        self.vec = {}
        r = HALO + G
        for name, width in widths:
            self.vec[name] = (r, width)
            r += width // PACK_W
        self.rows = -(-r // 8) * 8


def _pack_small(layout, dwdw, dwp, vecs):
    names = list(vecs)

    def body(*refs):
        dw_ref, wp_ref = refs[0], refs[1]
        vec_refs = refs[2 : 2 + len(names)]
        o_ref = refs[-1]
        o_ref[...] = jnp.zeros_like(o_ref)
        for j in range(layout.n_cc):
            o_ref[layout.dw_rows[0] : layout.dw_rows[1], j * LANES : (j + 1) * LANES] = dw_ref[j]
        for i in range(layout.n_grp):
            o_ref[layout.wp_rows[0] : layout.wp_rows[1], i * layout.G : (i + 1) * layout.G] = wp_ref[i]
        for name, ref in zip(names, vec_refs):
            r, width = layout.vec[name]
            for h in range(width // PACK_W):
                o_ref[r + h : r + h + 1, :] = ref[:, h * PACK_W : (h + 1) * PACK_W]

    return pl.pallas_call(
        body,
        name="pack_small",
        out_shape=jax.ShapeDtypeStruct((layout.rows, PACK_W), F32),
    )(dwdw, dwp, *[vecs[k] for k in names])


def _adamw_small(layout, g_here, g_there, w_dw, m_dw, v_dw, w_pool, m_pool, v_pool, vec_w, vec_m, vec_v, row):
    names = list(vec_w)
    nv = len(names)

    def body(*refs):
        ga_ref, gb_ref = refs[0], refs[1]
        wdw, mdw, vdw, wp, mp, vp = refs[2:8]
        vw, vm, vv = refs[8 : 8 + nv], refs[8 + nv : 8 + 2 * nv], refs[8 + 2 * nv : 8 + 3 * nv]
        row_g, row_w, row_m, row_v = refs[8 + 3 * nv : 12 + 3 * nv]
        outs = refs[12 + 3 * nv :]
        acc = outs[-1]
        acc[...] = ga_ref[...] + gb_ref[...]

        def emit(o, g, w, m, v, idx=()):
            res = (g,) + _adamw_math(w, g, m, v)
            for ref, val in zip(o, res):
                ref[idx] = val

        me = 2 * lax.axis_index("x") + lax.axis_index("y")
        for j in range(layout.n_cc):

            @pl.when(me == j)
            def _(j=j):
                for k in range(wdw.shape[0]):
                    g = acc[layout.dw_rows[0] + k : layout.dw_rows[0] + k + 1, j * LANES : (j + 1) * LANES]
                    emit(outs[0:4], g, wdw[k], mdw[k], vdw[k], idx=k)

        for i in range(layout.n_grp):
            g = acc[layout.wp_rows[0] : layout.wp_rows[1], i * layout.G : (i + 1) * layout.G]
            emit(outs[4:8], g, wp[i], mp[i], vp[i], idx=i)
        for q, name in enumerate(names):
            r, width = layout.vec[name]
            for h in range(width // PACK_W):
                ls = slice(h * PACK_W, (h + 1) * PACK_W)
                g = acc[r + h : r + h + 1, :]
                emit(outs[8 + 4 * q : 12 + 4 * q], g, vw[q][:, ls], vm[q][:, ls], vv[q][:, ls], idx=(slice(None), ls))
        emit(outs[8 + 4 * nv : 12 + 4 * nv], row_g[...], row_w[...], row_m[...], row_v[...], idx=...)

    shapes = [w_dw.shape] * 4 + [w_pool.shape] * 4
    for name in names:
        shapes += [vec_w[name].shape] * 4
    shapes += [row[1].shape] * 4
    return pl.pallas_call(
        body,
        name="adamw_small",
        out_shape=[jax.ShapeDtypeStruct(s, F32) for s in shapes],
        scratch_shapes=[pltpu.VMEM(g_here.shape, F32)],
    )(g_here, g_there, w_dw, m_dw, v_dw, w_pool, m_pool, v_pool,
      *[vec_w[k] for k in names], *[vec_m[k] for k in names], *[vec_v[k] for k in names], *row)


def _allreduce_rows(g_part, loss_part, comm=()):
    n_pairs = N_DEV - 1

    def body(g_ref, l_ref, go_ref, lo_ref, land_g, land_l, sems):
        x, y, c = _place()
        copies = []
        for q, (src, land) in enumerate(((g_ref, land_g), (l_ref, land_l))):
            for r in range(1, N_DEV):
                fx, fy, fc = (r >> 2) & 1, (r >> 1) & 1, r & 1
                peer = (1 - x if fx else x, 1 - y if fy else y, 1 - c if fc else c)
                cp = _remote(src, land.at[r], sems, 2 * (q * n_pairs + r - 1), peer)
                cp.start()
                copies.append(cp)
        for cp in copies:
            cp.wait()

        def total(src, land):
            row = lambda r: src[...] if r == 0 else land[r]
            return ((row(0) + row(4)) + (row(2) + row(6))) + ((row(1) + row(5)) + (row(3) + row(7)))

        go_ref[...] = total(g_ref, land_g)
        lo_ref[...] = total(l_ref, land_l)

    vm = pl.BlockSpec(memory_space=pltpu.VMEM)
    return _call(
        "allreduce_rows",
        body,
        (),
        [vm] * 2,
        [vm] * 2,
        [jax.ShapeDtypeStruct(g_part.shape, F32), jax.ShapeDtypeStruct(loss_part.shape, F32)],
        (g_part, loss_part),
        scratch=[pltpu.VMEM((N_DEV,) + g_part.shape, F32), pltpu.VMEM((N_DEV,) + loss_part.shape, F32),
                 pltpu.SemaphoreType.DMA((4 * n_pairs,))],
        comm=comm,
    )


def kernel(x, g_mix, w_in, b_in, w_dw, b_dw, ln_g, ln_b, w_pool, s_pool, w_out, g_ffn, w_gate, w_up, w_down, g_final, loss_target, m_g_mix, m_w_in, m_b_in, m_w_dw, m_b_dw, m_ln_g, m_ln_b, m_w_pool, m_s_pool, m_w_out, m_g_ffn, m_w_gate, m_w_up, m_w_down, m_g_final, v_g_mix, v_w_in, v_b_in, v_w_dw, v_b_dw, v_ln_g, v_ln_b, v_w_pool, v_s_pool, v_w_out, v_g_ffn, v_w_gate, v_w_up, v_w_down, v_g_final):
    x2 = x[0]
    target = loss_target[0]
    T, D = x2.shape
    w_in2, w_out2, w_down2 = w_in[0], w_out[0], w_down[0]
    taps_first = lambda a: jnp.transpose(a, (1, 0, 2))
    w_dw3 = taps_first(w_dw)
    w_gateT, w_upT = w_gate[0].T, w_up[0].T
    CI = w_in2.shape[1] * N_CHIPS
    DM = w_out2.shape[0] * N_CHIPS
    F = w_down2.shape[0] * N_CHIPS
    KW, _, dw_cols = w_dw3.shape
    assert dw_cols == LANES
    n_grp, G = w_pool.shape[1], w_pool.shape[-1]
    w_pool3 = w_pool[0]
    g_final2 = g_final.reshape(1, D)

    me = (2 * lax.axis_index("x") + lax.axis_index("y")).astype(jnp.int32).reshape(1)

    w_inT_b, w_dw4, f_out, f_gate, f_up, f_down = _place_and_gather(
        [(w_in2, "rows", (CI, D), BF16, True, True), (w_dw3, "lead", (N_CHIPS, KW, 1, dw_cols), F32, False, False)],
        [(w, "rows", shape, BF16, False, True)
         for w, shape in ((w_out2, (DM, D)), (w_gateT, (F, D)), (w_upT, (F, D)), (w_down2, (F, D)))])
    ici = lambda f: _GatherIci([f], ["rows"], [True])
    d2d = lambda f: _GatherD2d([f], ["rows"])
    gather = _start("gather_start", [ici(f_out), ici(f_gate), ici(f_up), ici(f_down)])
    (z, xn_b), _ = _in_proj(x2, g_mix, w_inT_b, b_in, after=[gather.token])
    (f_out,) = _wait("gather_out_wait", gather, 0, xn_b)
    s_out = _start("share_out_start", [d2d(f_out)], sibling_only=True)
    (y_b, v), _ = _seq_fwd(z, w_dw4, b_dw, ln_g, ln_b, w_pool3, s_pool, after=[s_out.token])
    (w_out_b,) = _wait("share_out_wait", s_out, 0, y_b)
    (f_gate,) = _wait("gather_gate_wait", gather, 1, y_b)
    s_gate = _start("share_gate_start", [d2d(f_gate)], sibling_only=True)
    (h1, hn_b), _ = _out_proj(y_b, x2, w_out_b, g_ffn, after=[s_gate.token])
    (f_up,) = _wait("gather_up_wait", gather, 2, hn_b)
    s_up = _start("share_up_start", [d2d(f_up)], sibling_only=True)
    (wgT_b,) = _wait("share_gate_wait", s_gate, 0, hn_b)
    (wuT_b,) = _wait("share_up_wait", s_up, 0, hn_b)
    (silu_b, uds_b, a_b), _ = _gate_up(hn_b, wgT_b, wuT_b)
    (f_down,) = _wait("gather_down_wait", gather, 3, a_b)
    s_down = _start("share_down_start", [d2d(f_down)], sibling_only=True)
    (wd_b,) = _wait("share_down_wait", s_down, 0, a_b)
    (dh2, dh2_b, loss_part, d_g_final), _ = _down_loss(a_b, wd_b, h1, target, g_final2)

    gw_down = _weight_grad("grad_w_down", a_b, dh2_b)
    x_down = _start("scatter_down_start", [_Scatter([gw_down], ["rows"])])
    (dg_b, du_b), _ = _ffn_bwd_act(dh2_b, wd_b, silu_b, uds_b, after=[x_down.token])
    gw_gateT = _weight_grad("grad_w_gate", dg_b, hn_b)
    gw_upT = _weight_grad("grad_w_up", du_b, hn_b)
    gw_down, p_down = _wait("scatter_down_wait", x_down, 0, gw_upT)
    sum_down = _sum_parts("sum_w_down", gw_down, "rows", [p_down], me)
    (dh1, dh1_b, dy, d_g_ffn), (p_gate, oth_down) = _ffn_bwd_in(
        dg_b, du_b, wgT_b, wuT_b, h1, dh2, g_ffn, w_out_b, comm=[_Scatter([gw_gateT], ["rows"]), _Swap([sum_down])])
    gw_out = _weight_grad("grad_w_out", y_b, dh1_b)
    sum_gate = _sum_parts("sum_w_gate", gw_gateT, "rows", [p_gate], me)
    res = {}
    res["w_down"] = _adamw_on_sparsecore("adamw_w_down", w_down2, m_w_down[0], v_w_down[0], sum_down, oth_down, sum_down)
    (dz_b, d_wdw, d_bdw, d_lng, d_lnb, d_wp, d_sp, d_bin), (p_up, p_out, oth_gate) = _seq_bwd(
        z, dy, v, w_dw4, ln_g, ln_b, w_pool3, s_pool,
        comm=[_Scatter([gw_upT, gw_out], ["rows", "rows"]), _Swap([sum_gate])])
    res["w_gate"] = _adamw_on_sparsecore(
        "adamw_w_gate", w_gateT, m_w_gate[0].T, v_w_gate[0].T, sum_gate, oth_gate, res["w_down"][0])
    vec_grads ={"b_dw": d_bdw, "ln_g": d_lng, "ln_b": d_lnb, "s_pool": d_sp, "g_ffn": d_g_ffn, "g_final": d_g_final, "b_in": d_bin}
    layout = _PackLayout(dw_cols * N_CHIPS // LANES, n_grp, G, [(k, a.shape[1]) for k, a in vec_grads.items()])
    pack = _pack_small(layout, d_wdw, d_wp, vec_grads)
    sum_up = _sum_parts("sum_w_up", gw_upT, "rows", [p_up], me)
    sum_out = _sum_parts("sum_w_out", gw_out, "rows", [p_out], me)
    mid = _start("mid_start", [_Swap([sum_up, sum_out]), _Scatter([pack], ["all"])])
    gw_inT = _weight_grad("grad_w_in", dz_b, xn_b, after=[mid.token])
    sum_up, sum_out, oth_up, oth_out = _wait("mid_swap_wait", mid, 0, gw_inT)
    late = _start("late_start", [_Scatter([gw_inT], ["rows"])])
    (grad_x, d_g_mix), _ = _in_proj_bwd(dz_b, w_inT_b, x2, dh1, g_mix, after=[late.token])
    pack, p_small = _wait("mid_small_wait", mid, 1, d_g_mix)
    gw_inT, p_in = _wait("late_w_in_wait", late, 0, d_g_mix)
    sum_small = _sum_parts("sum_small", pack, "all", [p_small], me)
    res["w_up"] = _adamw_on_sparsecore("adamw_w_up", w_upT, m_w_up[0].T, v_w_up[0].T, sum_up, oth_up, res["w_gate"][0])
    res["w_out"] = _adamw_on_sparsecore("adamw_w_out", w_out2, m_w_out[0], v_w_out[0], sum_out, oth_out, res["w_gate"][0])
    sum_in = _sum_parts("sum_w_in", gw_inT, "rows", [p_in], me)
    (d_g_mix, loss_row), (oth_in, oth_small) = _allreduce_rows(d_g_mix, loss_part, comm=[_Swap([sum_in, sum_small])])
    loss = loss_row[0, 0]
    res["w_in"], _ = _adamw("adamw_w_in", w_in2, m_w_in[0], v_w_in[0], sum_in, oth_in, g_transposed=True)

    vec_w = {"b_dw": b_dw, "ln_g": ln_g, "ln_b": ln_b, "s_pool": s_pool, "g_ffn": g_ffn, "g_final": g_final2, "b_in": b_in}
    vec_m = {"b_dw": m_b_dw, "ln_g": m_ln_g, "ln_b": m_ln_b, "s_pool": m_s_pool, "g_ffn": m_g_ffn,
             "g_final": m_g_final.reshape(1, D), "b_in": m_b_in}
    vec_v = {"b_dw": v_b_dw, "ln_g": v_ln_g, "ln_b": v_ln_b, "s_pool": v_s_pool, "g_ffn": v_g_ffn,
             "g_final": v_g_final.reshape(1, D), "b_in": v_b_in}
    small = _adamw_small(layout, sum_small, oth_small, w_dw3, taps_first(m_w_dw), taps_first(v_w_dw),
                         w_pool3, m_w_pool[0], v_w_pool[0], vec_w, vec_m, vec_v, (d_g_mix, g_mix, m_g_mix, v_g_mix))
    res["w_dw"] = [taps_first(a) for a in small[0:4]]
    res["w_pool"] = [a[None] for a in small[4:8]]
    for q, k in enumerate(vec_w):
        res[k] = list(small[8 + 4 * q : 12 + 4 * q])
    res["g_mix"] = list(small[-4:])
    res["g_final"] = [a.reshape(D) for a in res["g_final"]]
    for k in ("w_in", "w_out", "w_down"):
        res[k] = [a[None] for a in res[k]]
    for k in ("w_gate", "w_up"):
        res[k] = [a.T[None] for a in res[k]]

    order = ["g_mix", "w_in", "b_in", "w_dw", "b_dw", "ln_g", "ln_b", "w_pool", "s_pool", "w_out", "g_ffn", "w_gate", "w_up", "w_down", "g_final"]
    outs = [loss, grad_x[None]]
    for q in range(4):
        outs += [res[k][q] for k in order]
    return tuple(outs)
```

```python
import jax
import jax.numpy as jnp
from jax import lax
from jax.experimental import pallas as pl
from jax.experimental.pallas import tpu as pltpu
from jax.experimental.pallas import tpu_sc as plsc

F32 = jnp.float32
BF16 = jnp.bfloat16
MESH = pl.DeviceIdType.MESH
ANY = pl.BlockSpec(memory_space=pl.ANY)

RMS_EPS = 1e-6
LN_EPS = 1e-5
POOL_WINDOWS = (2, 4, 8, 16)
ADAM_LR = 0.001
ADAM_B1 = 0.9
ADAM_B2 = 0.999
ADAM_EPS = 1e-08
ADAM_WD = 0.01
ADAM_STEP = 10

LANES = 128
SUBLANES = 8
BF16_ROWS = 16
HALO = 32
CONV_ROWS = 64
HIDDEN_CHUNK = 512
VMEM_LIMIT = 56 * 1024 * 1024
PACK_W = 512
N_CHIPS = 4
N_DEV = 8
SIBLING_BARRIER_ID = 0
SC_CORES = 2
SC_TILES = 32
SC_LANES = 16


def _tile(n, want, mult=8):
    t = min(n, want)
    while n % t or t % mult:
        t -= 1
    return t


def _sigmoid(x):
    return 1.0 / (1.0 + jnp.exp(-x))


def _dot(a, b, dims):
    return lax.dot_general(a, b, (dims, ((), ())), preferred_element_type=F32)


NN = ((1,), (0,))
NT = ((1,), (1,))
TN = ((0,), (0,))


def _rms_bwd(x, g, dy):
    r = lax.rsqrt(jnp.mean(x * x, axis=-1, keepdims=True) + RMS_EPS)
    xh = x * r
    gy = dy * g
    dx = r * (gy - xh * jnp.mean(gy * xh, axis=-1, keepdims=True))
    return dx, dy * xh


def _accumulate(ref, first, val):
    @pl.when(first)
    def _():
        ref[...] = val

    @pl.when(jnp.logical_not(first))
    def _():
        ref[...] += val


def _place():
    return lax.axis_index("x"), lax.axis_index("y"), lax.axis_index("c")


def _other_chips(x, y):
    return [(1 - x, y), (x, 1 - y), (1 - x, 1 - y)]


def _rows(ref, start, n):
    return ref.at[pl.ds(pl.multiple_of(start, BF16_ROWS), n)]


def _window(ref, how, k, c=None):
    if how == "all":
        return ref
    if how == "lead":
        return ref.at[k]
    assert how == "rows"
    n = ref.shape[0] // N_CHIPS
    if c is None:
        return _rows(ref, k * n, n)
    return _rows(ref, k * n + c * (n // 2), n // 2)


def _remote(src, dst, sems, s, device):
    return pltpu.make_async_remote_copy(
        src_ref=src, dst_ref=dst, send_sem=sems.at[s], recv_sem=sems.at[s + 1], device_id=device, device_id_type=MESH)


class _GatherIci:
    aliased = True

    def __init__(self, fulls, hows, splits):
        self.fulls, self.hows, self.splits = list(fulls), list(hows), list(splits)

    def inputs(self):
        return self.fulls

    def out_shapes(self):
        return [jax.ShapeDtypeStruct(a.shape, a.dtype) for a in self.fulls]

    def n_sems(self):
        return 6 * len(self.fulls)

    def build(self, ins, outs, sems, base):
        x, y, c = _place()
        me = 2 * x + y
        chips = _other_chips(x, y)
        starts, waits = [], []
        for a, (how, sp) in enumerate(zip(self.hows, self.splits)):
            half = c if sp else None
            mine = _window(outs[a], how, me, half)
            for j, (px, py) in enumerate(chips):
                s = base + 6 * a + 2 * j
                cp = _remote(mine, mine, sems, s, (px, py, c))
                landing = _remote(mine, _window(outs[a], how, 2 * px + py, half), sems, s, (px, py, c))
                starts.append(cp.start)
                waits += [landing.wait_recv, cp.wait_send]
        return starts, waits


class _GatherD2d:
    aliased = True

    def __init__(self, fulls, hows):
        self.fulls, self.hows = list(fulls), list(hows)

    def inputs(self):
        return self.fulls

    def out_shapes(self):
        return [jax.ShapeDtypeStruct(a.shape, a.dtype) for a in self.fulls]

    def n_sems(self):
        return 6 * len(self.fulls)

    def build(self, ins, outs, sems, base):
        x, y, c = _place()
        starts, waits = [], []
        for a, how in enumerate(self.hows):
            for j, (px, py) in enumerate(_other_chips(x, y)):
                s = base + 6 * a + 2 * j
                got = _window(outs[a], how, 2 * px + py, c)
                cp = _remote(got, got, sems, s, (x, y, 1 - c))
                landing = _remote(got, _window(outs[a], how, 2 * px + py, 1 - c), sems, s, (x, y, 1 - c))
                starts.append(cp.start)
                waits += [landing.wait_recv, cp.wait_send]
        return starts, waits


def _part_shape(a, how):
    if how == "all":
        return a.shape
    assert how == "rows"
    return (a.shape[0] // N_CHIPS, a.shape[1])


class _Scatter:
    aliased = False

    def __init__(self, fulls, hows):
        self.fulls, self.hows = list(fulls), list(hows)

    def inputs(self):
        return self.fulls

    def out_shapes(self):
        return [jax.ShapeDtypeStruct((N_CHIPS - 1,) + _part_shape(a, h), a.dtype) for a, h in zip(self.fulls, self.hows)]

    def n_sems(self):
        return 6 * len(self.fulls)

    def build(self, ins, outs, sems, base):
        x, y, c = _place()
        chips = _other_chips(x, y)
        starts, waits = [], []
        for a, how in enumerate(self.hows):
            for j, (px, py) in enumerate(chips):
                cp = _remote(_window(ins[a], how, 2 * px + py), outs[a].at[j], sems, base + 6 * a + 2 * j, (px, py, c))
                starts.append(cp.start)
                waits += [cp.wait_recv, cp.wait_send]
        return starts, waits


class _Swap:
    aliased = False

    def __init__(self, arrays):
        self.arrays = list(arrays)

    def inputs(self):
        return self.arrays

    def out_shapes(self):
        return [jax.ShapeDtypeStruct(a.shape, a.dtype) for a in self.arrays]

    def n_sems(self):
        return 2 * len(self.arrays)

    def build(self, ins, outs, sems, base):
        x, y, c = _place()
        starts, waits = [], []
        for a in range(len(ins)):
            cp = _remote(ins[a], outs[a], sems, base + 2 * a, (x, y, 1 - c))
            starts.append(cp.start)
            waits += [cp.wait_recv, cp.wait_send]
        return starts, waits


def _call(name, body, grid, in_specs, out_specs, out_shape, args, scratch=(), comm=(), after=()):
    comm, after = list(comm), list(after)
    n_in, n_out, n_scr, n_after = len(args), len(out_shape), len(scratch), len(after)
    c_in = [a for op in comm for a in op.inputs()]
    c_out = [s for op in comm for s in op.out_shapes()]
    n_sems = sum(op.n_sems() for op in comm)
    aliases, i_in, i_out = {}, 0, 0
    for op in comm:
        if op.aliased:
            for q in range(len(op.inputs())):
                aliases[n_in + n_after + i_in + q] = n_out + i_out + q
        i_in, i_out = i_in + len(op.inputs()), i_out + len(op.out_shapes())

    def wrapped(*refs):
        ins = refs[:n_in]
        cin = refs[n_in + n_after : n_in + n_after + len(c_in)]
        o0 = n_in + n_after + len(c_in)
        outs = refs[o0 : o0 + n_out]
        cout = refs[o0 + n_out : o0 + n_out + len(c_out)]
        s0 = o0 + n_out + len(c_out)
        scr = refs[s0 : s0 + n_scr]

        def copies():
            sems = refs[s0 + n_scr]
            starts, waits = [], []
            i_in = i_out = base = 0
            for op in comm:
                ni, no = len(op.inputs()), len(op.out_shapes())
                s, w = op.build(cin[i_in : i_in + ni], cout[i_out : i_out + no], sems, base)
                starts += s
                waits += w
                i_in, i_out, base = i_in + ni, i_out + no, base + op.n_sems()
            return starts, waits

        def run_starts():
            for start in copies()[0]:
                start()

        def run_waits():
            for wait in copies()[1]:
                wait()

        if comm and grid:
            first = last = True
            for d, n in enumerate(grid):
                first = jnp.logical_and(first, pl.program_id(d) == 0)
                last = jnp.logical_and(last, pl.program_id(d) == n - 1)
            pl.when(first)(run_starts)
        elif comm:
            run_starts()
        if body is not None:
            body(*ins, *outs, *scr)
        if comm and grid:
            pl.when(last)(run_waits)
        elif comm:
            run_waits()

    res = pl.pallas_call(
        wrapped,
        name=name,
        grid=grid,
        in_specs=list(in_specs) + [ANY] * (n_after + len(c_in)),
        out_specs=list(out_specs) + [ANY] * len(c_out),
        out_shape=list(out_shape) + c_out,
        scratch_shapes=list(scratch) + ([pltpu.SemaphoreType.DMA((n_sems,))] if comm else []),
        input_output_aliases=aliases,
        compiler_params=pltpu.CompilerParams(dimension_semantics=("arbitrary",) * len(grid), vmem_limit_bytes=VMEM_LIMIT),
    )(*args, *after, *c_in)
    return tuple(res[:n_out]), tuple(res[n_out:])


def _place_and_gather(now, later):
    items = list(now) + list(later)
    n, n_now = len(items), len(now)
    buf_shape = lambda it: it[0].shape[::-1] if it[4] else it[0].shape
    split_now = [a for a in range(n_now) if items[a][5]]

    def body(*refs):
        ins, outs = refs[:n], refs[n : 2 * n]
        stage, bufs = refs[2 * n : 3 * n - n_now], refs[3 * n - n_now : 4 * n - n_now]
        sems = refs[4 * n - n_now]
        x, y, c = _place()
        me = 2 * x + y
        chips = _other_chips(x, y)
        loads = [pltpu.make_async_copy(ins[a], stage[a - n_now], sems.at[a]) for a in range(n_now, n)]
        for ld in loads:
            ld.start()
        pending = []

        def place(a, val):
            _, how, _, dtype, transposed, _ = items[a]
            bufs[a][...] = (val.T if transposed else val).astype(dtype)
            cp = pltpu.make_async_copy(bufs[a], _window(outs[a], how, me), sems.at[n + a])
            cp.start()
            pending.append(cp.wait)

        arrivals = []
        for a in range(n_now):
            place(a, ins[a][...])
            how, split = items[a][1], items[a][5]
            half = c if split else None
            src = _rows(bufs[a], c * (bufs[a].shape[0] // 2), bufs[a].shape[0] // 2) if split else bufs[a]
            for j, (px, py) in enumerate(chips):
                s = 2 * n + 6 * a + 2 * j
                cp = _remote(src, _window(outs[a], how, me, half), sems, s, (px, py, c))
                landing = _remote(src, _window(outs[a], how, 2 * px + py, half), sems, s, (px, py, c))
                cp.start()
                arrivals.append(landing.wait_recv)
                pending.append(cp.wait_send)
        for a in range(n_now, n):
            loads[a - n_now].wait()
            place(a, stage[a - n_now][...])
        for wait in arrivals:
            wait()
        d2d = _GatherD2d([None] * len(split_now), [items[a][1] for a in split_now])
        starts, waits = d2d.build(None, [outs[a] for a in split_now], sems, 2 * n + 6 * n_now)
        for start in starts:
            start()
        for wait in waits + pending:
            wait()

    vm = pl.BlockSpec(memory_space=pltpu.VMEM)
    return pl.pallas_call(
        body,
        name="place_and_gather",
        in_specs=[vm] * n_now + [ANY] * (n - n_now),
        out_specs=[ANY] * n,
        out_shape=[jax.ShapeDtypeStruct(it[2], it[3]) for it in items],
        scratch_shapes=[pltpu.VMEM(it[0].shape, it[0].dtype) for it in later]
        + [pltpu.VMEM(buf_shape(it), it[3]) for it in items]
        + [pltpu.SemaphoreType.DMA((2 * n + 6 * n_now + 6 * len(split_now),))],
        compiler_params=pltpu.CompilerParams(vmem_limit_bytes=VMEM_LIMIT),
    )(*[it[0] for it in items])


_HBM = pl.BlockSpec(memory_space=pltpu.HBM)
_SEM = pl.BlockSpec(memory_space=pltpu.SEMAPHORE)
_DATAFLOW = pltpu.SideEffectType.DATAFLOW_SIDE_EFFECTING


class _Pending:
    def __init__(self, ops, bases, sems, arrays, token):
        self.ops, self.bases, self.sems, self.arrays, self.token = ops, bases, sems, arrays, token


def _op_refs(op, refs):
    n_src = len(op.inputs())
    return refs[:n_src], (refs[:n_src] if op.aliased else refs[n_src:])


def _start(name, ops, sibling_only=False):
    per_op = [list(op.inputs()) + ([] if op.aliased else [lax.empty(sd.shape, sd.dtype) for sd in op.out_shapes()])
              for op in ops]
    arrays = [a for group in per_op for a in group]
    bases = [sum(op.n_sems() for op in ops[:k]) for k in range(len(ops))]
    n = len(arrays)

    def body(*refs):
        sems, token = refs[n], refs[-1]
        if sibling_only:
            x, y, c = _place()
            barrier = pltpu.get_barrier_semaphore()
            pl.semaphore_signal(barrier, inc=1, device_id=(x, y, 1 - c), device_id_type=MESH)
            pl.semaphore_wait(barrier, 1)
        at = 0
        for op, group, base in zip(ops, per_op, bases):
            starts, _ = op.build(*_op_refs(op, refs[at : at + len(group)]), sems, base)
            for start in starts:
                start()
            at += len(group)
        token[...] = jnp.zeros_like(token)

    res = pl.pallas_call(
        body,
        name=name,
        out_shape=(pltpu.SemaphoreType.DMA((sum(op.n_sems() for op in ops),)),)
        + tuple(pltpu.HBM(a.shape, a.dtype) for a in arrays) + (jax.ShapeDtypeStruct((SUBLANES, LANES), F32),),
        in_specs=(_HBM,) * n,
        out_specs=(_SEM,) + (_HBM,) * n + (pl.BlockSpec(memory_space=pltpu.VMEM),),
        input_output_aliases={i: 1 + i for i in range(n)},
        compiler_params=pltpu.CompilerParams(
            has_side_effects=_DATAFLOW, collective_id=SIBLING_BARRIER_ID if sibling_only else None),
    )(*[pltpu.with_memory_space_constraint(a, pltpu.HBM) for a in arrays])
    thru, at, groups = list(res[1 : 1 + n]), 0, []
    for group in per_op:
        groups.append(thru[at : at + len(group)])
        at += len(group)
    return _Pending(list(ops), bases, res[0], groups, res[-1])


def _wait(name, pending, k, after):
    op, arrays = pending.ops[k], pending.arrays[k]
    n = len(arrays)

    def body(*refs):
        _, waits = op.build(*_op_refs(op, refs[:n]), refs[n], pending.bases[k])
        for wait in waits:
            wait()

    return pl.pallas_call(
        body,
        name=name,
        out_shape=tuple(pltpu.HBM(a.shape, a.dtype) for a in arrays),
        in_specs=(_HBM,) * n + (_SEM, ANY),
        out_specs=(_HBM,) * n,
        input_output_aliases={i: i for i in range(n)},
        compiler_params=pltpu.CompilerParams(has_side_effects=_DATAFLOW),
    )(*arrays, pending.sems, after)


def _in_proj(x, g_mix, w_inT_b, b_in, after=()):
    T, D = x.shape
    CI = w_inT_b.shape[0]
    tm = _tile(T, 512)

    def body(x_ref, g_ref, w_ref, b_ref, z_ref, xn_ref):
        xv = x_ref[...]
        r = lax.rsqrt(jnp.mean(xv * xv, axis=-1, keepdims=True) + RMS_EPS)
        xn = (xv * r * g_ref[...]).astype(BF16)
        xn_ref[...] = xn
        z_ref[...] = _dot(xn, w_ref[...], NT) + b_ref[...]

    return _call(
        "in_proj",
        body,
        (T // tm,),
        [
            pl.BlockSpec((tm, D), lambda i: (i, 0)),
            pl.BlockSpec((1, D), lambda i: (0, 0)),
            pl.BlockSpec((CI, D), lambda i: (0, 0)),
            pl.BlockSpec((1, CI), lambda i: (0, 0)),
        ],
        [pl.BlockSpec((tm, CI), lambda i: (i, 0)), pl.BlockSpec((tm, D), lambda i: (i, 0))],
        [jax.ShapeDtypeStruct((T, CI), F32), jax.ShapeDtypeStruct((T, D), BF16)],
        (x, g_mix, w_inT_b, b_in),
        after=after,
    )


def _fill_shifted(scr):
    n = scr.shape[1] - SUBLANES
    for s in range(1, SUBLANES):
        scr[s, 0:n, :] = scr[0, s : s + n, :]


def _shifted_rows(scr, off, n, cs):
    s = off % SUBLANES
    return scr[s, off - s : off - s + n, cs]


def _pool_mean_minus_token(p_scr, cs, w, cnt, tt):
    tok = p_scr[HALO : HALO + tt, cs]
    s = tok
    for d in range(1, w):
        s = s + p_scr[HALO - d : HALO - d + tt, cs]
    return s / cnt - tok


def _seq_fwd(z, w_dw4, b_dw, ln_g, ln_b, w_pool, s_pool, after=()):
    T, CI = z.shape
    CC = ln_g.shape[1]
    n_grp, G = w_pool.shape[0], w_pool.shape[-1]
    KW = w_dw4.shape[1]
    D = CC + n_grp * G
    tt = _tile(T, 512, HALO)
    per = tt // HALO

    def body(zc_ref, zp_ref, wdw_ref, bdw_ref, lng_ref, lnb_ref, wp_ref, sp_ref, y_ref, v_ref, u_scr, p_scr):
        i = pl.program_id(0)
        first = i == 0
        u_prev = zp_ref[:, 0:CC] * _sigmoid(zp_ref[:, CC : 2 * CC])
        u_scr[0, 0:HALO, :] = jnp.where(first, 0.0, u_prev)
        p_scr[0:HALO, :] = jnp.where(first, 0.0, zp_ref[:, 2 * CC :])
        u_scr[0, HALO:, :] = zc_ref[:, 0:CC] * _sigmoid(zc_ref[:, CC : 2 * CC])
        p_scr[HALO:, :] = zc_ref[:, 2 * CC :]
        _fill_shifted(u_scr)

        for j in range(CC // LANES):
            cs = slice(LANES * j, LANES * (j + 1))
            for rb in range(tt // CONV_ROWS):
                acc = jnp.zeros((CONV_ROWS, LANES), F32)
                for k in range(KW):
                    off = HALO - (KW - 1) + k + rb * CONV_ROWS
                    acc = acc + _shifted_rows(u_scr, off, CONV_ROWS, cs) * wdw_ref[j, k]
                v_ref[rb * CONV_ROWS : (rb + 1) * CONV_ROWS, cs] = acc + bdw_ref[:, cs]

        v = v_ref[...]
        mu = jnp.mean(v, axis=-1, keepdims=True)
        d = v - mu
        var = jnp.mean(d * d, axis=-1, keepdims=True)
        ln = d * lax.rsqrt(var + LN_EPS) * lng_ref[...] + lnb_ref[...]
        y_ref[:, 0:CC] = (ln * _sigmoid(ln)).astype(BF16)

        tpos = i * tt + lax.broadcasted_iota(jnp.int32, (tt, 1), 0)
        for gi, w in enumerate(POOL_WINDOWS):
            cs = slice(G * gi, G * (gi + 1))
            cnt = jnp.minimum(tpos + 1, w).astype(F32)
            yi = _pool_mean_minus_token(p_scr, cs, w, cnt, tt)
            q = _dot(yi.astype(BF16), wp_ref[gi].astype(BF16), NN)
            y_ref[:, CC + G * gi : CC + G * (gi + 1)] = (q * sp_ref[:, cs]).astype(BF16)

    const2 = lambda i: (0, 0)
    return _call(
        "seq_fwd",
        body,
        (T // tt,),
        [
            pl.BlockSpec((tt, CI), lambda i: (i, 0)),
            pl.BlockSpec((HALO, CI), lambda i: (jnp.maximum(i * per - 1, 0), 0)),
            pl.BlockSpec(w_dw4.shape, lambda i: (0,) * w_dw4.ndim),
            pl.BlockSpec((1, CC), const2),
            pl.BlockSpec((1, CC), const2),
            pl.BlockSpec((1, CC), const2),
            pl.BlockSpec(w_pool.shape, lambda i: (0, 0, 0)),
            pl.BlockSpec((1, n_grp * G), const2),
        ],
        [pl.BlockSpec((tt, D), lambda i: (i, 0)), pl.BlockSpec((tt, CC), lambda i: (i, 0))],
        [jax.ShapeDtypeStruct((T, D), BF16), jax.ShapeDtypeStruct((T, CC), F32)],
        (z, z, w_dw4, b_dw, ln_g, ln_b, w_pool, s_pool),
        scratch=[pltpu.VMEM((SUBLANES, HALO + tt, CC), F32), pltpu.VMEM((HALO + tt, n_grp * G), F32)],
        after=after,
    )


def _out_proj(y_b, x, w_out_b, g_ffn, after=()):
    T, D = x.shape
    tm = _tile(T, 512)

    def body(y_ref, x_ref, w_ref, g_ref, h1_ref, hn_ref):
        h1 = x_ref[...] + _dot(y_ref[...], w_ref[...], NN)
        h1_ref[...] = h1
        r = lax.rsqrt(jnp.mean(h1 * h1, axis=-1, keepdims=True) + RMS_EPS)
        hn_ref[...] = (h1 * r * g_ref[...]).astype(BF16)

    row = lambda i: (i, 0)
    return _call(
        "out_proj",
        body,
        (T // tm,),
        [
            pl.BlockSpec((tm, y_b.shape[1]), row),
            pl.BlockSpec((tm, D), row),
            pl.BlockSpec(w_out_b.shape, lambda i: (0, 0)),
            pl.BlockSpec((1, D), lambda i: (0, 0)),
        ],
        [pl.BlockSpec((tm, D), row), pl.BlockSpec((tm, D), row)],
        [jax.ShapeDtypeStruct((T, D), F32), jax.ShapeDtypeStruct((T, D), BF16)],
        (y_b, x, w_out_b, g_ffn),
        after=after,
    )


def _hidden_tile(F):
    return _tile(F, 1408, LANES)


def _gate_up(hn_b, wgT_b, wuT_b):
    T, D = hn_b.shape
    F = wgT_b.shape[0]
    tm, tf = _tile(T, 1024), _hidden_tile(F)

    def body(hn_ref, wg_ref, wu_ref, silu_ref, uds_ref, a_ref):
        hn = hn_ref[...]
        for c0 in range(0, tf, HIDDEN_CHUNK):
            cs = slice(c0, min(c0 + HIDDEN_CHUNK, tf))
            gv = _dot(hn, wg_ref[cs, :], NT)
            uv = _dot(hn, wu_ref[cs, :], NT)
            sg = _sigmoid(gv)
            silu = gv * sg
            silu_ref[:, cs] = silu.astype(BF16)
            uds_ref[:, cs] = (uv * (sg * (1.0 + gv * (1.0 - sg)))).astype(BF16)
            a_ref[:, cs] = (silu * uv).astype(BF16)

    wspec = pl.BlockSpec((tf, D), lambda j, i: (j, 0))
    ospec = pl.BlockSpec((tm, tf), lambda j, i: (i, j))
    return _call(
        "gate_up",
        body,
        (F // tf, T // tm),
        [pl.BlockSpec((tm, D), lambda j, i: (i, 0)), wspec, wspec],
        [ospec, ospec, ospec],
        [jax.ShapeDtypeStruct((T, F), BF16)] * 3,
        (hn_b, wgT_b, wuT_b),
    )


def _down_loss(a_b, wd_b, h1, target, g_final):
    T, D = h1.shape
    F = a_b.shape[1]
    tm = _tile(T, 512)
    nt = T // tm

    def body(a_ref, w_ref, h1_ref, t_ref, g_ref, dh2_ref, dh2b_ref, loss_ref, dg_ref):
        i = pl.program_id(0)
        h2 = h1_ref[...] + _dot(a_ref[...], w_ref[...], NN)
        r = lax.rsqrt(jnp.mean(h2 * h2, axis=-1, keepdims=True) + RMS_EPS)
        g = g_ref[...]
        diff = h2 * r * g - t_ref[...]
        _accumulate(loss_ref, i == 0, jnp.full(loss_ref.shape, jnp.sum(diff * diff) * (0.5 / D), F32))
        dh2, dg_rows = _rms_bwd(h2, g, diff * (1.0 / D))
        dh2_ref[...] = dh2
        dh2b_ref[...] = dh2.astype(BF16)
        _accumulate(dg_ref, i == 0, jnp.sum(dg_rows, axis=0, keepdims=True))

    row = lambda i: (i, 0)
    return _call(
        "down_loss",
        body,
        (nt,),
        [
            pl.BlockSpec((tm, F), row),
            pl.BlockSpec((F, D), lambda i: (0, 0), pipeline_mode=pl.Buffered(1)),
            pl.BlockSpec((tm, D), row),
            pl.BlockSpec((tm, D), row),
            pl.BlockSpec((1, D), lambda i: (0, 0)),
        ],
        [
            pl.BlockSpec((tm, D), row),
            pl.BlockSpec((tm, D), row),
            pl.BlockSpec((1, LANES), lambda i: (0, 0)),
            pl.BlockSpec((1, D), lambda i: (0, 0)),
        ],
        [
            jax.ShapeDtypeStruct((T, D), F32),
            jax.ShapeDtypeStruct((T, D), BF16),
            jax.ShapeDtypeStruct((1, LANES), F32),
            jax.ShapeDtypeStruct((1, D), F32),
        ],
        (a_b, wd_b, h1, target, g_final),
    )


def _ffn_bwd_act(dh2_b, wd_b, silu_b, uds_b, after=()):
    T, D = dh2_b.shape
    F = wd_b.shape[0]
    tm, tf = _tile(T, 1024), _hidden_tile(F)

    def body(d_ref, w_ref, silu_ref, uds_ref, dg_ref, du_ref):
        d = d_ref[...]
        for c0 in range(0, tf, HIDDEN_CHUNK):
            cs = slice(c0, min(c0 + HIDDEN_CHUNK, tf))
            da = _dot(d, w_ref[cs, :], NT)
            dg_ref[:, cs] = (da * uds_ref[:, cs].astype(F32)).astype(BF16)
            du_ref[:, cs] = (da * silu_ref[:, cs].astype(F32)).astype(BF16)

    aspec = pl.BlockSpec((tm, tf), lambda j, i: (i, j))
    return _call(
        "ffn_bwd_act",
        body,
        (F // tf, T // tm),
        [pl.BlockSpec((tm, D), lambda j, i: (i, 0)), pl.BlockSpec((tf, D), lambda j, i: (j, 0)), aspec, aspec],
        [aspec, aspec],
        [jax.ShapeDtypeStruct((T, F), BF16)] * 2,
        (dh2_b, wd_b, silu_b, uds_b),
        after=after,
    )


def _ffn_bwd_in(dg_b, du_b, wgT_b, wuT_b, h1, dh2, g_ffn, w_out_b, comm=()):
    T, D = h1.shape
    F = wgT_b.shape[0]
    DM = w_out_b.shape[0]
    tm = _tile(T, 512)

    def body(dg_ref, du_ref, wg_ref, wu_ref, h1_ref, dh2_ref, g_ref, wo_ref, dh1_ref, dh1b_ref, dy_ref, dgf_ref):
        i = pl.program_id(0)
        dhn = _dot(dg_ref[...], wg_ref[...], NN) + _dot(du_ref[...], wu_ref[...], NN)
        dx, dg_rows = _rms_bwd(h1_ref[...], g_ref[...], dhn)
        dh1 = dh2_ref[...] + dx
        dh1b = dh1.astype(BF16)
        dh1_ref[...] = dh1
        dh1b_ref[...] = dh1b
        dy_ref[...] = _dot(dh1b, wo_ref[...], NT)
        _accumulate(dgf_ref, i == 0, jnp.sum(dg_rows, axis=0, keepdims=True))

    row = lambda i: (i, 0)
    const = lambda i: (0, 0)
    return _call(
        "ffn_bwd_in",
        body,
        (T // tm,),
        [
            pl.BlockSpec((tm, F), row),
            pl.BlockSpec((tm, F), row),
            pl.BlockSpec((F, D), const, pipeline_mode=pl.Buffered(1)),
            pl.BlockSpec((F, D), const, pipeline_mode=pl.Buffered(1)),
            pl.BlockSpec((tm, D), row),
            pl.BlockSpec((tm, D), row),
            pl.BlockSpec((1, D), const),
            pl.BlockSpec((DM, D), const, pipeline_mode=pl.Buffered(1)),
        ],
        [pl.BlockSpec((tm, D), row), pl.BlockSpec((tm, D), row), pl.BlockSpec((tm, DM), row), pl.BlockSpec((1, D), const)],
        [
            jax.ShapeDtypeStruct((T, D), F32),
            jax.ShapeDtypeStruct((T, D), BF16),
            jax.ShapeDtypeStruct((T, DM), F32),
            jax.ShapeDtypeStruct((1, D), F32),
        ],
        (dg_b, du_b, wgT_b, wuT_b, h1, dh2, g_ffn, w_out_b),
        comm=comm,
    )


def _seq_bwd(z, dy, v, w_dw4, ln_g, ln_b, w_pool, s_pool, comm=()):
    T, CI = z.shape
    CC = ln_g.shape[1]
    n_grp, G = w_pool.shape[0], w_pool.shape[-1]
    CP = n_grp * G
    KW = w_dw4.shape[1]
    n_cc = CC // LANES
    D = CC + CP
    tt = _tile(T, 512, HALO)
    per = tt // HALO
    n_tiles = T // tt
    last_halo = T // HALO - 1

    def body(zc_ref, zp_ref, dyc_ref, dyn_ref, vc_ref, vn_ref, wdw_ref, lng_ref, lnb_ref, wp_ref, sp_ref,
             dz_ref, dwdw_ref, dbdw_ref, dlng_ref, dlnb_ref, dwp_ref, dsp_ref, dbin_ref,
             dv_scr, u_scr, p_scr, g_scr, dw_scr):
        i = pl.program_id(0)
        first = i == 0
        last = i == n_tiles - 1
        lng, lnb = lng_ref[...], lnb_ref[...]

        def conv_pre(vv, dyc):
            mu = jnp.mean(vv, axis=-1, keepdims=True)
            d = vv - mu
            rs = lax.rsqrt(jnp.mean(d * d, axis=-1, keepdims=True) + LN_EPS)
            xh = d * rs
            ln = xh * lng + lnb
            sg = _sigmoid(ln)
            dln = dyc * (sg * (1.0 + ln * (1.0 - sg)))
            dxh = dln * lng
            dv = rs * (dxh - jnp.mean(dxh, axis=-1, keepdims=True) - xh * jnp.mean(dxh * xh, axis=-1, keepdims=True))
            return dv, dln, xh

        dv_c, dln_c, xh_c = conv_pre(vc_ref[...], dyc_ref[:, 0:CC])
        dv_scr[0, 0:tt, :] = dv_c
        dv_n, _, _ = conv_pre(vn_ref[...], dyn_ref[:, 0:CC])
        dv_scr[0, tt:, :] = jnp.where(last, 0.0, dv_n)
        _fill_shifted(dv_scr)
        _accumulate(dlng_ref, first, jnp.sum(dln_c * xh_c, axis=0, keepdims=True))
        _accumulate(dlnb_ref, first, jnp.sum(dln_c, axis=0, keepdims=True))
        _accumulate(dbdw_ref, first, jnp.sum(dv_c, axis=0, keepdims=True))

        u_scr[...] = zc_ref[:, 0:CC] * _sigmoid(zc_ref[:, CC : 2 * CC])

        @pl.when(first)
        def _():
            dw_scr[...] = jnp.zeros_like(dw_scr)

        for j in range(n_cc):
            cs = slice(LANES * j, LANES * (j + 1))
            gs = slice(CC + LANES * j, CC + LANES * (j + 1))
            dbin_a = jnp.zeros((1, LANES), F32)
            dbin_g = jnp.zeros((1, LANES), F32)
            for rb in range(tt // CONV_ROWS):
                rows = slice(rb * CONV_ROWS, (rb + 1) * CONV_ROWS)
                u_blk = u_scr[rows, cs]
                du = jnp.zeros((CONV_ROWS, LANES), F32)
                for k in range(KW):
                    off = rb * CONV_ROWS + (KW - 1) - k
                    d = _shifted_rows(dv_scr, off, CONV_ROWS, cs)
                    du = du + d * wdw_ref[j, k]
                    dw_scr[j * HALO + k] += jnp.sum((u_blk * d).reshape(CONV_ROWS // 8, 8, LANES), axis=0)
                a = zc_ref[rows, cs]
                sg = _sigmoid(zc_ref[rows, gs])
                da = du * sg
                dgate = du * a * sg * (1.0 - sg)
                dz_ref[rows, cs] = da.astype(BF16)
                dz_ref[rows, gs] = dgate.astype(BF16)
                dbin_a = dbin_a + jnp.sum(da, axis=0, keepdims=True)
                dbin_g = dbin_g + jnp.sum(dgate, axis=0, keepdims=True)
            _accumulate(dbin_ref.at[:, cs], first, dbin_a)
            _accumulate(dbin_ref.at[:, gs], first, dbin_g)

        @pl.when(last)
        def _():
            dwdw_ref[...] = jnp.sum(dw_scr[...], axis=1).reshape(dwdw_ref.shape)

        p_scr[0:HALO, :] = jnp.where(first, 0.0, zp_ref[:, 2 * CC :])
        p_scr[HALO:, :] = zc_ref[:, 2 * CC :]
        tpos = i * tt + lax.broadcasted_iota(jnp.int32, (tt, 1), 0)
        for gi, w in enumerate(POOL_WINDOWS):
            cs = slice(G * gi, G * (gi + 1))
            ys = slice(CC + G * gi, CC + G * (gi + 1))
            ps = slice(2 * CC + G * gi, 2 * CC + G * (gi + 1))
            cnt = jnp.minimum(tpos + 1, w).astype(F32)
            yib = _pool_mean_minus_token(p_scr, cs, w, cnt, tt).astype(BF16)
            wp = wp_ref[gi].astype(BF16)
            sp = sp_ref[:, cs]
            dyp = dyc_ref[:, ys]
            q = _dot(yib, wp, NN)
            _accumulate(dsp_ref.at[:, cs], first, jnp.sum(dyp * q, axis=0, keepdims=True))
            dq_c = (dyp * sp).astype(BF16)
            dq_n = (jnp.where(last, 0.0, dyn_ref[:, ys]) * sp).astype(BF16)
            _accumulate(dwp_ref.at[gi], first, _dot(yib, dq_c, TN))
            dyi_c = _dot(dq_c, wp, NT)
            g_scr[0:tt, cs] = dyi_c / cnt
            g_scr[tt:, cs] = _dot(dq_n, wp, NT) * (1.0 / w)
            dp = -dyi_c
            for d in range(w):
                dp = dp + g_scr[d : d + tt, cs]
            dz_ref[:, ps] = dp.astype(BF16)
            _accumulate(dbin_ref.at[:, ps], first, jnp.sum(dp, axis=0, keepdims=True))

    cur = lambda i: (i, 0)
    prev = lambda i: (jnp.maximum(i * per - 1, 0), 0)
    nxt = lambda i: (jnp.minimum((i + 1) * per, last_halo), 0)
    c2 = lambda i: (0, 0)
    c3 = lambda i: (0, 0, 0)
    return _call(
        "seq_bwd",
        body,
        (n_tiles,),
        [
            pl.BlockSpec((tt, CI), cur),
            pl.BlockSpec((HALO, CI), prev),
            pl.BlockSpec((tt, D), cur),
            pl.BlockSpec((HALO, D), nxt),
            pl.BlockSpec((tt, CC), cur),
            pl.BlockSpec((HALO, CC), nxt),
            pl.BlockSpec(w_dw4.shape, lambda i: (0,) * w_dw4.ndim),
            pl.BlockSpec((1, CC), c2),
            pl.BlockSpec((1, CC), c2),
            pl.BlockSpec(w_pool.shape, c3),
            pl.BlockSpec((1, CP), c2),
        ],
        [
            pl.BlockSpec((tt, CI), cur),
            pl.BlockSpec((n_cc, HALO, LANES), c3),
            pl.BlockSpec((1, CC), c2),
            pl.BlockSpec((1, CC), c2),
            pl.BlockSpec((1, CC), c2),
            pl.BlockSpec((n_grp, G, G), c3),
            pl.BlockSpec((1, CP), c2),
            pl.BlockSpec((1, CI), c2),
        ],
        [
            jax.ShapeDtypeStruct((T, CI), BF16),
            jax.ShapeDtypeStruct((n_cc, HALO, LANES), F32),
            jax.ShapeDtypeStruct((1, CC), F32),
            jax.ShapeDtypeStruct((1, CC), F32),
            jax.ShapeDtypeStruct((1, CC), F32),
            jax.ShapeDtypeStruct((n_grp, G, G), F32),
            jax.ShapeDtypeStruct((1, CP), F32),
            jax.ShapeDtypeStruct((1, CI), F32),
        ],
        (z, z, dy, dy, v, v, w_dw4, ln_g, ln_b, w_pool, s_pool),
        scratch=[
            pltpu.VMEM((SUBLANES, tt + HALO, CC), F32),
            pltpu.VMEM((tt, CC), F32),
            pltpu.VMEM((HALO + tt, CP), F32),
            pltpu.VMEM((tt + HALO, CP), F32),
            pltpu.VMEM((n_cc * HALO, 8, LANES), F32),
        ],
        comm=comm,
    )


def _in_proj_bwd(dz_b, w_inT_b, x, dh1, g_mix, after=()):
    T, D = x.shape
    CI = w_inT_b.shape[0]
    tm = _tile(T, 512)

    def body(dz_ref, w_ref, x_ref, dh1_ref, g_ref, dx_ref, dg_ref):
        i = pl.program_id(0)
        dxn = _dot(dz_ref[...], w_ref[...], NN)
        dx, dg_rows = _rms_bwd(x_ref[...], g_ref[...], dxn)
        dx_ref[...] = dh1_ref[...] + dx
        _accumulate(dg_ref, i == 0, jnp.sum(dg_rows, axis=0, keepdims=True))

    row = lambda i: (i, 0)
    const = lambda i: (0, 0)
    return _call(
        "in_proj_bwd",
        body,
        (T // tm,),
        [
            pl.BlockSpec((tm, CI), row),
            pl.BlockSpec((CI, D), const),
            pl.BlockSpec((tm, D), row),
            pl.BlockSpec((tm, D), row),
            pl.BlockSpec((1, D), const),
        ],
        [pl.BlockSpec((tm, D), row), pl.BlockSpec((1, D), const)],
        [jax.ShapeDtypeStruct((T, D), F32), jax.ShapeDtypeStruct((1, D), F32)],
        (dz_b, w_inT_b, x, dh1, g_mix),
        after=after,
    )


def _weight_grad(name, a_b, b_b, after=()):
    T, N1 = a_b.shape
    N2 = b_b.shape[1]
    t1 = _tile(N1, 1408, LANES)
    tk = _tile(T, 2048 if N1 > t1 else 1024)
    nk = T // tk

    def body(a_ref, b_ref, o_ref, acc):
        k = pl.program_id(1)
        _accumulate(acc, k == 0, _dot(a_ref[...], b_ref[...], TN))

        @pl.when(k == nk - 1)
        def _():
            o_ref[...] = acc[...].astype(BF16)

    (out,), _ = _call(
        name,
        body,
        (N1 // t1, nk),
        [pl.BlockSpec((tk, t1), lambda n, k: (k, n)), pl.BlockSpec((tk, N2), lambda n, k: (k, 0))],
        [pl.BlockSpec((t1, N2), lambda n, k: (n, 0))],
        [jax.ShapeDtypeStruct((N1, N2), BF16)],
        (a_b, b_b),
        scratch=[pltpu.VMEM((t1, N2), F32)],
        after=after,
    )
    return out


def _sum_parts(name, full, how, parts, me):
    _, R, C = parts[0].shape
    tr = _tile(R, 512)
    nb = R // tr
    where = [(q, r) for q, p in enumerate(parts) for r in range(p.shape[0])]
    assert len(where) == 3

    def body(me_ref, own_ref, *refs):
        o_ref = refs[-1]
        f = lambda j: refs[where[j][0]][where[j][1]].astype(F32)
        o_ref[...] = (own_ref[...].astype(F32) + f(0)) + (f(1) + f(2))

    own_map = {"rows": lambda i, me_ref: (me_ref[0] * nb + i, 0), "all": lambda i, me_ref: (i, 0)}[how]
    return pl.pallas_call(
        body,
        name=name,
        grid_spec=pltpu.PrefetchScalarGridSpec(
            num_scalar_prefetch=1,
            grid=(nb,),
            in_specs=[pl.BlockSpec((tr, C), own_map)]
            + [pl.BlockSpec((p.shape[0], tr, C), lambda i, me_ref: (0, i, 0)) for p in parts],
            out_specs=pl.BlockSpec((tr, C), lambda i, me_ref: (i, 0)),
        ),
        out_shape=jax.ShapeDtypeStruct((R, C), F32),
        compiler_params=pltpu.CompilerParams(dimension_semantics=("arbitrary",), vmem_limit_bytes=VMEM_LIMIT),
    )(me, full, *parts)


_M_CORR = 1.0 - ADAM_B1**ADAM_STEP
_V_CORR = 1.0 - ADAM_B2**ADAM_STEP


def _adamw_math(w, g, m, v):
    m = ADAM_B1 * m + (1.0 - ADAM_B1) * g
    v = ADAM_B2 * v + (1.0 - ADAM_B2) * (g * g)
    delta = -ADAM_LR * ((m / _M_CORR) / (jnp.sqrt(v / _V_CORR) + ADAM_EPS) + ADAM_WD * w)
    return delta, m, v


def _adamw(name, w, m, v, g_here, g_there, g_transposed=False):
    R, C = w.shape
    tr = _tile(R, 256, LANES if g_transposed else 8)

    def body(w_ref, m_ref, v_ref, ga_ref, gb_ref, g_ref, d_ref, nm_ref, nv_ref):
        g = ga_ref[...] + gb_ref[...]
        if g_transposed:
            g = g.T
        g_ref[...] = g
        d_ref[...], nm_ref[...], nv_ref[...] = _adamw_math(w_ref[...], g, m_ref[...], v_ref[...])

    spec = pl.BlockSpec((tr, C), lambda i: (i, 0))
    gspec = pl.BlockSpec((C, tr), lambda i: (0, i)) if g_transposed else spec
    return _call(name, body, (R // tr,), [spec] * 3 + [gspec] * 2, [spec] * 4, [jax.ShapeDtypeStruct((R, C), F32)] * 4,
                 (w, m, v, g_here, g_there))


def _adamw_on_sparsecore(name, w, m, v, g_here, g_there, after):
    R, C = w.shape
    n_groups = R // SUBLANES
    n_turns = -(-n_groups // SC_TILES)
    n_in, n_out = 5, 4

    def body(w_hbm, m_hbm, v_hbm, ga_hbm, gb_hbm, after_hbm, g_out, d_out, nm_out, nv_out, bufs, sems):
        tile = lax.axis_index("subcore") * SC_CORES + lax.axis_index("sparsecore")
        srcs = (w_hbm, m_hbm, v_hbm, ga_hbm, gb_hbm)
        dsts = (d_out, nm_out, nv_out, g_out)

        def rows(turn):
            return pl.ds((tile + turn * SC_TILES) * SUBLANES, SUBLANES)

        def loads(turn):
            slot = turn % 2
            return [pltpu.make_async_copy(srcs[q].at[rows(turn), :], bufs.at[slot, q], sems.at[slot, q]) for q in range(n_in)]

        def stores(turn):
            slot = turn % 2
            return [pltpu.make_async_copy(bufs.at[slot, q], dsts[q].at[rows(turn), :], sems.at[slot, n_in + q])
                    for q in range(n_out)]

        def when_mine(turn, fn):
            pl.when(tile + turn * SC_TILES < n_groups)(fn)

        def compute(slot):
            wb, mb, vb, gab, gbb = (bufs.at[slot, q] for q in range(n_in))

            @pl.loop(0, SUBLANES)
            def _(r):
                @pl.loop(0, C, step=SC_LANES)
                def _(i):
                    at = (r, pl.ds(i, SC_LANES))
                    g = gab[at] + gbb[at]
                    delta, new_m, new_v = _adamw_math(wb[at], g, mb[at], vb[at])
                    gab[at], wb[at], mb[at], vb[at] = g, delta, new_m, new_v

        def start_loads(turn):
            def fn():
                for cp in loads(turn):
                    cp.start()

            when_mine(turn, fn)

        start_loads(0)
        for turn in range(n_turns):
            def step(turn=turn):
                for cp in loads(turn):
                    cp.wait()
                if turn >= 1:
                    for cp in stores(turn - 1):
                        cp.wait()
                if turn + 1 < n_turns:
                    start_loads(turn + 1)
                compute(turn % 2)
                for cp in stores(turn):
                    cp.start()

            when_mine(turn, step)
        for turn in range(n_turns):
            def drain(turn=turn):
                for cp in stores(turn):
                    cp.wait()

            last_mine = jnp.logical_and(tile + turn * SC_TILES < n_groups, tile + (turn + 1) * SC_TILES >= n_groups)
            pl.when(last_mine)(drain)

    return pl.kernel(
        body,
        name=name,
        out_type=[jax.ShapeDtypeStruct((R, C), F32)] * 4,
        mesh=plsc.VectorSubcoreMesh(core_axis_name="sparsecore", subcore_axis_name="subcore"),
        scratch_types=[pltpu.VMEM((2, n_in, SUBLANES, C), F32), pltpu.SemaphoreType.DMA((2, n_in + n_out))],
        compiler_params=pltpu.CompilerParams(use_tc_tiling_on_sc=True),
    )(w, m, v, g_here, g_there, after)


class _PackLayout:
    def __init__(self, n_cc, n_grp, G, widths):
        self.dw_rows = (0, HALO)
        self.wp_rows = (HALO, HALO + G)
        self.n_cc, self.n_grp, self.G = n_cc, n_grp, G
        self.vec = {}
        r = HALO + G
        for name, width in widths:
            self.vec[name] = (r, width)
            r += width // PACK_W
        self.rows = -(-r // 8) * 8


def _pack_small(layout, dwdw, dwp, vecs):
    names = list(vecs)

    def body(*refs):
        dw_ref, wp_ref = refs[0], refs[1]
        vec_refs = refs[2 : 2 + len(names)]
        o_ref = refs[-1]
        o_ref[...] = jnp.zeros_like(o_ref)
        for j in range(layout.n_cc):
            o_ref[layout.dw_rows[0] : layout.dw_rows[1], j * LANES : (j + 1) * LANES] = dw_ref[j]
        for i in range(layout.n_grp):
            o_ref[layout.wp_rows[0] : layout.wp_rows[1], i * layout.G : (i + 1) * layout.G] = wp_ref[i]
        for name, ref in zip(names, vec_refs):
            r, width = layout.vec[name]
            for h in range(width // PACK_W):
                o_ref[r + h : r + h + 1, :] = ref[:, h * PACK_W : (h + 1) * PACK_W]

    return pl.pallas_call(
        body,
        name="pack_small",
        out_shape=jax.ShapeDtypeStruct((layout.rows, PACK_W), F32),
    )(dwdw, dwp, *[vecs[k] for k in names])


def _adamw_small(layout, g_here, g_there, w_dw, m_dw, v_dw, w_pool, m_pool, v_pool, vec_w, vec_m, vec_v, row):
    names = list(vec_w)
    nv = len(names)

    def body(*refs):
        ga_ref, gb_ref = refs[0], refs[1]
        wdw, mdw, vdw, wp, mp, vp = refs[2:8]
        vw, vm, vv = refs[8 : 8 + nv], refs[8 + nv : 8 + 2 * nv], refs[8 + 2 * nv : 8 + 3 * nv]
        row_g, row_w, row_m, row_v = refs[8 + 3 * nv : 12 + 3 * nv]
        outs = refs[12 + 3 * nv :]
        acc = outs[-1]
        acc[...] = ga_ref[...] + gb_ref[...]

        def emit(o, g, w, m, v, idx=()):
            res = (g,) + _adamw_math(w, g, m, v)
            for ref, val in zip(o, res):
                ref[idx] = val

        me = 2 * lax.axis_index("x") + lax.axis_index("y")
        for j in range(layout.n_cc):

            @pl.when(me == j)
            def _(j=j):
                for k in range(wdw.shape[0]):
                    g = acc[layout.dw_rows[0] + k : layout.dw_rows[0] + k + 1, j * LANES : (j + 1) * LANES]
                    emit(outs[0:4], g, wdw[k], mdw[k], vdw[k], idx=k)

        for i in range(layout.n_grp):
            g = acc[layout.wp_rows[0] : layout.wp_rows[1], i * layout.G : (i + 1) * layout.G]
            emit(outs[4:8], g, wp[i], mp[i], vp[i], idx=i)
        for q, name in enumerate(names):
            r, width = layout.vec[name]
            for h in range(width // PACK_W):
                ls = slice(h * PACK_W, (h + 1) * PACK_W)
                g = acc[r + h : r + h + 1, :]
                emit(outs[8 + 4 * q : 12 + 4 * q], g, vw[q][:, ls], vm[q][:, ls], vv[q][:, ls], idx=(slice(None), ls))
        emit(outs[8 + 4 * nv : 12 + 4 * nv], row_g[...], row_w[...], row_m[...], row_v[...], idx=...)

    shapes = [w_dw.shape] * 4 + [w_pool.shape] * 4
    for name in names:
        shapes += [vec_w[name].shape] * 4
    shapes += [row[1].shape] * 4
    return pl.pallas_call(
        body,
        name="adamw_small",
        out_shape=[jax.ShapeDtypeStruct(s, F32) for s in shapes],
        scratch_shapes=[pltpu.VMEM(g_here.shape, F32)],
    )(g_here, g_there, w_dw, m_dw, v_dw, w_pool, m_pool, v_pool,
      *[vec_w[k] for k in names], *[vec_m[k] for k in names], *[vec_v[k] for k in names], *row)


def _allreduce_rows(g_part, loss_part, comm=()):
    n_pairs = N_DEV - 1

    def body(g_ref, l_ref, go_ref, lo_ref, land_g, land_l, sems):
        x, y, c = _place()
        copies = []
        for q, (src, land) in enumerate(((g_ref, land_g), (l_ref, land_l))):
            for r in range(1, N_DEV):
                fx, fy, fc = (r >> 2) & 1, (r >> 1) & 1, r & 1
                peer = (1 - x if fx else x, 1 - y if fy else y, 1 - c if fc else c)
                cp = _remote(src, land.at[r], sems, 2 * (q * n_pairs + r - 1), peer)
                cp.start()
                copies.append(cp)
        for cp in copies:
            cp.wait()

        def total(src, land):
            row = lambda r: src[...] if r == 0 else land[r]
            return ((row(0) + row(4)) + (row(2) + row(6))) + ((row(1) + row(5)) + (row(3) + row(7)))

        go_ref[...] = total(g_ref, land_g)
        lo_ref[...] = total(l_ref, land_l)

    vm = pl.BlockSpec(memory_space=pltpu.VMEM)
    return _call(
        "allreduce_rows",
        body,
        (),
        [vm] * 2,
        [vm] * 2,
        [jax.ShapeDtypeStruct(g_part.shape, F32), jax.ShapeDtypeStruct(loss_part.shape, F32)],
        (g_part, loss_part),
        scratch=[pltpu.VMEM((N_DEV,) + g_part.shape, F32), pltpu.VMEM((N_DEV,) + loss_part.shape, F32),
                 pltpu.SemaphoreType.DMA((4 * n_pairs,))],
        comm=comm,
    )


def kernel(x, g_mix, w_in, b_in, w_dw, b_dw, ln_g, ln_b, w_pool, s_pool, w_out, g_ffn, w_gate, w_up, w_down, g_final, loss_target, m_g_mix, m_w_in, m_b_in, m_w_dw, m_b_dw, m_ln_g, m_ln_b, m_w_pool, m_s_pool, m_w_out, m_g_ffn, m_w_gate, m_w_up, m_w_down, m_g_final, v_g_mix, v_w_in, v_b_in, v_w_dw, v_b_dw, v_ln_g, v_ln_b, v_w_pool, v_s_pool, v_w_out, v_g_ffn, v_w_gate, v_w_up, v_w_down, v_g_final):
    x2 = x[0]
    target = loss_target[0]
    T, D = x2.shape
    w_in2, w_out2, w_down2 = w_in[0], w_out[0], w_down[0]
    taps_first = lambda a: jnp.transpose(a, (1, 0, 2))
    w_dw3 = taps_first(w_dw)
    w_gateT, w_upT = w_gate[0].T, w_up[0].T
    CI = w_in2.shape[1] * N_CHIPS
    DM = w_out2.shape[0] * N_CHIPS
    F = w_down2.shape[0] * N_CHIPS
    KW, _, dw_cols = w_dw3.shape
    assert dw_cols == LANES
    n_grp, G = w_pool.shape[1], w_pool.shape[-1]
    w_pool3 = w_pool[0]
    g_final2 = g_final.reshape(1, D)

    me = (2 * lax.axis_index("x") + lax.axis_index("y")).astype(jnp.int32).reshape(1)

    w_inT_b, w_dw4, f_out, f_gate, f_up, f_down = _place_and_gather(
        [(w_in2, "rows", (CI, D), BF16, True, True), (w_dw3, "lead", (N_CHIPS, KW, 1, dw_cols), F32, False, False)],
        [(w, "rows", shape, BF16, False, True)
         for w, shape in ((w_out2, (DM, D)), (w_gateT, (F, D)), (w_upT, (F, D)), (w_down2, (F, D)))])
    ici = lambda f: _GatherIci([f], ["rows"], [True])
    d2d = lambda f: _GatherD2d([f], ["rows"])
    gather = _start("gather_start", [ici(f_out), ici(f_gate), ici(f_up), ici(f_down)])
    (z, xn_b), _ = _in_proj(x2, g_mix, w_inT_b, b_in, after=[gather.token])
    (f_out,) = _wait("gather_out_wait", gather, 0, xn_b)
    s_out = _start("share_out_start", [d2d(f_out)], sibling_only=True)
    (y_b, v), _ = _seq_fwd(z, w_dw4, b_dw, ln_g, ln_b, w_pool3, s_pool, after=[s_out.token])
    (w_out_b,) = _wait("share_out_wait", s_out, 0, y_b)
    (f_gate,) = _wait("gather_gate_wait", gather, 1, y_b)
    s_gate = _start("share_gate_start", [d2d(f_gate)], sibling_only=True)
    (h1, hn_b), _ = _out_proj(y_b, x2, w_out_b, g_ffn, after=[s_gate.token])
    (f_up,) = _wait("gather_up_wait", gather, 2, hn_b)
    s_up = _start("share_up_start", [d2d(f_up)], sibling_only=True)
    (wgT_b,) = _wait("share_gate_wait", s_gate, 0, hn_b)
    (wuT_b,) = _wait("share_up_wait", s_up, 0, hn_b)
    (silu_b, uds_b, a_b), _ = _gate_up(hn_b, wgT_b, wuT_b)
    (f_down,) = _wait("gather_down_wait", gather, 3, a_b)
    s_down = _start("share_down_start", [d2d(f_down)], sibling_only=True)
    (wd_b,) = _wait("share_down_wait", s_down, 0, a_b)
    (dh2, dh2_b, loss_part, d_g_final), _ = _down_loss(a_b, wd_b, h1, target, g_final2)

    gw_down = _weight_grad("grad_w_down", a_b, dh2_b)
    x_down = _start("scatter_down_start", [_Scatter([gw_down], ["rows"])])
    (dg_b, du_b), _ = _ffn_bwd_act(dh2_b, wd_b, silu_b, uds_b, after=[x_down.token])
    gw_gateT = _weight_grad("grad_w_gate", dg_b, hn_b)
    gw_upT = _weight_grad("grad_w_up", du_b, hn_b)
    gw_down, p_down = _wait("scatter_down_wait", x_down, 0, gw_upT)
    sum_down = _sum_parts("sum_w_down", gw_down, "rows", [p_down], me)
    (dh1, dh1_b, dy, d_g_ffn), (p_gate, oth_down) = _ffn_bwd_in(
        dg_b, du_b, wgT_b, wuT_b, h1, dh2, g_ffn, w_out_b, comm=[_Scatter([gw_gateT], ["rows"]), _Swap([sum_down])])
    gw_out = _weight_grad("grad_w_out", y_b, dh1_b)
    sum_gate = _sum_parts("sum_w_gate", gw_gateT, "rows", [p_gate], me)
    res = {}
    res["w_down"] = _adamw_on_sparsecore("adamw_w_down", w_down2, m_w_down[0], v_w_down[0], sum_down, oth_down, sum_down)
    (dz_b, d_wdw, d_bdw, d_lng, d_lnb, d_wp, d_sp, d_bin), (p_up, p_out, oth_gate) = _seq_bwd(
        z, dy, v, w_dw4, ln_g, ln_b, w_pool3, s_pool,
        comm=[_Scatter([gw_upT, gw_out], ["rows", "rows"]), _Swap([sum_gate])])
    res["w_gate"] = _adamw_on_sparsecore(
        "adamw_w_gate", w_gateT, m_w_gate[0].T, v_w_gate[0].T, sum_gate, oth_gate, res["w_down"][0])
    vec_grads ={"b_dw": d_bdw, "ln_g": d_lng, "ln_b": d_lnb, "s_pool": d_sp, "g_ffn": d_g_ffn, "g_final": d_g_final, "b_in": d_bin}
    layout = _PackLayout(dw_cols * N_CHIPS // LANES, n_grp, G, [(k, a.shape[1]) for k, a in vec_grads.items()])
    pack = _pack_small(layout, d_wdw, d_wp, vec_grads)
    sum_up = _sum_parts("sum_w_up", gw_upT, "rows", [p_up], me)
    sum_out = _sum_parts("sum_w_out", gw_out, "rows", [p_out], me)
    mid = _start("mid_start", [_Swap([sum_up, sum_out]), _Scatter([pack], ["all"])])
    gw_inT = _weight_grad("grad_w_in", dz_b, xn_b, after=[mid.token])
    sum_up, sum_out, oth_up, oth_out = _wait("mid_swap_wait", mid, 0, gw_inT)
    late = _start("late_start", [_Scatter([gw_inT], ["rows"])])
    (grad_x, d_g_mix), _ = _in_proj_bwd(dz_b, w_inT_b, x2, dh1, g_mix, after=[late.token])
    pack, p_small = _wait("mid_small_wait", mid, 1, d_g_mix)
    gw_inT, p_in = _wait("late_w_in_wait", late, 0, d_g_mix)
    sum_small = _sum_parts("sum_small", pack, "all", [p_small], me)
    res["w_up"] = _adamw_on_sparsecore("adamw_w_up", w_upT, m_w_up[0].T, v_w_up[0].T, sum_up, oth_up, res["w_gate"][0])
    res["w_out"] = _adamw_on_sparsecore("adamw_w_out", w_out2, m_w_out[0], v_w_out[0], sum_out, oth_out, res["w_gate"][0])
    sum_in = _sum_parts("sum_w_in", gw_inT, "rows", [p_in], me)
    (d_g_mix, loss_row), (oth_in, oth_small) = _allreduce_rows(d_g_mix, loss_part, comm=[_Swap([sum_in, sum_small])])
    loss = loss_row[0, 0]
    res["w_in"], _ = _adamw("adamw_w_in", w_in2, m_w_in[0], v_w_in[0], sum_in, oth_in, g_transposed=True)

    vec_w = {"b_dw": b_dw, "ln_g": ln_g, "ln_b": ln_b, "s_pool": s_pool, "g_ffn": g_ffn, "g_final": g_final2, "b_in": b_in}
    vec_m = {"b_dw": m_b_dw, "ln_g": m_ln_g, "ln_b": m_ln_b, "s_pool": m_s_pool, "g_ffn": m_g_ffn,
             "g_final": m_g_final.reshape(1, D), "b_in": m_b_in}
    vec_v = {"b_dw": v_b_dw, "ln_g": v_ln_g, "ln_b": v_ln_b, "s_pool": v_s_pool, "g_ffn": v_g_ffn,
             "g_final": v_g_final.reshape(1, D), "b_in": v_b_in}
    small = _adamw_small(layout, sum_small, oth_small, w_dw3, taps_first(m_w_dw), taps_first(v_w_dw),
                         w_pool3, m_w_pool[0], v_w_pool[0], vec_w, vec_m, vec_v, (d_g_mix, g_mix, m_g_mix, v_g_mix))
    res["w_dw"] = [taps_first(a) for a in small[0:4]]
    res["w_pool"] = [a[None] for a in small[4:8]]
    for q, k in enumerate(vec_w):
        res[k] = list(small[8 + 4 * q : 12 + 4 * q])
    res["g_mix"] = list(small[-4:])
    res["g_final"] = [a.reshape(D) for a in res["g_final"]]
    for k in ("w_in", "w_out", "w_down"):
        res[k] = [a[None] for a in res[k]]
    for k in ("w_gate", "w_up"):
        res[k] = [a.T[None] for a in res[k]]

    order = ["g_mix", "w_in", "b_in", "w_dw", "b_dw", "ln_g", "ln_b", "w_pool", "s_pool", "w_out", "g_ffn", "w_gate", "w_up", "w_down", "g_final"]
    outs = [loss, grad_x[None]]
    for q in range(4):
        outs += [res[k][q] for k in order]
    return tuple(outs)
```

```python
import jax
import jax.numpy as jnp
from jax import lax
from jax.experimental import pallas as pl
from jax.experimental.pallas import tpu as pltpu
from jax.experimental.pallas import tpu_sc as plsc

F32 = jnp.float32
BF16 = jnp.bfloat16
MESH = pl.DeviceIdType.MESH
ANY = pl.BlockSpec(memory_space=pl.ANY)

RMS_EPS = 1e-6
LN_EPS = 1e-5
POOL_WINDOWS = (2, 4, 8, 16)
ADAM_LR = 0.001
ADAM_B1 = 0.9
ADAM_B2 = 0.999
ADAM_EPS = 1e-08
ADAM_WD = 0.01
ADAM_STEP = 10

LANES = 128
SUBLANES = 8
BF16_ROWS = 16
HALO = 32
CONV_ROWS = 64
HIDDEN_CHUNK = 512
VMEM_LIMIT = 56 * 1024 * 1024
PACK_W = 512
N_CHIPS = 4
N_DEV = 8
SIBLING_BARRIER_ID = 0
SC_CORES = 2
SC_TILES = 32
SC_LANES = 16


def _tile(n, want, mult=8):
    t = min(n, want)
    while n % t or t % mult:
        t -= 1
    return t


def _sigmoid(x):
    return 1.0 / (1.0 + jnp.exp(-x))


def _dot(a, b, dims):
    return lax.dot_general(a, b, (dims, ((), ())), preferred_element_type=F32)


NN = ((1,), (0,))
NT = ((1,), (1,))
TN = ((0,), (0,))


def _rms_bwd(x, g, dy):
    r = lax.rsqrt(jnp.mean(x * x, axis=-1, keepdims=True) + RMS_EPS)
    xh = x * r
    gy = dy * g
    dx = r * (gy - xh * jnp.mean(gy * xh, axis=-1, keepdims=True))
    return dx, dy * xh


def _accumulate(ref, first, val):
    @pl.when(first)
    def _():
        ref[...] = val

    @pl.when(jnp.logical_not(first))
    def _():
        ref[...] += val


def _place():
    return lax.axis_index("x"), lax.axis_index("y"), lax.axis_index("c")


def _other_chips(x, y):
    return [(1 - x, y), (x, 1 - y), (1 - x, 1 - y)]


def _rows(ref, start, n):
    return ref.at[pl.ds(pl.multiple_of(start, BF16_ROWS), n)]


def _window(ref, how, k, c=None):
    if how == "all":
        return ref
    if how == "lead":
        return ref.at[k]
    assert how == "rows"
    n = ref.shape[0] // N_CHIPS
    if c is None:
        return _rows(ref, k * n, n)
    return _rows(ref, k * n + c * (n // 2), n // 2)


def _remote(src, dst, sems, s, device):
    return pltpu.make_async_remote_copy(
        src_ref=src, dst_ref=dst, send_sem=sems.at[s], recv_sem=sems.at[s + 1], device_id=device, device_id_type=MESH)


class _GatherIci:
    aliased = True

    def __init__(self, fulls, hows, splits):
        self.fulls, self.hows, self.splits = list(fulls), list(hows), list(splits)

    def inputs(self):
        return self.fulls

    def out_shapes(self):
        return [jax.ShapeDtypeStruct(a.shape, a.dtype) for a in self.fulls]

    def n_sems(self):
        return 6 * len(self.fulls)

    def build(self, ins, outs, sems, base):
        x, y, c = _place()
        me = 2 * x + y
        chips = _other_chips(x, y)
        starts, waits = [], []
        for a, (how, sp) in enumerate(zip(self.hows, self.splits)):
            half = c if sp else None
            mine = _window(outs[a], how, me, half)
            for j, (px, py) in enumerate(chips):
                s = base + 6 * a + 2 * j
                cp = _remote(mine, mine, sems, s, (px, py, c))
                landing = _remote(mine, _window(outs[a], how, 2 * px + py, half), sems, s, (px, py, c))
                starts.append(cp.start)
                waits += [landing.wait_recv, cp.wait_send]
        return starts, waits


class _GatherD2d:
    aliased = True

    def __init__(self, fulls, hows):
        self.fulls, self.hows = list(fulls), list(hows)

    def inputs(self):
        return self.fulls

    def out_shapes(self):
        return [jax.ShapeDtypeStruct(a.shape, a.dtype) for a in self.fulls]

    def n_sems(self):
        return 6 * len(self.fulls)

    def build(self, ins, outs, sems, base):
        x, y, c = _place()
        starts, waits = [], []
        for a, how in enumerate(self.hows):
            for j, (px, py) in enumerate(_other_chips(x, y)):
                s = base + 6 * a + 2 * j
                got = _window(outs[a], how, 2 * px + py, c)
                cp = _remote(got, got, sems, s, (x, y, 1 - c))
                landing = _remote(got, _window(outs[a], how, 2 * px + py, 1 - c), sems, s, (x, y, 1 - c))
                starts.append(cp.start)
                waits += [landing.wait_recv, cp.wait_send]
        return starts, waits


def _part_shape(a, how):
    if how == "all":
        return a.shape
    assert how == "rows"
    return (a.shape[0] // N_CHIPS, a.shape[1])


class _Scatter:
    aliased = False

    def __init__(self, fulls, hows):
        self.fulls, self.hows = list(fulls), list(hows)

    def inputs(self):
        return self.fulls

    def out_shapes(self):
        return [jax.ShapeDtypeStruct((N_CHIPS - 1,) + _part_shape(a, h), a.dtype) for a, h in zip(self.fulls, self.hows)]

    def n_sems(self):
        return 6 * len(self.fulls)

    def build(self, ins, outs, sems, base):
        x, y, c = _place()
        chips = _other_chips(x, y)
        starts, waits = [], []
        for a, how in enumerate(self.hows):
            for j, (px, py) in enumerate(chips):
                cp = _remote(_window(ins[a], how, 2 * px + py), outs[a].at[j], sems, base + 6 * a + 2 * j, (px, py, c))
                starts.append(cp.start)
                waits += [cp.wait_recv, cp.wait_send]
        return starts, waits


class _Swap:
    aliased = False

    def __init__(self, arrays):
        self.arrays = list(arrays)

    def inputs(self):
        return self.arrays

    def out_shapes(self):
        return [jax.ShapeDtypeStruct(a.shape, a.dtype) for a in self.arrays]

    def n_sems(self):
        return 2 * len(self.arrays)

    def build(self, ins, outs, sems, base):
        x, y, c = _place()
        starts, waits = [], []
        for a in range(len(ins)):
            cp = _remote(ins[a], outs[a], sems, base + 2 * a, (x, y, 1 - c))
            starts.append(cp.start)
            waits += [cp.wait_recv, cp.wait_send]
        return starts, waits


def _call(name, body, grid, in_specs, out_specs, out_shape, args, scratch=(), comm=(), after=()):
    comm, after = list(comm), list(after)
    n_in, n_out, n_scr, n_after = len(args), len(out_shape), len(scratch), len(after)
    c_in = [a for op in comm for a in op.inputs()]
    c_out = [s for op in comm for s in op.out_shapes()]
    n_sems = sum(op.n_sems() for op in comm)
    aliases, i_in, i_out = {}, 0, 0
    for op in comm:
        if op.aliased:
            for q in range(len(op.inputs())):
                aliases[n_in + n_after + i_in + q] = n_out + i_out + q
        i_in, i_out = i_in + len(op.inputs()), i_out + len(op.out_shapes())

    def wrapped(*refs):
        ins = refs[:n_in]
        cin = refs[n_in + n_after : n_in + n_after + len(c_in)]
        o0 = n_in + n_after + len(c_in)
        outs = refs[o0 : o0 + n_out]
        cout = refs[o0 + n_out : o0 + n_out + len(c_out)]
        s0 = o0 + n_out + len(c_out)
        scr = refs[s0 : s0 + n_scr]

        def copies():
            sems = refs[s0 + n_scr]
            starts, waits = [], []
            i_in = i_out = base = 0
            for op in comm:
                ni, no = len(op.inputs()), len(op.out_shapes())
                s, w = op.build(cin[i_in : i_in + ni], cout[i_out : i_out + no], sems, base)
                starts += s
                waits += w
                i_in, i_out, base = i_in + ni, i_out + no, base + op.n_sems()
            return starts, waits

        def run_starts():
            for start in copies()[0]:
                start()

        def run_waits():
            for wait in copies()[1]:
                wait()

        if comm and grid:
            first = last = True
            for d, n in enumerate(grid):
                first = jnp.logical_and(first, pl.program_id(d) == 0)
                last = jnp.logical_and(last, pl.program_id(d) == n - 1)
            pl.when(first)(run_starts)
        elif comm:
            run_starts()
        if body is not None:
            body(*ins, *outs, *scr)
        if comm and grid:
            pl.when(last)(run_waits)
        elif comm:
            run_waits()

    res = pl.pallas_call(
        wrapped,
        name=name,
        grid=grid,
        in_specs=list(in_specs) + [ANY] * (n_after + len(c_in)),
        out_specs=list(out_specs) + [ANY] * len(c_out),
        out_shape=list(out_shape) + c_out,
        scratch_shapes=list(scratch) + ([pltpu.SemaphoreType.DMA((n_sems,))] if comm else []),
        input_output_aliases=aliases,
        compiler_params=pltpu.CompilerParams(dimension_semantics=("arbitrary",) * len(grid), vmem_limit_bytes=VMEM_LIMIT),
    )(*args, *after, *c_in)
    return tuple(res[:n_out]), tuple(res[n_out:])


def _place_and_gather(now, later):
    items = list(now) + list(later)
    n, n_now = len(items), len(now)
    buf_shape = lambda it: it[0].shape[::-1] if it[4] else it[0].shape
    split_now = [a for a in range(n_now) if items[a][5]]

    def body(*refs):
        ins, outs = refs[:n], refs[n : 2 * n]
        stage, bufs = refs[2 * n : 3 * n - n_now], refs[3 * n - n_now : 4 * n - n_now]
        sems = refs[4 * n - n_now]
        x, y, c = _place()
        me = 2 * x + y
        chips = _other_chips(x, y)
        loads = [pltpu.make_async_copy(ins[a], stage[a - n_now], sems.at[a]) for a in range(n_now, n)]
        for ld in loads:
            ld.start()
        pending = []

        def place(a, val):
            _, how, _, dtype, transposed, _ = items[a]
            bufs[a][...] = (val.T if transposed else val).astype(dtype)
            cp = pltpu.make_async_copy(bufs[a], _window(outs[a], how, me), sems.at[n + a])
            cp.start()
            pending.append(cp.wait)

        arrivals = []
        for a in range(n_now):
            place(a, ins[a][...])
            how, split = items[a][1], items[a][5]
            half = c if split else None
            src = _rows(bufs[a], c * (bufs[a].shape[0] // 2), bufs[a].shape[0] // 2) if split else bufs[a]
            for j, (px, py) in enumerate(chips):
                s = 2 * n + 6 * a + 2 * j
                cp = _remote(src, _window(outs[a], how, me, half), sems, s, (px, py, c))
                landing = _remote(src, _window(outs[a], how, 2 * px + py, half), sems, s, (px, py, c))
                cp.start()
                arrivals.append(landing.wait_recv)
                pending.append(cp.wait_send)
        for a in range(n_now, n):
            loads[a - n_now].wait()
            place(a, stage[a - n_now][...])
        for wait in arrivals:
            wait()
        d2d = _GatherD2d([None] * len(split_now), [items[a][1] for a in split_now])
        starts, waits = d2d.build(None, [outs[a] for a in split_now], sems, 2 * n + 6 * n_now)
        for start in starts:
            start()
        for wait in waits + pending:
            wait()

    vm = pl.BlockSpec(memory_space=pltpu.VMEM)
    return pl.pallas_call(
        body,
        name="place_and_gather",
        in_specs=[vm] * n_now + [ANY] * (n - n_now),
        out_specs=[ANY] * n,
        out_shape=[jax.ShapeDtypeStruct(it[2], it[3]) for it in items],
        scratch_shapes=[pltpu.VMEM(it[0].shape, it[0].dtype) for it in later]
        + [pltpu.VMEM(buf_shape(it), it[3]) for it in items]
        + [pltpu.SemaphoreType.DMA((2 * n + 6 * n_now + 6 * len(split_now),))],
        compiler_params=pltpu.CompilerParams(vmem_limit_bytes=VMEM_LIMIT),
    )(*[it[0] for it in items])


_HBM = pl.BlockSpec(memory_space=pltpu.HBM)
_SEM = pl.BlockSpec(memory_space=pltpu.SEMAPHORE)
_DATAFLOW = pltpu.SideEffectType.DATAFLOW_SIDE_EFFECTING


class _Pending:
    def __init__(self, ops, bases, sems, arrays, token):
        self.ops, self.bases, self.sems, self.arrays, self.token = ops, bases, sems, arrays, token


def _op_refs(op, refs):
    n_src = len(op.inputs())
    return refs[:n_src], (refs[:n_src] if op.aliased else refs[n_src:])


def _start(name, ops, sibling_only=False):
    per_op = [list(op.inputs()) + ([] if op.aliased else [lax.empty(sd.shape, sd.dtype) for sd in op.out_shapes()])
              for op in ops]
    arrays = [a for group in per_op for a in group]
    bases = [sum(op.n_sems() for op in ops[:k]) for k in range(len(ops))]
    n = len(arrays)

    def body(*refs):
        sems, token = refs[n], refs[-1]
        if sibling_only:
            x, y, c = _place()
            barrier = pltpu.get_barrier_semaphore()
            pl.semaphore_signal(barrier, inc=1, device_id=(x, y, 1 - c), device_id_type=MESH)
            pl.semaphore_wait(barrier, 1)
        at = 0
        for op, group, base in zip(ops, per_op, bases):
            starts, _ = op.build(*_op_refs(op, refs[at : at + len(group)]), sems, base)
            for start in starts:
                start()
            at += len(group)
        token[...] = jnp.zeros_like(token)

    res = pl.pallas_call(
        body,
        name=name,
        out_shape=(pltpu.SemaphoreType.DMA((sum(op.n_sems() for op in ops),)),)
        + tuple(pltpu.HBM(a.shape, a.dtype) for a in arrays) + (jax.ShapeDtypeStruct((SUBLANES, LANES), F32),),
        in_specs=(_HBM,) * n,
        out_specs=(_SEM,) + (_HBM,) * n + (pl.BlockSpec(memory_space=pltpu.VMEM),),
        input_output_aliases={i: 1 + i for i in range(n)},
        compiler_params=pltpu.CompilerParams(
            has_side_effects=_DATAFLOW, collective_id=SIBLING_BARRIER_ID if sibling_only else None),
    )(*[pltpu.with_memory_space_constraint(a, pltpu.HBM) for a in arrays])
    thru, at, groups = list(res[1 : 1 + n]), 0, []
    for group in per_op:
        groups.append(thru[at : at + len(group)])
        at += len(group)
    return _Pending(list(ops), bases, res[0], groups, res[-1])


def _wait(name, pending, k, after):
    op, arrays = pending.ops[k], pending.arrays[k]
    n = len(arrays)

    def body(*refs):
        _, waits = op.build(*_op_refs(op, refs[:n]), refs[n], pending.bases[k])
        for wait in waits:
            wait()

    return pl.pallas_call(
        body,
        name=name,
        out_shape=tuple(pltpu.HBM(a.shape, a.dtype) for a in arrays),
        in_specs=(_HBM,) * n + (_SEM, ANY),
        out_specs=(_HBM,) * n,
        input_output_aliases={i: i for i in range(n)},
        compiler_params=pltpu.CompilerParams(has_side_effects=_DATAFLOW),
    )(*arrays, pending.sems, after)


def _in_proj(x, g_mix, w_inT_b, b_in, after=()):
    T, D = x.shape
    CI = w_inT_b.shape[0]
    tm = _tile(T, 512)

    def body(x_ref, g_ref, w_ref, b_ref, z_ref, xn_ref):
        xv = x_ref[...]
        r = lax.rsqrt(jnp.mean(xv * xv, axis=-1, keepdims=True) + RMS_EPS)
        xn = (xv * r * g_ref[...]).astype(BF16)
        xn_ref[...] = xn
        z_ref[...] = _dot(xn, w_ref[...], NT) + b_ref[...]

    return _call(
        "in_proj",
        body,
        (T // tm,),
        [
            pl.BlockSpec((tm, D), lambda i: (i, 0)),
            pl.BlockSpec((1, D), lambda i: (0, 0)),
            pl.BlockSpec((CI, D), lambda i: (0, 0)),
            pl.BlockSpec((1, CI), lambda i: (0, 0)),
        ],
        [pl.BlockSpec((tm, CI), lambda i: (i, 0)), pl.BlockSpec((tm, D), lambda i: (i, 0))],
        [jax.ShapeDtypeStruct((T, CI), F32), jax.ShapeDtypeStruct((T, D), BF16)],
        (x, g_mix, w_inT_b, b_in),
        after=after,
    )


def _fill_shifted(scr):
    n = scr.shape[1] - SUBLANES
    for s in range(1, SUBLANES):
        scr[s, 0:n, :] = scr[0, s : s + n, :]


def _shifted_rows(scr, off, n, cs):
    s = off % SUBLANES
    return scr[s, off - s : off - s + n, cs]


def _pool_mean_minus_token(p_scr, cs, w, cnt, tt):
    tok = p_scr[HALO : HALO + tt, cs]
    s = tok
    for d in range(1, w):
        s = s + p_scr[HALO - d : HALO - d + tt, cs]
    return s / cnt - tok


def _seq_fwd(z, w_dw4, b_dw, ln_g, ln_b, w_pool, s_pool, after=()):
    T, CI = z.shape
    CC = ln_g.shape[1]
    n_grp, G = w_pool.shape[0], w_pool.shape[-1]
    KW = w_dw4.shape[1]
    D = CC + n_grp * G
    tt = _tile(T, 512, HALO)
    per = tt // HALO

    def body(zc_ref, zp_ref, wdw_ref, bdw_ref, lng_ref, lnb_ref, wp_ref, sp_ref, y_ref, v_ref, u_scr, p_scr):
        i = pl.program_id(0)
        first = i == 0
        u_prev = zp_ref[:, 0:CC] * _sigmoid(zp_ref[:, CC : 2 * CC])
        u_scr[0, 0:HALO, :] = jnp.where(first, 0.0, u_prev)
        p_scr[0:HALO, :] = jnp.where(first, 0.0, zp_ref[:, 2 * CC :])
        u_scr[0, HALO:, :] = zc_ref[:, 0:CC] * _sigmoid(zc_ref[:, CC : 2 * CC])
        p_scr[HALO:, :] = zc_ref[:, 2 * CC :]
        _fill_shifted(u_scr)

        for j in range(CC // LANES):
            cs = slice(LANES * j, LANES * (j + 1))
            for rb in range(tt // CONV_ROWS):
                acc = jnp.zeros((CONV_ROWS, LANES), F32)
                for k in range(KW):
                    off = HALO - (KW - 1) + k + rb * CONV_ROWS
                    acc = acc + _shifted_rows(u_scr, off, CONV_ROWS, cs) * wdw_ref[j, k]
                v_ref[rb * CONV_ROWS : (rb + 1) * CONV_ROWS, cs] = acc + bdw_ref[:, cs]

        v = v_ref[...]
        mu = jnp.mean(v, axis=-1, keepdims=True)
        d = v - mu
        var = jnp.mean(d * d, axis=-1, keepdims=True)
        ln = d * lax.rsqrt(var + LN_EPS) * lng_ref[...] + lnb_ref[...]
        y_ref[:, 0:CC] = (ln * _sigmoid(ln)).astype(BF16)

        tpos = i * tt + lax.broadcasted_iota(jnp.int32, (tt, 1), 0)
        for gi, w in enumerate(POOL_WINDOWS):
            cs = slice(G * gi, G * (gi + 1))
            cnt = jnp.minimum(tpos + 1, w).astype(F32)
            yi = _pool_mean_minus_token(p_scr, cs, w, cnt, tt)
            q = _dot(yi.astype(BF16), wp_ref[gi].astype(BF16), NN)
            y_ref[:, CC + G * gi : CC + G * (gi + 1)] = (q * sp_ref[:, cs]).astype(BF16)

    const2 = lambda i: (0, 0)
    return _call(
        "seq_fwd",
        body,
        (T // tt,),
        [
            pl.BlockSpec((tt, CI), lambda i: (i, 0)),
            pl.BlockSpec((HALO, CI), lambda i: (jnp.maximum(i * per - 1, 0), 0)),
            pl.BlockSpec(w_dw4.shape, lambda i: (0,) * w_dw4.ndim),
            pl.BlockSpec((1, CC), const2),
            pl.BlockSpec((1, CC), const2),
            pl.BlockSpec((1, CC), const2),
            pl.BlockSpec(w_pool.shape, lambda i: (0, 0, 0)),
            pl.BlockSpec((1, n_grp * G), const2),
        ],
        [pl.BlockSpec((tt, D), lambda i: (i, 0)), pl.BlockSpec((tt, CC), lambda i: (i, 0))],
        [jax.ShapeDtypeStruct((T, D), BF16), jax.ShapeDtypeStruct((T, CC), F32)],
        (z, z, w_dw4, b_dw, ln_g, ln_b, w_pool, s_pool),
        scratch=[pltpu.VMEM((SUBLANES, HALO + tt, CC), F32), pltpu.VMEM((HALO + tt, n_grp * G), F32)],
        after=after,
    )


def _out_proj(y_b, x, w_out_b, g_ffn, after=()):
    T, D = x.shape
    tm = _tile(T, 512)

    def body(y_ref, x_ref, w_ref, g_ref, h1_ref, hn_ref):
        h1 = x_ref[...] + _dot(y_ref[...], w_ref[...], NN)
        h1_ref[...] = h1
        r = lax.rsqrt(jnp.mean(h1 * h1, axis=-1, keepdims=True) + RMS_EPS)
        hn_ref[...] = (h1 * r * g_ref[...]).astype(BF16)

    row = lambda i: (i, 0)
    return _call(
        "out_proj",
        body,
        (T // tm,),
        [
            pl.BlockSpec((tm, y_b.shape[1]), row),
            pl.BlockSpec((tm, D), row),
            pl.BlockSpec(w_out_b.shape, lambda i: (0, 0)),
            pl.BlockSpec((1, D), lambda i: (0, 0)),
        ],
        [pl.BlockSpec((tm, D), row), pl.BlockSpec((tm, D), row)],
        [jax.ShapeDtypeStruct((T, D), F32), jax.ShapeDtypeStruct((T, D), BF16)],
        (y_b, x, w_out_b, g_ffn),
        after=after,
    )


def _hidden_tile(F):
    return _tile(F, 1408, LANES)


def _gate_up(hn_b, wgT_b, wuT_b):
    T, D = hn_b.shape
    F = wgT_b.shape[0]
    tm, tf = _tile(T, 1024), _hidden_tile(F)

    def body(hn_ref, wg_ref, wu_ref, silu_ref, uds_ref, a_ref):
        hn = hn_ref[...]
        for c0 in range(0, tf, HIDDEN_CHUNK):
            cs = slice(c0, min(c0 + HIDDEN_CHUNK, tf))
            gv = _dot(hn, wg_ref[cs, :], NT)
            uv = _dot(hn, wu_ref[cs, :], NT)
            sg = _sigmoid(gv)
            silu = gv * sg
            silu_ref[:, cs] = silu.astype(BF16)
            uds_ref[:, cs] = (uv * (sg * (1.0 + gv * (1.0 - sg)))).astype(BF16)
            a_ref[:, cs] = (silu * uv).astype(BF16)

    wspec = pl.BlockSpec((tf, D), lambda j, i: (j, 0))
    ospec = pl.BlockSpec((tm, tf), lambda j, i: (i, j))
    return _call(
        "gate_up",
        body,
        (F // tf, T // tm),
        [pl.BlockSpec((tm, D), lambda j, i: (i, 0)), wspec, wspec],
        [ospec, ospec, ospec],
        [jax.ShapeDtypeStruct((T, F), BF16)] * 3,
        (hn_b, wgT_b, wuT_b),
    )


def _down_loss(a_b, wd_b, h1, target, g_final):
    T, D = h1.shape
    F = a_b.shape[1]
    tm = _tile(T, 512)
    nt = T // tm

    def body(a_ref, w_ref, h1_ref, t_ref, g_ref, dh2_ref, dh2b_ref, loss_ref, dg_ref):
        i = pl.program_id(0)
        h2 = h1_ref[...] + _dot(a_ref[...], w_ref[...], NN)
        r = lax.rsqrt(jnp.mean(h2 * h2, axis=-1, keepdims=True) + RMS_EPS)
        g = g_ref[...]
        diff = h2 * r * g - t_ref[...]
        _accumulate(loss_ref, i == 0, jnp.full(loss_ref.shape, jnp.sum(diff * diff) * (0.5 / D), F32))
        dh2, dg_rows = _rms_bwd(h2, g, diff * (1.0 / D))
        dh2_ref[...] = dh2
        dh2b_ref[...] = dh2.astype(BF16)
        _accumulate(dg_ref, i == 0, jnp.sum(dg_rows, axis=0, keepdims=True))

    row = lambda i: (i, 0)
    return _call(
        "down_loss",
        body,
        (nt,),
        [
            pl.BlockSpec((tm, F), row),
            pl.BlockSpec((F, D), lambda i: (0, 0), pipeline_mode=pl.Buffered(1)),
            pl.BlockSpec((tm, D), row),
            pl.BlockSpec((tm, D), row),
            pl.BlockSpec((1, D), lambda i: (0, 0)),
        ],
        [
            pl.BlockSpec((tm, D), row),
            pl.BlockSpec((tm, D), row),
            pl.BlockSpec((1, LANES), lambda i: (0, 0)),
            pl.BlockSpec((1, D), lambda i: (0, 0)),
        ],
        [
            jax.ShapeDtypeStruct((T, D), F32),
            jax.ShapeDtypeStruct((T, D), BF16),
            jax.ShapeDtypeStruct((1, LANES), F32),
            jax.ShapeDtypeStruct((1, D), F32),
        ],
        (a_b, wd_b, h1, target, g_final),
    )


def _ffn_bwd_act(dh2_b, wd_b, silu_b, uds_b, after=()):
    T, D = dh2_b.shape
    F = wd_b.shape[0]
    tm, tf = _tile(T, 1024), _hidden_tile(F)

    def body(d_ref, w_ref, silu_ref, uds_ref, dg_ref, du_ref):
        d = d_ref[...]
        for c0 in range(0, tf, HIDDEN_CHUNK):
            cs = slice(c0, min(c0 + HIDDEN_CHUNK, tf))
            da = _dot(d, w_ref[cs, :], NT)
            dg_ref[:, cs] = (da * uds_ref[:, cs].astype(F32)).astype(BF16)
            du_ref[:, cs] = (da * silu_ref[:, cs].astype(F32)).astype(BF16)

    aspec = pl.BlockSpec((tm, tf), lambda j, i: (i, j))
    return _call(
        "ffn_bwd_act",
        body,
        (F // tf, T // tm),
        [pl.BlockSpec((tm, D), lambda j, i: (i, 0)), pl.BlockSpec((tf, D), lambda j, i: (j, 0)), aspec, aspec],
        [aspec, aspec],
        [jax.ShapeDtypeStruct((T, F), BF16)] * 2,
        (dh2_b, wd_b, silu_b, uds_b),
        after=after,
    )


def _ffn_bwd_in(dg_b, du_b, wgT_b, wuT_b, h1, dh2, g_ffn, w_out_b, comm=()):
    T, D = h1.shape
    F = wgT_b.shape[0]
    DM = w_out_b.shape[0]
    tm = _tile(T, 512)

    def body(dg_ref, du_ref, wg_ref, wu_ref, h1_ref, dh2_ref, g_ref, wo_ref, dh1_ref, dh1b_ref, dy_ref, dgf_ref):
        i = pl.program_id(0)
        dhn = _dot(dg_ref[...], wg_ref[...], NN) + _dot(du_ref[...], wu_ref[...], NN)
        dx, dg_rows = _rms_bwd(h1_ref[...], g_ref[...], dhn)
        dh1 = dh2_ref[...] + dx
        dh1b = dh1.astype(BF16)
        dh1_ref[...] = dh1
        dh1b_ref[...] = dh1b
        dy_ref[...] = _dot(dh1b, wo_ref[...], NT)
        _accumulate(dgf_ref, i == 0, jnp.sum(dg_rows, axis=0, keepdims=True))

    row = lambda i: (i, 0)
    const = lambda i: (0, 0)
    return _call(
        "ffn_bwd_in",
        body,
        (T // tm,),
        [
            pl.BlockSpec((tm, F), row),
            pl.BlockSpec((tm, F), row),
            pl.BlockSpec((F, D), const, pipeline_mode=pl.Buffered(1)),
            pl.BlockSpec((F, D), const, pipeline_mode=pl.Buffered(1)),
            pl.BlockSpec((tm, D), row),
            pl.BlockSpec((tm, D), row),
            pl.BlockSpec((1, D), const),
            pl.BlockSpec((DM, D), const, pipeline_mode=pl.Buffered(1)),
        ],
        [pl.BlockSpec((tm, D), row), pl.BlockSpec((tm, D), row), pl.BlockSpec((tm, DM), row), pl.BlockSpec((1, D), const)],
        [
            jax.ShapeDtypeStruct((T, D), F32),
            jax.ShapeDtypeStruct((T, D), BF16),
            jax.ShapeDtypeStruct((T, DM), F32),
            jax.ShapeDtypeStruct((1, D), F32),
        ],
        (dg_b, du_b, wgT_b, wuT_b, h1, dh2, g_ffn, w_out_b),
        comm=comm,
    )


def _seq_bwd(z, dy, v, w_dw4, ln_g, ln_b, w_pool, s_pool, comm=()):
    T, CI = z.shape
    CC = ln_g.shape[1]
    n_grp, G = w_pool.shape[0], w_pool.shape[-1]
    CP = n_grp * G
    KW = w_dw4.shape[1]
    n_cc = CC // LANES
    D = CC + CP
    tt = _tile(T, 512, HALO)
    per = tt // HALO
    n_tiles = T // tt
    last_halo = T // HALO - 1

    def body(zc_ref, zp_ref, dyc_ref, dyn_ref, vc_ref, vn_ref, wdw_ref, lng_ref, lnb_ref, wp_ref, sp_ref,
             dz_ref, dwdw_ref, dbdw_ref, dlng_ref, dlnb_ref, dwp_ref, dsp_ref, dbin_ref,
             dv_scr, u_scr, p_scr, g_scr, dw_scr):
        i = pl.program_id(0)
        first = i == 0
        last = i == n_tiles - 1
        lng, lnb = lng_ref[...], lnb_ref[...]

        def conv_pre(vv, dyc):
            mu = jnp.mean(vv, axis=-1, keepdims=True)
            d = vv - mu
            rs = lax.rsqrt(jnp.mean(d * d, axis=-1, keepdims=True) + LN_EPS)
            xh = d * rs
            ln = xh * lng + lnb
            sg = _sigmoid(ln)
            dln = dyc * (sg * (1.0 + ln * (1.0 - sg)))
            dxh = dln * lng
            dv = rs * (dxh - jnp.mean(dxh, axis=-1, keepdims=True) - xh * jnp.mean(dxh * xh, axis=-1, keepdims=True))
            return dv, dln, xh

        dv_c, dln_c, xh_c = conv_pre(vc_ref[...], dyc_ref[:, 0:CC])
        dv_scr[0, 0:tt, :] = dv_c
        dv_n, _, _ = conv_pre(vn_ref[...], dyn_ref[:, 0:CC])
        dv_scr[0, tt:, :] = jnp.where(last, 0.0, dv_n)
        _fill_shifted(dv_scr)
        _accumulate(dlng_ref, first, jnp.sum(dln_c * xh_c, axis=0, keepdims=True))
        _accumulate(dlnb_ref, first, jnp.sum(dln_c, axis=0, keepdims=True))
        _accumulate(dbdw_ref, first, jnp.sum(dv_c, axis=0, keepdims=True))

        u_scr[...] = zc_ref[:, 0:CC] * _sigmoid(zc_ref[:, CC : 2 * CC])

        @pl.when(first)
        def _():
            dw_scr[...] = jnp.zeros_like(dw_scr)

        for j in range(n_cc):
            cs = slice(LANES * j, LANES * (j + 1))
            gs = slice(CC + LANES * j, CC + LANES * (j + 1))
            dbin_a = jnp.zeros((1, LANES), F32)
            dbin_g = jnp.zeros((1, LANES), F32)
            for rb in range(tt // CONV_ROWS):
                rows = slice(rb * CONV_ROWS, (rb + 1) * CONV_ROWS)
                u_blk = u_scr[rows, cs]
                du = jnp.zeros((CONV_ROWS, LANES), F32)
                for k in range(KW):
                    off = rb * CONV_ROWS + (KW - 1) - k
                    d = _shifted_rows(dv_scr, off, CONV_ROWS, cs)
                    du = du + d * wdw_ref[j, k]
                    dw_scr[j * HALO + k] += jnp.sum((u_blk * d).reshape(CONV_ROWS // 8, 8, LANES), axis=0)
                a = zc_ref[rows, cs]
                sg = _sigmoid(zc_ref[rows, gs])
                da = du * sg
                dgate = du * a * sg * (1.0 - sg)
                dz_ref[rows, cs] = da.astype(BF16)
                dz_ref[rows, gs] = dgate.astype(BF16)
                dbin_a = dbin_a + jnp.sum(da, axis=0, keepdims=True)
                dbin_g = dbin_g + jnp.sum(dgate, axis=0, keepdims=True)
            _accumulate(dbin_ref.at[:, cs], first, dbin_a)
            _accumulate(dbin_ref.at[:, gs], first, dbin_g)

        @pl.when(last)
        def _():
            dwdw_ref[...] = jnp.sum(dw_scr[...], axis=1).reshape(dwdw_ref.shape)

        p_scr[0:HALO, :] = jnp.where(first, 0.0, zp_ref[:, 2 * CC :])
        p_scr[HALO:, :] = zc_ref[:, 2 * CC :]
        tpos = i * tt + lax.broadcasted_iota(jnp.int32, (tt, 1), 0)
        for gi, w in enumerate(POOL_WINDOWS):
            cs = slice(G * gi, G * (gi + 1))
            ys = slice(CC + G * gi, CC + G * (gi + 1))
            ps = slice(2 * CC + G * gi, 2 * CC + G * (gi + 1))
            cnt = jnp.minimum(tpos + 1, w).astype(F32)
            yib = _pool_mean_minus_token(p_scr, cs, w, cnt, tt).astype(BF16)
            wp = wp_ref[gi].astype(BF16)
            sp = sp_ref[:, cs]
            dyp = dyc_ref[:, ys]
            q = _dot(yib, wp, NN)
            _accumulate(dsp_ref.at[:, cs], first, jnp.sum(dyp * q, axis=0, keepdims=True))
            dq_c = (dyp * sp).astype(BF16)
            dq_n = (jnp.where(last, 0.0, dyn_ref[:, ys]) * sp).astype(BF16)
            _accumulate(dwp_ref.at[gi], first, _dot(yib, dq_c, TN))
            dyi_c = _dot(dq_c, wp, NT)
            g_scr[0:tt, cs] = dyi_c / cnt
            g_scr[tt:, cs] = _dot(dq_n, wp, NT) * (1.0 / w)
            dp = -dyi_c
            for d in range(w):
                dp = dp + g_scr[d : d + tt, cs]
            dz_ref[:, ps] = dp.astype(BF16)
            _accumulate(dbin_ref.at[:, ps], first, jnp.sum(dp, axis=0, keepdims=True))

    cur = lambda i: (i, 0)
    prev = lambda i: (jnp.maximum(i * per - 1, 0), 0)
    nxt = lambda i: (jnp.minimum((i + 1) * per, last_halo), 0)
    c2 = lambda i: (0, 0)
    c3 = lambda i: (0, 0, 0)
    return _call(
        "seq_bwd",
        body,
        (n_tiles,),
        [
            pl.BlockSpec((tt, CI), cur),
            pl.BlockSpec((HALO, CI), prev),
            pl.BlockSpec((tt, D), cur),
            pl.BlockSpec((HALO, D), nxt),
            pl.BlockSpec((tt, CC), cur),
            pl.BlockSpec((HALO, CC), nxt),
            pl.BlockSpec(w_dw4.shape, lambda i: (0,) * w_dw4.ndim),
            pl.BlockSpec((1, CC), c2),
            pl.BlockSpec((1, CC), c2),
            pl.BlockSpec(w_pool.shape, c3),
            pl.BlockSpec((1, CP), c2),
        ],
        [
            pl.BlockSpec((tt, CI), cur),
            pl.BlockSpec((n_cc, HALO, LANES), c3),
            pl.BlockSpec((1, CC), c2),
            pl.BlockSpec((1, CC), c2),
            pl.BlockSpec((1, CC), c2),
            pl.BlockSpec((n_grp, G, G), c3),
            pl.BlockSpec((1, CP), c2),
            pl.BlockSpec((1, CI), c2),
        ],
        [
            jax.ShapeDtypeStruct((T, CI), BF16),
            jax.ShapeDtypeStruct((n_cc, HALO, LANES), F32),
            jax.ShapeDtypeStruct((1, CC), F32),
            jax.ShapeDtypeStruct((1, CC), F32),
            jax.ShapeDtypeStruct((1, CC), F32),
            jax.ShapeDtypeStruct((n_grp, G, G), F32),
            jax.ShapeDtypeStruct((1, CP), F32),
            jax.ShapeDtypeStruct((1, CI), F32),
        ],
        (z, z, dy, dy, v, v, w_dw4, ln_g, ln_b, w_pool, s_pool),
        scratch=[
            pltpu.VMEM((SUBLANES, tt + HALO, CC), F32),
            pltpu.VMEM((tt, CC), F32),
            pltpu.VMEM((HALO + tt, CP), F32),
            pltpu.VMEM((tt + HALO, CP), F32),
            pltpu.VMEM((n_cc * HALO, 8, LANES), F32),
        ],
        comm=comm,
    )


def _in_proj_bwd(dz_b, w_inT_b, x, dh1, g_mix, after=()):
    T, D = x.shape
    CI = w_inT_b.shape[0]
    tm = _tile(T, 512)

    def body(dz_ref, w_ref, x_ref, dh1_ref, g_ref, dx_ref, dg_ref):
        i = pl.program_id(0)
        dxn = _dot(dz_ref[...], w_ref[...], NN)
        dx, dg_rows = _rms_bwd(x_ref[...], g_ref[...], dxn)
        dx_ref[...] = dh1_ref[...] + dx
        _accumulate(dg_ref, i == 0, jnp.sum(dg_rows, axis=0, keepdims=True))

    row = lambda i: (i, 0)
    const = lambda i: (0, 0)
    return _call(
        "in_proj_bwd",
        body,
        (T // tm,),
        [
            pl.BlockSpec((tm, CI), row),
            pl.BlockSpec((CI, D), const),
            pl.BlockSpec((tm, D), row),
            pl.BlockSpec((tm, D), row),
            pl.BlockSpec((1, D), const),
        ],
        [pl.BlockSpec((tm, D), row), pl.BlockSpec((1, D), const)],
        [jax.ShapeDtypeStruct((T, D), F32), jax.ShapeDtypeStruct((1, D), F32)],
        (dz_b, w_inT_b, x, dh1, g_mix),
        after=after,
    )


def _weight_grad(name, a_b, b_b, after=()):
    T, N1 = a_b.shape
    N2 = b_b.shape[1]
    t1 = _tile(N1, 1408, LANES)
    tk = _tile(T, 2048)
    nk = T // tk

    def body(a_ref, b_ref, o_ref, acc):
        k = pl.program_id(1)
        _accumulate(acc, k == 0, _dot(a_ref[...], b_ref[...], TN))

        @pl.when(k == nk - 1)
        def _():
            o_ref[...] = acc[...].astype(BF16)

    (out,), _ = _call(
        name,
        body,
        (N1 // t1, nk),
        [pl.BlockSpec((tk, t1), lambda n, k: (k, n)), pl.BlockSpec((tk, N2), lambda n, k: (k, 0))],
        [pl.BlockSpec((t1, N2), lambda n, k: (n, 0))],
        [jax.ShapeDtypeStruct((N1, N2), BF16)],
        (a_b, b_b),
        scratch=[pltpu.VMEM((t1, N2), F32)],
        after=after,
    )
    return out


def _sum_parts(name, full, how, parts, me):
    _, R, C = parts[0].shape
    tr = _tile(R, 1024)
    nb = R // tr
    where = [(q, r) for q, p in enumerate(parts) for r in range(p.shape[0])]
    assert len(where) == 3

    def body(me_ref, own_ref, *refs):
        o_ref = refs[-1]
        f = lambda j: refs[where[j][0]][where[j][1]].astype(F32)
        o_ref[...] = (own_ref[...].astype(F32) + f(0)) + (f(1) + f(2))

    own_map = {"rows": lambda i, me_ref: (me_ref[0] * nb + i, 0), "all": lambda i, me_ref: (i, 0)}[how]
    return pl.pallas_call(
        body,
        name=name,
        grid_spec=pltpu.PrefetchScalarGridSpec(
            num_scalar_prefetch=1,
            grid=(nb,),
            in_specs=[pl.BlockSpec((tr, C), own_map)]
            + [pl.BlockSpec((p.shape[0], tr, C), lambda i, me_ref: (0, i, 0)) for p in parts],
            out_specs=pl.BlockSpec((tr, C), lambda i, me_ref: (i, 0)),
        ),
        out_shape=jax.ShapeDtypeStruct((R, C), F32),
        compiler_params=pltpu.CompilerParams(dimension_semantics=("arbitrary",), vmem_limit_bytes=VMEM_LIMIT),
    )(me, full, *parts)


_M_CORR = 1.0 - ADAM_B1**ADAM_STEP
_V_CORR = 1.0 - ADAM_B2**ADAM_STEP


def _adamw_math(w, g, m, v):
    m = ADAM_B1 * m + (1.0 - ADAM_B1) * g
    v = ADAM_B2 * v + (1.0 - ADAM_B2) * (g * g)
    delta = -ADAM_LR * ((m / _M_CORR) / (jnp.sqrt(v / _V_CORR) + ADAM_EPS) + ADAM_WD * w)
    return delta, m, v


def _adamw(name, w, m, v, g_here, g_there, g_transposed=False):
    R, C = w.shape
    tr = _tile(R, 256, LANES if g_transposed else 8)

    def body(w_ref, m_ref, v_ref, ga_ref, gb_ref, g_ref, d_ref, nm_ref, nv_ref):
        g = ga_ref[...] + gb_ref[...]
        if g_transposed:
            g = g.T
        g_ref[...] = g
        d_ref[...], nm_ref[...], nv_ref[...] = _adamw_math(w_ref[...], g, m_ref[...], v_ref[...])

    spec = pl.BlockSpec((tr, C), lambda i: (i, 0))
    gspec = pl.BlockSpec((C, tr), lambda i: (0, i)) if g_transposed else spec
    return _call(name, body, (R // tr,), [spec] * 3 + [gspec] * 2, [spec] * 4, [jax.ShapeDtypeStruct((R, C), F32)] * 4,
                 (w, m, v, g_here, g_there))


def _adamw_on_sparsecore(name, w, m, v, g_here, g_there, after):
    R, C = w.shape
    n_groups = R // SUBLANES
    n_turns = -(-n_groups // SC_TILES)
    n_in, n_out = 5, 4

    def body(w_hbm, m_hbm, v_hbm, ga_hbm, gb_hbm, after_hbm, g_out, d_out, nm_out, nv_out, bufs, sems):
        tile = lax.axis_index("subcore") * SC_CORES + lax.axis_index("sparsecore")
        srcs = (w_hbm, m_hbm, v_hbm, ga_hbm, gb_hbm)
        dsts = (d_out, nm_out, nv_out, g_out)

        def rows(turn):
            return pl.ds((tile + turn * SC_TILES) * SUBLANES, SUBLANES)

        def loads(turn):
            slot = turn % 2
            return [pltpu.make_async_copy(srcs[q].at[rows(turn), :], bufs.at[slot, q], sems.at[slot, q]) for q in range(n_in)]

        def stores(turn):
            slot = turn % 2
            return [pltpu.make_async_copy(bufs.at[slot, q], dsts[q].at[rows(turn), :], sems.at[slot, n_in + q])
                    for q in range(n_out)]

        def when_mine(turn, fn):
            pl.when(tile + turn * SC_TILES < n_groups)(fn)

        def compute(slot):
            wb, mb, vb, gab, gbb = (bufs.at[slot, q] for q in range(n_in))

            @pl.loop(0, SUBLANES)
            def _(r):
                @pl.loop(0, C, step=SC_LANES)
                def _(i):
                    at = (r, pl.ds(i, SC_LANES))
                    g = gab[at] + gbb[at]
                    delta, new_m, new_v = _adamw_math(wb[at], g, mb[at], vb[at])
                    gab[at], wb[at], mb[at], vb[at] = g, delta, new_m, new_v

        def start_loads(turn):
            def fn():
                for cp in loads(turn):
                    cp.start()

            when_mine(turn, fn)

        start_loads(0)
        for turn in range(n_turns):
            def step(turn=turn):
                for cp in loads(turn):
                    cp.wait()
                if turn >= 1:
                    for cp in stores(turn - 1):
                        cp.wait()
                if turn + 1 < n_turns:
                    start_loads(turn + 1)
                compute(turn % 2)
                for cp in stores(turn):
                    cp.start()

            when_mine(turn, step)
        for turn in range(n_turns):
            def drain(turn=turn):
                for cp in stores(turn):
                    cp.wait()

            last_mine = jnp.logical_and(tile + turn * SC_TILES < n_groups, tile + (turn + 1) * SC_TILES >= n_groups)
            pl.when(last_mine)(drain)

    return pl.kernel(
        body,
        name=name,
        out_type=[jax.ShapeDtypeStruct((R, C), F32)] * 4,
        mesh=plsc.VectorSubcoreMesh(core_axis_name="sparsecore", subcore_axis_name="subcore"),
        scratch_types=[pltpu.VMEM((2, n_in, SUBLANES, C), F32), pltpu.SemaphoreType.DMA((2, n_in + n_out))],
        compiler_params=pltpu.CompilerParams(use_tc_tiling_on_sc=True),
    )(w, m, v, g_here, g_there, after)


class _PackLayout:
    def __init__(self, n_cc, n_grp, G, widths):
        self.dw_rows = (0, HALO)
        self.wp_rows = (HALO, HALO + G)
        self.n_cc, self.n_grp, self.G = n_cc, n_grp, G
        self.vec = {}
        r = HALO + G
        for name, width in widths:
            self.vec[name] = (r, width)
            r += width // PACK_W
        self.rows = -(-r // 8) * 8


def _pack_small(layout, dwdw, dwp, vecs):
    names = list(vecs)

    def body(*refs):
        dw_ref, wp_ref = refs[0], refs[1]
        vec_refs = refs[2 : 2 + len(names)]
        o_ref = refs[-1]
        o_ref[...] = jnp.zeros_like(o_ref)
        for j in range(layout.n_cc):
            o_ref[layout.dw_rows[0] : layout.dw_rows[1], j * LANES : (j + 1) * LANES] = dw_ref[j]
        for i in range(layout.n_grp):
            o_ref[layout.wp_rows[0] : layout.wp_rows[1], i * layout.G : (i + 1) * layout.G] = wp_ref[i]
        for name, ref in zip(names, vec_refs):
            r, width = layout.vec[name]
            for h in range(width // PACK_W):
                o_ref[r + h : r + h + 1, :] = ref[:, h * PACK_W : (h + 1) * PACK_W]

    return pl.pallas_call(
        body,
        name="pack_small",
        out_shape=jax.ShapeDtypeStruct((layout.rows, PACK_W), F32),
    )(dwdw, dwp, *[vecs[k] for k in names])


def _adamw_small(layout, g_here, g_there, w_dw, m_dw, v_dw, w_pool, m_pool, v_pool, vec_w, vec_m, vec_v, row):
    names = list(vec_w)
    nv = len(names)

    def body(*refs):
        ga_ref, gb_ref = refs[0], refs[1]
        wdw, mdw, vdw, wp, mp, vp = refs[2:8]
        vw, vm, vv = refs[8 : 8 + nv], refs[8 + nv : 8 + 2 * nv], refs[8 + 2 * nv : 8 + 3 * nv]
        row_g, row_w, row_m, row_v = refs[8 + 3 * nv : 12 + 3 * nv]
        outs = refs[12 + 3 * nv :]
        acc = outs[-1]
        acc[...] = ga_ref[...] + gb_ref[...]

        def emit(o, g, w, m, v, idx=()):
            res = (g,) + _adamw_math(w, g, m, v)
            for ref, val in zip(o, res):
                ref[idx] = val

        me = 2 * lax.axis_index("x") + lax.axis_index("y")
        for j in range(layout.n_cc):

            @pl.when(me == j)
            def _(j=j):
                for k in range(wdw.shape[0]):
                    g = acc[layout.dw_rows[0] + k : layout.dw_rows[0] + k + 1, j * LANES : (j + 1) * LANES]
                    emit(outs[0:4], g, wdw[k], mdw[k], vdw[k], idx=k)

        for i in range(layout.n_grp):
            g = acc[layout.wp_rows[0] : layout.wp_rows[1], i * layout.G : (i + 1) * layout.G]
            emit(outs[4:8], g, wp[i], mp[i], vp[i], idx=i)
        for q, name in enumerate(names):
            r, width = layout.vec[name]
            for h in range(width // PACK_W):
                ls = slice(h * PACK_W, (h + 1) * PACK_W)
                g = acc[r + h : r + h + 1, :]
                emit(outs[8 + 4 * q : 12 + 4 * q], g, vw[q][:, ls], vm[q][:, ls], vv[q][:, ls], idx=(slice(None), ls))
        emit(outs[8 + 4 * nv : 12 + 4 * nv], row_g[...], row_w[...], row_m[...], row_v[...], idx=...)

    shapes = [w_dw.shape] * 4 + [w_pool.shape] * 4
    for name in names:
        shapes += [vec_w[name].shape] * 4
    shapes += [row[1].shape] * 4
    return pl.pallas_call(
        body,
        name="adamw_small",
        out_shape=[jax.ShapeDtypeStruct(s, F32) for s in shapes],
        scratch_shapes=[pltpu.VMEM(g_here.shape, F32)],
    )(g_here, g_there, w_dw, m_dw, v_dw, w_pool, m_pool, v_pool,
      *[vec_w[k] for k in names], *[vec_m[k] for k in names], *[vec_v[k] for k in names], *row)


def _allreduce_rows(g_part, loss_part, comm=()):
    n_pairs = N_DEV - 1

    def body(g_ref, l_ref, go_ref, lo_ref, land_g, land_l, sems):
        x, y, c = _place()
        copies = []
        for q, (src, land) in enumerate(((g_ref, land_g), (l_ref, land_l))):
            for r in range(1, N_DEV):
                fx, fy, fc = (r >> 2) & 1, (r >> 1) & 1, r & 1
                peer = (1 - x if fx else x, 1 - y if fy else y, 1 - c if fc else c)
                cp = _remote(src, land.at[r], sems, 2 * (q * n_pairs + r - 1), peer)
                cp.start()
                copies.append(cp)
        for cp in copies:
            cp.wait()

        def total(src, land):
            row = lambda r: src[...] if r == 0 else land[r]
            return ((row(0) + row(4)) + (row(2) + row(6))) + ((row(1) + row(5)) + (row(3) + row(7)))

        go_ref[...] = total(g_ref, land_g)
        lo_ref[...] = total(l_ref, land_l)

    vm = pl.BlockSpec(memory_space=pltpu.VMEM)
    return _call(
        "allreduce_rows",
        body,
        (),
        [vm] * 2,
        [vm] * 2,
        [jax.ShapeDtypeStruct(g_part.shape, F32), jax.ShapeDtypeStruct(loss_part.shape, F32)],
        (g_part, loss_part),
        scratch=[pltpu.VMEM((N_DEV,) + g_part.shape, F32), pltpu.VMEM((N_DEV,) + loss_part.shape, F32),
                 pltpu.SemaphoreType.DMA((4 * n_pairs,))],
        comm=comm,
    )


def kernel(x, g_mix, w_in, b_in, w_dw, b_dw, ln_g, ln_b, w_pool, s_pool, w_out, g_ffn, w_gate, w_up, w_down, g_final, loss_target, m_g_mix, m_w_in, m_b_in, m_w_dw, m_b_dw, m_ln_g, m_ln_b, m_w_pool, m_s_pool, m_w_out, m_g_ffn, m_w_gate, m_w_up, m_w_down, m_g_final, v_g_mix, v_w_in, v_b_in, v_w_dw, v_b_dw, v_ln_g, v_ln_b, v_w_pool, v_s_pool, v_w_out, v_g_ffn, v_w_gate, v_w_up, v_w_down, v_g_final):
    x2 = x[0]
    target = loss_target[0]
    T, D = x2.shape
    w_in2, w_out2, w_down2 = w_in[0], w_out[0], w_down[0]
    taps_first = lambda a: jnp.transpose(a, (1, 0, 2))
    w_dw3 = taps_first(w_dw)
    w_gateT, w_upT = w_gate[0].T, w_up[0].T
    CI = w_in2.shape[1] * N_CHIPS
    DM = w_out2.shape[0] * N_CHIPS
    F = w_down2.shape[0] * N_CHIPS
    KW, _, dw_cols = w_dw3.shape
    assert dw_cols == LANES
    n_grp, G = w_pool.shape[1], w_pool.shape[-1]
    w_pool3 = w_pool[0]
    g_final2 = g_final.reshape(1, D)

    me = (2 * lax.axis_index("x") + lax.axis_index("y")).astype(jnp.int32).reshape(1)

    w_inT_b, w_dw4, f_out, f_gate, f_up, f_down = _place_and_gather(
        [(w_in2, "rows", (CI, D), BF16, True, True), (w_dw3, "lead", (N_CHIPS, KW, 1, dw_cols), F32, False, False)],
        [(w, "rows", shape, BF16, False, True)
         for w, shape in ((w_out2, (DM, D)), (w_gateT, (F, D)), (w_upT, (F, D)), (w_down2, (F, D)))])
    ici = lambda f: _GatherIci([f], ["rows"], [True])
    d2d = lambda f: _GatherD2d([f], ["rows"])
    gather = _start("gather_start", [ici(f_out), ici(f_gate), ici(f_up), ici(f_down)])
    (z, xn_b), _ = _in_proj(x2, g_mix, w_inT_b, b_in, after=[gather.token])
    (f_out,) = _wait("gather_out_wait", gather, 0, xn_b)
    s_out = _start("share_out_start", [d2d(f_out)], sibling_only=True)
    (y_b, v), _ = _seq_fwd(z, w_dw4, b_dw, ln_g, ln_b, w_pool3, s_pool, after=[s_out.token])
    (w_out_b,) = _wait("share_out_wait", s_out, 0, y_b)
    (f_gate,) = _wait("gather_gate_wait", gather, 1, y_b)
    s_gate = _start("share_gate_start", [d2d(f_gate)], sibling_only=True)
    (h1, hn_b), _ = _out_proj(y_b, x2, w_out_b, g_ffn, after=[s_gate.token])
    (f_up,) = _wait("gather_up_wait", gather, 2, hn_b)
    s_up = _start("share_up_start", [d2d(f_up)], sibling_only=True)
    (wgT_b,) = _wait("share_gate_wait", s_gate, 0, hn_b)
    (wuT_b,) = _wait("share_up_wait", s_up, 0, hn_b)
    (silu_b, uds_b, a_b), _ = _gate_up(hn_b, wgT_b, wuT_b)
    (f_down,) = _wait("gather_down_wait", gather, 3, a_b)
    s_down = _start("share_down_start", [d2d(f_down)], sibling_only=True)
    (wd_b,) = _wait("share_down_wait", s_down, 0, a_b)
    (dh2, dh2_b, loss_part, d_g_final), _ = _down_loss(a_b, wd_b, h1, target, g_final2)

    gw_down = _weight_grad("grad_w_down", a_b, dh2_b)
    x_down = _start("scatter_down_start", [_Scatter([gw_down], ["rows"])])
    (dg_b, du_b), _ = _ffn_bwd_act(dh2_b, wd_b, silu_b, uds_b, after=[x_down.token])
    gw_gateT = _weight_grad("grad_w_gate", dg_b, hn_b)
    gw_upT = _weight_grad("grad_w_up", du_b, hn_b)
    gw_down, p_down = _wait("scatter_down_wait", x_down, 0, gw_upT)
    sum_down = _sum_parts("sum_w_down", gw_down, "rows", [p_down], me)
    (dh1, dh1_b, dy, d_g_ffn), (p_gate, oth_down) = _ffn_bwd_in(
        dg_b, du_b, wgT_b, wuT_b, h1, dh2, g_ffn, w_out_b, comm=[_Scatter([gw_gateT], ["rows"]), _Swap([sum_down])])
    gw_out = _weight_grad("grad_w_out", y_b, dh1_b)
    sum_gate = _sum_parts("sum_w_gate", gw_gateT, "rows", [p_gate], me)
    res = {}
    res["w_down"] = _adamw_on_sparsecore("adamw_w_down", w_down2, m_w_down[0], v_w_down[0], sum_down, oth_down, sum_down)
    (dz_b, d_wdw, d_bdw, d_lng, d_lnb, d_wp, d_sp, d_bin), (p_up, p_out, oth_gate) = _seq_bwd(
        z, dy, v, w_dw4, ln_g, ln_b, w_pool3, s_pool,
        comm=[_Scatter([gw_upT, gw_out], ["rows", "rows"]), _Swap([sum_gate])])
    res["w_gate"] = _adamw_on_sparsecore(
        "adamw_w_gate", w_gateT, m_w_gate[0].T, v_w_gate[0].T, sum_gate, oth_gate, res["w_down"][0])
    vec_grads ={"b_dw": d_bdw, "ln_g": d_lng, "ln_b": d_lnb, "s_pool": d_sp, "g_ffn": d_g_ffn, "g_final": d_g_final, "b_in": d_bin}
    layout = _PackLayout(dw_cols * N_CHIPS // LANES, n_grp, G, [(k, a.shape[1]) for k, a in vec_grads.items()])
    pack = _pack_small(layout, d_wdw, d_wp, vec_grads)
    sum_up = _sum_parts("sum_w_up", gw_upT, "rows", [p_up], me)
    sum_out = _sum_parts("sum_w_out", gw_out, "rows", [p_out], me)
    mid = _start("mid_start", [_Swap([sum_up, sum_out]), _Scatter([pack], ["all"])])
    gw_inT = _weight_grad("grad_w_in", dz_b, xn_b, after=[mid.token])
    sum_up, sum_out, oth_up, oth_out = _wait("mid_swap_wait", mid, 0, gw_inT)
    late = _start("late_start", [_Scatter([gw_inT], ["rows"])])
    (grad_x, d_g_mix), _ = _in_proj_bwd(dz_b, w_inT_b, x2, dh1, g_mix, after=[late.token])
    pack, p_small = _wait("mid_small_wait", mid, 1, d_g_mix)
    gw_inT, p_in = _wait("late_w_in_wait", late, 0, d_g_mix)
    sum_small = _sum_parts("sum_small", pack, "all", [p_small], me)
    res["w_up"] = _adamw_on_sparsecore("adamw_w_up", w_upT, m_w_up[0].T, v_w_up[0].T, sum_up, oth_up, res["w_gate"][0])
    res["w_out"] = _adamw_on_sparsecore("adamw_w_out", w_out2, m_w_out[0], v_w_out[0], sum_out, oth_out, res["w_gate"][0])
    sum_in = _sum_parts("sum_w_in", gw_inT, "rows", [p_in], me)
    (d_g_mix, loss_row), (oth_in, oth_small) = _allreduce_rows(d_g_mix, loss_part, comm=[_Swap([sum_in, sum_small])])
    loss = loss_row[0, 0]
    res["w_in"], _ = _adamw("adamw_w_in", w_in2, m_w_in[0], v_w_in[0], sum_in, oth_in, g_transposed=True)

    vec_w = {"b_dw": b_dw, "ln_g": ln_g, "ln_b": ln_b, "s_pool": s_pool, "g_ffn": g_ffn, "g_final": g_final2, "b_in": b_in}
    vec_m = {"b_dw": m_b_dw, "ln_g": m_ln_g, "ln_b": m_ln_b, "s_pool": m_s_pool, "g_ffn": m_g_ffn,
             "g_final": m_g_final.reshape(1, D), "b_in": m_b_in}
    vec_v = {"b_dw": v_b_dw, "ln_g": v_ln_g, "ln_b": v_ln_b, "s_pool": v_s_pool, "g_ffn": v_g_ffn,
             "g_final": v_g_final.reshape(1, D), "b_in": v_b_in}
    small = _adamw_small(layout, sum_small, oth_small, w_dw3, taps_first(m_w_dw), taps_first(v_w_dw),
                         w_pool3, m_w_pool[0], v_w_pool[0], vec_w, vec_m, vec_v, (d_g_mix, g_mix, m_g_mix, v_g_mix))
    res["w_dw"] = [taps_first(a) for a in small[0:4]]
    res["w_pool"] = [a[None] for a in small[4:8]]
    for q, k in enumerate(vec_w):
        res[k] = list(small[8 + 4 * q : 12 + 4 * q])
    res["g_mix"] = list(small[-4:])
    res["g_final"] = [a.reshape(D) for a in res["g_final"]]
    for k in ("w_in", "w_out", "w_down"):
        res[k] = [a[None] for a in res[k]]
    for k in ("w_gate", "w_up"):
        res[k] = [a.T[None] for a in res[k]]

    order = ["g_mix", "w_in", "b_in", "w_dw", "b_dw", "ln_g", "ln_b", "w_pool", "s_pool", "w_out", "g_ffn", "w_gate", "w_up", "w_down", "g_final"]
    outs = [loss, grad_x[None]]
    for q in range(4):
        outs += [res[k][q] for k in order]
    return tuple(outs)
```

```python
import jax
import jax.numpy as jnp
from jax import lax
from jax.experimental import pallas as pl
from jax.experimental.pallas import tpu as pltpu
from jax.experimental.pallas import tpu_sc as plsc

F32 = jnp.float32
BF16 = jnp.bfloat16
MESH = pl.DeviceIdType.MESH
ANY = pl.BlockSpec(memory_space=pl.ANY)

RMS_EPS = 1e-6
LN_EPS = 1e-5
POOL_WINDOWS = (2, 4, 8, 16)
ADAM_LR = 0.001
ADAM_B1 = 0.9
ADAM_B2 = 0.999
ADAM_EPS = 1e-08
ADAM_WD = 0.01
ADAM_STEP = 10

LANES = 128
SUBLANES = 8
BF16_ROWS = 16
HALO = 32
CONV_ROWS = 64
HIDDEN_CHUNK = 512
VMEM_LIMIT = 56 * 1024 * 1024
PACK_W = 512
N_CHIPS = 4
N_DEV = 8
SIBLING_BARRIER_ID = 0
SC_CORES = 2
SC_TILES = 32
SC_LANES = 16


def _tile(n, want, mult=8):
    t = min(n, want)
    while n % t or t % mult:
        t -= 1
    return t


def _sigmoid(x):
    return 1.0 / (1.0 + jnp.exp(-x))


def _dot(a, b, dims):
    return lax.dot_general(a, b, (dims, ((), ())), preferred_element_type=F32)


NN = ((1,), (0,))
NT = ((1,), (1,))
TN = ((0,), (0,))


def _rms_bwd(x, g, dy):
    r = lax.rsqrt(jnp.mean(x * x, axis=-1, keepdims=True) + RMS_EPS)
    xh = x * r
    gy = dy * g
    dx = r * (gy - xh * jnp.mean(gy * xh, axis=-1, keepdims=True))
    return dx, dy * xh


def _accumulate(ref, first, val):
    @pl.when(first)
    def _():
        ref[...] = val

    @pl.when(jnp.logical_not(first))
    def _():
        ref[...] += val


def _place():
    return lax.axis_index("x"), lax.axis_index("y"), lax.axis_index("c")


def _other_chips(x, y):
    return [(1 - x, y), (x, 1 - y), (1 - x, 1 - y)]


def _rows(ref, start, n):
    return ref.at[pl.ds(pl.multiple_of(start, BF16_ROWS), n)]


def _window(ref, how, k, c=None):
    if how == "all":
        return ref
    if how == "lead":
        return ref.at[k]
    assert how == "rows"
    n = ref.shape[0] // N_CHIPS
    if c is None:
        return _rows(ref, k * n, n)
    return _rows(ref, k * n + c * (n // 2), n // 2)


def _remote(src, dst, sems, s, device):
    return pltpu.make_async_remote_copy(
        src_ref=src, dst_ref=dst, send_sem=sems.at[s], recv_sem=sems.at[s + 1], device_id=device, device_id_type=MESH)


class _GatherIci:
    aliased = True

    def __init__(self, fulls, hows, splits):
        self.fulls, self.hows, self.splits = list(fulls), list(hows), list(splits)

    def inputs(self):
        return self.fulls

    def out_shapes(self):
        return [jax.ShapeDtypeStruct(a.shape, a.dtype) for a in self.fulls]

    def n_sems(self):
        return 6 * len(self.fulls)

    def build(self, ins, outs, sems, base):
        x, y, c = _place()
        me = 2 * x + y
        chips = _other_chips(x, y)
        starts, waits = [], []
        for a, (how, sp) in enumerate(zip(self.hows, self.splits)):
            half = c if sp else None
            mine = _window(outs[a], how, me, half)
            for j, (px, py) in enumerate(chips):
                s = base + 6 * a + 2 * j
                cp = _remote(mine, mine, sems, s, (px, py, c))
                landing = _remote(mine, _window(outs[a], how, 2 * px + py, half), sems, s, (px, py, c))
                starts.append(cp.start)
                waits += [landing.wait_recv, cp.wait_send]
        return starts, waits


class _GatherD2d:
    aliased = True

    def __init__(self, fulls, hows):
        self.fulls, self.hows = list(fulls), list(hows)

    def inputs(self):
        return self.fulls

    def out_shapes(self):
        return [jax.ShapeDtypeStruct(a.shape, a.dtype) for a in self.fulls]

    def n_sems(self):
        return 6 * len(self.fulls)

    def build(self, ins, outs, sems, base):
        x, y, c = _place()
        starts, waits = [], []
        for a, how in enumerate(self.hows):
            for j, (px, py) in enumerate(_other_chips(x, y)):
                s = base + 6 * a + 2 * j
                got = _window(outs[a], how, 2 * px + py, c)
                cp = _remote(got, got, sems, s, (x, y, 1 - c))
                landing = _remote(got, _window(outs[a], how, 2 * px + py, 1 - c), sems, s, (x, y, 1 - c))
                starts.append(cp.start)
                waits += [landing.wait_recv, cp.wait_send]
        return starts, waits


def _part_shape(a, how):
    if how == "all":
        return a.shape
    assert how == "rows"
    return (a.shape[0] // N_CHIPS, a.shape[1])


class _Scatter:
    aliased = False

    def __init__(self, fulls, hows):
        self.fulls, self.hows = list(fulls), list(hows)

    def inputs(self):
        return self.fulls

    def out_shapes(self):
        return [jax.ShapeDtypeStruct((N_CHIPS - 1,) + _part_shape(a, h), a.dtype) for a, h in zip(self.fulls, self.hows)]

    def n_sems(self):
        return 6 * len(self.fulls)

    def build(self, ins, outs, sems, base):
        x, y, c = _place()
        chips = _other_chips(x, y)
        starts, waits = [], []
        for a, how in enumerate(self.hows):
            for j, (px, py) in enumerate(chips):
                cp = _remote(_window(ins[a], how, 2 * px + py), outs[a].at[j], sems, base + 6 * a + 2 * j, (px, py, c))
                starts.append(cp.start)
                waits += [cp.wait_recv, cp.wait_send]
        return starts, waits


class _Swap:
    aliased = False

    def __init__(self, arrays):
        self.arrays = list(arrays)

    def inputs(self):
        return self.arrays

    def out_shapes(self):
        return [jax.ShapeDtypeStruct(a.shape, a.dtype) for a in self.arrays]

    def n_sems(self):
        return 2 * len(self.arrays)

    def build(self, ins, outs, sems, base):
        x, y, c = _place()
        starts, waits = [], []
        for a in range(len(ins)):
            cp = _remote(ins[a], outs[a], sems, base + 2 * a, (x, y, 1 - c))
            starts.append(cp.start)
            waits += [cp.wait_recv, cp.wait_send]
        return starts, waits


def _call(name, body, grid, in_specs, out_specs, out_shape, args, scratch=(), comm=(), after=()):
    comm, after = list(comm), list(after)
    n_in, n_out, n_scr, n_after = len(args), len(out_shape), len(scratch), len(after)
    c_in = [a for op in comm for a in op.inputs()]
    c_out = [s for op in comm for s in op.out_shapes()]
    n_sems = sum(op.n_sems() for op in comm)
    aliases, i_in, i_out = {}, 0, 0
    for op in comm:
        if op.aliased:
            for q in range(len(op.inputs())):
                aliases[n_in + n_after + i_in + q] = n_out + i_out + q
        i_in, i_out = i_in + len(op.inputs()), i_out + len(op.out_shapes())

    def wrapped(*refs):
        ins = refs[:n_in]
        cin = refs[n_in + n_after : n_in + n_after + len(c_in)]
        o0 = n_in + n_after + len(c_in)
        outs = refs[o0 : o0 + n_out]
        cout = refs[o0 + n_out : o0 + n_out + len(c_out)]
        s0 = o0 + n_out + len(c_out)
        scr = refs[s0 : s0 + n_scr]

        def copies():
            sems = refs[s0 + n_scr]
            starts, waits = [], []
            i_in = i_out = base = 0
            for op in comm:
                ni, no = len(op.inputs()), len(op.out_shapes())
                s, w = op.build(cin[i_in : i_in + ni], cout[i_out : i_out + no], sems, base)
                starts += s
                waits += w
                i_in, i_out, base = i_in + ni, i_out + no, base + op.n_sems()
            return starts, waits

        def run_starts():
            for start in copies()[0]:
                start()

        def run_waits():
            for wait in copies()[1]:
                wait()

        if comm and grid:
            first = last = True
            for d, n in enumerate(grid):
                first = jnp.logical_and(first, pl.program_id(d) == 0)
                last = jnp.logical_and(last, pl.program_id(d) == n - 1)
            pl.when(first)(run_starts)
        elif comm:
            run_starts()
        if body is not None:
            body(*ins, *outs, *scr)
        if comm and grid:
            pl.when(last)(run_waits)
        elif comm:
            run_waits()

    res = pl.pallas_call(
        wrapped,
        name=name,
        grid=grid,
        in_specs=list(in_specs) + [ANY] * (n_after + len(c_in)),
        out_specs=list(out_specs) + [ANY] * len(c_out),
        out_shape=list(out_shape) + c_out,
        scratch_shapes=list(scratch) + ([pltpu.SemaphoreType.DMA((n_sems,))] if comm else []),
        input_output_aliases=aliases,
        compiler_params=pltpu.CompilerParams(dimension_semantics=("arbitrary",) * len(grid), vmem_limit_bytes=VMEM_LIMIT),
    )(*args, *after, *c_in)
    return tuple(res[:n_out]), tuple(res[n_out:])


def _place_and_gather(now, later):
    items = list(now) + list(later)
    n, n_now = len(items), len(now)
    buf_shape = lambda it: it[0].shape[::-1] if it[4] else it[0].shape
    split_now = [a for a in range(n_now) if items[a][5]]

    def body(*refs):
        ins, outs = refs[:n], refs[n : 2 * n]
        stage, bufs = refs[2 * n : 3 * n - n_now], refs[3 * n - n_now : 4 * n - n_now]
        sems = refs[4 * n - n_now]
        x, y, c = _place()
        me = 2 * x + y
        chips = _other_chips(x, y)
        loads = [pltpu.make_async_copy(ins[a], stage[a - n_now], sems.at[a]) for a in range(n_now, n)]
        for ld in loads:
            ld.start()
        pending = []

        def place(a, val):
            _, how, _, dtype, transposed, _ = items[a]
            bufs[a][...] = (val.T if transposed else val).astype(dtype)
            cp = pltpu.make_async_copy(bufs[a], _window(outs[a], how, me), sems.at[n + a])
            cp.start()
            pending.append(cp.wait)

        arrivals = []
        for a in range(n_now):
            place(a, ins[a][...])
            how, split = items[a][1], items[a][5]
            half = c if split else None
            src = _rows(bufs[a], c * (bufs[a].shape[0] // 2), bufs[a].shape[0] // 2) if split else bufs[a]
            for j, (px, py) in enumerate(chips):
                s = 2 * n + 6 * a + 2 * j
                cp = _remote(src, _window(outs[a], how, me, half), sems, s, (px, py, c))
                landing = _remote(src, _window(outs[a], how, 2 * px + py, half), sems, s, (px, py, c))
                cp.start()
                arrivals.append(landing.wait_recv)
                pending.append(cp.wait_send)
        for a in range(n_now, n):
            loads[a - n_now].wait()
            place(a, stage[a - n_now][...])
        for wait in arrivals:
            wait()
        d2d = _GatherD2d([None] * len(split_now), [items[a][1] for a in split_now])
        starts, waits = d2d.build(None, [outs[a] for a in split_now], sems, 2 * n + 6 * n_now)
        for start in starts:
            start()
        for wait in waits + pending:
            wait()

    vm = pl.BlockSpec(memory_space=pltpu.VMEM)
    return pl.pallas_call(
        body,
        name="place_and_gather",
        in_specs=[vm] * n_now + [ANY] * (n - n_now),
        out_specs=[ANY] * n,
        out_shape=[jax.ShapeDtypeStruct(it[2], it[3]) for it in items],
        scratch_shapes=[pltpu.VMEM(it[0].shape, it[0].dtype) for it in later]
        + [pltpu.VMEM(buf_shape(it), it[3]) for it in items]
        + [pltpu.SemaphoreType.DMA((2 * n + 6 * n_now + 6 * len(split_now),))],
        compiler_params=pltpu.CompilerParams(vmem_limit_bytes=VMEM_LIMIT),
    )(*[it[0] for it in items])


_HBM = pl.BlockSpec(memory_space=pltpu.HBM)
_SEM = pl.BlockSpec(memory_space=pltpu.SEMAPHORE)
_DATAFLOW = pltpu.SideEffectType.DATAFLOW_SIDE_EFFECTING


class _Pending:
    def __init__(self, ops, bases, sems, arrays, token):
        self.ops, self.bases, self.sems, self.arrays, self.token = ops, bases, sems, arrays, token


def _op_refs(op, refs):
    n_src = len(op.inputs())
    return refs[:n_src], (refs[:n_src] if op.aliased else refs[n_src:])


def _start(name, ops, sibling_only=False):
    per_op = [list(op.inputs()) + ([] if op.aliased else [lax.empty(sd.shape, sd.dtype) for sd in op.out_shapes()])
              for op in ops]
    arrays = [a for group in per_op for a in group]
    bases = [sum(op.n_sems() for op in ops[:k]) for k in range(len(ops))]
    n = len(arrays)

    def body(*refs):
        sems, token = refs[n], refs[-1]
        if sibling_only:
            x, y, c = _place()
            barrier = pltpu.get_barrier_semaphore()
            pl.semaphore_signal(barrier, inc=1, device_id=(x, y, 1 - c), device_id_type=MESH)
            pl.semaphore_wait(barrier, 1)
        at = 0
        for op, group, base in zip(ops, per_op, bases):
            starts, _ = op.build(*_op_refs(op, refs[at : at + len(group)]), sems, base)
            for start in starts:
                start()
            at += len(group)
        token[...] = jnp.zeros_like(token)

    res = pl.pallas_call(
        body,
        name=name,
        out_shape=(pltpu.SemaphoreType.DMA((sum(op.n_sems() for op in ops),)),)
        + tuple(pltpu.HBM(a.shape, a.dtype) for a in arrays) + (jax.ShapeDtypeStruct((SUBLANES, LANES), F32),),
        in_specs=(_HBM,) * n,
        out_specs=(_SEM,) + (_HBM,) * n + (pl.BlockSpec(memory_space=pltpu.VMEM),),
        input_output_aliases={i: 1 + i for i in range(n)},
        compiler_params=pltpu.CompilerParams(
            has_side_effects=_DATAFLOW, collective_id=SIBLING_BARRIER_ID if sibling_only else None),
    )(*[pltpu.with_memory_space_constraint(a, pltpu.HBM) for a in arrays])
    thru, at, groups = list(res[1 : 1 + n]), 0, []
    for group in per_op:
        groups.append(thru[at : at + len(group)])
        at += len(group)
    return _Pending(list(ops), bases, res[0], groups, res[-1])


def _wait(name, pending, k, after):
    op, arrays = pending.ops[k], pending.arrays[k]
    n = len(arrays)

    def body(*refs):
        _, waits = op.build(*_op_refs(op, refs[:n]), refs[n], pending.bases[k])
        for wait in waits:
            wait()

    return pl.pallas_call(
        body,
        name=name,
        out_shape=tuple(pltpu.HBM(a.shape, a.dtype) for a in arrays),
        in_specs=(_HBM,) * n + (_SEM, ANY),
        out_specs=(_HBM,) * n,
        input_output_aliases={i: i for i in range(n)},
        compiler_params=pltpu.CompilerParams(has_side_effects=_DATAFLOW),
    )(*arrays, pending.sems, after)


def _in_proj(x, g_mix, w_inT_b, b_in, after=()):
    T, D = x.shape
    CI = w_inT_b.shape[0]
    tm = _tile(T, 512)

    def body(x_ref, g_ref, w_ref, b_ref, z_ref, xn_ref):
        xv = x_ref[...]
        r = lax.rsqrt(jnp.mean(xv * xv, axis=-1, keepdims=True) + RMS_EPS)
        xn = (xv * r * g_ref[...]).astype(BF16)
        xn_ref[...] = xn
        z_ref[...] = _dot(xn, w_ref[...], NT) + b_ref[...]

    return _call(
        "in_proj",
        body,
        (T // tm,),
        [
            pl.BlockSpec((tm, D), lambda i: (i, 0)),
            pl.BlockSpec((1, D), lambda i: (0, 0)),
            pl.BlockSpec((CI, D), lambda i: (0, 0)),
            pl.BlockSpec((1, CI), lambda i: (0, 0)),
        ],
        [pl.BlockSpec((tm, CI), lambda i: (i, 0)), pl.BlockSpec((tm, D), lambda i: (i, 0))],
        [jax.ShapeDtypeStruct((T, CI), F32), jax.ShapeDtypeStruct((T, D), BF16)],
        (x, g_mix, w_inT_b, b_in),
        after=after,
    )


def _fill_shifted(scr):
    n = scr.shape[1] - SUBLANES
    for s in range(1, SUBLANES):
        scr[s, 0:n, :] = scr[0, s : s + n, :]


def _shifted_rows(scr, off, n, cs):
    s = off % SUBLANES
    return scr[s, off - s : off - s + n, cs]


def _pool_mean_minus_token(p_scr, cs, w, cnt, tt):
    tok = p_scr[HALO : HALO + tt, cs]
    s = tok
    for d in range(1, w):
        s = s + p_scr[HALO - d : HALO - d + tt, cs]
    return s / cnt - tok


def _seq_fwd(z, w_dw4, b_dw, ln_g, ln_b, w_pool, s_pool, after=()):
    T, CI = z.shape
    CC = ln_g.shape[1]
    n_grp, G = w_pool.shape[0], w_pool.shape[-1]
    KW = w_dw4.shape[1]
    D = CC + n_grp * G
    tt = _tile(T, 512, HALO)
    per = tt // HALO

    def body(zc_ref, zp_ref, wdw_ref, bdw_ref, lng_ref, lnb_ref, wp_ref, sp_ref, y_ref, v_ref, u_scr, p_scr):
        i = pl.program_id(0)
        first = i == 0
        u_prev = zp_ref[:, 0:CC] * _sigmoid(zp_ref[:, CC : 2 * CC])
        u_scr[0, 0:HALO, :] = jnp.where(first, 0.0, u_prev)
        p_scr[0:HALO, :] = jnp.where(first, 0.0, zp_ref[:, 2 * CC :])
        u_scr[0, HALO:, :] = zc_ref[:, 0:CC] * _sigmoid(zc_ref[:, CC : 2 * CC])
        p_scr[HALO:, :] = zc_ref[:, 2 * CC :]
        _fill_shifted(u_scr)

        for j in range(CC // LANES):
            cs = slice(LANES * j, LANES * (j + 1))
            for rb in range(tt // CONV_ROWS):
                acc = jnp.zeros((CONV_ROWS, LANES), F32)
                for k in range(KW):
                    off = HALO - (KW - 1) + k + rb * CONV_ROWS
                    acc = acc + _shifted_rows(u_scr, off, CONV_ROWS, cs) * wdw_ref[j, k]
                v_ref[rb * CONV_ROWS : (rb + 1) * CONV_ROWS, cs] = acc + bdw_ref[:, cs]

        v = v_ref[...]
        mu = jnp.mean(v, axis=-1, keepdims=True)
        d = v - mu
        var = jnp.mean(d * d, axis=-1, keepdims=True)
        ln = d * lax.rsqrt(var + LN_EPS) * lng_ref[...] + lnb_ref[...]
        y_ref[:, 0:CC] = (ln * _sigmoid(ln)).astype(BF16)

        tpos = i * tt + lax.broadcasted_iota(jnp.int32, (tt, 1), 0)
        for gi, w in enumerate(POOL_WINDOWS):
            cs = slice(G * gi, G * (gi + 1))
            cnt = jnp.minimum(tpos + 1, w).astype(F32)
            yi = _pool_mean_minus_token(p_scr, cs, w, cnt, tt)
            q = _dot(yi.astype(BF16), wp_ref[gi].astype(BF16), NN)
            y_ref[:, CC + G * gi : CC + G * (gi + 1)] = (q * sp_ref[:, cs]).astype(BF16)

    const2 = lambda i: (0, 0)
    return _call(
        "seq_fwd",
        body,
        (T // tt,),
        [
            pl.BlockSpec((tt, CI), lambda i: (i, 0)),
            pl.BlockSpec((HALO, CI), lambda i: (jnp.maximum(i * per - 1, 0), 0)),
            pl.BlockSpec(w_dw4.shape, lambda i: (0,) * w_dw4.ndim),
            pl.BlockSpec((1, CC), const2),
            pl.BlockSpec((1, CC), const2),
            pl.BlockSpec((1, CC), const2),
            pl.BlockSpec(w_pool.shape, lambda i: (0, 0, 0)),
            pl.BlockSpec((1, n_grp * G), const2),
        ],
        [pl.BlockSpec((tt, D), lambda i: (i, 0)), pl.BlockSpec((tt, CC), lambda i: (i, 0))],
        [jax.ShapeDtypeStruct((T, D), BF16), jax.ShapeDtypeStruct((T, CC), F32)],
        (z, z, w_dw4, b_dw, ln_g, ln_b, w_pool, s_pool),
        scratch=[pltpu.VMEM((SUBLANES, HALO + tt, CC), F32), pltpu.VMEM((HALO + tt, n_grp * G), F32)],
        after=after,
    )


def _out_proj(y_b, x, w_out_b, g_ffn, after=()):
    T, D = x.shape
    tm = _tile(T, 512)

    def body(y_ref, x_ref, w_ref, g_ref, h1_ref, hn_ref):
        h1 = x_ref[...] + _dot(y_ref[...], w_ref[...], NN)
        h1_ref[...] = h1
        r = lax.rsqrt(jnp.mean(h1 * h1, axis=-1, keepdims=True) + RMS_EPS)
        hn_ref[...] = (h1 * r * g_ref[...]).astype(BF16)

    row = lambda i: (i, 0)
    return _call(
        "out_proj",
        body,
        (T // tm,),
        [
            pl.BlockSpec((tm, y_b.shape[1]), row),
            pl.BlockSpec((tm, D), row),
            pl.BlockSpec(w_out_b.shape, lambda i: (0, 0)),
            pl.BlockSpec((1, D), lambda i: (0, 0)),
        ],
        [pl.BlockSpec((tm, D), row), pl.BlockSpec((tm, D), row)],
        [jax.ShapeDtypeStruct((T, D), F32), jax.ShapeDtypeStruct((T, D), BF16)],
        (y_b, x, w_out_b, g_ffn),
        after=after,
    )


def _hidden_tile(F):
    return _tile(F, 1408, LANES)


def _gate_up(hn_b, wgT_b, wuT_b):
    T, D = hn_b.shape
    F = wgT_b.shape[0]
    tm, tf = _tile(T, 1024), _hidden_tile(F)

    def body(hn_ref, wg_ref, wu_ref, silu_ref, uds_ref, a_ref):
        hn = hn_ref[...]
        for c0 in range(0, tf, HIDDEN_CHUNK):
            cs = slice(c0, min(c0 + HIDDEN_CHUNK, tf))
            gv = _dot(hn, wg_ref[cs, :], NT)
            uv = _dot(hn, wu_ref[cs, :], NT)
            sg = _sigmoid(gv)
            silu = gv * sg
            silu_ref[:, cs] = silu.astype(BF16)
            uds_ref[:, cs] = (uv * (sg * (1.0 + gv * (1.0 - sg)))).astype(BF16)
            a_ref[:, cs] = (silu * uv).astype(BF16)

    wspec = pl.BlockSpec((tf, D), lambda j, i: (j, 0))
    ospec = pl.BlockSpec((tm, tf), lambda j, i: (i, j))
    return _call(
        "gate_up",
        body,
        (F // tf, T // tm),
        [pl.BlockSpec((tm, D), lambda j, i: (i, 0)), wspec, wspec],
        [ospec, ospec, ospec],
        [jax.ShapeDtypeStruct((T, F), BF16)] * 3,
        (hn_b, wgT_b, wuT_b),
    )


def _down_loss(a_b, wd_b, h1, target, g_final):
    T, D = h1.shape
    F = a_b.shape[1]
    tm = _tile(T, 512)
    nt = T // tm

    def body(a_ref, w_ref, h1_ref, t_ref, g_ref, dh2_ref, dh2b_ref, loss_ref, dg_ref):
        i = pl.program_id(0)
        h2 = h1_ref[...] + _dot(a_ref[...], w_ref[...], NN)
        r = lax.rsqrt(jnp.mean(h2 * h2, axis=-1, keepdims=True) + RMS_EPS)
        g = g_ref[...]
        diff = h2 * r * g - t_ref[...]
        _accumulate(loss_ref, i == 0, jnp.full(loss_ref.shape, jnp.sum(diff * diff) * (0.5 / D), F32))
        dh2, dg_rows = _rms_bwd(h2, g, diff * (1.0 / D))
        dh2_ref[...] = dh2
        dh2b_ref[...] = dh2.astype(BF16)
        _accumulate(dg_ref, i == 0, jnp.sum(dg_rows, axis=0, keepdims=True))

    row = lambda i: (i, 0)
    return _call(
        "down_loss",
        body,
        (nt,),
        [
            pl.BlockSpec((tm, F), row),
            pl.BlockSpec((F, D), lambda i: (0, 0), pipeline_mode=pl.Buffered(1)),
            pl.BlockSpec((tm, D), row),
            pl.BlockSpec((tm, D), row),
            pl.BlockSpec((1, D), lambda i: (0, 0)),
        ],
        [
            pl.BlockSpec((tm, D), row),
            pl.BlockSpec((tm, D), row),
            pl.BlockSpec((1, LANES), lambda i: (0, 0)),
            pl.BlockSpec((1, D), lambda i: (0, 0)),
        ],
        [
            jax.ShapeDtypeStruct((T, D), F32),
            jax.ShapeDtypeStruct((T, D), BF16),
            jax.ShapeDtypeStruct((1, LANES), F32),
            jax.ShapeDtypeStruct((1, D), F32),
        ],
        (a_b, wd_b, h1, target, g_final),
    )


def _ffn_bwd_act(dh2_b, wd_b, silu_b, uds_b, after=()):
    T, D = dh2_b.shape
    F = wd_b.shape[0]
    tm, tf = _tile(T, 1024), _hidden_tile(F)

    def body(d_ref, w_ref, silu_ref, uds_ref, dg_ref, du_ref):
        d = d_ref[...]
        for c0 in range(0, tf, HIDDEN_CHUNK):
            cs = slice(c0, min(c0 + HIDDEN_CHUNK, tf))
            da = _dot(d, w_ref[cs, :], NT)
            dg_ref[:, cs] = (da * uds_ref[:, cs].astype(F32)).astype(BF16)
            du_ref[:, cs] = (da * silu_ref[:, cs].astype(F32)).astype(BF16)

    aspec = pl.BlockSpec((tm, tf), lambda j, i: (i, j))
    return _call(
        "ffn_bwd_act",
        body,
        (F // tf, T // tm),
        [pl.BlockSpec((tm, D), lambda j, i: (i, 0)), pl.BlockSpec((tf, D), lambda j, i: (j, 0)), aspec, aspec],
        [aspec, aspec],
        [jax.ShapeDtypeStruct((T, F), BF16)] * 2,
        (dh2_b, wd_b, silu_b, uds_b),
        after=after,
    )


def _ffn_bwd_in(dg_b, du_b, wgT_b, wuT_b, h1, dh2, g_ffn, w_out_b, comm=()):
    T, D = h1.shape
    F = wgT_b.shape[0]
    DM = w_out_b.shape[0]
    tm = _tile(T, 512)

    def body(dg_ref, du_ref, wg_ref, wu_ref, h1_ref, dh2_ref, g_ref, wo_ref, dh1_ref, dh1b_ref, dy_ref, dgf_ref):
        i = pl.program_id(0)
        dhn = _dot(dg_ref[...], wg_ref[...], NN) + _dot(du_ref[...], wu_ref[...], NN)
        dx, dg_rows = _rms_bwd(h1_ref[...], g_ref[...], dhn)
        dh1 = dh2_ref[...] + dx
        dh1b = dh1.astype(BF16)
        dh1_ref[...] = dh1
        dh1b_ref[...] = dh1b
        dy_ref[...] = _dot(dh1b, wo_ref[...], NT)
        _accumulate(dgf_ref, i == 0, jnp.sum(dg_rows, axis=0, keepdims=True))

    row = lambda i: (i, 0)
    const = lambda i: (0, 0)
    return _call(
        "ffn_bwd_in",
        body,
        (T // tm,),
        [
            pl.BlockSpec((tm, F), row),
            pl.BlockSpec((tm, F), row),
            pl.BlockSpec((F, D), const, pipeline_mode=pl.Buffered(1)),
            pl.BlockSpec((F, D), const, pipeline_mode=pl.Buffered(1)),
            pl.BlockSpec((tm, D), row),
            pl.BlockSpec((tm, D), row),
            pl.BlockSpec((1, D), const),
            pl.BlockSpec((DM, D), const, pipeline_mode=pl.Buffered(1)),
        ],
        [pl.BlockSpec((tm, D), row), pl.BlockSpec((tm, D), row), pl.BlockSpec((tm, DM), row), pl.BlockSpec((1, D), const)],
        [
            jax.ShapeDtypeStruct((T, D), F32),
            jax.ShapeDtypeStruct((T, D), BF16),
            jax.ShapeDtypeStruct((T, DM), F32),
            jax.ShapeDtypeStruct((1, D), F32),
        ],
        (dg_b, du_b, wgT_b, wuT_b, h1, dh2, g_ffn, w_out_b),
        comm=comm,
    )


def _seq_bwd(z, dy, v, w_dw4, ln_g, ln_b, w_pool, s_pool, comm=()):
    T, CI = z.shape
    CC = ln_g.shape[1]
    n_grp, G = w_pool.shape[0], w_pool.shape[-1]
    CP = n_grp * G
    KW = w_dw4.shape[1]
    n_cc = CC // LANES
    D = CC + CP
    tt = _tile(T, 512, HALO)
    per = tt // HALO
    n_tiles = T // tt
    last_halo = T // HALO - 1

    def body(zc_ref, zp_ref, dyc_ref, dyn_ref, vc_ref, vn_ref, wdw_ref, lng_ref, lnb_ref, wp_ref, sp_ref,
             dz_ref, dwdw_ref, dbdw_ref, dlng_ref, dlnb_ref, dwp_ref, dsp_ref, dbin_ref,
             dv_scr, u_scr, p_scr, g_scr, dw_scr):
        i = pl.program_id(0)
        first = i == 0
        last = i == n_tiles - 1
        lng, lnb = lng_ref[...], lnb_ref[...]

        def conv_pre(vv, dyc):
            mu = jnp.mean(vv, axis=-1, keepdims=True)
            d = vv - mu
            rs = lax.rsqrt(jnp.mean(d * d, axis=-1, keepdims=True) + LN_EPS)
            xh = d * rs
            ln = xh * lng + lnb
            sg = _sigmoid(ln)
            dln = dyc * (sg * (1.0 + ln * (1.0 - sg)))
            dxh = dln * lng
            dv = rs * (dxh - jnp.mean(dxh, axis=-1, keepdims=True) - xh * jnp.mean(dxh * xh, axis=-1, keepdims=True))
            return dv, dln, xh

        dv_c, dln_c, xh_c = conv_pre(vc_ref[...], dyc_ref[:, 0:CC])
        dv_scr[0, 0:tt, :] = dv_c
        dv_n, _, _ = conv_pre(vn_ref[...], dyn_ref[:, 0:CC])
        dv_scr[0, tt:, :] = jnp.where(last, 0.0, dv_n)
        _fill_shifted(dv_scr)
        _accumulate(dlng_ref, first, jnp.sum(dln_c * xh_c, axis=0, keepdims=True))
        _accumulate(dlnb_ref, first, jnp.sum(dln_c, axis=0, keepdims=True))
        _accumulate(dbdw_ref, first, jnp.sum(dv_c, axis=0, keepdims=True))

        u_scr[...] = zc_ref[:, 0:CC] * _sigmoid(zc_ref[:, CC : 2 * CC])

        @pl.when(first)
        def _():
            dw_scr[...] = jnp.zeros_like(dw_scr)

        for j in range(n_cc):
            cs = slice(LANES * j, LANES * (j + 1))
            gs = slice(CC + LANES * j, CC + LANES * (j + 1))
            dbin_a = jnp.zeros((1, LANES), F32)
            dbin_g = jnp.zeros((1, LANES), F32)
            for rb in range(tt // CONV_ROWS):
                rows = slice(rb * CONV_ROWS, (rb + 1) * CONV_ROWS)
                u_blk = u_scr[rows, cs]
                du = jnp.zeros((CONV_ROWS, LANES), F32)
                for k in range(KW):
                    off = rb * CONV_ROWS + (KW - 1) - k
                    d = _shifted_rows(dv_scr, off, CONV_ROWS, cs)
                    du = du + d * wdw_ref[j, k]
                    dw_scr[j * HALO + k] += jnp.sum((u_blk * d).reshape(CONV_ROWS // 8, 8, LANES), axis=0)
                a = zc_ref[rows, cs]
                sg = _sigmoid(zc_ref[rows, gs])
                da = du * sg
                dgate = du * a * sg * (1.0 - sg)
                dz_ref[rows, cs] = da.astype(BF16)
                dz_ref[rows, gs] = dgate.astype(BF16)
                dbin_a = dbin_a + jnp.sum(da, axis=0, keepdims=True)
                dbin_g = dbin_g + jnp.sum(dgate, axis=0, keepdims=True)
            _accumulate(dbin_ref.at[:, cs], first, dbin_a)
            _accumulate(dbin_ref.at[:, gs], first, dbin_g)

        @pl.when(last)
        def _():
            dwdw_ref[...] = jnp.sum(dw_scr[...], axis=1).reshape(dwdw_ref.shape)

        p_scr[0:HALO, :] = jnp.where(first, 0.0, zp_ref[:, 2 * CC :])
        p_scr[HALO:, :] = zc_ref[:, 2 * CC :]
        tpos = i * tt + lax.broadcasted_iota(jnp.int32, (tt, 1), 0)
        for gi, w in enumerate(POOL_WINDOWS):
            cs = slice(G * gi, G * (gi + 1))
            ys = slice(CC + G * gi, CC + G * (gi + 1))
            ps = slice(2 * CC + G * gi, 2 * CC + G * (gi + 1))
            cnt = jnp.minimum(tpos + 1, w).astype(F32)
            yib = _pool_mean_minus_token(p_scr, cs, w, cnt, tt).astype(BF16)
            wp = wp_ref[gi].astype(BF16)
            sp = sp_ref[:, cs]
            dyp = dyc_ref[:, ys]
            q = _dot(yib, wp, NN)
            _accumulate(dsp_ref.at[:, cs], first, jnp.sum(dyp * q, axis=0, keepdims=True))
            dq_c = (dyp * sp).astype(BF16)
            dq_n = (jnp.where(last, 0.0, dyn_ref[:, ys]) * sp).astype(BF16)
            _accumulate(dwp_ref.at[gi], first, _dot(yib, dq_c, TN))
            dyi_c = _dot(dq_c, wp, NT)
            g_scr[0:tt, cs] = dyi_c / cnt
            g_scr[tt:, cs] = _dot(dq_n, wp, NT) * (1.0 / w)
            dp = -dyi_c
            for d in range(w):
                dp = dp + g_scr[d : d + tt, cs]
            dz_ref[:, ps] = dp.astype(BF16)
            _accumulate(dbin_ref.at[:, ps], first, jnp.sum(dp, axis=0, keepdims=True))

    cur = lambda i: (i, 0)
    prev = lambda i: (jnp.maximum(i * per - 1, 0), 0)
    nxt = lambda i: (jnp.minimum((i + 1) * per, last_halo), 0)
    c2 = lambda i: (0, 0)
    c3 = lambda i: (0, 0, 0)
    return _call(
        "seq_bwd",
        body,
        (n_tiles,),
        [
            pl.BlockSpec((tt, CI), cur),
            pl.BlockSpec((HALO, CI), prev),
            pl.BlockSpec((tt, D), cur),
            pl.BlockSpec((HALO, D), nxt),
            pl.BlockSpec((tt, CC), cur),
            pl.BlockSpec((HALO, CC), nxt),
            pl.BlockSpec(w_dw4.shape, lambda i: (0,) * w_dw4.ndim),
            pl.BlockSpec((1, CC), c2),
            pl.BlockSpec((1, CC), c2),
            pl.BlockSpec(w_pool.shape, c3),
            pl.BlockSpec((1, CP), c2),
        ],
        [
            pl.BlockSpec((tt, CI), cur),
            pl.BlockSpec((n_cc, HALO, LANES), c3),
            pl.BlockSpec((1, CC), c2),
            pl.BlockSpec((1, CC), c2),
            pl.BlockSpec((1, CC), c2),
            pl.BlockSpec((n_grp, G, G), c3),
            pl.BlockSpec((1, CP), c2),
            pl.BlockSpec((1, CI), c2),
        ],
        [
            jax.ShapeDtypeStruct((T, CI), BF16),
            jax.ShapeDtypeStruct((n_cc, HALO, LANES), F32),
            jax.ShapeDtypeStruct((1, CC), F32),
            jax.ShapeDtypeStruct((1, CC), F32),
            jax.ShapeDtypeStruct((1, CC), F32),
            jax.ShapeDtypeStruct((n_grp, G, G), F32),
            jax.ShapeDtypeStruct((1, CP), F32),
            jax.ShapeDtypeStruct((1, CI), F32),
        ],
        (z, z, dy, dy, v, v, w_dw4, ln_g, ln_b, w_pool, s_pool),
        scratch=[
            pltpu.VMEM((SUBLANES, tt + HALO, CC), F32),
            pltpu.VMEM((tt, CC), F32),
            pltpu.VMEM((HALO + tt, CP), F32),
            pltpu.VMEM((tt + HALO, CP), F32),
            pltpu.VMEM((n_cc * HALO, 8, LANES), F32),
        ],
        comm=comm,
    )


def _in_proj_bwd(dz_b, w_inT_b, x, dh1, g_mix, after=()):
    T, D = x.shape
    CI = w_inT_b.shape[0]
    tm = _tile(T, 256)

    def body(dz_ref, w_ref, x_ref, dh1_ref, g_ref, dx_ref, dg_ref):
        i = pl.program_id(0)
        dxn = _dot(dz_ref[...], w_ref[...], NN)
        dx, dg_rows = _rms_bwd(x_ref[...], g_ref[...], dxn)
        dx_ref[...] = dh1_ref[...] + dx
        _accumulate(dg_ref, i == 0, jnp.sum(dg_rows, axis=0, keepdims=True))

    row = lambda i: (i, 0)
    const = lambda i: (0, 0)
    return _call(
        "in_proj_bwd",
        body,
        (T // tm,),
        [
            pl.BlockSpec((tm, CI), row),
            pl.BlockSpec((CI, D), const),
            pl.BlockSpec((tm, D), row),
            pl.BlockSpec((tm, D), row),
            pl.BlockSpec((1, D), const),
        ],
        [pl.BlockSpec((tm, D), row), pl.BlockSpec((1, D), const)],
        [jax.ShapeDtypeStruct((T, D), F32), jax.ShapeDtypeStruct((1, D), F32)],
        (dz_b, w_inT_b, x, dh1, g_mix),
        after=after,
    )


def _weight_grad(name, a_b, b_b, after=()):
    T, N1 = a_b.shape
    N2 = b_b.shape[1]
    t1 = _tile(N1, 1408, LANES)
    tk = _tile(T, 2048)
    nk = T // tk

    def body(a_ref, b_ref, o_ref, acc):
        k = pl.program_id(1)
        _accumulate(acc, k == 0, _dot(a_ref[...], b_ref[...], TN))

        @pl.when(k == nk - 1)
        def _():
            o_ref[...] = acc[...].astype(BF16)

    (out,), _ = _call(
        name,
        body,
        (N1 // t1, nk),
        [pl.BlockSpec((tk, t1), lambda n, k: (k, n)), pl.BlockSpec((tk, N2), lambda n, k: (k, 0))],
        [pl.BlockSpec((t1, N2), lambda n, k: (n, 0))],
        [jax.ShapeDtypeStruct((N1, N2), BF16)],
        (a_b, b_b),
        scratch=[pltpu.VMEM((t1, N2), F32)],
        after=after,
    )
    return out


def _sum_parts(name, full, how, parts, me):
    _, R, C = parts[0].shape
    tr = _tile(R, 512)
    nb = R // tr
    where = [(q, r) for q, p in enumerate(parts) for r in range(p.shape[0])]
    assert len(where) == 3

    def body(me_ref, own_ref, *refs):
        o_ref = refs[-1]
        f = lambda j: refs[where[j][0]][where[j][1]].astype(F32)
        o_ref[...] = (own_ref[...].astype(F32) + f(0)) + (f(1) + f(2))

    own_map = {"rows": lambda i, me_ref: (me_ref[0] * nb + i, 0), "all": lambda i, me_ref: (i, 0)}[how]
    return pl.pallas_call(
        body,
        name=name,
        grid_spec=pltpu.PrefetchScalarGridSpec(
            num_scalar_prefetch=1,
            grid=(nb,),
            in_specs=[pl.BlockSpec((tr, C), own_map)]
            + [pl.BlockSpec((p.shape[0], tr, C), lambda i, me_ref: (0, i, 0)) for p in parts],
            out_specs=pl.BlockSpec((tr, C), lambda i, me_ref: (i, 0)),
        ),
        out_shape=jax.ShapeDtypeStruct((R, C), F32),
        compiler_params=pltpu.CompilerParams(dimension_semantics=("arbitrary",), vmem_limit_bytes=VMEM_LIMIT),
    )(me, full, *parts)


_M_CORR = 1.0 - ADAM_B1**ADAM_STEP
_V_CORR = 1.0 - ADAM_B2**ADAM_STEP


def _adamw_math(w, g, m, v):
    m = ADAM_B1 * m + (1.0 - ADAM_B1) * g
    v = ADAM_B2 * v + (1.0 - ADAM_B2) * (g * g)
    delta = -ADAM_LR * ((m / _M_CORR) / (jnp.sqrt(v / _V_CORR) + ADAM_EPS) + ADAM_WD * w)
    return delta, m, v


def _adamw(name, w, m, v, g_here, g_there, g_transposed=False):
    R, C = w.shape
    tr = _tile(R, 256, LANES if g_transposed else 8)

    def body(w_ref, m_ref, v_ref, ga_ref, gb_ref, g_ref, d_ref, nm_ref, nv_ref):
        g = ga_ref[...] + gb_ref[...]
        if g_transposed:
            g = g.T
        g_ref[...] = g
        d_ref[...], nm_ref[...], nv_ref[...] = _adamw_math(w_ref[...], g, m_ref[...], v_ref[...])

    spec = pl.BlockSpec((tr, C), lambda i: (i, 0))
    gspec = pl.BlockSpec((C, tr), lambda i: (0, i)) if g_transposed else spec
    return _call(name, body, (R // tr,), [spec] * 3 + [gspec] * 2, [spec] * 4, [jax.ShapeDtypeStruct((R, C), F32)] * 4,
                 (w, m, v, g_here, g_there))


def _adamw_on_sparsecore(name, w, m, v, g_here, g_there, after):
    R, C = w.shape
    n_groups = R // SUBLANES
    n_turns = -(-n_groups // SC_TILES)
    n_in, n_out = 5, 4

    def body(w_hbm, m_hbm, v_hbm, ga_hbm, gb_hbm, after_hbm, g_out, d_out, nm_out, nv_out, bufs, sems):
        tile = lax.axis_index("subcore") * SC_CORES + lax.axis_index("sparsecore")
        srcs = (w_hbm, m_hbm, v_hbm, ga_hbm, gb_hbm)
        dsts = (d_out, nm_out, nv_out, g_out)

        def rows(turn):
            return pl.ds((tile + turn * SC_TILES) * SUBLANES, SUBLANES)

        def loads(turn):
            slot = turn % 2
            return [pltpu.make_async_copy(srcs[q].at[rows(turn), :], bufs.at[slot, q], sems.at[slot, q]) for q in range(n_in)]

        def stores(turn):
            slot = turn % 2
            return [pltpu.make_async_copy(bufs.at[slot, q], dsts[q].at[rows(turn), :], sems.at[slot, n_in + q])
                    for q in range(n_out)]

        def when_mine(turn, fn):
            pl.when(tile + turn * SC_TILES < n_groups)(fn)

        def compute(slot):
            wb, mb, vb, gab, gbb = (bufs.at[slot, q] for q in range(n_in))

            @pl.loop(0, SUBLANES)
            def _(r):
                @pl.loop(0, C, step=SC_LANES)
                def _(i):
                    at = (r, pl.ds(i, SC_LANES))
                    g = gab[at] + gbb[at]
                    delta, new_m, new_v = _adamw_math(wb[at], g, mb[at], vb[at])
                    gab[at], wb[at], mb[at], vb[at] = g, delta, new_m, new_v

        def start_loads(turn):
            def fn():
                for cp in loads(turn):
                    cp.start()

            when_mine(turn, fn)

        start_loads(0)
        for turn in range(n_turns):
            def step(turn=turn):
                for cp in loads(turn):
                    cp.wait()
                if turn >= 1:
                    for cp in stores(turn - 1):
                        cp.wait()
                if turn + 1 < n_turns:
                    start_loads(turn + 1)
                compute(turn % 2)
                for cp in stores(turn):
                    cp.start()

            when_mine(turn, step)
        for turn in range(n_turns):
            def drain(turn=turn):
                for cp in stores(turn):
                    cp.wait()

            last_mine = jnp.logical_and(tile + turn * SC_TILES < n_groups, tile + (turn + 1) * SC_TILES >= n_groups)
            pl.when(last_mine)(drain)

    return pl.kernel(
        body,
        name=name,
        out_type=[jax.ShapeDtypeStruct((R, C), F32)] * 4,
        mesh=plsc.VectorSubcoreMesh(core_axis_name="sparsecore", subcore_axis_name="subcore"),
        scratch_types=[pltpu.VMEM((2, n_in, SUBLANES, C), F32), pltpu.SemaphoreType.DMA((2, n_in + n_out))],
        compiler_params=pltpu.CompilerParams(use_tc_tiling_on_sc=True),
    )(w, m, v, g_here, g_there, after)


class _PackLayout:
    def __init__(self, n_cc, n_grp, G, widths):
        self.dw_rows = (0, HALO)
        self.wp_rows = (HALO, HALO + G)
        self.n_cc, self.n_grp, self.G = n_cc, n_grp, G
        self.vec = {}
        r = HALO + G
        for name, width in widths:
            self.vec[name] = (r, width)
            r += width // PACK_W
        self.rows = -(-r // 8) * 8


def _pack_small(layout, dwdw, dwp, vecs):
    names = list(vecs)

    def body(*refs):
        dw_ref, wp_ref = refs[0], refs[1]
        vec_refs = refs[2 : 2 + len(names)]
        o_ref = refs[-1]
        o_ref[...] = jnp.zeros_like(o_ref)
        for j in range(layout.n_cc):
            o_ref[layout.dw_rows[0] : layout.dw_rows[1], j * LANES : (j + 1) * LANES] = dw_ref[j]
        for i in range(layout.n_grp):
            o_ref[layout.wp_rows[0] : layout.wp_rows[1], i * layout.G : (i + 1) * layout.G] = wp_ref[i]
        for name, ref in zip(names, vec_refs):
            r, width = layout.vec[name]
            for h in range(width // PACK_W):
                o_ref[r + h : r + h + 1, :] = ref[:, h * PACK_W : (h + 1) * PACK_W]

    return pl.pallas_call(
        body,
        name="pack_small",
        out_shape=jax.ShapeDtypeStruct((layout.rows, PACK_W), F32),
    )(dwdw, dwp, *[vecs[k] for k in names])


def _adamw_small(layout, g_here, g_there, w_dw, m_dw, v_dw, w_pool, m_pool, v_pool, vec_w, vec_m, vec_v, row):
    names = list(vec_w)
    nv = len(names)

    def body(*refs):
        ga_ref, gb_ref = refs[0], refs[1]
        wdw, mdw, vdw, wp, mp, vp = refs[2:8]
        vw, vm, vv = refs[8 : 8 + nv], refs[8 + nv : 8 + 2 * nv], refs[8 + 2 * nv : 8 + 3 * nv]
        row_g, row_w, row_m, row_v = refs[8 + 3 * nv : 12 + 3 * nv]
        outs = refs[12 + 3 * nv :]
        acc = outs[-1]
        acc[...] = ga_ref[...] + gb_ref[...]

        def emit(o, g, w, m, v, idx=()):
            res = (g,) + _adamw_math(w, g, m, v)
            for ref, val in zip(o, res):
                ref[idx] = val

        me = 2 * lax.axis_index("x") + lax.axis_index("y")
        for j in range(layout.n_cc):

            @pl.when(me == j)
            def _(j=j):
                for k in range(wdw.shape[0]):
                    g = acc[layout.dw_rows[0] + k : layout.dw_rows[0] + k + 1, j * LANES : (j + 1) * LANES]
                    emit(outs[0:4], g, wdw[k], mdw[k], vdw[k], idx=k)

        for i in range(layout.n_grp):
            g = acc[layout.wp_rows[0] : layout.wp_rows[1], i * layout.G : (i + 1) * layout.G]
            emit(outs[4:8], g, wp[i], mp[i], vp[i], idx=i)
        for q, name in enumerate(names):
            r, width = layout.vec[name]
            for h in range(width // PACK_W):
                ls = slice(h * PACK_W, (h + 1) * PACK_W)
                g = acc[r + h : r + h + 1, :]
                emit(outs[8 + 4 * q : 12 + 4 * q], g, vw[q][:, ls], vm[q][:, ls], vv[q][:, ls], idx=(slice(None), ls))
        emit(outs[8 + 4 * nv : 12 + 4 * nv], row_g[...], row_w[...], row_m[...], row_v[...], idx=...)

    shapes = [w_dw.shape] * 4 + [w_pool.shape] * 4
    for name in names:
        shapes += [vec_w[name].shape] * 4
    shapes += [row[1].shape] * 4
    return pl.pallas_call(
        body,
        name="adamw_small",
        out_shape=[jax.ShapeDtypeStruct(s, F32) for s in shapes],
        scratch_shapes=[pltpu.VMEM(g_here.shape, F32)],
    )(g_here, g_there, w_dw, m_dw, v_dw, w_pool, m_pool, v_pool,
      *[vec_w[k] for k in names], *[vec_m[k] for k in names], *[vec_v[k] for k in names], *row)


def _allreduce_rows(g_part, loss_part, comm=()):
    n_pairs = N_DEV - 1

    def body(g_ref, l_ref, go_ref, lo_ref, land_g, land_l, sems):
        x, y, c = _place()
        copies = []
        for q, (src, land) in enumerate(((g_ref, land_g), (l_ref, land_l))):
            for r in range(1, N_DEV):
                fx, fy, fc = (r >> 2) & 1, (r >> 1) & 1, r & 1
                peer = (1 - x if fx else x, 1 - y if fy else y, 1 - c if fc else c)
                cp = _remote(src, land.at[r], sems, 2 * (q * n_pairs + r - 1), peer)
                cp.start()
                copies.append(cp)
        for cp in copies:
            cp.wait()

        def total(src, land):
            row = lambda r: src[...] if r == 0 else land[r]
            return ((row(0) + row(4)) + (row(2) + row(6))) + ((row(1) + row(5)) + (row(3) + row(7)))

        go_ref[...] = total(g_ref, land_g)
        lo_ref[...] = total(l_ref, land_l)

    vm = pl.BlockSpec(memory_space=pltpu.VMEM)
    return _call(
        "allreduce_rows",
        body,
        (),
        [vm] * 2,
        [vm] * 2,
        [jax.ShapeDtypeStruct(g_part.shape, F32), jax.ShapeDtypeStruct(loss_part.shape, F32)],
        (g_part, loss_part),
        scratch=[pltpu.VMEM((N_DEV,) + g_part.shape, F32), pltpu.VMEM((N_DEV,) + loss_part.shape, F32),
                 pltpu.SemaphoreType.DMA((4 * n_pairs,))],
        comm=comm,
    )


def kernel(x, g_mix, w_in, b_in, w_dw, b_dw, ln_g, ln_b, w_pool, s_pool, w_out, g_ffn, w_gate, w_up, w_down, g_final, loss_target, m_g_mix, m_w_in, m_b_in, m_w_dw, m_b_dw, m_ln_g, m_ln_b, m_w_pool, m_s_pool, m_w_out, m_g_ffn, m_w_gate, m_w_up, m_w_down, m_g_final, v_g_mix, v_w_in, v_b_in, v_w_dw, v_b_dw, v_ln_g, v_ln_b, v_w_pool, v_s_pool, v_w_out, v_g_ffn, v_w_gate, v_w_up, v_w_down, v_g_final):
    x2 = x[0]
    target = loss_target[0]
    T, D = x2.shape
    w_in2, w_out2, w_down2 = w_in[0], w_out[0], w_down[0]
    taps_first = lambda a: jnp.transpose(a, (1, 0, 2))
    w_dw3 = taps_first(w_dw)
    w_gateT, w_upT = w_gate[0].T, w_up[0].T
    CI = w_in2.shape[1] * N_CHIPS
    DM = w_out2.shape[0] * N_CHIPS
    F = w_down2.shape[0] * N_CHIPS
    KW, _, dw_cols = w_dw3.shape
    assert dw_cols == LANES
    n_grp, G = w_pool.shape[1], w_pool.shape[-1]
    w_pool3 = w_pool[0]
    g_final2 = g_final.reshape(1, D)

    me = (2 * lax.axis_index("x") + lax.axis_index("y")).astype(jnp.int32).reshape(1)

    w_inT_b, w_dw4, f_out, f_gate, f_up, f_down = _place_and_gather(
        [(w_in2, "rows", (CI, D), BF16, True, True), (w_dw3, "lead", (N_CHIPS, KW, 1, dw_cols), F32, False, False)],
        [(w, "rows", shape, BF16, False, True)
         for w, shape in ((w_out2, (DM, D)), (w_gateT, (F, D)), (w_upT, (F, D)), (w_down2, (F, D)))])
    ici = lambda f: _GatherIci([f], ["rows"], [True])
    d2d = lambda f: _GatherD2d([f], ["rows"])
    gather = _start("gather_start", [ici(f_out), ici(f_gate), ici(f_up), ici(f_down)])
    (z, xn_b), _ = _in_proj(x2, g_mix, w_inT_b, b_in, after=[gather.token])
    (f_out,) = _wait("gather_out_wait", gather, 0, xn_b)
    s_out = _start("share_out_start", [d2d(f_out)], sibling_only=True)
    (y_b, v), _ = _seq_fwd(z, w_dw4, b_dw, ln_g, ln_b, w_pool3, s_pool, after=[s_out.token])
    (w_out_b,) = _wait("share_out_wait", s_out, 0, y_b)
    (f_gate,) = _wait("gather_gate_wait", gather, 1, y_b)
    s_gate = _start("share_gate_start", [d2d(f_gate)], sibling_only=True)
    (h1, hn_b), _ = _out_proj(y_b, x2, w_out_b, g_ffn, after=[s_gate.token])
    (f_up,) = _wait("gather_up_wait", gather, 2, hn_b)
    s_up = _start("share_up_start", [d2d(f_up)], sibling_only=True)
    (wgT_b,) = _wait("share_gate_wait", s_gate, 0, hn_b)
    (wuT_b,) = _wait("share_up_wait", s_up, 0, hn_b)
    (silu_b, uds_b, a_b), _ = _gate_up(hn_b, wgT_b, wuT_b)
    (f_down,) = _wait("gather_down_wait", gather, 3, a_b)
    s_down = _start("share_down_start", [d2d(f_down)], sibling_only=True)
    (wd_b,) = _wait("share_down_wait", s_down, 0, a_b)
    (dh2, dh2_b, loss_part, d_g_final), _ = _down_loss(a_b, wd_b, h1, target, g_final2)

    gw_down = _weight_grad("grad_w_down", a_b, dh2_b)
    x_down = _start("scatter_down_start", [_Scatter([gw_down], ["rows"])])
    (dg_b, du_b), _ = _ffn_bwd_act(dh2_b, wd_b, silu_b, uds_b, after=[x_down.token])
    gw_gateT = _weight_grad("grad_w_gate", dg_b, hn_b)
    gw_upT = _weight_grad("grad_w_up", du_b, hn_b)
    gw_down, p_down = _wait("scatter_down_wait", x_down, 0, gw_upT)
    sum_down = _sum_parts("sum_w_down", gw_down, "rows", [p_down], me)
    (dh1, dh1_b, dy, d_g_ffn), (p_gate, oth_down) = _ffn_bwd_in(
        dg_b, du_b, wgT_b, wuT_b, h1, dh2, g_ffn, w_out_b, comm=[_Scatter([gw_gateT], ["rows"]), _Swap([sum_down])])
    gw_out = _weight_grad("grad_w_out", y_b, dh1_b)
    sum_gate = _sum_parts("sum_w_gate", gw_gateT, "rows", [p_gate], me)
    res = {}
    res["w_down"] = _adamw_on_sparsecore("adamw_w_down", w_down2, m_w_down[0], v_w_down[0], sum_down, oth_down, sum_down)
    (dz_b, d_wdw, d_bdw, d_lng, d_lnb, d_wp, d_sp, d_bin), (p_up, p_out, oth_gate) = _seq_bwd(
        z, dy, v, w_dw4, ln_g, ln_b, w_pool3, s_pool,
        comm=[_Scatter([gw_upT, gw_out], ["rows", "rows"]), _Swap([sum_gate])])
    res["w_gate"] = _adamw_on_sparsecore(
        "adamw_w_gate", w_gateT, m_w_gate[0].T, v_w_gate[0].T, sum_gate, oth_gate, res["w_down"][0])
    vec_grads ={"b_dw": d_bdw, "ln_g": d_lng, "ln_b": d_lnb, "s_pool": d_sp, "g_ffn": d_g_ffn, "g_final": d_g_final, "b_in": d_bin}
    layout = _PackLayout(dw_cols * N_CHIPS // LANES, n_grp, G, [(k, a.shape[1]) for k, a in vec_grads.items()])
    pack = _pack_small(layout, d_wdw, d_wp, vec_grads)
    sum_up = _sum_parts("sum_w_up", gw_upT, "rows", [p_up], me)
    sum_out = _sum_parts("sum_w_out", gw_out, "rows", [p_out], me)
    mid = _start("mid_start", [_Swap([sum_up, sum_out]), _Scatter([pack], ["all"])])
    gw_inT = _weight_grad("grad_w_in", dz_b, xn_b, after=[mid.token])
    sum_up, sum_out, oth_up, oth_out = _wait("mid_swap_wait", mid, 0, gw_inT)
    late = _start("late_start", [_Scatter([gw_inT], ["rows"])])
    (grad_x, d_g_mix), _ = _in_proj_bwd(dz_b, w_inT_b, x2, dh1, g_mix, after=[late.token])
    pack, p_small = _wait("mid_small_wait", mid, 1, d_g_mix)
    gw_inT, p_in = _wait("late_w_in_wait", late, 0, d_g_mix)
    sum_small = _sum_parts("sum_small", pack, "all", [p_small], me)
    res["w_up"] = _adamw_on_sparsecore("adamw_w_up", w_upT, m_w_up[0].T, v_w_up[0].T, sum_up, oth_up, res["w_gate"][0])
    res["w_out"] = _adamw_on_sparsecore("adamw_w_out", w_out2, m_w_out[0], v_w_out[0], sum_out, oth_out, res["w_gate"][0])
    sum_in = _sum_parts("sum_w_in", gw_inT, "rows", [p_in], me)
    (d_g_mix, loss_row), (oth_in, oth_small) = _allreduce_rows(d_g_mix, loss_part, comm=[_Swap([sum_in, sum_small])])
    loss = loss_row[0, 0]
    res["w_in"], _ = _adamw("adamw_w_in", w_in2, m_w_in[0], v_w_in[0], sum_in, oth_in, g_transposed=True)

    vec_w = {"b_dw": b_dw, "ln_g": ln_g, "ln_b": ln_b, "s_pool": s_pool, "g_ffn": g_ffn, "g_final": g_final2, "b_in": b_in}
    vec_m = {"b_dw": m_b_dw, "ln_g": m_ln_g, "ln_b": m_ln_b, "s_pool": m_s_pool, "g_ffn": m_g_ffn,
             "g_final": m_g_final.reshape(1, D), "b_in": m_b_in}
    vec_v = {"b_dw": v_b_dw, "ln_g": v_ln_g, "ln_b": v_ln_b, "s_pool": v_s_pool, "g_ffn": v_g_ffn,
             "g_final": v_g_final.reshape(1, D), "b_in": v_b_in}
    small = _adamw_small(layout, sum_small, oth_small, w_dw3, taps_first(m_w_dw), taps_first(v_w_dw),
                         w_pool3, m_w_pool[0], v_w_pool[0], vec_w, vec_m, vec_v, (d_g_mix, g_mix, m_g_mix, v_g_mix))
    res["w_dw"] = [taps_first(a) for a in small[0:4]]
    res["w_pool"] = [a[None] for a in small[4:8]]
    for q, k in enumerate(vec_w):
        res[k] = list(small[8 + 4 * q : 12 + 4 * q])
    res["g_mix"] = list(small[-4:])
    res["g_final"] = [a.reshape(D) for a in res["g_final"]]
    for k in ("w_in", "w_out", "w_down"):
        res[k] = [a[None] for a in res[k]]
    for k in ("w_gate", "w_up"):
        res[k] = [a.T[None] for a in res[k]]

    order = ["g_mix", "w_in", "b_in", "w_dw", "b_dw", "ln_g", "ln_b", "w_pool", "s_pool", "w_out", "g_ffn", "w_gate", "w_up", "w_down", "g_final"]
    outs = [loss, grad_x[None]]
    for q in range(4):
        outs += [res[k][q] for k in order]
    return tuple(outs)
```
